```python
import math
import jax, jax.numpy as jnp
from jax import lax
import numpy as np

D_MODEL = 1024
BATCH = 16
SEQ = 4096
DEPTH = 1

D_MIX = 2 * D_MODEL
SSD_HEADDIM = 64
D_SSD = 3 * D_MIX // 4
SSD_HEADS = D_SSD // SSD_HEADDIM
SSD_GROUPS = 4
SSD_HPG = SSD_HEADS // SSD_GROUPS
SSD_STATE = 128
CONV_WIDTH = 4
CHUNK = 128
D_XBC = D_SSD + 2 * SSD_GROUPS * SSD_STATE
D_S5 = D_MIX - D_SSD
S5_CH = 16
S5_GROUPS = D_S5 // S5_CH
S5_STATE = 64
D_IN_PROJ = D_SSD + D_XBC + SSD_HEADS + D_S5
D_FF = 4 * D_MODEL
ALPHA = (2.0 * DEPTH) ** 0.25
BETA = (8.0 * DEPTH) ** -0.25
EPS = 1e-5
N_MOD = 6

kernel_name = 'hymba_ssd_s5_deepnorm_adaln_layer'


def layer_norm(x, g, b):
    xf = x.astype(jnp.float32)
    mu = jnp.mean(xf, axis=-1, keepdims=True)
    xc = xf - mu
    var = jnp.mean(xc * xc, axis=-1, keepdims=True)
    y = xc * lax.rsqrt(var + EPS) * g.astype(jnp.float32) + b.astype(jnp.float32)
    return y.astype(x.dtype)


def causal_dwconv(x, w, b):
    ch = x.shape[-1]
    y = lax.conv_general_dilated(x, w[:, None, :].astype(x.dtype), window_strides=(1,),
                                 padding=[(CONV_WIDTH - 1, 0)],
                                 dimension_numbers=('NWC', 'WIO', 'NWC'),
                                 feature_group_count=ch)
    return y + b.astype(x.dtype)


def ssd_chunked(x, dt, a, bmat, cmat):
    bsz, seqlen = x.shape[0], x.shape[1]
    nc = seqlen // CHUNK
    xdt = (x * dt[..., None]).reshape(bsz, nc, CHUNK, SSD_GROUPS, SSD_HPG, SSD_HEADDIM)
    adt = (dt * a).reshape(bsz, nc, CHUNK, SSD_GROUPS, SSD_HPG)
    adt = jnp.transpose(adt, (0, 3, 4, 1, 2))
    bmat = bmat.reshape(bsz, nc, CHUNK, SSD_GROUPS, SSD_STATE)
    cmat = cmat.reshape(bsz, nc, CHUNK, SSD_GROUPS, SSD_STATE)
    a_cs = jnp.cumsum(adt, axis=-1)
    causal = jnp.tril(jnp.ones((CHUNK, CHUNK), dtype=bool))
    seg = a_cs[..., :, None] - a_cs[..., None, :]
    decay = jnp.exp(jnp.where(causal, seg, -jnp.inf))
    scores = jnp.einsum('bclgn,bcsgn->bgcls', cmat, bmat)
    y_diag = jnp.einsum('bgcls,bgjcls,bcsgjp->bclgjp', scores, decay, xdt)
    decay_to_end = jnp.exp(a_cs[..., -1:] - a_cs)
    states = jnp.einsum('bclgn,bgjcl,bclgjp->bcgjpn', bmat, decay_to_end, xdt)
    chunk_decay = jnp.exp(a_cs[..., -1])

    def step(h, inp):
        s_c, d_c = inp
        return h * d_c[..., None, None] + s_c, h

    h0 = jnp.zeros((bsz, SSD_GROUPS, SSD_HPG, SSD_HEADDIM, SSD_STATE), jnp.float32)
    _, prev = lax.scan(step, h0, (jnp.moveaxis(states, 1, 0), jnp.moveaxis(chunk_decay, -1, 0)))
    y_off = jnp.einsum('bclgn,cbgjpn,bgjcl->bclgjp', cmat, prev, jnp.exp(a_cs))
    return (y_diag + y_off).reshape(bsz, seqlen, SSD_HEADS, SSD_HEADDIM)


def ssd_mixer(zxbcdt, conv_w, conv_b, dt_bias, a_log, d_skip, norm_w):
    bsz, seqlen = zxbcdt.shape[0], zxbcdt.shape[1]
    z = zxbcdt[..., :D_SSD]
    xbc = zxbcdt[..., D_SSD:D_SSD + D_XBC]
    dt_raw = zxbcdt[..., D_SSD + D_XBC:]
    xbc = jax.nn.silu(causal_dwconv(xbc, conv_w, conv_b)).astype(jnp.float32)
    xs = xbc[..., :D_SSD].reshape(bsz, seqlen, SSD_HEADS, SSD_HEADDIM)
    bm = xbc[..., D_SSD:D_SSD + SSD_GROUPS * SSD_STATE].reshape(bsz, seqlen, SSD_GROUPS, SSD_STATE)
    cm = xbc[..., D_SSD + SSD_GROUPS * SSD_STATE:].reshape(bsz, seqlen, SSD_GROUPS, SSD_STATE)
    dt = jax.nn.softplus(dt_raw.astype(jnp.float32) + dt_bias.astype(jnp.float32))
    a = -jnp.exp(a_log.astype(jnp.float32))
    y = ssd_chunked(xs, dt, a, bm, cm) + xs * d_skip.astype(jnp.float32)[:, None]
    y = y.reshape(bsz, seqlen, D_SSD) * jax.nn.silu(z.astype(jnp.float32))
    yg = y.reshape(bsz, seqlen, SSD_GROUPS, D_SSD // SSD_GROUPS)
    yg = yg * lax.rsqrt(jnp.mean(yg * yg, axis=-1, keepdims=True) + EPS)
    return (yg.reshape(bsz, seqlen, D_SSD) * norm_w.astype(jnp.float32)).astype(zxbcdt.dtype)


def complex_affine_combine(e1, e2):
    a1r, a1i, b1r, b1i = e1
    a2r, a2i, b2r, b2i = e2
    return (a2r * a1r - a2i * a1i,
            a2r * a1i + a2i * a1r,
            a2r * b1r - a2i * b1i + b2r,
            a2r * b1i + a2i * b1r + b2i)


def s5_mixer(u, a_re, a_im, log_dt, b_re, b_im, c_re, c_im, d_skip, w_glu, b_glu):
    f32 = jnp.float32
    bsz, seqlen = u.shape[0], u.shape[1]
    uf = u.astype(f32).reshape(bsz, seqlen, S5_GROUPS, S5_CH)
    ar = a_re.astype(f32)
    ai = a_im.astype(f32)
    dt = jnp.exp(log_dt.astype(f32))[:, None]
    mag = jnp.exp(ar * dt)
    ang = ai * dt
    ab_re = mag * jnp.cos(ang)
    ab_im = mag * jnp.sin(ang)
    den = ar * ar + ai * ai
    n_re = ab_re - 1.0
    coef_re = (n_re * ar + ab_im * ai) / den
    coef_im = (ab_im * ar - n_re * ai) / den
    br = b_re.astype(f32)
    bi = b_im.astype(f32)
    bb_re = coef_re[..., None] * br - coef_im[..., None] * bi
    bb_im = coef_re[..., None] * bi + coef_im[..., None] * br
    bu_re = jnp.einsum('bsgh,gph->bsgp', uf, bb_re)
    bu_im = jnp.einsum('bsgh,gph->bsgp', uf, bb_im)
    shape = (1, seqlen, S5_GROUPS, S5_STATE)
    elems = (jnp.broadcast_to(ab_re, shape), jnp.broadcast_to(ab_im, shape), bu_re, bu_im)
    _, _, s_re, s_im = lax.associative_scan(complex_affine_combine, elems, axis=1)
    y = (jnp.einsum('bsgp,ghp->bsgh', s_re, c_re.astype(f32))
         - jnp.einsum('bsgp,ghp->bsgh', s_im, c_im.astype(f32))
         + uf * d_skip.astype(f32))
    y = jax.nn.gelu(y.reshape(bsz, seqlen, D_S5))
    y = y * jax.nn.sigmoid(y @ w_glu.astype(f32) + b_glu.astype(f32))
    return y.astype(u.dtype)


def _fwd_setup_inputs(seed: int = 0) -> dict:
    key = jax.random.key(seed)
    ks = jax.random.split(key, 32)
    nrm = lambda k, shp: jax.random.normal(k, shp, jnp.float32)
    L = DEPTH
    dt0 = jnp.exp(jax.random.uniform(ks[6], (L, SSD_HEADS), jnp.float32, math.log(1e-3), math.log(1e-1)))
    inputs = {
        'x': nrm(ks[0], (BATCH, SEQ, D_MODEL)),
        'c': nrm(ks[1], (BATCH, D_MODEL)),
        'w_ada': nrm(ks[2], (L, D_MODEL, N_MOD * D_MODEL)) * (0.5 * D_MODEL ** -0.5),
        'b_ada': 0.01 * nrm(ks[3], (L, N_MOD * D_MODEL)),
        'w_in': nrm(ks[4], (L, D_MODEL, D_IN_PROJ)) * D_MODEL ** -0.5,
        'conv_w': nrm(ks[5], (L, CONV_WIDTH, D_XBC)) * CONV_WIDTH ** -0.5,
        'conv_b': 0.01 * nrm(ks[7], (L, D_XBC)),
        'dt_bias': dt0 + jnp.log(-jnp.expm1(-dt0)),
        'a_log': jnp.log(jax.random.uniform(ks[8], (L, SSD_HEADS), jnp.float32, 1.0, 16.0)),
        'd_ssd': 1.0 + 0.01 * nrm(ks[9], (L, SSD_HEADS)),
        'norm_w': 1.0 + 0.01 * nrm(ks[10], (L, D_SSD)),
        's5_a_re': -0.5 + 0.01 * nrm(ks[11], (L, S5_GROUPS, S5_STATE)),
        's5_a_im': jnp.pi * jnp.arange(S5_STATE, dtype=jnp.float32) + 0.01 * nrm(ks[12], (L, S5_GROUPS, S5_STATE)),
        's5_log_dt': jax.random.uniform(ks[13], (L, S5_GROUPS), jnp.float32, math.log(1e-3), math.log(1e-1)),
        's5_b_re': nrm(ks[14], (L, S5_GROUPS, S5_STATE, S5_CH)) * (2 * S5_CH) ** -0.5,
        's5_b_im': nrm(ks[15], (L, S5_GROUPS, S5_STATE, S5_CH)) * (2 * S5_CH) ** -0.5,
        's5_c_re': nrm(ks[16], (L, S5_GROUPS, S5_CH, S5_STATE)) * S5_STATE ** -0.5,
        's5_c_im': nrm(ks[17], (L, S5_GROUPS, S5_CH, S5_STATE)) * S5_STATE ** -0.5,
        's5_d': nrm(ks[18], (L, S5_GROUPS, S5_CH)),
        'w_glu': nrm(ks[19], (L, D_S5, D_S5)) * D_S5 ** -0.5,
        'b_glu': 0.01 * nrm(ks[20], (L, D_S5)),
        'w_out': nrm(ks[21], (L, D_MIX, D_MODEL)) * (D_MIX ** -0.5 * BETA),
        'ln1_g': 1.0 + 0.01 * nrm(ks[22], (L, D_MODEL)),
        'ln1_b': 0.01 * nrm(ks[23], (L, D_MODEL)),
        'w1': nrm(ks[24], (L, D_MODEL, D_FF)) * D_MODEL ** -0.5,
        'b1': 0.01 * nrm(ks[25], (L, D_FF)),
        'w2': nrm(ks[26], (L, D_FF, D_MODEL)) * (D_FF ** -0.5 * BETA),
        'b2': 0.01 * nrm(ks[27], (L, D_MODEL)),
        'ln2_g': 1.0 + 0.01 * nrm(ks[28], (L, D_MODEL)),
        'ln2_b': 0.01 * nrm(ks[29], (L, D_MODEL)),
    }
    return inputs


def _fwd_reference(x, c, w_ada, b_ada, w_in, conv_w, conv_b, dt_bias, a_log, d_ssd, norm_w,
              s5_a_re, s5_a_im, s5_log_dt, s5_b_re, s5_b_im, s5_c_re, s5_c_im, s5_d,
              w_glu, b_glu, w_out, ln1_g, ln1_b, w1, b1, w2, b2, ln2_g, ln2_b):
    cond = jax.nn.silu(c)
    for l in range(DEPTH):
        mod = (cond @ w_ada[l] + b_ada[l])[:, None, :]
        sh1, sc1, g1, sh2, sc2, g2 = jnp.split(mod, N_MOD, axis=-1)
        u = x * (1.0 + sc1) + sh1
        proj = u @ w_in[l]
        y_ssd = ssd_mixer(proj[..., :D_IN_PROJ - D_S5], conv_w[l], conv_b[l],
                          dt_bias[l], a_log[l], d_ssd[l], norm_w[l])
        y_s5 = s5_mixer(proj[..., D_IN_PROJ - D_S5:], s5_a_re[l], s5_a_im[l], s5_log_dt[l],
                        s5_b_re[l], s5_b_im[l], s5_c_re[l], s5_c_im[l], s5_d[l],
                        w_glu[l], b_glu[l])
        mix = jnp.concatenate([y_ssd, y_s5], axis=-1) @ w_out[l]
        x = layer_norm(ALPHA * x + (1.0 + g1) * mix, ln1_g[l], ln1_b[l])
        u = x * (1.0 + sc2) + sh2
        h = jnp.square(jax.nn.relu(u @ w1[l] + b1[l]))
        x = layer_norm(ALPHA * x + (1.0 + g2) * (h @ w2[l] + b2[l]), ln2_g[l], ln2_b[l])
    return x


import jax as _jax
import jax.numpy as _jnp

TWIN_FORMAT = 'train_step'
FWD_PARAMS = ['x', 'c', 'w_ada', 'b_ada', 'w_in', 'conv_w', 'conv_b', 'dt_bias', 'a_log', 'd_ssd', 'norm_w', 's5_a_re', 's5_a_im', 's5_log_dt', 's5_b_re', 's5_b_im', 's5_c_re', 's5_c_im', 's5_d', 'w_glu', 'b_glu', 'w_out', 'ln1_g', 'ln1_b', 'w1', 'b1', 'w2', 'b2', 'ln2_g', 'ln2_b']
TWIN_WEIGHTS = ['w_ada', 'b_ada', 'w_in', 'conv_w', 'conv_b', 'dt_bias', 'a_log', 'd_ssd', 'norm_w', 's5_a_re', 's5_a_im', 's5_log_dt', 's5_b_re', 's5_b_im', 's5_c_re', 's5_c_im', 's5_d', 'w_glu', 'b_glu', 'w_out', 'ln1_g', 'ln1_b', 'w1', 'b1', 'w2', 'b2', 'ln2_g', 'ln2_b']
TWIN_DIFF_INPUT = 'x'
TWIN_INPUTS = ['x', 'c', 'w_ada', 'b_ada', 'w_in', 'conv_w', 'conv_b', 'dt_bias', 'a_log', 'd_ssd', 'norm_w', 's5_a_re', 's5_a_im', 's5_log_dt', 's5_b_re', 's5_b_im', 's5_c_re', 's5_c_im', 's5_d', 'w_glu', 'b_glu', 'w_out', 'ln1_g', 'ln1_b', 'w1', 'b1', 'w2', 'b2', 'ln2_g', 'ln2_b', 'loss_target', 'm_w_ada', 'm_b_ada', 'm_w_in', 'm_conv_w', 'm_conv_b', 'm_dt_bias', 'm_a_log', 'm_d_ssd', 'm_norm_w', 'm_s5_a_re', 'm_s5_a_im', 'm_s5_log_dt', 'm_s5_b_re', 'm_s5_b_im', 'm_s5_c_re', 'm_s5_c_im', 'm_s5_d', 'm_w_glu', 'm_b_glu', 'm_w_out', 'm_ln1_g', 'm_ln1_b', 'm_w1', 'm_b1', 'm_w2', 'm_b2', 'm_ln2_g', 'm_ln2_b', 'v_w_ada', 'v_b_ada', 'v_w_in', 'v_conv_w', 'v_conv_b', 'v_dt_bias', 'v_a_log', 'v_d_ssd', 'v_norm_w', 'v_s5_a_re', 'v_s5_a_im', 'v_s5_log_dt', 'v_s5_b_re', 'v_s5_b_im', 'v_s5_c_re', 'v_s5_c_im', 'v_s5_d', 'v_w_glu', 'v_b_glu', 'v_w_out', 'v_ln1_g', 'v_ln1_b', 'v_w1', 'v_b1', 'v_w2', 'v_b2', 'v_ln2_g', 'v_ln2_b']
TWIN_OUTPUTS = ['loss', 'grad_x', 'grad_w_ada', 'grad_b_ada', 'grad_w_in', 'grad_conv_w', 'grad_conv_b', 'grad_dt_bias', 'grad_a_log', 'grad_d_ssd', 'grad_norm_w', 'grad_s5_a_re', 'grad_s5_a_im', 'grad_s5_log_dt', 'grad_s5_b_re', 'grad_s5_b_im', 'grad_s5_c_re', 'grad_s5_c_im', 'grad_s5_d', 'grad_w_glu', 'grad_b_glu', 'grad_w_out', 'grad_ln1_g', 'grad_ln1_b', 'grad_w1', 'grad_b1', 'grad_w2', 'grad_b2', 'grad_ln2_g', 'grad_ln2_b', 'delta_w_ada', 'delta_b_ada', 'delta_w_in', 'delta_conv_w', 'delta_conv_b', 'delta_dt_bias', 'delta_a_log', 'delta_d_ssd', 'delta_norm_w', 'delta_s5_a_re', 'delta_s5_a_im', 'delta_s5_log_dt', 'delta_s5_b_re', 'delta_s5_b_im', 'delta_s5_c_re', 'delta_s5_c_im', 'delta_s5_d', 'delta_w_glu', 'delta_b_glu', 'delta_w_out', 'delta_ln1_g', 'delta_ln1_b', 'delta_w1', 'delta_b1', 'delta_w2', 'delta_b2', 'delta_ln2_g', 'delta_ln2_b', 'new_m_w_ada', 'new_m_b_ada', 'new_m_w_in', 'new_m_conv_w', 'new_m_conv_b', 'new_m_dt_bias', 'new_m_a_log', 'new_m_d_ssd', 'new_m_norm_w', 'new_m_s5_a_re', 'new_m_s5_a_im', 'new_m_s5_log_dt', 'new_m_s5_b_re', 'new_m_s5_b_im', 'new_m_s5_c_re', 'new_m_s5_c_im', 'new_m_s5_d', 'new_m_w_glu', 'new_m_b_glu', 'new_m_w_out', 'new_m_ln1_g', 'new_m_ln1_b', 'new_m_w1', 'new_m_b1', 'new_m_w2', 'new_m_b2', 'new_m_ln2_g', 'new_m_ln2_b', 'new_v_w_ada', 'new_v_b_ada', 'new_v_w_in', 'new_v_conv_w', 'new_v_conv_b', 'new_v_dt_bias', 'new_v_a_log', 'new_v_d_ssd', 'new_v_norm_w', 'new_v_s5_a_re', 'new_v_s5_a_im', 'new_v_s5_log_dt', 'new_v_s5_b_re', 'new_v_s5_b_im', 'new_v_s5_c_re', 'new_v_s5_c_im', 'new_v_s5_d', 'new_v_w_glu', 'new_v_b_glu', 'new_v_w_out', 'new_v_ln1_g', 'new_v_ln1_b', 'new_v_w1', 'new_v_b1', 'new_v_w2', 'new_v_b2', 'new_v_ln2_g', 'new_v_ln2_b']
TWIN_LEAF_KINDS = {'loss': 'loss', 'grad_x': 'grad_x', 'grad_w_ada': 'grad_w', 'grad_b_ada': 'grad_w', 'grad_w_in': 'grad_w', 'grad_conv_w': 'grad_w', 'grad_conv_b': 'grad_w', 'grad_dt_bias': 'grad_w', 'grad_a_log': 'grad_w', 'grad_d_ssd': 'grad_w', 'grad_norm_w': 'grad_w', 'grad_s5_a_re': 'grad_w', 'grad_s5_a_im': 'grad_w', 'grad_s5_log_dt': 'grad_w', 'grad_s5_b_re': 'grad_w', 'grad_s5_b_im': 'grad_w', 'grad_s5_c_re': 'grad_w', 'grad_s5_c_im': 'grad_w', 'grad_s5_d': 'grad_w', 'grad_w_glu': 'grad_w', 'grad_b_glu': 'grad_w', 'grad_w_out': 'grad_w', 'grad_ln1_g': 'grad_w', 'grad_ln1_b': 'grad_w', 'grad_w1': 'grad_w', 'grad_b1': 'grad_w', 'grad_w2': 'grad_w', 'grad_b2': 'grad_w', 'grad_ln2_g': 'grad_w', 'grad_ln2_b': 'grad_w', 'delta_w_ada': 'delta_w', 'delta_b_ada': 'delta_w', 'delta_w_in': 'delta_w', 'delta_conv_w': 'delta_w', 'delta_conv_b': 'delta_w', 'delta_dt_bias': 'delta_w', 'delta_a_log': 'delta_w', 'delta_d_ssd': 'delta_w', 'delta_norm_w': 'delta_w', 'delta_s5_a_re': 'delta_w', 'delta_s5_a_im': 'delta_w', 'delta_s5_log_dt': 'delta_w', 'delta_s5_b_re': 'delta_w', 'delta_s5_b_im': 'delta_w', 'delta_s5_c_re': 'delta_w', 'delta_s5_c_im': 'delta_w', 'delta_s5_d': 'delta_w', 'delta_w_glu': 'delta_w', 'delta_b_glu': 'delta_w', 'delta_w_out': 'delta_w', 'delta_ln1_g': 'delta_w', 'delta_ln1_b': 'delta_w', 'delta_w1': 'delta_w', 'delta_b1': 'delta_w', 'delta_w2': 'delta_w', 'delta_b2': 'delta_w', 'delta_ln2_g': 'delta_w', 'delta_ln2_b': 'delta_w', 'new_m_w_ada': 'new_m', 'new_m_b_ada': 'new_m', 'new_m_w_in': 'new_m', 'new_m_conv_w': 'new_m', 'new_m_conv_b': 'new_m', 'new_m_dt_bias': 'new_m', 'new_m_a_log': 'new_m', 'new_m_d_ssd': 'new_m', 'new_m_norm_w': 'new_m', 'new_m_s5_a_re': 'new_m', 'new_m_s5_a_im': 'new_m', 'new_m_s5_log_dt': 'new_m', 'new_m_s5_b_re': 'new_m', 'new_m_s5_b_im': 'new_m', 'new_m_s5_c_re': 'new_m', 'new_m_s5_c_im': 'new_m', 'new_m_s5_d': 'new_m', 'new_m_w_glu': 'new_m', 'new_m_b_glu': 'new_m', 'new_m_w_out': 'new_m', 'new_m_ln1_g': 'new_m', 'new_m_ln1_b': 'new_m', 'new_m_w1': 'new_m', 'new_m_b1': 'new_m', 'new_m_w2': 'new_m', 'new_m_b2': 'new_m', 'new_m_ln2_g': 'new_m', 'new_m_ln2_b': 'new_m', 'new_v_w_ada': 'new_v', 'new_v_b_ada': 'new_v', 'new_v_w_in': 'new_v', 'new_v_conv_w': 'new_v', 'new_v_conv_b': 'new_v', 'new_v_dt_bias': 'new_v', 'new_v_a_log': 'new_v', 'new_v_d_ssd': 'new_v', 'new_v_norm_w': 'new_v', 'new_v_s5_a_re': 'new_v', 'new_v_s5_a_im': 'new_v', 'new_v_s5_log_dt': 'new_v', 'new_v_s5_b_re': 'new_v', 'new_v_s5_b_im': 'new_v', 'new_v_s5_c_re': 'new_v', 'new_v_s5_c_im': 'new_v', 'new_v_s5_d': 'new_v', 'new_v_w_glu': 'new_v', 'new_v_b_glu': 'new_v', 'new_v_w_out': 'new_v', 'new_v_ln1_g': 'new_v', 'new_v_ln1_b': 'new_v', 'new_v_w1': 'new_v', 'new_v_b1': 'new_v', 'new_v_w2': 'new_v', 'new_v_b2': 'new_v', 'new_v_ln2_g': 'new_v', 'new_v_ln2_b': 'new_v'}


def _forward(args):
    return _fwd_reference(*[args[k] for k in FWD_PARAMS])


def _output_shape():
    out = _jax.eval_shape(lambda: _forward(_fwd_setup_inputs(0)))
    return out.shape, out.dtype

N_MICROBATCH = 1
ADAM_LR = 0.001
ADAM_B1 = 0.9
ADAM_B2 = 0.999
ADAM_EPS = 1e-08
ADAM_WD = 0.01
ADAM_STEP = 10
PER_EXAMPLE_BATCH_AXIS = {'x': 0, 'c': 0, 'loss_target': 0}
SHARED_INPUTS = []
_WEIGHT_DTYPES = {'w_ada': _jnp.float32, 'b_ada': _jnp.float32, 'w_in': _jnp.float32, 'conv_w': _jnp.float32, 'conv_b': _jnp.float32, 'dt_bias': _jnp.float32, 'a_log': _jnp.float32, 'd_ssd': _jnp.float32, 'norm_w': _jnp.float32, 's5_a_re': _jnp.float32, 's5_a_im': _jnp.float32, 's5_log_dt': _jnp.float32, 's5_b_re': _jnp.float32, 's5_b_im': _jnp.float32, 's5_c_re': _jnp.float32, 's5_c_im': _jnp.float32, 's5_d': _jnp.float32, 'w_glu': _jnp.float32, 'b_glu': _jnp.float32, 'w_out': _jnp.float32, 'ln1_g': _jnp.float32, 'ln1_b': _jnp.float32, 'w1': _jnp.float32, 'b1': _jnp.float32, 'w2': _jnp.float32, 'b2': _jnp.float32, 'ln2_g': _jnp.float32, 'ln2_b': _jnp.float32}
MOMENT_SCALE = {'w_ada': 1.023989e-01, 'b_ada': 1.834321e-01, 'w_in': 7.764956e-02, 'conv_w': 7.345156e-02, 'conv_b': 9.173992e-02, 'dt_bias': 2.191650e-01, 'a_log': 3.280331e-01, 'd_ssd': 3.636887e-01, 'norm_w': 9.304589e-02, 's5_a_re': 6.566887e-03, 's5_a_im': 4.372555e-03, 's5_log_dt': 1.817823e+00, 's5_b_re': 2.113129e-03, 's5_b_im': 2.311821e-03, 's5_c_re': 3.293383e-03, 's5_c_im': 3.117678e-03, 's5_d': 3.801848e-02, 'w_glu': 1.022243e-02, 'b_glu': 1.442580e-02, 'w_out': 1.931002e-01, 'ln1_g': 5.321517e-01, 'ln1_b': 4.162477e-01, 'w1': 8.792343e-02, 'b1': 9.297839e-02, 'w2': 2.798322e-01, 'b2': 2.768119e-01, 'ln2_g': 6.486880e+01, 'ln2_b': 1.647294e+01}


def _to_microbatches(a, axis):
    t = _jnp.moveaxis(a, axis, 0)
    t = t.reshape((N_MICROBATCH, t.shape[0] // N_MICROBATCH) + t.shape[1:])
    return _jnp.moveaxis(t, 1, axis + 1)


def setup_inputs(seed: int = 0) -> dict:
    inp = _fwd_setup_inputs(seed)
    key = _jax.random.fold_in(_jax.random.key(seed), 7919)
    shape, _ = _output_shape()
    out = dict(inp)
    out["loss_target"] = _jax.random.normal(_jax.random.fold_in(key, 0), shape, _jnp.float32)
    for i, name in enumerate(TWIN_WEIGHTS):
        w = inp[name].astype(_jnp.float32)
        if MOMENT_SCALE is None:
            s = _jnp.sqrt(_jnp.mean(_jnp.square(w)) + 1e-30)
        else:
            s = MOMENT_SCALE[name]
        km, kv = _jax.random.split(_jax.random.fold_in(key, i + 1))
        out[name] = w
        out["m_" + name] = s * _jax.random.normal(km, w.shape, _jnp.float32)
        out["v_" + name] = (s * s) * _jax.random.uniform(kv, w.shape, _jnp.float32, 0.5, 1.5)
    if N_MICROBATCH > 1:
        for name, axis in PER_EXAMPLE_BATCH_AXIS.items():
            out[name] = _to_microbatches(out[name], axis)
    return {'x': out['x'], 'c': out['c'], 'w_ada': out['w_ada'], 'b_ada': out['b_ada'], 'w_in': out['w_in'], 'conv_w': out['conv_w'], 'conv_b': out['conv_b'], 'dt_bias': out['dt_bias'], 'a_log': out['a_log'], 'd_ssd': out['d_ssd'], 'norm_w': out['norm_w'], 's5_a_re': out['s5_a_re'], 's5_a_im': out['s5_a_im'], 's5_log_dt': out['s5_log_dt'], 's5_b_re': out['s5_b_re'], 's5_b_im': out['s5_b_im'], 's5_c_re': out['s5_c_re'], 's5_c_im': out['s5_c_im'], 's5_d': out['s5_d'], 'w_glu': out['w_glu'], 'b_glu': out['b_glu'], 'w_out': out['w_out'], 'ln1_g': out['ln1_g'], 'ln1_b': out['ln1_b'], 'w1': out['w1'], 'b1': out['b1'], 'w2': out['w2'], 'b2': out['b2'], 'ln2_g': out['ln2_g'], 'ln2_b': out['ln2_b'], 'loss_target': out['loss_target'], 'm_w_ada': out['m_w_ada'], 'm_b_ada': out['m_b_ada'], 'm_w_in': out['m_w_in'], 'm_conv_w': out['m_conv_w'], 'm_conv_b': out['m_conv_b'], 'm_dt_bias': out['m_dt_bias'], 'm_a_log': out['m_a_log'], 'm_d_ssd': out['m_d_ssd'], 'm_norm_w': out['m_norm_w'], 'm_s5_a_re': out['m_s5_a_re'], 'm_s5_a_im': out['m_s5_a_im'], 'm_s5_log_dt': out['m_s5_log_dt'], 'm_s5_b_re': out['m_s5_b_re'], 'm_s5_b_im': out['m_s5_b_im'], 'm_s5_c_re': out['m_s5_c_re'], 'm_s5_c_im': out['m_s5_c_im'], 'm_s5_d': out['m_s5_d'], 'm_w_glu': out['m_w_glu'], 'm_b_glu': out['m_b_glu'], 'm_w_out': out['m_w_out'], 'm_ln1_g': out['m_ln1_g'], 'm_ln1_b': out['m_ln1_b'], 'm_w1': out['m_w1'], 'm_b1': out['m_b1'], 'm_w2': out['m_w2'], 'm_b2': out['m_b2'], 'm_ln2_g': out['m_ln2_g'], 'm_ln2_b': out['m_ln2_b'], 'v_w_ada': out['v_w_ada'], 'v_b_ada': out['v_b_ada'], 'v_w_in': out['v_w_in'], 'v_conv_w': out['v_conv_w'], 'v_conv_b': out['v_conv_b'], 'v_dt_bias': out['v_dt_bias'], 'v_a_log': out['v_a_log'], 'v_d_ssd': out['v_d_ssd'], 'v_norm_w': out['v_norm_w'], 'v_s5_a_re': out['v_s5_a_re'], 'v_s5_a_im': out['v_s5_a_im'], 'v_s5_log_dt': out['v_s5_log_dt'], 'v_s5_b_re': out['v_s5_b_re'], 'v_s5_b_im': out['v_s5_b_im'], 'v_s5_c_re': out['v_s5_c_re'], 'v_s5_c_im': out['v_s5_c_im'], 'v_s5_d': out['v_s5_d'], 'v_w_glu': out['v_w_glu'], 'v_b_glu': out['v_b_glu'], 'v_w_out': out['v_w_out'], 'v_ln1_g': out['v_ln1_g'], 'v_ln1_b': out['v_ln1_b'], 'v_w1': out['v_w1'], 'v_b1': out['v_b1'], 'v_w2': out['v_w2'], 'v_b2': out['v_b2'], 'v_ln2_g': out['v_ln2_g'], 'v_ln2_b': out['v_ln2_b']}


def _loss(weights, diff, rest, loss_target):
    with _jax.named_scope("forward"):
        args = {**rest, TWIN_DIFF_INPUT: diff, **{k: w.astype(_WEIGHT_DTYPES[k]) for k, w in weights.items()}}
        y = _forward(args)
    with _jax.named_scope("loss_head"):
        err = _jnp.square(y.astype(_jnp.float32) - loss_target)
        return 0.5 * _jnp.sum(_jnp.mean(err, axis=-1)) if err.ndim else 0.5 * err


def _adamw(w, g, m, v):
    m = ADAM_B1 * m + (1.0 - ADAM_B1) * g
    v = ADAM_B2 * v + (1.0 - ADAM_B2) * _jnp.square(g)
    m_hat = m / (1.0 - ADAM_B1 ** ADAM_STEP)
    v_hat = v / (1.0 - ADAM_B2 ** ADAM_STEP)
    delta = -ADAM_LR * (m_hat / (_jnp.sqrt(v_hat) + ADAM_EPS) + ADAM_WD * w)
    return delta, m, v


def reference(x, c, w_ada, b_ada, w_in, conv_w, conv_b, dt_bias, a_log, d_ssd, norm_w, s5_a_re, s5_a_im, s5_log_dt, s5_b_re, s5_b_im, s5_c_re, s5_c_im, s5_d, w_glu, b_glu, w_out, ln1_g, ln1_b, w1, b1, w2, b2, ln2_g, ln2_b, loss_target, m_w_ada, m_b_ada, m_w_in, m_conv_w, m_conv_b, m_dt_bias, m_a_log, m_d_ssd, m_norm_w, m_s5_a_re, m_s5_a_im, m_s5_log_dt, m_s5_b_re, m_s5_b_im, m_s5_c_re, m_s5_c_im, m_s5_d, m_w_glu, m_b_glu, m_w_out, m_ln1_g, m_ln1_b, m_w1, m_b1, m_w2, m_b2, m_ln2_g, m_ln2_b, v_w_ada, v_b_ada, v_w_in, v_conv_w, v_conv_b, v_dt_bias, v_a_log, v_d_ssd, v_norm_w, v_s5_a_re, v_s5_a_im, v_s5_log_dt, v_s5_b_re, v_s5_b_im, v_s5_c_re, v_s5_c_im, v_s5_d, v_w_glu, v_b_glu, v_w_out, v_ln1_g, v_ln1_b, v_w1, v_b1, v_w2, v_b2, v_ln2_g, v_ln2_b):
    given = dict(x=x, c=c, w_ada=w_ada, b_ada=b_ada, w_in=w_in, conv_w=conv_w, conv_b=conv_b, dt_bias=dt_bias, a_log=a_log, d_ssd=d_ssd, norm_w=norm_w, s5_a_re=s5_a_re, s5_a_im=s5_a_im, s5_log_dt=s5_log_dt, s5_b_re=s5_b_re, s5_b_im=s5_b_im, s5_c_re=s5_c_re, s5_c_im=s5_c_im, s5_d=s5_d, w_glu=w_glu, b_glu=b_glu, w_out=w_out, ln1_g=ln1_g, ln1_b=ln1_b, w1=w1, b1=b1, w2=w2, b2=b2, ln2_g=ln2_g, ln2_b=ln2_b, loss_target=loss_target, m_w_ada=m_w_ada, m_b_ada=m_b_ada, m_w_in=m_w_in, m_conv_w=m_conv_w, m_conv_b=m_conv_b, m_dt_bias=m_dt_bias, m_a_log=m_a_log, m_d_ssd=m_d_ssd, m_norm_w=m_norm_w, m_s5_a_re=m_s5_a_re, m_s5_a_im=m_s5_a_im, m_s5_log_dt=m_s5_log_dt, m_s5_b_re=m_s5_b_re, m_s5_b_im=m_s5_b_im, m_s5_c_re=m_s5_c_re, m_s5_c_im=m_s5_c_im, m_s5_d=m_s5_d, m_w_glu=m_w_glu, m_b_glu=m_b_glu, m_w_out=m_w_out, m_ln1_g=m_ln1_g, m_ln1_b=m_ln1_b, m_w1=m_w1, m_b1=m_b1, m_w2=m_w2, m_b2=m_b2, m_ln2_g=m_ln2_g, m_ln2_b=m_ln2_b, v_w_ada=v_w_ada, v_b_ada=v_b_ada, v_w_in=v_w_in, v_conv_w=v_conv_w, v_conv_b=v_conv_b, v_dt_bias=v_dt_bias, v_a_log=v_a_log, v_d_ssd=v_d_ssd, v_norm_w=v_norm_w, v_s5_a_re=v_s5_a_re, v_s5_a_im=v_s5_a_im, v_s5_log_dt=v_s5_log_dt, v_s5_b_re=v_s5_b_re, v_s5_b_im=v_s5_b_im, v_s5_c_re=v_s5_c_re, v_s5_c_im=v_s5_c_im, v_s5_d=v_s5_d, v_w_glu=v_w_glu, v_b_glu=v_b_glu, v_w_out=v_w_out, v_ln1_g=v_ln1_g, v_ln1_b=v_ln1_b, v_w1=v_w1, v_b1=v_b1, v_w2=v_w2, v_b2=v_b2, v_ln2_g=v_ln2_g, v_ln2_b=v_ln2_b)
    weights = {n: given[n] for n in TWIN_WEIGHTS}
    shared = {n: given[n] for n in SHARED_INPUTS}
    per_example = {n: given[n] for n in ['x', 'c']}
    grad_fn = _jax.value_and_grad(_loss, argnums=(0, 1))

    def one_microbatch(ex, loss_target):
        ex = dict(ex)
        diff = ex.pop(TWIN_DIFF_INPUT)
        return grad_fn(weights, diff, {**shared, **ex}, loss_target)

    if N_MICROBATCH == 1:
        loss, (grad_w, grad_x) = one_microbatch(per_example, given["loss_target"])
    else:
        def body(carry, xs):
            loss_sum, grad_sum = carry
            l_k, (gw_k, gx_k) = one_microbatch(xs[0], xs[1])
            with _jax.named_scope("update"):
                return (loss_sum + l_k, _jax.tree.map(_jnp.add, grad_sum, gw_k)), gx_k

        init = (_jnp.zeros((), _jnp.float32), _jax.tree.map(_jnp.zeros_like, weights))
        (loss, grad_w), grad_x = _jax.lax.scan(body, init, (per_example, given["loss_target"]))
    with _jax.named_scope("update"):
        delta_w, new_m, new_v = {}, {}, {}
        for n in TWIN_WEIGHTS:
            delta_w[n], new_m[n], new_v[n] = _adamw(weights[n], grad_w[n], given["m_" + n], given["v_" + n])
    return (loss, grad_x, *[grad_w[n] for n in TWIN_WEIGHTS], *[delta_w[n] for n in TWIN_WEIGHTS],
            *[new_m[n] for n in TWIN_WEIGHTS], *[new_v[n] for n in TWIN_WEIGHTS])
```

```python
import functools
import math

import jax
import jax.numpy as jnp
from jax import lax
from jax.experimental import pallas as pl
from jax.experimental.pallas import tpu as pltpu

F32, BF16 = jnp.float32, jnp.bfloat16
MESH = pl.DeviceIdType.MESH
N_DEV = 8

D_MODEL = 1024
D_SSD = 1536
N_HEADS = 24
HEADDIM = 64
N_GROUPS = 4
HPG = 6
GW = HPG * HEADDIM
N_STATE = 128
CHUNK = 128
D_XBC = 2560
D_S5 = 512
S5_GROUPS = 32
S5_CH = 16
S5_P = 64
S5_N = S5_GROUPS * S5_P
D_IN = 4632
DT_PAD = 128
D_INP = D_SSD + D_XBC + DT_PAD + D_S5
D_FF = 4096
N_MOD = 6
ALPHA = 2.0 ** 0.25
EPS = 1e-5
LR, B1, B2, AEPS, WD, STEP = 0.001, 0.9, 0.999, 1e-08, 0.01, 10

NT = (((1,), (1,)), ((), ()))
TN = (((0,), (0,)), ((), ()))
ANY = pl.BlockSpec(memory_space=pl.ANY)
HIGHEST = lax.Precision.HIGHEST
PACK_W = 1024


def _mm(a, b):
    return jnp.dot(a.astype(BF16), b.astype(BF16), preferred_element_type=F32)


def _mm_nt(a, b):
    return lax.dot_general(a.astype(BF16), b.astype(BF16), NT, preferred_element_type=F32)


def _mm_tn(a, b):
    return lax.dot_general(a.astype(BF16), b.astype(BF16), TN, preferred_element_type=F32)


def _row_block(r, cap):
    best = r
    for cand in range(8, min(r, cap) + 1, 8):
        if r % cand == 0:
            best = cand
    return best if best <= cap else r


def _params(vmem_mb):
    return pltpu.CompilerParams(vmem_limit_bytes=vmem_mb << 20)


def _sigmoid(x):
    return 1.0 / (1.0 + jnp.exp(-x))


def _softplus(x):
    return jnp.maximum(x, 0.0) + jnp.log(1.0 + jnp.exp(-jnp.abs(x)))


_GK = math.sqrt(2.0 / math.pi)


def _gelu(x):
    return 0.5 * x * (1.0 + jnp.tanh(_GK * (x + 0.044715 * x * x * x)))


def _gelu_grad(x):
    t = jnp.tanh(_GK * (x + 0.044715 * x * x * x))
    return 0.5 * (1.0 + t) + 0.5 * x * (1.0 - t * t) * _GK * (1.0 + 3.0 * 0.044715 * x * x)


def _dev_index():
    return 4 * lax.axis_index("x") + 2 * lax.axis_index("y") + lax.axis_index("c")


def _all_gather(x, name):
    def body(x_ref, out_ref, send_sems, recv_sems, local_sem):
        ix, iy, ic = lax.axis_index("x"), lax.axis_index("y"), lax.axis_index("c")
        me, sibling = (ix, iy, ic), (ix, iy, 1 - ic)
        chips = [(1 - ix, iy), (ix, 1 - iy), (1 - ix, 1 - iy)]

        def slot(px, py, pc):
            return out_ref.at[4 * px + 2 * py + pc]

        def copy(k, block, to, src=None):
            return pltpu.make_async_remote_copy(
                src_ref=slot(*block) if src is None else src, dst_ref=slot(*block),
                send_sem=send_sems.at[k], recv_sem=recv_sems.at[k], device_id=to, device_id_type=MESH)

        mine = pltpu.make_async_copy(x_ref, slot(*me), local_sem)
        mine.start()
        first = [copy(0, me, sibling, src=x_ref)]
        first += [copy(1 + j, me, (*chip, ic), src=x_ref) for j, chip in enumerate(chips)]
        for cp in first:
            cp.start()
        passed = [copy(4 + j, (*chip, ic), sibling) for j, chip in enumerate(chips)]
        for j, chip in enumerate(chips):
            copy(1 + j, (*chip, ic), me).wait_recv()
            passed[j].start()
        copy(0, sibling, me).wait_recv()
        for j, chip in enumerate(chips):
            copy(4 + j, (*chip, 1 - ic), me).wait_recv()
        for cp in first + passed:
            cp.wait_send()
        mine.wait()

    return pl.pallas_call(
        body, name=name, out_shape=jax.ShapeDtypeStruct((N_DEV,) + x.shape, x.dtype),
        in_specs=[ANY], out_specs=ANY,
        scratch_shapes=[pltpu.SemaphoreType.DMA((7,)), pltpu.SemaphoreType.DMA((7,)), pltpu.SemaphoreType.DMA],
    )(x)


def _sibling_swap(g, name):
    shape = g.shape[1:]

    def body(g_ref, recv_ref, mine_ref, send_sem, recv_sem, local_sem):
        ix, iy, ic = lax.axis_index("x"), lax.axis_index("y"), lax.axis_index("c")
        mine = pltpu.make_async_copy(g_ref.at[ic], mine_ref, local_sem)
        mine.start()
        cp = pltpu.make_async_remote_copy(
            src_ref=g_ref.at[1 - ic], dst_ref=recv_ref, send_sem=send_sem, recv_sem=recv_sem,
            device_id=(ix, iy, 1 - ic), device_id_type=MESH)
        cp.start()
        cp.wait()
        mine.wait()

    return pl.pallas_call(
        body, name=name,
        out_shape=(jax.ShapeDtypeStruct(shape, g.dtype), jax.ShapeDtypeStruct(shape, g.dtype)),
        in_specs=[ANY], out_specs=(ANY, ANY),
        scratch_shapes=[pltpu.SemaphoreType.DMA, pltpu.SemaphoreType.DMA, pltpu.SemaphoreType.DMA],
    )(g)


def _chip_all_to_all(h, name):
    def body(h_ref, out_ref, send_sems, recv_sems, local_sem):
        ix, iy, ic = lax.axis_index("x"), lax.axis_index("y"), lax.axis_index("c")
        me = 2 * ix + iy
        peers = [(1 - ix, iy), (ix, 1 - iy), (1 - ix, 1 - iy)]
        mine = pltpu.make_async_copy(h_ref.at[me], out_ref.at[me], local_sem)
        mine.start()

        def copy(k, src_slot, dst_slot, peer):
            return pltpu.make_async_remote_copy(
                src_ref=h_ref.at[src_slot], dst_ref=out_ref.at[dst_slot],
                send_sem=send_sems.at[k], recv_sem=recv_sems.at[k], device_id=(*peer, ic), device_id_type=MESH)

        sends = [copy(k, 2 * px + py, me, (px, py)) for k, (px, py) in enumerate(peers)]
        for cp in sends:
            cp.start()
        for k, (px, py) in enumerate(peers):
            copy(k, 2 * px + py, 2 * px + py, (px, py)).wait_recv()
        for cp in sends:
            cp.wait_send()
        mine.wait()

    return pl.pallas_call(
        body, name=name, out_shape=jax.ShapeDtypeStruct(h.shape, h.dtype),
        in_specs=[ANY], out_specs=ANY,
        scratch_shapes=[pltpu.SemaphoreType.DMA((3,)), pltpu.SemaphoreType.DMA((3,)), pltpu.SemaphoreType.DMA],
    )(h)


def _add_halves(a, b, name):
    _, r, c = a.shape
    br = _row_block(r, 512)

    def body(a_ref, b_ref, o_ref):
        o_ref[...] = (a_ref[...] + b_ref[...]).astype(BF16)

    spec = pl.BlockSpec((1, br, c), lambda i, j: (i, j, 0))
    return pl.pallas_call(
        body, name=name, out_shape=jax.ShapeDtypeStruct(a.shape, BF16), grid=(4, r // br),
        in_specs=[spec, spec], out_specs=spec, compiler_params=_params(32),
    )(a, b)


def _adamw(parts, w, m, v, name):
    n_parts, r, c = parts.shape
    br = _row_block(r, 512 if c <= 1024 else 256)

    def body(p_ref, w_ref, m_ref, v_ref, g_out, d_out, m_out, v_out):
        g = p_ref[0].astype(F32)
        for p in range(1, n_parts):
            g = g + p_ref[p].astype(F32)
        m2 = B1 * m_ref[...] + (1.0 - B1) * g
        v2 = B2 * v_ref[...] + (1.0 - B2) * (g * g)
        m_hat = m2 / (1.0 - B1 ** STEP)
        v_hat = v2 / (1.0 - B2 ** STEP)
        g_out[...] = g
        d_out[...] = -LR * (m_hat / (jnp.sqrt(v_hat) + AEPS) + WD * w_ref[...])
        m_out[...] = m2
        v_out[...] = v2

    spec = pl.BlockSpec((br, c), lambda i: (i, 0))
    out = jax.ShapeDtypeStruct((r, c), F32)
    return pl.pallas_call(
        body, name=name, out_shape=(out, out, out, out), grid=(r // br,),
        in_specs=[pl.BlockSpec((n_parts, br, c), lambda i: (0, i, 0)), spec, spec, spec],
        out_specs=(spec, spec, spec, spec), compiler_params=_params(40),
    )(parts, w, m, v)


def _atb(a, b, name, bt=512):
    t, k1 = a.shape
    k2 = b.shape[1]

    def pick(k):
        for cand in (1024, 768, 512, 384, 256, 128):
            if k % cand == 0:
                return cand
        return k

    b1, b2 = pick(k1), pick(k2)

    def body(a_ref, b_ref, o_ref):
        @pl.when(pl.program_id(2) == 0)
        def _():
            o_ref[...] = jnp.zeros_like(o_ref)
        o_ref[...] += _mm_tn(a_ref[...], b_ref[...])

    return pl.pallas_call(
        body, name=name, out_shape=jax.ShapeDtypeStruct((k1, k2), F32), grid=(k1 // b1, k2 // b2, t // bt),
        in_specs=[pl.BlockSpec((bt, b1), lambda i, j, k: (k, i)), pl.BlockSpec((bt, b2), lambda i, j, k: (k, j))],
        out_specs=pl.BlockSpec((b1, b2), lambda i, j, k: (i, j)), compiler_params=_params(40),
    )(a, b)


def _mod_fwd(c_all, w_ada, b_cols):
    def body(c_ref, w_ref, b_ref, o_ref):
        cc = c_ref[...]
        cond = cc * _sigmoid(cc)
        o_ref[...] = _mm(cond, w_ref[...]) + b_ref[...]

    return pl.pallas_call(body, name="mod_fwd", out_shape=jax.ShapeDtypeStruct((c_all.shape[0], w_ada.shape[1]), F32),
                          compiler_params=_params(32))(c_all, w_ada, b_cols)


def _mod_bwd(c_all, dmod_cols, dmod_all):
    def body(c_ref, dc_ref, da_ref, gw_ref, gb_ref):
        cc = c_ref[...]
        cond = cc * _sigmoid(cc)
        gw_ref[...] = _mm_tn(cond, dc_ref[...])
        gb_ref[...] = jnp.sum(da_ref[...], axis=0, keepdims=True)

    return pl.pallas_call(
        body, name="mod_bwd",
        out_shape=(jax.ShapeDtypeStruct((D_MODEL, dmod_cols.shape[1]), F32), jax.ShapeDtypeStruct((1, dmod_all.shape[1]), F32)),
        compiler_params=_params(32))(c_all, dmod_cols, dmod_all)


def _load_once(hbm_ref, vmem_ref, sem):
    @pl.when(pl.program_id(0) == 0)
    def _():
        cp = pltpu.make_async_copy(hbm_ref, vmem_ref, sem)
        cp.start()
        cp.wait()


def _proj_fwd(x2, mod3, w_in_pad, seq):
    t = x2.shape[0]
    tb = 512
    npb = seq // tb

    def body(x_ref, mod_ref, w_hbm, z_ref, xbc_ref, dt_ref, u5_ref, w_vmem, sem):
        _load_once(w_hbm, w_vmem, sem)
        m = mod_ref[0]
        u = (x_ref[...] * (1.0 + m[1:2]) + m[0:1]).astype(BF16)
        z_ref[...] = jnp.dot(u, w_vmem[:, 0:D_SSD], preferred_element_type=F32)
        xbc_ref[...] = jnp.dot(u, w_vmem[:, D_SSD:D_SSD + D_XBC], preferred_element_type=F32)
        dt_ref[...] = jnp.dot(u, w_vmem[:, D_SSD + D_XBC:D_SSD + D_XBC + DT_PAD], preferred_element_type=F32)
        u5_ref[...] = jnp.dot(u, w_vmem[:, D_SSD + D_XBC + DT_PAD:], preferred_element_type=F32)

    row = lambda w: pl.BlockSpec((tb, w), lambda i: (i, 0))
    return pl.pallas_call(
        body, name="proj_fwd", grid=(t // tb,),
        out_shape=(jax.ShapeDtypeStruct((t, D_SSD), F32), jax.ShapeDtypeStruct((t, D_XBC), F32),
                   jax.ShapeDtypeStruct((t, DT_PAD), F32), jax.ShapeDtypeStruct((t, D_S5), F32)),
        in_specs=[row(D_MODEL), pl.BlockSpec((1, N_MOD, D_MODEL), lambda i: (i // npb, 0, 0)), ANY],
        out_specs=(row(D_SSD), row(D_XBC), row(DT_PAD), row(D_S5)),
        scratch_shapes=[pltpu.VMEM((D_MODEL, D_INP), BF16), pltpu.SemaphoreType.DMA],
        compiler_params=_params(56),
    )(x2, mod3, w_in_pad)


def _conv_taps(cur, halo, w):
    tb = cur.shape[0]
    xx = jnp.concatenate([halo, cur], axis=0)
    acc = w[3:4] * cur
    shifted = [cur]
    for j in (1, 2, 3):
        sj = pltpu.roll(xx, j, axis=0)[8:8 + tb]
        shifted.append(sj)
        acc = acc + w[3 - j:4 - j] * sj
    return acc, shifted


def _conv_fwd(xbc_pre, conv_w, conv_b, seq):
    t = xbc_pre.shape[0]
    tb = 256
    npb = seq // tb
    cw = 512

    def body(cur_ref, halo_ref, w_ref, b_ref, o_ref):
        first = (pl.program_id(0) % npb) == 0
        halo = jnp.where(first, 0.0, halo_ref[...])
        pre, _ = _conv_taps(cur_ref[...], halo, w_ref[...])
        pre = pre + b_ref[...]
        o_ref[...] = pre * _sigmoid(pre)

    return pl.pallas_call(
        body, name="conv_fwd", grid=(t // tb, D_XBC // cw), out_shape=jax.ShapeDtypeStruct((t, D_XBC), F32),
        in_specs=[pl.BlockSpec((tb, cw), lambda i, j: (i, j)),
                  pl.BlockSpec((8, cw), lambda i, j: (jnp.maximum(i * (tb // 8) - 1, 0), j)),
                  pl.BlockSpec((4, cw), lambda i, j: (0, j)), pl.BlockSpec((1, cw), lambda i, j: (0, j))],
        out_specs=pl.BlockSpec((tb, cw), lambda i, j: (i, j)), compiler_params=_params(32),
    )(xbc_pre, xbc_pre, conv_w, conv_b)


def _ssd_chunk_common(dt_raw, par):
    dtb = par[0:1]
    a = -jnp.exp(par[1:2])
    dt = _softplus(dt_raw + dtb)
    adt = dt * a
    row = lax.broadcasted_iota(jnp.int32, (CHUNK, CHUNK), 0)
    col = lax.broadcasted_iota(jnp.int32, (CHUNK, CHUNK), 1)
    causal = row >= col
    tri = causal.astype(F32)
    cs = jnp.dot(tri, adt, preferred_element_type=F32, precision=HIGHEST)
    return dt, a, cs, cs.T, causal, tri


def _ssd_fwd(xbc, z, dt_raw, par, dsk, normw, seq):
    t = xbc.shape[0]
    nc = seq // CHUNK
    n_chunks = t // CHUNK

    def body(xbc_ref, z_ref, dt_ref, par_ref, dsk_ref, nw_ref, yraw_ref, yssd_ref, hprev_ref, h_ref):
        @pl.when(pl.program_id(0) % nc == 0)
        def _():
            h_ref[...] = jnp.zeros_like(h_ref)
        hprev_ref[0] = h_ref[...]
        dt, _, cs, cst, causal, _ = _ssd_chunk_common(dt_ref[...], par_ref[...])
        cs_last = cs[CHUNK - 1:CHUNK, :]
        zz = z_ref[...]
        silu_z = zz * _sigmoid(zz)
        for g in range(N_GROUPS):
            bg = xbc_ref[:, D_SSD + g * N_STATE:D_SSD + (g + 1) * N_STATE].astype(BF16)
            cg = xbc_ref[:, D_SSD + N_GROUPS * N_STATE + g * N_STATE:D_SSD + N_GROUPS * N_STATE + (g + 1) * N_STATE].astype(BF16)
            scores = lax.dot_general(cg, bg, NT, preferred_element_type=F32)
            hg = h_ref[g * GW:(g + 1) * GW, :]
            p_all = lax.dot_general(cg, hg.astype(BF16), NT, preferred_element_type=F32)
            ys, a_cols = [], []
            for j in range(HPG):
                hh = g * HPG + j
                cs_col = cs[:, hh:hh + 1]
                cs_row = cst[hh:hh + 1, :]
                decay = jnp.exp(jnp.where(causal, cs_col - cs_row, -jnp.inf))
                xh = xbc_ref[:, hh * HEADDIM:(hh + 1) * HEADDIM]
                xdt = xh * dt[:, hh:hh + 1]
                y = _mm(scores * decay, xdt)
                y = y + jnp.exp(cs_col) * p_all[:, j * HEADDIM:(j + 1) * HEADDIM]
                y = y + xh * dsk_ref[:, hh * HEADDIM:(hh + 1) * HEADDIM]
                ys.append(y)
                a_cols.append(jnp.exp(cs_last[:, hh:hh + 1] - cs_col) * xdt)
            s_new = _mm_tn(jnp.concatenate(a_cols, axis=1), bg)
            for j in range(HPG):
                hh = g * HPG + j
                rows = slice(g * GW + j * HEADDIM, g * GW + (j + 1) * HEADDIM)
                h_ref[rows, :] = (hg[j * HEADDIM:(j + 1) * HEADDIM, :] * jnp.exp(cs_last[:, hh:hh + 1])
                                  + s_new[j * HEADDIM:(j + 1) * HEADDIM, :])
            yg = jnp.concatenate(ys, axis=1)
            yraw_ref[:, g * GW:(g + 1) * GW] = yg
            v = yg * silu_z[:, g * GW:(g + 1) * GW]
            r = lax.rsqrt(jnp.mean(v * v, axis=-1, keepdims=True) + EPS)
            yssd_ref[:, g * GW:(g + 1) * GW] = (v * r * nw_ref[:, g * GW:(g + 1) * GW]).astype(BF16)

    row = lambda w: pl.BlockSpec((CHUNK, w), lambda i: (i, 0))
    full = lambda s: pl.BlockSpec(s, lambda i: (0, 0))
    return pl.pallas_call(
        body, name="ssd_fwd", grid=(n_chunks,),
        out_shape=(jax.ShapeDtypeStruct((t, D_SSD), F32), jax.ShapeDtypeStruct((t, D_SSD), BF16),
                   jax.ShapeDtypeStruct((n_chunks, D_SSD, N_STATE), F32)),
        in_specs=[row(D_XBC), row(D_SSD), row(DT_PAD), full((8, 128)), full((1, D_SSD)), full((1, D_SSD))],
        out_specs=(row(D_SSD), row(D_SSD), pl.BlockSpec((1, D_SSD, N_STATE), lambda i: (i, 0, 0))),
        scratch_shapes=[pltpu.VMEM((D_SSD, N_STATE), F32)],
        compiler_params=_params(40),
    )(xbc, z, dt_raw, par, dsk, normw)


S5_CW = 512


def _tile_scan(in_re, in_im, out_re, out_im, carry_re, carry_im, pw_re, pw_im, n_tiles, reverse):
    steps = (1, 2, 4)
    for cc in range(S5_N // S5_CW):
        cols = slice(cc * S5_CW, (cc + 1) * S5_CW)
        if reverse:
            pows = [(pw_re[8 - d:9 - d, cols], pw_im[8 - d:9 - d, cols]) for d in steps]
        else:
            pows = [(pw_re[d - 1:d, cols], pw_im[d - 1:d, cols]) for d in steps]
        a_re, a_im = pw_re[:, cols], pw_im[:, cols]
        rid = lax.broadcasted_iota(jnp.int32, (8, S5_CW), 0)

        def tile(i, carry, cols=cols, pows=pows, a_re=a_re, a_im=a_im, rid=rid):
            r = (n_tiles - 1 - i) if reverse else i
            rows = pl.ds(pl.multiple_of(r * 8, 8), 8)
            xr, xi = in_re[rows, cols], in_im[rows, cols]
            for (pr, pi), d in zip(pows, steps):
                if reverse:
                    sr, si = pltpu.roll(xr, 8 - d, axis=0), pltpu.roll(xi, 8 - d, axis=0)
                    keep = rid < 8 - d
                else:
                    sr, si = pltpu.roll(xr, d, axis=0), pltpu.roll(xi, d, axis=0)
                    keep = rid >= d
                sr, si = jnp.where(keep, sr, 0.0), jnp.where(keep, si, 0.0)
                xr, xi = xr + pr * sr - pi * si, xi + pr * si + pi * sr
            cr, ci = carry
            xr, xi = xr + a_re * cr - a_im * ci, xi + a_re * ci + a_im * cr
            out_re[rows, cols] = xr
            out_im[rows, cols] = xi
            edge = slice(0, 1) if reverse else slice(7, 8)
            return (jnp.broadcast_to(xr[edge], (8, S5_CW)), jnp.broadcast_to(xi[edge], (8, S5_CW)))

        c0 = (jnp.broadcast_to(carry_re[0:1, cols], (8, S5_CW)), jnp.broadcast_to(carry_im[0:1, cols], (8, S5_CW)))
        cr, ci = lax.fori_loop(0, n_tiles, tile, c0)
        carry_re[:, cols] = cr
        carry_im[:, cols] = ci


def _s5_params_math(ar, ai, ldt, br, bi):
    dt = jnp.exp(ldt)
    mag = jnp.exp(ar * dt)
    ang = ai * dt
    ab_re = mag * jnp.cos(ang)
    ab_im = mag * jnp.sin(ang)
    den = ar * ar + ai * ai
    n_re = ab_re - 1.0
    coef_re = (n_re * ar + ab_im * ai) / den
    coef_im = (ab_im * ar - n_re * ai) / den
    bb_re = coef_re * br - coef_im * bi
    bb_im = coef_re * bi + coef_im * br
    return ab_re, ab_im, bb_re, bb_im


def _s5_params_fwd(ar, ai, ldt, br, bi):
    def body(ar_ref, ai_ref, ldt_ref, br_ref, bi_ref, bbr_ref, bbi_ref, pfr_ref, pfi_ref, prr_ref, pri_ref):
        ab_re, ab_im, bb_re, bb_im = _s5_params_math(ar_ref[...], ai_ref[...], ldt_ref[...], br_ref[...], bi_ref[...])
        bbr_ref[...] = bb_re
        bbi_ref[...] = bb_im
        pr, pi = ab_re, ab_im
        for k in range(8):
            pfr_ref[k:k + 1, :] = pr
            pfi_ref[k:k + 1, :] = pi
            prr_ref[7 - k:8 - k, :] = pr
            pri_ref[7 - k:8 - k, :] = -pi
            pr, pi = pr * ab_re - pi * ab_im, pr * ab_im + pi * ab_re

    b16 = jax.ShapeDtypeStruct((S5_CH, S5_N), F32)
    p8 = jax.ShapeDtypeStruct((8, S5_N), F32)
    return pl.pallas_call(body, name="s5_params_fwd", out_shape=(b16, b16, p8, p8, p8, p8),
                          compiler_params=_params(32))(ar, ai, ldt, br, bi)


def _s5_params_bwd(ar, ai, ldt, br, bi, d_ab_re, d_ab_im, d_bb_re, d_bb_im):
    def body(ar_ref, ai_ref, ldt_ref, br_ref, bi_ref, dar_ref, dai_ref, dbr_ref, dbi_ref,
             gar_ref, gai_ref, gldt_ref, gbr_ref, gbi_ref):
        _, vjp = jax.vjp(_s5_params_math, ar_ref[...], ai_ref[...], ldt_ref[...], br_ref[...], bi_ref[...])
        g_ar, g_ai, g_ldt, g_br, g_bi = vjp((dar_ref[...], dai_ref[...], dbr_ref[...], dbi_ref[...]))
        gar_ref[...] = g_ar
        gai_ref[...] = g_ai
        gbr_ref[...] = g_br
        gbi_ref[...] = g_bi
        lane = lax.broadcasted_iota(jnp.int32, (S5_N, 128), 0) // S5_P
        grp = lax.broadcasted_iota(jnp.int32, (S5_N, 128), 1)
        fold = (lane == grp).astype(F32)
        gldt_ref[...] = jnp.dot(g_ldt, fold, preferred_element_type=F32, precision=HIGHEST)

    v1 = jax.ShapeDtypeStruct((1, S5_N), F32)
    b16 = jax.ShapeDtypeStruct((S5_CH, S5_N), F32)
    return pl.pallas_call(body, name="s5_params_bwd",
                          out_shape=(v1, v1, jax.ShapeDtypeStruct((1, 128), F32), b16, b16),
                          compiler_params=_params(32))(ar, ai, ldt, br, bi, d_ab_re, d_ab_im, d_bb_re, d_bb_im)


def _s5_fwd(u5, bb_re, bb_im, cc_re, cc_im, pf_re, pf_im, s5d, w_glu, b_glu, seq):
    t = u5.shape[0]
    tb = 256
    npb = seq // tb

    def body(u_ref, bbr_ref, bbi_ref, ccr_ref, cci_ref, pfr_ref, pfi_ref, d_ref, wg_ref, bg_ref,
             sre_ref, sim_ref, ypre_ref, y5_ref, bur, bui, car, cai):
        @pl.when(pl.program_id(0) % npb == 0)
        def _():
            car[...] = jnp.zeros_like(car)
            cai[...] = jnp.zeros_like(cai)
        u = u_ref[...]
        ub = u.astype(BF16)
        bur[...] = jnp.dot(ub, bbr_ref[...], preferred_element_type=F32)
        bui[...] = jnp.dot(ub, bbi_ref[...], preferred_element_type=F32)
        _tile_scan(bur, bui, sre_ref, sim_ref, car, cai, pfr_ref, pfi_ref, tb // 8, reverse=False)
        ypre = _mm(sre_ref[...], ccr_ref[...]) - _mm(sim_ref[...], cci_ref[...]) + u * d_ref[...]
        ypre_ref[...] = ypre
        yg = _gelu(ypre)
        y5_ref[...] = (yg * _sigmoid(_mm(yg, wg_ref[...]) + bg_ref[...])).astype(BF16)

    row = lambda w: pl.BlockSpec((tb, w), lambda i: (i, 0))
    full = lambda a: pl.BlockSpec(a.shape, lambda i: (0, 0))
    return pl.pallas_call(
        body, name="s5_fwd", grid=(t // tb,),
        out_shape=(jax.ShapeDtypeStruct((t, S5_N), F32), jax.ShapeDtypeStruct((t, S5_N), F32),
                   jax.ShapeDtypeStruct((t, D_S5), F32), jax.ShapeDtypeStruct((t, D_S5), BF16)),
        in_specs=[row(D_S5), full(bb_re), full(bb_im), full(cc_re), full(cc_im), full(pf_re), full(pf_im),
                  full(s5d), full(w_glu), full(b_glu)],
        out_specs=(row(S5_N), row(S5_N), row(D_S5), row(D_S5)),
        scratch_shapes=[pltpu.VMEM((tb, S5_N), F32), pltpu.VMEM((tb, S5_N), F32),
                        pltpu.VMEM((8, S5_N), F32), pltpu.VMEM((8, S5_N), F32)],
        compiler_params=_params(48),
    )(u5, bb_re, bb_im, cc_re, cc_im, pf_re, pf_im, s5d, w_glu, b_glu)


def _layer_norm(r, g, b):
    mu = jnp.mean(r, axis=-1, keepdims=True)
    xc = r - mu
    rstd = lax.rsqrt(jnp.mean(xc * xc, axis=-1, keepdims=True) + EPS)
    xhat = xc * rstd
    return xhat * g + b, xhat, rstd


def _layer_norm_bwd(dy, xhat, rstd, g):
    dxhat = dy * g
    return rstd * (dxhat - jnp.mean(dxhat, axis=-1, keepdims=True)
                   - xhat * jnp.mean(dxhat * xhat, axis=-1, keepdims=True))


def _out_ln1(yssd, y5, x2, mod3, w_out, ln1, seq):
    t = x2.shape[0]
    tb = 512
    npb = seq // tb

    def body(ys_ref, y5_ref, x_ref, mod_ref, w_ref, ln_ref, mix_ref, x1_ref):
        m = mod_ref[0]
        mix = (jnp.dot(ys_ref[...], w_ref[0:D_SSD, :], preferred_element_type=F32)
               + jnp.dot(y5_ref[...], w_ref[D_SSD:, :], preferred_element_type=F32))
        mix_ref[...] = mix
        r1 = ALPHA * x_ref[...] + (1.0 + m[2:3]) * mix
        x1_ref[...] = _layer_norm(r1, ln_ref[0:1], ln_ref[1:2])[0]

    row = lambda w: pl.BlockSpec((tb, w), lambda i: (i, 0))
    return pl.pallas_call(
        body, name="out_ln1", grid=(t // tb,),
        out_shape=(jax.ShapeDtypeStruct((t, D_MODEL), F32), jax.ShapeDtypeStruct((t, D_MODEL), F32)),
        in_specs=[row(D_SSD), row(D_S5), row(D_MODEL), pl.BlockSpec((1, N_MOD, D_MODEL), lambda i: (i // npb, 0, 0)),
                  pl.BlockSpec(w_out.shape, lambda i: (0, 0)), pl.BlockSpec(ln1.shape, lambda i: (0, 0))],
        out_specs=(row(D_MODEL), row(D_MODEL)), compiler_params=_params(48),
    )(yssd, y5, x2, mod3, w_out, ln1)


def _mlp_fwd_bwd(x1, tgt, mod3, w1, w2, vec1, b1, seq):
    t = x1.shape[0]
    tb = 256
    npb = seq // tb

    def body(x1_ref, tgt_ref, mod_ref, w1_hbm, w2_hbm, v_ref, b1_ref,
             dx1_ref, u2_ref, h_ref, dhp_ref, do_ref, gacc_ref, db1_ref, bacc_ref, w1_v, w2_v, sem1, sem2):
        i = pl.program_id(0)
        _load_once(w1_hbm, w1_v, sem1)
        _load_once(w2_hbm, w2_v, sem2)

        @pl.when(i == 0)
        def _():
            gacc_ref[...] = jnp.zeros_like(gacc_ref)
            db1_ref[...] = jnp.zeros_like(db1_ref)

        @pl.when(i % npb == 0)
        def _():
            bacc_ref[...] = jnp.zeros_like(bacc_ref)

        m = mod_ref[0]
        sh2, sc2, g2 = m[3:4], m[4:5], m[5:6]
        x1v = x1_ref[...]
        u2 = (x1v * (1.0 + sc2) + sh2).astype(BF16)
        u2_ref[...] = u2
        hpre = jnp.dot(u2, w1_v[...], preferred_element_type=F32) + b1_ref[...]
        hr = jnp.maximum(hpre, 0.0)
        hb = (hr * hr).astype(BF16)
        h_ref[...] = hb
        o = jnp.dot(hb, w2_v[...], preferred_element_type=F32) + v_ref[0:1]
        r2 = ALPHA * x1v + (1.0 + g2) * o
        y, xhat, rstd = _layer_norm(r2, v_ref[1:2], v_ref[2:3])
        err = y - tgt_ref[...]
        dy = err * (1.0 / D_MODEL)
        dr2 = _layer_norm_bwd(dy, xhat, rstd, v_ref[1:2])
        do = (1.0 + g2) * dr2
        dob = do.astype(BF16)
        do_ref[...] = dob
        gacc_ref[0:1, :] += jnp.sum(dy * xhat, axis=0, keepdims=True)
        gacc_ref[1:2, :] += jnp.sum(dy, axis=0, keepdims=True)
        gacc_ref[2:3, :] += jnp.sum(do, axis=0, keepdims=True)
        gacc_ref[3:4, :] += jnp.sum(err * err, axis=0, keepdims=True)
        dh = lax.dot_general(dob, w2_v[...], NT, preferred_element_type=F32)
        dhpre = dh * (2.0 * hr)
        dhpb = dhpre.astype(BF16)
        dhp_ref[...] = dhpb
        db1_ref[...] += jnp.sum(dhpre, axis=0, keepdims=True)
        du2 = lax.dot_general(dhpb, w1_v[...], NT, preferred_element_type=F32)
        dx1_ref[...] = ALPHA * dr2 + du2 * (1.0 + sc2)
        bacc_ref[0, 0:1, :] += jnp.sum(du2, axis=0, keepdims=True)
        bacc_ref[0, 1:2, :] += jnp.sum(du2 * x1v, axis=0, keepdims=True)
        bacc_ref[0, 2:3, :] += jnp.sum(dr2 * o, axis=0, keepdims=True)

    row = lambda w: pl.BlockSpec((tb, w), lambda i: (i, 0))
    return pl.pallas_call(
        body, name="mlp_fwd_bwd", grid=(t // tb,),
        out_shape=(jax.ShapeDtypeStruct((t, D_MODEL), F32), jax.ShapeDtypeStruct((t, D_MODEL), BF16),
                   jax.ShapeDtypeStruct((t, D_FF), BF16), jax.ShapeDtypeStruct((t, D_FF), BF16),
                   jax.ShapeDtypeStruct((t, D_MODEL), BF16), jax.ShapeDtypeStruct((8, D_MODEL), F32),
                   jax.ShapeDtypeStruct((1, D_FF), F32), jax.ShapeDtypeStruct((t // seq, 8, D_MODEL), F32)),
        in_specs=[row(D_MODEL), row(D_MODEL), pl.BlockSpec((1, N_MOD, D_MODEL), lambda i: (i // npb, 0, 0)), ANY, ANY,
                  pl.BlockSpec(vec1.shape, lambda i: (0, 0)), pl.BlockSpec(b1.shape, lambda i: (0, 0))],
        out_specs=(row(D_MODEL), row(D_MODEL), row(D_FF), row(D_FF), row(D_MODEL),
                   pl.BlockSpec((8, D_MODEL), lambda i: (0, 0)), pl.BlockSpec((1, D_FF), lambda i: (0, 0)),
                   pl.BlockSpec((1, 8, D_MODEL), lambda i: (i // npb, 0, 0))),
        scratch_shapes=[pltpu.VMEM((D_MODEL, D_FF), BF16), pltpu.VMEM((D_FF, D_MODEL), BF16),
                        pltpu.SemaphoreType.DMA, pltpu.SemaphoreType.DMA],
        compiler_params=_params(60),
    )(x1, tgt, mod3, w1, w2, vec1, b1)


def _ln1_out_bwd(dx1, x2, mix, mod3, w_out, ln1, seq):
    t = x2.shape[0]
    tb = 512
    npb = seq // tb

    def body(dx1_ref, x_ref, mix_ref, mod_ref, w_ref, ln_ref, dmix_ref, dxa_ref, dys_ref, dy5_ref, gacc_ref, bacc_ref):
        i = pl.program_id(0)

        @pl.when(i == 0)
        def _():
            gacc_ref[...] = jnp.zeros_like(gacc_ref)

        @pl.when(i % npb == 0)
        def _():
            bacc_ref[...] = jnp.zeros_like(bacc_ref)

        m = mod_ref[0]
        mix = mix_ref[...]
        r1 = ALPHA * x_ref[...] + (1.0 + m[2:3]) * mix
        _, xhat, rstd = _layer_norm(r1, ln_ref[0:1], ln_ref[1:2])
        dx1v = dx1_ref[...]
        dr1 = _layer_norm_bwd(dx1v, xhat, rstd, ln_ref[0:1])
        gacc_ref[0:1, :] += jnp.sum(dx1v * xhat, axis=0, keepdims=True)
        gacc_ref[1:2, :] += jnp.sum(dx1v, axis=0, keepdims=True)
        bacc_ref[0, 0:1, :] += jnp.sum(dr1 * mix, axis=0, keepdims=True)
        dmix = ((1.0 + m[2:3]) * dr1).astype(BF16)
        dmix_ref[...] = dmix
        dxa_ref[...] = ALPHA * dr1
        dys_ref[...] = lax.dot_general(dmix, w_ref[0:D_SSD, :], NT, preferred_element_type=F32)
        dy5_ref[...] = lax.dot_general(dmix, w_ref[D_SSD:, :], NT, preferred_element_type=F32)

    row = lambda w: pl.BlockSpec((tb, w), lambda i: (i, 0))
    return pl.pallas_call(
        body, name="ln1_out_bwd", grid=(t // tb,),
        out_shape=(jax.ShapeDtypeStruct((t, D_MODEL), BF16), jax.ShapeDtypeStruct((t, D_MODEL), F32),
                   jax.ShapeDtypeStruct((t, D_SSD), F32), jax.ShapeDtypeStruct((t, D_S5), F32),
                   jax.ShapeDtypeStruct((8, D_MODEL), F32), jax.ShapeDtypeStruct((t // seq, 8, D_MODEL), F32)),
        in_specs=[row(D_MODEL), row(D_MODEL), row(D_MODEL), pl.BlockSpec((1, N_MOD, D_MODEL), lambda i: (i // npb, 0, 0)),
                  pl.BlockSpec(w_out.shape, lambda i: (0, 0)), pl.BlockSpec(ln1.shape, lambda i: (0, 0))],
        out_specs=(row(D_MODEL), row(D_MODEL), row(D_SSD), row(D_S5), pl.BlockSpec((8, D_MODEL), lambda i: (0, 0)),
                   pl.BlockSpec((1, 8, D_MODEL), lambda i: (i // npb, 0, 0))),
        compiler_params=_params(48),
    )(dx1, x2, mix, mod3, w_out, ln1)


def _s5_bwd(dy5, ypre, u5, s_re, s_im, bb_re, bb_im, cc_re, cc_im, pr_re, pr_im, s5d, w_glu, b_glu, seq):
    t = u5.shape[0]
    tb = 256
    npb = seq // tb
    n_blocks = t // tb

    def blk(i):
        return (i // npb) * npb + (npb - 1 - i % npb)

    def body(dy_ref, ypre_ref, u_ref, sre_ref, sim_ref, hre_ref, him_ref, bbr_ref, bbi_ref, ccr_ref, cci_ref,
             prr_ref, pri_ref, d_ref, wg_ref, bg_ref,
             du_ref, gre_ref, gim_ref, yg_ref, dq_ref, dyp_ref, vacc_ref, sacc_ref, dsr, dsi, gr, gi, car, cai):
        i = pl.program_id(0)

        @pl.when(i == 0)
        def _():
            vacc_ref[...] = jnp.zeros_like(vacc_ref)
            sacc_ref[...] = jnp.zeros_like(sacc_ref)

        @pl.when(i % npb == 0)
        def _():
            car[...] = jnp.zeros_like(car)
            cai[...] = jnp.zeros_like(cai)

        dy = dy_ref[...]
        ypre = ypre_ref[...]
        u = u_ref[...]
        yg = _gelu(ypre)
        sg = _sigmoid(_mm(yg, wg_ref[...]) + bg_ref[...])
        dq = dy * yg * sg * (1.0 - sg)
        dyg = dy * sg + _mm_nt(dq, wg_ref[...])
        dyp = dyg * _gelu_grad(ypre)
        yg_ref[...] = yg.astype(BF16)
        dq_ref[...] = dq.astype(BF16)
        dypb = dyp.astype(BF16)
        dyp_ref[...] = dypb
        dsr[...] = lax.dot_general(dypb, ccr_ref[...], NT, preferred_element_type=F32)
        dsi[...] = -lax.dot_general(dypb, cci_ref[...], NT, preferred_element_type=F32)
        _tile_scan(dsr, dsi, gr, gi, car, cai, prr_ref, pri_ref, tb // 8, reverse=True)
        g_re, g_im = gr[...], gi[...]
        first_rows = (i % npb) == npb - 1
        hre = jnp.where(first_rows, 0.0, hre_ref[...])
        him = jnp.where(first_rows, 0.0, him_ref[...])
        sp_re = pltpu.roll(jnp.concatenate([hre, sre_ref[...]], axis=0), 1, axis=0)[8:8 + tb]
        sp_im = pltpu.roll(jnp.concatenate([him, sim_ref[...]], axis=0), 1, axis=0)[8:8 + tb]
        vacc_ref[0:1, :] += jnp.sum(g_re * sp_re + g_im * sp_im, axis=0, keepdims=True)
        vacc_ref[1:2, :] += jnp.sum(g_im * sp_re - g_re * sp_im, axis=0, keepdims=True)
        grb, gib = g_re.astype(BF16), g_im.astype(BF16)
        gre_ref[...] = grb
        gim_ref[...] = gib
        du_ref[...] = (lax.dot_general(grb, bbr_ref[...], NT, preferred_element_type=F32)
                       + lax.dot_general(gib, bbi_ref[...], NT, preferred_element_type=F32) + dyp * d_ref[...])
        sacc_ref[0:1, :] += jnp.sum(dyp * u, axis=0, keepdims=True)
        sacc_ref[1:2, :] += jnp.sum(dq, axis=0, keepdims=True)

    row = lambda w: pl.BlockSpec((tb, w), lambda i: (blk(i), 0))
    halo = pl.BlockSpec((8, S5_N), lambda i: (jnp.maximum(blk(i) * (tb // 8) - 1, 0), 0))
    full = lambda a: pl.BlockSpec(a.shape, lambda i: (0, 0))
    return pl.pallas_call(
        body, name="s5_bwd", grid=(n_blocks,),
        out_shape=(jax.ShapeDtypeStruct((t, D_S5), F32), jax.ShapeDtypeStruct((t, S5_N), BF16),
                   jax.ShapeDtypeStruct((t, S5_N), BF16), jax.ShapeDtypeStruct((t, D_S5), BF16),
                   jax.ShapeDtypeStruct((t, D_S5), BF16), jax.ShapeDtypeStruct((t, D_S5), BF16),
                   jax.ShapeDtypeStruct((8, S5_N), F32), jax.ShapeDtypeStruct((8, D_S5), F32)),
        in_specs=[row(D_S5), row(D_S5), row(D_S5), row(S5_N), row(S5_N), halo, halo, full(bb_re), full(bb_im),
                  full(cc_re), full(cc_im), full(pr_re), full(pr_im), full(s5d), full(w_glu), full(b_glu)],
        out_specs=(row(D_S5), row(S5_N), row(S5_N), row(D_S5), row(D_S5), row(D_S5),
                   pl.BlockSpec((8, S5_N), lambda i: (0, 0)), pl.BlockSpec((8, D_S5), lambda i: (0, 0))),
        scratch_shapes=[pltpu.VMEM((tb, S5_N), F32), pltpu.VMEM((tb, S5_N), F32), pltpu.VMEM((tb, S5_N), F32),
                        pltpu.VMEM((tb, S5_N), F32), pltpu.VMEM((8, S5_N), F32), pltpu.VMEM((8, S5_N), F32)],
        compiler_params=_params(56),
    )(dy5, ypre, u5, s_re, s_im, s_re, s_im, bb_re, bb_im, cc_re, cc_im, pr_re, pr_im, s5d, w_glu, b_glu)


def _ssd_bwd(dyssd, yraw, z, xbc, dt_raw, hprev, par, dsk, normw, seq):
    t = xbc.shape[0]
    nc = seq // CHUNK
    n_chunks = t // CHUNK

    def blk(i):
        return (i // nc) * nc + (nc - 1 - i % nc)

    def body(dy_ref, yraw_ref, z_ref, xbc_ref, dt_ref, hprev_ref, par_ref, dsk_ref, nw_ref,
             dxbc_ref, dz_ref, ddt_ref, dpar_ref, cacc_ref, dh_ref, dyr_ref):
        i = pl.program_id(0)

        @pl.when(i == 0)
        def _():
            dpar_ref[...] = jnp.zeros_like(dpar_ref)
            cacc_ref[...] = jnp.zeros_like(cacc_ref)

        @pl.when(i % nc == 0)
        def _():
            dh_ref[...] = jnp.zeros_like(dh_ref)

        zz = z_ref[...]
        sz = _sigmoid(zz)
        silu_z = zz * sz
        yraw = yraw_ref[...]
        for g in range(N_GROUPS):
            sl = slice(g * GW, (g + 1) * GW)
            v = yraw[:, sl] * silu_z[:, sl]
            r = lax.rsqrt(jnp.mean(v * v, axis=-1, keepdims=True) + EPS)
            dyg = dy_ref[:, sl]
            cacc_ref[1:2, sl] += jnp.sum(dyg * v * r, axis=0, keepdims=True)
            dyw = dyg * nw_ref[:, sl]
            dv = r * dyw - v * (r * r * r) * jnp.mean(dyw * v, axis=-1, keepdims=True)
            dyr_ref[:, sl] = dv * silu_z[:, sl]
            dz_ref[:, sl] = dv * yraw[:, sl] * (sz[:, sl] * (1.0 + zz[:, sl] * (1.0 - sz[:, sl])))

        dt, a, cs, cst, causal, tri = _ssd_chunk_common(dt_ref[...], par_ref[...])
        cs_last = cs[CHUNK - 1:CHUNK, :]
        lane = lax.broadcasted_iota(jnp.int32, (CHUNK, 128), 1)
        rowid = lax.broadcasted_iota(jnp.int32, (CHUNK, 128), 0)
        ddt = jnp.zeros((CHUNK, 128), F32)
        dcs = jnp.zeros((CHUNK, 128), F32)
        dcs_t = jnp.zeros((CHUNK, 128), F32)
        dcs_last = jnp.zeros((1, 128), F32)
        lane1 = lax.broadcasted_iota(jnp.int32, (1, 128), 1)
        for g in range(N_GROUPS):
            b_sl = slice(D_SSD + g * N_STATE, D_SSD + (g + 1) * N_STATE)
            c_sl = slice(D_SSD + N_GROUPS * N_STATE + g * N_STATE, D_SSD + N_GROUPS * N_STATE + (g + 1) * N_STATE)
            bg = xbc_ref[:, b_sl].astype(BF16)
            cg = xbc_ref[:, c_sl].astype(BF16)
            scores = lax.dot_general(cg, bg, NT, preferred_element_type=F32)
            hg = hprev_ref[0, g * GW:(g + 1) * GW, :]
            hgb = hg.astype(BF16)
            dhg = dh_ref[g * GW:(g + 1) * GW, :]
            dhgb = dhg.astype(BF16)
            p_all = lax.dot_general(cg, hgb, NT, preferred_element_type=F32)
            q_all = lax.dot_general(bg, dhgb, NT, preferred_element_type=F32)
            dscores = jnp.zeros((CHUNK, CHUNK), F32)
            dp_cols, a_cols = [], []
            for j in range(HPG):
                hh = g * HPG + j
                hs = slice(hh * HEADDIM, (hh + 1) * HEADDIM)
                js = slice(j * HEADDIM, (j + 1) * HEADDIM)
                cs_col = cs[:, hh:hh + 1]
                cs_row = cst[hh:hh + 1, :]
                decay = jnp.exp(jnp.where(causal, cs_col - cs_row, -jnp.inf))
                mmat = scores * decay
                xh = xbc_ref[:, hs]
                dtc = dt[:, hh:hh + 1]
                xdt = xh * dtc
                dyh = dyr_ref[:, hs]
                dm = _mm_nt(dyh, xdt)
                dxdt = _mm_tn(mmat, dyh)
                dscores = dscores + dm * decay
                e = dm * mmat
                dcs_c = jnp.sum(e, axis=1, keepdims=True)
                dcs_r = jnp.sum(e, axis=0, keepdims=True)
                ecs = jnp.exp(cs_col)
                dcs_c = dcs_c + jnp.sum(dyh * (ecs * p_all[:, js]), axis=1, keepdims=True)
                dp_cols.append(ecs * dyh)
                cl = cs_last[:, hh:hh + 1]
                w = jnp.exp(cl - cs_col)
                qh = q_all[:, js]
                dxdt = dxdt + w * qh
                dww = jnp.sum(qh * xdt, axis=1, keepdims=True) * w
                dcs_c = dcs_c - dww
                ecl = jnp.exp(cl)
                dlast = jnp.sum(dww) + ecl * jnp.sum(dhg[js, :] * hg[js, :])
                a_cols.append(w * xdt)
                dxbc_ref[:, hs] = dxdt * dtc + dyh * dsk_ref[:, hs]
                cacc_ref[0:1, hs] += jnp.sum(dyh * xh, axis=0, keepdims=True)
                ddt_c = jnp.sum(dxdt * xh, axis=1, keepdims=True)
                ddt = ddt + jnp.where(lane == hh, ddt_c, 0.0)
                dcs = dcs + jnp.where(lane == hh, dcs_c, 0.0)
                dcs_t = dcs_t + jnp.where(rowid == hh, dcs_r, 0.0)
                dcs_last = dcs_last + jnp.where(lane1 == hh, dlast, 0.0)
                dh_ref[g * GW + j * HEADDIM:g * GW + (j + 1) * HEADDIM, :] = ecl * dhg[js, :]
            dp = jnp.concatenate(dp_cols, axis=1).astype(BF16)
            amat = jnp.concatenate(a_cols, axis=1).astype(BF16)
            dsb = dscores.astype(BF16)
            dxbc_ref[:, c_sl] = (jnp.dot(dsb, bg, preferred_element_type=F32)
                                 + jnp.dot(dp, hgb, preferred_element_type=F32))
            dxbc_ref[:, b_sl] = (lax.dot_general(dsb, cg, TN, preferred_element_type=F32)
                                 + jnp.dot(amat, dhgb, preferred_element_type=F32))
            dh_ref[g * GW:(g + 1) * GW, :] += lax.dot_general(dp, cg, TN, preferred_element_type=F32)
        dcs = dcs - dcs_t.T
        dcs = dcs + jnp.where(rowid == CHUNK - 1, dcs_last, 0.0)
        dadt = lax.dot_general(tri, dcs, TN, preferred_element_type=F32, precision=HIGHEST)
        ddt = ddt + dadt * a
        da = jnp.sum(dadt * dt, axis=0, keepdims=True)
        ddt_raw = ddt * _sigmoid(dt_ref[...] + par_ref[0:1])
        ddt_raw = jnp.where(lane < N_HEADS, ddt_raw, 0.0)
        ddt_ref[...] = ddt_raw
        dpar_ref[0:1, :] += jnp.sum(ddt_raw, axis=0, keepdims=True)
        dpar_ref[1:2, :] += jnp.where(lane1 < N_HEADS, da * a, 0.0)

    row = lambda w: pl.BlockSpec((CHUNK, w), lambda i: (blk(i), 0))
    full = lambda s: pl.BlockSpec(s, lambda i: (0, 0))
    return pl.pallas_call(
        body, name="ssd_bwd", grid=(n_chunks,),
        out_shape=(jax.ShapeDtypeStruct((t, D_XBC), F32), jax.ShapeDtypeStruct((t, D_SSD), F32),
                   jax.ShapeDtypeStruct((t, DT_PAD), F32), jax.ShapeDtypeStruct((8, 128), F32),
                   jax.ShapeDtypeStruct((8, D_SSD), F32)),
        in_specs=[row(D_SSD), row(D_SSD), row(D_SSD), row(D_XBC), row(DT_PAD),
                  pl.BlockSpec((1, D_SSD, N_STATE), lambda i: (blk(i), 0, 0)),
                  full((8, 128)), full((1, D_SSD)), full((1, D_SSD))],
        out_specs=(row(D_XBC), row(D_SSD), row(DT_PAD), full((8, 128)), full((8, D_SSD))),
        scratch_shapes=[pltpu.VMEM((D_SSD, N_STATE), F32), pltpu.VMEM((CHUNK, D_SSD), F32)],
        compiler_params=_params(48),
    )(dyssd, yraw, z, xbc, dt_raw, hprev, par, dsk, normw)


def _conv_bwd(dxbc, xbc_pre, conv_w, conv_b, seq):
    t = xbc_pre.shape[0]
    tb = 256
    npb = seq // tb
    cw = 512

    def body(d_ref, cur_ref, halo_ref, w_ref, b_ref, o_ref, acc_ref):
        i = pl.program_id(1)

        @pl.when(i == 0)
        def _():
            acc_ref[...] = jnp.zeros_like(acc_ref)

        first = (i % npb) == 0
        halo = jnp.where(first, 0.0, halo_ref[...])
        pre, shifted = _conv_taps(cur_ref[...], halo, w_ref[...])
        pre = pre + b_ref[...]
        sg = _sigmoid(pre)
        dpre = d_ref[...] * (sg * (1.0 + pre * (1.0 - sg)))
        o_ref[...] = dpre
        for j in range(4):
            acc_ref[3 - j:4 - j, :] += jnp.sum(dpre * shifted[j], axis=0, keepdims=True)
        acc_ref[4:5, :] += jnp.sum(dpre, axis=0, keepdims=True)

    return pl.pallas_call(
        body, name="conv_bwd", grid=(D_XBC // cw, t // tb),
        out_shape=(jax.ShapeDtypeStruct((t, D_XBC), F32), jax.ShapeDtypeStruct((8, D_XBC), F32)),
        in_specs=[pl.BlockSpec((tb, cw), lambda j, i: (i, j)), pl.BlockSpec((tb, cw), lambda j, i: (i, j)),
                  pl.BlockSpec((8, cw), lambda j, i: (jnp.maximum(i * (tb // 8) - 1, 0), j)),
                  pl.BlockSpec((4, cw), lambda j, i: (0, j)), pl.BlockSpec((1, cw), lambda j, i: (0, j))],
        out_specs=(pl.BlockSpec((tb, cw), lambda j, i: (i, j)), pl.BlockSpec((8, cw), lambda j, i: (0, j))),
        compiler_params=_params(32),
    )(dxbc, xbc_pre, xbc_pre, conv_w, conv_b)


def _proj_bwd(dz, dpre, ddt, du5, x2, dxa, mod3, conv_w, w_in_pad, seq):
    t = x2.shape[0]
    tb = 512
    npb = seq // tb
    n_blocks = t // tb

    def body(dz_ref, dp_ref, nxt_ref, ddt_ref, du5_ref, x_ref, dxa_ref, mod_ref, cw_ref, w_hbm,
             gx_ref, u_ref, dxp_ref, bacc_ref, w_vmem, sem):
        i = pl.program_id(0)
        _load_once(w_hbm, w_vmem, sem)

        @pl.when(i % npb == 0)
        def _():
            bacc_ref[...] = jnp.zeros_like(bacc_ref)

        last = (i % npb) == npb - 1
        nxt = jnp.where(last, 0.0, nxt_ref[...])
        cur = dp_ref[...]
        xx = jnp.concatenate([cur, nxt], axis=0)
        w = cw_ref[...]
        dxp = w[3:4] * cur
        for j in (1, 2, 3):
            dxp = dxp + w[3 - j:4 - j] * pltpu.roll(xx, tb + 8 - j, axis=0)[0:tb]
        dxpb = dxp.astype(BF16)
        dxp_ref[...] = dxpb
        o1, o2, o3 = D_SSD, D_SSD + D_XBC, D_SSD + D_XBC + DT_PAD
        du = (lax.dot_general(dz_ref[...].astype(BF16), w_vmem[:, 0:o1], NT, preferred_element_type=F32)
              + lax.dot_general(dxpb, w_vmem[:, o1:o2], NT, preferred_element_type=F32)
              + lax.dot_general(ddt_ref[...].astype(BF16), w_vmem[:, o2:o3], NT, preferred_element_type=F32)
              + lax.dot_general(du5_ref[...].astype(BF16), w_vmem[:, o3:], NT, preferred_element_type=F32))
        m = mod_ref[0]
        xv = x_ref[...]
        u_ref[...] = (xv * (1.0 + m[1:2]) + m[0:1]).astype(BF16)
        gx_ref[...] = dxa_ref[...] + du * (1.0 + m[1:2])
        bacc_ref[0, 0:1, :] += jnp.sum(du, axis=0, keepdims=True)
        bacc_ref[0, 1:2, :] += jnp.sum(du * xv, axis=0, keepdims=True)

    row = lambda w: pl.BlockSpec((tb, w), lambda i: (i, 0))
    nxt_rows = pl.BlockSpec((8, D_XBC), lambda i: (jnp.minimum((i + 1) * (tb // 8), t // 8 - 1), 0))
    return pl.pallas_call(
        body, name="proj_bwd", grid=(n_blocks,),
        out_shape=(jax.ShapeDtypeStruct((t, D_MODEL), F32), jax.ShapeDtypeStruct((t, D_MODEL), BF16),
                   jax.ShapeDtypeStruct((t, D_XBC), BF16), jax.ShapeDtypeStruct((t // seq, 8, D_MODEL), F32)),
        in_specs=[row(D_SSD), row(D_XBC), nxt_rows, row(DT_PAD), row(D_S5), row(D_MODEL), row(D_MODEL),
                  pl.BlockSpec((1, N_MOD, D_MODEL), lambda i: (i // npb, 0, 0)),
                  pl.BlockSpec((4, D_XBC), lambda i: (0, 0)), ANY],
        out_specs=(row(D_MODEL), row(D_MODEL), row(D_XBC), pl.BlockSpec((1, 8, D_MODEL), lambda i: (i // npb, 0, 0))),
        scratch_shapes=[pltpu.VMEM((D_MODEL, D_INP), BF16), pltpu.SemaphoreType.DMA],
        compiler_params=_params(60),
    )(dz, dpre, dpre, ddt, du5, x2, dxa, mod3, conv_w, w_in_pad)


def _pad_rows(a, mult):
    r = a.shape[0]
    pad = (-r) % mult
    return a if pad == 0 else jnp.concatenate([a, jnp.zeros((pad,) + a.shape[1:], a.dtype)], axis=0)


_SMALL = ["conv_w", "conv_b", "dt_bias", "a_log", "d_ssd", "norm_w", "s5_a_re", "s5_a_im", "s5_log_dt", "s5_b_re",
          "s5_b_im", "s5_c_re", "s5_c_im", "s5_d", "b_glu", "ln1_g", "ln1_b", "b1", "b2", "ln2_g", "ln2_b"]


def _pack_small(d):
    flat = jnp.concatenate([d[n].reshape(-1).astype(F32) for n in _SMALL])
    pad = (-flat.shape[0]) % (128 * 256)
    return jnp.concatenate([flat, jnp.zeros((pad,), F32)]).reshape(-1, 128)


def _unpack_small(p, shapes):
    flat = p.reshape(-1)
    out, off = {}, 0
    for n in _SMALL:
        size = math.prod(shapes[n])
        out[n] = flat[off:off + size].reshape(shapes[n])
        off += size
    return out


def kernel(x, c, w_ada, b_ada, w_in, conv_w, conv_b, dt_bias, a_log, d_ssd, norm_w, s5_a_re, s5_a_im, s5_log_dt, s5_b_re, s5_b_im, s5_c_re, s5_c_im, s5_d, w_glu, b_glu, w_out, ln1_g, ln1_b, w1, b1, w2, b2, ln2_g, ln2_b, loss_target, m_w_ada, m_b_ada, m_w_in, m_conv_w, m_conv_b, m_dt_bias, m_a_log, m_d_ssd, m_norm_w, m_s5_a_re, m_s5_a_im, m_s5_log_dt, m_s5_b_re, m_s5_b_im, m_s5_c_re, m_s5_c_im, m_s5_d, m_w_glu, m_b_glu, m_w_out, m_ln1_g, m_ln1_b, m_w1, m_b1, m_w2, m_b2, m_ln2_g, m_ln2_b, v_w_ada, v_b_ada, v_w_in, v_conv_w, v_conv_b, v_dt_bias, v_a_log, v_d_ssd, v_norm_w, v_s5_a_re, v_s5_a_im, v_s5_log_dt, v_s5_b_re, v_s5_b_im, v_s5_c_re, v_s5_c_im, v_s5_d, v_w_glu, v_b_glu, v_w_out, v_ln1_g, v_ln1_b, v_w1, v_b1, v_w2, v_b2, v_ln2_g, v_ln2_b):
    weights = dict(w_ada=w_ada, b_ada=b_ada, w_in=w_in, conv_w=conv_w, conv_b=conv_b, dt_bias=dt_bias, a_log=a_log,
                   d_ssd=d_ssd, norm_w=norm_w, s5_a_re=s5_a_re, s5_a_im=s5_a_im, s5_log_dt=s5_log_dt, s5_b_re=s5_b_re,
                   s5_b_im=s5_b_im, s5_c_re=s5_c_re, s5_c_im=s5_c_im, s5_d=s5_d, w_glu=w_glu, b_glu=b_glu, w_out=w_out,
                   ln1_g=ln1_g, ln1_b=ln1_b, w1=w1, b1=b1, w2=w2, b2=b2, ln2_g=ln2_g, ln2_b=ln2_b)
    mom = dict(w_ada=m_w_ada, b_ada=m_b_ada, w_in=m_w_in, conv_w=m_conv_w, conv_b=m_conv_b, dt_bias=m_dt_bias,
               a_log=m_a_log, d_ssd=m_d_ssd, norm_w=m_norm_w, s5_a_re=m_s5_a_re, s5_a_im=m_s5_a_im,
               s5_log_dt=m_s5_log_dt, s5_b_re=m_s5_b_re, s5_b_im=m_s5_b_im, s5_c_re=m_s5_c_re, s5_c_im=m_s5_c_im,
               s5_d=m_s5_d, w_glu=m_w_glu, b_glu=m_b_glu, w_out=m_w_out, ln1_g=m_ln1_g, ln1_b=m_ln1_b, w1=m_w1, b1=m_b1,
               w2=m_w2, b2=m_b2, ln2_g=m_ln2_g, ln2_b=m_ln2_b)
    var = dict(w_ada=v_w_ada, b_ada=v_b_ada, w_in=v_w_in, conv_w=v_conv_w, conv_b=v_conv_b, dt_bias=v_dt_bias,
               a_log=v_a_log, d_ssd=v_d_ssd, norm_w=v_norm_w, s5_a_re=v_s5_a_re, s5_a_im=v_s5_a_im,
               s5_log_dt=v_s5_log_dt, s5_b_re=v_s5_b_re, s5_b_im=v_s5_b_im, s5_c_re=v_s5_c_re, s5_c_im=v_s5_c_im,
               s5_d=v_s5_d, w_glu=v_w_glu, b_glu=v_b_glu, w_out=v_w_out, ln1_g=v_ln1_g, ln1_b=v_ln1_b, w1=v_w1, b1=v_b1,
               w2=v_w2, b2=v_b2, ln2_g=v_ln2_g, ln2_b=v_ln2_b)
    names = list(weights)
    shapes = {n: weights[n].shape for n in names}

    nb, seq, _ = x.shape
    t = nb * seq
    dev = _dev_index()
    x2 = x.reshape(t, D_MODEL)
    tgt2 = loss_target.reshape(t, D_MODEL)

    cw_cols = conv_w.shape[2]
    small_in = jnp.concatenate([c.reshape(-1), conv_w.reshape(-1)]).reshape(-1, 128)
    small_all = _all_gather(small_in, "gather_c_conv").reshape(N_DEV, -1)
    c_all = small_all[:, :nb * D_MODEL].reshape(N_DEV * nb, D_MODEL)
    conv_w_full = small_all[:, nb * D_MODEL:].reshape(N_DEV, 4, cw_cols).transpose(1, 0, 2).reshape(4, D_XBC)

    def rows_of(a):
        return a.reshape(-1, PACK_W)

    pack_names = ["w_in", "w_out", "w1", "w2", "w_glu"]
    pack_rows = [rows_of(weights[n][0]).shape[0] for n in pack_names]
    w_pack = _pad_rows(jnp.concatenate([rows_of(weights[n][0]).astype(BF16) for n in pack_names], axis=0), 16)
    w_all = _all_gather(w_pack, "gather_weights")
    offs = [0]
    for r in pack_rows:
        offs.append(offs[-1] + r)
    seg = {n: w_all[:, offs[i]:offs[i + 1], :] for i, n in enumerate(pack_names)}
    in_cols = w_in.shape[2]
    w_in_f = seg["w_in"].reshape(N_DEV, D_MODEL, in_cols).transpose(1, 0, 2).reshape(D_MODEL, D_IN)
    w_in_pad = jnp.concatenate(
        [w_in_f[:, :D_SSD + D_XBC], w_in_f[:, D_SSD + D_XBC:D_SSD + D_XBC + N_HEADS],
         jnp.zeros((D_MODEL, DT_PAD - N_HEADS), BF16), w_in_f[:, D_SSD + D_XBC + N_HEADS:]], axis=1)
    w_out_f = seg["w_out"].reshape(2 * D_MODEL, D_MODEL)
    w1_f = seg["w1"].reshape(N_DEV, D_MODEL, D_FF // N_DEV).transpose(1, 0, 2).reshape(D_MODEL, D_FF)
    w2_f = seg["w2"].reshape(D_FF, D_MODEL)
    w_glu_f = seg["w_glu"].reshape(D_S5, D_S5)

    ada_cols = w_ada.shape[2]
    b_cols = lax.dynamic_slice_in_dim(b_ada, dev * ada_cols, ada_cols, axis=1)
    mod_cols = _mod_fwd(c_all, w_ada[0], b_cols)
    mod_all = _all_gather(mod_cols, "gather_mod")
    mod_mine = lax.dynamic_slice_in_dim(mod_all, dev * nb, nb, axis=1)
    mod3 = mod_mine.transpose(1, 0, 2).reshape(nb, N_MOD, D_MODEL)

    def pad_lanes(v, n):
        return jnp.concatenate([v, jnp.zeros((v.shape[0], n - v.shape[1]), F32)], axis=1)

    par = _pad_rows(jnp.concatenate([pad_lanes(dt_bias, 128), pad_lanes(a_log, 128)], axis=0), 8)
    dsk = jnp.repeat(d_ssd[0], HEADDIM).reshape(1, D_SSD)
    ar = s5_a_re.reshape(1, S5_N)
    ai = s5_a_im.reshape(1, S5_N)
    ldt = jnp.repeat(s5_log_dt[0], S5_P).reshape(1, S5_N)
    br_t = s5_b_re[0].transpose(2, 0, 1).reshape(S5_CH, S5_N)
    bi_t = s5_b_im[0].transpose(2, 0, 1).reshape(S5_CH, S5_N)
    bb_re_t, bb_im_t, pf_re, pf_im, pr_re, pr_im = _s5_params_fwd(ar, ai, ldt, br_t, bi_t)
    mask_b = (jnp.arange(D_S5)[:, None] // S5_CH) == (jnp.arange(S5_N)[None, :] // S5_P)

    def dense_b(bt_):
        return jnp.where(mask_b, jnp.tile(bt_, (S5_GROUPS, 1)), 0.0).astype(BF16)

    def dense_c(cc):
        ct = cc[0].transpose(0, 2, 1).reshape(S5_N, S5_CH)
        return jnp.where(mask_b.T, jnp.tile(ct, (1, S5_GROUPS)), 0.0).astype(BF16)

    bb_re, bb_im = dense_b(bb_re_t), dense_b(bb_im_t)
    cc_re, cc_im = dense_c(s5_c_re), dense_c(s5_c_im)
    s5d = s5_d.reshape(1, D_S5)
    ln1 = jnp.concatenate([ln1_g, ln1_b], axis=0)
    vec1 = _pad_rows(jnp.concatenate([b2, ln2_g, ln2_b], axis=0), 8)

    z, xbc_pre, dt_raw, u5 = _proj_fwd(x2, mod3, w_in_pad, seq)
    xbc = _conv_fwd(xbc_pre, conv_w_full, conv_b, seq)
    yraw, yssd, hprev = _ssd_fwd(xbc, z, dt_raw, par, dsk, norm_w, seq)
    s_re, s_im, ypre, y5 = _s5_fwd(u5, bb_re, bb_im, cc_re, cc_im, pf_re, pf_im, s5d, w_glu_f, b_glu, seq)
    mix, x1 = _out_ln1(yssd, y5, x2, mod3, w_out_f, ln1, seq)

    dx1, u2b, hb, dhpb, dob, gacc2, db1, bacc2 = _mlp_fwd_bwd(x1, tgt2, mod3, w1_f, w2_f, vec1, b1, seq)
    loss = lax.psum(0.5 / D_MODEL * jnp.sum(gacc2[3]), ("x", "y", "c"))

    dmixb, dxa, dyssd, dy5, gacc1, bacc1 = _ln1_out_bwd(dx1, x2, mix, mod3, w_out_f, ln1, seq)
    du5, g_re, g_im, ygb, dqb, dypb, vacc, sacc = _s5_bwd(dy5, ypre, u5, s_re, s_im, bb_re, bb_im, cc_re, cc_im,
                                                          pr_re, pr_im, s5d, w_glu_f, b_glu, seq)
    dxbc, dz, ddt, dpar, cacc = _ssd_bwd(dyssd, yraw, z, xbc, dt_raw, hprev, par, dsk, norm_w, seq)
    dpre, conv_acc = _conv_bwd(dxbc, xbc_pre, conv_w_full, conv_b, seq)
    grad_x2, ub, dxpb, bacc0 = _proj_bwd(dz, dpre, ddt, du5, x2, dxa, mod3, conv_w_full, w_in_pad, seq)

    g_w2 = _atb(hb, dob, "gw2")
    g_w1 = _atb(u2b, dhpb, "gw1")
    g_wout = jnp.concatenate([_atb(yssd, dmixb, "gwout_ssd"), _atb(y5, dmixb, "gwout_s5")], axis=0)
    g_win = jnp.concatenate([_atb(ub, dz, "gwin_z"), _atb(ub, dxpb, "gwin_xbc"),
                             _atb(ub, ddt, "gwin_dt")[:, :N_HEADS], _atb(ub, du5, "gwin_s5")], axis=1)
    g_wglu = _atb(ygb, dqb, "gwglu")
    d_cc_re = _atb(s_re, dypb, "gcc_re")
    d_cc_im = -_atb(s_im, dypb, "gcc_im")
    d_bb_re = _atb(u5, g_re, "gbb_re")
    d_bb_im = _atb(u5, g_im, "gbb_im")

    def diag_b(dd):
        return jnp.where(mask_b, dd, 0.0).reshape(S5_GROUPS, S5_CH, S5_N).sum(0)

    def diag_c(dd):
        return jnp.where(mask_b.T, dd, 0.0).reshape(S5_N, S5_GROUPS, S5_CH).sum(1).reshape(S5_GROUPS, S5_P, S5_CH).transpose(0, 2, 1)

    g_ar, g_ai, g_ldt, g_br_t, g_bi_t = _s5_params_bwd(ar, ai, ldt, br_t, bi_t, vacc[0:1], vacc[1:2],
                                                      diag_b(d_bb_re), diag_b(d_bb_im))

    def from_t(gt):
        return gt.reshape(S5_CH, S5_GROUPS, S5_P).transpose(1, 2, 0)

    small_g = dict(
        conv_w=conv_acc[0:4], conv_b=conv_acc[4:5], dt_bias=dpar[0:1, :N_HEADS], a_log=dpar[1:2, :N_HEADS],
        d_ssd=cacc[0].reshape(N_HEADS, HEADDIM).sum(1), norm_w=cacc[1:2],
        s5_a_re=g_ar, s5_a_im=g_ai, s5_log_dt=g_ldt[:, :S5_GROUPS], s5_b_re=from_t(g_br_t), s5_b_im=from_t(g_bi_t),
        s5_c_re=diag_c(d_cc_re), s5_c_im=diag_c(d_cc_im), s5_d=sacc[0:1], b_glu=sacc[1:2],
        ln1_g=gacc1[0:1], ln1_b=gacc1[1:2], b1=db1, b2=gacc2[2:3], ln2_g=gacc2[0:1], ln2_b=gacc2[1:2])

    dmod = jnp.concatenate([bacc0[:, 0], bacc0[:, 1], bacc1[:, 0], bacc2[:, 0], bacc2[:, 1], bacc2[:, 2]], axis=1)
    dmod_all = _all_gather(dmod, "gather_dmod").reshape(N_DEV * nb, N_MOD * D_MODEL)
    dmod_cols = lax.dynamic_slice_in_dim(dmod_all, dev * ada_cols, ada_cols, axis=1)
    g_wada, g_bada = _mod_bwd(c_all, dmod_cols, dmod_all)

    def dest_blocks(n, g):
        if n in ("w_in", "w1"):
            cols = g.shape[1] // N_DEV
            return g.reshape(g.shape[0], N_DEV, cols).transpose(1, 0, 2).reshape(N_DEV, -1, PACK_W)
        return g.reshape(N_DEV, -1, PACK_W)

    big_g = dict(w_in=g_win, w_out=g_wout, w1=g_w1, w2=g_w2, w_glu=g_wglu)
    g_pack = jnp.concatenate([dest_blocks(n, big_g[n]) for n in pack_names], axis=1)
    pad = (-g_pack.shape[1]) % 16
    if pad:
        g_pack = jnp.concatenate([g_pack, jnp.zeros((N_DEV, pad, PACK_W), F32)], axis=1)
    g_by_core = g_pack.reshape(4, 2, -1, PACK_W).transpose(1, 0, 2, 3)
    from_sibling, own_half = _sibling_swap(g_by_core, "rs_sibling_swap")
    chip_sum = _add_halves(own_half, from_sibling, "rs_add")
    parts = _chip_all_to_all(chip_sum, "rs_chip_all_to_all")

    def pack_mine(d):
        return _pad_rows(jnp.concatenate([rows_of(d[n][0]) for n in pack_names], axis=0), 16)

    pg, pd, pm, pv = _adamw(parts, pack_mine(weights), pack_mine(mom), pack_mine(var), "adamw_big")

    def unpack_mine(p):
        return {n: p[offs[i]:offs[i + 1]].reshape(shapes[n]) for i, n in enumerate(pack_names)}

    res = {k: unpack_mine(p) for k, p in (("g", pg), ("d", pd), ("m", pm), ("v", pv))}

    ag, ad, am, av = _adamw(g_wada[None], w_ada[0], m_w_ada[0], v_w_ada[0], "adamw_w_ada")
    for k, a in (("g", ag), ("d", ad), ("m", am), ("v", av)):
        res[k]["w_ada"] = a[None]
    bg_, bd_, bm_, bv_ = _adamw(g_bada.reshape(1, -1, 128), b_ada.reshape(-1, 128), m_b_ada.reshape(-1, 128),
                                v_b_ada.reshape(-1, 128), "adamw_b_ada")
    for k, a in (("g", bg_), ("d", bd_), ("m", bm_), ("v", bv_)):
        res[k]["b_ada"] = a.reshape(shapes["b_ada"])

    small_shapes = dict(shapes)
    small_shapes["conv_w"] = (1, 4, D_XBC)
    small_parts = _all_gather(_pack_small(small_g), "gather_small_grads")
    rep = {n: (jnp.zeros((1, 4, D_XBC), F32) if n == "conv_w" else weights[n]) for n in _SMALL}
    rep_m = {n: (jnp.zeros((1, 4, D_XBC), F32) if n == "conv_w" else mom[n]) for n in _SMALL}
    rep_v = {n: (jnp.ones((1, 4, D_XBC), F32) if n == "conv_w" else var[n]) for n in _SMALL}
    sg_, sd_, sm_, sv_ = _adamw(small_parts, _pack_small(rep), _pack_small(rep_m), _pack_small(rep_v), "adamw_small")
    for k, p in (("g", sg_), ("d", sd_), ("m", sm_), ("v", sv_)):
        un = _unpack_small(p, small_shapes)
        for n in _SMALL:
            if n != "conv_w":
                res[k][n] = un[n]
    g_conv_full = _unpack_small(sg_, small_shapes)["conv_w"][0]
    g_conv_mine = lax.dynamic_slice_in_dim(g_conv_full, dev * cw_cols, cw_cols, axis=1)
    cg_, cd_, cm_, cv_ = _adamw(g_conv_mine[None], conv_w[0], m_conv_w[0], v_conv_w[0], "adamw_conv_w")
    for k, a in (("g", cg_), ("d", cd_), ("m", cm_), ("v", cv_)):
        res[k]["conv_w"] = a[None]

    grad_x = grad_x2.reshape(nb, seq, D_MODEL)
    return (loss, grad_x, *[res["g"][n] for n in names], *[res["d"][n] for n in names],
            *[res["m"][n] for n in names], *[res["v"][n] for n in names])
```

```python
import functools
import math

import jax
import jax.numpy as jnp
from jax import lax
from jax.experimental import pallas as pl
from jax.experimental.pallas import tpu as pltpu

F32, BF16 = jnp.float32, jnp.bfloat16
MESH = pl.DeviceIdType.MESH
N_DEV = 8

D_MODEL = 1024
D_SSD = 1536
N_HEADS = 24
HEADDIM = 64
N_GROUPS = 4
HPG = 6
GW = HPG * HEADDIM
N_STATE = 128
CHUNK = 128
D_XBC = 2560
D_S5 = 512
S5_GROUPS = 32
S5_CH = 16
S5_P = 64
S5_N = S5_GROUPS * S5_P
D_IN = 4632
DT_PAD = 128
D_INP = D_SSD + D_XBC + DT_PAD + D_S5
D_FF = 4096
N_MOD = 6
ALPHA = 2.0 ** 0.25
EPS = 1e-5
LR, B1, B2, AEPS, WD, STEP = 0.001, 0.9, 0.999, 1e-08, 0.01, 10

NT = (((1,), (1,)), ((), ()))
TN = (((0,), (0,)), ((), ()))
ANY = pl.BlockSpec(memory_space=pl.ANY)
HIGHEST = lax.Precision.HIGHEST


def _mm(a, b):
    return jnp.dot(a.astype(BF16), b.astype(BF16), preferred_element_type=F32)


def _mm_nt(a, b):
    return lax.dot_general(a.astype(BF16), b.astype(BF16), NT, preferred_element_type=F32)


def _mm_tn(a, b):
    return lax.dot_general(a.astype(BF16), b.astype(BF16), TN, preferred_element_type=F32)


def _row_block(r, cap):
    best = r
    for cand in range(8, min(r, cap) + 1, 8):
        if r % cand == 0:
            best = cand
    return best if best <= cap else r


def _params(vmem_mb):
    return pltpu.CompilerParams(vmem_limit_bytes=vmem_mb << 20)


def _sigmoid(x):
    return 1.0 / (1.0 + jnp.exp(-x))


def _softplus(x):
    return jnp.maximum(x, 0.0) + jnp.log(1.0 + jnp.exp(-jnp.abs(x)))


_GK = math.sqrt(2.0 / math.pi)


def _gelu(x):
    return 0.5 * x * (1.0 + jnp.tanh(_GK * (x + 0.044715 * x * x * x)))


def _gelu_grad(x):
    t = jnp.tanh(_GK * (x + 0.044715 * x * x * x))
    return 0.5 * (1.0 + t) + 0.5 * x * (1.0 - t * t) * _GK * (1.0 + 3.0 * 0.044715 * x * x)


def _dev_index():
    return 4 * lax.axis_index("x") + 2 * lax.axis_index("y") + lax.axis_index("c")


def _all_gather(xs, name):
    n = len(xs)

    def body(*refs):
        x_refs, out_refs = refs[:n], refs[n:2 * n]
        send_sems, recv_sems, local_sems = refs[2 * n:]
        ix, iy, ic = lax.axis_index("x"), lax.axis_index("y"), lax.axis_index("c")
        me, sibling = (ix, iy, ic), (ix, iy, 1 - ic)
        chips = [(1 - ix, iy), (ix, 1 - iy), (1 - ix, 1 - iy)]

        def slot(a, px, py, pc):
            return out_refs[a].at[4 * px + 2 * py + pc]

        def copy(a, k, block, to, src=None):
            return pltpu.make_async_remote_copy(
                src_ref=slot(a, *block) if src is None else src, dst_ref=slot(a, *block),
                send_sem=send_sems.at[7 * a + k], recv_sem=recv_sems.at[7 * a + k], device_id=to, device_id_type=MESH)

        mine = [pltpu.make_async_copy(x_refs[a], slot(a, *me), local_sems.at[a]) for a in range(n)]
        for cp in mine:
            cp.start()
        first = []
        for j, chip in enumerate(chips):
            first += [copy(a, 1 + j, me, (*chip, ic), src=x_refs[a]) for a in range(n)]
        first += [copy(a, 0, me, sibling, src=x_refs[a]) for a in range(n)]
        for cp in first:
            cp.start()
        passed = []
        for j, chip in enumerate(chips):
            for a in range(n):
                copy(a, 1 + j, (*chip, ic), me).wait_recv()
                cp = copy(a, 4 + j, (*chip, ic), sibling)
                cp.start()
                passed.append(cp)
        for a in range(n):
            copy(a, 0, sibling, me).wait_recv()
            for j, chip in enumerate(chips):
                copy(a, 4 + j, (*chip, 1 - ic), me).wait_recv()
        for cp in first + passed:
            cp.wait_send()
        for cp in mine:
            cp.wait()

    return pl.pallas_call(
        body, name=name, out_shape=tuple(jax.ShapeDtypeStruct((N_DEV,) + x.shape, x.dtype) for x in xs),
        in_specs=[ANY] * n, out_specs=tuple([ANY] * n),
        scratch_shapes=[pltpu.SemaphoreType.DMA((7 * n,)), pltpu.SemaphoreType.DMA((7 * n,)),
                        pltpu.SemaphoreType.DMA((n,))],
    )(*xs)


def _sibling_swap(gs, name):
    n = len(gs)

    def body(*refs):
        g_refs, recv_refs = refs[:n], refs[n:2 * n]
        send_sems, recv_sems = refs[2 * n:]
        ix, iy, ic = lax.axis_index("x"), lax.axis_index("y"), lax.axis_index("c")
        cps = []
        for a in range(n):
            for q in range(4):
                cps.append(pltpu.make_async_remote_copy(
                    src_ref=g_refs[a].at[q, 1 - ic], dst_ref=recv_refs[a].at[q],
                    send_sem=send_sems.at[4 * a + q], recv_sem=recv_sems.at[4 * a + q],
                    device_id=(ix, iy, 1 - ic), device_id_type=MESH))
        for cp in cps:
            cp.start()
        for cp in cps:
            cp.wait()

    return pl.pallas_call(
        body, name=name,
        out_shape=tuple(jax.ShapeDtypeStruct((4,) + g.shape[2:], g.dtype) for g in gs),
        in_specs=[ANY] * n, out_specs=tuple([ANY] * n),
        scratch_shapes=[pltpu.SemaphoreType.DMA((4 * n,)), pltpu.SemaphoreType.DMA((4 * n,))],
    )(*gs)


def _chip_all_to_all(hs, name):
    n = len(hs)

    def body(*refs):
        h_refs, out_refs = refs[:n], refs[n:2 * n]
        send_sems, recv_sems, local_sems = refs[2 * n:]
        ix, iy, ic = lax.axis_index("x"), lax.axis_index("y"), lax.axis_index("c")
        me = 2 * ix + iy
        peers = [(1 - ix, iy), (ix, 1 - iy), (1 - ix, 1 - iy)]
        mine = [pltpu.make_async_copy(h_refs[a].at[me], out_refs[a].at[me], local_sems.at[a]) for a in range(n)]
        for cp in mine:
            cp.start()

        def copy(a, k, src_slot, dst_slot, peer):
            return pltpu.make_async_remote_copy(
                src_ref=h_refs[a].at[src_slot], dst_ref=out_refs[a].at[dst_slot],
                send_sem=send_sems.at[3 * a + k], recv_sem=recv_sems.at[3 * a + k],
                device_id=(*peer, ic), device_id_type=MESH)

        sends = [copy(a, k, 2 * px + py, me, (px, py)) for a in range(n) for k, (px, py) in enumerate(peers)]
        for cp in sends:
            cp.start()
        for a in range(n):
            for k, (px, py) in enumerate(peers):
                copy(a, k, 2 * px + py, 2 * px + py, (px, py)).wait_recv()
        for cp in sends:
            cp.wait_send()
        for cp in mine:
            cp.wait()

    return pl.pallas_call(
        body, name=name, out_shape=tuple(jax.ShapeDtypeStruct(h.shape, h.dtype) for h in hs),
        in_specs=[ANY] * n, out_specs=tuple([ANY] * n),
        scratch_shapes=[pltpu.SemaphoreType.DMA((3 * n,)), pltpu.SemaphoreType.DMA((3 * n,)),
                        pltpu.SemaphoreType.DMA((n,))],
    )(*hs)


def _add_halves(g, recv, core, name):
    r, c = g.shape[2:]
    br = _row_block(r, 512)

    def body(core_ref, g_ref, r_ref, o_ref):
        o_ref[0] = (g_ref[0, 0] + r_ref[0]).astype(BF16)

    spec = pl.BlockSpec((1, br, c), lambda i, j, core_ref: (i, j, 0))
    return pl.pallas_call(
        body, name=name, out_shape=jax.ShapeDtypeStruct(recv.shape, BF16),
        grid_spec=pltpu.PrefetchScalarGridSpec(
            num_scalar_prefetch=1, grid=(4, r // br),
            in_specs=[pl.BlockSpec((1, 1, br, c), lambda i, j, core_ref: (i, core_ref[0], j, 0)), spec],
            out_specs=spec),
        compiler_params=_params(32),
    )(core, g, recv)


def _adamw(parts, w, m, v, name):
    n_parts, r, c = parts.shape
    br = _row_block(r, 512 if c <= 1024 else 256)

    def body(p_ref, w_ref, m_ref, v_ref, g_out, d_out, m_out, v_out):
        g = p_ref[0].astype(F32)
        for p in range(1, n_parts):
            g = g + p_ref[p].astype(F32)
        m2 = B1 * m_ref[...] + (1.0 - B1) * g
        v2 = B2 * v_ref[...] + (1.0 - B2) * (g * g)
        m_hat = m2 / (1.0 - B1 ** STEP)
        v_hat = v2 / (1.0 - B2 ** STEP)
        g_out[...] = g
        d_out[...] = -LR * (m_hat / (jnp.sqrt(v_hat) + AEPS) + WD * w_ref[...])
        m_out[...] = m2
        v_out[...] = v2

    spec = pl.BlockSpec((br, c), lambda i: (i, 0))
    out = jax.ShapeDtypeStruct((r, c), F32)
    return pl.pallas_call(
        body, name=name, out_shape=(out, out, out, out), grid=(r // br,),
        in_specs=[pl.BlockSpec((n_parts, br, c), lambda i: (0, i, 0)), spec, spec, spec],
        out_specs=(spec, spec, spec, spec), compiler_params=_params(40),
    )(parts, w, m, v)


def _atb(a, b, name, bt=512, col_blocks=None):
    t, k1 = a.shape
    k2 = b.shape[1]

    def pick(k):
        for cand in (1024, 768, 512, 384, 256, 128):
            if k % cand == 0:
                return cand
        return k

    b1 = pick(k1)
    b2 = pick(k2) if col_blocks is None else k2 // col_blocks

    def body(a_ref, b_ref, o_ref):
        @pl.when(pl.program_id(2) == 0)
        def _():
            o_ref[...] = jnp.zeros_like(o_ref)
        acc = _mm_tn(a_ref[...], b_ref[...])
        o_ref[...] += acc if col_blocks is None else acc[None]

    if col_blocks is None:
        out_shape = jax.ShapeDtypeStruct((k1, k2), F32)
        out_spec = pl.BlockSpec((b1, b2), lambda i, j, k: (i, j))
    else:
        out_shape = jax.ShapeDtypeStruct((col_blocks, k1, b2), F32)
        out_spec = pl.BlockSpec((1, b1, b2), lambda i, j, k: (j, i, 0))
    return pl.pallas_call(
        body, name=name, out_shape=out_shape, grid=(k1 // b1, k2 // b2, t // bt),
        in_specs=[pl.BlockSpec((bt, b1), lambda i, j, k: (k, i)), pl.BlockSpec((bt, b2), lambda i, j, k: (k, j))],
        out_specs=out_spec, compiler_params=_params(40),
    )(a, b)


def _mod_fwd(c_all, w_ada, b_cols):
    def body(c_ref, w_ref, b_ref, o_ref):
        cc = c_ref[...]
        cond = cc * _sigmoid(cc)
        o_ref[...] = _mm(cond, w_ref[...]) + b_ref[...]

    return pl.pallas_call(body, name="mod_fwd", out_shape=jax.ShapeDtypeStruct((c_all.shape[0], w_ada.shape[1]), F32),
                          compiler_params=_params(32))(c_all, w_ada, b_cols)


def _mod_bwd(c_all, dmod_cols, dmod_all):
    def body(c_ref, dc_ref, da_ref, gw_ref, gb_ref):
        cc = c_ref[...]
        cond = cc * _sigmoid(cc)
        gw_ref[...] = _mm_tn(cond, dc_ref[...])
        gb_ref[...] = jnp.sum(da_ref[...], axis=0, keepdims=True)

    return pl.pallas_call(
        body, name="mod_bwd",
        out_shape=(jax.ShapeDtypeStruct((D_MODEL, dmod_cols.shape[1]), F32), jax.ShapeDtypeStruct((1, dmod_all.shape[1]), F32)),
        compiler_params=_params(32))(c_all, dmod_cols, dmod_all)


def _load_once(hbm_ref, vmem_ref, sem):
    @pl.when(pl.program_id(0) == 0)
    def _():
        cp = pltpu.make_async_copy(hbm_ref, vmem_ref, sem)
        cp.start()
        cp.wait()


def _proj_fwd(x2, mod3, w_in_pad, seq):
    t = x2.shape[0]
    tb = 512
    npb = seq // tb

    def body(x_ref, mod_ref, w_hbm, z_ref, xbc_ref, dt_ref, u5_ref, w_vmem, sem):
        _load_once(w_hbm, w_vmem, sem)
        m = mod_ref[0]
        u = (x_ref[...] * (1.0 + m[1:2]) + m[0:1]).astype(BF16)
        z_ref[...] = jnp.dot(u, w_vmem[:, 0:D_SSD], preferred_element_type=F32)
        xbc_ref[...] = jnp.dot(u, w_vmem[:, D_SSD:D_SSD + D_XBC], preferred_element_type=F32)
        dt_ref[...] = jnp.dot(u, w_vmem[:, D_SSD + D_XBC:D_SSD + D_XBC + DT_PAD], preferred_element_type=F32)
        u5_ref[...] = jnp.dot(u, w_vmem[:, D_SSD + D_XBC + DT_PAD:], preferred_element_type=F32)

    row = lambda w: pl.BlockSpec((tb, w), lambda i: (i, 0))
    return pl.pallas_call(
        body, name="proj_fwd", grid=(t // tb,),
        out_shape=(jax.ShapeDtypeStruct((t, D_SSD), F32), jax.ShapeDtypeStruct((t, D_XBC), F32),
                   jax.ShapeDtypeStruct((t, DT_PAD), F32), jax.ShapeDtypeStruct((t, D_S5), F32)),
        in_specs=[row(D_MODEL), pl.BlockSpec((1, N_MOD, D_MODEL), lambda i: (i // npb, 0, 0)), ANY],
        out_specs=(row(D_SSD), row(D_XBC), row(DT_PAD), row(D_S5)),
        scratch_shapes=[pltpu.VMEM((D_MODEL, D_INP), BF16), pltpu.SemaphoreType.DMA],
        compiler_params=_params(56),
    )(x2, mod3, w_in_pad)


def _conv_taps(cur, halo, w):
    tb = cur.shape[0]
    xx = jnp.concatenate([halo, cur], axis=0)
    acc = w[3:4] * cur
    shifted = [cur]
    for j in (1, 2, 3):
        sj = pltpu.roll(xx, j, axis=0)[8:8 + tb]
        shifted.append(sj)
        acc = acc + w[3 - j:4 - j] * sj
    return acc, shifted


def _conv_fwd(xbc_pre, conv_w, conv_b, seq):
    t = xbc_pre.shape[0]
    tb = 256
    npb = seq // tb
    cw = 512

    def body(cur_ref, halo_ref, w_ref, b_ref, o_ref):
        first = (pl.program_id(0) % npb) == 0
        halo = jnp.where(first, 0.0, halo_ref[...])
        pre, _ = _conv_taps(cur_ref[...], halo, w_ref[...])
        pre = pre + b_ref[...]
        o_ref[...] = pre * _sigmoid(pre)

    return pl.pallas_call(
        body, name="conv_fwd", grid=(t // tb, D_XBC // cw), out_shape=jax.ShapeDtypeStruct((t, D_XBC), F32),
        in_specs=[pl.BlockSpec((tb, cw), lambda i, j: (i, j)),
                  pl.BlockSpec((8, cw), lambda i, j: (jnp.maximum(i * (tb // 8) - 1, 0), j)),
                  pl.BlockSpec((4, cw), lambda i, j: (0, j)), pl.BlockSpec((1, cw), lambda i, j: (0, j))],
        out_specs=pl.BlockSpec((tb, cw), lambda i, j: (i, j)), compiler_params=_params(32),
    )(xbc_pre, xbc_pre, conv_w, conv_b)


def _ssd_chunk_common(dt_raw, par):
    dtb = par[0:1]
    a = -jnp.exp(par[1:2])
    dt = _softplus(dt_raw + dtb)
    adt = dt * a
    row = lax.broadcasted_iota(jnp.int32, (CHUNK, CHUNK), 0)
    col = lax.broadcasted_iota(jnp.int32, (CHUNK, CHUNK), 1)
    causal = row >= col
    tri = causal.astype(F32)
    cs = jnp.dot(tri, adt, preferred_element_type=F32, precision=HIGHEST)
    return dt, a, cs, cs.T, causal, tri


def _ssd_fwd(xbc, z, dt_raw, par, dsk, normw, seq):
    t = xbc.shape[0]
    nc = seq // CHUNK
    n_chunks = t // CHUNK

    def body(xbc_ref, z_ref, dt_ref, par_ref, dsk_ref, nw_ref, yraw_ref, yssd_ref, hprev_ref, h_ref):
        @pl.when(pl.program_id(0) % nc == 0)
        def _():
            h_ref[...] = jnp.zeros_like(h_ref)
        hprev_ref[0] = h_ref[...]
        dt, _, cs, cst, causal, _ = _ssd_chunk_common(dt_ref[...], par_ref[...])
        cs_last = cs[CHUNK - 1:CHUNK, :]
        zz = z_ref[...]
        silu_z = zz * _sigmoid(zz)
        for g in range(N_GROUPS):
            bg = xbc_ref[:, D_SSD + g * N_STATE:D_SSD + (g + 1) * N_STATE].astype(BF16)
            cg = xbc_ref[:, D_SSD + N_GROUPS * N_STATE + g * N_STATE:D_SSD + N_GROUPS * N_STATE + (g + 1) * N_STATE].astype(BF16)
            scores = lax.dot_general(cg, bg, NT, preferred_element_type=F32)
            hg = h_ref[g * GW:(g + 1) * GW, :]
            p_all = lax.dot_general(cg, hg.astype(BF16), NT, preferred_element_type=F32)
            ys, a_cols = [], []
            for j in range(HPG):
                hh = g * HPG + j
                cs_col = cs[:, hh:hh + 1]
                cs_row = cst[hh:hh + 1, :]
                decay = jnp.exp(jnp.where(causal, cs_col - cs_row, -jnp.inf))
                xh = xbc_ref[:, hh * HEADDIM:(hh + 1) * HEADDIM]
                xdt = xh * dt[:, hh:hh + 1]
                y = _mm(scores * decay, xdt)
                y = y + jnp.exp(cs_col) * p_all[:, j * HEADDIM:(j + 1) * HEADDIM]
                y = y + xh * dsk_ref[:, hh * HEADDIM:(hh + 1) * HEADDIM]
                ys.append(y)
                a_cols.append(jnp.exp(cs_last[:, hh:hh + 1] - cs_col) * xdt)
            s_new = _mm_tn(jnp.concatenate(a_cols, axis=1), bg)
            for j in range(HPG):
                hh = g * HPG + j
                rows = slice(g * GW + j * HEADDIM, g * GW + (j + 1) * HEADDIM)
                h_ref[rows, :] = (hg[j * HEADDIM:(j + 1) * HEADDIM, :] * jnp.exp(cs_last[:, hh:hh + 1])
                                  + s_new[j * HEADDIM:(j + 1) * HEADDIM, :])
            yg = jnp.concatenate(ys, axis=1)
            yraw_ref[:, g * GW:(g + 1) * GW] = yg
            v = yg * silu_z[:, g * GW:(g + 1) * GW]
            r = lax.rsqrt(jnp.mean(v * v, axis=-1, keepdims=True) + EPS)
            yssd_ref[:, g * GW:(g + 1) * GW] = (v * r * nw_ref[:, g * GW:(g + 1) * GW]).astype(BF16)

    row = lambda w: pl.BlockSpec((CHUNK, w), lambda i: (i, 0))
    full = lambda s: pl.BlockSpec(s, lambda i: (0, 0))
    return pl.pallas_call(
        body, name="ssd_fwd", grid=(n_chunks,),
        out_shape=(jax.ShapeDtypeStruct((t, D_SSD), F32), jax.ShapeDtypeStruct((t, D_SSD + D_S5), BF16),
                   jax.ShapeDtypeStruct((n_chunks, D_SSD, N_STATE), F32)),
        in_specs=[row(D_XBC), row(D_SSD), row(DT_PAD), full((8, 128)), full((1, D_SSD)), full((1, D_SSD))],
        out_specs=(row(D_SSD), row(D_SSD), pl.BlockSpec((1, D_SSD, N_STATE), lambda i: (i, 0, 0))),
        scratch_shapes=[pltpu.VMEM((D_SSD, N_STATE), F32)],
        compiler_params=_params(40),
    )(xbc, z, dt_raw, par, dsk, normw)


S5_CW = 512


def _tile_scan(in_re, in_im, out_re, out_im, carry_re, carry_im, pw_re, pw_im, n_tiles, reverse):
    steps = (1, 2, 4)
    for cc in range(S5_N // S5_CW):
        cols = slice(cc * S5_CW, (cc + 1) * S5_CW)
        if reverse:
            pows = [(pw_re[8 - d:9 - d, cols], pw_im[8 - d:9 - d, cols]) for d in steps]
        else:
            pows = [(pw_re[d - 1:d, cols], pw_im[d - 1:d, cols]) for d in steps]
        a_re, a_im = pw_re[:, cols], pw_im[:, cols]
        rid = lax.broadcasted_iota(jnp.int32, (8, S5_CW), 0)

        def tile(i, carry, cols=cols, pows=pows, a_re=a_re, a_im=a_im, rid=rid):
            r = (n_tiles - 1 - i) if reverse else i
            rows = pl.ds(pl.multiple_of(r * 8, 8), 8)
            xr, xi = in_re[rows, cols], in_im[rows, cols]
            for (pr, pi), d in zip(pows, steps):
                if reverse:
                    sr, si = pltpu.roll(xr, 8 - d, axis=0), pltpu.roll(xi, 8 - d, axis=0)
                    keep = rid < 8 - d
                else:
                    sr, si = pltpu.roll(xr, d, axis=0), pltpu.roll(xi, d, axis=0)
                    keep = rid >= d
                sr, si = jnp.where(keep, sr, 0.0), jnp.where(keep, si, 0.0)
                xr, xi = xr + pr * sr - pi * si, xi + pr * si + pi * sr
            cr, ci = carry
            xr, xi = xr + a_re * cr - a_im * ci, xi + a_re * ci + a_im * cr
            out_re[rows, cols] = xr
            out_im[rows, cols] = xi
            edge = slice(0, 1) if reverse else slice(7, 8)
            return (jnp.broadcast_to(xr[edge], (8, S5_CW)), jnp.broadcast_to(xi[edge], (8, S5_CW)))

        c0 = (jnp.broadcast_to(carry_re[0:1, cols], (8, S5_CW)), jnp.broadcast_to(carry_im[0:1, cols], (8, S5_CW)))
        cr, ci = lax.fori_loop(0, n_tiles, tile, c0)
        carry_re[:, cols] = cr
        carry_im[:, cols] = ci


def _s5_params_math(ar, ai, ldt, br, bi):
    dt = jnp.exp(ldt)
    mag = jnp.exp(ar * dt)
    ang = ai * dt
    ab_re = mag * jnp.cos(ang)
    ab_im = mag * jnp.sin(ang)
    den = ar * ar + ai * ai
    n_re = ab_re - 1.0
    coef_re = (n_re * ar + ab_im * ai) / den
    coef_im = (ab_im * ar - n_re * ai) / den
    bb_re = coef_re * br - coef_im * bi
    bb_im = coef_re * bi + coef_im * br
    return ab_re, ab_im, bb_re, bb_im


def _s5_params_fwd(ar, ai, ldt, br, bi):
    def body(ar_ref, ai_ref, ldt_ref, br_ref, bi_ref, bbr_ref, bbi_ref, pfr_ref, pfi_ref, prr_ref, pri_ref):
        ab_re, ab_im, bb_re, bb_im = _s5_params_math(ar_ref[...], ai_ref[...], ldt_ref[...], br_ref[...], bi_ref[...])
        bbr_ref[...] = bb_re
        bbi_ref[...] = bb_im
        pr, pi = ab_re, ab_im
        for k in range(8):
            pfr_ref[k:k + 1, :] = pr
            pfi_ref[k:k + 1, :] = pi
            prr_ref[7 - k:8 - k, :] = pr
            pri_ref[7 - k:8 - k, :] = -pi
            pr, pi = pr * ab_re - pi * ab_im, pr * ab_im + pi * ab_re

    b16 = jax.ShapeDtypeStruct((S5_CH, S5_N), F32)
    p8 = jax.ShapeDtypeStruct((8, S5_N), F32)
    return pl.pallas_call(body, name="s5_params_fwd", out_shape=(b16, b16, p8, p8, p8, p8),
                          compiler_params=_params(32))(ar, ai, ldt, br, bi)


def _s5_params_bwd(ar, ai, ldt, br, bi, d_ab_re, d_ab_im, d_bb_re, d_bb_im):
    def body(ar_ref, ai_ref, ldt_ref, br_ref, bi_ref, dar_ref, dai_ref, dbr_ref, dbi_ref,
             gar_ref, gai_ref, gldt_ref, gbr_ref, gbi_ref):
        _, vjp = jax.vjp(_s5_params_math, ar_ref[...], ai_ref[...], ldt_ref[...], br_ref[...], bi_ref[...])
        g_ar, g_ai, g_ldt, g_br, g_bi = vjp((dar_ref[...], dai_ref[...], dbr_ref[...], dbi_ref[...]))
        gar_ref[...] = g_ar
        gai_ref[...] = g_ai
        gbr_ref[...] = g_br
        gbi_ref[...] = g_bi
        lane = lax.broadcasted_iota(jnp.int32, (S5_N, 128), 0) // S5_P
        grp = lax.broadcasted_iota(jnp.int32, (S5_N, 128), 1)
        fold = (lane == grp).astype(F32)
        gldt_ref[...] = jnp.dot(g_ldt, fold, preferred_element_type=F32, precision=HIGHEST)

    v1 = jax.ShapeDtypeStruct((1, S5_N), F32)
    b16 = jax.ShapeDtypeStruct((S5_CH, S5_N), F32)
    return pl.pallas_call(body, name="s5_params_bwd",
                          out_shape=(v1, v1, jax.ShapeDtypeStruct((1, 128), F32), b16, b16),
                          compiler_params=_params(32))(ar, ai, ldt, br, bi, d_ab_re, d_ab_im, d_bb_re, d_bb_im)


def _s5_fwd(u5, bb_re, bb_im, cc_re, cc_im, pf_re, pf_im, s5d, w_glu, b_glu, ycat, seq):
    t = u5.shape[0]
    tb = 256
    npb = seq // tb

    def body(u_ref, bbr_ref, bbi_ref, ccr_ref, cci_ref, pfr_ref, pfi_ref, d_ref, wg_ref, bg_ref, ycat_hbm,
             sre_ref, sim_ref, ypre_ref, y5_ref, bur, bui, car, cai):
        del ycat_hbm

        @pl.when(pl.program_id(0) % npb == 0)
        def _():
            car[...] = jnp.zeros_like(car)
            cai[...] = jnp.zeros_like(cai)
        u = u_ref[...]
        ub = u.astype(BF16)
        bur[...] = jnp.dot(ub, bbr_ref[...], preferred_element_type=F32)
        bui[...] = jnp.dot(ub, bbi_ref[...], preferred_element_type=F32)
        _tile_scan(bur, bui, sre_ref, sim_ref, car, cai, pfr_ref, pfi_ref, tb // 8, reverse=False)
        ypre = _mm(sre_ref[...], ccr_ref[...]) - _mm(sim_ref[...], cci_ref[...]) + u * d_ref[...]
        ypre_ref[...] = ypre
        yg = _gelu(ypre)
        y5_ref[...] = (yg * _sigmoid(_mm(yg, wg_ref[...]) + bg_ref[...])).astype(BF16)

    row = lambda w: pl.BlockSpec((tb, w), lambda i: (i, 0))
    full = lambda a: pl.BlockSpec(a.shape, lambda i: (0, 0))
    return pl.pallas_call(
        body, name="s5_fwd", grid=(t // tb,),
        out_shape=(jax.ShapeDtypeStruct((t, S5_N), F32), jax.ShapeDtypeStruct((t, S5_N), F32),
                   jax.ShapeDtypeStruct((t, D_S5), F32), jax.ShapeDtypeStruct(ycat.shape, BF16)),
        in_specs=[row(D_S5), full(bb_re), full(bb_im), full(cc_re), full(cc_im), full(pf_re), full(pf_im),
                  full(s5d), full(w_glu), full(b_glu), ANY],
        out_specs=(row(S5_N), row(S5_N), row(D_S5), pl.BlockSpec((tb, D_S5), lambda i: (i, D_SSD // D_S5))),
        input_output_aliases={10: 3},
        scratch_shapes=[pltpu.VMEM((tb, S5_N), F32), pltpu.VMEM((tb, S5_N), F32),
                        pltpu.VMEM((8, S5_N), F32), pltpu.VMEM((8, S5_N), F32)],
        compiler_params=_params(48),
    )(u5, bb_re, bb_im, cc_re, cc_im, pf_re, pf_im, s5d, w_glu, b_glu, ycat)


def _layer_norm(r, g, b):
    mu = jnp.mean(r, axis=-1, keepdims=True)
    xc = r - mu
    rstd = lax.rsqrt(jnp.mean(xc * xc, axis=-1, keepdims=True) + EPS)
    xhat = xc * rstd
    return xhat * g + b, xhat, rstd


def _layer_norm_bwd(dy, xhat, rstd, g):
    dxhat = dy * g
    return rstd * (dxhat - jnp.mean(dxhat, axis=-1, keepdims=True)
                   - xhat * jnp.mean(dxhat * xhat, axis=-1, keepdims=True))


def _out_ln1(ycat, x2, mod3, w_out, ln1, seq):
    t = x2.shape[0]
    tb = 512
    npb = seq // tb

    def body(y_ref, x_ref, mod_ref, w_ref, ln_ref, mix_ref, x1_ref):
        m = mod_ref[0]
        mix = jnp.dot(y_ref[...], w_ref[...], preferred_element_type=F32)
        mix_ref[...] = mix
        r1 = ALPHA * x_ref[...] + (1.0 + m[2:3]) * mix
        x1_ref[...] = _layer_norm(r1, ln_ref[0:1], ln_ref[1:2])[0]

    row = lambda w: pl.BlockSpec((tb, w), lambda i: (i, 0))
    return pl.pallas_call(
        body, name="out_ln1", grid=(t // tb,),
        out_shape=(jax.ShapeDtypeStruct((t, D_MODEL), F32), jax.ShapeDtypeStruct((t, D_MODEL), F32)),
        in_specs=[row(D_SSD + D_S5), row(D_MODEL), pl.BlockSpec((1, N_MOD, D_MODEL), lambda i: (i // npb, 0, 0)),
                  pl.BlockSpec(w_out.shape, lambda i: (0, 0)), pl.BlockSpec(ln1.shape, lambda i: (0, 0))],
        out_specs=(row(D_MODEL), row(D_MODEL)), compiler_params=_params(48),
    )(ycat, x2, mod3, w_out, ln1)


def _mlp_fwd_bwd(x1, tgt, mod3, w1, w2, vec1, b1, seq):
    t = x1.shape[0]
    tb = 256
    npb = seq // tb
    n_fb, _, fb = w1.shape

    def body(x1_ref, tgt_ref, mod_ref, w1_hbm, w2_hbm, v_ref, b1_ref,
             dx1_ref, u2_ref, h_ref, dhp_ref, do_ref, gacc_ref, db1_ref, bacc_ref, w1_v, w2_v, sem1, sem2):
        i = pl.program_id(0)
        _load_once(w1_hbm, w1_v, sem1)
        _load_once(w2_hbm, w2_v, sem2)

        @pl.when(i == 0)
        def _():
            gacc_ref[...] = jnp.zeros_like(gacc_ref)
            db1_ref[...] = jnp.zeros_like(db1_ref)

        @pl.when(i % npb == 0)
        def _():
            bacc_ref[...] = jnp.zeros_like(bacc_ref)

        m = mod_ref[0]
        sh2, sc2, g2 = m[3:4], m[4:5], m[5:6]
        x1v = x1_ref[...]
        u2 = (x1v * (1.0 + sc2) + sh2).astype(BF16)
        u2_ref[...] = u2
        o = jnp.zeros((tb, D_MODEL), F32)
        hrs = []
        for k in range(n_fb):
            cols = slice(k * fb, (k + 1) * fb)
            hr = jnp.maximum(jnp.dot(u2, w1_v[k], preferred_element_type=F32) + b1_ref[:, cols], 0.0)
            hrs.append(hr)
            hb = (hr * hr).astype(BF16)
            h_ref[:, cols] = hb
            o = o + jnp.dot(hb, w2_v[cols, :], preferred_element_type=F32)
        o = o + v_ref[0:1]
        r2 = ALPHA * x1v + (1.0 + g2) * o
        y, xhat, rstd = _layer_norm(r2, v_ref[1:2], v_ref[2:3])
        err = y - tgt_ref[...]
        dy = err * (1.0 / D_MODEL)
        dr2 = _layer_norm_bwd(dy, xhat, rstd, v_ref[1:2])
        do = (1.0 + g2) * dr2
        dob = do.astype(BF16)
        do_ref[...] = dob
        gacc_ref[0:1, :] += jnp.sum(dy * xhat, axis=0, keepdims=True)
        gacc_ref[1:2, :] += jnp.sum(dy, axis=0, keepdims=True)
        gacc_ref[2:3, :] += jnp.sum(do, axis=0, keepdims=True)
        gacc_ref[3:4, :] += jnp.sum(err * err, axis=0, keepdims=True)
        du2 = jnp.zeros((tb, D_MODEL), F32)
        for k in range(n_fb):
            cols = slice(k * fb, (k + 1) * fb)
            dhpre = lax.dot_general(dob, w2_v[cols, :], NT, preferred_element_type=F32) * (2.0 * hrs[k])
            dhpb = dhpre.astype(BF16)
            dhp_ref[:, cols] = dhpb
            db1_ref[:, cols] += jnp.sum(dhpre, axis=0, keepdims=True)
            du2 = du2 + lax.dot_general(dhpb, w1_v[k], NT, preferred_element_type=F32)
        dx1_ref[...] = ALPHA * dr2 + du2 * (1.0 + sc2)
        bacc_ref[0, 0:1, :] += jnp.sum(du2, axis=0, keepdims=True)
        bacc_ref[0, 1:2, :] += jnp.sum(du2 * x1v, axis=0, keepdims=True)
        bacc_ref[0, 2:3, :] += jnp.sum(dr2 * o, axis=0, keepdims=True)

    row = lambda w: pl.BlockSpec((tb, w), lambda i: (i, 0))
    return pl.pallas_call(
        body, name="mlp_fwd_bwd", grid=(t // tb,),
        out_shape=(jax.ShapeDtypeStruct((t, D_MODEL), F32), jax.ShapeDtypeStruct((t, D_MODEL), BF16),
                   jax.ShapeDtypeStruct((t, D_FF), BF16), jax.ShapeDtypeStruct((t, D_FF), BF16),
                   jax.ShapeDtypeStruct((t, D_MODEL), BF16), jax.ShapeDtypeStruct((8, D_MODEL), F32),
                   jax.ShapeDtypeStruct((1, D_FF), F32), jax.ShapeDtypeStruct((t // seq, 8, D_MODEL), F32)),
        in_specs=[row(D_MODEL), row(D_MODEL), pl.BlockSpec((1, N_MOD, D_MODEL), lambda i: (i // npb, 0, 0)), ANY, ANY,
                  pl.BlockSpec(vec1.shape, lambda i: (0, 0)), pl.BlockSpec(b1.shape, lambda i: (0, 0))],
        out_specs=(row(D_MODEL), row(D_MODEL), row(D_FF), row(D_FF), row(D_MODEL),
                   pl.BlockSpec((8, D_MODEL), lambda i: (0, 0)), pl.BlockSpec((1, D_FF), lambda i: (0, 0)),
                   pl.BlockSpec((1, 8, D_MODEL), lambda i: (i // npb, 0, 0))),
        scratch_shapes=[pltpu.VMEM(w1.shape, BF16), pltpu.VMEM((D_FF, D_MODEL), BF16),
                        pltpu.SemaphoreType.DMA, pltpu.SemaphoreType.DMA],
        compiler_params=_params(60),
    )(x1, tgt, mod3, w1, w2, vec1, b1)


def _ln1_out_bwd(dx1, x2, mix, mod3, w_out, ln1, seq):
    t = x2.shape[0]
    tb = 512
    npb = seq // tb

    def body(dx1_ref, x_ref, mix_ref, mod_ref, w_ref, ln_ref, dmix_ref, dxa_ref, dys_ref, dy5_ref, gacc_ref, bacc_ref):
        i = pl.program_id(0)

        @pl.when(i == 0)
        def _():
            gacc_ref[...] = jnp.zeros_like(gacc_ref)

        @pl.when(i % npb == 0)
        def _():
            bacc_ref[...] = jnp.zeros_like(bacc_ref)

        m = mod_ref[0]
        mix = mix_ref[...]
        r1 = ALPHA * x_ref[...] + (1.0 + m[2:3]) * mix
        _, xhat, rstd = _layer_norm(r1, ln_ref[0:1], ln_ref[1:2])
        dx1v = dx1_ref[...]
        dr1 = _layer_norm_bwd(dx1v, xhat, rstd, ln_ref[0:1])
        gacc_ref[0:1, :] += jnp.sum(dx1v * xhat, axis=0, keepdims=True)
        gacc_ref[1:2, :] += jnp.sum(dx1v, axis=0, keepdims=True)
        bacc_ref[0, 0:1, :] += jnp.sum(dr1 * mix, axis=0, keepdims=True)
        dmix = ((1.0 + m[2:3]) * dr1).astype(BF16)
        dmix_ref[...] = dmix
        dxa_ref[...] = ALPHA * dr1
        dys_ref[...] = lax.dot_general(dmix, w_ref[0:D_SSD, :], NT, preferred_element_type=F32)
        dy5_ref[...] = lax.dot_general(dmix, w_ref[D_SSD:, :], NT, preferred_element_type=F32)

    row = lambda w: pl.BlockSpec((tb, w), lambda i: (i, 0))
    return pl.pallas_call(
        body, name="ln1_out_bwd", grid=(t // tb,),
        out_shape=(jax.ShapeDtypeStruct((t, D_MODEL), BF16), jax.ShapeDtypeStruct((t, D_MODEL), F32),
                   jax.ShapeDtypeStruct((t, D_SSD), F32), jax.ShapeDtypeStruct((t, D_S5), F32),
                   jax.ShapeDtypeStruct((8, D_MODEL), F32), jax.ShapeDtypeStruct((t // seq, 8, D_MODEL), F32)),
        in_specs=[row(D_MODEL), row(D_MODEL), row(D_MODEL), pl.BlockSpec((1, N_MOD, D_MODEL), lambda i: (i // npb, 0, 0)),
                  pl.BlockSpec(w_out.shape, lambda i: (0, 0)), pl.BlockSpec(ln1.shape, lambda i: (0, 0))],
        out_specs=(row(D_MODEL), row(D_MODEL), row(D_SSD), row(D_S5), pl.BlockSpec((8, D_MODEL), lambda i: (0, 0)),
                   pl.BlockSpec((1, 8, D_MODEL), lambda i: (i // npb, 0, 0))),
        compiler_params=_params(48),
    )(dx1, x2, mix, mod3, w_out, ln1)


def _s5_bwd(dy5, ypre, u5, s_re, s_im, bb_re, bb_im, cc_re, cc_im, pr_re, pr_im, s5d, w_glu, b_glu, seq):
    t = u5.shape[0]
    tb = 256
    npb = seq // tb
    n_blocks = t // tb

    def blk(i):
        return (i // npb) * npb + (npb - 1 - i % npb)

    def body(dy_ref, ypre_ref, u_ref, sre_ref, sim_ref, hre_ref, him_ref, bbr_ref, bbi_ref, ccr_ref, cci_ref,
             prr_ref, pri_ref, d_ref, wg_ref, bg_ref,
             du_ref, gre_ref, gim_ref, yg_ref, dq_ref, dyp_ref, vacc_ref, sacc_ref, dsr, dsi, gr, gi, car, cai):
        i = pl.program_id(0)

        @pl.when(i == 0)
        def _():
            vacc_ref[...] = jnp.zeros_like(vacc_ref)
            sacc_ref[...] = jnp.zeros_like(sacc_ref)

        @pl.when(i % npb == 0)
        def _():
            car[...] = jnp.zeros_like(car)
            cai[...] = jnp.zeros_like(cai)

        dy = dy_ref[...]
        ypre = ypre_ref[...]
        u = u_ref[...]
        yg = _gelu(ypre)
        sg = _sigmoid(_mm(yg, wg_ref[...]) + bg_ref[...])
        dq = dy * yg * sg * (1.0 - sg)
        dyg = dy * sg + _mm_nt(dq, wg_ref[...])
        dyp = dyg * _gelu_grad(ypre)
        yg_ref[...] = yg.astype(BF16)
        dq_ref[...] = dq.astype(BF16)
        dypb = dyp.astype(BF16)
        dyp_ref[...] = dypb
        dsr[...] = lax.dot_general(dypb, ccr_ref[...], NT, preferred_element_type=F32)
        dsi[...] = -lax.dot_general(dypb, cci_ref[...], NT, preferred_element_type=F32)
        _tile_scan(dsr, dsi, gr, gi, car, cai, prr_ref, pri_ref, tb // 8, reverse=True)
        g_re, g_im = gr[...], gi[...]
        first_rows = (i % npb) == npb - 1
        hre = jnp.where(first_rows, 0.0, hre_ref[...])
        him = jnp.where(first_rows, 0.0, him_ref[...])
        sp_re = pltpu.roll(jnp.concatenate([hre, sre_ref[...]], axis=0), 1, axis=0)[8:8 + tb]
        sp_im = pltpu.roll(jnp.concatenate([him, sim_ref[...]], axis=0), 1, axis=0)[8:8 + tb]
        vacc_ref[0:1, :] += jnp.sum(g_re * sp_re + g_im * sp_im, axis=0, keepdims=True)
        vacc_ref[1:2, :] += jnp.sum(g_im * sp_re - g_re * sp_im, axis=0, keepdims=True)
        grb, gib = g_re.astype(BF16), g_im.astype(BF16)
        gre_ref[...] = grb
        gim_ref[...] = gib
        du_ref[...] = (lax.dot_general(grb, bbr_ref[...], NT, preferred_element_type=F32)
                       + lax.dot_general(gib, bbi_ref[...], NT, preferred_element_type=F32) + dyp * d_ref[...])
        sacc_ref[0:1, :] += jnp.sum(dyp * u, axis=0, keepdims=True)
        sacc_ref[1:2, :] += jnp.sum(dq, axis=0, keepdims=True)

    row = lambda w: pl.BlockSpec((tb, w), lambda i: (blk(i), 0))
    halo = pl.BlockSpec((8, S5_N), lambda i: (jnp.maximum(blk(i) * (tb // 8) - 1, 0), 0))
    full = lambda a: pl.BlockSpec(a.shape, lambda i: (0, 0))
    return pl.pallas_call(
        body, name="s5_bwd", grid=(n_blocks,),
        out_shape=(jax.ShapeDtypeStruct((t, D_S5), F32), jax.ShapeDtypeStruct((t, S5_N), BF16),
                   jax.ShapeDtypeStruct((t, S5_N), BF16), jax.ShapeDtypeStruct((t, D_S5), BF16),
                   jax.ShapeDtypeStruct((t, D_S5), BF16), jax.ShapeDtypeStruct((t, D_S5), BF16),
                   jax.ShapeDtypeStruct((8, S5_N), F32), jax.ShapeDtypeStruct((8, D_S5), F32)),
        in_specs=[row(D_S5), row(D_S5), row(D_S5), row(S5_N), row(S5_N), halo, halo, full(bb_re), full(bb_im),
                  full(cc_re), full(cc_im), full(pr_re), full(pr_im), full(s5d), full(w_glu), full(b_glu)],
        out_specs=(row(D_S5), row(S5_N), row(S5_N), row(D_S5), row(D_S5), row(D_S5),
                   pl.BlockSpec((8, S5_N), lambda i: (0, 0)), pl.BlockSpec((8, D_S5), lambda i: (0, 0))),
        scratch_shapes=[pltpu.VMEM((tb, S5_N), F32), pltpu.VMEM((tb, S5_N), F32), pltpu.VMEM((tb, S5_N), F32),
                        pltpu.VMEM((tb, S5_N), F32), pltpu.VMEM((8, S5_N), F32), pltpu.VMEM((8, S5_N), F32)],
        compiler_params=_params(56),
    )(dy5, ypre, u5, s_re, s_im, s_re, s_im, bb_re, bb_im, cc_re, cc_im, pr_re, pr_im, s5d, w_glu, b_glu)


def _ssd_bwd(dyssd, yraw, z, xbc, dt_raw, hprev, par, dsk, normw, seq):
    t = xbc.shape[0]
    nc = seq // CHUNK
    n_chunks = t // CHUNK

    def blk(i):
        return (i // nc) * nc + (nc - 1 - i % nc)

    def body(dy_ref, yraw_ref, z_ref, xbc_ref, dt_ref, hprev_ref, par_ref, dsk_ref, nw_ref,
             dxbc_ref, dz_ref, ddt_ref, dpar_ref, cacc_ref, dh_ref, dyr_ref):
        i = pl.program_id(0)

        @pl.when(i == 0)
        def _():
            dpar_ref[...] = jnp.zeros_like(dpar_ref)
            cacc_ref[...] = jnp.zeros_like(cacc_ref)

        @pl.when(i % nc == 0)
        def _():
            dh_ref[...] = jnp.zeros_like(dh_ref)

        zz = z_ref[...]
        sz = _sigmoid(zz)
        silu_z = zz * sz
        yraw = yraw_ref[...]
        for g in range(N_GROUPS):
            sl = slice(g * GW, (g + 1) * GW)
            v = yraw[:, sl] * silu_z[:, sl]
            r = lax.rsqrt(jnp.mean(v * v, axis=-1, keepdims=True) + EPS)
            dyg = dy_ref[:, sl]
            cacc_ref[1:2, sl] += jnp.sum(dyg * v * r, axis=0, keepdims=True)
            dyw = dyg * nw_ref[:, sl]
            dv = r * dyw - v * (r * r * r) * jnp.mean(dyw * v, axis=-1, keepdims=True)
            dyr_ref[:, sl] = dv * silu_z[:, sl]
            dz_ref[:, sl] = dv * yraw[:, sl] * (sz[:, sl] * (1.0 + zz[:, sl] * (1.0 - sz[:, sl])))

        dt, a, cs, cst, causal, tri = _ssd_chunk_common(dt_ref[...], par_ref[...])
        cs_last = cs[CHUNK - 1:CHUNK, :]
        lane = lax.broadcasted_iota(jnp.int32, (CHUNK, 128), 1)
        rowid = lax.broadcasted_iota(jnp.int32, (CHUNK, 128), 0)
        ddt = jnp.zeros((CHUNK, 128), F32)
        dcs = jnp.zeros((CHUNK, 128), F32)
        dcs_t = jnp.zeros((CHUNK, 128), F32)
        dcs_last = jnp.zeros((1, 128), F32)
        lane1 = lax.broadcasted_iota(jnp.int32, (1, 128), 1)
        for g in range(N_GROUPS):
            b_sl = slice(D_SSD + g * N_STATE, D_SSD + (g + 1) * N_STATE)
            c_sl = slice(D_SSD + N_GROUPS * N_STATE + g * N_STATE, D_SSD + N_GROUPS * N_STATE + (g + 1) * N_STATE)
            bg = xbc_ref[:, b_sl].astype(BF16)
            cg = xbc_ref[:, c_sl].astype(BF16)
            scores = lax.dot_general(cg, bg, NT, preferred_element_type=F32)
            hg = hprev_ref[0, g * GW:(g + 1) * GW, :]
            hgb = hg.astype(BF16)
            dhg = dh_ref[g * GW:(g + 1) * GW, :]
            dhgb = dhg.astype(BF16)
            p_all = lax.dot_general(cg, hgb, NT, preferred_element_type=F32)
            q_all = lax.dot_general(bg, dhgb, NT, preferred_element_type=F32)
            dscores = jnp.zeros((CHUNK, CHUNK), F32)
            dp_cols, a_cols = [], []
            for j in range(HPG):
                hh = g * HPG + j
                hs = slice(hh * HEADDIM, (hh + 1) * HEADDIM)
                js = slice(j * HEADDIM, (j + 1) * HEADDIM)
                cs_col = cs[:, hh:hh + 1]
                cs_row = cst[hh:hh + 1, :]
                decay = jnp.exp(jnp.where(causal, cs_col - cs_row, -jnp.inf))
                mmat = scores * decay
                xh = xbc_ref[:, hs]
                dtc = dt[:, hh:hh + 1]
                xdt = xh * dtc
                dyh = dyr_ref[:, hs]
                dm = _mm_nt(dyh, xdt)
                dxdt = _mm_tn(mmat, dyh)
                dscores = dscores + dm * decay
                e = dm * mmat
                dcs_c = jnp.sum(e, axis=1, keepdims=True)
                dcs_r = jnp.sum(e, axis=0, keepdims=True)
                ecs = jnp.exp(cs_col)
                dcs_c = dcs_c + jnp.sum(dyh * (ecs * p_all[:, js]), axis=1, keepdims=True)
                dp_cols.append(ecs * dyh)
                cl = cs_last[:, hh:hh + 1]
                w = jnp.exp(cl - cs_col)
                qh = q_all[:, js]
                dxdt = dxdt + w * qh
                dww = jnp.sum(qh * xdt, axis=1, keepdims=True) * w
                dcs_c = dcs_c - dww
                ecl = jnp.exp(cl)
                dlast = jnp.sum(dww) + ecl * jnp.sum(dhg[js, :] * hg[js, :])
                a_cols.append(w * xdt)
                dxbc_ref[:, hs] = dxdt * dtc + dyh * dsk_ref[:, hs]
                cacc_ref[0:1, hs] += jnp.sum(dyh * xh, axis=0, keepdims=True)
                ddt_c = jnp.sum(dxdt * xh, axis=1, keepdims=True)
                ddt = ddt + jnp.where(lane == hh, ddt_c, 0.0)
                dcs = dcs + jnp.where(lane == hh, dcs_c, 0.0)
                dcs_t = dcs_t + jnp.where(rowid == hh, dcs_r, 0.0)
                dcs_last = dcs_last + jnp.where(lane1 == hh, dlast, 0.0)
                dh_ref[g * GW + j * HEADDIM:g * GW + (j + 1) * HEADDIM, :] = ecl * dhg[js, :]
            dp = jnp.concatenate(dp_cols, axis=1).astype(BF16)
            amat = jnp.concatenate(a_cols, axis=1).astype(BF16)
            dsb = dscores.astype(BF16)
            dxbc_ref[:, c_sl] = (jnp.dot(dsb, bg, preferred_element_type=F32)
                                 + jnp.dot(dp, hgb, preferred_element_type=F32))
            dxbc_ref[:, b_sl] = (lax.dot_general(dsb, cg, TN, preferred_element_type=F32)
                                 + jnp.dot(amat, dhgb, preferred_element_type=F32))
            dh_ref[g * GW:(g + 1) * GW, :] += lax.dot_general(dp, cg, TN, preferred_element_type=F32)
        dcs = dcs - dcs_t.T
        dcs = dcs + jnp.where(rowid == CHUNK - 1, dcs_last, 0.0)
        dadt = lax.dot_general(tri, dcs, TN, preferred_element_type=F32, precision=HIGHEST)
        ddt = ddt + dadt * a
        da = jnp.sum(dadt * dt, axis=0, keepdims=True)
        ddt_raw = ddt * _sigmoid(dt_ref[...] + par_ref[0:1])
        ddt_raw = jnp.where(lane < N_HEADS, ddt_raw, 0.0)
        ddt_ref[...] = ddt_raw
        dpar_ref[0:1, :] += jnp.sum(ddt_raw, axis=0, keepdims=True)
        dpar_ref[1:2, :] += jnp.where(lane1 < N_HEADS, da * a, 0.0)

    row = lambda w: pl.BlockSpec((CHUNK, w), lambda i: (blk(i), 0))
    full = lambda s: pl.BlockSpec(s, lambda i: (0, 0))
    return pl.pallas_call(
        body, name="ssd_bwd", grid=(n_chunks,),
        out_shape=(jax.ShapeDtypeStruct((t, D_XBC), F32), jax.ShapeDtypeStruct((t, D_SSD), F32),
                   jax.ShapeDtypeStruct((t, DT_PAD), F32), jax.ShapeDtypeStruct((8, 128), F32),
                   jax.ShapeDtypeStruct((8, D_SSD), F32)),
        in_specs=[row(D_SSD), row(D_SSD), row(D_SSD), row(D_XBC), row(DT_PAD),
                  pl.BlockSpec((1, D_SSD, N_STATE), lambda i: (blk(i), 0, 0)),
                  full((8, 128)), full((1, D_SSD)), full((1, D_SSD))],
        out_specs=(row(D_XBC), row(D_SSD), row(DT_PAD), full((8, 128)), full((8, D_SSD))),
        scratch_shapes=[pltpu.VMEM((D_SSD, N_STATE), F32), pltpu.VMEM((CHUNK, D_SSD), F32)],
        compiler_params=_params(48),
    )(dyssd, yraw, z, xbc, dt_raw, hprev, par, dsk, normw)


def _conv_bwd(dxbc, xbc_pre, conv_w, conv_b, seq):
    t = xbc_pre.shape[0]
    tb = 256
    npb = seq // tb
    cw = 512

    def body(d_ref, cur_ref, halo_ref, w_ref, b_ref, o_ref, acc_ref):
        i = pl.program_id(1)

        @pl.when(i == 0)
        def _():
            acc_ref[...] = jnp.zeros_like(acc_ref)

        first = (i % npb) == 0
        halo = jnp.where(first, 0.0, halo_ref[...])
        pre, shifted = _conv_taps(cur_ref[...], halo, w_ref[...])
        pre = pre + b_ref[...]
        sg = _sigmoid(pre)
        dpre = d_ref[...] * (sg * (1.0 + pre * (1.0 - sg)))
        o_ref[...] = dpre
        for j in range(4):
            acc_ref[3 - j:4 - j, :] += jnp.sum(dpre * shifted[j], axis=0, keepdims=True)
        acc_ref[4:5, :] += jnp.sum(dpre, axis=0, keepdims=True)

    return pl.pallas_call(
        body, name="conv_bwd", grid=(D_XBC // cw, t // tb),
        out_shape=(jax.ShapeDtypeStruct((t, D_XBC), F32), jax.ShapeDtypeStruct((8, D_XBC), F32)),
        in_specs=[pl.BlockSpec((tb, cw), lambda j, i: (i, j)), pl.BlockSpec((tb, cw), lambda j, i: (i, j)),
                  pl.BlockSpec((8, cw), lambda j, i: (jnp.maximum(i * (tb // 8) - 1, 0), j)),
                  pl.BlockSpec((4, cw), lambda j, i: (0, j)), pl.BlockSpec((1, cw), lambda j, i: (0, j))],
        out_specs=(pl.BlockSpec((tb, cw), lambda j, i: (i, j)), pl.BlockSpec((8, cw), lambda j, i: (0, j))),
        compiler_params=_params(32),
    )(dxbc, xbc_pre, xbc_pre, conv_w, conv_b)


def _proj_bwd(dz, dpre, ddt, du5, x2, dxa, mod3, conv_w, w_in_pad, seq):
    t = x2.shape[0]
    tb = 512
    npb = seq // tb
    n_blocks = t // tb

    def body(dz_ref, dp_ref, nxt_ref, ddt_ref, du5_ref, x_ref, dxa_ref, mod_ref, cw_ref, w_hbm,
             gx_ref, u_ref, dxp_ref, bacc_ref, w_vmem, sem):
        i = pl.program_id(0)
        _load_once(w_hbm, w_vmem, sem)

        @pl.when(i % npb == 0)
        def _():
            bacc_ref[...] = jnp.zeros_like(bacc_ref)

        last = (i % npb) == npb - 1
        nxt = jnp.where(last, 0.0, nxt_ref[...])
        cur = dp_ref[...]
        xx = jnp.concatenate([cur, nxt], axis=0)
        w = cw_ref[...]
        dxp = w[3:4] * cur
        for j in (1, 2, 3):
            dxp = dxp + w[3 - j:4 - j] * pltpu.roll(xx, tb + 8 - j, axis=0)[0:tb]
        dxpb = dxp.astype(BF16)
        dxp_ref[...] = dxpb
        o1, o2, o3 = D_SSD, D_SSD + D_XBC, D_SSD + D_XBC + DT_PAD
        du = (lax.dot_general(dz_ref[...].astype(BF16), w_vmem[:, 0:o1], NT, preferred_element_type=F32)
              + lax.dot_general(dxpb, w_vmem[:, o1:o2], NT, preferred_element_type=F32)
              + lax.dot_general(ddt_ref[...].astype(BF16), w_vmem[:, o2:o3], NT, preferred_element_type=F32)
              + lax.dot_general(du5_ref[...].astype(BF16), w_vmem[:, o3:], NT, preferred_element_type=F32))
        m = mod_ref[0]
        xv = x_ref[...]
        u_ref[...] = (xv * (1.0 + m[1:2]) + m[0:1]).astype(BF16)
        gx_ref[...] = dxa_ref[...] + du * (1.0 + m[1:2])
        bacc_ref[0, 0:1, :] += jnp.sum(du, axis=0, keepdims=True)
        bacc_ref[0, 1:2, :] += jnp.sum(du * xv, axis=0, keepdims=True)

    row = lambda w: pl.BlockSpec((tb, w), lambda i: (i, 0))
    nxt_rows = pl.BlockSpec((8, D_XBC), lambda i: (jnp.minimum((i + 1) * (tb // 8), t // 8 - 1), 0))
    return pl.pallas_call(
        body, name="proj_bwd", grid=(n_blocks,),
        out_shape=(jax.ShapeDtypeStruct((t, D_MODEL), F32), jax.ShapeDtypeStruct((t, D_MODEL), BF16),
                   jax.ShapeDtypeStruct((t, D_XBC), BF16), jax.ShapeDtypeStruct((t // seq, 8, D_MODEL), F32)),
        in_specs=[row(D_SSD), row(D_XBC), nxt_rows, row(DT_PAD), row(D_S5), row(D_MODEL), row(D_MODEL),
                  pl.BlockSpec((1, N_MOD, D_MODEL), lambda i: (i // npb, 0, 0)),
                  pl.BlockSpec((4, D_XBC), lambda i: (0, 0)), ANY],
        out_specs=(row(D_MODEL), row(D_MODEL), row(D_XBC), pl.BlockSpec((1, 8, D_MODEL), lambda i: (i // npb, 0, 0))),
        scratch_shapes=[pltpu.VMEM((D_MODEL, D_INP), BF16), pltpu.SemaphoreType.DMA],
        compiler_params=_params(60),
    )(dz, dpre, dpre, ddt, du5, x2, dxa, mod3, conv_w, w_in_pad)


def _pad_rows(a, mult):
    r = a.shape[0]
    pad = (-r) % mult
    return a if pad == 0 else jnp.concatenate([a, jnp.zeros((pad,) + a.shape[1:], a.dtype)], axis=0)


_SMALL = ["conv_w", "conv_b", "dt_bias", "a_log", "d_ssd", "norm_w", "s5_a_re", "s5_a_im", "s5_log_dt", "s5_b_re",
          "s5_b_im", "s5_c_re", "s5_c_im", "s5_d", "b_glu", "ln1_g", "ln1_b", "b1", "b2", "ln2_g", "ln2_b"]


def _pack_small(d):
    flat = jnp.concatenate([d[n].reshape(-1).astype(F32) for n in _SMALL])
    pad = (-flat.shape[0]) % (128 * 256)
    return jnp.concatenate([flat, jnp.zeros((pad,), F32)]).reshape(-1, 128)


def _unpack_small(p, shapes):
    flat = p.reshape(-1)
    out, off = {}, 0
    for n in _SMALL:
        size = math.prod(shapes[n])
        out[n] = flat[off:off + size].reshape(shapes[n])
        off += size
    return out


def kernel(x, c, w_ada, b_ada, w_in, conv_w, conv_b, dt_bias, a_log, d_ssd, norm_w, s5_a_re, s5_a_im, s5_log_dt, s5_b_re, s5_b_im, s5_c_re, s5_c_im, s5_d, w_glu, b_glu, w_out, ln1_g, ln1_b, w1, b1, w2, b2, ln2_g, ln2_b, loss_target, m_w_ada, m_b_ada, m_w_in, m_conv_w, m_conv_b, m_dt_bias, m_a_log, m_d_ssd, m_norm_w, m_s5_a_re, m_s5_a_im, m_s5_log_dt, m_s5_b_re, m_s5_b_im, m_s5_c_re, m_s5_c_im, m_s5_d, m_w_glu, m_b_glu, m_w_out, m_ln1_g, m_ln1_b, m_w1, m_b1, m_w2, m_b2, m_ln2_g, m_ln2_b, v_w_ada, v_b_ada, v_w_in, v_conv_w, v_conv_b, v_dt_bias, v_a_log, v_d_ssd, v_norm_w, v_s5_a_re, v_s5_a_im, v_s5_log_dt, v_s5_b_re, v_s5_b_im, v_s5_c_re, v_s5_c_im, v_s5_d, v_w_glu, v_b_glu, v_w_out, v_ln1_g, v_ln1_b, v_w1, v_b1, v_w2, v_b2, v_ln2_g, v_ln2_b):
    weights = dict(w_ada=w_ada, b_ada=b_ada, w_in=w_in, conv_w=conv_w, conv_b=conv_b, dt_bias=dt_bias, a_log=a_log,
                   d_ssd=d_ssd, norm_w=norm_w, s5_a_re=s5_a_re, s5_a_im=s5_a_im, s5_log_dt=s5_log_dt, s5_b_re=s5_b_re,
                   s5_b_im=s5_b_im, s5_c_re=s5_c_re, s5_c_im=s5_c_im, s5_d=s5_d, w_glu=w_glu, b_glu=b_glu, w_out=w_out,
                   ln1_g=ln1_g, ln1_b=ln1_b, w1=w1, b1=b1, w2=w2, b2=b2, ln2_g=ln2_g, ln2_b=ln2_b)
    mom = dict(w_ada=m_w_ada, b_ada=m_b_ada, w_in=m_w_in, conv_w=m_conv_w, conv_b=m_conv_b, dt_bias=m_dt_bias,
               a_log=m_a_log, d_ssd=m_d_ssd, norm_w=m_norm_w, s5_a_re=m_s5_a_re, s5_a_im=m_s5_a_im,
               s5_log_dt=m_s5_log_dt, s5_b_re=m_s5_b_re, s5_b_im=m_s5_b_im, s5_c_re=m_s5_c_re, s5_c_im=m_s5_c_im,
               s5_d=m_s5_d, w_glu=m_w_glu, b_glu=m_b_glu, w_out=m_w_out, ln1_g=m_ln1_g, ln1_b=m_ln1_b, w1=m_w1, b1=m_b1,
               w2=m_w2, b2=m_b2, ln2_g=m_ln2_g, ln2_b=m_ln2_b)
    var = dict(w_ada=v_w_ada, b_ada=v_b_ada, w_in=v_w_in, conv_w=v_conv_w, conv_b=v_conv_b, dt_bias=v_dt_bias,
               a_log=v_a_log, d_ssd=v_d_ssd, norm_w=v_norm_w, s5_a_re=v_s5_a_re, s5_a_im=v_s5_a_im,
               s5_log_dt=v_s5_log_dt, s5_b_re=v_s5_b_re, s5_b_im=v_s5_b_im, s5_c_re=v_s5_c_re, s5_c_im=v_s5_c_im,
               s5_d=v_s5_d, w_glu=v_w_glu, b_glu=v_b_glu, w_out=v_w_out, ln1_g=v_ln1_g, ln1_b=v_ln1_b, w1=v_w1, b1=v_b1,
               w2=v_w2, b2=v_b2, ln2_g=v_ln2_g, ln2_b=v_ln2_b)
    names = list(weights)
    shapes = {n: weights[n].shape for n in names}

    nb, seq, _ = x.shape
    t = nb * seq
    dev = _dev_index()
    x2 = x.reshape(t, D_MODEL)
    tgt2 = loss_target.reshape(t, D_MODEL)

    cw_cols = conv_w.shape[2]
    small_in = jnp.concatenate([c.reshape(-1), conv_w.reshape(-1)]).reshape(-1, 128)
    small_all = _all_gather([small_in], "gather_c_conv")[0].reshape(N_DEV, -1)
    c_all = small_all[:, :nb * D_MODEL].reshape(N_DEV * nb, D_MODEL)
    conv_w_full = small_all[:, nb * D_MODEL:].reshape(N_DEV, 4, cw_cols).transpose(1, 0, 2).reshape(4, D_XBC)

    big_names = ["w_in", "w_out", "w1", "w2", "w_glu"]
    gathered = dict(zip(big_names, _all_gather([weights[n][0].astype(BF16) for n in big_names], "gather_weights")))
    w_in_f = gathered["w_in"].transpose(1, 0, 2).reshape(D_MODEL, D_IN)
    w_in_pad = jnp.concatenate(
        [w_in_f[:, :D_SSD + D_XBC], w_in_f[:, D_SSD + D_XBC:D_SSD + D_XBC + N_HEADS],
         jnp.zeros((D_MODEL, DT_PAD - N_HEADS), BF16), w_in_f[:, D_SSD + D_XBC + N_HEADS:]], axis=1)
    w_out_f = gathered["w_out"].reshape(2 * D_MODEL, D_MODEL)
    w1_blocks = gathered["w1"]
    w2_f = gathered["w2"].reshape(D_FF, D_MODEL)
    w_glu_f = gathered["w_glu"].reshape(D_S5, D_S5)

    ada_cols = w_ada.shape[2]
    b_cols = lax.dynamic_slice_in_dim(b_ada, dev * ada_cols, ada_cols, axis=1)
    mod_cols = _mod_fwd(c_all, w_ada[0], b_cols)
    mod_all = _all_gather([mod_cols], "gather_mod")[0]
    mod_mine = lax.dynamic_slice_in_dim(mod_all, dev * nb, nb, axis=1)
    mod3 = mod_mine.transpose(1, 0, 2).reshape(nb, N_MOD, D_MODEL)

    def pad_lanes(v, n):
        return jnp.concatenate([v, jnp.zeros((v.shape[0], n - v.shape[1]), F32)], axis=1)

    par = _pad_rows(jnp.concatenate([pad_lanes(dt_bias, 128), pad_lanes(a_log, 128)], axis=0), 8)
    dsk = jnp.repeat(d_ssd[0], HEADDIM).reshape(1, D_SSD)
    ar = s5_a_re.reshape(1, S5_N)
    ai = s5_a_im.reshape(1, S5_N)
    ldt = jnp.repeat(s5_log_dt[0], S5_P).reshape(1, S5_N)
    br_t = s5_b_re[0].transpose(2, 0, 1).reshape(S5_CH, S5_N)
    bi_t = s5_b_im[0].transpose(2, 0, 1).reshape(S5_CH, S5_N)
    bb_re_t, bb_im_t, pf_re, pf_im, pr_re, pr_im = _s5_params_fwd(ar, ai, ldt, br_t, bi_t)
    mask_b = (jnp.arange(D_S5)[:, None] // S5_CH) == (jnp.arange(S5_N)[None, :] // S5_P)

    def dense_b(bt_):
        return jnp.where(mask_b, jnp.tile(bt_, (S5_GROUPS, 1)), 0.0).astype(BF16)

    def dense_c(cc):
        ct = cc[0].transpose(0, 2, 1).reshape(S5_N, S5_CH)
        return jnp.where(mask_b.T, jnp.tile(ct, (1, S5_GROUPS)), 0.0).astype(BF16)

    bb_re, bb_im = dense_b(bb_re_t), dense_b(bb_im_t)
    cc_re, cc_im = dense_c(s5_c_re), dense_c(s5_c_im)
    s5d = s5_d.reshape(1, D_S5)
    ln1 = jnp.concatenate([ln1_g, ln1_b], axis=0)
    vec1 = _pad_rows(jnp.concatenate([b2, ln2_g, ln2_b], axis=0), 8)

    z, xbc_pre, dt_raw, u5 = _proj_fwd(x2, mod3, w_in_pad, seq)
    xbc = _conv_fwd(xbc_pre, conv_w_full, conv_b, seq)
    yraw, ycat, hprev = _ssd_fwd(xbc, z, dt_raw, par, dsk, norm_w, seq)
    s_re, s_im, ypre, ycat = _s5_fwd(u5, bb_re, bb_im, cc_re, cc_im, pf_re, pf_im, s5d, w_glu_f, b_glu, ycat, seq)
    mix, x1 = _out_ln1(ycat, x2, mod3, w_out_f, ln1, seq)

    dx1, u2b, hb, dhpb, dob, gacc2, db1, bacc2 = _mlp_fwd_bwd(x1, tgt2, mod3, w1_blocks, w2_f, vec1, b1, seq)
    loss = lax.psum(0.5 / D_MODEL * jnp.sum(gacc2[3]), ("x", "y", "c"))

    dmixb, dxa, dyssd, dy5, gacc1, bacc1 = _ln1_out_bwd(dx1, x2, mix, mod3, w_out_f, ln1, seq)
    du5, g_re, g_im, ygb, dqb, dypb, vacc, sacc = _s5_bwd(dy5, ypre, u5, s_re, s_im, bb_re, bb_im, cc_re, cc_im,
                                                          pr_re, pr_im, s5d, w_glu_f, b_glu, seq)
    dxbc, dz, ddt, dpar, cacc = _ssd_bwd(dyssd, yraw, z, xbc, dt_raw, hprev, par, dsk, norm_w, seq)
    dpre, conv_acc = _conv_bwd(dxbc, xbc_pre, conv_w_full, conv_b, seq)
    grad_x2, ub, dxpb, bacc0 = _proj_bwd(dz, dpre, ddt, du5, x2, dxa, mod3, conv_w_full, w_in_pad, seq)

    g_w2 = _atb(hb, dob, "gw2")
    g_w1 = _atb(u2b, dhpb, "gw1", col_blocks=N_DEV)
    g_wout = _atb(ycat, dmixb, "gwout")
    g_win = jnp.concatenate([_atb(ub, dz, "gwin_z"), _atb(ub, dxpb, "gwin_xbc"),
                             _atb(ub, ddt, "gwin_dt")[:, :N_HEADS], _atb(ub, du5, "gwin_s5")], axis=1)
    g_wglu = _atb(ygb, dqb, "gwglu")
    d_cc_re = _atb(s_re, dypb, "gcc_re")
    d_cc_im = -_atb(s_im, dypb, "gcc_im")
    d_bb_re = _atb(u5, g_re, "gbb_re")
    d_bb_im = _atb(u5, g_im, "gbb_im")

    def diag_b(dd):
        return jnp.where(mask_b, dd, 0.0).reshape(S5_GROUPS, S5_CH, S5_N).sum(0)

    def diag_c(dd):
        return jnp.where(mask_b.T, dd, 0.0).reshape(S5_N, S5_GROUPS, S5_CH).sum(1).reshape(S5_GROUPS, S5_P, S5_CH).transpose(0, 2, 1)

    g_ar, g_ai, g_ldt, g_br_t, g_bi_t = _s5_params_bwd(ar, ai, ldt, br_t, bi_t, vacc[0:1], vacc[1:2],
                                                      diag_b(d_bb_re), diag_b(d_bb_im))

    def from_t(gt):
        return gt.reshape(S5_CH, S5_GROUPS, S5_P).transpose(1, 2, 0)

    small_g = dict(
        conv_w=conv_acc[0:4], conv_b=conv_acc[4:5], dt_bias=dpar[0:1, :N_HEADS], a_log=dpar[1:2, :N_HEADS],
        d_ssd=cacc[0].reshape(N_HEADS, HEADDIM).sum(1), norm_w=cacc[1:2],
        s5_a_re=g_ar, s5_a_im=g_ai, s5_log_dt=g_ldt[:, :S5_GROUPS], s5_b_re=from_t(g_br_t), s5_b_im=from_t(g_bi_t),
        s5_c_re=diag_c(d_cc_re), s5_c_im=diag_c(d_cc_im), s5_d=sacc[0:1], b_glu=sacc[1:2],
        ln1_g=gacc1[0:1], ln1_b=gacc1[1:2], b1=db1, b2=gacc2[2:3], ln2_g=gacc2[0:1], ln2_b=gacc2[1:2])

    dmod = jnp.concatenate([bacc0[:, 0], bacc0[:, 1], bacc1[:, 0], bacc2[:, 0], bacc2[:, 1], bacc2[:, 2]], axis=1)
    dmod_all = _all_gather([dmod], "gather_dmod")[0].reshape(N_DEV * nb, N_MOD * D_MODEL)
    dmod_cols = lax.dynamic_slice_in_dim(dmod_all, dev * ada_cols, ada_cols, axis=1)
    g_wada, g_bada = _mod_bwd(c_all, dmod_cols, dmod_all)

    in_cols = w_in.shape[2]
    big_g = dict(
        w_in=g_win.reshape(D_MODEL, N_DEV, in_cols).transpose(1, 0, 2),
        w_out=g_wout.reshape((N_DEV,) + w_out.shape[1:]), w1=g_w1,
        w2=g_w2.reshape((N_DEV,) + w2.shape[1:]), w_glu=g_wglu.reshape((N_DEV,) + w_glu.shape[1:]))
    by_dest = [big_g[n].reshape((4, 2) + big_g[n].shape[1:]) for n in big_names]
    from_sibling = _sibling_swap(by_dest, "rs_sibling_swap")
    core = lax.axis_index("c").astype(jnp.int32).reshape(1)
    chip_sums = [_add_halves(g, r, core, "rs_add_" + n) for g, r, n in zip(by_dest, from_sibling, big_names)]
    parts = _chip_all_to_all(chip_sums, "rs_chip_all_to_all")

    res = {k: {} for k in "gdmv"}
    for n, p in zip(big_names, parts):
        outs = _adamw(p, weights[n][0], mom[n][0], var[n][0], "adamw_" + n)
        for k, a in zip("gdmv", outs):
            res[k][n] = a[None]

    ag, ad, am, av = _adamw(g_wada[None], w_ada[0], m_w_ada[0], v_w_ada[0], "adamw_w_ada")
    for k, a in (("g", ag), ("d", ad), ("m", am), ("v", av)):
        res[k]["w_ada"] = a[None]
    bg_, bd_, bm_, bv_ = _adamw(g_bada.reshape(1, -1, 128), b_ada.reshape(-1, 128), m_b_ada.reshape(-1, 128),
                                v_b_ada.reshape(-1, 128), "adamw_b_ada")
    for k, a in (("g", bg_), ("d", bd_), ("m", bm_), ("v", bv_)):
        res[k]["b_ada"] = a.reshape(shapes["b_ada"])

    small_shapes = dict(shapes)
    small_shapes["conv_w"] = (1, 4, D_XBC)
    small_parts = _all_gather([_pack_small(small_g)], "gather_small_grads")[0]
    rep = {n: (jnp.zeros((1, 4, D_XBC), F32) if n == "conv_w" else weights[n]) for n in _SMALL}
    rep_m = {n: (jnp.zeros((1, 4, D_XBC), F32) if n == "conv_w" else mom[n]) for n in _SMALL}
    rep_v = {n: (jnp.ones((1, 4, D_XBC), F32) if n == "conv_w" else var[n]) for n in _SMALL}
    sg_, sd_, sm_, sv_ = _adamw(small_parts, _pack_small(rep), _pack_small(rep_m), _pack_small(rep_v), "adamw_small")
    for k, p in (("g", sg_), ("d", sd_), ("m", sm_), ("v", sv_)):
        un = _unpack_small(p, small_shapes)
        for n in _SMALL:
            if n != "conv_w":
                res[k][n] = un[n]
    g_conv_full = _unpack_small(sg_, small_shapes)["conv_w"][0]
    g_conv_mine = lax.dynamic_slice_in_dim(g_conv_full, dev * cw_cols, cw_cols, axis=1)
    cg_, cd_, cm_, cv_ = _adamw(g_conv_mine[None], conv_w[0], m_conv_w[0], v_conv_w[0], "adamw_conv_w")
    for k, a in (("g", cg_), ("d", cd_), ("m", cm_), ("v", cv_)):
        res[k]["conv_w"] = a[None]

    grad_x = grad_x2.reshape(nb, seq, D_MODEL)
    return (loss, grad_x, *[res["g"][n] for n in names], *[res["d"][n] for n in names],
            *[res["m"][n] for n in names], *[res["v"][n] for n in names])
```

```python
import functools
import math

import jax
import jax.numpy as jnp
from jax import lax
from jax.experimental import pallas as pl
from jax.experimental.pallas import tpu as pltpu

F32, BF16 = jnp.float32, jnp.bfloat16
MESH = pl.DeviceIdType.MESH
N_DEV = 8

D_MODEL = 1024
D_SSD = 1536
N_HEADS = 24
HEADDIM = 64
N_GROUPS = 4
HPG = 6
GW = HPG * HEADDIM
N_STATE = 128
CHUNK = 128
D_XBC = 2560
D_S5 = 512
S5_GROUPS = 32
S5_CH = 16
S5_P = 64
S5_N = S5_GROUPS * S5_P
D_IN = 4632
DT_PAD = 128
D_INP = D_SSD + D_XBC + DT_PAD + D_S5
D_FF = 4096
N_MOD = 6
ALPHA = 2.0 ** 0.25
EPS = 1e-5
LR, B1, B2, AEPS, WD, STEP = 0.001, 0.9, 0.999, 1e-08, 0.01, 10

NT = (((1,), (1,)), ((), ()))
TN = (((0,), (0,)), ((), ()))
ANY = pl.BlockSpec(memory_space=pl.ANY)
HIGHEST = lax.Precision.HIGHEST


def _mm(a, b):
    return jnp.dot(a.astype(BF16), b.astype(BF16), preferred_element_type=F32)


def _mm_nt(a, b):
    return lax.dot_general(a.astype(BF16), b.astype(BF16), NT, preferred_element_type=F32)


def _mm_tn(a, b):
    return lax.dot_general(a.astype(BF16), b.astype(BF16), TN, preferred_element_type=F32)


def _row_block(r, cap):
    best = r
    for cand in range(8, min(r, cap) + 1, 8):
        if r % cand == 0:
            best = cand
    return best if best <= cap else r


def _params(vmem_mb):
    return pltpu.CompilerParams(vmem_limit_bytes=vmem_mb << 20)


def _sigmoid(x):
    return 1.0 / (1.0 + jnp.exp(-x))


def _softplus(x):
    return jnp.maximum(x, 0.0) + jnp.log(1.0 + jnp.exp(-jnp.abs(x)))


_GK = math.sqrt(2.0 / math.pi)


def _gelu(x):
    return 0.5 * x * (1.0 + jnp.tanh(_GK * (x + 0.044715 * x * x * x)))


def _gelu_grad(x):
    t = jnp.tanh(_GK * (x + 0.044715 * x * x * x))
    return 0.5 * (1.0 + t) + 0.5 * x * (1.0 - t * t) * _GK * (1.0 + 3.0 * 0.044715 * x * x)


def _dev_index():
    return 4 * lax.axis_index("x") + 2 * lax.axis_index("y") + lax.axis_index("c")


def _all_gather(xs, name):
    n = len(xs)

    def body(*refs):
        x_refs, out_refs = refs[:n], refs[n:2 * n]
        send_sems, recv_sems, local_sems = refs[2 * n:]
        ix, iy, ic = lax.axis_index("x"), lax.axis_index("y"), lax.axis_index("c")
        me, sibling = (ix, iy, ic), (ix, iy, 1 - ic)
        chips = [(1 - ix, iy), (ix, 1 - iy), (1 - ix, 1 - iy)]

        def slot(a, px, py, pc):
            return out_refs[a].at[4 * px + 2 * py + pc]

        def copy(a, k, block, to, src=None):
            return pltpu.make_async_remote_copy(
                src_ref=slot(a, *block) if src is None else src, dst_ref=slot(a, *block),
                send_sem=send_sems.at[7 * a + k], recv_sem=recv_sems.at[7 * a + k], device_id=to, device_id_type=MESH)

        mine = [pltpu.make_async_copy(x_refs[a], slot(a, *me), local_sems.at[a]) for a in range(n)]
        for cp in mine:
            cp.start()
        first = []
        for j, chip in enumerate(chips):
            first += [copy(a, 1 + j, me, (*chip, ic), src=x_refs[a]) for a in range(n)]
        first += [copy(a, 0, me, sibling, src=x_refs[a]) for a in range(n)]
        for cp in first:
            cp.start()
        passed = []
        for j, chip in enumerate(chips):
            for a in range(n):
                copy(a, 1 + j, (*chip, ic), me).wait_recv()
                cp = copy(a, 4 + j, (*chip, ic), sibling)
                cp.start()
                passed.append(cp)
        for a in range(n):
            copy(a, 0, sibling, me).wait_recv()
            for j, chip in enumerate(chips):
                copy(a, 4 + j, (*chip, 1 - ic), me).wait_recv()
        for cp in first + passed:
            cp.wait_send()
        for cp in mine:
            cp.wait()

    return pl.pallas_call(
        body, name=name, out_shape=tuple(jax.ShapeDtypeStruct((N_DEV,) + x.shape, x.dtype) for x in xs),
        in_specs=[ANY] * n, out_specs=tuple([ANY] * n),
        scratch_shapes=[pltpu.SemaphoreType.DMA((7 * n,)), pltpu.SemaphoreType.DMA((7 * n,)),
                        pltpu.SemaphoreType.DMA((n,))],
    )(*xs)


def _sibling_swap(gs, name):
    n = len(gs)

    def body(*refs):
        g_refs, recv_refs = refs[:n], refs[n:2 * n]
        send_sems, recv_sems = refs[2 * n:]
        ix, iy, ic = lax.axis_index("x"), lax.axis_index("y"), lax.axis_index("c")

        def block(g_ref, q):
            if len(g_ref.shape) == 4:
                return g_ref.at[q, 1 - ic]
            cw = g_ref.shape[1] // N_DEV
            return g_ref.at[:, pl.ds(pl.multiple_of((2 * q + 1 - ic) * cw, 128), cw)]

        cps = []
        for a in range(n):
            for q in range(4):
                cps.append(pltpu.make_async_remote_copy(
                    src_ref=block(g_refs[a], q), dst_ref=recv_refs[a].at[q],
                    send_sem=send_sems.at[4 * a + q], recv_sem=recv_sems.at[4 * a + q],
                    device_id=(ix, iy, 1 - ic), device_id_type=MESH))
        for cp in cps:
            cp.start()
        for cp in cps:
            cp.wait()

    return pl.pallas_call(
        body, name=name,
        out_shape=tuple(jax.ShapeDtypeStruct(
            (4,) + (g.shape[2:] if g.ndim == 4 else (g.shape[0], g.shape[1] // N_DEV)), g.dtype) for g in gs),
        in_specs=[ANY] * n, out_specs=tuple([ANY] * n),
        scratch_shapes=[pltpu.SemaphoreType.DMA((4 * n,)), pltpu.SemaphoreType.DMA((4 * n,))],
    )(*gs)


def _chip_all_to_all(hs, name):
    n = len(hs)

    def body(*refs):
        h_refs, out_refs = refs[:n], refs[n:2 * n]
        send_sems, recv_sems, local_sems = refs[2 * n:]
        ix, iy, ic = lax.axis_index("x"), lax.axis_index("y"), lax.axis_index("c")
        me = 2 * ix + iy
        peers = [(1 - ix, iy), (ix, 1 - iy), (1 - ix, 1 - iy)]
        mine = [pltpu.make_async_copy(h_refs[a].at[me], out_refs[a].at[me], local_sems.at[a]) for a in range(n)]
        for cp in mine:
            cp.start()

        def copy(a, k, src_slot, dst_slot, peer):
            return pltpu.make_async_remote_copy(
                src_ref=h_refs[a].at[src_slot], dst_ref=out_refs[a].at[dst_slot],
                send_sem=send_sems.at[3 * a + k], recv_sem=recv_sems.at[3 * a + k],
                device_id=(*peer, ic), device_id_type=MESH)

        sends = [copy(a, k, 2 * px + py, me, (px, py)) for a in range(n) for k, (px, py) in enumerate(peers)]
        for cp in sends:
            cp.start()
        for a in range(n):
            for k, (px, py) in enumerate(peers):
                copy(a, k, 2 * px + py, 2 * px + py, (px, py)).wait_recv()
        for cp in sends:
            cp.wait_send()
        for cp in mine:
            cp.wait()

    return pl.pallas_call(
        body, name=name, out_shape=tuple(jax.ShapeDtypeStruct(h.shape, h.dtype) for h in hs),
        in_specs=[ANY] * n, out_specs=tuple([ANY] * n),
        scratch_shapes=[pltpu.SemaphoreType.DMA((3 * n,)), pltpu.SemaphoreType.DMA((3 * n,)),
                        pltpu.SemaphoreType.DMA((n,))],
    )(*hs)


def _add_halves(g, recv, core, name):
    _, r, c = recv.shape
    br = _row_block(r, 512)
    stacked = g.ndim == 4

    def body(core_ref, g_ref, r_ref, o_ref):
        o_ref[0] = ((g_ref[0, 0] if stacked else g_ref[...]) + r_ref[0]).astype(BF16)

    spec = pl.BlockSpec((1, br, c), lambda i, j, core_ref: (i, j, 0))
    if stacked:
        g_spec = pl.BlockSpec((1, 1, br, c), lambda i, j, core_ref: (i, core_ref[0], j, 0))
    else:
        g_spec = pl.BlockSpec((br, c), lambda i, j, core_ref: (j, 2 * i + core_ref[0]))
    return pl.pallas_call(
        body, name=name, out_shape=jax.ShapeDtypeStruct(recv.shape, BF16),
        grid_spec=pltpu.PrefetchScalarGridSpec(
            num_scalar_prefetch=1, grid=(4, r // br), in_specs=[g_spec, spec], out_specs=spec),
        compiler_params=_params(32),
    )(core, g, recv)


def _adamw(parts, w, m, v, name):
    n_parts, r, c = parts.shape
    br = _row_block(r, 512 if c <= 1024 else 256)

    def body(p_ref, w_ref, m_ref, v_ref, g_out, d_out, m_out, v_out):
        g = p_ref[0].astype(F32)
        for p in range(1, n_parts):
            g = g + p_ref[p].astype(F32)
        m2 = B1 * m_ref[...] + (1.0 - B1) * g
        v2 = B2 * v_ref[...] + (1.0 - B2) * (g * g)
        m_hat = m2 / (1.0 - B1 ** STEP)
        v_hat = v2 / (1.0 - B2 ** STEP)
        g_out[...] = g
        d_out[...] = -LR * (m_hat / (jnp.sqrt(v_hat) + AEPS) + WD * w_ref[...])
        m_out[...] = m2
        v_out[...] = v2

    spec = pl.BlockSpec((br, c), lambda i: (i, 0))
    out = jax.ShapeDtypeStruct((r, c), F32)
    return pl.pallas_call(
        body, name=name, out_shape=(out, out, out, out), grid=(r // br,),
        in_specs=[pl.BlockSpec((n_parts, br, c), lambda i: (0, i, 0)), spec, spec, spec],
        out_specs=(spec, spec, spec, spec), compiler_params=_params(40),
    )(parts, w, m, v)


def _atb(a, b, name, bt=512):
    t, k1 = a.shape
    k2 = b.shape[1]

    def pick(k):
        for cand in (1024, 768, 512, 384, 256, 128):
            if k % cand == 0:
                return cand
        return k

    b1, b2 = pick(k1), pick(k2)

    def body(a_ref, b_ref, o_ref):
        @pl.when(pl.program_id(2) == 0)
        def _():
            o_ref[...] = jnp.zeros_like(o_ref)
        o_ref[...] += _mm_tn(a_ref[...], b_ref[...])

    return pl.pallas_call(
        body, name=name, out_shape=jax.ShapeDtypeStruct((k1, k2), F32), grid=(k1 // b1, k2 // b2, t // bt),
        in_specs=[pl.BlockSpec((bt, b1), lambda i, j, k: (k, i)), pl.BlockSpec((bt, b2), lambda i, j, k: (k, j))],
        out_specs=pl.BlockSpec((b1, b2), lambda i, j, k: (i, j)), compiler_params=_params(40),
    )(a, b)


def _mod_fwd(c_all, w_ada, b_cols):
    def body(c_ref, w_ref, b_ref, o_ref):
        cc = c_ref[...]
        cond = cc * _sigmoid(cc)
        o_ref[...] = _mm(cond, w_ref[...]) + b_ref[...]

    return pl.pallas_call(body, name="mod_fwd", out_shape=jax.ShapeDtypeStruct((c_all.shape[0], w_ada.shape[1]), F32),
                          compiler_params=_params(32))(c_all, w_ada, b_cols)


def _mod_bwd(c_all, dmod_cols, dmod_all):
    def body(c_ref, dc_ref, da_ref, gw_ref, gb_ref):
        cc = c_ref[...]
        cond = cc * _sigmoid(cc)
        gw_ref[...] = _mm_tn(cond, dc_ref[...])
        gb_ref[...] = jnp.sum(da_ref[...], axis=0, keepdims=True)

    return pl.pallas_call(
        body, name="mod_bwd",
        out_shape=(jax.ShapeDtypeStruct((D_MODEL, dmod_cols.shape[1]), F32), jax.ShapeDtypeStruct((1, dmod_all.shape[1]), F32)),
        compiler_params=_params(32))(c_all, dmod_cols, dmod_all)


def _load_once(hbm_ref, vmem_ref, sem):
    @pl.when(pl.program_id(0) == 0)
    def _():
        cp = pltpu.make_async_copy(hbm_ref, vmem_ref, sem)
        cp.start()
        cp.wait()


def _proj_fwd(x2, mod3, w_in_pad, seq):
    t = x2.shape[0]
    tb = 512
    npb = seq // tb

    def body(x_ref, mod_ref, w_hbm, z_ref, xbc_ref, dt_ref, u5_ref, w_vmem, sem):
        _load_once(w_hbm, w_vmem, sem)
        m = mod_ref[0]
        u = (x_ref[...] * (1.0 + m[1:2]) + m[0:1]).astype(BF16)
        z_ref[...] = jnp.dot(u, w_vmem[:, 0:D_SSD], preferred_element_type=F32)
        xbc_ref[...] = jnp.dot(u, w_vmem[:, D_SSD:D_SSD + D_XBC], preferred_element_type=F32)
        dt_ref[...] = jnp.dot(u, w_vmem[:, D_SSD + D_XBC:D_SSD + D_XBC + DT_PAD], preferred_element_type=F32)
        u5_ref[...] = jnp.dot(u, w_vmem[:, D_SSD + D_XBC + DT_PAD:], preferred_element_type=F32)

    row = lambda w: pl.BlockSpec((tb, w), lambda i: (i, 0))
    return pl.pallas_call(
        body, name="proj_fwd", grid=(t // tb,),
        out_shape=(jax.ShapeDtypeStruct((t, D_SSD), F32), jax.ShapeDtypeStruct((t, D_XBC), F32),
                   jax.ShapeDtypeStruct((t, DT_PAD), F32), jax.ShapeDtypeStruct((t, D_S5), F32)),
        in_specs=[row(D_MODEL), pl.BlockSpec((1, N_MOD, D_MODEL), lambda i: (i // npb, 0, 0)), ANY],
        out_specs=(row(D_SSD), row(D_XBC), row(DT_PAD), row(D_S5)),
        scratch_shapes=[pltpu.VMEM((D_MODEL, D_INP), BF16), pltpu.SemaphoreType.DMA],
        compiler_params=_params(56),
    )(x2, mod3, w_in_pad)


def _conv_taps(cur, halo, w):
    tb = cur.shape[0]
    xx = jnp.concatenate([halo, cur], axis=0)
    acc = w[3:4] * cur
    shifted = [cur]
    for j in (1, 2, 3):
        sj = pltpu.roll(xx, j, axis=0)[8:8 + tb]
        shifted.append(sj)
        acc = acc + w[3 - j:4 - j] * sj
    return acc, shifted


def _conv_fwd(xbc_pre, conv_w, conv_b, seq):
    t = xbc_pre.shape[0]
    tb = 256
    npb = seq // tb
    cw = 512

    def body(cur_ref, halo_ref, w_ref, b_ref, o_ref):
        first = (pl.program_id(0) % npb) == 0
        halo = jnp.where(first, 0.0, halo_ref[...])
        pre, _ = _conv_taps(cur_ref[...], halo, w_ref[...])
        pre = pre + b_ref[...]
        o_ref[...] = pre * _sigmoid(pre)

    return pl.pallas_call(
        body, name="conv_fwd", grid=(t // tb, D_XBC // cw), out_shape=jax.ShapeDtypeStruct((t, D_XBC), F32),
        in_specs=[pl.BlockSpec((tb, cw), lambda i, j: (i, j)),
                  pl.BlockSpec((8, cw), lambda i, j: (jnp.maximum(i * (tb // 8) - 1, 0), j)),
                  pl.BlockSpec((4, cw), lambda i, j: (0, j)), pl.BlockSpec((1, cw), lambda i, j: (0, j))],
        out_specs=pl.BlockSpec((tb, cw), lambda i, j: (i, j)), compiler_params=_params(32),
    )(xbc_pre, xbc_pre, conv_w, conv_b)


N_PAIRS = N_HEADS // 2


def _split3(x):
    hi = x.astype(BF16)
    r = x - hi.astype(F32)
    mid = r.astype(BF16)
    lo = (r - mid.astype(F32)).astype(BF16)
    return hi, mid, lo


def _dot3(x, e, dims=(((1,), (0,)), ((), ()))):
    return sum(lax.dot_general(p, e, dims, preferred_element_type=F32) for p in _split3(x))


def _dot3_left(e, x, dims=(((1,), (0,)), ((), ()))):
    return sum(lax.dot_general(e, p, dims, preferred_element_type=F32) for p in _split3(x))


def _ssd_consts():
    head = jnp.arange(128)
    expand = (head[:, None] == jnp.arange(D_SSD)[None, :] // HEADDIM).astype(BF16)
    sel = (head[None, :, None] == 2 * jnp.arange(N_PAIRS)[:, None, None] + jnp.arange(256)[None, None, :] // 128)
    return expand, expand.T, sel.astype(BF16)


def _ssd_prep(dt_raw, par, expand):
    dtb = par[0:1]
    a = -jnp.exp(par[1:2])
    dt = _softplus(dt_raw + dtb)
    adt = dt * a
    row = lax.broadcasted_iota(jnp.int32, (CHUNK, CHUNK), 0)
    col = lax.broadcasted_iota(jnp.int32, (CHUNK, CHUNK), 1)
    causal = row >= col
    tri = causal.astype(BF16)
    cs = _dot3_left(tri, adt)
    cs_last = cs[CHUNK - 1:CHUNK, :]
    dt_c = _dot3(dt, expand)
    ecs_c = _dot3(jnp.exp(cs), expand)
    w_c = _dot3(jnp.exp(cs_last - cs), expand)
    return dt, a, cs, cs.T, causal, tri, dt_c, ecs_c, w_c


def _pair_decay(cs, cst, sel_p, pair, causal2):
    cols = _dot3(cs, sel_p)
    rows = jnp.concatenate([jnp.broadcast_to(cst[2 * pair:2 * pair + 1, :], (CHUNK, CHUNK)),
                            jnp.broadcast_to(cst[2 * pair + 1:2 * pair + 2, :], (CHUNK, CHUNK))], axis=1)
    return jnp.exp(jnp.where(causal2, cols - rows, -jnp.inf))


def _stack_heads(xp, left):
    return jnp.concatenate([jnp.where(left, xp, 0.0), jnp.where(left, 0.0, xp)], axis=0).astype(BF16)


def _ssd_fwd(xbc, z, dt_raw, par, dsk, normw, seq):
    t = xbc.shape[0]
    nc = seq // CHUNK
    n_chunks = t // CHUNK
    expand, _, sel = _ssd_consts()

    def body(xbc_ref, z_ref, dt_ref, par_ref, dsk_ref, nw_ref, ex_ref, sel_ref, yraw_ref, ycat_ref, hprev_ref, h_ref):
        @pl.when(pl.program_id(0) % nc == 0)
        def _():
            h_ref[...] = jnp.zeros_like(h_ref)
        hprev_ref[0] = h_ref[...]
        _, _, cs, cst, causal, _, dt_c, ecs_c, w_c = _ssd_prep(dt_ref[...], par_ref[...], ex_ref[...])
        cs_last = cs[CHUNK - 1:CHUNK, :]
        causal2 = jnp.concatenate([causal, causal], axis=1)
        left = lax.broadcasted_iota(jnp.int32, (CHUNK, 128), 1) < HEADDIM
        x = xbc_ref[:, 0:D_SSD]
        xdt = x * dt_c
        amat = (w_c * xdt).astype(BF16)
        zz = z_ref[...]
        silu_z = zz * _sigmoid(zz)
        for g in range(N_GROUPS):
            gs = slice(g * GW, (g + 1) * GW)
            bg = xbc_ref[:, D_SSD + g * N_STATE:D_SSD + (g + 1) * N_STATE].astype(BF16)
            cg = xbc_ref[:, D_SSD + (N_GROUPS + g) * N_STATE:D_SSD + (N_GROUPS + g + 1) * N_STATE].astype(BF16)
            scores = lax.dot_general(cg, bg, NT, preferred_element_type=F32)
            scores2 = jnp.concatenate([scores, scores], axis=1)
            hg = h_ref[gs, :]
            p_all = lax.dot_general(cg, hg.astype(BF16), NT, preferred_element_type=F32)
            ys = []
            for q in range(GW // 128):
                pair = g * (GW // 128) + q
                decay = _pair_decay(cs, cst, sel_ref[pair], pair, causal2)
                mcat = (scores2 * decay).astype(BF16)
                ys.append(jnp.dot(mcat, _stack_heads(xdt[:, pair * 128:(pair + 1) * 128], left),
                                  preferred_element_type=F32))
            yg = jnp.concatenate(ys, axis=1) + ecs_c[:, gs] * p_all + x[:, gs] * dsk_ref[:, gs]
            s_new = lax.dot_general(amat[:, gs], bg, TN, preferred_element_type=F32)
            for j in range(HPG):
                hh = g * HPG + j
                js = slice(j * HEADDIM, (j + 1) * HEADDIM)
                h_ref[g * GW + j * HEADDIM:g * GW + (j + 1) * HEADDIM, :] = (
                    hg[js, :] * jnp.exp(cs_last[:, hh:hh + 1]) + s_new[js, :])
            yraw_ref[:, gs] = yg
            v = yg * silu_z[:, gs]
            r = lax.rsqrt(jnp.mean(v * v, axis=-1, keepdims=True) + EPS)
            ycat_ref[:, gs] = (v * r * nw_ref[:, gs]).astype(BF16)

    row = lambda w: pl.BlockSpec((CHUNK, w), lambda i: (i, 0))
    full = lambda s: pl.BlockSpec(s, lambda i: (0,) * len(s))
    return pl.pallas_call(
        body, name="ssd_fwd", grid=(n_chunks,),
        out_shape=(jax.ShapeDtypeStruct((t, D_SSD), F32), jax.ShapeDtypeStruct((t, D_SSD + D_S5), BF16),
                   jax.ShapeDtypeStruct((n_chunks, D_SSD, N_STATE), F32)),
        in_specs=[row(D_XBC), row(D_SSD), row(DT_PAD), full((8, 128)), full((1, D_SSD)), full((1, D_SSD)),
                  full(expand.shape), full(sel.shape)],
        out_specs=(row(D_SSD), row(D_SSD), pl.BlockSpec((1, D_SSD, N_STATE), lambda i: (i, 0, 0))),
        scratch_shapes=[pltpu.VMEM((D_SSD, N_STATE), F32)],
        compiler_params=_params(40),
    )(xbc, z, dt_raw, par, dsk, normw, expand, sel)


S5_CW = 512


def _tile_scan(in_re, in_im, out_re, out_im, carry_re, carry_im, pw_re, pw_im, n_tiles, reverse):
    steps = (1, 2, 4)
    for cc in range(S5_N // S5_CW):
        cols = slice(cc * S5_CW, (cc + 1) * S5_CW)
        if reverse:
            pows = [(pw_re[8 - d:9 - d, cols], pw_im[8 - d:9 - d, cols]) for d in steps]
        else:
            pows = [(pw_re[d - 1:d, cols], pw_im[d - 1:d, cols]) for d in steps]
        a_re, a_im = pw_re[:, cols], pw_im[:, cols]
        rid = lax.broadcasted_iota(jnp.int32, (8, S5_CW), 0)

        def tile(i, carry, cols=cols, pows=pows, a_re=a_re, a_im=a_im, rid=rid):
            r = (n_tiles - 1 - i) if reverse else i
            rows = pl.ds(pl.multiple_of(r * 8, 8), 8)
            xr, xi = in_re[rows, cols], in_im[rows, cols]
            for (pr, pi), d in zip(pows, steps):
                if reverse:
                    sr, si = pltpu.roll(xr, 8 - d, axis=0), pltpu.roll(xi, 8 - d, axis=0)
                    keep = rid < 8 - d
                else:
                    sr, si = pltpu.roll(xr, d, axis=0), pltpu.roll(xi, d, axis=0)
                    keep = rid >= d
                sr, si = jnp.where(keep, sr, 0.0), jnp.where(keep, si, 0.0)
                xr, xi = xr + pr * sr - pi * si, xi + pr * si + pi * sr
            cr, ci = carry
            xr, xi = xr + a_re * cr - a_im * ci, xi + a_re * ci + a_im * cr
            out_re[rows, cols] = xr
            out_im[rows, cols] = xi
            edge = slice(0, 1) if reverse else slice(7, 8)
            return (jnp.broadcast_to(xr[edge], (8, S5_CW)), jnp.broadcast_to(xi[edge], (8, S5_CW)))

        c0 = (jnp.broadcast_to(carry_re[0:1, cols], (8, S5_CW)), jnp.broadcast_to(carry_im[0:1, cols], (8, S5_CW)))
        cr, ci = lax.fori_loop(0, n_tiles, tile, c0)
        carry_re[:, cols] = cr
        carry_im[:, cols] = ci


def _s5_params_math(ar, ai, ldt, br, bi):
    dt = jnp.exp(ldt)
    mag = jnp.exp(ar * dt)
    ang = ai * dt
    ab_re = mag * jnp.cos(ang)
    ab_im = mag * jnp.sin(ang)
    den = ar * ar + ai * ai
    n_re = ab_re - 1.0
    coef_re = (n_re * ar + ab_im * ai) / den
    coef_im = (ab_im * ar - n_re * ai) / den
    bb_re = coef_re * br - coef_im * bi
    bb_im = coef_re * bi + coef_im * br
    return ab_re, ab_im, bb_re, bb_im


def _s5_params_fwd(ar, ai, ldt, br, bi):
    def body(ar_ref, ai_ref, ldt_ref, br_ref, bi_ref, bbr_ref, bbi_ref, pfr_ref, pfi_ref, prr_ref, pri_ref):
        ab_re, ab_im, bb_re, bb_im = _s5_params_math(ar_ref[...], ai_ref[...], ldt_ref[...], br_ref[...], bi_ref[...])
        bbr_ref[...] = bb_re
        bbi_ref[...] = bb_im
        pr, pi = ab_re, ab_im
        for k in range(8):
            pfr_ref[k:k + 1, :] = pr
            pfi_ref[k:k + 1, :] = pi
            prr_ref[7 - k:8 - k, :] = pr
            pri_ref[7 - k:8 - k, :] = -pi
            pr, pi = pr * ab_re - pi * ab_im, pr * ab_im + pi * ab_re

    b16 = jax.ShapeDtypeStruct((S5_CH, S5_N), F32)
    p8 = jax.ShapeDtypeStruct((8, S5_N), F32)
    return pl.pallas_call(body, name="s5_params_fwd", out_shape=(b16, b16, p8, p8, p8, p8),
                          compiler_params=_params(32))(ar, ai, ldt, br, bi)


def _s5_params_bwd(ar, ai, ldt, br, bi, d_ab_re, d_ab_im, d_bb_re, d_bb_im):
    def body(ar_ref, ai_ref, ldt_ref, br_ref, bi_ref, dar_ref, dai_ref, dbr_ref, dbi_ref,
             gar_ref, gai_ref, gldt_ref, gbr_ref, gbi_ref):
        _, vjp = jax.vjp(_s5_params_math, ar_ref[...], ai_ref[...], ldt_ref[...], br_ref[...], bi_ref[...])
        g_ar, g_ai, g_ldt, g_br, g_bi = vjp((dar_ref[...], dai_ref[...], dbr_ref[...], dbi_ref[...]))
        gar_ref[...] = g_ar
        gai_ref[...] = g_ai
        gbr_ref[...] = g_br
        gbi_ref[...] = g_bi
        lane = lax.broadcasted_iota(jnp.int32, (S5_N, 128), 0) // S5_P
        grp = lax.broadcasted_iota(jnp.int32, (S5_N, 128), 1)
        fold = (lane == grp).astype(F32)
        gldt_ref[...] = jnp.dot(g_ldt, fold, preferred_element_type=F32, precision=HIGHEST)

    v1 = jax.ShapeDtypeStruct((1, S5_N), F32)
    b16 = jax.ShapeDtypeStruct((S5_CH, S5_N), F32)
    return pl.pallas_call(body, name="s5_params_bwd",
                          out_shape=(v1, v1, jax.ShapeDtypeStruct((1, 128), F32), b16, b16),
                          compiler_params=_params(32))(ar, ai, ldt, br, bi, d_ab_re, d_ab_im, d_bb_re, d_bb_im)


def _s5_fwd(u5, bb_re, bb_im, cc_re, cc_im, pf_re, pf_im, s5d, w_glu, b_glu, ycat, seq):
    t = u5.shape[0]
    tb = 256
    npb = seq // tb

    def body(u_ref, bbr_ref, bbi_ref, ccr_ref, cci_ref, pfr_ref, pfi_ref, d_ref, wg_ref, bg_ref, ycat_hbm,
             sre_ref, sim_ref, ypre_ref, y5_ref, bur, bui, car, cai):
        del ycat_hbm

        @pl.when(pl.program_id(0) % npb == 0)
        def _():
            car[...] = jnp.zeros_like(car)
            cai[...] = jnp.zeros_like(cai)
        u = u_ref[...]
        ub = u.astype(BF16)
        bur[...] = jnp.dot(ub, bbr_ref[...], preferred_element_type=F32)
        bui[...] = jnp.dot(ub, bbi_ref[...], preferred_element_type=F32)
        _tile_scan(bur, bui, sre_ref, sim_ref, car, cai, pfr_ref, pfi_ref, tb // 8, reverse=False)
        ypre = _mm(sre_ref[...], ccr_ref[...]) - _mm(sim_ref[...], cci_ref[...]) + u * d_ref[...]
        ypre_ref[...] = ypre
        yg = _gelu(ypre)
        y5_ref[...] = (yg * _sigmoid(_mm(yg, wg_ref[...]) + bg_ref[...])).astype(BF16)

    row = lambda w: pl.BlockSpec((tb, w), lambda i: (i, 0))
    full = lambda a: pl.BlockSpec(a.shape, lambda i: (0, 0))
    return pl.pallas_call(
        body, name="s5_fwd", grid=(t // tb,),
        out_shape=(jax.ShapeDtypeStruct((t, S5_N), F32), jax.ShapeDtypeStruct((t, S5_N), F32),
                   jax.ShapeDtypeStruct((t, D_S5), F32), jax.ShapeDtypeStruct(ycat.shape, BF16)),
        in_specs=[row(D_S5), full(bb_re), full(bb_im), full(cc_re), full(cc_im), full(pf_re), full(pf_im),
                  full(s5d), full(w_glu), full(b_glu), ANY],
        out_specs=(row(S5_N), row(S5_N), row(D_S5), pl.BlockSpec((tb, D_S5), lambda i: (i, D_SSD // D_S5))),
        input_output_aliases={10: 3},
        scratch_shapes=[pltpu.VMEM((tb, S5_N), F32), pltpu.VMEM((tb, S5_N), F32),
                        pltpu.VMEM((8, S5_N), F32), pltpu.VMEM((8, S5_N), F32)],
        compiler_params=_params(48),
    )(u5, bb_re, bb_im, cc_re, cc_im, pf_re, pf_im, s5d, w_glu, b_glu, ycat)


def _layer_norm(r, g, b):
    mu = jnp.mean(r, axis=-1, keepdims=True)
    xc = r - mu
    rstd = lax.rsqrt(jnp.mean(xc * xc, axis=-1, keepdims=True) + EPS)
    xhat = xc * rstd
    return xhat * g + b, xhat, rstd


def _layer_norm_bwd(dy, xhat, rstd, g):
    dxhat = dy * g
    return rstd * (dxhat - jnp.mean(dxhat, axis=-1, keepdims=True)
                   - xhat * jnp.mean(dxhat * xhat, axis=-1, keepdims=True))


def _out_ln1(ycat, x2, mod3, w_out, ln1, seq):
    t = x2.shape[0]
    tb = 512
    npb = seq // tb

    def body(y_ref, x_ref, mod_ref, w_ref, ln_ref, mix_ref, x1_ref):
        m = mod_ref[0]
        mix = jnp.dot(y_ref[...], w_ref[...], preferred_element_type=F32)
        mix_ref[...] = mix
        r1 = ALPHA * x_ref[...] + (1.0 + m[2:3]) * mix
        x1_ref[...] = _layer_norm(r1, ln_ref[0:1], ln_ref[1:2])[0]

    row = lambda w: pl.BlockSpec((tb, w), lambda i: (i, 0))
    return pl.pallas_call(
        body, name="out_ln1", grid=(t // tb,),
        out_shape=(jax.ShapeDtypeStruct((t, D_MODEL), F32), jax.ShapeDtypeStruct((t, D_MODEL), F32)),
        in_specs=[row(D_SSD + D_S5), row(D_MODEL), pl.BlockSpec((1, N_MOD, D_MODEL), lambda i: (i // npb, 0, 0)),
                  pl.BlockSpec(w_out.shape, lambda i: (0, 0)), pl.BlockSpec(ln1.shape, lambda i: (0, 0))],
        out_specs=(row(D_MODEL), row(D_MODEL)), compiler_params=_params(48),
    )(ycat, x2, mod3, w_out, ln1)


def _mlp_fwd_bwd(x1, tgt, mod3, w1, w2, vec1, b1, seq):
    t = x1.shape[0]
    tb = 256
    npb = seq // tb
    n_fb, _, fb = w1.shape

    def body(x1_ref, tgt_ref, mod_ref, w1_hbm, w2_hbm, v_ref, b1_ref,
             dx1_ref, u2_ref, h_ref, dhp_ref, do_ref, gacc_ref, db1_ref, bacc_ref, w1_v, w2_v, sem1, sem2):
        i = pl.program_id(0)
        @pl.when(i == 0)
        def _():
            cps = [pltpu.make_async_copy(w1_hbm.at[k], w1_v.at[:, k * fb:(k + 1) * fb], sem1.at[k])
                   for k in range(n_fb)]
            for cp in cps:
                cp.start()
            for cp in cps:
                cp.wait()
        _load_once(w2_hbm, w2_v, sem2)

        @pl.when(i == 0)
        def _():
            gacc_ref[...] = jnp.zeros_like(gacc_ref)
            db1_ref[...] = jnp.zeros_like(db1_ref)

        @pl.when(i % npb == 0)
        def _():
            bacc_ref[...] = jnp.zeros_like(bacc_ref)

        m = mod_ref[0]
        sh2, sc2, g2 = m[3:4], m[4:5], m[5:6]
        x1v = x1_ref[...]
        u2 = (x1v * (1.0 + sc2) + sh2).astype(BF16)
        u2_ref[...] = u2
        hr = jnp.maximum(jnp.dot(u2, w1_v[...], preferred_element_type=F32) + b1_ref[...], 0.0)
        hb = (hr * hr).astype(BF16)
        h_ref[...] = hb
        o = jnp.dot(hb, w2_v[...], preferred_element_type=F32) + v_ref[0:1]
        r2 = ALPHA * x1v + (1.0 + g2) * o
        y, xhat, rstd = _layer_norm(r2, v_ref[1:2], v_ref[2:3])
        err = y - tgt_ref[...]
        dy = err * (1.0 / D_MODEL)
        dr2 = _layer_norm_bwd(dy, xhat, rstd, v_ref[1:2])
        do = (1.0 + g2) * dr2
        dob = do.astype(BF16)
        do_ref[...] = dob
        gacc_ref[0:1, :] += jnp.sum(dy * xhat, axis=0, keepdims=True)
        gacc_ref[1:2, :] += jnp.sum(dy, axis=0, keepdims=True)
        gacc_ref[2:3, :] += jnp.sum(do, axis=0, keepdims=True)
        gacc_ref[3:4, :] += jnp.sum(err * err, axis=0, keepdims=True)
        dhpre = lax.dot_general(dob, w2_v[...], NT, preferred_element_type=F32) * (2.0 * hr)
        dhpb = dhpre.astype(BF16)
        dhp_ref[...] = dhpb
        db1_ref[...] += jnp.sum(dhpre, axis=0, keepdims=True)
        du2 = lax.dot_general(dhpb, w1_v[...], NT, preferred_element_type=F32)
        dx1_ref[...] = ALPHA * dr2 + du2 * (1.0 + sc2)
        bacc_ref[0, 0:1, :] += jnp.sum(du2, axis=0, keepdims=True)
        bacc_ref[0, 1:2, :] += jnp.sum(du2 * x1v, axis=0, keepdims=True)
        bacc_ref[0, 2:3, :] += jnp.sum(dr2 * o, axis=0, keepdims=True)

    row = lambda w: pl.BlockSpec((tb, w), lambda i: (i, 0))
    return pl.pallas_call(
        body, name="mlp_fwd_bwd", grid=(t // tb,),
        out_shape=(jax.ShapeDtypeStruct((t, D_MODEL), F32), jax.ShapeDtypeStruct((t, D_MODEL), BF16),
                   jax.ShapeDtypeStruct((t, D_FF), BF16), jax.ShapeDtypeStruct((t, D_FF), BF16),
                   jax.ShapeDtypeStruct((t, D_MODEL), BF16), jax.ShapeDtypeStruct((8, D_MODEL), F32),
                   jax.ShapeDtypeStruct((1, D_FF), F32), jax.ShapeDtypeStruct((t // seq, 8, D_MODEL), F32)),
        in_specs=[row(D_MODEL), row(D_MODEL), pl.BlockSpec((1, N_MOD, D_MODEL), lambda i: (i // npb, 0, 0)), ANY, ANY,
                  pl.BlockSpec(vec1.shape, lambda i: (0, 0)), pl.BlockSpec(b1.shape, lambda i: (0, 0))],
        out_specs=(row(D_MODEL), row(D_MODEL), row(D_FF), row(D_FF), row(D_MODEL),
                   pl.BlockSpec((8, D_MODEL), lambda i: (0, 0)), pl.BlockSpec((1, D_FF), lambda i: (0, 0)),
                   pl.BlockSpec((1, 8, D_MODEL), lambda i: (i // npb, 0, 0))),
        scratch_shapes=[pltpu.VMEM((D_MODEL, n_fb * fb), BF16), pltpu.VMEM((D_FF, D_MODEL), BF16),
                        pltpu.SemaphoreType.DMA((n_fb,)), pltpu.SemaphoreType.DMA],
        compiler_params=_params(60),
    )(x1, tgt, mod3, w1, w2, vec1, b1)


def _ln1_out_bwd(dx1, x2, mix, mod3, w_out, ln1, seq):
    t = x2.shape[0]
    tb = 512
    npb = seq // tb

    def body(dx1_ref, x_ref, mix_ref, mod_ref, w_ref, ln_ref, dmix_ref, dxa_ref, dys_ref, dy5_ref, gacc_ref, bacc_ref):
        i = pl.program_id(0)

        @pl.when(i == 0)
        def _():
            gacc_ref[...] = jnp.zeros_like(gacc_ref)

        @pl.when(i % npb == 0)
        def _():
            bacc_ref[...] = jnp.zeros_like(bacc_ref)

        m = mod_ref[0]
        mix = mix_ref[...]
        r1 = ALPHA * x_ref[...] + (1.0 + m[2:3]) * mix
        _, xhat, rstd = _layer_norm(r1, ln_ref[0:1], ln_ref[1:2])
        dx1v = dx1_ref[...]
        dr1 = _layer_norm_bwd(dx1v, xhat, rstd, ln_ref[0:1])
        gacc_ref[0:1, :] += jnp.sum(dx1v * xhat, axis=0, keepdims=True)
        gacc_ref[1:2, :] += jnp.sum(dx1v, axis=0, keepdims=True)
        bacc_ref[0, 0:1, :] += jnp.sum(dr1 * mix, axis=0, keepdims=True)
        dmix = ((1.0 + m[2:3]) * dr1).astype(BF16)
        dmix_ref[...] = dmix
        dxa_ref[...] = ALPHA * dr1
        dys_ref[...] = lax.dot_general(dmix, w_ref[0:D_SSD, :], NT, preferred_element_type=F32)
        dy5_ref[...] = lax.dot_general(dmix, w_ref[D_SSD:, :], NT, preferred_element_type=F32)

    row = lambda w: pl.BlockSpec((tb, w), lambda i: (i, 0))
    return pl.pallas_call(
        body, name="ln1_out_bwd", grid=(t // tb,),
        out_shape=(jax.ShapeDtypeStruct((t, D_MODEL), BF16), jax.ShapeDtypeStruct((t, D_MODEL), F32),
                   jax.ShapeDtypeStruct((t, D_SSD), F32), jax.ShapeDtypeStruct((t, D_S5), F32),
                   jax.ShapeDtypeStruct((8, D_MODEL), F32), jax.ShapeDtypeStruct((t // seq, 8, D_MODEL), F32)),
        in_specs=[row(D_MODEL), row(D_MODEL), row(D_MODEL), pl.BlockSpec((1, N_MOD, D_MODEL), lambda i: (i // npb, 0, 0)),
                  pl.BlockSpec(w_out.shape, lambda i: (0, 0)), pl.BlockSpec(ln1.shape, lambda i: (0, 0))],
        out_specs=(row(D_MODEL), row(D_MODEL), row(D_SSD), row(D_S5), pl.BlockSpec((8, D_MODEL), lambda i: (0, 0)),
                   pl.BlockSpec((1, 8, D_MODEL), lambda i: (i // npb, 0, 0))),
        compiler_params=_params(48),
    )(dx1, x2, mix, mod3, w_out, ln1)


def _s5_bwd(dy5, ypre, u5, s_re, s_im, bb_re, bb_im, cc_re, cc_im, pr_re, pr_im, s5d, w_glu, b_glu, seq):
    t = u5.shape[0]
    tb = 256
    npb = seq // tb
    n_blocks = t // tb

    def blk(i):
        return (i // npb) * npb + (npb - 1 - i % npb)

    def body(dy_ref, ypre_ref, u_ref, sre_ref, sim_ref, hre_ref, him_ref, bbr_ref, bbi_ref, ccr_ref, cci_ref,
             prr_ref, pri_ref, d_ref, wg_ref, bg_ref,
             du_ref, gre_ref, gim_ref, yg_ref, dq_ref, dyp_ref, vacc_ref, sacc_ref, dsr, dsi, gr, gi, car, cai):
        i = pl.program_id(0)

        @pl.when(i == 0)
        def _():
            vacc_ref[...] = jnp.zeros_like(vacc_ref)
            sacc_ref[...] = jnp.zeros_like(sacc_ref)

        @pl.when(i % npb == 0)
        def _():
            car[...] = jnp.zeros_like(car)
            cai[...] = jnp.zeros_like(cai)

        dy = dy_ref[...]
        ypre = ypre_ref[...]
        u = u_ref[...]
        yg = _gelu(ypre)
        sg = _sigmoid(_mm(yg, wg_ref[...]) + bg_ref[...])
        dq = dy * yg * sg * (1.0 - sg)
        dyg = dy * sg + _mm_nt(dq, wg_ref[...])
        dyp = dyg * _gelu_grad(ypre)
        yg_ref[...] = yg.astype(BF16)
        dq_ref[...] = dq.astype(BF16)
        dypb = dyp.astype(BF16)
        dyp_ref[...] = dypb
        dsr[...] = lax.dot_general(dypb, ccr_ref[...], NT, preferred_element_type=F32)
        dsi[...] = -lax.dot_general(dypb, cci_ref[...], NT, preferred_element_type=F32)
        _tile_scan(dsr, dsi, gr, gi, car, cai, prr_ref, pri_ref, tb // 8, reverse=True)
        g_re, g_im = gr[...], gi[...]
        first_rows = (i % npb) == npb - 1
        hre = jnp.where(first_rows, 0.0, hre_ref[...])
        him = jnp.where(first_rows, 0.0, him_ref[...])
        sp_re = pltpu.roll(jnp.concatenate([hre, sre_ref[...]], axis=0), 1, axis=0)[8:8 + tb]
        sp_im = pltpu.roll(jnp.concatenate([him, sim_ref[...]], axis=0), 1, axis=0)[8:8 + tb]
        vacc_ref[0:1, :] += jnp.sum(g_re * sp_re + g_im * sp_im, axis=0, keepdims=True)
        vacc_ref[1:2, :] += jnp.sum(g_im * sp_re - g_re * sp_im, axis=0, keepdims=True)
        grb, gib = g_re.astype(BF16), g_im.astype(BF16)
        gre_ref[...] = grb
        gim_ref[...] = gib
        du_ref[...] = (lax.dot_general(grb, bbr_ref[...], NT, preferred_element_type=F32)
                       + lax.dot_general(gib, bbi_ref[...], NT, preferred_element_type=F32) + dyp * d_ref[...])
        sacc_ref[0:1, :] += jnp.sum(dyp * u, axis=0, keepdims=True)
        sacc_ref[1:2, :] += jnp.sum(dq, axis=0, keepdims=True)

    row = lambda w: pl.BlockSpec((tb, w), lambda i: (blk(i), 0))
    halo = pl.BlockSpec((8, S5_N), lambda i: (jnp.maximum(blk(i) * (tb // 8) - 1, 0), 0))
    full = lambda a: pl.BlockSpec(a.shape, lambda i: (0, 0))
    return pl.pallas_call(
        body, name="s5_bwd", grid=(n_blocks,),
        out_shape=(jax.ShapeDtypeStruct((t, D_S5), F32), jax.ShapeDtypeStruct((t, S5_N), BF16),
                   jax.ShapeDtypeStruct((t, S5_N), BF16), jax.ShapeDtypeStruct((t, D_S5), BF16),
                   jax.ShapeDtypeStruct((t, D_S5), BF16), jax.ShapeDtypeStruct((t, D_S5), BF16),
                   jax.ShapeDtypeStruct((8, S5_N), F32), jax.ShapeDtypeStruct((8, D_S5), F32)),
        in_specs=[row(D_S5), row(D_S5), row(D_S5), row(S5_N), row(S5_N), halo, halo, full(bb_re), full(bb_im),
                  full(cc_re), full(cc_im), full(pr_re), full(pr_im), full(s5d), full(w_glu), full(b_glu)],
        out_specs=(row(D_S5), row(S5_N), row(S5_N), row(D_S5), row(D_S5), row(D_S5),
                   pl.BlockSpec((8, S5_N), lambda i: (0, 0)), pl.BlockSpec((8, D_S5), lambda i: (0, 0))),
        scratch_shapes=[pltpu.VMEM((tb, S5_N), F32), pltpu.VMEM((tb, S5_N), F32), pltpu.VMEM((tb, S5_N), F32),
                        pltpu.VMEM((tb, S5_N), F32), pltpu.VMEM((8, S5_N), F32), pltpu.VMEM((8, S5_N), F32)],
        compiler_params=_params(56),
    )(dy5, ypre, u5, s_re, s_im, s_re, s_im, bb_re, bb_im, cc_re, cc_im, pr_re, pr_im, s5d, w_glu, b_glu)


def _ssd_bwd(dyssd, yraw, z, xbc, dt_raw, hprev, par, dsk, normw, seq):
    t = xbc.shape[0]
    nc = seq // CHUNK
    n_chunks = t // CHUNK
    expand, fold, sel = _ssd_consts()

    def blk(i):
        return (i // nc) * nc + (nc - 1 - i % nc)

    def body(dy_ref, yraw_ref, z_ref, xbc_ref, dt_ref, hprev_ref, par_ref, dsk_ref, nw_ref, ex_ref, fold_ref, sel_ref,
             dxbc_ref, dz_ref, ddt_ref, dpar_ref, cacc_ref, dh_ref, dyr_ref):
        i = pl.program_id(0)

        @pl.when(i == 0)
        def _():
            dpar_ref[...] = jnp.zeros_like(dpar_ref)
            cacc_ref[...] = jnp.zeros_like(cacc_ref)

        @pl.when(i % nc == 0)
        def _():
            dh_ref[...] = jnp.zeros_like(dh_ref)

        zz = z_ref[...]
        sz = _sigmoid(zz)
        silu_z = zz * sz
        yraw = yraw_ref[...]
        for g in range(N_GROUPS):
            sl = slice(g * GW, (g + 1) * GW)
            v = yraw[:, sl] * silu_z[:, sl]
            r = lax.rsqrt(jnp.mean(v * v, axis=-1, keepdims=True) + EPS)
            dyg = dy_ref[:, sl]
            cacc_ref[1:2, sl] += jnp.sum(dyg * v * r, axis=0, keepdims=True)
            dyw = dyg * nw_ref[:, sl]
            dv = r * dyw - v * (r * r * r) * jnp.mean(dyw * v, axis=-1, keepdims=True)
            dyr_ref[:, sl] = dv * silu_z[:, sl]
            dz_ref[:, sl] = dv * yraw[:, sl] * (sz[:, sl] * (1.0 + zz[:, sl] * (1.0 - sz[:, sl])))

        dt, a, cs, cst, causal, tri, dt_c, ecs_c, w_c = _ssd_prep(dt_ref[...], par_ref[...], ex_ref[...])
        cs_last = cs[CHUNK - 1:CHUNK, :]
        causal2 = jnp.concatenate([causal, causal], axis=1)
        lane = lax.broadcasted_iota(jnp.int32, (CHUNK, 128), 1)
        left = lane < HEADDIM
        lane1 = lax.broadcasted_iota(jnp.int32, (1, 128), 1)
        x = xbc_ref[:, 0:D_SSD]
        xdt = x * dt_c
        dyr = dyr_ref[...]
        dyrb = dyr.astype(BF16)
        cacc_ref[0:1, :] += jnp.sum(dyr * x, axis=0, keepdims=True)
        dlast = jnp.zeros((1, 128), F32)
        dxdt_cols, diag_all, dww_cols = [], [], []
        for g in range(N_GROUPS):
            gs = slice(g * GW, (g + 1) * GW)
            b_sl = slice(D_SSD + g * N_STATE, D_SSD + (g + 1) * N_STATE)
            c_sl = slice(D_SSD + (N_GROUPS + g) * N_STATE, D_SSD + (N_GROUPS + g + 1) * N_STATE)
            bg = xbc_ref[:, b_sl].astype(BF16)
            cg = xbc_ref[:, c_sl].astype(BF16)
            scores = lax.dot_general(cg, bg, NT, preferred_element_type=F32)
            scores2 = jnp.concatenate([scores, scores], axis=1)
            hg = hprev_ref[0, gs, :]
            hgb = hg.astype(BF16)
            dhg = dh_ref[gs, :]
            dhgb = dhg.astype(BF16)
            q_all = lax.dot_general(bg, dhgb, NT, preferred_element_type=F32)
            dscores = jnp.zeros((CHUNK, CHUNK), F32)
            diag_cols = []
            for q in range(GW // 128):
                pair = g * (GW // 128) + q
                ps = slice(pair * 128, (pair + 1) * 128)
                decay = _pair_decay(cs, cst, sel_ref[pair], pair, causal2)
                mcat = (scores2 * decay).astype(BF16)
                dyp = dyrb[:, ps]
                dm = lax.dot_general(dyp, _stack_heads(xdt[:, ps], left), NT, preferred_element_type=F32)
                dmd = dm * decay
                dscores = dscores + dmd[:, 0:CHUNK] + dmd[:, CHUNK:]
                rr = lax.dot_general(mcat, dyp, TN, preferred_element_type=F32)
                diag_cols.append(jnp.where(left, rr[0:CHUNK], rr[CHUNK:]))
            wq = w_c[:, gs] * q_all
            diag_g = jnp.concatenate(diag_cols, axis=1)
            diag_all.append(diag_g)
            dxdt_cols.append(diag_g + wq)
            dww_cols.append(wq * xdt[:, gs])
            dp = (ecs_c[:, gs] * dyr[:, gs]).astype(BF16)
            amat = (w_c[:, gs] * xdt[:, gs]).astype(BF16)
            dsb = dscores.astype(BF16)
            dxbc_ref[:, c_sl] = (jnp.dot(dsb, bg, preferred_element_type=F32)
                                 + jnp.dot(dp, hgb, preferred_element_type=F32))
            dxbc_ref[:, b_sl] = (lax.dot_general(dsb, cg, TN, preferred_element_type=F32)
                                 + jnp.dot(amat, dhgb, preferred_element_type=F32))
            dh_in = lax.dot_general(dp, cg, TN, preferred_element_type=F32)
            for j in range(HPG):
                hh = g * HPG + j
                js = slice(j * HEADDIM, (j + 1) * HEADDIM)
                ecl = jnp.exp(cs_last[:, hh:hh + 1])
                dlast = dlast + jnp.where(lane1 == hh, ecl * jnp.sum(dhg[js, :] * hg[js, :]), 0.0)
                dh_ref[g * GW + j * HEADDIM:g * GW + (j + 1) * HEADDIM, :] = ecl * dhg[js, :] + dh_in[js, :]
        dxdt = jnp.concatenate(dxdt_cols, axis=1)
        dxbc_ref[:, 0:D_SSD] = dxdt * dt_c + dyr * dsk_ref[...]
        dww = _dot3(jnp.concatenate(dww_cols, axis=1), fold_ref[...])
        dcs = (_dot3(dyrb.astype(F32) * (yraw - x * dsk_ref[...]), fold_ref[...])
               - _dot3(xdt.astype(BF16).astype(F32) * jnp.concatenate(diag_all, axis=1), fold_ref[...]) - dww)
        rowid = lax.broadcasted_iota(jnp.int32, (CHUNK, 128), 0)
        dcs = dcs + jnp.where(rowid == CHUNK - 1, jnp.sum(dww, axis=0, keepdims=True) + dlast, 0.0)
        dadt = _dot3_left(tri, dcs, TN)
        ddt = _dot3(dxdt * x, fold_ref[...]) + dadt * a
        da = jnp.sum(dadt * dt, axis=0, keepdims=True)
        ddt_raw = ddt * _sigmoid(dt_ref[...] + par_ref[0:1])
        ddt_raw = jnp.where(lane < N_HEADS, ddt_raw, 0.0)
        ddt_ref[...] = ddt_raw
        dpar_ref[0:1, :] += jnp.sum(ddt_raw, axis=0, keepdims=True)
        dpar_ref[1:2, :] += jnp.where(lane1 < N_HEADS, da * a, 0.0)

    row = lambda w: pl.BlockSpec((CHUNK, w), lambda i: (blk(i), 0))
    full = lambda s: pl.BlockSpec(s, lambda i: (0,) * len(s))
    return pl.pallas_call(
        body, name="ssd_bwd", grid=(n_chunks,),
        out_shape=(jax.ShapeDtypeStruct((t, D_XBC), F32), jax.ShapeDtypeStruct((t, D_SSD), F32),
                   jax.ShapeDtypeStruct((t, DT_PAD), F32), jax.ShapeDtypeStruct((8, 128), F32),
                   jax.ShapeDtypeStruct((8, D_SSD), F32)),
        in_specs=[row(D_SSD), row(D_SSD), row(D_SSD), row(D_XBC), row(DT_PAD),
                  pl.BlockSpec((1, D_SSD, N_STATE), lambda i: (blk(i), 0, 0)),
                  full((8, 128)), full((1, D_SSD)), full((1, D_SSD)), full(expand.shape), full(fold.shape),
                  full(sel.shape)],
        out_specs=(row(D_XBC), row(D_SSD), row(DT_PAD), full((8, 128)), full((8, D_SSD))),
        scratch_shapes=[pltpu.VMEM((D_SSD, N_STATE), F32), pltpu.VMEM((CHUNK, D_SSD), F32)],
        compiler_params=_params(48),
    )(dyssd, yraw, z, xbc, dt_raw, hprev, par, dsk, normw, expand, fold, sel)


def _conv_bwd(dxbc, xbc_pre, conv_w, conv_b, seq):
    t = xbc_pre.shape[0]
    tb = 256
    npb = seq // tb
    cw = 512

    def body(d_ref, cur_ref, halo_ref, w_ref, b_ref, o_ref, acc_ref):
        i = pl.program_id(1)

        @pl.when(i == 0)
        def _():
            acc_ref[...] = jnp.zeros_like(acc_ref)

        first = (i % npb) == 0
        halo = jnp.where(first, 0.0, halo_ref[...])
        pre, shifted = _conv_taps(cur_ref[...], halo, w_ref[...])
        pre = pre + b_ref[...]
        sg = _sigmoid(pre)
        dpre = d_ref[...] * (sg * (1.0 + pre * (1.0 - sg)))
        o_ref[...] = dpre
        for j in range(4):
            acc_ref[3 - j:4 - j, :] += jnp.sum(dpre * shifted[j], axis=0, keepdims=True)
        acc_ref[4:5, :] += jnp.sum(dpre, axis=0, keepdims=True)

    return pl.pallas_call(
        body, name="conv_bwd", grid=(D_XBC // cw, t // tb),
        out_shape=(jax.ShapeDtypeStruct((t, D_XBC), F32), jax.ShapeDtypeStruct((8, D_XBC), F32)),
        in_specs=[pl.BlockSpec((tb, cw), lambda j, i: (i, j)), pl.BlockSpec((tb, cw), lambda j, i: (i, j)),
                  pl.BlockSpec((8, cw), lambda j, i: (jnp.maximum(i * (tb // 8) - 1, 0), j)),
                  pl.BlockSpec((4, cw), lambda j, i: (0, j)), pl.BlockSpec((1, cw), lambda j, i: (0, j))],
        out_specs=(pl.BlockSpec((tb, cw), lambda j, i: (i, j)), pl.BlockSpec((8, cw), lambda j, i: (0, j))),
        compiler_params=_params(32),
    )(dxbc, xbc_pre, xbc_pre, conv_w, conv_b)


def _proj_bwd(dz, dpre, ddt, du5, x2, dxa, mod3, conv_w, w_in_pad, seq):
    t = x2.shape[0]
    tb = 512
    npb = seq // tb
    n_blocks = t // tb

    def body(dz_ref, dp_ref, nxt_ref, ddt_ref, du5_ref, x_ref, dxa_ref, mod_ref, cw_ref, w_hbm,
             gx_ref, u_ref, dxp_ref, bacc_ref, w_vmem, sem):
        i = pl.program_id(0)
        _load_once(w_hbm, w_vmem, sem)

        @pl.when(i % npb == 0)
        def _():
            bacc_ref[...] = jnp.zeros_like(bacc_ref)

        last = (i % npb) == npb - 1
        nxt = jnp.where(last, 0.0, nxt_ref[...])
        cur = dp_ref[...]
        xx = jnp.concatenate([cur, nxt], axis=0)
        w = cw_ref[...]
        dxp = w[3:4] * cur
        for j in (1, 2, 3):
            dxp = dxp + w[3 - j:4 - j] * pltpu.roll(xx, tb + 8 - j, axis=0)[0:tb]
        dxpb = dxp.astype(BF16)
        dxp_ref[...] = dxpb
        o1, o2, o3 = D_SSD, D_SSD + D_XBC, D_SSD + D_XBC + DT_PAD
        du = (lax.dot_general(dz_ref[...].astype(BF16), w_vmem[:, 0:o1], NT, preferred_element_type=F32)
              + lax.dot_general(dxpb, w_vmem[:, o1:o2], NT, preferred_element_type=F32)
              + lax.dot_general(ddt_ref[...].astype(BF16), w_vmem[:, o2:o3], NT, preferred_element_type=F32)
              + lax.dot_general(du5_ref[...].astype(BF16), w_vmem[:, o3:], NT, preferred_element_type=F32))
        m = mod_ref[0]
        xv = x_ref[...]
        u_ref[...] = (xv * (1.0 + m[1:2]) + m[0:1]).astype(BF16)
        gx_ref[...] = dxa_ref[...] + du * (1.0 + m[1:2])
        bacc_ref[0, 0:1, :] += jnp.sum(du, axis=0, keepdims=True)
        bacc_ref[0, 1:2, :] += jnp.sum(du * xv, axis=0, keepdims=True)

    row = lambda w: pl.BlockSpec((tb, w), lambda i: (i, 0))
    nxt_rows = pl.BlockSpec((8, D_XBC), lambda i: (jnp.minimum((i + 1) * (tb // 8), t // 8 - 1), 0))
    return pl.pallas_call(
        body, name="proj_bwd", grid=(n_blocks,),
        out_shape=(jax.ShapeDtypeStruct((t, D_MODEL), F32), jax.ShapeDtypeStruct((t, D_MODEL), BF16),
                   jax.ShapeDtypeStruct((t, D_XBC), BF16), jax.ShapeDtypeStruct((t // seq, 8, D_MODEL), F32)),
        in_specs=[row(D_SSD), row(D_XBC), nxt_rows, row(DT_PAD), row(D_S5), row(D_MODEL), row(D_MODEL),
                  pl.BlockSpec((1, N_MOD, D_MODEL), lambda i: (i // npb, 0, 0)),
                  pl.BlockSpec((4, D_XBC), lambda i: (0, 0)), ANY],
        out_specs=(row(D_MODEL), row(D_MODEL), row(D_XBC), pl.BlockSpec((1, 8, D_MODEL), lambda i: (i // npb, 0, 0))),
        scratch_shapes=[pltpu.VMEM((D_MODEL, D_INP), BF16), pltpu.SemaphoreType.DMA],
        compiler_params=_params(60),
    )(dz, dpre, dpre, ddt, du5, x2, dxa, mod3, conv_w, w_in_pad)


def _pad_rows(a, mult):
    r = a.shape[0]
    pad = (-r) % mult
    return a if pad == 0 else jnp.concatenate([a, jnp.zeros((pad,) + a.shape[1:], a.dtype)], axis=0)


_SMALL = ["conv_w", "conv_b", "dt_bias", "a_log", "d_ssd", "norm_w", "s5_a_re", "s5_a_im", "s5_log_dt", "s5_b_re",
          "s5_b_im", "s5_c_re", "s5_c_im", "s5_d", "b_glu", "ln1_g", "ln1_b", "b1", "b2", "ln2_g", "ln2_b"]


def _pack_small(d):
    flat = jnp.concatenate([d[n].reshape(-1).astype(F32) for n in _SMALL])
    pad = (-flat.shape[0]) % (128 * 256)
    return jnp.concatenate([flat, jnp.zeros((pad,), F32)]).reshape(-1, 128)


def _unpack_small(p, shapes):
    flat = p.reshape(-1)
    out, off = {}, 0
    for n in _SMALL:
        size = math.prod(shapes[n])
        out[n] = flat[off:off + size].reshape(shapes[n])
        off += size
    return out


def kernel(x, c, w_ada, b_ada, w_in, conv_w, conv_b, dt_bias, a_log, d_ssd, norm_w, s5_a_re, s5_a_im, s5_log_dt, s5_b_re, s5_b_im, s5_c_re, s5_c_im, s5_d, w_glu, b_glu, w_out, ln1_g, ln1_b, w1, b1, w2, b2, ln2_g, ln2_b, loss_target, m_w_ada, m_b_ada, m_w_in, m_conv_w, m_conv_b, m_dt_bias, m_a_log, m_d_ssd, m_norm_w, m_s5_a_re, m_s5_a_im, m_s5_log_dt, m_s5_b_re, m_s5_b_im, m_s5_c_re, m_s5_c_im, m_s5_d, m_w_glu, m_b_glu, m_w_out, m_ln1_g, m_ln1_b, m_w1, m_b1, m_w2, m_b2, m_ln2_g, m_ln2_b, v_w_ada, v_b_ada, v_w_in, v_conv_w, v_conv_b, v_dt_bias, v_a_log, v_d_ssd, v_norm_w, v_s5_a_re, v_s5_a_im, v_s5_log_dt, v_s5_b_re, v_s5_b_im, v_s5_c_re, v_s5_c_im, v_s5_d, v_w_glu, v_b_glu, v_w_out, v_ln1_g, v_ln1_b, v_w1, v_b1, v_w2, v_b2, v_ln2_g, v_ln2_b):
    weights = dict(w_ada=w_ada, b_ada=b_ada, w_in=w_in, conv_w=conv_w, conv_b=conv_b, dt_bias=dt_bias, a_log=a_log,
                   d_ssd=d_ssd, norm_w=norm_w, s5_a_re=s5_a_re, s5_a_im=s5_a_im, s5_log_dt=s5_log_dt, s5_b_re=s5_b_re,
                   s5_b_im=s5_b_im, s5_c_re=s5_c_re, s5_c_im=s5_c_im, s5_d=s5_d, w_glu=w_glu, b_glu=b_glu, w_out=w_out,
                   ln1_g=ln1_g, ln1_b=ln1_b, w1=w1, b1=b1, w2=w2, b2=b2, ln2_g=ln2_g, ln2_b=ln2_b)
    mom = dict(w_ada=m_w_ada, b_ada=m_b_ada, w_in=m_w_in, conv_w=m_conv_w, conv_b=m_conv_b, dt_bias=m_dt_bias,
               a_log=m_a_log, d_ssd=m_d_ssd, norm_w=m_norm_w, s5_a_re=m_s5_a_re, s5_a_im=m_s5_a_im,
               s5_log_dt=m_s5_log_dt, s5_b_re=m_s5_b_re, s5_b_im=m_s5_b_im, s5_c_re=m_s5_c_re, s5_c_im=m_s5_c_im,
               s5_d=m_s5_d, w_glu=m_w_glu, b_glu=m_b_glu, w_out=m_w_out, ln1_g=m_ln1_g, ln1_b=m_ln1_b, w1=m_w1, b1=m_b1,
               w2=m_w2, b2=m_b2, ln2_g=m_ln2_g, ln2_b=m_ln2_b)
    var = dict(w_ada=v_w_ada, b_ada=v_b_ada, w_in=v_w_in, conv_w=v_conv_w, conv_b=v_conv_b, dt_bias=v_dt_bias,
               a_log=v_a_log, d_ssd=v_d_ssd, norm_w=v_norm_w, s5_a_re=v_s5_a_re, s5_a_im=v_s5_a_im,
               s5_log_dt=v_s5_log_dt, s5_b_re=v_s5_b_re, s5_b_im=v_s5_b_im, s5_c_re=v_s5_c_re, s5_c_im=v_s5_c_im,
               s5_d=v_s5_d, w_glu=v_w_glu, b_glu=v_b_glu, w_out=v_w_out, ln1_g=v_ln1_g, ln1_b=v_ln1_b, w1=v_w1, b1=v_b1,
               w2=v_w2, b2=v_b2, ln2_g=v_ln2_g, ln2_b=v_ln2_b)
    names = list(weights)
    shapes = {n: weights[n].shape for n in names}

    nb, seq, _ = x.shape
    t = nb * seq
    dev = _dev_index()
    x2 = x.reshape(t, D_MODEL)
    tgt2 = loss_target.reshape(t, D_MODEL)

    cw_cols = conv_w.shape[2]
    small_in = jnp.concatenate([c.reshape(-1), conv_w.reshape(-1)]).reshape(-1, 128)
    small_all = _all_gather([small_in], "gather_c_conv")[0].reshape(N_DEV, -1)
    c_all = small_all[:, :nb * D_MODEL].reshape(N_DEV * nb, D_MODEL)
    conv_w_full = small_all[:, nb * D_MODEL:].reshape(N_DEV, 4, cw_cols).transpose(1, 0, 2).reshape(4, D_XBC)

    big_names = ["w_in", "w_out", "w1", "w2", "w_glu"]
    gathered = dict(zip(big_names, _all_gather([weights[n][0].astype(BF16) for n in big_names], "gather_weights")))
    w_in_f = gathered["w_in"].transpose(1, 0, 2).reshape(D_MODEL, D_IN)
    w_in_pad = jnp.concatenate(
        [w_in_f[:, :D_SSD + D_XBC], w_in_f[:, D_SSD + D_XBC:D_SSD + D_XBC + N_HEADS],
         jnp.zeros((D_MODEL, DT_PAD - N_HEADS), BF16), w_in_f[:, D_SSD + D_XBC + N_HEADS:]], axis=1)
    w_out_f = gathered["w_out"].reshape(2 * D_MODEL, D_MODEL)
    w1_blocks = gathered["w1"]
    w2_f = gathered["w2"].reshape(D_FF, D_MODEL)
    w_glu_f = gathered["w_glu"].reshape(D_S5, D_S5)

    ada_cols = w_ada.shape[2]
    b_cols = lax.dynamic_slice_in_dim(b_ada, dev * ada_cols, ada_cols, axis=1)
    mod_cols = _mod_fwd(c_all, w_ada[0], b_cols)
    mod_all = _all_gather([mod_cols], "gather_mod")[0]
    mod_mine = lax.dynamic_slice_in_dim(mod_all, dev * nb, nb, axis=1)
    mod3 = mod_mine.transpose(1, 0, 2).reshape(nb, N_MOD, D_MODEL)

    def pad_lanes(v, n):
        return jnp.concatenate([v, jnp.zeros((v.shape[0], n - v.shape[1]), F32)], axis=1)

    par = _pad_rows(jnp.concatenate([pad_lanes(dt_bias, 128), pad_lanes(a_log, 128)], axis=0), 8)
    dsk = jnp.repeat(d_ssd[0], HEADDIM).reshape(1, D_SSD)
    ar = s5_a_re.reshape(1, S5_N)
    ai = s5_a_im.reshape(1, S5_N)
    ldt = jnp.repeat(s5_log_dt[0], S5_P).reshape(1, S5_N)
    br_t = s5_b_re[0].transpose(2, 0, 1).reshape(S5_CH, S5_N)
    bi_t = s5_b_im[0].transpose(2, 0, 1).reshape(S5_CH, S5_N)
    bb_re_t, bb_im_t, pf_re, pf_im, pr_re, pr_im = _s5_params_fwd(ar, ai, ldt, br_t, bi_t)
    mask_b = (jnp.arange(D_S5)[:, None] // S5_CH) == (jnp.arange(S5_N)[None, :] // S5_P)

    def dense_b(bt_):
        return jnp.where(mask_b, jnp.tile(bt_, (S5_GROUPS, 1)), 0.0).astype(BF16)

    def dense_c(cc):
        ct = cc[0].transpose(0, 2, 1).reshape(S5_N, S5_CH)
        return jnp.where(mask_b.T, jnp.tile(ct, (1, S5_GROUPS)), 0.0).astype(BF16)

    bb_re, bb_im = dense_b(bb_re_t), dense_b(bb_im_t)
    cc_re, cc_im = dense_c(s5_c_re), dense_c(s5_c_im)
    s5d = s5_d.reshape(1, D_S5)
    ln1 = jnp.concatenate([ln1_g, ln1_b], axis=0)
    vec1 = _pad_rows(jnp.concatenate([b2, ln2_g, ln2_b], axis=0), 8)

    z, xbc_pre, dt_raw, u5 = _proj_fwd(x2, mod3, w_in_pad, seq)
    xbc = _conv_fwd(xbc_pre, conv_w_full, conv_b, seq)
    yraw, ycat, hprev = _ssd_fwd(xbc, z, dt_raw, par, dsk, norm_w, seq)
    s_re, s_im, ypre, ycat = _s5_fwd(u5, bb_re, bb_im, cc_re, cc_im, pf_re, pf_im, s5d, w_glu_f, b_glu, ycat, seq)
    mix, x1 = _out_ln1(ycat, x2, mod3, w_out_f, ln1, seq)

    dx1, u2b, hb, dhpb, dob, gacc2, db1, bacc2 = _mlp_fwd_bwd(x1, tgt2, mod3, w1_blocks, w2_f, vec1, b1, seq)
    loss = lax.psum(0.5 / D_MODEL * jnp.sum(gacc2[3]), ("x", "y", "c"))

    dmixb, dxa, dyssd, dy5, gacc1, bacc1 = _ln1_out_bwd(dx1, x2, mix, mod3, w_out_f, ln1, seq)
    du5, g_re, g_im, ygb, dqb, dypb, vacc, sacc = _s5_bwd(dy5, ypre, u5, s_re, s_im, bb_re, bb_im, cc_re, cc_im,
                                                          pr_re, pr_im, s5d, w_glu_f, b_glu, seq)
    dxbc, dz, ddt, dpar, cacc = _ssd_bwd(dyssd, yraw, z, xbc, dt_raw, hprev, par, dsk, norm_w, seq)
    dpre, conv_acc = _conv_bwd(dxbc, xbc_pre, conv_w_full, conv_b, seq)
    grad_x2, ub, dxpb, bacc0 = _proj_bwd(dz, dpre, ddt, du5, x2, dxa, mod3, conv_w_full, w_in_pad, seq)

    g_w2 = _atb(hb, dob, "gw2")
    g_w1 = _atb(u2b, dhpb, "gw1")
    g_wout = _atb(ycat, dmixb, "gwout")
    g_win = jnp.concatenate([_atb(ub, dz, "gwin_z"), _atb(ub, dxpb, "gwin_xbc"),
                             _atb(ub, ddt, "gwin_dt")[:, :N_HEADS], _atb(ub, du5, "gwin_s5")], axis=1)
    g_wglu = _atb(ygb, dqb, "gwglu")
    d_cc_re = _atb(s_re, dypb, "gcc_re")
    d_cc_im = -_atb(s_im, dypb, "gcc_im")
    d_bb_re = _atb(u5, g_re, "gbb_re")
    d_bb_im = _atb(u5, g_im, "gbb_im")

    def diag_b(dd):
        return jnp.where(mask_b, dd, 0.0).reshape(S5_GROUPS, S5_CH, S5_N).sum(0)

    def diag_c(dd):
        return jnp.where(mask_b.T, dd, 0.0).reshape(S5_N, S5_GROUPS, S5_CH).sum(1).reshape(S5_GROUPS, S5_P, S5_CH).transpose(0, 2, 1)

    g_ar, g_ai, g_ldt, g_br_t, g_bi_t = _s5_params_bwd(ar, ai, ldt, br_t, bi_t, vacc[0:1], vacc[1:2],
                                                      diag_b(d_bb_re), diag_b(d_bb_im))

    def from_t(gt):
        return gt.reshape(S5_CH, S5_GROUPS, S5_P).transpose(1, 2, 0)

    small_g = dict(
        conv_w=conv_acc[0:4], conv_b=conv_acc[4:5], dt_bias=dpar[0:1, :N_HEADS], a_log=dpar[1:2, :N_HEADS],
        d_ssd=cacc[0].reshape(N_HEADS, HEADDIM).sum(1), norm_w=cacc[1:2],
        s5_a_re=g_ar, s5_a_im=g_ai, s5_log_dt=g_ldt[:, :S5_GROUPS], s5_b_re=from_t(g_br_t), s5_b_im=from_t(g_bi_t),
        s5_c_re=diag_c(d_cc_re), s5_c_im=diag_c(d_cc_im), s5_d=sacc[0:1], b_glu=sacc[1:2],
        ln1_g=gacc1[0:1], ln1_b=gacc1[1:2], b1=db1, b2=gacc2[2:3], ln2_g=gacc2[0:1], ln2_b=gacc2[1:2])

    dmod = jnp.concatenate([bacc0[:, 0], bacc0[:, 1], bacc1[:, 0], bacc2[:, 0], bacc2[:, 1], bacc2[:, 2]], axis=1)
    dmod_all = _all_gather([dmod], "gather_dmod")[0].reshape(N_DEV * nb, N_MOD * D_MODEL)
    dmod_cols = lax.dynamic_slice_in_dim(dmod_all, dev * ada_cols, ada_cols, axis=1)
    g_wada, g_bada = _mod_bwd(c_all, dmod_cols, dmod_all)

    in_cols = w_in.shape[2]
    big_g = dict(
        w_in=g_win.reshape(D_MODEL, N_DEV, in_cols).transpose(1, 0, 2),
        w_out=g_wout.reshape((N_DEV,) + w_out.shape[1:]), w1=g_w1,
        w2=g_w2.reshape((N_DEV,) + w2.shape[1:]), w_glu=g_wglu.reshape((N_DEV,) + w_glu.shape[1:]))
    by_dest = [big_g[n] if n == "w1" else big_g[n].reshape((4, 2) + big_g[n].shape[1:]) for n in big_names]
    from_sibling = _sibling_swap(by_dest, "rs_sibling_swap")
    core = lax.axis_index("c").astype(jnp.int32).reshape(1)
    chip_sums = [_add_halves(g, r, core, "rs_add_" + n) for g, r, n in zip(by_dest, from_sibling, big_names)]
    parts = _chip_all_to_all(chip_sums, "rs_chip_all_to_all")

    res = {k: {} for k in "gdmv"}
    for n, p in zip(big_names, parts):
        outs = _adamw(p, weights[n][0], mom[n][0], var[n][0], "adamw_" + n)
        for k, a in zip("gdmv", outs):
            res[k][n] = a[None]

    ag, ad, am, av = _adamw(g_wada[None], w_ada[0], m_w_ada[0], v_w_ada[0], "adamw_w_ada")
    for k, a in (("g", ag), ("d", ad), ("m", am), ("v", av)):
        res[k]["w_ada"] = a[None]
    bg_, bd_, bm_, bv_ = _adamw(g_bada.reshape(1, -1, 128), b_ada.reshape(-1, 128), m_b_ada.reshape(-1, 128),
                                v_b_ada.reshape(-1, 128), "adamw_b_ada")
    for k, a in (("g", bg_), ("d", bd_), ("m", bm_), ("v", bv_)):
        res[k]["b_ada"] = a.reshape(shapes["b_ada"])

    small_shapes = dict(shapes)
    small_shapes["conv_w"] = (1, 4, D_XBC)
    small_parts = _all_gather([_pack_small(small_g)], "gather_small_grads")[0]
    rep = {n: (jnp.zeros((1, 4, D_XBC), F32) if n == "conv_w" else weights[n]) for n in _SMALL}
    rep_m = {n: (jnp.zeros((1, 4, D_XBC), F32) if n == "conv_w" else mom[n]) for n in _SMALL}
    rep_v = {n: (jnp.ones((1, 4, D_XBC), F32) if n == "conv_w" else var[n]) for n in _SMALL}
    sg_, sd_, sm_, sv_ = _adamw(small_parts, _pack_small(rep), _pack_small(rep_m), _pack_small(rep_v), "adamw_small")
    for k, p in (("g", sg_), ("d", sd_), ("m", sm_), ("v", sv_)):
        un = _unpack_small(p, small_shapes)
        for n in _SMALL:
            if n != "conv_w":
                res[k][n] = un[n]
    g_conv_full = _unpack_small(sg_, small_shapes)["conv_w"][0]
    g_conv_mine = lax.dynamic_slice_in_dim(g_conv_full, dev * cw_cols, cw_cols, axis=1)
    cg_, cd_, cm_, cv_ = _adamw(g_conv_mine[None], conv_w[0], m_conv_w[0], v_conv_w[0], "adamw_conv_w")
    for k, a in (("g", cg_), ("d", cd_), ("m", cm_), ("v", cv_)):
        res[k]["conv_w"] = a[None]

    grad_x = grad_x2.reshape(nb, seq, D_MODEL)
    return (loss, grad_x, *[res["g"][n] for n in names], *[res["d"][n] for n in names],
            *[res["m"][n] for n in names], *[res["v"][n] for n in names])
```

```python
import functools
import math

import jax
import jax.numpy as jnp
from jax import lax
from jax.experimental import pallas as pl
from jax.experimental.pallas import tpu as pltpu

F32, BF16 = jnp.float32, jnp.bfloat16
MESH = pl.DeviceIdType.MESH
N_DEV = 8

D_MODEL = 1024
D_SSD = 1536
N_HEADS = 24
HEADDIM = 64
N_GROUPS = 4
HPG = 6
GW = HPG * HEADDIM
N_STATE = 128
CHUNK = 128
D_XBC = 2560
D_S5 = 512
S5_GROUPS = 32
S5_CH = 16
S5_P = 64
S5_N = S5_GROUPS * S5_P
D_IN = 4632
DT_PAD = 128
D_INP = D_SSD + D_XBC + DT_PAD + D_S5
D_FF = 4096
N_MOD = 6
ALPHA = 2.0 ** 0.25
EPS = 1e-5
LR, B1, B2, AEPS, WD, STEP = 0.001, 0.9, 0.999, 1e-08, 0.01, 10

NT = (((1,), (1,)), ((), ()))
TN = (((0,), (0,)), ((), ()))
ANY = pl.BlockSpec(memory_space=pl.ANY)
HIGHEST = lax.Precision.HIGHEST


def _mm(a, b):
    return jnp.dot(a.astype(BF16), b.astype(BF16), preferred_element_type=F32)


def _mm_nt(a, b):
    return lax.dot_general(a.astype(BF16), b.astype(BF16), NT, preferred_element_type=F32)


def _mm_tn(a, b):
    return lax.dot_general(a.astype(BF16), b.astype(BF16), TN, preferred_element_type=F32)


def _row_block(r, cap):
    best = r
    for cand in range(8, min(r, cap) + 1, 8):
        if r % cand == 0:
            best = cand
    return best if best <= cap else r


def _params(vmem_mb):
    return pltpu.CompilerParams(vmem_limit_bytes=vmem_mb << 20)


def _sigmoid(x):
    return 1.0 / (1.0 + jnp.exp(-x))


def _softplus(x):
    return jnp.maximum(x, 0.0) + jnp.log(1.0 + jnp.exp(-jnp.abs(x)))


_GK = math.sqrt(2.0 / math.pi)


def _gelu(x):
    return 0.5 * x * (1.0 + jnp.tanh(_GK * (x + 0.044715 * x * x * x)))


def _gelu_grad(x):
    t = jnp.tanh(_GK * (x + 0.044715 * x * x * x))
    return 0.5 * (1.0 + t) + 0.5 * x * (1.0 - t * t) * _GK * (1.0 + 3.0 * 0.044715 * x * x)


def _dev_index():
    return 4 * lax.axis_index("x") + 2 * lax.axis_index("y") + lax.axis_index("c")


def _all_gather(xs, name):
    n = len(xs)

    def body(*refs):
        x_refs, out_refs = refs[:n], refs[n:2 * n]
        send_sems, recv_sems, local_sems = refs[2 * n:]
        ix, iy, ic = lax.axis_index("x"), lax.axis_index("y"), lax.axis_index("c")
        me, sibling = (ix, iy, ic), (ix, iy, 1 - ic)
        chips = [(1 - ix, iy), (ix, 1 - iy), (1 - ix, 1 - iy)]

        def slot(a, px, py, pc):
            return out_refs[a].at[4 * px + 2 * py + pc]

        def copy(a, k, block, to, src=None):
            return pltpu.make_async_remote_copy(
                src_ref=slot(a, *block) if src is None else src, dst_ref=slot(a, *block),
                send_sem=send_sems.at[7 * a + k], recv_sem=recv_sems.at[7 * a + k], device_id=to, device_id_type=MESH)

        mine = [pltpu.make_async_copy(x_refs[a], slot(a, *me), local_sems.at[a]) for a in range(n)]
        for cp in mine:
            cp.start()
        first = []
        for j, chip in enumerate(chips):
            first += [copy(a, 1 + j, me, (*chip, ic), src=x_refs[a]) for a in range(n)]
        first += [copy(a, 0, me, sibling, src=x_refs[a]) for a in range(n)]
        for cp in first:
            cp.start()
        passed = []
        for j, chip in enumerate(chips):
            for a in range(n):
                copy(a, 1 + j, (*chip, ic), me).wait_recv()
                cp = copy(a, 4 + j, (*chip, ic), sibling)
                cp.start()
                passed.append(cp)
        for a in range(n):
            copy(a, 0, sibling, me).wait_recv()
            for j, chip in enumerate(chips):
                copy(a, 4 + j, (*chip, 1 - ic), me).wait_recv()
        for cp in first + passed:
            cp.wait_send()
        for cp in mine:
            cp.wait()

    return pl.pallas_call(
        body, name=name, out_shape=tuple(jax.ShapeDtypeStruct((N_DEV,) + x.shape, x.dtype) for x in xs),
        in_specs=[ANY] * n, out_specs=tuple([ANY] * n),
        scratch_shapes=[pltpu.SemaphoreType.DMA((7 * n,)), pltpu.SemaphoreType.DMA((7 * n,)),
                        pltpu.SemaphoreType.DMA((n,))],
    )(*xs)


def _sibling_swap(gs, name):
    n = len(gs)

    def body(*refs):
        g_refs, recv_refs = refs[:n], refs[n:2 * n]
        send_sems, recv_sems = refs[2 * n:]
        ix, iy, ic = lax.axis_index("x"), lax.axis_index("y"), lax.axis_index("c")

        def block(g_ref, q):
            if len(g_ref.shape) == 4:
                return g_ref.at[q, 1 - ic]
            cw = g_ref.shape[1] // N_DEV
            return g_ref.at[:, pl.ds(pl.multiple_of((2 * q + 1 - ic) * cw, 128), cw)]

        cps = []
        for a in range(n):
            for q in range(4):
                cps.append(pltpu.make_async_remote_copy(
                    src_ref=block(g_refs[a], q), dst_ref=recv_refs[a].at[q],
                    send_sem=send_sems.at[4 * a + q], recv_sem=recv_sems.at[4 * a + q],
                    device_id=(ix, iy, 1 - ic), device_id_type=MESH))
        for cp in cps:
            cp.start()
        for cp in cps:
            cp.wait()

    return pl.pallas_call(
        body, name=name,
        out_shape=tuple(jax.ShapeDtypeStruct(
            (4,) + (g.shape[2:] if g.ndim == 4 else (g.shape[0], g.shape[1] // N_DEV)), g.dtype) for g in gs),
        in_specs=[ANY] * n, out_specs=tuple([ANY] * n),
        scratch_shapes=[pltpu.SemaphoreType.DMA((4 * n,)), pltpu.SemaphoreType.DMA((4 * n,))],
    )(*gs)


def _chip_all_to_all(hs, name):
    n = len(hs)

    def body(*refs):
        h_refs, out_refs = refs[:n], refs[n:2 * n]
        send_sems, recv_sems, local_sems = refs[2 * n:]
        ix, iy, ic = lax.axis_index("x"), lax.axis_index("y"), lax.axis_index("c")
        me = 2 * ix + iy
        peers = [(1 - ix, iy), (ix, 1 - iy), (1 - ix, 1 - iy)]
        mine = [pltpu.make_async_copy(h_refs[a].at[me], out_refs[a].at[me], local_sems.at[a]) for a in range(n)]
        for cp in mine:
            cp.start()

        def copy(a, k, src_slot, dst_slot, peer):
            return pltpu.make_async_remote_copy(
                src_ref=h_refs[a].at[src_slot], dst_ref=out_refs[a].at[dst_slot],
                send_sem=send_sems.at[3 * a + k], recv_sem=recv_sems.at[3 * a + k],
                device_id=(*peer, ic), device_id_type=MESH)

        sends = [copy(a, k, 2 * px + py, me, (px, py)) for a in range(n) for k, (px, py) in enumerate(peers)]
        for cp in sends:
            cp.start()
        for a in range(n):
            for k, (px, py) in enumerate(peers):
                copy(a, k, 2 * px + py, 2 * px + py, (px, py)).wait_recv()
        for cp in sends:
            cp.wait_send()
        for cp in mine:
            cp.wait()

    return pl.pallas_call(
        body, name=name, out_shape=tuple(jax.ShapeDtypeStruct(h.shape, h.dtype) for h in hs),
        in_specs=[ANY] * n, out_specs=tuple([ANY] * n),
        scratch_shapes=[pltpu.SemaphoreType.DMA((3 * n,)), pltpu.SemaphoreType.DMA((3 * n,)),
                        pltpu.SemaphoreType.DMA((n,))],
    )(*hs)


def _add_halves(g, recv, core, name):
    _, r, c = recv.shape
    br = _row_block(r, 512)
    stacked = g.ndim == 4

    def body(core_ref, g_ref, r_ref, o_ref):
        o_ref[0] = ((g_ref[0, 0] if stacked else g_ref[...]) + r_ref[0]).astype(BF16)

    spec = pl.BlockSpec((1, br, c), lambda i, j, core_ref: (i, j, 0))
    if stacked:
        g_spec = pl.BlockSpec((1, 1, br, c), lambda i, j, core_ref: (i, core_ref[0], j, 0))
    else:
        g_spec = pl.BlockSpec((br, c), lambda i, j, core_ref: (j, 2 * i + core_ref[0]))
    return pl.pallas_call(
        body, name=name, out_shape=jax.ShapeDtypeStruct(recv.shape, BF16),
        grid_spec=pltpu.PrefetchScalarGridSpec(
            num_scalar_prefetch=1, grid=(4, r // br), in_specs=[g_spec, spec], out_specs=spec),
        compiler_params=_params(32),
    )(core, g, recv)


def _adamw(parts, w, m, v, name):
    n_parts, r, c = parts.shape
    br = _row_block(r, 512 if c <= 1024 else 256)

    def body(p_ref, w_ref, m_ref, v_ref, g_out, d_out, m_out, v_out):
        g = p_ref[0].astype(F32)
        for p in range(1, n_parts):
            g = g + p_ref[p].astype(F32)
        m2 = B1 * m_ref[...] + (1.0 - B1) * g
        v2 = B2 * v_ref[...] + (1.0 - B2) * (g * g)
        m_hat = m2 / (1.0 - B1 ** STEP)
        v_hat = v2 / (1.0 - B2 ** STEP)
        g_out[...] = g
        d_out[...] = -LR * (m_hat / (jnp.sqrt(v_hat) + AEPS) + WD * w_ref[...])
        m_out[...] = m2
        v_out[...] = v2

    spec = pl.BlockSpec((br, c), lambda i: (i, 0))
    out = jax.ShapeDtypeStruct((r, c), F32)
    return pl.pallas_call(
        body, name=name, out_shape=(out, out, out, out), grid=(r // br,),
        in_specs=[pl.BlockSpec((n_parts, br, c), lambda i: (0, i, 0)), spec, spec, spec],
        out_specs=(spec, spec, spec, spec), compiler_params=_params(40),
    )(parts, w, m, v)


def _atb(a, b, name, bt=512):
    t, k1 = a.shape
    k2 = b.shape[1]

    def pick(k):
        for cand in (1024, 768, 512, 384, 256, 128):
            if k % cand == 0:
                return cand
        return k

    b1, b2 = pick(k1), pick(k2)

    def body(a_ref, b_ref, o_ref):
        @pl.when(pl.program_id(2) == 0)
        def _():
            o_ref[...] = jnp.zeros_like(o_ref)
        o_ref[...] += _mm_tn(a_ref[...], b_ref[...])

    return pl.pallas_call(
        body, name=name, out_shape=jax.ShapeDtypeStruct((k1, k2), F32), grid=(k1 // b1, k2 // b2, t // bt),
        in_specs=[pl.BlockSpec((bt, b1), lambda i, j, k: (k, i)), pl.BlockSpec((bt, b2), lambda i, j, k: (k, j))],
        out_specs=pl.BlockSpec((b1, b2), lambda i, j, k: (i, j)), compiler_params=_params(40),
    )(a, b)


def _mod_fwd(c_all, w_ada, b_cols):
    def body(c_ref, w_ref, b_ref, o_ref):
        cc = c_ref[...]
        cond = cc * _sigmoid(cc)
        o_ref[...] = _mm(cond, w_ref[...]) + b_ref[...]

    return pl.pallas_call(body, name="mod_fwd", out_shape=jax.ShapeDtypeStruct((c_all.shape[0], w_ada.shape[1]), F32),
                          compiler_params=_params(32))(c_all, w_ada, b_cols)


def _mod_bwd(c_all, dmod_cols, dmod_all):
    def body(c_ref, dc_ref, da_ref, gw_ref, gb_ref):
        cc = c_ref[...]
        cond = cc * _sigmoid(cc)
        gw_ref[...] = _mm_tn(cond, dc_ref[...])
        gb_ref[...] = jnp.sum(da_ref[...], axis=0, keepdims=True)

    return pl.pallas_call(
        body, name="mod_bwd",
        out_shape=(jax.ShapeDtypeStruct((D_MODEL, dmod_cols.shape[1]), F32), jax.ShapeDtypeStruct((1, dmod_all.shape[1]), F32)),
        compiler_params=_params(32))(c_all, dmod_cols, dmod_all)


def _load_once(hbm_ref, vmem_ref, sem):
    @pl.when(pl.program_id(0) == 0)
    def _():
        cp = pltpu.make_async_copy(hbm_ref, vmem_ref, sem)
        cp.start()
        cp.wait()


def _conv_taps(win_ref, w, tb, cols):
    shifted = [win_ref[8 - j:8 - j + tb, cols] for j in range(4)]
    acc = w[3:4] * shifted[0]
    for j in (1, 2, 3):
        acc = acc + w[3 - j:4 - j] * shifted[j]
    return acc, shifted


def _proj_conv_fwd(x2, mod3, w_in_pad, conv_w, conv_b, seq):
    t = x2.shape[0]
    tb = 256
    npb = seq // tb
    cw = 512

    def body(x_ref, mod_ref, w_hbm, cw_ref, cb_ref, z_ref, pre_ref, xbc_ref, dt_ref, u5_ref, w_vmem, win, sem):
        _load_once(w_hbm, w_vmem, sem)
        first = (pl.program_id(0) % npb) == 0

        @pl.when(first)
        def _():
            win[0:8, :] = jnp.zeros((8, D_XBC), F32)

        @pl.when(jnp.logical_not(first))
        def _():
            win[0:8, :] = win[tb:tb + 8, :]

        m = mod_ref[0]
        u = (x_ref[...] * (1.0 + m[1:2]) + m[0:1]).astype(BF16)
        z_ref[...] = jnp.dot(u, w_vmem[:, 0:D_SSD], preferred_element_type=F32)
        dt_ref[...] = jnp.dot(u, w_vmem[:, D_SSD + D_XBC:D_SSD + D_XBC + DT_PAD], preferred_element_type=F32)
        u5_ref[...] = jnp.dot(u, w_vmem[:, D_SSD + D_XBC + DT_PAD:], preferred_element_type=F32)
        for k in range(D_XBC // cw):
            cols = slice(k * cw, (k + 1) * cw)
            pre_k = jnp.dot(u, w_vmem[:, D_SSD + k * cw:D_SSD + (k + 1) * cw], preferred_element_type=F32)
            win[8:8 + tb, cols] = pre_k
            pre_ref[:, cols] = pre_k
            conv, _ = _conv_taps(win, cw_ref[:, cols], tb, cols)
            conv = conv + cb_ref[:, cols]
            xbc_ref[:, cols] = conv * _sigmoid(conv)

    row = lambda w: pl.BlockSpec((tb, w), lambda i: (i, 0))
    return pl.pallas_call(
        body, name="proj_conv_fwd", grid=(t // tb,),
        out_shape=(jax.ShapeDtypeStruct((t, D_SSD), F32), jax.ShapeDtypeStruct((t, D_XBC), F32),
                   jax.ShapeDtypeStruct((t, D_XBC), F32), jax.ShapeDtypeStruct((t, DT_PAD), F32),
                   jax.ShapeDtypeStruct((t, D_S5), F32)),
        in_specs=[row(D_MODEL), pl.BlockSpec((1, N_MOD, D_MODEL), lambda i: (i // npb, 0, 0)), ANY,
                  pl.BlockSpec((4, D_XBC), lambda i: (0, 0)), pl.BlockSpec((1, D_XBC), lambda i: (0, 0))],
        out_specs=(row(D_SSD), row(D_XBC), row(D_XBC), row(DT_PAD), row(D_S5)),
        scratch_shapes=[pltpu.VMEM((D_MODEL, D_INP), BF16), pltpu.VMEM((tb + 8, D_XBC), F32), pltpu.SemaphoreType.DMA],
        compiler_params=_params(56),
    )(x2, mod3, w_in_pad, conv_w, conv_b)


N_PAIRS = N_HEADS // 2


def _split3(x):
    hi = x.astype(BF16)
    r = x - hi.astype(F32)
    mid = r.astype(BF16)
    lo = (r - mid.astype(F32)).astype(BF16)
    return hi, mid, lo


def _dot3(x, e, dims=(((1,), (0,)), ((), ()))):
    return sum(lax.dot_general(p, e, dims, preferred_element_type=F32) for p in _split3(x))


def _dot3_left(e, x, dims=(((1,), (0,)), ((), ()))):
    return sum(lax.dot_general(e, p, dims, preferred_element_type=F32) for p in _split3(x))


def _ssd_consts():
    head = jnp.arange(128)
    expand = (head[:, None] == jnp.arange(D_SSD)[None, :] // HEADDIM).astype(BF16)
    sel = (head[None, :, None] == 2 * jnp.arange(N_PAIRS)[:, None, None] + jnp.arange(256)[None, None, :] // 128)
    return expand, expand.T, sel.astype(BF16)


def _ssd_prep(dt_raw, par, expand):
    dtb = par[0:1]
    a = -jnp.exp(par[1:2])
    dt = _softplus(dt_raw + dtb)
    adt = dt * a
    row = lax.broadcasted_iota(jnp.int32, (CHUNK, CHUNK), 0)
    col = lax.broadcasted_iota(jnp.int32, (CHUNK, CHUNK), 1)
    causal = row >= col
    tri = causal.astype(BF16)
    cs = _dot3_left(tri, adt)
    cs_last = cs[CHUNK - 1:CHUNK, :]
    dt_c = _dot3(dt, expand)
    ecs_c = _dot3(jnp.exp(cs), expand)
    w_c = _dot3(jnp.exp(cs_last - cs), expand)
    return dt, a, cs, cs.T, causal, tri, dt_c, ecs_c, w_c


def _pair_decay(cs, cst, sel_p, pair, causal2):
    cols = _dot3(cs, sel_p)
    rows = jnp.concatenate([jnp.broadcast_to(cst[2 * pair:2 * pair + 1, :], (CHUNK, CHUNK)),
                            jnp.broadcast_to(cst[2 * pair + 1:2 * pair + 2, :], (CHUNK, CHUNK))], axis=1)
    return jnp.exp(jnp.where(causal2, cols - rows, -jnp.inf))


def _stack_heads(xp, left):
    return jnp.concatenate([jnp.where(left, xp, 0.0), jnp.where(left, 0.0, xp)], axis=0).astype(BF16)


def _ssd_fwd(xbc, z, dt_raw, par, dsk, normw, seq):
    t = xbc.shape[0]
    nc = seq // CHUNK
    n_chunks = t // CHUNK
    expand, _, sel = _ssd_consts()

    def body(xbc_ref, z_ref, dt_ref, par_ref, dsk_ref, nw_ref, ex_ref, sel_ref, yraw_ref, ycat_ref, hprev_ref, h_ref):
        @pl.when(pl.program_id(0) % nc == 0)
        def _():
            h_ref[...] = jnp.zeros_like(h_ref)
        hprev_ref[0] = h_ref[...]
        _, _, cs, cst, causal, _, dt_c, ecs_c, w_c = _ssd_prep(dt_ref[...], par_ref[...], ex_ref[...])
        cs_last = cs[CHUNK - 1:CHUNK, :]
        causal2 = jnp.concatenate([causal, causal], axis=1)
        left = lax.broadcasted_iota(jnp.int32, (CHUNK, 128), 1) < HEADDIM
        x = xbc_ref[:, 0:D_SSD]
        xdt = x * dt_c
        amat = (w_c * xdt).astype(BF16)
        zz = z_ref[...]
        silu_z = zz * _sigmoid(zz)
        for g in range(N_GROUPS):
            gs = slice(g * GW, (g + 1) * GW)
            bg = xbc_ref[:, D_SSD + g * N_STATE:D_SSD + (g + 1) * N_STATE].astype(BF16)
            cg = xbc_ref[:, D_SSD + (N_GROUPS + g) * N_STATE:D_SSD + (N_GROUPS + g + 1) * N_STATE].astype(BF16)
            scores = lax.dot_general(cg, bg, NT, preferred_element_type=F32)
            scores2 = jnp.concatenate([scores, scores], axis=1)
            hg = h_ref[gs, :]
            p_all = lax.dot_general(cg, hg.astype(BF16), NT, preferred_element_type=F32)
            ys = []
            for q in range(GW // 128):
                pair = g * (GW // 128) + q
                decay = _pair_decay(cs, cst, sel_ref[pair], pair, causal2)
                mcat = (scores2 * decay).astype(BF16)
                ys.append(jnp.dot(mcat, _stack_heads(xdt[:, pair * 128:(pair + 1) * 128], left),
                                  preferred_element_type=F32))
            yg = jnp.concatenate(ys, axis=1) + ecs_c[:, gs] * p_all + x[:, gs] * dsk_ref[:, gs]
            s_new = lax.dot_general(amat[:, gs], bg, TN, preferred_element_type=F32)
            for j in range(HPG):
                hh = g * HPG + j
                js = slice(j * HEADDIM, (j + 1) * HEADDIM)
                h_ref[g * GW + j * HEADDIM:g * GW + (j + 1) * HEADDIM, :] = (
                    hg[js, :] * jnp.exp(cs_last[:, hh:hh + 1]) + s_new[js, :])
            yraw_ref[:, gs] = yg
            v = yg * silu_z[:, gs]
            r = lax.rsqrt(jnp.mean(v * v, axis=-1, keepdims=True) + EPS)
            ycat_ref[:, gs] = (v * r * nw_ref[:, gs]).astype(BF16)

    row = lambda w: pl.BlockSpec((CHUNK, w), lambda i: (i, 0))
    full = lambda s: pl.BlockSpec(s, lambda i: (0,) * len(s))
    return pl.pallas_call(
        body, name="ssd_fwd", grid=(n_chunks,),
        out_shape=(jax.ShapeDtypeStruct((t, D_SSD), F32), jax.ShapeDtypeStruct((t, D_SSD + D_S5), BF16),
                   jax.ShapeDtypeStruct((n_chunks, D_SSD, N_STATE), F32)),
        in_specs=[row(D_XBC), row(D_SSD), row(DT_PAD), full((8, 128)), full((1, D_SSD)), full((1, D_SSD)),
                  full(expand.shape), full(sel.shape)],
        out_specs=(row(D_SSD), row(D_SSD), pl.BlockSpec((1, D_SSD, N_STATE), lambda i: (i, 0, 0))),
        scratch_shapes=[pltpu.VMEM((D_SSD, N_STATE), F32)],
        compiler_params=_params(40),
    )(xbc, z, dt_raw, par, dsk, normw, expand, sel)


S5_CW = 512
S5_BLOCKS = 4


def _tile_scan(in_re, in_im, out_re, out_im, carry_re, carry_im, pw_re, pw_im, n_tiles, reverse):
    steps = (1, 2, 4)
    for cc in range(S5_N // S5_CW):
        cols = slice(cc * S5_CW, (cc + 1) * S5_CW)
        a_re, a_im = pw_re[:, cols], pw_im[:, cols]
        rid = lax.broadcasted_iota(jnp.int32, (8, S5_CW), 0)
        pows = []
        for d in steps:
            k = 8 - d if reverse else d - 1
            keep = (rid < 8 - d) if reverse else (rid >= d)
            pows.append((jnp.where(keep, pw_re[k:k + 1, cols], 0.0), jnp.where(keep, pw_im[k:k + 1, cols], 0.0)))

        def tile(i, carry, cols=cols, pows=pows, a_re=a_re, a_im=a_im):
            r = (n_tiles - 1 - i) if reverse else i
            rows = pl.ds(pl.multiple_of(r * 8, 8), 8)
            xr, xi = in_re[rows, cols], in_im[rows, cols]
            for (pr, pi), d in zip(pows, steps):
                shift = 8 - d if reverse else d
                sr, si = pltpu.roll(xr, shift, axis=0), pltpu.roll(xi, shift, axis=0)
                xr, xi = xr + pr * sr - pi * si, xi + pr * si + pi * sr
            cr, ci = carry
            xr, xi = xr + a_re * cr - a_im * ci, xi + a_re * ci + a_im * cr
            out_re[rows, cols] = xr
            out_im[rows, cols] = xi
            edge = slice(0, 1) if reverse else slice(7, 8)
            return (jnp.broadcast_to(xr[edge], (8, S5_CW)), jnp.broadcast_to(xi[edge], (8, S5_CW)))

        c0 = (jnp.broadcast_to(carry_re[0:1, cols], (8, S5_CW)), jnp.broadcast_to(carry_im[0:1, cols], (8, S5_CW)))
        cr, ci = lax.fori_loop(0, n_tiles, tile, c0)
        carry_re[:, cols] = cr
        carry_im[:, cols] = ci


def _s5_params_math(ar, ai, ldt, br, bi):
    dt = jnp.exp(ldt)
    mag = jnp.exp(ar * dt)
    ang = ai * dt
    ab_re = mag * jnp.cos(ang)
    ab_im = mag * jnp.sin(ang)
    den = ar * ar + ai * ai
    n_re = ab_re - 1.0
    coef_re = (n_re * ar + ab_im * ai) / den
    coef_im = (ab_im * ar - n_re * ai) / den
    bb_re = coef_re * br - coef_im * bi
    bb_im = coef_re * bi + coef_im * br
    return ab_re, ab_im, bb_re, bb_im


def _s5_params_fwd(ar, ai, ldt, br, bi):
    def body(ar_ref, ai_ref, ldt_ref, br_ref, bi_ref, bbr_ref, bbi_ref, pfr_ref, pfi_ref, prr_ref, pri_ref):
        ab_re, ab_im, bb_re, bb_im = _s5_params_math(ar_ref[...], ai_ref[...], ldt_ref[...], br_ref[...], bi_ref[...])
        bbr_ref[...] = bb_re
        bbi_ref[...] = bb_im
        pr, pi = ab_re, ab_im
        for k in range(8):
            pfr_ref[k:k + 1, :] = pr
            pfi_ref[k:k + 1, :] = pi
            prr_ref[7 - k:8 - k, :] = pr
            pri_ref[7 - k:8 - k, :] = -pi
            pr, pi = pr * ab_re - pi * ab_im, pr * ab_im + pi * ab_re

    b16 = jax.ShapeDtypeStruct((S5_CH, S5_N), F32)
    p8 = jax.ShapeDtypeStruct((8, S5_N), F32)
    return pl.pallas_call(body, name="s5_params_fwd", out_shape=(b16, b16, p8, p8, p8, p8),
                          compiler_params=_params(32))(ar, ai, ldt, br, bi)


def _s5_params_bwd(ar, ai, ldt, br, bi, d_ab_re, d_ab_im, d_bb_re, d_bb_im):
    def body(ar_ref, ai_ref, ldt_ref, br_ref, bi_ref, dar_ref, dai_ref, dbr_ref, dbi_ref,
             gar_ref, gai_ref, gldt_ref, gbr_ref, gbi_ref):
        _, vjp = jax.vjp(_s5_params_math, ar_ref[...], ai_ref[...], ldt_ref[...], br_ref[...], bi_ref[...])
        g_ar, g_ai, g_ldt, g_br, g_bi = vjp((dar_ref[...], dai_ref[...], dbr_ref[...], dbi_ref[...]))
        gar_ref[...] = g_ar
        gai_ref[...] = g_ai
        gbr_ref[...] = g_br
        gbi_ref[...] = g_bi
        lane = lax.broadcasted_iota(jnp.int32, (S5_N, 128), 0) // S5_P
        grp = lax.broadcasted_iota(jnp.int32, (S5_N, 128), 1)
        fold = (lane == grp).astype(F32)
        gldt_ref[...] = jnp.dot(g_ldt, fold, preferred_element_type=F32, precision=HIGHEST)

    v1 = jax.ShapeDtypeStruct((1, S5_N), F32)
    b16 = jax.ShapeDtypeStruct((S5_CH, S5_N), F32)
    return pl.pallas_call(body, name="s5_params_bwd",
                          out_shape=(v1, v1, jax.ShapeDtypeStruct((1, 128), F32), b16, b16),
                          compiler_params=_params(32))(ar, ai, ldt, br, bi, d_ab_re, d_ab_im, d_bb_re, d_bb_im)


def _s5_fwd(u5, bb_re, bb_im, cc_re, cc_im, pf_re, pf_im, s5d, w_glu, b_glu, ycat, seq):
    t = u5.shape[0]
    tb = 256
    npb = seq // tb

    def body(u_ref, bbr_ref, bbi_ref, ccr_ref, cci_ref, pfr_ref, pfi_ref, d_ref, wg_ref, bg_ref, ycat_hbm,
             sre_ref, sim_ref, ypre_ref, y5_ref, bur, bui, car, cai):
        del ycat_hbm

        @pl.when(pl.program_id(0) % npb == 0)
        def _():
            car[...] = jnp.zeros_like(car)
            cai[...] = jnp.zeros_like(cai)
        u = u_ref[...]
        ub = u.astype(BF16)
        for j in range(S5_BLOCKS):
            ch, st = slice(j * 128, (j + 1) * 128), slice(j * 512, (j + 1) * 512)
            bur[:, st] = jnp.dot(ub[:, ch], bbr_ref[j], preferred_element_type=F32)
            bui[:, st] = jnp.dot(ub[:, ch], bbi_ref[j], preferred_element_type=F32)
        _tile_scan(bur, bui, sre_ref, sim_ref, car, cai, pfr_ref, pfi_ref, tb // 8, reverse=False)
        cs_y = []
        for j in range(S5_BLOCKS):
            st = slice(j * 512, (j + 1) * 512)
            cs_y.append(_mm(sre_ref[:, st], ccr_ref[j]) - _mm(sim_ref[:, st], cci_ref[j]))
        ypre = jnp.concatenate(cs_y, axis=1) + u * d_ref[...]
        ypre_ref[...] = ypre
        yg = _gelu(ypre)
        y5_ref[...] = (yg * _sigmoid(_mm(yg, wg_ref[...]) + bg_ref[...])).astype(BF16)

    row = lambda w: pl.BlockSpec((tb, w), lambda i: (i, 0))
    full = lambda a: pl.BlockSpec(a.shape, lambda i: (0,) * a.ndim)
    return pl.pallas_call(
        body, name="s5_fwd", grid=(t // tb,),
        out_shape=(jax.ShapeDtypeStruct((t, S5_N), F32), jax.ShapeDtypeStruct((t, S5_N), F32),
                   jax.ShapeDtypeStruct((t, D_S5), F32), jax.ShapeDtypeStruct(ycat.shape, BF16)),
        in_specs=[row(D_S5), full(bb_re), full(bb_im), full(cc_re), full(cc_im), full(pf_re), full(pf_im),
                  full(s5d), full(w_glu), full(b_glu), ANY],
        out_specs=(row(S5_N), row(S5_N), row(D_S5), pl.BlockSpec((tb, D_S5), lambda i: (i, D_SSD // D_S5))),
        input_output_aliases={10: 3},
        scratch_shapes=[pltpu.VMEM((tb, S5_N), F32), pltpu.VMEM((tb, S5_N), F32),
                        pltpu.VMEM((8, S5_N), F32), pltpu.VMEM((8, S5_N), F32)],
        compiler_params=_params(48),
    )(u5, bb_re, bb_im, cc_re, cc_im, pf_re, pf_im, s5d, w_glu, b_glu, ycat)


def _layer_norm(r, g, b):
    mu = jnp.mean(r, axis=-1, keepdims=True)
    xc = r - mu
    rstd = lax.rsqrt(jnp.mean(xc * xc, axis=-1, keepdims=True) + EPS)
    xhat = xc * rstd
    return xhat * g + b, xhat, rstd


def _layer_norm_bwd(dy, xhat, rstd, g):
    dxhat = dy * g
    return rstd * (dxhat - jnp.mean(dxhat, axis=-1, keepdims=True)
                   - xhat * jnp.mean(dxhat * xhat, axis=-1, keepdims=True))


def _out_ln1(ycat, x2, mod3, w_out, ln1, seq):
    t = x2.shape[0]
    tb = 512
    npb = seq // tb

    def body(y_ref, x_ref, mod_ref, w_ref, ln_ref, mix_ref, x1_ref):
        m = mod_ref[0]
        mix = jnp.dot(y_ref[...], w_ref[...], preferred_element_type=F32)
        mix_ref[...] = mix
        r1 = ALPHA * x_ref[...] + (1.0 + m[2:3]) * mix
        x1_ref[...] = _layer_norm(r1, ln_ref[0:1], ln_ref[1:2])[0]

    row = lambda w: pl.BlockSpec((tb, w), lambda i: (i, 0))
    return pl.pallas_call(
        body, name="out_ln1", grid=(t // tb,),
        out_shape=(jax.ShapeDtypeStruct((t, D_MODEL), F32), jax.ShapeDtypeStruct((t, D_MODEL), F32)),
        in_specs=[row(D_SSD + D_S5), row(D_MODEL), pl.BlockSpec((1, N_MOD, D_MODEL), lambda i: (i // npb, 0, 0)),
                  pl.BlockSpec(w_out.shape, lambda i: (0, 0)), pl.BlockSpec(ln1.shape, lambda i: (0, 0))],
        out_specs=(row(D_MODEL), row(D_MODEL)), compiler_params=_params(48),
    )(ycat, x2, mod3, w_out, ln1)


def _mlp_fwd_bwd(x1, tgt, mod3, w1, w2, vec1, b1, seq):
    t = x1.shape[0]
    tb = 256
    npb = seq // tb
    n_fb, _, fb = w1.shape

    def body(x1_ref, tgt_ref, mod_ref, w1_hbm, w2_hbm, v_ref, b1_ref,
             dx1_ref, u2_ref, h_ref, dhp_ref, do_ref, gacc_ref, db1_ref, bacc_ref, w1_v, w2_v, sem1, sem2):
        i = pl.program_id(0)
        @pl.when(i == 0)
        def _():
            cps = [pltpu.make_async_copy(w1_hbm.at[k], w1_v.at[:, k * fb:(k + 1) * fb], sem1.at[k])
                   for k in range(n_fb)]
            for cp in cps:
                cp.start()
            for cp in cps:
                cp.wait()
        _load_once(w2_hbm, w2_v, sem2)

        @pl.when(i == 0)
        def _():
            gacc_ref[...] = jnp.zeros_like(gacc_ref)
            db1_ref[...] = jnp.zeros_like(db1_ref)

        @pl.when(i % npb == 0)
        def _():
            bacc_ref[...] = jnp.zeros_like(bacc_ref)

        m = mod_ref[0]
        sh2, sc2, g2 = m[3:4], m[4:5], m[5:6]
        x1v = x1_ref[...]
        u2 = (x1v * (1.0 + sc2) + sh2).astype(BF16)
        u2_ref[...] = u2
        hr = jnp.maximum(jnp.dot(u2, w1_v[...], preferred_element_type=F32) + b1_ref[...], 0.0)
        hb = (hr * hr).astype(BF16)
        h_ref[...] = hb
        o = jnp.dot(hb, w2_v[...], preferred_element_type=F32) + v_ref[0:1]
        r2 = ALPHA * x1v + (1.0 + g2) * o
        y, xhat, rstd = _layer_norm(r2, v_ref[1:2], v_ref[2:3])
        err = y - tgt_ref[...]
        dy = err * (1.0 / D_MODEL)
        dr2 = _layer_norm_bwd(dy, xhat, rstd, v_ref[1:2])
        do = (1.0 + g2) * dr2
        dob = do.astype(BF16)
        do_ref[...] = dob
        gacc_ref[0:1, :] += jnp.sum(dy * xhat, axis=0, keepdims=True)
        gacc_ref[1:2, :] += jnp.sum(dy, axis=0, keepdims=True)
        gacc_ref[2:3, :] += jnp.sum(do, axis=0, keepdims=True)
        gacc_ref[3:4, :] += jnp.sum(err * err, axis=0, keepdims=True)
        dhpre = lax.dot_general(dob, w2_v[...], NT, preferred_element_type=F32) * (2.0 * hr)
        dhpb = dhpre.astype(BF16)
        dhp_ref[...] = dhpb
        db1_ref[...] += jnp.sum(dhpre, axis=0, keepdims=True)
        du2 = lax.dot_general(dhpb, w1_v[...], NT, preferred_element_type=F32)
        dx1_ref[...] = ALPHA * dr2 + du2 * (1.0 + sc2)
        bacc_ref[0, 0:1, :] += jnp.sum(du2, axis=0, keepdims=True)
        bacc_ref[0, 1:2, :] += jnp.sum(du2 * x1v, axis=0, keepdims=True)
        bacc_ref[0, 2:3, :] += jnp.sum(dr2 * o, axis=0, keepdims=True)

    row = lambda w: pl.BlockSpec((tb, w), lambda i: (i, 0))
    return pl.pallas_call(
        body, name="mlp_fwd_bwd", grid=(t // tb,),
        out_shape=(jax.ShapeDtypeStruct((t, D_MODEL), F32), jax.ShapeDtypeStruct((t, D_MODEL), BF16),
                   jax.ShapeDtypeStruct((t, D_FF), BF16), jax.ShapeDtypeStruct((t, D_FF), BF16),
                   jax.ShapeDtypeStruct((t, D_MODEL), BF16), jax.ShapeDtypeStruct((8, D_MODEL), F32),
                   jax.ShapeDtypeStruct((1, D_FF), F32), jax.ShapeDtypeStruct((t // seq, 8, D_MODEL), F32)),
        in_specs=[row(D_MODEL), row(D_MODEL), pl.BlockSpec((1, N_MOD, D_MODEL), lambda i: (i // npb, 0, 0)), ANY, ANY,
                  pl.BlockSpec(vec1.shape, lambda i: (0, 0)), pl.BlockSpec(b1.shape, lambda i: (0, 0))],
        out_specs=(row(D_MODEL), row(D_MODEL), row(D_FF), row(D_FF), row(D_MODEL),
                   pl.BlockSpec((8, D_MODEL), lambda i: (0, 0)), pl.BlockSpec((1, D_FF), lambda i: (0, 0)),
                   pl.BlockSpec((1, 8, D_MODEL), lambda i: (i // npb, 0, 0))),
        scratch_shapes=[pltpu.VMEM((D_MODEL, n_fb * fb), BF16), pltpu.VMEM((D_FF, D_MODEL), BF16),
                        pltpu.SemaphoreType.DMA((n_fb,)), pltpu.SemaphoreType.DMA],
        compiler_params=_params(60),
    )(x1, tgt, mod3, w1, w2, vec1, b1)


def _ln1_out_bwd(dx1, x2, mix, mod3, w_out, ln1, seq):
    t = x2.shape[0]
    tb = 512
    npb = seq // tb

    def body(dx1_ref, x_ref, mix_ref, mod_ref, w_ref, ln_ref, dmix_ref, dxa_ref, dys_ref, dy5_ref, gacc_ref, bacc_ref):
        i = pl.program_id(0)

        @pl.when(i == 0)
        def _():
            gacc_ref[...] = jnp.zeros_like(gacc_ref)

        @pl.when(i % npb == 0)
        def _():
            bacc_ref[...] = jnp.zeros_like(bacc_ref)

        m = mod_ref[0]
        mix = mix_ref[...]
        r1 = ALPHA * x_ref[...] + (1.0 + m[2:3]) * mix
        _, xhat, rstd = _layer_norm(r1, ln_ref[0:1], ln_ref[1:2])
        dx1v = dx1_ref[...]
        dr1 = _layer_norm_bwd(dx1v, xhat, rstd, ln_ref[0:1])
        gacc_ref[0:1, :] += jnp.sum(dx1v * xhat, axis=0, keepdims=True)
        gacc_ref[1:2, :] += jnp.sum(dx1v, axis=0, keepdims=True)
        bacc_ref[0, 0:1, :] += jnp.sum(dr1 * mix, axis=0, keepdims=True)
        dmix = ((1.0 + m[2:3]) * dr1).astype(BF16)
        dmix_ref[...] = dmix
        dxa_ref[...] = ALPHA * dr1
        dys_ref[...] = lax.dot_general(dmix, w_ref[0:D_SSD, :], NT, preferred_element_type=F32)
        dy5_ref[...] = lax.dot_general(dmix, w_ref[D_SSD:, :], NT, preferred_element_type=F32)

    row = lambda w: pl.BlockSpec((tb, w), lambda i: (i, 0))
    return pl.pallas_call(
        body, name="ln1_out_bwd", grid=(t // tb,),
        out_shape=(jax.ShapeDtypeStruct((t, D_MODEL), BF16), jax.ShapeDtypeStruct((t, D_MODEL), F32),
                   jax.ShapeDtypeStruct((t, D_SSD), F32), jax.ShapeDtypeStruct((t, D_S5), F32),
                   jax.ShapeDtypeStruct((8, D_MODEL), F32), jax.ShapeDtypeStruct((t // seq, 8, D_MODEL), F32)),
        in_specs=[row(D_MODEL), row(D_MODEL), row(D_MODEL), pl.BlockSpec((1, N_MOD, D_MODEL), lambda i: (i // npb, 0, 0)),
                  pl.BlockSpec(w_out.shape, lambda i: (0, 0)), pl.BlockSpec(ln1.shape, lambda i: (0, 0))],
        out_specs=(row(D_MODEL), row(D_MODEL), row(D_SSD), row(D_S5), pl.BlockSpec((8, D_MODEL), lambda i: (0, 0)),
                   pl.BlockSpec((1, 8, D_MODEL), lambda i: (i // npb, 0, 0))),
        compiler_params=_params(48),
    )(dx1, x2, mix, mod3, w_out, ln1)


def _s5_bwd(dy5, ypre, u5, s_re, s_im, bb_re, bb_im, cc_re, cc_im, pr_re, pr_im, s5d, w_glu, b_glu, seq):
    t = u5.shape[0]
    tb = 256
    npb = seq // tb
    n_blocks = t // tb

    def blk(i):
        return (i // npb) * npb + (npb - 1 - i % npb)

    def body(dy_ref, ypre_ref, u_ref, sre_ref, sim_ref, hre_ref, him_ref, bbr_ref, bbi_ref, ccr_ref, cci_ref,
             prr_ref, pri_ref, d_ref, wg_ref, bg_ref,
             du_ref, vacc_ref, sacc_ref, dcc_ref, dbb_ref, dwg_ref, dsr, dsi, gr, gi, car, cai):
        i = pl.program_id(0)

        @pl.when(i == 0)
        def _():
            for acc in (vacc_ref, sacc_ref, dcc_ref, dbb_ref, dwg_ref):
                acc[...] = jnp.zeros_like(acc)

        @pl.when(i % npb == 0)
        def _():
            car[...] = jnp.zeros_like(car)
            cai[...] = jnp.zeros_like(cai)

        dy = dy_ref[...]
        ypre = ypre_ref[...]
        u = u_ref[...]
        ub = u.astype(BF16)
        yg = _gelu(ypre)
        sg = _sigmoid(_mm(yg, wg_ref[...]) + bg_ref[...])
        dq = dy * yg * sg * (1.0 - sg)
        dqb = dq.astype(BF16)
        dyg = dy * sg + lax.dot_general(dqb, wg_ref[...], NT, preferred_element_type=F32)
        dyp = dyg * _gelu_grad(ypre)
        dypb = dyp.astype(BF16)
        dwg_ref[...] += lax.dot_general(yg.astype(BF16), dqb, TN, preferred_element_type=F32)
        blocks = [(slice(j * 128, (j + 1) * 128), slice(j * 512, (j + 1) * 512)) for j in range(S5_BLOCKS)]
        for j, (ch, st) in enumerate(blocks):
            dsr[:, st] = lax.dot_general(dypb[:, ch], ccr_ref[j], NT, preferred_element_type=F32)
            dsi[:, st] = -lax.dot_general(dypb[:, ch], cci_ref[j], NT, preferred_element_type=F32)
        _tile_scan(dsr, dsi, gr, gi, car, cai, prr_ref, pri_ref, tb // 8, reverse=True)
        g_re, g_im = gr[...], gi[...]
        first_rows = (i % npb) == npb - 1
        hre = jnp.where(first_rows, 0.0, hre_ref[...])
        him = jnp.where(first_rows, 0.0, him_ref[...])
        s_re_v, s_im_v = sre_ref[...], sim_ref[...]
        sp_re = pltpu.roll(jnp.concatenate([hre, s_re_v], axis=0), 1, axis=0)[8:8 + tb]
        sp_im = pltpu.roll(jnp.concatenate([him, s_im_v], axis=0), 1, axis=0)[8:8 + tb]
        vacc_ref[0:1, :] += jnp.sum(g_re * sp_re + g_im * sp_im, axis=0, keepdims=True)
        vacc_ref[1:2, :] += jnp.sum(g_im * sp_re - g_re * sp_im, axis=0, keepdims=True)
        grb, gib = g_re.astype(BF16), g_im.astype(BF16)
        srb, sib = s_re_v.astype(BF16), s_im_v.astype(BF16)
        du_cols = []
        for j, (ch, st) in enumerate(blocks):
            dcc_ref[j] += lax.dot_general(srb[:, st], dypb[:, ch], TN, preferred_element_type=F32)
            dcc_ref[S5_BLOCKS + j] -= lax.dot_general(sib[:, st], dypb[:, ch], TN, preferred_element_type=F32)
            dbb_ref[j] += lax.dot_general(ub[:, ch], grb[:, st], TN, preferred_element_type=F32)
            dbb_ref[S5_BLOCKS + j] += lax.dot_general(ub[:, ch], gib[:, st], TN, preferred_element_type=F32)
            du_cols.append(lax.dot_general(grb[:, st], bbr_ref[j], NT, preferred_element_type=F32)
                           + lax.dot_general(gib[:, st], bbi_ref[j], NT, preferred_element_type=F32))
        du_ref[...] = jnp.concatenate(du_cols, axis=1) + dyp * d_ref[...]
        sacc_ref[0:1, :] += jnp.sum(dyp * u, axis=0, keepdims=True)
        sacc_ref[1:2, :] += jnp.sum(dq, axis=0, keepdims=True)

    row = lambda w: pl.BlockSpec((tb, w), lambda i: (blk(i), 0))
    halo = pl.BlockSpec((8, S5_N), lambda i: (jnp.maximum(blk(i) * (tb // 8) - 1, 0), 0))
    full = lambda a: pl.BlockSpec(a.shape, lambda i: (0,) * a.ndim)
    acc = lambda s: pl.BlockSpec(s, lambda i: (0,) * len(s))
    acc_shapes = [(8, S5_N), (8, D_S5), (2 * S5_BLOCKS, 512, 128), (2 * S5_BLOCKS, 128, 512), (D_S5, D_S5)]
    return pl.pallas_call(
        body, name="s5_bwd", grid=(n_blocks,),
        out_shape=(jax.ShapeDtypeStruct((t, D_S5), F32),) + tuple(jax.ShapeDtypeStruct(s, F32) for s in acc_shapes),
        in_specs=[row(D_S5), row(D_S5), row(D_S5), row(S5_N), row(S5_N), halo, halo, full(bb_re), full(bb_im),
                  full(cc_re), full(cc_im), full(pr_re), full(pr_im), full(s5d), full(w_glu), full(b_glu)],
        out_specs=(row(D_S5),) + tuple(acc(s) for s in acc_shapes),
        scratch_shapes=[pltpu.VMEM((tb, S5_N), F32), pltpu.VMEM((tb, S5_N), F32), pltpu.VMEM((tb, S5_N), F32),
                        pltpu.VMEM((tb, S5_N), F32), pltpu.VMEM((8, S5_N), F32), pltpu.VMEM((8, S5_N), F32)],
        compiler_params=_params(56),
    )(dy5, ypre, u5, s_re, s_im, s_re, s_im, bb_re, bb_im, cc_re, cc_im, pr_re, pr_im, s5d, w_glu, b_glu)


def _ssd_bwd(dyssd, yraw, z, xbc, dt_raw, hprev, par, dsk, normw, seq):
    t = xbc.shape[0]
    nc = seq // CHUNK
    n_chunks = t // CHUNK
    expand, fold, sel = _ssd_consts()

    def blk(i):
        return (i // nc) * nc + (nc - 1 - i % nc)

    def body(dy_ref, yraw_ref, z_ref, xbc_ref, dt_ref, hprev_ref, par_ref, dsk_ref, nw_ref, ex_ref, fold_ref, sel_ref,
             dxbc_ref, dz_ref, ddt_ref, dpar_ref, cacc_ref, dh_ref, dyr_ref):
        i = pl.program_id(0)

        @pl.when(i == 0)
        def _():
            dpar_ref[...] = jnp.zeros_like(dpar_ref)
            cacc_ref[...] = jnp.zeros_like(cacc_ref)

        @pl.when(i % nc == 0)
        def _():
            dh_ref[...] = jnp.zeros_like(dh_ref)

        zz = z_ref[...]
        sz = _sigmoid(zz)
        silu_z = zz * sz
        yraw = yraw_ref[...]
        for g in range(N_GROUPS):
            sl = slice(g * GW, (g + 1) * GW)
            v = yraw[:, sl] * silu_z[:, sl]
            r = lax.rsqrt(jnp.mean(v * v, axis=-1, keepdims=True) + EPS)
            dyg = dy_ref[:, sl]
            cacc_ref[1:2, sl] += jnp.sum(dyg * v * r, axis=0, keepdims=True)
            dyw = dyg * nw_ref[:, sl]
            dv = r * dyw - v * (r * r * r) * jnp.mean(dyw * v, axis=-1, keepdims=True)
            dyr_ref[:, sl] = dv * silu_z[:, sl]
            dz_ref[:, sl] = dv * yraw[:, sl] * (sz[:, sl] * (1.0 + zz[:, sl] * (1.0 - sz[:, sl])))

        dt, a, cs, cst, causal, tri, dt_c, ecs_c, w_c = _ssd_prep(dt_ref[...], par_ref[...], ex_ref[...])
        cs_last = cs[CHUNK - 1:CHUNK, :]
        causal2 = jnp.concatenate([causal, causal], axis=1)
        lane = lax.broadcasted_iota(jnp.int32, (CHUNK, 128), 1)
        left = lane < HEADDIM
        lane1 = lax.broadcasted_iota(jnp.int32, (1, 128), 1)
        x = xbc_ref[:, 0:D_SSD]
        xdt = x * dt_c
        dyr = dyr_ref[...]
        dyrb = dyr.astype(BF16)
        cacc_ref[0:1, :] += jnp.sum(dyr * x, axis=0, keepdims=True)
        dlast = jnp.zeros((1, 128), F32)
        dxdt_cols, diag_all, dww_cols = [], [], []
        for g in range(N_GROUPS):
            gs = slice(g * GW, (g + 1) * GW)
            b_sl = slice(D_SSD + g * N_STATE, D_SSD + (g + 1) * N_STATE)
            c_sl = slice(D_SSD + (N_GROUPS + g) * N_STATE, D_SSD + (N_GROUPS + g + 1) * N_STATE)
            bg = xbc_ref[:, b_sl].astype(BF16)
            cg = xbc_ref[:, c_sl].astype(BF16)
            scores = lax.dot_general(cg, bg, NT, preferred_element_type=F32)
            scores2 = jnp.concatenate([scores, scores], axis=1)
            hg = hprev_ref[0, gs, :]
            hgb = hg.astype(BF16)
            dhg = dh_ref[gs, :]
            dhgb = dhg.astype(BF16)
            q_all = lax.dot_general(bg, dhgb, NT, preferred_element_type=F32)
            dscores = jnp.zeros((CHUNK, CHUNK), F32)
            diag_cols = []
            for q in range(GW // 128):
                pair = g * (GW // 128) + q
                ps = slice(pair * 128, (pair + 1) * 128)
                decay = _pair_decay(cs, cst, sel_ref[pair], pair, causal2)
                mcat = (scores2 * decay).astype(BF16)
                dyp = dyrb[:, ps]
                dm = lax.dot_general(dyp, _stack_heads(xdt[:, ps], left), NT, preferred_element_type=F32)
                dmd = dm * decay
                dscores = dscores + dmd[:, 0:CHUNK] + dmd[:, CHUNK:]
                rr = lax.dot_general(mcat, dyp, TN, preferred_element_type=F32)
                diag_cols.append(jnp.where(left, rr[0:CHUNK], rr[CHUNK:]))
            wq = w_c[:, gs] * q_all
            diag_g = jnp.concatenate(diag_cols, axis=1)
            diag_all.append(diag_g)
            dxdt_cols.append(diag_g + wq)
            dww_cols.append(wq * xdt[:, gs])
            dp = (ecs_c[:, gs] * dyr[:, gs]).astype(BF16)
            amat = (w_c[:, gs] * xdt[:, gs]).astype(BF16)
            dsb = dscores.astype(BF16)
            dxbc_ref[:, c_sl] = (jnp.dot(dsb, bg, preferred_element_type=F32)
                                 + jnp.dot(dp, hgb, preferred_element_type=F32))
            dxbc_ref[:, b_sl] = (lax.dot_general(dsb, cg, TN, preferred_element_type=F32)
                                 + jnp.dot(amat, dhgb, preferred_element_type=F32))
            dh_in = lax.dot_general(dp, cg, TN, preferred_element_type=F32)
            for j in range(HPG):
                hh = g * HPG + j
                js = slice(j * HEADDIM, (j + 1) * HEADDIM)
                ecl = jnp.exp(cs_last[:, hh:hh + 1])
                dlast = dlast + jnp.where(lane1 == hh, ecl * jnp.sum(dhg[js, :] * hg[js, :]), 0.0)
                dh_ref[g * GW + j * HEADDIM:g * GW + (j + 1) * HEADDIM, :] = ecl * dhg[js, :] + dh_in[js, :]
        dxdt = jnp.concatenate(dxdt_cols, axis=1)
        dxbc_ref[:, 0:D_SSD] = dxdt * dt_c + dyr * dsk_ref[...]
        dww = _dot3(jnp.concatenate(dww_cols, axis=1), fold_ref[...])
        dcs = (_dot3(dyrb.astype(F32) * (yraw - x * dsk_ref[...]), fold_ref[...])
               - _dot3(xdt.astype(BF16).astype(F32) * jnp.concatenate(diag_all, axis=1), fold_ref[...]) - dww)
        rowid = lax.broadcasted_iota(jnp.int32, (CHUNK, 128), 0)
        dcs = dcs + jnp.where(rowid == CHUNK - 1, jnp.sum(dww, axis=0, keepdims=True) + dlast, 0.0)
        dadt = _dot3_left(tri, dcs, TN)
        ddt = _dot3(dxdt * x, fold_ref[...]) + dadt * a
        da = jnp.sum(dadt * dt, axis=0, keepdims=True)
        ddt_raw = ddt * _sigmoid(dt_ref[...] + par_ref[0:1])
        ddt_raw = jnp.where(lane < N_HEADS, ddt_raw, 0.0)
        ddt_ref[...] = ddt_raw
        dpar_ref[0:1, :] += jnp.sum(ddt_raw, axis=0, keepdims=True)
        dpar_ref[1:2, :] += jnp.where(lane1 < N_HEADS, da * a, 0.0)

    row = lambda w: pl.BlockSpec((CHUNK, w), lambda i: (blk(i), 0))
    full = lambda s: pl.BlockSpec(s, lambda i: (0,) * len(s))
    return pl.pallas_call(
        body, name="ssd_bwd", grid=(n_chunks,),
        out_shape=(jax.ShapeDtypeStruct((t, D_XBC), F32), jax.ShapeDtypeStruct((t, D_SSD), F32),
                   jax.ShapeDtypeStruct((t, DT_PAD), F32), jax.ShapeDtypeStruct((8, 128), F32),
                   jax.ShapeDtypeStruct((8, D_SSD), F32)),
        in_specs=[row(D_SSD), row(D_SSD), row(D_SSD), row(D_XBC), row(DT_PAD),
                  pl.BlockSpec((1, D_SSD, N_STATE), lambda i: (blk(i), 0, 0)),
                  full((8, 128)), full((1, D_SSD)), full((1, D_SSD)), full(expand.shape), full(fold.shape),
                  full(sel.shape)],
        out_specs=(row(D_XBC), row(D_SSD), row(DT_PAD), full((8, 128)), full((8, D_SSD))),
        scratch_shapes=[pltpu.VMEM((D_SSD, N_STATE), F32), pltpu.VMEM((CHUNK, D_SSD), F32)],
        compiler_params=_params(48),
    )(dyssd, yraw, z, xbc, dt_raw, hprev, par, dsk, normw, expand, fold, sel)


def _conv_bwd(dxbc, xbc_pre, conv_w, conv_b, seq):
    t = xbc_pre.shape[0]
    tb = 512
    npb = seq // tb
    cw = 640

    def body(d_ref, cur_ref, halo_ref, w_ref, b_ref, o_ref, acc_ref, win):
        i = pl.program_id(1)

        @pl.when(i == 0)
        def _():
            acc_ref[...] = jnp.zeros_like(acc_ref)

        first = (i % npb) == 0
        win[0:8, :] = jnp.where(first, 0.0, halo_ref[...])
        win[8:8 + tb, :] = cur_ref[...]
        pre, shifted = _conv_taps(win, w_ref[...], tb, slice(None))
        pre = pre + b_ref[...]
        sg = _sigmoid(pre)
        dpre = d_ref[...] * (sg * (1.0 + pre * (1.0 - sg)))
        o_ref[...] = dpre
        for j in range(4):
            acc_ref[3 - j:4 - j, :] += jnp.sum(dpre * shifted[j], axis=0, keepdims=True)
        acc_ref[4:5, :] += jnp.sum(dpre, axis=0, keepdims=True)

    return pl.pallas_call(
        body, name="conv_bwd", grid=(D_XBC // cw, t // tb),
        out_shape=(jax.ShapeDtypeStruct((t, D_XBC), F32), jax.ShapeDtypeStruct((8, D_XBC), F32)),
        in_specs=[pl.BlockSpec((tb, cw), lambda j, i: (i, j)), pl.BlockSpec((tb, cw), lambda j, i: (i, j)),
                  pl.BlockSpec((8, cw), lambda j, i: (jnp.maximum(i * (tb // 8) - 1, 0), j)),
                  pl.BlockSpec((4, cw), lambda j, i: (0, j)), pl.BlockSpec((1, cw), lambda j, i: (0, j))],
        out_specs=(pl.BlockSpec((tb, cw), lambda j, i: (i, j)), pl.BlockSpec((8, cw), lambda j, i: (0, j))),
        scratch_shapes=[pltpu.VMEM((tb + 8, cw), F32)],
        compiler_params=_params(32),
    )(dxbc, xbc_pre, xbc_pre, conv_w, conv_b)


def _proj_bwd(dz, dpre, ddt, du5, x2, dxa, mod3, conv_w, w_in_pad, seq):
    t = x2.shape[0]
    tb = 512
    npb = seq // tb
    n_blocks = t // tb

    def body(dz_ref, dp_ref, nxt_ref, ddt_ref, du5_ref, x_ref, dxa_ref, mod_ref, cw_ref, w_hbm,
             gx_ref, u_ref, dxp_ref, bacc_ref, w_vmem, sem):
        i = pl.program_id(0)
        _load_once(w_hbm, w_vmem, sem)

        @pl.when(i % npb == 0)
        def _():
            bacc_ref[...] = jnp.zeros_like(bacc_ref)

        last = (i % npb) == npb - 1
        nxt = jnp.where(last, 0.0, nxt_ref[...])
        cur = dp_ref[...]
        xx = jnp.concatenate([cur, nxt], axis=0)
        w = cw_ref[...]
        dxp = w[3:4] * cur
        for j in (1, 2, 3):
            dxp = dxp + w[3 - j:4 - j] * pltpu.roll(xx, tb + 8 - j, axis=0)[0:tb]
        dxpb = dxp.astype(BF16)
        dxp_ref[...] = dxpb
        o1, o2, o3 = D_SSD, D_SSD + D_XBC, D_SSD + D_XBC + DT_PAD
        du = (lax.dot_general(dz_ref[...].astype(BF16), w_vmem[:, 0:o1], NT, preferred_element_type=F32)
              + lax.dot_general(dxpb, w_vmem[:, o1:o2], NT, preferred_element_type=F32)
              + lax.dot_general(ddt_ref[...].astype(BF16), w_vmem[:, o2:o3], NT, preferred_element_type=F32)
              + lax.dot_general(du5_ref[...].astype(BF16), w_vmem[:, o3:], NT, preferred_element_type=F32))
        m = mod_ref[0]
        xv = x_ref[...]
        u_ref[...] = (xv * (1.0 + m[1:2]) + m[0:1]).astype(BF16)
        gx_ref[...] = dxa_ref[...] + du * (1.0 + m[1:2])
        bacc_ref[0, 0:1, :] += jnp.sum(du, axis=0, keepdims=True)
        bacc_ref[0, 1:2, :] += jnp.sum(du * xv, axis=0, keepdims=True)

    row = lambda w: pl.BlockSpec((tb, w), lambda i: (i, 0))
    nxt_rows = pl.BlockSpec((8, D_XBC), lambda i: (jnp.minimum((i + 1) * (tb // 8), t // 8 - 1), 0))
    return pl.pallas_call(
        body, name="proj_bwd", grid=(n_blocks,),
        out_shape=(jax.ShapeDtypeStruct((t, D_MODEL), F32), jax.ShapeDtypeStruct((t, D_MODEL), BF16),
                   jax.ShapeDtypeStruct((t, D_XBC), BF16), jax.ShapeDtypeStruct((t // seq, 8, D_MODEL), F32)),
        in_specs=[row(D_SSD), row(D_XBC), nxt_rows, row(DT_PAD), row(D_S5), row(D_MODEL), row(D_MODEL),
                  pl.BlockSpec((1, N_MOD, D_MODEL), lambda i: (i // npb, 0, 0)),
                  pl.BlockSpec((4, D_XBC), lambda i: (0, 0)), ANY],
        out_specs=(row(D_MODEL), row(D_MODEL), row(D_XBC), pl.BlockSpec((1, 8, D_MODEL), lambda i: (i // npb, 0, 0))),
        scratch_shapes=[pltpu.VMEM((D_MODEL, D_INP), BF16), pltpu.SemaphoreType.DMA],
        compiler_params=_params(60),
    )(dz, dpre, dpre, ddt, du5, x2, dxa, mod3, conv_w, w_in_pad)


def _pad_rows(a, mult):
    r = a.shape[0]
    pad = (-r) % mult
    return a if pad == 0 else jnp.concatenate([a, jnp.zeros((pad,) + a.shape[1:], a.dtype)], axis=0)


_SMALL = ["conv_w", "conv_b", "dt_bias", "a_log", "d_ssd", "norm_w", "s5_a_re", "s5_a_im", "s5_log_dt", "s5_b_re",
          "s5_b_im", "s5_c_re", "s5_c_im", "s5_d", "b_glu", "ln1_g", "ln1_b", "b1", "b2", "ln2_g", "ln2_b"]


def _tile_rows(size):
    return 8 * (-(-size // 1024))


def _pack_small(d):
    parts = []
    for n in _SMALL:
        flat = d[n].reshape(-1).astype(F32)
        rows = _tile_rows(flat.shape[0])
        pad = rows * 128 - flat.shape[0]
        if pad:
            flat = jnp.concatenate([flat, jnp.zeros((pad,), F32)])
        parts.append(flat.reshape(rows, 128))
    return jnp.concatenate(parts, axis=0)


def _unpack_small(p, shapes):
    out, off = {}, 0
    for n in _SMALL:
        size = math.prod(shapes[n])
        rows = _tile_rows(size)
        out[n] = p[off:off + rows].reshape(-1)[:size].reshape(shapes[n])
        off += rows
    return out


def kernel(x, c, w_ada, b_ada, w_in, conv_w, conv_b, dt_bias, a_log, d_ssd, norm_w, s5_a_re, s5_a_im, s5_log_dt, s5_b_re, s5_b_im, s5_c_re, s5_c_im, s5_d, w_glu, b_glu, w_out, ln1_g, ln1_b, w1, b1, w2, b2, ln2_g, ln2_b, loss_target, m_w_ada, m_b_ada, m_w_in, m_conv_w, m_conv_b, m_dt_bias, m_a_log, m_d_ssd, m_norm_w, m_s5_a_re, m_s5_a_im, m_s5_log_dt, m_s5_b_re, m_s5_b_im, m_s5_c_re, m_s5_c_im, m_s5_d, m_w_glu, m_b_glu, m_w_out, m_ln1_g, m_ln1_b, m_w1, m_b1, m_w2, m_b2, m_ln2_g, m_ln2_b, v_w_ada, v_b_ada, v_w_in, v_conv_w, v_conv_b, v_dt_bias, v_a_log, v_d_ssd, v_norm_w, v_s5_a_re, v_s5_a_im, v_s5_log_dt, v_s5_b_re, v_s5_b_im, v_s5_c_re, v_s5_c_im, v_s5_d, v_w_glu, v_b_glu, v_w_out, v_ln1_g, v_ln1_b, v_w1, v_b1, v_w2, v_b2, v_ln2_g, v_ln2_b):
    weights = dict(w_ada=w_ada, b_ada=b_ada, w_in=w_in, conv_w=conv_w, conv_b=conv_b, dt_bias=dt_bias, a_log=a_log,
                   d_ssd=d_ssd, norm_w=norm_w, s5_a_re=s5_a_re, s5_a_im=s5_a_im, s5_log_dt=s5_log_dt, s5_b_re=s5_b_re,
                   s5_b_im=s5_b_im, s5_c_re=s5_c_re, s5_c_im=s5_c_im, s5_d=s5_d, w_glu=w_glu, b_glu=b_glu, w_out=w_out,
                   ln1_g=ln1_g, ln1_b=ln1_b, w1=w1, b1=b1, w2=w2, b2=b2, ln2_g=ln2_g, ln2_b=ln2_b)
    mom = dict(w_ada=m_w_ada, b_ada=m_b_ada, w_in=m_w_in, conv_w=m_conv_w, conv_b=m_conv_b, dt_bias=m_dt_bias,
               a_log=m_a_log, d_ssd=m_d_ssd, norm_w=m_norm_w, s5_a_re=m_s5_a_re, s5_a_im=m_s5_a_im,
               s5_log_dt=m_s5_log_dt, s5_b_re=m_s5_b_re, s5_b_im=m_s5_b_im, s5_c_re=m_s5_c_re, s5_c_im=m_s5_c_im,
               s5_d=m_s5_d, w_glu=m_w_glu, b_glu=m_b_glu, w_out=m_w_out, ln1_g=m_ln1_g, ln1_b=m_ln1_b, w1=m_w1, b1=m_b1,
               w2=m_w2, b2=m_b2, ln2_g=m_ln2_g, ln2_b=m_ln2_b)
    var = dict(w_ada=v_w_ada, b_ada=v_b_ada, w_in=v_w_in, conv_w=v_conv_w, conv_b=v_conv_b, dt_bias=v_dt_bias,
               a_log=v_a_log, d_ssd=v_d_ssd, norm_w=v_norm_w, s5_a_re=v_s5_a_re, s5_a_im=v_s5_a_im,
               s5_log_dt=v_s5_log_dt, s5_b_re=v_s5_b_re, s5_b_im=v_s5_b_im, s5_c_re=v_s5_c_re, s5_c_im=v_s5_c_im,
               s5_d=v_s5_d, w_glu=v_w_glu, b_glu=v_b_glu, w_out=v_w_out, ln1_g=v_ln1_g, ln1_b=v_ln1_b, w1=v_w1, b1=v_b1,
               w2=v_w2, b2=v_b2, ln2_g=v_ln2_g, ln2_b=v_ln2_b)
    names = list(weights)
    shapes = {n: weights[n].shape for n in names}

    nb, seq, _ = x.shape
    t = nb * seq
    dev = _dev_index()
    x2 = x.reshape(t, D_MODEL)
    tgt2 = loss_target.reshape(t, D_MODEL)

    cw_cols = conv_w.shape[2]
    small_in = jnp.concatenate([c.reshape(-1), conv_w.reshape(-1)]).reshape(-1, 128)
    small_all = _all_gather([small_in], "gather_c_conv")[0].reshape(N_DEV, -1)
    c_all = small_all[:, :nb * D_MODEL].reshape(N_DEV * nb, D_MODEL)
    conv_w_full = small_all[:, nb * D_MODEL:].reshape(N_DEV, 4, cw_cols).transpose(1, 0, 2).reshape(4, D_XBC)

    big_names = ["w_in", "w_out", "w1", "w2", "w_glu"]
    gathered = dict(zip(big_names, _all_gather([weights[n][0].astype(BF16) for n in big_names], "gather_weights")))
    w_in_f = gathered["w_in"].transpose(1, 0, 2).reshape(D_MODEL, D_IN)
    w_in_pad = jnp.concatenate(
        [w_in_f[:, :D_SSD + D_XBC], w_in_f[:, D_SSD + D_XBC:D_SSD + D_XBC + N_HEADS],
         jnp.zeros((D_MODEL, DT_PAD - N_HEADS), BF16), w_in_f[:, D_SSD + D_XBC + N_HEADS:]], axis=1)
    w_out_f = gathered["w_out"].reshape(2 * D_MODEL, D_MODEL)
    w1_blocks = gathered["w1"]
    w2_f = gathered["w2"].reshape(D_FF, D_MODEL)
    w_glu_f = gathered["w_glu"].reshape(D_S5, D_S5)

    ada_cols = w_ada.shape[2]
    b_cols = lax.dynamic_slice_in_dim(b_ada, dev * ada_cols, ada_cols, axis=1)
    mod_cols = _mod_fwd(c_all, w_ada[0], b_cols)
    mod_all = _all_gather([mod_cols], "gather_mod")[0]
    mod_mine = lax.dynamic_slice_in_dim(mod_all, dev * nb, nb, axis=1)
    mod3 = mod_mine.transpose(1, 0, 2).reshape(nb, N_MOD, D_MODEL)

    def pad_lanes(v, n):
        return jnp.concatenate([v, jnp.zeros((v.shape[0], n - v.shape[1]), F32)], axis=1)

    par = _pad_rows(jnp.concatenate([pad_lanes(dt_bias, 128), pad_lanes(a_log, 128)], axis=0), 8)
    dsk = jnp.repeat(d_ssd[0], HEADDIM).reshape(1, D_SSD)
    ar = s5_a_re.reshape(1, S5_N)
    ai = s5_a_im.reshape(1, S5_N)
    ldt = jnp.repeat(s5_log_dt[0], S5_P).reshape(1, S5_N)
    br_t = s5_b_re[0].transpose(2, 0, 1).reshape(S5_CH, S5_N)
    bi_t = s5_b_im[0].transpose(2, 0, 1).reshape(S5_CH, S5_N)
    bb_re_t, bb_im_t, pf_re, pf_im, pr_re, pr_im = _s5_params_fwd(ar, ai, ldt, br_t, bi_t)
    gpb = S5_GROUPS // S5_BLOCKS
    mask_b = (jnp.arange(128)[:, None] // S5_CH) == (jnp.arange(512)[None, :] // S5_P)

    def dense_b(bt_):
        blocks = bt_.reshape(S5_CH, S5_BLOCKS, 512).transpose(1, 0, 2)
        return jnp.where(mask_b, jnp.tile(blocks, (1, gpb, 1)), 0.0).astype(BF16)

    def dense_c(cc):
        blocks = cc[0].transpose(0, 2, 1).reshape(S5_BLOCKS, 512, S5_CH)
        return jnp.where(mask_b.T, jnp.tile(blocks, (1, 1, gpb)), 0.0).astype(BF16)

    bb_re, bb_im = dense_b(bb_re_t), dense_b(bb_im_t)
    cc_re, cc_im = dense_c(s5_c_re), dense_c(s5_c_im)
    s5d = s5_d.reshape(1, D_S5)
    ln1 = jnp.concatenate([ln1_g, ln1_b], axis=0)
    vec1 = _pad_rows(jnp.concatenate([b2, ln2_g, ln2_b], axis=0), 8)

    z, xbc_pre, xbc, dt_raw, u5 = _proj_conv_fwd(x2, mod3, w_in_pad, conv_w_full, conv_b, seq)
    yraw, ycat, hprev = _ssd_fwd(xbc, z, dt_raw, par, dsk, norm_w, seq)
    s_re, s_im, ypre, ycat = _s5_fwd(u5, bb_re, bb_im, cc_re, cc_im, pf_re, pf_im, s5d, w_glu_f, b_glu, ycat, seq)
    mix, x1 = _out_ln1(ycat, x2, mod3, w_out_f, ln1, seq)

    dx1, u2b, hb, dhpb, dob, gacc2, db1, bacc2 = _mlp_fwd_bwd(x1, tgt2, mod3, w1_blocks, w2_f, vec1, b1, seq)
    loss = lax.psum(0.5 / D_MODEL * jnp.sum(gacc2[3]), ("x", "y", "c"))

    dmixb, dxa, dyssd, dy5, gacc1, bacc1 = _ln1_out_bwd(dx1, x2, mix, mod3, w_out_f, ln1, seq)
    du5, vacc, sacc, d_cc, d_bb, g_wglu = _s5_bwd(dy5, ypre, u5, s_re, s_im, bb_re, bb_im, cc_re, cc_im,
                                                  pr_re, pr_im, s5d, w_glu_f, b_glu, seq)
    dxbc, dz, ddt, dpar, cacc = _ssd_bwd(dyssd, yraw, z, xbc, dt_raw, hprev, par, dsk, norm_w, seq)
    dpre, conv_acc = _conv_bwd(dxbc, xbc_pre, conv_w_full, conv_b, seq)
    grad_x2, ub, dxpb, bacc0 = _proj_bwd(dz, dpre, ddt, du5, x2, dxa, mod3, conv_w_full, w_in_pad, seq)

    g_w2 = _atb(hb, dob, "gw2")
    g_w1 = _atb(u2b, dhpb, "gw1")
    g_wout = _atb(ycat, dmixb, "gwout")
    g_win = jnp.concatenate([_atb(ub, dz, "gwin_z"), _atb(ub, dxpb, "gwin_xbc"),
                             _atb(ub, ddt, "gwin_dt")[:, :N_HEADS], _atb(ub, du5, "gwin_s5")], axis=1)

    def diag_b(dd):
        kept = jnp.where(mask_b, dd, 0.0).reshape(S5_BLOCKS, gpb, S5_CH, 512).sum(1)
        return kept.transpose(1, 0, 2).reshape(S5_CH, S5_N)

    def diag_c(dd):
        kept = jnp.where(mask_b.T, dd, 0.0).reshape(S5_BLOCKS, 512, gpb, S5_CH).sum(2)
        return kept.reshape(S5_GROUPS, S5_P, S5_CH).transpose(0, 2, 1)

    g_ar, g_ai, g_ldt, g_br_t, g_bi_t = _s5_params_bwd(ar, ai, ldt, br_t, bi_t, vacc[0:1], vacc[1:2],
                                                      diag_b(d_bb[:S5_BLOCKS]), diag_b(d_bb[S5_BLOCKS:]))

    def from_t(gt):
        return gt.reshape(S5_CH, S5_GROUPS, S5_P).transpose(1, 2, 0)

    small_g = dict(
        conv_w=conv_acc[0:4], conv_b=conv_acc[4:5], dt_bias=dpar[0:1, :N_HEADS], a_log=dpar[1:2, :N_HEADS],
        d_ssd=cacc[0].reshape(N_HEADS, HEADDIM).sum(1), norm_w=cacc[1:2],
        s5_a_re=g_ar, s5_a_im=g_ai, s5_log_dt=g_ldt[:, :S5_GROUPS], s5_b_re=from_t(g_br_t), s5_b_im=from_t(g_bi_t),
        s5_c_re=diag_c(d_cc[:S5_BLOCKS]), s5_c_im=diag_c(d_cc[S5_BLOCKS:]), s5_d=sacc[0:1], b_glu=sacc[1:2],
        ln1_g=gacc1[0:1], ln1_b=gacc1[1:2], b1=db1, b2=gacc2[2:3], ln2_g=gacc2[0:1], ln2_b=gacc2[1:2])

    dmod = jnp.concatenate([bacc0[:, 0], bacc0[:, 1], bacc1[:, 0], bacc2[:, 0], bacc2[:, 1], bacc2[:, 2]], axis=1)
    dmod_all = _all_gather([dmod], "gather_dmod")[0].reshape(N_DEV * nb, N_MOD * D_MODEL)
    dmod_cols = lax.dynamic_slice_in_dim(dmod_all, dev * ada_cols, ada_cols, axis=1)
    g_wada, g_bada = _mod_bwd(c_all, dmod_cols, dmod_all)

    in_cols = w_in.shape[2]
    big_g = dict(
        w_in=g_win.reshape(D_MODEL, N_DEV, in_cols).transpose(1, 0, 2),
        w_out=g_wout.reshape((N_DEV,) + w_out.shape[1:]), w1=g_w1,
        w2=g_w2.reshape((N_DEV,) + w2.shape[1:]), w_glu=g_wglu.reshape((N_DEV,) + w_glu.shape[1:]))
    by_dest = [big_g[n] if n == "w1" else big_g[n].reshape((4, 2) + big_g[n].shape[1:]) for n in big_names]
    from_sibling = _sibling_swap(by_dest, "rs_sibling_swap")
    core = lax.axis_index("c").astype(jnp.int32).reshape(1)
    chip_sums = [_add_halves(g, r, core, "rs_add_" + n) for g, r, n in zip(by_dest, from_sibling, big_names)]
    parts = _chip_all_to_all(chip_sums, "rs_chip_all_to_all")

    res = {k: {} for k in "gdmv"}
    for n, p in zip(big_names, parts):
        outs = _adamw(p, weights[n][0], mom[n][0], var[n][0], "adamw_" + n)
        for k, a in zip("gdmv", outs):
            res[k][n] = a[None]

    ag, ad, am, av = _adamw(g_wada[None], w_ada[0], m_w_ada[0], v_w_ada[0], "adamw_w_ada")
    for k, a in (("g", ag), ("d", ad), ("m", am), ("v", av)):
        res[k]["w_ada"] = a[None]
    bg_, bd_, bm_, bv_ = _adamw(g_bada.reshape(1, -1, 128), b_ada.reshape(-1, 128), m_b_ada.reshape(-1, 128),
                                v_b_ada.reshape(-1, 128), "adamw_b_ada")
    for k, a in (("g", bg_), ("d", bd_), ("m", bm_), ("v", bv_)):
        res[k]["b_ada"] = a.reshape(shapes["b_ada"])

    small_shapes = dict(shapes)
    small_shapes["conv_w"] = (1, 4, D_XBC)
    small_parts = _all_gather([_pack_small(small_g)], "gather_small_grads")[0]
    rep = {n: (jnp.zeros((1, 4, D_XBC), F32) if n == "conv_w" else weights[n]) for n in _SMALL}
    rep_m = {n: (jnp.zeros((1, 4, D_XBC), F32) if n == "conv_w" else mom[n]) for n in _SMALL}
    rep_v = {n: (jnp.ones((1, 4, D_XBC), F32) if n == "conv_w" else var[n]) for n in _SMALL}
    sg_, sd_, sm_, sv_ = _adamw(small_parts, _pack_small(rep), _pack_small(rep_m), _pack_small(rep_v), "adamw_small")
    for k, p in (("g", sg_), ("d", sd_), ("m", sm_), ("v", sv_)):
        un = _unpack_small(p, small_shapes)
        for n in _SMALL:
            if n != "conv_w":
                res[k][n] = un[n]
    g_conv_full = _unpack_small(sg_, small_shapes)["conv_w"][0]
    g_conv_mine = lax.dynamic_slice_in_dim(g_conv_full, dev * cw_cols, cw_cols, axis=1)
    cg_, cd_, cm_, cv_ = _adamw(g_conv_mine[None], conv_w[0], m_conv_w[0], v_conv_w[0], "adamw_conv_w")
    for k, a in (("g", cg_), ("d", cd_), ("m", cm_), ("v", cv_)):
        res[k]["conv_w"] = a[None]

    grad_x = grad_x2.reshape(nb, seq, D_MODEL)
    return (loss, grad_x, *[res["g"][n] for n in names], *[res["d"][n] for n in names],
            *[res["m"][n] for n in names], *[res["v"][n] for n in names])
```

```python
import functools
import math

import jax
import jax.numpy as jnp
from jax import lax
from jax.experimental import pallas as pl
from jax.experimental.pallas import tpu as pltpu

F32, BF16 = jnp.float32, jnp.bfloat16
MESH = pl.DeviceIdType.MESH
N_DEV = 8

D_MODEL = 1024
D_SSD = 1536
N_HEADS = 24
HEADDIM = 64
N_GROUPS = 4
HPG = 6
GW = HPG * HEADDIM
N_STATE = 128
CHUNK = 128
D_XBC = 2560
D_S5 = 512
S5_GROUPS = 32
S5_CH = 16
S5_P = 64
S5_N = S5_GROUPS * S5_P
D_IN = 4632
DT_PAD = 128
D_INP = D_SSD + D_XBC + DT_PAD + D_S5
D_FF = 4096
N_MOD = 6
ALPHA = 2.0 ** 0.25
EPS = 1e-5
LR, B1, B2, AEPS, WD, STEP = 0.001, 0.9, 0.999, 1e-08, 0.01, 10

NT = (((1,), (1,)), ((), ()))
TN = (((0,), (0,)), ((), ()))
ANY = pl.BlockSpec(memory_space=pl.ANY)
HIGHEST = lax.Precision.HIGHEST


def _mm(a, b):
    return jnp.dot(a.astype(BF16), b.astype(BF16), preferred_element_type=F32)


def _mm_nt(a, b):
    return lax.dot_general(a.astype(BF16), b.astype(BF16), NT, preferred_element_type=F32)


def _mm_tn(a, b):
    return lax.dot_general(a.astype(BF16), b.astype(BF16), TN, preferred_element_type=F32)


def _row_block(r, cap):
    best = r
    for cand in range(8, min(r, cap) + 1, 8):
        if r % cand == 0:
            best = cand
    return best if best <= cap else r


def _params(vmem_mb):
    return pltpu.CompilerParams(vmem_limit_bytes=vmem_mb << 20)


def _sigmoid(x):
    return 1.0 / (1.0 + jnp.exp(-x))


def _softplus(x):
    return jnp.maximum(x, 0.0) + jnp.log(1.0 + jnp.exp(-jnp.abs(x)))


_GK = math.sqrt(2.0 / math.pi)


def _gelu(x):
    return 0.5 * x * (1.0 + jnp.tanh(_GK * (x + 0.044715 * x * x * x)))


def _gelu_grad(x):
    t = jnp.tanh(_GK * (x + 0.044715 * x * x * x))
    return 0.5 * (1.0 + t) + 0.5 * x * (1.0 - t * t) * _GK * (1.0 + 3.0 * 0.044715 * x * x)


def _dev_index():
    return 4 * lax.axis_index("x") + 2 * lax.axis_index("y") + lax.axis_index("c")


def _all_gather(xs, name):
    n = len(xs)

    def body(*refs):
        x_refs, out_refs = refs[:n], refs[n:2 * n]
        send_sems, recv_sems, local_sems = refs[2 * n:]
        ix, iy, ic = lax.axis_index("x"), lax.axis_index("y"), lax.axis_index("c")
        me, sibling = (ix, iy, ic), (ix, iy, 1 - ic)
        chips = [(1 - ix, iy), (ix, 1 - iy), (1 - ix, 1 - iy)]

        def slot(a, px, py, pc):
            return out_refs[a].at[4 * px + 2 * py + pc]

        def copy(a, k, block, to, src=None):
            return pltpu.make_async_remote_copy(
                src_ref=slot(a, *block) if src is None else src, dst_ref=slot(a, *block),
                send_sem=send_sems.at[7 * a + k], recv_sem=recv_sems.at[7 * a + k], device_id=to, device_id_type=MESH)

        mine = [pltpu.make_async_copy(x_refs[a], slot(a, *me), local_sems.at[a]) for a in range(n)]
        for cp in mine:
            cp.start()
        first = []
        for j, chip in enumerate(chips):
            first += [copy(a, 1 + j, me, (*chip, ic), src=x_refs[a]) for a in range(n)]
        first += [copy(a, 0, me, sibling, src=x_refs[a]) for a in range(n)]
        for cp in first:
            cp.start()
        passed = []
        for j, chip in enumerate(chips):
            for a in range(n):
                copy(a, 1 + j, (*chip, ic), me).wait_recv()
                cp = copy(a, 4 + j, (*chip, ic), sibling)
                cp.start()
                passed.append(cp)
        for a in range(n):
            copy(a, 0, sibling, me).wait_recv()
            for j, chip in enumerate(chips):
                copy(a, 4 + j, (*chip, 1 - ic), me).wait_recv()
        for cp in first + passed:
            cp.wait_send()
        for cp in mine:
            cp.wait()

    return pl.pallas_call(
        body, name=name, out_shape=tuple(jax.ShapeDtypeStruct((N_DEV,) + x.shape, x.dtype) for x in xs),
        in_specs=[ANY] * n, out_specs=tuple([ANY] * n),
        scratch_shapes=[pltpu.SemaphoreType.DMA((7 * n,)), pltpu.SemaphoreType.DMA((7 * n,)),
                        pltpu.SemaphoreType.DMA((n,))],
    )(*xs)


def _sibling_swap(gs, name):
    n = len(gs)

    def body(*refs):
        g_refs, recv_refs = refs[:n], refs[n:2 * n]
        send_sems, recv_sems = refs[2 * n:]
        ix, iy, ic = lax.axis_index("x"), lax.axis_index("y"), lax.axis_index("c")

        def block(g_ref, q):
            if len(g_ref.shape) == 4:
                return g_ref.at[q, 1 - ic]
            cw = g_ref.shape[1] // N_DEV
            return g_ref.at[:, pl.ds(pl.multiple_of((2 * q + 1 - ic) * cw, 128), cw)]

        cps = []
        for a in range(n):
            for q in range(4):
                cps.append(pltpu.make_async_remote_copy(
                    src_ref=block(g_refs[a], q), dst_ref=recv_refs[a].at[q],
                    send_sem=send_sems.at[4 * a + q], recv_sem=recv_sems.at[4 * a + q],
                    device_id=(ix, iy, 1 - ic), device_id_type=MESH))
        for cp in cps:
            cp.start()
        for cp in cps:
            cp.wait()

    return pl.pallas_call(
        body, name=name,
        out_shape=tuple(jax.ShapeDtypeStruct(
            (4,) + (g.shape[2:] if g.ndim == 4 else (g.shape[0], g.shape[1] // N_DEV)), g.dtype) for g in gs),
        in_specs=[ANY] * n, out_specs=tuple([ANY] * n),
        scratch_shapes=[pltpu.SemaphoreType.DMA((4 * n,)), pltpu.SemaphoreType.DMA((4 * n,))],
    )(*gs)


def _chip_all_to_all(hs, name):
    n = len(hs)

    def body(*refs):
        h_refs, out_refs = refs[:n], refs[n:2 * n]
        send_sems, recv_sems, local_sems = refs[2 * n:]
        ix, iy, ic = lax.axis_index("x"), lax.axis_index("y"), lax.axis_index("c")
        me = 2 * ix + iy
        peers = [(1 - ix, iy), (ix, 1 - iy), (1 - ix, 1 - iy)]
        mine = [pltpu.make_async_copy(h_refs[a].at[me], out_refs[a].at[me], local_sems.at[a]) for a in range(n)]
        for cp in mine:
            cp.start()

        def copy(a, k, src_slot, dst_slot, peer):
            return pltpu.make_async_remote_copy(
                src_ref=h_refs[a].at[src_slot], dst_ref=out_refs[a].at[dst_slot],
                send_sem=send_sems.at[3 * a + k], recv_sem=recv_sems.at[3 * a + k],
                device_id=(*peer, ic), device_id_type=MESH)

        sends = [copy(a, k, 2 * px + py, me, (px, py)) for a in range(n) for k, (px, py) in enumerate(peers)]
        for cp in sends:
            cp.start()
        for a in range(n):
            for k, (px, py) in enumerate(peers):
                copy(a, k, 2 * px + py, 2 * px + py, (px, py)).wait_recv()
        for cp in sends:
            cp.wait_send()
        for cp in mine:
            cp.wait()

    return pl.pallas_call(
        body, name=name, out_shape=tuple(jax.ShapeDtypeStruct(h.shape, h.dtype) for h in hs),
        in_specs=[ANY] * n, out_specs=tuple([ANY] * n),
        scratch_shapes=[pltpu.SemaphoreType.DMA((3 * n,)), pltpu.SemaphoreType.DMA((3 * n,)),
                        pltpu.SemaphoreType.DMA((n,))],
    )(*hs)


def _add_halves(g, recv, core, name):
    _, r, c = recv.shape
    br = _row_block(r, 512)
    stacked = g.ndim == 4

    def body(core_ref, g_ref, r_ref, o_ref):
        o_ref[0] = ((g_ref[0, 0] if stacked else g_ref[...]) + r_ref[0]).astype(BF16)

    spec = pl.BlockSpec((1, br, c), lambda i, j, core_ref: (i, j, 0))
    if stacked:
        g_spec = pl.BlockSpec((1, 1, br, c), lambda i, j, core_ref: (i, core_ref[0], j, 0))
    else:
        g_spec = pl.BlockSpec((br, c), lambda i, j, core_ref: (j, 2 * i + core_ref[0]))
    return pl.pallas_call(
        body, name=name, out_shape=jax.ShapeDtypeStruct(recv.shape, BF16),
        grid_spec=pltpu.PrefetchScalarGridSpec(
            num_scalar_prefetch=1, grid=(4, r // br), in_specs=[g_spec, spec], out_specs=spec),
        compiler_params=_params(32),
    )(core, g, recv)


def _adamw(parts, w, m, v, name):
    n_parts, r, c = parts.shape
    br = _row_block(r, 512 if c <= 1024 else 256)

    def body(p_ref, w_ref, m_ref, v_ref, g_out, d_out, m_out, v_out):
        g = p_ref[0].astype(F32)
        for p in range(1, n_parts):
            g = g + p_ref[p].astype(F32)
        m2 = B1 * m_ref[...] + (1.0 - B1) * g
        v2 = B2 * v_ref[...] + (1.0 - B2) * (g * g)
        m_hat = m2 / (1.0 - B1 ** STEP)
        v_hat = v2 / (1.0 - B2 ** STEP)
        g_out[...] = g
        d_out[...] = -LR * (m_hat / (jnp.sqrt(v_hat) + AEPS) + WD * w_ref[...])
        m_out[...] = m2
        v_out[...] = v2

    spec = pl.BlockSpec((br, c), lambda i: (i, 0))
    out = jax.ShapeDtypeStruct((r, c), F32)
    return pl.pallas_call(
        body, name=name, out_shape=(out, out, out, out), grid=(r // br,),
        in_specs=[pl.BlockSpec((n_parts, br, c), lambda i: (0, i, 0)), spec, spec, spec],
        out_specs=(spec, spec, spec, spec), compiler_params=_params(40),
    )(parts, w, m, v)


def _atb(a, b, name, bt=512):
    t, k1 = a.shape
    k2 = b.shape[1]

    def pick(k):
        for cand in (1024, 768, 512, 384, 256, 128):
            if k % cand == 0:
                return cand
        return k

    b1, b2 = pick(k1), pick(k2)

    def body(a_ref, b_ref, o_ref):
        @pl.when(pl.program_id(2) == 0)
        def _():
            o_ref[...] = jnp.zeros_like(o_ref)
        o_ref[...] += _mm_tn(a_ref[...], b_ref[...])

    return pl.pallas_call(
        body, name=name, out_shape=jax.ShapeDtypeStruct((k1, k2), F32), grid=(k1 // b1, k2 // b2, t // bt),
        in_specs=[pl.BlockSpec((bt, b1), lambda i, j, k: (k, i)), pl.BlockSpec((bt, b2), lambda i, j, k: (k, j))],
        out_specs=pl.BlockSpec((b1, b2), lambda i, j, k: (i, j)), compiler_params=_params(40),
    )(a, b)


def _mod_fwd(c_all, w_ada, b_cols):
    def body(c_ref, w_ref, b_ref, o_ref):
        cc = c_ref[...]
        cond = cc * _sigmoid(cc)
        o_ref[...] = _mm(cond, w_ref[...]) + b_ref[...]

    return pl.pallas_call(body, name="mod_fwd", out_shape=jax.ShapeDtypeStruct((c_all.shape[0], w_ada.shape[1]), F32),
                          compiler_params=_params(32))(c_all, w_ada, b_cols)


def _mod_bwd(c_all, dmod_cols, dmod_all):
    def body(c_ref, dc_ref, da_ref, gw_ref, gb_ref):
        cc = c_ref[...]
        cond = cc * _sigmoid(cc)
        gw_ref[...] = _mm_tn(cond, dc_ref[...])
        gb_ref[...] = jnp.sum(da_ref[...], axis=0, keepdims=True)

    return pl.pallas_call(
        body, name="mod_bwd",
        out_shape=(jax.ShapeDtypeStruct((D_MODEL, dmod_cols.shape[1]), F32), jax.ShapeDtypeStruct((1, dmod_all.shape[1]), F32)),
        compiler_params=_params(32))(c_all, dmod_cols, dmod_all)


def _load_once(hbm_ref, vmem_ref, sem):
    @pl.when(pl.program_id(0) == 0)
    def _():
        cp = pltpu.make_async_copy(hbm_ref, vmem_ref, sem)
        cp.start()
        cp.wait()


def _conv_taps(win_ref, w, tb, cols):
    shifted = [win_ref[8 - j:8 - j + tb, cols] for j in range(4)]
    acc = w[3:4] * shifted[0]
    for j in (1, 2, 3):
        acc = acc + w[3 - j:4 - j] * shifted[j]
    return acc, shifted


def _proj_conv_fwd(x2, mod3, w_in_pad, conv_w, conv_b, seq):
    t = x2.shape[0]
    tb = 256
    npb = seq // tb
    cw = 512

    def body(x_ref, mod_ref, w_hbm, cw_ref, cb_ref, z_ref, pre_ref, xbc_ref, dt_ref, u5_ref, w_vmem, win, sem):
        _load_once(w_hbm, w_vmem, sem)
        first = (pl.program_id(0) % npb) == 0

        @pl.when(first)
        def _():
            win[0:8, :] = jnp.zeros((8, D_XBC), F32)

        @pl.when(jnp.logical_not(first))
        def _():
            win[0:8, :] = win[tb:tb + 8, :]

        m = mod_ref[0]
        u = (x_ref[...] * (1.0 + m[1:2]) + m[0:1]).astype(BF16)
        z_ref[...] = jnp.dot(u, w_vmem[:, 0:D_SSD], preferred_element_type=F32)
        dt_ref[...] = jnp.dot(u, w_vmem[:, D_SSD + D_XBC:D_SSD + D_XBC + DT_PAD], preferred_element_type=F32)
        u5_ref[...] = jnp.dot(u, w_vmem[:, D_SSD + D_XBC + DT_PAD:], preferred_element_type=F32)
        for k in range(D_XBC // cw):
            cols = slice(k * cw, (k + 1) * cw)
            pre_k = jnp.dot(u, w_vmem[:, D_SSD + k * cw:D_SSD + (k + 1) * cw], preferred_element_type=F32)
            win[8:8 + tb, cols] = pre_k
            pre_ref[:, cols] = pre_k
            conv, _ = _conv_taps(win, cw_ref[:, cols], tb, cols)
            conv = conv + cb_ref[:, cols]
            xbc_ref[:, cols] = conv * _sigmoid(conv)

    row = lambda w: pl.BlockSpec((tb, w), lambda i: (i, 0))
    return pl.pallas_call(
        body, name="proj_conv_fwd", grid=(t // tb,),
        out_shape=(jax.ShapeDtypeStruct((t, D_SSD), F32), jax.ShapeDtypeStruct((t, D_XBC), F32),
                   jax.ShapeDtypeStruct((t, D_XBC), F32), jax.ShapeDtypeStruct((t, DT_PAD), F32),
                   jax.ShapeDtypeStruct((t, D_S5), F32)),
        in_specs=[row(D_MODEL), pl.BlockSpec((1, N_MOD, D_MODEL), lambda i: (i // npb, 0, 0)), ANY,
                  pl.BlockSpec((4, D_XBC), lambda i: (0, 0)), pl.BlockSpec((1, D_XBC), lambda i: (0, 0))],
        out_specs=(row(D_SSD), row(D_XBC), row(D_XBC), row(DT_PAD), row(D_S5)),
        scratch_shapes=[pltpu.VMEM((D_MODEL, D_INP), BF16), pltpu.VMEM((tb + 8, D_XBC), F32), pltpu.SemaphoreType.DMA],
        compiler_params=_params(56),
    )(x2, mod3, w_in_pad, conv_w, conv_b)


N_PAIRS = N_HEADS // 2


def _split3(x):
    hi = x.astype(BF16)
    r = x - hi.astype(F32)
    mid = r.astype(BF16)
    lo = (r - mid.astype(F32)).astype(BF16)
    return hi, mid, lo


def _dot3(x, e, dims=(((1,), (0,)), ((), ()))):
    return sum(lax.dot_general(p, e, dims, preferred_element_type=F32) for p in _split3(x))


def _dot3_left(e, x, dims=(((1,), (0,)), ((), ()))):
    return sum(lax.dot_general(e, p, dims, preferred_element_type=F32) for p in _split3(x))


def _head_fold():
    return (jnp.arange(D_SSD)[:, None] // HEADDIM == jnp.arange(128)[None, :]).astype(BF16)


def _ssd_prep(dt_raw, par):
    dtb = par[0:1]
    a = -jnp.exp(par[1:2])
    dt = _softplus(dt_raw + dtb)
    adt = dt * a
    row = lax.broadcasted_iota(jnp.int32, (CHUNK, CHUNK), 0)
    col = lax.broadcasted_iota(jnp.int32, (CHUNK, CHUNK), 1)
    causal = row >= col
    tri = causal.astype(BF16)
    cs = _dot3_left(tri, adt)
    left = col < HEADDIM

    def lanes(v, h):
        return jnp.broadcast_to(v[:, h:h + 1], (CHUNK, 128))

    dt_c, cs_c, pair_cols = [], [], []
    for p in range(N_PAIRS):
        c0, c1 = lanes(cs, 2 * p), lanes(cs, 2 * p + 1)
        pair_cols.append(jnp.concatenate([c0, c1], axis=1))
        cs_c.append(jnp.where(left, c0, c1))
        dt_c.append(jnp.where(left, lanes(dt, 2 * p), lanes(dt, 2 * p + 1)))
    cs_c = jnp.concatenate(cs_c, axis=1)
    dt_c = jnp.concatenate(dt_c, axis=1)
    return dt, a, cs, cs.T, causal, tri, dt_c, jnp.exp(cs_c), jnp.exp(cs_c[CHUNK - 1:CHUNK, :] - cs_c), pair_cols


def _pair_decay(cols, cst, pair, causal2):
    rows = jnp.concatenate([jnp.broadcast_to(cst[2 * pair:2 * pair + 1, :], (CHUNK, CHUNK)),
                            jnp.broadcast_to(cst[2 * pair + 1:2 * pair + 2, :], (CHUNK, CHUNK))], axis=1)
    return jnp.exp(jnp.where(causal2, cols - rows, -jnp.inf))


def _stack_heads(xp, left):
    return jnp.concatenate([jnp.where(left, xp, 0.0), jnp.where(left, 0.0, xp)], axis=0).astype(BF16)


def _ssd_fwd(xbc, z, dt_raw, par, dsk, normw, seq):
    t = xbc.shape[0]
    nc = seq // CHUNK
    n_chunks = t // CHUNK

    def body(xbc_ref, z_ref, dt_ref, par_ref, dsk_ref, nw_ref, yraw_ref, ycat_ref, hprev_ref, h_ref):
        @pl.when(pl.program_id(0) % nc == 0)
        def _():
            h_ref[...] = jnp.zeros_like(h_ref)
        hprev_ref[0] = h_ref[...]
        _, _, cs, cst, causal, _, dt_c, ecs_c, w_c, pair_cols = _ssd_prep(dt_ref[...], par_ref[...])
        cs_last = cs[CHUNK - 1:CHUNK, :]
        causal2 = jnp.concatenate([causal, causal], axis=1)
        left = lax.broadcasted_iota(jnp.int32, (CHUNK, 128), 1) < HEADDIM
        x = xbc_ref[:, 0:D_SSD]
        xdt = x * dt_c
        amat = (w_c * xdt).astype(BF16)
        zz = z_ref[...]
        silu_z = zz * _sigmoid(zz)
        for g in range(N_GROUPS):
            gs = slice(g * GW, (g + 1) * GW)
            bg = xbc_ref[:, D_SSD + g * N_STATE:D_SSD + (g + 1) * N_STATE].astype(BF16)
            cg = xbc_ref[:, D_SSD + (N_GROUPS + g) * N_STATE:D_SSD + (N_GROUPS + g + 1) * N_STATE].astype(BF16)
            scores = lax.dot_general(cg, bg, NT, preferred_element_type=F32)
            scores2 = jnp.concatenate([scores, scores], axis=1)
            hg = h_ref[gs, :]
            p_all = lax.dot_general(cg, hg.astype(BF16), NT, preferred_element_type=F32)
            ys = []
            for q in range(GW // 128):
                pair = g * (GW // 128) + q
                decay = _pair_decay(pair_cols[pair], cst, pair, causal2)
                mcat = (scores2 * decay).astype(BF16)
                ys.append(jnp.dot(mcat, _stack_heads(xdt[:, pair * 128:(pair + 1) * 128], left),
                                  preferred_element_type=F32))
            yg = jnp.concatenate(ys, axis=1) + ecs_c[:, gs] * p_all + x[:, gs] * dsk_ref[:, gs]
            s_new = lax.dot_general(amat[:, gs], bg, TN, preferred_element_type=F32)
            for j in range(HPG):
                hh = g * HPG + j
                js = slice(j * HEADDIM, (j + 1) * HEADDIM)
                h_ref[g * GW + j * HEADDIM:g * GW + (j + 1) * HEADDIM, :] = (
                    hg[js, :] * jnp.exp(cs_last[:, hh:hh + 1]) + s_new[js, :])
            yraw_ref[:, gs] = yg
            v = yg * silu_z[:, gs]
            r = lax.rsqrt(jnp.mean(v * v, axis=-1, keepdims=True) + EPS)
            ycat_ref[:, gs] = (v * r * nw_ref[:, gs]).astype(BF16)

    row = lambda w: pl.BlockSpec((CHUNK, w), lambda i: (i, 0))
    full = lambda s: pl.BlockSpec(s, lambda i: (0,) * len(s))
    return pl.pallas_call(
        body, name="ssd_fwd", grid=(n_chunks,),
        out_shape=(jax.ShapeDtypeStruct((t, D_SSD), F32), jax.ShapeDtypeStruct((t, D_SSD + D_S5), BF16),
                   jax.ShapeDtypeStruct((n_chunks, D_SSD, N_STATE), F32)),
        in_specs=[row(D_XBC), row(D_SSD), row(DT_PAD), full((8, 128)), full((1, D_SSD)), full((1, D_SSD))],
        out_specs=(row(D_SSD), row(D_SSD), pl.BlockSpec((1, D_SSD, N_STATE), lambda i: (i, 0, 0))),
        scratch_shapes=[pltpu.VMEM((D_SSD, N_STATE), F32)],
        compiler_params=_params(40),
    )(xbc, z, dt_raw, par, dsk, normw)


S5_CW = 512
S5_BLOCKS = 4


def _tile_scan(in_re, in_im, out_re, out_im, carry_re, carry_im, pw_re, pw_im, n_tiles, reverse):
    steps = (1, 2, 4)
    for cc in range(S5_N // S5_CW):
        cols = slice(cc * S5_CW, (cc + 1) * S5_CW)
        a_re, a_im = pw_re[:, cols], pw_im[:, cols]
        rid = lax.broadcasted_iota(jnp.int32, (8, S5_CW), 0)
        pows = []
        for d in steps:
            k = 8 - d if reverse else d - 1
            keep = (rid < 8 - d) if reverse else (rid >= d)
            pows.append((jnp.where(keep, pw_re[k:k + 1, cols], 0.0), jnp.where(keep, pw_im[k:k + 1, cols], 0.0)))

        def tile(i, carry, cols=cols, pows=pows, a_re=a_re, a_im=a_im):
            r = (n_tiles - 1 - i) if reverse else i
            rows = pl.ds(pl.multiple_of(r * 8, 8), 8)
            xr, xi = in_re[rows, cols], in_im[rows, cols]
            for (pr, pi), d in zip(pows, steps):
                shift = 8 - d if reverse else d
                sr, si = pltpu.roll(xr, shift, axis=0), pltpu.roll(xi, shift, axis=0)
                xr, xi = xr + pr * sr - pi * si, xi + pr * si + pi * sr
            cr, ci = carry
            xr, xi = xr + a_re * cr - a_im * ci, xi + a_re * ci + a_im * cr
            out_re[rows, cols] = xr
            out_im[rows, cols] = xi
            edge = slice(0, 1) if reverse else slice(7, 8)
            return (jnp.broadcast_to(xr[edge], (8, S5_CW)), jnp.broadcast_to(xi[edge], (8, S5_CW)))

        c0 = (jnp.broadcast_to(carry_re[0:1, cols], (8, S5_CW)), jnp.broadcast_to(carry_im[0:1, cols], (8, S5_CW)))
        cr, ci = lax.fori_loop(0, n_tiles, tile, c0)
        carry_re[:, cols] = cr
        carry_im[:, cols] = ci


def _s5_params_math(ar, ai, ldt, br, bi):
    dt = jnp.exp(ldt)
    mag = jnp.exp(ar * dt)
    ang = ai * dt
    ab_re = mag * jnp.cos(ang)
    ab_im = mag * jnp.sin(ang)
    den = ar * ar + ai * ai
    n_re = ab_re - 1.0
    coef_re = (n_re * ar + ab_im * ai) / den
    coef_im = (ab_im * ar - n_re * ai) / den
    bb_re = coef_re * br - coef_im * bi
    bb_im = coef_re * bi + coef_im * br
    return ab_re, ab_im, bb_re, bb_im


def _s5_params_fwd(ar, ai, ldt, br, bi):
    def body(ar_ref, ai_ref, ldt_ref, br_ref, bi_ref, bbr_ref, bbi_ref, pfr_ref, pfi_ref, prr_ref, pri_ref):
        ab_re, ab_im, bb_re, bb_im = _s5_params_math(ar_ref[...], ai_ref[...], ldt_ref[...], br_ref[...], bi_ref[...])
        bbr_ref[...] = bb_re
        bbi_ref[...] = bb_im
        pr, pi = ab_re, ab_im
        for k in range(8):
            pfr_ref[k:k + 1, :] = pr
            pfi_ref[k:k + 1, :] = pi
            prr_ref[7 - k:8 - k, :] = pr
            pri_ref[7 - k:8 - k, :] = -pi
            pr, pi = pr * ab_re - pi * ab_im, pr * ab_im + pi * ab_re

    b16 = jax.ShapeDtypeStruct((S5_CH, S5_N), F32)
    p8 = jax.ShapeDtypeStruct((8, S5_N), F32)
    return pl.pallas_call(body, name="s5_params_fwd", out_shape=(b16, b16, p8, p8, p8, p8),
                          compiler_params=_params(32))(ar, ai, ldt, br, bi)


def _s5_params_bwd(ar, ai, ldt, br, bi, d_ab_re, d_ab_im, d_bb_re, d_bb_im):
    def body(ar_ref, ai_ref, ldt_ref, br_ref, bi_ref, dar_ref, dai_ref, dbr_ref, dbi_ref,
             gar_ref, gai_ref, gldt_ref, gbr_ref, gbi_ref):
        _, vjp = jax.vjp(_s5_params_math, ar_ref[...], ai_ref[...], ldt_ref[...], br_ref[...], bi_ref[...])
        g_ar, g_ai, g_ldt, g_br, g_bi = vjp((dar_ref[...], dai_ref[...], dbr_ref[...], dbi_ref[...]))
        gar_ref[...] = g_ar
        gai_ref[...] = g_ai
        gbr_ref[...] = g_br
        gbi_ref[...] = g_bi
        lane = lax.broadcasted_iota(jnp.int32, (S5_N, 128), 0) // S5_P
        grp = lax.broadcasted_iota(jnp.int32, (S5_N, 128), 1)
        fold = (lane == grp).astype(F32)
        gldt_ref[...] = jnp.dot(g_ldt, fold, preferred_element_type=F32, precision=HIGHEST)

    v1 = jax.ShapeDtypeStruct((1, S5_N), F32)
    b16 = jax.ShapeDtypeStruct((S5_CH, S5_N), F32)
    return pl.pallas_call(body, name="s5_params_bwd",
                          out_shape=(v1, v1, jax.ShapeDtypeStruct((1, 128), F32), b16, b16),
                          compiler_params=_params(32))(ar, ai, ldt, br, bi, d_ab_re, d_ab_im, d_bb_re, d_bb_im)


def _s5_fwd(u5, bb_re, bb_im, cc_re, cc_im, pf_re, pf_im, s5d, w_glu, b_glu, ycat, seq):
    t = u5.shape[0]
    tb = 256
    npb = seq // tb

    def body(u_ref, bbr_ref, bbi_ref, ccr_ref, cci_ref, pfr_ref, pfi_ref, d_ref, wg_ref, bg_ref, ycat_hbm,
             sre_ref, sim_ref, ypre_ref, y5_ref, bur, bui, car, cai):
        del ycat_hbm

        @pl.when(pl.program_id(0) % npb == 0)
        def _():
            car[...] = jnp.zeros_like(car)
            cai[...] = jnp.zeros_like(cai)
        u = u_ref[...]
        ub = u.astype(BF16)
        for j in range(S5_BLOCKS):
            ch, st = slice(j * 128, (j + 1) * 128), slice(j * 512, (j + 1) * 512)
            bur[:, st] = jnp.dot(ub[:, ch], bbr_ref[j], preferred_element_type=F32)
            bui[:, st] = jnp.dot(ub[:, ch], bbi_ref[j], preferred_element_type=F32)
        _tile_scan(bur, bui, sre_ref, sim_ref, car, cai, pfr_ref, pfi_ref, tb // 8, reverse=False)
        cs_y = []
        for j in range(S5_BLOCKS):
            st = slice(j * 512, (j + 1) * 512)
            cs_y.append(_mm(sre_ref[:, st], ccr_ref[j]) - _mm(sim_ref[:, st], cci_ref[j]))
        ypre = jnp.concatenate(cs_y, axis=1) + u * d_ref[...]
        ypre_ref[...] = ypre
        yg = _gelu(ypre)
        y5_ref[...] = (yg * _sigmoid(_mm(yg, wg_ref[...]) + bg_ref[...])).astype(BF16)

    row = lambda w: pl.BlockSpec((tb, w), lambda i: (i, 0))
    full = lambda a: pl.BlockSpec(a.shape, lambda i: (0,) * a.ndim)
    return pl.pallas_call(
        body, name="s5_fwd", grid=(t // tb,),
        out_shape=(jax.ShapeDtypeStruct((t, S5_N), F32), jax.ShapeDtypeStruct((t, S5_N), F32),
                   jax.ShapeDtypeStruct((t, D_S5), F32), jax.ShapeDtypeStruct(ycat.shape, BF16)),
        in_specs=[row(D_S5), full(bb_re), full(bb_im), full(cc_re), full(cc_im), full(pf_re), full(pf_im),
                  full(s5d), full(w_glu), full(b_glu), ANY],
        out_specs=(row(S5_N), row(S5_N), row(D_S5), pl.BlockSpec((tb, D_S5), lambda i: (i, D_SSD // D_S5))),
        input_output_aliases={10: 3},
        scratch_shapes=[pltpu.VMEM((tb, S5_N), F32), pltpu.VMEM((tb, S5_N), F32),
                        pltpu.VMEM((8, S5_N), F32), pltpu.VMEM((8, S5_N), F32)],
        compiler_params=_params(48),
    )(u5, bb_re, bb_im, cc_re, cc_im, pf_re, pf_im, s5d, w_glu, b_glu, ycat)


def _layer_norm(r, g, b):
    mu = jnp.mean(r, axis=-1, keepdims=True)
    xc = r - mu
    rstd = lax.rsqrt(jnp.mean(xc * xc, axis=-1, keepdims=True) + EPS)
    xhat = xc * rstd
    return xhat * g + b, xhat, rstd


def _layer_norm_bwd(dy, xhat, rstd, g):
    dxhat = dy * g
    return rstd * (dxhat - jnp.mean(dxhat, axis=-1, keepdims=True)
                   - xhat * jnp.mean(dxhat * xhat, axis=-1, keepdims=True))


def _out_ln1(ycat, x2, mod3, w_out, ln1, seq):
    t = x2.shape[0]
    tb = 512
    npb = seq // tb

    def body(y_ref, x_ref, mod_ref, w_ref, ln_ref, mix_ref, x1_ref):
        m = mod_ref[0]
        mix = jnp.dot(y_ref[...], w_ref[...], preferred_element_type=F32)
        mix_ref[...] = mix
        r1 = ALPHA * x_ref[...] + (1.0 + m[2:3]) * mix
        x1_ref[...] = _layer_norm(r1, ln_ref[0:1], ln_ref[1:2])[0]

    row = lambda w: pl.BlockSpec((tb, w), lambda i: (i, 0))
    return pl.pallas_call(
        body, name="out_ln1", grid=(t // tb,),
        out_shape=(jax.ShapeDtypeStruct((t, D_MODEL), F32), jax.ShapeDtypeStruct((t, D_MODEL), F32)),
        in_specs=[row(D_SSD + D_S5), row(D_MODEL), pl.BlockSpec((1, N_MOD, D_MODEL), lambda i: (i // npb, 0, 0)),
                  pl.BlockSpec(w_out.shape, lambda i: (0, 0)), pl.BlockSpec(ln1.shape, lambda i: (0, 0))],
        out_specs=(row(D_MODEL), row(D_MODEL)), compiler_params=_params(48),
    )(ycat, x2, mod3, w_out, ln1)


def _mlp_fwd_bwd(x1, tgt, mod3, w1, w2, vec1, b1, seq):
    t = x1.shape[0]
    tb = 256
    npb = seq // tb
    n_fb, _, fb = w1.shape

    def body(x1_ref, tgt_ref, mod_ref, w1_hbm, w2_hbm, v_ref, b1_ref,
             dx1_ref, u2_ref, h_ref, dhp_ref, do_ref, gacc_ref, db1_ref, bacc_ref, w1_v, w2_v, sem1, sem2):
        i = pl.program_id(0)
        @pl.when(i == 0)
        def _():
            cps = [pltpu.make_async_copy(w1_hbm.at[k], w1_v.at[:, k * fb:(k + 1) * fb], sem1.at[k])
                   for k in range(n_fb)]
            for cp in cps:
                cp.start()
            for cp in cps:
                cp.wait()
        _load_once(w2_hbm, w2_v, sem2)

        @pl.when(i == 0)
        def _():
            gacc_ref[...] = jnp.zeros_like(gacc_ref)
            db1_ref[...] = jnp.zeros_like(db1_ref)

        @pl.when(i % npb == 0)
        def _():
            bacc_ref[...] = jnp.zeros_like(bacc_ref)

        m = mod_ref[0]
        sh2, sc2, g2 = m[3:4], m[4:5], m[5:6]
        x1v = x1_ref[...]
        u2 = (x1v * (1.0 + sc2) + sh2).astype(BF16)
        u2_ref[...] = u2
        hr = jnp.maximum(jnp.dot(u2, w1_v[...], preferred_element_type=F32) + b1_ref[...], 0.0)
        hb = (hr * hr).astype(BF16)
        h_ref[...] = hb
        o = jnp.dot(hb, w2_v[...], preferred_element_type=F32) + v_ref[0:1]
        r2 = ALPHA * x1v + (1.0 + g2) * o
        y, xhat, rstd = _layer_norm(r2, v_ref[1:2], v_ref[2:3])
        err = y - tgt_ref[...]
        dy = err * (1.0 / D_MODEL)
        dr2 = _layer_norm_bwd(dy, xhat, rstd, v_ref[1:2])
        do = (1.0 + g2) * dr2
        dob = do.astype(BF16)
        do_ref[...] = dob
        gacc_ref[0:1, :] += jnp.sum(dy * xhat, axis=0, keepdims=True)
        gacc_ref[1:2, :] += jnp.sum(dy, axis=0, keepdims=True)
        gacc_ref[2:3, :] += jnp.sum(do, axis=0, keepdims=True)
        gacc_ref[3:4, :] += jnp.sum(err * err, axis=0, keepdims=True)
        dhpre = lax.dot_general(dob, w2_v[...], NT, preferred_element_type=F32) * (2.0 * hr)
        dhpb = dhpre.astype(BF16)
        dhp_ref[...] = dhpb
        db1_ref[...] += jnp.sum(dhpre, axis=0, keepdims=True)
        du2 = lax.dot_general(dhpb, w1_v[...], NT, preferred_element_type=F32)
        dx1_ref[...] = ALPHA * dr2 + du2 * (1.0 + sc2)
        bacc_ref[0, 0:1, :] += jnp.sum(du2, axis=0, keepdims=True)
        bacc_ref[0, 1:2, :] += jnp.sum(du2 * x1v, axis=0, keepdims=True)
        bacc_ref[0, 2:3, :] += jnp.sum(dr2 * o, axis=0, keepdims=True)

    row = lambda w: pl.BlockSpec((tb, w), lambda i: (i, 0))
    return pl.pallas_call(
        body, name="mlp_fwd_bwd", grid=(t // tb,),
        out_shape=(jax.ShapeDtypeStruct((t, D_MODEL), F32), jax.ShapeDtypeStruct((t, D_MODEL), BF16),
                   jax.ShapeDtypeStruct((t, D_FF), BF16), jax.ShapeDtypeStruct((t, D_FF), BF16),
                   jax.ShapeDtypeStruct((t, D_MODEL), BF16), jax.ShapeDtypeStruct((8, D_MODEL), F32),
                   jax.ShapeDtypeStruct((1, D_FF), F32), jax.ShapeDtypeStruct((t // seq, 8, D_MODEL), F32)),
        in_specs=[row(D_MODEL), row(D_MODEL), pl.BlockSpec((1, N_MOD, D_MODEL), lambda i: (i // npb, 0, 0)), ANY, ANY,
                  pl.BlockSpec(vec1.shape, lambda i: (0, 0)), pl.BlockSpec(b1.shape, lambda i: (0, 0))],
        out_specs=(row(D_MODEL), row(D_MODEL), row(D_FF), row(D_FF), row(D_MODEL),
                   pl.BlockSpec((8, D_MODEL), lambda i: (0, 0)), pl.BlockSpec((1, D_FF), lambda i: (0, 0)),
                   pl.BlockSpec((1, 8, D_MODEL), lambda i: (i // npb, 0, 0))),
        scratch_shapes=[pltpu.VMEM((D_MODEL, n_fb * fb), BF16), pltpu.VMEM((D_FF, D_MODEL), BF16),
                        pltpu.SemaphoreType.DMA((n_fb,)), pltpu.SemaphoreType.DMA],
        compiler_params=_params(60),
    )(x1, tgt, mod3, w1, w2, vec1, b1)


def _ln1_out_bwd(dx1, x2, mix, mod3, w_out, ln1, seq):
    t = x2.shape[0]
    tb = 512
    npb = seq // tb

    def body(dx1_ref, x_ref, mix_ref, mod_ref, w_ref, ln_ref, dmix_ref, dxa_ref, dys_ref, dy5_ref, gacc_ref, bacc_ref):
        i = pl.program_id(0)

        @pl.when(i == 0)
        def _():
            gacc_ref[...] = jnp.zeros_like(gacc_ref)

        @pl.when(i % npb == 0)
        def _():
            bacc_ref[...] = jnp.zeros_like(bacc_ref)

        m = mod_ref[0]
        mix = mix_ref[...]
        r1 = ALPHA * x_ref[...] + (1.0 + m[2:3]) * mix
        _, xhat, rstd = _layer_norm(r1, ln_ref[0:1], ln_ref[1:2])
        dx1v = dx1_ref[...]
        dr1 = _layer_norm_bwd(dx1v, xhat, rstd, ln_ref[0:1])
        gacc_ref[0:1, :] += jnp.sum(dx1v * xhat, axis=0, keepdims=True)
        gacc_ref[1:2, :] += jnp.sum(dx1v, axis=0, keepdims=True)
        bacc_ref[0, 0:1, :] += jnp.sum(dr1 * mix, axis=0, keepdims=True)
        dmix = ((1.0 + m[2:3]) * dr1).astype(BF16)
        dmix_ref[...] = dmix
        dxa_ref[...] = ALPHA * dr1
        dys_ref[...] = lax.dot_general(dmix, w_ref[0:D_SSD, :], NT, preferred_element_type=F32)
        dy5_ref[...] = lax.dot_general(dmix, w_ref[D_SSD:, :], NT, preferred_element_type=F32)

    row = lambda w: pl.BlockSpec((tb, w), lambda i: (i, 0))
    return pl.pallas_call(
        body, name="ln1_out_bwd", grid=(t // tb,),
        out_shape=(jax.ShapeDtypeStruct((t, D_MODEL), BF16), jax.ShapeDtypeStruct((t, D_MODEL), F32),
                   jax.ShapeDtypeStruct((t, D_SSD), F32), jax.ShapeDtypeStruct((t, D_S5), F32),
                   jax.ShapeDtypeStruct((8, D_MODEL), F32), jax.ShapeDtypeStruct((t // seq, 8, D_MODEL), F32)),
        in_specs=[row(D_MODEL), row(D_MODEL), row(D_MODEL), pl.BlockSpec((1, N_MOD, D_MODEL), lambda i: (i // npb, 0, 0)),
                  pl.BlockSpec(w_out.shape, lambda i: (0, 0)), pl.BlockSpec(ln1.shape, lambda i: (0, 0))],
        out_specs=(row(D_MODEL), row(D_MODEL), row(D_SSD), row(D_S5), pl.BlockSpec((8, D_MODEL), lambda i: (0, 0)),
                   pl.BlockSpec((1, 8, D_MODEL), lambda i: (i // npb, 0, 0))),
        compiler_params=_params(48),
    )(dx1, x2, mix, mod3, w_out, ln1)


def _s5_bwd(dy5, ypre, u5, s_re, s_im, bb_re, bb_im, cc_re, cc_im, pr_re, pr_im, s5d, w_glu, b_glu, seq):
    t = u5.shape[0]
    tb = 256
    npb = seq // tb
    n_blocks = t // tb

    def blk(i):
        return (i // npb) * npb + (npb - 1 - i % npb)

    def body(dy_ref, ypre_ref, u_ref, sre_ref, sim_ref, hre_ref, him_ref, bbr_ref, bbi_ref, ccr_ref, cci_ref,
             prr_ref, pri_ref, d_ref, wg_ref, bg_ref,
             du_ref, vacc_ref, sacc_ref, dcc_ref, dbb_ref, dwg_ref, dsr, dsi, gr, gi, car, cai):
        i = pl.program_id(0)

        @pl.when(i == 0)
        def _():
            for acc in (vacc_ref, sacc_ref, dcc_ref, dbb_ref, dwg_ref):
                acc[...] = jnp.zeros_like(acc)

        @pl.when(i % npb == 0)
        def _():
            car[...] = jnp.zeros_like(car)
            cai[...] = jnp.zeros_like(cai)

        dy = dy_ref[...]
        ypre = ypre_ref[...]
        u = u_ref[...]
        ub = u.astype(BF16)
        yg = _gelu(ypre)
        sg = _sigmoid(_mm(yg, wg_ref[...]) + bg_ref[...])
        dq = dy * yg * sg * (1.0 - sg)
        dqb = dq.astype(BF16)
        dyg = dy * sg + lax.dot_general(dqb, wg_ref[...], NT, preferred_element_type=F32)
        dyp = dyg * _gelu_grad(ypre)
        dypb = dyp.astype(BF16)
        dwg_ref[...] += lax.dot_general(yg.astype(BF16), dqb, TN, preferred_element_type=F32)
        blocks = [(slice(j * 128, (j + 1) * 128), slice(j * 512, (j + 1) * 512)) for j in range(S5_BLOCKS)]
        for j, (ch, st) in enumerate(blocks):
            dsr[:, st] = lax.dot_general(dypb[:, ch], ccr_ref[j], NT, preferred_element_type=F32)
            dsi[:, st] = -lax.dot_general(dypb[:, ch], cci_ref[j], NT, preferred_element_type=F32)
        _tile_scan(dsr, dsi, gr, gi, car, cai, prr_ref, pri_ref, tb // 8, reverse=True)
        g_re, g_im = gr[...], gi[...]
        first_rows = (i % npb) == npb - 1
        hre = jnp.where(first_rows, 0.0, hre_ref[...])
        him = jnp.where(first_rows, 0.0, him_ref[...])
        s_re_v, s_im_v = sre_ref[...], sim_ref[...]
        sp_re = pltpu.roll(jnp.concatenate([hre, s_re_v], axis=0), 1, axis=0)[8:8 + tb]
        sp_im = pltpu.roll(jnp.concatenate([him, s_im_v], axis=0), 1, axis=0)[8:8 + tb]
        vacc_ref[0:1, :] += jnp.sum(g_re * sp_re + g_im * sp_im, axis=0, keepdims=True)
        vacc_ref[1:2, :] += jnp.sum(g_im * sp_re - g_re * sp_im, axis=0, keepdims=True)
        grb, gib = g_re.astype(BF16), g_im.astype(BF16)
        srb, sib = s_re_v.astype(BF16), s_im_v.astype(BF16)
        du_cols = []
        for j, (ch, st) in enumerate(blocks):
            dcc_ref[j] += lax.dot_general(srb[:, st], dypb[:, ch], TN, preferred_element_type=F32)
            dcc_ref[S5_BLOCKS + j] -= lax.dot_general(sib[:, st], dypb[:, ch], TN, preferred_element_type=F32)
            dbb_ref[j] += lax.dot_general(ub[:, ch], grb[:, st], TN, preferred_element_type=F32)
            dbb_ref[S5_BLOCKS + j] += lax.dot_general(ub[:, ch], gib[:, st], TN, preferred_element_type=F32)
            du_cols.append(lax.dot_general(grb[:, st], bbr_ref[j], NT, preferred_element_type=F32)
                           + lax.dot_general(gib[:, st], bbi_ref[j], NT, preferred_element_type=F32))
        du_ref[...] = jnp.concatenate(du_cols, axis=1) + dyp * d_ref[...]
        sacc_ref[0:1, :] += jnp.sum(dyp * u, axis=0, keepdims=True)
        sacc_ref[1:2, :] += jnp.sum(dq, axis=0, keepdims=True)

    row = lambda w: pl.BlockSpec((tb, w), lambda i: (blk(i), 0))
    halo = pl.BlockSpec((8, S5_N), lambda i: (jnp.maximum(blk(i) * (tb // 8) - 1, 0), 0))
    full = lambda a: pl.BlockSpec(a.shape, lambda i: (0,) * a.ndim)
    acc = lambda s: pl.BlockSpec(s, lambda i: (0,) * len(s))
    acc_shapes = [(8, S5_N), (8, D_S5), (2 * S5_BLOCKS, 512, 128), (2 * S5_BLOCKS, 128, 512), (D_S5, D_S5)]
    return pl.pallas_call(
        body, name="s5_bwd", grid=(n_blocks,),
        out_shape=(jax.ShapeDtypeStruct((t, D_S5), F32),) + tuple(jax.ShapeDtypeStruct(s, F32) for s in acc_shapes),
        in_specs=[row(D_S5), row(D_S5), row(D_S5), row(S5_N), row(S5_N), halo, halo, full(bb_re), full(bb_im),
                  full(cc_re), full(cc_im), full(pr_re), full(pr_im), full(s5d), full(w_glu), full(b_glu)],
        out_specs=(row(D_S5),) + tuple(acc(s) for s in acc_shapes),
        scratch_shapes=[pltpu.VMEM((tb, S5_N), F32), pltpu.VMEM((tb, S5_N), F32), pltpu.VMEM((tb, S5_N), F32),
                        pltpu.VMEM((tb, S5_N), F32), pltpu.VMEM((8, S5_N), F32), pltpu.VMEM((8, S5_N), F32)],
        compiler_params=_params(56),
    )(dy5, ypre, u5, s_re, s_im, s_re, s_im, bb_re, bb_im, cc_re, cc_im, pr_re, pr_im, s5d, w_glu, b_glu)


def _ssd_bwd(dyssd, yraw, z, xbc, dt_raw, hprev, par, dsk, normw, seq):
    t = xbc.shape[0]
    nc = seq // CHUNK
    n_chunks = t // CHUNK
    fold = _head_fold()

    def blk(i):
        return (i // nc) * nc + (nc - 1 - i % nc)

    def body(dy_ref, yraw_ref, z_ref, xbc_ref, dt_ref, hprev_ref, par_ref, dsk_ref, nw_ref, fold_ref,
             dxbc_ref, dz_ref, ddt_ref, dpar_ref, cacc_ref, dh_ref, dyr_ref):
        i = pl.program_id(0)

        @pl.when(i == 0)
        def _():
            dpar_ref[...] = jnp.zeros_like(dpar_ref)
            cacc_ref[...] = jnp.zeros_like(cacc_ref)

        @pl.when(i % nc == 0)
        def _():
            dh_ref[...] = jnp.zeros_like(dh_ref)

        zz = z_ref[...]
        sz = _sigmoid(zz)
        silu_z = zz * sz
        yraw = yraw_ref[...]
        for g in range(N_GROUPS):
            sl = slice(g * GW, (g + 1) * GW)
            v = yraw[:, sl] * silu_z[:, sl]
            r = lax.rsqrt(jnp.mean(v * v, axis=-1, keepdims=True) + EPS)
            dyg = dy_ref[:, sl]
            cacc_ref[1:2, sl] += jnp.sum(dyg * v * r, axis=0, keepdims=True)
            dyw = dyg * nw_ref[:, sl]
            dv = r * dyw - v * (r * r * r) * jnp.mean(dyw * v, axis=-1, keepdims=True)
            dyr_ref[:, sl] = dv * silu_z[:, sl]
            dz_ref[:, sl] = dv * yraw[:, sl] * (sz[:, sl] * (1.0 + zz[:, sl] * (1.0 - sz[:, sl])))

        dt, a, cs, cst, causal, tri, dt_c, ecs_c, w_c, pair_cols = _ssd_prep(dt_ref[...], par_ref[...])
        cs_last = cs[CHUNK - 1:CHUNK, :]
        causal2 = jnp.concatenate([causal, causal], axis=1)
        lane = lax.broadcasted_iota(jnp.int32, (CHUNK, 128), 1)
        left = lane < HEADDIM
        lane1 = lax.broadcasted_iota(jnp.int32, (1, 128), 1)
        x = xbc_ref[:, 0:D_SSD]
        xdt = x * dt_c
        dyr = dyr_ref[...]
        dyrb = dyr.astype(BF16)
        cacc_ref[0:1, :] += jnp.sum(dyr * x, axis=0, keepdims=True)
        dlast = jnp.zeros((1, 128), F32)
        dxdt_cols, diag_all, dww_cols = [], [], []
        for g in range(N_GROUPS):
            gs = slice(g * GW, (g + 1) * GW)
            b_sl = slice(D_SSD + g * N_STATE, D_SSD + (g + 1) * N_STATE)
            c_sl = slice(D_SSD + (N_GROUPS + g) * N_STATE, D_SSD + (N_GROUPS + g + 1) * N_STATE)
            bg = xbc_ref[:, b_sl].astype(BF16)
            cg = xbc_ref[:, c_sl].astype(BF16)
            scores = lax.dot_general(cg, bg, NT, preferred_element_type=F32)
            scores2 = jnp.concatenate([scores, scores], axis=1)
            hg = hprev_ref[0, gs, :]
            hgb = hg.astype(BF16)
            dhg = dh_ref[gs, :]
            dhgb = dhg.astype(BF16)
            q_all = lax.dot_general(bg, dhgb, NT, preferred_element_type=F32)
            dscores = jnp.zeros((CHUNK, CHUNK), F32)
            diag_cols = []
            for q in range(GW // 128):
                pair = g * (GW // 128) + q
                ps = slice(pair * 128, (pair + 1) * 128)
                decay = _pair_decay(pair_cols[pair], cst, pair, causal2)
                mcat = (scores2 * decay).astype(BF16)
                dyp = dyrb[:, ps]
                dm = lax.dot_general(dyp, _stack_heads(xdt[:, ps], left), NT, preferred_element_type=F32)
                dmd = dm * decay
                dscores = dscores + dmd[:, 0:CHUNK] + dmd[:, CHUNK:]
                rr = lax.dot_general(mcat, dyp, TN, preferred_element_type=F32)
                diag_cols.append(jnp.where(left, rr[0:CHUNK], rr[CHUNK:]))
            wq = w_c[:, gs] * q_all
            diag_g = jnp.concatenate(diag_cols, axis=1)
            diag_all.append(diag_g)
            dxdt_cols.append(diag_g + wq)
            dww_cols.append(wq * xdt[:, gs])
            dp = (ecs_c[:, gs] * dyr[:, gs]).astype(BF16)
            amat = (w_c[:, gs] * xdt[:, gs]).astype(BF16)
            dsb = dscores.astype(BF16)
            dxbc_ref[:, c_sl] = (jnp.dot(dsb, bg, preferred_element_type=F32)
                                 + jnp.dot(dp, hgb, preferred_element_type=F32))
            dxbc_ref[:, b_sl] = (lax.dot_general(dsb, cg, TN, preferred_element_type=F32)
                                 + jnp.dot(amat, dhgb, preferred_element_type=F32))
            dh_in = lax.dot_general(dp, cg, TN, preferred_element_type=F32)
            for j in range(HPG):
                hh = g * HPG + j
                js = slice(j * HEADDIM, (j + 1) * HEADDIM)
                ecl = jnp.exp(cs_last[:, hh:hh + 1])
                dlast = dlast + jnp.where(lane1 == hh, ecl * jnp.sum(dhg[js, :] * hg[js, :]), 0.0)
                dh_ref[g * GW + j * HEADDIM:g * GW + (j + 1) * HEADDIM, :] = ecl * dhg[js, :] + dh_in[js, :]
        dxdt = jnp.concatenate(dxdt_cols, axis=1)
        dxbc_ref[:, 0:D_SSD] = dxdt * dt_c + dyr * dsk_ref[...]
        dww = _dot3(jnp.concatenate(dww_cols, axis=1), fold_ref[...])
        dcs = (_dot3(dyrb.astype(F32) * (yraw - x * dsk_ref[...]), fold_ref[...])
               - _dot3(xdt.astype(BF16).astype(F32) * jnp.concatenate(diag_all, axis=1), fold_ref[...]) - dww)
        rowid = lax.broadcasted_iota(jnp.int32, (CHUNK, 128), 0)
        dcs = dcs + jnp.where(rowid == CHUNK - 1, jnp.sum(dww, axis=0, keepdims=True) + dlast, 0.0)
        dadt = _dot3_left(tri, dcs, TN)
        ddt = _dot3(dxdt * x, fold_ref[...]) + dadt * a
        da = jnp.sum(dadt * dt, axis=0, keepdims=True)
        ddt_raw = ddt * _sigmoid(dt_ref[...] + par_ref[0:1])
        ddt_raw = jnp.where(lane < N_HEADS, ddt_raw, 0.0)
        ddt_ref[...] = ddt_raw
        dpar_ref[0:1, :] += jnp.sum(ddt_raw, axis=0, keepdims=True)
        dpar_ref[1:2, :] += jnp.where(lane1 < N_HEADS, da * a, 0.0)

    row = lambda w: pl.BlockSpec((CHUNK, w), lambda i: (blk(i), 0))
    full = lambda s: pl.BlockSpec(s, lambda i: (0,) * len(s))
    return pl.pallas_call(
        body, name="ssd_bwd", grid=(n_chunks,),
        out_shape=(jax.ShapeDtypeStruct((t, D_XBC), F32), jax.ShapeDtypeStruct((t, D_SSD), F32),
                   jax.ShapeDtypeStruct((t, DT_PAD), F32), jax.ShapeDtypeStruct((8, 128), F32),
                   jax.ShapeDtypeStruct((8, D_SSD), F32)),
        in_specs=[row(D_SSD), row(D_SSD), row(D_SSD), row(D_XBC), row(DT_PAD),
                  pl.BlockSpec((1, D_SSD, N_STATE), lambda i: (blk(i), 0, 0)),
                  full((8, 128)), full((1, D_SSD)), full((1, D_SSD)), full(fold.shape)],
        out_specs=(row(D_XBC), row(D_SSD), row(DT_PAD), full((8, 128)), full((8, D_SSD))),
        scratch_shapes=[pltpu.VMEM((D_SSD, N_STATE), F32), pltpu.VMEM((CHUNK, D_SSD), F32)],
        compiler_params=_params(48),
    )(dyssd, yraw, z, xbc, dt_raw, hprev, par, dsk, normw, fold)


def _conv_bwd(dxbc, xbc_pre, conv_w, conv_b, seq):
    t = xbc_pre.shape[0]
    tb = 512
    npb = seq // tb
    cw = 640

    def body(d_ref, cur_ref, halo_ref, w_ref, b_ref, o_ref, acc_ref, win):
        i = pl.program_id(1)

        @pl.when(i == 0)
        def _():
            acc_ref[...] = jnp.zeros_like(acc_ref)

        first = (i % npb) == 0
        win[0:8, :] = jnp.where(first, 0.0, halo_ref[...])
        win[8:8 + tb, :] = cur_ref[...]
        pre, shifted = _conv_taps(win, w_ref[...], tb, slice(None))
        pre = pre + b_ref[...]
        sg = _sigmoid(pre)
        dpre = d_ref[...] * (sg * (1.0 + pre * (1.0 - sg)))
        o_ref[...] = dpre
        for j in range(4):
            acc_ref[3 - j:4 - j, :] += jnp.sum(dpre * shifted[j], axis=0, keepdims=True)
        acc_ref[4:5, :] += jnp.sum(dpre, axis=0, keepdims=True)

    return pl.pallas_call(
        body, name="conv_bwd", grid=(D_XBC // cw, t // tb),
        out_shape=(jax.ShapeDtypeStruct((t, D_XBC), F32), jax.ShapeDtypeStruct((8, D_XBC), F32)),
        in_specs=[pl.BlockSpec((tb, cw), lambda j, i: (i, j)), pl.BlockSpec((tb, cw), lambda j, i: (i, j)),
                  pl.BlockSpec((8, cw), lambda j, i: (jnp.maximum(i * (tb // 8) - 1, 0), j)),
                  pl.BlockSpec((4, cw), lambda j, i: (0, j)), pl.BlockSpec((1, cw), lambda j, i: (0, j))],
        out_specs=(pl.BlockSpec((tb, cw), lambda j, i: (i, j)), pl.BlockSpec((8, cw), lambda j, i: (0, j))),
        scratch_shapes=[pltpu.VMEM((tb + 8, cw), F32)],
        compiler_params=_params(32),
    )(dxbc, xbc_pre, xbc_pre, conv_w, conv_b)


def _proj_bwd(dz, dpre, ddt, du5, x2, dxa, mod3, conv_w, w_in_pad, seq):
    t = x2.shape[0]
    tb = 512
    npb = seq // tb
    n_blocks = t // tb

    def body(dz_ref, dp_ref, nxt_ref, ddt_ref, du5_ref, x_ref, dxa_ref, mod_ref, cw_ref, w_hbm,
             gx_ref, u_ref, dxp_ref, bacc_ref, w_vmem, sem):
        i = pl.program_id(0)
        _load_once(w_hbm, w_vmem, sem)

        @pl.when(i % npb == 0)
        def _():
            bacc_ref[...] = jnp.zeros_like(bacc_ref)

        last = (i % npb) == npb - 1
        nxt = jnp.where(last, 0.0, nxt_ref[...])
        cur = dp_ref[...]
        xx = jnp.concatenate([cur, nxt], axis=0)
        w = cw_ref[...]
        dxp = w[3:4] * cur
        for j in (1, 2, 3):
            dxp = dxp + w[3 - j:4 - j] * pltpu.roll(xx, tb + 8 - j, axis=0)[0:tb]
        dxpb = dxp.astype(BF16)
        dxp_ref[...] = dxpb
        o1, o2, o3 = D_SSD, D_SSD + D_XBC, D_SSD + D_XBC + DT_PAD
        du = (lax.dot_general(dz_ref[...].astype(BF16), w_vmem[:, 0:o1], NT, preferred_element_type=F32)
              + lax.dot_general(dxpb, w_vmem[:, o1:o2], NT, preferred_element_type=F32)
              + lax.dot_general(ddt_ref[...].astype(BF16), w_vmem[:, o2:o3], NT, preferred_element_type=F32)
              + lax.dot_general(du5_ref[...].astype(BF16), w_vmem[:, o3:], NT, preferred_element_type=F32))
        m = mod_ref[0]
        xv = x_ref[...]
        u_ref[...] = (xv * (1.0 + m[1:2]) + m[0:1]).astype(BF16)
        gx_ref[...] = dxa_ref[...] + du * (1.0 + m[1:2])
        bacc_ref[0, 0:1, :] += jnp.sum(du, axis=0, keepdims=True)
        bacc_ref[0, 1:2, :] += jnp.sum(du * xv, axis=0, keepdims=True)

    row = lambda w: pl.BlockSpec((tb, w), lambda i: (i, 0))
    nxt_rows = pl.BlockSpec((8, D_XBC), lambda i: (jnp.minimum((i + 1) * (tb // 8), t // 8 - 1), 0))
    return pl.pallas_call(
        body, name="proj_bwd", grid=(n_blocks,),
        out_shape=(jax.ShapeDtypeStruct((t, D_MODEL), F32), jax.ShapeDtypeStruct((t, D_MODEL), BF16),
                   jax.ShapeDtypeStruct((t, D_XBC), BF16), jax.ShapeDtypeStruct((t // seq, 8, D_MODEL), F32)),
        in_specs=[row(D_SSD), row(D_XBC), nxt_rows, row(DT_PAD), row(D_S5), row(D_MODEL), row(D_MODEL),
                  pl.BlockSpec((1, N_MOD, D_MODEL), lambda i: (i // npb, 0, 0)),
                  pl.BlockSpec((4, D_XBC), lambda i: (0, 0)), ANY],
        out_specs=(row(D_MODEL), row(D_MODEL), row(D_XBC), pl.BlockSpec((1, 8, D_MODEL), lambda i: (i // npb, 0, 0))),
        scratch_shapes=[pltpu.VMEM((D_MODEL, D_INP), BF16), pltpu.SemaphoreType.DMA],
        compiler_params=_params(60),
    )(dz, dpre, dpre, ddt, du5, x2, dxa, mod3, conv_w, w_in_pad)


def _pad_rows(a, mult):
    r = a.shape[0]
    pad = (-r) % mult
    return a if pad == 0 else jnp.concatenate([a, jnp.zeros((pad,) + a.shape[1:], a.dtype)], axis=0)


_SMALL = ["conv_w", "conv_b", "dt_bias", "a_log", "d_ssd", "norm_w", "s5_a_re", "s5_a_im", "s5_log_dt", "s5_b_re",
          "s5_b_im", "s5_c_re", "s5_c_im", "s5_d", "b_glu", "ln1_g", "ln1_b", "b1", "b2", "ln2_g", "ln2_b"]


def _tile_rows(size):
    return 8 * (-(-size // 1024))


def _pack_small(d):
    parts = []
    for n in _SMALL:
        flat = d[n].reshape(-1).astype(F32)
        rows = _tile_rows(flat.shape[0])
        pad = rows * 128 - flat.shape[0]
        if pad:
            flat = jnp.concatenate([flat, jnp.zeros((pad,), F32)])
        parts.append(flat.reshape(rows, 128))
    return jnp.concatenate(parts, axis=0)


def _unpack_small(p, shapes):
    out, off = {}, 0
    for n in _SMALL:
        size = math.prod(shapes[n])
        rows = _tile_rows(size)
        out[n] = p[off:off + rows].reshape(-1)[:size].reshape(shapes[n])
        off += rows
    return out


def kernel(x, c, w_ada, b_ada, w_in, conv_w, conv_b, dt_bias, a_log, d_ssd, norm_w, s5_a_re, s5_a_im, s5_log_dt, s5_b_re, s5_b_im, s5_c_re, s5_c_im, s5_d, w_glu, b_glu, w_out, ln1_g, ln1_b, w1, b1, w2, b2, ln2_g, ln2_b, loss_target, m_w_ada, m_b_ada, m_w_in, m_conv_w, m_conv_b, m_dt_bias, m_a_log, m_d_ssd, m_norm_w, m_s5_a_re, m_s5_a_im, m_s5_log_dt, m_s5_b_re, m_s5_b_im, m_s5_c_re, m_s5_c_im, m_s5_d, m_w_glu, m_b_glu, m_w_out, m_ln1_g, m_ln1_b, m_w1, m_b1, m_w2, m_b2, m_ln2_g, m_ln2_b, v_w_ada, v_b_ada, v_w_in, v_conv_w, v_conv_b, v_dt_bias, v_a_log, v_d_ssd, v_norm_w, v_s5_a_re, v_s5_a_im, v_s5_log_dt, v_s5_b_re, v_s5_b_im, v_s5_c_re, v_s5_c_im, v_s5_d, v_w_glu, v_b_glu, v_w_out, v_ln1_g, v_ln1_b, v_w1, v_b1, v_w2, v_b2, v_ln2_g, v_ln2_b):
    weights = dict(w_ada=w_ada, b_ada=b_ada, w_in=w_in, conv_w=conv_w, conv_b=conv_b, dt_bias=dt_bias, a_log=a_log,
                   d_ssd=d_ssd, norm_w=norm_w, s5_a_re=s5_a_re, s5_a_im=s5_a_im, s5_log_dt=s5_log_dt, s5_b_re=s5_b_re,
                   s5_b_im=s5_b_im, s5_c_re=s5_c_re, s5_c_im=s5_c_im, s5_d=s5_d, w_glu=w_glu, b_glu=b_glu, w_out=w_out,
                   ln1_g=ln1_g, ln1_b=ln1_b, w1=w1, b1=b1, w2=w2, b2=b2, ln2_g=ln2_g, ln2_b=ln2_b)
    mom = dict(w_ada=m_w_ada, b_ada=m_b_ada, w_in=m_w_in, conv_w=m_conv_w, conv_b=m_conv_b, dt_bias=m_dt_bias,
               a_log=m_a_log, d_ssd=m_d_ssd, norm_w=m_norm_w, s5_a_re=m_s5_a_re, s5_a_im=m_s5_a_im,
               s5_log_dt=m_s5_log_dt, s5_b_re=m_s5_b_re, s5_b_im=m_s5_b_im, s5_c_re=m_s5_c_re, s5_c_im=m_s5_c_im,
               s5_d=m_s5_d, w_glu=m_w_glu, b_glu=m_b_glu, w_out=m_w_out, ln1_g=m_ln1_g, ln1_b=m_ln1_b, w1=m_w1, b1=m_b1,
               w2=m_w2, b2=m_b2, ln2_g=m_ln2_g, ln2_b=m_ln2_b)
    var = dict(w_ada=v_w_ada, b_ada=v_b_ada, w_in=v_w_in, conv_w=v_conv_w, conv_b=v_conv_b, dt_bias=v_dt_bias,
               a_log=v_a_log, d_ssd=v_d_ssd, norm_w=v_norm_w, s5_a_re=v_s5_a_re, s5_a_im=v_s5_a_im,
               s5_log_dt=v_s5_log_dt, s5_b_re=v_s5_b_re, s5_b_im=v_s5_b_im, s5_c_re=v_s5_c_re, s5_c_im=v_s5_c_im,
               s5_d=v_s5_d, w_glu=v_w_glu, b_glu=v_b_glu, w_out=v_w_out, ln1_g=v_ln1_g, ln1_b=v_ln1_b, w1=v_w1, b1=v_b1,
               w2=v_w2, b2=v_b2, ln2_g=v_ln2_g, ln2_b=v_ln2_b)
    names = list(weights)
    shapes = {n: weights[n].shape for n in names}

    nb, seq, _ = x.shape
    t = nb * seq
    dev = _dev_index()
    x2 = x.reshape(t, D_MODEL)
    tgt2 = loss_target.reshape(t, D_MODEL)

    cw_cols = conv_w.shape[2]
    small_in = jnp.concatenate([c.reshape(-1), conv_w.reshape(-1)]).reshape(-1, 128)
    small_all = _all_gather([small_in], "gather_c_conv")[0].reshape(N_DEV, -1)
    c_all = small_all[:, :nb * D_MODEL].reshape(N_DEV * nb, D_MODEL)
    conv_w_full = small_all[:, nb * D_MODEL:].reshape(N_DEV, 4, cw_cols).transpose(1, 0, 2).reshape(4, D_XBC)

    big_names = ["w_in", "w_out", "w1", "w2", "w_glu"]
    gathered = dict(zip(big_names, _all_gather([weights[n][0].astype(BF16) for n in big_names], "gather_weights")))
    w_in_f = gathered["w_in"].transpose(1, 0, 2).reshape(D_MODEL, D_IN)
    w_in_pad = jnp.concatenate(
        [w_in_f[:, :D_SSD + D_XBC], w_in_f[:, D_SSD + D_XBC:D_SSD + D_XBC + N_HEADS],
         jnp.zeros((D_MODEL, DT_PAD - N_HEADS), BF16), w_in_f[:, D_SSD + D_XBC + N_HEADS:]], axis=1)
    w_out_f = gathered["w_out"].reshape(2 * D_MODEL, D_MODEL)
    w1_blocks = gathered["w1"]
    w2_f = gathered["w2"].reshape(D_FF, D_MODEL)
    w_glu_f = gathered["w_glu"].reshape(D_S5, D_S5)

    ada_cols = w_ada.shape[2]
    b_cols = lax.dynamic_slice_in_dim(b_ada, dev * ada_cols, ada_cols, axis=1)
    mod_cols = _mod_fwd(c_all, w_ada[0], b_cols)
    mod_all = _all_gather([mod_cols], "gather_mod")[0]
    mod_mine = lax.dynamic_slice_in_dim(mod_all, dev * nb, nb, axis=1)
    mod3 = mod_mine.transpose(1, 0, 2).reshape(nb, N_MOD, D_MODEL)

    def pad_lanes(v, n):
        return jnp.concatenate([v, jnp.zeros((v.shape[0], n - v.shape[1]), F32)], axis=1)

    par = _pad_rows(jnp.concatenate([pad_lanes(dt_bias, 128), pad_lanes(a_log, 128)], axis=0), 8)
    dsk = jnp.repeat(d_ssd[0], HEADDIM).reshape(1, D_SSD)
    ar = s5_a_re.reshape(1, S5_N)
    ai = s5_a_im.reshape(1, S5_N)
    ldt = jnp.repeat(s5_log_dt[0], S5_P).reshape(1, S5_N)
    br_t = s5_b_re[0].transpose(2, 0, 1).reshape(S5_CH, S5_N)
    bi_t = s5_b_im[0].transpose(2, 0, 1).reshape(S5_CH, S5_N)
    bb_re_t, bb_im_t, pf_re, pf_im, pr_re, pr_im = _s5_params_fwd(ar, ai, ldt, br_t, bi_t)
    gpb = S5_GROUPS // S5_BLOCKS
    mask_b = (jnp.arange(128)[:, None] // S5_CH) == (jnp.arange(512)[None, :] // S5_P)

    def dense_b(bt_):
        blocks = bt_.reshape(S5_CH, S5_BLOCKS, 512).transpose(1, 0, 2)
        return jnp.where(mask_b, jnp.tile(blocks, (1, gpb, 1)), 0.0).astype(BF16)

    def dense_c(cc):
        blocks = cc[0].transpose(0, 2, 1).reshape(S5_BLOCKS, 512, S5_CH)
        return jnp.where(mask_b.T, jnp.tile(blocks, (1, 1, gpb)), 0.0).astype(BF16)

    bb_re, bb_im = dense_b(bb_re_t), dense_b(bb_im_t)
    cc_re, cc_im = dense_c(s5_c_re), dense_c(s5_c_im)
    s5d = s5_d.reshape(1, D_S5)
    ln1 = jnp.concatenate([ln1_g, ln1_b], axis=0)
    vec1 = _pad_rows(jnp.concatenate([b2, ln2_g, ln2_b], axis=0), 8)

    z, xbc_pre, xbc, dt_raw, u5 = _proj_conv_fwd(x2, mod3, w_in_pad, conv_w_full, conv_b, seq)
    yraw, ycat, hprev = _ssd_fwd(xbc, z, dt_raw, par, dsk, norm_w, seq)
    s_re, s_im, ypre, ycat = _s5_fwd(u5, bb_re, bb_im, cc_re, cc_im, pf_re, pf_im, s5d, w_glu_f, b_glu, ycat, seq)
    mix, x1 = _out_ln1(ycat, x2, mod3, w_out_f, ln1, seq)

    dx1, u2b, hb, dhpb, dob, gacc2, db1, bacc2 = _mlp_fwd_bwd(x1, tgt2, mod3, w1_blocks, w2_f, vec1, b1, seq)
    loss = lax.psum(0.5 / D_MODEL * jnp.sum(gacc2[3]), ("x", "y", "c"))

    dmixb, dxa, dyssd, dy5, gacc1, bacc1 = _ln1_out_bwd(dx1, x2, mix, mod3, w_out_f, ln1, seq)
    du5, vacc, sacc, d_cc, d_bb, g_wglu = _s5_bwd(dy5, ypre, u5, s_re, s_im, bb_re, bb_im, cc_re, cc_im,
                                                  pr_re, pr_im, s5d, w_glu_f, b_glu, seq)
    dxbc, dz, ddt, dpar, cacc = _ssd_bwd(dyssd, yraw, z, xbc, dt_raw, hprev, par, dsk, norm_w, seq)
    dpre, conv_acc = _conv_bwd(dxbc, xbc_pre, conv_w_full, conv_b, seq)
    grad_x2, ub, dxpb, bacc0 = _proj_bwd(dz, dpre, ddt, du5, x2, dxa, mod3, conv_w_full, w_in_pad, seq)

    g_w2 = _atb(hb, dob, "gw2")
    g_w1 = _atb(u2b, dhpb, "gw1")
    g_wout = _atb(ycat, dmixb, "gwout")
    g_win = jnp.concatenate([_atb(ub, dz, "gwin_z"), _atb(ub, dxpb, "gwin_xbc"),
                             _atb(ub, ddt, "gwin_dt")[:, :N_HEADS], _atb(ub, du5, "gwin_s5")], axis=1)

    def diag_b(dd):
        kept = jnp.where(mask_b, dd, 0.0).reshape(S5_BLOCKS, gpb, S5_CH, 512).sum(1)
        return kept.transpose(1, 0, 2).reshape(S5_CH, S5_N)

    def diag_c(dd):
        kept = jnp.where(mask_b.T, dd, 0.0).reshape(S5_BLOCKS, 512, gpb, S5_CH).sum(2)
        return kept.reshape(S5_GROUPS, S5_P, S5_CH).transpose(0, 2, 1)

    g_ar, g_ai, g_ldt, g_br_t, g_bi_t = _s5_params_bwd(ar, ai, ldt, br_t, bi_t, vacc[0:1], vacc[1:2],
                                                      diag_b(d_bb[:S5_BLOCKS]), diag_b(d_bb[S5_BLOCKS:]))

    def from_t(gt):
        return gt.reshape(S5_CH, S5_GROUPS, S5_P).transpose(1, 2, 0)

    small_g = dict(
        conv_w=conv_acc[0:4], conv_b=conv_acc[4:5], dt_bias=dpar[0:1, :N_HEADS], a_log=dpar[1:2, :N_HEADS],
        d_ssd=cacc[0].reshape(N_HEADS, HEADDIM).sum(1), norm_w=cacc[1:2],
        s5_a_re=g_ar, s5_a_im=g_ai, s5_log_dt=g_ldt[:, :S5_GROUPS], s5_b_re=from_t(g_br_t), s5_b_im=from_t(g_bi_t),
        s5_c_re=diag_c(d_cc[:S5_BLOCKS]), s5_c_im=diag_c(d_cc[S5_BLOCKS:]), s5_d=sacc[0:1], b_glu=sacc[1:2],
        ln1_g=gacc1[0:1], ln1_b=gacc1[1:2], b1=db1, b2=gacc2[2:3], ln2_g=gacc2[0:1], ln2_b=gacc2[1:2])

    dmod = jnp.concatenate([bacc0[:, 0], bacc0[:, 1], bacc1[:, 0], bacc2[:, 0], bacc2[:, 1], bacc2[:, 2]], axis=1)
    dmod_all = _all_gather([dmod], "gather_dmod")[0].reshape(N_DEV * nb, N_MOD * D_MODEL)
    dmod_cols = lax.dynamic_slice_in_dim(dmod_all, dev * ada_cols, ada_cols, axis=1)
    g_wada, g_bada = _mod_bwd(c_all, dmod_cols, dmod_all)

    in_cols = w_in.shape[2]
    big_g = dict(
        w_in=g_win.reshape(D_MODEL, N_DEV, in_cols).transpose(1, 0, 2),
        w_out=g_wout.reshape((N_DEV,) + w_out.shape[1:]), w1=g_w1,
        w2=g_w2.reshape((N_DEV,) + w2.shape[1:]), w_glu=g_wglu.reshape((N_DEV,) + w_glu.shape[1:]))
    by_dest = [big_g[n] if n == "w1" else big_g[n].reshape((4, 2) + big_g[n].shape[1:]) for n in big_names]
    from_sibling = _sibling_swap(by_dest, "rs_sibling_swap")
    core = lax.axis_index("c").astype(jnp.int32).reshape(1)
    chip_sums = [_add_halves(g, r, core, "rs_add_" + n) for g, r, n in zip(by_dest, from_sibling, big_names)]
    parts = _chip_all_to_all(chip_sums, "rs_chip_all_to_all")

    res = {k: {} for k in "gdmv"}
    for n, p in zip(big_names, parts):
        outs = _adamw(p, weights[n][0], mom[n][0], var[n][0], "adamw_" + n)
        for k, a in zip("gdmv", outs):
            res[k][n] = a[None]

    ag, ad, am, av = _adamw(g_wada[None], w_ada[0], m_w_ada[0], v_w_ada[0], "adamw_w_ada")
    for k, a in (("g", ag), ("d", ad), ("m", am), ("v", av)):
        res[k]["w_ada"] = a[None]
    bg_, bd_, bm_, bv_ = _adamw(g_bada.reshape(1, -1, 128), b_ada.reshape(-1, 128), m_b_ada.reshape(-1, 128),
                                v_b_ada.reshape(-1, 128), "adamw_b_ada")
    for k, a in (("g", bg_), ("d", bd_), ("m", bm_), ("v", bv_)):
        res[k]["b_ada"] = a.reshape(shapes["b_ada"])

    small_shapes = dict(shapes)
    small_shapes["conv_w"] = (1, 4, D_XBC)
    small_parts = _all_gather([_pack_small(small_g)], "gather_small_grads")[0]
    rep = {n: (jnp.zeros((1, 4, D_XBC), F32) if n == "conv_w" else weights[n]) for n in _SMALL}
    rep_m = {n: (jnp.zeros((1, 4, D_XBC), F32) if n == "conv_w" else mom[n]) for n in _SMALL}
    rep_v = {n: (jnp.ones((1, 4, D_XBC), F32) if n == "conv_w" else var[n]) for n in _SMALL}
    sg_, sd_, sm_, sv_ = _adamw(small_parts, _pack_small(rep), _pack_small(rep_m), _pack_small(rep_v), "adamw_small")
    for k, p in (("g", sg_), ("d", sd_), ("m", sm_), ("v", sv_)):
        un = _unpack_small(p, small_shapes)
        for n in _SMALL:
            if n != "conv_w":
                res[k][n] = un[n]
    g_conv_full = _unpack_small(sg_, small_shapes)["conv_w"][0]
    g_conv_mine = lax.dynamic_slice_in_dim(g_conv_full, dev * cw_cols, cw_cols, axis=1)
    cg_, cd_, cm_, cv_ = _adamw(g_conv_mine[None], conv_w[0], m_conv_w[0], v_conv_w[0], "adamw_conv_w")
    for k, a in (("g", cg_), ("d", cd_), ("m", cm_), ("v", cv_)):
        res[k]["conv_w"] = a[None]

    grad_x = grad_x2.reshape(nb, seq, D_MODEL)
    return (loss, grad_x, *[res["g"][n] for n in names], *[res["d"][n] for n in names],
            *[res["m"][n] for n in names], *[res["v"][n] for n in names])
```

```python
import functools
import math

import jax
import jax.numpy as jnp
from jax import lax
from jax.experimental import pallas as pl
from jax.experimental.pallas import tpu as pltpu

F32, BF16 = jnp.float32, jnp.bfloat16
MESH = pl.DeviceIdType.MESH
N_DEV = 8

D_MODEL = 1024
D_SSD = 1536
N_HEADS = 24
HEADDIM = 64
N_GROUPS = 4
HPG = 6
GW = HPG * HEADDIM
N_STATE = 128
CHUNK = 128
D_XBC = 2560
D_S5 = 512
S5_GROUPS = 32
S5_CH = 16
S5_P = 64
S5_N = S5_GROUPS * S5_P
D_IN = 4632
DT_PAD = 128
D_INP = D_SSD + D_XBC + DT_PAD + D_S5
D_FF = 4096
N_MOD = 6
ALPHA = 2.0 ** 0.25
EPS = 1e-5
LR, B1, B2, AEPS, WD, STEP = 0.001, 0.9, 0.999, 1e-08, 0.01, 10

NT = (((1,), (1,)), ((), ()))
TN = (((0,), (0,)), ((), ()))
ANY = pl.BlockSpec(memory_space=pl.ANY)
HIGHEST = lax.Precision.HIGHEST


def _mm(a, b):
    return jnp.dot(a.astype(BF16), b.astype(BF16), preferred_element_type=F32)


def _mm_nt(a, b):
    return lax.dot_general(a.astype(BF16), b.astype(BF16), NT, preferred_element_type=F32)


def _mm_tn(a, b):
    return lax.dot_general(a.astype(BF16), b.astype(BF16), TN, preferred_element_type=F32)


def _row_block(r, cap):
    best = r
    for cand in range(8, min(r, cap) + 1, 8):
        if r % cand == 0:
            best = cand
    return best if best <= cap else r


def _params(vmem_mb):
    return pltpu.CompilerParams(vmem_limit_bytes=vmem_mb << 20)


def _sigmoid(x):
    return 0.5 * (jnp.tanh(0.5 * x) + 1.0)


def _softplus(x):
    return jnp.maximum(x, 0.0) + jnp.log(1.0 + jnp.exp(-jnp.abs(x)))


_GK = math.sqrt(2.0 / math.pi)


def _gelu(x):
    return 0.5 * x * (1.0 + jnp.tanh(_GK * (x + 0.044715 * x * x * x)))


def _gelu_grad(x):
    t = jnp.tanh(_GK * (x + 0.044715 * x * x * x))
    return 0.5 * (1.0 + t) + 0.5 * x * (1.0 - t * t) * _GK * (1.0 + 3.0 * 0.044715 * x * x)


def _dev_index():
    return 4 * lax.axis_index("x") + 2 * lax.axis_index("y") + lax.axis_index("c")


def _all_gather(xs, name):
    n = len(xs)

    def body(*refs):
        x_refs, out_refs = refs[:n], refs[n:2 * n]
        send_sems, recv_sems, local_sems = refs[2 * n:]
        ix, iy, ic = lax.axis_index("x"), lax.axis_index("y"), lax.axis_index("c")
        me, sibling = (ix, iy, ic), (ix, iy, 1 - ic)
        chips = [(1 - ix, iy), (ix, 1 - iy), (1 - ix, 1 - iy)]

        def slot(a, px, py, pc):
            return out_refs[a].at[4 * px + 2 * py + pc]

        def copy(a, k, block, to, src=None):
            return pltpu.make_async_remote_copy(
                src_ref=slot(a, *block) if src is None else src, dst_ref=slot(a, *block),
                send_sem=send_sems.at[7 * a + k], recv_sem=recv_sems.at[7 * a + k], device_id=to, device_id_type=MESH)

        mine = [pltpu.make_async_copy(x_refs[a], slot(a, *me), local_sems.at[a]) for a in range(n)]
        for cp in mine:
            cp.start()
        first = []
        for j, chip in enumerate(chips):
            first += [copy(a, 1 + j, me, (*chip, ic), src=x_refs[a]) for a in range(n)]
        first += [copy(a, 0, me, sibling, src=x_refs[a]) for a in range(n)]
        for cp in first:
            cp.start()
        passed = []
        for j, chip in enumerate(chips):
            for a in range(n):
                copy(a, 1 + j, (*chip, ic), me).wait_recv()
                cp = copy(a, 4 + j, (*chip, ic), sibling)
                cp.start()
                passed.append(cp)
        for a in range(n):
            copy(a, 0, sibling, me).wait_recv()
            for j, chip in enumerate(chips):
                copy(a, 4 + j, (*chip, 1 - ic), me).wait_recv()
        for cp in first + passed:
            cp.wait_send()
        for cp in mine:
            cp.wait()

    return pl.pallas_call(
        body, name=name, out_shape=tuple(jax.ShapeDtypeStruct((N_DEV,) + x.shape, x.dtype) for x in xs),
        in_specs=[ANY] * n, out_specs=tuple([ANY] * n),
        scratch_shapes=[pltpu.SemaphoreType.DMA((7 * n,)), pltpu.SemaphoreType.DMA((7 * n,)),
                        pltpu.SemaphoreType.DMA((n,))],
    )(*xs)


HBM = pl.BlockSpec(memory_space=pltpu.HBM)
SEM = pl.BlockSpec(memory_space=pltpu.SEMAPHORE)
DATAFLOW = pltpu.SideEffectType.DATAFLOW_SIDE_EFFECTING


def _peer(k):
    ix, iy, ic = lax.axis_index("x"), lax.axis_index("y"), lax.axis_index("c")
    return (1 - ix if k & 4 else ix, 1 - iy if k & 2 else iy, 1 - ic if k & 1 else ic)


def _block_of(p):
    return 4 * p[0] + 2 * p[1] + p[2]


def _gather_start(xs, name):
    n = len(xs)
    lands = [lax.empty((N_DEV,) + x.shape, x.dtype) for x in xs]

    def body(*refs):
        x_refs, land_refs = refs[:n], refs[n:2 * n]
        send_sems, recv_sems = refs[2 * n], refs[2 * n + 1]
        token = refs[-1]
        me = _block_of(_peer(0))
        for a in range(n):
            for k in range(1, N_DEV):
                pltpu.make_async_remote_copy(
                    src_ref=x_refs[a], dst_ref=land_refs[a].at[me], send_sem=send_sems.at[7 * a + k - 1],
                    recv_sem=recv_sems.at[7 * a + k - 1], device_id=_peer(k), device_id_type=MESH).start()
        token[...] = jnp.zeros_like(token)

    outs = pl.pallas_call(
        body, name=name,
        out_shape=(pltpu.SemaphoreType.DMA((7 * n,)), pltpu.SemaphoreType.DMA((7 * n,)))
        + tuple(pltpu.HBM(x.shape, x.dtype) for x in xs) + tuple(pltpu.HBM(l.shape, l.dtype) for l in lands)
        + (jax.ShapeDtypeStruct((8, 128), F32),),
        in_specs=[HBM] * (2 * n), out_specs=(SEM, SEM) + (HBM,) * (2 * n) + (pl.BlockSpec(memory_space=pltpu.VMEM),),
        input_output_aliases={i: 2 + i for i in range(2 * n)},
        compiler_params=pltpu.CompilerParams(has_side_effects=DATAFLOW),
    )(*[pltpu.with_memory_space_constraint(x, pltpu.HBM) for x in xs],
      *[pltpu.with_memory_space_constraint(l, pltpu.HBM) for l in lands])
    return outs[0], outs[1], outs[2:2 + n], outs[2 + n:2 + 2 * n], outs[-1]


def _gather_wait(send_sems, recv_sems, xs_thru, lands_thru, after, name):
    n = len(xs_thru)

    def body(*refs):
        x_refs, land_refs = refs[:n], refs[n:2 * n]
        send_sems, recv_sems = refs[2 * n], refs[2 * n + 1]
        for a in range(n):
            for k in range(1, N_DEV):
                cp = pltpu.make_async_remote_copy(
                    src_ref=x_refs[a], dst_ref=land_refs[a].at[_block_of(_peer(k))], send_sem=send_sems.at[7 * a + k - 1],
                    recv_sem=recv_sems.at[7 * a + k - 1], device_id=_peer(k), device_id_type=MESH)
                cp.wait_send()
                cp.wait_recv()

    outs = pl.pallas_call(
        body, name=name,
        out_shape=tuple(pltpu.HBM(x.shape, x.dtype) for x in xs_thru)
        + tuple(pltpu.HBM(l.shape, l.dtype) for l in lands_thru),
        in_specs=[HBM] * (2 * n) + [SEM, SEM, ANY], out_specs=(HBM,) * (2 * n),
        input_output_aliases={i: i for i in range(2 * n)},
        compiler_params=pltpu.CompilerParams(has_side_effects=DATAFLOW),
    )(*xs_thru, *lands_thru, send_sems, recv_sems, after)
    return outs[n:]


def _sibling_swap(gs, name):
    n = len(gs)

    def body(*refs):
        g_refs, recv_refs = refs[:n], refs[n:2 * n]
        send_sems, recv_sems = refs[2 * n:]
        ix, iy, ic = lax.axis_index("x"), lax.axis_index("y"), lax.axis_index("c")

        def block(g_ref, q):
            if len(g_ref.shape) == 4:
                return g_ref.at[q, 1 - ic]
            cw = g_ref.shape[1] // N_DEV
            return g_ref.at[:, pl.ds(pl.multiple_of((2 * q + 1 - ic) * cw, 128), cw)]

        cps = []
        for a in range(n):
            for q in range(4):
                cps.append(pltpu.make_async_remote_copy(
                    src_ref=block(g_refs[a], q), dst_ref=recv_refs[a].at[q],
                    send_sem=send_sems.at[4 * a + q], recv_sem=recv_sems.at[4 * a + q],
                    device_id=(ix, iy, 1 - ic), device_id_type=MESH))
        for cp in cps:
            cp.start()
        for cp in cps:
            cp.wait()

    return pl.pallas_call(
        body, name=name,
        out_shape=tuple(jax.ShapeDtypeStruct(
            (4,) + (g.shape[2:] if g.ndim == 4 else (g.shape[0], g.shape[1] // N_DEV)), g.dtype) for g in gs),
        in_specs=[ANY] * n, out_specs=tuple([ANY] * n),
        scratch_shapes=[pltpu.SemaphoreType.DMA((4 * n,)), pltpu.SemaphoreType.DMA((4 * n,))],
    )(*gs)


def _chip_all_to_all(hs, name):
    n = len(hs)

    def body(*refs):
        h_refs, out_refs = refs[:n], refs[n:2 * n]
        send_sems, recv_sems, local_sems = refs[2 * n:]
        ix, iy, ic = lax.axis_index("x"), lax.axis_index("y"), lax.axis_index("c")
        me = 2 * ix + iy
        peers = [(1 - ix, iy), (ix, 1 - iy), (1 - ix, 1 - iy)]
        mine = [pltpu.make_async_copy(h_refs[a].at[me], out_refs[a].at[me], local_sems.at[a]) for a in range(n)]
        for cp in mine:
            cp.start()

        def copy(a, k, src_slot, dst_slot, peer):
            return pltpu.make_async_remote_copy(
                src_ref=h_refs[a].at[src_slot], dst_ref=out_refs[a].at[dst_slot],
                send_sem=send_sems.at[3 * a + k], recv_sem=recv_sems.at[3 * a + k],
                device_id=(*peer, ic), device_id_type=MESH)

        sends = [copy(a, k, 2 * px + py, me, (px, py)) for a in range(n) for k, (px, py) in enumerate(peers)]
        for cp in sends:
            cp.start()
        for a in range(n):
            for k, (px, py) in enumerate(peers):
                copy(a, k, 2 * px + py, 2 * px + py, (px, py)).wait_recv()
        for cp in sends:
            cp.wait_send()
        for cp in mine:
            cp.wait()

    return pl.pallas_call(
        body, name=name, out_shape=tuple(jax.ShapeDtypeStruct(h.shape, h.dtype) for h in hs),
        in_specs=[ANY] * n, out_specs=tuple([ANY] * n),
        scratch_shapes=[pltpu.SemaphoreType.DMA((3 * n,)), pltpu.SemaphoreType.DMA((3 * n,)),
                        pltpu.SemaphoreType.DMA((n,))],
    )(*hs)


def _add_halves(g, recv, core, name):
    _, r, c = recv.shape
    br = _row_block(r, 512)
    stacked = g.ndim == 4

    def body(core_ref, g_ref, r_ref, o_ref):
        o_ref[0] = ((g_ref[0, 0] if stacked else g_ref[...]) + r_ref[0]).astype(BF16)

    spec = pl.BlockSpec((1, br, c), lambda i, j, core_ref: (i, j, 0))
    if stacked:
        g_spec = pl.BlockSpec((1, 1, br, c), lambda i, j, core_ref: (i, core_ref[0], j, 0))
    else:
        g_spec = pl.BlockSpec((br, c), lambda i, j, core_ref: (j, 2 * i + core_ref[0]))
    return pl.pallas_call(
        body, name=name, out_shape=jax.ShapeDtypeStruct(recv.shape, BF16),
        grid_spec=pltpu.PrefetchScalarGridSpec(
            num_scalar_prefetch=1, grid=(4, r // br), in_specs=[g_spec, spec], out_specs=spec),
        compiler_params=_params(32),
    )(core, g, recv)


def _adamw(parts, w, m, v, name):
    n_parts, r, c = parts.shape
    br = _row_block(r, 512 if c <= 1024 else 256)

    def body(p_ref, w_ref, m_ref, v_ref, g_out, d_out, m_out, v_out):
        g = p_ref[0].astype(F32)
        for p in range(1, n_parts):
            g = g + p_ref[p].astype(F32)
        m2 = B1 * m_ref[...] + (1.0 - B1) * g
        v2 = B2 * v_ref[...] + (1.0 - B2) * (g * g)
        m_hat = m2 / (1.0 - B1 ** STEP)
        v_hat = v2 / (1.0 - B2 ** STEP)
        g_out[...] = g
        d_out[...] = -LR * (m_hat / (jnp.sqrt(v_hat) + AEPS) + WD * w_ref[...])
        m_out[...] = m2
        v_out[...] = v2

    spec = pl.BlockSpec((br, c), lambda i: (i, 0))
    out = jax.ShapeDtypeStruct((r, c), F32)
    return pl.pallas_call(
        body, name=name, out_shape=(out, out, out, out), grid=(r // br,),
        in_specs=[pl.BlockSpec((n_parts, br, c), lambda i: (0, i, 0)), spec, spec, spec],
        out_specs=(spec, spec, spec, spec), compiler_params=_params(40),
    )(parts, w, m, v)


def _atb(a, b, name, bt=512):
    t, k1 = a.shape
    k2 = b.shape[1]

    def pick(k):
        for cand in (1024, 768, 512, 384, 256, 128):
            if k % cand == 0:
                return cand
        return k

    b1, b2 = pick(k1), pick(k2)

    def body(a_ref, b_ref, o_ref):
        @pl.when(pl.program_id(2) == 0)
        def _():
            o_ref[...] = jnp.zeros_like(o_ref)
        o_ref[...] += _mm_tn(a_ref[...], b_ref[...])

    return pl.pallas_call(
        body, name=name, out_shape=jax.ShapeDtypeStruct((k1, k2), F32), grid=(k1 // b1, k2 // b2, t // bt),
        in_specs=[pl.BlockSpec((bt, b1), lambda i, j, k: (k, i)), pl.BlockSpec((bt, b2), lambda i, j, k: (k, j))],
        out_specs=pl.BlockSpec((b1, b2), lambda i, j, k: (i, j)), compiler_params=_params(40),
    )(a, b)


def _mod_fwd(c_all, w_ada, b_cols):
    def body(c_ref, w_ref, b_ref, o_ref):
        cc = c_ref[...]
        cond = cc * _sigmoid(cc)
        o_ref[...] = _mm(cond, w_ref[...]) + b_ref[...]

    return pl.pallas_call(body, name="mod_fwd", out_shape=jax.ShapeDtypeStruct((c_all.shape[0], w_ada.shape[1]), F32),
                          compiler_params=_params(32))(c_all, w_ada, b_cols)


def _mod_bwd(c_all, dmod_cols, dmod_all):
    def body(c_ref, dc_ref, da_ref, gw_ref, gb_ref):
        cc = c_ref[...]
        cond = cc * _sigmoid(cc)
        gw_ref[...] = _mm_tn(cond, dc_ref[...])
        gb_ref[...] = jnp.sum(da_ref[...], axis=0, keepdims=True)

    return pl.pallas_call(
        body, name="mod_bwd",
        out_shape=(jax.ShapeDtypeStruct((D_MODEL, dmod_cols.shape[1]), F32), jax.ShapeDtypeStruct((1, dmod_all.shape[1]), F32)),
        compiler_params=_params(32))(c_all, dmod_cols, dmod_all)


def _load_once(hbm_ref, vmem_ref, sem):
    @pl.when(pl.program_id(0) == 0)
    def _():
        cp = pltpu.make_async_copy(hbm_ref, vmem_ref, sem)
        cp.start()
        cp.wait()


def _conv_taps(win_ref, w, tb, cols):
    shifted = [win_ref[8 - j:8 - j + tb, cols] for j in range(4)]
    acc = w[3:4] * shifted[0]
    for j in (1, 2, 3):
        acc = acc + w[3 - j:4 - j] * shifted[j]
    return acc, shifted


def _proj_conv_fwd(x2, mod3, w_in_pad, conv_w, conv_b, seq):
    t = x2.shape[0]
    tb = 256
    npb = seq // tb
    cw = 512

    def body(x_ref, mod_ref, w_hbm, cw_ref, cb_ref, z_ref, pre_ref, xbc_ref, dt_ref, u5_ref, w_vmem, win, sem):
        _load_once(w_hbm, w_vmem, sem)
        first = (pl.program_id(0) % npb) == 0

        @pl.when(first)
        def _():
            win[0:8, :] = jnp.zeros((8, D_XBC), F32)

        @pl.when(jnp.logical_not(first))
        def _():
            win[0:8, :] = win[tb:tb + 8, :]

        m = mod_ref[0]
        u = (x_ref[...] * (1.0 + m[1:2]) + m[0:1]).astype(BF16)
        z_ref[...] = jnp.dot(u, w_vmem[:, 0:D_SSD], preferred_element_type=F32)
        dt_ref[...] = jnp.dot(u, w_vmem[:, D_SSD + D_XBC:D_SSD + D_XBC + DT_PAD], preferred_element_type=F32)
        u5_ref[...] = jnp.dot(u, w_vmem[:, D_SSD + D_XBC + DT_PAD:], preferred_element_type=F32)
        for k in range(D_XBC // cw):
            cols = slice(k * cw, (k + 1) * cw)
            pre_k = jnp.dot(u, w_vmem[:, D_SSD + k * cw:D_SSD + (k + 1) * cw], preferred_element_type=F32)
            win[8:8 + tb, cols] = pre_k
            pre_ref[:, cols] = pre_k
            conv, _ = _conv_taps(win, cw_ref[:, cols], tb, cols)
            conv = conv + cb_ref[:, cols]
            xbc_ref[:, cols] = conv * _sigmoid(conv)

    row = lambda w: pl.BlockSpec((tb, w), lambda i: (i, 0))
    return pl.pallas_call(
        body, name="proj_conv_fwd", grid=(t // tb,),
        out_shape=(jax.ShapeDtypeStruct((t, D_SSD), F32), jax.ShapeDtypeStruct((t, D_XBC), F32),
                   jax.ShapeDtypeStruct((t, D_XBC), F32), jax.ShapeDtypeStruct((t, DT_PAD), F32),
                   jax.ShapeDtypeStruct((t, D_S5), F32)),
        in_specs=[row(D_MODEL), pl.BlockSpec((1, N_MOD, D_MODEL), lambda i: (i // npb, 0, 0)), ANY,
                  pl.BlockSpec((4, D_XBC), lambda i: (0, 0)), pl.BlockSpec((1, D_XBC), lambda i: (0, 0))],
        out_specs=(row(D_SSD), row(D_XBC), row(D_XBC), row(DT_PAD), row(D_S5)),
        scratch_shapes=[pltpu.VMEM((D_MODEL, D_INP), BF16), pltpu.VMEM((tb + 8, D_XBC), F32), pltpu.SemaphoreType.DMA],
        compiler_params=_params(56),
    )(x2, mod3, w_in_pad, conv_w, conv_b)


N_PAIRS = N_HEADS // 2


def _split3(x):
    hi = x.astype(BF16)
    r = x - hi.astype(F32)
    mid = r.astype(BF16)
    lo = (r - mid.astype(F32)).astype(BF16)
    return hi, mid, lo


def _dot3(x, e, dims=(((1,), (0,)), ((), ()))):
    return sum(lax.dot_general(p, e, dims, preferred_element_type=F32) for p in _split3(x))


def _dot3_left(e, x, dims=(((1,), (0,)), ((), ()))):
    return sum(lax.dot_general(e, p, dims, preferred_element_type=F32) for p in _split3(x))


def _head_fold():
    return (jnp.arange(D_SSD)[:, None] // HEADDIM == jnp.arange(128)[None, :]).astype(BF16)


def _ssd_prep(dt_raw, par):
    dtb = par[0:1]
    a = -jnp.exp(par[1:2])
    dt = _softplus(dt_raw + dtb)
    adt = dt * a
    row = lax.broadcasted_iota(jnp.int32, (CHUNK, CHUNK), 0)
    col = lax.broadcasted_iota(jnp.int32, (CHUNK, CHUNK), 1)
    causal = row >= col
    tri = causal.astype(BF16)
    cs = _dot3_left(tri, adt)
    left = col < HEADDIM

    def lanes(v, h):
        return jnp.broadcast_to(v[:, h:h + 1], (CHUNK, 128))

    dt_c, cs_c, pair_cols = [], [], []
    for p in range(N_PAIRS):
        c0, c1 = lanes(cs, 2 * p), lanes(cs, 2 * p + 1)
        pair_cols.append(jnp.concatenate([c0, c1], axis=1))
        cs_c.append(jnp.where(left, c0, c1))
        dt_c.append(jnp.where(left, lanes(dt, 2 * p), lanes(dt, 2 * p + 1)))
    cs_c = jnp.concatenate(cs_c, axis=1)
    dt_c = jnp.concatenate(dt_c, axis=1)
    return dt, a, cs, cs.T, causal, tri, dt_c, jnp.exp(cs_c), jnp.exp(cs_c[CHUNK - 1:CHUNK, :] - cs_c), pair_cols


def _pair_decay(cols, cst, pair, causal2):
    rows = jnp.concatenate([jnp.broadcast_to(cst[2 * pair:2 * pair + 1, :], (CHUNK, CHUNK)),
                            jnp.broadcast_to(cst[2 * pair + 1:2 * pair + 2, :], (CHUNK, CHUNK))], axis=1)
    return jnp.exp(jnp.where(causal2, cols - rows, -jnp.inf))


def _stack_heads(xp, left):
    return jnp.concatenate([jnp.where(left, xp, 0.0), jnp.where(left, 0.0, xp)], axis=0).astype(BF16)


def _ssd_fwd(xbc, z, dt_raw, par, dsk, normw, seq):
    t = xbc.shape[0]
    nc = seq // CHUNK
    n_chunks = t // CHUNK

    def body(xbc_ref, z_ref, dt_ref, par_ref, dsk_ref, nw_ref, yraw_ref, ycat_ref, hprev_ref, h_ref):
        @pl.when(pl.program_id(0) % nc == 0)
        def _():
            h_ref[...] = jnp.zeros_like(h_ref)
        hprev_ref[0] = h_ref[...]
        _, _, cs, cst, causal, _, dt_c, ecs_c, w_c, pair_cols = _ssd_prep(dt_ref[...], par_ref[...])
        cs_last = cs[CHUNK - 1:CHUNK, :]
        causal2 = jnp.concatenate([causal, causal], axis=1)
        left = lax.broadcasted_iota(jnp.int32, (CHUNK, 128), 1) < HEADDIM
        x = xbc_ref[:, 0:D_SSD]
        xdt = x * dt_c
        amat = (w_c * xdt).astype(BF16)
        zz = z_ref[...]
        silu_z = zz * _sigmoid(zz)
        for g in range(N_GROUPS):
            gs = slice(g * GW, (g + 1) * GW)
            bg = xbc_ref[:, D_SSD + g * N_STATE:D_SSD + (g + 1) * N_STATE].astype(BF16)
            cg = xbc_ref[:, D_SSD + (N_GROUPS + g) * N_STATE:D_SSD + (N_GROUPS + g + 1) * N_STATE].astype(BF16)
            scores = lax.dot_general(cg, bg, NT, preferred_element_type=F32)
            scores2 = jnp.concatenate([scores, scores], axis=1)
            hg = h_ref[gs, :]
            p_all = lax.dot_general(cg, hg.astype(BF16), NT, preferred_element_type=F32)
            ys = []
            for q in range(GW // 128):
                pair = g * (GW // 128) + q
                decay = _pair_decay(pair_cols[pair], cst, pair, causal2)
                mcat = (scores2 * decay).astype(BF16)
                ys.append(jnp.dot(mcat, _stack_heads(xdt[:, pair * 128:(pair + 1) * 128], left),
                                  preferred_element_type=F32))
            yg = jnp.concatenate(ys, axis=1) + ecs_c[:, gs] * p_all + x[:, gs] * dsk_ref[:, gs]
            s_new = lax.dot_general(amat[:, gs], bg, TN, preferred_element_type=F32)
            for j in range(HPG):
                hh = g * HPG + j
                js = slice(j * HEADDIM, (j + 1) * HEADDIM)
                h_ref[g * GW + j * HEADDIM:g * GW + (j + 1) * HEADDIM, :] = (
                    hg[js, :] * jnp.exp(cs_last[:, hh:hh + 1]) + s_new[js, :])
            yraw_ref[:, gs] = yg
            v = yg * silu_z[:, gs]
            r = lax.rsqrt(jnp.mean(v * v, axis=-1, keepdims=True) + EPS)
            ycat_ref[:, gs] = (v * r * nw_ref[:, gs]).astype(BF16)

    row = lambda w: pl.BlockSpec((CHUNK, w), lambda i: (i, 0))
    full = lambda s: pl.BlockSpec(s, lambda i: (0,) * len(s))
    return pl.pallas_call(
        body, name="ssd_fwd", grid=(n_chunks,),
        out_shape=(jax.ShapeDtypeStruct((t, D_SSD), F32), jax.ShapeDtypeStruct((t, D_SSD + D_S5), BF16),
                   jax.ShapeDtypeStruct((n_chunks, D_SSD, N_STATE), F32)),
        in_specs=[row(D_XBC), row(D_SSD), row(DT_PAD), full((8, 128)), full((1, D_SSD)), full((1, D_SSD))],
        out_specs=(row(D_SSD), row(D_SSD), pl.BlockSpec((1, D_SSD, N_STATE), lambda i: (i, 0, 0))),
        scratch_shapes=[pltpu.VMEM((D_SSD, N_STATE), F32)],
        compiler_params=_params(40),
    )(xbc, z, dt_raw, par, dsk, normw)


S5_CW = 512
S5_BLOCKS = 4


def _tile_scan(in_re, in_im, out_re, out_im, carry_re, carry_im, pw_re, pw_im, n_tiles, reverse):
    steps = (1, 2, 4)
    for cc in range(S5_N // S5_CW):
        cols = slice(cc * S5_CW, (cc + 1) * S5_CW)
        a_re, a_im = pw_re[:, cols], pw_im[:, cols]
        rid = lax.broadcasted_iota(jnp.int32, (8, S5_CW), 0)
        pows = []
        for d in steps:
            k = 8 - d if reverse else d - 1
            keep = (rid < 8 - d) if reverse else (rid >= d)
            pows.append((jnp.where(keep, pw_re[k:k + 1, cols], 0.0), jnp.where(keep, pw_im[k:k + 1, cols], 0.0)))

        def tile(i, carry, cols=cols, pows=pows, a_re=a_re, a_im=a_im):
            r = (n_tiles - 1 - i) if reverse else i
            rows = pl.ds(pl.multiple_of(r * 8, 8), 8)
            xr, xi = in_re[rows, cols], in_im[rows, cols]
            for (pr, pi), d in zip(pows, steps):
                shift = 8 - d if reverse else d
                sr, si = pltpu.roll(xr, shift, axis=0), pltpu.roll(xi, shift, axis=0)
                xr, xi = xr + pr * sr - pi * si, xi + pr * si + pi * sr
            cr, ci = carry
            xr, xi = xr + a_re * cr - a_im * ci, xi + a_re * ci + a_im * cr
            out_re[rows, cols] = xr
            out_im[rows, cols] = xi
            edge = slice(0, 1) if reverse else slice(7, 8)
            return (jnp.broadcast_to(xr[edge], (8, S5_CW)), jnp.broadcast_to(xi[edge], (8, S5_CW)))

        c0 = (jnp.broadcast_to(carry_re[0:1, cols], (8, S5_CW)), jnp.broadcast_to(carry_im[0:1, cols], (8, S5_CW)))
        cr, ci = lax.fori_loop(0, n_tiles, tile, c0)
        carry_re[:, cols] = cr
        carry_im[:, cols] = ci


def _s5_params_math(ar, ai, ldt, br, bi):
    dt = jnp.exp(ldt)
    mag = jnp.exp(ar * dt)
    ang = ai * dt
    ab_re = mag * jnp.cos(ang)
    ab_im = mag * jnp.sin(ang)
    den = ar * ar + ai * ai
    n_re = ab_re - 1.0
    coef_re = (n_re * ar + ab_im * ai) / den
    coef_im = (ab_im * ar - n_re * ai) / den
    bb_re = coef_re * br - coef_im * bi
    bb_im = coef_re * bi + coef_im * br
    return ab_re, ab_im, bb_re, bb_im


def _s5_params_fwd(ar, ai, ldt, br, bi):
    def body(ar_ref, ai_ref, ldt_ref, br_ref, bi_ref, bbr_ref, bbi_ref, pfr_ref, pfi_ref, prr_ref, pri_ref):
        ab_re, ab_im, bb_re, bb_im = _s5_params_math(ar_ref[...], ai_ref[...], ldt_ref[...], br_ref[...], bi_ref[...])
        bbr_ref[...] = bb_re
        bbi_ref[...] = bb_im
        pr, pi = ab_re, ab_im
        for k in range(8):
            pfr_ref[k:k + 1, :] = pr
            pfi_ref[k:k + 1, :] = pi
            prr_ref[7 - k:8 - k, :] = pr
            pri_ref[7 - k:8 - k, :] = -pi
            pr, pi = pr * ab_re - pi * ab_im, pr * ab_im + pi * ab_re

    b16 = jax.ShapeDtypeStruct((S5_CH, S5_N), F32)
    p8 = jax.ShapeDtypeStruct((8, S5_N), F32)
    return pl.pallas_call(body, name="s5_params_fwd", out_shape=(b16, b16, p8, p8, p8, p8),
                          compiler_params=_params(32))(ar, ai, ldt, br, bi)


def _s5_params_bwd(ar, ai, ldt, br, bi, d_ab_re, d_ab_im, d_bb_re, d_bb_im):
    def body(ar_ref, ai_ref, ldt_ref, br_ref, bi_ref, dar_ref, dai_ref, dbr_ref, dbi_ref,
             gar_ref, gai_ref, gldt_ref, gbr_ref, gbi_ref):
        _, vjp = jax.vjp(_s5_params_math, ar_ref[...], ai_ref[...], ldt_ref[...], br_ref[...], bi_ref[...])
        g_ar, g_ai, g_ldt, g_br, g_bi = vjp((dar_ref[...], dai_ref[...], dbr_ref[...], dbi_ref[...]))
        gar_ref[...] = g_ar
        gai_ref[...] = g_ai
        gbr_ref[...] = g_br
        gbi_ref[...] = g_bi
        lane = lax.broadcasted_iota(jnp.int32, (S5_N, 128), 0) // S5_P
        grp = lax.broadcasted_iota(jnp.int32, (S5_N, 128), 1)
        fold = (lane == grp).astype(F32)
        gldt_ref[...] = jnp.dot(g_ldt, fold, preferred_element_type=F32, precision=HIGHEST)

    v1 = jax.ShapeDtypeStruct((1, S5_N), F32)
    b16 = jax.ShapeDtypeStruct((S5_CH, S5_N), F32)
    return pl.pallas_call(body, name="s5_params_bwd",
                          out_shape=(v1, v1, jax.ShapeDtypeStruct((1, 128), F32), b16, b16),
                          compiler_params=_params(32))(ar, ai, ldt, br, bi, d_ab_re, d_ab_im, d_bb_re, d_bb_im)


def _s5_fwd(u5, bb_re, bb_im, cc_re, cc_im, pf_re, pf_im, s5d, w_glu, b_glu, ycat, seq):
    t = u5.shape[0]
    tb = 256
    npb = seq // tb

    def body(u_ref, bbr_ref, bbi_ref, ccr_ref, cci_ref, pfr_ref, pfi_ref, d_ref, wg_ref, bg_ref, ycat_hbm,
             sre_ref, sim_ref, ypre_ref, y5_ref, bur, bui, car, cai):
        del ycat_hbm

        @pl.when(pl.program_id(0) % npb == 0)
        def _():
            car[...] = jnp.zeros_like(car)
            cai[...] = jnp.zeros_like(cai)
        u = u_ref[...]
        ub = u.astype(BF16)
        for j in range(S5_BLOCKS):
            ch, st = slice(j * 128, (j + 1) * 128), slice(j * 512, (j + 1) * 512)
            bur[:, st] = jnp.dot(ub[:, ch], bbr_ref[j], preferred_element_type=F32)
            bui[:, st] = jnp.dot(ub[:, ch], bbi_ref[j], preferred_element_type=F32)
        _tile_scan(bur, bui, sre_ref, sim_ref, car, cai, pfr_ref, pfi_ref, tb // 8, reverse=False)
        cs_y = []
        for j in range(S5_BLOCKS):
            st = slice(j * 512, (j + 1) * 512)
            cs_y.append(_mm(sre_ref[:, st], ccr_ref[j]) - _mm(sim_ref[:, st], cci_ref[j]))
        ypre = jnp.concatenate(cs_y, axis=1) + u * d_ref[...]
        ypre_ref[...] = ypre
        yg = _gelu(ypre)
        y5_ref[...] = (yg * _sigmoid(_mm(yg, wg_ref[...]) + bg_ref[...])).astype(BF16)

    row = lambda w: pl.BlockSpec((tb, w), lambda i: (i, 0))
    full = lambda a: pl.BlockSpec(a.shape, lambda i: (0,) * a.ndim)
    return pl.pallas_call(
        body, name="s5_fwd", grid=(t // tb,),
        out_shape=(jax.ShapeDtypeStruct((t, S5_N), F32), jax.ShapeDtypeStruct((t, S5_N), F32),
                   jax.ShapeDtypeStruct((t, D_S5), F32), jax.ShapeDtypeStruct(ycat.shape, BF16)),
        in_specs=[row(D_S5), full(bb_re), full(bb_im), full(cc_re), full(cc_im), full(pf_re), full(pf_im),
                  full(s5d), full(w_glu), full(b_glu), ANY],
        out_specs=(row(S5_N), row(S5_N), row(D_S5), pl.BlockSpec((tb, D_S5), lambda i: (i, D_SSD // D_S5))),
        input_output_aliases={10: 3},
        scratch_shapes=[pltpu.VMEM((tb, S5_N), F32), pltpu.VMEM((tb, S5_N), F32),
                        pltpu.VMEM((8, S5_N), F32), pltpu.VMEM((8, S5_N), F32)],
        compiler_params=_params(48),
    )(u5, bb_re, bb_im, cc_re, cc_im, pf_re, pf_im, s5d, w_glu, b_glu, ycat)


def _layer_norm(r, g, b):
    mu = jnp.mean(r, axis=-1, keepdims=True)
    xc = r - mu
    rstd = lax.rsqrt(jnp.mean(xc * xc, axis=-1, keepdims=True) + EPS)
    xhat = xc * rstd
    return xhat * g + b, xhat, rstd


def _layer_norm_bwd(dy, xhat, rstd, g):
    dxhat = dy * g
    return rstd * (dxhat - jnp.mean(dxhat, axis=-1, keepdims=True)
                   - xhat * jnp.mean(dxhat * xhat, axis=-1, keepdims=True))


def _out_ln1(ycat, x2, mod3, w_out, ln1, seq):
    t = x2.shape[0]
    tb = 512
    npb = seq // tb

    def body(y_ref, x_ref, mod_ref, w_ref, ln_ref, mix_ref, x1_ref):
        m = mod_ref[0]
        mix = jnp.dot(y_ref[...], w_ref[...], preferred_element_type=F32)
        mix_ref[...] = mix
        r1 = ALPHA * x_ref[...] + (1.0 + m[2:3]) * mix
        x1_ref[...] = _layer_norm(r1, ln_ref[0:1], ln_ref[1:2])[0]

    row = lambda w: pl.BlockSpec((tb, w), lambda i: (i, 0))
    return pl.pallas_call(
        body, name="out_ln1", grid=(t // tb,),
        out_shape=(jax.ShapeDtypeStruct((t, D_MODEL), F32), jax.ShapeDtypeStruct((t, D_MODEL), F32)),
        in_specs=[row(D_SSD + D_S5), row(D_MODEL), pl.BlockSpec((1, N_MOD, D_MODEL), lambda i: (i // npb, 0, 0)),
                  pl.BlockSpec(w_out.shape, lambda i: (0, 0)), pl.BlockSpec(ln1.shape, lambda i: (0, 0))],
        out_specs=(row(D_MODEL), row(D_MODEL)), compiler_params=_params(48),
    )(ycat, x2, mod3, w_out, ln1)


def _mlp_fwd_bwd(x1, tgt, mod3, w1, w2, vec1, b1, seq):
    t = x1.shape[0]
    tb = 256
    npb = seq // tb
    n_fb, _, fb = w1.shape

    def body(x1_ref, tgt_ref, mod_ref, w1_hbm, w2_hbm, v_ref, b1_ref,
             dx1_ref, u2_ref, h_ref, dhp_ref, do_ref, gacc_ref, db1_ref, bacc_ref, w1_v, w2_v, sem1, sem2):
        i = pl.program_id(0)
        @pl.when(i == 0)
        def _():
            cps = [pltpu.make_async_copy(w1_hbm.at[k], w1_v.at[:, k * fb:(k + 1) * fb], sem1.at[k])
                   for k in range(n_fb)]
            for cp in cps:
                cp.start()
            for cp in cps:
                cp.wait()
        _load_once(w2_hbm, w2_v, sem2)

        @pl.when(i == 0)
        def _():
            gacc_ref[...] = jnp.zeros_like(gacc_ref)
            db1_ref[...] = jnp.zeros_like(db1_ref)

        @pl.when(i % npb == 0)
        def _():
            bacc_ref[...] = jnp.zeros_like(bacc_ref)

        m = mod_ref[0]
        sh2, sc2, g2 = m[3:4], m[4:5], m[5:6]
        x1v = x1_ref[...]
        u2 = (x1v * (1.0 + sc2) + sh2).astype(BF16)
        u2_ref[...] = u2
        hr = jnp.maximum(jnp.dot(u2, w1_v[...], preferred_element_type=F32) + b1_ref[...], 0.0)
        hb = (hr * hr).astype(BF16)
        h_ref[...] = hb
        o = jnp.dot(hb, w2_v[...], preferred_element_type=F32) + v_ref[0:1]
        r2 = ALPHA * x1v + (1.0 + g2) * o
        y, xhat, rstd = _layer_norm(r2, v_ref[1:2], v_ref[2:3])
        err = y - tgt_ref[...]
        dy = err * (1.0 / D_MODEL)
        dr2 = _layer_norm_bwd(dy, xhat, rstd, v_ref[1:2])
        do = (1.0 + g2) * dr2
        dob = do.astype(BF16)
        do_ref[...] = dob
        gacc_ref[0:1, :] += jnp.sum(dy * xhat, axis=0, keepdims=True)
        gacc_ref[1:2, :] += jnp.sum(dy, axis=0, keepdims=True)
        gacc_ref[2:3, :] += jnp.sum(do, axis=0, keepdims=True)
        gacc_ref[3:4, :] += jnp.sum(err * err, axis=0, keepdims=True)
        dhpre = lax.dot_general(dob, w2_v[...], NT, preferred_element_type=F32) * (2.0 * hr)
        dhpb = dhpre.astype(BF16)
        dhp_ref[...] = dhpb
        db1_ref[...] += jnp.sum(dhpre, axis=0, keepdims=True)
        du2 = lax.dot_general(dhpb, w1_v[...], NT, preferred_element_type=F32)
        dx1_ref[...] = ALPHA * dr2 + du2 * (1.0 + sc2)
        bacc_ref[0, 0:1, :] += jnp.sum(du2, axis=0, keepdims=True)
        bacc_ref[0, 1:2, :] += jnp.sum(du2 * x1v, axis=0, keepdims=True)
        bacc_ref[0, 2:3, :] += jnp.sum(dr2 * o, axis=0, keepdims=True)

    row = lambda w: pl.BlockSpec((tb, w), lambda i: (i, 0))
    return pl.pallas_call(
        body, name="mlp_fwd_bwd", grid=(t // tb,),
        out_shape=(jax.ShapeDtypeStruct((t, D_MODEL), F32), jax.ShapeDtypeStruct((t, D_MODEL), BF16),
                   jax.ShapeDtypeStruct((t, D_FF), BF16), jax.ShapeDtypeStruct((t, D_FF), BF16),
                   jax.ShapeDtypeStruct((t, D_MODEL), BF16), jax.ShapeDtypeStruct((8, D_MODEL), F32),
                   jax.ShapeDtypeStruct((1, D_FF), F32), jax.ShapeDtypeStruct((t // seq, 8, D_MODEL), F32)),
        in_specs=[row(D_MODEL), row(D_MODEL), pl.BlockSpec((1, N_MOD, D_MODEL), lambda i: (i // npb, 0, 0)), ANY, ANY,
                  pl.BlockSpec(vec1.shape, lambda i: (0, 0)), pl.BlockSpec(b1.shape, lambda i: (0, 0))],
        out_specs=(row(D_MODEL), row(D_MODEL), row(D_FF), row(D_FF), row(D_MODEL),
                   pl.BlockSpec((8, D_MODEL), lambda i: (0, 0)), pl.BlockSpec((1, D_FF), lambda i: (0, 0)),
                   pl.BlockSpec((1, 8, D_MODEL), lambda i: (i // npb, 0, 0))),
        scratch_shapes=[pltpu.VMEM((D_MODEL, n_fb * fb), BF16), pltpu.VMEM((D_FF, D_MODEL), BF16),
                        pltpu.SemaphoreType.DMA((n_fb,)), pltpu.SemaphoreType.DMA],
        compiler_params=_params(60),
    )(x1, tgt, mod3, w1, w2, vec1, b1)


def _ln1_out_bwd(dx1, x2, mix, mod3, w_out, ln1, seq):
    t = x2.shape[0]
    tb = 512
    npb = seq // tb

    def body(dx1_ref, x_ref, mix_ref, mod_ref, w_ref, ln_ref, dmix_ref, dxa_ref, dys_ref, dy5_ref, gacc_ref, bacc_ref):
        i = pl.program_id(0)

        @pl.when(i == 0)
        def _():
            gacc_ref[...] = jnp.zeros_like(gacc_ref)

        @pl.when(i % npb == 0)
        def _():
            bacc_ref[...] = jnp.zeros_like(bacc_ref)

        m = mod_ref[0]
        mix = mix_ref[...]
        r1 = ALPHA * x_ref[...] + (1.0 + m[2:3]) * mix
        _, xhat, rstd = _layer_norm(r1, ln_ref[0:1], ln_ref[1:2])
        dx1v = dx1_ref[...]
        dr1 = _layer_norm_bwd(dx1v, xhat, rstd, ln_ref[0:1])
        gacc_ref[0:1, :] += jnp.sum(dx1v * xhat, axis=0, keepdims=True)
        gacc_ref[1:2, :] += jnp.sum(dx1v, axis=0, keepdims=True)
        bacc_ref[0, 0:1, :] += jnp.sum(dr1 * mix, axis=0, keepdims=True)
        dmix = ((1.0 + m[2:3]) * dr1).astype(BF16)
        dmix_ref[...] = dmix
        dxa_ref[...] = ALPHA * dr1
        dys_ref[...] = lax.dot_general(dmix, w_ref[0:D_SSD, :], NT, preferred_element_type=F32)
        dy5_ref[...] = lax.dot_general(dmix, w_ref[D_SSD:, :], NT, preferred_element_type=F32)

    row = lambda w: pl.BlockSpec((tb, w), lambda i: (i, 0))
    return pl.pallas_call(
        body, name="ln1_out_bwd", grid=(t // tb,),
        out_shape=(jax.ShapeDtypeStruct((t, D_MODEL), BF16), jax.ShapeDtypeStruct((t, D_MODEL), F32),
                   jax.ShapeDtypeStruct((t, D_SSD), F32), jax.ShapeDtypeStruct((t, D_S5), F32),
                   jax.ShapeDtypeStruct((8, D_MODEL), F32), jax.ShapeDtypeStruct((t // seq, 8, D_MODEL), F32)),
        in_specs=[row(D_MODEL), row(D_MODEL), row(D_MODEL), pl.BlockSpec((1, N_MOD, D_MODEL), lambda i: (i // npb, 0, 0)),
                  pl.BlockSpec(w_out.shape, lambda i: (0, 0)), pl.BlockSpec(ln1.shape, lambda i: (0, 0))],
        out_specs=(row(D_MODEL), row(D_MODEL), row(D_SSD), row(D_S5), pl.BlockSpec((8, D_MODEL), lambda i: (0, 0)),
                   pl.BlockSpec((1, 8, D_MODEL), lambda i: (i // npb, 0, 0))),
        compiler_params=_params(48),
    )(dx1, x2, mix, mod3, w_out, ln1)


def _s5_bwd(dy5, ypre, u5, s_re, s_im, bb_re, bb_im, cc_re, cc_im, pr_re, pr_im, s5d, w_glu, b_glu, seq):
    t = u5.shape[0]
    tb = 256
    npb = seq // tb
    n_blocks = t // tb

    def blk(i):
        return (i // npb) * npb + (npb - 1 - i % npb)

    def body(dy_ref, ypre_ref, u_ref, sre_ref, sim_ref, hre_ref, him_ref, bbr_ref, bbi_ref, ccr_ref, cci_ref,
             prr_ref, pri_ref, d_ref, wg_ref, bg_ref,
             du_ref, vacc_ref, sacc_ref, dcc_ref, dbb_ref, dwg_ref, dsr, dsi, gr, gi, car, cai):
        i = pl.program_id(0)

        @pl.when(i == 0)
        def _():
            for acc in (vacc_ref, sacc_ref, dcc_ref, dbb_ref, dwg_ref):
                acc[...] = jnp.zeros_like(acc)

        @pl.when(i % npb == 0)
        def _():
            car[...] = jnp.zeros_like(car)
            cai[...] = jnp.zeros_like(cai)

        dy = dy_ref[...]
        ypre = ypre_ref[...]
        u = u_ref[...]
        ub = u.astype(BF16)
        yg = _gelu(ypre)
        sg = _sigmoid(_mm(yg, wg_ref[...]) + bg_ref[...])
        dq = dy * yg * sg * (1.0 - sg)
        dqb = dq.astype(BF16)
        dyg = dy * sg + lax.dot_general(dqb, wg_ref[...], NT, preferred_element_type=F32)
        dyp = dyg * _gelu_grad(ypre)
        dypb = dyp.astype(BF16)
        dwg_ref[...] += lax.dot_general(yg.astype(BF16), dqb, TN, preferred_element_type=F32)
        blocks = [(slice(j * 128, (j + 1) * 128), slice(j * 512, (j + 1) * 512)) for j in range(S5_BLOCKS)]
        for j, (ch, st) in enumerate(blocks):
            dsr[:, st] = lax.dot_general(dypb[:, ch], ccr_ref[j], NT, preferred_element_type=F32)
            dsi[:, st] = -lax.dot_general(dypb[:, ch], cci_ref[j], NT, preferred_element_type=F32)
        _tile_scan(dsr, dsi, gr, gi, car, cai, prr_ref, pri_ref, tb // 8, reverse=True)
        g_re, g_im = gr[...], gi[...]
        first_rows = (i % npb) == npb - 1
        hre = jnp.where(first_rows, 0.0, hre_ref[...])
        him = jnp.where(first_rows, 0.0, him_ref[...])
        s_re_v, s_im_v = sre_ref[...], sim_ref[...]
        sp_re = pltpu.roll(jnp.concatenate([hre, s_re_v], axis=0), 1, axis=0)[8:8 + tb]
        sp_im = pltpu.roll(jnp.concatenate([him, s_im_v], axis=0), 1, axis=0)[8:8 + tb]
        vacc_ref[0:1, :] += jnp.sum(g_re * sp_re + g_im * sp_im, axis=0, keepdims=True)
        vacc_ref[1:2, :] += jnp.sum(g_im * sp_re - g_re * sp_im, axis=0, keepdims=True)
        grb, gib = g_re.astype(BF16), g_im.astype(BF16)
        srb, sib = s_re_v.astype(BF16), s_im_v.astype(BF16)
        du_cols = []
        for j, (ch, st) in enumerate(blocks):
            dcc_ref[j] += lax.dot_general(srb[:, st], dypb[:, ch], TN, preferred_element_type=F32)
            dcc_ref[S5_BLOCKS + j] -= lax.dot_general(sib[:, st], dypb[:, ch], TN, preferred_element_type=F32)
            dbb_ref[j] += lax.dot_general(ub[:, ch], grb[:, st], TN, preferred_element_type=F32)
            dbb_ref[S5_BLOCKS + j] += lax.dot_general(ub[:, ch], gib[:, st], TN, preferred_element_type=F32)
            du_cols.append(lax.dot_general(grb[:, st], bbr_ref[j], NT, preferred_element_type=F32)
                           + lax.dot_general(gib[:, st], bbi_ref[j], NT, preferred_element_type=F32))
        du_ref[...] = jnp.concatenate(du_cols, axis=1) + dyp * d_ref[...]
        sacc_ref[0:1, :] += jnp.sum(dyp * u, axis=0, keepdims=True)
        sacc_ref[1:2, :] += jnp.sum(dq, axis=0, keepdims=True)

    row = lambda w: pl.BlockSpec((tb, w), lambda i: (blk(i), 0))
    halo = pl.BlockSpec((8, S5_N), lambda i: (jnp.maximum(blk(i) * (tb // 8) - 1, 0), 0))
    full = lambda a: pl.BlockSpec(a.shape, lambda i: (0,) * a.ndim)
    acc = lambda s: pl.BlockSpec(s, lambda i: (0,) * len(s))
    acc_shapes = [(8, S5_N), (8, D_S5), (2 * S5_BLOCKS, 512, 128), (2 * S5_BLOCKS, 128, 512), (D_S5, D_S5)]
    return pl.pallas_call(
        body, name="s5_bwd", grid=(n_blocks,),
        out_shape=(jax.ShapeDtypeStruct((t, D_S5), F32),) + tuple(jax.ShapeDtypeStruct(s, F32) for s in acc_shapes),
        in_specs=[row(D_S5), row(D_S5), row(D_S5), row(S5_N), row(S5_N), halo, halo, full(bb_re), full(bb_im),
                  full(cc_re), full(cc_im), full(pr_re), full(pr_im), full(s5d), full(w_glu), full(b_glu)],
        out_specs=(row(D_S5),) + tuple(acc(s) for s in acc_shapes),
        scratch_shapes=[pltpu.VMEM((tb, S5_N), F32), pltpu.VMEM((tb, S5_N), F32), pltpu.VMEM((tb, S5_N), F32),
                        pltpu.VMEM((tb, S5_N), F32), pltpu.VMEM((8, S5_N), F32), pltpu.VMEM((8, S5_N), F32)],
        compiler_params=_params(56),
    )(dy5, ypre, u5, s_re, s_im, s_re, s_im, bb_re, bb_im, cc_re, cc_im, pr_re, pr_im, s5d, w_glu, b_glu)


def _ssd_bwd(dyssd, yraw, z, xbc, dt_raw, hprev, par, dsk, normw, seq):
    t = xbc.shape[0]
    nc = seq // CHUNK
    n_chunks = t // CHUNK
    fold = _head_fold()

    def blk(i):
        return (i // nc) * nc + (nc - 1 - i % nc)

    def body(dy_ref, yraw_ref, z_ref, xbc_ref, dt_ref, hprev_ref, par_ref, dsk_ref, nw_ref, fold_ref,
             dxbc_ref, dz_ref, ddt_ref, dpar_ref, cacc_ref, dh_ref, dyr_ref):
        i = pl.program_id(0)

        @pl.when(i == 0)
        def _():
            dpar_ref[...] = jnp.zeros_like(dpar_ref)
            cacc_ref[...] = jnp.zeros_like(cacc_ref)

        @pl.when(i % nc == 0)
        def _():
            dh_ref[...] = jnp.zeros_like(dh_ref)

        zz = z_ref[...]
        sz = _sigmoid(zz)
        silu_z = zz * sz
        yraw = yraw_ref[...]
        for g in range(N_GROUPS):
            sl = slice(g * GW, (g + 1) * GW)
            v = yraw[:, sl] * silu_z[:, sl]
            r = lax.rsqrt(jnp.mean(v * v, axis=-1, keepdims=True) + EPS)
            dyg = dy_ref[:, sl]
            cacc_ref[1:2, sl] += jnp.sum(dyg * v * r, axis=0, keepdims=True)
            dyw = dyg * nw_ref[:, sl]
            dv = r * dyw - v * (r * r * r) * jnp.mean(dyw * v, axis=-1, keepdims=True)
            dyr_ref[:, sl] = dv * silu_z[:, sl]
            dz_ref[:, sl] = dv * yraw[:, sl] * (sz[:, sl] * (1.0 + zz[:, sl] * (1.0 - sz[:, sl])))

        dt, a, cs, cst, causal, tri, dt_c, ecs_c, w_c, pair_cols = _ssd_prep(dt_ref[...], par_ref[...])
        cs_last = cs[CHUNK - 1:CHUNK, :]
        causal2 = jnp.concatenate([causal, causal], axis=1)
        lane = lax.broadcasted_iota(jnp.int32, (CHUNK, 128), 1)
        left = lane < HEADDIM
        lane1 = lax.broadcasted_iota(jnp.int32, (1, 128), 1)
        x = xbc_ref[:, 0:D_SSD]
        xdt = x * dt_c
        dyr = dyr_ref[...]
        dyrb = dyr.astype(BF16)
        cacc_ref[0:1, :] += jnp.sum(dyr * x, axis=0, keepdims=True)
        dlast = jnp.zeros((1, 128), F32)
        dxdt_cols, diag_all, dww_cols = [], [], []
        for g in range(N_GROUPS):
            gs = slice(g * GW, (g + 1) * GW)
            b_sl = slice(D_SSD + g * N_STATE, D_SSD + (g + 1) * N_STATE)
            c_sl = slice(D_SSD + (N_GROUPS + g) * N_STATE, D_SSD + (N_GROUPS + g + 1) * N_STATE)
            bg = xbc_ref[:, b_sl].astype(BF16)
            cg = xbc_ref[:, c_sl].astype(BF16)
            scores = lax.dot_general(cg, bg, NT, preferred_element_type=F32)
            scores2 = jnp.concatenate([scores, scores], axis=1)
            hg = hprev_ref[0, gs, :]
            hgb = hg.astype(BF16)
            dhg = dh_ref[gs, :]
            dhgb = dhg.astype(BF16)
            q_all = lax.dot_general(bg, dhgb, NT, preferred_element_type=F32)
            dscores = jnp.zeros((CHUNK, CHUNK), F32)
            diag_cols = []
            for q in range(GW // 128):
                pair = g * (GW // 128) + q
                ps = slice(pair * 128, (pair + 1) * 128)
                decay = _pair_decay(pair_cols[pair], cst, pair, causal2)
                mcat = (scores2 * decay).astype(BF16)
                dyp = dyrb[:, ps]
                dm = lax.dot_general(dyp, _stack_heads(xdt[:, ps], left), NT, preferred_element_type=F32)
                dmd = dm * decay
                dscores = dscores + dmd[:, 0:CHUNK] + dmd[:, CHUNK:]
                rr = lax.dot_general(mcat, dyp, TN, preferred_element_type=F32)
                diag_cols.append(jnp.where(left, rr[0:CHUNK], rr[CHUNK:]))
            wq = w_c[:, gs] * q_all
            diag_g = jnp.concatenate(diag_cols, axis=1)
            diag_all.append(diag_g)
            dxdt_cols.append(diag_g + wq)
            dww_cols.append(wq * xdt[:, gs])
            dp = (ecs_c[:, gs] * dyr[:, gs]).astype(BF16)
            amat = (w_c[:, gs] * xdt[:, gs]).astype(BF16)
            dsb = dscores.astype(BF16)
            dxbc_ref[:, c_sl] = (jnp.dot(dsb, bg, preferred_element_type=F32)
                                 + jnp.dot(dp, hgb, preferred_element_type=F32))
            dxbc_ref[:, b_sl] = (lax.dot_general(dsb, cg, TN, preferred_element_type=F32)
                                 + jnp.dot(amat, dhgb, preferred_element_type=F32))
            dh_in = lax.dot_general(dp, cg, TN, preferred_element_type=F32)
            for j in range(HPG):
                hh = g * HPG + j
                js = slice(j * HEADDIM, (j + 1) * HEADDIM)
                ecl = jnp.exp(cs_last[:, hh:hh + 1])
                dlast = dlast + jnp.where(lane1 == hh, ecl * jnp.sum(dhg[js, :] * hg[js, :]), 0.0)
                dh_ref[g * GW + j * HEADDIM:g * GW + (j + 1) * HEADDIM, :] = ecl * dhg[js, :] + dh_in[js, :]
        dxdt = jnp.concatenate(dxdt_cols, axis=1)
        dxbc_ref[:, 0:D_SSD] = dxdt * dt_c + dyr * dsk_ref[...]
        dww = _dot3(jnp.concatenate(dww_cols, axis=1), fold_ref[...])
        dcs = (_dot3(dyrb.astype(F32) * (yraw - x * dsk_ref[...]), fold_ref[...])
               - _dot3(xdt.astype(BF16).astype(F32) * jnp.concatenate(diag_all, axis=1), fold_ref[...]) - dww)
        rowid = lax.broadcasted_iota(jnp.int32, (CHUNK, 128), 0)
        dcs = dcs + jnp.where(rowid == CHUNK - 1, jnp.sum(dww, axis=0, keepdims=True) + dlast, 0.0)
        dadt = _dot3_left(tri, dcs, TN)
        ddt = _dot3(dxdt * x, fold_ref[...]) + dadt * a
        da = jnp.sum(dadt * dt, axis=0, keepdims=True)
        ddt_raw = ddt * _sigmoid(dt_ref[...] + par_ref[0:1])
        ddt_raw = jnp.where(lane < N_HEADS, ddt_raw, 0.0)
        ddt_ref[...] = ddt_raw
        dpar_ref[0:1, :] += jnp.sum(ddt_raw, axis=0, keepdims=True)
        dpar_ref[1:2, :] += jnp.where(lane1 < N_HEADS, da * a, 0.0)

    row = lambda w: pl.BlockSpec((CHUNK, w), lambda i: (blk(i), 0))
    full = lambda s: pl.BlockSpec(s, lambda i: (0,) * len(s))
    return pl.pallas_call(
        body, name="ssd_bwd", grid=(n_chunks,),
        out_shape=(jax.ShapeDtypeStruct((t, D_XBC), F32), jax.ShapeDtypeStruct((t, D_SSD), F32),
                   jax.ShapeDtypeStruct((t, DT_PAD), F32), jax.ShapeDtypeStruct((8, 128), F32),
                   jax.ShapeDtypeStruct((8, D_SSD), F32)),
        in_specs=[row(D_SSD), row(D_SSD), row(D_SSD), row(D_XBC), row(DT_PAD),
                  pl.BlockSpec((1, D_SSD, N_STATE), lambda i: (blk(i), 0, 0)),
                  full((8, 128)), full((1, D_SSD)), full((1, D_SSD)), full(fold.shape)],
        out_specs=(row(D_XBC), row(D_SSD), row(DT_PAD), full((8, 128)), full((8, D_SSD))),
        scratch_shapes=[pltpu.VMEM((D_SSD, N_STATE), F32), pltpu.VMEM((CHUNK, D_SSD), F32)],
        compiler_params=_params(48),
    )(dyssd, yraw, z, xbc, dt_raw, hprev, par, dsk, normw, fold)


def _conv_bwd(dxbc, xbc_pre, conv_w, conv_b, seq):
    t = xbc_pre.shape[0]
    tb = 512
    npb = seq // tb
    cw = 640

    def body(d_ref, cur_ref, halo_ref, w_ref, b_ref, o_ref, acc_ref, win):
        i = pl.program_id(1)

        @pl.when(i == 0)
        def _():
            acc_ref[...] = jnp.zeros_like(acc_ref)

        first = (i % npb) == 0
        win[0:8, :] = jnp.where(first, 0.0, halo_ref[...])
        win[8:8 + tb, :] = cur_ref[...]
        pre, shifted = _conv_taps(win, w_ref[...], tb, slice(None))
        pre = pre + b_ref[...]
        sg = _sigmoid(pre)
        dpre = d_ref[...] * (sg * (1.0 + pre * (1.0 - sg)))
        o_ref[...] = dpre
        for j in range(4):
            acc_ref[3 - j:4 - j, :] += jnp.sum(dpre * shifted[j], axis=0, keepdims=True)
        acc_ref[4:5, :] += jnp.sum(dpre, axis=0, keepdims=True)

    return pl.pallas_call(
        body, name="conv_bwd", grid=(D_XBC // cw, t // tb),
        out_shape=(jax.ShapeDtypeStruct((t, D_XBC), F32), jax.ShapeDtypeStruct((8, D_XBC), F32)),
        in_specs=[pl.BlockSpec((tb, cw), lambda j, i: (i, j)), pl.BlockSpec((tb, cw), lambda j, i: (i, j)),
                  pl.BlockSpec((8, cw), lambda j, i: (jnp.maximum(i * (tb // 8) - 1, 0), j)),
                  pl.BlockSpec((4, cw), lambda j, i: (0, j)), pl.BlockSpec((1, cw), lambda j, i: (0, j))],
        out_specs=(pl.BlockSpec((tb, cw), lambda j, i: (i, j)), pl.BlockSpec((8, cw), lambda j, i: (0, j))),
        scratch_shapes=[pltpu.VMEM((tb + 8, cw), F32)],
        compiler_params=_params(32),
    )(dxbc, xbc_pre, xbc_pre, conv_w, conv_b)


def _proj_bwd(dz, dpre, ddt, du5, x2, dxa, mod3, conv_w, w_in_pad, seq):
    t = x2.shape[0]
    tb = 512
    npb = seq // tb
    n_blocks = t // tb

    def body(dz_ref, dp_ref, nxt_ref, ddt_ref, du5_ref, x_ref, dxa_ref, mod_ref, cw_ref, w_hbm,
             gx_ref, u_ref, dxp_ref, bacc_ref, w_vmem, sem):
        i = pl.program_id(0)
        _load_once(w_hbm, w_vmem, sem)

        @pl.when(i % npb == 0)
        def _():
            bacc_ref[...] = jnp.zeros_like(bacc_ref)

        last = (i % npb) == npb - 1
        nxt = jnp.where(last, 0.0, nxt_ref[...])
        cur = dp_ref[...]
        xx = jnp.concatenate([cur, nxt], axis=0)
        w = cw_ref[...]
        dxp = w[3:4] * cur
        for j in (1, 2, 3):
            dxp = dxp + w[3 - j:4 - j] * pltpu.roll(xx, tb + 8 - j, axis=0)[0:tb]
        dxpb = dxp.astype(BF16)
        dxp_ref[...] = dxpb
        o1, o2, o3 = D_SSD, D_SSD + D_XBC, D_SSD + D_XBC + DT_PAD
        du = (lax.dot_general(dz_ref[...].astype(BF16), w_vmem[:, 0:o1], NT, preferred_element_type=F32)
              + lax.dot_general(dxpb, w_vmem[:, o1:o2], NT, preferred_element_type=F32)
              + lax.dot_general(ddt_ref[...].astype(BF16), w_vmem[:, o2:o3], NT, preferred_element_type=F32)
              + lax.dot_general(du5_ref[...].astype(BF16), w_vmem[:, o3:], NT, preferred_element_type=F32))
        m = mod_ref[0]
        xv = x_ref[...]
        u_ref[...] = (xv * (1.0 + m[1:2]) + m[0:1]).astype(BF16)
        gx_ref[...] = dxa_ref[...] + du * (1.0 + m[1:2])
        bacc_ref[0, 0:1, :] += jnp.sum(du, axis=0, keepdims=True)
        bacc_ref[0, 1:2, :] += jnp.sum(du * xv, axis=0, keepdims=True)

    row = lambda w: pl.BlockSpec((tb, w), lambda i: (i, 0))
    nxt_rows = pl.BlockSpec((8, D_XBC), lambda i: (jnp.minimum((i + 1) * (tb // 8), t // 8 - 1), 0))
    return pl.pallas_call(
        body, name="proj_bwd", grid=(n_blocks,),
        out_shape=(jax.ShapeDtypeStruct((t, D_MODEL), F32), jax.ShapeDtypeStruct((t, D_MODEL), BF16),
                   jax.ShapeDtypeStruct((t, D_XBC), BF16), jax.ShapeDtypeStruct((t // seq, 8, D_MODEL), F32)),
        in_specs=[row(D_SSD), row(D_XBC), nxt_rows, row(DT_PAD), row(D_S5), row(D_MODEL), row(D_MODEL),
                  pl.BlockSpec((1, N_MOD, D_MODEL), lambda i: (i // npb, 0, 0)),
                  pl.BlockSpec((4, D_XBC), lambda i: (0, 0)), ANY],
        out_specs=(row(D_MODEL), row(D_MODEL), row(D_XBC), pl.BlockSpec((1, 8, D_MODEL), lambda i: (i // npb, 0, 0))),
        scratch_shapes=[pltpu.VMEM((D_MODEL, D_INP), BF16), pltpu.SemaphoreType.DMA],
        compiler_params=_params(60),
    )(dz, dpre, dpre, ddt, du5, x2, dxa, mod3, conv_w, w_in_pad)


def _pad_rows(a, mult):
    r = a.shape[0]
    pad = (-r) % mult
    return a if pad == 0 else jnp.concatenate([a, jnp.zeros((pad,) + a.shape[1:], a.dtype)], axis=0)


_SMALL = ["conv_w", "conv_b", "dt_bias", "a_log", "d_ssd", "norm_w", "s5_a_re", "s5_a_im", "s5_log_dt", "s5_b_re",
          "s5_b_im", "s5_c_re", "s5_c_im", "s5_d", "b_glu", "ln1_g", "ln1_b", "b1", "b2", "ln2_g", "ln2_b"]


def _tile_rows(size):
    return 8 * (-(-size // 1024))


def _pack_small(d):
    parts = []
    for n in _SMALL:
        flat = d[n].reshape(-1).astype(F32)
        rows = _tile_rows(flat.shape[0])
        pad = rows * 128 - flat.shape[0]
        if pad:
            flat = jnp.concatenate([flat, jnp.zeros((pad,), F32)])
        parts.append(flat.reshape(rows, 128))
    return jnp.concatenate(parts, axis=0)


def _unpack_small(p, shapes):
    out, off = {}, 0
    for n in _SMALL:
        size = math.prod(shapes[n])
        rows = _tile_rows(size)
        out[n] = p[off:off + rows].reshape(-1)[:size].reshape(shapes[n])
        off += rows
    return out


def kernel(x, c, w_ada, b_ada, w_in, conv_w, conv_b, dt_bias, a_log, d_ssd, norm_w, s5_a_re, s5_a_im, s5_log_dt, s5_b_re, s5_b_im, s5_c_re, s5_c_im, s5_d, w_glu, b_glu, w_out, ln1_g, ln1_b, w1, b1, w2, b2, ln2_g, ln2_b, loss_target, m_w_ada, m_b_ada, m_w_in, m_conv_w, m_conv_b, m_dt_bias, m_a_log, m_d_ssd, m_norm_w, m_s5_a_re, m_s5_a_im, m_s5_log_dt, m_s5_b_re, m_s5_b_im, m_s5_c_re, m_s5_c_im, m_s5_d, m_w_glu, m_b_glu, m_w_out, m_ln1_g, m_ln1_b, m_w1, m_b1, m_w2, m_b2, m_ln2_g, m_ln2_b, v_w_ada, v_b_ada, v_w_in, v_conv_w, v_conv_b, v_dt_bias, v_a_log, v_d_ssd, v_norm_w, v_s5_a_re, v_s5_a_im, v_s5_log_dt, v_s5_b_re, v_s5_b_im, v_s5_c_re, v_s5_c_im, v_s5_d, v_w_glu, v_b_glu, v_w_out, v_ln1_g, v_ln1_b, v_w1, v_b1, v_w2, v_b2, v_ln2_g, v_ln2_b):
    weights = dict(w_ada=w_ada, b_ada=b_ada, w_in=w_in, conv_w=conv_w, conv_b=conv_b, dt_bias=dt_bias, a_log=a_log,
                   d_ssd=d_ssd, norm_w=norm_w, s5_a_re=s5_a_re, s5_a_im=s5_a_im, s5_log_dt=s5_log_dt, s5_b_re=s5_b_re,
                   s5_b_im=s5_b_im, s5_c_re=s5_c_re, s5_c_im=s5_c_im, s5_d=s5_d, w_glu=w_glu, b_glu=b_glu, w_out=w_out,
                   ln1_g=ln1_g, ln1_b=ln1_b, w1=w1, b1=b1, w2=w2, b2=b2, ln2_g=ln2_g, ln2_b=ln2_b)
    mom = dict(w_ada=m_w_ada, b_ada=m_b_ada, w_in=m_w_in, conv_w=m_conv_w, conv_b=m_conv_b, dt_bias=m_dt_bias,
               a_log=m_a_log, d_ssd=m_d_ssd, norm_w=m_norm_w, s5_a_re=m_s5_a_re, s5_a_im=m_s5_a_im,
               s5_log_dt=m_s5_log_dt, s5_b_re=m_s5_b_re, s5_b_im=m_s5_b_im, s5_c_re=m_s5_c_re, s5_c_im=m_s5_c_im,
               s5_d=m_s5_d, w_glu=m_w_glu, b_glu=m_b_glu, w_out=m_w_out, ln1_g=m_ln1_g, ln1_b=m_ln1_b, w1=m_w1, b1=m_b1,
               w2=m_w2, b2=m_b2, ln2_g=m_ln2_g, ln2_b=m_ln2_b)
    var = dict(w_ada=v_w_ada, b_ada=v_b_ada, w_in=v_w_in, conv_w=v_conv_w, conv_b=v_conv_b, dt_bias=v_dt_bias,
               a_log=v_a_log, d_ssd=v_d_ssd, norm_w=v_norm_w, s5_a_re=v_s5_a_re, s5_a_im=v_s5_a_im,
               s5_log_dt=v_s5_log_dt, s5_b_re=v_s5_b_re, s5_b_im=v_s5_b_im, s5_c_re=v_s5_c_re, s5_c_im=v_s5_c_im,
               s5_d=v_s5_d, w_glu=v_w_glu, b_glu=v_b_glu, w_out=v_w_out, ln1_g=v_ln1_g, ln1_b=v_ln1_b, w1=v_w1, b1=v_b1,
               w2=v_w2, b2=v_b2, ln2_g=v_ln2_g, ln2_b=v_ln2_b)
    names = list(weights)
    shapes = {n: weights[n].shape for n in names}

    nb, seq, _ = x.shape
    t = nb * seq
    dev = _dev_index()
    x2 = x.reshape(t, D_MODEL)
    tgt2 = loss_target.reshape(t, D_MODEL)

    cw_cols = conv_w.shape[2]
    small_in = jnp.concatenate([c.reshape(-1), conv_w.reshape(-1)]).reshape(-1, 128)
    big_names = ["w_in", "w_out", "w1", "w2", "w_glu"]
    shard_bf16 = {n: weights[n][0].astype(BF16) for n in big_names}
    first = _all_gather([small_in, shard_bf16["w_in"], shard_bf16["w_glu"]], "gather_first")
    small_all = first[0].reshape(N_DEV, -1)
    c_all = small_all[:, :nb * D_MODEL].reshape(N_DEV * nb, D_MODEL)
    conv_w_full = small_all[:, nb * D_MODEL:].reshape(N_DEV, 4, cw_cols).transpose(1, 0, 2).reshape(4, D_XBC)

    w_in_f = first[1].transpose(1, 0, 2).reshape(D_MODEL, D_IN)
    w_in_pad = jnp.concatenate(
        [w_in_f[:, :D_SSD + D_XBC], w_in_f[:, D_SSD + D_XBC:D_SSD + D_XBC + N_HEADS],
         jnp.zeros((D_MODEL, DT_PAD - N_HEADS), BF16), w_in_f[:, D_SSD + D_XBC + N_HEADS:]], axis=1)
    w_glu_f = first[2].reshape(D_S5, D_S5)
    late_names = ["w_out", "w1", "w2"]
    late_sems = _gather_start([shard_bf16[n] for n in late_names], "gather_late_start")

    ada_cols = w_ada.shape[2]
    b_cols = lax.dynamic_slice_in_dim(b_ada, dev * ada_cols, ada_cols, axis=1)
    mod_cols = _mod_fwd(c_all, w_ada[0], b_cols)
    mod_all = _all_gather([mod_cols], "gather_mod")[0]
    mod_mine = lax.dynamic_slice_in_dim(mod_all, dev * nb, nb, axis=1)
    mod3 = mod_mine.transpose(1, 0, 2).reshape(nb, N_MOD, D_MODEL) + late_sems[4][0, 0]

    def pad_lanes(v, n):
        return jnp.concatenate([v, jnp.zeros((v.shape[0], n - v.shape[1]), F32)], axis=1)

    par = _pad_rows(jnp.concatenate([pad_lanes(dt_bias, 128), pad_lanes(a_log, 128)], axis=0), 8)
    dsk = jnp.repeat(d_ssd[0], HEADDIM).reshape(1, D_SSD)
    ar = s5_a_re.reshape(1, S5_N)
    ai = s5_a_im.reshape(1, S5_N)
    ldt = jnp.repeat(s5_log_dt[0], S5_P).reshape(1, S5_N)
    br_t = s5_b_re[0].transpose(2, 0, 1).reshape(S5_CH, S5_N)
    bi_t = s5_b_im[0].transpose(2, 0, 1).reshape(S5_CH, S5_N)
    bb_re_t, bb_im_t, pf_re, pf_im, pr_re, pr_im = _s5_params_fwd(ar, ai, ldt, br_t, bi_t)
    gpb = S5_GROUPS // S5_BLOCKS
    mask_b = (jnp.arange(128)[:, None] // S5_CH) == (jnp.arange(512)[None, :] // S5_P)

    def dense_b(bt_):
        blocks = bt_.reshape(S5_CH, S5_BLOCKS, 512).transpose(1, 0, 2)
        return jnp.where(mask_b, jnp.tile(blocks, (1, gpb, 1)), 0.0).astype(BF16)

    def dense_c(cc):
        blocks = cc[0].transpose(0, 2, 1).reshape(S5_BLOCKS, 512, S5_CH)
        return jnp.where(mask_b.T, jnp.tile(blocks, (1, 1, gpb)), 0.0).astype(BF16)

    bb_re, bb_im = dense_b(bb_re_t), dense_b(bb_im_t)
    cc_re, cc_im = dense_c(s5_c_re), dense_c(s5_c_im)
    s5d = s5_d.reshape(1, D_S5)
    ln1 = jnp.concatenate([ln1_g, ln1_b], axis=0)
    vec1 = _pad_rows(jnp.concatenate([b2, ln2_g, ln2_b], axis=0), 8)

    z, xbc_pre, xbc, dt_raw, u5 = _proj_conv_fwd(x2, mod3, w_in_pad, conv_w_full, conv_b, seq)
    yraw, ycat, hprev = _ssd_fwd(xbc, z, dt_raw, par, dsk, norm_w, seq)
    s_re, s_im, ypre, ycat = _s5_fwd(u5, bb_re, bb_im, cc_re, cc_im, pf_re, pf_im, s5d, w_glu_f, b_glu, ycat, seq)
    landed = _gather_wait(late_sems[0], late_sems[1], late_sems[2], late_sems[3], ycat, "gather_late_wait")
    gathered = {n: lax.dynamic_update_index_in_dim(l, shard_bf16[n], dev, 0) for n, l in zip(late_names, landed)}
    w_out_f = gathered["w_out"].reshape(2 * D_MODEL, D_MODEL)
    w1_blocks = gathered["w1"]
    w2_f = gathered["w2"].reshape(D_FF, D_MODEL)
    mix, x1 = _out_ln1(ycat, x2, mod3, w_out_f, ln1, seq)

    dx1, u2b, hb, dhpb, dob, gacc2, db1, bacc2 = _mlp_fwd_bwd(x1, tgt2, mod3, w1_blocks, w2_f, vec1, b1, seq)
    loss = lax.psum(0.5 / D_MODEL * jnp.sum(gacc2[3]), ("x", "y", "c"))

    dmixb, dxa, dyssd, dy5, gacc1, bacc1 = _ln1_out_bwd(dx1, x2, mix, mod3, w_out_f, ln1, seq)
    du5, vacc, sacc, d_cc, d_bb, g_wglu = _s5_bwd(dy5, ypre, u5, s_re, s_im, bb_re, bb_im, cc_re, cc_im,
                                                  pr_re, pr_im, s5d, w_glu_f, b_glu, seq)
    dxbc, dz, ddt, dpar, cacc = _ssd_bwd(dyssd, yraw, z, xbc, dt_raw, hprev, par, dsk, norm_w, seq)
    dpre, conv_acc = _conv_bwd(dxbc, xbc_pre, conv_w_full, conv_b, seq)
    grad_x2, ub, dxpb, bacc0 = _proj_bwd(dz, dpre, ddt, du5, x2, dxa, mod3, conv_w_full, w_in_pad, seq)

    g_w2 = _atb(hb, dob, "gw2")
    g_w1 = _atb(u2b, dhpb, "gw1")
    g_wout = _atb(ycat, dmixb, "gwout")
    g_win = jnp.concatenate([_atb(ub, dz, "gwin_z"), _atb(ub, dxpb, "gwin_xbc"),
                             _atb(ub, ddt, "gwin_dt")[:, :N_HEADS], _atb(ub, du5, "gwin_s5")], axis=1)

    def diag_b(dd):
        kept = jnp.where(mask_b, dd, 0.0).reshape(S5_BLOCKS, gpb, S5_CH, 512).sum(1)
        return kept.transpose(1, 0, 2).reshape(S5_CH, S5_N)

    def diag_c(dd):
        kept = jnp.where(mask_b.T, dd, 0.0).reshape(S5_BLOCKS, 512, gpb, S5_CH).sum(2)
        return kept.reshape(S5_GROUPS, S5_P, S5_CH).transpose(0, 2, 1)

    g_ar, g_ai, g_ldt, g_br_t, g_bi_t = _s5_params_bwd(ar, ai, ldt, br_t, bi_t, vacc[0:1], vacc[1:2],
                                                      diag_b(d_bb[:S5_BLOCKS]), diag_b(d_bb[S5_BLOCKS:]))

    def from_t(gt):
        return gt.reshape(S5_CH, S5_GROUPS, S5_P).transpose(1, 2, 0)

    small_g = dict(
        conv_w=conv_acc[0:4], conv_b=conv_acc[4:5], dt_bias=dpar[0:1, :N_HEADS], a_log=dpar[1:2, :N_HEADS],
        d_ssd=cacc[0].reshape(N_HEADS, HEADDIM).sum(1), norm_w=cacc[1:2],
        s5_a_re=g_ar, s5_a_im=g_ai, s5_log_dt=g_ldt[:, :S5_GROUPS], s5_b_re=from_t(g_br_t), s5_b_im=from_t(g_bi_t),
        s5_c_re=diag_c(d_cc[:S5_BLOCKS]), s5_c_im=diag_c(d_cc[S5_BLOCKS:]), s5_d=sacc[0:1], b_glu=sacc[1:2],
        ln1_g=gacc1[0:1], ln1_b=gacc1[1:2], b1=db1, b2=gacc2[2:3], ln2_g=gacc2[0:1], ln2_b=gacc2[1:2])

    dmod = jnp.concatenate([bacc0[:, 0], bacc0[:, 1], bacc1[:, 0], bacc2[:, 0], bacc2[:, 1], bacc2[:, 2]], axis=1)
    dmod_all = _all_gather([dmod], "gather_dmod")[0].reshape(N_DEV * nb, N_MOD * D_MODEL)
    dmod_cols = lax.dynamic_slice_in_dim(dmod_all, dev * ada_cols, ada_cols, axis=1)
    g_wada, g_bada = _mod_bwd(c_all, dmod_cols, dmod_all)

    in_cols = w_in.shape[2]
    big_g = dict(
        w_in=g_win.reshape(D_MODEL, N_DEV, in_cols).transpose(1, 0, 2),
        w_out=g_wout.reshape((N_DEV,) + w_out.shape[1:]), w1=g_w1,
        w2=g_w2.reshape((N_DEV,) + w2.shape[1:]), w_glu=g_wglu.reshape((N_DEV,) + w_glu.shape[1:]))
    by_dest = [big_g[n] if n == "w1" else big_g[n].reshape((4, 2) + big_g[n].shape[1:]) for n in big_names]
    from_sibling = _sibling_swap(by_dest, "rs_sibling_swap")
    core = lax.axis_index("c").astype(jnp.int32).reshape(1)
    chip_sums = [_add_halves(g, r, core, "rs_add_" + n) for g, r, n in zip(by_dest, from_sibling, big_names)]
    parts = _chip_all_to_all(chip_sums, "rs_chip_all_to_all")

    res = {k: {} for k in "gdmv"}
    for n, p in zip(big_names, parts):
        outs = _adamw(p, weights[n][0], mom[n][0], var[n][0], "adamw_" + n)
        for k, a in zip("gdmv", outs):
            res[k][n] = a[None]

    ag, ad, am, av = _adamw(g_wada[None], w_ada[0], m_w_ada[0], v_w_ada[0], "adamw_w_ada")
    for k, a in (("g", ag), ("d", ad), ("m", am), ("v", av)):
        res[k]["w_ada"] = a[None]
    bg_, bd_, bm_, bv_ = _adamw(g_bada.reshape(1, -1, 128), b_ada.reshape(-1, 128), m_b_ada.reshape(-1, 128),
                                v_b_ada.reshape(-1, 128), "adamw_b_ada")
    for k, a in (("g", bg_), ("d", bd_), ("m", bm_), ("v", bv_)):
        res[k]["b_ada"] = a.reshape(shapes["b_ada"])

    small_shapes = dict(shapes)
    small_shapes["conv_w"] = (1, 4, D_XBC)
    small_parts = _all_gather([_pack_small(small_g)], "gather_small_grads")[0]
    rep = {n: (jnp.zeros((1, 4, D_XBC), F32) if n == "conv_w" else weights[n]) for n in _SMALL}
    rep_m = {n: (jnp.zeros((1, 4, D_XBC), F32) if n == "conv_w" else mom[n]) for n in _SMALL}
    rep_v = {n: (jnp.ones((1, 4, D_XBC), F32) if n == "conv_w" else var[n]) for n in _SMALL}
    sg_, sd_, sm_, sv_ = _adamw(small_parts, _pack_small(rep), _pack_small(rep_m), _pack_small(rep_v), "adamw_small")
    for k, p in (("g", sg_), ("d", sd_), ("m", sm_), ("v", sv_)):
        un = _unpack_small(p, small_shapes)
        for n in _SMALL:
            if n != "conv_w":
                res[k][n] = un[n]
    g_conv_full = _unpack_small(sg_, small_shapes)["conv_w"][0]
    g_conv_mine = lax.dynamic_slice_in_dim(g_conv_full, dev * cw_cols, cw_cols, axis=1)
    cg_, cd_, cm_, cv_ = _adamw(g_conv_mine[None], conv_w[0], m_conv_w[0], v_conv_w[0], "adamw_conv_w")
    for k, a in (("g", cg_), ("d", cd_), ("m", cm_), ("v", cv_)):
        res[k]["conv_w"] = a[None]

    grad_x = grad_x2.reshape(nb, seq, D_MODEL)
    return (loss, grad_x, *[res["g"][n] for n in names], *[res["d"][n] for n in names],
            *[res["m"][n] for n in names], *[res["v"][n] for n in names])
```

```python
import functools
import math

import jax
import jax.numpy as jnp
from jax import lax
from jax.experimental import pallas as pl
from jax.experimental.pallas import tpu as pltpu

F32, BF16 = jnp.float32, jnp.bfloat16
MESH = pl.DeviceIdType.MESH
N_DEV = 8

D_MODEL = 1024
D_SSD = 1536
N_HEADS = 24
HEADDIM = 64
N_GROUPS = 4
HPG = 6
GW = HPG * HEADDIM
N_STATE = 128
CHUNK = 128
D_XBC = 2560
D_S5 = 512
S5_GROUPS = 32
S5_CH = 16
S5_P = 64
S5_N = S5_GROUPS * S5_P
D_IN = 4632
DT_PAD = 128
D_INP = D_SSD + D_XBC + DT_PAD + D_S5
D_FF = 4096
N_MOD = 6
ALPHA = 2.0 ** 0.25
EPS = 1e-5
LR, B1, B2, AEPS, WD, STEP = 0.001, 0.9, 0.999, 1e-08, 0.01, 10

NT = (((1,), (1,)), ((), ()))
TN = (((0,), (0,)), ((), ()))
ANY = pl.BlockSpec(memory_space=pl.ANY)
HIGHEST = lax.Precision.HIGHEST


def _mm(a, b):
    return jnp.dot(a.astype(BF16), b.astype(BF16), preferred_element_type=F32)


def _mm_nt(a, b):
    return lax.dot_general(a.astype(BF16), b.astype(BF16), NT, preferred_element_type=F32)


def _mm_tn(a, b):
    return lax.dot_general(a.astype(BF16), b.astype(BF16), TN, preferred_element_type=F32)


def _row_block(r, cap):
    best = r
    for cand in range(8, min(r, cap) + 1, 8):
        if r % cand == 0:
            best = cand
    return best if best <= cap else r


def _params(vmem_mb):
    return pltpu.CompilerParams(vmem_limit_bytes=vmem_mb << 20)


def _sigmoid(x):
    return 0.5 * (jnp.tanh(0.5 * x) + 1.0)


def _softplus(x):
    return jnp.maximum(x, 0.0) + jnp.log(1.0 + jnp.exp(-jnp.abs(x)))


_GK = math.sqrt(2.0 / math.pi)


def _gelu(x):
    return 0.5 * x * (1.0 + jnp.tanh(_GK * (x + 0.044715 * x * x * x)))


def _gelu_grad(x):
    t = jnp.tanh(_GK * (x + 0.044715 * x * x * x))
    return 0.5 * (1.0 + t) + 0.5 * x * (1.0 - t * t) * _GK * (1.0 + 3.0 * 0.044715 * x * x)


def _dev_index():
    return 4 * lax.axis_index("x") + 2 * lax.axis_index("y") + lax.axis_index("c")


def _all_gather(xs, name):
    n = len(xs)

    def body(*refs):
        x_refs, out_refs = refs[:n], refs[n:2 * n]
        send_sems, recv_sems, local_sems = refs[2 * n:]
        ix, iy, ic = lax.axis_index("x"), lax.axis_index("y"), lax.axis_index("c")
        me, sibling = (ix, iy, ic), (ix, iy, 1 - ic)
        chips = [(1 - ix, iy), (ix, 1 - iy), (1 - ix, 1 - iy)]

        def slot(a, px, py, pc):
            return out_refs[a].at[4 * px + 2 * py + pc]

        def copy(a, k, block, to, src=None):
            return pltpu.make_async_remote_copy(
                src_ref=slot(a, *block) if src is None else src, dst_ref=slot(a, *block),
                send_sem=send_sems.at[7 * a + k], recv_sem=recv_sems.at[7 * a + k], device_id=to, device_id_type=MESH)

        mine = [pltpu.make_async_copy(x_refs[a], slot(a, *me), local_sems.at[a]) for a in range(n)]
        for cp in mine:
            cp.start()
        first = []
        for j, chip in enumerate(chips):
            first += [copy(a, 1 + j, me, (*chip, ic), src=x_refs[a]) for a in range(n)]
        first += [copy(a, 0, me, sibling, src=x_refs[a]) for a in range(n)]
        for cp in first:
            cp.start()
        passed = []
        for j, chip in enumerate(chips):
            for a in range(n):
                copy(a, 1 + j, (*chip, ic), me).wait_recv()
                cp = copy(a, 4 + j, (*chip, ic), sibling)
                cp.start()
                passed.append(cp)
        for a in range(n):
            copy(a, 0, sibling, me).wait_recv()
            for j, chip in enumerate(chips):
                copy(a, 4 + j, (*chip, 1 - ic), me).wait_recv()
        for cp in first + passed:
            cp.wait_send()
        for cp in mine:
            cp.wait()

    return pl.pallas_call(
        body, name=name, out_shape=tuple(jax.ShapeDtypeStruct((N_DEV,) + x.shape, x.dtype) for x in xs),
        in_specs=[ANY] * n, out_specs=tuple([ANY] * n),
        scratch_shapes=[pltpu.SemaphoreType.DMA((7 * n,)), pltpu.SemaphoreType.DMA((7 * n,)),
                        pltpu.SemaphoreType.DMA((n,))],
    )(*xs)


HBM = pl.BlockSpec(memory_space=pltpu.HBM)
SEM = pl.BlockSpec(memory_space=pltpu.SEMAPHORE)
DATAFLOW = pltpu.SideEffectType.DATAFLOW_SIDE_EFFECTING


def _peer(k):
    ix, iy, ic = lax.axis_index("x"), lax.axis_index("y"), lax.axis_index("c")
    return (1 - ix if k & 4 else ix, 1 - iy if k & 2 else iy, 1 - ic if k & 1 else ic)


def _block_of(p):
    return 4 * p[0] + 2 * p[1] + p[2]


def _gather_start(xs, name):
    n = len(xs)
    lands = [lax.empty((N_DEV,) + x.shape, x.dtype) for x in xs]

    def body(*refs):
        x_refs, land_refs = refs[:n], refs[n:2 * n]
        send_sems, recv_sems = refs[2 * n], refs[2 * n + 1]
        token = refs[-1]
        me = _block_of(_peer(0))
        for a in range(n):
            for k in range(1, N_DEV):
                pltpu.make_async_remote_copy(
                    src_ref=x_refs[a], dst_ref=land_refs[a].at[me], send_sem=send_sems.at[7 * a + k - 1],
                    recv_sem=recv_sems.at[7 * a + k - 1], device_id=_peer(k), device_id_type=MESH).start()
        token[...] = jnp.zeros_like(token)

    outs = pl.pallas_call(
        body, name=name,
        out_shape=(pltpu.SemaphoreType.DMA((7 * n,)), pltpu.SemaphoreType.DMA((7 * n,)))
        + tuple(pltpu.HBM(x.shape, x.dtype) for x in xs) + tuple(pltpu.HBM(l.shape, l.dtype) for l in lands)
        + (jax.ShapeDtypeStruct((8, 128), F32),),
        in_specs=[HBM] * (2 * n), out_specs=(SEM, SEM) + (HBM,) * (2 * n) + (pl.BlockSpec(memory_space=pltpu.VMEM),),
        input_output_aliases={i: 2 + i for i in range(2 * n)},
        compiler_params=pltpu.CompilerParams(has_side_effects=DATAFLOW),
    )(*[pltpu.with_memory_space_constraint(x, pltpu.HBM) for x in xs],
      *[pltpu.with_memory_space_constraint(l, pltpu.HBM) for l in lands])
    return outs[0], outs[1], outs[2:2 + n], outs[2 + n:2 + 2 * n], outs[-1]


def _gather_wait(send_sems, recv_sems, xs_thru, lands_thru, after, name):
    n = len(xs_thru)

    def body(*refs):
        x_refs, land_refs = refs[:n], refs[n:2 * n]
        send_sems, recv_sems = refs[2 * n], refs[2 * n + 1]
        for a in range(n):
            for k in range(1, N_DEV):
                cp = pltpu.make_async_remote_copy(
                    src_ref=x_refs[a], dst_ref=land_refs[a].at[_block_of(_peer(k))], send_sem=send_sems.at[7 * a + k - 1],
                    recv_sem=recv_sems.at[7 * a + k - 1], device_id=_peer(k), device_id_type=MESH)
                cp.wait_send()
                cp.wait_recv()

    outs = pl.pallas_call(
        body, name=name,
        out_shape=tuple(pltpu.HBM(x.shape, x.dtype) for x in xs_thru)
        + tuple(pltpu.HBM(l.shape, l.dtype) for l in lands_thru),
        in_specs=[HBM] * (2 * n) + [SEM, SEM, ANY], out_specs=(HBM,) * (2 * n),
        input_output_aliases={i: i for i in range(2 * n)},
        compiler_params=pltpu.CompilerParams(has_side_effects=DATAFLOW),
    )(*xs_thru, *lands_thru, send_sems, recv_sems, after)
    return outs[n:]


def _sibling_swap(gs, name):
    n = len(gs)

    def body(*refs):
        g_refs, recv_refs = refs[:n], refs[n:2 * n]
        send_sems, recv_sems = refs[2 * n:]
        ix, iy, ic = lax.axis_index("x"), lax.axis_index("y"), lax.axis_index("c")

        def block(g_ref, q):
            if len(g_ref.shape) == 4:
                return g_ref.at[q, 1 - ic]
            cw = g_ref.shape[1] // N_DEV
            return g_ref.at[:, pl.ds(pl.multiple_of((2 * q + 1 - ic) * cw, 128), cw)]

        cps = []
        for a in range(n):
            for q in range(4):
                cps.append(pltpu.make_async_remote_copy(
                    src_ref=block(g_refs[a], q), dst_ref=recv_refs[a].at[q],
                    send_sem=send_sems.at[4 * a + q], recv_sem=recv_sems.at[4 * a + q],
                    device_id=(ix, iy, 1 - ic), device_id_type=MESH))
        for cp in cps:
            cp.start()
        for cp in cps:
            cp.wait()

    return pl.pallas_call(
        body, name=name,
        out_shape=tuple(jax.ShapeDtypeStruct(
            (4,) + (g.shape[2:] if g.ndim == 4 else (g.shape[0], g.shape[1] // N_DEV)), g.dtype) for g in gs),
        in_specs=[ANY] * n, out_specs=tuple([ANY] * n),
        scratch_shapes=[pltpu.SemaphoreType.DMA((4 * n,)), pltpu.SemaphoreType.DMA((4 * n,))],
    )(*gs)


def _chip_all_to_all(hs, name):
    n = len(hs)

    def body(*refs):
        h_refs, out_refs = refs[:n], refs[n:2 * n]
        send_sems, recv_sems, local_sems = refs[2 * n:]
        ix, iy, ic = lax.axis_index("x"), lax.axis_index("y"), lax.axis_index("c")
        me = 2 * ix + iy
        peers = [(1 - ix, iy), (ix, 1 - iy), (1 - ix, 1 - iy)]
        mine = [pltpu.make_async_copy(h_refs[a].at[me], out_refs[a].at[me], local_sems.at[a]) for a in range(n)]
        for cp in mine:
            cp.start()

        def copy(a, k, src_slot, dst_slot, peer):
            return pltpu.make_async_remote_copy(
                src_ref=h_refs[a].at[src_slot], dst_ref=out_refs[a].at[dst_slot],
                send_sem=send_sems.at[3 * a + k], recv_sem=recv_sems.at[3 * a + k],
                device_id=(*peer, ic), device_id_type=MESH)

        sends = [copy(a, k, 2 * px + py, me, (px, py)) for a in range(n) for k, (px, py) in enumerate(peers)]
        for cp in sends:
            cp.start()
        for a in range(n):
            for k, (px, py) in enumerate(peers):
                copy(a, k, 2 * px + py, 2 * px + py, (px, py)).wait_recv()
        for cp in sends:
            cp.wait_send()
        for cp in mine:
            cp.wait()

    return pl.pallas_call(
        body, name=name, out_shape=tuple(jax.ShapeDtypeStruct(h.shape, h.dtype) for h in hs),
        in_specs=[ANY] * n, out_specs=tuple([ANY] * n),
        scratch_shapes=[pltpu.SemaphoreType.DMA((3 * n,)), pltpu.SemaphoreType.DMA((3 * n,)),
                        pltpu.SemaphoreType.DMA((n,))],
    )(*hs)


def _add_halves(g, recv, core, name):
    _, r, c = recv.shape
    br = _row_block(r, 512)
    stacked = g.ndim == 4

    def body(core_ref, g_ref, r_ref, o_ref):
        o_ref[0] = ((g_ref[0, 0] if stacked else g_ref[...]) + r_ref[0]).astype(BF16)

    spec = pl.BlockSpec((1, br, c), lambda i, j, core_ref: (i, j, 0))
    if stacked:
        g_spec = pl.BlockSpec((1, 1, br, c), lambda i, j, core_ref: (i, core_ref[0], j, 0))
    else:
        g_spec = pl.BlockSpec((br, c), lambda i, j, core_ref: (j, 2 * i + core_ref[0]))
    return pl.pallas_call(
        body, name=name, out_shape=jax.ShapeDtypeStruct(recv.shape, BF16),
        grid_spec=pltpu.PrefetchScalarGridSpec(
            num_scalar_prefetch=1, grid=(4, r // br), in_specs=[g_spec, spec], out_specs=spec),
        compiler_params=_params(32),
    )(core, g, recv)


def _adamw(parts, w, m, v, name):
    n_parts, r, c = parts.shape
    br = _row_block(r, 512 if c <= 1024 else 256)

    def body(p_ref, w_ref, m_ref, v_ref, g_out, d_out, m_out, v_out):
        g = p_ref[0].astype(F32)
        for p in range(1, n_parts):
            g = g + p_ref[p].astype(F32)
        m2 = B1 * m_ref[...] + (1.0 - B1) * g
        v2 = B2 * v_ref[...] + (1.0 - B2) * (g * g)
        m_hat = m2 / (1.0 - B1 ** STEP)
        v_hat = v2 / (1.0 - B2 ** STEP)
        g_out[...] = g
        d_out[...] = -LR * (m_hat / (jnp.sqrt(v_hat) + AEPS) + WD * w_ref[...])
        m_out[...] = m2
        v_out[...] = v2

    spec = pl.BlockSpec((br, c), lambda i: (i, 0))
    out = jax.ShapeDtypeStruct((r, c), F32)
    return pl.pallas_call(
        body, name=name, out_shape=(out, out, out, out), grid=(r // br,),
        in_specs=[pl.BlockSpec((n_parts, br, c), lambda i: (0, i, 0)), spec, spec, spec],
        out_specs=(spec, spec, spec, spec), compiler_params=_params(40),
    )(parts, w, m, v)


def _atb(a, b, name, bt=512):
    t, k1 = a.shape
    k2 = b.shape[1]

    def pick(k):
        for cand in (1024, 768, 512, 384, 256, 128):
            if k % cand == 0:
                return cand
        return k

    b1, b2 = pick(k1), pick(k2)

    def body(a_ref, b_ref, o_ref):
        @pl.when(pl.program_id(2) == 0)
        def _():
            o_ref[...] = jnp.zeros_like(o_ref)
        o_ref[...] += _mm_tn(a_ref[...], b_ref[...])

    return pl.pallas_call(
        body, name=name, out_shape=jax.ShapeDtypeStruct((k1, k2), F32), grid=(k1 // b1, k2 // b2, t // bt),
        in_specs=[pl.BlockSpec((bt, b1), lambda i, j, k: (k, i)), pl.BlockSpec((bt, b2), lambda i, j, k: (k, j))],
        out_specs=pl.BlockSpec((b1, b2), lambda i, j, k: (i, j)), compiler_params=_params(40),
    )(a, b)


def _mod_fwd(c_all, w_ada, b_cols):
    def body(c_ref, w_ref, b_ref, o_ref):
        cc = c_ref[...]
        cond = cc * _sigmoid(cc)
        o_ref[...] = _mm(cond, w_ref[...]) + b_ref[...]

    return pl.pallas_call(body, name="mod_fwd", out_shape=jax.ShapeDtypeStruct((c_all.shape[0], w_ada.shape[1]), F32),
                          compiler_params=_params(32))(c_all, w_ada, b_cols)


def _mod_bwd(c_all, dmod_cols, dmod_all):
    def body(c_ref, dc_ref, da_ref, gw_ref, gb_ref):
        cc = c_ref[...]
        cond = cc * _sigmoid(cc)
        gw_ref[...] = _mm_tn(cond, dc_ref[...])
        gb_ref[...] = jnp.sum(da_ref[...], axis=0, keepdims=True)

    return pl.pallas_call(
        body, name="mod_bwd",
        out_shape=(jax.ShapeDtypeStruct((D_MODEL, dmod_cols.shape[1]), F32), jax.ShapeDtypeStruct((1, dmod_all.shape[1]), F32)),
        compiler_params=_params(32))(c_all, dmod_cols, dmod_all)


def _load_once(hbm_ref, vmem_ref, sem):
    @pl.when(pl.program_id(0) == 0)
    def _():
        cp = pltpu.make_async_copy(hbm_ref, vmem_ref, sem)
        cp.start()
        cp.wait()


def _conv_taps(win_ref, w, tb, cols):
    shifted = [win_ref[8 - j:8 - j + tb, cols] for j in range(4)]
    acc = w[3:4] * shifted[0]
    for j in (1, 2, 3):
        acc = acc + w[3 - j:4 - j] * shifted[j]
    return acc, shifted


def _proj_conv_fwd(x2, mod3, w_in_pad, conv_w, conv_b, seq):
    t = x2.shape[0]
    tb = 256
    npb = seq // tb
    cw = 512

    def body(x_ref, mod_ref, w_hbm, cw_ref, cb_ref, z_ref, pre_ref, xbc_ref, dt_ref, u5_ref, w_vmem, win, sem):
        _load_once(w_hbm, w_vmem, sem)
        first = (pl.program_id(0) % npb) == 0

        @pl.when(first)
        def _():
            win[0:8, :] = jnp.zeros((8, D_XBC), F32)

        @pl.when(jnp.logical_not(first))
        def _():
            win[0:8, :] = win[tb:tb + 8, :]

        m = mod_ref[0]
        u = (x_ref[...] * (1.0 + m[1:2]) + m[0:1]).astype(BF16)
        z_ref[...] = jnp.dot(u, w_vmem[:, 0:D_SSD], preferred_element_type=F32)
        dt_ref[...] = jnp.dot(u, w_vmem[:, D_SSD + D_XBC:D_SSD + D_XBC + DT_PAD], preferred_element_type=F32)
        u5_ref[...] = jnp.dot(u, w_vmem[:, D_SSD + D_XBC + DT_PAD:], preferred_element_type=F32)
        for k in range(D_XBC // cw):
            cols = slice(k * cw, (k + 1) * cw)
            pre_k = jnp.dot(u, w_vmem[:, D_SSD + k * cw:D_SSD + (k + 1) * cw], preferred_element_type=F32)
            win[8:8 + tb, cols] = pre_k
            pre_ref[:, cols] = pre_k
            conv, _ = _conv_taps(win, cw_ref[:, cols], tb, cols)
            conv = conv + cb_ref[:, cols]
            xbc_ref[:, cols] = conv * _sigmoid(conv)

    row = lambda w: pl.BlockSpec((tb, w), lambda i: (i, 0))
    return pl.pallas_call(
        body, name="proj_conv_fwd", grid=(t // tb,),
        out_shape=(jax.ShapeDtypeStruct((t, D_SSD), F32), jax.ShapeDtypeStruct((t, D_XBC), F32),
                   jax.ShapeDtypeStruct((t, D_XBC), F32), jax.ShapeDtypeStruct((t, DT_PAD), F32),
                   jax.ShapeDtypeStruct((t, D_S5), F32)),
        in_specs=[row(D_MODEL), pl.BlockSpec((1, N_MOD, D_MODEL), lambda i: (i // npb, 0, 0)), ANY,
                  pl.BlockSpec((4, D_XBC), lambda i: (0, 0)), pl.BlockSpec((1, D_XBC), lambda i: (0, 0))],
        out_specs=(row(D_SSD), row(D_XBC), row(D_XBC), row(DT_PAD), row(D_S5)),
        scratch_shapes=[pltpu.VMEM((D_MODEL, D_INP), BF16), pltpu.VMEM((tb + 8, D_XBC), F32), pltpu.SemaphoreType.DMA],
        compiler_params=_params(56),
    )(x2, mod3, w_in_pad, conv_w, conv_b)


N_PAIRS = N_HEADS // 2


def _split3(x):
    hi = x.astype(BF16)
    r = x - hi.astype(F32)
    mid = r.astype(BF16)
    lo = (r - mid.astype(F32)).astype(BF16)
    return hi, mid, lo


def _dot3(x, e, dims=(((1,), (0,)), ((), ()))):
    return sum(lax.dot_general(p, e, dims, preferred_element_type=F32) for p in _split3(x))


def _dot3_left(e, x, dims=(((1,), (0,)), ((), ()))):
    return sum(lax.dot_general(e, p, dims, preferred_element_type=F32) for p in _split3(x))


def _head_fold():
    return (jnp.arange(D_SSD)[:, None] // HEADDIM == jnp.arange(128)[None, :]).astype(BF16)


def _ssd_prep(dt_raw, par):
    dtb = par[0:1]
    a = -jnp.exp(par[1:2])
    dt = _softplus(dt_raw + dtb)
    adt = dt * a
    row = lax.broadcasted_iota(jnp.int32, (CHUNK, CHUNK), 0)
    col = lax.broadcasted_iota(jnp.int32, (CHUNK, CHUNK), 1)
    causal = row >= col
    tri = causal.astype(BF16)
    cs = _dot3_left(tri, adt)
    left = col < HEADDIM

    def lanes(v, h):
        return jnp.broadcast_to(v[:, h:h + 1], (CHUNK, 128))

    dt_c, cs_c, pair_cols = [], [], []
    for p in range(N_PAIRS):
        c0, c1 = lanes(cs, 2 * p), lanes(cs, 2 * p + 1)
        pair_cols.append(jnp.concatenate([c0, c1], axis=1))
        cs_c.append(jnp.where(left, c0, c1))
        dt_c.append(jnp.where(left, lanes(dt, 2 * p), lanes(dt, 2 * p + 1)))
    cs_c = jnp.concatenate(cs_c, axis=1)
    dt_c = jnp.concatenate(dt_c, axis=1)
    return dt, a, cs, cs.T, causal, tri, dt_c, jnp.exp(cs_c), jnp.exp(cs_c[CHUNK - 1:CHUNK, :] - cs_c), pair_cols


def _pair_decay(cols, cst, pair, causal2):
    rows = jnp.concatenate([jnp.broadcast_to(cst[2 * pair:2 * pair + 1, :], (CHUNK, CHUNK)),
                            jnp.broadcast_to(cst[2 * pair + 1:2 * pair + 2, :], (CHUNK, CHUNK))], axis=1)
    return jnp.exp(jnp.where(causal2, cols - rows, -jnp.inf))


def _stack_heads(xp, left):
    return jnp.concatenate([jnp.where(left, xp, 0.0), jnp.where(left, 0.0, xp)], axis=0).astype(BF16)


def _ssd_fwd(xbc, z, dt_raw, par, dsk, normw, seq):
    t = xbc.shape[0]
    nc = seq // CHUNK
    n_chunks = t // CHUNK

    def body(xbc_ref, z_ref, dt_ref, par_ref, dsk_ref, nw_ref, yraw_ref, ycat_ref, hprev_ref, h_ref):
        @pl.when(pl.program_id(0) % nc == 0)
        def _():
            h_ref[...] = jnp.zeros_like(h_ref)
        hprev_ref[0] = h_ref[...]
        _, _, cs, cst, causal, _, dt_c, ecs_c, w_c, pair_cols = _ssd_prep(dt_ref[...], par_ref[...])
        cs_last = cs[CHUNK - 1:CHUNK, :]
        causal2 = jnp.concatenate([causal, causal], axis=1)
        left = lax.broadcasted_iota(jnp.int32, (CHUNK, 128), 1) < HEADDIM
        x = xbc_ref[:, 0:D_SSD]
        xdt = x * dt_c
        amat = (w_c * xdt).astype(BF16)
        zz = z_ref[...]
        silu_z = zz * _sigmoid(zz)
        for g in range(N_GROUPS):
            gs = slice(g * GW, (g + 1) * GW)
            bg = xbc_ref[:, D_SSD + g * N_STATE:D_SSD + (g + 1) * N_STATE].astype(BF16)
            cg = xbc_ref[:, D_SSD + (N_GROUPS + g) * N_STATE:D_SSD + (N_GROUPS + g + 1) * N_STATE].astype(BF16)
            scores = lax.dot_general(cg, bg, NT, preferred_element_type=F32)
            scores2 = jnp.concatenate([scores, scores], axis=1)
            hg = h_ref[gs, :]
            p_all = lax.dot_general(cg, hg.astype(BF16), NT, preferred_element_type=F32)
            ys = []
            for q in range(GW // 128):
                pair = g * (GW // 128) + q
                decay = _pair_decay(pair_cols[pair], cst, pair, causal2)
                mcat = (scores2 * decay).astype(BF16)
                ys.append(jnp.dot(mcat, _stack_heads(xdt[:, pair * 128:(pair + 1) * 128], left),
                                  preferred_element_type=F32))
            yg = jnp.concatenate(ys, axis=1) + ecs_c[:, gs] * p_all + x[:, gs] * dsk_ref[:, gs]
            s_new = lax.dot_general(amat[:, gs], bg, TN, preferred_element_type=F32)
            for j in range(HPG):
                hh = g * HPG + j
                js = slice(j * HEADDIM, (j + 1) * HEADDIM)
                h_ref[g * GW + j * HEADDIM:g * GW + (j + 1) * HEADDIM, :] = (
                    hg[js, :] * jnp.exp(cs_last[:, hh:hh + 1]) + s_new[js, :])
            yraw_ref[:, gs] = yg
            v = yg * silu_z[:, gs]
            r = lax.rsqrt(jnp.mean(v * v, axis=-1, keepdims=True) + EPS)
            ycat_ref[:, gs] = (v * r * nw_ref[:, gs]).astype(BF16)

    row = lambda w: pl.BlockSpec((CHUNK, w), lambda i: (i, 0))
    full = lambda s: pl.BlockSpec(s, lambda i: (0,) * len(s))
    return pl.pallas_call(
        body, name="ssd_fwd", grid=(n_chunks,),
        out_shape=(jax.ShapeDtypeStruct((t, D_SSD), F32), jax.ShapeDtypeStruct((t, D_SSD + D_S5), BF16),
                   jax.ShapeDtypeStruct((n_chunks, D_SSD, N_STATE), F32)),
        in_specs=[row(D_XBC), row(D_SSD), row(DT_PAD), full((8, 128)), full((1, D_SSD)), full((1, D_SSD))],
        out_specs=(row(D_SSD), row(D_SSD), pl.BlockSpec((1, D_SSD, N_STATE), lambda i: (i, 0, 0))),
        scratch_shapes=[pltpu.VMEM((D_SSD, N_STATE), F32)],
        compiler_params=_params(40),
    )(xbc, z, dt_raw, par, dsk, normw)


S5_CW = 512
S5_BLOCKS = 4


def _tile_scan(in_re, in_im, out_re, out_im, carry_re, carry_im, pw_re, pw_im, n_tiles, reverse):
    steps = (1, 2, 4)
    for cc in range(S5_N // S5_CW):
        cols = slice(cc * S5_CW, (cc + 1) * S5_CW)
        a_re, a_im = pw_re[:, cols], pw_im[:, cols]
        rid = lax.broadcasted_iota(jnp.int32, (8, S5_CW), 0)
        pows = []
        for d in steps:
            k = 8 - d if reverse else d - 1
            keep = (rid < 8 - d) if reverse else (rid >= d)
            pows.append((jnp.where(keep, pw_re[k:k + 1, cols], 0.0), jnp.where(keep, pw_im[k:k + 1, cols], 0.0)))

        def tile(i, carry, cols=cols, pows=pows, a_re=a_re, a_im=a_im):
            r = (n_tiles - 1 - i) if reverse else i
            rows = pl.ds(pl.multiple_of(r * 8, 8), 8)
            xr, xi = in_re[rows, cols], in_im[rows, cols]
            for (pr, pi), d in zip(pows, steps):
                shift = 8 - d if reverse else d
                sr, si = pltpu.roll(xr, shift, axis=0), pltpu.roll(xi, shift, axis=0)
                xr, xi = xr + pr * sr - pi * si, xi + pr * si + pi * sr
            cr, ci = carry
            xr, xi = xr + a_re * cr - a_im * ci, xi + a_re * ci + a_im * cr
            out_re[rows, cols] = xr
            out_im[rows, cols] = xi
            edge = slice(0, 1) if reverse else slice(7, 8)
            return (jnp.broadcast_to(xr[edge], (8, S5_CW)), jnp.broadcast_to(xi[edge], (8, S5_CW)))

        c0 = (jnp.broadcast_to(carry_re[0:1, cols], (8, S5_CW)), jnp.broadcast_to(carry_im[0:1, cols], (8, S5_CW)))
        cr, ci = lax.fori_loop(0, n_tiles, tile, c0)
        carry_re[:, cols] = cr
        carry_im[:, cols] = ci


def _s5_params_math(ar, ai, ldt, br, bi):
    dt = jnp.exp(ldt)
    mag = jnp.exp(ar * dt)
    ang = ai * dt
    ab_re = mag * jnp.cos(ang)
    ab_im = mag * jnp.sin(ang)
    den = ar * ar + ai * ai
    n_re = ab_re - 1.0
    coef_re = (n_re * ar + ab_im * ai) / den
    coef_im = (ab_im * ar - n_re * ai) / den
    bb_re = coef_re * br - coef_im * bi
    bb_im = coef_re * bi + coef_im * br
    return ab_re, ab_im, bb_re, bb_im


def _s5_params_fwd(ar, ai, ldt, br, bi):
    def body(ar_ref, ai_ref, ldt_ref, br_ref, bi_ref, bbr_ref, bbi_ref, pfr_ref, pfi_ref, prr_ref, pri_ref):
        ab_re, ab_im, bb_re, bb_im = _s5_params_math(ar_ref[...], ai_ref[...], ldt_ref[...], br_ref[...], bi_ref[...])
        bbr_ref[...] = bb_re
        bbi_ref[...] = bb_im
        pr, pi = ab_re, ab_im
        for k in range(8):
            pfr_ref[k:k + 1, :] = pr
            pfi_ref[k:k + 1, :] = pi
            prr_ref[7 - k:8 - k, :] = pr
            pri_ref[7 - k:8 - k, :] = -pi
            pr, pi = pr * ab_re - pi * ab_im, pr * ab_im + pi * ab_re

    b16 = jax.ShapeDtypeStruct((S5_CH, S5_N), F32)
    p8 = jax.ShapeDtypeStruct((8, S5_N), F32)
    return pl.pallas_call(body, name="s5_params_fwd", out_shape=(b16, b16, p8, p8, p8, p8),
                          compiler_params=_params(32))(ar, ai, ldt, br, bi)


def _s5_params_bwd(ar, ai, ldt, br, bi, d_ab_re, d_ab_im, d_bb_re, d_bb_im):
    def body(ar_ref, ai_ref, ldt_ref, br_ref, bi_ref, dar_ref, dai_ref, dbr_ref, dbi_ref,
             gar_ref, gai_ref, gldt_ref, gbr_ref, gbi_ref):
        _, vjp = jax.vjp(_s5_params_math, ar_ref[...], ai_ref[...], ldt_ref[...], br_ref[...], bi_ref[...])
        g_ar, g_ai, g_ldt, g_br, g_bi = vjp((dar_ref[...], dai_ref[...], dbr_ref[...], dbi_ref[...]))
        gar_ref[...] = g_ar
        gai_ref[...] = g_ai
        gbr_ref[...] = g_br
        gbi_ref[...] = g_bi
        lane = lax.broadcasted_iota(jnp.int32, (S5_N, 128), 0) // S5_P
        grp = lax.broadcasted_iota(jnp.int32, (S5_N, 128), 1)
        fold = (lane == grp).astype(F32)
        gldt_ref[...] = jnp.dot(g_ldt, fold, preferred_element_type=F32, precision=HIGHEST)

    v1 = jax.ShapeDtypeStruct((1, S5_N), F32)
    b16 = jax.ShapeDtypeStruct((S5_CH, S5_N), F32)
    return pl.pallas_call(body, name="s5_params_bwd",
                          out_shape=(v1, v1, jax.ShapeDtypeStruct((1, 128), F32), b16, b16),
                          compiler_params=_params(32))(ar, ai, ldt, br, bi, d_ab_re, d_ab_im, d_bb_re, d_bb_im)


def _s5_fwd(u5, bb_re, bb_im, cc_re, cc_im, pf_re, pf_im, s5d, w_glu, b_glu, ycat, seq):
    t = u5.shape[0]
    tb = 256
    npb = seq // tb

    def body(u_ref, bbr_ref, bbi_ref, ccr_ref, cci_ref, pfr_ref, pfi_ref, d_ref, wg_ref, bg_ref, ycat_hbm,
             sre_ref, sim_ref, ypre_ref, y5_ref, bur, bui, car, cai):
        del ycat_hbm

        @pl.when(pl.program_id(0) % npb == 0)
        def _():
            car[...] = jnp.zeros_like(car)
            cai[...] = jnp.zeros_like(cai)
        u = u_ref[...]
        ub = u.astype(BF16)
        for j in range(S5_BLOCKS):
            ch, st = slice(j * 128, (j + 1) * 128), slice(j * 512, (j + 1) * 512)
            bur[:, st] = jnp.dot(ub[:, ch], bbr_ref[j], preferred_element_type=F32)
            bui[:, st] = jnp.dot(ub[:, ch], bbi_ref[j], preferred_element_type=F32)
        _tile_scan(bur, bui, sre_ref, sim_ref, car, cai, pfr_ref, pfi_ref, tb // 8, reverse=False)
        cs_y = []
        for j in range(S5_BLOCKS):
            st = slice(j * 512, (j + 1) * 512)
            cs_y.append(_mm(sre_ref[:, st], ccr_ref[j]) - _mm(sim_ref[:, st], cci_ref[j]))
        ypre = jnp.concatenate(cs_y, axis=1) + u * d_ref[...]
        ypre_ref[...] = ypre
        yg = _gelu(ypre)
        y5_ref[...] = (yg * _sigmoid(_mm(yg, wg_ref[...]) + bg_ref[...])).astype(BF16)

    row = lambda w: pl.BlockSpec((tb, w), lambda i: (i, 0))
    full = lambda a: pl.BlockSpec(a.shape, lambda i: (0,) * a.ndim)
    return pl.pallas_call(
        body, name="s5_fwd", grid=(t // tb,),
        out_shape=(jax.ShapeDtypeStruct((t, S5_N), F32), jax.ShapeDtypeStruct((t, S5_N), F32),
                   jax.ShapeDtypeStruct((t, D_S5), F32), jax.ShapeDtypeStruct(ycat.shape, BF16)),
        in_specs=[row(D_S5), full(bb_re), full(bb_im), full(cc_re), full(cc_im), full(pf_re), full(pf_im),
                  full(s5d), full(w_glu), full(b_glu), ANY],
        out_specs=(row(S5_N), row(S5_N), row(D_S5), pl.BlockSpec((tb, D_S5), lambda i: (i, D_SSD // D_S5))),
        input_output_aliases={10: 3},
        scratch_shapes=[pltpu.VMEM((tb, S5_N), F32), pltpu.VMEM((tb, S5_N), F32),
                        pltpu.VMEM((8, S5_N), F32), pltpu.VMEM((8, S5_N), F32)],
        compiler_params=_params(48),
    )(u5, bb_re, bb_im, cc_re, cc_im, pf_re, pf_im, s5d, w_glu, b_glu, ycat)


def _layer_norm(r, g, b):
    mu = jnp.mean(r, axis=-1, keepdims=True)
    xc = r - mu
    rstd = lax.rsqrt(jnp.mean(xc * xc, axis=-1, keepdims=True) + EPS)
    xhat = xc * rstd
    return xhat * g + b, xhat, rstd


def _layer_norm_bwd(dy, xhat, rstd, g):
    dxhat = dy * g
    return rstd * (dxhat - jnp.mean(dxhat, axis=-1, keepdims=True)
                   - xhat * jnp.mean(dxhat * xhat, axis=-1, keepdims=True))


def _out_ln1(ycat, x2, mod3, w_out, ln1, seq):
    t = x2.shape[0]
    tb = 512
    npb = seq // tb

    def body(y_ref, x_ref, mod_ref, w_ref, ln_ref, mix_ref, x1_ref):
        m = mod_ref[0]
        mix = jnp.dot(y_ref[...], w_ref[...], preferred_element_type=F32)
        mix_ref[...] = mix
        r1 = ALPHA * x_ref[...] + (1.0 + m[2:3]) * mix
        x1_ref[...] = _layer_norm(r1, ln_ref[0:1], ln_ref[1:2])[0]

    row = lambda w: pl.BlockSpec((tb, w), lambda i: (i, 0))
    return pl.pallas_call(
        body, name="out_ln1", grid=(t // tb,),
        out_shape=(jax.ShapeDtypeStruct((t, D_MODEL), F32), jax.ShapeDtypeStruct((t, D_MODEL), F32)),
        in_specs=[row(D_SSD + D_S5), row(D_MODEL), pl.BlockSpec((1, N_MOD, D_MODEL), lambda i: (i // npb, 0, 0)),
                  pl.BlockSpec(w_out.shape, lambda i: (0, 0)), pl.BlockSpec(ln1.shape, lambda i: (0, 0))],
        out_specs=(row(D_MODEL), row(D_MODEL)), compiler_params=_params(48),
    )(ycat, x2, mod3, w_out, ln1)


def _mlp_fwd_bwd(x1, tgt, mod3, w1, w2, vec1, b1, seq):
    t = x1.shape[0]
    tb = 256
    npb = seq // tb
    n_fb, _, fb = w1.shape

    def body(x1_ref, tgt_ref, mod_ref, w1_hbm, w2_hbm, v_ref, b1_ref,
             dx1_ref, u2_ref, h_ref, dhp_ref, do_ref, gacc_ref, db1_ref, bacc_ref, w1_v, w2_v, sem1, sem2):
        i = pl.program_id(0)
        @pl.when(i == 0)
        def _():
            cps = [pltpu.make_async_copy(w1_hbm.at[k], w1_v.at[:, k * fb:(k + 1) * fb], sem1.at[k])
                   for k in range(n_fb)]
            for cp in cps:
                cp.start()
            for cp in cps:
                cp.wait()
        _load_once(w2_hbm, w2_v, sem2)

        @pl.when(i == 0)
        def _():
            gacc_ref[...] = jnp.zeros_like(gacc_ref)
            db1_ref[...] = jnp.zeros_like(db1_ref)

        @pl.when(i % npb == 0)
        def _():
            bacc_ref[...] = jnp.zeros_like(bacc_ref)

        m = mod_ref[0]
        sh2, sc2, g2 = m[3:4], m[4:5], m[5:6]
        x1v = x1_ref[...]
        u2 = (x1v * (1.0 + sc2) + sh2).astype(BF16)
        u2_ref[...] = u2
        hr = jnp.maximum(jnp.dot(u2, w1_v[...], preferred_element_type=F32) + b1_ref[...], 0.0)
        hb = (hr * hr).astype(BF16)
        h_ref[...] = hb
        o = jnp.dot(hb, w2_v[...], preferred_element_type=F32) + v_ref[0:1]
        r2 = ALPHA * x1v + (1.0 + g2) * o
        y, xhat, rstd = _layer_norm(r2, v_ref[1:2], v_ref[2:3])
        err = y - tgt_ref[...]
        dy = err * (1.0 / D_MODEL)
        dr2 = _layer_norm_bwd(dy, xhat, rstd, v_ref[1:2])
        do = (1.0 + g2) * dr2
        dob = do.astype(BF16)
        do_ref[...] = dob
        gacc_ref[0:1, :] += jnp.sum(dy * xhat, axis=0, keepdims=True)
        gacc_ref[1:2, :] += jnp.sum(dy, axis=0, keepdims=True)
        gacc_ref[2:3, :] += jnp.sum(do, axis=0, keepdims=True)
        gacc_ref[3:4, :] += jnp.sum(err * err, axis=0, keepdims=True)
        dhpre = lax.dot_general(dob, w2_v[...], NT, preferred_element_type=F32) * (2.0 * hr)
        dhpb = dhpre.astype(BF16)
        dhp_ref[...] = dhpb
        db1_ref[...] += jnp.sum(dhpre, axis=0, keepdims=True)
        du2 = lax.dot_general(dhpb, w1_v[...], NT, preferred_element_type=F32)
        dx1_ref[...] = ALPHA * dr2 + du2 * (1.0 + sc2)
        bacc_ref[0, 0:1, :] += jnp.sum(du2, axis=0, keepdims=True)
        bacc_ref[0, 1:2, :] += jnp.sum(du2 * x1v, axis=0, keepdims=True)
        bacc_ref[0, 2:3, :] += jnp.sum(dr2 * o, axis=0, keepdims=True)

    row = lambda w: pl.BlockSpec((tb, w), lambda i: (i, 0))
    return pl.pallas_call(
        body, name="mlp_fwd_bwd", grid=(t // tb,),
        out_shape=(jax.ShapeDtypeStruct((t, D_MODEL), F32), jax.ShapeDtypeStruct((t, D_MODEL), BF16),
                   jax.ShapeDtypeStruct((t, D_FF), BF16), jax.ShapeDtypeStruct((t, D_FF), BF16),
                   jax.ShapeDtypeStruct((t, D_MODEL), BF16), jax.ShapeDtypeStruct((8, D_MODEL), F32),
                   jax.ShapeDtypeStruct((1, D_FF), F32), jax.ShapeDtypeStruct((t // seq, 8, D_MODEL), F32)),
        in_specs=[row(D_MODEL), row(D_MODEL), pl.BlockSpec((1, N_MOD, D_MODEL), lambda i: (i // npb, 0, 0)), ANY, ANY,
                  pl.BlockSpec(vec1.shape, lambda i: (0, 0)), pl.BlockSpec(b1.shape, lambda i: (0, 0))],
        out_specs=(row(D_MODEL), row(D_MODEL), row(D_FF), row(D_FF), row(D_MODEL),
                   pl.BlockSpec((8, D_MODEL), lambda i: (0, 0)), pl.BlockSpec((1, D_FF), lambda i: (0, 0)),
                   pl.BlockSpec((1, 8, D_MODEL), lambda i: (i // npb, 0, 0))),
        scratch_shapes=[pltpu.VMEM((D_MODEL, n_fb * fb), BF16), pltpu.VMEM((D_FF, D_MODEL), BF16),
                        pltpu.SemaphoreType.DMA((n_fb,)), pltpu.SemaphoreType.DMA],
        compiler_params=_params(60),
    )(x1, tgt, mod3, w1, w2, vec1, b1)


def _ln1_out_bwd(dx1, x2, mix, mod3, w_out, ln1, seq):
    t = x2.shape[0]
    tb = 512
    npb = seq // tb

    def body(dx1_ref, x_ref, mix_ref, mod_ref, w_ref, ln_ref, dmix_ref, dxa_ref, dys_ref, dy5_ref, gacc_ref, bacc_ref):
        i = pl.program_id(0)

        @pl.when(i == 0)
        def _():
            gacc_ref[...] = jnp.zeros_like(gacc_ref)

        @pl.when(i % npb == 0)
        def _():
            bacc_ref[...] = jnp.zeros_like(bacc_ref)

        m = mod_ref[0]
        mix = mix_ref[...]
        r1 = ALPHA * x_ref[...] + (1.0 + m[2:3]) * mix
        _, xhat, rstd = _layer_norm(r1, ln_ref[0:1], ln_ref[1:2])
        dx1v = dx1_ref[...]
        dr1 = _layer_norm_bwd(dx1v, xhat, rstd, ln_ref[0:1])
        gacc_ref[0:1, :] += jnp.sum(dx1v * xhat, axis=0, keepdims=True)
        gacc_ref[1:2, :] += jnp.sum(dx1v, axis=0, keepdims=True)
        bacc_ref[0, 0:1, :] += jnp.sum(dr1 * mix, axis=0, keepdims=True)
        dmix = ((1.0 + m[2:3]) * dr1).astype(BF16)
        dmix_ref[...] = dmix
        dxa_ref[...] = ALPHA * dr1
        dys_ref[...] = lax.dot_general(dmix, w_ref[0:D_SSD, :], NT, preferred_element_type=F32)
        dy5_ref[...] = lax.dot_general(dmix, w_ref[D_SSD:, :], NT, preferred_element_type=F32)

    row = lambda w: pl.BlockSpec((tb, w), lambda i: (i, 0))
    return pl.pallas_call(
        body, name="ln1_out_bwd", grid=(t // tb,),
        out_shape=(jax.ShapeDtypeStruct((t, D_MODEL), BF16), jax.ShapeDtypeStruct((t, D_MODEL), F32),
                   jax.ShapeDtypeStruct((t, D_SSD), F32), jax.ShapeDtypeStruct((t, D_S5), F32),
                   jax.ShapeDtypeStruct((8, D_MODEL), F32), jax.ShapeDtypeStruct((t // seq, 8, D_MODEL), F32)),
        in_specs=[row(D_MODEL), row(D_MODEL), row(D_MODEL), pl.BlockSpec((1, N_MOD, D_MODEL), lambda i: (i // npb, 0, 0)),
                  pl.BlockSpec(w_out.shape, lambda i: (0, 0)), pl.BlockSpec(ln1.shape, lambda i: (0, 0))],
        out_specs=(row(D_MODEL), row(D_MODEL), row(D_SSD), row(D_S5), pl.BlockSpec((8, D_MODEL), lambda i: (0, 0)),
                   pl.BlockSpec((1, 8, D_MODEL), lambda i: (i // npb, 0, 0))),
        compiler_params=_params(48),
    )(dx1, x2, mix, mod3, w_out, ln1)


def _s5_bwd(dy5, ypre, u5, s_re, s_im, bb_re, bb_im, cc_re, cc_im, pr_re, pr_im, s5d, w_glu, b_glu, seq):
    t = u5.shape[0]
    tb = 256
    npb = seq // tb
    n_blocks = t // tb

    def blk(i):
        return (i // npb) * npb + (npb - 1 - i % npb)

    def body(dy_ref, ypre_ref, u_ref, sre_ref, sim_ref, hre_ref, him_ref, bbr_ref, bbi_ref, ccr_ref, cci_ref,
             prr_ref, pri_ref, d_ref, wg_ref, bg_ref,
             du_ref, vacc_ref, sacc_ref, dcc_ref, dbb_ref, dwg_ref, dsr, dsi, gr, gi, car, cai):
        i = pl.program_id(0)

        @pl.when(i == 0)
        def _():
            for acc in (vacc_ref, sacc_ref, dcc_ref, dbb_ref, dwg_ref):
                acc[...] = jnp.zeros_like(acc)

        @pl.when(i % npb == 0)
        def _():
            car[...] = jnp.zeros_like(car)
            cai[...] = jnp.zeros_like(cai)

        dy = dy_ref[...]
        ypre = ypre_ref[...]
        u = u_ref[...]
        ub = u.astype(BF16)
        yg = _gelu(ypre)
        sg = _sigmoid(_mm(yg, wg_ref[...]) + bg_ref[...])
        dq = dy * yg * sg * (1.0 - sg)
        dqb = dq.astype(BF16)
        dyg = dy * sg + lax.dot_general(dqb, wg_ref[...], NT, preferred_element_type=F32)
        dyp = dyg * _gelu_grad(ypre)
        dypb = dyp.astype(BF16)
        dwg_ref[...] += lax.dot_general(yg.astype(BF16), dqb, TN, preferred_element_type=F32)
        blocks = [(slice(j * 128, (j + 1) * 128), slice(j * 512, (j + 1) * 512)) for j in range(S5_BLOCKS)]
        for j, (ch, st) in enumerate(blocks):
            dsr[:, st] = lax.dot_general(dypb[:, ch], ccr_ref[j], NT, preferred_element_type=F32)
            dsi[:, st] = -lax.dot_general(dypb[:, ch], cci_ref[j], NT, preferred_element_type=F32)
        _tile_scan(dsr, dsi, gr, gi, car, cai, prr_ref, pri_ref, tb // 8, reverse=True)
        g_re, g_im = gr[...], gi[...]
        first_rows = (i % npb) == npb - 1
        hre = jnp.where(first_rows, 0.0, hre_ref[...])
        him = jnp.where(first_rows, 0.0, him_ref[...])
        s_re_v, s_im_v = sre_ref[...], sim_ref[...]
        sp_re = pltpu.roll(jnp.concatenate([hre, s_re_v], axis=0), 1, axis=0)[8:8 + tb]
        sp_im = pltpu.roll(jnp.concatenate([him, s_im_v], axis=0), 1, axis=0)[8:8 + tb]
        vacc_ref[0:1, :] += jnp.sum(g_re * sp_re + g_im * sp_im, axis=0, keepdims=True)
        vacc_ref[1:2, :] += jnp.sum(g_im * sp_re - g_re * sp_im, axis=0, keepdims=True)
        grb, gib = g_re.astype(BF16), g_im.astype(BF16)
        srb, sib = s_re_v.astype(BF16), s_im_v.astype(BF16)
        du_cols = []
        for j, (ch, st) in enumerate(blocks):
            dcc_ref[j] += lax.dot_general(srb[:, st], dypb[:, ch], TN, preferred_element_type=F32)
            dcc_ref[S5_BLOCKS + j] -= lax.dot_general(sib[:, st], dypb[:, ch], TN, preferred_element_type=F32)
            dbb_ref[j] += lax.dot_general(ub[:, ch], grb[:, st], TN, preferred_element_type=F32)
            dbb_ref[S5_BLOCKS + j] += lax.dot_general(ub[:, ch], gib[:, st], TN, preferred_element_type=F32)
            du_cols.append(lax.dot_general(grb[:, st], bbr_ref[j], NT, preferred_element_type=F32)
                           + lax.dot_general(gib[:, st], bbi_ref[j], NT, preferred_element_type=F32))
        du_ref[...] = jnp.concatenate(du_cols, axis=1) + dyp * d_ref[...]
        sacc_ref[0:1, :] += jnp.sum(dyp * u, axis=0, keepdims=True)
        sacc_ref[1:2, :] += jnp.sum(dq, axis=0, keepdims=True)

    row = lambda w: pl.BlockSpec((tb, w), lambda i: (blk(i), 0))
    halo = pl.BlockSpec((8, S5_N), lambda i: (jnp.maximum(blk(i) * (tb // 8) - 1, 0), 0))
    full = lambda a: pl.BlockSpec(a.shape, lambda i: (0,) * a.ndim)
    acc = lambda s: pl.BlockSpec(s, lambda i: (0,) * len(s))
    acc_shapes = [(8, S5_N), (8, D_S5), (2 * S5_BLOCKS, 512, 128), (2 * S5_BLOCKS, 128, 512), (D_S5, D_S5)]
    return pl.pallas_call(
        body, name="s5_bwd", grid=(n_blocks,),
        out_shape=(jax.ShapeDtypeStruct((t, D_S5), F32),) + tuple(jax.ShapeDtypeStruct(s, F32) for s in acc_shapes),
        in_specs=[row(D_S5), row(D_S5), row(D_S5), row(S5_N), row(S5_N), halo, halo, full(bb_re), full(bb_im),
                  full(cc_re), full(cc_im), full(pr_re), full(pr_im), full(s5d), full(w_glu), full(b_glu)],
        out_specs=(row(D_S5),) + tuple(acc(s) for s in acc_shapes),
        scratch_shapes=[pltpu.VMEM((tb, S5_N), F32), pltpu.VMEM((tb, S5_N), F32), pltpu.VMEM((tb, S5_N), F32),
                        pltpu.VMEM((tb, S5_N), F32), pltpu.VMEM((8, S5_N), F32), pltpu.VMEM((8, S5_N), F32)],
        compiler_params=_params(56),
    )(dy5, ypre, u5, s_re, s_im, s_re, s_im, bb_re, bb_im, cc_re, cc_im, pr_re, pr_im, s5d, w_glu, b_glu)


def _ssd_bwd(dyssd, yraw, z, xbc, dt_raw, hprev, par, dsk, normw, seq):
    t = xbc.shape[0]
    nc = seq // CHUNK
    n_chunks = t // CHUNK
    fold = _head_fold()

    def blk(i):
        return (i // nc) * nc + (nc - 1 - i % nc)

    def body(dy_ref, yraw_ref, z_ref, xbc_ref, dt_ref, hprev_ref, par_ref, dsk_ref, nw_ref, fold_ref,
             dxbc_ref, dz_ref, ddt_ref, dpar_ref, cacc_ref, dh_ref, dyr_ref):
        i = pl.program_id(0)

        @pl.when(i == 0)
        def _():
            dpar_ref[...] = jnp.zeros_like(dpar_ref)
            cacc_ref[...] = jnp.zeros_like(cacc_ref)

        @pl.when(i % nc == 0)
        def _():
            dh_ref[...] = jnp.zeros_like(dh_ref)

        zz = z_ref[...]
        sz = _sigmoid(zz)
        silu_z = zz * sz
        yraw = yraw_ref[...]
        for g in range(N_GROUPS):
            sl = slice(g * GW, (g + 1) * GW)
            v = yraw[:, sl] * silu_z[:, sl]
            r = lax.rsqrt(jnp.mean(v * v, axis=-1, keepdims=True) + EPS)
            dyg = dy_ref[:, sl]
            cacc_ref[1:2, sl] += jnp.sum(dyg * v * r, axis=0, keepdims=True)
            dyw = dyg * nw_ref[:, sl]
            dv = r * dyw - v * (r * r * r) * jnp.mean(dyw * v, axis=-1, keepdims=True)
            dyr_ref[:, sl] = dv * silu_z[:, sl]
            dz_ref[:, sl] = dv * yraw[:, sl] * (sz[:, sl] * (1.0 + zz[:, sl] * (1.0 - sz[:, sl])))

        dt, a, cs, cst, causal, tri, dt_c, ecs_c, w_c, pair_cols = _ssd_prep(dt_ref[...], par_ref[...])
        cs_last = cs[CHUNK - 1:CHUNK, :]
        causal2 = jnp.concatenate([causal, causal], axis=1)
        lane = lax.broadcasted_iota(jnp.int32, (CHUNK, 128), 1)
        left = lane < HEADDIM
        lane1 = lax.broadcasted_iota(jnp.int32, (1, 128), 1)
        x = xbc_ref[:, 0:D_SSD]
        xdt = x * dt_c
        dyr = dyr_ref[...]
        dyrb = dyr.astype(BF16)
        cacc_ref[0:1, :] += jnp.sum(dyr * x, axis=0, keepdims=True)
        dlast = jnp.zeros((1, 128), F32)
        dxdt_cols, diag_all, dww_cols = [], [], []
        for g in range(N_GROUPS):
            gs = slice(g * GW, (g + 1) * GW)
            b_sl = slice(D_SSD + g * N_STATE, D_SSD + (g + 1) * N_STATE)
            c_sl = slice(D_SSD + (N_GROUPS + g) * N_STATE, D_SSD + (N_GROUPS + g + 1) * N_STATE)
            bg = xbc_ref[:, b_sl].astype(BF16)
            cg = xbc_ref[:, c_sl].astype(BF16)
            scores = lax.dot_general(cg, bg, NT, preferred_element_type=F32)
            scores2 = jnp.concatenate([scores, scores], axis=1)
            hg = hprev_ref[0, gs, :]
            hgb = hg.astype(BF16)
            dhg = dh_ref[gs, :]
            dhgb = dhg.astype(BF16)
            q_all = lax.dot_general(bg, dhgb, NT, preferred_element_type=F32)
            dscores = jnp.zeros((CHUNK, CHUNK), F32)
            diag_cols = []
            for q in range(GW // 128):
                pair = g * (GW // 128) + q
                ps = slice(pair * 128, (pair + 1) * 128)
                decay = _pair_decay(pair_cols[pair], cst, pair, causal2)
                mcat = (scores2 * decay).astype(BF16)
                dyp = dyrb[:, ps]
                dm = lax.dot_general(dyp, _stack_heads(xdt[:, ps], left), NT, preferred_element_type=F32)
                dmd = dm * decay
                dscores = dscores + dmd[:, 0:CHUNK] + dmd[:, CHUNK:]
                rr = lax.dot_general(mcat, dyp, TN, preferred_element_type=F32)
                diag_cols.append(jnp.where(left, rr[0:CHUNK], rr[CHUNK:]))
            wq = w_c[:, gs] * q_all
            diag_g = jnp.concatenate(diag_cols, axis=1)
            diag_all.append(diag_g)
            dxdt_cols.append(diag_g + wq)
            dww_cols.append(wq * xdt[:, gs])
            dp = (ecs_c[:, gs] * dyr[:, gs]).astype(BF16)
            amat = (w_c[:, gs] * xdt[:, gs]).astype(BF16)
            dsb = dscores.astype(BF16)
            dxbc_ref[:, c_sl] = (jnp.dot(dsb, bg, preferred_element_type=F32)
                                 + jnp.dot(dp, hgb, preferred_element_type=F32))
            dxbc_ref[:, b_sl] = (lax.dot_general(dsb, cg, TN, preferred_element_type=F32)
                                 + jnp.dot(amat, dhgb, preferred_element_type=F32))
            dh_in = lax.dot_general(dp, cg, TN, preferred_element_type=F32)
            for j in range(HPG):
                hh = g * HPG + j
                js = slice(j * HEADDIM, (j + 1) * HEADDIM)
                ecl = jnp.exp(cs_last[:, hh:hh + 1])
                dlast = dlast + jnp.where(lane1 == hh, ecl * jnp.sum(dhg[js, :] * hg[js, :]), 0.0)
                dh_ref[g * GW + j * HEADDIM:g * GW + (j + 1) * HEADDIM, :] = ecl * dhg[js, :] + dh_in[js, :]
        dxdt = jnp.concatenate(dxdt_cols, axis=1)
        dxbc_ref[:, 0:D_SSD] = dxdt * dt_c + dyr * dsk_ref[...]
        dww = _dot3(jnp.concatenate(dww_cols, axis=1), fold_ref[...])
        dcs = (_dot3(dyrb.astype(F32) * (yraw - x * dsk_ref[...]), fold_ref[...])
               - _dot3(xdt.astype(BF16).astype(F32) * jnp.concatenate(diag_all, axis=1), fold_ref[...]) - dww)
        rowid = lax.broadcasted_iota(jnp.int32, (CHUNK, 128), 0)
        dcs = dcs + jnp.where(rowid == CHUNK - 1, jnp.sum(dww, axis=0, keepdims=True) + dlast, 0.0)
        dadt = _dot3_left(tri, dcs, TN)
        ddt = _dot3(dxdt * x, fold_ref[...]) + dadt * a
        da = jnp.sum(dadt * dt, axis=0, keepdims=True)
        ddt_raw = ddt * _sigmoid(dt_ref[...] + par_ref[0:1])
        ddt_raw = jnp.where(lane < N_HEADS, ddt_raw, 0.0)
        ddt_ref[...] = ddt_raw
        dpar_ref[0:1, :] += jnp.sum(ddt_raw, axis=0, keepdims=True)
        dpar_ref[1:2, :] += jnp.where(lane1 < N_HEADS, da * a, 0.0)

    row = lambda w: pl.BlockSpec((CHUNK, w), lambda i: (blk(i), 0))
    full = lambda s: pl.BlockSpec(s, lambda i: (0,) * len(s))
    return pl.pallas_call(
        body, name="ssd_bwd", grid=(n_chunks,),
        out_shape=(jax.ShapeDtypeStruct((t, D_XBC), F32), jax.ShapeDtypeStruct((t, D_SSD), F32),
                   jax.ShapeDtypeStruct((t, DT_PAD), F32), jax.ShapeDtypeStruct((8, 128), F32),
                   jax.ShapeDtypeStruct((8, D_SSD), F32)),
        in_specs=[row(D_SSD), row(D_SSD), row(D_SSD), row(D_XBC), row(DT_PAD),
                  pl.BlockSpec((1, D_SSD, N_STATE), lambda i: (blk(i), 0, 0)),
                  full((8, 128)), full((1, D_SSD)), full((1, D_SSD)), full(fold.shape)],
        out_specs=(row(D_XBC), row(D_SSD), row(DT_PAD), full((8, 128)), full((8, D_SSD))),
        scratch_shapes=[pltpu.VMEM((D_SSD, N_STATE), F32), pltpu.VMEM((CHUNK, D_SSD), F32)],
        compiler_params=_params(48),
    )(dyssd, yraw, z, xbc, dt_raw, hprev, par, dsk, normw, fold)


def _conv_bwd(dxbc, xbc_pre, conv_w, conv_b, seq):
    t = xbc_pre.shape[0]
    tb = 512
    npb = seq // tb
    cw = 640

    def body(d_ref, cur_ref, halo_ref, w_ref, b_ref, o_ref, acc_ref, win):
        i = pl.program_id(1)

        @pl.when(i == 0)
        def _():
            acc_ref[...] = jnp.zeros_like(acc_ref)

        first = (i % npb) == 0
        win[0:8, :] = jnp.where(first, 0.0, halo_ref[...])
        win[8:8 + tb, :] = cur_ref[...]
        pre, shifted = _conv_taps(win, w_ref[...], tb, slice(None))
        pre = pre + b_ref[...]
        sg = _sigmoid(pre)
        dpre = d_ref[...] * (sg * (1.0 + pre * (1.0 - sg)))
        o_ref[...] = dpre
        for j in range(4):
            acc_ref[3 - j:4 - j, :] += jnp.sum(dpre * shifted[j], axis=0, keepdims=True)
        acc_ref[4:5, :] += jnp.sum(dpre, axis=0, keepdims=True)

    return pl.pallas_call(
        body, name="conv_bwd", grid=(D_XBC // cw, t // tb),
        out_shape=(jax.ShapeDtypeStruct((t, D_XBC), F32), jax.ShapeDtypeStruct((8, D_XBC), F32)),
        in_specs=[pl.BlockSpec((tb, cw), lambda j, i: (i, j)), pl.BlockSpec((tb, cw), lambda j, i: (i, j)),
                  pl.BlockSpec((8, cw), lambda j, i: (jnp.maximum(i * (tb // 8) - 1, 0), j)),
                  pl.BlockSpec((4, cw), lambda j, i: (0, j)), pl.BlockSpec((1, cw), lambda j, i: (0, j))],
        out_specs=(pl.BlockSpec((tb, cw), lambda j, i: (i, j)), pl.BlockSpec((8, cw), lambda j, i: (0, j))),
        scratch_shapes=[pltpu.VMEM((tb + 8, cw), F32)],
        compiler_params=_params(32),
    )(dxbc, xbc_pre, xbc_pre, conv_w, conv_b)


def _proj_bwd(dz, dpre, ddt, du5, x2, dxa, mod3, conv_w, w_in_pad, seq):
    t = x2.shape[0]
    tb = 512
    npb = seq // tb
    n_blocks = t // tb

    def body(dz_ref, dp_ref, nxt_ref, ddt_ref, du5_ref, x_ref, dxa_ref, mod_ref, cw_ref, w_hbm,
             gx_ref, u_ref, dxp_ref, bacc_ref, w_vmem, sem):
        i = pl.program_id(0)
        _load_once(w_hbm, w_vmem, sem)

        @pl.when(i % npb == 0)
        def _():
            bacc_ref[...] = jnp.zeros_like(bacc_ref)

        last = (i % npb) == npb - 1
        nxt = jnp.where(last, 0.0, nxt_ref[...])
        cur = dp_ref[...]
        xx = jnp.concatenate([cur, nxt], axis=0)
        w = cw_ref[...]
        dxp = w[3:4] * cur
        for j in (1, 2, 3):
            dxp = dxp + w[3 - j:4 - j] * pltpu.roll(xx, tb + 8 - j, axis=0)[0:tb]
        dxpb = dxp.astype(BF16)
        dxp_ref[...] = dxpb
        o1, o2, o3 = D_SSD, D_SSD + D_XBC, D_SSD + D_XBC + DT_PAD
        du = (lax.dot_general(dz_ref[...].astype(BF16), w_vmem[:, 0:o1], NT, preferred_element_type=F32)
              + lax.dot_general(dxpb, w_vmem[:, o1:o2], NT, preferred_element_type=F32)
              + lax.dot_general(ddt_ref[...].astype(BF16), w_vmem[:, o2:o3], NT, preferred_element_type=F32)
              + lax.dot_general(du5_ref[...].astype(BF16), w_vmem[:, o3:], NT, preferred_element_type=F32))
        m = mod_ref[0]
        xv = x_ref[...]
        u_ref[...] = (xv * (1.0 + m[1:2]) + m[0:1]).astype(BF16)
        gx_ref[...] = dxa_ref[...] + du * (1.0 + m[1:2])
        bacc_ref[0, 0:1, :] += jnp.sum(du, axis=0, keepdims=True)
        bacc_ref[0, 1:2, :] += jnp.sum(du * xv, axis=0, keepdims=True)

    row = lambda w: pl.BlockSpec((tb, w), lambda i: (i, 0))
    nxt_rows = pl.BlockSpec((8, D_XBC), lambda i: (jnp.minimum((i + 1) * (tb // 8), t // 8 - 1), 0))
    return pl.pallas_call(
        body, name="proj_bwd", grid=(n_blocks,),
        out_shape=(jax.ShapeDtypeStruct((t, D_MODEL), F32), jax.ShapeDtypeStruct((t, D_MODEL), BF16),
                   jax.ShapeDtypeStruct((t, D_XBC), BF16), jax.ShapeDtypeStruct((t // seq, 8, D_MODEL), F32)),
        in_specs=[row(D_SSD), row(D_XBC), nxt_rows, row(DT_PAD), row(D_S5), row(D_MODEL), row(D_MODEL),
                  pl.BlockSpec((1, N_MOD, D_MODEL), lambda i: (i // npb, 0, 0)),
                  pl.BlockSpec((4, D_XBC), lambda i: (0, 0)), ANY],
        out_specs=(row(D_MODEL), row(D_MODEL), row(D_XBC), pl.BlockSpec((1, 8, D_MODEL), lambda i: (i // npb, 0, 0))),
        scratch_shapes=[pltpu.VMEM((D_MODEL, D_INP), BF16), pltpu.SemaphoreType.DMA],
        compiler_params=_params(60),
    )(dz, dpre, dpre, ddt, du5, x2, dxa, mod3, conv_w, w_in_pad)


def _pad_rows(a, mult):
    r = a.shape[0]
    pad = (-r) % mult
    return a if pad == 0 else jnp.concatenate([a, jnp.zeros((pad,) + a.shape[1:], a.dtype)], axis=0)


_SMALL = ["conv_w", "conv_b", "dt_bias", "a_log", "d_ssd", "norm_w", "s5_a_re", "s5_a_im", "s5_log_dt", "s5_b_re",
          "s5_b_im", "s5_c_re", "s5_c_im", "s5_d", "b_glu", "ln1_g", "ln1_b", "b1", "b2", "ln2_g", "ln2_b"]


def _tile_rows(size):
    return 8 * (-(-size // 1024))


def _pack_small(d):
    parts = []
    for n in _SMALL:
        flat = d[n].reshape(-1).astype(F32)
        rows = _tile_rows(flat.shape[0])
        pad = rows * 128 - flat.shape[0]
        if pad:
            flat = jnp.concatenate([flat, jnp.zeros((pad,), F32)])
        parts.append(flat.reshape(rows, 128))
    return jnp.concatenate(parts, axis=0)


def _unpack_small(p, shapes):
    out, off = {}, 0
    for n in _SMALL:
        size = math.prod(shapes[n])
        rows = _tile_rows(size)
        out[n] = p[off:off + rows].reshape(-1)[:size].reshape(shapes[n])
        off += rows
    return out


def kernel(x, c, w_ada, b_ada, w_in, conv_w, conv_b, dt_bias, a_log, d_ssd, norm_w, s5_a_re, s5_a_im, s5_log_dt, s5_b_re, s5_b_im, s5_c_re, s5_c_im, s5_d, w_glu, b_glu, w_out, ln1_g, ln1_b, w1, b1, w2, b2, ln2_g, ln2_b, loss_target, m_w_ada, m_b_ada, m_w_in, m_conv_w, m_conv_b, m_dt_bias, m_a_log, m_d_ssd, m_norm_w, m_s5_a_re, m_s5_a_im, m_s5_log_dt, m_s5_b_re, m_s5_b_im, m_s5_c_re, m_s5_c_im, m_s5_d, m_w_glu, m_b_glu, m_w_out, m_ln1_g, m_ln1_b, m_w1, m_b1, m_w2, m_b2, m_ln2_g, m_ln2_b, v_w_ada, v_b_ada, v_w_in, v_conv_w, v_conv_b, v_dt_bias, v_a_log, v_d_ssd, v_norm_w, v_s5_a_re, v_s5_a_im, v_s5_log_dt, v_s5_b_re, v_s5_b_im, v_s5_c_re, v_s5_c_im, v_s5_d, v_w_glu, v_b_glu, v_w_out, v_ln1_g, v_ln1_b, v_w1, v_b1, v_w2, v_b2, v_ln2_g, v_ln2_b):
    weights = dict(w_ada=w_ada, b_ada=b_ada, w_in=w_in, conv_w=conv_w, conv_b=conv_b, dt_bias=dt_bias, a_log=a_log,
                   d_ssd=d_ssd, norm_w=norm_w, s5_a_re=s5_a_re, s5_a_im=s5_a_im, s5_log_dt=s5_log_dt, s5_b_re=s5_b_re,
                   s5_b_im=s5_b_im, s5_c_re=s5_c_re, s5_c_im=s5_c_im, s5_d=s5_d, w_glu=w_glu, b_glu=b_glu, w_out=w_out,
                   ln1_g=ln1_g, ln1_b=ln1_b, w1=w1, b1=b1, w2=w2, b2=b2, ln2_g=ln2_g, ln2_b=ln2_b)
    mom = dict(w_ada=m_w_ada, b_ada=m_b_ada, w_in=m_w_in, conv_w=m_conv_w, conv_b=m_conv_b, dt_bias=m_dt_bias,
               a_log=m_a_log, d_ssd=m_d_ssd, norm_w=m_norm_w, s5_a_re=m_s5_a_re, s5_a_im=m_s5_a_im,
               s5_log_dt=m_s5_log_dt, s5_b_re=m_s5_b_re, s5_b_im=m_s5_b_im, s5_c_re=m_s5_c_re, s5_c_im=m_s5_c_im,
               s5_d=m_s5_d, w_glu=m_w_glu, b_glu=m_b_glu, w_out=m_w_out, ln1_g=m_ln1_g, ln1_b=m_ln1_b, w1=m_w1, b1=m_b1,
               w2=m_w2, b2=m_b2, ln2_g=m_ln2_g, ln2_b=m_ln2_b)
    var = dict(w_ada=v_w_ada, b_ada=v_b_ada, w_in=v_w_in, conv_w=v_conv_w, conv_b=v_conv_b, dt_bias=v_dt_bias,
               a_log=v_a_log, d_ssd=v_d_ssd, norm_w=v_norm_w, s5_a_re=v_s5_a_re, s5_a_im=v_s5_a_im,
               s5_log_dt=v_s5_log_dt, s5_b_re=v_s5_b_re, s5_b_im=v_s5_b_im, s5_c_re=v_s5_c_re, s5_c_im=v_s5_c_im,
               s5_d=v_s5_d, w_glu=v_w_glu, b_glu=v_b_glu, w_out=v_w_out, ln1_g=v_ln1_g, ln1_b=v_ln1_b, w1=v_w1, b1=v_b1,
               w2=v_w2, b2=v_b2, ln2_g=v_ln2_g, ln2_b=v_ln2_b)
    names = list(weights)
    shapes = {n: weights[n].shape for n in names}

    nb, seq, _ = x.shape
    t = nb * seq
    dev = _dev_index()
    x2 = x.reshape(t, D_MODEL)
    tgt2 = loss_target.reshape(t, D_MODEL)

    cw_cols = conv_w.shape[2]
    small_in = jnp.concatenate([c.reshape(-1), conv_w.reshape(-1)]).reshape(-1, 128)
    big_names = ["w_in", "w_out", "w1", "w2", "w_glu"]
    shard_bf16 = {n: weights[n][0].astype(BF16) for n in big_names}
    first = _all_gather([small_in, shard_bf16["w_in"], shard_bf16["w_glu"]], "gather_first")
    small_all = first[0].reshape(N_DEV, -1)
    c_all = small_all[:, :nb * D_MODEL].reshape(N_DEV * nb, D_MODEL)
    conv_w_full = small_all[:, nb * D_MODEL:].reshape(N_DEV, 4, cw_cols).transpose(1, 0, 2).reshape(4, D_XBC)

    w_in_f = first[1].transpose(1, 0, 2).reshape(D_MODEL, D_IN)
    w_in_pad = jnp.concatenate(
        [w_in_f[:, :D_SSD + D_XBC], w_in_f[:, D_SSD + D_XBC:D_SSD + D_XBC + N_HEADS],
         jnp.zeros((D_MODEL, DT_PAD - N_HEADS), BF16), w_in_f[:, D_SSD + D_XBC + N_HEADS:]], axis=1)
    w_glu_f = first[2].reshape(D_S5, D_S5)
    late_names = ["w_out", "w1", "w2"]

    ada_cols = w_ada.shape[2]
    b_cols = lax.dynamic_slice_in_dim(b_ada, dev * ada_cols, ada_cols, axis=1)
    mod_cols = _mod_fwd(c_all, w_ada[0], b_cols)
    mod_all = _all_gather([mod_cols], "gather_mod")[0]
    mod_mine = lax.dynamic_slice_in_dim(mod_all, dev * nb, nb, axis=1)
    mod3 = mod_mine.transpose(1, 0, 2).reshape(nb, N_MOD, D_MODEL)
    late_in, mod3 = lax.optimization_barrier(([shard_bf16[n] for n in late_names], mod3))
    late_sems = _gather_start(late_in, "gather_late_start")
    mod3 = mod3 + late_sems[4][0, 0]

    def pad_lanes(v, n):
        return jnp.concatenate([v, jnp.zeros((v.shape[0], n - v.shape[1]), F32)], axis=1)

    par = _pad_rows(jnp.concatenate([pad_lanes(dt_bias, 128), pad_lanes(a_log, 128)], axis=0), 8)
    dsk = jnp.repeat(d_ssd[0], HEADDIM).reshape(1, D_SSD)
    ar = s5_a_re.reshape(1, S5_N)
    ai = s5_a_im.reshape(1, S5_N)
    ldt = jnp.repeat(s5_log_dt[0], S5_P).reshape(1, S5_N)
    br_t = s5_b_re[0].transpose(2, 0, 1).reshape(S5_CH, S5_N)
    bi_t = s5_b_im[0].transpose(2, 0, 1).reshape(S5_CH, S5_N)
    bb_re_t, bb_im_t, pf_re, pf_im, pr_re, pr_im = _s5_params_fwd(ar, ai, ldt, br_t, bi_t)
    gpb = S5_GROUPS // S5_BLOCKS
    mask_b = (jnp.arange(128)[:, None] // S5_CH) == (jnp.arange(512)[None, :] // S5_P)

    def dense_b(bt_):
        blocks = bt_.reshape(S5_CH, S5_BLOCKS, 512).transpose(1, 0, 2)
        return jnp.where(mask_b, jnp.tile(blocks, (1, gpb, 1)), 0.0).astype(BF16)

    def dense_c(cc):
        blocks = cc[0].transpose(0, 2, 1).reshape(S5_BLOCKS, 512, S5_CH)
        return jnp.where(mask_b.T, jnp.tile(blocks, (1, 1, gpb)), 0.0).astype(BF16)

    bb_re, bb_im = dense_b(bb_re_t), dense_b(bb_im_t)
    cc_re, cc_im = dense_c(s5_c_re), dense_c(s5_c_im)
    s5d = s5_d.reshape(1, D_S5)
    ln1 = jnp.concatenate([ln1_g, ln1_b], axis=0)
    vec1 = _pad_rows(jnp.concatenate([b2, ln2_g, ln2_b], axis=0), 8)

    z, xbc_pre, xbc, dt_raw, u5 = _proj_conv_fwd(x2, mod3, w_in_pad, conv_w_full, conv_b, seq)
    yraw, ycat, hprev = _ssd_fwd(xbc, z, dt_raw, par, dsk, norm_w, seq)
    s_re, s_im, ypre, ycat = _s5_fwd(u5, bb_re, bb_im, cc_re, cc_im, pf_re, pf_im, s5d, w_glu_f, b_glu, ycat, seq)
    landed = _gather_wait(late_sems[0], late_sems[1], late_sems[2], late_sems[3], ycat, "gather_late_wait")
    gathered = {n: lax.dynamic_update_index_in_dim(l, shard_bf16[n], dev, 0) for n, l in zip(late_names, landed)}
    w_out_f = gathered["w_out"].reshape(2 * D_MODEL, D_MODEL)
    w1_blocks = gathered["w1"]
    w2_f = gathered["w2"].reshape(D_FF, D_MODEL)
    mix, x1 = _out_ln1(ycat, x2, mod3, w_out_f, ln1, seq)

    dx1, u2b, hb, dhpb, dob, gacc2, db1, bacc2 = _mlp_fwd_bwd(x1, tgt2, mod3, w1_blocks, w2_f, vec1, b1, seq)
    loss = lax.psum(0.5 / D_MODEL * jnp.sum(gacc2[3]), ("x", "y", "c"))

    dmixb, dxa, dyssd, dy5, gacc1, bacc1 = _ln1_out_bwd(dx1, x2, mix, mod3, w_out_f, ln1, seq)
    du5, vacc, sacc, d_cc, d_bb, g_wglu = _s5_bwd(dy5, ypre, u5, s_re, s_im, bb_re, bb_im, cc_re, cc_im,
                                                  pr_re, pr_im, s5d, w_glu_f, b_glu, seq)
    dxbc, dz, ddt, dpar, cacc = _ssd_bwd(dyssd, yraw, z, xbc, dt_raw, hprev, par, dsk, norm_w, seq)
    dpre, conv_acc = _conv_bwd(dxbc, xbc_pre, conv_w_full, conv_b, seq)
    grad_x2, ub, dxpb, bacc0 = _proj_bwd(dz, dpre, ddt, du5, x2, dxa, mod3, conv_w_full, w_in_pad, seq)

    g_w2 = _atb(hb, dob, "gw2")
    g_w1 = _atb(u2b, dhpb, "gw1")
    g_wout = _atb(ycat, dmixb, "gwout")
    g_win = jnp.concatenate([_atb(ub, dz, "gwin_z"), _atb(ub, dxpb, "gwin_xbc"),
                             _atb(ub, ddt, "gwin_dt")[:, :N_HEADS], _atb(ub, du5, "gwin_s5")], axis=1)

    def diag_b(dd):
        kept = jnp.where(mask_b, dd, 0.0).reshape(S5_BLOCKS, gpb, S5_CH, 512).sum(1)
        return kept.transpose(1, 0, 2).reshape(S5_CH, S5_N)

    def diag_c(dd):
        kept = jnp.where(mask_b.T, dd, 0.0).reshape(S5_BLOCKS, 512, gpb, S5_CH).sum(2)
        return kept.reshape(S5_GROUPS, S5_P, S5_CH).transpose(0, 2, 1)

    g_ar, g_ai, g_ldt, g_br_t, g_bi_t = _s5_params_bwd(ar, ai, ldt, br_t, bi_t, vacc[0:1], vacc[1:2],
                                                      diag_b(d_bb[:S5_BLOCKS]), diag_b(d_bb[S5_BLOCKS:]))

    def from_t(gt):
        return gt.reshape(S5_CH, S5_GROUPS, S5_P).transpose(1, 2, 0)

    small_g = dict(
        conv_w=conv_acc[0:4], conv_b=conv_acc[4:5], dt_bias=dpar[0:1, :N_HEADS], a_log=dpar[1:2, :N_HEADS],
        d_ssd=cacc[0].reshape(N_HEADS, HEADDIM).sum(1), norm_w=cacc[1:2],
        s5_a_re=g_ar, s5_a_im=g_ai, s5_log_dt=g_ldt[:, :S5_GROUPS], s5_b_re=from_t(g_br_t), s5_b_im=from_t(g_bi_t),
        s5_c_re=diag_c(d_cc[:S5_BLOCKS]), s5_c_im=diag_c(d_cc[S5_BLOCKS:]), s5_d=sacc[0:1], b_glu=sacc[1:2],
        ln1_g=gacc1[0:1], ln1_b=gacc1[1:2], b1=db1, b2=gacc2[2:3], ln2_g=gacc2[0:1], ln2_b=gacc2[1:2])

    dmod = jnp.concatenate([bacc0[:, 0], bacc0[:, 1], bacc1[:, 0], bacc2[:, 0], bacc2[:, 1], bacc2[:, 2]], axis=1)
    dmod_all = _all_gather([dmod], "gather_dmod")[0].reshape(N_DEV * nb, N_MOD * D_MODEL)
    dmod_cols = lax.dynamic_slice_in_dim(dmod_all, dev * ada_cols, ada_cols, axis=1)
    g_wada, g_bada = _mod_bwd(c_all, dmod_cols, dmod_all)

    in_cols = w_in.shape[2]
    big_g = dict(
        w_in=g_win.reshape(D_MODEL, N_DEV, in_cols).transpose(1, 0, 2),
        w_out=g_wout.reshape((N_DEV,) + w_out.shape[1:]), w1=g_w1,
        w2=g_w2.reshape((N_DEV,) + w2.shape[1:]), w_glu=g_wglu.reshape((N_DEV,) + w_glu.shape[1:]))
    by_dest = [big_g[n] if n == "w1" else big_g[n].reshape((4, 2) + big_g[n].shape[1:]) for n in big_names]
    from_sibling = _sibling_swap(by_dest, "rs_sibling_swap")
    core = lax.axis_index("c").astype(jnp.int32).reshape(1)
    chip_sums = [_add_halves(g, r, core, "rs_add_" + n) for g, r, n in zip(by_dest, from_sibling, big_names)]
    parts = _chip_all_to_all(chip_sums, "rs_chip_all_to_all")

    res = {k: {} for k in "gdmv"}
    for n, p in zip(big_names, parts):
        outs = _adamw(p, weights[n][0], mom[n][0], var[n][0], "adamw_" + n)
        for k, a in zip("gdmv", outs):
            res[k][n] = a[None]

    ag, ad, am, av = _adamw(g_wada[None], w_ada[0], m_w_ada[0], v_w_ada[0], "adamw_w_ada")
    for k, a in (("g", ag), ("d", ad), ("m", am), ("v", av)):
        res[k]["w_ada"] = a[None]
    bg_, bd_, bm_, bv_ = _adamw(g_bada.reshape(1, -1, 128), b_ada.reshape(-1, 128), m_b_ada.reshape(-1, 128),
                                v_b_ada.reshape(-1, 128), "adamw_b_ada")
    for k, a in (("g", bg_), ("d", bd_), ("m", bm_), ("v", bv_)):
        res[k]["b_ada"] = a.reshape(shapes["b_ada"])

    small_shapes = dict(shapes)
    small_shapes["conv_w"] = (1, 4, D_XBC)
    small_parts = _all_gather([_pack_small(small_g)], "gather_small_grads")[0]
    rep = {n: (jnp.zeros((1, 4, D_XBC), F32) if n == "conv_w" else weights[n]) for n in _SMALL}
    rep_m = {n: (jnp.zeros((1, 4, D_XBC), F32) if n == "conv_w" else mom[n]) for n in _SMALL}
    rep_v = {n: (jnp.ones((1, 4, D_XBC), F32) if n == "conv_w" else var[n]) for n in _SMALL}
    sg_, sd_, sm_, sv_ = _adamw(small_parts, _pack_small(rep), _pack_small(rep_m), _pack_small(rep_v), "adamw_small")
    for k, p in (("g", sg_), ("d", sd_), ("m", sm_), ("v", sv_)):
        un = _unpack_small(p, small_shapes)
        for n in _SMALL:
            if n != "conv_w":
                res[k][n] = un[n]
    g_conv_full = _unpack_small(sg_, small_shapes)["conv_w"][0]
    g_conv_mine = lax.dynamic_slice_in_dim(g_conv_full, dev * cw_cols, cw_cols, axis=1)
    cg_, cd_, cm_, cv_ = _adamw(g_conv_mine[None], conv_w[0], m_conv_w[0], v_conv_w[0], "adamw_conv_w")
    for k, a in (("g", cg_), ("d", cd_), ("m", cm_), ("v", cv_)):
        res[k]["conv_w"] = a[None]

    grad_x = grad_x2.reshape(nb, seq, D_MODEL)
    return (loss, grad_x, *[res["g"][n] for n in names], *[res["d"][n] for n in names],
            *[res["m"][n] for n in names], *[res["v"][n] for n in names])
```

```python
import functools
import math

import jax
import jax.numpy as jnp
from jax import lax
from jax.experimental import pallas as pl
from jax.experimental.pallas import tpu as pltpu

F32, BF16 = jnp.float32, jnp.bfloat16
MESH = pl.DeviceIdType.MESH
N_DEV = 8

D_MODEL = 1024
D_SSD = 1536
N_HEADS = 24
HEADDIM = 64
N_GROUPS = 4
HPG = 6
GW = HPG * HEADDIM
N_STATE = 128
CHUNK = 128
D_XBC = 2560
D_S5 = 512
S5_GROUPS = 32
S5_CH = 16
S5_P = 64
S5_N = S5_GROUPS * S5_P
D_IN = 4632
DT_PAD = 128
D_INP = D_SSD + D_XBC + DT_PAD + D_S5
D_FF = 4096
N_MOD = 6
ALPHA = 2.0 ** 0.25
EPS = 1e-5
LR, B1, B2, AEPS, WD, STEP = 0.001, 0.9, 0.999, 1e-08, 0.01, 10

NT = (((1,), (1,)), ((), ()))
TN = (((0,), (0,)), ((), ()))
ANY = pl.BlockSpec(memory_space=pl.ANY)
HIGHEST = lax.Precision.HIGHEST


def _mm(a, b):
    return jnp.dot(a.astype(BF16), b.astype(BF16), preferred_element_type=F32)


def _mm_nt(a, b):
    return lax.dot_general(a.astype(BF16), b.astype(BF16), NT, preferred_element_type=F32)


def _mm_tn(a, b):
    return lax.dot_general(a.astype(BF16), b.astype(BF16), TN, preferred_element_type=F32)


def _row_block(r, cap):
    best = r
    for cand in range(8, min(r, cap) + 1, 8):
        if r % cand == 0:
            best = cand
    return best if best <= cap else r


def _params(vmem_mb):
    return pltpu.CompilerParams(vmem_limit_bytes=vmem_mb << 20)


def _sigmoid(x):
    return 0.5 * (jnp.tanh(0.5 * x) + 1.0)


def _softplus(x):
    return jnp.maximum(x, 0.0) + jnp.log(1.0 + jnp.exp(-jnp.abs(x)))


_GK = math.sqrt(2.0 / math.pi)


def _gelu(x):
    return 0.5 * x * (1.0 + jnp.tanh(_GK * (x + 0.044715 * x * x * x)))


def _gelu_grad(x):
    t = jnp.tanh(_GK * (x + 0.044715 * x * x * x))
    return 0.5 * (1.0 + t) + 0.5 * x * (1.0 - t * t) * _GK * (1.0 + 3.0 * 0.044715 * x * x)


def _dev_index():
    return 4 * lax.axis_index("x") + 2 * lax.axis_index("y") + lax.axis_index("c")


def _all_gather(xs, name):
    n = len(xs)

    def body(*refs):
        x_refs, out_refs = refs[:n], refs[n:2 * n]
        send_sems, recv_sems, local_sems = refs[2 * n:]
        ix, iy, ic = lax.axis_index("x"), lax.axis_index("y"), lax.axis_index("c")
        me, sibling = (ix, iy, ic), (ix, iy, 1 - ic)
        chips = [(1 - ix, iy), (ix, 1 - iy), (1 - ix, 1 - iy)]

        def slot(a, px, py, pc):
            return out_refs[a].at[4 * px + 2 * py + pc]

        def copy(a, k, block, to, src=None):
            return pltpu.make_async_remote_copy(
                src_ref=slot(a, *block) if src is None else src, dst_ref=slot(a, *block),
                send_sem=send_sems.at[7 * a + k], recv_sem=recv_sems.at[7 * a + k], device_id=to, device_id_type=MESH)

        mine = [pltpu.make_async_copy(x_refs[a], slot(a, *me), local_sems.at[a]) for a in range(n)]
        for cp in mine:
            cp.start()
        first = []
        for j, chip in enumerate(chips):
            first += [copy(a, 1 + j, me, (*chip, ic), src=x_refs[a]) for a in range(n)]
        first += [copy(a, 0, me, sibling, src=x_refs[a]) for a in range(n)]
        for cp in first:
            cp.start()
        passed = []
        for j, chip in enumerate(chips):
            for a in range(n):
                copy(a, 1 + j, (*chip, ic), me).wait_recv()
                cp = copy(a, 4 + j, (*chip, ic), sibling)
                cp.start()
                passed.append(cp)
        for a in range(n):
            copy(a, 0, sibling, me).wait_recv()
            for j, chip in enumerate(chips):
                copy(a, 4 + j, (*chip, 1 - ic), me).wait_recv()
        for cp in first + passed:
            cp.wait_send()
        for cp in mine:
            cp.wait()

    return pl.pallas_call(
        body, name=name, out_shape=tuple(jax.ShapeDtypeStruct((N_DEV,) + x.shape, x.dtype) for x in xs),
        in_specs=[ANY] * n, out_specs=tuple([ANY] * n),
        scratch_shapes=[pltpu.SemaphoreType.DMA((7 * n,)), pltpu.SemaphoreType.DMA((7 * n,)),
                        pltpu.SemaphoreType.DMA((n,))],
    )(*xs)


HBM = pl.BlockSpec(memory_space=pltpu.HBM)
SEM = pl.BlockSpec(memory_space=pltpu.SEMAPHORE)
DATAFLOW = pltpu.SideEffectType.DATAFLOW_SIDE_EFFECTING


def _peer(k):
    ix, iy, ic = lax.axis_index("x"), lax.axis_index("y"), lax.axis_index("c")
    return (1 - ix if k & 4 else ix, 1 - iy if k & 2 else iy, 1 - ic if k & 1 else ic)


def _block_of(p):
    return 4 * p[0] + 2 * p[1] + p[2]


def _gather_start(xs, name):
    n = len(xs)
    lands = [lax.empty((N_DEV,) + x.shape, x.dtype) for x in xs]

    def body(*refs):
        x_refs, land_refs = refs[:n], refs[n:2 * n]
        send_sems, recv_sems = refs[2 * n], refs[2 * n + 1]
        token = refs[-1]
        me = _block_of(_peer(0))
        for a in range(n):
            for k in range(1, N_DEV):
                pltpu.make_async_remote_copy(
                    src_ref=x_refs[a], dst_ref=land_refs[a].at[me], send_sem=send_sems.at[7 * a + k - 1],
                    recv_sem=recv_sems.at[7 * a + k - 1], device_id=_peer(k), device_id_type=MESH).start()
        token[...] = jnp.zeros_like(token)

    outs = pl.pallas_call(
        body, name=name,
        out_shape=(pltpu.SemaphoreType.DMA((7 * n,)), pltpu.SemaphoreType.DMA((7 * n,)))
        + tuple(pltpu.HBM(x.shape, x.dtype) for x in xs) + tuple(pltpu.HBM(l.shape, l.dtype) for l in lands)
        + (jax.ShapeDtypeStruct((8, 128), F32),),
        in_specs=[HBM] * (2 * n), out_specs=(SEM, SEM) + (HBM,) * (2 * n) + (pl.BlockSpec(memory_space=pltpu.VMEM),),
        input_output_aliases={i: 2 + i for i in range(2 * n)},
        compiler_params=pltpu.CompilerParams(has_side_effects=DATAFLOW),
    )(*[pltpu.with_memory_space_constraint(x, pltpu.HBM) for x in xs],
      *[pltpu.with_memory_space_constraint(l, pltpu.HBM) for l in lands])
    return outs[0], outs[1], outs[2:2 + n], outs[2 + n:2 + 2 * n], outs[-1]


def _gather_wait(send_sems, recv_sems, xs_thru, lands_thru, after, name):
    n = len(xs_thru)

    def body(*refs):
        x_refs, land_refs = refs[:n], refs[n:2 * n]
        send_sems, recv_sems = refs[2 * n], refs[2 * n + 1]
        for a in range(n):
            for k in range(1, N_DEV):
                cp = pltpu.make_async_remote_copy(
                    src_ref=x_refs[a], dst_ref=land_refs[a].at[_block_of(_peer(k))], send_sem=send_sems.at[7 * a + k - 1],
                    recv_sem=recv_sems.at[7 * a + k - 1], device_id=_peer(k), device_id_type=MESH)
                cp.wait_send()
                cp.wait_recv()

    outs = pl.pallas_call(
        body, name=name,
        out_shape=tuple(pltpu.HBM(x.shape, x.dtype) for x in xs_thru)
        + tuple(pltpu.HBM(l.shape, l.dtype) for l in lands_thru),
        in_specs=[HBM] * (2 * n) + [SEM, SEM, ANY], out_specs=(HBM,) * (2 * n),
        input_output_aliases={i: i for i in range(2 * n)},
        compiler_params=pltpu.CompilerParams(has_side_effects=DATAFLOW),
    )(*xs_thru, *lands_thru, send_sems, recv_sems, after)
    return outs[:n], outs[n:]


def _chip_peer(k):
    ix, iy = lax.axis_index("x"), lax.axis_index("y")
    return (1 - ix if k & 2 else ix, 1 - iy if k & 1 else iy)


def _all_to_all_start(hs, name):
    n = len(hs)
    lands = [lax.empty(h.shape, h.dtype) for h in hs]

    def body(*refs):
        h_refs, land_refs = refs[:n], refs[n:2 * n]
        send_sems, recv_sems = refs[2 * n], refs[2 * n + 1]
        token = refs[-1]
        ic = lax.axis_index("c")
        mx, my = _chip_peer(0)
        for a in range(n):
            for k in range(1, 4):
                px, py = _chip_peer(k)
                pltpu.make_async_remote_copy(
                    src_ref=h_refs[a].at[2 * px + py], dst_ref=land_refs[a].at[2 * mx + my],
                    send_sem=send_sems.at[3 * a + k - 1], recv_sem=recv_sems.at[3 * a + k - 1],
                    device_id=(px, py, ic), device_id_type=MESH).start()
        token[...] = jnp.zeros_like(token)

    outs = pl.pallas_call(
        body, name=name,
        out_shape=(pltpu.SemaphoreType.DMA((3 * n,)), pltpu.SemaphoreType.DMA((3 * n,)))
        + tuple(pltpu.HBM(h.shape, h.dtype) for h in hs) + tuple(pltpu.HBM(l.shape, l.dtype) for l in lands)
        + (jax.ShapeDtypeStruct((8, 128), F32),),
        in_specs=[HBM] * (2 * n), out_specs=(SEM, SEM) + (HBM,) * (2 * n) + (pl.BlockSpec(memory_space=pltpu.VMEM),),
        input_output_aliases={i: 2 + i for i in range(2 * n)},
        compiler_params=pltpu.CompilerParams(has_side_effects=DATAFLOW),
    )(*[pltpu.with_memory_space_constraint(h, pltpu.HBM) for h in hs],
      *[pltpu.with_memory_space_constraint(l, pltpu.HBM) for l in lands])
    return outs[0], outs[1], outs[2:2 + n], outs[2 + n:2 + 2 * n], outs[-1]


def _all_to_all_wait(send_sems, recv_sems, hs_thru, lands_thru, after, name):
    n = len(hs_thru)

    def body(*refs):
        h_refs, land_refs = refs[:n], refs[n:2 * n]
        send_sems, recv_sems = refs[2 * n], refs[2 * n + 1]
        ic = lax.axis_index("c")
        for a in range(n):
            for k in range(1, 4):
                px, py = _chip_peer(k)
                cp = pltpu.make_async_remote_copy(
                    src_ref=h_refs[a].at[2 * px + py], dst_ref=land_refs[a].at[2 * px + py],
                    send_sem=send_sems.at[3 * a + k - 1], recv_sem=recv_sems.at[3 * a + k - 1],
                    device_id=(px, py, ic), device_id_type=MESH)
                cp.wait_send()
                cp.wait_recv()

    outs = pl.pallas_call(
        body, name=name,
        out_shape=tuple(pltpu.HBM(h.shape, h.dtype) for h in hs_thru)
        + tuple(pltpu.HBM(l.shape, l.dtype) for l in lands_thru),
        in_specs=[HBM] * (2 * n) + [SEM, SEM, ANY], out_specs=(HBM,) * (2 * n),
        input_output_aliases={i: i for i in range(2 * n)},
        compiler_params=pltpu.CompilerParams(has_side_effects=DATAFLOW),
    )(*hs_thru, *lands_thru, send_sems, recv_sems, after)
    return outs[:n], outs[n:]


def _sibling_swap(gs, name):
    n = len(gs)

    def body(*refs):
        g_refs, recv_refs = refs[:n], refs[n:2 * n]
        send_sems, recv_sems = refs[2 * n:]
        ix, iy, ic = lax.axis_index("x"), lax.axis_index("y"), lax.axis_index("c")

        def block(g_ref, q):
            if len(g_ref.shape) == 4:
                return g_ref.at[q, 1 - ic]
            cw = g_ref.shape[1] // N_DEV
            return g_ref.at[:, pl.ds(pl.multiple_of((2 * q + 1 - ic) * cw, 128), cw)]

        cps = []
        for a in range(n):
            for q in range(4):
                cps.append(pltpu.make_async_remote_copy(
                    src_ref=block(g_refs[a], q), dst_ref=recv_refs[a].at[q],
                    send_sem=send_sems.at[4 * a + q], recv_sem=recv_sems.at[4 * a + q],
                    device_id=(ix, iy, 1 - ic), device_id_type=MESH))
        for cp in cps:
            cp.start()
        for cp in cps:
            cp.wait()

    return pl.pallas_call(
        body, name=name,
        out_shape=tuple(jax.ShapeDtypeStruct(
            (4,) + (g.shape[2:] if g.ndim == 4 else (g.shape[0], g.shape[1] // N_DEV)), g.dtype) for g in gs),
        in_specs=[ANY] * n, out_specs=tuple([ANY] * n),
        scratch_shapes=[pltpu.SemaphoreType.DMA((4 * n,)), pltpu.SemaphoreType.DMA((4 * n,))],
    )(*gs)


def _chip_all_to_all(hs, name):
    n = len(hs)

    def body(*refs):
        h_refs, out_refs = refs[:n], refs[n:2 * n]
        send_sems, recv_sems, local_sems = refs[2 * n:]
        ix, iy, ic = lax.axis_index("x"), lax.axis_index("y"), lax.axis_index("c")
        me = 2 * ix + iy
        peers = [(1 - ix, iy), (ix, 1 - iy), (1 - ix, 1 - iy)]
        mine = [pltpu.make_async_copy(h_refs[a].at[me], out_refs[a].at[me], local_sems.at[a]) for a in range(n)]
        for cp in mine:
            cp.start()

        def copy(a, k, src_slot, dst_slot, peer):
            return pltpu.make_async_remote_copy(
                src_ref=h_refs[a].at[src_slot], dst_ref=out_refs[a].at[dst_slot],
                send_sem=send_sems.at[3 * a + k], recv_sem=recv_sems.at[3 * a + k],
                device_id=(*peer, ic), device_id_type=MESH)

        sends = [copy(a, k, 2 * px + py, me, (px, py)) for a in range(n) for k, (px, py) in enumerate(peers)]
        for cp in sends:
            cp.start()
        for a in range(n):
            for k, (px, py) in enumerate(peers):
                copy(a, k, 2 * px + py, 2 * px + py, (px, py)).wait_recv()
        for cp in sends:
            cp.wait_send()
        for cp in mine:
            cp.wait()

    return pl.pallas_call(
        body, name=name, out_shape=tuple(jax.ShapeDtypeStruct(h.shape, h.dtype) for h in hs),
        in_specs=[ANY] * n, out_specs=tuple([ANY] * n),
        scratch_shapes=[pltpu.SemaphoreType.DMA((3 * n,)), pltpu.SemaphoreType.DMA((3 * n,)),
                        pltpu.SemaphoreType.DMA((n,))],
    )(*hs)


def _add_halves(g, recv, core, name):
    _, r, c = recv.shape
    br = _row_block(r, 512)
    stacked = g.ndim == 4

    def body(core_ref, g_ref, r_ref, o_ref):
        o_ref[0] = ((g_ref[0, 0] if stacked else g_ref[...]) + r_ref[0]).astype(BF16)

    spec = pl.BlockSpec((1, br, c), lambda i, j, core_ref: (i, j, 0))
    if stacked:
        g_spec = pl.BlockSpec((1, 1, br, c), lambda i, j, core_ref: (i, core_ref[0], j, 0))
    else:
        g_spec = pl.BlockSpec((br, c), lambda i, j, core_ref: (j, 2 * i + core_ref[0]))
    return pl.pallas_call(
        body, name=name, out_shape=jax.ShapeDtypeStruct(recv.shape, BF16),
        grid_spec=pltpu.PrefetchScalarGridSpec(
            num_scalar_prefetch=1, grid=(4, r // br), in_specs=[g_spec, spec], out_specs=spec),
        compiler_params=_params(32),
    )(core, g, recv)


def _adamw(parts, w, m, v, name):
    n_parts, r, c = parts.shape
    br = _row_block(r, 512 if c <= 1024 else 256)

    def body(p_ref, w_ref, m_ref, v_ref, g_out, d_out, m_out, v_out):
        g = p_ref[0].astype(F32)
        for p in range(1, n_parts):
            g = g + p_ref[p].astype(F32)
        m2 = B1 * m_ref[...] + (1.0 - B1) * g
        v2 = B2 * v_ref[...] + (1.0 - B2) * (g * g)
        m_hat = m2 / (1.0 - B1 ** STEP)
        v_hat = v2 / (1.0 - B2 ** STEP)
        g_out[...] = g
        d_out[...] = -LR * (m_hat / (jnp.sqrt(v_hat) + AEPS) + WD * w_ref[...])
        m_out[...] = m2
        v_out[...] = v2

    spec = pl.BlockSpec((br, c), lambda i: (i, 0))
    out = jax.ShapeDtypeStruct((r, c), F32)
    return pl.pallas_call(
        body, name=name, out_shape=(out, out, out, out), grid=(r // br,),
        in_specs=[pl.BlockSpec((n_parts, br, c), lambda i: (0, i, 0)), spec, spec, spec],
        out_specs=(spec, spec, spec, spec), compiler_params=_params(40),
    )(parts, w, m, v)


def _atb(a, b, name, bt=512):
    t, k1 = a.shape
    k2 = b.shape[1]

    def pick(k):
        for cand in (1024, 768, 512, 384, 256, 128):
            if k % cand == 0:
                return cand
        return k

    b1, b2 = pick(k1), pick(k2)

    def body(a_ref, b_ref, o_ref):
        @pl.when(pl.program_id(2) == 0)
        def _():
            o_ref[...] = jnp.zeros_like(o_ref)
        o_ref[...] += _mm_tn(a_ref[...], b_ref[...])

    return pl.pallas_call(
        body, name=name, out_shape=jax.ShapeDtypeStruct((k1, k2), F32), grid=(k1 // b1, k2 // b2, t // bt),
        in_specs=[pl.BlockSpec((bt, b1), lambda i, j, k: (k, i)), pl.BlockSpec((bt, b2), lambda i, j, k: (k, j))],
        out_specs=pl.BlockSpec((b1, b2), lambda i, j, k: (i, j)), compiler_params=_params(40),
    )(a, b)


def _mod_fwd(c_all, w_ada, b_cols):
    def body(c_ref, w_ref, b_ref, o_ref):
        cc = c_ref[...]
        cond = cc * _sigmoid(cc)
        o_ref[...] = _mm(cond, w_ref[...]) + b_ref[...]

    return pl.pallas_call(body, name="mod_fwd", out_shape=jax.ShapeDtypeStruct((c_all.shape[0], w_ada.shape[1]), F32),
                          compiler_params=_params(32))(c_all, w_ada, b_cols)


def _mod_bwd(c_all, dmod_cols, dmod_all):
    def body(c_ref, dc_ref, da_ref, gw_ref, gb_ref):
        cc = c_ref[...]
        cond = cc * _sigmoid(cc)
        gw_ref[...] = _mm_tn(cond, dc_ref[...])
        gb_ref[...] = jnp.sum(da_ref[...], axis=0, keepdims=True)

    return pl.pallas_call(
        body, name="mod_bwd",
        out_shape=(jax.ShapeDtypeStruct((D_MODEL, dmod_cols.shape[1]), F32), jax.ShapeDtypeStruct((1, dmod_all.shape[1]), F32)),
        compiler_params=_params(32))(c_all, dmod_cols, dmod_all)


def _load_once(hbm_ref, vmem_ref, sem):
    @pl.when(pl.program_id(0) == 0)
    def _():
        cp = pltpu.make_async_copy(hbm_ref, vmem_ref, sem)
        cp.start()
        cp.wait()


def _conv_taps(win_ref, w, tb, cols):
    shifted = [win_ref[8 - j:8 - j + tb, cols] for j in range(4)]
    acc = w[3:4] * shifted[0]
    for j in (1, 2, 3):
        acc = acc + w[3 - j:4 - j] * shifted[j]
    return acc, shifted


def _proj_conv_fwd(x2, mod3, w_in_pad, conv_w, conv_b, seq):
    t = x2.shape[0]
    tb = 256
    npb = seq // tb
    cw = 512

    def body(x_ref, mod_ref, w_hbm, cw_ref, cb_ref, z_ref, pre_ref, xbc_ref, dt_ref, u5_ref, w_vmem, win, sem):
        _load_once(w_hbm, w_vmem, sem)
        first = (pl.program_id(0) % npb) == 0

        @pl.when(first)
        def _():
            win[0:8, :] = jnp.zeros((8, D_XBC), F32)

        @pl.when(jnp.logical_not(first))
        def _():
            win[0:8, :] = win[tb:tb + 8, :]

        m = mod_ref[0]
        u = (x_ref[...] * (1.0 + m[1:2]) + m[0:1]).astype(BF16)
        z_ref[...] = jnp.dot(u, w_vmem[:, 0:D_SSD], preferred_element_type=F32)
        dt_ref[...] = jnp.dot(u, w_vmem[:, D_SSD + D_XBC:D_SSD + D_XBC + DT_PAD], preferred_element_type=F32)
        u5_ref[...] = jnp.dot(u, w_vmem[:, D_SSD + D_XBC + DT_PAD:], preferred_element_type=F32)
        for k in range(D_XBC // cw):
            cols = slice(k * cw, (k + 1) * cw)
            pre_k = jnp.dot(u, w_vmem[:, D_SSD + k * cw:D_SSD + (k + 1) * cw], preferred_element_type=F32)
            win[8:8 + tb, cols] = pre_k
            pre_ref[:, cols] = pre_k
            conv, _ = _conv_taps(win, cw_ref[:, cols], tb, cols)
            conv = conv + cb_ref[:, cols]
            xbc_ref[:, cols] = conv * _sigmoid(conv)

    row = lambda w: pl.BlockSpec((tb, w), lambda i: (i, 0))
    return pl.pallas_call(
        body, name="proj_conv_fwd", grid=(t // tb,),
        out_shape=(jax.ShapeDtypeStruct((t, D_SSD), F32), jax.ShapeDtypeStruct((t, D_XBC), F32),
                   jax.ShapeDtypeStruct((t, D_XBC), F32), jax.ShapeDtypeStruct((t, DT_PAD), F32),
                   jax.ShapeDtypeStruct((t, D_S5), F32)),
        in_specs=[row(D_MODEL), pl.BlockSpec((1, N_MOD, D_MODEL), lambda i: (i // npb, 0, 0)), ANY,
                  pl.BlockSpec((4, D_XBC), lambda i: (0, 0)), pl.BlockSpec((1, D_XBC), lambda i: (0, 0))],
        out_specs=(row(D_SSD), row(D_XBC), row(D_XBC), row(DT_PAD), row(D_S5)),
        scratch_shapes=[pltpu.VMEM((D_MODEL, D_INP), BF16), pltpu.VMEM((tb + 8, D_XBC), F32), pltpu.SemaphoreType.DMA],
        compiler_params=_params(56),
    )(x2, mod3, w_in_pad, conv_w, conv_b)


N_PAIRS = N_HEADS // 2


def _split3(x):
    hi = x.astype(BF16)
    r = x - hi.astype(F32)
    mid = r.astype(BF16)
    lo = (r - mid.astype(F32)).astype(BF16)
    return hi, mid, lo


def _dot3(x, e, dims=(((1,), (0,)), ((), ()))):
    return sum(lax.dot_general(p, e, dims, preferred_element_type=F32) for p in _split3(x))


def _dot3_left(e, x, dims=(((1,), (0,)), ((), ()))):
    return sum(lax.dot_general(e, p, dims, preferred_element_type=F32) for p in _split3(x))


def _head_fold():
    return (jnp.arange(D_SSD)[:, None] // HEADDIM == jnp.arange(128)[None, :]).astype(BF16)


def _ssd_prep(dt_raw, par):
    dtb = par[0:1]
    a = -jnp.exp(par[1:2])
    dt = _softplus(dt_raw + dtb)
    adt = dt * a
    row = lax.broadcasted_iota(jnp.int32, (CHUNK, CHUNK), 0)
    col = lax.broadcasted_iota(jnp.int32, (CHUNK, CHUNK), 1)
    causal = row >= col
    tri = causal.astype(BF16)
    cs = _dot3_left(tri, adt)
    left = col < HEADDIM

    def lanes(v, h):
        return jnp.broadcast_to(v[:, h:h + 1], (CHUNK, 128))

    dt_c, cs_c, pair_cols = [], [], []
    for p in range(N_PAIRS):
        c0, c1 = lanes(cs, 2 * p), lanes(cs, 2 * p + 1)
        pair_cols.append(jnp.concatenate([c0, c1], axis=1))
        cs_c.append(jnp.where(left, c0, c1))
        dt_c.append(jnp.where(left, lanes(dt, 2 * p), lanes(dt, 2 * p + 1)))
    cs_c = jnp.concatenate(cs_c, axis=1)
    dt_c = jnp.concatenate(dt_c, axis=1)
    return dt, a, cs, cs.T, causal, tri, dt_c, jnp.exp(cs_c), jnp.exp(cs_c[CHUNK - 1:CHUNK, :] - cs_c), pair_cols


def _pair_decay(cols, cst, pair, causal2):
    rows = jnp.concatenate([jnp.broadcast_to(cst[2 * pair:2 * pair + 1, :], (CHUNK, CHUNK)),
                            jnp.broadcast_to(cst[2 * pair + 1:2 * pair + 2, :], (CHUNK, CHUNK))], axis=1)
    return jnp.exp(jnp.where(causal2, cols - rows, -jnp.inf))


def _stack_heads(xp, left):
    return jnp.concatenate([jnp.where(left, xp, 0.0), jnp.where(left, 0.0, xp)], axis=0).astype(BF16)


def _ssd_fwd(xbc, z, dt_raw, par, dsk, normw, seq):
    t = xbc.shape[0]
    nc = seq // CHUNK
    n_chunks = t // CHUNK

    def body(xbc_ref, z_ref, dt_ref, par_ref, dsk_ref, nw_ref, yraw_ref, ycat_ref, hprev_ref, h_ref):
        @pl.when(pl.program_id(0) % nc == 0)
        def _():
            h_ref[...] = jnp.zeros_like(h_ref)
        hprev_ref[0] = h_ref[...]
        _, _, cs, cst, causal, _, dt_c, ecs_c, w_c, pair_cols = _ssd_prep(dt_ref[...], par_ref[...])
        cs_last = cs[CHUNK - 1:CHUNK, :]
        causal2 = jnp.concatenate([causal, causal], axis=1)
        left = lax.broadcasted_iota(jnp.int32, (CHUNK, 128), 1) < HEADDIM
        x = xbc_ref[:, 0:D_SSD]
        xdt = x * dt_c
        amat = (w_c * xdt).astype(BF16)
        zz = z_ref[...]
        silu_z = zz * _sigmoid(zz)
        for g in range(N_GROUPS):
            gs = slice(g * GW, (g + 1) * GW)
            bg = xbc_ref[:, D_SSD + g * N_STATE:D_SSD + (g + 1) * N_STATE].astype(BF16)
            cg = xbc_ref[:, D_SSD + (N_GROUPS + g) * N_STATE:D_SSD + (N_GROUPS + g + 1) * N_STATE].astype(BF16)
            scores = lax.dot_general(cg, bg, NT, preferred_element_type=F32)
            scores2 = jnp.concatenate([scores, scores], axis=1)
            hg = h_ref[gs, :]
            p_all = lax.dot_general(cg, hg.astype(BF16), NT, preferred_element_type=F32)
            ys = []
            for q in range(GW // 128):
                pair = g * (GW // 128) + q
                decay = _pair_decay(pair_cols[pair], cst, pair, causal2)
                mcat = (scores2 * decay).astype(BF16)
                ys.append(jnp.dot(mcat, _stack_heads(xdt[:, pair * 128:(pair + 1) * 128], left),
                                  preferred_element_type=F32))
            yg = jnp.concatenate(ys, axis=1) + ecs_c[:, gs] * p_all + x[:, gs] * dsk_ref[:, gs]
            s_new = lax.dot_general(amat[:, gs], bg, TN, preferred_element_type=F32)
            for j in range(HPG):
                hh = g * HPG + j
                js = slice(j * HEADDIM, (j + 1) * HEADDIM)
                h_ref[g * GW + j * HEADDIM:g * GW + (j + 1) * HEADDIM, :] = (
                    hg[js, :] * jnp.exp(cs_last[:, hh:hh + 1]) + s_new[js, :])
            yraw_ref[:, gs] = yg
            v = yg * silu_z[:, gs]
            r = lax.rsqrt(jnp.mean(v * v, axis=-1, keepdims=True) + EPS)
            ycat_ref[:, gs] = (v * r * nw_ref[:, gs]).astype(BF16)

    row = lambda w: pl.BlockSpec((CHUNK, w), lambda i: (i, 0))
    full = lambda s: pl.BlockSpec(s, lambda i: (0,) * len(s))
    return pl.pallas_call(
        body, name="ssd_fwd", grid=(n_chunks,),
        out_shape=(jax.ShapeDtypeStruct((t, D_SSD), F32), jax.ShapeDtypeStruct((t, D_SSD + D_S5), BF16),
                   jax.ShapeDtypeStruct((n_chunks, D_SSD, N_STATE), F32)),
        in_specs=[row(D_XBC), row(D_SSD), row(DT_PAD), full((8, 128)), full((1, D_SSD)), full((1, D_SSD))],
        out_specs=(row(D_SSD), row(D_SSD), pl.BlockSpec((1, D_SSD, N_STATE), lambda i: (i, 0, 0))),
        scratch_shapes=[pltpu.VMEM((D_SSD, N_STATE), F32)],
        compiler_params=_params(40),
    )(xbc, z, dt_raw, par, dsk, normw)


S5_CW = 512
S5_BLOCKS = 4


def _tile_scan(in_re, in_im, out_re, out_im, carry_re, carry_im, pw_re, pw_im, n_tiles, reverse):
    steps = (1, 2, 4)
    for cc in range(S5_N // S5_CW):
        cols = slice(cc * S5_CW, (cc + 1) * S5_CW)
        a_re, a_im = pw_re[:, cols], pw_im[:, cols]
        rid = lax.broadcasted_iota(jnp.int32, (8, S5_CW), 0)
        pows = []
        for d in steps:
            k = 8 - d if reverse else d - 1
            keep = (rid < 8 - d) if reverse else (rid >= d)
            pows.append((jnp.where(keep, pw_re[k:k + 1, cols], 0.0), jnp.where(keep, pw_im[k:k + 1, cols], 0.0)))

        def tile(i, carry, cols=cols, pows=pows, a_re=a_re, a_im=a_im):
            r = (n_tiles - 1 - i) if reverse else i
            rows = pl.ds(pl.multiple_of(r * 8, 8), 8)
            xr, xi = in_re[rows, cols], in_im[rows, cols]
            for (pr, pi), d in zip(pows, steps):
                shift = 8 - d if reverse else d
                sr, si = pltpu.roll(xr, shift, axis=0), pltpu.roll(xi, shift, axis=0)
                xr, xi = xr + pr * sr - pi * si, xi + pr * si + pi * sr
            cr, ci = carry
            xr, xi = xr + a_re * cr - a_im * ci, xi + a_re * ci + a_im * cr
            out_re[rows, cols] = xr
            out_im[rows, cols] = xi
            edge = slice(0, 1) if reverse else slice(7, 8)
            return (jnp.broadcast_to(xr[edge], (8, S5_CW)), jnp.broadcast_to(xi[edge], (8, S5_CW)))

        c0 = (jnp.broadcast_to(carry_re[0:1, cols], (8, S5_CW)), jnp.broadcast_to(carry_im[0:1, cols], (8, S5_CW)))
        cr, ci = lax.fori_loop(0, n_tiles, tile, c0)
        carry_re[:, cols] = cr
        carry_im[:, cols] = ci


def _s5_params_math(ar, ai, ldt, br, bi):
    dt = jnp.exp(ldt)
    mag = jnp.exp(ar * dt)
    ang = ai * dt
    ab_re = mag * jnp.cos(ang)
    ab_im = mag * jnp.sin(ang)
    den = ar * ar + ai * ai
    n_re = ab_re - 1.0
    coef_re = (n_re * ar + ab_im * ai) / den
    coef_im = (ab_im * ar - n_re * ai) / den
    bb_re = coef_re * br - coef_im * bi
    bb_im = coef_re * bi + coef_im * br
    return ab_re, ab_im, bb_re, bb_im


def _s5_params_fwd(ar, ai, ldt, br, bi):
    def body(ar_ref, ai_ref, ldt_ref, br_ref, bi_ref, bbr_ref, bbi_ref, pfr_ref, pfi_ref, prr_ref, pri_ref):
        ab_re, ab_im, bb_re, bb_im = _s5_params_math(ar_ref[...], ai_ref[...], ldt_ref[...], br_ref[...], bi_ref[...])
        bbr_ref[...] = bb_re
        bbi_ref[...] = bb_im
        pr, pi = ab_re, ab_im
        for k in range(8):
            pfr_ref[k:k + 1, :] = pr
            pfi_ref[k:k + 1, :] = pi
            prr_ref[7 - k:8 - k, :] = pr
            pri_ref[7 - k:8 - k, :] = -pi
            pr, pi = pr * ab_re - pi * ab_im, pr * ab_im + pi * ab_re

    b16 = jax.ShapeDtypeStruct((S5_CH, S5_N), F32)
    p8 = jax.ShapeDtypeStruct((8, S5_N), F32)
    return pl.pallas_call(body, name="s5_params_fwd", out_shape=(b16, b16, p8, p8, p8, p8),
                          compiler_params=_params(32))(ar, ai, ldt, br, bi)


def _s5_params_bwd(ar, ai, ldt, br, bi, d_ab_re, d_ab_im, d_bb_re, d_bb_im):
    def body(ar_ref, ai_ref, ldt_ref, br_ref, bi_ref, dar_ref, dai_ref, dbr_ref, dbi_ref,
             gar_ref, gai_ref, gldt_ref, gbr_ref, gbi_ref):
        _, vjp = jax.vjp(_s5_params_math, ar_ref[...], ai_ref[...], ldt_ref[...], br_ref[...], bi_ref[...])
        g_ar, g_ai, g_ldt, g_br, g_bi = vjp((dar_ref[...], dai_ref[...], dbr_ref[...], dbi_ref[...]))
        gar_ref[...] = g_ar
        gai_ref[...] = g_ai
        gbr_ref[...] = g_br
        gbi_ref[...] = g_bi
        lane = lax.broadcasted_iota(jnp.int32, (S5_N, 128), 0) // S5_P
        grp = lax.broadcasted_iota(jnp.int32, (S5_N, 128), 1)
        fold = (lane == grp).astype(F32)
        gldt_ref[...] = jnp.dot(g_ldt, fold, preferred_element_type=F32, precision=HIGHEST)

    v1 = jax.ShapeDtypeStruct((1, S5_N), F32)
    b16 = jax.ShapeDtypeStruct((S5_CH, S5_N), F32)
    return pl.pallas_call(body, name="s5_params_bwd",
                          out_shape=(v1, v1, jax.ShapeDtypeStruct((1, 128), F32), b16, b16),
                          compiler_params=_params(32))(ar, ai, ldt, br, bi, d_ab_re, d_ab_im, d_bb_re, d_bb_im)


def _s5_fwd(u5, bb_re, bb_im, cc_re, cc_im, pf_re, pf_im, s5d, w_glu, b_glu, ycat, seq):
    t = u5.shape[0]
    tb = 256
    npb = seq // tb

    def body(u_ref, bbr_ref, bbi_ref, ccr_ref, cci_ref, pfr_ref, pfi_ref, d_ref, wg_ref, bg_ref, ycat_hbm,
             sre_ref, sim_ref, ypre_ref, y5_ref, bur, bui, car, cai):
        del ycat_hbm

        @pl.when(pl.program_id(0) % npb == 0)
        def _():
            car[...] = jnp.zeros_like(car)
            cai[...] = jnp.zeros_like(cai)
        u = u_ref[...]
        ub = u.astype(BF16)
        for j in range(S5_BLOCKS):
            ch, st = slice(j * 128, (j + 1) * 128), slice(j * 512, (j + 1) * 512)
            bur[:, st] = jnp.dot(ub[:, ch], bbr_ref[j], preferred_element_type=F32)
            bui[:, st] = jnp.dot(ub[:, ch], bbi_ref[j], preferred_element_type=F32)
        _tile_scan(bur, bui, sre_ref, sim_ref, car, cai, pfr_ref, pfi_ref, tb // 8, reverse=False)
        cs_y = []
        for j in range(S5_BLOCKS):
            st = slice(j * 512, (j + 1) * 512)
            cs_y.append(_mm(sre_ref[:, st], ccr_ref[j]) - _mm(sim_ref[:, st], cci_ref[j]))
        ypre = jnp.concatenate(cs_y, axis=1) + u * d_ref[...]
        ypre_ref[...] = ypre
        yg = _gelu(ypre)
        y5_ref[...] = (yg * _sigmoid(_mm(yg, wg_ref[...]) + bg_ref[...])).astype(BF16)

    row = lambda w: pl.BlockSpec((tb, w), lambda i: (i, 0))
    full = lambda a: pl.BlockSpec(a.shape, lambda i: (0,) * a.ndim)
    return pl.pallas_call(
        body, name="s5_fwd", grid=(t // tb,),
        out_shape=(jax.ShapeDtypeStruct((t, S5_N), F32), jax.ShapeDtypeStruct((t, S5_N), F32),
                   jax.ShapeDtypeStruct((t, D_S5), F32), jax.ShapeDtypeStruct(ycat.shape, BF16)),
        in_specs=[row(D_S5), full(bb_re), full(bb_im), full(cc_re), full(cc_im), full(pf_re), full(pf_im),
                  full(s5d), full(w_glu), full(b_glu), ANY],
        out_specs=(row(S5_N), row(S5_N), row(D_S5), pl.BlockSpec((tb, D_S5), lambda i: (i, D_SSD // D_S5))),
        input_output_aliases={10: 3},
        scratch_shapes=[pltpu.VMEM((tb, S5_N), F32), pltpu.VMEM((tb, S5_N), F32),
                        pltpu.VMEM((8, S5_N), F32), pltpu.VMEM((8, S5_N), F32)],
        compiler_params=_params(48),
    )(u5, bb_re, bb_im, cc_re, cc_im, pf_re, pf_im, s5d, w_glu, b_glu, ycat)


def _layer_norm(r, g, b):
    mu = jnp.mean(r, axis=-1, keepdims=True)
    xc = r - mu
    rstd = lax.rsqrt(jnp.mean(xc * xc, axis=-1, keepdims=True) + EPS)
    xhat = xc * rstd
    return xhat * g + b, xhat, rstd


def _layer_norm_bwd(dy, xhat, rstd, g):
    dxhat = dy * g
    return rstd * (dxhat - jnp.mean(dxhat, axis=-1, keepdims=True)
                   - xhat * jnp.mean(dxhat * xhat, axis=-1, keepdims=True))


def _out_ln1(ycat, x2, mod3, w_out, ln1, seq):
    t = x2.shape[0]
    tb = 512
    npb = seq // tb

    def body(y_ref, x_ref, mod_ref, w_ref, ln_ref, mix_ref, x1_ref):
        m = mod_ref[0]
        mix = jnp.dot(y_ref[...], w_ref[...], preferred_element_type=F32)
        mix_ref[...] = mix
        r1 = ALPHA * x_ref[...] + (1.0 + m[2:3]) * mix
        x1_ref[...] = _layer_norm(r1, ln_ref[0:1], ln_ref[1:2])[0]

    row = lambda w: pl.BlockSpec((tb, w), lambda i: (i, 0))
    return pl.pallas_call(
        body, name="out_ln1", grid=(t // tb,),
        out_shape=(jax.ShapeDtypeStruct((t, D_MODEL), F32), jax.ShapeDtypeStruct((t, D_MODEL), F32)),
        in_specs=[row(D_SSD + D_S5), row(D_MODEL), pl.BlockSpec((1, N_MOD, D_MODEL), lambda i: (i // npb, 0, 0)),
                  pl.BlockSpec(w_out.shape, lambda i: (0, 0)), pl.BlockSpec(ln1.shape, lambda i: (0, 0))],
        out_specs=(row(D_MODEL), row(D_MODEL)), compiler_params=_params(48),
    )(ycat, x2, mod3, w_out, ln1)


def _mlp_fwd_bwd(x1, tgt, mod3, w1, w2, vec1, b1, seq):
    t = x1.shape[0]
    tb = 256
    npb = seq // tb
    n_fb, _, fb = w1.shape

    def body(x1_ref, tgt_ref, mod_ref, w1_hbm, w2_hbm, v_ref, b1_ref,
             dx1_ref, u2_ref, h_ref, dhp_ref, do_ref, gacc_ref, db1_ref, bacc_ref, w1_v, w2_v, sem1, sem2):
        i = pl.program_id(0)
        @pl.when(i == 0)
        def _():
            cps = [pltpu.make_async_copy(w1_hbm.at[k], w1_v.at[:, k * fb:(k + 1) * fb], sem1.at[k])
                   for k in range(n_fb)]
            for cp in cps:
                cp.start()
            for cp in cps:
                cp.wait()
        _load_once(w2_hbm, w2_v, sem2)

        @pl.when(i == 0)
        def _():
            gacc_ref[...] = jnp.zeros_like(gacc_ref)
            db1_ref[...] = jnp.zeros_like(db1_ref)

        @pl.when(i % npb == 0)
        def _():
            bacc_ref[...] = jnp.zeros_like(bacc_ref)

        m = mod_ref[0]
        sh2, sc2, g2 = m[3:4], m[4:5], m[5:6]
        x1v = x1_ref[...]
        u2 = (x1v * (1.0 + sc2) + sh2).astype(BF16)
        u2_ref[...] = u2
        hr = jnp.maximum(jnp.dot(u2, w1_v[...], preferred_element_type=F32) + b1_ref[...], 0.0)
        hb = (hr * hr).astype(BF16)
        h_ref[...] = hb
        o = jnp.dot(hb, w2_v[...], preferred_element_type=F32) + v_ref[0:1]
        r2 = ALPHA * x1v + (1.0 + g2) * o
        y, xhat, rstd = _layer_norm(r2, v_ref[1:2], v_ref[2:3])
        err = y - tgt_ref[...]
        dy = err * (1.0 / D_MODEL)
        dr2 = _layer_norm_bwd(dy, xhat, rstd, v_ref[1:2])
        do = (1.0 + g2) * dr2
        dob = do.astype(BF16)
        do_ref[...] = dob
        gacc_ref[0:1, :] += jnp.sum(dy * xhat, axis=0, keepdims=True)
        gacc_ref[1:2, :] += jnp.sum(dy, axis=0, keepdims=True)
        gacc_ref[2:3, :] += jnp.sum(do, axis=0, keepdims=True)
        gacc_ref[3:4, :] += jnp.sum(err * err, axis=0, keepdims=True)
        dhpre = lax.dot_general(dob, w2_v[...], NT, preferred_element_type=F32) * (2.0 * hr)
        dhpb = dhpre.astype(BF16)
        dhp_ref[...] = dhpb
        db1_ref[...] += jnp.sum(dhpre, axis=0, keepdims=True)
        du2 = lax.dot_general(dhpb, w1_v[...], NT, preferred_element_type=F32)
        dx1_ref[...] = ALPHA * dr2 + du2 * (1.0 + sc2)
        bacc_ref[0, 0:1, :] += jnp.sum(du2, axis=0, keepdims=True)
        bacc_ref[0, 1:2, :] += jnp.sum(du2 * x1v, axis=0, keepdims=True)
        bacc_ref[0, 2:3, :] += jnp.sum(dr2 * o, axis=0, keepdims=True)

    row = lambda w: pl.BlockSpec((tb, w), lambda i: (i, 0))
    return pl.pallas_call(
        body, name="mlp_fwd_bwd", grid=(t // tb,),
        out_shape=(jax.ShapeDtypeStruct((t, D_MODEL), F32), jax.ShapeDtypeStruct((t, D_MODEL), BF16),
                   jax.ShapeDtypeStruct((t, D_FF), BF16), jax.ShapeDtypeStruct((t, D_FF), BF16),
                   jax.ShapeDtypeStruct((t, D_MODEL), BF16), jax.ShapeDtypeStruct((8, D_MODEL), F32),
                   jax.ShapeDtypeStruct((1, D_FF), F32), jax.ShapeDtypeStruct((t // seq, 8, D_MODEL), F32)),
        in_specs=[row(D_MODEL), row(D_MODEL), pl.BlockSpec((1, N_MOD, D_MODEL), lambda i: (i // npb, 0, 0)), ANY, ANY,
                  pl.BlockSpec(vec1.shape, lambda i: (0, 0)), pl.BlockSpec(b1.shape, lambda i: (0, 0))],
        out_specs=(row(D_MODEL), row(D_MODEL), row(D_FF), row(D_FF), row(D_MODEL),
                   pl.BlockSpec((8, D_MODEL), lambda i: (0, 0)), pl.BlockSpec((1, D_FF), lambda i: (0, 0)),
                   pl.BlockSpec((1, 8, D_MODEL), lambda i: (i // npb, 0, 0))),
        scratch_shapes=[pltpu.VMEM((D_MODEL, n_fb * fb), BF16), pltpu.VMEM((D_FF, D_MODEL), BF16),
                        pltpu.SemaphoreType.DMA((n_fb,)), pltpu.SemaphoreType.DMA],
        compiler_params=_params(60),
    )(x1, tgt, mod3, w1, w2, vec1, b1)


def _ln1_out_bwd(dx1, x2, mix, mod3, w_out, ln1, seq):
    t = x2.shape[0]
    tb = 512
    npb = seq // tb

    def body(dx1_ref, x_ref, mix_ref, mod_ref, w_ref, ln_ref, dmix_ref, dxa_ref, dys_ref, dy5_ref, gacc_ref, bacc_ref):
        i = pl.program_id(0)

        @pl.when(i == 0)
        def _():
            gacc_ref[...] = jnp.zeros_like(gacc_ref)

        @pl.when(i % npb == 0)
        def _():
            bacc_ref[...] = jnp.zeros_like(bacc_ref)

        m = mod_ref[0]
        mix = mix_ref[...]
        r1 = ALPHA * x_ref[...] + (1.0 + m[2:3]) * mix
        _, xhat, rstd = _layer_norm(r1, ln_ref[0:1], ln_ref[1:2])
        dx1v = dx1_ref[...]
        dr1 = _layer_norm_bwd(dx1v, xhat, rstd, ln_ref[0:1])
        gacc_ref[0:1, :] += jnp.sum(dx1v * xhat, axis=0, keepdims=True)
        gacc_ref[1:2, :] += jnp.sum(dx1v, axis=0, keepdims=True)
        bacc_ref[0, 0:1, :] += jnp.sum(dr1 * mix, axis=0, keepdims=True)
        dmix = ((1.0 + m[2:3]) * dr1).astype(BF16)
        dmix_ref[...] = dmix
        dxa_ref[...] = ALPHA * dr1
        dys_ref[...] = lax.dot_general(dmix, w_ref[0:D_SSD, :], NT, preferred_element_type=F32)
        dy5_ref[...] = lax.dot_general(dmix, w_ref[D_SSD:, :], NT, preferred_element_type=F32)

    row = lambda w: pl.BlockSpec((tb, w), lambda i: (i, 0))
    return pl.pallas_call(
        body, name="ln1_out_bwd", grid=(t // tb,),
        out_shape=(jax.ShapeDtypeStruct((t, D_MODEL), BF16), jax.ShapeDtypeStruct((t, D_MODEL), F32),
                   jax.ShapeDtypeStruct((t, D_SSD), F32), jax.ShapeDtypeStruct((t, D_S5), F32),
                   jax.ShapeDtypeStruct((8, D_MODEL), F32), jax.ShapeDtypeStruct((t // seq, 8, D_MODEL), F32)),
        in_specs=[row(D_MODEL), row(D_MODEL), row(D_MODEL), pl.BlockSpec((1, N_MOD, D_MODEL), lambda i: (i // npb, 0, 0)),
                  pl.BlockSpec(w_out.shape, lambda i: (0, 0)), pl.BlockSpec(ln1.shape, lambda i: (0, 0))],
        out_specs=(row(D_MODEL), row(D_MODEL), row(D_SSD), row(D_S5), pl.BlockSpec((8, D_MODEL), lambda i: (0, 0)),
                   pl.BlockSpec((1, 8, D_MODEL), lambda i: (i // npb, 0, 0))),
        compiler_params=_params(48),
    )(dx1, x2, mix, mod3, w_out, ln1)


def _s5_bwd(dy5, ypre, u5, s_re, s_im, bb_re, bb_im, cc_re, cc_im, pr_re, pr_im, s5d, w_glu, b_glu, seq):
    t = u5.shape[0]
    tb = 256
    npb = seq // tb
    n_blocks = t // tb

    def blk(i):
        return (i // npb) * npb + (npb - 1 - i % npb)

    def body(dy_ref, ypre_ref, u_ref, sre_ref, sim_ref, hre_ref, him_ref, bbr_ref, bbi_ref, ccr_ref, cci_ref,
             prr_ref, pri_ref, d_ref, wg_ref, bg_ref,
             du_ref, vacc_ref, sacc_ref, dcc_ref, dbb_ref, dwg_ref, dsr, dsi, gr, gi, car, cai):
        i = pl.program_id(0)

        @pl.when(i == 0)
        def _():
            for acc in (vacc_ref, sacc_ref, dcc_ref, dbb_ref, dwg_ref):
                acc[...] = jnp.zeros_like(acc)

        @pl.when(i % npb == 0)
        def _():
            car[...] = jnp.zeros_like(car)
            cai[...] = jnp.zeros_like(cai)

        dy = dy_ref[...]
        ypre = ypre_ref[...]
        u = u_ref[...]
        ub = u.astype(BF16)
        yg = _gelu(ypre)
        sg = _sigmoid(_mm(yg, wg_ref[...]) + bg_ref[...])
        dq = dy * yg * sg * (1.0 - sg)
        dqb = dq.astype(BF16)
        dyg = dy * sg + lax.dot_general(dqb, wg_ref[...], NT, preferred_element_type=F32)
        dyp = dyg * _gelu_grad(ypre)
        dypb = dyp.astype(BF16)
        dwg_ref[...] += lax.dot_general(yg.astype(BF16), dqb, TN, preferred_element_type=F32)
        blocks = [(slice(j * 128, (j + 1) * 128), slice(j * 512, (j + 1) * 512)) for j in range(S5_BLOCKS)]
        for j, (ch, st) in enumerate(blocks):
            dsr[:, st] = lax.dot_general(dypb[:, ch], ccr_ref[j], NT, preferred_element_type=F32)
            dsi[:, st] = -lax.dot_general(dypb[:, ch], cci_ref[j], NT, preferred_element_type=F32)
        _tile_scan(dsr, dsi, gr, gi, car, cai, prr_ref, pri_ref, tb // 8, reverse=True)
        g_re, g_im = gr[...], gi[...]
        first_rows = (i % npb) == npb - 1
        hre = jnp.where(first_rows, 0.0, hre_ref[...])
        him = jnp.where(first_rows, 0.0, him_ref[...])
        s_re_v, s_im_v = sre_ref[...], sim_ref[...]
        sp_re = pltpu.roll(jnp.concatenate([hre, s_re_v], axis=0), 1, axis=0)[8:8 + tb]
        sp_im = pltpu.roll(jnp.concatenate([him, s_im_v], axis=0), 1, axis=0)[8:8 + tb]
        vacc_ref[0:1, :] += jnp.sum(g_re * sp_re + g_im * sp_im, axis=0, keepdims=True)
        vacc_ref[1:2, :] += jnp.sum(g_im * sp_re - g_re * sp_im, axis=0, keepdims=True)
        grb, gib = g_re.astype(BF16), g_im.astype(BF16)
        srb, sib = s_re_v.astype(BF16), s_im_v.astype(BF16)
        du_cols = []
        for j, (ch, st) in enumerate(blocks):
            dcc_ref[j] += lax.dot_general(srb[:, st], dypb[:, ch], TN, preferred_element_type=F32)
            dcc_ref[S5_BLOCKS + j] -= lax.dot_general(sib[:, st], dypb[:, ch], TN, preferred_element_type=F32)
            dbb_ref[j] += lax.dot_general(ub[:, ch], grb[:, st], TN, preferred_element_type=F32)
            dbb_ref[S5_BLOCKS + j] += lax.dot_general(ub[:, ch], gib[:, st], TN, preferred_element_type=F32)
            du_cols.append(lax.dot_general(grb[:, st], bbr_ref[j], NT, preferred_element_type=F32)
                           + lax.dot_general(gib[:, st], bbi_ref[j], NT, preferred_element_type=F32))
        du_ref[...] = jnp.concatenate(du_cols, axis=1) + dyp * d_ref[...]
        sacc_ref[0:1, :] += jnp.sum(dyp * u, axis=0, keepdims=True)
        sacc_ref[1:2, :] += jnp.sum(dq, axis=0, keepdims=True)

    row = lambda w: pl.BlockSpec((tb, w), lambda i: (blk(i), 0))
    halo = pl.BlockSpec((8, S5_N), lambda i: (jnp.maximum(blk(i) * (tb // 8) - 1, 0), 0))
    full = lambda a: pl.BlockSpec(a.shape, lambda i: (0,) * a.ndim)
    acc = lambda s: pl.BlockSpec(s, lambda i: (0,) * len(s))
    acc_shapes = [(8, S5_N), (8, D_S5), (2 * S5_BLOCKS, 512, 128), (2 * S5_BLOCKS, 128, 512), (D_S5, D_S5)]
    return pl.pallas_call(
        body, name="s5_bwd", grid=(n_blocks,),
        out_shape=(jax.ShapeDtypeStruct((t, D_S5), F32),) + tuple(jax.ShapeDtypeStruct(s, F32) for s in acc_shapes),
        in_specs=[row(D_S5), row(D_S5), row(D_S5), row(S5_N), row(S5_N), halo, halo, full(bb_re), full(bb_im),
                  full(cc_re), full(cc_im), full(pr_re), full(pr_im), full(s5d), full(w_glu), full(b_glu)],
        out_specs=(row(D_S5),) + tuple(acc(s) for s in acc_shapes),
        scratch_shapes=[pltpu.VMEM((tb, S5_N), F32), pltpu.VMEM((tb, S5_N), F32), pltpu.VMEM((tb, S5_N), F32),
                        pltpu.VMEM((tb, S5_N), F32), pltpu.VMEM((8, S5_N), F32), pltpu.VMEM((8, S5_N), F32)],
        compiler_params=_params(56),
    )(dy5, ypre, u5, s_re, s_im, s_re, s_im, bb_re, bb_im, cc_re, cc_im, pr_re, pr_im, s5d, w_glu, b_glu)


def _ssd_bwd(dyssd, yraw, z, xbc, dt_raw, hprev, par, dsk, normw, seq):
    t = xbc.shape[0]
    nc = seq // CHUNK
    n_chunks = t // CHUNK
    fold = _head_fold()

    def blk(i):
        return (i // nc) * nc + (nc - 1 - i % nc)

    def body(dy_ref, yraw_ref, z_ref, xbc_ref, dt_ref, hprev_ref, par_ref, dsk_ref, nw_ref, fold_ref,
             dxbc_ref, dz_ref, ddt_ref, dpar_ref, cacc_ref, dh_ref, dyr_ref):
        i = pl.program_id(0)

        @pl.when(i == 0)
        def _():
            dpar_ref[...] = jnp.zeros_like(dpar_ref)
            cacc_ref[...] = jnp.zeros_like(cacc_ref)

        @pl.when(i % nc == 0)
        def _():
            dh_ref[...] = jnp.zeros_like(dh_ref)

        zz = z_ref[...]
        sz = _sigmoid(zz)
        silu_z = zz * sz
        yraw = yraw_ref[...]
        for g in range(N_GROUPS):
            sl = slice(g * GW, (g + 1) * GW)
            v = yraw[:, sl] * silu_z[:, sl]
            r = lax.rsqrt(jnp.mean(v * v, axis=-1, keepdims=True) + EPS)
            dyg = dy_ref[:, sl]
            cacc_ref[1:2, sl] += jnp.sum(dyg * v * r, axis=0, keepdims=True)
            dyw = dyg * nw_ref[:, sl]
            dv = r * dyw - v * (r * r * r) * jnp.mean(dyw * v, axis=-1, keepdims=True)
            dyr_ref[:, sl] = dv * silu_z[:, sl]
            dz_ref[:, sl] = dv * yraw[:, sl] * (sz[:, sl] * (1.0 + zz[:, sl] * (1.0 - sz[:, sl])))

        dt, a, cs, cst, causal, tri, dt_c, ecs_c, w_c, pair_cols = _ssd_prep(dt_ref[...], par_ref[...])
        cs_last = cs[CHUNK - 1:CHUNK, :]
        causal2 = jnp.concatenate([causal, causal], axis=1)
        lane = lax.broadcasted_iota(jnp.int32, (CHUNK, 128), 1)
        left = lane < HEADDIM
        lane1 = lax.broadcasted_iota(jnp.int32, (1, 128), 1)
        x = xbc_ref[:, 0:D_SSD]
        xdt = x * dt_c
        dyr = dyr_ref[...]
        dyrb = dyr.astype(BF16)
        cacc_ref[0:1, :] += jnp.sum(dyr * x, axis=0, keepdims=True)
        dlast = jnp.zeros((1, 128), F32)
        dxdt_cols, diag_all, dww_cols = [], [], []
        for g in range(N_GROUPS):
            gs = slice(g * GW, (g + 1) * GW)
            b_sl = slice(D_SSD + g * N_STATE, D_SSD + (g + 1) * N_STATE)
            c_sl = slice(D_SSD + (N_GROUPS + g) * N_STATE, D_SSD + (N_GROUPS + g + 1) * N_STATE)
            bg = xbc_ref[:, b_sl].astype(BF16)
            cg = xbc_ref[:, c_sl].astype(BF16)
            scores = lax.dot_general(cg, bg, NT, preferred_element_type=F32)
            scores2 = jnp.concatenate([scores, scores], axis=1)
            hg = hprev_ref[0, gs, :]
            hgb = hg.astype(BF16)
            dhg = dh_ref[gs, :]
            dhgb = dhg.astype(BF16)
            q_all = lax.dot_general(bg, dhgb, NT, preferred_element_type=F32)
            dscores = jnp.zeros((CHUNK, CHUNK), F32)
            diag_cols = []
            for q in range(GW // 128):
                pair = g * (GW // 128) + q
                ps = slice(pair * 128, (pair + 1) * 128)
                decay = _pair_decay(pair_cols[pair], cst, pair, causal2)
                mcat = (scores2 * decay).astype(BF16)
                dyp = dyrb[:, ps]
                dm = lax.dot_general(dyp, _stack_heads(xdt[:, ps], left), NT, preferred_element_type=F32)
                dmd = dm * decay
                dscores = dscores + dmd[:, 0:CHUNK] + dmd[:, CHUNK:]
                rr = lax.dot_general(mcat, dyp, TN, preferred_element_type=F32)
                diag_cols.append(jnp.where(left, rr[0:CHUNK], rr[CHUNK:]))
            wq = w_c[:, gs] * q_all
            diag_g = jnp.concatenate(diag_cols, axis=1)
            diag_all.append(diag_g)
            dxdt_cols.append(diag_g + wq)
            dww_cols.append(wq * xdt[:, gs])
            dp = (ecs_c[:, gs] * dyr[:, gs]).astype(BF16)
            amat = (w_c[:, gs] * xdt[:, gs]).astype(BF16)
            dsb = dscores.astype(BF16)
            dxbc_ref[:, c_sl] = (jnp.dot(dsb, bg, preferred_element_type=F32)
                                 + jnp.dot(dp, hgb, preferred_element_type=F32))
            dxbc_ref[:, b_sl] = (lax.dot_general(dsb, cg, TN, preferred_element_type=F32)
                                 + jnp.dot(amat, dhgb, preferred_element_type=F32))
            dh_in = lax.dot_general(dp, cg, TN, preferred_element_type=F32)
            for j in range(HPG):
                hh = g * HPG + j
                js = slice(j * HEADDIM, (j + 1) * HEADDIM)
                ecl = jnp.exp(cs_last[:, hh:hh + 1])
                dlast = dlast + jnp.where(lane1 == hh, ecl * jnp.sum(dhg[js, :] * hg[js, :]), 0.0)
                dh_ref[g * GW + j * HEADDIM:g * GW + (j + 1) * HEADDIM, :] = ecl * dhg[js, :] + dh_in[js, :]
        dxdt = jnp.concatenate(dxdt_cols, axis=1)
        dxbc_ref[:, 0:D_SSD] = dxdt * dt_c + dyr * dsk_ref[...]
        dww = _dot3(jnp.concatenate(dww_cols, axis=1), fold_ref[...])
        dcs = (_dot3(dyrb.astype(F32) * (yraw - x * dsk_ref[...]), fold_ref[...])
               - _dot3(xdt.astype(BF16).astype(F32) * jnp.concatenate(diag_all, axis=1), fold_ref[...]) - dww)
        rowid = lax.broadcasted_iota(jnp.int32, (CHUNK, 128), 0)
        dcs = dcs + jnp.where(rowid == CHUNK - 1, jnp.sum(dww, axis=0, keepdims=True) + dlast, 0.0)
        dadt = _dot3_left(tri, dcs, TN)
        ddt = _dot3(dxdt * x, fold_ref[...]) + dadt * a
        da = jnp.sum(dadt * dt, axis=0, keepdims=True)
        ddt_raw = ddt * _sigmoid(dt_ref[...] + par_ref[0:1])
        ddt_raw = jnp.where(lane < N_HEADS, ddt_raw, 0.0)
        ddt_ref[...] = ddt_raw
        dpar_ref[0:1, :] += jnp.sum(ddt_raw, axis=0, keepdims=True)
        dpar_ref[1:2, :] += jnp.where(lane1 < N_HEADS, da * a, 0.0)

    row = lambda w: pl.BlockSpec((CHUNK, w), lambda i: (blk(i), 0))
    full = lambda s: pl.BlockSpec(s, lambda i: (0,) * len(s))
    return pl.pallas_call(
        body, name="ssd_bwd", grid=(n_chunks,),
        out_shape=(jax.ShapeDtypeStruct((t, D_XBC), F32), jax.ShapeDtypeStruct((t, D_SSD), F32),
                   jax.ShapeDtypeStruct((t, DT_PAD), F32), jax.ShapeDtypeStruct((8, 128), F32),
                   jax.ShapeDtypeStruct((8, D_SSD), F32)),
        in_specs=[row(D_SSD), row(D_SSD), row(D_SSD), row(D_XBC), row(DT_PAD),
                  pl.BlockSpec((1, D_SSD, N_STATE), lambda i: (blk(i), 0, 0)),
                  full((8, 128)), full((1, D_SSD)), full((1, D_SSD)), full(fold.shape)],
        out_specs=(row(D_XBC), row(D_SSD), row(DT_PAD), full((8, 128)), full((8, D_SSD))),
        scratch_shapes=[pltpu.VMEM((D_SSD, N_STATE), F32), pltpu.VMEM((CHUNK, D_SSD), F32)],
        compiler_params=_params(48),
    )(dyssd, yraw, z, xbc, dt_raw, hprev, par, dsk, normw, fold)


def _conv_bwd(dxbc, xbc_pre, conv_w, conv_b, seq):
    t = xbc_pre.shape[0]
    tb = 512
    npb = seq // tb
    cw = 640

    def body(d_ref, cur_ref, halo_ref, w_ref, b_ref, o_ref, acc_ref, win):
        i = pl.program_id(1)

        @pl.when(i == 0)
        def _():
            acc_ref[...] = jnp.zeros_like(acc_ref)

        first = (i % npb) == 0
        win[0:8, :] = jnp.where(first, 0.0, halo_ref[...])
        win[8:8 + tb, :] = cur_ref[...]
        pre, shifted = _conv_taps(win, w_ref[...], tb, slice(None))
        pre = pre + b_ref[...]
        sg = _sigmoid(pre)
        dpre = d_ref[...] * (sg * (1.0 + pre * (1.0 - sg)))
        o_ref[...] = dpre
        for j in range(4):
            acc_ref[3 - j:4 - j, :] += jnp.sum(dpre * shifted[j], axis=0, keepdims=True)
        acc_ref[4:5, :] += jnp.sum(dpre, axis=0, keepdims=True)

    return pl.pallas_call(
        body, name="conv_bwd", grid=(D_XBC // cw, t // tb),
        out_shape=(jax.ShapeDtypeStruct((t, D_XBC), F32), jax.ShapeDtypeStruct((8, D_XBC), F32)),
        in_specs=[pl.BlockSpec((tb, cw), lambda j, i: (i, j)), pl.BlockSpec((tb, cw), lambda j, i: (i, j)),
                  pl.BlockSpec((8, cw), lambda j, i: (jnp.maximum(i * (tb // 8) - 1, 0), j)),
                  pl.BlockSpec((4, cw), lambda j, i: (0, j)), pl.BlockSpec((1, cw), lambda j, i: (0, j))],
        out_specs=(pl.BlockSpec((tb, cw), lambda j, i: (i, j)), pl.BlockSpec((8, cw), lambda j, i: (0, j))),
        scratch_shapes=[pltpu.VMEM((tb + 8, cw), F32)],
        compiler_params=_params(32),
    )(dxbc, xbc_pre, xbc_pre, conv_w, conv_b)


def _proj_bwd(dz, dpre, ddt, du5, x2, dxa, mod3, conv_w, w_in_pad, seq):
    t = x2.shape[0]
    tb = 512
    npb = seq // tb
    n_blocks = t // tb

    def body(dz_ref, dp_ref, nxt_ref, ddt_ref, du5_ref, x_ref, dxa_ref, mod_ref, cw_ref, w_hbm,
             gx_ref, u_ref, dxp_ref, bacc_ref, w_vmem, sem):
        i = pl.program_id(0)
        _load_once(w_hbm, w_vmem, sem)

        @pl.when(i % npb == 0)
        def _():
            bacc_ref[...] = jnp.zeros_like(bacc_ref)

        last = (i % npb) == npb - 1
        nxt = jnp.where(last, 0.0, nxt_ref[...])
        cur = dp_ref[...]
        xx = jnp.concatenate([cur, nxt], axis=0)
        w = cw_ref[...]
        dxp = w[3:4] * cur
        for j in (1, 2, 3):
            dxp = dxp + w[3 - j:4 - j] * pltpu.roll(xx, tb + 8 - j, axis=0)[0:tb]
        dxpb = dxp.astype(BF16)
        dxp_ref[...] = dxpb
        o1, o2, o3 = D_SSD, D_SSD + D_XBC, D_SSD + D_XBC + DT_PAD
        du = (lax.dot_general(dz_ref[...].astype(BF16), w_vmem[:, 0:o1], NT, preferred_element_type=F32)
              + lax.dot_general(dxpb, w_vmem[:, o1:o2], NT, preferred_element_type=F32)
              + lax.dot_general(ddt_ref[...].astype(BF16), w_vmem[:, o2:o3], NT, preferred_element_type=F32)
              + lax.dot_general(du5_ref[...].astype(BF16), w_vmem[:, o3:], NT, preferred_element_type=F32))
        m = mod_ref[0]
        xv = x_ref[...]
        u_ref[...] = (xv * (1.0 + m[1:2]) + m[0:1]).astype(BF16)
        gx_ref[...] = dxa_ref[...] + du * (1.0 + m[1:2])
        bacc_ref[0, 0:1, :] += jnp.sum(du, axis=0, keepdims=True)
        bacc_ref[0, 1:2, :] += jnp.sum(du * xv, axis=0, keepdims=True)

    row = lambda w: pl.BlockSpec((tb, w), lambda i: (i, 0))
    nxt_rows = pl.BlockSpec((8, D_XBC), lambda i: (jnp.minimum((i + 1) * (tb // 8), t // 8 - 1), 0))
    return pl.pallas_call(
        body, name="proj_bwd", grid=(n_blocks,),
        out_shape=(jax.ShapeDtypeStruct((t, D_MODEL), F32), jax.ShapeDtypeStruct((t, D_MODEL), BF16),
                   jax.ShapeDtypeStruct((t, D_XBC), BF16), jax.ShapeDtypeStruct((t // seq, 8, D_MODEL), F32)),
        in_specs=[row(D_SSD), row(D_XBC), nxt_rows, row(DT_PAD), row(D_S5), row(D_MODEL), row(D_MODEL),
                  pl.BlockSpec((1, N_MOD, D_MODEL), lambda i: (i // npb, 0, 0)),
                  pl.BlockSpec((4, D_XBC), lambda i: (0, 0)), ANY],
        out_specs=(row(D_MODEL), row(D_MODEL), row(D_XBC), pl.BlockSpec((1, 8, D_MODEL), lambda i: (i // npb, 0, 0))),
        scratch_shapes=[pltpu.VMEM((D_MODEL, D_INP), BF16), pltpu.SemaphoreType.DMA],
        compiler_params=_params(60),
    )(dz, dpre, dpre, ddt, du5, x2, dxa, mod3, conv_w, w_in_pad)


def _pad_rows(a, mult):
    r = a.shape[0]
    pad = (-r) % mult
    return a if pad == 0 else jnp.concatenate([a, jnp.zeros((pad,) + a.shape[1:], a.dtype)], axis=0)


_SMALL = ["conv_w", "conv_b", "dt_bias", "a_log", "d_ssd", "norm_w", "s5_a_re", "s5_a_im", "s5_log_dt", "s5_b_re",
          "s5_b_im", "s5_c_re", "s5_c_im", "s5_d", "b_glu", "ln1_g", "ln1_b", "b1", "b2", "ln2_g", "ln2_b"]


def _tile_rows(size):
    return 8 * (-(-size // 1024))


def _pack_small(d):
    parts = []
    for n in _SMALL:
        flat = d[n].reshape(-1).astype(F32)
        rows = _tile_rows(flat.shape[0])
        pad = rows * 128 - flat.shape[0]
        if pad:
            flat = jnp.concatenate([flat, jnp.zeros((pad,), F32)])
        parts.append(flat.reshape(rows, 128))
    return jnp.concatenate(parts, axis=0)


def _unpack_small(p, shapes):
    out, off = {}, 0
    for n in _SMALL:
        size = math.prod(shapes[n])
        rows = _tile_rows(size)
        out[n] = p[off:off + rows].reshape(-1)[:size].reshape(shapes[n])
        off += rows
    return out


def kernel(x, c, w_ada, b_ada, w_in, conv_w, conv_b, dt_bias, a_log, d_ssd, norm_w, s5_a_re, s5_a_im, s5_log_dt, s5_b_re, s5_b_im, s5_c_re, s5_c_im, s5_d, w_glu, b_glu, w_out, ln1_g, ln1_b, w1, b1, w2, b2, ln2_g, ln2_b, loss_target, m_w_ada, m_b_ada, m_w_in, m_conv_w, m_conv_b, m_dt_bias, m_a_log, m_d_ssd, m_norm_w, m_s5_a_re, m_s5_a_im, m_s5_log_dt, m_s5_b_re, m_s5_b_im, m_s5_c_re, m_s5_c_im, m_s5_d, m_w_glu, m_b_glu, m_w_out, m_ln1_g, m_ln1_b, m_w1, m_b1, m_w2, m_b2, m_ln2_g, m_ln2_b, v_w_ada, v_b_ada, v_w_in, v_conv_w, v_conv_b, v_dt_bias, v_a_log, v_d_ssd, v_norm_w, v_s5_a_re, v_s5_a_im, v_s5_log_dt, v_s5_b_re, v_s5_b_im, v_s5_c_re, v_s5_c_im, v_s5_d, v_w_glu, v_b_glu, v_w_out, v_ln1_g, v_ln1_b, v_w1, v_b1, v_w2, v_b2, v_ln2_g, v_ln2_b):
    weights = dict(w_ada=w_ada, b_ada=b_ada, w_in=w_in, conv_w=conv_w, conv_b=conv_b, dt_bias=dt_bias, a_log=a_log,
                   d_ssd=d_ssd, norm_w=norm_w, s5_a_re=s5_a_re, s5_a_im=s5_a_im, s5_log_dt=s5_log_dt, s5_b_re=s5_b_re,
                   s5_b_im=s5_b_im, s5_c_re=s5_c_re, s5_c_im=s5_c_im, s5_d=s5_d, w_glu=w_glu, b_glu=b_glu, w_out=w_out,
                   ln1_g=ln1_g, ln1_b=ln1_b, w1=w1, b1=b1, w2=w2, b2=b2, ln2_g=ln2_g, ln2_b=ln2_b)
    mom = dict(w_ada=m_w_ada, b_ada=m_b_ada, w_in=m_w_in, conv_w=m_conv_w, conv_b=m_conv_b, dt_bias=m_dt_bias,
               a_log=m_a_log, d_ssd=m_d_ssd, norm_w=m_norm_w, s5_a_re=m_s5_a_re, s5_a_im=m_s5_a_im,
               s5_log_dt=m_s5_log_dt, s5_b_re=m_s5_b_re, s5_b_im=m_s5_b_im, s5_c_re=m_s5_c_re, s5_c_im=m_s5_c_im,
               s5_d=m_s5_d, w_glu=m_w_glu, b_glu=m_b_glu, w_out=m_w_out, ln1_g=m_ln1_g, ln1_b=m_ln1_b, w1=m_w1, b1=m_b1,
               w2=m_w2, b2=m_b2, ln2_g=m_ln2_g, ln2_b=m_ln2_b)
    var = dict(w_ada=v_w_ada, b_ada=v_b_ada, w_in=v_w_in, conv_w=v_conv_w, conv_b=v_conv_b, dt_bias=v_dt_bias,
               a_log=v_a_log, d_ssd=v_d_ssd, norm_w=v_norm_w, s5_a_re=v_s5_a_re, s5_a_im=v_s5_a_im,
               s5_log_dt=v_s5_log_dt, s5_b_re=v_s5_b_re, s5_b_im=v_s5_b_im, s5_c_re=v_s5_c_re, s5_c_im=v_s5_c_im,
               s5_d=v_s5_d, w_glu=v_w_glu, b_glu=v_b_glu, w_out=v_w_out, ln1_g=v_ln1_g, ln1_b=v_ln1_b, w1=v_w1, b1=v_b1,
               w2=v_w2, b2=v_b2, ln2_g=v_ln2_g, ln2_b=v_ln2_b)
    names = list(weights)
    shapes = {n: weights[n].shape for n in names}

    nb, seq, _ = x.shape
    t = nb * seq
    dev = _dev_index()
    x2 = x.reshape(t, D_MODEL)
    tgt2 = loss_target.reshape(t, D_MODEL)

    cw_cols = conv_w.shape[2]
    small_in = jnp.concatenate([c.reshape(-1), conv_w.reshape(-1)]).reshape(-1, 128)
    big_names = ["w_in", "w_out", "w1", "w2", "w_glu"]
    shard_bf16 = {n: weights[n][0].astype(BF16) for n in big_names}
    first = _all_gather([small_in, shard_bf16["w_in"], shard_bf16["w_glu"]], "gather_first")
    small_all = first[0].reshape(N_DEV, -1)
    c_all = small_all[:, :nb * D_MODEL].reshape(N_DEV * nb, D_MODEL)
    conv_w_full = small_all[:, nb * D_MODEL:].reshape(N_DEV, 4, cw_cols).transpose(1, 0, 2).reshape(4, D_XBC)

    w_in_f = first[1].transpose(1, 0, 2).reshape(D_MODEL, D_IN)
    w_in_pad = jnp.concatenate(
        [w_in_f[:, :D_SSD + D_XBC], w_in_f[:, D_SSD + D_XBC:D_SSD + D_XBC + N_HEADS],
         jnp.zeros((D_MODEL, DT_PAD - N_HEADS), BF16), w_in_f[:, D_SSD + D_XBC + N_HEADS:]], axis=1)
    w_glu_f = first[2].reshape(D_S5, D_S5)
    late_names = ["w_out", "w1", "w2"]

    ada_cols = w_ada.shape[2]
    b_cols = lax.dynamic_slice_in_dim(b_ada, dev * ada_cols, ada_cols, axis=1)
    mod_cols = _mod_fwd(c_all, w_ada[0], b_cols)
    mod_all = _all_gather([mod_cols], "gather_mod")[0]
    mod_mine = lax.dynamic_slice_in_dim(mod_all, dev * nb, nb, axis=1)
    mod3 = mod_mine.transpose(1, 0, 2).reshape(nb, N_MOD, D_MODEL)
    late_in, mod3 = lax.optimization_barrier(([shard_bf16[n] for n in late_names], mod3))
    late_sems = _gather_start(late_in, "gather_late_start")
    mod3 = mod3 + late_sems[4][0, 0]

    def pad_lanes(v, n):
        return jnp.concatenate([v, jnp.zeros((v.shape[0], n - v.shape[1]), F32)], axis=1)

    par = _pad_rows(jnp.concatenate([pad_lanes(dt_bias, 128), pad_lanes(a_log, 128)], axis=0), 8)
    dsk = jnp.repeat(d_ssd[0], HEADDIM).reshape(1, D_SSD)
    ar = s5_a_re.reshape(1, S5_N)
    ai = s5_a_im.reshape(1, S5_N)
    ldt = jnp.repeat(s5_log_dt[0], S5_P).reshape(1, S5_N)
    br_t = s5_b_re[0].transpose(2, 0, 1).reshape(S5_CH, S5_N)
    bi_t = s5_b_im[0].transpose(2, 0, 1).reshape(S5_CH, S5_N)
    bb_re_t, bb_im_t, pf_re, pf_im, pr_re, pr_im = _s5_params_fwd(ar, ai, ldt, br_t, bi_t)
    gpb = S5_GROUPS // S5_BLOCKS
    mask_b = (jnp.arange(128)[:, None] // S5_CH) == (jnp.arange(512)[None, :] // S5_P)

    def dense_b(bt_):
        blocks = bt_.reshape(S5_CH, S5_BLOCKS, 512).transpose(1, 0, 2)
        return jnp.where(mask_b, jnp.tile(blocks, (1, gpb, 1)), 0.0).astype(BF16)

    def dense_c(cc):
        blocks = cc[0].transpose(0, 2, 1).reshape(S5_BLOCKS, 512, S5_CH)
        return jnp.where(mask_b.T, jnp.tile(blocks, (1, 1, gpb)), 0.0).astype(BF16)

    bb_re, bb_im = dense_b(bb_re_t), dense_b(bb_im_t)
    cc_re, cc_im = dense_c(s5_c_re), dense_c(s5_c_im)
    s5d = s5_d.reshape(1, D_S5)
    ln1 = jnp.concatenate([ln1_g, ln1_b], axis=0)
    vec1 = _pad_rows(jnp.concatenate([b2, ln2_g, ln2_b], axis=0), 8)

    z, xbc_pre, xbc, dt_raw, u5 = _proj_conv_fwd(x2, mod3, w_in_pad, conv_w_full, conv_b, seq)
    yraw, ycat, hprev = _ssd_fwd(xbc, z, dt_raw, par, dsk, norm_w, seq)
    s_re, s_im, ypre, ycat = _s5_fwd(u5, bb_re, bb_im, cc_re, cc_im, pf_re, pf_im, s5d, w_glu_f, b_glu, ycat, seq)
    sent, landed = _gather_wait(late_sems[0], late_sems[1], late_sems[2], late_sems[3], ycat, "gather_late_wait")
    gathered = {n: lax.dynamic_update_index_in_dim(l, x, dev, 0) for n, x, l in zip(late_names, sent, landed)}
    w_out_f = gathered["w_out"].reshape(2 * D_MODEL, D_MODEL)
    w1_blocks = gathered["w1"]
    w2_f = gathered["w2"].reshape(D_FF, D_MODEL)
    mix, x1 = _out_ln1(ycat, x2, mod3, w_out_f, ln1, seq)

    dx1, u2b, hb, dhpb, dob, gacc2, db1, bacc2 = _mlp_fwd_bwd(x1, tgt2, mod3, w1_blocks, w2_f, vec1, b1, seq)
    loss = lax.psum(0.5 / D_MODEL * jnp.sum(gacc2[3]), ("x", "y", "c"))

    dmixb, dxa, dyssd, dy5, gacc1, bacc1 = _ln1_out_bwd(dx1, x2, mix, mod3, w_out_f, ln1, seq)

    g_w2 = _atb(hb, dob, "gw2")
    g_w1 = _atb(u2b, dhpb, "gw1")
    g_wout = _atb(ycat, dmixb, "gwout")
    core = lax.axis_index("c").astype(jnp.int32).reshape(1)
    chip = 2 * lax.axis_index("x") + lax.axis_index("y")

    def chip_sums_of(names, grads, tag):
        by_dest = [g if g.ndim == 2 else g.reshape((4, 2) + g.shape[1:]) for g in grads]
        from_sibling = _sibling_swap(by_dest, "rs_swap_" + tag)
        return [_add_halves(g, r, core, "rs_add_" + n) for g, r, n in zip(by_dest, from_sibling, names)]

    early_names = ["w_out", "w1", "w2"]
    early_sums = chip_sums_of(early_names, [g_wout.reshape((N_DEV,) + w_out.shape[1:]), g_w1,
                                            g_w2.reshape((N_DEV,) + w2.shape[1:])], "early")
    early = _all_to_all_start(early_sums, "rs_early_start")
    s5d_after = s5d + early[4][0, 0]

    du5, vacc, sacc, d_cc, d_bb, g_wglu = _s5_bwd(dy5, ypre, u5, s_re, s_im, bb_re, bb_im, cc_re, cc_im,
                                                  pr_re, pr_im, s5d_after, w_glu_f, b_glu, seq)
    dxbc, dz, ddt, dpar, cacc = _ssd_bwd(dyssd, yraw, z, xbc, dt_raw, hprev, par, dsk, norm_w, seq)
    dpre, conv_acc = _conv_bwd(dxbc, xbc_pre, conv_w_full, conv_b, seq)
    grad_x2, ub, dxpb, bacc0 = _proj_bwd(dz, dpre, ddt, du5, x2, dxa, mod3, conv_w_full, w_in_pad, seq)

    g_win = jnp.concatenate([_atb(ub, dz, "gwin_z"), _atb(ub, dxpb, "gwin_xbc"),
                             _atb(ub, ddt, "gwin_dt")[:, :N_HEADS], _atb(ub, du5, "gwin_s5")], axis=1)

    def diag_b(dd):
        kept = jnp.where(mask_b, dd, 0.0).reshape(S5_BLOCKS, gpb, S5_CH, 512).sum(1)
        return kept.transpose(1, 0, 2).reshape(S5_CH, S5_N)

    def diag_c(dd):
        kept = jnp.where(mask_b.T, dd, 0.0).reshape(S5_BLOCKS, 512, gpb, S5_CH).sum(2)
        return kept.reshape(S5_GROUPS, S5_P, S5_CH).transpose(0, 2, 1)

    g_ar, g_ai, g_ldt, g_br_t, g_bi_t = _s5_params_bwd(ar, ai, ldt, br_t, bi_t, vacc[0:1], vacc[1:2],
                                                      diag_b(d_bb[:S5_BLOCKS]), diag_b(d_bb[S5_BLOCKS:]))

    def from_t(gt):
        return gt.reshape(S5_CH, S5_GROUPS, S5_P).transpose(1, 2, 0)

    small_g = dict(
        conv_w=conv_acc[0:4], conv_b=conv_acc[4:5], dt_bias=dpar[0:1, :N_HEADS], a_log=dpar[1:2, :N_HEADS],
        d_ssd=cacc[0].reshape(N_HEADS, HEADDIM).sum(1), norm_w=cacc[1:2],
        s5_a_re=g_ar, s5_a_im=g_ai, s5_log_dt=g_ldt[:, :S5_GROUPS], s5_b_re=from_t(g_br_t), s5_b_im=from_t(g_bi_t),
        s5_c_re=diag_c(d_cc[:S5_BLOCKS]), s5_c_im=diag_c(d_cc[S5_BLOCKS:]), s5_d=sacc[0:1], b_glu=sacc[1:2],
        ln1_g=gacc1[0:1], ln1_b=gacc1[1:2], b1=db1, b2=gacc2[2:3], ln2_g=gacc2[0:1], ln2_b=gacc2[1:2])

    dmod = jnp.concatenate([bacc0[:, 0], bacc0[:, 1], bacc1[:, 0], bacc2[:, 0], bacc2[:, 1], bacc2[:, 2]], axis=1)
    dmod_all = _all_gather([dmod], "gather_dmod")[0].reshape(N_DEV * nb, N_MOD * D_MODEL)
    dmod_cols = lax.dynamic_slice_in_dim(dmod_all, dev * ada_cols, ada_cols, axis=1)
    g_wada, g_bada = _mod_bwd(c_all, dmod_cols, dmod_all)

    in_cols = w_in.shape[2]
    late_rs = ["w_in", "w_glu"]
    late_sums = chip_sums_of(late_rs, [g_win.reshape(D_MODEL, N_DEV, in_cols).transpose(1, 0, 2),
                                       g_wglu.reshape((N_DEV,) + w_glu.shape[1:])], "late")
    parts = dict(zip(late_rs, _chip_all_to_all(late_sums, "rs_late_all_to_all")))
    sent, landed = _all_to_all_wait(early[0], early[1], early[2], early[3], parts["w_in"], "rs_early_wait")
    for n, l, h in zip(early_names, landed, sent):
        parts[n] = lax.dynamic_update_index_in_dim(l, lax.dynamic_index_in_dim(h, chip, 0, keepdims=False), chip, 0)

    res = {k: {} for k in "gdmv"}
    for n in big_names:
        outs = _adamw(parts[n], weights[n][0], mom[n][0], var[n][0], "adamw_" + n)
        for k, a in zip("gdmv", outs):
            res[k][n] = a[None]

    ag, ad, am, av = _adamw(g_wada[None], w_ada[0], m_w_ada[0], v_w_ada[0], "adamw_w_ada")
    for k, a in (("g", ag), ("d", ad), ("m", am), ("v", av)):
        res[k]["w_ada"] = a[None]
    bg_, bd_, bm_, bv_ = _adamw(g_bada.reshape(1, -1, 128), b_ada.reshape(-1, 128), m_b_ada.reshape(-1, 128),
                                v_b_ada.reshape(-1, 128), "adamw_b_ada")
    for k, a in (("g", bg_), ("d", bd_), ("m", bm_), ("v", bv_)):
        res[k]["b_ada"] = a.reshape(shapes["b_ada"])

    small_shapes = dict(shapes)
    small_shapes["conv_w"] = (1, 4, D_XBC)
    small_parts = _all_gather([_pack_small(small_g)], "gather_small_grads")[0]
    rep = {n: (jnp.zeros((1, 4, D_XBC), F32) if n == "conv_w" else weights[n]) for n in _SMALL}
    rep_m = {n: (jnp.zeros((1, 4, D_XBC), F32) if n == "conv_w" else mom[n]) for n in _SMALL}
    rep_v = {n: (jnp.ones((1, 4, D_XBC), F32) if n == "conv_w" else var[n]) for n in _SMALL}
    sg_, sd_, sm_, sv_ = _adamw(small_parts, _pack_small(rep), _pack_small(rep_m), _pack_small(rep_v), "adamw_small")
    for k, p in (("g", sg_), ("d", sd_), ("m", sm_), ("v", sv_)):
        un = _unpack_small(p, small_shapes)
        for n in _SMALL:
            if n != "conv_w":
                res[k][n] = un[n]
    g_conv_full = _unpack_small(sg_, small_shapes)["conv_w"][0]
    g_conv_mine = lax.dynamic_slice_in_dim(g_conv_full, dev * cw_cols, cw_cols, axis=1)
    cg_, cd_, cm_, cv_ = _adamw(g_conv_mine[None], conv_w[0], m_conv_w[0], v_conv_w[0], "adamw_conv_w")
    for k, a in (("g", cg_), ("d", cd_), ("m", cm_), ("v", cv_)):
        res[k]["conv_w"] = a[None]

    grad_x = grad_x2.reshape(nb, seq, D_MODEL)
    return (loss, grad_x, *[res["g"][n] for n in names], *[res["d"][n] for n in names],
            *[res["m"][n] for n in names], *[res["v"][n] for n in names])
```

```python
import functools
import math

import jax
import jax.numpy as jnp
from jax import lax
from jax.experimental import pallas as pl
from jax.experimental.pallas import tpu as pltpu

F32, BF16 = jnp.float32, jnp.bfloat16
MESH = pl.DeviceIdType.MESH
N_DEV = 8

D_MODEL = 1024
D_SSD = 1536
N_HEADS = 24
HEADDIM = 64
N_GROUPS = 4
HPG = 6
GW = HPG * HEADDIM
N_STATE = 128
CHUNK = 128
D_XBC = 2560
D_S5 = 512
S5_GROUPS = 32
S5_CH = 16
S5_P = 64
S5_N = S5_GROUPS * S5_P
D_IN = 4632
DT_PAD = 128
D_INP = D_SSD + D_XBC + DT_PAD + D_S5
D_FF = 4096
N_MOD = 6
ALPHA = 2.0 ** 0.25
EPS = 1e-5
LR, B1, B2, AEPS, WD, STEP = 0.001, 0.9, 0.999, 1e-08, 0.01, 10

NT = (((1,), (1,)), ((), ()))
TN = (((0,), (0,)), ((), ()))
ANY = pl.BlockSpec(memory_space=pl.ANY)
HIGHEST = lax.Precision.HIGHEST


def _mm(a, b):
    return jnp.dot(a.astype(BF16), b.astype(BF16), preferred_element_type=F32)


def _mm_nt(a, b):
    return lax.dot_general(a.astype(BF16), b.astype(BF16), NT, preferred_element_type=F32)


def _mm_tn(a, b):
    return lax.dot_general(a.astype(BF16), b.astype(BF16), TN, preferred_element_type=F32)


def _row_block(r, cap):
    best = r
    for cand in range(8, min(r, cap) + 1, 8):
        if r % cand == 0:
            best = cand
    return best if best <= cap else r


def _params(vmem_mb):
    return pltpu.CompilerParams(vmem_limit_bytes=vmem_mb << 20)


def _sigmoid(x):
    return 0.5 * (jnp.tanh(0.5 * x) + 1.0)


def _softplus(x):
    return jnp.maximum(x, 0.0) + jnp.log(1.0 + jnp.exp(-jnp.abs(x)))


_GK = math.sqrt(2.0 / math.pi)


def _gelu(x):
    return 0.5 * x * (1.0 + jnp.tanh(_GK * (x + 0.044715 * x * x * x)))


def _gelu_grad(x):
    t = jnp.tanh(_GK * (x + 0.044715 * x * x * x))
    return 0.5 * (1.0 + t) + 0.5 * x * (1.0 - t * t) * _GK * (1.0 + 3.0 * 0.044715 * x * x)


def _dev_index():
    return 4 * lax.axis_index("x") + 2 * lax.axis_index("y") + lax.axis_index("c")


def _all_gather(xs, name):
    n = len(xs)

    def body(*refs):
        x_refs, out_refs = refs[:n], refs[n:2 * n]
        send_sems, recv_sems, local_sems = refs[2 * n:]
        ix, iy, ic = lax.axis_index("x"), lax.axis_index("y"), lax.axis_index("c")
        me, sibling = (ix, iy, ic), (ix, iy, 1 - ic)
        chips = [(1 - ix, iy), (ix, 1 - iy), (1 - ix, 1 - iy)]

        def slot(a, px, py, pc):
            return out_refs[a].at[4 * px + 2 * py + pc]

        def copy(a, k, block, to, src=None):
            return pltpu.make_async_remote_copy(
                src_ref=slot(a, *block) if src is None else src, dst_ref=slot(a, *block),
                send_sem=send_sems.at[7 * a + k], recv_sem=recv_sems.at[7 * a + k], device_id=to, device_id_type=MESH)

        mine = [pltpu.make_async_copy(x_refs[a], slot(a, *me), local_sems.at[a]) for a in range(n)]
        for cp in mine:
            cp.start()
        first = []
        for j, chip in enumerate(chips):
            first += [copy(a, 1 + j, me, (*chip, ic), src=x_refs[a]) for a in range(n)]
        first += [copy(a, 0, me, sibling, src=x_refs[a]) for a in range(n)]
        for cp in first:
            cp.start()
        passed = []
        for j, chip in enumerate(chips):
            for a in range(n):
                copy(a, 1 + j, (*chip, ic), me).wait_recv()
                cp = copy(a, 4 + j, (*chip, ic), sibling)
                cp.start()
                passed.append(cp)
        for a in range(n):
            copy(a, 0, sibling, me).wait_recv()
            for j, chip in enumerate(chips):
                copy(a, 4 + j, (*chip, 1 - ic), me).wait_recv()
        for cp in first + passed:
            cp.wait_send()
        for cp in mine:
            cp.wait()

    return pl.pallas_call(
        body, name=name, out_shape=tuple(jax.ShapeDtypeStruct((N_DEV,) + x.shape, x.dtype) for x in xs),
        in_specs=[ANY] * n, out_specs=tuple([ANY] * n),
        scratch_shapes=[pltpu.SemaphoreType.DMA((7 * n,)), pltpu.SemaphoreType.DMA((7 * n,)),
                        pltpu.SemaphoreType.DMA((n,))],
    )(*xs)


HBM = pl.BlockSpec(memory_space=pltpu.HBM)
SEM = pl.BlockSpec(memory_space=pltpu.SEMAPHORE)
DATAFLOW = pltpu.SideEffectType.DATAFLOW_SIDE_EFFECTING


def _peer(k):
    ix, iy, ic = lax.axis_index("x"), lax.axis_index("y"), lax.axis_index("c")
    return (1 - ix if k & 4 else ix, 1 - iy if k & 2 else iy, 1 - ic if k & 1 else ic)


def _block_of(p):
    return 4 * p[0] + 2 * p[1] + p[2]


def _gather_start(xs, name):
    n = len(xs)
    lands = [lax.empty((N_DEV,) + x.shape, x.dtype) for x in xs]

    def body(*refs):
        x_refs, land_refs = refs[:n], refs[n:2 * n]
        send_sems, recv_sems = refs[2 * n], refs[2 * n + 1]
        token = refs[-1]
        me = _block_of(_peer(0))
        for a in range(n):
            for k in range(1, N_DEV):
                pltpu.make_async_remote_copy(
                    src_ref=x_refs[a], dst_ref=land_refs[a].at[me], send_sem=send_sems.at[7 * a + k - 1],
                    recv_sem=recv_sems.at[7 * a + k - 1], device_id=_peer(k), device_id_type=MESH).start()
        token[...] = jnp.zeros_like(token)

    outs = pl.pallas_call(
        body, name=name,
        out_shape=(pltpu.SemaphoreType.DMA((7 * n,)), pltpu.SemaphoreType.DMA((7 * n,)))
        + tuple(pltpu.HBM(x.shape, x.dtype) for x in xs) + tuple(pltpu.HBM(l.shape, l.dtype) for l in lands)
        + (jax.ShapeDtypeStruct((8, 128), F32),),
        in_specs=[HBM] * (2 * n), out_specs=(SEM, SEM) + (HBM,) * (2 * n) + (pl.BlockSpec(memory_space=pltpu.VMEM),),
        input_output_aliases={i: 2 + i for i in range(2 * n)},
        compiler_params=pltpu.CompilerParams(has_side_effects=DATAFLOW),
    )(*[pltpu.with_memory_space_constraint(x, pltpu.HBM) for x in xs],
      *[pltpu.with_memory_space_constraint(l, pltpu.HBM) for l in lands])
    return outs[0], outs[1], outs[2:2 + n], outs[2 + n:2 + 2 * n], outs[-1]


def _gather_wait(send_sems, recv_sems, xs_thru, lands_thru, after, name):
    n = len(xs_thru)

    def body(*refs):
        x_refs, land_refs = refs[:n], refs[n:2 * n]
        send_sems, recv_sems = refs[2 * n], refs[2 * n + 1]
        for a in range(n):
            for k in range(1, N_DEV):
                cp = pltpu.make_async_remote_copy(
                    src_ref=x_refs[a], dst_ref=land_refs[a].at[_block_of(_peer(k))], send_sem=send_sems.at[7 * a + k - 1],
                    recv_sem=recv_sems.at[7 * a + k - 1], device_id=_peer(k), device_id_type=MESH)
                cp.wait_send()
                cp.wait_recv()

    outs = pl.pallas_call(
        body, name=name,
        out_shape=tuple(pltpu.HBM(x.shape, x.dtype) for x in xs_thru)
        + tuple(pltpu.HBM(l.shape, l.dtype) for l in lands_thru),
        in_specs=[HBM] * (2 * n) + [SEM, SEM, ANY], out_specs=(HBM,) * (2 * n),
        input_output_aliases={i: i for i in range(2 * n)},
        compiler_params=pltpu.CompilerParams(has_side_effects=DATAFLOW),
    )(*xs_thru, *lands_thru, send_sems, recv_sems, after)
    return outs[:n], outs[n:]


def _chip_peer(k):
    ix, iy = lax.axis_index("x"), lax.axis_index("y")
    return (1 - ix if k & 2 else ix, 1 - iy if k & 1 else iy)


def _all_to_all_start(hs, name):
    n = len(hs)
    lands = [lax.empty(h.shape, h.dtype) for h in hs]

    def body(*refs):
        h_refs, land_refs = refs[:n], refs[n:2 * n]
        send_sems, recv_sems = refs[2 * n], refs[2 * n + 1]
        token = refs[-1]
        ic = lax.axis_index("c")
        mx, my = _chip_peer(0)
        for a in range(n):
            for k in range(1, 4):
                px, py = _chip_peer(k)
                pltpu.make_async_remote_copy(
                    src_ref=h_refs[a].at[2 * px + py], dst_ref=land_refs[a].at[2 * mx + my],
                    send_sem=send_sems.at[3 * a + k - 1], recv_sem=recv_sems.at[3 * a + k - 1],
                    device_id=(px, py, ic), device_id_type=MESH).start()
        token[...] = jnp.zeros_like(token)

    outs = pl.pallas_call(
        body, name=name,
        out_shape=(pltpu.SemaphoreType.DMA((3 * n,)), pltpu.SemaphoreType.DMA((3 * n,)))
        + tuple(pltpu.HBM(h.shape, h.dtype) for h in hs) + tuple(pltpu.HBM(l.shape, l.dtype) for l in lands)
        + (jax.ShapeDtypeStruct((8, 128), F32),),
        in_specs=[HBM] * (2 * n), out_specs=(SEM, SEM) + (HBM,) * (2 * n) + (pl.BlockSpec(memory_space=pltpu.VMEM),),
        input_output_aliases={i: 2 + i for i in range(2 * n)},
        compiler_params=pltpu.CompilerParams(has_side_effects=DATAFLOW),
    )(*[pltpu.with_memory_space_constraint(h, pltpu.HBM) for h in hs],
      *[pltpu.with_memory_space_constraint(l, pltpu.HBM) for l in lands])
    return outs[0], outs[1], outs[2:2 + n], outs[2 + n:2 + 2 * n], outs[-1]


def _all_to_all_wait(send_sems, recv_sems, hs_thru, lands_thru, after, name):
    n = len(hs_thru)

    def body(*refs):
        h_refs, land_refs = refs[:n], refs[n:2 * n]
        send_sems, recv_sems = refs[2 * n], refs[2 * n + 1]
        ic = lax.axis_index("c")
        for a in range(n):
            for k in range(1, 4):
                px, py = _chip_peer(k)
                cp = pltpu.make_async_remote_copy(
                    src_ref=h_refs[a].at[2 * px + py], dst_ref=land_refs[a].at[2 * px + py],
                    send_sem=send_sems.at[3 * a + k - 1], recv_sem=recv_sems.at[3 * a + k - 1],
                    device_id=(px, py, ic), device_id_type=MESH)
                cp.wait_send()
                cp.wait_recv()

    outs = pl.pallas_call(
        body, name=name,
        out_shape=tuple(pltpu.HBM(h.shape, h.dtype) for h in hs_thru)
        + tuple(pltpu.HBM(l.shape, l.dtype) for l in lands_thru),
        in_specs=[HBM] * (2 * n) + [SEM, SEM, ANY], out_specs=(HBM,) * (2 * n),
        input_output_aliases={i: i for i in range(2 * n)},
        compiler_params=pltpu.CompilerParams(has_side_effects=DATAFLOW),
    )(*hs_thru, *lands_thru, send_sems, recv_sems, after)
    return outs[:n], outs[n:]


def _sibling_swap(gs, name):
    n = len(gs)

    def body(*refs):
        g_refs, recv_refs = refs[:n], refs[n:2 * n]
        send_sems, recv_sems = refs[2 * n:]
        ix, iy, ic = lax.axis_index("x"), lax.axis_index("y"), lax.axis_index("c")

        def block(g_ref, q):
            if len(g_ref.shape) == 4:
                return g_ref.at[q, 1 - ic]
            cw = g_ref.shape[1] // N_DEV
            return g_ref.at[:, pl.ds(pl.multiple_of((2 * q + 1 - ic) * cw, 128), cw)]

        cps = []
        for a in range(n):
            for q in range(4):
                cps.append(pltpu.make_async_remote_copy(
                    src_ref=block(g_refs[a], q), dst_ref=recv_refs[a].at[q],
                    send_sem=send_sems.at[4 * a + q], recv_sem=recv_sems.at[4 * a + q],
                    device_id=(ix, iy, 1 - ic), device_id_type=MESH))
        for cp in cps:
            cp.start()
        for cp in cps:
            cp.wait()

    return pl.pallas_call(
        body, name=name,
        out_shape=tuple(jax.ShapeDtypeStruct(
            (4,) + (g.shape[2:] if g.ndim == 4 else (g.shape[0], g.shape[1] // N_DEV)), g.dtype) for g in gs),
        in_specs=[ANY] * n, out_specs=tuple([ANY] * n),
        scratch_shapes=[pltpu.SemaphoreType.DMA((4 * n,)), pltpu.SemaphoreType.DMA((4 * n,))],
    )(*gs)


def _chip_all_to_all(hs, name):
    n = len(hs)

    def body(*refs):
        h_refs, out_refs = refs[:n], refs[n:2 * n]
        send_sems, recv_sems, local_sems = refs[2 * n:]
        ix, iy, ic = lax.axis_index("x"), lax.axis_index("y"), lax.axis_index("c")
        me = 2 * ix + iy
        peers = [(1 - ix, iy), (ix, 1 - iy), (1 - ix, 1 - iy)]
        mine = [pltpu.make_async_copy(h_refs[a].at[me], out_refs[a].at[me], local_sems.at[a]) for a in range(n)]
        for cp in mine:
            cp.start()

        def copy(a, k, src_slot, dst_slot, peer):
            return pltpu.make_async_remote_copy(
                src_ref=h_refs[a].at[src_slot], dst_ref=out_refs[a].at[dst_slot],
                send_sem=send_sems.at[3 * a + k], recv_sem=recv_sems.at[3 * a + k],
                device_id=(*peer, ic), device_id_type=MESH)

        sends = [copy(a, k, 2 * px + py, me, (px, py)) for a in range(n) for k, (px, py) in enumerate(peers)]
        for cp in sends:
            cp.start()
        for a in range(n):
            for k, (px, py) in enumerate(peers):
                copy(a, k, 2 * px + py, 2 * px + py, (px, py)).wait_recv()
        for cp in sends:
            cp.wait_send()
        for cp in mine:
            cp.wait()

    return pl.pallas_call(
        body, name=name, out_shape=tuple(jax.ShapeDtypeStruct(h.shape, h.dtype) for h in hs),
        in_specs=[ANY] * n, out_specs=tuple([ANY] * n),
        scratch_shapes=[pltpu.SemaphoreType.DMA((3 * n,)), pltpu.SemaphoreType.DMA((3 * n,)),
                        pltpu.SemaphoreType.DMA((n,))],
    )(*hs)


def _add_halves(g, recv, core, name):
    _, r, c = recv.shape
    br = _row_block(r, 512)
    stacked = g.ndim == 4

    def body(core_ref, g_ref, r_ref, o_ref):
        o_ref[0] = ((g_ref[0, 0] if stacked else g_ref[...]) + r_ref[0]).astype(BF16)

    spec = pl.BlockSpec((1, br, c), lambda i, j, core_ref: (i, j, 0))
    if stacked:
        g_spec = pl.BlockSpec((1, 1, br, c), lambda i, j, core_ref: (i, core_ref[0], j, 0))
    else:
        g_spec = pl.BlockSpec((br, c), lambda i, j, core_ref: (j, 2 * i + core_ref[0]))
    return pl.pallas_call(
        body, name=name, out_shape=jax.ShapeDtypeStruct(recv.shape, BF16),
        grid_spec=pltpu.PrefetchScalarGridSpec(
            num_scalar_prefetch=1, grid=(4, r // br), in_specs=[g_spec, spec], out_specs=spec),
        compiler_params=_params(32),
    )(core, g, recv)


def _adamw(parts, w, m, v, name):
    n_parts, r, c = parts.shape
    br = _row_block(r, 512 if c <= 1024 else 256)

    def body(p_ref, w_ref, m_ref, v_ref, g_out, d_out, m_out, v_out):
        g = p_ref[0].astype(F32)
        for p in range(1, n_parts):
            g = g + p_ref[p].astype(F32)
        m2 = B1 * m_ref[...] + (1.0 - B1) * g
        v2 = B2 * v_ref[...] + (1.0 - B2) * (g * g)
        m_hat = m2 / (1.0 - B1 ** STEP)
        v_hat = v2 / (1.0 - B2 ** STEP)
        g_out[...] = g
        d_out[...] = -LR * (m_hat / (jnp.sqrt(v_hat) + AEPS) + WD * w_ref[...])
        m_out[...] = m2
        v_out[...] = v2

    spec = pl.BlockSpec((br, c), lambda i: (i, 0))
    out = jax.ShapeDtypeStruct((r, c), F32)
    return pl.pallas_call(
        body, name=name, out_shape=(out, out, out, out), grid=(r // br,),
        in_specs=[pl.BlockSpec((n_parts, br, c), lambda i: (0, i, 0)), spec, spec, spec],
        out_specs=(spec, spec, spec, spec), compiler_params=_params(40),
    )(parts, w, m, v)


def _atb(a, b, name):
    t, k1 = a.shape
    k2 = b.shape[1]
    bt = math.gcd(t, 2048)

    def pick(k):
        for cand in (1024, 768, 512, 384, 256, 128):
            if k % cand == 0:
                return cand
        return k

    b1, b2 = pick(k1), pick(k2)

    def body(a_ref, b_ref, o_ref):
        @pl.when(pl.program_id(2) == 0)
        def _():
            o_ref[...] = jnp.zeros_like(o_ref)
        o_ref[...] += _mm_tn(a_ref[...], b_ref[...])

    return pl.pallas_call(
        body, name=name, out_shape=jax.ShapeDtypeStruct((k1, k2), F32), grid=(k1 // b1, k2 // b2, t // bt),
        in_specs=[pl.BlockSpec((bt, b1), lambda i, j, k: (k, i)), pl.BlockSpec((bt, b2), lambda i, j, k: (k, j))],
        out_specs=pl.BlockSpec((b1, b2), lambda i, j, k: (i, j)), compiler_params=_params(48),
    )(a, b)


def _mod_fwd(c_all, w_ada, b_cols):
    def body(c_ref, w_ref, b_ref, o_ref):
        cc = c_ref[...]
        cond = cc * _sigmoid(cc)
        o_ref[...] = _mm(cond, w_ref[...]) + b_ref[...]

    return pl.pallas_call(body, name="mod_fwd", out_shape=jax.ShapeDtypeStruct((c_all.shape[0], w_ada.shape[1]), F32),
                          compiler_params=_params(32))(c_all, w_ada, b_cols)


def _mod_bwd(c_all, dmod_cols, dmod_all):
    def body(c_ref, dc_ref, da_ref, gw_ref, gb_ref):
        cc = c_ref[...]
        cond = cc * _sigmoid(cc)
        gw_ref[...] = _mm_tn(cond, dc_ref[...])
        gb_ref[...] = jnp.sum(da_ref[...], axis=0, keepdims=True)

    return pl.pallas_call(
        body, name="mod_bwd",
        out_shape=(jax.ShapeDtypeStruct((D_MODEL, dmod_cols.shape[1]), F32), jax.ShapeDtypeStruct((1, dmod_all.shape[1]), F32)),
        compiler_params=_params(32))(c_all, dmod_cols, dmod_all)


def _load_once(hbm_ref, vmem_ref, sem):
    @pl.when(pl.program_id(0) == 0)
    def _():
        cp = pltpu.make_async_copy(hbm_ref, vmem_ref, sem)
        cp.start()
        cp.wait()


def _conv_taps(win_ref, w, tb, cols):
    shifted = [win_ref[8 - j:8 - j + tb, cols] for j in range(4)]
    acc = w[3:4] * shifted[0]
    for j in (1, 2, 3):
        acc = acc + w[3 - j:4 - j] * shifted[j]
    return acc, shifted


def _proj_conv_fwd(x2, mod3, w_in_pad, conv_w, conv_b, seq):
    t = x2.shape[0]
    tb = 256
    npb = seq // tb
    cw = 512

    def body(x_ref, mod_ref, w_hbm, cw_ref, cb_ref, z_ref, pre_ref, xbc_ref, dsilu_ref, dt_ref, u5_ref, w_vmem, win, sem):
        _load_once(w_hbm, w_vmem, sem)
        first = (pl.program_id(0) % npb) == 0

        @pl.when(first)
        def _():
            win[0:8, :] = jnp.zeros((8, D_XBC), F32)

        @pl.when(jnp.logical_not(first))
        def _():
            win[0:8, :] = win[tb:tb + 8, :]

        m = mod_ref[0]
        u = (x_ref[...] * (1.0 + m[1:2]) + m[0:1]).astype(BF16)
        z_ref[...] = jnp.dot(u, w_vmem[:, 0:D_SSD], preferred_element_type=F32)
        dt_ref[...] = jnp.dot(u, w_vmem[:, D_SSD + D_XBC:D_SSD + D_XBC + DT_PAD], preferred_element_type=F32)
        u5_ref[...] = jnp.dot(u, w_vmem[:, D_SSD + D_XBC + DT_PAD:], preferred_element_type=F32)
        for k in range(D_XBC // cw):
            cols = slice(k * cw, (k + 1) * cw)
            pre_k = jnp.dot(u, w_vmem[:, D_SSD + k * cw:D_SSD + (k + 1) * cw], preferred_element_type=F32)
            win[8:8 + tb, cols] = pre_k
            pre_ref[:, cols] = pre_k
            conv, _ = _conv_taps(win, cw_ref[:, cols], tb, cols)
            conv = conv + cb_ref[:, cols]
            sg = _sigmoid(conv)
            xbc_ref[:, cols] = conv * sg
            dsilu_ref[:, cols] = sg * (1.0 + conv * (1.0 - sg))

    row = lambda w: pl.BlockSpec((tb, w), lambda i: (i, 0))
    return pl.pallas_call(
        body, name="proj_conv_fwd", grid=(t // tb,),
        out_shape=(jax.ShapeDtypeStruct((t, D_SSD), F32), jax.ShapeDtypeStruct((t, D_XBC), F32),
                   jax.ShapeDtypeStruct((t, D_XBC), F32), jax.ShapeDtypeStruct((t, D_XBC), F32),
                   jax.ShapeDtypeStruct((t, DT_PAD), F32), jax.ShapeDtypeStruct((t, D_S5), F32)),
        in_specs=[row(D_MODEL), pl.BlockSpec((1, N_MOD, D_MODEL), lambda i: (i // npb, 0, 0)), ANY,
                  pl.BlockSpec((4, D_XBC), lambda i: (0, 0)), pl.BlockSpec((1, D_XBC), lambda i: (0, 0))],
        out_specs=(row(D_SSD), row(D_XBC), row(D_XBC), row(D_XBC), row(DT_PAD), row(D_S5)),
        scratch_shapes=[pltpu.VMEM((D_MODEL, D_INP), BF16), pltpu.VMEM((tb + 8, D_XBC), F32), pltpu.SemaphoreType.DMA],
        compiler_params=_params(56),
    )(x2, mod3, w_in_pad, conv_w, conv_b)


N_PAIRS = N_HEADS // 2


def _split3(x):
    hi = x.astype(BF16)
    r = x - hi.astype(F32)
    mid = r.astype(BF16)
    lo = (r - mid.astype(F32)).astype(BF16)
    return hi, mid, lo


def _dot3(x, e, dims=(((1,), (0,)), ((), ()))):
    return sum(lax.dot_general(p, e, dims, preferred_element_type=F32) for p in _split3(x))


def _dot3_left(e, x, dims=(((1,), (0,)), ((), ()))):
    return sum(lax.dot_general(e, p, dims, preferred_element_type=F32) for p in _split3(x))


def _head_fold():
    return (jnp.arange(D_SSD)[:, None] // HEADDIM == jnp.arange(128)[None, :]).astype(BF16)


def _ssd_prep(dt_raw, par):
    dtb = par[0:1]
    a = -jnp.exp(par[1:2])
    dt = _softplus(dt_raw + dtb)
    adt = dt * a
    row = lax.broadcasted_iota(jnp.int32, (CHUNK, CHUNK), 0)
    col = lax.broadcasted_iota(jnp.int32, (CHUNK, CHUNK), 1)
    causal = row >= col
    tri = causal.astype(BF16)
    cs = _dot3_left(tri, adt)
    left = col < HEADDIM

    def lanes(v, h):
        return jnp.broadcast_to(v[:, h:h + 1], (CHUNK, 128))

    dt_c, cs_c, pair_cols = [], [], []
    for p in range(N_PAIRS):
        c0, c1 = lanes(cs, 2 * p), lanes(cs, 2 * p + 1)
        pair_cols.append(jnp.concatenate([c0, c1], axis=1))
        cs_c.append(jnp.where(left, c0, c1))
        dt_c.append(jnp.where(left, lanes(dt, 2 * p), lanes(dt, 2 * p + 1)))
    cs_c = jnp.concatenate(cs_c, axis=1)
    dt_c = jnp.concatenate(dt_c, axis=1)
    return dt, a, cs, cs.T, causal, tri, dt_c, jnp.exp(cs_c), jnp.exp(cs_c[CHUNK - 1:CHUNK, :] - cs_c), pair_cols


def _pair_decay(cols, cst, pair, causal2):
    rows = jnp.concatenate([jnp.broadcast_to(cst[2 * pair:2 * pair + 1, :], (CHUNK, CHUNK)),
                            jnp.broadcast_to(cst[2 * pair + 1:2 * pair + 2, :], (CHUNK, CHUNK))], axis=1)
    return jnp.exp(jnp.where(causal2, cols - rows, -jnp.inf))


def _stack_heads(xp, left):
    return jnp.concatenate([jnp.where(left, xp, 0.0), jnp.where(left, 0.0, xp)], axis=0).astype(BF16)


def _ssd_fwd(xbc, z, dt_raw, par, dsk, normw, seq):
    t = xbc.shape[0]
    nc = seq // CHUNK
    n_chunks = t // CHUNK

    def body(xbc_ref, z_ref, dt_ref, par_ref, dsk_ref, nw_ref, yraw_ref, ycat_ref, hprev_ref, h_ref):
        @pl.when(pl.program_id(0) % nc == 0)
        def _():
            h_ref[...] = jnp.zeros_like(h_ref)
        hprev_ref[0] = h_ref[...]
        _, _, cs, cst, causal, _, dt_c, ecs_c, w_c, pair_cols = _ssd_prep(dt_ref[...], par_ref[...])
        cs_last = cs[CHUNK - 1:CHUNK, :]
        causal2 = jnp.concatenate([causal, causal], axis=1)
        left = lax.broadcasted_iota(jnp.int32, (CHUNK, 128), 1) < HEADDIM
        x = xbc_ref[:, 0:D_SSD]
        xdt = x * dt_c
        amat = (w_c * xdt).astype(BF16)
        zz = z_ref[...]
        silu_z = zz * _sigmoid(zz)
        for g in range(N_GROUPS):
            gs = slice(g * GW, (g + 1) * GW)
            bg = xbc_ref[:, D_SSD + g * N_STATE:D_SSD + (g + 1) * N_STATE].astype(BF16)
            cg = xbc_ref[:, D_SSD + (N_GROUPS + g) * N_STATE:D_SSD + (N_GROUPS + g + 1) * N_STATE].astype(BF16)
            scores = lax.dot_general(cg, bg, NT, preferred_element_type=F32)
            scores2 = jnp.concatenate([scores, scores], axis=1)
            hg = h_ref[gs, :]
            p_all = lax.dot_general(cg, hg.astype(BF16), NT, preferred_element_type=F32)
            ys = []
            for q in range(GW // 128):
                pair = g * (GW // 128) + q
                decay = _pair_decay(pair_cols[pair], cst, pair, causal2)
                mcat = (scores2 * decay).astype(BF16)
                ys.append(jnp.dot(mcat, _stack_heads(xdt[:, pair * 128:(pair + 1) * 128], left),
                                  preferred_element_type=F32))
            yg = jnp.concatenate(ys, axis=1) + ecs_c[:, gs] * p_all + x[:, gs] * dsk_ref[:, gs]
            s_new = lax.dot_general(amat[:, gs], bg, TN, preferred_element_type=F32)
            for j in range(HPG):
                hh = g * HPG + j
                js = slice(j * HEADDIM, (j + 1) * HEADDIM)
                h_ref[g * GW + j * HEADDIM:g * GW + (j + 1) * HEADDIM, :] = (
                    hg[js, :] * jnp.exp(cs_last[:, hh:hh + 1]) + s_new[js, :])
            yraw_ref[:, gs] = yg
            v = yg * silu_z[:, gs]
            r = lax.rsqrt(jnp.mean(v * v, axis=-1, keepdims=True) + EPS)
            ycat_ref[:, gs] = (v * r * nw_ref[:, gs]).astype(BF16)

    row = lambda w: pl.BlockSpec((CHUNK, w), lambda i: (i, 0))
    full = lambda s: pl.BlockSpec(s, lambda i: (0,) * len(s))
    return pl.pallas_call(
        body, name="ssd_fwd", grid=(n_chunks,),
        out_shape=(jax.ShapeDtypeStruct((t, D_SSD), F32), jax.ShapeDtypeStruct((t, D_SSD + D_S5), BF16),
                   jax.ShapeDtypeStruct((n_chunks, D_SSD, N_STATE), F32)),
        in_specs=[row(D_XBC), row(D_SSD), row(DT_PAD), full((8, 128)), full((1, D_SSD)), full((1, D_SSD))],
        out_specs=(row(D_SSD), row(D_SSD), pl.BlockSpec((1, D_SSD, N_STATE), lambda i: (i, 0, 0))),
        scratch_shapes=[pltpu.VMEM((D_SSD, N_STATE), F32)],
        compiler_params=_params(40),
    )(xbc, z, dt_raw, par, dsk, normw)


S5_CW = 512
S5_BLOCKS = 4


def _tile_scan(in_re, in_im, out_re, out_im, carry_re, carry_im, pw_re, pw_im, n_tiles, reverse):
    steps = (1, 2, 4)
    for cc in range(S5_N // S5_CW):
        cols = slice(cc * S5_CW, (cc + 1) * S5_CW)
        a_re, a_im = pw_re[:, cols], pw_im[:, cols]
        rid = lax.broadcasted_iota(jnp.int32, (8, S5_CW), 0)
        pows = []
        for d in steps:
            k = 8 - d if reverse else d - 1
            keep = (rid < 8 - d) if reverse else (rid >= d)
            pows.append((jnp.where(keep, pw_re[k:k + 1, cols], 0.0), jnp.where(keep, pw_im[k:k + 1, cols], 0.0)))

        def tile(i, carry, cols=cols, pows=pows, a_re=a_re, a_im=a_im):
            r = (n_tiles - 1 - i) if reverse else i
            rows = pl.ds(pl.multiple_of(r * 8, 8), 8)
            xr, xi = in_re[rows, cols], in_im[rows, cols]
            for (pr, pi), d in zip(pows, steps):
                shift = 8 - d if reverse else d
                sr, si = pltpu.roll(xr, shift, axis=0), pltpu.roll(xi, shift, axis=0)
                xr, xi = xr + pr * sr - pi * si, xi + pr * si + pi * sr
            cr, ci = carry
            xr, xi = xr + a_re * cr - a_im * ci, xi + a_re * ci + a_im * cr
            out_re[rows, cols] = xr
            out_im[rows, cols] = xi
            edge = slice(0, 1) if reverse else slice(7, 8)
            return (jnp.broadcast_to(xr[edge], (8, S5_CW)), jnp.broadcast_to(xi[edge], (8, S5_CW)))

        c0 = (jnp.broadcast_to(carry_re[0:1, cols], (8, S5_CW)), jnp.broadcast_to(carry_im[0:1, cols], (8, S5_CW)))
        cr, ci = lax.fori_loop(0, n_tiles, tile, c0)
        carry_re[:, cols] = cr
        carry_im[:, cols] = ci


def _s5_params_math(ar, ai, ldt, br, bi):
    dt = jnp.exp(ldt)
    mag = jnp.exp(ar * dt)
    ang = ai * dt
    ab_re = mag * jnp.cos(ang)
    ab_im = mag * jnp.sin(ang)
    den = ar * ar + ai * ai
    n_re = ab_re - 1.0
    coef_re = (n_re * ar + ab_im * ai) / den
    coef_im = (ab_im * ar - n_re * ai) / den
    bb_re = coef_re * br - coef_im * bi
    bb_im = coef_re * bi + coef_im * br
    return ab_re, ab_im, bb_re, bb_im


def _s5_params_fwd(ar, ai, ldt, br, bi):
    def body(ar_ref, ai_ref, ldt_ref, br_ref, bi_ref, bbr_ref, bbi_ref, pfr_ref, pfi_ref, prr_ref, pri_ref):
        ab_re, ab_im, bb_re, bb_im = _s5_params_math(ar_ref[...], ai_ref[...], ldt_ref[...], br_ref[...], bi_ref[...])
        bbr_ref[...] = bb_re
        bbi_ref[...] = bb_im
        pr, pi = ab_re, ab_im
        for k in range(8):
            pfr_ref[k:k + 1, :] = pr
            pfi_ref[k:k + 1, :] = pi
            prr_ref[7 - k:8 - k, :] = pr
            pri_ref[7 - k:8 - k, :] = -pi
            pr, pi = pr * ab_re - pi * ab_im, pr * ab_im + pi * ab_re

    b16 = jax.ShapeDtypeStruct((S5_CH, S5_N), F32)
    p8 = jax.ShapeDtypeStruct((8, S5_N), F32)
    return pl.pallas_call(body, name="s5_params_fwd", out_shape=(b16, b16, p8, p8, p8, p8),
                          compiler_params=_params(32))(ar, ai, ldt, br, bi)


def _s5_params_bwd(ar, ai, ldt, br, bi, d_ab_re, d_ab_im, d_bb_re, d_bb_im):
    def body(ar_ref, ai_ref, ldt_ref, br_ref, bi_ref, dar_ref, dai_ref, dbr_ref, dbi_ref,
             gar_ref, gai_ref, gldt_ref, gbr_ref, gbi_ref):
        _, vjp = jax.vjp(_s5_params_math, ar_ref[...], ai_ref[...], ldt_ref[...], br_ref[...], bi_ref[...])
        g_ar, g_ai, g_ldt, g_br, g_bi = vjp((dar_ref[...], dai_ref[...], dbr_ref[...], dbi_ref[...]))
        gar_ref[...] = g_ar
        gai_ref[...] = g_ai
        gbr_ref[...] = g_br
        gbi_ref[...] = g_bi
        lane = lax.broadcasted_iota(jnp.int32, (S5_N, 128), 0) // S5_P
        grp = lax.broadcasted_iota(jnp.int32, (S5_N, 128), 1)
        fold = (lane == grp).astype(F32)
        gldt_ref[...] = jnp.dot(g_ldt, fold, preferred_element_type=F32, precision=HIGHEST)

    v1 = jax.ShapeDtypeStruct((1, S5_N), F32)
    b16 = jax.ShapeDtypeStruct((S5_CH, S5_N), F32)
    return pl.pallas_call(body, name="s5_params_bwd",
                          out_shape=(v1, v1, jax.ShapeDtypeStruct((1, 128), F32), b16, b16),
                          compiler_params=_params(32))(ar, ai, ldt, br, bi, d_ab_re, d_ab_im, d_bb_re, d_bb_im)


def _s5_fwd(u5, bb_re, bb_im, cc_re, cc_im, pf_re, pf_im, s5d, w_glu, b_glu, ycat, seq):
    t = u5.shape[0]
    tb = 256
    npb = seq // tb

    def body(u_ref, bbr_ref, bbi_ref, ccr_ref, cci_ref, pfr_ref, pfi_ref, d_ref, wg_ref, bg_ref, ycat_hbm,
             sre_ref, sim_ref, ypre_ref, y5_ref, bur, bui, car, cai):
        del ycat_hbm

        @pl.when(pl.program_id(0) % npb == 0)
        def _():
            car[...] = jnp.zeros_like(car)
            cai[...] = jnp.zeros_like(cai)
        u = u_ref[...]
        ub = u.astype(BF16)
        for j in range(S5_BLOCKS):
            ch, st = slice(j * 128, (j + 1) * 128), slice(j * 512, (j + 1) * 512)
            bur[:, st] = jnp.dot(ub[:, ch], bbr_ref[j], preferred_element_type=F32)
            bui[:, st] = jnp.dot(ub[:, ch], bbi_ref[j], preferred_element_type=F32)
        _tile_scan(bur, bui, sre_ref, sim_ref, car, cai, pfr_ref, pfi_ref, tb // 8, reverse=False)
        cs_y = []
        for j in range(S5_BLOCKS):
            st = slice(j * 512, (j + 1) * 512)
            cs_y.append(_mm(sre_ref[:, st], ccr_ref[j]) - _mm(sim_ref[:, st], cci_ref[j]))
        ypre = jnp.concatenate(cs_y, axis=1) + u * d_ref[...]
        ypre_ref[...] = ypre
        yg = _gelu(ypre)
        y5_ref[...] = (yg * _sigmoid(_mm(yg, wg_ref[...]) + bg_ref[...])).astype(BF16)

    row = lambda w: pl.BlockSpec((tb, w), lambda i: (i, 0))
    full = lambda a: pl.BlockSpec(a.shape, lambda i: (0,) * a.ndim)
    return pl.pallas_call(
        body, name="s5_fwd", grid=(t // tb,),
        out_shape=(jax.ShapeDtypeStruct((t, S5_N), F32), jax.ShapeDtypeStruct((t, S5_N), F32),
                   jax.ShapeDtypeStruct((t, D_S5), F32), jax.ShapeDtypeStruct(ycat.shape, BF16)),
        in_specs=[row(D_S5), full(bb_re), full(bb_im), full(cc_re), full(cc_im), full(pf_re), full(pf_im),
                  full(s5d), full(w_glu), full(b_glu), ANY],
        out_specs=(row(S5_N), row(S5_N), row(D_S5), pl.BlockSpec((tb, D_S5), lambda i: (i, D_SSD // D_S5))),
        input_output_aliases={10: 3},
        scratch_shapes=[pltpu.VMEM((tb, S5_N), F32), pltpu.VMEM((tb, S5_N), F32),
                        pltpu.VMEM((8, S5_N), F32), pltpu.VMEM((8, S5_N), F32)],
        compiler_params=_params(48),
    )(u5, bb_re, bb_im, cc_re, cc_im, pf_re, pf_im, s5d, w_glu, b_glu, ycat)


def _layer_norm(r, g, b):
    mu = jnp.mean(r, axis=-1, keepdims=True)
    xc = r - mu
    rstd = lax.rsqrt(jnp.mean(xc * xc, axis=-1, keepdims=True) + EPS)
    xhat = xc * rstd
    return xhat * g + b, xhat, rstd


def _layer_norm_bwd(dy, xhat, rstd, g):
    dxhat = dy * g
    return rstd * (dxhat - jnp.mean(dxhat, axis=-1, keepdims=True)
                   - xhat * jnp.mean(dxhat * xhat, axis=-1, keepdims=True))


def _out_ln1(ycat, x2, mod3, w_out, ln1, seq):
    t = x2.shape[0]
    tb = 512
    npb = seq // tb

    def body(y_ref, x_ref, mod_ref, w_ref, ln_ref, mix_ref, x1_ref):
        m = mod_ref[0]
        mix = jnp.dot(y_ref[...], w_ref[...], preferred_element_type=F32)
        mix_ref[...] = mix
        r1 = ALPHA * x_ref[...] + (1.0 + m[2:3]) * mix
        x1_ref[...] = _layer_norm(r1, ln_ref[0:1], ln_ref[1:2])[0]

    row = lambda w: pl.BlockSpec((tb, w), lambda i: (i, 0))
    return pl.pallas_call(
        body, name="out_ln1", grid=(t // tb,),
        out_shape=(jax.ShapeDtypeStruct((t, D_MODEL), F32), jax.ShapeDtypeStruct((t, D_MODEL), F32)),
        in_specs=[row(D_SSD + D_S5), row(D_MODEL), pl.BlockSpec((1, N_MOD, D_MODEL), lambda i: (i // npb, 0, 0)),
                  pl.BlockSpec(w_out.shape, lambda i: (0, 0)), pl.BlockSpec(ln1.shape, lambda i: (0, 0))],
        out_specs=(row(D_MODEL), row(D_MODEL)), compiler_params=_params(48),
    )(ycat, x2, mod3, w_out, ln1)


def _mlp_fwd_bwd(x1, tgt, mod3, w1, w2, vec1, b1, seq):
    t = x1.shape[0]
    tb = 256
    npb = seq // tb
    n_fb, _, fb = w1.shape

    def body(x1_ref, tgt_ref, mod_ref, w1_hbm, w2_hbm, v_ref, b1_ref,
             dx1_ref, u2_ref, h_ref, dhp_ref, do_ref, gacc_ref, db1_ref, bacc_ref, w1_v, w2_v, sem1, sem2):
        i = pl.program_id(0)
        @pl.when(i == 0)
        def _():
            cps = [pltpu.make_async_copy(w1_hbm.at[k], w1_v.at[:, k * fb:(k + 1) * fb], sem1.at[k])
                   for k in range(n_fb)]
            for cp in cps:
                cp.start()
            for cp in cps:
                cp.wait()
        _load_once(w2_hbm, w2_v, sem2)

        @pl.when(i == 0)
        def _():
            gacc_ref[...] = jnp.zeros_like(gacc_ref)
            db1_ref[...] = jnp.zeros_like(db1_ref)

        @pl.when(i % npb == 0)
        def _():
            bacc_ref[...] = jnp.zeros_like(bacc_ref)

        m = mod_ref[0]
        sh2, sc2, g2 = m[3:4], m[4:5], m[5:6]
        x1v = x1_ref[...]
        u2 = (x1v * (1.0 + sc2) + sh2).astype(BF16)
        u2_ref[...] = u2
        hr = jnp.maximum(jnp.dot(u2, w1_v[...], preferred_element_type=F32) + b1_ref[...], 0.0)
        hb = (hr * hr).astype(BF16)
        h_ref[...] = hb
        o = jnp.dot(hb, w2_v[...], preferred_element_type=F32) + v_ref[0:1]
        r2 = ALPHA * x1v + (1.0 + g2) * o
        y, xhat, rstd = _layer_norm(r2, v_ref[1:2], v_ref[2:3])
        err = y - tgt_ref[...]
        dy = err * (1.0 / D_MODEL)
        dr2 = _layer_norm_bwd(dy, xhat, rstd, v_ref[1:2])
        do = (1.0 + g2) * dr2
        dob = do.astype(BF16)
        do_ref[...] = dob
        gacc_ref[0:1, :] += jnp.sum(dy * xhat, axis=0, keepdims=True)
        gacc_ref[1:2, :] += jnp.sum(dy, axis=0, keepdims=True)
        gacc_ref[2:3, :] += jnp.sum(do, axis=0, keepdims=True)
        gacc_ref[3:4, :] += jnp.sum(err * err, axis=0, keepdims=True)
        dhpre = lax.dot_general(dob, w2_v[...], NT, preferred_element_type=F32) * (2.0 * hr)
        dhpb = dhpre.astype(BF16)
        dhp_ref[...] = dhpb
        db1_ref[...] += jnp.sum(dhpre, axis=0, keepdims=True)
        du2 = lax.dot_general(dhpb, w1_v[...], NT, preferred_element_type=F32)
        dx1_ref[...] = ALPHA * dr2 + du2 * (1.0 + sc2)
        bacc_ref[0, 0:1, :] += jnp.sum(du2, axis=0, keepdims=True)
        bacc_ref[0, 1:2, :] += jnp.sum(du2 * x1v, axis=0, keepdims=True)
        bacc_ref[0, 2:3, :] += jnp.sum(dr2 * o, axis=0, keepdims=True)

    row = lambda w: pl.BlockSpec((tb, w), lambda i: (i, 0))
    return pl.pallas_call(
        body, name="mlp_fwd_bwd", grid=(t // tb,),
        out_shape=(jax.ShapeDtypeStruct((t, D_MODEL), F32), jax.ShapeDtypeStruct((t, D_MODEL), BF16),
                   jax.ShapeDtypeStruct((t, D_FF), BF16), jax.ShapeDtypeStruct((t, D_FF), BF16),
                   jax.ShapeDtypeStruct((t, D_MODEL), BF16), jax.ShapeDtypeStruct((8, D_MODEL), F32),
                   jax.ShapeDtypeStruct((1, D_FF), F32), jax.ShapeDtypeStruct((t // seq, 8, D_MODEL), F32)),
        in_specs=[row(D_MODEL), row(D_MODEL), pl.BlockSpec((1, N_MOD, D_MODEL), lambda i: (i // npb, 0, 0)), ANY, ANY,
                  pl.BlockSpec(vec1.shape, lambda i: (0, 0)), pl.BlockSpec(b1.shape, lambda i: (0, 0))],
        out_specs=(row(D_MODEL), row(D_MODEL), row(D_FF), row(D_FF), row(D_MODEL),
                   pl.BlockSpec((8, D_MODEL), lambda i: (0, 0)), pl.BlockSpec((1, D_FF), lambda i: (0, 0)),
                   pl.BlockSpec((1, 8, D_MODEL), lambda i: (i // npb, 0, 0))),
        scratch_shapes=[pltpu.VMEM((D_MODEL, n_fb * fb), BF16), pltpu.VMEM((D_FF, D_MODEL), BF16),
                        pltpu.SemaphoreType.DMA((n_fb,)), pltpu.SemaphoreType.DMA],
        compiler_params=_params(60),
    )(x1, tgt, mod3, w1, w2, vec1, b1)


def _ln1_out_bwd(dx1, x2, mix, mod3, w_out, ln1, seq):
    t = x2.shape[0]
    tb = 512
    npb = seq // tb

    def body(dx1_ref, x_ref, mix_ref, mod_ref, w_ref, ln_ref, dmix_ref, dxa_ref, dys_ref, dy5_ref, gacc_ref, bacc_ref):
        i = pl.program_id(0)

        @pl.when(i == 0)
        def _():
            gacc_ref[...] = jnp.zeros_like(gacc_ref)

        @pl.when(i % npb == 0)
        def _():
            bacc_ref[...] = jnp.zeros_like(bacc_ref)

        m = mod_ref[0]
        mix = mix_ref[...]
        r1 = ALPHA * x_ref[...] + (1.0 + m[2:3]) * mix
        _, xhat, rstd = _layer_norm(r1, ln_ref[0:1], ln_ref[1:2])
        dx1v = dx1_ref[...]
        dr1 = _layer_norm_bwd(dx1v, xhat, rstd, ln_ref[0:1])
        gacc_ref[0:1, :] += jnp.sum(dx1v * xhat, axis=0, keepdims=True)
        gacc_ref[1:2, :] += jnp.sum(dx1v, axis=0, keepdims=True)
        bacc_ref[0, 0:1, :] += jnp.sum(dr1 * mix, axis=0, keepdims=True)
        dmix = ((1.0 + m[2:3]) * dr1).astype(BF16)
        dmix_ref[...] = dmix
        dxa_ref[...] = ALPHA * dr1
        dys_ref[...] = lax.dot_general(dmix, w_ref[0:D_SSD, :], NT, preferred_element_type=F32)
        dy5_ref[...] = lax.dot_general(dmix, w_ref[D_SSD:, :], NT, preferred_element_type=F32)

    row = lambda w: pl.BlockSpec((tb, w), lambda i: (i, 0))
    return pl.pallas_call(
        body, name="ln1_out_bwd", grid=(t // tb,),
        out_shape=(jax.ShapeDtypeStruct((t, D_MODEL), BF16), jax.ShapeDtypeStruct((t, D_MODEL), F32),
                   jax.ShapeDtypeStruct((t, D_SSD), F32), jax.ShapeDtypeStruct((t, D_S5), F32),
                   jax.ShapeDtypeStruct((8, D_MODEL), F32), jax.ShapeDtypeStruct((t // seq, 8, D_MODEL), F32)),
        in_specs=[row(D_MODEL), row(D_MODEL), row(D_MODEL), pl.BlockSpec((1, N_MOD, D_MODEL), lambda i: (i // npb, 0, 0)),
                  pl.BlockSpec(w_out.shape, lambda i: (0, 0)), pl.BlockSpec(ln1.shape, lambda i: (0, 0))],
        out_specs=(row(D_MODEL), row(D_MODEL), row(D_SSD), row(D_S5), pl.BlockSpec((8, D_MODEL), lambda i: (0, 0)),
                   pl.BlockSpec((1, 8, D_MODEL), lambda i: (i // npb, 0, 0))),
        compiler_params=_params(48),
    )(dx1, x2, mix, mod3, w_out, ln1)


def _s5_bwd(dy5, ypre, u5, s_re, s_im, bb_re, bb_im, cc_re, cc_im, pr_re, pr_im, s5d, w_glu, b_glu, seq):
    t = u5.shape[0]
    tb = 256
    npb = seq // tb
    n_blocks = t // tb

    def blk(i):
        return (i // npb) * npb + (npb - 1 - i % npb)

    def body(dy_ref, ypre_ref, u_ref, sre_ref, sim_ref, hre_ref, him_ref, bbr_ref, bbi_ref, ccr_ref, cci_ref,
             prr_ref, pri_ref, d_ref, wg_ref, bg_ref,
             du_ref, vacc_ref, sacc_ref, dcc_ref, dbb_ref, dwg_ref, dsr, dsi, gr, gi, car, cai):
        i = pl.program_id(0)

        @pl.when(i == 0)
        def _():
            for acc in (vacc_ref, sacc_ref, dcc_ref, dbb_ref, dwg_ref):
                acc[...] = jnp.zeros_like(acc)

        @pl.when(i % npb == 0)
        def _():
            car[...] = jnp.zeros_like(car)
            cai[...] = jnp.zeros_like(cai)

        dy = dy_ref[...]
        ypre = ypre_ref[...]
        u = u_ref[...]
        ub = u.astype(BF16)
        yg = _gelu(ypre)
        sg = _sigmoid(_mm(yg, wg_ref[...]) + bg_ref[...])
        dq = dy * yg * sg * (1.0 - sg)
        dqb = dq.astype(BF16)
        dyg = dy * sg + lax.dot_general(dqb, wg_ref[...], NT, preferred_element_type=F32)
        dyp = dyg * _gelu_grad(ypre)
        dypb = dyp.astype(BF16)
        dwg_ref[...] += lax.dot_general(yg.astype(BF16), dqb, TN, preferred_element_type=F32)
        blocks = [(slice(j * 128, (j + 1) * 128), slice(j * 512, (j + 1) * 512)) for j in range(S5_BLOCKS)]
        for j, (ch, st) in enumerate(blocks):
            dsr[:, st] = lax.dot_general(dypb[:, ch], ccr_ref[j], NT, preferred_element_type=F32)
            dsi[:, st] = -lax.dot_general(dypb[:, ch], cci_ref[j], NT, preferred_element_type=F32)
        _tile_scan(dsr, dsi, gr, gi, car, cai, prr_ref, pri_ref, tb // 8, reverse=True)
        g_re, g_im = gr[...], gi[...]
        first_rows = (i % npb) == npb - 1
        hre = jnp.where(first_rows, 0.0, hre_ref[...])
        him = jnp.where(first_rows, 0.0, him_ref[...])
        s_re_v, s_im_v = sre_ref[...], sim_ref[...]
        sp_re = pltpu.roll(jnp.concatenate([hre, s_re_v], axis=0), 1, axis=0)[8:8 + tb]
        sp_im = pltpu.roll(jnp.concatenate([him, s_im_v], axis=0), 1, axis=0)[8:8 + tb]
        vacc_ref[0:1, :] += jnp.sum(g_re * sp_re + g_im * sp_im, axis=0, keepdims=True)
        vacc_ref[1:2, :] += jnp.sum(g_im * sp_re - g_re * sp_im, axis=0, keepdims=True)
        grb, gib = g_re.astype(BF16), g_im.astype(BF16)
        srb, sib = s_re_v.astype(BF16), s_im_v.astype(BF16)
        du_cols = []
        for j, (ch, st) in enumerate(blocks):
            dcc_ref[j] += lax.dot_general(srb[:, st], dypb[:, ch], TN, preferred_element_type=F32)
            dcc_ref[S5_BLOCKS + j] -= lax.dot_general(sib[:, st], dypb[:, ch], TN, preferred_element_type=F32)
            dbb_ref[j] += lax.dot_general(ub[:, ch], grb[:, st], TN, preferred_element_type=F32)
            dbb_ref[S5_BLOCKS + j] += lax.dot_general(ub[:, ch], gib[:, st], TN, preferred_element_type=F32)
            du_cols.append(lax.dot_general(grb[:, st], bbr_ref[j], NT, preferred_element_type=F32)
                           + lax.dot_general(gib[:, st], bbi_ref[j], NT, preferred_element_type=F32))
        du_ref[...] = jnp.concatenate(du_cols, axis=1) + dyp * d_ref[...]
        sacc_ref[0:1, :] += jnp.sum(dyp * u, axis=0, keepdims=True)
        sacc_ref[1:2, :] += jnp.sum(dq, axis=0, keepdims=True)

    row = lambda w: pl.BlockSpec((tb, w), lambda i: (blk(i), 0))
    halo = pl.BlockSpec((8, S5_N), lambda i: (jnp.maximum(blk(i) * (tb // 8) - 1, 0), 0))
    full = lambda a: pl.BlockSpec(a.shape, lambda i: (0,) * a.ndim)
    acc = lambda s: pl.BlockSpec(s, lambda i: (0,) * len(s))
    acc_shapes = [(8, S5_N), (8, D_S5), (2 * S5_BLOCKS, 512, 128), (2 * S5_BLOCKS, 128, 512), (D_S5, D_S5)]
    return pl.pallas_call(
        body, name="s5_bwd", grid=(n_blocks,),
        out_shape=(jax.ShapeDtypeStruct((t, D_S5), F32),) + tuple(jax.ShapeDtypeStruct(s, F32) for s in acc_shapes),
        in_specs=[row(D_S5), row(D_S5), row(D_S5), row(S5_N), row(S5_N), halo, halo, full(bb_re), full(bb_im),
                  full(cc_re), full(cc_im), full(pr_re), full(pr_im), full(s5d), full(w_glu), full(b_glu)],
        out_specs=(row(D_S5),) + tuple(acc(s) for s in acc_shapes),
        scratch_shapes=[pltpu.VMEM((tb, S5_N), F32), pltpu.VMEM((tb, S5_N), F32), pltpu.VMEM((tb, S5_N), F32),
                        pltpu.VMEM((tb, S5_N), F32), pltpu.VMEM((8, S5_N), F32), pltpu.VMEM((8, S5_N), F32)],
        compiler_params=_params(56),
    )(dy5, ypre, u5, s_re, s_im, s_re, s_im, bb_re, bb_im, cc_re, cc_im, pr_re, pr_im, s5d, w_glu, b_glu)


def _ssd_bwd(dyssd, yraw, z, xbc, dt_raw, hprev, par, dsk, normw, seq):
    t = xbc.shape[0]
    nc = seq // CHUNK
    n_chunks = t // CHUNK
    fold = _head_fold()

    def blk(i):
        return (i // nc) * nc + (nc - 1 - i % nc)

    def body(dy_ref, yraw_ref, z_ref, xbc_ref, dt_ref, hprev_ref, par_ref, dsk_ref, nw_ref, fold_ref,
             dxbc_ref, dz_ref, ddt_ref, dpar_ref, cacc_ref, dh_ref, dyr_ref):
        i = pl.program_id(0)

        @pl.when(i == 0)
        def _():
            dpar_ref[...] = jnp.zeros_like(dpar_ref)
            cacc_ref[...] = jnp.zeros_like(cacc_ref)

        @pl.when(i % nc == 0)
        def _():
            dh_ref[...] = jnp.zeros_like(dh_ref)

        zz = z_ref[...]
        sz = _sigmoid(zz)
        silu_z = zz * sz
        yraw = yraw_ref[...]
        for g in range(N_GROUPS):
            sl = slice(g * GW, (g + 1) * GW)
            v = yraw[:, sl] * silu_z[:, sl]
            r = lax.rsqrt(jnp.mean(v * v, axis=-1, keepdims=True) + EPS)
            dyg = dy_ref[:, sl]
            cacc_ref[1:2, sl] += jnp.sum(dyg * v * r, axis=0, keepdims=True)
            dyw = dyg * nw_ref[:, sl]
            dv = r * dyw - v * (r * r * r) * jnp.mean(dyw * v, axis=-1, keepdims=True)
            dyr_ref[:, sl] = dv * silu_z[:, sl]
            dz_ref[:, sl] = dv * yraw[:, sl] * (sz[:, sl] * (1.0 + zz[:, sl] * (1.0 - sz[:, sl])))

        dt, a, cs, cst, causal, tri, dt_c, ecs_c, w_c, pair_cols = _ssd_prep(dt_ref[...], par_ref[...])
        cs_last = cs[CHUNK - 1:CHUNK, :]
        causal2 = jnp.concatenate([causal, causal], axis=1)
        lane = lax.broadcasted_iota(jnp.int32, (CHUNK, 128), 1)
        left = lane < HEADDIM
        lane1 = lax.broadcasted_iota(jnp.int32, (1, 128), 1)
        x = xbc_ref[:, 0:D_SSD]
        xdt = x * dt_c
        dyr = dyr_ref[...]
        dyrb = dyr.astype(BF16)
        cacc_ref[0:1, :] += jnp.sum(dyr * x, axis=0, keepdims=True)
        dlast = jnp.zeros((1, 128), F32)
        dxdt_cols, diag_all, dww_cols = [], [], []
        for g in range(N_GROUPS):
            gs = slice(g * GW, (g + 1) * GW)
            b_sl = slice(D_SSD + g * N_STATE, D_SSD + (g + 1) * N_STATE)
            c_sl = slice(D_SSD + (N_GROUPS + g) * N_STATE, D_SSD + (N_GROUPS + g + 1) * N_STATE)
            bg = xbc_ref[:, b_sl].astype(BF16)
            cg = xbc_ref[:, c_sl].astype(BF16)
            scores = lax.dot_general(cg, bg, NT, preferred_element_type=F32)
            scores2 = jnp.concatenate([scores, scores], axis=1)
            hg = hprev_ref[0, gs, :]
            hgb = hg.astype(BF16)
            dhg = dh_ref[gs, :]
            dhgb = dhg.astype(BF16)
            q_all = lax.dot_general(bg, dhgb, NT, preferred_element_type=F32)
            dscores = jnp.zeros((CHUNK, CHUNK), F32)
            diag_cols = []
            for q in range(GW // 128):
                pair = g * (GW // 128) + q
                ps = slice(pair * 128, (pair + 1) * 128)
                decay = _pair_decay(pair_cols[pair], cst, pair, causal2)
                mcat = (scores2 * decay).astype(BF16)
                dyp = dyrb[:, ps]
                dm = lax.dot_general(dyp, _stack_heads(xdt[:, ps], left), NT, preferred_element_type=F32)
                dmd = dm * decay
                dscores = dscores + dmd[:, 0:CHUNK] + dmd[:, CHUNK:]
                rr = lax.dot_general(mcat, dyp, TN, preferred_element_type=F32)
                diag_cols.append(jnp.where(left, rr[0:CHUNK], rr[CHUNK:]))
            wq = w_c[:, gs] * q_all
            diag_g = jnp.concatenate(diag_cols, axis=1)
            diag_all.append(diag_g)
            dxdt_cols.append(diag_g + wq)
            dww_cols.append(wq * xdt[:, gs])
            dp = (ecs_c[:, gs] * dyr[:, gs]).astype(BF16)
            amat = (w_c[:, gs] * xdt[:, gs]).astype(BF16)
            dsb = dscores.astype(BF16)
            dxbc_ref[:, c_sl] = (jnp.dot(dsb, bg, preferred_element_type=F32)
                                 + jnp.dot(dp, hgb, preferred_element_type=F32))
            dxbc_ref[:, b_sl] = (lax.dot_general(dsb, cg, TN, preferred_element_type=F32)
                                 + jnp.dot(amat, dhgb, preferred_element_type=F32))
            dh_in = lax.dot_general(dp, cg, TN, preferred_element_type=F32)
            for j in range(HPG):
                hh = g * HPG + j
                js = slice(j * HEADDIM, (j + 1) * HEADDIM)
                ecl = jnp.exp(cs_last[:, hh:hh + 1])
                dlast = dlast + jnp.where(lane1 == hh, ecl * jnp.sum(dhg[js, :] * hg[js, :]), 0.0)
                dh_ref[g * GW + j * HEADDIM:g * GW + (j + 1) * HEADDIM, :] = ecl * dhg[js, :] + dh_in[js, :]
        dxdt = jnp.concatenate(dxdt_cols, axis=1)
        dxbc_ref[:, 0:D_SSD] = dxdt * dt_c + dyr * dsk_ref[...]
        dww = _dot3(jnp.concatenate(dww_cols, axis=1), fold_ref[...])
        dcs = _dot3(dyrb.astype(F32) * (yraw - x * dsk_ref[...])
                    - xdt.astype(BF16).astype(F32) * jnp.concatenate(diag_all, axis=1), fold_ref[...]) - dww
        rowid = lax.broadcasted_iota(jnp.int32, (CHUNK, 128), 0)
        dcs = dcs + jnp.where(rowid == CHUNK - 1, jnp.sum(dww, axis=0, keepdims=True) + dlast, 0.0)
        dadt = _dot3_left(tri, dcs, TN)
        ddt = _dot3(dxdt * x, fold_ref[...]) + dadt * a
        da = jnp.sum(dadt * dt, axis=0, keepdims=True)
        ddt_raw = ddt * _sigmoid(dt_ref[...] + par_ref[0:1])
        ddt_raw = jnp.where(lane < N_HEADS, ddt_raw, 0.0)
        ddt_ref[...] = ddt_raw
        dpar_ref[0:1, :] += jnp.sum(ddt_raw, axis=0, keepdims=True)
        dpar_ref[1:2, :] += jnp.where(lane1 < N_HEADS, da * a, 0.0)

    row = lambda w: pl.BlockSpec((CHUNK, w), lambda i: (blk(i), 0))
    full = lambda s: pl.BlockSpec(s, lambda i: (0,) * len(s))
    return pl.pallas_call(
        body, name="ssd_bwd", grid=(n_chunks,),
        out_shape=(jax.ShapeDtypeStruct((t, D_XBC), F32), jax.ShapeDtypeStruct((t, D_SSD), F32),
                   jax.ShapeDtypeStruct((t, DT_PAD), F32), jax.ShapeDtypeStruct((8, 128), F32),
                   jax.ShapeDtypeStruct((8, D_SSD), F32)),
        in_specs=[row(D_SSD), row(D_SSD), row(D_SSD), row(D_XBC), row(DT_PAD),
                  pl.BlockSpec((1, D_SSD, N_STATE), lambda i: (blk(i), 0, 0)),
                  full((8, 128)), full((1, D_SSD)), full((1, D_SSD)), full(fold.shape)],
        out_specs=(row(D_XBC), row(D_SSD), row(DT_PAD), full((8, 128)), full((8, D_SSD))),
        scratch_shapes=[pltpu.VMEM((D_SSD, N_STATE), F32), pltpu.VMEM((CHUNK, D_SSD), F32)],
        compiler_params=_params(48),
    )(dyssd, yraw, z, xbc, dt_raw, hprev, par, dsk, normw, fold)


def _conv_bwd(dxbc, dsilu, xbc_pre, seq):
    t = xbc_pre.shape[0]
    tb = 512
    npb = seq // tb
    cw = 640

    def body(d_ref, ds_ref, cur_ref, halo_ref, o_ref, acc_ref, win):
        i = pl.program_id(1)

        @pl.when(i == 0)
        def _():
            acc_ref[...] = jnp.zeros_like(acc_ref)

        first = (i % npb) == 0
        win[0:8, :] = jnp.where(first, 0.0, halo_ref[...])
        win[8:8 + tb, :] = cur_ref[...]
        dpre = d_ref[...] * ds_ref[...]
        o_ref[...] = dpre
        for j in range(4):
            acc_ref[3 - j:4 - j, :] += jnp.sum(dpre * win[8 - j:8 - j + tb, :], axis=0, keepdims=True)
        acc_ref[4:5, :] += jnp.sum(dpre, axis=0, keepdims=True)

    blk = pl.BlockSpec((tb, cw), lambda j, i: (i, j))
    return pl.pallas_call(
        body, name="conv_bwd", grid=(D_XBC // cw, t // tb),
        out_shape=(jax.ShapeDtypeStruct((t, D_XBC), F32), jax.ShapeDtypeStruct((8, D_XBC), F32)),
        in_specs=[blk, blk, blk, pl.BlockSpec((8, cw), lambda j, i: (jnp.maximum(i * (tb // 8) - 1, 0), j))],
        out_specs=(blk, pl.BlockSpec((8, cw), lambda j, i: (0, j))),
        scratch_shapes=[pltpu.VMEM((tb + 8, cw), F32)],
        compiler_params=_params(32),
    )(dxbc, dsilu, xbc_pre, xbc_pre)


def _proj_bwd(dz, dpre, ddt, du5, x2, dxa, mod3, conv_w, w_in_pad, seq):
    t = x2.shape[0]
    tb = 512
    npb = seq // tb
    n_blocks = t // tb

    def body(dz_ref, dp_ref, nxt_ref, ddt_ref, du5_ref, x_ref, dxa_ref, mod_ref, cw_ref, w_hbm,
             gx_ref, u_ref, dxp_ref, bacc_ref, w_vmem, sem):
        i = pl.program_id(0)
        _load_once(w_hbm, w_vmem, sem)

        @pl.when(i % npb == 0)
        def _():
            bacc_ref[...] = jnp.zeros_like(bacc_ref)

        last = (i % npb) == npb - 1
        nxt = jnp.where(last, 0.0, nxt_ref[...])
        cur = dp_ref[...]
        xx = jnp.concatenate([cur, nxt], axis=0)
        w = cw_ref[...]
        dxp = w[3:4] * cur
        for j in (1, 2, 3):
            dxp = dxp + w[3 - j:4 - j] * pltpu.roll(xx, tb + 8 - j, axis=0)[0:tb]
        dxpb = dxp.astype(BF16)
        dxp_ref[...] = dxpb
        o1, o2, o3 = D_SSD, D_SSD + D_XBC, D_SSD + D_XBC + DT_PAD
        du = (lax.dot_general(dz_ref[...].astype(BF16), w_vmem[:, 0:o1], NT, preferred_element_type=F32)
              + lax.dot_general(dxpb, w_vmem[:, o1:o2], NT, preferred_element_type=F32)
              + lax.dot_general(ddt_ref[...].astype(BF16), w_vmem[:, o2:o3], NT, preferred_element_type=F32)
              + lax.dot_general(du5_ref[...].astype(BF16), w_vmem[:, o3:], NT, preferred_element_type=F32))
        m = mod_ref[0]
        xv = x_ref[...]
        u_ref[...] = (xv * (1.0 + m[1:2]) + m[0:1]).astype(BF16)
        gx_ref[...] = dxa_ref[...] + du * (1.0 + m[1:2])
        bacc_ref[0, 0:1, :] += jnp.sum(du, axis=0, keepdims=True)
        bacc_ref[0, 1:2, :] += jnp.sum(du * xv, axis=0, keepdims=True)

    row = lambda w: pl.BlockSpec((tb, w), lambda i: (i, 0))
    nxt_rows = pl.BlockSpec((8, D_XBC), lambda i: (jnp.minimum((i + 1) * (tb // 8), t // 8 - 1), 0))
    return pl.pallas_call(
        body, name="proj_bwd", grid=(n_blocks,),
        out_shape=(jax.ShapeDtypeStruct((t, D_MODEL), F32), jax.ShapeDtypeStruct((t, D_MODEL), BF16),
                   jax.ShapeDtypeStruct((t, D_XBC), BF16), jax.ShapeDtypeStruct((t // seq, 8, D_MODEL), F32)),
        in_specs=[row(D_SSD), row(D_XBC), nxt_rows, row(DT_PAD), row(D_S5), row(D_MODEL), row(D_MODEL),
                  pl.BlockSpec((1, N_MOD, D_MODEL), lambda i: (i // npb, 0, 0)),
                  pl.BlockSpec((4, D_XBC), lambda i: (0, 0)), ANY],
        out_specs=(row(D_MODEL), row(D_MODEL), row(D_XBC), pl.BlockSpec((1, 8, D_MODEL), lambda i: (i // npb, 0, 0))),
        scratch_shapes=[pltpu.VMEM((D_MODEL, D_INP), BF16), pltpu.SemaphoreType.DMA],
        compiler_params=_params(60),
    )(dz, dpre, dpre, ddt, du5, x2, dxa, mod3, conv_w, w_in_pad)


def _pad_rows(a, mult):
    r = a.shape[0]
    pad = (-r) % mult
    return a if pad == 0 else jnp.concatenate([a, jnp.zeros((pad,) + a.shape[1:], a.dtype)], axis=0)


_SMALL = ["conv_w", "conv_b", "dt_bias", "a_log", "d_ssd", "norm_w", "s5_a_re", "s5_a_im", "s5_log_dt", "s5_b_re",
          "s5_b_im", "s5_c_re", "s5_c_im", "s5_d", "b_glu", "ln1_g", "ln1_b", "b1", "b2", "ln2_g", "ln2_b"]


def _tile_rows(size):
    return 8 * (-(-size // 1024))


def _pack_small(d):
    parts = []
    for n in _SMALL:
        flat = d[n].reshape(-1).astype(F32)
        rows = _tile_rows(flat.shape[0])
        pad = rows * 128 - flat.shape[0]
        if pad:
            flat = jnp.concatenate([flat, jnp.zeros((pad,), F32)])
        parts.append(flat.reshape(rows, 128))
    return jnp.concatenate(parts, axis=0)


def _unpack_small(p, shapes):
    out, off = {}, 0
    for n in _SMALL:
        size = math.prod(shapes[n])
        rows = _tile_rows(size)
        out[n] = p[off:off + rows].reshape(-1)[:size].reshape(shapes[n])
        off += rows
    return out


def kernel(x, c, w_ada, b_ada, w_in, conv_w, conv_b, dt_bias, a_log, d_ssd, norm_w, s5_a_re, s5_a_im, s5_log_dt, s5_b_re, s5_b_im, s5_c_re, s5_c_im, s5_d, w_glu, b_glu, w_out, ln1_g, ln1_b, w1, b1, w2, b2, ln2_g, ln2_b, loss_target, m_w_ada, m_b_ada, m_w_in, m_conv_w, m_conv_b, m_dt_bias, m_a_log, m_d_ssd, m_norm_w, m_s5_a_re, m_s5_a_im, m_s5_log_dt, m_s5_b_re, m_s5_b_im, m_s5_c_re, m_s5_c_im, m_s5_d, m_w_glu, m_b_glu, m_w_out, m_ln1_g, m_ln1_b, m_w1, m_b1, m_w2, m_b2, m_ln2_g, m_ln2_b, v_w_ada, v_b_ada, v_w_in, v_conv_w, v_conv_b, v_dt_bias, v_a_log, v_d_ssd, v_norm_w, v_s5_a_re, v_s5_a_im, v_s5_log_dt, v_s5_b_re, v_s5_b_im, v_s5_c_re, v_s5_c_im, v_s5_d, v_w_glu, v_b_glu, v_w_out, v_ln1_g, v_ln1_b, v_w1, v_b1, v_w2, v_b2, v_ln2_g, v_ln2_b):
    weights = dict(w_ada=w_ada, b_ada=b_ada, w_in=w_in, conv_w=conv_w, conv_b=conv_b, dt_bias=dt_bias, a_log=a_log,
                   d_ssd=d_ssd, norm_w=norm_w, s5_a_re=s5_a_re, s5_a_im=s5_a_im, s5_log_dt=s5_log_dt, s5_b_re=s5_b_re,
                   s5_b_im=s5_b_im, s5_c_re=s5_c_re, s5_c_im=s5_c_im, s5_d=s5_d, w_glu=w_glu, b_glu=b_glu, w_out=w_out,
                   ln1_g=ln1_g, ln1_b=ln1_b, w1=w1, b1=b1, w2=w2, b2=b2, ln2_g=ln2_g, ln2_b=ln2_b)
    mom = dict(w_ada=m_w_ada, b_ada=m_b_ada, w_in=m_w_in, conv_w=m_conv_w, conv_b=m_conv_b, dt_bias=m_dt_bias,
               a_log=m_a_log, d_ssd=m_d_ssd, norm_w=m_norm_w, s5_a_re=m_s5_a_re, s5_a_im=m_s5_a_im,
               s5_log_dt=m_s5_log_dt, s5_b_re=m_s5_b_re, s5_b_im=m_s5_b_im, s5_c_re=m_s5_c_re, s5_c_im=m_s5_c_im,
               s5_d=m_s5_d, w_glu=m_w_glu, b_glu=m_b_glu, w_out=m_w_out, ln1_g=m_ln1_g, ln1_b=m_ln1_b, w1=m_w1, b1=m_b1,
               w2=m_w2, b2=m_b2, ln2_g=m_ln2_g, ln2_b=m_ln2_b)
    var = dict(w_ada=v_w_ada, b_ada=v_b_ada, w_in=v_w_in, conv_w=v_conv_w, conv_b=v_conv_b, dt_bias=v_dt_bias,
               a_log=v_a_log, d_ssd=v_d_ssd, norm_w=v_norm_w, s5_a_re=v_s5_a_re, s5_a_im=v_s5_a_im,
               s5_log_dt=v_s5_log_dt, s5_b_re=v_s5_b_re, s5_b_im=v_s5_b_im, s5_c_re=v_s5_c_re, s5_c_im=v_s5_c_im,
               s5_d=v_s5_d, w_glu=v_w_glu, b_glu=v_b_glu, w_out=v_w_out, ln1_g=v_ln1_g, ln1_b=v_ln1_b, w1=v_w1, b1=v_b1,
               w2=v_w2, b2=v_b2, ln2_g=v_ln2_g, ln2_b=v_ln2_b)
    names = list(weights)
    shapes = {n: weights[n].shape for n in names}

    nb, seq, _ = x.shape
    t = nb * seq
    dev = _dev_index()
    x2 = x.reshape(t, D_MODEL)
    tgt2 = loss_target.reshape(t, D_MODEL)

    cw_cols = conv_w.shape[2]
    small_in = jnp.concatenate([c.reshape(-1), conv_w.reshape(-1)]).reshape(-1, 128)
    big_names = ["w_in", "w_out", "w1", "w2", "w_glu"]
    shard_bf16 = {n: weights[n][0].astype(BF16) for n in big_names}
    first = _all_gather([small_in, shard_bf16["w_in"], shard_bf16["w_glu"]], "gather_first")
    small_all = first[0].reshape(N_DEV, -1)
    c_all = small_all[:, :nb * D_MODEL].reshape(N_DEV * nb, D_MODEL)
    conv_w_full = small_all[:, nb * D_MODEL:].reshape(N_DEV, 4, cw_cols).transpose(1, 0, 2).reshape(4, D_XBC)

    w_in_f = first[1].transpose(1, 0, 2).reshape(D_MODEL, D_IN)
    w_in_pad = jnp.concatenate(
        [w_in_f[:, :D_SSD + D_XBC], w_in_f[:, D_SSD + D_XBC:D_SSD + D_XBC + N_HEADS],
         jnp.zeros((D_MODEL, DT_PAD - N_HEADS), BF16), w_in_f[:, D_SSD + D_XBC + N_HEADS:]], axis=1)
    w_glu_f = first[2].reshape(D_S5, D_S5)
    late_names = ["w_out", "w1", "w2"]

    ada_cols = w_ada.shape[2]
    b_cols = lax.dynamic_slice_in_dim(b_ada, dev * ada_cols, ada_cols, axis=1)
    mod_cols = _mod_fwd(c_all, w_ada[0], b_cols)
    mod_all = _all_gather([mod_cols], "gather_mod")[0]
    mod_mine = lax.dynamic_slice_in_dim(mod_all, dev * nb, nb, axis=1)
    mod3 = mod_mine.transpose(1, 0, 2).reshape(nb, N_MOD, D_MODEL)
    late_in, mod3 = lax.optimization_barrier(([shard_bf16[n] for n in late_names], mod3))
    late_sems = _gather_start(late_in, "gather_late_start")
    mod3 = mod3 + late_sems[4][0, 0]

    def pad_lanes(v, n):
        return jnp.concatenate([v, jnp.zeros((v.shape[0], n - v.shape[1]), F32)], axis=1)

    par = _pad_rows(jnp.concatenate([pad_lanes(dt_bias, 128), pad_lanes(a_log, 128)], axis=0), 8)
    dsk = jnp.repeat(d_ssd[0], HEADDIM).reshape(1, D_SSD)
    ar = s5_a_re.reshape(1, S5_N)
    ai = s5_a_im.reshape(1, S5_N)
    ldt = jnp.repeat(s5_log_dt[0], S5_P).reshape(1, S5_N)
    br_t = s5_b_re[0].transpose(2, 0, 1).reshape(S5_CH, S5_N)
    bi_t = s5_b_im[0].transpose(2, 0, 1).reshape(S5_CH, S5_N)
    bb_re_t, bb_im_t, pf_re, pf_im, pr_re, pr_im = _s5_params_fwd(ar, ai, ldt, br_t, bi_t)
    gpb = S5_GROUPS // S5_BLOCKS
    mask_b = (jnp.arange(128)[:, None] // S5_CH) == (jnp.arange(512)[None, :] // S5_P)

    def dense_b(bt_):
        blocks = bt_.reshape(S5_CH, S5_BLOCKS, 512).transpose(1, 0, 2)
        return jnp.where(mask_b, jnp.tile(blocks, (1, gpb, 1)), 0.0).astype(BF16)

    def dense_c(cc):
        blocks = cc[0].transpose(0, 2, 1).reshape(S5_BLOCKS, 512, S5_CH)
        return jnp.where(mask_b.T, jnp.tile(blocks, (1, 1, gpb)), 0.0).astype(BF16)

    bb_re, bb_im = dense_b(bb_re_t), dense_b(bb_im_t)
    cc_re, cc_im = dense_c(s5_c_re), dense_c(s5_c_im)
    s5d = s5_d.reshape(1, D_S5)
    ln1 = jnp.concatenate([ln1_g, ln1_b], axis=0)
    vec1 = _pad_rows(jnp.concatenate([b2, ln2_g, ln2_b], axis=0), 8)

    z, xbc_pre, xbc, dsilu, dt_raw, u5 = _proj_conv_fwd(x2, mod3, w_in_pad, conv_w_full, conv_b, seq)
    yraw, ycat, hprev = _ssd_fwd(xbc, z, dt_raw, par, dsk, norm_w, seq)
    s_re, s_im, ypre, ycat = _s5_fwd(u5, bb_re, bb_im, cc_re, cc_im, pf_re, pf_im, s5d, w_glu_f, b_glu, ycat, seq)
    sent, landed = _gather_wait(late_sems[0], late_sems[1], late_sems[2], late_sems[3], ycat, "gather_late_wait")
    gathered = {n: lax.dynamic_update_index_in_dim(l, x, dev, 0) for n, x, l in zip(late_names, sent, landed)}
    w_out_f = gathered["w_out"].reshape(2 * D_MODEL, D_MODEL)
    w1_blocks = gathered["w1"]
    w2_f = gathered["w2"].reshape(D_FF, D_MODEL)
    mix, x1 = _out_ln1(ycat, x2, mod3, w_out_f, ln1, seq)

    dx1, u2b, hb, dhpb, dob, gacc2, db1, bacc2 = _mlp_fwd_bwd(x1, tgt2, mod3, w1_blocks, w2_f, vec1, b1, seq)
    loss = lax.psum(0.5 / D_MODEL * jnp.sum(gacc2[3]), ("x", "y", "c"))

    dmixb, dxa, dyssd, dy5, gacc1, bacc1 = _ln1_out_bwd(dx1, x2, mix, mod3, w_out_f, ln1, seq)

    g_w2 = _atb(hb, dob, "gw2")
    g_w1 = _atb(u2b, dhpb, "gw1")
    g_wout = _atb(ycat, dmixb, "gwout")
    core = lax.axis_index("c").astype(jnp.int32).reshape(1)
    chip = 2 * lax.axis_index("x") + lax.axis_index("y")

    def chip_sums_of(names, grads, tag):
        by_dest = [g if g.ndim == 2 else g.reshape((4, 2) + g.shape[1:]) for g in grads]
        from_sibling = _sibling_swap(by_dest, "rs_swap_" + tag)
        return [_add_halves(g, r, core, "rs_add_" + n) for g, r, n in zip(by_dest, from_sibling, names)]

    early_names = ["w_out", "w1", "w2"]
    early_sums = chip_sums_of(early_names, [g_wout.reshape((N_DEV,) + w_out.shape[1:]), g_w1,
                                            g_w2.reshape((N_DEV,) + w2.shape[1:])], "early")
    early = _all_to_all_start(early_sums, "rs_early_start")
    s5d_after = s5d + early[4][0, 0]

    du5, vacc, sacc, d_cc, d_bb, g_wglu = _s5_bwd(dy5, ypre, u5, s_re, s_im, bb_re, bb_im, cc_re, cc_im,
                                                  pr_re, pr_im, s5d_after, w_glu_f, b_glu, seq)
    dxbc, dz, ddt, dpar, cacc = _ssd_bwd(dyssd, yraw, z, xbc, dt_raw, hprev, par, dsk, norm_w, seq)
    dpre, conv_acc = _conv_bwd(dxbc, dsilu, xbc_pre, seq)
    grad_x2, ub, dxpb, bacc0 = _proj_bwd(dz, dpre, ddt, du5, x2, dxa, mod3, conv_w_full, w_in_pad, seq)

    g_win = jnp.concatenate([_atb(ub, dz, "gwin_z"), _atb(ub, dxpb, "gwin_xbc"),
                             _atb(ub, ddt, "gwin_dt")[:, :N_HEADS], _atb(ub, du5, "gwin_s5")], axis=1)

    def diag_b(dd):
        kept = jnp.where(mask_b, dd, 0.0).reshape(S5_BLOCKS, gpb, S5_CH, 512).sum(1)
        return kept.transpose(1, 0, 2).reshape(S5_CH, S5_N)

    def diag_c(dd):
        kept = jnp.where(mask_b.T, dd, 0.0).reshape(S5_BLOCKS, 512, gpb, S5_CH).sum(2)
        return kept.reshape(S5_GROUPS, S5_P, S5_CH).transpose(0, 2, 1)

    g_ar, g_ai, g_ldt, g_br_t, g_bi_t = _s5_params_bwd(ar, ai, ldt, br_t, bi_t, vacc[0:1], vacc[1:2],
                                                      diag_b(d_bb[:S5_BLOCKS]), diag_b(d_bb[S5_BLOCKS:]))

    def from_t(gt):
        return gt.reshape(S5_CH, S5_GROUPS, S5_P).transpose(1, 2, 0)

    small_g = dict(
        conv_w=conv_acc[0:4], conv_b=conv_acc[4:5], dt_bias=dpar[0:1, :N_HEADS], a_log=dpar[1:2, :N_HEADS],
        d_ssd=cacc[0].reshape(N_HEADS, HEADDIM).sum(1), norm_w=cacc[1:2],
        s5_a_re=g_ar, s5_a_im=g_ai, s5_log_dt=g_ldt[:, :S5_GROUPS], s5_b_re=from_t(g_br_t), s5_b_im=from_t(g_bi_t),
        s5_c_re=diag_c(d_cc[:S5_BLOCKS]), s5_c_im=diag_c(d_cc[S5_BLOCKS:]), s5_d=sacc[0:1], b_glu=sacc[1:2],
        ln1_g=gacc1[0:1], ln1_b=gacc1[1:2], b1=db1, b2=gacc2[2:3], ln2_g=gacc2[0:1], ln2_b=gacc2[1:2])

    dmod = jnp.concatenate([bacc0[:, 0], bacc0[:, 1], bacc1[:, 0], bacc2[:, 0], bacc2[:, 1], bacc2[:, 2]], axis=1)
    dmod_all = _all_gather([dmod], "gather_dmod")[0].reshape(N_DEV * nb, N_MOD * D_MODEL)
    dmod_cols = lax.dynamic_slice_in_dim(dmod_all, dev * ada_cols, ada_cols, axis=1)
    g_wada, g_bada = _mod_bwd(c_all, dmod_cols, dmod_all)

    in_cols = w_in.shape[2]
    late_rs = ["w_in", "w_glu"]
    late_sums = chip_sums_of(late_rs, [g_win.reshape(D_MODEL, N_DEV, in_cols).transpose(1, 0, 2),
                                       g_wglu.reshape((N_DEV,) + w_glu.shape[1:])], "late")
    parts = dict(zip(late_rs, _chip_all_to_all(late_sums, "rs_late_all_to_all")))
    sent, landed = _all_to_all_wait(early[0], early[1], early[2], early[3], parts["w_in"], "rs_early_wait")
    for n, l, h in zip(early_names, landed, sent):
        parts[n] = lax.dynamic_update_index_in_dim(l, lax.dynamic_index_in_dim(h, chip, 0, keepdims=False), chip, 0)

    res = {k: {} for k in "gdmv"}
    for n in big_names:
        outs = _adamw(parts[n], weights[n][0], mom[n][0], var[n][0], "adamw_" + n)
        for k, a in zip("gdmv", outs):
            res[k][n] = a[None]

    ag, ad, am, av = _adamw(g_wada[None], w_ada[0], m_w_ada[0], v_w_ada[0], "adamw_w_ada")
    for k, a in (("g", ag), ("d", ad), ("m", am), ("v", av)):
        res[k]["w_ada"] = a[None]
    bg_, bd_, bm_, bv_ = _adamw(g_bada.reshape(1, -1, 128), b_ada.reshape(-1, 128), m_b_ada.reshape(-1, 128),
                                v_b_ada.reshape(-1, 128), "adamw_b_ada")
    for k, a in (("g", bg_), ("d", bd_), ("m", bm_), ("v", bv_)):
        res[k]["b_ada"] = a.reshape(shapes["b_ada"])

    small_shapes = dict(shapes)
    small_shapes["conv_w"] = (1, 4, D_XBC)
    small_parts = _all_gather([_pack_small(small_g)], "gather_small_grads")[0]
    rep = {n: (jnp.zeros((1, 4, D_XBC), F32) if n == "conv_w" else weights[n]) for n in _SMALL}
    rep_m = {n: (jnp.zeros((1, 4, D_XBC), F32) if n == "conv_w" else mom[n]) for n in _SMALL}
    rep_v = {n: (jnp.ones((1, 4, D_XBC), F32) if n == "conv_w" else var[n]) for n in _SMALL}
    sg_, sd_, sm_, sv_ = _adamw(small_parts, _pack_small(rep), _pack_small(rep_m), _pack_small(rep_v), "adamw_small")
    for k, p in (("g", sg_), ("d", sd_), ("m", sm_), ("v", sv_)):
        un = _unpack_small(p, small_shapes)
        for n in _SMALL:
            if n != "conv_w":
                res[k][n] = un[n]
    g_conv_full = _unpack_small(sg_, small_shapes)["conv_w"][0]
    g_conv_mine = lax.dynamic_slice_in_dim(g_conv_full, dev * cw_cols, cw_cols, axis=1)
    cg_, cd_, cm_, cv_ = _adamw(g_conv_mine[None], conv_w[0], m_conv_w[0], v_conv_w[0], "adamw_conv_w")
    for k, a in (("g", cg_), ("d", cd_), ("m", cm_), ("v", cv_)):
        res[k]["conv_w"] = a[None]

    grad_x = grad_x2.reshape(nb, seq, D_MODEL)
    return (loss, grad_x, *[res["g"][n] for n in names], *[res["d"][n] for n in names],
            *[res["m"][n] for n in names], *[res["v"][n] for n in names])
```

```python
import functools
import math

import jax
import jax.numpy as jnp
from jax import lax
from jax.experimental import pallas as pl
from jax.experimental.pallas import tpu as pltpu

F32, BF16 = jnp.float32, jnp.bfloat16
MESH = pl.DeviceIdType.MESH
N_DEV = 8

D_MODEL = 1024
D_SSD = 1536
N_HEADS = 24
HEADDIM = 64
N_GROUPS = 4
HPG = 6
GW = HPG * HEADDIM
N_STATE = 128
CHUNK = 128
D_XBC = 2560
D_S5 = 512
S5_GROUPS = 32
S5_CH = 16
S5_P = 64
S5_N = S5_GROUPS * S5_P
D_IN = 4632
DT_PAD = 128
D_INP = D_SSD + D_XBC + DT_PAD + D_S5
D_FF = 4096
N_MOD = 6
ALPHA = 2.0 ** 0.25
EPS = 1e-5
LR, B1, B2, AEPS, WD, STEP = 0.001, 0.9, 0.999, 1e-08, 0.01, 10

NT = (((1,), (1,)), ((), ()))
TN = (((0,), (0,)), ((), ()))
ANY = pl.BlockSpec(memory_space=pl.ANY)
HIGHEST = lax.Precision.HIGHEST


def _mm(a, b):
    return jnp.dot(a.astype(BF16), b.astype(BF16), preferred_element_type=F32)


def _mm_nt(a, b):
    return lax.dot_general(a.astype(BF16), b.astype(BF16), NT, preferred_element_type=F32)


def _mm_tn(a, b):
    return lax.dot_general(a.astype(BF16), b.astype(BF16), TN, preferred_element_type=F32)


def _row_block(r, cap):
    best = r
    for cand in range(8, min(r, cap) + 1, 8):
        if r % cand == 0:
            best = cand
    return best if best <= cap else r


def _params(vmem_mb):
    return pltpu.CompilerParams(vmem_limit_bytes=vmem_mb << 20)


def _sigmoid(x):
    return 0.5 * (jnp.tanh(0.5 * x) + 1.0)


def _softplus(x):
    return jnp.maximum(x, 0.0) + jnp.log(1.0 + jnp.exp(-jnp.abs(x)))


_GK = math.sqrt(2.0 / math.pi)


def _gelu(x):
    return 0.5 * x * (1.0 + jnp.tanh(_GK * (x + 0.044715 * x * x * x)))


def _gelu_grad(x):
    t = jnp.tanh(_GK * (x + 0.044715 * x * x * x))
    return 0.5 * (1.0 + t) + 0.5 * x * (1.0 - t * t) * _GK * (1.0 + 3.0 * 0.044715 * x * x)


def _dev_index():
    return 4 * lax.axis_index("x") + 2 * lax.axis_index("y") + lax.axis_index("c")


def _all_gather(xs, name):
    n = len(xs)

    def body(*refs):
        x_refs, out_refs = refs[:n], refs[n:2 * n]
        send_sems, recv_sems, local_sems = refs[2 * n:]
        ix, iy, ic = lax.axis_index("x"), lax.axis_index("y"), lax.axis_index("c")
        me, sibling = (ix, iy, ic), (ix, iy, 1 - ic)
        chips = [(1 - ix, iy), (ix, 1 - iy), (1 - ix, 1 - iy)]

        def slot(a, px, py, pc):
            return out_refs[a].at[4 * px + 2 * py + pc]

        def copy(a, k, block, to, src=None):
            return pltpu.make_async_remote_copy(
                src_ref=slot(a, *block) if src is None else src, dst_ref=slot(a, *block),
                send_sem=send_sems.at[7 * a + k], recv_sem=recv_sems.at[7 * a + k], device_id=to, device_id_type=MESH)

        mine = [pltpu.make_async_copy(x_refs[a], slot(a, *me), local_sems.at[a]) for a in range(n)]
        for cp in mine:
            cp.start()
        first = []
        for j, chip in enumerate(chips):
            first += [copy(a, 1 + j, me, (*chip, ic), src=x_refs[a]) for a in range(n)]
        first += [copy(a, 0, me, sibling, src=x_refs[a]) for a in range(n)]
        for cp in first:
            cp.start()
        passed = []
        for j, chip in enumerate(chips):
            for a in range(n):
                copy(a, 1 + j, (*chip, ic), me).wait_recv()
                cp = copy(a, 4 + j, (*chip, ic), sibling)
                cp.start()
                passed.append(cp)
        for a in range(n):
            copy(a, 0, sibling, me).wait_recv()
            for j, chip in enumerate(chips):
                copy(a, 4 + j, (*chip, 1 - ic), me).wait_recv()
        for cp in first + passed:
            cp.wait_send()
        for cp in mine:
            cp.wait()

    return pl.pallas_call(
        body, name=name, out_shape=tuple(jax.ShapeDtypeStruct((N_DEV,) + x.shape, x.dtype) for x in xs),
        in_specs=[ANY] * n, out_specs=tuple([ANY] * n),
        scratch_shapes=[pltpu.SemaphoreType.DMA((7 * n,)), pltpu.SemaphoreType.DMA((7 * n,)),
                        pltpu.SemaphoreType.DMA((n,))],
    )(*xs)


HBM = pl.BlockSpec(memory_space=pltpu.HBM)
SEM = pl.BlockSpec(memory_space=pltpu.SEMAPHORE)
DATAFLOW = pltpu.SideEffectType.DATAFLOW_SIDE_EFFECTING


def _peer(k):
    ix, iy, ic = lax.axis_index("x"), lax.axis_index("y"), lax.axis_index("c")
    return (1 - ix if k & 4 else ix, 1 - iy if k & 2 else iy, 1 - ic if k & 1 else ic)


def _block_of(p):
    return 4 * p[0] + 2 * p[1] + p[2]


def _gather_start(xs, name):
    n = len(xs)
    lands = [lax.empty((N_DEV,) + x.shape, x.dtype) for x in xs]

    def body(*refs):
        x_refs, land_refs = refs[:n], refs[n:2 * n]
        send_sems, recv_sems = refs[2 * n], refs[2 * n + 1]
        token = refs[-1]
        me = _block_of(_peer(0))
        for a in range(n):
            for k in range(1, N_DEV):
                pltpu.make_async_remote_copy(
                    src_ref=x_refs[a], dst_ref=land_refs[a].at[me], send_sem=send_sems.at[7 * a + k - 1],
                    recv_sem=recv_sems.at[7 * a + k - 1], device_id=_peer(k), device_id_type=MESH).start()
        token[...] = jnp.zeros_like(token)

    outs = pl.pallas_call(
        body, name=name,
        out_shape=(pltpu.SemaphoreType.DMA((7 * n,)), pltpu.SemaphoreType.DMA((7 * n,)))
        + tuple(pltpu.HBM(x.shape, x.dtype) for x in xs) + tuple(pltpu.HBM(l.shape, l.dtype) for l in lands)
        + (jax.ShapeDtypeStruct((8, 128), F32),),
        in_specs=[HBM] * (2 * n), out_specs=(SEM, SEM) + (HBM,) * (2 * n) + (pl.BlockSpec(memory_space=pltpu.VMEM),),
        input_output_aliases={i: 2 + i for i in range(2 * n)},
        compiler_params=pltpu.CompilerParams(has_side_effects=DATAFLOW),
    )(*[pltpu.with_memory_space_constraint(x, pltpu.HBM) for x in xs],
      *[pltpu.with_memory_space_constraint(l, pltpu.HBM) for l in lands])
    return outs[0], outs[1], outs[2:2 + n], outs[2 + n:2 + 2 * n], outs[-1]


def _gather_wait(send_sems, recv_sems, xs_thru, lands_thru, after, name):
    n = len(xs_thru)

    def body(*refs):
        x_refs, land_refs = refs[:n], refs[n:2 * n]
        send_sems, recv_sems = refs[2 * n], refs[2 * n + 1]
        for a in range(n):
            for k in range(1, N_DEV):
                cp = pltpu.make_async_remote_copy(
                    src_ref=x_refs[a], dst_ref=land_refs[a].at[_block_of(_peer(k))], send_sem=send_sems.at[7 * a + k - 1],
                    recv_sem=recv_sems.at[7 * a + k - 1], device_id=_peer(k), device_id_type=MESH)
                cp.wait_send()
                cp.wait_recv()

    outs = pl.pallas_call(
        body, name=name,
        out_shape=tuple(pltpu.HBM(x.shape, x.dtype) for x in xs_thru)
        + tuple(pltpu.HBM(l.shape, l.dtype) for l in lands_thru),
        in_specs=[HBM] * (2 * n) + [SEM, SEM, ANY], out_specs=(HBM,) * (2 * n),
        input_output_aliases={i: i for i in range(2 * n)},
        compiler_params=pltpu.CompilerParams(has_side_effects=DATAFLOW),
    )(*xs_thru, *lands_thru, send_sems, recv_sems, after)
    return outs[:n], outs[n:]


def _chip_peer(k):
    ix, iy = lax.axis_index("x"), lax.axis_index("y")
    return (1 - ix if k & 2 else ix, 1 - iy if k & 1 else iy)


def _all_to_all_start(hs, name):
    n = len(hs)
    lands = [lax.empty(h.shape, h.dtype) for h in hs]

    def body(*refs):
        h_refs, land_refs = refs[:n], refs[n:2 * n]
        send_sems, recv_sems = refs[2 * n], refs[2 * n + 1]
        token = refs[-1]
        ic = lax.axis_index("c")
        mx, my = _chip_peer(0)
        for a in range(n):
            for k in range(1, 4):
                px, py = _chip_peer(k)
                pltpu.make_async_remote_copy(
                    src_ref=h_refs[a].at[2 * px + py], dst_ref=land_refs[a].at[2 * mx + my],
                    send_sem=send_sems.at[3 * a + k - 1], recv_sem=recv_sems.at[3 * a + k - 1],
                    device_id=(px, py, ic), device_id_type=MESH).start()
        token[...] = jnp.zeros_like(token)

    outs = pl.pallas_call(
        body, name=name,
        out_shape=(pltpu.SemaphoreType.DMA((3 * n,)), pltpu.SemaphoreType.DMA((3 * n,)))
        + tuple(pltpu.HBM(h.shape, h.dtype) for h in hs) + tuple(pltpu.HBM(l.shape, l.dtype) for l in lands)
        + (jax.ShapeDtypeStruct((8, 128), F32),),
        in_specs=[HBM] * (2 * n), out_specs=(SEM, SEM) + (HBM,) * (2 * n) + (pl.BlockSpec(memory_space=pltpu.VMEM),),
        input_output_aliases={i: 2 + i for i in range(2 * n)},
        compiler_params=pltpu.CompilerParams(has_side_effects=DATAFLOW),
    )(*[pltpu.with_memory_space_constraint(h, pltpu.HBM) for h in hs],
      *[pltpu.with_memory_space_constraint(l, pltpu.HBM) for l in lands])
    return outs[0], outs[1], outs[2:2 + n], outs[2 + n:2 + 2 * n], outs[-1]


def _all_to_all_wait(send_sems, recv_sems, hs_thru, lands_thru, after, name):
    n = len(hs_thru)

    def body(*refs):
        h_refs, land_refs = refs[:n], refs[n:2 * n]
        send_sems, recv_sems = refs[2 * n], refs[2 * n + 1]
        ic = lax.axis_index("c")
        for a in range(n):
            for k in range(1, 4):
                px, py = _chip_peer(k)
                cp = pltpu.make_async_remote_copy(
                    src_ref=h_refs[a].at[2 * px + py], dst_ref=land_refs[a].at[2 * px + py],
                    send_sem=send_sems.at[3 * a + k - 1], recv_sem=recv_sems.at[3 * a + k - 1],
                    device_id=(px, py, ic), device_id_type=MESH)
                cp.wait_send()
                cp.wait_recv()

    outs = pl.pallas_call(
        body, name=name,
        out_shape=tuple(pltpu.HBM(h.shape, h.dtype) for h in hs_thru)
        + tuple(pltpu.HBM(l.shape, l.dtype) for l in lands_thru),
        in_specs=[HBM] * (2 * n) + [SEM, SEM, ANY], out_specs=(HBM,) * (2 * n),
        input_output_aliases={i: i for i in range(2 * n)},
        compiler_params=pltpu.CompilerParams(has_side_effects=DATAFLOW),
    )(*hs_thru, *lands_thru, send_sems, recv_sems, after)
    return outs[:n], outs[n:]


def _sibling_swap(gs, name):
    n = len(gs)

    def body(*refs):
        g_refs, recv_refs = refs[:n], refs[n:2 * n]
        send_sems, recv_sems = refs[2 * n:]
        ix, iy, ic = lax.axis_index("x"), lax.axis_index("y"), lax.axis_index("c")

        def block(g_ref, q):
            if len(g_ref.shape) == 4:
                return g_ref.at[q, 1 - ic]
            cw = g_ref.shape[1] // N_DEV
            return g_ref.at[:, pl.ds(pl.multiple_of((2 * q + 1 - ic) * cw, 128), cw)]

        cps = []
        for a in range(n):
            for q in range(4):
                cps.append(pltpu.make_async_remote_copy(
                    src_ref=block(g_refs[a], q), dst_ref=recv_refs[a].at[q],
                    send_sem=send_sems.at[4 * a + q], recv_sem=recv_sems.at[4 * a + q],
                    device_id=(ix, iy, 1 - ic), device_id_type=MESH))
        for cp in cps:
            cp.start()
        for cp in cps:
            cp.wait()

    return pl.pallas_call(
        body, name=name,
        out_shape=tuple(jax.ShapeDtypeStruct(
            (4,) + (g.shape[2:] if g.ndim == 4 else (g.shape[0], g.shape[1] // N_DEV)), g.dtype) for g in gs),
        in_specs=[ANY] * n, out_specs=tuple([ANY] * n),
        scratch_shapes=[pltpu.SemaphoreType.DMA((4 * n,)), pltpu.SemaphoreType.DMA((4 * n,))],
    )(*gs)


def _chip_all_to_all(hs, name):
    n = len(hs)

    def body(*refs):
        h_refs, out_refs = refs[:n], refs[n:2 * n]
        send_sems, recv_sems, local_sems = refs[2 * n:]
        ix, iy, ic = lax.axis_index("x"), lax.axis_index("y"), lax.axis_index("c")
        me = 2 * ix + iy
        peers = [(1 - ix, iy), (ix, 1 - iy), (1 - ix, 1 - iy)]
        mine = [pltpu.make_async_copy(h_refs[a].at[me], out_refs[a].at[me], local_sems.at[a]) for a in range(n)]
        for cp in mine:
            cp.start()

        def copy(a, k, src_slot, dst_slot, peer):
            return pltpu.make_async_remote_copy(
                src_ref=h_refs[a].at[src_slot], dst_ref=out_refs[a].at[dst_slot],
                send_sem=send_sems.at[3 * a + k], recv_sem=recv_sems.at[3 * a + k],
                device_id=(*peer, ic), device_id_type=MESH)

        sends = [copy(a, k, 2 * px + py, me, (px, py)) for a in range(n) for k, (px, py) in enumerate(peers)]
        for cp in sends:
            cp.start()
        for a in range(n):
            for k, (px, py) in enumerate(peers):
                copy(a, k, 2 * px + py, 2 * px + py, (px, py)).wait_recv()
        for cp in sends:
            cp.wait_send()
        for cp in mine:
            cp.wait()

    return pl.pallas_call(
        body, name=name, out_shape=tuple(jax.ShapeDtypeStruct(h.shape, h.dtype) for h in hs),
        in_specs=[ANY] * n, out_specs=tuple([ANY] * n),
        scratch_shapes=[pltpu.SemaphoreType.DMA((3 * n,)), pltpu.SemaphoreType.DMA((3 * n,)),
                        pltpu.SemaphoreType.DMA((n,))],
    )(*hs)


def _add_halves(g, recv, core, name):
    _, r, c = recv.shape
    br = _row_block(r, 512)
    stacked = g.ndim == 4

    def body(core_ref, g_ref, r_ref, o_ref):
        o_ref[0] = ((g_ref[0, 0] if stacked else g_ref[...]) + r_ref[0]).astype(BF16)

    spec = pl.BlockSpec((1, br, c), lambda i, j, core_ref: (i, j, 0))
    if stacked:
        g_spec = pl.BlockSpec((1, 1, br, c), lambda i, j, core_ref: (i, core_ref[0], j, 0))
    else:
        g_spec = pl.BlockSpec((br, c), lambda i, j, core_ref: (j, 2 * i + core_ref[0]))
    return pl.pallas_call(
        body, name=name, out_shape=jax.ShapeDtypeStruct(recv.shape, BF16),
        grid_spec=pltpu.PrefetchScalarGridSpec(
            num_scalar_prefetch=1, grid=(4, r // br), in_specs=[g_spec, spec], out_specs=spec),
        compiler_params=_params(32),
    )(core, g, recv)


def _adamw(parts, w, m, v, name):
    n_parts, r, c = parts.shape
    br = _row_block(r, 512 if c <= 1024 else 256)

    def body(p_ref, w_ref, m_ref, v_ref, g_out, d_out, m_out, v_out):
        g = p_ref[0].astype(F32)
        for p in range(1, n_parts):
            g = g + p_ref[p].astype(F32)
        m2 = B1 * m_ref[...] + (1.0 - B1) * g
        v2 = B2 * v_ref[...] + (1.0 - B2) * (g * g)
        m_hat = m2 / (1.0 - B1 ** STEP)
        v_hat = v2 / (1.0 - B2 ** STEP)
        g_out[...] = g
        d_out[...] = -LR * (m_hat / (jnp.sqrt(v_hat) + AEPS) + WD * w_ref[...])
        m_out[...] = m2
        v_out[...] = v2

    spec = pl.BlockSpec((br, c), lambda i: (i, 0))
    out = jax.ShapeDtypeStruct((r, c), F32)
    return pl.pallas_call(
        body, name=name, out_shape=(out, out, out, out), grid=(r // br,),
        in_specs=[pl.BlockSpec((n_parts, br, c), lambda i: (0, i, 0)), spec, spec, spec],
        out_specs=(spec, spec, spec, spec), compiler_params=_params(40),
    )(parts, w, m, v)


def _atb(a, b, name):
    t, k1 = a.shape
    k2 = b.shape[1]
    bt = math.gcd(t, 2048)

    def pick(k):
        for cand in (1024, 768, 512, 384, 256, 128):
            if k % cand == 0:
                return cand
        return k

    b1, b2 = pick(k1), pick(k2)

    def body(a_ref, b_ref, o_ref):
        @pl.when(pl.program_id(2) == 0)
        def _():
            o_ref[...] = jnp.zeros_like(o_ref)
        o_ref[...] += _mm_tn(a_ref[...], b_ref[...])

    return pl.pallas_call(
        body, name=name, out_shape=jax.ShapeDtypeStruct((k1, k2), F32), grid=(k1 // b1, k2 // b2, t // bt),
        in_specs=[pl.BlockSpec((bt, b1), lambda i, j, k: (k, i)), pl.BlockSpec((bt, b2), lambda i, j, k: (k, j))],
        out_specs=pl.BlockSpec((b1, b2), lambda i, j, k: (i, j)), compiler_params=_params(48),
    )(a, b)


def _mod_fwd(c_all, w_ada, b_cols):
    def body(c_ref, w_ref, b_ref, o_ref):
        cc = c_ref[...]
        cond = cc * _sigmoid(cc)
        o_ref[...] = _mm(cond, w_ref[...]) + b_ref[...]

    return pl.pallas_call(body, name="mod_fwd", out_shape=jax.ShapeDtypeStruct((c_all.shape[0], w_ada.shape[1]), F32),
                          compiler_params=_params(32))(c_all, w_ada, b_cols)


def _mod_bwd(c_all, dmod_cols, dmod_all):
    def body(c_ref, dc_ref, da_ref, gw_ref, gb_ref):
        cc = c_ref[...]
        cond = cc * _sigmoid(cc)
        gw_ref[...] = _mm_tn(cond, dc_ref[...])
        gb_ref[...] = jnp.sum(da_ref[...], axis=0, keepdims=True)

    return pl.pallas_call(
        body, name="mod_bwd",
        out_shape=(jax.ShapeDtypeStruct((D_MODEL, dmod_cols.shape[1]), F32), jax.ShapeDtypeStruct((1, dmod_all.shape[1]), F32)),
        compiler_params=_params(32))(c_all, dmod_cols, dmod_all)


def _load_once(hbm_ref, vmem_ref, sem):
    @pl.when(pl.program_id(0) == 0)
    def _():
        cp = pltpu.make_async_copy(hbm_ref, vmem_ref, sem)
        cp.start()
        cp.wait()


def _conv_taps(win_ref, w, tb, cols):
    shifted = [win_ref[8 - j:8 - j + tb, cols] for j in range(4)]
    acc = w[3:4] * shifted[0]
    for j in (1, 2, 3):
        acc = acc + w[3 - j:4 - j] * shifted[j]
    return acc, shifted


def _proj_conv_fwd(x2, mod3, w_in_pad, conv_w, conv_b, seq):
    t = x2.shape[0]
    tb = 256
    npb = seq // tb
    cw = 512

    def body(x_ref, mod_ref, w_hbm, cw_ref, cb_ref, z_ref, pre_ref, xbc_ref, dsilu_ref, dt_ref, u5_ref, w_vmem, win, sem):
        _load_once(w_hbm, w_vmem, sem)
        first = (pl.program_id(0) % npb) == 0

        @pl.when(first)
        def _():
            win[0:8, :] = jnp.zeros((8, D_XBC), F32)

        @pl.when(jnp.logical_not(first))
        def _():
            win[0:8, :] = win[tb:tb + 8, :]

        m = mod_ref[0]
        u = (x_ref[...] * (1.0 + m[1:2]) + m[0:1]).astype(BF16)
        z_ref[...] = jnp.dot(u, w_vmem[:, 0:D_SSD], preferred_element_type=F32)
        dt_ref[...] = jnp.dot(u, w_vmem[:, D_SSD + D_XBC:D_SSD + D_XBC + DT_PAD], preferred_element_type=F32)
        u5_ref[...] = jnp.dot(u, w_vmem[:, D_SSD + D_XBC + DT_PAD:], preferred_element_type=F32)
        for k in range(D_XBC // cw):
            cols = slice(k * cw, (k + 1) * cw)
            pre_k = jnp.dot(u, w_vmem[:, D_SSD + k * cw:D_SSD + (k + 1) * cw], preferred_element_type=F32)
            win[8:8 + tb, cols] = pre_k
            pre_ref[:, cols] = pre_k
            conv, _ = _conv_taps(win, cw_ref[:, cols], tb, cols)
            conv = conv + cb_ref[:, cols]
            sg = _sigmoid(conv)
            xbc_ref[:, cols] = conv * sg
            dsilu_ref[:, cols] = sg * (1.0 + conv * (1.0 - sg))

    row = lambda w: pl.BlockSpec((tb, w), lambda i: (i, 0))
    return pl.pallas_call(
        body, name="proj_conv_fwd", grid=(t // tb,),
        out_shape=(jax.ShapeDtypeStruct((t, D_SSD), F32), jax.ShapeDtypeStruct((t, D_XBC), F32),
                   jax.ShapeDtypeStruct((t, D_XBC), F32), jax.ShapeDtypeStruct((t, D_XBC), F32),
                   jax.ShapeDtypeStruct((t, DT_PAD), F32), jax.ShapeDtypeStruct((t, D_S5), F32)),
        in_specs=[row(D_MODEL), pl.BlockSpec((1, N_MOD, D_MODEL), lambda i: (i // npb, 0, 0)), ANY,
                  pl.BlockSpec((4, D_XBC), lambda i: (0, 0)), pl.BlockSpec((1, D_XBC), lambda i: (0, 0))],
        out_specs=(row(D_SSD), row(D_XBC), row(D_XBC), row(D_XBC), row(DT_PAD), row(D_S5)),
        scratch_shapes=[pltpu.VMEM((D_MODEL, D_INP), BF16), pltpu.VMEM((tb + 8, D_XBC), F32), pltpu.SemaphoreType.DMA],
        compiler_params=_params(56),
    )(x2, mod3, w_in_pad, conv_w, conv_b)


N_PAIRS = N_HEADS // 2


def _split3(x):
    hi = x.astype(BF16)
    r = x - hi.astype(F32)
    mid = r.astype(BF16)
    lo = (r - mid.astype(F32)).astype(BF16)
    return hi, mid, lo


def _dot3(x, e, dims=(((1,), (0,)), ((), ()))):
    return sum(lax.dot_general(p, e, dims, preferred_element_type=F32) for p in _split3(x))


def _dot3_left(e, x, dims=(((1,), (0,)), ((), ()))):
    return sum(lax.dot_general(e, p, dims, preferred_element_type=F32) for p in _split3(x))


def _head_fold():
    return (jnp.arange(D_SSD)[:, None] // HEADDIM == jnp.arange(128)[None, :]).astype(BF16)


def _ssd_prep(dt_raw, par):
    dtb = par[0:1]
    a = -jnp.exp(par[1:2])
    dt = _softplus(dt_raw + dtb)
    adt = dt * a
    row = lax.broadcasted_iota(jnp.int32, (CHUNK, CHUNK), 0)
    col = lax.broadcasted_iota(jnp.int32, (CHUNK, CHUNK), 1)
    causal = row >= col
    tri = causal.astype(BF16)
    cs = _dot3_left(tri, adt)
    left = col < HEADDIM

    def lanes(v, h):
        return jnp.broadcast_to(v[:, h:h + 1], (CHUNK, 128))

    dt_c, cs_c, pair_cols = [], [], []
    for p in range(N_PAIRS):
        c0, c1 = lanes(cs, 2 * p), lanes(cs, 2 * p + 1)
        pair_cols.append(jnp.concatenate([c0, c1], axis=1))
        cs_c.append(jnp.where(left, c0, c1))
        dt_c.append(jnp.where(left, lanes(dt, 2 * p), lanes(dt, 2 * p + 1)))
    cs_c = jnp.concatenate(cs_c, axis=1)
    dt_c = jnp.concatenate(dt_c, axis=1)
    return dt, a, cs, cs.T, causal, tri, dt_c, jnp.exp(cs_c), jnp.exp(cs_c[CHUNK - 1:CHUNK, :] - cs_c), pair_cols


def _pair_decay(cols, cst, pair, causal2):
    rows = jnp.concatenate([jnp.broadcast_to(cst[2 * pair:2 * pair + 1, :], (CHUNK, CHUNK)),
                            jnp.broadcast_to(cst[2 * pair + 1:2 * pair + 2, :], (CHUNK, CHUNK))], axis=1)
    return jnp.exp(jnp.where(causal2, cols - rows, -jnp.inf))


def _stack_heads(xp, left):
    return jnp.concatenate([jnp.where(left, xp, 0.0), jnp.where(left, 0.0, xp)], axis=0).astype(BF16)


def _ssd_fwd(xbc, z, dt_raw, par, dsk, normw, seq):
    t = xbc.shape[0]
    nc = seq // CHUNK
    n_chunks = t // CHUNK

    def body(xbc_ref, z_ref, dt_ref, par_ref, dsk_ref, nw_ref, yraw_ref, ycat_ref, hprev_ref, h_ref):
        @pl.when(pl.program_id(0) % nc == 0)
        def _():
            h_ref[...] = jnp.zeros_like(h_ref)
        hprev_ref[0] = h_ref[...]
        _, _, cs, cst, causal, _, dt_c, ecs_c, w_c, pair_cols = _ssd_prep(dt_ref[...], par_ref[...])
        cs_last = cs[CHUNK - 1:CHUNK, :]
        causal2 = jnp.concatenate([causal, causal], axis=1)
        left = lax.broadcasted_iota(jnp.int32, (CHUNK, 128), 1) < HEADDIM
        x = xbc_ref[:, 0:D_SSD]
        xdt = x * dt_c
        amat = (w_c * xdt).astype(BF16)
        zz = z_ref[...]
        silu_z = zz * _sigmoid(zz)
        for g in range(N_GROUPS):
            gs = slice(g * GW, (g + 1) * GW)
            bg = xbc_ref[:, D_SSD + g * N_STATE:D_SSD + (g + 1) * N_STATE].astype(BF16)
            cg = xbc_ref[:, D_SSD + (N_GROUPS + g) * N_STATE:D_SSD + (N_GROUPS + g + 1) * N_STATE].astype(BF16)
            scores = lax.dot_general(cg, bg, NT, preferred_element_type=F32)
            scores2 = jnp.concatenate([scores, scores], axis=1)
            hg = h_ref[gs, :]
            p_all = lax.dot_general(cg, hg.astype(BF16), NT, preferred_element_type=F32)
            ys = []
            for q in range(GW // 128):
                pair = g * (GW // 128) + q
                decay = _pair_decay(pair_cols[pair], cst, pair, causal2)
                mcat = (scores2 * decay).astype(BF16)
                ys.append(jnp.dot(mcat, _stack_heads(xdt[:, pair * 128:(pair + 1) * 128], left),
                                  preferred_element_type=F32))
            yg = jnp.concatenate(ys, axis=1) + ecs_c[:, gs] * p_all + x[:, gs] * dsk_ref[:, gs]
            s_new = lax.dot_general(amat[:, gs], bg, TN, preferred_element_type=F32)
            for j in range(HPG):
                hh = g * HPG + j
                js = slice(j * HEADDIM, (j + 1) * HEADDIM)
                h_ref[g * GW + j * HEADDIM:g * GW + (j + 1) * HEADDIM, :] = (
                    hg[js, :] * jnp.exp(cs_last[:, hh:hh + 1]) + s_new[js, :])
            yraw_ref[:, gs] = yg
            v = yg * silu_z[:, gs]
            r = lax.rsqrt(jnp.mean(v * v, axis=-1, keepdims=True) + EPS)
            ycat_ref[:, gs] = (v * r * nw_ref[:, gs]).astype(BF16)

    row = lambda w: pl.BlockSpec((CHUNK, w), lambda i: (i, 0))
    full = lambda s: pl.BlockSpec(s, lambda i: (0,) * len(s))
    return pl.pallas_call(
        body, name="ssd_fwd", grid=(n_chunks,),
        out_shape=(jax.ShapeDtypeStruct((t, D_SSD), F32), jax.ShapeDtypeStruct((t, D_SSD + D_S5), BF16),
                   jax.ShapeDtypeStruct((n_chunks, D_SSD, N_STATE), F32)),
        in_specs=[row(D_XBC), row(D_SSD), row(DT_PAD), full((8, 128)), full((1, D_SSD)), full((1, D_SSD))],
        out_specs=(row(D_SSD), row(D_SSD), pl.BlockSpec((1, D_SSD, N_STATE), lambda i: (i, 0, 0))),
        scratch_shapes=[pltpu.VMEM((D_SSD, N_STATE), F32)],
        compiler_params=_params(40),
    )(xbc, z, dt_raw, par, dsk, normw)


S5_CW = 512
S5_BLOCKS = 4


def _tile_scan(in_re, in_im, out_re, out_im, carry_re, carry_im, pw_re, pw_im, n_tiles, reverse):
    steps = (1, 2, 4)
    for cc in range(S5_N // S5_CW):
        cols = slice(cc * S5_CW, (cc + 1) * S5_CW)
        a_re, a_im = pw_re[:, cols], pw_im[:, cols]
        rid = lax.broadcasted_iota(jnp.int32, (8, S5_CW), 0)
        pows = []
        for d in steps:
            k = 8 - d if reverse else d - 1
            keep = (rid < 8 - d) if reverse else (rid >= d)
            pows.append((jnp.where(keep, pw_re[k:k + 1, cols], 0.0), jnp.where(keep, pw_im[k:k + 1, cols], 0.0)))

        def tile(i, carry, cols=cols, pows=pows, a_re=a_re, a_im=a_im):
            r = (n_tiles - 1 - i) if reverse else i
            rows = pl.ds(pl.multiple_of(r * 8, 8), 8)
            xr, xi = in_re[rows, cols], in_im[rows, cols]
            for (pr, pi), d in zip(pows, steps):
                shift = 8 - d if reverse else d
                sr, si = pltpu.roll(xr, shift, axis=0), pltpu.roll(xi, shift, axis=0)
                xr, xi = xr + pr * sr - pi * si, xi + pr * si + pi * sr
            cr, ci = carry
            xr, xi = xr + a_re * cr - a_im * ci, xi + a_re * ci + a_im * cr
            out_re[rows, cols] = xr
            out_im[rows, cols] = xi
            edge = slice(0, 1) if reverse else slice(7, 8)
            return (jnp.broadcast_to(xr[edge], (8, S5_CW)), jnp.broadcast_to(xi[edge], (8, S5_CW)))

        c0 = (jnp.broadcast_to(carry_re[0:1, cols], (8, S5_CW)), jnp.broadcast_to(carry_im[0:1, cols], (8, S5_CW)))
        cr, ci = lax.fori_loop(0, n_tiles, tile, c0, unroll=True)
        carry_re[:, cols] = cr
        carry_im[:, cols] = ci


def _s5_params_math(ar, ai, ldt, br, bi):
    dt = jnp.exp(ldt)
    mag = jnp.exp(ar * dt)
    ang = ai * dt
    ab_re = mag * jnp.cos(ang)
    ab_im = mag * jnp.sin(ang)
    den = ar * ar + ai * ai
    n_re = ab_re - 1.0
    coef_re = (n_re * ar + ab_im * ai) / den
    coef_im = (ab_im * ar - n_re * ai) / den
    bb_re = coef_re * br - coef_im * bi
    bb_im = coef_re * bi + coef_im * br
    return ab_re, ab_im, bb_re, bb_im


def _s5_params_fwd(ar, ai, ldt, br, bi):
    def body(ar_ref, ai_ref, ldt_ref, br_ref, bi_ref, bbr_ref, bbi_ref, pfr_ref, pfi_ref, prr_ref, pri_ref):
        ab_re, ab_im, bb_re, bb_im = _s5_params_math(ar_ref[...], ai_ref[...], ldt_ref[...], br_ref[...], bi_ref[...])
        bbr_ref[...] = bb_re
        bbi_ref[...] = bb_im
        pr, pi = ab_re, ab_im
        for k in range(8):
            pfr_ref[k:k + 1, :] = pr
            pfi_ref[k:k + 1, :] = pi
            prr_ref[7 - k:8 - k, :] = pr
            pri_ref[7 - k:8 - k, :] = -pi
            pr, pi = pr * ab_re - pi * ab_im, pr * ab_im + pi * ab_re

    b16 = jax.ShapeDtypeStruct((S5_CH, S5_N), F32)
    p8 = jax.ShapeDtypeStruct((8, S5_N), F32)
    return pl.pallas_call(body, name="s5_params_fwd", out_shape=(b16, b16, p8, p8, p8, p8),
                          compiler_params=_params(32))(ar, ai, ldt, br, bi)


def _s5_params_bwd(ar, ai, ldt, br, bi, d_ab_re, d_ab_im, d_bb_re, d_bb_im):
    def body(ar_ref, ai_ref, ldt_ref, br_ref, bi_ref, dar_ref, dai_ref, dbr_ref, dbi_ref,
             gar_ref, gai_ref, gldt_ref, gbr_ref, gbi_ref):
        _, vjp = jax.vjp(_s5_params_math, ar_ref[...], ai_ref[...], ldt_ref[...], br_ref[...], bi_ref[...])
        g_ar, g_ai, g_ldt, g_br, g_bi = vjp((dar_ref[...], dai_ref[...], dbr_ref[...], dbi_ref[...]))
        gar_ref[...] = g_ar
        gai_ref[...] = g_ai
        gbr_ref[...] = g_br
        gbi_ref[...] = g_bi
        lane = lax.broadcasted_iota(jnp.int32, (S5_N, 128), 0) // S5_P
        grp = lax.broadcasted_iota(jnp.int32, (S5_N, 128), 1)
        fold = (lane == grp).astype(F32)
        gldt_ref[...] = jnp.dot(g_ldt, fold, preferred_element_type=F32, precision=HIGHEST)

    v1 = jax.ShapeDtypeStruct((1, S5_N), F32)
    b16 = jax.ShapeDtypeStruct((S5_CH, S5_N), F32)
    return pl.pallas_call(body, name="s5_params_bwd",
                          out_shape=(v1, v1, jax.ShapeDtypeStruct((1, 128), F32), b16, b16),
                          compiler_params=_params(32))(ar, ai, ldt, br, bi, d_ab_re, d_ab_im, d_bb_re, d_bb_im)


def _s5_fwd(u5, bb_re, bb_im, cc_re, cc_im, pf_re, pf_im, s5d, w_glu, b_glu, ycat, seq):
    t = u5.shape[0]
    tb = 256
    npb = seq // tb

    def body(u_ref, bbr_ref, bbi_ref, ccr_ref, cci_ref, pfr_ref, pfi_ref, d_ref, wg_ref, bg_ref, ycat_hbm,
             sre_ref, sim_ref, ypre_ref, y5_ref, bur, bui, car, cai):
        del ycat_hbm

        @pl.when(pl.program_id(0) % npb == 0)
        def _():
            car[...] = jnp.zeros_like(car)
            cai[...] = jnp.zeros_like(cai)
        u = u_ref[...]
        ub = u.astype(BF16)
        for j in range(S5_BLOCKS):
            ch, st = slice(j * 128, (j + 1) * 128), slice(j * 512, (j + 1) * 512)
            bur[:, st] = jnp.dot(ub[:, ch], bbr_ref[j], preferred_element_type=F32)
            bui[:, st] = jnp.dot(ub[:, ch], bbi_ref[j], preferred_element_type=F32)
        _tile_scan(bur, bui, sre_ref, sim_ref, car, cai, pfr_ref, pfi_ref, tb // 8, reverse=False)
        cs_y = []
        for j in range(S5_BLOCKS):
            st = slice(j * 512, (j + 1) * 512)
            cs_y.append(_mm(sre_ref[:, st], ccr_ref[j]) - _mm(sim_ref[:, st], cci_ref[j]))
        ypre = jnp.concatenate(cs_y, axis=1) + u * d_ref[...]
        ypre_ref[...] = ypre
        yg = _gelu(ypre)
        y5_ref[...] = (yg * _sigmoid(_mm(yg, wg_ref[...]) + bg_ref[...])).astype(BF16)

    row = lambda w: pl.BlockSpec((tb, w), lambda i: (i, 0))
    full = lambda a: pl.BlockSpec(a.shape, lambda i: (0,) * a.ndim)
    return pl.pallas_call(
        body, name="s5_fwd", grid=(t // tb,),
        out_shape=(jax.ShapeDtypeStruct((t, S5_N), F32), jax.ShapeDtypeStruct((t, S5_N), F32),
                   jax.ShapeDtypeStruct((t, D_S5), F32), jax.ShapeDtypeStruct(ycat.shape, BF16)),
        in_specs=[row(D_S5), full(bb_re), full(bb_im), full(cc_re), full(cc_im), full(pf_re), full(pf_im),
                  full(s5d), full(w_glu), full(b_glu), ANY],
        out_specs=(row(S5_N), row(S5_N), row(D_S5), pl.BlockSpec((tb, D_S5), lambda i: (i, D_SSD // D_S5))),
        input_output_aliases={10: 3},
        scratch_shapes=[pltpu.VMEM((tb, S5_N), F32), pltpu.VMEM((tb, S5_N), F32),
                        pltpu.VMEM((8, S5_N), F32), pltpu.VMEM((8, S5_N), F32)],
        compiler_params=_params(48),
    )(u5, bb_re, bb_im, cc_re, cc_im, pf_re, pf_im, s5d, w_glu, b_glu, ycat)


def _layer_norm(r, g, b):
    mu = jnp.mean(r, axis=-1, keepdims=True)
    xc = r - mu
    rstd = lax.rsqrt(jnp.mean(xc * xc, axis=-1, keepdims=True) + EPS)
    xhat = xc * rstd
    return xhat * g + b, xhat, rstd


def _layer_norm_bwd(dy, xhat, rstd, g):
    dxhat = dy * g
    return rstd * (dxhat - jnp.mean(dxhat, axis=-1, keepdims=True)
                   - xhat * jnp.mean(dxhat * xhat, axis=-1, keepdims=True))


def _out_ln1(ycat, x2, mod3, w_out, ln1, seq):
    t = x2.shape[0]
    tb = 512
    npb = seq // tb

    def body(y_ref, x_ref, mod_ref, w_ref, ln_ref, mix_ref, x1_ref):
        m = mod_ref[0]
        mix = jnp.dot(y_ref[...], w_ref[...], preferred_element_type=F32)
        mix_ref[...] = mix
        r1 = ALPHA * x_ref[...] + (1.0 + m[2:3]) * mix
        x1_ref[...] = _layer_norm(r1, ln_ref[0:1], ln_ref[1:2])[0]

    row = lambda w: pl.BlockSpec((tb, w), lambda i: (i, 0))
    return pl.pallas_call(
        body, name="out_ln1", grid=(t // tb,),
        out_shape=(jax.ShapeDtypeStruct((t, D_MODEL), F32), jax.ShapeDtypeStruct((t, D_MODEL), F32)),
        in_specs=[row(D_SSD + D_S5), row(D_MODEL), pl.BlockSpec((1, N_MOD, D_MODEL), lambda i: (i // npb, 0, 0)),
                  pl.BlockSpec(w_out.shape, lambda i: (0, 0)), pl.BlockSpec(ln1.shape, lambda i: (0, 0))],
        out_specs=(row(D_MODEL), row(D_MODEL)), compiler_params=_params(48),
    )(ycat, x2, mod3, w_out, ln1)


def _mlp_fwd_bwd(x1, tgt, mod3, w1, w2, vec1, b1, seq):
    t = x1.shape[0]
    tb = 256
    npb = seq // tb
    n_fb, _, fb = w1.shape

    def body(x1_ref, tgt_ref, mod_ref, w1_hbm, w2_hbm, v_ref, b1_ref,
             dx1_ref, u2_ref, h_ref, dhp_ref, do_ref, gacc_ref, db1_ref, bacc_ref, w1_v, w2_v, sem1, sem2):
        i = pl.program_id(0)
        @pl.when(i == 0)
        def _():
            cps = [pltpu.make_async_copy(w1_hbm.at[k], w1_v.at[:, k * fb:(k + 1) * fb], sem1.at[k])
                   for k in range(n_fb)]
            for cp in cps:
                cp.start()
            for cp in cps:
                cp.wait()
        _load_once(w2_hbm, w2_v, sem2)

        @pl.when(i == 0)
        def _():
            gacc_ref[...] = jnp.zeros_like(gacc_ref)
            db1_ref[...] = jnp.zeros_like(db1_ref)

        @pl.when(i % npb == 0)
        def _():
            bacc_ref[...] = jnp.zeros_like(bacc_ref)

        m = mod_ref[0]
        sh2, sc2, g2 = m[3:4], m[4:5], m[5:6]
        x1v = x1_ref[...]
        u2 = (x1v * (1.0 + sc2) + sh2).astype(BF16)
        u2_ref[...] = u2
        hr = jnp.maximum(jnp.dot(u2, w1_v[...], preferred_element_type=F32) + b1_ref[...], 0.0)
        hb = (hr * hr).astype(BF16)
        h_ref[...] = hb
        o = jnp.dot(hb, w2_v[...], preferred_element_type=F32) + v_ref[0:1]
        r2 = ALPHA * x1v + (1.0 + g2) * o
        y, xhat, rstd = _layer_norm(r2, v_ref[1:2], v_ref[2:3])
        err = y - tgt_ref[...]
        dy = err * (1.0 / D_MODEL)
        dr2 = _layer_norm_bwd(dy, xhat, rstd, v_ref[1:2])
        do = (1.0 + g2) * dr2
        dob = do.astype(BF16)
        do_ref[...] = dob
        gacc_ref[0:1, :] += jnp.sum(dy * xhat, axis=0, keepdims=True)
        gacc_ref[1:2, :] += jnp.sum(dy, axis=0, keepdims=True)
        gacc_ref[2:3, :] += jnp.sum(do, axis=0, keepdims=True)
        gacc_ref[3:4, :] += jnp.sum(err * err, axis=0, keepdims=True)
        dhpre = lax.dot_general(dob, w2_v[...], NT, preferred_element_type=F32) * (2.0 * hr)
        dhpb = dhpre.astype(BF16)
        dhp_ref[...] = dhpb
        db1_ref[...] += jnp.sum(dhpre, axis=0, keepdims=True)
        du2 = lax.dot_general(dhpb, w1_v[...], NT, preferred_element_type=F32)
        dx1_ref[...] = ALPHA * dr2 + du2 * (1.0 + sc2)
        bacc_ref[0, 0:1, :] += jnp.sum(du2, axis=0, keepdims=True)
        bacc_ref[0, 1:2, :] += jnp.sum(du2 * x1v, axis=0, keepdims=True)
        bacc_ref[0, 2:3, :] += jnp.sum(dr2 * o, axis=0, keepdims=True)

    row = lambda w: pl.BlockSpec((tb, w), lambda i: (i, 0))
    return pl.pallas_call(
        body, name="mlp_fwd_bwd", grid=(t // tb,),
        out_shape=(jax.ShapeDtypeStruct((t, D_MODEL), F32), jax.ShapeDtypeStruct((t, D_MODEL), BF16),
                   jax.ShapeDtypeStruct((t, D_FF), BF16), jax.ShapeDtypeStruct((t, D_FF), BF16),
                   jax.ShapeDtypeStruct((t, D_MODEL), BF16), jax.ShapeDtypeStruct((8, D_MODEL), F32),
                   jax.ShapeDtypeStruct((1, D_FF), F32), jax.ShapeDtypeStruct((t // seq, 8, D_MODEL), F32)),
        in_specs=[row(D_MODEL), row(D_MODEL), pl.BlockSpec((1, N_MOD, D_MODEL), lambda i: (i // npb, 0, 0)), ANY, ANY,
                  pl.BlockSpec(vec1.shape, lambda i: (0, 0)), pl.BlockSpec(b1.shape, lambda i: (0, 0))],
        out_specs=(row(D_MODEL), row(D_MODEL), row(D_FF), row(D_FF), row(D_MODEL),
                   pl.BlockSpec((8, D_MODEL), lambda i: (0, 0)), pl.BlockSpec((1, D_FF), lambda i: (0, 0)),
                   pl.BlockSpec((1, 8, D_MODEL), lambda i: (i // npb, 0, 0))),
        scratch_shapes=[pltpu.VMEM((D_MODEL, n_fb * fb), BF16), pltpu.VMEM((D_FF, D_MODEL), BF16),
                        pltpu.SemaphoreType.DMA((n_fb,)), pltpu.SemaphoreType.DMA],
        compiler_params=_params(60),
    )(x1, tgt, mod3, w1, w2, vec1, b1)


def _ln1_out_bwd(dx1, x2, mix, mod3, w_out, ln1, seq):
    t = x2.shape[0]
    tb = 512
    npb = seq // tb

    def body(dx1_ref, x_ref, mix_ref, mod_ref, w_ref, ln_ref, dmix_ref, dxa_ref, dys_ref, dy5_ref, gacc_ref, bacc_ref):
        i = pl.program_id(0)

        @pl.when(i == 0)
        def _():
            gacc_ref[...] = jnp.zeros_like(gacc_ref)

        @pl.when(i % npb == 0)
        def _():
            bacc_ref[...] = jnp.zeros_like(bacc_ref)

        m = mod_ref[0]
        mix = mix_ref[...]
        r1 = ALPHA * x_ref[...] + (1.0 + m[2:3]) * mix
        _, xhat, rstd = _layer_norm(r1, ln_ref[0:1], ln_ref[1:2])
        dx1v = dx1_ref[...]
        dr1 = _layer_norm_bwd(dx1v, xhat, rstd, ln_ref[0:1])
        gacc_ref[0:1, :] += jnp.sum(dx1v * xhat, axis=0, keepdims=True)
        gacc_ref[1:2, :] += jnp.sum(dx1v, axis=0, keepdims=True)
        bacc_ref[0, 0:1, :] += jnp.sum(dr1 * mix, axis=0, keepdims=True)
        dmix = ((1.0 + m[2:3]) * dr1).astype(BF16)
        dmix_ref[...] = dmix
        dxa_ref[...] = ALPHA * dr1
        dys_ref[...] = lax.dot_general(dmix, w_ref[0:D_SSD, :], NT, preferred_element_type=F32)
        dy5_ref[...] = lax.dot_general(dmix, w_ref[D_SSD:, :], NT, preferred_element_type=F32)

    row = lambda w: pl.BlockSpec((tb, w), lambda i: (i, 0))
    return pl.pallas_call(
        body, name="ln1_out_bwd", grid=(t // tb,),
        out_shape=(jax.ShapeDtypeStruct((t, D_MODEL), BF16), jax.ShapeDtypeStruct((t, D_MODEL), F32),
                   jax.ShapeDtypeStruct((t, D_SSD), F32), jax.ShapeDtypeStruct((t, D_S5), F32),
                   jax.ShapeDtypeStruct((8, D_MODEL), F32), jax.ShapeDtypeStruct((t // seq, 8, D_MODEL), F32)),
        in_specs=[row(D_MODEL), row(D_MODEL), row(D_MODEL), pl.BlockSpec((1, N_MOD, D_MODEL), lambda i: (i // npb, 0, 0)),
                  pl.BlockSpec(w_out.shape, lambda i: (0, 0)), pl.BlockSpec(ln1.shape, lambda i: (0, 0))],
        out_specs=(row(D_MODEL), row(D_MODEL), row(D_SSD), row(D_S5), pl.BlockSpec((8, D_MODEL), lambda i: (0, 0)),
                   pl.BlockSpec((1, 8, D_MODEL), lambda i: (i // npb, 0, 0))),
        compiler_params=_params(48),
    )(dx1, x2, mix, mod3, w_out, ln1)


def _s5_bwd(dy5, ypre, u5, s_re, s_im, bb_re, bb_im, cc_re, cc_im, pr_re, pr_im, s5d, w_glu, b_glu, seq):
    t = u5.shape[0]
    tb = 256
    npb = seq // tb
    n_blocks = t // tb

    def blk(i):
        return (i // npb) * npb + (npb - 1 - i % npb)

    def body(dy_ref, ypre_ref, u_ref, sre_ref, sim_ref, hre_ref, him_ref, bbr_ref, bbi_ref, ccr_ref, cci_ref,
             prr_ref, pri_ref, d_ref, wg_ref, bg_ref,
             du_ref, vacc_ref, sacc_ref, dcc_ref, dbb_ref, dwg_ref, dsr, dsi, gr, gi, car, cai):
        i = pl.program_id(0)

        @pl.when(i == 0)
        def _():
            for acc in (vacc_ref, sacc_ref, dcc_ref, dbb_ref, dwg_ref):
                acc[...] = jnp.zeros_like(acc)

        @pl.when(i % npb == 0)
        def _():
            car[...] = jnp.zeros_like(car)
            cai[...] = jnp.zeros_like(cai)

        dy = dy_ref[...]
        ypre = ypre_ref[...]
        u = u_ref[...]
        ub = u.astype(BF16)
        yg = _gelu(ypre)
        sg = _sigmoid(_mm(yg, wg_ref[...]) + bg_ref[...])
        dq = dy * yg * sg * (1.0 - sg)
        dqb = dq.astype(BF16)
        dyg = dy * sg + lax.dot_general(dqb, wg_ref[...], NT, preferred_element_type=F32)
        dyp = dyg * _gelu_grad(ypre)
        dypb = dyp.astype(BF16)
        dwg_ref[...] += lax.dot_general(yg.astype(BF16), dqb, TN, preferred_element_type=F32)
        blocks = [(slice(j * 128, (j + 1) * 128), slice(j * 512, (j + 1) * 512)) for j in range(S5_BLOCKS)]
        for j, (ch, st) in enumerate(blocks):
            dsr[:, st] = lax.dot_general(dypb[:, ch], ccr_ref[j], NT, preferred_element_type=F32)
            dsi[:, st] = -lax.dot_general(dypb[:, ch], cci_ref[j], NT, preferred_element_type=F32)
        _tile_scan(dsr, dsi, gr, gi, car, cai, prr_ref, pri_ref, tb // 8, reverse=True)
        g_re, g_im = gr[...], gi[...]
        first_rows = (i % npb) == npb - 1
        hre = jnp.where(first_rows, 0.0, hre_ref[...])
        him = jnp.where(first_rows, 0.0, him_ref[...])
        s_re_v, s_im_v = sre_ref[...], sim_ref[...]
        sp_re = pltpu.roll(jnp.concatenate([hre, s_re_v], axis=0), 1, axis=0)[8:8 + tb]
        sp_im = pltpu.roll(jnp.concatenate([him, s_im_v], axis=0), 1, axis=0)[8:8 + tb]
        vacc_ref[0:1, :] += jnp.sum(g_re * sp_re + g_im * sp_im, axis=0, keepdims=True)
        vacc_ref[1:2, :] += jnp.sum(g_im * sp_re - g_re * sp_im, axis=0, keepdims=True)
        grb, gib = g_re.astype(BF16), g_im.astype(BF16)
        srb, sib = s_re_v.astype(BF16), s_im_v.astype(BF16)
        du_cols = []
        for j, (ch, st) in enumerate(blocks):
            dcc_ref[j] += lax.dot_general(srb[:, st], dypb[:, ch], TN, preferred_element_type=F32)
            dcc_ref[S5_BLOCKS + j] -= lax.dot_general(sib[:, st], dypb[:, ch], TN, preferred_element_type=F32)
            dbb_ref[j] += lax.dot_general(ub[:, ch], grb[:, st], TN, preferred_element_type=F32)
            dbb_ref[S5_BLOCKS + j] += lax.dot_general(ub[:, ch], gib[:, st], TN, preferred_element_type=F32)
            du_cols.append(lax.dot_general(grb[:, st], bbr_ref[j], NT, preferred_element_type=F32)
                           + lax.dot_general(gib[:, st], bbi_ref[j], NT, preferred_element_type=F32))
        du_ref[...] = jnp.concatenate(du_cols, axis=1) + dyp * d_ref[...]
        sacc_ref[0:1, :] += jnp.sum(dyp * u, axis=0, keepdims=True)
        sacc_ref[1:2, :] += jnp.sum(dq, axis=0, keepdims=True)

    row = lambda w: pl.BlockSpec((tb, w), lambda i: (blk(i), 0))
    halo = pl.BlockSpec((8, S5_N), lambda i: (jnp.maximum(blk(i) * (tb // 8) - 1, 0), 0))
    full = lambda a: pl.BlockSpec(a.shape, lambda i: (0,) * a.ndim)
    acc = lambda s: pl.BlockSpec(s, lambda i: (0,) * len(s))
    acc_shapes = [(8, S5_N), (8, D_S5), (2 * S5_BLOCKS, 512, 128), (2 * S5_BLOCKS, 128, 512), (D_S5, D_S5)]
    return pl.pallas_call(
        body, name="s5_bwd", grid=(n_blocks,),
        out_shape=(jax.ShapeDtypeStruct((t, D_S5), F32),) + tuple(jax.ShapeDtypeStruct(s, F32) for s in acc_shapes),
        in_specs=[row(D_S5), row(D_S5), row(D_S5), row(S5_N), row(S5_N), halo, halo, full(bb_re), full(bb_im),
                  full(cc_re), full(cc_im), full(pr_re), full(pr_im), full(s5d), full(w_glu), full(b_glu)],
        out_specs=(row(D_S5),) + tuple(acc(s) for s in acc_shapes),
        scratch_shapes=[pltpu.VMEM((tb, S5_N), F32), pltpu.VMEM((tb, S5_N), F32), pltpu.VMEM((tb, S5_N), F32),
                        pltpu.VMEM((tb, S5_N), F32), pltpu.VMEM((8, S5_N), F32), pltpu.VMEM((8, S5_N), F32)],
        compiler_params=_params(56),
    )(dy5, ypre, u5, s_re, s_im, s_re, s_im, bb_re, bb_im, cc_re, cc_im, pr_re, pr_im, s5d, w_glu, b_glu)


def _ssd_bwd(dyssd, yraw, z, xbc, dt_raw, hprev, par, dsk, normw, seq):
    t = xbc.shape[0]
    nc = seq // CHUNK
    n_chunks = t // CHUNK
    fold = _head_fold()

    def blk(i):
        return (i // nc) * nc + (nc - 1 - i % nc)

    def body(dy_ref, yraw_ref, z_ref, xbc_ref, dt_ref, hprev_ref, par_ref, dsk_ref, nw_ref, fold_ref,
             dxbc_ref, dz_ref, ddt_ref, dpar_ref, cacc_ref, dh_ref, dyr_ref):
        i = pl.program_id(0)

        @pl.when(i == 0)
        def _():
            dpar_ref[...] = jnp.zeros_like(dpar_ref)
            cacc_ref[...] = jnp.zeros_like(cacc_ref)

        @pl.when(i % nc == 0)
        def _():
            dh_ref[...] = jnp.zeros_like(dh_ref)

        zz = z_ref[...]
        sz = _sigmoid(zz)
        silu_z = zz * sz
        yraw = yraw_ref[...]
        for g in range(N_GROUPS):
            sl = slice(g * GW, (g + 1) * GW)
            v = yraw[:, sl] * silu_z[:, sl]
            r = lax.rsqrt(jnp.mean(v * v, axis=-1, keepdims=True) + EPS)
            dyg = dy_ref[:, sl]
            cacc_ref[1:2, sl] += jnp.sum(dyg * v * r, axis=0, keepdims=True)
            dyw = dyg * nw_ref[:, sl]
            dv = r * dyw - v * (r * r * r) * jnp.mean(dyw * v, axis=-1, keepdims=True)
            dyr_ref[:, sl] = dv * silu_z[:, sl]
            dz_ref[:, sl] = dv * yraw[:, sl] * (sz[:, sl] * (1.0 + zz[:, sl] * (1.0 - sz[:, sl])))

        dt, a, cs, cst, causal, tri, dt_c, ecs_c, w_c, pair_cols = _ssd_prep(dt_ref[...], par_ref[...])
        cs_last = cs[CHUNK - 1:CHUNK, :]
        causal2 = jnp.concatenate([causal, causal], axis=1)
        lane = lax.broadcasted_iota(jnp.int32, (CHUNK, 128), 1)
        left = lane < HEADDIM
        lane1 = lax.broadcasted_iota(jnp.int32, (1, 128), 1)
        x = xbc_ref[:, 0:D_SSD]
        xdt = x * dt_c
        dyr = dyr_ref[...]
        dyrb = dyr.astype(BF16)
        cacc_ref[0:1, :] += jnp.sum(dyr * x, axis=0, keepdims=True)
        dlast = jnp.zeros((1, 128), F32)
        dxdt_cols, diag_all, dww_cols = [], [], []
        for g in range(N_GROUPS):
            gs = slice(g * GW, (g + 1) * GW)
            b_sl = slice(D_SSD + g * N_STATE, D_SSD + (g + 1) * N_STATE)
            c_sl = slice(D_SSD + (N_GROUPS + g) * N_STATE, D_SSD + (N_GROUPS + g + 1) * N_STATE)
            bg = xbc_ref[:, b_sl].astype(BF16)
            cg = xbc_ref[:, c_sl].astype(BF16)
            scores = lax.dot_general(cg, bg, NT, preferred_element_type=F32)
            scores2 = jnp.concatenate([scores, scores], axis=1)
            hg = hprev_ref[0, gs, :]
            hgb = hg.astype(BF16)
            dhg = dh_ref[gs, :]
            dhgb = dhg.astype(BF16)
            q_all = lax.dot_general(bg, dhgb, NT, preferred_element_type=F32)
            dscores = jnp.zeros((CHUNK, CHUNK), F32)
            diag_cols = []
            for q in range(GW // 128):
                pair = g * (GW // 128) + q
                ps = slice(pair * 128, (pair + 1) * 128)
                decay = _pair_decay(pair_cols[pair], cst, pair, causal2)
                mcat = (scores2 * decay).astype(BF16)
                dyp = dyrb[:, ps]
                dm = lax.dot_general(dyp, _stack_heads(xdt[:, ps], left), NT, preferred_element_type=F32)
                dmd = dm * decay
                dscores = dscores + dmd[:, 0:CHUNK] + dmd[:, CHUNK:]
                rr = lax.dot_general(mcat, dyp, TN, preferred_element_type=F32)
                diag_cols.append(jnp.where(left, rr[0:CHUNK], rr[CHUNK:]))
            wq = w_c[:, gs] * q_all
            diag_g = jnp.concatenate(diag_cols, axis=1)
            diag_all.append(diag_g)
            dxdt_cols.append(diag_g + wq)
            dww_cols.append(wq * xdt[:, gs])
            dp = (ecs_c[:, gs] * dyr[:, gs]).astype(BF16)
            amat = (w_c[:, gs] * xdt[:, gs]).astype(BF16)
            dsb = dscores.astype(BF16)
            dxbc_ref[:, c_sl] = (jnp.dot(dsb, bg, preferred_element_type=F32)
                                 + jnp.dot(dp, hgb, preferred_element_type=F32))
            dxbc_ref[:, b_sl] = (lax.dot_general(dsb, cg, TN, preferred_element_type=F32)
                                 + jnp.dot(amat, dhgb, preferred_element_type=F32))
            dh_in = lax.dot_general(dp, cg, TN, preferred_element_type=F32)
            for j in range(HPG):
                hh = g * HPG + j
                js = slice(j * HEADDIM, (j + 1) * HEADDIM)
                ecl = jnp.exp(cs_last[:, hh:hh + 1])
                dlast = dlast + jnp.where(lane1 == hh, ecl * jnp.sum(dhg[js, :] * hg[js, :]), 0.0)
                dh_ref[g * GW + j * HEADDIM:g * GW + (j + 1) * HEADDIM, :] = ecl * dhg[js, :] + dh_in[js, :]
        dxdt = jnp.concatenate(dxdt_cols, axis=1)
        dxbc_ref[:, 0:D_SSD] = dxdt * dt_c + dyr * dsk_ref[...]
        dww = _dot3(jnp.concatenate(dww_cols, axis=1), fold_ref[...])
        dcs = _dot3(dyrb.astype(F32) * (yraw - x * dsk_ref[...])
                    - xdt.astype(BF16).astype(F32) * jnp.concatenate(diag_all, axis=1), fold_ref[...]) - dww
        rowid = lax.broadcasted_iota(jnp.int32, (CHUNK, 128), 0)
        dcs = dcs + jnp.where(rowid == CHUNK - 1, jnp.sum(dww, axis=0, keepdims=True) + dlast, 0.0)
        dadt = _dot3_left(tri, dcs, TN)
        ddt = _dot3(dxdt * x, fold_ref[...]) + dadt * a
        da = jnp.sum(dadt * dt, axis=0, keepdims=True)
        ddt_raw = ddt * _sigmoid(dt_ref[...] + par_ref[0:1])
        ddt_raw = jnp.where(lane < N_HEADS, ddt_raw, 0.0)
        ddt_ref[...] = ddt_raw
        dpar_ref[0:1, :] += jnp.sum(ddt_raw, axis=0, keepdims=True)
        dpar_ref[1:2, :] += jnp.where(lane1 < N_HEADS, da * a, 0.0)

    row = lambda w: pl.BlockSpec((CHUNK, w), lambda i: (blk(i), 0))
    full = lambda s: pl.BlockSpec(s, lambda i: (0,) * len(s))
    return pl.pallas_call(
        body, name="ssd_bwd", grid=(n_chunks,),
        out_shape=(jax.ShapeDtypeStruct((t, D_XBC), F32), jax.ShapeDtypeStruct((t, D_SSD), F32),
                   jax.ShapeDtypeStruct((t, DT_PAD), F32), jax.ShapeDtypeStruct((8, 128), F32),
                   jax.ShapeDtypeStruct((8, D_SSD), F32)),
        in_specs=[row(D_SSD), row(D_SSD), row(D_SSD), row(D_XBC), row(DT_PAD),
                  pl.BlockSpec((1, D_SSD, N_STATE), lambda i: (blk(i), 0, 0)),
                  full((8, 128)), full((1, D_SSD)), full((1, D_SSD)), full(fold.shape)],
        out_specs=(row(D_XBC), row(D_SSD), row(DT_PAD), full((8, 128)), full((8, D_SSD))),
        scratch_shapes=[pltpu.VMEM((D_SSD, N_STATE), F32), pltpu.VMEM((CHUNK, D_SSD), F32)],
        compiler_params=_params(48),
    )(dyssd, yraw, z, xbc, dt_raw, hprev, par, dsk, normw, fold)


def _conv_bwd(dxbc, dsilu, xbc_pre, seq):
    t = xbc_pre.shape[0]
    tb = 512
    npb = seq // tb
    cw = 640

    def body(d_ref, ds_ref, cur_ref, halo_ref, o_ref, acc_ref, win):
        i = pl.program_id(1)

        @pl.when(i == 0)
        def _():
            acc_ref[...] = jnp.zeros_like(acc_ref)

        first = (i % npb) == 0
        win[0:8, :] = jnp.where(first, 0.0, halo_ref[...])
        win[8:8 + tb, :] = cur_ref[...]
        dpre = d_ref[...] * ds_ref[...]
        o_ref[...] = dpre
        for j in range(4):
            acc_ref[3 - j:4 - j, :] += jnp.sum(dpre * win[8 - j:8 - j + tb, :], axis=0, keepdims=True)
        acc_ref[4:5, :] += jnp.sum(dpre, axis=0, keepdims=True)

    blk = pl.BlockSpec((tb, cw), lambda j, i: (i, j))
    return pl.pallas_call(
        body, name="conv_bwd", grid=(D_XBC // cw, t // tb),
        out_shape=(jax.ShapeDtypeStruct((t, D_XBC), F32), jax.ShapeDtypeStruct((8, D_XBC), F32)),
        in_specs=[blk, blk, blk, pl.BlockSpec((8, cw), lambda j, i: (jnp.maximum(i * (tb // 8) - 1, 0), j))],
        out_specs=(blk, pl.BlockSpec((8, cw), lambda j, i: (0, j))),
        scratch_shapes=[pltpu.VMEM((tb + 8, cw), F32)],
        compiler_params=_params(32),
    )(dxbc, dsilu, xbc_pre, xbc_pre)


def _proj_bwd(dz, dpre, ddt, du5, x2, dxa, mod3, conv_w, w_in_pad, seq):
    t = x2.shape[0]
    tb = 512
    npb = seq // tb
    n_blocks = t // tb

    def body(dz_ref, dp_ref, nxt_ref, ddt_ref, du5_ref, x_ref, dxa_ref, mod_ref, cw_ref, w_hbm,
             gx_ref, u_ref, dxp_ref, bacc_ref, w_vmem, sem):
        i = pl.program_id(0)
        _load_once(w_hbm, w_vmem, sem)

        @pl.when(i % npb == 0)
        def _():
            bacc_ref[...] = jnp.zeros_like(bacc_ref)

        last = (i % npb) == npb - 1
        nxt = jnp.where(last, 0.0, nxt_ref[...])
        cur = dp_ref[...]
        xx = jnp.concatenate([cur, nxt], axis=0)
        w = cw_ref[...]
        dxp = w[3:4] * cur
        for j in (1, 2, 3):
            dxp = dxp + w[3 - j:4 - j] * pltpu.roll(xx, tb + 8 - j, axis=0)[0:tb]
        dxpb = dxp.astype(BF16)
        dxp_ref[...] = dxpb
        o1, o2, o3 = D_SSD, D_SSD + D_XBC, D_SSD + D_XBC + DT_PAD
        du = (lax.dot_general(dz_ref[...].astype(BF16), w_vmem[:, 0:o1], NT, preferred_element_type=F32)
              + lax.dot_general(dxpb, w_vmem[:, o1:o2], NT, preferred_element_type=F32)
              + lax.dot_general(ddt_ref[...].astype(BF16), w_vmem[:, o2:o3], NT, preferred_element_type=F32)
              + lax.dot_general(du5_ref[...].astype(BF16), w_vmem[:, o3:], NT, preferred_element_type=F32))
        m = mod_ref[0]
        xv = x_ref[...]
        u_ref[...] = (xv * (1.0 + m[1:2]) + m[0:1]).astype(BF16)
        gx_ref[...] = dxa_ref[...] + du * (1.0 + m[1:2])
        bacc_ref[0, 0:1, :] += jnp.sum(du, axis=0, keepdims=True)
        bacc_ref[0, 1:2, :] += jnp.sum(du * xv, axis=0, keepdims=True)

    row = lambda w: pl.BlockSpec((tb, w), lambda i: (i, 0))
    nxt_rows = pl.BlockSpec((8, D_XBC), lambda i: (jnp.minimum((i + 1) * (tb // 8), t // 8 - 1), 0))
    return pl.pallas_call(
        body, name="proj_bwd", grid=(n_blocks,),
        out_shape=(jax.ShapeDtypeStruct((t, D_MODEL), F32), jax.ShapeDtypeStruct((t, D_MODEL), BF16),
                   jax.ShapeDtypeStruct((t, D_XBC), BF16), jax.ShapeDtypeStruct((t // seq, 8, D_MODEL), F32)),
        in_specs=[row(D_SSD), row(D_XBC), nxt_rows, row(DT_PAD), row(D_S5), row(D_MODEL), row(D_MODEL),
                  pl.BlockSpec((1, N_MOD, D_MODEL), lambda i: (i // npb, 0, 0)),
                  pl.BlockSpec((4, D_XBC), lambda i: (0, 0)), ANY],
        out_specs=(row(D_MODEL), row(D_MODEL), row(D_XBC), pl.BlockSpec((1, 8, D_MODEL), lambda i: (i // npb, 0, 0))),
        scratch_shapes=[pltpu.VMEM((D_MODEL, D_INP), BF16), pltpu.SemaphoreType.DMA],
        compiler_params=_params(60),
    )(dz, dpre, dpre, ddt, du5, x2, dxa, mod3, conv_w, w_in_pad)


def _pad_rows(a, mult):
    r = a.shape[0]
    pad = (-r) % mult
    return a if pad == 0 else jnp.concatenate([a, jnp.zeros((pad,) + a.shape[1:], a.dtype)], axis=0)


_SMALL = ["conv_w", "conv_b", "dt_bias", "a_log", "d_ssd", "norm_w", "s5_a_re", "s5_a_im", "s5_log_dt", "s5_b_re",
          "s5_b_im", "s5_c_re", "s5_c_im", "s5_d", "b_glu", "ln1_g", "ln1_b", "b1", "b2", "ln2_g", "ln2_b"]


def _tile_rows(size):
    return 8 * (-(-size // 1024))


def _pack_small(d):
    parts = []
    for n in _SMALL:
        flat = d[n].reshape(-1).astype(F32)
        rows = _tile_rows(flat.shape[0])
        pad = rows * 128 - flat.shape[0]
        if pad:
            flat = jnp.concatenate([flat, jnp.zeros((pad,), F32)])
        parts.append(flat.reshape(rows, 128))
    return jnp.concatenate(parts, axis=0)


def _unpack_small(p, shapes):
    out, off = {}, 0
    for n in _SMALL:
        size = math.prod(shapes[n])
        rows = _tile_rows(size)
        out[n] = p[off:off + rows].reshape(-1)[:size].reshape(shapes[n])
        off += rows
    return out


def kernel(x, c, w_ada, b_ada, w_in, conv_w, conv_b, dt_bias, a_log, d_ssd, norm_w, s5_a_re, s5_a_im, s5_log_dt, s5_b_re, s5_b_im, s5_c_re, s5_c_im, s5_d, w_glu, b_glu, w_out, ln1_g, ln1_b, w1, b1, w2, b2, ln2_g, ln2_b, loss_target, m_w_ada, m_b_ada, m_w_in, m_conv_w, m_conv_b, m_dt_bias, m_a_log, m_d_ssd, m_norm_w, m_s5_a_re, m_s5_a_im, m_s5_log_dt, m_s5_b_re, m_s5_b_im, m_s5_c_re, m_s5_c_im, m_s5_d, m_w_glu, m_b_glu, m_w_out, m_ln1_g, m_ln1_b, m_w1, m_b1, m_w2, m_b2, m_ln2_g, m_ln2_b, v_w_ada, v_b_ada, v_w_in, v_conv_w, v_conv_b, v_dt_bias, v_a_log, v_d_ssd, v_norm_w, v_s5_a_re, v_s5_a_im, v_s5_log_dt, v_s5_b_re, v_s5_b_im, v_s5_c_re, v_s5_c_im, v_s5_d, v_w_glu, v_b_glu, v_w_out, v_ln1_g, v_ln1_b, v_w1, v_b1, v_w2, v_b2, v_ln2_g, v_ln2_b):
    weights = dict(w_ada=w_ada, b_ada=b_ada, w_in=w_in, conv_w=conv_w, conv_b=conv_b, dt_bias=dt_bias, a_log=a_log,
                   d_ssd=d_ssd, norm_w=norm_w, s5_a_re=s5_a_re, s5_a_im=s5_a_im, s5_log_dt=s5_log_dt, s5_b_re=s5_b_re,
                   s5_b_im=s5_b_im, s5_c_re=s5_c_re, s5_c_im=s5_c_im, s5_d=s5_d, w_glu=w_glu, b_glu=b_glu, w_out=w_out,
                   ln1_g=ln1_g, ln1_b=ln1_b, w1=w1, b1=b1, w2=w2, b2=b2, ln2_g=ln2_g, ln2_b=ln2_b)
    mom = dict(w_ada=m_w_ada, b_ada=m_b_ada, w_in=m_w_in, conv_w=m_conv_w, conv_b=m_conv_b, dt_bias=m_dt_bias,
               a_log=m_a_log, d_ssd=m_d_ssd, norm_w=m_norm_w, s5_a_re=m_s5_a_re, s5_a_im=m_s5_a_im,
               s5_log_dt=m_s5_log_dt, s5_b_re=m_s5_b_re, s5_b_im=m_s5_b_im, s5_c_re=m_s5_c_re, s5_c_im=m_s5_c_im,
               s5_d=m_s5_d, w_glu=m_w_glu, b_glu=m_b_glu, w_out=m_w_out, ln1_g=m_ln1_g, ln1_b=m_ln1_b, w1=m_w1, b1=m_b1,
               w2=m_w2, b2=m_b2, ln2_g=m_ln2_g, ln2_b=m_ln2_b)
    var = dict(w_ada=v_w_ada, b_ada=v_b_ada, w_in=v_w_in, conv_w=v_conv_w, conv_b=v_conv_b, dt_bias=v_dt_bias,
               a_log=v_a_log, d_ssd=v_d_ssd, norm_w=v_norm_w, s5_a_re=v_s5_a_re, s5_a_im=v_s5_a_im,
               s5_log_dt=v_s5_log_dt, s5_b_re=v_s5_b_re, s5_b_im=v_s5_b_im, s5_c_re=v_s5_c_re, s5_c_im=v_s5_c_im,
               s5_d=v_s5_d, w_glu=v_w_glu, b_glu=v_b_glu, w_out=v_w_out, ln1_g=v_ln1_g, ln1_b=v_ln1_b, w1=v_w1, b1=v_b1,
               w2=v_w2, b2=v_b2, ln2_g=v_ln2_g, ln2_b=v_ln2_b)
    names = list(weights)
    shapes = {n: weights[n].shape for n in names}

    nb, seq, _ = x.shape
    t = nb * seq
    dev = _dev_index()
    x2 = x.reshape(t, D_MODEL)
    tgt2 = loss_target.reshape(t, D_MODEL)

    cw_cols = conv_w.shape[2]
    small_in = jnp.concatenate([c.reshape(-1), conv_w.reshape(-1)]).reshape(-1, 128)
    big_names = ["w_in", "w_out", "w1", "w2", "w_glu"]
    shard_bf16 = {n: weights[n][0].astype(BF16) for n in big_names}
    first = _all_gather([small_in, shard_bf16["w_in"], shard_bf16["w_glu"]], "gather_first")
    small_all = first[0].reshape(N_DEV, -1)
    c_all = small_all[:, :nb * D_MODEL].reshape(N_DEV * nb, D_MODEL)
    conv_w_full = small_all[:, nb * D_MODEL:].reshape(N_DEV, 4, cw_cols).transpose(1, 0, 2).reshape(4, D_XBC)

    w_in_f = first[1].transpose(1, 0, 2).reshape(D_MODEL, D_IN)
    w_in_pad = jnp.concatenate(
        [w_in_f[:, :D_SSD + D_XBC], w_in_f[:, D_SSD + D_XBC:D_SSD + D_XBC + N_HEADS],
         jnp.zeros((D_MODEL, DT_PAD - N_HEADS), BF16), w_in_f[:, D_SSD + D_XBC + N_HEADS:]], axis=1)
    w_glu_f = first[2].reshape(D_S5, D_S5)
    late_names = ["w_out", "w1", "w2"]

    ada_cols = w_ada.shape[2]
    b_cols = lax.dynamic_slice_in_dim(b_ada, dev * ada_cols, ada_cols, axis=1)
    mod_cols = _mod_fwd(c_all, w_ada[0], b_cols)
    mod_all = _all_gather([mod_cols], "gather_mod")[0]
    mod_mine = lax.dynamic_slice_in_dim(mod_all, dev * nb, nb, axis=1)
    mod3 = mod_mine.transpose(1, 0, 2).reshape(nb, N_MOD, D_MODEL)
    late_in, mod3 = lax.optimization_barrier(([shard_bf16[n] for n in late_names], mod3))
    late_sems = _gather_start(late_in, "gather_late_start")
    mod3 = mod3 + late_sems[4][0, 0]

    def pad_lanes(v, n):
        return jnp.concatenate([v, jnp.zeros((v.shape[0], n - v.shape[1]), F32)], axis=1)

    par = _pad_rows(jnp.concatenate([pad_lanes(dt_bias, 128), pad_lanes(a_log, 128)], axis=0), 8)
    dsk = jnp.repeat(d_ssd[0], HEADDIM).reshape(1, D_SSD)
    ar = s5_a_re.reshape(1, S5_N)
    ai = s5_a_im.reshape(1, S5_N)
    ldt = jnp.repeat(s5_log_dt[0], S5_P).reshape(1, S5_N)
    br_t = s5_b_re[0].transpose(2, 0, 1).reshape(S5_CH, S5_N)
    bi_t = s5_b_im[0].transpose(2, 0, 1).reshape(S5_CH, S5_N)
    bb_re_t, bb_im_t, pf_re, pf_im, pr_re, pr_im = _s5_params_fwd(ar, ai, ldt, br_t, bi_t)
    gpb = S5_GROUPS // S5_BLOCKS
    mask_b = (jnp.arange(128)[:, None] // S5_CH) == (jnp.arange(512)[None, :] // S5_P)

    def dense_b(bt_):
        blocks = bt_.reshape(S5_CH, S5_BLOCKS, 512).transpose(1, 0, 2)
        return jnp.where(mask_b, jnp.tile(blocks, (1, gpb, 1)), 0.0).astype(BF16)

    def dense_c(cc):
        blocks = cc[0].transpose(0, 2, 1).reshape(S5_BLOCKS, 512, S5_CH)
        return jnp.where(mask_b.T, jnp.tile(blocks, (1, 1, gpb)), 0.0).astype(BF16)

    bb_re, bb_im = dense_b(bb_re_t), dense_b(bb_im_t)
    cc_re, cc_im = dense_c(s5_c_re), dense_c(s5_c_im)
    s5d = s5_d.reshape(1, D_S5)
    ln1 = jnp.concatenate([ln1_g, ln1_b], axis=0)
    vec1 = _pad_rows(jnp.concatenate([b2, ln2_g, ln2_b], axis=0), 8)

    z, xbc_pre, xbc, dsilu, dt_raw, u5 = _proj_conv_fwd(x2, mod3, w_in_pad, conv_w_full, conv_b, seq)
    yraw, ycat, hprev = _ssd_fwd(xbc, z, dt_raw, par, dsk, norm_w, seq)
    s_re, s_im, ypre, ycat = _s5_fwd(u5, bb_re, bb_im, cc_re, cc_im, pf_re, pf_im, s5d, w_glu_f, b_glu, ycat, seq)
    sent, landed = _gather_wait(late_sems[0], late_sems[1], late_sems[2], late_sems[3], ycat, "gather_late_wait")
    gathered = {n: lax.dynamic_update_index_in_dim(l, x, dev, 0) for n, x, l in zip(late_names, sent, landed)}
    w_out_f = gathered["w_out"].reshape(2 * D_MODEL, D_MODEL)
    w1_blocks = gathered["w1"]
    w2_f = gathered["w2"].reshape(D_FF, D_MODEL)
    mix, x1 = _out_ln1(ycat, x2, mod3, w_out_f, ln1, seq)

    dx1, u2b, hb, dhpb, dob, gacc2, db1, bacc2 = _mlp_fwd_bwd(x1, tgt2, mod3, w1_blocks, w2_f, vec1, b1, seq)
    loss = lax.psum(0.5 / D_MODEL * jnp.sum(gacc2[3]), ("x", "y", "c"))

    dmixb, dxa, dyssd, dy5, gacc1, bacc1 = _ln1_out_bwd(dx1, x2, mix, mod3, w_out_f, ln1, seq)

    g_w2 = _atb(hb, dob, "gw2")
    g_w1 = _atb(u2b, dhpb, "gw1")
    g_wout = _atb(ycat, dmixb, "gwout")
    core = lax.axis_index("c").astype(jnp.int32).reshape(1)
    chip = 2 * lax.axis_index("x") + lax.axis_index("y")

    def chip_sums_of(names, grads, tag):
        by_dest = [g if g.ndim == 2 else g.reshape((4, 2) + g.shape[1:]) for g in grads]
        from_sibling = _sibling_swap(by_dest, "rs_swap_" + tag)
        return [_add_halves(g, r, core, "rs_add_" + n) for g, r, n in zip(by_dest, from_sibling, names)]

    early_names = ["w_out", "w1", "w2"]
    early_sums = chip_sums_of(early_names, [g_wout.reshape((N_DEV,) + w_out.shape[1:]), g_w1,
                                            g_w2.reshape((N_DEV,) + w2.shape[1:])], "early")
    early = _all_to_all_start(early_sums, "rs_early_start")
    s5d_after = s5d + early[4][0, 0]

    du5, vacc, sacc, d_cc, d_bb, g_wglu = _s5_bwd(dy5, ypre, u5, s_re, s_im, bb_re, bb_im, cc_re, cc_im,
                                                  pr_re, pr_im, s5d_after, w_glu_f, b_glu, seq)
    dxbc, dz, ddt, dpar, cacc = _ssd_bwd(dyssd, yraw, z, xbc, dt_raw, hprev, par, dsk, norm_w, seq)
    dpre, conv_acc = _conv_bwd(dxbc, dsilu, xbc_pre, seq)
    grad_x2, ub, dxpb, bacc0 = _proj_bwd(dz, dpre, ddt, du5, x2, dxa, mod3, conv_w_full, w_in_pad, seq)

    g_win = jnp.concatenate([_atb(ub, dz, "gwin_z"), _atb(ub, dxpb, "gwin_xbc"),
                             _atb(ub, ddt, "gwin_dt")[:, :N_HEADS], _atb(ub, du5, "gwin_s5")], axis=1)

    def diag_b(dd):
        kept = jnp.where(mask_b, dd, 0.0).reshape(S5_BLOCKS, gpb, S5_CH, 512).sum(1)
        return kept.transpose(1, 0, 2).reshape(S5_CH, S5_N)

    def diag_c(dd):
        kept = jnp.where(mask_b.T, dd, 0.0).reshape(S5_BLOCKS, 512, gpb, S5_CH).sum(2)
        return kept.reshape(S5_GROUPS, S5_P, S5_CH).transpose(0, 2, 1)

    g_ar, g_ai, g_ldt, g_br_t, g_bi_t = _s5_params_bwd(ar, ai, ldt, br_t, bi_t, vacc[0:1], vacc[1:2],
                                                      diag_b(d_bb[:S5_BLOCKS]), diag_b(d_bb[S5_BLOCKS:]))

    def from_t(gt):
        return gt.reshape(S5_CH, S5_GROUPS, S5_P).transpose(1, 2, 0)

    small_g = dict(
        conv_w=conv_acc[0:4], conv_b=conv_acc[4:5], dt_bias=dpar[0:1, :N_HEADS], a_log=dpar[1:2, :N_HEADS],
        d_ssd=cacc[0].reshape(N_HEADS, HEADDIM).sum(1), norm_w=cacc[1:2],
        s5_a_re=g_ar, s5_a_im=g_ai, s5_log_dt=g_ldt[:, :S5_GROUPS], s5_b_re=from_t(g_br_t), s5_b_im=from_t(g_bi_t),
        s5_c_re=diag_c(d_cc[:S5_BLOCKS]), s5_c_im=diag_c(d_cc[S5_BLOCKS:]), s5_d=sacc[0:1], b_glu=sacc[1:2],
        ln1_g=gacc1[0:1], ln1_b=gacc1[1:2], b1=db1, b2=gacc2[2:3], ln2_g=gacc2[0:1], ln2_b=gacc2[1:2])

    dmod = jnp.concatenate([bacc0[:, 0], bacc0[:, 1], bacc1[:, 0], bacc2[:, 0], bacc2[:, 1], bacc2[:, 2]], axis=1)
    dmod_all = _all_gather([dmod], "gather_dmod")[0].reshape(N_DEV * nb, N_MOD * D_MODEL)
    dmod_cols = lax.dynamic_slice_in_dim(dmod_all, dev * ada_cols, ada_cols, axis=1)
    g_wada, g_bada = _mod_bwd(c_all, dmod_cols, dmod_all)

    in_cols = w_in.shape[2]
    late_rs = ["w_in", "w_glu"]
    late_sums = chip_sums_of(late_rs, [g_win.reshape(D_MODEL, N_DEV, in_cols).transpose(1, 0, 2),
                                       g_wglu.reshape((N_DEV,) + w_glu.shape[1:])], "late")
    parts = dict(zip(late_rs, _chip_all_to_all(late_sums, "rs_late_all_to_all")))
    sent, landed = _all_to_all_wait(early[0], early[1], early[2], early[3], parts["w_in"], "rs_early_wait")
    for n, l, h in zip(early_names, landed, sent):
        parts[n] = lax.dynamic_update_index_in_dim(l, lax.dynamic_index_in_dim(h, chip, 0, keepdims=False), chip, 0)

    res = {k: {} for k in "gdmv"}
    for n in big_names:
        outs = _adamw(parts[n], weights[n][0], mom[n][0], var[n][0], "adamw_" + n)
        for k, a in zip("gdmv", outs):
            res[k][n] = a[None]

    ag, ad, am, av = _adamw(g_wada[None], w_ada[0], m_w_ada[0], v_w_ada[0], "adamw_w_ada")
    for k, a in (("g", ag), ("d", ad), ("m", am), ("v", av)):
        res[k]["w_ada"] = a[None]
    bg_, bd_, bm_, bv_ = _adamw(g_bada.reshape(1, -1, 128), b_ada.reshape(-1, 128), m_b_ada.reshape(-1, 128),
                                v_b_ada.reshape(-1, 128), "adamw_b_ada")
    for k, a in (("g", bg_), ("d", bd_), ("m", bm_), ("v", bv_)):
        res[k]["b_ada"] = a.reshape(shapes["b_ada"])

    small_shapes = dict(shapes)
    small_shapes["conv_w"] = (1, 4, D_XBC)
    small_parts = _all_gather([_pack_small(small_g)], "gather_small_grads")[0]
    rep = {n: (jnp.zeros((1, 4, D_XBC), F32) if n == "conv_w" else weights[n]) for n in _SMALL}
    rep_m = {n: (jnp.zeros((1, 4, D_XBC), F32) if n == "conv_w" else mom[n]) for n in _SMALL}
    rep_v = {n: (jnp.ones((1, 4, D_XBC), F32) if n == "conv_w" else var[n]) for n in _SMALL}
    sg_, sd_, sm_, sv_ = _adamw(small_parts, _pack_small(rep), _pack_small(rep_m), _pack_small(rep_v), "adamw_small")
    for k, p in (("g", sg_), ("d", sd_), ("m", sm_), ("v", sv_)):
        un = _unpack_small(p, small_shapes)
        for n in _SMALL:
            if n != "conv_w":
                res[k][n] = un[n]
    g_conv_full = _unpack_small(sg_, small_shapes)["conv_w"][0]
    g_conv_mine = lax.dynamic_slice_in_dim(g_conv_full, dev * cw_cols, cw_cols, axis=1)
    cg_, cd_, cm_, cv_ = _adamw(g_conv_mine[None], conv_w[0], m_conv_w[0], v_conv_w[0], "adamw_conv_w")
    for k, a in (("g", cg_), ("d", cd_), ("m", cm_), ("v", cv_)):
        res[k]["conv_w"] = a[None]

    grad_x = grad_x2.reshape(nb, seq, D_MODEL)
    return (loss, grad_x, *[res["g"][n] for n in names], *[res["d"][n] for n in names],
            *[res["m"][n] for n in names], *[res["v"][n] for n in names])
```

```python
import functools
import math

import jax
import jax.numpy as jnp
from jax import lax
from jax.experimental import pallas as pl
from jax.experimental.pallas import tpu as pltpu

F32, BF16 = jnp.float32, jnp.bfloat16
MESH = pl.DeviceIdType.MESH
N_DEV = 8

D_MODEL = 1024
D_SSD = 1536
N_HEADS = 24
HEADDIM = 64
N_GROUPS = 4
HPG = 6
GW = HPG * HEADDIM
N_STATE = 128
CHUNK = 128
D_XBC = 2560
D_S5 = 512
S5_GROUPS = 32
S5_CH = 16
S5_P = 64
S5_N = S5_GROUPS * S5_P
D_IN = 4632
DT_PAD = 128
D_INP = D_SSD + D_XBC + DT_PAD + D_S5
D_FF = 4096
N_MOD = 6
ALPHA = 2.0 ** 0.25
EPS = 1e-5
LR, B1, B2, AEPS, WD, STEP = 0.001, 0.9, 0.999, 1e-08, 0.01, 10

NT = (((1,), (1,)), ((), ()))
TN = (((0,), (0,)), ((), ()))
ANY = pl.BlockSpec(memory_space=pl.ANY)
HIGHEST = lax.Precision.HIGHEST


def _mm(a, b):
    return jnp.dot(a.astype(BF16), b.astype(BF16), preferred_element_type=F32)


def _mm_nt(a, b):
    return lax.dot_general(a.astype(BF16), b.astype(BF16), NT, preferred_element_type=F32)


def _mm_tn(a, b):
    return lax.dot_general(a.astype(BF16), b.astype(BF16), TN, preferred_element_type=F32)


def _row_block(r, cap):
    best = r
    for cand in range(8, min(r, cap) + 1, 8):
        if r % cand == 0:
            best = cand
    return best if best <= cap else r


def _params(vmem_mb):
    return pltpu.CompilerParams(vmem_limit_bytes=vmem_mb << 20)


def _sigmoid(x):
    return 0.5 * (jnp.tanh(0.5 * x) + 1.0)


def _softplus(x):
    return jnp.maximum(x, 0.0) + jnp.log(1.0 + jnp.exp(-jnp.abs(x)))


_GK = math.sqrt(2.0 / math.pi)


def _gelu(x):
    return 0.5 * x * (1.0 + jnp.tanh(_GK * (x + 0.044715 * x * x * x)))


def _gelu_grad(x):
    t = jnp.tanh(_GK * (x + 0.044715 * x * x * x))
    return 0.5 * (1.0 + t) + 0.5 * x * (1.0 - t * t) * _GK * (1.0 + 3.0 * 0.044715 * x * x)


def _dev_index():
    return 4 * lax.axis_index("x") + 2 * lax.axis_index("y") + lax.axis_index("c")


def _all_gather(xs, name):
    n = len(xs)

    def body(*refs):
        x_refs, out_refs = refs[:n], refs[n:2 * n]
        send_sems, recv_sems, local_sems = refs[2 * n:]
        ix, iy, ic = lax.axis_index("x"), lax.axis_index("y"), lax.axis_index("c")
        me, sibling = (ix, iy, ic), (ix, iy, 1 - ic)
        chips = [(1 - ix, iy), (ix, 1 - iy), (1 - ix, 1 - iy)]

        def slot(a, px, py, pc):
            return out_refs[a].at[4 * px + 2 * py + pc]

        def copy(a, k, block, to, src=None):
            return pltpu.make_async_remote_copy(
                src_ref=slot(a, *block) if src is None else src, dst_ref=slot(a, *block),
                send_sem=send_sems.at[7 * a + k], recv_sem=recv_sems.at[7 * a + k], device_id=to, device_id_type=MESH)

        mine = [pltpu.make_async_copy(x_refs[a], slot(a, *me), local_sems.at[a]) for a in range(n)]
        for cp in mine:
            cp.start()
        first = []
        for j, chip in enumerate(chips):
            first += [copy(a, 1 + j, me, (*chip, ic), src=x_refs[a]) for a in range(n)]
        first += [copy(a, 0, me, sibling, src=x_refs[a]) for a in range(n)]
        for cp in first:
            cp.start()
        passed = []
        for j, chip in enumerate(chips):
            for a in range(n):
                copy(a, 1 + j, (*chip, ic), me).wait_recv()
                cp = copy(a, 4 + j, (*chip, ic), sibling)
                cp.start()
                passed.append(cp)
        for a in range(n):
            copy(a, 0, sibling, me).wait_recv()
            for j, chip in enumerate(chips):
                copy(a, 4 + j, (*chip, 1 - ic), me).wait_recv()
        for cp in first + passed:
            cp.wait_send()
        for cp in mine:
            cp.wait()

    return pl.pallas_call(
        body, name=name, out_shape=tuple(jax.ShapeDtypeStruct((N_DEV,) + x.shape, x.dtype) for x in xs),
        in_specs=[ANY] * n, out_specs=tuple([ANY] * n),
        scratch_shapes=[pltpu.SemaphoreType.DMA((7 * n,)), pltpu.SemaphoreType.DMA((7 * n,)),
                        pltpu.SemaphoreType.DMA((n,))],
    )(*xs)


HBM = pl.BlockSpec(memory_space=pltpu.HBM)
SEM = pl.BlockSpec(memory_space=pltpu.SEMAPHORE)
DATAFLOW = pltpu.SideEffectType.DATAFLOW_SIDE_EFFECTING


def _peer(k):
    ix, iy, ic = lax.axis_index("x"), lax.axis_index("y"), lax.axis_index("c")
    return (1 - ix if k & 4 else ix, 1 - iy if k & 2 else iy, 1 - ic if k & 1 else ic)


def _block_of(p):
    return 4 * p[0] + 2 * p[1] + p[2]


def _gather_start(xs, name):
    n = len(xs)
    lands = [lax.empty((N_DEV,) + x.shape, x.dtype) for x in xs]

    def body(*refs):
        x_refs, land_refs = refs[:n], refs[n:2 * n]
        send_sems, recv_sems = refs[2 * n], refs[2 * n + 1]
        token = refs[-1]
        me = _block_of(_peer(0))
        for a in range(n):
            for k in range(1, N_DEV):
                pltpu.make_async_remote_copy(
                    src_ref=x_refs[a], dst_ref=land_refs[a].at[me], send_sem=send_sems.at[7 * a + k - 1],
                    recv_sem=recv_sems.at[7 * a + k - 1], device_id=_peer(k), device_id_type=MESH).start()
        token[...] = jnp.zeros_like(token)

    outs = pl.pallas_call(
        body, name=name,
        out_shape=(pltpu.SemaphoreType.DMA((7 * n,)), pltpu.SemaphoreType.DMA((7 * n,)))
        + tuple(pltpu.HBM(x.shape, x.dtype) for x in xs) + tuple(pltpu.HBM(l.shape, l.dtype) for l in lands)
        + (jax.ShapeDtypeStruct((8, 128), F32),),
        in_specs=[HBM] * (2 * n), out_specs=(SEM, SEM) + (HBM,) * (2 * n) + (pl.BlockSpec(memory_space=pltpu.VMEM),),
        input_output_aliases={i: 2 + i for i in range(2 * n)},
        compiler_params=pltpu.CompilerParams(has_side_effects=DATAFLOW),
    )(*[pltpu.with_memory_space_constraint(x, pltpu.HBM) for x in xs],
      *[pltpu.with_memory_space_constraint(l, pltpu.HBM) for l in lands])
    return outs[0], outs[1], outs[2:2 + n], outs[2 + n:2 + 2 * n], outs[-1]


def _gather_wait(send_sems, recv_sems, xs_thru, lands_thru, after, name):
    n = len(xs_thru)

    def body(*refs):
        x_refs, land_refs = refs[:n], refs[n:2 * n]
        send_sems, recv_sems = refs[2 * n], refs[2 * n + 1]
        for a in range(n):
            for k in range(1, N_DEV):
                cp = pltpu.make_async_remote_copy(
                    src_ref=x_refs[a], dst_ref=land_refs[a].at[_block_of(_peer(k))], send_sem=send_sems.at[7 * a + k - 1],
                    recv_sem=recv_sems.at[7 * a + k - 1], device_id=_peer(k), device_id_type=MESH)
                cp.wait_send()
                cp.wait_recv()

    outs = pl.pallas_call(
        body, name=name,
        out_shape=tuple(pltpu.HBM(x.shape, x.dtype) for x in xs_thru)
        + tuple(pltpu.HBM(l.shape, l.dtype) for l in lands_thru),
        in_specs=[HBM] * (2 * n) + [SEM, SEM, ANY], out_specs=(HBM,) * (2 * n),
        input_output_aliases={i: i for i in range(2 * n)},
        compiler_params=pltpu.CompilerParams(has_side_effects=DATAFLOW),
    )(*xs_thru, *lands_thru, send_sems, recv_sems, after)
    return outs[:n], outs[n:]


def _chip_peer(k):
    ix, iy = lax.axis_index("x"), lax.axis_index("y")
    return (1 - ix if k & 2 else ix, 1 - iy if k & 1 else iy)


def _all_to_all_start(hs, name):
    n = len(hs)
    lands = [lax.empty(h.shape, h.dtype) for h in hs]

    def body(*refs):
        h_refs, land_refs = refs[:n], refs[n:2 * n]
        send_sems, recv_sems = refs[2 * n], refs[2 * n + 1]
        token = refs[-1]
        ic = lax.axis_index("c")
        mx, my = _chip_peer(0)
        for a in range(n):
            for k in range(1, 4):
                px, py = _chip_peer(k)
                pltpu.make_async_remote_copy(
                    src_ref=h_refs[a].at[2 * px + py], dst_ref=land_refs[a].at[2 * mx + my],
                    send_sem=send_sems.at[3 * a + k - 1], recv_sem=recv_sems.at[3 * a + k - 1],
                    device_id=(px, py, ic), device_id_type=MESH).start()
        token[...] = jnp.zeros_like(token)

    outs = pl.pallas_call(
        body, name=name,
        out_shape=(pltpu.SemaphoreType.DMA((3 * n,)), pltpu.SemaphoreType.DMA((3 * n,)))
        + tuple(pltpu.HBM(h.shape, h.dtype) for h in hs) + tuple(pltpu.HBM(l.shape, l.dtype) for l in lands)
        + (jax.ShapeDtypeStruct((8, 128), F32),),
        in_specs=[HBM] * (2 * n), out_specs=(SEM, SEM) + (HBM,) * (2 * n) + (pl.BlockSpec(memory_space=pltpu.VMEM),),
        input_output_aliases={i: 2 + i for i in range(2 * n)},
        compiler_params=pltpu.CompilerParams(has_side_effects=DATAFLOW),
    )(*[pltpu.with_memory_space_constraint(h, pltpu.HBM) for h in hs],
      *[pltpu.with_memory_space_constraint(l, pltpu.HBM) for l in lands])
    return outs[0], outs[1], outs[2:2 + n], outs[2 + n:2 + 2 * n], outs[-1]


def _all_to_all_wait(send_sems, recv_sems, hs_thru, lands_thru, after, name):
    n = len(hs_thru)

    def body(*refs):
        h_refs, land_refs = refs[:n], refs[n:2 * n]
        send_sems, recv_sems = refs[2 * n], refs[2 * n + 1]
        ic = lax.axis_index("c")
        for a in range(n):
            for k in range(1, 4):
                px, py = _chip_peer(k)
                cp = pltpu.make_async_remote_copy(
                    src_ref=h_refs[a].at[2 * px + py], dst_ref=land_refs[a].at[2 * px + py],
                    send_sem=send_sems.at[3 * a + k - 1], recv_sem=recv_sems.at[3 * a + k - 1],
                    device_id=(px, py, ic), device_id_type=MESH)
                cp.wait_send()
                cp.wait_recv()

    outs = pl.pallas_call(
        body, name=name,
        out_shape=tuple(pltpu.HBM(h.shape, h.dtype) for h in hs_thru)
        + tuple(pltpu.HBM(l.shape, l.dtype) for l in lands_thru),
        in_specs=[HBM] * (2 * n) + [SEM, SEM, ANY], out_specs=(HBM,) * (2 * n),
        input_output_aliases={i: i for i in range(2 * n)},
        compiler_params=pltpu.CompilerParams(has_side_effects=DATAFLOW),
    )(*hs_thru, *lands_thru, send_sems, recv_sems, after)
    return outs[:n], outs[n:]


def _sibling_swap(gs, name):
    n = len(gs)

    def body(*refs):
        g_refs, recv_refs = refs[:n], refs[n:2 * n]
        send_sems, recv_sems = refs[2 * n:]
        ix, iy, ic = lax.axis_index("x"), lax.axis_index("y"), lax.axis_index("c")

        def block(g_ref, q):
            if len(g_ref.shape) == 4:
                return g_ref.at[q, 1 - ic]
            cw = g_ref.shape[1] // N_DEV
            return g_ref.at[:, pl.ds(pl.multiple_of((2 * q + 1 - ic) * cw, 128), cw)]

        cps = []
        for a in range(n):
            for q in range(4):
                cps.append(pltpu.make_async_remote_copy(
                    src_ref=block(g_refs[a], q), dst_ref=recv_refs[a].at[q],
                    send_sem=send_sems.at[4 * a + q], recv_sem=recv_sems.at[4 * a + q],
                    device_id=(ix, iy, 1 - ic), device_id_type=MESH))
        for cp in cps:
            cp.start()
        for cp in cps:
            cp.wait()

    return pl.pallas_call(
        body, name=name,
        out_shape=tuple(jax.ShapeDtypeStruct(
            (4,) + (g.shape[2:] if g.ndim == 4 else (g.shape[0], g.shape[1] // N_DEV)), g.dtype) for g in gs),
        in_specs=[ANY] * n, out_specs=tuple([ANY] * n),
        scratch_shapes=[pltpu.SemaphoreType.DMA((4 * n,)), pltpu.SemaphoreType.DMA((4 * n,))],
    )(*gs)


def _chip_all_to_all(hs, name):
    n = len(hs)

    def body(*refs):
        h_refs, out_refs = refs[:n], refs[n:2 * n]
        send_sems, recv_sems, local_sems = refs[2 * n:]
        ix, iy, ic = lax.axis_index("x"), lax.axis_index("y"), lax.axis_index("c")
        me = 2 * ix + iy
        peers = [(1 - ix, iy), (ix, 1 - iy), (1 - ix, 1 - iy)]
        mine = [pltpu.make_async_copy(h_refs[a].at[me], out_refs[a].at[me], local_sems.at[a]) for a in range(n)]
        for cp in mine:
            cp.start()

        def copy(a, k, src_slot, dst_slot, peer):
            return pltpu.make_async_remote_copy(
                src_ref=h_refs[a].at[src_slot], dst_ref=out_refs[a].at[dst_slot],
                send_sem=send_sems.at[3 * a + k], recv_sem=recv_sems.at[3 * a + k],
                device_id=(*peer, ic), device_id_type=MESH)

        sends = [copy(a, k, 2 * px + py, me, (px, py)) for a in range(n) for k, (px, py) in enumerate(peers)]
        for cp in sends:
            cp.start()
        for a in range(n):
            for k, (px, py) in enumerate(peers):
                copy(a, k, 2 * px + py, 2 * px + py, (px, py)).wait_recv()
        for cp in sends:
            cp.wait_send()
        for cp in mine:
            cp.wait()

    return pl.pallas_call(
        body, name=name, out_shape=tuple(jax.ShapeDtypeStruct(h.shape, h.dtype) for h in hs),
        in_specs=[ANY] * n, out_specs=tuple([ANY] * n),
        scratch_shapes=[pltpu.SemaphoreType.DMA((3 * n,)), pltpu.SemaphoreType.DMA((3 * n,)),
                        pltpu.SemaphoreType.DMA((n,))],
    )(*hs)


def _add_halves(g, recv, core, name):
    _, r, c = recv.shape
    br = _row_block(r, 512)
    stacked = g.ndim == 4

    def body(core_ref, g_ref, r_ref, o_ref):
        o_ref[0] = ((g_ref[0, 0] if stacked else g_ref[...]) + r_ref[0]).astype(BF16)

    spec = pl.BlockSpec((1, br, c), lambda i, j, core_ref: (i, j, 0))
    if stacked:
        g_spec = pl.BlockSpec((1, 1, br, c), lambda i, j, core_ref: (i, core_ref[0], j, 0))
    else:
        g_spec = pl.BlockSpec((br, c), lambda i, j, core_ref: (j, 2 * i + core_ref[0]))
    return pl.pallas_call(
        body, name=name, out_shape=jax.ShapeDtypeStruct(recv.shape, BF16),
        grid_spec=pltpu.PrefetchScalarGridSpec(
            num_scalar_prefetch=1, grid=(4, r // br), in_specs=[g_spec, spec], out_specs=spec),
        compiler_params=_params(32),
    )(core, g, recv)


def _adamw(parts, w, m, v, name):
    n_parts, r, c = parts.shape
    if r % 8 == 0:
        br, bc = _row_block(r, 512 if c <= 1024 else 256), c
    else:
        br, bc = r, (256 if c % 256 == 0 else c)

    def body(p_ref, w_ref, m_ref, v_ref, g_out, d_out, m_out, v_out):
        g = p_ref[0].astype(F32)
        for p in range(1, n_parts):
            g = g + p_ref[p].astype(F32)
        m2 = B1 * m_ref[...] + (1.0 - B1) * g
        v2 = B2 * v_ref[...] + (1.0 - B2) * (g * g)
        m_hat = m2 / (1.0 - B1 ** STEP)
        v_hat = v2 / (1.0 - B2 ** STEP)
        g_out[...] = g
        d_out[...] = -LR * (m_hat / (jnp.sqrt(v_hat) + AEPS) + WD * w_ref[...])
        m_out[...] = m2
        v_out[...] = v2

    spec = pl.BlockSpec((br, bc), lambda i, j: (i, j))
    out = jax.ShapeDtypeStruct((r, c), F32)
    return pl.pallas_call(
        body, name=name, out_shape=(out, out, out, out), grid=(r // br, c // bc),
        in_specs=[pl.BlockSpec((n_parts, br, bc), lambda i, j: (0, i, j)), spec, spec, spec],
        out_specs=(spec, spec, spec, spec), compiler_params=_params(40),
    )(parts, w, m, v)


def _atb(a, b, name):
    t, k1 = a.shape
    k2 = b.shape[1]
    bt = math.gcd(t, 2048)

    def pick(k):
        for cand in (1024, 768, 512, 384, 256, 128):
            if k % cand == 0:
                return cand
        return k

    b1, b2 = pick(k1), pick(k2)

    def body(a_ref, b_ref, o_ref):
        @pl.when(pl.program_id(2) == 0)
        def _():
            o_ref[...] = jnp.zeros_like(o_ref)
        o_ref[...] += _mm_tn(a_ref[...], b_ref[...])

    return pl.pallas_call(
        body, name=name, out_shape=jax.ShapeDtypeStruct((k1, k2), F32), grid=(k1 // b1, k2 // b2, t // bt),
        in_specs=[pl.BlockSpec((bt, b1), lambda i, j, k: (k, i)), pl.BlockSpec((bt, b2), lambda i, j, k: (k, j))],
        out_specs=pl.BlockSpec((b1, b2), lambda i, j, k: (i, j)), compiler_params=_params(48),
    )(a, b)


def _mod_fwd(c_all, w_ada, b_cols):
    def body(c_ref, w_ref, b_ref, o_ref):
        cc = c_ref[...]
        cond = cc * _sigmoid(cc)
        o_ref[...] = _mm(cond, w_ref[...]) + b_ref[...]

    return pl.pallas_call(body, name="mod_fwd", out_shape=jax.ShapeDtypeStruct((c_all.shape[0], w_ada.shape[1]), F32),
                          compiler_params=_params(32))(c_all, w_ada, b_cols)


def _mod_bwd(c_all, dmod_cols, dmod_all):
    def body(c_ref, dc_ref, da_ref, gw_ref, gb_ref):
        cc = c_ref[...]
        cond = cc * _sigmoid(cc)
        gw_ref[...] = _mm_tn(cond, dc_ref[...])
        gb_ref[...] = jnp.sum(da_ref[...], axis=0, keepdims=True)

    return pl.pallas_call(
        body, name="mod_bwd",
        out_shape=(jax.ShapeDtypeStruct((D_MODEL, dmod_cols.shape[1]), F32), jax.ShapeDtypeStruct((1, dmod_all.shape[1]), F32)),
        compiler_params=_params(32))(c_all, dmod_cols, dmod_all)


def _load_once(hbm_ref, vmem_ref, sem):
    @pl.when(pl.program_id(0) == 0)
    def _():
        cp = pltpu.make_async_copy(hbm_ref, vmem_ref, sem)
        cp.start()
        cp.wait()


def _conv_taps(win_ref, w, tb, cols):
    shifted = [win_ref[8 - j:8 - j + tb, cols] for j in range(4)]
    acc = w[3:4] * shifted[0]
    for j in (1, 2, 3):
        acc = acc + w[3 - j:4 - j] * shifted[j]
    return acc, shifted


def _proj_conv_fwd(x2, mod3, w_in_pad, conv_w, conv_b, seq):
    t = x2.shape[0]
    tb = 256
    npb = seq // tb
    cw = 512

    def body(x_ref, mod_ref, w_hbm, cw_ref, cb_ref, z_ref, pre_ref, xbc_ref, dsilu_ref, dt_ref, u5_ref, w_vmem, win, sem):
        _load_once(w_hbm, w_vmem, sem)
        first = (pl.program_id(0) % npb) == 0

        @pl.when(first)
        def _():
            win[0:8, :] = jnp.zeros((8, D_XBC), F32)

        @pl.when(jnp.logical_not(first))
        def _():
            win[0:8, :] = win[tb:tb + 8, :]

        m = mod_ref[0]
        u = (x_ref[...] * (1.0 + m[1:2]) + m[0:1]).astype(BF16)
        z_ref[...] = lax.dot_general(u, w_vmem[0:D_SSD, :], NT, preferred_element_type=F32)
        dt_ref[...] = lax.dot_general(u, w_vmem[D_SSD + D_XBC:D_SSD + D_XBC + DT_PAD, :], NT,
                                      preferred_element_type=F32)
        u5_ref[...] = lax.dot_general(u, w_vmem[D_SSD + D_XBC + DT_PAD:, :], NT, preferred_element_type=F32)
        for k in range(D_XBC // cw):
            cols = slice(k * cw, (k + 1) * cw)
            pre_k = lax.dot_general(u, w_vmem[D_SSD + k * cw:D_SSD + (k + 1) * cw, :], NT,
                                    preferred_element_type=F32)
            win[8:8 + tb, cols] = pre_k
            pre_ref[:, cols] = pre_k
            conv, _ = _conv_taps(win, cw_ref[:, cols], tb, cols)
            conv = conv + cb_ref[:, cols]
            sg = _sigmoid(conv)
            xbc_ref[:, cols] = conv * sg
            dsilu_ref[:, cols] = sg * (1.0 + conv * (1.0 - sg))

    row = lambda w: pl.BlockSpec((tb, w), lambda i: (i, 0))
    return pl.pallas_call(
        body, name="proj_conv_fwd", grid=(t // tb,),
        out_shape=(jax.ShapeDtypeStruct((t, D_SSD), F32), jax.ShapeDtypeStruct((t, D_XBC), F32),
                   jax.ShapeDtypeStruct((t, D_XBC), F32), jax.ShapeDtypeStruct((t, D_XBC), F32),
                   jax.ShapeDtypeStruct((t, DT_PAD), F32), jax.ShapeDtypeStruct((t, D_S5), F32)),
        in_specs=[row(D_MODEL), pl.BlockSpec((1, N_MOD, D_MODEL), lambda i: (i // npb, 0, 0)), ANY,
                  pl.BlockSpec((4, D_XBC), lambda i: (0, 0)), pl.BlockSpec((1, D_XBC), lambda i: (0, 0))],
        out_specs=(row(D_SSD), row(D_XBC), row(D_XBC), row(D_XBC), row(DT_PAD), row(D_S5)),
        scratch_shapes=[pltpu.VMEM((D_INP, D_MODEL), BF16), pltpu.VMEM((tb + 8, D_XBC), F32), pltpu.SemaphoreType.DMA],
        compiler_params=_params(56),
    )(x2, mod3, w_in_pad, conv_w, conv_b)


N_PAIRS = N_HEADS // 2


def _split3(x):
    hi = x.astype(BF16)
    r = x - hi.astype(F32)
    mid = r.astype(BF16)
    lo = (r - mid.astype(F32)).astype(BF16)
    return hi, mid, lo


def _dot3(x, e, dims=(((1,), (0,)), ((), ()))):
    return sum(lax.dot_general(p, e, dims, preferred_element_type=F32) for p in _split3(x))


def _dot3_left(e, x, dims=(((1,), (0,)), ((), ()))):
    return sum(lax.dot_general(e, p, dims, preferred_element_type=F32) for p in _split3(x))


def _head_fold():
    return (jnp.arange(D_SSD)[:, None] // HEADDIM == jnp.arange(128)[None, :]).astype(BF16)


def _ssd_prep(dt_raw, par):
    dtb = par[0:1]
    a = -jnp.exp(par[1:2])
    dt = _softplus(dt_raw + dtb)
    adt = dt * a
    row = lax.broadcasted_iota(jnp.int32, (CHUNK, CHUNK), 0)
    col = lax.broadcasted_iota(jnp.int32, (CHUNK, CHUNK), 1)
    causal = row >= col
    tri = causal.astype(BF16)
    cs = _dot3_left(tri, adt)
    left = col < HEADDIM

    def lanes(v, h):
        return jnp.broadcast_to(v[:, h:h + 1], (CHUNK, 128))

    dt_c, cs_c, pair_cols = [], [], []
    for p in range(N_PAIRS):
        c0, c1 = lanes(cs, 2 * p), lanes(cs, 2 * p + 1)
        pair_cols.append(jnp.concatenate([c0, c1], axis=1))
        cs_c.append(jnp.where(left, c0, c1))
        dt_c.append(jnp.where(left, lanes(dt, 2 * p), lanes(dt, 2 * p + 1)))
    cs_c = jnp.concatenate(cs_c, axis=1)
    dt_c = jnp.concatenate(dt_c, axis=1)
    return dt, a, cs, cs.T, causal, tri, dt_c, jnp.exp(cs_c), jnp.exp(cs_c[CHUNK - 1:CHUNK, :] - cs_c), pair_cols


def _pair_decay(cols, cst, pair, causal2):
    rows = jnp.concatenate([jnp.broadcast_to(cst[2 * pair:2 * pair + 1, :], (CHUNK, CHUNK)),
                            jnp.broadcast_to(cst[2 * pair + 1:2 * pair + 2, :], (CHUNK, CHUNK))], axis=1)
    return jnp.exp(jnp.where(causal2, cols - rows, -jnp.inf))


def _stack_heads(xp, left):
    return jnp.concatenate([jnp.where(left, xp, 0.0), jnp.where(left, 0.0, xp)], axis=0).astype(BF16)


def _ssd_fwd(xbc, z, dt_raw, par, dsk, normw, seq):
    t = xbc.shape[0]
    nc = seq // CHUNK
    n_chunks = t // CHUNK

    def body(xbc_ref, z_ref, dt_ref, par_ref, dsk_ref, nw_ref, yraw_ref, ycat_ref, hprev_ref, h_ref):
        @pl.when(pl.program_id(0) % nc == 0)
        def _():
            h_ref[...] = jnp.zeros_like(h_ref)
        hprev_ref[0] = h_ref[...]
        _, _, cs, cst, causal, _, dt_c, ecs_c, w_c, pair_cols = _ssd_prep(dt_ref[...], par_ref[...])
        cs_last = cs[CHUNK - 1:CHUNK, :]
        causal2 = jnp.concatenate([causal, causal], axis=1)
        left = lax.broadcasted_iota(jnp.int32, (CHUNK, 128), 1) < HEADDIM
        x = xbc_ref[:, 0:D_SSD]
        xdt = x * dt_c
        amat = (w_c * xdt).astype(BF16)
        zz = z_ref[...]
        silu_z = zz * _sigmoid(zz)
        for g in range(N_GROUPS):
            gs = slice(g * GW, (g + 1) * GW)
            bg = xbc_ref[:, D_SSD + g * N_STATE:D_SSD + (g + 1) * N_STATE].astype(BF16)
            cg = xbc_ref[:, D_SSD + (N_GROUPS + g) * N_STATE:D_SSD + (N_GROUPS + g + 1) * N_STATE].astype(BF16)
            scores = lax.dot_general(cg, bg, NT, preferred_element_type=F32)
            scores2 = jnp.concatenate([scores, scores], axis=1)
            hg = h_ref[gs, :]
            p_all = lax.dot_general(cg, hg.astype(BF16), NT, preferred_element_type=F32)
            ys = []
            for q in range(GW // 128):
                pair = g * (GW // 128) + q
                decay = _pair_decay(pair_cols[pair], cst, pair, causal2)
                mcat = (scores2 * decay).astype(BF16)
                ys.append(jnp.dot(mcat, _stack_heads(xdt[:, pair * 128:(pair + 1) * 128], left),
                                  preferred_element_type=F32))
            yg = jnp.concatenate(ys, axis=1) + ecs_c[:, gs] * p_all + x[:, gs] * dsk_ref[:, gs]
            s_new = lax.dot_general(amat[:, gs], bg, TN, preferred_element_type=F32)
            for j in range(HPG):
                hh = g * HPG + j
                js = slice(j * HEADDIM, (j + 1) * HEADDIM)
                h_ref[g * GW + j * HEADDIM:g * GW + (j + 1) * HEADDIM, :] = (
                    hg[js, :] * jnp.exp(cs_last[:, hh:hh + 1]) + s_new[js, :])
            yraw_ref[:, gs] = yg
            v = yg * silu_z[:, gs]
            r = lax.rsqrt(jnp.mean(v * v, axis=-1, keepdims=True) + EPS)
            ycat_ref[:, gs] = (v * r * nw_ref[:, gs]).astype(BF16)

    row = lambda w: pl.BlockSpec((CHUNK, w), lambda i: (i, 0))
    full = lambda s: pl.BlockSpec(s, lambda i: (0,) * len(s))
    return pl.pallas_call(
        body, name="ssd_fwd", grid=(n_chunks,),
        out_shape=(jax.ShapeDtypeStruct((t, D_SSD), F32), jax.ShapeDtypeStruct((t, D_SSD + D_S5), BF16),
                   jax.ShapeDtypeStruct((n_chunks, D_SSD, N_STATE), F32)),
        in_specs=[row(D_XBC), row(D_SSD), row(DT_PAD), full((8, 128)), full((1, D_SSD)), full((1, D_SSD))],
        out_specs=(row(D_SSD), row(D_SSD), pl.BlockSpec((1, D_SSD, N_STATE), lambda i: (i, 0, 0))),
        scratch_shapes=[pltpu.VMEM((D_SSD, N_STATE), F32)],
        compiler_params=_params(40),
    )(xbc, z, dt_raw, par, dsk, normw)


S5_CW = 512
S5_BLOCKS = 4


def _tile_scan(in_re, in_im, out_re, out_im, carry_re, carry_im, pw_re, pw_im, n_tiles, reverse):
    steps = (1, 2, 4)
    for cc in range(S5_N // S5_CW):
        cols = slice(cc * S5_CW, (cc + 1) * S5_CW)
        a_re, a_im = pw_re[:, cols], pw_im[:, cols]
        rid = lax.broadcasted_iota(jnp.int32, (8, S5_CW), 0)
        pows = []
        for d in steps:
            k = 8 - d if reverse else d - 1
            keep = (rid < 8 - d) if reverse else (rid >= d)
            pows.append((jnp.where(keep, pw_re[k:k + 1, cols], 0.0), jnp.where(keep, pw_im[k:k + 1, cols], 0.0)))

        def tile(i, carry, cols=cols, pows=pows, a_re=a_re, a_im=a_im):
            r = (n_tiles - 1 - i) if reverse else i
            rows = pl.ds(pl.multiple_of(r * 8, 8), 8)
            xr, xi = in_re[rows, cols], in_im[rows, cols]
            for (pr, pi), d in zip(pows, steps):
                shift = 8 - d if reverse else d
                sr, si = pltpu.roll(xr, shift, axis=0), pltpu.roll(xi, shift, axis=0)
                xr, xi = xr + pr * sr - pi * si, xi + pr * si + pi * sr
            cr, ci = carry
            xr, xi = xr + a_re * cr - a_im * ci, xi + a_re * ci + a_im * cr
            out_re[rows, cols] = xr
            out_im[rows, cols] = xi
            edge = slice(0, 1) if reverse else slice(7, 8)
            return (jnp.broadcast_to(xr[edge], (8, S5_CW)), jnp.broadcast_to(xi[edge], (8, S5_CW)))

        c0 = (jnp.broadcast_to(carry_re[0:1, cols], (8, S5_CW)), jnp.broadcast_to(carry_im[0:1, cols], (8, S5_CW)))
        cr, ci = lax.fori_loop(0, n_tiles, tile, c0, unroll=True)
        carry_re[:, cols] = cr
        carry_im[:, cols] = ci


def _s5_params_math(ar, ai, ldt, br, bi):
    dt = jnp.exp(ldt)
    mag = jnp.exp(ar * dt)
    ang = ai * dt
    ab_re = mag * jnp.cos(ang)
    ab_im = mag * jnp.sin(ang)
    den = ar * ar + ai * ai
    n_re = ab_re - 1.0
    coef_re = (n_re * ar + ab_im * ai) / den
    coef_im = (ab_im * ar - n_re * ai) / den
    bb_re = coef_re * br - coef_im * bi
    bb_im = coef_re * bi + coef_im * br
    return ab_re, ab_im, bb_re, bb_im


def _s5_params_fwd(ar, ai, ldt, br, bi):
    def body(ar_ref, ai_ref, ldt_ref, br_ref, bi_ref, bbr_ref, bbi_ref, pfr_ref, pfi_ref, prr_ref, pri_ref):
        ab_re, ab_im, bb_re, bb_im = _s5_params_math(ar_ref[...], ai_ref[...], ldt_ref[...], br_ref[...], bi_ref[...])
        bbr_ref[...] = bb_re
        bbi_ref[...] = bb_im
        pr, pi = ab_re, ab_im
        for k in range(8):
            pfr_ref[k:k + 1, :] = pr
            pfi_ref[k:k + 1, :] = pi
            prr_ref[7 - k:8 - k, :] = pr
            pri_ref[7 - k:8 - k, :] = -pi
            pr, pi = pr * ab_re - pi * ab_im, pr * ab_im + pi * ab_re

    b16 = jax.ShapeDtypeStruct((S5_CH, S5_N), F32)
    p8 = jax.ShapeDtypeStruct((8, S5_N), F32)
    return pl.pallas_call(body, name="s5_params_fwd", out_shape=(b16, b16, p8, p8, p8, p8),
                          compiler_params=_params(32))(ar, ai, ldt, br, bi)


def _s5_params_bwd(ar, ai, ldt, br, bi, d_ab_re, d_ab_im, d_bb_re, d_bb_im):
    def body(ar_ref, ai_ref, ldt_ref, br_ref, bi_ref, dar_ref, dai_ref, dbr_ref, dbi_ref,
             gar_ref, gai_ref, gldt_ref, gbr_ref, gbi_ref):
        _, vjp = jax.vjp(_s5_params_math, ar_ref[...], ai_ref[...], ldt_ref[...], br_ref[...], bi_ref[...])
        g_ar, g_ai, g_ldt, g_br, g_bi = vjp((dar_ref[...], dai_ref[...], dbr_ref[...], dbi_ref[...]))
        gar_ref[...] = g_ar
        gai_ref[...] = g_ai
        gbr_ref[...] = g_br
        gbi_ref[...] = g_bi
        lane = lax.broadcasted_iota(jnp.int32, (S5_N, 128), 0) // S5_P
        grp = lax.broadcasted_iota(jnp.int32, (S5_N, 128), 1)
        fold = (lane == grp).astype(F32)
        gldt_ref[...] = jnp.dot(g_ldt, fold, preferred_element_type=F32, precision=HIGHEST)

    v1 = jax.ShapeDtypeStruct((1, S5_N), F32)
    b16 = jax.ShapeDtypeStruct((S5_CH, S5_N), F32)
    return pl.pallas_call(body, name="s5_params_bwd",
                          out_shape=(v1, v1, jax.ShapeDtypeStruct((1, 128), F32), b16, b16),
                          compiler_params=_params(32))(ar, ai, ldt, br, bi, d_ab_re, d_ab_im, d_bb_re, d_bb_im)


def _s5_fwd(u5, bb_re, bb_im, cc_re, cc_im, pf_re, pf_im, s5d, w_glu, b_glu, ycat, seq):
    t = u5.shape[0]
    tb = 256
    npb = seq // tb

    def body(u_ref, bbr_ref, bbi_ref, ccr_ref, cci_ref, pfr_ref, pfi_ref, d_ref, wg_ref, bg_ref, ycat_hbm,
             sre_ref, sim_ref, ypre_ref, y5_ref, bur, bui, car, cai):
        del ycat_hbm

        @pl.when(pl.program_id(0) % npb == 0)
        def _():
            car[...] = jnp.zeros_like(car)
            cai[...] = jnp.zeros_like(cai)
        u = u_ref[...]
        ub = u.astype(BF16)
        for j in range(S5_BLOCKS):
            ch, st = slice(j * 128, (j + 1) * 128), slice(j * 512, (j + 1) * 512)
            bur[:, st] = jnp.dot(ub[:, ch], bbr_ref[j], preferred_element_type=F32)
            bui[:, st] = jnp.dot(ub[:, ch], bbi_ref[j], preferred_element_type=F32)
        _tile_scan(bur, bui, sre_ref, sim_ref, car, cai, pfr_ref, pfi_ref, tb // 8, reverse=False)
        cs_y = []
        for j in range(S5_BLOCKS):
            st = slice(j * 512, (j + 1) * 512)
            cs_y.append(_mm(sre_ref[:, st], ccr_ref[j]) - _mm(sim_ref[:, st], cci_ref[j]))
        ypre = jnp.concatenate(cs_y, axis=1) + u * d_ref[...]
        ypre_ref[...] = ypre
        yg = _gelu(ypre)
        y5_ref[...] = (yg * _sigmoid(_mm(yg, wg_ref[...]) + bg_ref[...])).astype(BF16)

    row = lambda w: pl.BlockSpec((tb, w), lambda i: (i, 0))
    full = lambda a: pl.BlockSpec(a.shape, lambda i: (0,) * a.ndim)
    return pl.pallas_call(
        body, name="s5_fwd", grid=(t // tb,),
        out_shape=(jax.ShapeDtypeStruct((t, S5_N), F32), jax.ShapeDtypeStruct((t, S5_N), F32),
                   jax.ShapeDtypeStruct((t, D_S5), F32), jax.ShapeDtypeStruct(ycat.shape, BF16)),
        in_specs=[row(D_S5), full(bb_re), full(bb_im), full(cc_re), full(cc_im), full(pf_re), full(pf_im),
                  full(s5d), full(w_glu), full(b_glu), ANY],
        out_specs=(row(S5_N), row(S5_N), row(D_S5), pl.BlockSpec((tb, D_S5), lambda i: (i, D_SSD // D_S5))),
        input_output_aliases={10: 3},
        scratch_shapes=[pltpu.VMEM((tb, S5_N), F32), pltpu.VMEM((tb, S5_N), F32),
                        pltpu.VMEM((8, S5_N), F32), pltpu.VMEM((8, S5_N), F32)],
        compiler_params=_params(48),
    )(u5, bb_re, bb_im, cc_re, cc_im, pf_re, pf_im, s5d, w_glu, b_glu, ycat)


def _layer_norm(r, g, b):
    mu = jnp.mean(r, axis=-1, keepdims=True)
    xc = r - mu
    rstd = lax.rsqrt(jnp.mean(xc * xc, axis=-1, keepdims=True) + EPS)
    xhat = xc * rstd
    return xhat * g + b, xhat, rstd


def _layer_norm_bwd(dy, xhat, rstd, g):
    dxhat = dy * g
    return rstd * (dxhat - jnp.mean(dxhat, axis=-1, keepdims=True)
                   - xhat * jnp.mean(dxhat * xhat, axis=-1, keepdims=True))


def _out_ln1(ycat, x2, mod3, w_out, ln1, seq):
    t = x2.shape[0]
    tb = 512
    npb = seq // tb

    def body(y_ref, x_ref, mod_ref, w_ref, ln_ref, mix_ref, x1_ref):
        m = mod_ref[0]
        mix = jnp.dot(y_ref[...], w_ref[...], preferred_element_type=F32)
        mix_ref[...] = mix
        r1 = ALPHA * x_ref[...] + (1.0 + m[2:3]) * mix
        x1_ref[...] = _layer_norm(r1, ln_ref[0:1], ln_ref[1:2])[0]

    row = lambda w: pl.BlockSpec((tb, w), lambda i: (i, 0))
    return pl.pallas_call(
        body, name="out_ln1", grid=(t // tb,),
        out_shape=(jax.ShapeDtypeStruct((t, D_MODEL), F32), jax.ShapeDtypeStruct((t, D_MODEL), F32)),
        in_specs=[row(D_SSD + D_S5), row(D_MODEL), pl.BlockSpec((1, N_MOD, D_MODEL), lambda i: (i // npb, 0, 0)),
                  pl.BlockSpec(w_out.shape, lambda i: (0, 0)), pl.BlockSpec(ln1.shape, lambda i: (0, 0))],
        out_specs=(row(D_MODEL), row(D_MODEL)), compiler_params=_params(48),
    )(ycat, x2, mod3, w_out, ln1)


def _mlp_fwd_bwd(x1, tgt, mod3, w1, w2, vec1, b1, seq):
    t = x1.shape[0]
    tb = 256
    npb = seq // tb
    n_fb, _, fb = w1.shape

    def body(x1_ref, tgt_ref, mod_ref, w1_hbm, w2_hbm, v_ref, b1_ref,
             dx1_ref, u2_ref, h_ref, dhp_ref, do_ref, gacc_ref, db1_ref, bacc_ref, w1_v, w2_v, sem1, sem2):
        i = pl.program_id(0)
        @pl.when(i == 0)
        def _():
            cps = [pltpu.make_async_copy(w1_hbm.at[k], w1_v.at[:, k * fb:(k + 1) * fb], sem1.at[k])
                   for k in range(n_fb)]
            for cp in cps:
                cp.start()
            for cp in cps:
                cp.wait()
        _load_once(w2_hbm, w2_v, sem2)

        @pl.when(i == 0)
        def _():
            gacc_ref[...] = jnp.zeros_like(gacc_ref)
            db1_ref[...] = jnp.zeros_like(db1_ref)

        @pl.when(i % npb == 0)
        def _():
            bacc_ref[...] = jnp.zeros_like(bacc_ref)

        m = mod_ref[0]
        sh2, sc2, g2 = m[3:4], m[4:5], m[5:6]
        x1v = x1_ref[...]
        u2 = (x1v * (1.0 + sc2) + sh2).astype(BF16)
        u2_ref[...] = u2
        hr = jnp.maximum(jnp.dot(u2, w1_v[...], preferred_element_type=F32) + b1_ref[...], 0.0)
        hb = (hr * hr).astype(BF16)
        h_ref[...] = hb
        o = jnp.dot(hb, w2_v[...], preferred_element_type=F32) + v_ref[0:1]
        r2 = ALPHA * x1v + (1.0 + g2) * o
        y, xhat, rstd = _layer_norm(r2, v_ref[1:2], v_ref[2:3])
        err = y - tgt_ref[...]
        dy = err * (1.0 / D_MODEL)
        dr2 = _layer_norm_bwd(dy, xhat, rstd, v_ref[1:2])
        do = (1.0 + g2) * dr2
        dob = do.astype(BF16)
        do_ref[...] = dob
        gacc_ref[0:1, :] += jnp.sum(dy * xhat, axis=0, keepdims=True)
        gacc_ref[1:2, :] += jnp.sum(dy, axis=0, keepdims=True)
        gacc_ref[2:3, :] += jnp.sum(do, axis=0, keepdims=True)
        gacc_ref[3:4, :] += jnp.sum(err * err, axis=0, keepdims=True)
        dhpre = lax.dot_general(dob, w2_v[...], NT, preferred_element_type=F32) * (2.0 * hr)
        dhpb = dhpre.astype(BF16)
        dhp_ref[...] = dhpb
        db1_ref[...] += jnp.sum(dhpre, axis=0, keepdims=True)
        du2 = lax.dot_general(dhpb, w1_v[...], NT, preferred_element_type=F32)
        dx1_ref[...] = ALPHA * dr2 + du2 * (1.0 + sc2)
        bacc_ref[0, 0:1, :] += jnp.sum(du2, axis=0, keepdims=True)
        bacc_ref[0, 1:2, :] += jnp.sum(du2 * x1v, axis=0, keepdims=True)
        bacc_ref[0, 2:3, :] += jnp.sum(dr2 * o, axis=0, keepdims=True)

    row = lambda w: pl.BlockSpec((tb, w), lambda i: (i, 0))
    return pl.pallas_call(
        body, name="mlp_fwd_bwd", grid=(t // tb,),
        out_shape=(jax.ShapeDtypeStruct((t, D_MODEL), F32), jax.ShapeDtypeStruct((t, D_MODEL), BF16),
                   jax.ShapeDtypeStruct((t, D_FF), BF16), jax.ShapeDtypeStruct((t, D_FF), BF16),
                   jax.ShapeDtypeStruct((t, D_MODEL), BF16), jax.ShapeDtypeStruct((8, D_MODEL), F32),
                   jax.ShapeDtypeStruct((1, D_FF), F32), jax.ShapeDtypeStruct((t // seq, 8, D_MODEL), F32)),
        in_specs=[row(D_MODEL), row(D_MODEL), pl.BlockSpec((1, N_MOD, D_MODEL), lambda i: (i // npb, 0, 0)), ANY, ANY,
                  pl.BlockSpec(vec1.shape, lambda i: (0, 0)), pl.BlockSpec(b1.shape, lambda i: (0, 0))],
        out_specs=(row(D_MODEL), row(D_MODEL), row(D_FF), row(D_FF), row(D_MODEL),
                   pl.BlockSpec((8, D_MODEL), lambda i: (0, 0)), pl.BlockSpec((1, D_FF), lambda i: (0, 0)),
                   pl.BlockSpec((1, 8, D_MODEL), lambda i: (i // npb, 0, 0))),
        scratch_shapes=[pltpu.VMEM((D_MODEL, n_fb * fb), BF16), pltpu.VMEM((D_FF, D_MODEL), BF16),
                        pltpu.SemaphoreType.DMA((n_fb,)), pltpu.SemaphoreType.DMA],
        compiler_params=_params(60),
    )(x1, tgt, mod3, w1, w2, vec1, b1)


def _ln1_out_bwd(dx1, x2, mix, mod3, w_out, ln1, seq):
    t = x2.shape[0]
    tb = 512
    npb = seq // tb

    def body(dx1_ref, x_ref, mix_ref, mod_ref, w_ref, ln_ref, dmix_ref, dxa_ref, dys_ref, dy5_ref, gacc_ref, bacc_ref):
        i = pl.program_id(0)

        @pl.when(i == 0)
        def _():
            gacc_ref[...] = jnp.zeros_like(gacc_ref)

        @pl.when(i % npb == 0)
        def _():
            bacc_ref[...] = jnp.zeros_like(bacc_ref)

        m = mod_ref[0]
        mix = mix_ref[...]
        r1 = ALPHA * x_ref[...] + (1.0 + m[2:3]) * mix
        _, xhat, rstd = _layer_norm(r1, ln_ref[0:1], ln_ref[1:2])
        dx1v = dx1_ref[...]
        dr1 = _layer_norm_bwd(dx1v, xhat, rstd, ln_ref[0:1])
        gacc_ref[0:1, :] += jnp.sum(dx1v * xhat, axis=0, keepdims=True)
        gacc_ref[1:2, :] += jnp.sum(dx1v, axis=0, keepdims=True)
        bacc_ref[0, 0:1, :] += jnp.sum(dr1 * mix, axis=0, keepdims=True)
        dmix = ((1.0 + m[2:3]) * dr1).astype(BF16)
        dmix_ref[...] = dmix
        dxa_ref[...] = ALPHA * dr1
        dys_ref[...] = lax.dot_general(dmix, w_ref[0:D_SSD, :], NT, preferred_element_type=F32)
        dy5_ref[...] = lax.dot_general(dmix, w_ref[D_SSD:, :], NT, preferred_element_type=F32)

    row = lambda w: pl.BlockSpec((tb, w), lambda i: (i, 0))
    return pl.pallas_call(
        body, name="ln1_out_bwd", grid=(t // tb,),
        out_shape=(jax.ShapeDtypeStruct((t, D_MODEL), BF16), jax.ShapeDtypeStruct((t, D_MODEL), F32),
                   jax.ShapeDtypeStruct((t, D_SSD), F32), jax.ShapeDtypeStruct((t, D_S5), F32),
                   jax.ShapeDtypeStruct((8, D_MODEL), F32), jax.ShapeDtypeStruct((t // seq, 8, D_MODEL), F32)),
        in_specs=[row(D_MODEL), row(D_MODEL), row(D_MODEL), pl.BlockSpec((1, N_MOD, D_MODEL), lambda i: (i // npb, 0, 0)),
                  pl.BlockSpec(w_out.shape, lambda i: (0, 0)), pl.BlockSpec(ln1.shape, lambda i: (0, 0))],
        out_specs=(row(D_MODEL), row(D_MODEL), row(D_SSD), row(D_S5), pl.BlockSpec((8, D_MODEL), lambda i: (0, 0)),
                   pl.BlockSpec((1, 8, D_MODEL), lambda i: (i // npb, 0, 0))),
        compiler_params=_params(48),
    )(dx1, x2, mix, mod3, w_out, ln1)


def _s5_bwd(dy5, ypre, u5, s_re, s_im, bb_re, bb_im, cc_re, cc_im, pr_re, pr_im, s5d, w_glu, b_glu, seq):
    t = u5.shape[0]
    tb = 256
    npb = seq // tb
    n_blocks = t // tb

    def blk(i):
        return (i // npb) * npb + (npb - 1 - i % npb)

    def body(dy_ref, ypre_ref, u_ref, sre_ref, sim_ref, hre_ref, him_ref, bbr_ref, bbi_ref, ccr_ref, cci_ref,
             prr_ref, pri_ref, d_ref, wg_ref, bg_ref,
             du_ref, vacc_ref, sacc_ref, dcc_ref, dbb_ref, dwg_ref, dsr, dsi, gr, gi, car, cai):
        i = pl.program_id(0)

        @pl.when(i == 0)
        def _():
            for acc in (vacc_ref, sacc_ref, dcc_ref, dbb_ref, dwg_ref):
                acc[...] = jnp.zeros_like(acc)

        @pl.when(i % npb == 0)
        def _():
            car[...] = jnp.zeros_like(car)
            cai[...] = jnp.zeros_like(cai)

        dy = dy_ref[...]
        ypre = ypre_ref[...]
        u = u_ref[...]
        ub = u.astype(BF16)
        yg = _gelu(ypre)
        sg = _sigmoid(_mm(yg, wg_ref[...]) + bg_ref[...])
        dq = dy * yg * sg * (1.0 - sg)
        dqb = dq.astype(BF16)
        dyg = dy * sg + lax.dot_general(dqb, wg_ref[...], NT, preferred_element_type=F32)
        dyp = dyg * _gelu_grad(ypre)
        dypb = dyp.astype(BF16)
        dwg_ref[...] += lax.dot_general(yg.astype(BF16), dqb, TN, preferred_element_type=F32)
        blocks = [(slice(j * 128, (j + 1) * 128), slice(j * 512, (j + 1) * 512)) for j in range(S5_BLOCKS)]
        for j, (ch, st) in enumerate(blocks):
            dsr[:, st] = lax.dot_general(dypb[:, ch], ccr_ref[j], NT, preferred_element_type=F32)
            dsi[:, st] = -lax.dot_general(dypb[:, ch], cci_ref[j], NT, preferred_element_type=F32)
        _tile_scan(dsr, dsi, gr, gi, car, cai, prr_ref, pri_ref, tb // 8, reverse=True)
        g_re, g_im = gr[...], gi[...]
        first_rows = (i % npb) == npb - 1
        hre = jnp.where(first_rows, 0.0, hre_ref[...])
        him = jnp.where(first_rows, 0.0, him_ref[...])
        s_re_v, s_im_v = sre_ref[...], sim_ref[...]
        sp_re = pltpu.roll(jnp.concatenate([hre, s_re_v], axis=0), 1, axis=0)[8:8 + tb]
        sp_im = pltpu.roll(jnp.concatenate([him, s_im_v], axis=0), 1, axis=0)[8:8 + tb]
        vacc_ref[0:1, :] += jnp.sum(g_re * sp_re + g_im * sp_im, axis=0, keepdims=True)
        vacc_ref[1:2, :] += jnp.sum(g_im * sp_re - g_re * sp_im, axis=0, keepdims=True)
        grb, gib = g_re.astype(BF16), g_im.astype(BF16)
        srb, sib = s_re_v.astype(BF16), s_im_v.astype(BF16)
        du_cols = []
        for j, (ch, st) in enumerate(blocks):
            dcc_ref[j] += lax.dot_general(srb[:, st], dypb[:, ch], TN, preferred_element_type=F32)
            dcc_ref[S5_BLOCKS + j] -= lax.dot_general(sib[:, st], dypb[:, ch], TN, preferred_element_type=F32)
            dbb_ref[j] += lax.dot_general(ub[:, ch], grb[:, st], TN, preferred_element_type=F32)
            dbb_ref[S5_BLOCKS + j] += lax.dot_general(ub[:, ch], gib[:, st], TN, preferred_element_type=F32)
            du_cols.append(lax.dot_general(grb[:, st], bbr_ref[j], NT, preferred_element_type=F32)
                           + lax.dot_general(gib[:, st], bbi_ref[j], NT, preferred_element_type=F32))
        du_ref[...] = jnp.concatenate(du_cols, axis=1) + dyp * d_ref[...]
        sacc_ref[0:1, :] += jnp.sum(dyp * u, axis=0, keepdims=True)
        sacc_ref[1:2, :] += jnp.sum(dq, axis=0, keepdims=True)

    row = lambda w: pl.BlockSpec((tb, w), lambda i: (blk(i), 0))
    halo = pl.BlockSpec((8, S5_N), lambda i: (jnp.maximum(blk(i) * (tb // 8) - 1, 0), 0))
    full = lambda a: pl.BlockSpec(a.shape, lambda i: (0,) * a.ndim)
    acc = lambda s: pl.BlockSpec(s, lambda i: (0,) * len(s))
    acc_shapes = [(8, S5_N), (8, D_S5), (2 * S5_BLOCKS, 512, 128), (2 * S5_BLOCKS, 128, 512), (D_S5, D_S5)]
    return pl.pallas_call(
        body, name="s5_bwd", grid=(n_blocks,),
        out_shape=(jax.ShapeDtypeStruct((t, D_S5), F32),) + tuple(jax.ShapeDtypeStruct(s, F32) for s in acc_shapes),
        in_specs=[row(D_S5), row(D_S5), row(D_S5), row(S5_N), row(S5_N), halo, halo, full(bb_re), full(bb_im),
                  full(cc_re), full(cc_im), full(pr_re), full(pr_im), full(s5d), full(w_glu), full(b_glu)],
        out_specs=(row(D_S5),) + tuple(acc(s) for s in acc_shapes),
        scratch_shapes=[pltpu.VMEM((tb, S5_N), F32), pltpu.VMEM((tb, S5_N), F32), pltpu.VMEM((tb, S5_N), F32),
                        pltpu.VMEM((tb, S5_N), F32), pltpu.VMEM((8, S5_N), F32), pltpu.VMEM((8, S5_N), F32)],
        compiler_params=_params(56),
    )(dy5, ypre, u5, s_re, s_im, s_re, s_im, bb_re, bb_im, cc_re, cc_im, pr_re, pr_im, s5d, w_glu, b_glu)


def _ssd_bwd(dyssd, yraw, z, xbc, dt_raw, hprev, par, dsk, normw, seq):
    t = xbc.shape[0]
    nc = seq // CHUNK
    n_chunks = t // CHUNK
    fold = _head_fold()

    def blk(i):
        return (i // nc) * nc + (nc - 1 - i % nc)

    def body(dy_ref, yraw_ref, z_ref, xbc_ref, dt_ref, hprev_ref, par_ref, dsk_ref, nw_ref, fold_ref,
             dxbc_ref, dz_ref, ddt_ref, dpar_ref, cacc_ref, dh_ref, dyr_ref):
        i = pl.program_id(0)

        @pl.when(i == 0)
        def _():
            dpar_ref[...] = jnp.zeros_like(dpar_ref)
            cacc_ref[...] = jnp.zeros_like(cacc_ref)

        @pl.when(i % nc == 0)
        def _():
            dh_ref[...] = jnp.zeros_like(dh_ref)

        zz = z_ref[...]
        sz = _sigmoid(zz)
        silu_z = zz * sz
        yraw = yraw_ref[...]
        for g in range(N_GROUPS):
            sl = slice(g * GW, (g + 1) * GW)
            v = yraw[:, sl] * silu_z[:, sl]
            r = lax.rsqrt(jnp.mean(v * v, axis=-1, keepdims=True) + EPS)
            dyg = dy_ref[:, sl]
            cacc_ref[1:2, sl] += jnp.sum(dyg * v * r, axis=0, keepdims=True)
            dyw = dyg * nw_ref[:, sl]
            dv = r * dyw - v * (r * r * r) * jnp.mean(dyw * v, axis=-1, keepdims=True)
            dyr_ref[:, sl] = dv * silu_z[:, sl]
            dz_ref[:, sl] = dv * yraw[:, sl] * (sz[:, sl] * (1.0 + zz[:, sl] * (1.0 - sz[:, sl])))

        dt, a, cs, cst, causal, tri, dt_c, ecs_c, w_c, pair_cols = _ssd_prep(dt_ref[...], par_ref[...])
        cs_last = cs[CHUNK - 1:CHUNK, :]
        causal2 = jnp.concatenate([causal, causal], axis=1)
        lane = lax.broadcasted_iota(jnp.int32, (CHUNK, 128), 1)
        left = lane < HEADDIM
        lane1 = lax.broadcasted_iota(jnp.int32, (1, 128), 1)
        x = xbc_ref[:, 0:D_SSD]
        xdt = x * dt_c
        dyr = dyr_ref[...]
        dyrb = dyr.astype(BF16)
        cacc_ref[0:1, :] += jnp.sum(dyr * x, axis=0, keepdims=True)
        dlast = jnp.zeros((1, 128), F32)
        dxdt_cols, diag_all, dww_cols = [], [], []
        for g in range(N_GROUPS):
            gs = slice(g * GW, (g + 1) * GW)
            b_sl = slice(D_SSD + g * N_STATE, D_SSD + (g + 1) * N_STATE)
            c_sl = slice(D_SSD + (N_GROUPS + g) * N_STATE, D_SSD + (N_GROUPS + g + 1) * N_STATE)
            bg = xbc_ref[:, b_sl].astype(BF16)
            cg = xbc_ref[:, c_sl].astype(BF16)
            scores = lax.dot_general(cg, bg, NT, preferred_element_type=F32)
            scores2 = jnp.concatenate([scores, scores], axis=1)
            hg = hprev_ref[0, gs, :]
            hgb = hg.astype(BF16)
            dhg = dh_ref[gs, :]
            dhgb = dhg.astype(BF16)
            q_all = lax.dot_general(bg, dhgb, NT, preferred_element_type=F32)
            dscores = jnp.zeros((CHUNK, CHUNK), F32)
            diag_cols = []
            for q in range(GW // 128):
                pair = g * (GW // 128) + q
                ps = slice(pair * 128, (pair + 1) * 128)
                decay = _pair_decay(pair_cols[pair], cst, pair, causal2)
                mcat = (scores2 * decay).astype(BF16)
                dyp = dyrb[:, ps]
                dm = lax.dot_general(dyp, _stack_heads(xdt[:, ps], left), NT, preferred_element_type=F32)
                dmd = dm * decay
                dscores = dscores + dmd[:, 0:CHUNK] + dmd[:, CHUNK:]
                rr = lax.dot_general(mcat, dyp, TN, preferred_element_type=F32)
                diag_cols.append(jnp.where(left, rr[0:CHUNK], rr[CHUNK:]))
            wq = w_c[:, gs] * q_all
            diag_g = jnp.concatenate(diag_cols, axis=1)
            diag_all.append(diag_g)
            dxdt_cols.append(diag_g + wq)
            dww_cols.append(wq * xdt[:, gs])
            dp = (ecs_c[:, gs] * dyr[:, gs]).astype(BF16)
            amat = (w_c[:, gs] * xdt[:, gs]).astype(BF16)
            dsb = dscores.astype(BF16)
            dxbc_ref[:, c_sl] = (jnp.dot(dsb, bg, preferred_element_type=F32)
                                 + jnp.dot(dp, hgb, preferred_element_type=F32))
            dxbc_ref[:, b_sl] = (lax.dot_general(dsb, cg, TN, preferred_element_type=F32)
                                 + jnp.dot(amat, dhgb, preferred_element_type=F32))
            dh_in = lax.dot_general(dp, cg, TN, preferred_element_type=F32)
            for j in range(HPG):
                hh = g * HPG + j
                js = slice(j * HEADDIM, (j + 1) * HEADDIM)
                ecl = jnp.exp(cs_last[:, hh:hh + 1])
                dlast = dlast + jnp.where(lane1 == hh, ecl * jnp.sum(dhg[js, :] * hg[js, :]), 0.0)
                dh_ref[g * GW + j * HEADDIM:g * GW + (j + 1) * HEADDIM, :] = ecl * dhg[js, :] + dh_in[js, :]
        dxdt = jnp.concatenate(dxdt_cols, axis=1)
        dxbc_ref[:, 0:D_SSD] = dxdt * dt_c + dyr * dsk_ref[...]
        dww = _dot3(jnp.concatenate(dww_cols, axis=1), fold_ref[...])
        dcs = _dot3(dyrb.astype(F32) * (yraw - x * dsk_ref[...])
                    - xdt.astype(BF16).astype(F32) * jnp.concatenate(diag_all, axis=1), fold_ref[...]) - dww
        rowid = lax.broadcasted_iota(jnp.int32, (CHUNK, 128), 0)
        dcs = dcs + jnp.where(rowid == CHUNK - 1, jnp.sum(dww, axis=0, keepdims=True) + dlast, 0.0)
        dadt = _dot3_left(tri, dcs, TN)
        ddt = _dot3(dxdt * x, fold_ref[...]) + dadt * a
        da = jnp.sum(dadt * dt, axis=0, keepdims=True)
        ddt_raw = ddt * _sigmoid(dt_ref[...] + par_ref[0:1])
        ddt_raw = jnp.where(lane < N_HEADS, ddt_raw, 0.0)
        ddt_ref[...] = ddt_raw
        dpar_ref[0:1, :] += jnp.sum(ddt_raw, axis=0, keepdims=True)
        dpar_ref[1:2, :] += jnp.where(lane1 < N_HEADS, da * a, 0.0)

    row = lambda w: pl.BlockSpec((CHUNK, w), lambda i: (blk(i), 0))
    full = lambda s: pl.BlockSpec(s, lambda i: (0,) * len(s))
    return pl.pallas_call(
        body, name="ssd_bwd", grid=(n_chunks,),
        out_shape=(jax.ShapeDtypeStruct((t, D_XBC), F32), jax.ShapeDtypeStruct((t, D_SSD), F32),
                   jax.ShapeDtypeStruct((t, DT_PAD), F32), jax.ShapeDtypeStruct((8, 128), F32),
                   jax.ShapeDtypeStruct((8, D_SSD), F32)),
        in_specs=[row(D_SSD), row(D_SSD), row(D_SSD), row(D_XBC), row(DT_PAD),
                  pl.BlockSpec((1, D_SSD, N_STATE), lambda i: (blk(i), 0, 0)),
                  full((8, 128)), full((1, D_SSD)), full((1, D_SSD)), full(fold.shape)],
        out_specs=(row(D_XBC), row(D_SSD), row(DT_PAD), full((8, 128)), full((8, D_SSD))),
        scratch_shapes=[pltpu.VMEM((D_SSD, N_STATE), F32), pltpu.VMEM((CHUNK, D_SSD), F32)],
        compiler_params=_params(48),
    )(dyssd, yraw, z, xbc, dt_raw, hprev, par, dsk, normw, fold)


def _conv_bwd(dxbc, dsilu, xbc_pre, seq):
    t = xbc_pre.shape[0]
    tb = 512
    npb = seq // tb
    cw = 640

    def body(d_ref, ds_ref, cur_ref, halo_ref, o_ref, acc_ref, win):
        i = pl.program_id(1)

        @pl.when(i == 0)
        def _():
            acc_ref[...] = jnp.zeros_like(acc_ref)

        first = (i % npb) == 0
        win[0:8, :] = jnp.where(first, 0.0, halo_ref[...])
        win[8:8 + tb, :] = cur_ref[...]
        dpre = d_ref[...] * ds_ref[...]
        o_ref[...] = dpre
        for j in range(4):
            acc_ref[3 - j:4 - j, :] += jnp.sum(dpre * win[8 - j:8 - j + tb, :], axis=0, keepdims=True)
        acc_ref[4:5, :] += jnp.sum(dpre, axis=0, keepdims=True)

    blk = pl.BlockSpec((tb, cw), lambda j, i: (i, j))
    return pl.pallas_call(
        body, name="conv_bwd", grid=(D_XBC // cw, t // tb),
        out_shape=(jax.ShapeDtypeStruct((t, D_XBC), F32), jax.ShapeDtypeStruct((8, D_XBC), F32)),
        in_specs=[blk, blk, blk, pl.BlockSpec((8, cw), lambda j, i: (jnp.maximum(i * (tb // 8) - 1, 0), j))],
        out_specs=(blk, pl.BlockSpec((8, cw), lambda j, i: (0, j))),
        scratch_shapes=[pltpu.VMEM((tb + 8, cw), F32)],
        compiler_params=_params(32),
    )(dxbc, dsilu, xbc_pre, xbc_pre)


def _proj_bwd(dz, dpre, ddt, du5, x2, dxa, mod3, conv_w, w_in_pad, seq):
    t = x2.shape[0]
    tb = 512
    npb = seq // tb
    n_blocks = t // tb

    def body(dz_ref, dp_ref, nxt_ref, ddt_ref, du5_ref, x_ref, dxa_ref, mod_ref, cw_ref, w_hbm,
             gx_ref, u_ref, dxp_ref, bacc_ref, w_vmem, sem):
        i = pl.program_id(0)
        _load_once(w_hbm, w_vmem, sem)

        @pl.when(i % npb == 0)
        def _():
            bacc_ref[...] = jnp.zeros_like(bacc_ref)

        last = (i % npb) == npb - 1
        nxt = jnp.where(last, 0.0, nxt_ref[...])
        cur = dp_ref[...]
        xx = jnp.concatenate([cur, nxt], axis=0)
        w = cw_ref[...]
        dxp = w[3:4] * cur
        for j in (1, 2, 3):
            dxp = dxp + w[3 - j:4 - j] * pltpu.roll(xx, tb + 8 - j, axis=0)[0:tb]
        dxpb = dxp.astype(BF16)
        dxp_ref[...] = dxpb
        o1, o2, o3 = D_SSD, D_SSD + D_XBC, D_SSD + D_XBC + DT_PAD
        du = (jnp.dot(dz_ref[...].astype(BF16), w_vmem[0:o1, :], preferred_element_type=F32)
              + jnp.dot(dxpb, w_vmem[o1:o2, :], preferred_element_type=F32)
              + jnp.dot(ddt_ref[...].astype(BF16), w_vmem[o2:o3, :], preferred_element_type=F32)
              + jnp.dot(du5_ref[...].astype(BF16), w_vmem[o3:, :], preferred_element_type=F32))
        m = mod_ref[0]
        xv = x_ref[...]
        u_ref[...] = (xv * (1.0 + m[1:2]) + m[0:1]).astype(BF16)
        gx_ref[...] = dxa_ref[...] + du * (1.0 + m[1:2])
        bacc_ref[0, 0:1, :] += jnp.sum(du, axis=0, keepdims=True)
        bacc_ref[0, 1:2, :] += jnp.sum(du * xv, axis=0, keepdims=True)

    row = lambda w: pl.BlockSpec((tb, w), lambda i: (i, 0))
    nxt_rows = pl.BlockSpec((8, D_XBC), lambda i: (jnp.minimum((i + 1) * (tb // 8), t // 8 - 1), 0))
    return pl.pallas_call(
        body, name="proj_bwd", grid=(n_blocks,),
        out_shape=(jax.ShapeDtypeStruct((t, D_MODEL), F32), jax.ShapeDtypeStruct((t, D_MODEL), BF16),
                   jax.ShapeDtypeStruct((t, D_XBC), BF16), jax.ShapeDtypeStruct((t // seq, 8, D_MODEL), F32)),
        in_specs=[row(D_SSD), row(D_XBC), nxt_rows, row(DT_PAD), row(D_S5), row(D_MODEL), row(D_MODEL),
                  pl.BlockSpec((1, N_MOD, D_MODEL), lambda i: (i // npb, 0, 0)),
                  pl.BlockSpec((4, D_XBC), lambda i: (0, 0)), ANY],
        out_specs=(row(D_MODEL), row(D_MODEL), row(D_XBC), pl.BlockSpec((1, 8, D_MODEL), lambda i: (i // npb, 0, 0))),
        scratch_shapes=[pltpu.VMEM((D_INP, D_MODEL), BF16), pltpu.SemaphoreType.DMA],
        compiler_params=_params(60),
    )(dz, dpre, dpre, ddt, du5, x2, dxa, mod3, conv_w, w_in_pad)


def _pad_rows(a, mult):
    r = a.shape[0]
    pad = (-r) % mult
    return a if pad == 0 else jnp.concatenate([a, jnp.zeros((pad,) + a.shape[1:], a.dtype)], axis=0)


_SMALL = ["conv_w", "conv_b", "dt_bias", "a_log", "d_ssd", "norm_w", "s5_a_re", "s5_a_im", "s5_log_dt", "s5_b_re",
          "s5_b_im", "s5_c_re", "s5_c_im", "s5_d", "b_glu", "ln1_g", "ln1_b", "b1", "b2", "ln2_g", "ln2_b"]


def _tile_rows(size):
    return 8 * (-(-size // 1024))


def _pack_small(d):
    parts = []
    for n in _SMALL:
        flat = d[n].reshape(-1).astype(F32)
        rows = _tile_rows(flat.shape[0])
        pad = rows * 128 - flat.shape[0]
        if pad:
            flat = jnp.concatenate([flat, jnp.zeros((pad,), F32)])
        parts.append(flat.reshape(rows, 128))
    return jnp.concatenate(parts, axis=0)


def _unpack_small(p, shapes):
    out, off = {}, 0
    for n in _SMALL:
        size = math.prod(shapes[n])
        rows = _tile_rows(size)
        out[n] = p[off:off + rows].reshape(-1)[:size].reshape(shapes[n])
        off += rows
    return out


def kernel(x, c, w_ada, b_ada, w_in, conv_w, conv_b, dt_bias, a_log, d_ssd, norm_w, s5_a_re, s5_a_im, s5_log_dt, s5_b_re, s5_b_im, s5_c_re, s5_c_im, s5_d, w_glu, b_glu, w_out, ln1_g, ln1_b, w1, b1, w2, b2, ln2_g, ln2_b, loss_target, m_w_ada, m_b_ada, m_w_in, m_conv_w, m_conv_b, m_dt_bias, m_a_log, m_d_ssd, m_norm_w, m_s5_a_re, m_s5_a_im, m_s5_log_dt, m_s5_b_re, m_s5_b_im, m_s5_c_re, m_s5_c_im, m_s5_d, m_w_glu, m_b_glu, m_w_out, m_ln1_g, m_ln1_b, m_w1, m_b1, m_w2, m_b2, m_ln2_g, m_ln2_b, v_w_ada, v_b_ada, v_w_in, v_conv_w, v_conv_b, v_dt_bias, v_a_log, v_d_ssd, v_norm_w, v_s5_a_re, v_s5_a_im, v_s5_log_dt, v_s5_b_re, v_s5_b_im, v_s5_c_re, v_s5_c_im, v_s5_d, v_w_glu, v_b_glu, v_w_out, v_ln1_g, v_ln1_b, v_w1, v_b1, v_w2, v_b2, v_ln2_g, v_ln2_b):
    weights = dict(w_ada=w_ada, b_ada=b_ada, w_in=w_in, conv_w=conv_w, conv_b=conv_b, dt_bias=dt_bias, a_log=a_log,
                   d_ssd=d_ssd, norm_w=norm_w, s5_a_re=s5_a_re, s5_a_im=s5_a_im, s5_log_dt=s5_log_dt, s5_b_re=s5_b_re,
                   s5_b_im=s5_b_im, s5_c_re=s5_c_re, s5_c_im=s5_c_im, s5_d=s5_d, w_glu=w_glu, b_glu=b_glu, w_out=w_out,
                   ln1_g=ln1_g, ln1_b=ln1_b, w1=w1, b1=b1, w2=w2, b2=b2, ln2_g=ln2_g, ln2_b=ln2_b)
    mom = dict(w_ada=m_w_ada, b_ada=m_b_ada, w_in=m_w_in, conv_w=m_conv_w, conv_b=m_conv_b, dt_bias=m_dt_bias,
               a_log=m_a_log, d_ssd=m_d_ssd, norm_w=m_norm_w, s5_a_re=m_s5_a_re, s5_a_im=m_s5_a_im,
               s5_log_dt=m_s5_log_dt, s5_b_re=m_s5_b_re, s5_b_im=m_s5_b_im, s5_c_re=m_s5_c_re, s5_c_im=m_s5_c_im,
               s5_d=m_s5_d, w_glu=m_w_glu, b_glu=m_b_glu, w_out=m_w_out, ln1_g=m_ln1_g, ln1_b=m_ln1_b, w1=m_w1, b1=m_b1,
               w2=m_w2, b2=m_b2, ln2_g=m_ln2_g, ln2_b=m_ln2_b)
    var = dict(w_ada=v_w_ada, b_ada=v_b_ada, w_in=v_w_in, conv_w=v_conv_w, conv_b=v_conv_b, dt_bias=v_dt_bias,
               a_log=v_a_log, d_ssd=v_d_ssd, norm_w=v_norm_w, s5_a_re=v_s5_a_re, s5_a_im=v_s5_a_im,
               s5_log_dt=v_s5_log_dt, s5_b_re=v_s5_b_re, s5_b_im=v_s5_b_im, s5_c_re=v_s5_c_re, s5_c_im=v_s5_c_im,
               s5_d=v_s5_d, w_glu=v_w_glu, b_glu=v_b_glu, w_out=v_w_out, ln1_g=v_ln1_g, ln1_b=v_ln1_b, w1=v_w1, b1=v_b1,
               w2=v_w2, b2=v_b2, ln2_g=v_ln2_g, ln2_b=v_ln2_b)
    names = list(weights)
    shapes = {n: weights[n].shape for n in names}

    nb, seq, _ = x.shape
    t = nb * seq
    dev = _dev_index()
    x2 = x.reshape(t, D_MODEL)
    tgt2 = loss_target.reshape(t, D_MODEL)

    cw_cols = conv_w.shape[2]
    small_in = jnp.concatenate([c.reshape(-1), conv_w.reshape(-1)]).reshape(-1, 128)
    big_names = ["w_in", "w_out", "w1", "w2", "w_glu"]
    local = {n: (a[0].T if n == "w_in" else a[0]) for n, a in weights.items() if n in big_names}
    shard_bf16 = {n: local[n].astype(BF16) for n in big_names}
    first = _all_gather([small_in, shard_bf16["w_in"], shard_bf16["w_glu"]], "gather_first")
    small_all = first[0].reshape(N_DEV, -1)
    c_all = small_all[:, :nb * D_MODEL].reshape(N_DEV * nb, D_MODEL)
    conv_w_full = small_all[:, nb * D_MODEL:].reshape(N_DEV, 4, cw_cols).transpose(1, 0, 2).reshape(4, D_XBC)

    w_in_t = first[1].reshape(D_IN, D_MODEL)
    w_in_pad = jnp.concatenate(
        [w_in_t[:D_SSD + D_XBC + N_HEADS], jnp.zeros((DT_PAD - N_HEADS, D_MODEL), BF16),
         w_in_t[D_SSD + D_XBC + N_HEADS:]], axis=0)
    w_glu_f = first[2].reshape(D_S5, D_S5)
    late_names = ["w_out", "w1", "w2"]

    ada_cols = w_ada.shape[2]
    b_cols = lax.dynamic_slice_in_dim(b_ada, dev * ada_cols, ada_cols, axis=1)
    mod_cols = _mod_fwd(c_all, w_ada[0], b_cols)
    mod_all = _all_gather([mod_cols], "gather_mod")[0]
    mod_mine = lax.dynamic_slice_in_dim(mod_all, dev * nb, nb, axis=1)
    mod3 = mod_mine.transpose(1, 0, 2).reshape(nb, N_MOD, D_MODEL)
    late_in, mod3 = lax.optimization_barrier(([shard_bf16[n] for n in late_names], mod3))
    late_sems = _gather_start(late_in, "gather_late_start")
    mod3 = mod3 + late_sems[4][0, 0]

    def pad_lanes(v, n):
        return jnp.concatenate([v, jnp.zeros((v.shape[0], n - v.shape[1]), F32)], axis=1)

    par = _pad_rows(jnp.concatenate([pad_lanes(dt_bias, 128), pad_lanes(a_log, 128)], axis=0), 8)
    dsk = jnp.repeat(d_ssd[0], HEADDIM).reshape(1, D_SSD)
    ar = s5_a_re.reshape(1, S5_N)
    ai = s5_a_im.reshape(1, S5_N)
    ldt = jnp.repeat(s5_log_dt[0], S5_P).reshape(1, S5_N)
    br_t = s5_b_re[0].transpose(2, 0, 1).reshape(S5_CH, S5_N)
    bi_t = s5_b_im[0].transpose(2, 0, 1).reshape(S5_CH, S5_N)
    bb_re_t, bb_im_t, pf_re, pf_im, pr_re, pr_im = _s5_params_fwd(ar, ai, ldt, br_t, bi_t)
    gpb = S5_GROUPS // S5_BLOCKS
    mask_b = (jnp.arange(128)[:, None] // S5_CH) == (jnp.arange(512)[None, :] // S5_P)

    def dense_b(bt_):
        blocks = bt_.reshape(S5_CH, S5_BLOCKS, 512).transpose(1, 0, 2)
        return jnp.where(mask_b, jnp.tile(blocks, (1, gpb, 1)), 0.0).astype(BF16)

    def dense_c(cc):
        blocks = cc[0].transpose(0, 2, 1).reshape(S5_BLOCKS, 512, S5_CH)
        return jnp.where(mask_b.T, jnp.tile(blocks, (1, 1, gpb)), 0.0).astype(BF16)

    bb_re, bb_im = dense_b(bb_re_t), dense_b(bb_im_t)
    cc_re, cc_im = dense_c(s5_c_re), dense_c(s5_c_im)
    s5d = s5_d.reshape(1, D_S5)
    ln1 = jnp.concatenate([ln1_g, ln1_b], axis=0)
    vec1 = _pad_rows(jnp.concatenate([b2, ln2_g, ln2_b], axis=0), 8)

    z, xbc_pre, xbc, dsilu, dt_raw, u5 = _proj_conv_fwd(x2, mod3, w_in_pad, conv_w_full, conv_b, seq)
    yraw, ycat, hprev = _ssd_fwd(xbc, z, dt_raw, par, dsk, norm_w, seq)
    s_re, s_im, ypre, ycat = _s5_fwd(u5, bb_re, bb_im, cc_re, cc_im, pf_re, pf_im, s5d, w_glu_f, b_glu, ycat, seq)
    sent, landed = _gather_wait(late_sems[0], late_sems[1], late_sems[2], late_sems[3], ycat, "gather_late_wait")
    gathered = {n: lax.dynamic_update_index_in_dim(l, x, dev, 0) for n, x, l in zip(late_names, sent, landed)}
    w_out_f = gathered["w_out"].reshape(2 * D_MODEL, D_MODEL)
    w1_blocks = gathered["w1"]
    w2_f = gathered["w2"].reshape(D_FF, D_MODEL)
    mix, x1 = _out_ln1(ycat, x2, mod3, w_out_f, ln1, seq)

    dx1, u2b, hb, dhpb, dob, gacc2, db1, bacc2 = _mlp_fwd_bwd(x1, tgt2, mod3, w1_blocks, w2_f, vec1, b1, seq)
    loss = lax.psum(0.5 / D_MODEL * jnp.sum(gacc2[3]), ("x", "y", "c"))

    dmixb, dxa, dyssd, dy5, gacc1, bacc1 = _ln1_out_bwd(dx1, x2, mix, mod3, w_out_f, ln1, seq)

    g_w2 = _atb(hb, dob, "gw2")
    g_w1 = _atb(u2b, dhpb, "gw1")
    g_wout = _atb(ycat, dmixb, "gwout")
    core = lax.axis_index("c").astype(jnp.int32).reshape(1)
    chip = 2 * lax.axis_index("x") + lax.axis_index("y")

    def chip_sums_of(names, grads, tag):
        by_dest = [g if g.ndim == 2 else g.reshape((4, 2) + g.shape[1:]) for g in grads]
        from_sibling = _sibling_swap(by_dest, "rs_swap_" + tag)
        return [_add_halves(g, r, core, "rs_add_" + n) for g, r, n in zip(by_dest, from_sibling, names)]

    early_names = ["w_out", "w1", "w2"]
    early_sums = chip_sums_of(early_names, [g_wout.reshape((N_DEV,) + w_out.shape[1:]), g_w1,
                                            g_w2.reshape((N_DEV,) + w2.shape[1:])], "early")
    early = _all_to_all_start(early_sums, "rs_early_start")
    s5d_after = s5d + early[4][0, 0]

    du5, vacc, sacc, d_cc, d_bb, g_wglu = _s5_bwd(dy5, ypre, u5, s_re, s_im, bb_re, bb_im, cc_re, cc_im,
                                                  pr_re, pr_im, s5d_after, w_glu_f, b_glu, seq)
    dxbc, dz, ddt, dpar, cacc = _ssd_bwd(dyssd, yraw, z, xbc, dt_raw, hprev, par, dsk, norm_w, seq)
    dpre, conv_acc = _conv_bwd(dxbc, dsilu, xbc_pre, seq)
    grad_x2, ub, dxpb, bacc0 = _proj_bwd(dz, dpre, ddt, du5, x2, dxa, mod3, conv_w_full, w_in_pad, seq)

    g_win_t = jnp.concatenate([_atb(dz, ub, "gwin_z"), _atb(dxpb, ub, "gwin_xbc"),
                               _atb(ddt, ub, "gwin_dt")[:N_HEADS], _atb(du5, ub, "gwin_s5")], axis=0)

    def diag_b(dd):
        kept = jnp.where(mask_b, dd, 0.0).reshape(S5_BLOCKS, gpb, S5_CH, 512).sum(1)
        return kept.transpose(1, 0, 2).reshape(S5_CH, S5_N)

    def diag_c(dd):
        kept = jnp.where(mask_b.T, dd, 0.0).reshape(S5_BLOCKS, 512, gpb, S5_CH).sum(2)
        return kept.reshape(S5_GROUPS, S5_P, S5_CH).transpose(0, 2, 1)

    g_ar, g_ai, g_ldt, g_br_t, g_bi_t = _s5_params_bwd(ar, ai, ldt, br_t, bi_t, vacc[0:1], vacc[1:2],
                                                      diag_b(d_bb[:S5_BLOCKS]), diag_b(d_bb[S5_BLOCKS:]))

    def from_t(gt):
        return gt.reshape(S5_CH, S5_GROUPS, S5_P).transpose(1, 2, 0)

    small_g = dict(
        conv_w=conv_acc[0:4], conv_b=conv_acc[4:5], dt_bias=dpar[0:1, :N_HEADS], a_log=dpar[1:2, :N_HEADS],
        d_ssd=cacc[0].reshape(N_HEADS, HEADDIM).sum(1), norm_w=cacc[1:2],
        s5_a_re=g_ar, s5_a_im=g_ai, s5_log_dt=g_ldt[:, :S5_GROUPS], s5_b_re=from_t(g_br_t), s5_b_im=from_t(g_bi_t),
        s5_c_re=diag_c(d_cc[:S5_BLOCKS]), s5_c_im=diag_c(d_cc[S5_BLOCKS:]), s5_d=sacc[0:1], b_glu=sacc[1:2],
        ln1_g=gacc1[0:1], ln1_b=gacc1[1:2], b1=db1, b2=gacc2[2:3], ln2_g=gacc2[0:1], ln2_b=gacc2[1:2])

    dmod = jnp.concatenate([bacc0[:, 0], bacc0[:, 1], bacc1[:, 0], bacc2[:, 0], bacc2[:, 1], bacc2[:, 2]], axis=1)
    dmod_all = _all_gather([dmod], "gather_dmod")[0].reshape(N_DEV * nb, N_MOD * D_MODEL)
    dmod_cols = lax.dynamic_slice_in_dim(dmod_all, dev * ada_cols, ada_cols, axis=1)
    g_wada, g_bada = _mod_bwd(c_all, dmod_cols, dmod_all)

    late_rs = ["w_in", "w_glu"]
    late_sums = chip_sums_of(late_rs, [g_win_t.reshape(N_DEV, w_in.shape[2], D_MODEL),
                                       g_wglu.reshape((N_DEV,) + w_glu.shape[1:])], "late")
    parts = dict(zip(late_rs, _chip_all_to_all(late_sums, "rs_late_all_to_all")))
    sent, landed = _all_to_all_wait(early[0], early[1], early[2], early[3], parts["w_in"], "rs_early_wait")
    for n, l, h in zip(early_names, landed, sent):
        parts[n] = lax.dynamic_update_index_in_dim(l, lax.dynamic_index_in_dim(h, chip, 0, keepdims=False), chip, 0)

    res = {k: {} for k in "gdmv"}
    for n in big_names:
        w_m_v = [(a[n][0].T if n == "w_in" else a[n][0]) for a in (weights, mom, var)]
        outs = _adamw(parts[n], *w_m_v, "adamw_" + n)
        for k, a in zip("gdmv", outs):
            res[k][n] = (a.T if n == "w_in" else a)[None]

    ag, ad, am, av = _adamw(g_wada[None], w_ada[0], m_w_ada[0], v_w_ada[0], "adamw_w_ada")
    for k, a in (("g", ag), ("d", ad), ("m", am), ("v", av)):
        res[k]["w_ada"] = a[None]
    bg_, bd_, bm_, bv_ = _adamw(g_bada.reshape(1, -1, 128), b_ada.reshape(-1, 128), m_b_ada.reshape(-1, 128),
                                v_b_ada.reshape(-1, 128), "adamw_b_ada")
    for k, a in (("g", bg_), ("d", bd_), ("m", bm_), ("v", bv_)):
        res[k]["b_ada"] = a.reshape(shapes["b_ada"])

    small_shapes = dict(shapes)
    small_shapes["conv_w"] = (1, 4, D_XBC)
    small_parts = _all_gather([_pack_small(small_g)], "gather_small_grads")[0]
    rep = {n: (jnp.zeros((1, 4, D_XBC), F32) if n == "conv_w" else weights[n]) for n in _SMALL}
    rep_m = {n: (jnp.zeros((1, 4, D_XBC), F32) if n == "conv_w" else mom[n]) for n in _SMALL}
    rep_v = {n: (jnp.ones((1, 4, D_XBC), F32) if n == "conv_w" else var[n]) for n in _SMALL}
    sg_, sd_, sm_, sv_ = _adamw(small_parts, _pack_small(rep), _pack_small(rep_m), _pack_small(rep_v), "adamw_small")
    for k, p in (("g", sg_), ("d", sd_), ("m", sm_), ("v", sv_)):
        un = _unpack_small(p, small_shapes)
        for n in _SMALL:
            if n != "conv_w":
                res[k][n] = un[n]
    g_conv_full = _unpack_small(sg_, small_shapes)["conv_w"][0]
    g_conv_mine = lax.dynamic_slice_in_dim(g_conv_full, dev * cw_cols, cw_cols, axis=1)
    cg_, cd_, cm_, cv_ = _adamw(g_conv_mine[None], conv_w[0], m_conv_w[0], v_conv_w[0], "adamw_conv_w")
    for k, a in (("g", cg_), ("d", cd_), ("m", cm_), ("v", cv_)):
        res[k]["conv_w"] = a[None]

    grad_x = grad_x2.reshape(nb, seq, D_MODEL)
    return (loss, grad_x, *[res["g"][n] for n in names], *[res["d"][n] for n in names],
            *[res["m"][n] for n in names], *[res["v"][n] for n in names])
```

```python
import functools
import math

import jax
import jax.numpy as jnp
from jax import lax
from jax.experimental import pallas as pl
from jax.experimental.pallas import tpu as pltpu

F32, BF16 = jnp.float32, jnp.bfloat16
MESH = pl.DeviceIdType.MESH
N_DEV = 8

D_MODEL = 1024
D_SSD = 1536
N_HEADS = 24
HEADDIM = 64
N_GROUPS = 4
HPG = 6
GW = HPG * HEADDIM
N_STATE = 128
CHUNK = 128
D_XBC = 2560
D_S5 = 512
S5_GROUPS = 32
S5_CH = 16
S5_P = 64
S5_N = S5_GROUPS * S5_P
D_IN = 4632
DT_PAD = 128
D_INP = D_SSD + D_XBC + DT_PAD + D_S5
D_FF = 4096
N_MOD = 6
ALPHA = 2.0 ** 0.25
EPS = 1e-5
LR, B1, B2, AEPS, WD, STEP = 0.001, 0.9, 0.999, 1e-08, 0.01, 10

NT = (((1,), (1,)), ((), ()))
TN = (((0,), (0,)), ((), ()))
ANY = pl.BlockSpec(memory_space=pl.ANY)
HIGHEST = lax.Precision.HIGHEST


def _mm(a, b):
    return jnp.dot(a.astype(BF16), b.astype(BF16), preferred_element_type=F32)


def _mm_nt(a, b):
    return lax.dot_general(a.astype(BF16), b.astype(BF16), NT, preferred_element_type=F32)


def _mm_tn(a, b):
    return lax.dot_general(a.astype(BF16), b.astype(BF16), TN, preferred_element_type=F32)


def _row_block(r, cap):
    best = r
    for cand in range(8, min(r, cap) + 1, 8):
        if r % cand == 0:
            best = cand
    return best if best <= cap else r


def _params(vmem_mb):
    return pltpu.CompilerParams(vmem_limit_bytes=vmem_mb << 20)


def _sigmoid(x):
    return 0.5 * (jnp.tanh(0.5 * x) + 1.0)


def _softplus(x):
    return jnp.maximum(x, 0.0) + jnp.log(1.0 + jnp.exp(-jnp.abs(x)))


_GK = math.sqrt(2.0 / math.pi)


def _gelu(x):
    return 0.5 * x * (1.0 + jnp.tanh(_GK * (x + 0.044715 * x * x * x)))


def _gelu_grad(x):
    t = jnp.tanh(_GK * (x + 0.044715 * x * x * x))
    return 0.5 * (1.0 + t) + 0.5 * x * (1.0 - t * t) * _GK * (1.0 + 3.0 * 0.044715 * x * x)


def _dev_index():
    return 4 * lax.axis_index("x") + 2 * lax.axis_index("y") + lax.axis_index("c")


def _all_gather(xs, name):
    n = len(xs)

    def body(*refs):
        x_refs, out_refs = refs[:n], refs[n:2 * n]
        send_sems, recv_sems, local_sems = refs[2 * n:]
        ix, iy, ic = lax.axis_index("x"), lax.axis_index("y"), lax.axis_index("c")
        me, sibling = (ix, iy, ic), (ix, iy, 1 - ic)
        chips = [(1 - ix, iy), (ix, 1 - iy), (1 - ix, 1 - iy)]

        def slot(a, px, py, pc):
            return out_refs[a].at[4 * px + 2 * py + pc]

        def copy(a, k, block, to, src=None):
            return pltpu.make_async_remote_copy(
                src_ref=slot(a, *block) if src is None else src, dst_ref=slot(a, *block),
                send_sem=send_sems.at[7 * a + k], recv_sem=recv_sems.at[7 * a + k], device_id=to, device_id_type=MESH)

        mine = [pltpu.make_async_copy(x_refs[a], slot(a, *me), local_sems.at[a]) for a in range(n)]
        for cp in mine:
            cp.start()
        first = []
        for j, chip in enumerate(chips):
            first += [copy(a, 1 + j, me, (*chip, ic), src=x_refs[a]) for a in range(n)]
        first += [copy(a, 0, me, sibling, src=x_refs[a]) for a in range(n)]
        for cp in first:
            cp.start()
        passed = []
        for j, chip in enumerate(chips):
            for a in range(n):
                copy(a, 1 + j, (*chip, ic), me).wait_recv()
                cp = copy(a, 4 + j, (*chip, ic), sibling)
                cp.start()
                passed.append(cp)
        for a in range(n):
            copy(a, 0, sibling, me).wait_recv()
            for j, chip in enumerate(chips):
                copy(a, 4 + j, (*chip, 1 - ic), me).wait_recv()
        for cp in first + passed:
            cp.wait_send()
        for cp in mine:
            cp.wait()

    return pl.pallas_call(
        body, name=name, out_shape=tuple(jax.ShapeDtypeStruct((N_DEV,) + x.shape, x.dtype) for x in xs),
        in_specs=[ANY] * n, out_specs=tuple([ANY] * n),
        scratch_shapes=[pltpu.SemaphoreType.DMA((7 * n,)), pltpu.SemaphoreType.DMA((7 * n,)),
                        pltpu.SemaphoreType.DMA((n,))],
    )(*xs)


HBM = pl.BlockSpec(memory_space=pltpu.HBM)
SEM = pl.BlockSpec(memory_space=pltpu.SEMAPHORE)
DATAFLOW = pltpu.SideEffectType.DATAFLOW_SIDE_EFFECTING


def _peer(k):
    ix, iy, ic = lax.axis_index("x"), lax.axis_index("y"), lax.axis_index("c")
    return (1 - ix if k & 4 else ix, 1 - iy if k & 2 else iy, 1 - ic if k & 1 else ic)


def _block_of(p):
    return 4 * p[0] + 2 * p[1] + p[2]


def _gather_start(xs, name):
    n = len(xs)
    lands = [lax.empty((N_DEV,) + x.shape, x.dtype) for x in xs]

    def body(*refs):
        x_refs, land_refs = refs[:n], refs[n:2 * n]
        send_sems, recv_sems = refs[2 * n], refs[2 * n + 1]
        token = refs[-1]
        me = _block_of(_peer(0))
        for a in range(n):
            for k in range(1, N_DEV):
                pltpu.make_async_remote_copy(
                    src_ref=x_refs[a], dst_ref=land_refs[a].at[me], send_sem=send_sems.at[7 * a + k - 1],
                    recv_sem=recv_sems.at[7 * a + k - 1], device_id=_peer(k), device_id_type=MESH).start()
        token[...] = jnp.zeros_like(token)

    outs = pl.pallas_call(
        body, name=name,
        out_shape=(pltpu.SemaphoreType.DMA((7 * n,)), pltpu.SemaphoreType.DMA((7 * n,)))
        + tuple(pltpu.HBM(x.shape, x.dtype) for x in xs) + tuple(pltpu.HBM(l.shape, l.dtype) for l in lands)
        + (jax.ShapeDtypeStruct((8, 128), F32),),
        in_specs=[HBM] * (2 * n), out_specs=(SEM, SEM) + (HBM,) * (2 * n) + (pl.BlockSpec(memory_space=pltpu.VMEM),),
        input_output_aliases={i: 2 + i for i in range(2 * n)},
        compiler_params=pltpu.CompilerParams(has_side_effects=DATAFLOW),
    )(*[pltpu.with_memory_space_constraint(x, pltpu.HBM) for x in xs],
      *[pltpu.with_memory_space_constraint(l, pltpu.HBM) for l in lands])
    return outs[0], outs[1], outs[2:2 + n], outs[2 + n:2 + 2 * n], outs[-1]


def _gather_wait(send_sems, recv_sems, xs_thru, lands_thru, after, name):
    n = len(xs_thru)

    def body(*refs):
        x_refs, land_refs = refs[:n], refs[n:2 * n]
        send_sems, recv_sems = refs[2 * n], refs[2 * n + 1]
        for a in range(n):
            for k in range(1, N_DEV):
                cp = pltpu.make_async_remote_copy(
                    src_ref=x_refs[a], dst_ref=land_refs[a].at[_block_of(_peer(k))], send_sem=send_sems.at[7 * a + k - 1],
                    recv_sem=recv_sems.at[7 * a + k - 1], device_id=_peer(k), device_id_type=MESH)
                cp.wait_send()
                cp.wait_recv()

    outs = pl.pallas_call(
        body, name=name,
        out_shape=tuple(pltpu.HBM(x.shape, x.dtype) for x in xs_thru)
        + tuple(pltpu.HBM(l.shape, l.dtype) for l in lands_thru),
        in_specs=[HBM] * (2 * n) + [SEM, SEM, ANY], out_specs=(HBM,) * (2 * n),
        input_output_aliases={i: i for i in range(2 * n)},
        compiler_params=pltpu.CompilerParams(has_side_effects=DATAFLOW),
    )(*xs_thru, *lands_thru, send_sems, recv_sems, after)
    return outs[:n], outs[n:]


def _chip_peer(k):
    ix, iy = lax.axis_index("x"), lax.axis_index("y")
    return (1 - ix if k & 2 else ix, 1 - iy if k & 1 else iy)


def _all_to_all_start(hs, name):
    n = len(hs)
    lands = [lax.empty(h.shape, h.dtype) for h in hs]

    def body(*refs):
        h_refs, land_refs = refs[:n], refs[n:2 * n]
        send_sems, recv_sems = refs[2 * n], refs[2 * n + 1]
        token = refs[-1]
        ic = lax.axis_index("c")
        mx, my = _chip_peer(0)
        for a in range(n):
            for k in range(1, 4):
                px, py = _chip_peer(k)
                pltpu.make_async_remote_copy(
                    src_ref=h_refs[a].at[2 * px + py], dst_ref=land_refs[a].at[2 * mx + my],
                    send_sem=send_sems.at[3 * a + k - 1], recv_sem=recv_sems.at[3 * a + k - 1],
                    device_id=(px, py, ic), device_id_type=MESH).start()
        token[...] = jnp.zeros_like(token)

    outs = pl.pallas_call(
        body, name=name,
        out_shape=(pltpu.SemaphoreType.DMA((3 * n,)), pltpu.SemaphoreType.DMA((3 * n,)))
        + tuple(pltpu.HBM(h.shape, h.dtype) for h in hs) + tuple(pltpu.HBM(l.shape, l.dtype) for l in lands)
        + (jax.ShapeDtypeStruct((8, 128), F32),),
        in_specs=[HBM] * (2 * n), out_specs=(SEM, SEM) + (HBM,) * (2 * n) + (pl.BlockSpec(memory_space=pltpu.VMEM),),
        input_output_aliases={i: 2 + i for i in range(2 * n)},
        compiler_params=pltpu.CompilerParams(has_side_effects=DATAFLOW),
    )(*[pltpu.with_memory_space_constraint(h, pltpu.HBM) for h in hs],
      *[pltpu.with_memory_space_constraint(l, pltpu.HBM) for l in lands])
    return outs[0], outs[1], outs[2:2 + n], outs[2 + n:2 + 2 * n], outs[-1]


def _all_to_all_wait(send_sems, recv_sems, hs_thru, lands_thru, after, name):
    n = len(hs_thru)

    def body(*refs):
        h_refs, land_refs = refs[:n], refs[n:2 * n]
        send_sems, recv_sems = refs[2 * n], refs[2 * n + 1]
        ic = lax.axis_index("c")
        for a in range(n):
            for k in range(1, 4):
                px, py = _chip_peer(k)
                cp = pltpu.make_async_remote_copy(
                    src_ref=h_refs[a].at[2 * px + py], dst_ref=land_refs[a].at[2 * px + py],
                    send_sem=send_sems.at[3 * a + k - 1], recv_sem=recv_sems.at[3 * a + k - 1],
                    device_id=(px, py, ic), device_id_type=MESH)
                cp.wait_send()
                cp.wait_recv()

    outs = pl.pallas_call(
        body, name=name,
        out_shape=tuple(pltpu.HBM(h.shape, h.dtype) for h in hs_thru)
        + tuple(pltpu.HBM(l.shape, l.dtype) for l in lands_thru),
        in_specs=[HBM] * (2 * n) + [SEM, SEM, ANY], out_specs=(HBM,) * (2 * n),
        input_output_aliases={i: i for i in range(2 * n)},
        compiler_params=pltpu.CompilerParams(has_side_effects=DATAFLOW),
    )(*hs_thru, *lands_thru, send_sems, recv_sems, after)
    return outs[:n], outs[n:]


def _sibling_swap(gs, name):
    n = len(gs)

    def body(*refs):
        g_refs, recv_refs = refs[:n], refs[n:2 * n]
        send_sems, recv_sems = refs[2 * n:]
        ix, iy, ic = lax.axis_index("x"), lax.axis_index("y"), lax.axis_index("c")

        def block(g_ref, q):
            if len(g_ref.shape) == 4:
                return g_ref.at[q, 1 - ic]
            cw = g_ref.shape[1] // N_DEV
            return g_ref.at[:, pl.ds(pl.multiple_of((2 * q + 1 - ic) * cw, 128), cw)]

        cps = []
        for a in range(n):
            for q in range(4):
                cps.append(pltpu.make_async_remote_copy(
                    src_ref=block(g_refs[a], q), dst_ref=recv_refs[a].at[q],
                    send_sem=send_sems.at[4 * a + q], recv_sem=recv_sems.at[4 * a + q],
                    device_id=(ix, iy, 1 - ic), device_id_type=MESH))
        for cp in cps:
            cp.start()
        for cp in cps:
            cp.wait()

    return pl.pallas_call(
        body, name=name,
        out_shape=tuple(jax.ShapeDtypeStruct(
            (4,) + (g.shape[2:] if g.ndim == 4 else (g.shape[0], g.shape[1] // N_DEV)), g.dtype) for g in gs),
        in_specs=[ANY] * n, out_specs=tuple([ANY] * n),
        scratch_shapes=[pltpu.SemaphoreType.DMA((4 * n,)), pltpu.SemaphoreType.DMA((4 * n,))],
    )(*gs)


def _chip_all_to_all(hs, name):
    n = len(hs)

    def body(*refs):
        h_refs, out_refs = refs[:n], refs[n:2 * n]
        send_sems, recv_sems, local_sems = refs[2 * n:]
        ix, iy, ic = lax.axis_index("x"), lax.axis_index("y"), lax.axis_index("c")
        me = 2 * ix + iy
        peers = [(1 - ix, iy), (ix, 1 - iy), (1 - ix, 1 - iy)]
        mine = [pltpu.make_async_copy(h_refs[a].at[me], out_refs[a].at[me], local_sems.at[a]) for a in range(n)]
        for cp in mine:
            cp.start()

        def copy(a, k, src_slot, dst_slot, peer):
            return pltpu.make_async_remote_copy(
                src_ref=h_refs[a].at[src_slot], dst_ref=out_refs[a].at[dst_slot],
                send_sem=send_sems.at[3 * a + k], recv_sem=recv_sems.at[3 * a + k],
                device_id=(*peer, ic), device_id_type=MESH)

        sends = [copy(a, k, 2 * px + py, me, (px, py)) for a in range(n) for k, (px, py) in enumerate(peers)]
        for cp in sends:
            cp.start()
        for a in range(n):
            for k, (px, py) in enumerate(peers):
                copy(a, k, 2 * px + py, 2 * px + py, (px, py)).wait_recv()
        for cp in sends:
            cp.wait_send()
        for cp in mine:
            cp.wait()

    return pl.pallas_call(
        body, name=name, out_shape=tuple(jax.ShapeDtypeStruct(h.shape, h.dtype) for h in hs),
        in_specs=[ANY] * n, out_specs=tuple([ANY] * n),
        scratch_shapes=[pltpu.SemaphoreType.DMA((3 * n,)), pltpu.SemaphoreType.DMA((3 * n,)),
                        pltpu.SemaphoreType.DMA((n,))],
    )(*hs)


def _add_halves(g, recv, core, name):
    _, r, c = recv.shape
    br = _row_block(r, 512)
    stacked = g.ndim == 4

    def body(core_ref, g_ref, r_ref, o_ref):
        o_ref[0] = ((g_ref[0, 0] if stacked else g_ref[...]) + r_ref[0]).astype(BF16)

    spec = pl.BlockSpec((1, br, c), lambda i, j, core_ref: (i, j, 0))
    if stacked:
        g_spec = pl.BlockSpec((1, 1, br, c), lambda i, j, core_ref: (i, core_ref[0], j, 0))
    else:
        g_spec = pl.BlockSpec((br, c), lambda i, j, core_ref: (j, 2 * i + core_ref[0]))
    return pl.pallas_call(
        body, name=name, out_shape=jax.ShapeDtypeStruct(recv.shape, BF16),
        grid_spec=pltpu.PrefetchScalarGridSpec(
            num_scalar_prefetch=1, grid=(4, r // br), in_specs=[g_spec, spec], out_specs=spec),
        compiler_params=_params(32),
    )(core, g, recv)


def _adamw_math(g, w, m, v):
    m2 = B1 * m + (1.0 - B1) * g
    v2 = B2 * v + (1.0 - B2) * (g * g)
    m_hat = m2 / (1.0 - B1 ** STEP)
    v_hat = v2 / (1.0 - B2 ** STEP)
    return -LR * (m_hat / (jnp.sqrt(v_hat) + AEPS) + WD * w), m2, v2


def _sum_parts(parts, name):
    n_parts, r, c = parts.shape
    br = _row_block(r, 512)

    def body(p_ref, o_ref):
        g = p_ref[0].astype(F32)
        for p in range(1, n_parts):
            g = g + p_ref[p].astype(F32)
        o_ref[...] = g

    return pl.pallas_call(
        body, name=name, out_shape=jax.ShapeDtypeStruct((r, c), F32), grid=(r // br,),
        in_specs=[pl.BlockSpec((n_parts, br, c), lambda i: (0, i, 0))], out_specs=pl.BlockSpec((br, c), lambda i: (i, 0)),
        compiler_params=_params(32),
    )(parts)


def _adamw_many(gs, ws, ms, vs, name):
    n = len(gs)

    def body(*refs):
        ins, outs = refs[:4 * n], refs[4 * n:]
        for i in range(n):
            d, m2, v2 = _adamw_math(ins[i][...], ins[n + i][...], ins[2 * n + i][...], ins[3 * n + i][...])
            outs[i][...] = d
            outs[n + i][...] = m2
            outs[2 * n + i][...] = v2

    shapes = tuple(jax.ShapeDtypeStruct(w.shape, F32) for w in ws)
    outs = pl.pallas_call(body, name=name, out_shape=shapes * 3, compiler_params=_params(40))(*gs, *ws, *ms, *vs)
    return outs[:n], outs[n:2 * n], outs[2 * n:]


def _adamw(parts, w, m, v, name):
    n_parts, r, c = parts.shape
    if r % 8 == 0:
        br, bc = _row_block(r, 512 if c <= 1024 else 256), c
    else:
        br, bc = r, (256 if c % 256 == 0 else c)

    def body(p_ref, w_ref, m_ref, v_ref, g_out, d_out, m_out, v_out):
        g = p_ref[0].astype(F32)
        for p in range(1, n_parts):
            g = g + p_ref[p].astype(F32)
        g_out[...] = g
        d_out[...], m_out[...], v_out[...] = _adamw_math(g, w_ref[...], m_ref[...], v_ref[...])

    spec = pl.BlockSpec((br, bc), lambda i, j: (i, j))
    out = jax.ShapeDtypeStruct((r, c), F32)
    return pl.pallas_call(
        body, name=name, out_shape=(out, out, out, out), grid=(r // br, c // bc),
        in_specs=[pl.BlockSpec((n_parts, br, bc), lambda i, j: (0, i, j)), spec, spec, spec],
        out_specs=(spec, spec, spec, spec), compiler_params=_params(40),
    )(parts, w, m, v)


def _atb(a, b, name):
    t, k1 = a.shape
    k2 = b.shape[1]
    bt = math.gcd(t, 2048)

    def pick(k):
        for cand in (1024, 768, 512, 384, 256, 128):
            if k % cand == 0:
                return cand
        return k

    b1, b2 = pick(k1), pick(k2)

    def body(a_ref, b_ref, o_ref):
        @pl.when(pl.program_id(2) == 0)
        def _():
            o_ref[...] = jnp.zeros_like(o_ref)
        o_ref[...] += _mm_tn(a_ref[...], b_ref[...])

    return pl.pallas_call(
        body, name=name, out_shape=jax.ShapeDtypeStruct((k1, k2), F32), grid=(k1 // b1, k2 // b2, t // bt),
        in_specs=[pl.BlockSpec((bt, b1), lambda i, j, k: (k, i)), pl.BlockSpec((bt, b2), lambda i, j, k: (k, j))],
        out_specs=pl.BlockSpec((b1, b2), lambda i, j, k: (i, j)), compiler_params=_params(48),
    )(a, b)


def _mod_fwd(c_all, w_ada, b_cols):
    def body(c_ref, w_ref, b_ref, o_ref):
        cc = c_ref[...]
        cond = cc * _sigmoid(cc)
        o_ref[...] = _mm(cond, w_ref[...]) + b_ref[...]

    return pl.pallas_call(body, name="mod_fwd", out_shape=jax.ShapeDtypeStruct((c_all.shape[0], w_ada.shape[1]), F32),
                          compiler_params=_params(32))(c_all, w_ada, b_cols)


def _mod_bwd(c_all, dmod_cols, dmod_all):
    def body(c_ref, dc_ref, da_ref, gw_ref, gb_ref):
        cc = c_ref[...]
        cond = cc * _sigmoid(cc)
        gw_ref[...] = _mm_tn(cond, dc_ref[...])
        gb_ref[...] = jnp.sum(da_ref[...], axis=0, keepdims=True)

    return pl.pallas_call(
        body, name="mod_bwd",
        out_shape=(jax.ShapeDtypeStruct((D_MODEL, dmod_cols.shape[1]), F32), jax.ShapeDtypeStruct((1, dmod_all.shape[1]), F32)),
        compiler_params=_params(32))(c_all, dmod_cols, dmod_all)


def _load_once(hbm_ref, vmem_ref, sem):
    @pl.when(pl.program_id(0) == 0)
    def _():
        cp = pltpu.make_async_copy(hbm_ref, vmem_ref, sem)
        cp.start()
        cp.wait()


def _conv_taps(win_ref, w, tb, cols):
    shifted = [win_ref[8 - j:8 - j + tb, cols] for j in range(4)]
    acc = w[3:4] * shifted[0]
    for j in (1, 2, 3):
        acc = acc + w[3 - j:4 - j] * shifted[j]
    return acc, shifted


def _proj_conv_fwd(x2, mod3, w_in_pad, conv_w, conv_b, seq):
    t = x2.shape[0]
    tb = 256
    npb = seq // tb
    cw = 512

    def body(x_ref, mod_ref, w_hbm, cw_ref, cb_ref, z_ref, pre_ref, xbc_ref, dsilu_ref, dt_ref, u5_ref, w_vmem, win, sem):
        _load_once(w_hbm, w_vmem, sem)
        first = (pl.program_id(0) % npb) == 0

        @pl.when(first)
        def _():
            win[0:8, :] = jnp.zeros((8, D_XBC), F32)

        @pl.when(jnp.logical_not(first))
        def _():
            win[0:8, :] = win[tb:tb + 8, :]

        m = mod_ref[0]
        u = (x_ref[...] * (1.0 + m[1:2]) + m[0:1]).astype(BF16)
        z_ref[...] = lax.dot_general(u, w_vmem[0:D_SSD, :], NT, preferred_element_type=F32)
        dt_ref[...] = lax.dot_general(u, w_vmem[D_SSD + D_XBC:D_SSD + D_XBC + DT_PAD, :], NT,
                                      preferred_element_type=F32)
        u5_ref[...] = lax.dot_general(u, w_vmem[D_SSD + D_XBC + DT_PAD:, :], NT, preferred_element_type=F32)
        for k in range(D_XBC // cw):
            cols = slice(k * cw, (k + 1) * cw)
            pre_k = lax.dot_general(u, w_vmem[D_SSD + k * cw:D_SSD + (k + 1) * cw, :], NT,
                                    preferred_element_type=F32)
            win[8:8 + tb, cols] = pre_k
            pre_ref[:, cols] = pre_k
            conv, _ = _conv_taps(win, cw_ref[:, cols], tb, cols)
            conv = conv + cb_ref[:, cols]
            sg = _sigmoid(conv)
            xbc_ref[:, cols] = conv * sg
            dsilu_ref[:, cols] = sg * (1.0 + conv * (1.0 - sg))

    row = lambda w: pl.BlockSpec((tb, w), lambda i: (i, 0))
    return pl.pallas_call(
        body, name="proj_conv_fwd", grid=(t // tb,),
        out_shape=(jax.ShapeDtypeStruct((t, D_SSD), F32), jax.ShapeDtypeStruct((t, D_XBC), F32),
                   jax.ShapeDtypeStruct((t, D_XBC), F32), jax.ShapeDtypeStruct((t, D_XBC), F32),
                   jax.ShapeDtypeStruct((t, DT_PAD), F32), jax.ShapeDtypeStruct((t, D_S5), F32)),
        in_specs=[row(D_MODEL), pl.BlockSpec((1, N_MOD, D_MODEL), lambda i: (i // npb, 0, 0)), ANY,
                  pl.BlockSpec((4, D_XBC), lambda i: (0, 0)), pl.BlockSpec((1, D_XBC), lambda i: (0, 0))],
        out_specs=(row(D_SSD), row(D_XBC), row(D_XBC), row(D_XBC), row(DT_PAD), row(D_S5)),
        scratch_shapes=[pltpu.VMEM((D_INP, D_MODEL), BF16), pltpu.VMEM((tb + 8, D_XBC), F32), pltpu.SemaphoreType.DMA],
        compiler_params=_params(56),
    )(x2, mod3, w_in_pad, conv_w, conv_b)


N_PAIRS = N_HEADS // 2


def _split3(x):
    hi = x.astype(BF16)
    r = x - hi.astype(F32)
    mid = r.astype(BF16)
    lo = (r - mid.astype(F32)).astype(BF16)
    return hi, mid, lo


def _dot3(x, e, dims=(((1,), (0,)), ((), ()))):
    return sum(lax.dot_general(p, e, dims, preferred_element_type=F32) for p in _split3(x))


def _dot3_left(e, x, dims=(((1,), (0,)), ((), ()))):
    return sum(lax.dot_general(e, p, dims, preferred_element_type=F32) for p in _split3(x))


def _head_fold():
    return (jnp.arange(D_SSD)[:, None] // HEADDIM == jnp.arange(128)[None, :]).astype(BF16)


def _ssd_prep(dt_raw, par):
    dtb = par[0:1]
    a = -jnp.exp(par[1:2])
    dt = _softplus(dt_raw + dtb)
    adt = dt * a
    row = lax.broadcasted_iota(jnp.int32, (CHUNK, CHUNK), 0)
    col = lax.broadcasted_iota(jnp.int32, (CHUNK, CHUNK), 1)
    causal = row >= col
    tri = causal.astype(BF16)
    cs = _dot3_left(tri, adt)
    left = col < HEADDIM

    def lanes(v, h):
        return jnp.broadcast_to(v[:, h:h + 1], (CHUNK, 128))

    dt_c, cs_c, pair_cols = [], [], []
    for p in range(N_PAIRS):
        c0, c1 = lanes(cs, 2 * p), lanes(cs, 2 * p + 1)
        pair_cols.append(jnp.concatenate([c0, c1], axis=1))
        cs_c.append(jnp.where(left, c0, c1))
        dt_c.append(jnp.where(left, lanes(dt, 2 * p), lanes(dt, 2 * p + 1)))
    cs_c = jnp.concatenate(cs_c, axis=1)
    dt_c = jnp.concatenate(dt_c, axis=1)
    return dt, a, cs, cs.T, causal, tri, dt_c, jnp.exp(cs_c), jnp.exp(cs_c[CHUNK - 1:CHUNK, :] - cs_c), pair_cols


def _pair_decay(cols, cst, pair, causal2):
    rows = jnp.concatenate([jnp.broadcast_to(cst[2 * pair:2 * pair + 1, :], (CHUNK, CHUNK)),
                            jnp.broadcast_to(cst[2 * pair + 1:2 * pair + 2, :], (CHUNK, CHUNK))], axis=1)
    return jnp.exp(jnp.where(causal2, cols - rows, -jnp.inf))


def _stack_heads(xp, left):
    return jnp.concatenate([jnp.where(left, xp, 0.0), jnp.where(left, 0.0, xp)], axis=0).astype(BF16)


def _ssd_fwd(xbc, z, dt_raw, par, dsk, normw, seq):
    t = xbc.shape[0]
    nc = seq // CHUNK
    n_chunks = t // CHUNK

    def body(xbc_ref, z_ref, dt_ref, par_ref, dsk_ref, nw_ref, yraw_ref, ycat_ref, hprev_ref, h_ref):
        @pl.when(pl.program_id(0) % nc == 0)
        def _():
            h_ref[...] = jnp.zeros_like(h_ref)
        hprev_ref[0] = h_ref[...]
        _, _, cs, cst, causal, _, dt_c, ecs_c, w_c, pair_cols = _ssd_prep(dt_ref[...], par_ref[...])
        cs_last = cs[CHUNK - 1:CHUNK, :]
        causal2 = jnp.concatenate([causal, causal], axis=1)
        left = lax.broadcasted_iota(jnp.int32, (CHUNK, 128), 1) < HEADDIM
        x = xbc_ref[:, 0:D_SSD]
        xdt = x * dt_c
        amat = (w_c * xdt).astype(BF16)
        zz = z_ref[...]
        silu_z = zz * _sigmoid(zz)
        for g in range(N_GROUPS):
            gs = slice(g * GW, (g + 1) * GW)
            bg = xbc_ref[:, D_SSD + g * N_STATE:D_SSD + (g + 1) * N_STATE].astype(BF16)
            cg = xbc_ref[:, D_SSD + (N_GROUPS + g) * N_STATE:D_SSD + (N_GROUPS + g + 1) * N_STATE].astype(BF16)
            scores = lax.dot_general(cg, bg, NT, preferred_element_type=F32)
            scores2 = jnp.concatenate([scores, scores], axis=1)
            hg = h_ref[gs, :]
            p_all = lax.dot_general(cg, hg.astype(BF16), NT, preferred_element_type=F32)
            ys = []
            for q in range(GW // 128):
                pair = g * (GW // 128) + q
                decay = _pair_decay(pair_cols[pair], cst, pair, causal2)
                mcat = (scores2 * decay).astype(BF16)
                ys.append(jnp.dot(mcat, _stack_heads(xdt[:, pair * 128:(pair + 1) * 128], left),
                                  preferred_element_type=F32))
            yg = jnp.concatenate(ys, axis=1) + ecs_c[:, gs] * p_all + x[:, gs] * dsk_ref[:, gs]
            s_new = lax.dot_general(amat[:, gs], bg, TN, preferred_element_type=F32)
            for j in range(HPG):
                hh = g * HPG + j
                js = slice(j * HEADDIM, (j + 1) * HEADDIM)
                h_ref[g * GW + j * HEADDIM:g * GW + (j + 1) * HEADDIM, :] = (
                    hg[js, :] * jnp.exp(cs_last[:, hh:hh + 1]) + s_new[js, :])
            yraw_ref[:, gs] = yg
            v = yg * silu_z[:, gs]
            r = lax.rsqrt(jnp.mean(v * v, axis=-1, keepdims=True) + EPS)
            ycat_ref[:, gs] = (v * r * nw_ref[:, gs]).astype(BF16)

    row = lambda w: pl.BlockSpec((CHUNK, w), lambda i: (i, 0))
    full = lambda s: pl.BlockSpec(s, lambda i: (0,) * len(s))
    return pl.pallas_call(
        body, name="ssd_fwd", grid=(n_chunks,),
        out_shape=(jax.ShapeDtypeStruct((t, D_SSD), F32), jax.ShapeDtypeStruct((t, D_SSD + D_S5), BF16),
                   jax.ShapeDtypeStruct((n_chunks, D_SSD, N_STATE), F32)),
        in_specs=[row(D_XBC), row(D_SSD), row(DT_PAD), full((8, 128)), full((1, D_SSD)), full((1, D_SSD))],
        out_specs=(row(D_SSD), row(D_SSD), pl.BlockSpec((1, D_SSD, N_STATE), lambda i: (i, 0, 0))),
        scratch_shapes=[pltpu.VMEM((D_SSD, N_STATE), F32)],
        compiler_params=_params(40),
    )(xbc, z, dt_raw, par, dsk, normw)


S5_CW = 512
S5_BLOCKS = 4


def _tile_scan(in_re, in_im, out_re, out_im, carry_re, carry_im, pw_re, pw_im, n_tiles, reverse):
    steps = (1, 2, 4)
    for cc in range(S5_N // S5_CW):
        cols = slice(cc * S5_CW, (cc + 1) * S5_CW)
        a_re, a_im = pw_re[:, cols], pw_im[:, cols]
        rid = lax.broadcasted_iota(jnp.int32, (8, S5_CW), 0)
        pows = []
        for d in steps:
            k = 8 - d if reverse else d - 1
            keep = (rid < 8 - d) if reverse else (rid >= d)
            pows.append((jnp.where(keep, pw_re[k:k + 1, cols], 0.0), jnp.where(keep, pw_im[k:k + 1, cols], 0.0)))

        def tile(i, carry, cols=cols, pows=pows, a_re=a_re, a_im=a_im):
            r = (n_tiles - 1 - i) if reverse else i
            rows = pl.ds(pl.multiple_of(r * 8, 8), 8)
            xr, xi = in_re[rows, cols], in_im[rows, cols]
            for (pr, pi), d in zip(pows, steps):
                shift = 8 - d if reverse else d
                sr, si = pltpu.roll(xr, shift, axis=0), pltpu.roll(xi, shift, axis=0)
                xr, xi = xr + pr * sr - pi * si, xi + pr * si + pi * sr
            cr, ci = carry
            xr, xi = xr + a_re * cr - a_im * ci, xi + a_re * ci + a_im * cr
            out_re[rows, cols] = xr
            out_im[rows, cols] = xi
            edge = slice(0, 1) if reverse else slice(7, 8)
            return (jnp.broadcast_to(xr[edge], (8, S5_CW)), jnp.broadcast_to(xi[edge], (8, S5_CW)))

        c0 = (jnp.broadcast_to(carry_re[0:1, cols], (8, S5_CW)), jnp.broadcast_to(carry_im[0:1, cols], (8, S5_CW)))
        cr, ci = lax.fori_loop(0, n_tiles, tile, c0, unroll=True)
        carry_re[:, cols] = cr
        carry_im[:, cols] = ci


def _s5_params_math(ar, ai, ldt, br, bi):
    dt = jnp.exp(ldt)
    mag = jnp.exp(ar * dt)
    ang = ai * dt
    ab_re = mag * jnp.cos(ang)
    ab_im = mag * jnp.sin(ang)
    den = ar * ar + ai * ai
    n_re = ab_re - 1.0
    coef_re = (n_re * ar + ab_im * ai) / den
    coef_im = (ab_im * ar - n_re * ai) / den
    bb_re = coef_re * br - coef_im * bi
    bb_im = coef_re * bi + coef_im * br
    return ab_re, ab_im, bb_re, bb_im


def _s5_params_fwd(ar, ai, ldt, br, bi):
    def body(ar_ref, ai_ref, ldt_ref, br_ref, bi_ref, bbr_ref, bbi_ref, pfr_ref, pfi_ref, prr_ref, pri_ref):
        ab_re, ab_im, bb_re, bb_im = _s5_params_math(ar_ref[...], ai_ref[...], ldt_ref[...], br_ref[...], bi_ref[...])
        bbr_ref[...] = bb_re
        bbi_ref[...] = bb_im
        pr, pi = ab_re, ab_im
        for k in range(8):
            pfr_ref[k:k + 1, :] = pr
            pfi_ref[k:k + 1, :] = pi
            prr_ref[7 - k:8 - k, :] = pr
            pri_ref[7 - k:8 - k, :] = -pi
            pr, pi = pr * ab_re - pi * ab_im, pr * ab_im + pi * ab_re

    b16 = jax.ShapeDtypeStruct((S5_CH, S5_N), F32)
    p8 = jax.ShapeDtypeStruct((8, S5_N), F32)
    return pl.pallas_call(body, name="s5_params_fwd", out_shape=(b16, b16, p8, p8, p8, p8),
                          compiler_params=_params(32))(ar, ai, ldt, br, bi)


def _s5_params_bwd(ar, ai, ldt, br, bi, d_ab_re, d_ab_im, d_bb_re, d_bb_im):
    def body(ar_ref, ai_ref, ldt_ref, br_ref, bi_ref, dar_ref, dai_ref, dbr_ref, dbi_ref,
             gar_ref, gai_ref, gldt_ref, gbr_ref, gbi_ref):
        _, vjp = jax.vjp(_s5_params_math, ar_ref[...], ai_ref[...], ldt_ref[...], br_ref[...], bi_ref[...])
        g_ar, g_ai, g_ldt, g_br, g_bi = vjp((dar_ref[...], dai_ref[...], dbr_ref[...], dbi_ref[...]))
        gar_ref[...] = g_ar
        gai_ref[...] = g_ai
        gbr_ref[...] = g_br
        gbi_ref[...] = g_bi
        lane = lax.broadcasted_iota(jnp.int32, (S5_N, 128), 0) // S5_P
        grp = lax.broadcasted_iota(jnp.int32, (S5_N, 128), 1)
        fold = (lane == grp).astype(F32)
        gldt_ref[...] = jnp.dot(g_ldt, fold, preferred_element_type=F32, precision=HIGHEST)

    v1 = jax.ShapeDtypeStruct((1, S5_N), F32)
    b16 = jax.ShapeDtypeStruct((S5_CH, S5_N), F32)
    return pl.pallas_call(body, name="s5_params_bwd",
                          out_shape=(v1, v1, jax.ShapeDtypeStruct((1, 128), F32), b16, b16),
                          compiler_params=_params(32))(ar, ai, ldt, br, bi, d_ab_re, d_ab_im, d_bb_re, d_bb_im)


def _s5_fwd(u5, bb_re, bb_im, cc_re, cc_im, pf_re, pf_im, s5d, w_glu, b_glu, ycat, seq):
    t = u5.shape[0]
    tb = 256
    npb = seq // tb

    def body(u_ref, bbr_ref, bbi_ref, ccr_ref, cci_ref, pfr_ref, pfi_ref, d_ref, wg_ref, bg_ref, ycat_hbm,
             sre_ref, sim_ref, ypre_ref, y5_ref, bur, bui, car, cai):
        del ycat_hbm

        @pl.when(pl.program_id(0) % npb == 0)
        def _():
            car[...] = jnp.zeros_like(car)
            cai[...] = jnp.zeros_like(cai)
        u = u_ref[...]
        ub = u.astype(BF16)
        for j in range(S5_BLOCKS):
            ch, st = slice(j * 128, (j + 1) * 128), slice(j * 512, (j + 1) * 512)
            bur[:, st] = jnp.dot(ub[:, ch], bbr_ref[j], preferred_element_type=F32)
            bui[:, st] = jnp.dot(ub[:, ch], bbi_ref[j], preferred_element_type=F32)
        _tile_scan(bur, bui, sre_ref, sim_ref, car, cai, pfr_ref, pfi_ref, tb // 8, reverse=False)
        cs_y = []
        for j in range(S5_BLOCKS):
            st = slice(j * 512, (j + 1) * 512)
            cs_y.append(_mm(sre_ref[:, st], ccr_ref[j]) - _mm(sim_ref[:, st], cci_ref[j]))
        ypre = jnp.concatenate(cs_y, axis=1) + u * d_ref[...]
        ypre_ref[...] = ypre
        yg = _gelu(ypre)
        y5_ref[...] = (yg * _sigmoid(_mm(yg, wg_ref[...]) + bg_ref[...])).astype(BF16)

    row = lambda w: pl.BlockSpec((tb, w), lambda i: (i, 0))
    full = lambda a: pl.BlockSpec(a.shape, lambda i: (0,) * a.ndim)
    return pl.pallas_call(
        body, name="s5_fwd", grid=(t // tb,),
        out_shape=(jax.ShapeDtypeStruct((t, S5_N), F32), jax.ShapeDtypeStruct((t, S5_N), F32),
                   jax.ShapeDtypeStruct((t, D_S5), F32), jax.ShapeDtypeStruct(ycat.shape, BF16)),
        in_specs=[row(D_S5), full(bb_re), full(bb_im), full(cc_re), full(cc_im), full(pf_re), full(pf_im),
                  full(s5d), full(w_glu), full(b_glu), ANY],
        out_specs=(row(S5_N), row(S5_N), row(D_S5), pl.BlockSpec((tb, D_S5), lambda i: (i, D_SSD // D_S5))),
        input_output_aliases={10: 3},
        scratch_shapes=[pltpu.VMEM((tb, S5_N), F32), pltpu.VMEM((tb, S5_N), F32),
                        pltpu.VMEM((8, S5_N), F32), pltpu.VMEM((8, S5_N), F32)],
        compiler_params=_params(48),
    )(u5, bb_re, bb_im, cc_re, cc_im, pf_re, pf_im, s5d, w_glu, b_glu, ycat)


def _layer_norm(r, g, b):
    mu = jnp.mean(r, axis=-1, keepdims=True)
    xc = r - mu
    rstd = lax.rsqrt(jnp.mean(xc * xc, axis=-1, keepdims=True) + EPS)
    xhat = xc * rstd
    return xhat * g + b, xhat, rstd


def _layer_norm_bwd(dy, xhat, rstd, g):
    dxhat = dy * g
    return rstd * (dxhat - jnp.mean(dxhat, axis=-1, keepdims=True)
                   - xhat * jnp.mean(dxhat * xhat, axis=-1, keepdims=True))


def _out_ln1(ycat, x2, mod3, w_out, ln1, seq):
    t = x2.shape[0]
    tb = 512
    npb = seq // tb

    def body(y_ref, x_ref, mod_ref, w_ref, ln_ref, mix_ref, x1_ref):
        m = mod_ref[0]
        mix = jnp.dot(y_ref[...], w_ref[...], preferred_element_type=F32)
        mix_ref[...] = mix
        r1 = ALPHA * x_ref[...] + (1.0 + m[2:3]) * mix
        x1_ref[...] = _layer_norm(r1, ln_ref[0:1], ln_ref[1:2])[0]

    row = lambda w: pl.BlockSpec((tb, w), lambda i: (i, 0))
    return pl.pallas_call(
        body, name="out_ln1", grid=(t // tb,),
        out_shape=(jax.ShapeDtypeStruct((t, D_MODEL), F32), jax.ShapeDtypeStruct((t, D_MODEL), F32)),
        in_specs=[row(D_SSD + D_S5), row(D_MODEL), pl.BlockSpec((1, N_MOD, D_MODEL), lambda i: (i // npb, 0, 0)),
                  pl.BlockSpec(w_out.shape, lambda i: (0, 0)), pl.BlockSpec(ln1.shape, lambda i: (0, 0))],
        out_specs=(row(D_MODEL), row(D_MODEL)), compiler_params=_params(48),
    )(ycat, x2, mod3, w_out, ln1)


def _mlp_fwd_bwd(x1, tgt, mod3, w1, w2, vec1, b1, seq):
    t = x1.shape[0]
    tb = 256
    npb = seq // tb
    n_fb, _, fb = w1.shape

    def body(x1_ref, tgt_ref, mod_ref, w1_hbm, w2_hbm, v_ref, b1_ref,
             dx1_ref, u2_ref, h_ref, dhp_ref, do_ref, gacc_ref, db1_ref, bacc_ref, w1_v, w2_v, sem1, sem2):
        i = pl.program_id(0)
        @pl.when(i == 0)
        def _():
            cps = [pltpu.make_async_copy(w1_hbm.at[k], w1_v.at[:, k * fb:(k + 1) * fb], sem1.at[k])
                   for k in range(n_fb)]
            for cp in cps:
                cp.start()
            for cp in cps:
                cp.wait()
        _load_once(w2_hbm, w2_v, sem2)

        @pl.when(i == 0)
        def _():
            gacc_ref[...] = jnp.zeros_like(gacc_ref)
            db1_ref[...] = jnp.zeros_like(db1_ref)

        @pl.when(i % npb == 0)
        def _():
            bacc_ref[...] = jnp.zeros_like(bacc_ref)

        m = mod_ref[0]
        sh2, sc2, g2 = m[3:4], m[4:5], m[5:6]
        x1v = x1_ref[...]
        u2 = (x1v * (1.0 + sc2) + sh2).astype(BF16)
        u2_ref[...] = u2
        hr = jnp.maximum(jnp.dot(u2, w1_v[...], preferred_element_type=F32) + b1_ref[...], 0.0)
        hb = (hr * hr).astype(BF16)
        h_ref[...] = hb
        o = jnp.dot(hb, w2_v[...], preferred_element_type=F32) + v_ref[0:1]
        r2 = ALPHA * x1v + (1.0 + g2) * o
        y, xhat, rstd = _layer_norm(r2, v_ref[1:2], v_ref[2:3])
        err = y - tgt_ref[...]
        dy = err * (1.0 / D_MODEL)
        dr2 = _layer_norm_bwd(dy, xhat, rstd, v_ref[1:2])
        do = (1.0 + g2) * dr2
        dob = do.astype(BF16)
        do_ref[...] = dob
        gacc_ref[0:1, :] += jnp.sum(dy * xhat, axis=0, keepdims=True)
        gacc_ref[1:2, :] += jnp.sum(dy, axis=0, keepdims=True)
        gacc_ref[2:3, :] += jnp.sum(do, axis=0, keepdims=True)
        gacc_ref[3:4, :] += jnp.sum(err * err, axis=0, keepdims=True)
        dhpre = lax.dot_general(dob, w2_v[...], NT, preferred_element_type=F32) * (2.0 * hr)
        dhpb = dhpre.astype(BF16)
        dhp_ref[...] = dhpb
        db1_ref[...] += jnp.sum(dhpre, axis=0, keepdims=True)
        du2 = lax.dot_general(dhpb, w1_v[...], NT, preferred_element_type=F32)
        dx1_ref[...] = ALPHA * dr2 + du2 * (1.0 + sc2)
        bacc_ref[0, 0:1, :] += jnp.sum(du2, axis=0, keepdims=True)
        bacc_ref[0, 1:2, :] += jnp.sum(du2 * x1v, axis=0, keepdims=True)
        bacc_ref[0, 2:3, :] += jnp.sum(dr2 * o, axis=0, keepdims=True)

    row = lambda w: pl.BlockSpec((tb, w), lambda i: (i, 0))
    return pl.pallas_call(
        body, name="mlp_fwd_bwd", grid=(t // tb,),
        out_shape=(jax.ShapeDtypeStruct((t, D_MODEL), F32), jax.ShapeDtypeStruct((t, D_MODEL), BF16),
                   jax.ShapeDtypeStruct((t, D_FF), BF16), jax.ShapeDtypeStruct((t, D_FF), BF16),
                   jax.ShapeDtypeStruct((t, D_MODEL), BF16), jax.ShapeDtypeStruct((8, D_MODEL), F32),
                   jax.ShapeDtypeStruct((1, D_FF), F32), jax.ShapeDtypeStruct((t // seq, 8, D_MODEL), F32)),
        in_specs=[row(D_MODEL), row(D_MODEL), pl.BlockSpec((1, N_MOD, D_MODEL), lambda i: (i // npb, 0, 0)), ANY, ANY,
                  pl.BlockSpec(vec1.shape, lambda i: (0, 0)), pl.BlockSpec(b1.shape, lambda i: (0, 0))],
        out_specs=(row(D_MODEL), row(D_MODEL), row(D_FF), row(D_FF), row(D_MODEL),
                   pl.BlockSpec((8, D_MODEL), lambda i: (0, 0)), pl.BlockSpec((1, D_FF), lambda i: (0, 0)),
                   pl.BlockSpec((1, 8, D_MODEL), lambda i: (i // npb, 0, 0))),
        scratch_shapes=[pltpu.VMEM((D_MODEL, n_fb * fb), BF16), pltpu.VMEM((D_FF, D_MODEL), BF16),
                        pltpu.SemaphoreType.DMA((n_fb,)), pltpu.SemaphoreType.DMA],
        compiler_params=_params(60),
    )(x1, tgt, mod3, w1, w2, vec1, b1)


def _ln1_out_bwd(dx1, x2, mix, mod3, w_out, ln1, seq):
    t = x2.shape[0]
    tb = 512
    npb = seq // tb

    def body(dx1_ref, x_ref, mix_ref, mod_ref, w_ref, ln_ref, dmix_ref, dxa_ref, dys_ref, dy5_ref, gacc_ref, bacc_ref):
        i = pl.program_id(0)

        @pl.when(i == 0)
        def _():
            gacc_ref[...] = jnp.zeros_like(gacc_ref)

        @pl.when(i % npb == 0)
        def _():
            bacc_ref[...] = jnp.zeros_like(bacc_ref)

        m = mod_ref[0]
        mix = mix_ref[...]
        r1 = ALPHA * x_ref[...] + (1.0 + m[2:3]) * mix
        _, xhat, rstd = _layer_norm(r1, ln_ref[0:1], ln_ref[1:2])
        dx1v = dx1_ref[...]
        dr1 = _layer_norm_bwd(dx1v, xhat, rstd, ln_ref[0:1])
        gacc_ref[0:1, :] += jnp.sum(dx1v * xhat, axis=0, keepdims=True)
        gacc_ref[1:2, :] += jnp.sum(dx1v, axis=0, keepdims=True)
        bacc_ref[0, 0:1, :] += jnp.sum(dr1 * mix, axis=0, keepdims=True)
        dmix = ((1.0 + m[2:3]) * dr1).astype(BF16)
        dmix_ref[...] = dmix
        dxa_ref[...] = ALPHA * dr1
        dys_ref[...] = lax.dot_general(dmix, w_ref[0:D_SSD, :], NT, preferred_element_type=F32)
        dy5_ref[...] = lax.dot_general(dmix, w_ref[D_SSD:, :], NT, preferred_element_type=F32)

    row = lambda w: pl.BlockSpec((tb, w), lambda i: (i, 0))
    return pl.pallas_call(
        body, name="ln1_out_bwd", grid=(t // tb,),
        out_shape=(jax.ShapeDtypeStruct((t, D_MODEL), BF16), jax.ShapeDtypeStruct((t, D_MODEL), F32),
                   jax.ShapeDtypeStruct((t, D_SSD), F32), jax.ShapeDtypeStruct((t, D_S5), F32),
                   jax.ShapeDtypeStruct((8, D_MODEL), F32), jax.ShapeDtypeStruct((t // seq, 8, D_MODEL), F32)),
        in_specs=[row(D_MODEL), row(D_MODEL), row(D_MODEL), pl.BlockSpec((1, N_MOD, D_MODEL), lambda i: (i // npb, 0, 0)),
                  pl.BlockSpec(w_out.shape, lambda i: (0, 0)), pl.BlockSpec(ln1.shape, lambda i: (0, 0))],
        out_specs=(row(D_MODEL), row(D_MODEL), row(D_SSD), row(D_S5), pl.BlockSpec((8, D_MODEL), lambda i: (0, 0)),
                   pl.BlockSpec((1, 8, D_MODEL), lambda i: (i // npb, 0, 0))),
        compiler_params=_params(48),
    )(dx1, x2, mix, mod3, w_out, ln1)


def _s5_bwd(dy5, ypre, u5, s_re, s_im, bb_re, bb_im, cc_re, cc_im, pr_re, pr_im, s5d, w_glu, b_glu, seq):
    t = u5.shape[0]
    tb = 256
    npb = seq // tb
    n_blocks = t // tb

    def blk(i):
        return (i // npb) * npb + (npb - 1 - i % npb)

    def body(dy_ref, ypre_ref, u_ref, sre_ref, sim_ref, hre_ref, him_ref, bbr_ref, bbi_ref, ccr_ref, cci_ref,
             prr_ref, pri_ref, d_ref, wg_ref, bg_ref,
             du_ref, vacc_ref, sacc_ref, dcc_ref, dbb_ref, dwg_ref, dsr, dsi, gr, gi, car, cai):
        i = pl.program_id(0)

        @pl.when(i == 0)
        def _():
            for acc in (vacc_ref, sacc_ref, dcc_ref, dbb_ref, dwg_ref):
                acc[...] = jnp.zeros_like(acc)

        @pl.when(i % npb == 0)
        def _():
            car[...] = jnp.zeros_like(car)
            cai[...] = jnp.zeros_like(cai)

        dy = dy_ref[...]
        ypre = ypre_ref[...]
        u = u_ref[...]
        ub = u.astype(BF16)
        yg = _gelu(ypre)
        sg = _sigmoid(_mm(yg, wg_ref[...]) + bg_ref[...])
        dq = dy * yg * sg * (1.0 - sg)
        dqb = dq.astype(BF16)
        dyg = dy * sg + lax.dot_general(dqb, wg_ref[...], NT, preferred_element_type=F32)
        dyp = dyg * _gelu_grad(ypre)
        dypb = dyp.astype(BF16)
        dwg_ref[...] += lax.dot_general(yg.astype(BF16), dqb, TN, preferred_element_type=F32)
        blocks = [(slice(j * 128, (j + 1) * 128), slice(j * 512, (j + 1) * 512)) for j in range(S5_BLOCKS)]
        for j, (ch, st) in enumerate(blocks):
            dsr[:, st] = lax.dot_general(dypb[:, ch], ccr_ref[j], NT, preferred_element_type=F32)
            dsi[:, st] = -lax.dot_general(dypb[:, ch], cci_ref[j], NT, preferred_element_type=F32)
        _tile_scan(dsr, dsi, gr, gi, car, cai, prr_ref, pri_ref, tb // 8, reverse=True)
        g_re, g_im = gr[...], gi[...]
        first_rows = (i % npb) == npb - 1
        hre = jnp.where(first_rows, 0.0, hre_ref[...])
        him = jnp.where(first_rows, 0.0, him_ref[...])
        s_re_v, s_im_v = sre_ref[...], sim_ref[...]
        sp_re = pltpu.roll(jnp.concatenate([hre, s_re_v], axis=0), 1, axis=0)[8:8 + tb]
        sp_im = pltpu.roll(jnp.concatenate([him, s_im_v], axis=0), 1, axis=0)[8:8 + tb]
        vacc_ref[0:1, :] += jnp.sum(g_re * sp_re + g_im * sp_im, axis=0, keepdims=True)
        vacc_ref[1:2, :] += jnp.sum(g_im * sp_re - g_re * sp_im, axis=0, keepdims=True)
        grb, gib = g_re.astype(BF16), g_im.astype(BF16)
        srb, sib = s_re_v.astype(BF16), s_im_v.astype(BF16)
        du_cols = []
        for j, (ch, st) in enumerate(blocks):
            dcc_ref[j] += lax.dot_general(srb[:, st], dypb[:, ch], TN, preferred_element_type=F32)
            dcc_ref[S5_BLOCKS + j] -= lax.dot_general(sib[:, st], dypb[:, ch], TN, preferred_element_type=F32)
            dbb_ref[j] += lax.dot_general(ub[:, ch], grb[:, st], TN, preferred_element_type=F32)
            dbb_ref[S5_BLOCKS + j] += lax.dot_general(ub[:, ch], gib[:, st], TN, preferred_element_type=F32)
            du_cols.append(lax.dot_general(grb[:, st], bbr_ref[j], NT, preferred_element_type=F32)
                           + lax.dot_general(gib[:, st], bbi_ref[j], NT, preferred_element_type=F32))
        du_ref[...] = jnp.concatenate(du_cols, axis=1) + dyp * d_ref[...]
        sacc_ref[0:1, :] += jnp.sum(dyp * u, axis=0, keepdims=True)
        sacc_ref[1:2, :] += jnp.sum(dq, axis=0, keepdims=True)

    row = lambda w: pl.BlockSpec((tb, w), lambda i: (blk(i), 0))
    halo = pl.BlockSpec((8, S5_N), lambda i: (jnp.maximum(blk(i) * (tb // 8) - 1, 0), 0))
    full = lambda a: pl.BlockSpec(a.shape, lambda i: (0,) * a.ndim)
    acc = lambda s: pl.BlockSpec(s, lambda i: (0,) * len(s))
    acc_shapes = [(8, S5_N), (8, D_S5), (2 * S5_BLOCKS, 512, 128), (2 * S5_BLOCKS, 128, 512), (D_S5, D_S5)]
    return pl.pallas_call(
        body, name="s5_bwd", grid=(n_blocks,),
        out_shape=(jax.ShapeDtypeStruct((t, D_S5), F32),) + tuple(jax.ShapeDtypeStruct(s, F32) for s in acc_shapes),
        in_specs=[row(D_S5), row(D_S5), row(D_S5), row(S5_N), row(S5_N), halo, halo, full(bb_re), full(bb_im),
                  full(cc_re), full(cc_im), full(pr_re), full(pr_im), full(s5d), full(w_glu), full(b_glu)],
        out_specs=(row(D_S5),) + tuple(acc(s) for s in acc_shapes),
        scratch_shapes=[pltpu.VMEM((tb, S5_N), F32), pltpu.VMEM((tb, S5_N), F32), pltpu.VMEM((tb, S5_N), F32),
                        pltpu.VMEM((tb, S5_N), F32), pltpu.VMEM((8, S5_N), F32), pltpu.VMEM((8, S5_N), F32)],
        compiler_params=_params(56),
    )(dy5, ypre, u5, s_re, s_im, s_re, s_im, bb_re, bb_im, cc_re, cc_im, pr_re, pr_im, s5d, w_glu, b_glu)


def _ssd_bwd(dyssd, yraw, z, xbc, dt_raw, hprev, par, dsk, normw, seq):
    t = xbc.shape[0]
    nc = seq // CHUNK
    n_chunks = t // CHUNK
    fold = _head_fold()

    def blk(i):
        return (i // nc) * nc + (nc - 1 - i % nc)

    def body(dy_ref, yraw_ref, z_ref, xbc_ref, dt_ref, hprev_ref, par_ref, dsk_ref, nw_ref, fold_ref,
             dxbc_ref, dz_ref, ddt_ref, dpar_ref, cacc_ref, dh_ref, dyr_ref):
        i = pl.program_id(0)

        @pl.when(i == 0)
        def _():
            dpar_ref[...] = jnp.zeros_like(dpar_ref)
            cacc_ref[...] = jnp.zeros_like(cacc_ref)

        @pl.when(i % nc == 0)
        def _():
            dh_ref[...] = jnp.zeros_like(dh_ref)

        zz = z_ref[...]
        sz = _sigmoid(zz)
        silu_z = zz * sz
        yraw = yraw_ref[...]
        for g in range(N_GROUPS):
            sl = slice(g * GW, (g + 1) * GW)
            v = yraw[:, sl] * silu_z[:, sl]
            r = lax.rsqrt(jnp.mean(v * v, axis=-1, keepdims=True) + EPS)
            dyg = dy_ref[:, sl]
            cacc_ref[1:2, sl] += jnp.sum(dyg * v * r, axis=0, keepdims=True)
            dyw = dyg * nw_ref[:, sl]
            dv = r * dyw - v * (r * r * r) * jnp.mean(dyw * v, axis=-1, keepdims=True)
            dyr_ref[:, sl] = dv * silu_z[:, sl]
            dz_ref[:, sl] = dv * yraw[:, sl] * (sz[:, sl] * (1.0 + zz[:, sl] * (1.0 - sz[:, sl])))

        dt, a, cs, cst, causal, tri, dt_c, ecs_c, w_c, pair_cols = _ssd_prep(dt_ref[...], par_ref[...])
        cs_last = cs[CHUNK - 1:CHUNK, :]
        causal2 = jnp.concatenate([causal, causal], axis=1)
        lane = lax.broadcasted_iota(jnp.int32, (CHUNK, 128), 1)
        left = lane < HEADDIM
        lane1 = lax.broadcasted_iota(jnp.int32, (1, 128), 1)
        x = xbc_ref[:, 0:D_SSD]
        xdt = x * dt_c
        dyr = dyr_ref[...]
        dyrb = dyr.astype(BF16)
        cacc_ref[0:1, :] += jnp.sum(dyr * x, axis=0, keepdims=True)
        dlast = jnp.zeros((1, 128), F32)
        dxdt_cols, diag_all, dww_cols = [], [], []
        for g in range(N_GROUPS):
            gs = slice(g * GW, (g + 1) * GW)
            b_sl = slice(D_SSD + g * N_STATE, D_SSD + (g + 1) * N_STATE)
            c_sl = slice(D_SSD + (N_GROUPS + g) * N_STATE, D_SSD + (N_GROUPS + g + 1) * N_STATE)
            bg = xbc_ref[:, b_sl].astype(BF16)
            cg = xbc_ref[:, c_sl].astype(BF16)
            scores = lax.dot_general(cg, bg, NT, preferred_element_type=F32)
            scores2 = jnp.concatenate([scores, scores], axis=1)
            hg = hprev_ref[0, gs, :]
            hgb = hg.astype(BF16)
            dhg = dh_ref[gs, :]
            dhgb = dhg.astype(BF16)
            q_all = lax.dot_general(bg, dhgb, NT, preferred_element_type=F32)
            dscores = jnp.zeros((CHUNK, CHUNK), F32)
            diag_cols = []
            for q in range(GW // 128):
                pair = g * (GW // 128) + q
                ps = slice(pair * 128, (pair + 1) * 128)
                decay = _pair_decay(pair_cols[pair], cst, pair, causal2)
                mcat = (scores2 * decay).astype(BF16)
                dyp = dyrb[:, ps]
                dm = lax.dot_general(dyp, _stack_heads(xdt[:, ps], left), NT, preferred_element_type=F32)
                dmd = dm * decay
                dscores = dscores + dmd[:, 0:CHUNK] + dmd[:, CHUNK:]
                rr = lax.dot_general(mcat, dyp, TN, preferred_element_type=F32)
                diag_cols.append(jnp.where(left, rr[0:CHUNK], rr[CHUNK:]))
            wq = w_c[:, gs] * q_all
            diag_g = jnp.concatenate(diag_cols, axis=1)
            diag_all.append(diag_g)
            dxdt_cols.append(diag_g + wq)
            dww_cols.append(wq * xdt[:, gs])
            dp = (ecs_c[:, gs] * dyr[:, gs]).astype(BF16)
            amat = (w_c[:, gs] * xdt[:, gs]).astype(BF16)
            dsb = dscores.astype(BF16)
            dxbc_ref[:, c_sl] = (jnp.dot(dsb, bg, preferred_element_type=F32)
                                 + jnp.dot(dp, hgb, preferred_element_type=F32))
            dxbc_ref[:, b_sl] = (lax.dot_general(dsb, cg, TN, preferred_element_type=F32)
                                 + jnp.dot(amat, dhgb, preferred_element_type=F32))
            dh_in = lax.dot_general(dp, cg, TN, preferred_element_type=F32)
            for j in range(HPG):
                hh = g * HPG + j
                js = slice(j * HEADDIM, (j + 1) * HEADDIM)
                ecl = jnp.exp(cs_last[:, hh:hh + 1])
                dlast = dlast + jnp.where(lane1 == hh, ecl * jnp.sum(dhg[js, :] * hg[js, :]), 0.0)
                dh_ref[g * GW + j * HEADDIM:g * GW + (j + 1) * HEADDIM, :] = ecl * dhg[js, :] + dh_in[js, :]
        dxdt = jnp.concatenate(dxdt_cols, axis=1)
        dxbc_ref[:, 0:D_SSD] = dxdt * dt_c + dyr * dsk_ref[...]
        dww = _dot3(jnp.concatenate(dww_cols, axis=1), fold_ref[...])
        dcs = _dot3(dyrb.astype(F32) * (yraw - x * dsk_ref[...])
                    - xdt.astype(BF16).astype(F32) * jnp.concatenate(diag_all, axis=1), fold_ref[...]) - dww
        rowid = lax.broadcasted_iota(jnp.int32, (CHUNK, 128), 0)
        dcs = dcs + jnp.where(rowid == CHUNK - 1, jnp.sum(dww, axis=0, keepdims=True) + dlast, 0.0)
        dadt = _dot3_left(tri, dcs, TN)
        ddt = _dot3(dxdt * x, fold_ref[...]) + dadt * a
        da = jnp.sum(dadt * dt, axis=0, keepdims=True)
        ddt_raw = ddt * _sigmoid(dt_ref[...] + par_ref[0:1])
        ddt_raw = jnp.where(lane < N_HEADS, ddt_raw, 0.0)
        ddt_ref[...] = ddt_raw
        dpar_ref[0:1, :] += jnp.sum(ddt_raw, axis=0, keepdims=True)
        dpar_ref[1:2, :] += jnp.where(lane1 < N_HEADS, da * a, 0.0)

    row = lambda w: pl.BlockSpec((CHUNK, w), lambda i: (blk(i), 0))
    full = lambda s: pl.BlockSpec(s, lambda i: (0,) * len(s))
    return pl.pallas_call(
        body, name="ssd_bwd", grid=(n_chunks,),
        out_shape=(jax.ShapeDtypeStruct((t, D_XBC), F32), jax.ShapeDtypeStruct((t, D_SSD), F32),
                   jax.ShapeDtypeStruct((t, DT_PAD), F32), jax.ShapeDtypeStruct((8, 128), F32),
                   jax.ShapeDtypeStruct((8, D_SSD), F32)),
        in_specs=[row(D_SSD), row(D_SSD), row(D_SSD), row(D_XBC), row(DT_PAD),
                  pl.BlockSpec((1, D_SSD, N_STATE), lambda i: (blk(i), 0, 0)),
                  full((8, 128)), full((1, D_SSD)), full((1, D_SSD)), full(fold.shape)],
        out_specs=(row(D_XBC), row(D_SSD), row(DT_PAD), full((8, 128)), full((8, D_SSD))),
        scratch_shapes=[pltpu.VMEM((D_SSD, N_STATE), F32), pltpu.VMEM((CHUNK, D_SSD), F32)],
        compiler_params=_params(48),
    )(dyssd, yraw, z, xbc, dt_raw, hprev, par, dsk, normw, fold)


def _conv_bwd(dxbc, dsilu, xbc_pre, seq):
    t = xbc_pre.shape[0]
    tb = 512
    npb = seq // tb
    cw = 640

    def body(d_ref, ds_ref, cur_ref, halo_ref, o_ref, acc_ref, win):
        i = pl.program_id(1)

        @pl.when(i == 0)
        def _():
            acc_ref[...] = jnp.zeros_like(acc_ref)

        first = (i % npb) == 0
        win[0:8, :] = jnp.where(first, 0.0, halo_ref[...])
        win[8:8 + tb, :] = cur_ref[...]
        dpre = d_ref[...] * ds_ref[...]
        o_ref[...] = dpre
        for j in range(4):
            acc_ref[3 - j:4 - j, :] += jnp.sum(dpre * win[8 - j:8 - j + tb, :], axis=0, keepdims=True)
        acc_ref[4:5, :] += jnp.sum(dpre, axis=0, keepdims=True)

    blk = pl.BlockSpec((tb, cw), lambda j, i: (i, j))
    return pl.pallas_call(
        body, name="conv_bwd", grid=(D_XBC // cw, t // tb),
        out_shape=(jax.ShapeDtypeStruct((t, D_XBC), F32), jax.ShapeDtypeStruct((8, D_XBC), F32)),
        in_specs=[blk, blk, blk, pl.BlockSpec((8, cw), lambda j, i: (jnp.maximum(i * (tb // 8) - 1, 0), j))],
        out_specs=(blk, pl.BlockSpec((8, cw), lambda j, i: (0, j))),
        scratch_shapes=[pltpu.VMEM((tb + 8, cw), F32)],
        compiler_params=_params(32),
    )(dxbc, dsilu, xbc_pre, xbc_pre)


def _proj_bwd(dz, dpre, ddt, du5, x2, dxa, mod3, conv_w, w_in_pad, seq):
    t = x2.shape[0]
    tb = 512
    npb = seq // tb
    n_blocks = t // tb

    def body(dz_ref, dp_ref, nxt_ref, ddt_ref, du5_ref, x_ref, dxa_ref, mod_ref, cw_ref, w_hbm,
             gx_ref, u_ref, dxp_ref, bacc_ref, w_vmem, sem):
        i = pl.program_id(0)
        _load_once(w_hbm, w_vmem, sem)

        @pl.when(i % npb == 0)
        def _():
            bacc_ref[...] = jnp.zeros_like(bacc_ref)

        last = (i % npb) == npb - 1
        nxt = jnp.where(last, 0.0, nxt_ref[...])
        cur = dp_ref[...]
        xx = jnp.concatenate([cur, nxt], axis=0)
        w = cw_ref[...]
        dxp = w[3:4] * cur
        for j in (1, 2, 3):
            dxp = dxp + w[3 - j:4 - j] * pltpu.roll(xx, tb + 8 - j, axis=0)[0:tb]
        dxpb = dxp.astype(BF16)
        dxp_ref[...] = dxpb
        o1, o2, o3 = D_SSD, D_SSD + D_XBC, D_SSD + D_XBC + DT_PAD
        du = (jnp.dot(dz_ref[...].astype(BF16), w_vmem[0:o1, :], preferred_element_type=F32)
              + jnp.dot(dxpb, w_vmem[o1:o2, :], preferred_element_type=F32)
              + jnp.dot(ddt_ref[...].astype(BF16), w_vmem[o2:o3, :], preferred_element_type=F32)
              + jnp.dot(du5_ref[...].astype(BF16), w_vmem[o3:, :], preferred_element_type=F32))
        m = mod_ref[0]
        xv = x_ref[...]
        u_ref[...] = (xv * (1.0 + m[1:2]) + m[0:1]).astype(BF16)
        gx_ref[...] = dxa_ref[...] + du * (1.0 + m[1:2])
        bacc_ref[0, 0:1, :] += jnp.sum(du, axis=0, keepdims=True)
        bacc_ref[0, 1:2, :] += jnp.sum(du * xv, axis=0, keepdims=True)

    row = lambda w: pl.BlockSpec((tb, w), lambda i: (i, 0))
    nxt_rows = pl.BlockSpec((8, D_XBC), lambda i: (jnp.minimum((i + 1) * (tb // 8), t // 8 - 1), 0))
    return pl.pallas_call(
        body, name="proj_bwd", grid=(n_blocks,),
        out_shape=(jax.ShapeDtypeStruct((t, D_MODEL), F32), jax.ShapeDtypeStruct((t, D_MODEL), BF16),
                   jax.ShapeDtypeStruct((t, D_XBC), BF16), jax.ShapeDtypeStruct((t // seq, 8, D_MODEL), F32)),
        in_specs=[row(D_SSD), row(D_XBC), nxt_rows, row(DT_PAD), row(D_S5), row(D_MODEL), row(D_MODEL),
                  pl.BlockSpec((1, N_MOD, D_MODEL), lambda i: (i // npb, 0, 0)),
                  pl.BlockSpec((4, D_XBC), lambda i: (0, 0)), ANY],
        out_specs=(row(D_MODEL), row(D_MODEL), row(D_XBC), pl.BlockSpec((1, 8, D_MODEL), lambda i: (i // npb, 0, 0))),
        scratch_shapes=[pltpu.VMEM((D_INP, D_MODEL), BF16), pltpu.SemaphoreType.DMA],
        compiler_params=_params(60),
    )(dz, dpre, dpre, ddt, du5, x2, dxa, mod3, conv_w, w_in_pad)


def _pad_rows(a, mult):
    r = a.shape[0]
    pad = (-r) % mult
    return a if pad == 0 else jnp.concatenate([a, jnp.zeros((pad,) + a.shape[1:], a.dtype)], axis=0)


_SMALL = ["conv_w", "conv_b", "dt_bias", "a_log", "d_ssd", "norm_w", "s5_a_re", "s5_a_im", "s5_log_dt", "s5_b_re",
          "s5_b_im", "s5_c_re", "s5_c_im", "s5_d", "b_glu", "ln1_g", "ln1_b", "b1", "b2", "ln2_g", "ln2_b"]


def _tile_rows(size):
    return 8 * (-(-size // 1024))


def _pack_small(d):
    parts = []
    for n in _SMALL:
        flat = d[n].reshape(-1).astype(F32)
        rows = _tile_rows(flat.shape[0])
        pad = rows * 128 - flat.shape[0]
        if pad:
            flat = jnp.concatenate([flat, jnp.zeros((pad,), F32)])
        parts.append(flat.reshape(rows, 128))
    return jnp.concatenate(parts, axis=0)


def _unpack_small(p, shapes):
    out, off = {}, 0
    for n in _SMALL:
        size = math.prod(shapes[n])
        rows = _tile_rows(size)
        out[n] = p[off:off + rows].reshape(-1)[:size].reshape(shapes[n])
        off += rows
    return out


def kernel(x, c, w_ada, b_ada, w_in, conv_w, conv_b, dt_bias, a_log, d_ssd, norm_w, s5_a_re, s5_a_im, s5_log_dt, s5_b_re, s5_b_im, s5_c_re, s5_c_im, s5_d, w_glu, b_glu, w_out, ln1_g, ln1_b, w1, b1, w2, b2, ln2_g, ln2_b, loss_target, m_w_ada, m_b_ada, m_w_in, m_conv_w, m_conv_b, m_dt_bias, m_a_log, m_d_ssd, m_norm_w, m_s5_a_re, m_s5_a_im, m_s5_log_dt, m_s5_b_re, m_s5_b_im, m_s5_c_re, m_s5_c_im, m_s5_d, m_w_glu, m_b_glu, m_w_out, m_ln1_g, m_ln1_b, m_w1, m_b1, m_w2, m_b2, m_ln2_g, m_ln2_b, v_w_ada, v_b_ada, v_w_in, v_conv_w, v_conv_b, v_dt_bias, v_a_log, v_d_ssd, v_norm_w, v_s5_a_re, v_s5_a_im, v_s5_log_dt, v_s5_b_re, v_s5_b_im, v_s5_c_re, v_s5_c_im, v_s5_d, v_w_glu, v_b_glu, v_w_out, v_ln1_g, v_ln1_b, v_w1, v_b1, v_w2, v_b2, v_ln2_g, v_ln2_b):
    weights = dict(w_ada=w_ada, b_ada=b_ada, w_in=w_in, conv_w=conv_w, conv_b=conv_b, dt_bias=dt_bias, a_log=a_log,
                   d_ssd=d_ssd, norm_w=norm_w, s5_a_re=s5_a_re, s5_a_im=s5_a_im, s5_log_dt=s5_log_dt, s5_b_re=s5_b_re,
                   s5_b_im=s5_b_im, s5_c_re=s5_c_re, s5_c_im=s5_c_im, s5_d=s5_d, w_glu=w_glu, b_glu=b_glu, w_out=w_out,
                   ln1_g=ln1_g, ln1_b=ln1_b, w1=w1, b1=b1, w2=w2, b2=b2, ln2_g=ln2_g, ln2_b=ln2_b)
    mom = dict(w_ada=m_w_ada, b_ada=m_b_ada, w_in=m_w_in, conv_w=m_conv_w, conv_b=m_conv_b, dt_bias=m_dt_bias,
               a_log=m_a_log, d_ssd=m_d_ssd, norm_w=m_norm_w, s5_a_re=m_s5_a_re, s5_a_im=m_s5_a_im,
               s5_log_dt=m_s5_log_dt, s5_b_re=m_s5_b_re, s5_b_im=m_s5_b_im, s5_c_re=m_s5_c_re, s5_c_im=m_s5_c_im,
               s5_d=m_s5_d, w_glu=m_w_glu, b_glu=m_b_glu, w_out=m_w_out, ln1_g=m_ln1_g, ln1_b=m_ln1_b, w1=m_w1, b1=m_b1,
               w2=m_w2, b2=m_b2, ln2_g=m_ln2_g, ln2_b=m_ln2_b)
    var = dict(w_ada=v_w_ada, b_ada=v_b_ada, w_in=v_w_in, conv_w=v_conv_w, conv_b=v_conv_b, dt_bias=v_dt_bias,
               a_log=v_a_log, d_ssd=v_d_ssd, norm_w=v_norm_w, s5_a_re=v_s5_a_re, s5_a_im=v_s5_a_im,
               s5_log_dt=v_s5_log_dt, s5_b_re=v_s5_b_re, s5_b_im=v_s5_b_im, s5_c_re=v_s5_c_re, s5_c_im=v_s5_c_im,
               s5_d=v_s5_d, w_glu=v_w_glu, b_glu=v_b_glu, w_out=v_w_out, ln1_g=v_ln1_g, ln1_b=v_ln1_b, w1=v_w1, b1=v_b1,
               w2=v_w2, b2=v_b2, ln2_g=v_ln2_g, ln2_b=v_ln2_b)
    names = list(weights)
    shapes = {n: weights[n].shape for n in names}

    nb, seq, _ = x.shape
    t = nb * seq
    dev = _dev_index()
    x2 = x.reshape(t, D_MODEL)
    tgt2 = loss_target.reshape(t, D_MODEL)

    cw_cols = conv_w.shape[2]
    small_in = jnp.concatenate([c.reshape(-1), conv_w.reshape(-1)]).reshape(-1, 128)
    big_names = ["w_in", "w_out", "w1", "w2", "w_glu"]
    local = {n: (a[0].T if n == "w_in" else a[0]) for n, a in weights.items() if n in big_names}
    shard_bf16 = {n: local[n].astype(BF16) for n in big_names}
    first = _all_gather([small_in, shard_bf16["w_in"], shard_bf16["w_glu"]], "gather_first")
    small_all = first[0].reshape(N_DEV, -1)
    c_all = small_all[:, :nb * D_MODEL].reshape(N_DEV * nb, D_MODEL)
    conv_w_full = small_all[:, nb * D_MODEL:].reshape(N_DEV, 4, cw_cols).transpose(1, 0, 2).reshape(4, D_XBC)

    w_in_t = first[1].reshape(D_IN, D_MODEL)
    w_in_pad = jnp.concatenate(
        [w_in_t[:D_SSD + D_XBC + N_HEADS], jnp.zeros((DT_PAD - N_HEADS, D_MODEL), BF16),
         w_in_t[D_SSD + D_XBC + N_HEADS:]], axis=0)
    w_glu_f = first[2].reshape(D_S5, D_S5)
    late_names = ["w_out", "w1", "w2"]

    ada_cols = w_ada.shape[2]
    b_cols = lax.dynamic_slice_in_dim(b_ada, dev * ada_cols, ada_cols, axis=1)
    mod_cols = _mod_fwd(c_all, w_ada[0], b_cols)
    mod_all = _all_gather([mod_cols], "gather_mod")[0]
    mod_mine = lax.dynamic_slice_in_dim(mod_all, dev * nb, nb, axis=1)
    mod3 = mod_mine.transpose(1, 0, 2).reshape(nb, N_MOD, D_MODEL)
    late_in, mod3 = lax.optimization_barrier(([shard_bf16[n] for n in late_names], mod3))
    late_sems = _gather_start(late_in, "gather_late_start")
    mod3 = mod3 + late_sems[4][0, 0]

    def pad_lanes(v, n):
        return jnp.concatenate([v, jnp.zeros((v.shape[0], n - v.shape[1]), F32)], axis=1)

    par = _pad_rows(jnp.concatenate([pad_lanes(dt_bias, 128), pad_lanes(a_log, 128)], axis=0), 8)
    dsk = jnp.repeat(d_ssd[0], HEADDIM).reshape(1, D_SSD)
    ar = s5_a_re.reshape(1, S5_N)
    ai = s5_a_im.reshape(1, S5_N)
    ldt = jnp.repeat(s5_log_dt[0], S5_P).reshape(1, S5_N)
    br_t = s5_b_re[0].transpose(2, 0, 1).reshape(S5_CH, S5_N)
    bi_t = s5_b_im[0].transpose(2, 0, 1).reshape(S5_CH, S5_N)
    bb_re_t, bb_im_t, pf_re, pf_im, pr_re, pr_im = _s5_params_fwd(ar, ai, ldt, br_t, bi_t)
    gpb = S5_GROUPS // S5_BLOCKS
    mask_b = (jnp.arange(128)[:, None] // S5_CH) == (jnp.arange(512)[None, :] // S5_P)

    def dense_b(bt_):
        blocks = bt_.reshape(S5_CH, S5_BLOCKS, 512).transpose(1, 0, 2)
        return jnp.where(mask_b, jnp.tile(blocks, (1, gpb, 1)), 0.0).astype(BF16)

    def dense_c(cc):
        blocks = cc[0].transpose(0, 2, 1).reshape(S5_BLOCKS, 512, S5_CH)
        return jnp.where(mask_b.T, jnp.tile(blocks, (1, 1, gpb)), 0.0).astype(BF16)

    bb_re, bb_im = dense_b(bb_re_t), dense_b(bb_im_t)
    cc_re, cc_im = dense_c(s5_c_re), dense_c(s5_c_im)
    s5d = s5_d.reshape(1, D_S5)
    ln1 = jnp.concatenate([ln1_g, ln1_b], axis=0)
    vec1 = _pad_rows(jnp.concatenate([b2, ln2_g, ln2_b], axis=0), 8)

    z, xbc_pre, xbc, dsilu, dt_raw, u5 = _proj_conv_fwd(x2, mod3, w_in_pad, conv_w_full, conv_b, seq)
    yraw, ycat, hprev = _ssd_fwd(xbc, z, dt_raw, par, dsk, norm_w, seq)
    s_re, s_im, ypre, ycat = _s5_fwd(u5, bb_re, bb_im, cc_re, cc_im, pf_re, pf_im, s5d, w_glu_f, b_glu, ycat, seq)
    sent, landed = _gather_wait(late_sems[0], late_sems[1], late_sems[2], late_sems[3], ycat, "gather_late_wait")
    gathered = {n: lax.dynamic_update_index_in_dim(l, x, dev, 0) for n, x, l in zip(late_names, sent, landed)}
    w_out_f = gathered["w_out"].reshape(2 * D_MODEL, D_MODEL)
    w1_blocks = gathered["w1"]
    w2_f = gathered["w2"].reshape(D_FF, D_MODEL)
    mix, x1 = _out_ln1(ycat, x2, mod3, w_out_f, ln1, seq)

    dx1, u2b, hb, dhpb, dob, gacc2, db1, bacc2 = _mlp_fwd_bwd(x1, tgt2, mod3, w1_blocks, w2_f, vec1, b1, seq)
    loss = lax.psum(0.5 / D_MODEL * jnp.sum(gacc2[3]), ("x", "y", "c"))

    dmixb, dxa, dyssd, dy5, gacc1, bacc1 = _ln1_out_bwd(dx1, x2, mix, mod3, w_out_f, ln1, seq)

    g_w2 = _atb(hb, dob, "gw2")
    g_w1 = _atb(u2b, dhpb, "gw1")
    g_wout = _atb(ycat, dmixb, "gwout")
    core = lax.axis_index("c").astype(jnp.int32).reshape(1)
    chip = 2 * lax.axis_index("x") + lax.axis_index("y")

    def chip_sums_of(names, grads, tag):
        by_dest = [g if g.ndim == 2 else g.reshape((4, 2) + g.shape[1:]) for g in grads]
        from_sibling = _sibling_swap(by_dest, "rs_swap_" + tag)
        return [_add_halves(g, r, core, "rs_add_" + n) for g, r, n in zip(by_dest, from_sibling, names)]

    early_names = ["w_out", "w1", "w2"]
    early_sums = chip_sums_of(early_names, [g_wout.reshape((N_DEV,) + w_out.shape[1:]), g_w1,
                                            g_w2.reshape((N_DEV,) + w2.shape[1:])], "early")
    early = _all_to_all_start(early_sums, "rs_early_start")
    s5d_after = s5d + early[4][0, 0]

    du5, vacc, sacc, d_cc, d_bb, g_wglu = _s5_bwd(dy5, ypre, u5, s_re, s_im, bb_re, bb_im, cc_re, cc_im,
                                                  pr_re, pr_im, s5d_after, w_glu_f, b_glu, seq)
    dxbc, dz, ddt, dpar, cacc = _ssd_bwd(dyssd, yraw, z, xbc, dt_raw, hprev, par, dsk, norm_w, seq)
    dpre, conv_acc = _conv_bwd(dxbc, dsilu, xbc_pre, seq)
    grad_x2, ub, dxpb, bacc0 = _proj_bwd(dz, dpre, ddt, du5, x2, dxa, mod3, conv_w_full, w_in_pad, seq)

    g_win_t = jnp.concatenate([_atb(dz, ub, "gwin_z"), _atb(dxpb, ub, "gwin_xbc"),
                               _atb(ddt, ub, "gwin_dt")[:N_HEADS], _atb(du5, ub, "gwin_s5")], axis=0)

    def diag_b(dd):
        kept = jnp.where(mask_b, dd, 0.0).reshape(S5_BLOCKS, gpb, S5_CH, 512).sum(1)
        return kept.transpose(1, 0, 2).reshape(S5_CH, S5_N)

    def diag_c(dd):
        kept = jnp.where(mask_b.T, dd, 0.0).reshape(S5_BLOCKS, 512, gpb, S5_CH).sum(2)
        return kept.reshape(S5_GROUPS, S5_P, S5_CH).transpose(0, 2, 1)

    g_ar, g_ai, g_ldt, g_br_t, g_bi_t = _s5_params_bwd(ar, ai, ldt, br_t, bi_t, vacc[0:1], vacc[1:2],
                                                      diag_b(d_bb[:S5_BLOCKS]), diag_b(d_bb[S5_BLOCKS:]))

    def from_t(gt):
        return gt.reshape(S5_CH, S5_GROUPS, S5_P).transpose(1, 2, 0)

    small_g = dict(
        conv_w=conv_acc[0:4], conv_b=conv_acc[4:5], dt_bias=dpar[0:1, :N_HEADS], a_log=dpar[1:2, :N_HEADS],
        d_ssd=cacc[0].reshape(N_HEADS, HEADDIM).sum(1), norm_w=cacc[1:2],
        s5_a_re=g_ar, s5_a_im=g_ai, s5_log_dt=g_ldt[:, :S5_GROUPS], s5_b_re=from_t(g_br_t), s5_b_im=from_t(g_bi_t),
        s5_c_re=diag_c(d_cc[:S5_BLOCKS]), s5_c_im=diag_c(d_cc[S5_BLOCKS:]), s5_d=sacc[0:1], b_glu=sacc[1:2],
        ln1_g=gacc1[0:1], ln1_b=gacc1[1:2], b1=db1, b2=gacc2[2:3], ln2_g=gacc2[0:1], ln2_b=gacc2[1:2])

    dmod = jnp.concatenate([bacc0[:, 0], bacc0[:, 1], bacc1[:, 0], bacc2[:, 0], bacc2[:, 1], bacc2[:, 2]], axis=1)
    dmod_all = _all_gather([dmod], "gather_dmod")[0].reshape(N_DEV * nb, N_MOD * D_MODEL)
    dmod_cols = lax.dynamic_slice_in_dim(dmod_all, dev * ada_cols, ada_cols, axis=1)
    g_wada, g_bada = _mod_bwd(c_all, dmod_cols, dmod_all)

    late_rs = ["w_in", "w_glu"]
    late_sums = chip_sums_of(late_rs, [g_win_t.reshape(N_DEV, w_in.shape[2], D_MODEL),
                                       g_wglu.reshape((N_DEV,) + w_glu.shape[1:])], "late")
    parts = dict(zip(late_rs, _chip_all_to_all(late_sums, "rs_late_all_to_all")))
    sent, landed = _all_to_all_wait(early[0], early[1], early[2], early[3], parts["w_in"], "rs_early_wait")
    for n, l, h in zip(early_names, landed, sent):
        parts[n] = lax.dynamic_update_index_in_dim(l, lax.dynamic_index_in_dim(h, chip, 0, keepdims=False), chip, 0)

    res = {k: {} for k in "gdmv"}
    for n in big_names:
        w_m_v = [(a[n][0].T if n == "w_in" else a[n][0]) for a in (weights, mom, var)]
        outs = _adamw(parts[n], *w_m_v, "adamw_" + n)
        for k, a in zip("gdmv", outs):
            res[k][n] = (a.T if n == "w_in" else a)[None]

    ag, ad, am, av = _adamw(g_wada[None], w_ada[0], m_w_ada[0], v_w_ada[0], "adamw_w_ada")
    for k, a in (("g", ag), ("d", ad), ("m", am), ("v", av)):
        res[k]["w_ada"] = a[None]
    bg_, bd_, bm_, bv_ = _adamw(g_bada.reshape(1, -1, 128), b_ada.reshape(-1, 128), m_b_ada.reshape(-1, 128),
                                v_b_ada.reshape(-1, 128), "adamw_b_ada")
    for k, a in (("g", bg_), ("d", bd_), ("m", bm_), ("v", bv_)):
        res[k]["b_ada"] = a.reshape(shapes["b_ada"])

    small_shapes = dict(shapes)
    small_shapes["conv_w"] = (1, 4, D_XBC)
    small_parts = _all_gather([_pack_small(small_g)], "gather_small_grads")[0]
    g_small = _unpack_small(_sum_parts(small_parts, "sum_small_grads"), small_shapes)
    replicated = [n for n in _SMALL if n != "conv_w"]
    sd_, sm_, sv_ = _adamw_many([g_small[n] for n in replicated], [weights[n] for n in replicated],
                                [mom[n] for n in replicated], [var[n] for n in replicated], "adamw_small")
    for i, n in enumerate(replicated):
        res["g"][n], res["d"][n], res["m"][n], res["v"][n] = g_small[n], sd_[i], sm_[i], sv_[i]
    g_conv_full = g_small["conv_w"][0]
    g_conv_mine = lax.dynamic_slice_in_dim(g_conv_full, dev * cw_cols, cw_cols, axis=1)
    cg_, cd_, cm_, cv_ = _adamw(g_conv_mine[None], conv_w[0], m_conv_w[0], v_conv_w[0], "adamw_conv_w")
    for k, a in (("g", cg_), ("d", cd_), ("m", cm_), ("v", cv_)):
        res[k]["conv_w"] = a[None]

    grad_x = grad_x2.reshape(nb, seq, D_MODEL)
    return (loss, grad_x, *[res["g"][n] for n in names], *[res["d"][n] for n in names],
            *[res["m"][n] for n in names], *[res["v"][n] for n in names])
```

```python
import functools
import math

import jax
import jax.numpy as jnp
from jax import lax
from jax.experimental import pallas as pl
from jax.experimental.pallas import tpu as pltpu

F32, BF16 = jnp.float32, jnp.bfloat16
MESH = pl.DeviceIdType.MESH
N_DEV = 8

D_MODEL = 1024
D_SSD = 1536
N_HEADS = 24
HEADDIM = 64
N_GROUPS = 4
HPG = 6
GW = HPG * HEADDIM
N_STATE = 128
CHUNK = 128
D_XBC = 2560
D_S5 = 512
S5_GROUPS = 32
S5_CH = 16
S5_P = 64
S5_N = S5_GROUPS * S5_P
D_IN = 4632
DT_PAD = 128
D_INP = D_SSD + D_XBC + DT_PAD + D_S5
D_FF = 4096
N_MOD = 6
ALPHA = 2.0 ** 0.25
EPS = 1e-5
LR, B1, B2, AEPS, WD, STEP = 0.001, 0.9, 0.999, 1e-08, 0.01, 10

NT = (((1,), (1,)), ((), ()))
TN = (((0,), (0,)), ((), ()))
ANY = pl.BlockSpec(memory_space=pl.ANY)
HIGHEST = lax.Precision.HIGHEST


def _mm(a, b):
    return jnp.dot(a.astype(BF16), b.astype(BF16), preferred_element_type=F32)


def _mm_nt(a, b):
    return lax.dot_general(a.astype(BF16), b.astype(BF16), NT, preferred_element_type=F32)


def _mm_tn(a, b):
    return lax.dot_general(a.astype(BF16), b.astype(BF16), TN, preferred_element_type=F32)


def _row_block(r, cap):
    best = r
    for cand in range(8, min(r, cap) + 1, 8):
        if r % cand == 0:
            best = cand
    return best if best <= cap else r


def _params(vmem_mb):
    return pltpu.CompilerParams(vmem_limit_bytes=vmem_mb << 20)


def _sigmoid(x):
    return 0.5 * (jnp.tanh(0.5 * x) + 1.0)


def _softplus(x):
    return jnp.maximum(x, 0.0) + jnp.log(1.0 + jnp.exp(-jnp.abs(x)))


_GK = math.sqrt(2.0 / math.pi)


def _gelu(x):
    return 0.5 * x * (1.0 + jnp.tanh(_GK * (x + 0.044715 * x * x * x)))


def _gelu_grad(x):
    t = jnp.tanh(_GK * (x + 0.044715 * x * x * x))
    return 0.5 * (1.0 + t) + 0.5 * x * (1.0 - t * t) * _GK * (1.0 + 3.0 * 0.044715 * x * x)


def _dev_index():
    return 4 * lax.axis_index("x") + 2 * lax.axis_index("y") + lax.axis_index("c")


def _all_gather(xs, name):
    n = len(xs)

    def body(*refs):
        x_refs, out_refs = refs[:n], refs[n:2 * n]
        send_sems, recv_sems, local_sems = refs[2 * n:]
        ix, iy, ic = lax.axis_index("x"), lax.axis_index("y"), lax.axis_index("c")
        me, sibling = (ix, iy, ic), (ix, iy, 1 - ic)
        chips = [(1 - ix, iy), (ix, 1 - iy), (1 - ix, 1 - iy)]

        def slot(a, px, py, pc):
            return out_refs[a].at[4 * px + 2 * py + pc]

        def copy(a, k, block, to, src=None):
            return pltpu.make_async_remote_copy(
                src_ref=slot(a, *block) if src is None else src, dst_ref=slot(a, *block),
                send_sem=send_sems.at[7 * a + k], recv_sem=recv_sems.at[7 * a + k], device_id=to, device_id_type=MESH)

        mine = [pltpu.make_async_copy(x_refs[a], slot(a, *me), local_sems.at[a]) for a in range(n)]
        for cp in mine:
            cp.start()
        first = []
        for j, chip in enumerate(chips):
            first += [copy(a, 1 + j, me, (*chip, ic), src=x_refs[a]) for a in range(n)]
        first += [copy(a, 0, me, sibling, src=x_refs[a]) for a in range(n)]
        for cp in first:
            cp.start()
        passed = []
        for j, chip in enumerate(chips):
            for a in range(n):
                copy(a, 1 + j, (*chip, ic), me).wait_recv()
                cp = copy(a, 4 + j, (*chip, ic), sibling)
                cp.start()
                passed.append(cp)
        for a in range(n):
            copy(a, 0, sibling, me).wait_recv()
            for j, chip in enumerate(chips):
                copy(a, 4 + j, (*chip, 1 - ic), me).wait_recv()
        for cp in first + passed:
            cp.wait_send()
        for cp in mine:
            cp.wait()

    return pl.pallas_call(
        body, name=name, out_shape=tuple(jax.ShapeDtypeStruct((N_DEV,) + x.shape, x.dtype) for x in xs),
        in_specs=[ANY] * n, out_specs=tuple([ANY] * n),
        scratch_shapes=[pltpu.SemaphoreType.DMA((7 * n,)), pltpu.SemaphoreType.DMA((7 * n,)),
                        pltpu.SemaphoreType.DMA((n,))],
    )(*xs)


HBM = pl.BlockSpec(memory_space=pltpu.HBM)
SEM = pl.BlockSpec(memory_space=pltpu.SEMAPHORE)
DATAFLOW = pltpu.SideEffectType.DATAFLOW_SIDE_EFFECTING


def _peer(k):
    ix, iy, ic = lax.axis_index("x"), lax.axis_index("y"), lax.axis_index("c")
    return (1 - ix if k & 4 else ix, 1 - iy if k & 2 else iy, 1 - ic if k & 1 else ic)


def _block_of(p):
    return 4 * p[0] + 2 * p[1] + p[2]


def _gather_start(xs, name):
    n = len(xs)
    lands = [lax.empty((N_DEV,) + x.shape, x.dtype) for x in xs]

    def body(*refs):
        x_refs, land_refs = refs[:n], refs[n:2 * n]
        send_sems, recv_sems = refs[2 * n], refs[2 * n + 1]
        token = refs[-1]
        me = _block_of(_peer(0))
        for a in range(n):
            for k in range(1, N_DEV):
                pltpu.make_async_remote_copy(
                    src_ref=x_refs[a], dst_ref=land_refs[a].at[me], send_sem=send_sems.at[7 * a + k - 1],
                    recv_sem=recv_sems.at[7 * a + k - 1], device_id=_peer(k), device_id_type=MESH).start()
        token[...] = jnp.zeros_like(token)

    outs = pl.pallas_call(
        body, name=name,
        out_shape=(pltpu.SemaphoreType.DMA((7 * n,)), pltpu.SemaphoreType.DMA((7 * n,)))
        + tuple(pltpu.HBM(x.shape, x.dtype) for x in xs) + tuple(pltpu.HBM(l.shape, l.dtype) for l in lands)
        + (jax.ShapeDtypeStruct((8, 128), F32),),
        in_specs=[HBM] * (2 * n), out_specs=(SEM, SEM) + (HBM,) * (2 * n) + (pl.BlockSpec(memory_space=pltpu.VMEM),),
        input_output_aliases={i: 2 + i for i in range(2 * n)},
        compiler_params=pltpu.CompilerParams(has_side_effects=DATAFLOW),
    )(*[pltpu.with_memory_space_constraint(x, pltpu.HBM) for x in xs],
      *[pltpu.with_memory_space_constraint(l, pltpu.HBM) for l in lands])
    return outs[0], outs[1], outs[2:2 + n], outs[2 + n:2 + 2 * n], outs[-1]


def _gather_wait(send_sems, recv_sems, xs_thru, lands_thru, after, name):
    n = len(xs_thru)

    def body(*refs):
        x_refs, land_refs = refs[:n], refs[n:2 * n]
        send_sems, recv_sems = refs[2 * n], refs[2 * n + 1]
        for a in range(n):
            for k in range(1, N_DEV):
                cp = pltpu.make_async_remote_copy(
                    src_ref=x_refs[a], dst_ref=land_refs[a].at[_block_of(_peer(k))], send_sem=send_sems.at[7 * a + k - 1],
                    recv_sem=recv_sems.at[7 * a + k - 1], device_id=_peer(k), device_id_type=MESH)
                cp.wait_send()
                cp.wait_recv()

    outs = pl.pallas_call(
        body, name=name,
        out_shape=tuple(pltpu.HBM(x.shape, x.dtype) for x in xs_thru)
        + tuple(pltpu.HBM(l.shape, l.dtype) for l in lands_thru),
        in_specs=[HBM] * (2 * n) + [SEM, SEM, ANY], out_specs=(HBM,) * (2 * n),
        input_output_aliases={i: i for i in range(2 * n)},
        compiler_params=pltpu.CompilerParams(has_side_effects=DATAFLOW),
    )(*xs_thru, *lands_thru, send_sems, recv_sems, after)
    return outs[:n], outs[n:]


def _chip_peer(k):
    ix, iy = lax.axis_index("x"), lax.axis_index("y")
    return (1 - ix if k & 2 else ix, 1 - iy if k & 1 else iy)


def _all_to_all_start(hs, name):
    n = len(hs)
    lands = [lax.empty(h.shape, h.dtype) for h in hs]

    def body(*refs):
        h_refs, land_refs = refs[:n], refs[n:2 * n]
        send_sems, recv_sems = refs[2 * n], refs[2 * n + 1]
        token = refs[-1]
        ic = lax.axis_index("c")
        mx, my = _chip_peer(0)
        for a in range(n):
            for k in range(1, 4):
                px, py = _chip_peer(k)
                pltpu.make_async_remote_copy(
                    src_ref=h_refs[a].at[2 * px + py], dst_ref=land_refs[a].at[2 * mx + my],
                    send_sem=send_sems.at[3 * a + k - 1], recv_sem=recv_sems.at[3 * a + k - 1],
                    device_id=(px, py, ic), device_id_type=MESH).start()
        token[...] = jnp.zeros_like(token)

    outs = pl.pallas_call(
        body, name=name,
        out_shape=(pltpu.SemaphoreType.DMA((3 * n,)), pltpu.SemaphoreType.DMA((3 * n,)))
        + tuple(pltpu.HBM(h.shape, h.dtype) for h in hs) + tuple(pltpu.HBM(l.shape, l.dtype) for l in lands)
        + (jax.ShapeDtypeStruct((8, 128), F32),),
        in_specs=[HBM] * (2 * n), out_specs=(SEM, SEM) + (HBM,) * (2 * n) + (pl.BlockSpec(memory_space=pltpu.VMEM),),
        input_output_aliases={i: 2 + i for i in range(2 * n)},
        compiler_params=pltpu.CompilerParams(has_side_effects=DATAFLOW),
    )(*[pltpu.with_memory_space_constraint(h, pltpu.HBM) for h in hs],
      *[pltpu.with_memory_space_constraint(l, pltpu.HBM) for l in lands])
    return outs[0], outs[1], outs[2:2 + n], outs[2 + n:2 + 2 * n], outs[-1]


def _all_to_all_wait(send_sems, recv_sems, hs_thru, lands_thru, after, name):
    n = len(hs_thru)

    def body(*refs):
        h_refs, land_refs = refs[:n], refs[n:2 * n]
        send_sems, recv_sems = refs[2 * n], refs[2 * n + 1]
        ic = lax.axis_index("c")
        for a in range(n):
            for k in range(1, 4):
                px, py = _chip_peer(k)
                cp = pltpu.make_async_remote_copy(
                    src_ref=h_refs[a].at[2 * px + py], dst_ref=land_refs[a].at[2 * px + py],
                    send_sem=send_sems.at[3 * a + k - 1], recv_sem=recv_sems.at[3 * a + k - 1],
                    device_id=(px, py, ic), device_id_type=MESH)
                cp.wait_send()
                cp.wait_recv()

    outs = pl.pallas_call(
        body, name=name,
        out_shape=tuple(pltpu.HBM(h.shape, h.dtype) for h in hs_thru)
        + tuple(pltpu.HBM(l.shape, l.dtype) for l in lands_thru),
        in_specs=[HBM] * (2 * n) + [SEM, SEM, ANY], out_specs=(HBM,) * (2 * n),
        input_output_aliases={i: i for i in range(2 * n)},
        compiler_params=pltpu.CompilerParams(has_side_effects=DATAFLOW),
    )(*hs_thru, *lands_thru, send_sems, recv_sems, after)
    return outs[:n], outs[n:]


def _sibling_swap(gs, name):
    n = len(gs)

    def body(*refs):
        g_refs, recv_refs = refs[:n], refs[n:2 * n]
        send_sems, recv_sems = refs[2 * n:]
        ix, iy, ic = lax.axis_index("x"), lax.axis_index("y"), lax.axis_index("c")

        def block(g_ref, q):
            if len(g_ref.shape) == 4:
                return g_ref.at[q, 1 - ic]
            cw = g_ref.shape[1] // N_DEV
            return g_ref.at[:, pl.ds(pl.multiple_of((2 * q + 1 - ic) * cw, 128), cw)]

        cps = []
        for a in range(n):
            for q in range(4):
                cps.append(pltpu.make_async_remote_copy(
                    src_ref=block(g_refs[a], q), dst_ref=recv_refs[a].at[q],
                    send_sem=send_sems.at[4 * a + q], recv_sem=recv_sems.at[4 * a + q],
                    device_id=(ix, iy, 1 - ic), device_id_type=MESH))
        for cp in cps:
            cp.start()
        for cp in cps:
            cp.wait()

    return pl.pallas_call(
        body, name=name,
        out_shape=tuple(jax.ShapeDtypeStruct(
            (4,) + (g.shape[2:] if g.ndim == 4 else (g.shape[0], g.shape[1] // N_DEV)), g.dtype) for g in gs),
        in_specs=[ANY] * n, out_specs=tuple([ANY] * n),
        scratch_shapes=[pltpu.SemaphoreType.DMA((4 * n,)), pltpu.SemaphoreType.DMA((4 * n,))],
    )(*gs)


def _chip_all_to_all(hs, name):
    n = len(hs)

    def body(*refs):
        h_refs, out_refs = refs[:n], refs[n:2 * n]
        send_sems, recv_sems, local_sems = refs[2 * n:]
        ix, iy, ic = lax.axis_index("x"), lax.axis_index("y"), lax.axis_index("c")
        me = 2 * ix + iy
        peers = [(1 - ix, iy), (ix, 1 - iy), (1 - ix, 1 - iy)]
        mine = [pltpu.make_async_copy(h_refs[a].at[me], out_refs[a].at[me], local_sems.at[a]) for a in range(n)]
        for cp in mine:
            cp.start()

        def copy(a, k, src_slot, dst_slot, peer):
            return pltpu.make_async_remote_copy(
                src_ref=h_refs[a].at[src_slot], dst_ref=out_refs[a].at[dst_slot],
                send_sem=send_sems.at[3 * a + k], recv_sem=recv_sems.at[3 * a + k],
                device_id=(*peer, ic), device_id_type=MESH)

        sends = [copy(a, k, 2 * px + py, me, (px, py)) for a in range(n) for k, (px, py) in enumerate(peers)]
        for cp in sends:
            cp.start()
        for a in range(n):
            for k, (px, py) in enumerate(peers):
                copy(a, k, 2 * px + py, 2 * px + py, (px, py)).wait_recv()
        for cp in sends:
            cp.wait_send()
        for cp in mine:
            cp.wait()

    return pl.pallas_call(
        body, name=name, out_shape=tuple(jax.ShapeDtypeStruct(h.shape, h.dtype) for h in hs),
        in_specs=[ANY] * n, out_specs=tuple([ANY] * n),
        scratch_shapes=[pltpu.SemaphoreType.DMA((3 * n,)), pltpu.SemaphoreType.DMA((3 * n,)),
                        pltpu.SemaphoreType.DMA((n,))],
    )(*hs)


def _add_halves(g, recv, core, name):
    _, r, c = recv.shape
    br = _row_block(r, 512)
    stacked = g.ndim == 4

    def body(core_ref, g_ref, r_ref, o_ref):
        o_ref[0] = ((g_ref[0, 0] if stacked else g_ref[...]) + r_ref[0]).astype(BF16)

    spec = pl.BlockSpec((1, br, c), lambda i, j, core_ref: (i, j, 0))
    if stacked:
        g_spec = pl.BlockSpec((1, 1, br, c), lambda i, j, core_ref: (i, core_ref[0], j, 0))
    else:
        g_spec = pl.BlockSpec((br, c), lambda i, j, core_ref: (j, 2 * i + core_ref[0]))
    return pl.pallas_call(
        body, name=name, out_shape=jax.ShapeDtypeStruct(recv.shape, BF16),
        grid_spec=pltpu.PrefetchScalarGridSpec(
            num_scalar_prefetch=1, grid=(4, r // br), in_specs=[g_spec, spec], out_specs=spec),
        compiler_params=_params(32),
    )(core, g, recv)


def _adamw(parts, w, m, v, name):
    n_parts, r, c = parts.shape
    if r % 8 == 0:
        br, bc = _row_block(r, 512 if c <= 1024 else 256), c
    else:
        br, bc = r, (256 if c % 256 == 0 else c)

    def body(p_ref, w_ref, m_ref, v_ref, g_out, d_out, m_out, v_out):
        g = p_ref[0].astype(F32)
        for p in range(1, n_parts):
            g = g + p_ref[p].astype(F32)
        m2 = B1 * m_ref[...] + (1.0 - B1) * g
        v2 = B2 * v_ref[...] + (1.0 - B2) * (g * g)
        m_hat = m2 / (1.0 - B1 ** STEP)
        v_hat = v2 / (1.0 - B2 ** STEP)
        g_out[...] = g
        d_out[...] = -LR * (m_hat / (jnp.sqrt(v_hat) + AEPS) + WD * w_ref[...])
        m_out[...] = m2
        v_out[...] = v2

    spec = pl.BlockSpec((br, bc), lambda i, j: (i, j))
    out = jax.ShapeDtypeStruct((r, c), F32)
    return pl.pallas_call(
        body, name=name, out_shape=(out, out, out, out), grid=(r // br, c // bc),
        in_specs=[pl.BlockSpec((n_parts, br, bc), lambda i, j: (0, i, j)), spec, spec, spec],
        out_specs=(spec, spec, spec, spec), compiler_params=_params(40),
    )(parts, w, m, v)


def _atb(a, b, name):
    t, k1 = a.shape
    k2 = b.shape[1]
    bt = math.gcd(t, 2048)

    def pick(k):
        for cand in (1024, 768, 512, 384, 256, 128):
            if k % cand == 0:
                return cand
        return k

    b1, b2 = pick(k1), pick(k2)

    def body(a_ref, b_ref, o_ref):
        @pl.when(pl.program_id(2) == 0)
        def _():
            o_ref[...] = jnp.zeros_like(o_ref)
        o_ref[...] += _mm_tn(a_ref[...], b_ref[...])

    return pl.pallas_call(
        body, name=name, out_shape=jax.ShapeDtypeStruct((k1, k2), F32), grid=(k1 // b1, k2 // b2, t // bt),
        in_specs=[pl.BlockSpec((bt, b1), lambda i, j, k: (k, i)), pl.BlockSpec((bt, b2), lambda i, j, k: (k, j))],
        out_specs=pl.BlockSpec((b1, b2), lambda i, j, k: (i, j)), compiler_params=_params(48),
    )(a, b)


def _mod_fwd(c_all, w_ada, b_cols):
    def body(c_ref, w_ref, b_ref, o_ref):
        cc = c_ref[...]
        cond = cc * _sigmoid(cc)
        o_ref[...] = _mm(cond, w_ref[...]) + b_ref[...]

    return pl.pallas_call(body, name="mod_fwd", out_shape=jax.ShapeDtypeStruct((c_all.shape[0], w_ada.shape[1]), F32),
                          compiler_params=_params(32))(c_all, w_ada, b_cols)


def _mod_bwd(c_all, dmod_cols, dmod_all):
    def body(c_ref, dc_ref, da_ref, gw_ref, gb_ref):
        cc = c_ref[...]
        cond = cc * _sigmoid(cc)
        gw_ref[...] = _mm_tn(cond, dc_ref[...])
        gb_ref[...] = jnp.sum(da_ref[...], axis=0, keepdims=True)

    return pl.pallas_call(
        body, name="mod_bwd",
        out_shape=(jax.ShapeDtypeStruct((D_MODEL, dmod_cols.shape[1]), F32), jax.ShapeDtypeStruct((1, dmod_all.shape[1]), F32)),
        compiler_params=_params(32))(c_all, dmod_cols, dmod_all)


def _load_once(hbm_ref, vmem_ref, sem):
    @pl.when(pl.program_id(0) == 0)
    def _():
        cp = pltpu.make_async_copy(hbm_ref, vmem_ref, sem)
        cp.start()
        cp.wait()


def _conv_taps(win_ref, w, tb, cols):
    shifted = [win_ref[8 - j:8 - j + tb, cols] for j in range(4)]
    acc = w[3:4] * shifted[0]
    for j in (1, 2, 3):
        acc = acc + w[3 - j:4 - j] * shifted[j]
    return acc, shifted


def _proj_conv_fwd(x2, mod3, w_in_pad, conv_w, conv_b, seq):
    t = x2.shape[0]
    tb = 256
    npb = seq // tb
    cw = 512

    def body(x_ref, mod_ref, w_hbm, cw_ref, cb_ref, z_ref, pre_ref, xbc_ref, dsilu_ref, dt_ref, u5_ref, w_vmem, win, sem):
        _load_once(w_hbm, w_vmem, sem)
        first = (pl.program_id(0) % npb) == 0

        @pl.when(first)
        def _():
            win[0:8, :] = jnp.zeros((8, D_XBC), F32)

        @pl.when(jnp.logical_not(first))
        def _():
            win[0:8, :] = win[tb:tb + 8, :]

        m = mod_ref[0]
        u = (x_ref[...] * (1.0 + m[1:2]) + m[0:1]).astype(BF16)
        z_ref[...] = lax.dot_general(u, w_vmem[0:D_SSD, :], NT, preferred_element_type=F32)
        dt_ref[...] = lax.dot_general(u, w_vmem[D_SSD + D_XBC:D_SSD + D_XBC + DT_PAD, :], NT,
                                      preferred_element_type=F32)
        u5_ref[...] = lax.dot_general(u, w_vmem[D_SSD + D_XBC + DT_PAD:, :], NT, preferred_element_type=F32)
        for k in range(D_XBC // cw):
            cols = slice(k * cw, (k + 1) * cw)
            pre_k = lax.dot_general(u, w_vmem[D_SSD + k * cw:D_SSD + (k + 1) * cw, :], NT,
                                    preferred_element_type=F32)
            win[8:8 + tb, cols] = pre_k
            pre_ref[:, cols] = pre_k
            conv, _ = _conv_taps(win, cw_ref[:, cols], tb, cols)
            conv = conv + cb_ref[:, cols]
            sg = _sigmoid(conv)
            xbc_ref[:, cols] = conv * sg
            dsilu_ref[:, cols] = sg * (1.0 + conv * (1.0 - sg))

    row = lambda w: pl.BlockSpec((tb, w), lambda i: (i, 0))
    return pl.pallas_call(
        body, name="proj_conv_fwd", grid=(t // tb,),
        out_shape=(jax.ShapeDtypeStruct((t, D_SSD), F32), jax.ShapeDtypeStruct((t, D_XBC), F32),
                   jax.ShapeDtypeStruct((t, D_XBC), F32), jax.ShapeDtypeStruct((t, D_XBC), F32),
                   jax.ShapeDtypeStruct((t, DT_PAD), F32), jax.ShapeDtypeStruct((t, D_S5), F32)),
        in_specs=[row(D_MODEL), pl.BlockSpec((1, N_MOD, D_MODEL), lambda i: (i // npb, 0, 0)), ANY,
                  pl.BlockSpec((4, D_XBC), lambda i: (0, 0)), pl.BlockSpec((1, D_XBC), lambda i: (0, 0))],
        out_specs=(row(D_SSD), row(D_XBC), row(D_XBC), row(D_XBC), row(DT_PAD), row(D_S5)),
        scratch_shapes=[pltpu.VMEM((D_INP, D_MODEL), BF16), pltpu.VMEM((tb + 8, D_XBC), F32), pltpu.SemaphoreType.DMA],
        compiler_params=_params(56),
    )(x2, mod3, w_in_pad, conv_w, conv_b)


N_PAIRS = N_HEADS // 2


def _split3(x):
    hi = x.astype(BF16)
    r = x - hi.astype(F32)
    mid = r.astype(BF16)
    lo = (r - mid.astype(F32)).astype(BF16)
    return hi, mid, lo


def _dot3(x, e, dims=(((1,), (0,)), ((), ()))):
    return sum(lax.dot_general(p, e, dims, preferred_element_type=F32) for p in _split3(x))


def _dot3_left(e, x, dims=(((1,), (0,)), ((), ()))):
    return sum(lax.dot_general(e, p, dims, preferred_element_type=F32) for p in _split3(x))


def _head_fold():
    return (jnp.arange(D_SSD)[:, None] // HEADDIM == jnp.arange(128)[None, :]).astype(BF16)


def _ssd_prep(dt_raw, par):
    dtb = par[0:1]
    a = -jnp.exp(par[1:2])
    dt = _softplus(dt_raw + dtb)
    adt = dt * a
    row = lax.broadcasted_iota(jnp.int32, (CHUNK, CHUNK), 0)
    col = lax.broadcasted_iota(jnp.int32, (CHUNK, CHUNK), 1)
    causal = row >= col
    tri = causal.astype(BF16)
    cs = _dot3_left(tri, adt)
    left = col < HEADDIM

    def lanes(v, h):
        return jnp.broadcast_to(v[:, h:h + 1], (CHUNK, 128))

    dt_c, cs_c, pair_cols = [], [], []
    for p in range(N_PAIRS):
        c0, c1 = lanes(cs, 2 * p), lanes(cs, 2 * p + 1)
        pair_cols.append(jnp.concatenate([c0, c1], axis=1))
        cs_c.append(jnp.where(left, c0, c1))
        dt_c.append(jnp.where(left, lanes(dt, 2 * p), lanes(dt, 2 * p + 1)))
    cs_c = jnp.concatenate(cs_c, axis=1)
    dt_c = jnp.concatenate(dt_c, axis=1)
    return dt, a, cs, cs.T, causal, tri, dt_c, jnp.exp(cs_c), jnp.exp(cs_c[CHUNK - 1:CHUNK, :] - cs_c), pair_cols


def _pair_decay(cols, cst, pair, causal2):
    rows = jnp.concatenate([jnp.broadcast_to(cst[2 * pair:2 * pair + 1, :], (CHUNK, CHUNK)),
                            jnp.broadcast_to(cst[2 * pair + 1:2 * pair + 2, :], (CHUNK, CHUNK))], axis=1)
    return jnp.exp(jnp.where(causal2, cols - rows, -jnp.inf))


def _stack_heads(xp, left):
    return jnp.concatenate([jnp.where(left, xp, 0.0), jnp.where(left, 0.0, xp)], axis=0).astype(BF16)


def _ssd_fwd(xbc, z, dt_raw, par, dsk, normw, seq):
    t = xbc.shape[0]
    nc = seq // CHUNK
    n_chunks = t // CHUNK

    def body(xbc_ref, z_ref, dt_ref, par_ref, dsk_ref, nw_ref, yraw_ref, ycat_ref, hprev_ref, h_ref):
        @pl.when(pl.program_id(0) % nc == 0)
        def _():
            h_ref[...] = jnp.zeros_like(h_ref)
        hprev_ref[0] = h_ref[...]
        _, _, cs, cst, causal, _, dt_c, ecs_c, w_c, pair_cols = _ssd_prep(dt_ref[...], par_ref[...])
        cs_last = cs[CHUNK - 1:CHUNK, :]
        causal2 = jnp.concatenate([causal, causal], axis=1)
        left = lax.broadcasted_iota(jnp.int32, (CHUNK, 128), 1) < HEADDIM
        x = xbc_ref[:, 0:D_SSD]
        xdt = x * dt_c
        amat = (w_c * xdt).astype(BF16)
        zz = z_ref[...]
        silu_z = zz * _sigmoid(zz)
        for g in range(N_GROUPS):
            gs = slice(g * GW, (g + 1) * GW)
            bg = xbc_ref[:, D_SSD + g * N_STATE:D_SSD + (g + 1) * N_STATE].astype(BF16)
            cg = xbc_ref[:, D_SSD + (N_GROUPS + g) * N_STATE:D_SSD + (N_GROUPS + g + 1) * N_STATE].astype(BF16)
            scores = lax.dot_general(cg, bg, NT, preferred_element_type=F32)
            scores2 = jnp.concatenate([scores, scores], axis=1)
            hg = h_ref[gs, :]
            p_all = lax.dot_general(cg, hg.astype(BF16), NT, preferred_element_type=F32)
            ys = []
            for q in range(GW // 128):
                pair = g * (GW // 128) + q
                decay = _pair_decay(pair_cols[pair], cst, pair, causal2)
                mcat = (scores2 * decay).astype(BF16)
                ys.append(jnp.dot(mcat, _stack_heads(xdt[:, pair * 128:(pair + 1) * 128], left),
                                  preferred_element_type=F32))
            yg = jnp.concatenate(ys, axis=1) + ecs_c[:, gs] * p_all + x[:, gs] * dsk_ref[:, gs]
            s_new = lax.dot_general(amat[:, gs], bg, TN, preferred_element_type=F32)
            for j in range(HPG):
                hh = g * HPG + j
                js = slice(j * HEADDIM, (j + 1) * HEADDIM)
                h_ref[g * GW + j * HEADDIM:g * GW + (j + 1) * HEADDIM, :] = (
                    hg[js, :] * jnp.exp(cs_last[:, hh:hh + 1]) + s_new[js, :])
            yraw_ref[:, gs] = yg
            v = yg * silu_z[:, gs]
            r = lax.rsqrt(jnp.mean(v * v, axis=-1, keepdims=True) + EPS)
            ycat_ref[:, gs] = (v * r * nw_ref[:, gs]).astype(BF16)

    row = lambda w: pl.BlockSpec((CHUNK, w), lambda i: (i, 0))
    full = lambda s: pl.BlockSpec(s, lambda i: (0,) * len(s))
    return pl.pallas_call(
        body, name="ssd_fwd", grid=(n_chunks,),
        out_shape=(jax.ShapeDtypeStruct((t, D_SSD), F32), jax.ShapeDtypeStruct((t, D_SSD + D_S5), BF16),
                   jax.ShapeDtypeStruct((n_chunks, D_SSD, N_STATE), F32)),
        in_specs=[row(D_XBC), row(D_SSD), row(DT_PAD), full((8, 128)), full((1, D_SSD)), full((1, D_SSD))],
        out_specs=(row(D_SSD), row(D_SSD), pl.BlockSpec((1, D_SSD, N_STATE), lambda i: (i, 0, 0))),
        scratch_shapes=[pltpu.VMEM((D_SSD, N_STATE), F32)],
        compiler_params=_params(40),
    )(xbc, z, dt_raw, par, dsk, normw)


S5_CW = 512
S5_BLOCKS = 4


def _tile_scan(in_re, in_im, out_re, out_im, carry_re, carry_im, pw_re, pw_im, n_tiles, reverse):
    steps = (1, 2, 4)
    for cc in range(S5_N // S5_CW):
        cols = slice(cc * S5_CW, (cc + 1) * S5_CW)
        a_re, a_im = pw_re[:, cols], pw_im[:, cols]
        rid = lax.broadcasted_iota(jnp.int32, (8, S5_CW), 0)
        pows = []
        for d in steps:
            k = 8 - d if reverse else d - 1
            keep = (rid < 8 - d) if reverse else (rid >= d)
            pows.append((jnp.where(keep, pw_re[k:k + 1, cols], 0.0), jnp.where(keep, pw_im[k:k + 1, cols], 0.0)))

        def tile(i, carry, cols=cols, pows=pows, a_re=a_re, a_im=a_im):
            r = (n_tiles - 1 - i) if reverse else i
            rows = pl.ds(pl.multiple_of(r * 8, 8), 8)
            xr, xi = in_re[rows, cols], in_im[rows, cols]
            for (pr, pi), d in zip(pows, steps):
                shift = 8 - d if reverse else d
                sr, si = pltpu.roll(xr, shift, axis=0), pltpu.roll(xi, shift, axis=0)
                xr, xi = xr + pr * sr - pi * si, xi + pr * si + pi * sr
            cr, ci = carry
            xr, xi = xr + a_re * cr - a_im * ci, xi + a_re * ci + a_im * cr
            out_re[rows, cols] = xr
            out_im[rows, cols] = xi
            edge = slice(0, 1) if reverse else slice(7, 8)
            return (jnp.broadcast_to(xr[edge], (8, S5_CW)), jnp.broadcast_to(xi[edge], (8, S5_CW)))

        c0 = (jnp.broadcast_to(carry_re[0:1, cols], (8, S5_CW)), jnp.broadcast_to(carry_im[0:1, cols], (8, S5_CW)))
        cr, ci = lax.fori_loop(0, n_tiles, tile, c0, unroll=True)
        carry_re[:, cols] = cr
        carry_im[:, cols] = ci


def _s5_params_math(ar, ai, ldt, br, bi):
    dt = jnp.exp(ldt)
    mag = jnp.exp(ar * dt)
    ang = ai * dt
    ab_re = mag * jnp.cos(ang)
    ab_im = mag * jnp.sin(ang)
    den = ar * ar + ai * ai
    n_re = ab_re - 1.0
    coef_re = (n_re * ar + ab_im * ai) / den
    coef_im = (ab_im * ar - n_re * ai) / den
    bb_re = coef_re * br - coef_im * bi
    bb_im = coef_re * bi + coef_im * br
    return ab_re, ab_im, bb_re, bb_im


def _s5_params_fwd(ar, ai, ldt, br, bi):
    def body(ar_ref, ai_ref, ldt_ref, br_ref, bi_ref, bbr_ref, bbi_ref, pfr_ref, pfi_ref, prr_ref, pri_ref):
        ab_re, ab_im, bb_re, bb_im = _s5_params_math(ar_ref[...], ai_ref[...], ldt_ref[...], br_ref[...], bi_ref[...])
        bbr_ref[...] = bb_re
        bbi_ref[...] = bb_im
        pr, pi = ab_re, ab_im
        for k in range(8):
            pfr_ref[k:k + 1, :] = pr
            pfi_ref[k:k + 1, :] = pi
            prr_ref[7 - k:8 - k, :] = pr
            pri_ref[7 - k:8 - k, :] = -pi
            pr, pi = pr * ab_re - pi * ab_im, pr * ab_im + pi * ab_re

    b16 = jax.ShapeDtypeStruct((S5_CH, S5_N), F32)
    p8 = jax.ShapeDtypeStruct((8, S5_N), F32)
    return pl.pallas_call(body, name="s5_params_fwd", out_shape=(b16, b16, p8, p8, p8, p8),
                          compiler_params=_params(32))(ar, ai, ldt, br, bi)


def _s5_params_bwd(ar, ai, ldt, br, bi, d_ab_re, d_ab_im, d_bb_re, d_bb_im):
    def body(ar_ref, ai_ref, ldt_ref, br_ref, bi_ref, dar_ref, dai_ref, dbr_ref, dbi_ref,
             gar_ref, gai_ref, gldt_ref, gbr_ref, gbi_ref):
        _, vjp = jax.vjp(_s5_params_math, ar_ref[...], ai_ref[...], ldt_ref[...], br_ref[...], bi_ref[...])
        g_ar, g_ai, g_ldt, g_br, g_bi = vjp((dar_ref[...], dai_ref[...], dbr_ref[...], dbi_ref[...]))
        gar_ref[...] = g_ar
        gai_ref[...] = g_ai
        gbr_ref[...] = g_br
        gbi_ref[...] = g_bi
        lane = lax.broadcasted_iota(jnp.int32, (S5_N, 128), 0) // S5_P
        grp = lax.broadcasted_iota(jnp.int32, (S5_N, 128), 1)
        fold = (lane == grp).astype(F32)
        gldt_ref[...] = jnp.dot(g_ldt, fold, preferred_element_type=F32, precision=HIGHEST)

    v1 = jax.ShapeDtypeStruct((1, S5_N), F32)
    b16 = jax.ShapeDtypeStruct((S5_CH, S5_N), F32)
    return pl.pallas_call(body, name="s5_params_bwd",
                          out_shape=(v1, v1, jax.ShapeDtypeStruct((1, 128), F32), b16, b16),
                          compiler_params=_params(32))(ar, ai, ldt, br, bi, d_ab_re, d_ab_im, d_bb_re, d_bb_im)


def _s5_fwd(u5, bb_re, bb_im, cc_re, cc_im, pf_re, pf_im, s5d, w_glu, b_glu, ycat, seq):
    t = u5.shape[0]
    tb = 256
    npb = seq // tb

    def body(u_ref, bbr_ref, bbi_ref, ccr_ref, cci_ref, pfr_ref, pfi_ref, d_ref, wg_ref, bg_ref, ycat_hbm,
             sre_ref, sim_ref, ypre_ref, y5_ref, bur, bui, car, cai):
        del ycat_hbm

        @pl.when(pl.program_id(0) % npb == 0)
        def _():
            car[...] = jnp.zeros_like(car)
            cai[...] = jnp.zeros_like(cai)
        u = u_ref[...]
        ub = u.astype(BF16)
        for j in range(S5_BLOCKS):
            ch, st = slice(j * 128, (j + 1) * 128), slice(j * 512, (j + 1) * 512)
            bur[:, st] = jnp.dot(ub[:, ch], bbr_ref[j], preferred_element_type=F32)
            bui[:, st] = jnp.dot(ub[:, ch], bbi_ref[j], preferred_element_type=F32)
        _tile_scan(bur, bui, sre_ref, sim_ref, car, cai, pfr_ref, pfi_ref, tb // 8, reverse=False)
        cs_y = []
        for j in range(S5_BLOCKS):
            st = slice(j * 512, (j + 1) * 512)
            cs_y.append(_mm(sre_ref[:, st], ccr_ref[j]) - _mm(sim_ref[:, st], cci_ref[j]))
        ypre = jnp.concatenate(cs_y, axis=1) + u * d_ref[...]
        ypre_ref[...] = ypre
        yg = _gelu(ypre)
        y5_ref[...] = (yg * _sigmoid(_mm(yg, wg_ref[...]) + bg_ref[...])).astype(BF16)

    row = lambda w: pl.BlockSpec((tb, w), lambda i: (i, 0))
    full = lambda a: pl.BlockSpec(a.shape, lambda i: (0,) * a.ndim)
    return pl.pallas_call(
        body, name="s5_fwd", grid=(t // tb,),
        out_shape=(jax.ShapeDtypeStruct((t, S5_N), F32), jax.ShapeDtypeStruct((t, S5_N), F32),
                   jax.ShapeDtypeStruct((t, D_S5), F32), jax.ShapeDtypeStruct(ycat.shape, BF16)),
        in_specs=[row(D_S5), full(bb_re), full(bb_im), full(cc_re), full(cc_im), full(pf_re), full(pf_im),
                  full(s5d), full(w_glu), full(b_glu), ANY],
        out_specs=(row(S5_N), row(S5_N), row(D_S5), pl.BlockSpec((tb, D_S5), lambda i: (i, D_SSD // D_S5))),
        input_output_aliases={10: 3},
        scratch_shapes=[pltpu.VMEM((tb, S5_N), F32), pltpu.VMEM((tb, S5_N), F32),
                        pltpu.VMEM((8, S5_N), F32), pltpu.VMEM((8, S5_N), F32)],
        compiler_params=_params(48),
    )(u5, bb_re, bb_im, cc_re, cc_im, pf_re, pf_im, s5d, w_glu, b_glu, ycat)


def _layer_norm(r, g, b):
    mu = jnp.mean(r, axis=-1, keepdims=True)
    xc = r - mu
    rstd = lax.rsqrt(jnp.mean(xc * xc, axis=-1, keepdims=True) + EPS)
    xhat = xc * rstd
    return xhat * g + b, xhat, rstd


def _layer_norm_bwd(dy, xhat, rstd, g):
    dxhat = dy * g
    return rstd * (dxhat - jnp.mean(dxhat, axis=-1, keepdims=True)
                   - xhat * jnp.mean(dxhat * xhat, axis=-1, keepdims=True))


def _out_ln1(ycat, x2, mod3, w_out, ln1, seq):
    t = x2.shape[0]
    tb = 512
    npb = seq // tb

    def body(y_ref, x_ref, mod_ref, w_ref, ln_ref, mix_ref, x1_ref):
        m = mod_ref[0]
        mix = jnp.dot(y_ref[...], w_ref[...], preferred_element_type=F32)
        mix_ref[...] = mix
        r1 = ALPHA * x_ref[...] + (1.0 + m[2:3]) * mix
        x1_ref[...] = _layer_norm(r1, ln_ref[0:1], ln_ref[1:2])[0]

    row = lambda w: pl.BlockSpec((tb, w), lambda i: (i, 0))
    return pl.pallas_call(
        body, name="out_ln1", grid=(t // tb,),
        out_shape=(jax.ShapeDtypeStruct((t, D_MODEL), F32), jax.ShapeDtypeStruct((t, D_MODEL), F32)),
        in_specs=[row(D_SSD + D_S5), row(D_MODEL), pl.BlockSpec((1, N_MOD, D_MODEL), lambda i: (i // npb, 0, 0)),
                  pl.BlockSpec(w_out.shape, lambda i: (0, 0)), pl.BlockSpec(ln1.shape, lambda i: (0, 0))],
        out_specs=(row(D_MODEL), row(D_MODEL)), compiler_params=_params(48),
    )(ycat, x2, mod3, w_out, ln1)


def _mlp_fwd_bwd(x1, tgt, mod3, w1, w2, vec1, b1, seq):
    t = x1.shape[0]
    tb = 256
    npb = seq // tb
    n_fb, _, fb = w1.shape

    def body(x1_ref, tgt_ref, mod_ref, w1_hbm, w2_hbm, v_ref, b1_ref,
             dx1_ref, u2_ref, h_ref, dhp_ref, do_ref, gacc_ref, db1_ref, bacc_ref, w1_v, w2_v, sem1, sem2):
        i = pl.program_id(0)
        @pl.when(i == 0)
        def _():
            cps = [pltpu.make_async_copy(w1_hbm.at[k], w1_v.at[:, k * fb:(k + 1) * fb], sem1.at[k])
                   for k in range(n_fb)]
            for cp in cps:
                cp.start()
            for cp in cps:
                cp.wait()
        _load_once(w2_hbm, w2_v, sem2)

        @pl.when(i == 0)
        def _():
            gacc_ref[...] = jnp.zeros_like(gacc_ref)
            db1_ref[...] = jnp.zeros_like(db1_ref)

        @pl.when(i % npb == 0)
        def _():
            bacc_ref[...] = jnp.zeros_like(bacc_ref)

        m = mod_ref[0]
        sh2, sc2, g2 = m[3:4], m[4:5], m[5:6]
        x1v = x1_ref[...]
        u2 = (x1v * (1.0 + sc2) + sh2).astype(BF16)
        u2_ref[...] = u2
        hr = jnp.maximum(jnp.dot(u2, w1_v[...], preferred_element_type=F32) + b1_ref[...], 0.0)
        hb = (hr * hr).astype(BF16)
        h_ref[...] = hb
        o = jnp.dot(hb, w2_v[...], preferred_element_type=F32) + v_ref[0:1]
        r2 = ALPHA * x1v + (1.0 + g2) * o
        y, xhat, rstd = _layer_norm(r2, v_ref[1:2], v_ref[2:3])
        err = y - tgt_ref[...]
        dy = err * (1.0 / D_MODEL)
        dr2 = _layer_norm_bwd(dy, xhat, rstd, v_ref[1:2])
        do = (1.0 + g2) * dr2
        dob = do.astype(BF16)
        do_ref[...] = dob
        gacc_ref[0:1, :] += jnp.sum(dy * xhat, axis=0, keepdims=True)
        gacc_ref[1:2, :] += jnp.sum(dy, axis=0, keepdims=True)
        gacc_ref[2:3, :] += jnp.sum(do, axis=0, keepdims=True)
        gacc_ref[3:4, :] += jnp.sum(err * err, axis=0, keepdims=True)
        dhpre = lax.dot_general(dob, w2_v[...], NT, preferred_element_type=F32) * (2.0 * hr)
        dhpb = dhpre.astype(BF16)
        dhp_ref[...] = dhpb
        db1_ref[...] += jnp.sum(dhpre, axis=0, keepdims=True)
        du2 = lax.dot_general(dhpb, w1_v[...], NT, preferred_element_type=F32)
        dx1_ref[...] = ALPHA * dr2 + du2 * (1.0 + sc2)
        bacc_ref[0, 0:1, :] += jnp.sum(du2, axis=0, keepdims=True)
        bacc_ref[0, 1:2, :] += jnp.sum(du2 * x1v, axis=0, keepdims=True)
        bacc_ref[0, 2:3, :] += jnp.sum(dr2 * o, axis=0, keepdims=True)

    row = lambda w: pl.BlockSpec((tb, w), lambda i: (i, 0))
    return pl.pallas_call(
        body, name="mlp_fwd_bwd", grid=(t // tb,),
        out_shape=(jax.ShapeDtypeStruct((t, D_MODEL), F32), jax.ShapeDtypeStruct((t, D_MODEL), BF16),
                   jax.ShapeDtypeStruct((t, D_FF), BF16), jax.ShapeDtypeStruct((t, D_FF), BF16),
                   jax.ShapeDtypeStruct((t, D_MODEL), BF16), jax.ShapeDtypeStruct((8, D_MODEL), F32),
                   jax.ShapeDtypeStruct((1, D_FF), F32), jax.ShapeDtypeStruct((t // seq, 8, D_MODEL), F32)),
        in_specs=[row(D_MODEL), row(D_MODEL), pl.BlockSpec((1, N_MOD, D_MODEL), lambda i: (i // npb, 0, 0)), ANY, ANY,
                  pl.BlockSpec(vec1.shape, lambda i: (0, 0)), pl.BlockSpec(b1.shape, lambda i: (0, 0))],
        out_specs=(row(D_MODEL), row(D_MODEL), row(D_FF), row(D_FF), row(D_MODEL),
                   pl.BlockSpec((8, D_MODEL), lambda i: (0, 0)), pl.BlockSpec((1, D_FF), lambda i: (0, 0)),
                   pl.BlockSpec((1, 8, D_MODEL), lambda i: (i // npb, 0, 0))),
        scratch_shapes=[pltpu.VMEM((D_MODEL, n_fb * fb), BF16), pltpu.VMEM((D_FF, D_MODEL), BF16),
                        pltpu.SemaphoreType.DMA((n_fb,)), pltpu.SemaphoreType.DMA],
        compiler_params=_params(60),
    )(x1, tgt, mod3, w1, w2, vec1, b1)


def _ln1_out_bwd(dx1, x2, mix, mod3, w_out, ln1, seq):
    t = x2.shape[0]
    tb = 512
    npb = seq // tb

    def body(dx1_ref, x_ref, mix_ref, mod_ref, w_ref, ln_ref, dmix_ref, dxa_ref, dys_ref, dy5_ref, gacc_ref, bacc_ref):
        i = pl.program_id(0)

        @pl.when(i == 0)
        def _():
            gacc_ref[...] = jnp.zeros_like(gacc_ref)

        @pl.when(i % npb == 0)
        def _():
            bacc_ref[...] = jnp.zeros_like(bacc_ref)

        m = mod_ref[0]
        mix = mix_ref[...]
        r1 = ALPHA * x_ref[...] + (1.0 + m[2:3]) * mix
        _, xhat, rstd = _layer_norm(r1, ln_ref[0:1], ln_ref[1:2])
        dx1v = dx1_ref[...]
        dr1 = _layer_norm_bwd(dx1v, xhat, rstd, ln_ref[0:1])
        gacc_ref[0:1, :] += jnp.sum(dx1v * xhat, axis=0, keepdims=True)
        gacc_ref[1:2, :] += jnp.sum(dx1v, axis=0, keepdims=True)
        bacc_ref[0, 0:1, :] += jnp.sum(dr1 * mix, axis=0, keepdims=True)
        dmix = ((1.0 + m[2:3]) * dr1).astype(BF16)
        dmix_ref[...] = dmix
        dxa_ref[...] = ALPHA * dr1
        dys_ref[...] = lax.dot_general(dmix, w_ref[0:D_SSD, :], NT, preferred_element_type=F32)
        dy5_ref[...] = lax.dot_general(dmix, w_ref[D_SSD:, :], NT, preferred_element_type=F32)

    row = lambda w: pl.BlockSpec((tb, w), lambda i: (i, 0))
    return pl.pallas_call(
        body, name="ln1_out_bwd", grid=(t // tb,),
        out_shape=(jax.ShapeDtypeStruct((t, D_MODEL), BF16), jax.ShapeDtypeStruct((t, D_MODEL), F32),
                   jax.ShapeDtypeStruct((t, D_SSD), F32), jax.ShapeDtypeStruct((t, D_S5), F32),
                   jax.ShapeDtypeStruct((8, D_MODEL), F32), jax.ShapeDtypeStruct((t // seq, 8, D_MODEL), F32)),
        in_specs=[row(D_MODEL), row(D_MODEL), row(D_MODEL), pl.BlockSpec((1, N_MOD, D_MODEL), lambda i: (i // npb, 0, 0)),
                  pl.BlockSpec(w_out.shape, lambda i: (0, 0)), pl.BlockSpec(ln1.shape, lambda i: (0, 0))],
        out_specs=(row(D_MODEL), row(D_MODEL), row(D_SSD), row(D_S5), pl.BlockSpec((8, D_MODEL), lambda i: (0, 0)),
                   pl.BlockSpec((1, 8, D_MODEL), lambda i: (i // npb, 0, 0))),
        compiler_params=_params(48),
    )(dx1, x2, mix, mod3, w_out, ln1)


def _s5_bwd(dy5, ypre, u5, s_re, s_im, bb_re, bb_im, cc_re, cc_im, pr_re, pr_im, s5d, w_glu, b_glu, seq):
    t = u5.shape[0]
    tb = 256
    npb = seq // tb
    n_blocks = t // tb

    def blk(i):
        return (i // npb) * npb + (npb - 1 - i % npb)

    def body(dy_ref, ypre_ref, u_ref, sre_ref, sim_ref, hre_ref, him_ref, bbr_ref, bbi_ref, ccr_ref, cci_ref,
             prr_ref, pri_ref, d_ref, wg_ref, bg_ref,
             du_ref, vacc_ref, sacc_ref, dcc_ref, dbb_ref, dwg_ref, dsr, dsi, gr, gi, car, cai):
        i = pl.program_id(0)

        @pl.when(i == 0)
        def _():
            for acc in (vacc_ref, sacc_ref, dcc_ref, dbb_ref, dwg_ref):
                acc[...] = jnp.zeros_like(acc)

        @pl.when(i % npb == 0)
        def _():
            car[...] = jnp.zeros_like(car)
            cai[...] = jnp.zeros_like(cai)

        dy = dy_ref[...]
        ypre = ypre_ref[...]
        u = u_ref[...]
        ub = u.astype(BF16)
        yg = _gelu(ypre)
        sg = _sigmoid(_mm(yg, wg_ref[...]) + bg_ref[...])
        dq = dy * yg * sg * (1.0 - sg)
        dqb = dq.astype(BF16)
        dyg = dy * sg + lax.dot_general(dqb, wg_ref[...], NT, preferred_element_type=F32)
        dyp = dyg * _gelu_grad(ypre)
        dypb = dyp.astype(BF16)
        dwg_ref[...] += lax.dot_general(yg.astype(BF16), dqb, TN, preferred_element_type=F32)
        blocks = [(slice(j * 128, (j + 1) * 128), slice(j * 512, (j + 1) * 512)) for j in range(S5_BLOCKS)]
        for j, (ch, st) in enumerate(blocks):
            dsr[:, st] = lax.dot_general(dypb[:, ch], ccr_ref[j], NT, preferred_element_type=F32)
            dsi[:, st] = -lax.dot_general(dypb[:, ch], cci_ref[j], NT, preferred_element_type=F32)
        _tile_scan(dsr, dsi, gr, gi, car, cai, prr_ref, pri_ref, tb // 8, reverse=True)
        g_re, g_im = gr[...], gi[...]
        first_rows = (i % npb) == npb - 1
        hre = jnp.where(first_rows, 0.0, hre_ref[...])
        him = jnp.where(first_rows, 0.0, him_ref[...])
        s_re_v, s_im_v = sre_ref[...], sim_ref[...]
        sp_re = pltpu.roll(jnp.concatenate([hre, s_re_v], axis=0), 1, axis=0)[8:8 + tb]
        sp_im = pltpu.roll(jnp.concatenate([him, s_im_v], axis=0), 1, axis=0)[8:8 + tb]
        vacc_ref[0:1, :] += jnp.sum(g_re * sp_re + g_im * sp_im, axis=0, keepdims=True)
        vacc_ref[1:2, :] += jnp.sum(g_im * sp_re - g_re * sp_im, axis=0, keepdims=True)
        grb, gib = g_re.astype(BF16), g_im.astype(BF16)
        srb, sib = s_re_v.astype(BF16), s_im_v.astype(BF16)
        du_cols = []
        for j, (ch, st) in enumerate(blocks):
            dcc_ref[j] += lax.dot_general(srb[:, st], dypb[:, ch], TN, preferred_element_type=F32)
            dcc_ref[S5_BLOCKS + j] -= lax.dot_general(sib[:, st], dypb[:, ch], TN, preferred_element_type=F32)
            dbb_ref[j] += lax.dot_general(ub[:, ch], grb[:, st], TN, preferred_element_type=F32)
            dbb_ref[S5_BLOCKS + j] += lax.dot_general(ub[:, ch], gib[:, st], TN, preferred_element_type=F32)
            du_cols.append(lax.dot_general(grb[:, st], bbr_ref[j], NT, preferred_element_type=F32)
                           + lax.dot_general(gib[:, st], bbi_ref[j], NT, preferred_element_type=F32))
        du_ref[...] = jnp.concatenate(du_cols, axis=1) + dyp * d_ref[...]
        sacc_ref[0:1, :] += jnp.sum(dyp * u, axis=0, keepdims=True)
        sacc_ref[1:2, :] += jnp.sum(dq, axis=0, keepdims=True)

    row = lambda w: pl.BlockSpec((tb, w), lambda i: (blk(i), 0))
    halo = pl.BlockSpec((8, S5_N), lambda i: (jnp.maximum(blk(i) * (tb // 8) - 1, 0), 0))
    full = lambda a: pl.BlockSpec(a.shape, lambda i: (0,) * a.ndim)
    acc = lambda s: pl.BlockSpec(s, lambda i: (0,) * len(s))
    acc_shapes = [(8, S5_N), (8, D_S5), (2 * S5_BLOCKS, 512, 128), (2 * S5_BLOCKS, 128, 512), (D_S5, D_S5)]
    return pl.pallas_call(
        body, name="s5_bwd", grid=(n_blocks,),
        out_shape=(jax.ShapeDtypeStruct((t, D_S5), F32),) + tuple(jax.ShapeDtypeStruct(s, F32) for s in acc_shapes),
        in_specs=[row(D_S5), row(D_S5), row(D_S5), row(S5_N), row(S5_N), halo, halo, full(bb_re), full(bb_im),
                  full(cc_re), full(cc_im), full(pr_re), full(pr_im), full(s5d), full(w_glu), full(b_glu)],
        out_specs=(row(D_S5),) + tuple(acc(s) for s in acc_shapes),
        scratch_shapes=[pltpu.VMEM((tb, S5_N), F32), pltpu.VMEM((tb, S5_N), F32), pltpu.VMEM((tb, S5_N), F32),
                        pltpu.VMEM((tb, S5_N), F32), pltpu.VMEM((8, S5_N), F32), pltpu.VMEM((8, S5_N), F32)],
        compiler_params=_params(56),
    )(dy5, ypre, u5, s_re, s_im, s_re, s_im, bb_re, bb_im, cc_re, cc_im, pr_re, pr_im, s5d, w_glu, b_glu)


def _ssd_bwd(dyssd, yraw, z, xbc, dt_raw, hprev, par, dsk, normw, seq):
    t = xbc.shape[0]
    nc = seq // CHUNK
    n_chunks = t // CHUNK
    fold = _head_fold()

    def blk(i):
        return (i // nc) * nc + (nc - 1 - i % nc)

    def body(dy_ref, yraw_ref, z_ref, xbc_ref, dt_ref, hprev_ref, par_ref, dsk_ref, nw_ref, fold_ref,
             dxbc_ref, dz_ref, ddt_ref, dpar_ref, cacc_ref, dh_ref, dyr_ref):
        i = pl.program_id(0)

        @pl.when(i == 0)
        def _():
            dpar_ref[...] = jnp.zeros_like(dpar_ref)
            cacc_ref[...] = jnp.zeros_like(cacc_ref)

        @pl.when(i % nc == 0)
        def _():
            dh_ref[...] = jnp.zeros_like(dh_ref)

        zz = z_ref[...]
        sz = _sigmoid(zz)
        silu_z = zz * sz
        yraw = yraw_ref[...]
        for g in range(N_GROUPS):
            sl = slice(g * GW, (g + 1) * GW)
            v = yraw[:, sl] * silu_z[:, sl]
            r = lax.rsqrt(jnp.mean(v * v, axis=-1, keepdims=True) + EPS)
            dyg = dy_ref[:, sl]
            cacc_ref[1:2, sl] += jnp.sum(dyg * v * r, axis=0, keepdims=True)
            dyw = dyg * nw_ref[:, sl]
            dv = r * dyw - v * (r * r * r) * jnp.mean(dyw * v, axis=-1, keepdims=True)
            dyr_ref[:, sl] = dv * silu_z[:, sl]
            dz_ref[:, sl] = dv * yraw[:, sl] * (sz[:, sl] * (1.0 + zz[:, sl] * (1.0 - sz[:, sl])))

        dt, a, cs, cst, causal, tri, dt_c, ecs_c, w_c, pair_cols = _ssd_prep(dt_ref[...], par_ref[...])
        cs_last = cs[CHUNK - 1:CHUNK, :]
        causal2 = jnp.concatenate([causal, causal], axis=1)
        lane = lax.broadcasted_iota(jnp.int32, (CHUNK, 128), 1)
        left = lane < HEADDIM
        lane1 = lax.broadcasted_iota(jnp.int32, (1, 128), 1)
        x = xbc_ref[:, 0:D_SSD]
        xdt = x * dt_c
        dyr = dyr_ref[...]
        dyrb = dyr.astype(BF16)
        cacc_ref[0:1, :] += jnp.sum(dyr * x, axis=0, keepdims=True)
        dlast = jnp.zeros((1, 128), F32)
        dxdt_cols, diag_all, dww_cols = [], [], []
        for g in range(N_GROUPS):
            gs = slice(g * GW, (g + 1) * GW)
            b_sl = slice(D_SSD + g * N_STATE, D_SSD + (g + 1) * N_STATE)
            c_sl = slice(D_SSD + (N_GROUPS + g) * N_STATE, D_SSD + (N_GROUPS + g + 1) * N_STATE)
            bg = xbc_ref[:, b_sl].astype(BF16)
            cg = xbc_ref[:, c_sl].astype(BF16)
            scores = lax.dot_general(cg, bg, NT, preferred_element_type=F32)
            scores2 = jnp.concatenate([scores, scores], axis=1)
            hg = hprev_ref[0, gs, :]
            hgb = hg.astype(BF16)
            dhg = dh_ref[gs, :]
            dhgb = dhg.astype(BF16)
            q_all = lax.dot_general(bg, dhgb, NT, preferred_element_type=F32)
            dscores = jnp.zeros((CHUNK, CHUNK), F32)
            diag_cols = []
            for q in range(GW // 128):
                pair = g * (GW // 128) + q
                ps = slice(pair * 128, (pair + 1) * 128)
                decay = _pair_decay(pair_cols[pair], cst, pair, causal2)
                mcat = (scores2 * decay).astype(BF16)
                dyp = dyrb[:, ps]
                dm = lax.dot_general(dyp, _stack_heads(xdt[:, ps], left), NT, preferred_element_type=F32)
                dmd = dm * decay
                dscores = dscores + dmd[:, 0:CHUNK] + dmd[:, CHUNK:]
                rr = lax.dot_general(mcat, dyp, TN, preferred_element_type=F32)
                diag_cols.append(jnp.where(left, rr[0:CHUNK], rr[CHUNK:]))
            wq = w_c[:, gs] * q_all
            diag_g = jnp.concatenate(diag_cols, axis=1)
            diag_all.append(diag_g)
            dxdt_cols.append(diag_g + wq)
            dww_cols.append(wq * xdt[:, gs])
            dp = (ecs_c[:, gs] * dyr[:, gs]).astype(BF16)
            amat = (w_c[:, gs] * xdt[:, gs]).astype(BF16)
            dsb = dscores.astype(BF16)
            dxbc_ref[:, c_sl] = (jnp.dot(dsb, bg, preferred_element_type=F32)
                                 + jnp.dot(dp, hgb, preferred_element_type=F32))
            dxbc_ref[:, b_sl] = (lax.dot_general(dsb, cg, TN, preferred_element_type=F32)
                                 + jnp.dot(amat, dhgb, preferred_element_type=F32))
            dh_in = lax.dot_general(dp, cg, TN, preferred_element_type=F32)
            for j in range(HPG):
                hh = g * HPG + j
                js = slice(j * HEADDIM, (j + 1) * HEADDIM)
                ecl = jnp.exp(cs_last[:, hh:hh + 1])
                dlast = dlast + jnp.where(lane1 == hh, ecl * jnp.sum(dhg[js, :] * hg[js, :]), 0.0)
                dh_ref[g * GW + j * HEADDIM:g * GW + (j + 1) * HEADDIM, :] = ecl * dhg[js, :] + dh_in[js, :]
        dxdt = jnp.concatenate(dxdt_cols, axis=1)
        dxbc_ref[:, 0:D_SSD] = dxdt * dt_c + dyr * dsk_ref[...]
        dww = _mm(jnp.concatenate(dww_cols, axis=1), fold_ref[...])
        dcs = _dot3(dyrb.astype(F32) * (yraw - x * dsk_ref[...])
                    - xdt.astype(BF16).astype(F32) * jnp.concatenate(diag_all, axis=1), fold_ref[...]) - dww
        rowid = lax.broadcasted_iota(jnp.int32, (CHUNK, 128), 0)
        dcs = dcs + jnp.where(rowid == CHUNK - 1, jnp.sum(dww, axis=0, keepdims=True) + dlast, 0.0)
        dadt = _dot3_left(tri, dcs, TN)
        ddt = _mm(dxdt * x, fold_ref[...]) + dadt * a
        da = jnp.sum(dadt * dt, axis=0, keepdims=True)
        ddt_raw = ddt * _sigmoid(dt_ref[...] + par_ref[0:1])
        ddt_raw = jnp.where(lane < N_HEADS, ddt_raw, 0.0)
        ddt_ref[...] = ddt_raw
        dpar_ref[0:1, :] += jnp.sum(ddt_raw, axis=0, keepdims=True)
        dpar_ref[1:2, :] += jnp.where(lane1 < N_HEADS, da * a, 0.0)

    row = lambda w: pl.BlockSpec((CHUNK, w), lambda i: (blk(i), 0))
    full = lambda s: pl.BlockSpec(s, lambda i: (0,) * len(s))
    return pl.pallas_call(
        body, name="ssd_bwd", grid=(n_chunks,),
        out_shape=(jax.ShapeDtypeStruct((t, D_XBC), F32), jax.ShapeDtypeStruct((t, D_SSD), F32),
                   jax.ShapeDtypeStruct((t, DT_PAD), F32), jax.ShapeDtypeStruct((8, 128), F32),
                   jax.ShapeDtypeStruct((8, D_SSD), F32)),
        in_specs=[row(D_SSD), row(D_SSD), row(D_SSD), row(D_XBC), row(DT_PAD),
                  pl.BlockSpec((1, D_SSD, N_STATE), lambda i: (blk(i), 0, 0)),
                  full((8, 128)), full((1, D_SSD)), full((1, D_SSD)), full(fold.shape)],
        out_specs=(row(D_XBC), row(D_SSD), row(DT_PAD), full((8, 128)), full((8, D_SSD))),
        scratch_shapes=[pltpu.VMEM((D_SSD, N_STATE), F32), pltpu.VMEM((CHUNK, D_SSD), F32)],
        compiler_params=_params(48),
    )(dyssd, yraw, z, xbc, dt_raw, hprev, par, dsk, normw, fold)


def _conv_bwd(dxbc, dsilu, xbc_pre, seq):
    t = xbc_pre.shape[0]
    tb = 512
    npb = seq // tb
    cw = 640

    def body(d_ref, ds_ref, cur_ref, halo_ref, o_ref, acc_ref, win):
        i = pl.program_id(1)

        @pl.when(i == 0)
        def _():
            acc_ref[...] = jnp.zeros_like(acc_ref)

        first = (i % npb) == 0
        win[0:8, :] = jnp.where(first, 0.0, halo_ref[...])
        win[8:8 + tb, :] = cur_ref[...]
        dpre = d_ref[...] * ds_ref[...]
        o_ref[...] = dpre
        for j in range(4):
            acc_ref[3 - j:4 - j, :] += jnp.sum(dpre * win[8 - j:8 - j + tb, :], axis=0, keepdims=True)
        acc_ref[4:5, :] += jnp.sum(dpre, axis=0, keepdims=True)

    blk = pl.BlockSpec((tb, cw), lambda j, i: (i, j))
    return pl.pallas_call(
        body, name="conv_bwd", grid=(D_XBC // cw, t // tb),
        out_shape=(jax.ShapeDtypeStruct((t, D_XBC), F32), jax.ShapeDtypeStruct((8, D_XBC), F32)),
        in_specs=[blk, blk, blk, pl.BlockSpec((8, cw), lambda j, i: (jnp.maximum(i * (tb // 8) - 1, 0), j))],
        out_specs=(blk, pl.BlockSpec((8, cw), lambda j, i: (0, j))),
        scratch_shapes=[pltpu.VMEM((tb + 8, cw), F32)],
        compiler_params=_params(32),
    )(dxbc, dsilu, xbc_pre, xbc_pre)


def _proj_bwd(dz, dpre, ddt, du5, x2, dxa, mod3, conv_w, w_in_pad, seq):
    t = x2.shape[0]
    tb = 512
    npb = seq // tb
    n_blocks = t // tb

    def body(dz_ref, dp_ref, nxt_ref, ddt_ref, du5_ref, x_ref, dxa_ref, mod_ref, cw_ref, w_hbm,
             gx_ref, u_ref, dxp_ref, bacc_ref, w_vmem, sem):
        i = pl.program_id(0)
        _load_once(w_hbm, w_vmem, sem)

        @pl.when(i % npb == 0)
        def _():
            bacc_ref[...] = jnp.zeros_like(bacc_ref)

        last = (i % npb) == npb - 1
        nxt = jnp.where(last, 0.0, nxt_ref[...])
        cur = dp_ref[...]
        xx = jnp.concatenate([cur, nxt], axis=0)
        w = cw_ref[...]
        dxp = w[3:4] * cur
        for j in (1, 2, 3):
            dxp = dxp + w[3 - j:4 - j] * pltpu.roll(xx, tb + 8 - j, axis=0)[0:tb]
        dxpb = dxp.astype(BF16)
        dxp_ref[...] = dxpb
        o1, o2, o3 = D_SSD, D_SSD + D_XBC, D_SSD + D_XBC + DT_PAD
        du = (jnp.dot(dz_ref[...].astype(BF16), w_vmem[0:o1, :], preferred_element_type=F32)
              + jnp.dot(dxpb, w_vmem[o1:o2, :], preferred_element_type=F32)
              + jnp.dot(ddt_ref[...].astype(BF16), w_vmem[o2:o3, :], preferred_element_type=F32)
              + jnp.dot(du5_ref[...].astype(BF16), w_vmem[o3:, :], preferred_element_type=F32))
        m = mod_ref[0]
        xv = x_ref[...]
        u_ref[...] = (xv * (1.0 + m[1:2]) + m[0:1]).astype(BF16)
        gx_ref[...] = dxa_ref[...] + du * (1.0 + m[1:2])
        bacc_ref[0, 0:1, :] += jnp.sum(du, axis=0, keepdims=True)
        bacc_ref[0, 1:2, :] += jnp.sum(du * xv, axis=0, keepdims=True)

    row = lambda w: pl.BlockSpec((tb, w), lambda i: (i, 0))
    nxt_rows = pl.BlockSpec((8, D_XBC), lambda i: (jnp.minimum((i + 1) * (tb // 8), t // 8 - 1), 0))
    return pl.pallas_call(
        body, name="proj_bwd", grid=(n_blocks,),
        out_shape=(jax.ShapeDtypeStruct((t, D_MODEL), F32), jax.ShapeDtypeStruct((t, D_MODEL), BF16),
                   jax.ShapeDtypeStruct((t, D_XBC), BF16), jax.ShapeDtypeStruct((t // seq, 8, D_MODEL), F32)),
        in_specs=[row(D_SSD), row(D_XBC), nxt_rows, row(DT_PAD), row(D_S5), row(D_MODEL), row(D_MODEL),
                  pl.BlockSpec((1, N_MOD, D_MODEL), lambda i: (i // npb, 0, 0)),
                  pl.BlockSpec((4, D_XBC), lambda i: (0, 0)), ANY],
        out_specs=(row(D_MODEL), row(D_MODEL), row(D_XBC), pl.BlockSpec((1, 8, D_MODEL), lambda i: (i // npb, 0, 0))),
        scratch_shapes=[pltpu.VMEM((D_INP, D_MODEL), BF16), pltpu.SemaphoreType.DMA],
        compiler_params=_params(60),
    )(dz, dpre, dpre, ddt, du5, x2, dxa, mod3, conv_w, w_in_pad)


def _pad_rows(a, mult):
    r = a.shape[0]
    pad = (-r) % mult
    return a if pad == 0 else jnp.concatenate([a, jnp.zeros((pad,) + a.shape[1:], a.dtype)], axis=0)


_SMALL = ["conv_w", "conv_b", "dt_bias", "a_log", "d_ssd", "norm_w", "s5_a_re", "s5_a_im", "s5_log_dt", "s5_b_re",
          "s5_b_im", "s5_c_re", "s5_c_im", "s5_d", "b_glu", "ln1_g", "ln1_b", "b1", "b2", "ln2_g", "ln2_b"]


def _tile_rows(size):
    return 8 * (-(-size // 1024))


def _pack_small(d):
    parts = []
    for n in _SMALL:
        flat = d[n].reshape(-1).astype(F32)
        rows = _tile_rows(flat.shape[0])
        pad = rows * 128 - flat.shape[0]
        if pad:
            flat = jnp.concatenate([flat, jnp.zeros((pad,), F32)])
        parts.append(flat.reshape(rows, 128))
    return jnp.concatenate(parts, axis=0)


def _unpack_small(p, shapes):
    out, off = {}, 0
    for n in _SMALL:
        size = math.prod(shapes[n])
        rows = _tile_rows(size)
        out[n] = p[off:off + rows].reshape(-1)[:size].reshape(shapes[n])
        off += rows
    return out


def kernel(x, c, w_ada, b_ada, w_in, conv_w, conv_b, dt_bias, a_log, d_ssd, norm_w, s5_a_re, s5_a_im, s5_log_dt, s5_b_re, s5_b_im, s5_c_re, s5_c_im, s5_d, w_glu, b_glu, w_out, ln1_g, ln1_b, w1, b1, w2, b2, ln2_g, ln2_b, loss_target, m_w_ada, m_b_ada, m_w_in, m_conv_w, m_conv_b, m_dt_bias, m_a_log, m_d_ssd, m_norm_w, m_s5_a_re, m_s5_a_im, m_s5_log_dt, m_s5_b_re, m_s5_b_im, m_s5_c_re, m_s5_c_im, m_s5_d, m_w_glu, m_b_glu, m_w_out, m_ln1_g, m_ln1_b, m_w1, m_b1, m_w2, m_b2, m_ln2_g, m_ln2_b, v_w_ada, v_b_ada, v_w_in, v_conv_w, v_conv_b, v_dt_bias, v_a_log, v_d_ssd, v_norm_w, v_s5_a_re, v_s5_a_im, v_s5_log_dt, v_s5_b_re, v_s5_b_im, v_s5_c_re, v_s5_c_im, v_s5_d, v_w_glu, v_b_glu, v_w_out, v_ln1_g, v_ln1_b, v_w1, v_b1, v_w2, v_b2, v_ln2_g, v_ln2_b):
    weights = dict(w_ada=w_ada, b_ada=b_ada, w_in=w_in, conv_w=conv_w, conv_b=conv_b, dt_bias=dt_bias, a_log=a_log,
                   d_ssd=d_ssd, norm_w=norm_w, s5_a_re=s5_a_re, s5_a_im=s5_a_im, s5_log_dt=s5_log_dt, s5_b_re=s5_b_re,
                   s5_b_im=s5_b_im, s5_c_re=s5_c_re, s5_c_im=s5_c_im, s5_d=s5_d, w_glu=w_glu, b_glu=b_glu, w_out=w_out,
                   ln1_g=ln1_g, ln1_b=ln1_b, w1=w1, b1=b1, w2=w2, b2=b2, ln2_g=ln2_g, ln2_b=ln2_b)
    mom = dict(w_ada=m_w_ada, b_ada=m_b_ada, w_in=m_w_in, conv_w=m_conv_w, conv_b=m_conv_b, dt_bias=m_dt_bias,
               a_log=m_a_log, d_ssd=m_d_ssd, norm_w=m_norm_w, s5_a_re=m_s5_a_re, s5_a_im=m_s5_a_im,
               s5_log_dt=m_s5_log_dt, s5_b_re=m_s5_b_re, s5_b_im=m_s5_b_im, s5_c_re=m_s5_c_re, s5_c_im=m_s5_c_im,
               s5_d=m_s5_d, w_glu=m_w_glu, b_glu=m_b_glu, w_out=m_w_out, ln1_g=m_ln1_g, ln1_b=m_ln1_b, w1=m_w1, b1=m_b1,
               w2=m_w2, b2=m_b2, ln2_g=m_ln2_g, ln2_b=m_ln2_b)
    var = dict(w_ada=v_w_ada, b_ada=v_b_ada, w_in=v_w_in, conv_w=v_conv_w, conv_b=v_conv_b, dt_bias=v_dt_bias,
               a_log=v_a_log, d_ssd=v_d_ssd, norm_w=v_norm_w, s5_a_re=v_s5_a_re, s5_a_im=v_s5_a_im,
               s5_log_dt=v_s5_log_dt, s5_b_re=v_s5_b_re, s5_b_im=v_s5_b_im, s5_c_re=v_s5_c_re, s5_c_im=v_s5_c_im,
               s5_d=v_s5_d, w_glu=v_w_glu, b_glu=v_b_glu, w_out=v_w_out, ln1_g=v_ln1_g, ln1_b=v_ln1_b, w1=v_w1, b1=v_b1,
               w2=v_w2, b2=v_b2, ln2_g=v_ln2_g, ln2_b=v_ln2_b)
    names = list(weights)
    shapes = {n: weights[n].shape for n in names}

    nb, seq, _ = x.shape
    t = nb * seq
    dev = _dev_index()
    x2 = x.reshape(t, D_MODEL)
    tgt2 = loss_target.reshape(t, D_MODEL)

    cw_cols = conv_w.shape[2]
    small_in = jnp.concatenate([c.reshape(-1), conv_w.reshape(-1)]).reshape(-1, 128)
    big_names = ["w_in", "w_out", "w1", "w2", "w_glu"]
    local = {n: (a[0].T if n == "w_in" else a[0]) for n, a in weights.items() if n in big_names}
    shard_bf16 = {n: local[n].astype(BF16) for n in big_names}
    first = _all_gather([small_in, shard_bf16["w_in"], shard_bf16["w_glu"]], "gather_first")
    small_all = first[0].reshape(N_DEV, -1)
    c_all = small_all[:, :nb * D_MODEL].reshape(N_DEV * nb, D_MODEL)
    conv_w_full = small_all[:, nb * D_MODEL:].reshape(N_DEV, 4, cw_cols).transpose(1, 0, 2).reshape(4, D_XBC)

    w_in_t = first[1].reshape(D_IN, D_MODEL)
    w_in_pad = jnp.concatenate(
        [w_in_t[:D_SSD + D_XBC + N_HEADS], jnp.zeros((DT_PAD - N_HEADS, D_MODEL), BF16),
         w_in_t[D_SSD + D_XBC + N_HEADS:]], axis=0)
    w_glu_f = first[2].reshape(D_S5, D_S5)
    late_names = ["w_out", "w1", "w2"]

    ada_cols = w_ada.shape[2]
    b_cols = lax.dynamic_slice_in_dim(b_ada, dev * ada_cols, ada_cols, axis=1)
    mod_cols = _mod_fwd(c_all, w_ada[0], b_cols)
    mod_all = _all_gather([mod_cols], "gather_mod")[0]
    mod_mine = lax.dynamic_slice_in_dim(mod_all, dev * nb, nb, axis=1)
    mod3 = mod_mine.transpose(1, 0, 2).reshape(nb, N_MOD, D_MODEL)
    late_in, mod3 = lax.optimization_barrier(([shard_bf16[n] for n in late_names], mod3))
    late_sems = _gather_start(late_in, "gather_late_start")
    mod3 = mod3 + late_sems[4][0, 0]

    def pad_lanes(v, n):
        return jnp.concatenate([v, jnp.zeros((v.shape[0], n - v.shape[1]), F32)], axis=1)

    par = _pad_rows(jnp.concatenate([pad_lanes(dt_bias, 128), pad_lanes(a_log, 128)], axis=0), 8)
    dsk = jnp.repeat(d_ssd[0], HEADDIM).reshape(1, D_SSD)
    ar = s5_a_re.reshape(1, S5_N)
    ai = s5_a_im.reshape(1, S5_N)
    ldt = jnp.repeat(s5_log_dt[0], S5_P).reshape(1, S5_N)
    br_t = s5_b_re[0].transpose(2, 0, 1).reshape(S5_CH, S5_N)
    bi_t = s5_b_im[0].transpose(2, 0, 1).reshape(S5_CH, S5_N)
    bb_re_t, bb_im_t, pf_re, pf_im, pr_re, pr_im = _s5_params_fwd(ar, ai, ldt, br_t, bi_t)
    gpb = S5_GROUPS // S5_BLOCKS
    mask_b = (jnp.arange(128)[:, None] // S5_CH) == (jnp.arange(512)[None, :] // S5_P)

    def dense_b(bt_):
        blocks = bt_.reshape(S5_CH, S5_BLOCKS, 512).transpose(1, 0, 2)
        return jnp.where(mask_b, jnp.tile(blocks, (1, gpb, 1)), 0.0).astype(BF16)

    def dense_c(cc):
        blocks = cc[0].transpose(0, 2, 1).reshape(S5_BLOCKS, 512, S5_CH)
        return jnp.where(mask_b.T, jnp.tile(blocks, (1, 1, gpb)), 0.0).astype(BF16)

    bb_re, bb_im = dense_b(bb_re_t), dense_b(bb_im_t)
    cc_re, cc_im = dense_c(s5_c_re), dense_c(s5_c_im)
    s5d = s5_d.reshape(1, D_S5)
    ln1 = jnp.concatenate([ln1_g, ln1_b], axis=0)
    vec1 = _pad_rows(jnp.concatenate([b2, ln2_g, ln2_b], axis=0), 8)

    z, xbc_pre, xbc, dsilu, dt_raw, u5 = _proj_conv_fwd(x2, mod3, w_in_pad, conv_w_full, conv_b, seq)
    yraw, ycat, hprev = _ssd_fwd(xbc, z, dt_raw, par, dsk, norm_w, seq)
    s_re, s_im, ypre, ycat = _s5_fwd(u5, bb_re, bb_im, cc_re, cc_im, pf_re, pf_im, s5d, w_glu_f, b_glu, ycat, seq)
    sent, landed = _gather_wait(late_sems[0], late_sems[1], late_sems[2], late_sems[3], ycat, "gather_late_wait")
    gathered = {n: lax.dynamic_update_index_in_dim(l, x, dev, 0) for n, x, l in zip(late_names, sent, landed)}
    w_out_f = gathered["w_out"].reshape(2 * D_MODEL, D_MODEL)
    w1_blocks = gathered["w1"]
    w2_f = gathered["w2"].reshape(D_FF, D_MODEL)
    mix, x1 = _out_ln1(ycat, x2, mod3, w_out_f, ln1, seq)

    dx1, u2b, hb, dhpb, dob, gacc2, db1, bacc2 = _mlp_fwd_bwd(x1, tgt2, mod3, w1_blocks, w2_f, vec1, b1, seq)
    loss = lax.psum(0.5 / D_MODEL * jnp.sum(gacc2[3]), ("x", "y", "c"))

    dmixb, dxa, dyssd, dy5, gacc1, bacc1 = _ln1_out_bwd(dx1, x2, mix, mod3, w_out_f, ln1, seq)

    g_w2 = _atb(hb, dob, "gw2")
    g_w1 = _atb(u2b, dhpb, "gw1")
    g_wout = _atb(ycat, dmixb, "gwout")
    core = lax.axis_index("c").astype(jnp.int32).reshape(1)
    chip = 2 * lax.axis_index("x") + lax.axis_index("y")

    def chip_sums_of(names, grads, tag):
        by_dest = [g if g.ndim == 2 else g.reshape((4, 2) + g.shape[1:]) for g in grads]
        from_sibling = _sibling_swap(by_dest, "rs_swap_" + tag)
        return [_add_halves(g, r, core, "rs_add_" + n) for g, r, n in zip(by_dest, from_sibling, names)]

    early_names = ["w_out", "w1", "w2"]
    early_sums = chip_sums_of(early_names, [g_wout.reshape((N_DEV,) + w_out.shape[1:]), g_w1,
                                            g_w2.reshape((N_DEV,) + w2.shape[1:])], "early")
    early = _all_to_all_start(early_sums, "rs_early_start")
    s5d_after = s5d + early[4][0, 0]

    du5, vacc, sacc, d_cc, d_bb, g_wglu = _s5_bwd(dy5, ypre, u5, s_re, s_im, bb_re, bb_im, cc_re, cc_im,
                                                  pr_re, pr_im, s5d_after, w_glu_f, b_glu, seq)
    dxbc, dz, ddt, dpar, cacc = _ssd_bwd(dyssd, yraw, z, xbc, dt_raw, hprev, par, dsk, norm_w, seq)
    dpre, conv_acc = _conv_bwd(dxbc, dsilu, xbc_pre, seq)
    grad_x2, ub, dxpb, bacc0 = _proj_bwd(dz, dpre, ddt, du5, x2, dxa, mod3, conv_w_full, w_in_pad, seq)

    g_win_t = jnp.concatenate([_atb(dz, ub, "gwin_z"), _atb(dxpb, ub, "gwin_xbc"),
                               _atb(ddt, ub, "gwin_dt")[:N_HEADS], _atb(du5, ub, "gwin_s5")], axis=0)

    def diag_b(dd):
        kept = jnp.where(mask_b, dd, 0.0).reshape(S5_BLOCKS, gpb, S5_CH, 512).sum(1)
        return kept.transpose(1, 0, 2).reshape(S5_CH, S5_N)

    def diag_c(dd):
        kept = jnp.where(mask_b.T, dd, 0.0).reshape(S5_BLOCKS, 512, gpb, S5_CH).sum(2)
        return kept.reshape(S5_GROUPS, S5_P, S5_CH).transpose(0, 2, 1)

    g_ar, g_ai, g_ldt, g_br_t, g_bi_t = _s5_params_bwd(ar, ai, ldt, br_t, bi_t, vacc[0:1], vacc[1:2],
                                                      diag_b(d_bb[:S5_BLOCKS]), diag_b(d_bb[S5_BLOCKS:]))

    def from_t(gt):
        return gt.reshape(S5_CH, S5_GROUPS, S5_P).transpose(1, 2, 0)

    small_g = dict(
        conv_w=conv_acc[0:4], conv_b=conv_acc[4:5], dt_bias=dpar[0:1, :N_HEADS], a_log=dpar[1:2, :N_HEADS],
        d_ssd=cacc[0].reshape(N_HEADS, HEADDIM).sum(1), norm_w=cacc[1:2],
        s5_a_re=g_ar, s5_a_im=g_ai, s5_log_dt=g_ldt[:, :S5_GROUPS], s5_b_re=from_t(g_br_t), s5_b_im=from_t(g_bi_t),
        s5_c_re=diag_c(d_cc[:S5_BLOCKS]), s5_c_im=diag_c(d_cc[S5_BLOCKS:]), s5_d=sacc[0:1], b_glu=sacc[1:2],
        ln1_g=gacc1[0:1], ln1_b=gacc1[1:2], b1=db1, b2=gacc2[2:3], ln2_g=gacc2[0:1], ln2_b=gacc2[1:2])

    dmod = jnp.concatenate([bacc0[:, 0], bacc0[:, 1], bacc1[:, 0], bacc2[:, 0], bacc2[:, 1], bacc2[:, 2]], axis=1)
    dmod_all, small_parts = _all_gather([dmod, _pack_small(small_g)], "gather_dmod_small_grads")
    dmod_all = dmod_all.reshape(N_DEV * nb, N_MOD * D_MODEL)
    dmod_cols = lax.dynamic_slice_in_dim(dmod_all, dev * ada_cols, ada_cols, axis=1)
    g_wada, g_bada = _mod_bwd(c_all, dmod_cols, dmod_all)

    late_rs = ["w_in", "w_glu"]
    late_sums = chip_sums_of(late_rs, [g_win_t.reshape(N_DEV, w_in.shape[2], D_MODEL),
                                       g_wglu.reshape((N_DEV,) + w_glu.shape[1:])], "late")
    parts = dict(zip(late_rs, _chip_all_to_all(late_sums, "rs_late_all_to_all")))
    sent, landed = _all_to_all_wait(early[0], early[1], early[2], early[3], parts["w_in"], "rs_early_wait")
    for n, l, h in zip(early_names, landed, sent):
        parts[n] = lax.dynamic_update_index_in_dim(l, lax.dynamic_index_in_dim(h, chip, 0, keepdims=False), chip, 0)

    res = {k: {} for k in "gdmv"}
    for n in big_names:
        w_m_v = [(a[n][0].T if n == "w_in" else a[n][0]) for a in (weights, mom, var)]
        outs = _adamw(parts[n], *w_m_v, "adamw_" + n)
        for k, a in zip("gdmv", outs):
            res[k][n] = (a.T if n == "w_in" else a)[None]

    ag, ad, am, av = _adamw(g_wada[None], w_ada[0], m_w_ada[0], v_w_ada[0], "adamw_w_ada")
    for k, a in (("g", ag), ("d", ad), ("m", am), ("v", av)):
        res[k]["w_ada"] = a[None]
    bg_, bd_, bm_, bv_ = _adamw(g_bada.reshape(1, -1, 128), b_ada.reshape(-1, 128), m_b_ada.reshape(-1, 128),
                                v_b_ada.reshape(-1, 128), "adamw_b_ada")
    for k, a in (("g", bg_), ("d", bd_), ("m", bm_), ("v", bv_)):
        res[k]["b_ada"] = a.reshape(shapes["b_ada"])

    small_shapes = dict(shapes)
    small_shapes["conv_w"] = (1, 4, D_XBC)
    rep = {n: (jnp.zeros((1, 4, D_XBC), F32) if n == "conv_w" else weights[n]) for n in _SMALL}
    rep_m = {n: (jnp.zeros((1, 4, D_XBC), F32) if n == "conv_w" else mom[n]) for n in _SMALL}
    rep_v = {n: (jnp.ones((1, 4, D_XBC), F32) if n == "conv_w" else var[n]) for n in _SMALL}
    sg_, sd_, sm_, sv_ = _adamw(small_parts, _pack_small(rep), _pack_small(rep_m), _pack_small(rep_v), "adamw_small")
    for k, p in (("g", sg_), ("d", sd_), ("m", sm_), ("v", sv_)):
        un = _unpack_small(p, small_shapes)
        for n in _SMALL:
            if n != "conv_w":
                res[k][n] = un[n]
    g_conv_full = _unpack_small(sg_, small_shapes)["conv_w"][0]
    g_conv_mine = lax.dynamic_slice_in_dim(g_conv_full, dev * cw_cols, cw_cols, axis=1)
    cg_, cd_, cm_, cv_ = _adamw(g_conv_mine[None], conv_w[0], m_conv_w[0], v_conv_w[0], "adamw_conv_w")
    for k, a in (("g", cg_), ("d", cd_), ("m", cm_), ("v", cv_)):
        res[k]["conv_w"] = a[None]

    grad_x = grad_x2.reshape(nb, seq, D_MODEL)
    return (loss, grad_x, *[res["g"][n] for n in names], *[res["d"][n] for n in names],
            *[res["m"][n] for n in names], *[res["v"][n] for n in names])
```

```python
import functools
import math

import jax
import jax.numpy as jnp
from jax import lax
from jax.experimental import pallas as pl
from jax.experimental.pallas import tpu as pltpu

F32, BF16 = jnp.float32, jnp.bfloat16
MESH = pl.DeviceIdType.MESH
N_DEV = 8

D_MODEL = 1024
D_SSD = 1536
N_HEADS = 24
HEADDIM = 64
N_GROUPS = 4
HPG = 6
GW = HPG * HEADDIM
N_STATE = 128
CHUNK = 128
D_XBC = 2560
D_S5 = 512
S5_GROUPS = 32
S5_CH = 16
S5_P = 64
S5_N = S5_GROUPS * S5_P
D_IN = 4632
DT_PAD = 128
D_INP = D_SSD + D_XBC + DT_PAD + D_S5
D_FF = 4096
N_MOD = 6
ALPHA = 2.0 ** 0.25
EPS = 1e-5
LR, B1, B2, AEPS, WD, STEP = 0.001, 0.9, 0.999, 1e-08, 0.01, 10

NT = (((1,), (1,)), ((), ()))
TN = (((0,), (0,)), ((), ()))
ANY = pl.BlockSpec(memory_space=pl.ANY)
HIGHEST = lax.Precision.HIGHEST


def _mm(a, b):
    return jnp.dot(a.astype(BF16), b.astype(BF16), preferred_element_type=F32)


def _mm_nt(a, b):
    return lax.dot_general(a.astype(BF16), b.astype(BF16), NT, preferred_element_type=F32)


def _mm_tn(a, b):
    return lax.dot_general(a.astype(BF16), b.astype(BF16), TN, preferred_element_type=F32)


def _row_block(r, cap):
    best = r
    for cand in range(8, min(r, cap) + 1, 8):
        if r % cand == 0:
            best = cand
    return best if best <= cap else r


def _params(vmem_mb):
    return pltpu.CompilerParams(vmem_limit_bytes=vmem_mb << 20)


def _sigmoid(x):
    return 0.5 * (jnp.tanh(0.5 * x) + 1.0)


def _softplus(x):
    return jnp.maximum(x, 0.0) + jnp.log(1.0 + jnp.exp(-jnp.abs(x)))


_GK = math.sqrt(2.0 / math.pi)


def _gelu(x):
    return 0.5 * x * (1.0 + jnp.tanh(_GK * (x + 0.044715 * x * x * x)))


def _gelu_grad(x):
    t = jnp.tanh(_GK * (x + 0.044715 * x * x * x))
    return 0.5 * (1.0 + t) + 0.5 * x * (1.0 - t * t) * _GK * (1.0 + 3.0 * 0.044715 * x * x)


def _dev_index():
    return 4 * lax.axis_index("x") + 2 * lax.axis_index("y") + lax.axis_index("c")


def _all_gather(xs, name):
    n = len(xs)

    def body(*refs):
        x_refs, out_refs = refs[:n], refs[n:2 * n]
        send_sems, recv_sems, local_sems = refs[2 * n:]
        ix, iy, ic = lax.axis_index("x"), lax.axis_index("y"), lax.axis_index("c")
        me, sibling = (ix, iy, ic), (ix, iy, 1 - ic)
        chips = [(1 - ix, iy), (ix, 1 - iy), (1 - ix, 1 - iy)]

        def slot(a, px, py, pc):
            return out_refs[a].at[4 * px + 2 * py + pc]

        def copy(a, k, block, to, src=None):
            return pltpu.make_async_remote_copy(
                src_ref=slot(a, *block) if src is None else src, dst_ref=slot(a, *block),
                send_sem=send_sems.at[7 * a + k], recv_sem=recv_sems.at[7 * a + k], device_id=to, device_id_type=MESH)

        mine = [pltpu.make_async_copy(x_refs[a], slot(a, *me), local_sems.at[a]) for a in range(n)]
        for cp in mine:
            cp.start()
        first = []
        for j, chip in enumerate(chips):
            first += [copy(a, 1 + j, me, (*chip, ic), src=x_refs[a]) for a in range(n)]
        first += [copy(a, 0, me, sibling, src=x_refs[a]) for a in range(n)]
        for cp in first:
            cp.start()
        passed = []
        for j, chip in enumerate(chips):
            for a in range(n):
                copy(a, 1 + j, (*chip, ic), me).wait_recv()
                cp = copy(a, 4 + j, (*chip, ic), sibling)
                cp.start()
                passed.append(cp)
        for a in range(n):
            copy(a, 0, sibling, me).wait_recv()
            for j, chip in enumerate(chips):
                copy(a, 4 + j, (*chip, 1 - ic), me).wait_recv()
        for cp in first + passed:
            cp.wait_send()
        for cp in mine:
            cp.wait()

    return pl.pallas_call(
        body, name=name, out_shape=tuple(jax.ShapeDtypeStruct((N_DEV,) + x.shape, x.dtype) for x in xs),
        in_specs=[ANY] * n, out_specs=tuple([ANY] * n),
        scratch_shapes=[pltpu.SemaphoreType.DMA((7 * n,)), pltpu.SemaphoreType.DMA((7 * n,)),
                        pltpu.SemaphoreType.DMA((n,))],
    )(*xs)


HBM = pl.BlockSpec(memory_space=pltpu.HBM)
SEM = pl.BlockSpec(memory_space=pltpu.SEMAPHORE)
DATAFLOW = pltpu.SideEffectType.DATAFLOW_SIDE_EFFECTING


def _peer(k):
    ix, iy, ic = lax.axis_index("x"), lax.axis_index("y"), lax.axis_index("c")
    return (1 - ix if k & 4 else ix, 1 - iy if k & 2 else iy, 1 - ic if k & 1 else ic)


def _block_of(p):
    return 4 * p[0] + 2 * p[1] + p[2]


def _gather_start(xs, name):
    n = len(xs)
    lands = [lax.empty((N_DEV,) + x.shape, x.dtype) for x in xs]

    def body(*refs):
        x_refs, land_refs = refs[:n], refs[n:2 * n]
        send_sems, recv_sems = refs[2 * n], refs[2 * n + 1]
        token = refs[-1]
        me = _block_of(_peer(0))
        for a in range(n):
            for k in range(1, N_DEV):
                pltpu.make_async_remote_copy(
                    src_ref=x_refs[a], dst_ref=land_refs[a].at[me], send_sem=send_sems.at[7 * a + k - 1],
                    recv_sem=recv_sems.at[7 * a + k - 1], device_id=_peer(k), device_id_type=MESH).start()
        token[...] = jnp.zeros_like(token)

    outs = pl.pallas_call(
        body, name=name,
        out_shape=(pltpu.SemaphoreType.DMA((7 * n,)), pltpu.SemaphoreType.DMA((7 * n,)))
        + tuple(pltpu.HBM(x.shape, x.dtype) for x in xs) + tuple(pltpu.HBM(l.shape, l.dtype) for l in lands)
        + (jax.ShapeDtypeStruct((8, 128), F32),),
        in_specs=[HBM] * (2 * n), out_specs=(SEM, SEM) + (HBM,) * (2 * n) + (pl.BlockSpec(memory_space=pltpu.VMEM),),
        input_output_aliases={i: 2 + i for i in range(2 * n)},
        compiler_params=pltpu.CompilerParams(has_side_effects=DATAFLOW),
    )(*[pltpu.with_memory_space_constraint(x, pltpu.HBM) for x in xs],
      *[pltpu.with_memory_space_constraint(l, pltpu.HBM) for l in lands])
    return outs[0], outs[1], outs[2:2 + n], outs[2 + n:2 + 2 * n], outs[-1]


def _gather_wait(send_sems, recv_sems, xs_thru, lands_thru, after, name):
    n = len(xs_thru)

    def body(*refs):
        x_refs, land_refs = refs[:n], refs[n:2 * n]
        send_sems, recv_sems = refs[2 * n], refs[2 * n + 1]
        for a in range(n):
            for k in range(1, N_DEV):
                cp = pltpu.make_async_remote_copy(
                    src_ref=x_refs[a], dst_ref=land_refs[a].at[_block_of(_peer(k))], send_sem=send_sems.at[7 * a + k - 1],
                    recv_sem=recv_sems.at[7 * a + k - 1], device_id=_peer(k), device_id_type=MESH)
                cp.wait_send()
                cp.wait_recv()

    outs = pl.pallas_call(
        body, name=name,
        out_shape=tuple(pltpu.HBM(x.shape, x.dtype) for x in xs_thru)
        + tuple(pltpu.HBM(l.shape, l.dtype) for l in lands_thru),
        in_specs=[HBM] * (2 * n) + [SEM, SEM, ANY], out_specs=(HBM,) * (2 * n),
        input_output_aliases={i: i for i in range(2 * n)},
        compiler_params=pltpu.CompilerParams(has_side_effects=DATAFLOW),
    )(*xs_thru, *lands_thru, send_sems, recv_sems, after)
    return outs[:n], outs[n:]


def _chip_peer(k):
    ix, iy = lax.axis_index("x"), lax.axis_index("y")
    return (1 - ix if k & 2 else ix, 1 - iy if k & 1 else iy)


def _all_to_all_start(hs, name):
    n = len(hs)
    lands = [lax.empty(h.shape, h.dtype) for h in hs]

    def body(*refs):
        h_refs, land_refs = refs[:n], refs[n:2 * n]
        send_sems, recv_sems = refs[2 * n], refs[2 * n + 1]
        token = refs[-1]
        ic = lax.axis_index("c")
        mx, my = _chip_peer(0)
        for a in range(n):
            for k in range(1, 4):
                px, py = _chip_peer(k)
                pltpu.make_async_remote_copy(
                    src_ref=h_refs[a].at[2 * px + py], dst_ref=land_refs[a].at[2 * mx + my],
                    send_sem=send_sems.at[3 * a + k - 1], recv_sem=recv_sems.at[3 * a + k - 1],
                    device_id=(px, py, ic), device_id_type=MESH).start()
        token[...] = jnp.zeros_like(token)

    outs = pl.pallas_call(
        body, name=name,
        out_shape=(pltpu.SemaphoreType.DMA((3 * n,)), pltpu.SemaphoreType.DMA((3 * n,)))
        + tuple(pltpu.HBM(h.shape, h.dtype) for h in hs) + tuple(pltpu.HBM(l.shape, l.dtype) for l in lands)
        + (jax.ShapeDtypeStruct((8, 128), F32),),
        in_specs=[HBM] * (2 * n), out_specs=(SEM, SEM) + (HBM,) * (2 * n) + (pl.BlockSpec(memory_space=pltpu.VMEM),),
        input_output_aliases={i: 2 + i for i in range(2 * n)},
        compiler_params=pltpu.CompilerParams(has_side_effects=DATAFLOW),
    )(*[pltpu.with_memory_space_constraint(h, pltpu.HBM) for h in hs],
      *[pltpu.with_memory_space_constraint(l, pltpu.HBM) for l in lands])
    return outs[0], outs[1], outs[2:2 + n], outs[2 + n:2 + 2 * n], outs[-1]


def _all_to_all_wait(send_sems, recv_sems, hs_thru, lands_thru, after, name):
    n = len(hs_thru)

    def body(*refs):
        h_refs, land_refs = refs[:n], refs[n:2 * n]
        send_sems, recv_sems = refs[2 * n], refs[2 * n + 1]
        ic = lax.axis_index("c")
        for a in range(n):
            for k in range(1, 4):
                px, py = _chip_peer(k)
                cp = pltpu.make_async_remote_copy(
                    src_ref=h_refs[a].at[2 * px + py], dst_ref=land_refs[a].at[2 * px + py],
                    send_sem=send_sems.at[3 * a + k - 1], recv_sem=recv_sems.at[3 * a + k - 1],
                    device_id=(px, py, ic), device_id_type=MESH)
                cp.wait_send()
                cp.wait_recv()

    outs = pl.pallas_call(
        body, name=name,
        out_shape=tuple(pltpu.HBM(h.shape, h.dtype) for h in hs_thru)
        + tuple(pltpu.HBM(l.shape, l.dtype) for l in lands_thru),
        in_specs=[HBM] * (2 * n) + [SEM, SEM, ANY], out_specs=(HBM,) * (2 * n),
        input_output_aliases={i: i for i in range(2 * n)},
        compiler_params=pltpu.CompilerParams(has_side_effects=DATAFLOW),
    )(*hs_thru, *lands_thru, send_sems, recv_sems, after)
    return outs[:n], outs[n:]


def _sibling_swap(gs, name):
    n = len(gs)

    def body(*refs):
        g_refs, recv_refs = refs[:n], refs[n:2 * n]
        send_sems, recv_sems = refs[2 * n:]
        ix, iy, ic = lax.axis_index("x"), lax.axis_index("y"), lax.axis_index("c")

        def block(g_ref, q):
            if len(g_ref.shape) == 4:
                return g_ref.at[q, 1 - ic]
            cw = g_ref.shape[1] // N_DEV
            return g_ref.at[:, pl.ds(pl.multiple_of((2 * q + 1 - ic) * cw, 128), cw)]

        cps = []
        for a in range(n):
            for q in range(4):
                cps.append(pltpu.make_async_remote_copy(
                    src_ref=block(g_refs[a], q), dst_ref=recv_refs[a].at[q],
                    send_sem=send_sems.at[4 * a + q], recv_sem=recv_sems.at[4 * a + q],
                    device_id=(ix, iy, 1 - ic), device_id_type=MESH))
        for cp in cps:
            cp.start()
        for cp in cps:
            cp.wait()

    return pl.pallas_call(
        body, name=name,
        out_shape=tuple(jax.ShapeDtypeStruct(
            (4,) + (g.shape[2:] if g.ndim == 4 else (g.shape[0], g.shape[1] // N_DEV)), g.dtype) for g in gs),
        in_specs=[ANY] * n, out_specs=tuple([ANY] * n),
        scratch_shapes=[pltpu.SemaphoreType.DMA((4 * n,)), pltpu.SemaphoreType.DMA((4 * n,))],
    )(*gs)


def _chip_all_to_all(hs, name):
    n = len(hs)

    def body(*refs):
        h_refs, out_refs = refs[:n], refs[n:2 * n]
        send_sems, recv_sems, local_sems = refs[2 * n:]
        ix, iy, ic = lax.axis_index("x"), lax.axis_index("y"), lax.axis_index("c")
        me = 2 * ix + iy
        peers = [(1 - ix, iy), (ix, 1 - iy), (1 - ix, 1 - iy)]
        mine = [pltpu.make_async_copy(h_refs[a].at[me], out_refs[a].at[me], local_sems.at[a]) for a in range(n)]
        for cp in mine:
            cp.start()

        def copy(a, k, src_slot, dst_slot, peer):
            return pltpu.make_async_remote_copy(
                src_ref=h_refs[a].at[src_slot], dst_ref=out_refs[a].at[dst_slot],
                send_sem=send_sems.at[3 * a + k], recv_sem=recv_sems.at[3 * a + k],
                device_id=(*peer, ic), device_id_type=MESH)

        sends = [copy(a, k, 2 * px + py, me, (px, py)) for a in range(n) for k, (px, py) in enumerate(peers)]
        for cp in sends:
            cp.start()
        for a in range(n):
            for k, (px, py) in enumerate(peers):
                copy(a, k, 2 * px + py, 2 * px + py, (px, py)).wait_recv()
        for cp in sends:
            cp.wait_send()
        for cp in mine:
            cp.wait()

    return pl.pallas_call(
        body, name=name, out_shape=tuple(jax.ShapeDtypeStruct(h.shape, h.dtype) for h in hs),
        in_specs=[ANY] * n, out_specs=tuple([ANY] * n),
        scratch_shapes=[pltpu.SemaphoreType.DMA((3 * n,)), pltpu.SemaphoreType.DMA((3 * n,)),
                        pltpu.SemaphoreType.DMA((n,))],
    )(*hs)


def _add_halves(g, recv, core, name):
    _, r, c = recv.shape
    br = _row_block(r, 512)
    stacked = g.ndim == 4

    def body(core_ref, g_ref, r_ref, o_ref):
        o_ref[0] = ((g_ref[0, 0] if stacked else g_ref[...]) + r_ref[0]).astype(BF16)

    spec = pl.BlockSpec((1, br, c), lambda i, j, core_ref: (i, j, 0))
    if stacked:
        g_spec = pl.BlockSpec((1, 1, br, c), lambda i, j, core_ref: (i, core_ref[0], j, 0))
    else:
        g_spec = pl.BlockSpec((br, c), lambda i, j, core_ref: (j, 2 * i + core_ref[0]))
    return pl.pallas_call(
        body, name=name, out_shape=jax.ShapeDtypeStruct(recv.shape, BF16),
        grid_spec=pltpu.PrefetchScalarGridSpec(
            num_scalar_prefetch=1, grid=(4, r // br), in_specs=[g_spec, spec], out_specs=spec),
        compiler_params=_params(32),
    )(core, g, recv)


def _adamw(parts, w, m, v, name):
    n_parts, r, c = parts.shape
    if r % 8 == 0:
        br, bc = _row_block(r, 512 if c <= 1024 else 256), c
    else:
        br, bc = r, (256 if c % 256 == 0 else c)

    def body(p_ref, w_ref, m_ref, v_ref, g_out, d_out, m_out, v_out):
        g = p_ref[0].astype(F32)
        for p in range(1, n_parts):
            g = g + p_ref[p].astype(F32)
        m2 = B1 * m_ref[...] + (1.0 - B1) * g
        v2 = B2 * v_ref[...] + (1.0 - B2) * (g * g)
        m_hat = m2 / (1.0 - B1 ** STEP)
        v_hat = v2 / (1.0 - B2 ** STEP)
        g_out[...] = g
        d_out[...] = -LR * (m_hat / (jnp.sqrt(v_hat) + AEPS) + WD * w_ref[...])
        m_out[...] = m2
        v_out[...] = v2

    spec = pl.BlockSpec((br, bc), lambda i, j: (i, j))
    out = jax.ShapeDtypeStruct((r, c), F32)
    return pl.pallas_call(
        body, name=name, out_shape=(out, out, out, out), grid=(r // br, c // bc),
        in_specs=[pl.BlockSpec((n_parts, br, bc), lambda i, j: (0, i, j)), spec, spec, spec],
        out_specs=(spec, spec, spec, spec), compiler_params=_params(40),
    )(parts, w, m, v)


def _atb(a, b, name):
    t, k1 = a.shape
    k2 = b.shape[1]
    bt = math.gcd(t, 2048)

    def pick(k):
        for cand in (1024, 768, 512, 384, 256, 128):
            if k % cand == 0:
                return cand
        return k

    b1, b2 = pick(k1), pick(k2)

    def body(a_ref, b_ref, o_ref):
        @pl.when(pl.program_id(2) == 0)
        def _():
            o_ref[...] = jnp.zeros_like(o_ref)
        o_ref[...] += _mm_tn(a_ref[...], b_ref[...])

    return pl.pallas_call(
        body, name=name, out_shape=jax.ShapeDtypeStruct((k1, k2), F32), grid=(k1 // b1, k2 // b2, t // bt),
        in_specs=[pl.BlockSpec((bt, b1), lambda i, j, k: (k, i)), pl.BlockSpec((bt, b2), lambda i, j, k: (k, j))],
        out_specs=pl.BlockSpec((b1, b2), lambda i, j, k: (i, j)), compiler_params=_params(48),
    )(a, b)


def _mod_fwd(c_all, w_ada, b_cols):
    def body(c_ref, w_ref, b_ref, o_ref):
        cc = c_ref[...]
        cond = cc * _sigmoid(cc)
        o_ref[...] = _mm(cond, w_ref[...]) + b_ref[...]

    return pl.pallas_call(body, name="mod_fwd", out_shape=jax.ShapeDtypeStruct((c_all.shape[0], w_ada.shape[1]), F32),
                          compiler_params=_params(32))(c_all, w_ada, b_cols)


def _mod_bwd(c_all, dmod_cols, dmod_all):
    def body(c_ref, dc_ref, da_ref, gw_ref, gb_ref):
        cc = c_ref[...]
        cond = cc * _sigmoid(cc)
        gw_ref[...] = _mm_tn(cond, dc_ref[...])
        gb_ref[...] = jnp.sum(da_ref[...], axis=0, keepdims=True)

    return pl.pallas_call(
        body, name="mod_bwd",
        out_shape=(jax.ShapeDtypeStruct((D_MODEL, dmod_cols.shape[1]), F32), jax.ShapeDtypeStruct((1, dmod_all.shape[1]), F32)),
        compiler_params=_params(32))(c_all, dmod_cols, dmod_all)


def _load_once(hbm_ref, vmem_ref, sem):
    @pl.when(pl.program_id(0) == 0)
    def _():
        cp = pltpu.make_async_copy(hbm_ref, vmem_ref, sem)
        cp.start()
        cp.wait()


def _conv_taps(win_ref, w, tb, cols):
    shifted = [win_ref[8 - j:8 - j + tb, cols] for j in range(4)]
    acc = w[3:4] * shifted[0]
    for j in (1, 2, 3):
        acc = acc + w[3 - j:4 - j] * shifted[j]
    return acc, shifted


def _proj_conv_fwd(x2, mod3, w_in_pad, conv_w, conv_b, seq):
    t = x2.shape[0]
    tb = 256
    npb = seq // tb
    cw = 512

    def body(x_ref, mod_ref, w_hbm, cw_ref, cb_ref, z_ref, pre_ref, xbc_ref, dsilu_ref, dt_ref, u5_ref, w_vmem, win, sem):
        _load_once(w_hbm, w_vmem, sem)
        first = (pl.program_id(0) % npb) == 0

        @pl.when(first)
        def _():
            win[0:8, :] = jnp.zeros((8, D_XBC), F32)

        @pl.when(jnp.logical_not(first))
        def _():
            win[0:8, :] = win[tb:tb + 8, :]

        m = mod_ref[0]
        u = (x_ref[...] * (1.0 + m[1:2]) + m[0:1]).astype(BF16)
        z_ref[...] = lax.dot_general(u, w_vmem[0:D_SSD, :], NT, preferred_element_type=F32)
        dt_ref[...] = lax.dot_general(u, w_vmem[D_SSD + D_XBC:D_SSD + D_XBC + DT_PAD, :], NT,
                                      preferred_element_type=F32)
        u5_ref[...] = lax.dot_general(u, w_vmem[D_SSD + D_XBC + DT_PAD:, :], NT, preferred_element_type=F32)
        for k in range(D_XBC // cw):
            cols = slice(k * cw, (k + 1) * cw)
            pre_k = lax.dot_general(u, w_vmem[D_SSD + k * cw:D_SSD + (k + 1) * cw, :], NT,
                                    preferred_element_type=F32)
            win[8:8 + tb, cols] = pre_k
            pre_ref[:, cols] = pre_k.astype(BF16)
            conv, _ = _conv_taps(win, cw_ref[:, cols], tb, cols)
            conv = conv + cb_ref[:, cols]
            sg = _sigmoid(conv)
            xbc_ref[:, cols] = conv * sg
            dsilu_ref[:, cols] = (sg * (1.0 + conv * (1.0 - sg))).astype(BF16)

    row = lambda w: pl.BlockSpec((tb, w), lambda i: (i, 0))
    return pl.pallas_call(
        body, name="proj_conv_fwd", grid=(t // tb,),
        out_shape=(jax.ShapeDtypeStruct((t, D_SSD), F32), jax.ShapeDtypeStruct((t, D_XBC), BF16),
                   jax.ShapeDtypeStruct((t, D_XBC), F32), jax.ShapeDtypeStruct((t, D_XBC), BF16),
                   jax.ShapeDtypeStruct((t, DT_PAD), F32), jax.ShapeDtypeStruct((t, D_S5), F32)),
        in_specs=[row(D_MODEL), pl.BlockSpec((1, N_MOD, D_MODEL), lambda i: (i // npb, 0, 0)), ANY,
                  pl.BlockSpec((4, D_XBC), lambda i: (0, 0)), pl.BlockSpec((1, D_XBC), lambda i: (0, 0))],
        out_specs=(row(D_SSD), row(D_XBC), row(D_XBC), row(D_XBC), row(DT_PAD), row(D_S5)),
        scratch_shapes=[pltpu.VMEM((D_INP, D_MODEL), BF16), pltpu.VMEM((tb + 8, D_XBC), F32), pltpu.SemaphoreType.DMA],
        compiler_params=_params(56),
    )(x2, mod3, w_in_pad, conv_w, conv_b)


N_PAIRS = N_HEADS // 2


def _split3(x):
    hi = x.astype(BF16)
    r = x - hi.astype(F32)
    mid = r.astype(BF16)
    lo = (r - mid.astype(F32)).astype(BF16)
    return hi, mid, lo


def _dot3(x, e, dims=(((1,), (0,)), ((), ()))):
    return sum(lax.dot_general(p, e, dims, preferred_element_type=F32) for p in _split3(x))


def _dot3_left(e, x, dims=(((1,), (0,)), ((), ()))):
    return sum(lax.dot_general(e, p, dims, preferred_element_type=F32) for p in _split3(x))


def _head_fold():
    return (jnp.arange(D_SSD)[:, None] // HEADDIM == jnp.arange(128)[None, :]).astype(BF16)


def _ssd_prep(dt_raw, par):
    dtb = par[0:1]
    a = -jnp.exp(par[1:2])
    dt = _softplus(dt_raw + dtb)
    adt = dt * a
    row = lax.broadcasted_iota(jnp.int32, (CHUNK, CHUNK), 0)
    col = lax.broadcasted_iota(jnp.int32, (CHUNK, CHUNK), 1)
    causal = row >= col
    tri = causal.astype(BF16)
    cs = _dot3_left(tri, adt)
    left = col < HEADDIM

    def lanes(v, h):
        return jnp.broadcast_to(v[:, h:h + 1], (CHUNK, 128))

    dt_c, cs_c, pair_cols = [], [], []
    for p in range(N_PAIRS):
        c0, c1 = lanes(cs, 2 * p), lanes(cs, 2 * p + 1)
        pair_cols.append(jnp.concatenate([c0, c1], axis=1))
        cs_c.append(jnp.where(left, c0, c1))
        dt_c.append(jnp.where(left, lanes(dt, 2 * p), lanes(dt, 2 * p + 1)))
    cs_c = jnp.concatenate(cs_c, axis=1)
    dt_c = jnp.concatenate(dt_c, axis=1)
    return dt, a, cs, cs.T, causal, tri, dt_c, jnp.exp(cs_c), jnp.exp(cs_c[CHUNK - 1:CHUNK, :] - cs_c), pair_cols


def _pair_decay(cols, cst, pair, causal2):
    rows = jnp.concatenate([jnp.broadcast_to(cst[2 * pair:2 * pair + 1, :], (CHUNK, CHUNK)),
                            jnp.broadcast_to(cst[2 * pair + 1:2 * pair + 2, :], (CHUNK, CHUNK))], axis=1)
    return jnp.exp(jnp.where(causal2, cols - rows, -jnp.inf))


def _stack_heads(xp, left):
    return jnp.concatenate([jnp.where(left, xp, 0.0), jnp.where(left, 0.0, xp)], axis=0).astype(BF16)


def _ssd_fwd(xbc, z, dt_raw, par, dsk, normw, seq):
    t = xbc.shape[0]
    nc = seq // CHUNK
    n_chunks = t // CHUNK

    def body(xbc_ref, z_ref, dt_ref, par_ref, dsk_ref, nw_ref, yraw_ref, ycat_ref, hprev_ref, h_ref):
        @pl.when(pl.program_id(0) % nc == 0)
        def _():
            h_ref[...] = jnp.zeros_like(h_ref)
        hprev_ref[0] = h_ref[...]
        _, _, cs, cst, causal, _, dt_c, ecs_c, w_c, pair_cols = _ssd_prep(dt_ref[...], par_ref[...])
        cs_last = cs[CHUNK - 1:CHUNK, :]
        causal2 = jnp.concatenate([causal, causal], axis=1)
        left = lax.broadcasted_iota(jnp.int32, (CHUNK, 128), 1) < HEADDIM
        x = xbc_ref[:, 0:D_SSD]
        xdt = x * dt_c
        amat = (w_c * xdt).astype(BF16)
        zz = z_ref[...]
        silu_z = zz * _sigmoid(zz)
        for g in range(N_GROUPS):
            gs = slice(g * GW, (g + 1) * GW)
            bg = xbc_ref[:, D_SSD + g * N_STATE:D_SSD + (g + 1) * N_STATE].astype(BF16)
            cg = xbc_ref[:, D_SSD + (N_GROUPS + g) * N_STATE:D_SSD + (N_GROUPS + g + 1) * N_STATE].astype(BF16)
            scores = lax.dot_general(cg, bg, NT, preferred_element_type=F32)
            scores2 = jnp.concatenate([scores, scores], axis=1)
            hg = h_ref[gs, :]
            p_all = lax.dot_general(cg, hg.astype(BF16), NT, preferred_element_type=F32)
            ys = []
            for q in range(GW // 128):
                pair = g * (GW // 128) + q
                decay = _pair_decay(pair_cols[pair], cst, pair, causal2)
                mcat = (scores2 * decay).astype(BF16)
                ys.append(jnp.dot(mcat, _stack_heads(xdt[:, pair * 128:(pair + 1) * 128], left),
                                  preferred_element_type=F32))
            yg = jnp.concatenate(ys, axis=1) + ecs_c[:, gs] * p_all + x[:, gs] * dsk_ref[:, gs]
            s_new = lax.dot_general(amat[:, gs], bg, TN, preferred_element_type=F32)
            for j in range(HPG):
                hh = g * HPG + j
                js = slice(j * HEADDIM, (j + 1) * HEADDIM)
                h_ref[g * GW + j * HEADDIM:g * GW + (j + 1) * HEADDIM, :] = (
                    hg[js, :] * jnp.exp(cs_last[:, hh:hh + 1]) + s_new[js, :])
            yraw_ref[:, gs] = yg
            v = yg * silu_z[:, gs]
            r = lax.rsqrt(jnp.mean(v * v, axis=-1, keepdims=True) + EPS)
            ycat_ref[:, gs] = (v * r * nw_ref[:, gs]).astype(BF16)

    row = lambda w: pl.BlockSpec((CHUNK, w), lambda i: (i, 0))
    full = lambda s: pl.BlockSpec(s, lambda i: (0,) * len(s))
    return pl.pallas_call(
        body, name="ssd_fwd", grid=(n_chunks,),
        out_shape=(jax.ShapeDtypeStruct((t, D_SSD), F32), jax.ShapeDtypeStruct((t, D_SSD + D_S5), BF16),
                   jax.ShapeDtypeStruct((n_chunks, D_SSD, N_STATE), F32)),
        in_specs=[row(D_XBC), row(D_SSD), row(DT_PAD), full((8, 128)), full((1, D_SSD)), full((1, D_SSD))],
        out_specs=(row(D_SSD), row(D_SSD), pl.BlockSpec((1, D_SSD, N_STATE), lambda i: (i, 0, 0))),
        scratch_shapes=[pltpu.VMEM((D_SSD, N_STATE), F32)],
        compiler_params=_params(40),
    )(xbc, z, dt_raw, par, dsk, normw)


S5_CW = 512
S5_BLOCKS = 4


def _tile_scan(in_re, in_im, out_re, out_im, carry_re, carry_im, pw_re, pw_im, n_tiles, reverse):
    steps = (1, 2, 4)
    for cc in range(S5_N // S5_CW):
        cols = slice(cc * S5_CW, (cc + 1) * S5_CW)
        a_re, a_im = pw_re[:, cols], pw_im[:, cols]
        rid = lax.broadcasted_iota(jnp.int32, (8, S5_CW), 0)
        pows = []
        for d in steps:
            k = 8 - d if reverse else d - 1
            keep = (rid < 8 - d) if reverse else (rid >= d)
            pows.append((jnp.where(keep, pw_re[k:k + 1, cols], 0.0), jnp.where(keep, pw_im[k:k + 1, cols], 0.0)))

        def tile(i, carry, cols=cols, pows=pows, a_re=a_re, a_im=a_im):
            r = (n_tiles - 1 - i) if reverse else i
            rows = pl.ds(pl.multiple_of(r * 8, 8), 8)
            xr, xi = in_re[rows, cols], in_im[rows, cols]
            for (pr, pi), d in zip(pows, steps):
                shift = 8 - d if reverse else d
                sr, si = pltpu.roll(xr, shift, axis=0), pltpu.roll(xi, shift, axis=0)
                xr, xi = xr + pr * sr - pi * si, xi + pr * si + pi * sr
            cr, ci = carry
            xr, xi = xr + a_re * cr - a_im * ci, xi + a_re * ci + a_im * cr
            out_re[rows, cols] = xr
            out_im[rows, cols] = xi
            edge = slice(0, 1) if reverse else slice(7, 8)
            return (jnp.broadcast_to(xr[edge], (8, S5_CW)), jnp.broadcast_to(xi[edge], (8, S5_CW)))

        c0 = (jnp.broadcast_to(carry_re[0:1, cols], (8, S5_CW)), jnp.broadcast_to(carry_im[0:1, cols], (8, S5_CW)))
        cr, ci = lax.fori_loop(0, n_tiles, tile, c0, unroll=True)
        carry_re[:, cols] = cr
        carry_im[:, cols] = ci


def _s5_params_math(ar, ai, ldt, br, bi):
    dt = jnp.exp(ldt)
    mag = jnp.exp(ar * dt)
    ang = ai * dt
    ab_re = mag * jnp.cos(ang)
    ab_im = mag * jnp.sin(ang)
    den = ar * ar + ai * ai
    n_re = ab_re - 1.0
    coef_re = (n_re * ar + ab_im * ai) / den
    coef_im = (ab_im * ar - n_re * ai) / den
    bb_re = coef_re * br - coef_im * bi
    bb_im = coef_re * bi + coef_im * br
    return ab_re, ab_im, bb_re, bb_im


def _s5_params_fwd(ar, ai, ldt, br, bi):
    def body(ar_ref, ai_ref, ldt_ref, br_ref, bi_ref, bbr_ref, bbi_ref, pfr_ref, pfi_ref, prr_ref, pri_ref):
        ab_re, ab_im, bb_re, bb_im = _s5_params_math(ar_ref[...], ai_ref[...], ldt_ref[...], br_ref[...], bi_ref[...])
        bbr_ref[...] = bb_re
        bbi_ref[...] = bb_im
        pr, pi = ab_re, ab_im
        for k in range(8):
            pfr_ref[k:k + 1, :] = pr
            pfi_ref[k:k + 1, :] = pi
            prr_ref[7 - k:8 - k, :] = pr
            pri_ref[7 - k:8 - k, :] = -pi
            pr, pi = pr * ab_re - pi * ab_im, pr * ab_im + pi * ab_re

    b16 = jax.ShapeDtypeStruct((S5_CH, S5_N), F32)
    p8 = jax.ShapeDtypeStruct((8, S5_N), F32)
    return pl.pallas_call(body, name="s5_params_fwd", out_shape=(b16, b16, p8, p8, p8, p8),
                          compiler_params=_params(32))(ar, ai, ldt, br, bi)


def _s5_params_bwd(ar, ai, ldt, br, bi, d_ab_re, d_ab_im, d_bb_re, d_bb_im):
    def body(ar_ref, ai_ref, ldt_ref, br_ref, bi_ref, dar_ref, dai_ref, dbr_ref, dbi_ref,
             gar_ref, gai_ref, gldt_ref, gbr_ref, gbi_ref):
        _, vjp = jax.vjp(_s5_params_math, ar_ref[...], ai_ref[...], ldt_ref[...], br_ref[...], bi_ref[...])
        g_ar, g_ai, g_ldt, g_br, g_bi = vjp((dar_ref[...], dai_ref[...], dbr_ref[...], dbi_ref[...]))
        gar_ref[...] = g_ar
        gai_ref[...] = g_ai
        gbr_ref[...] = g_br
        gbi_ref[...] = g_bi
        lane = lax.broadcasted_iota(jnp.int32, (S5_N, 128), 0) // S5_P
        grp = lax.broadcasted_iota(jnp.int32, (S5_N, 128), 1)
        fold = (lane == grp).astype(F32)
        gldt_ref[...] = jnp.dot(g_ldt, fold, preferred_element_type=F32, precision=HIGHEST)

    v1 = jax.ShapeDtypeStruct((1, S5_N), F32)
    b16 = jax.ShapeDtypeStruct((S5_CH, S5_N), F32)
    return pl.pallas_call(body, name="s5_params_bwd",
                          out_shape=(v1, v1, jax.ShapeDtypeStruct((1, 128), F32), b16, b16),
                          compiler_params=_params(32))(ar, ai, ldt, br, bi, d_ab_re, d_ab_im, d_bb_re, d_bb_im)


def _s5_fwd(u5, bb_re, bb_im, cc_re, cc_im, pf_re, pf_im, s5d, w_glu, b_glu, ycat, seq):
    t = u5.shape[0]
    tb = 256
    npb = seq // tb

    def body(u_ref, bbr_ref, bbi_ref, ccr_ref, cci_ref, pfr_ref, pfi_ref, d_ref, wg_ref, bg_ref, ycat_hbm,
             sre_ref, sim_ref, ypre_ref, y5_ref, bur, bui, car, cai):
        del ycat_hbm

        @pl.when(pl.program_id(0) % npb == 0)
        def _():
            car[...] = jnp.zeros_like(car)
            cai[...] = jnp.zeros_like(cai)
        u = u_ref[...]
        ub = u.astype(BF16)
        for j in range(S5_BLOCKS):
            ch, st = slice(j * 128, (j + 1) * 128), slice(j * 512, (j + 1) * 512)
            bur[:, st] = jnp.dot(ub[:, ch], bbr_ref[j], preferred_element_type=F32)
            bui[:, st] = jnp.dot(ub[:, ch], bbi_ref[j], preferred_element_type=F32)
        _tile_scan(bur, bui, sre_ref, sim_ref, car, cai, pfr_ref, pfi_ref, tb // 8, reverse=False)
        cs_y = []
        for j in range(S5_BLOCKS):
            st = slice(j * 512, (j + 1) * 512)
            cs_y.append(_mm(sre_ref[:, st], ccr_ref[j]) - _mm(sim_ref[:, st], cci_ref[j]))
        ypre = jnp.concatenate(cs_y, axis=1) + u * d_ref[...]
        ypre_ref[...] = ypre
        yg = _gelu(ypre)
        y5_ref[...] = (yg * _sigmoid(_mm(yg, wg_ref[...]) + bg_ref[...])).astype(BF16)

    row = lambda w: pl.BlockSpec((tb, w), lambda i: (i, 0))
    full = lambda a: pl.BlockSpec(a.shape, lambda i: (0,) * a.ndim)
    return pl.pallas_call(
        body, name="s5_fwd", grid=(t // tb,),
        out_shape=(jax.ShapeDtypeStruct((t, S5_N), F32), jax.ShapeDtypeStruct((t, S5_N), F32),
                   jax.ShapeDtypeStruct((t, D_S5), F32), jax.ShapeDtypeStruct(ycat.shape, BF16)),
        in_specs=[row(D_S5), full(bb_re), full(bb_im), full(cc_re), full(cc_im), full(pf_re), full(pf_im),
                  full(s5d), full(w_glu), full(b_glu), ANY],
        out_specs=(row(S5_N), row(S5_N), row(D_S5), pl.BlockSpec((tb, D_S5), lambda i: (i, D_SSD // D_S5))),
        input_output_aliases={10: 3},
        scratch_shapes=[pltpu.VMEM((tb, S5_N), F32), pltpu.VMEM((tb, S5_N), F32),
                        pltpu.VMEM((8, S5_N), F32), pltpu.VMEM((8, S5_N), F32)],
        compiler_params=_params(48),
    )(u5, bb_re, bb_im, cc_re, cc_im, pf_re, pf_im, s5d, w_glu, b_glu, ycat)


def _layer_norm(r, g, b):
    mu = jnp.mean(r, axis=-1, keepdims=True)
    xc = r - mu
    rstd = lax.rsqrt(jnp.mean(xc * xc, axis=-1, keepdims=True) + EPS)
    xhat = xc * rstd
    return xhat * g + b, xhat, rstd


def _layer_norm_bwd(dy, xhat, rstd, g):
    dxhat = dy * g
    return rstd * (dxhat - jnp.mean(dxhat, axis=-1, keepdims=True)
                   - xhat * jnp.mean(dxhat * xhat, axis=-1, keepdims=True))


def _out_ln1(ycat, x2, mod3, w_out, ln1, seq):
    t = x2.shape[0]
    tb = 512
    npb = seq // tb

    def body(y_ref, x_ref, mod_ref, w_ref, ln_ref, mix_ref, x1_ref):
        m = mod_ref[0]
        mix = jnp.dot(y_ref[...], w_ref[...], preferred_element_type=F32)
        mix_ref[...] = mix
        r1 = ALPHA * x_ref[...] + (1.0 + m[2:3]) * mix
        x1_ref[...] = _layer_norm(r1, ln_ref[0:1], ln_ref[1:2])[0]

    row = lambda w: pl.BlockSpec((tb, w), lambda i: (i, 0))
    return pl.pallas_call(
        body, name="out_ln1", grid=(t // tb,),
        out_shape=(jax.ShapeDtypeStruct((t, D_MODEL), F32), jax.ShapeDtypeStruct((t, D_MODEL), F32)),
        in_specs=[row(D_SSD + D_S5), row(D_MODEL), pl.BlockSpec((1, N_MOD, D_MODEL), lambda i: (i // npb, 0, 0)),
                  pl.BlockSpec(w_out.shape, lambda i: (0, 0)), pl.BlockSpec(ln1.shape, lambda i: (0, 0))],
        out_specs=(row(D_MODEL), row(D_MODEL)), compiler_params=_params(48),
    )(ycat, x2, mod3, w_out, ln1)


def _mlp_fwd_bwd(x1, tgt, mod3, w1, w2, vec1, b1, seq):
    t = x1.shape[0]
    tb = 256
    npb = seq // tb
    n_fb, _, fb = w1.shape

    def body(x1_ref, tgt_ref, mod_ref, w1_hbm, w2_hbm, v_ref, b1_ref,
             dx1_ref, u2_ref, h_ref, dhp_ref, do_ref, gacc_ref, db1_ref, bacc_ref, w1_v, w2_v, sem1, sem2):
        i = pl.program_id(0)
        @pl.when(i == 0)
        def _():
            cps = [pltpu.make_async_copy(w1_hbm.at[k], w1_v.at[:, k * fb:(k + 1) * fb], sem1.at[k])
                   for k in range(n_fb)]
            for cp in cps:
                cp.start()
            for cp in cps:
                cp.wait()
        _load_once(w2_hbm, w2_v, sem2)

        @pl.when(i == 0)
        def _():
            gacc_ref[...] = jnp.zeros_like(gacc_ref)
            db1_ref[...] = jnp.zeros_like(db1_ref)

        @pl.when(i % npb == 0)
        def _():
            bacc_ref[...] = jnp.zeros_like(bacc_ref)

        m = mod_ref[0]
        sh2, sc2, g2 = m[3:4], m[4:5], m[5:6]
        x1v = x1_ref[...]
        u2 = (x1v * (1.0 + sc2) + sh2).astype(BF16)
        u2_ref[...] = u2
        hr = jnp.maximum(jnp.dot(u2, w1_v[...], preferred_element_type=F32) + b1_ref[...], 0.0)
        hb = (hr * hr).astype(BF16)
        h_ref[...] = hb
        o = jnp.dot(hb, w2_v[...], preferred_element_type=F32) + v_ref[0:1]
        r2 = ALPHA * x1v + (1.0 + g2) * o
        y, xhat, rstd = _layer_norm(r2, v_ref[1:2], v_ref[2:3])
        err = y - tgt_ref[...]
        dy = err * (1.0 / D_MODEL)
        dr2 = _layer_norm_bwd(dy, xhat, rstd, v_ref[1:2])
        do = (1.0 + g2) * dr2
        dob = do.astype(BF16)
        do_ref[...] = dob
        gacc_ref[0:1, :] += jnp.sum(dy * xhat, axis=0, keepdims=True)
        gacc_ref[1:2, :] += jnp.sum(dy, axis=0, keepdims=True)
        gacc_ref[2:3, :] += jnp.sum(do, axis=0, keepdims=True)
        gacc_ref[3:4, :] += jnp.sum(err * err, axis=0, keepdims=True)
        dhpre = lax.dot_general(dob, w2_v[...], NT, preferred_element_type=F32) * (2.0 * hr)
        dhpb = dhpre.astype(BF16)
        dhp_ref[...] = dhpb
        db1_ref[...] += jnp.sum(dhpre, axis=0, keepdims=True)
        du2 = lax.dot_general(dhpb, w1_v[...], NT, preferred_element_type=F32)
        dx1_ref[...] = ALPHA * dr2 + du2 * (1.0 + sc2)
        bacc_ref[0, 0:1, :] += jnp.sum(du2, axis=0, keepdims=True)
        bacc_ref[0, 1:2, :] += jnp.sum(du2 * x1v, axis=0, keepdims=True)
        bacc_ref[0, 2:3, :] += jnp.sum(dr2 * o, axis=0, keepdims=True)

    row = lambda w: pl.BlockSpec((tb, w), lambda i: (i, 0))
    return pl.pallas_call(
        body, name="mlp_fwd_bwd", grid=(t // tb,),
        out_shape=(jax.ShapeDtypeStruct((t, D_MODEL), F32), jax.ShapeDtypeStruct((t, D_MODEL), BF16),
                   jax.ShapeDtypeStruct((t, D_FF), BF16), jax.ShapeDtypeStruct((t, D_FF), BF16),
                   jax.ShapeDtypeStruct((t, D_MODEL), BF16), jax.ShapeDtypeStruct((8, D_MODEL), F32),
                   jax.ShapeDtypeStruct((1, D_FF), F32), jax.ShapeDtypeStruct((t // seq, 8, D_MODEL), F32)),
        in_specs=[row(D_MODEL), row(D_MODEL), pl.BlockSpec((1, N_MOD, D_MODEL), lambda i: (i // npb, 0, 0)), ANY, ANY,
                  pl.BlockSpec(vec1.shape, lambda i: (0, 0)), pl.BlockSpec(b1.shape, lambda i: (0, 0))],
        out_specs=(row(D_MODEL), row(D_MODEL), row(D_FF), row(D_FF), row(D_MODEL),
                   pl.BlockSpec((8, D_MODEL), lambda i: (0, 0)), pl.BlockSpec((1, D_FF), lambda i: (0, 0)),
                   pl.BlockSpec((1, 8, D_MODEL), lambda i: (i // npb, 0, 0))),
        scratch_shapes=[pltpu.VMEM((D_MODEL, n_fb * fb), BF16), pltpu.VMEM((D_FF, D_MODEL), BF16),
                        pltpu.SemaphoreType.DMA((n_fb,)), pltpu.SemaphoreType.DMA],
        compiler_params=_params(60),
    )(x1, tgt, mod3, w1, w2, vec1, b1)


def _ln1_out_bwd(dx1, x2, mix, mod3, w_out, ln1, seq):
    t = x2.shape[0]
    tb = 512
    npb = seq // tb

    def body(dx1_ref, x_ref, mix_ref, mod_ref, w_ref, ln_ref, dmix_ref, dxa_ref, dys_ref, dy5_ref, gacc_ref, bacc_ref):
        i = pl.program_id(0)

        @pl.when(i == 0)
        def _():
            gacc_ref[...] = jnp.zeros_like(gacc_ref)

        @pl.when(i % npb == 0)
        def _():
            bacc_ref[...] = jnp.zeros_like(bacc_ref)

        m = mod_ref[0]
        mix = mix_ref[...]
        r1 = ALPHA * x_ref[...] + (1.0 + m[2:3]) * mix
        _, xhat, rstd = _layer_norm(r1, ln_ref[0:1], ln_ref[1:2])
        dx1v = dx1_ref[...]
        dr1 = _layer_norm_bwd(dx1v, xhat, rstd, ln_ref[0:1])
        gacc_ref[0:1, :] += jnp.sum(dx1v * xhat, axis=0, keepdims=True)
        gacc_ref[1:2, :] += jnp.sum(dx1v, axis=0, keepdims=True)
        bacc_ref[0, 0:1, :] += jnp.sum(dr1 * mix, axis=0, keepdims=True)
        dmix = ((1.0 + m[2:3]) * dr1).astype(BF16)
        dmix_ref[...] = dmix
        dxa_ref[...] = ALPHA * dr1
        dys_ref[...] = lax.dot_general(dmix, w_ref[0:D_SSD, :], NT, preferred_element_type=F32)
        dy5_ref[...] = lax.dot_general(dmix, w_ref[D_SSD:, :], NT, preferred_element_type=F32)

    row = lambda w: pl.BlockSpec((tb, w), lambda i: (i, 0))
    return pl.pallas_call(
        body, name="ln1_out_bwd", grid=(t // tb,),
        out_shape=(jax.ShapeDtypeStruct((t, D_MODEL), BF16), jax.ShapeDtypeStruct((t, D_MODEL), F32),
                   jax.ShapeDtypeStruct((t, D_SSD), F32), jax.ShapeDtypeStruct((t, D_S5), F32),
                   jax.ShapeDtypeStruct((8, D_MODEL), F32), jax.ShapeDtypeStruct((t // seq, 8, D_MODEL), F32)),
        in_specs=[row(D_MODEL), row(D_MODEL), row(D_MODEL), pl.BlockSpec((1, N_MOD, D_MODEL), lambda i: (i // npb, 0, 0)),
                  pl.BlockSpec(w_out.shape, lambda i: (0, 0)), pl.BlockSpec(ln1.shape, lambda i: (0, 0))],
        out_specs=(row(D_MODEL), row(D_MODEL), row(D_SSD), row(D_S5), pl.BlockSpec((8, D_MODEL), lambda i: (0, 0)),
                   pl.BlockSpec((1, 8, D_MODEL), lambda i: (i // npb, 0, 0))),
        compiler_params=_params(48),
    )(dx1, x2, mix, mod3, w_out, ln1)


def _s5_bwd(dy5, ypre, u5, s_re, s_im, bb_re, bb_im, cc_re, cc_im, pr_re, pr_im, s5d, w_glu, b_glu, seq):
    t = u5.shape[0]
    tb = 256
    npb = seq // tb
    n_blocks = t // tb

    def blk(i):
        return (i // npb) * npb + (npb - 1 - i % npb)

    def body(dy_ref, ypre_ref, u_ref, sre_ref, sim_ref, hre_ref, him_ref, bbr_ref, bbi_ref, ccr_ref, cci_ref,
             prr_ref, pri_ref, d_ref, wg_ref, bg_ref,
             du_ref, vacc_ref, sacc_ref, dcc_ref, dbb_ref, dwg_ref, dsr, dsi, gr, gi, car, cai):
        i = pl.program_id(0)

        @pl.when(i == 0)
        def _():
            for acc in (vacc_ref, sacc_ref, dcc_ref, dbb_ref, dwg_ref):
                acc[...] = jnp.zeros_like(acc)

        @pl.when(i % npb == 0)
        def _():
            car[...] = jnp.zeros_like(car)
            cai[...] = jnp.zeros_like(cai)

        dy = dy_ref[...]
        ypre = ypre_ref[...]
        u = u_ref[...]
        ub = u.astype(BF16)
        yg = _gelu(ypre)
        sg = _sigmoid(_mm(yg, wg_ref[...]) + bg_ref[...])
        dq = dy * yg * sg * (1.0 - sg)
        dqb = dq.astype(BF16)
        dyg = dy * sg + lax.dot_general(dqb, wg_ref[...], NT, preferred_element_type=F32)
        dyp = dyg * _gelu_grad(ypre)
        dypb = dyp.astype(BF16)
        dwg_ref[...] += lax.dot_general(yg.astype(BF16), dqb, TN, preferred_element_type=F32)
        blocks = [(slice(j * 128, (j + 1) * 128), slice(j * 512, (j + 1) * 512)) for j in range(S5_BLOCKS)]
        for j, (ch, st) in enumerate(blocks):
            dsr[:, st] = lax.dot_general(dypb[:, ch], ccr_ref[j], NT, preferred_element_type=F32)
            dsi[:, st] = -lax.dot_general(dypb[:, ch], cci_ref[j], NT, preferred_element_type=F32)
        _tile_scan(dsr, dsi, gr, gi, car, cai, prr_ref, pri_ref, tb // 8, reverse=True)
        g_re, g_im = gr[...], gi[...]
        first_rows = (i % npb) == npb - 1
        hre = jnp.where(first_rows, 0.0, hre_ref[...])
        him = jnp.where(first_rows, 0.0, him_ref[...])
        s_re_v, s_im_v = sre_ref[...], sim_ref[...]
        sp_re = pltpu.roll(jnp.concatenate([hre, s_re_v], axis=0), 1, axis=0)[8:8 + tb]
        sp_im = pltpu.roll(jnp.concatenate([him, s_im_v], axis=0), 1, axis=0)[8:8 + tb]
        vacc_ref[0:1, :] += jnp.sum(g_re * sp_re + g_im * sp_im, axis=0, keepdims=True)
        vacc_ref[1:2, :] += jnp.sum(g_im * sp_re - g_re * sp_im, axis=0, keepdims=True)
        grb, gib = g_re.astype(BF16), g_im.astype(BF16)
        srb, sib = s_re_v.astype(BF16), s_im_v.astype(BF16)
        du_cols = []
        for j, (ch, st) in enumerate(blocks):
            dcc_ref[j] += lax.dot_general(srb[:, st], dypb[:, ch], TN, preferred_element_type=F32)
            dcc_ref[S5_BLOCKS + j] -= lax.dot_general(sib[:, st], dypb[:, ch], TN, preferred_element_type=F32)
            dbb_ref[j] += lax.dot_general(ub[:, ch], grb[:, st], TN, preferred_element_type=F32)
            dbb_ref[S5_BLOCKS + j] += lax.dot_general(ub[:, ch], gib[:, st], TN, preferred_element_type=F32)
            du_cols.append(lax.dot_general(grb[:, st], bbr_ref[j], NT, preferred_element_type=F32)
                           + lax.dot_general(gib[:, st], bbi_ref[j], NT, preferred_element_type=F32))
        du_ref[...] = jnp.concatenate(du_cols, axis=1) + dyp * d_ref[...]
        sacc_ref[0:1, :] += jnp.sum(dyp * u, axis=0, keepdims=True)
        sacc_ref[1:2, :] += jnp.sum(dq, axis=0, keepdims=True)

    row = lambda w: pl.BlockSpec((tb, w), lambda i: (blk(i), 0))
    halo = pl.BlockSpec((8, S5_N), lambda i: (jnp.maximum(blk(i) * (tb // 8) - 1, 0), 0))
    full = lambda a: pl.BlockSpec(a.shape, lambda i: (0,) * a.ndim)
    acc = lambda s: pl.BlockSpec(s, lambda i: (0,) * len(s))
    acc_shapes = [(8, S5_N), (8, D_S5), (2 * S5_BLOCKS, 512, 128), (2 * S5_BLOCKS, 128, 512), (D_S5, D_S5)]
    return pl.pallas_call(
        body, name="s5_bwd", grid=(n_blocks,),
        out_shape=(jax.ShapeDtypeStruct((t, D_S5), F32),) + tuple(jax.ShapeDtypeStruct(s, F32) for s in acc_shapes),
        in_specs=[row(D_S5), row(D_S5), row(D_S5), row(S5_N), row(S5_N), halo, halo, full(bb_re), full(bb_im),
                  full(cc_re), full(cc_im), full(pr_re), full(pr_im), full(s5d), full(w_glu), full(b_glu)],
        out_specs=(row(D_S5),) + tuple(acc(s) for s in acc_shapes),
        scratch_shapes=[pltpu.VMEM((tb, S5_N), F32), pltpu.VMEM((tb, S5_N), F32), pltpu.VMEM((tb, S5_N), F32),
                        pltpu.VMEM((tb, S5_N), F32), pltpu.VMEM((8, S5_N), F32), pltpu.VMEM((8, S5_N), F32)],
        compiler_params=_params(56),
    )(dy5, ypre, u5, s_re, s_im, s_re, s_im, bb_re, bb_im, cc_re, cc_im, pr_re, pr_im, s5d, w_glu, b_glu)


def _ssd_bwd(dyssd, yraw, z, xbc, dt_raw, hprev, par, dsk, normw, seq):
    t = xbc.shape[0]
    nc = seq // CHUNK
    n_chunks = t // CHUNK
    fold = _head_fold()

    def blk(i):
        return (i // nc) * nc + (nc - 1 - i % nc)

    def body(dy_ref, yraw_ref, z_ref, xbc_ref, dt_ref, hprev_ref, par_ref, dsk_ref, nw_ref, fold_ref,
             dxbc_ref, dz_ref, ddt_ref, dpar_ref, cacc_ref, dh_ref, dyr_ref):
        i = pl.program_id(0)

        @pl.when(i == 0)
        def _():
            dpar_ref[...] = jnp.zeros_like(dpar_ref)
            cacc_ref[...] = jnp.zeros_like(cacc_ref)

        @pl.when(i % nc == 0)
        def _():
            dh_ref[...] = jnp.zeros_like(dh_ref)

        zz = z_ref[...]
        sz = _sigmoid(zz)
        silu_z = zz * sz
        yraw = yraw_ref[...]
        for g in range(N_GROUPS):
            sl = slice(g * GW, (g + 1) * GW)
            v = yraw[:, sl] * silu_z[:, sl]
            r = lax.rsqrt(jnp.mean(v * v, axis=-1, keepdims=True) + EPS)
            dyg = dy_ref[:, sl]
            cacc_ref[1:2, sl] += jnp.sum(dyg * v * r, axis=0, keepdims=True)
            dyw = dyg * nw_ref[:, sl]
            dv = r * dyw - v * (r * r * r) * jnp.mean(dyw * v, axis=-1, keepdims=True)
            dyr_ref[:, sl] = dv * silu_z[:, sl]
            dz_ref[:, sl] = dv * yraw[:, sl] * (sz[:, sl] * (1.0 + zz[:, sl] * (1.0 - sz[:, sl])))

        dt, a, cs, cst, causal, tri, dt_c, ecs_c, w_c, pair_cols = _ssd_prep(dt_ref[...], par_ref[...])
        cs_last = cs[CHUNK - 1:CHUNK, :]
        causal2 = jnp.concatenate([causal, causal], axis=1)
        lane = lax.broadcasted_iota(jnp.int32, (CHUNK, 128), 1)
        left = lane < HEADDIM
        lane1 = lax.broadcasted_iota(jnp.int32, (1, 128), 1)
        x = xbc_ref[:, 0:D_SSD]
        xdt = x * dt_c
        dyr = dyr_ref[...]
        dyrb = dyr.astype(BF16)
        cacc_ref[0:1, :] += jnp.sum(dyr * x, axis=0, keepdims=True)
        dlast = jnp.zeros((1, 128), F32)
        dxdt_cols, diag_all, dww_cols = [], [], []
        for g in range(N_GROUPS):
            gs = slice(g * GW, (g + 1) * GW)
            b_sl = slice(D_SSD + g * N_STATE, D_SSD + (g + 1) * N_STATE)
            c_sl = slice(D_SSD + (N_GROUPS + g) * N_STATE, D_SSD + (N_GROUPS + g + 1) * N_STATE)
            bg = xbc_ref[:, b_sl].astype(BF16)
            cg = xbc_ref[:, c_sl].astype(BF16)
            scores = lax.dot_general(cg, bg, NT, preferred_element_type=F32)
            scores2 = jnp.concatenate([scores, scores], axis=1)
            hg = hprev_ref[0, gs, :]
            hgb = hg.astype(BF16)
            dhg = dh_ref[gs, :]
            dhgb = dhg.astype(BF16)
            q_all = lax.dot_general(bg, dhgb, NT, preferred_element_type=F32)
            dscores = jnp.zeros((CHUNK, CHUNK), F32)
            diag_cols = []
            for q in range(GW // 128):
                pair = g * (GW // 128) + q
                ps = slice(pair * 128, (pair + 1) * 128)
                decay = _pair_decay(pair_cols[pair], cst, pair, causal2)
                mcat = (scores2 * decay).astype(BF16)
                dyp = dyrb[:, ps]
                dm = lax.dot_general(dyp, _stack_heads(xdt[:, ps], left), NT, preferred_element_type=F32)
                dmd = dm * decay
                dscores = dscores + dmd[:, 0:CHUNK] + dmd[:, CHUNK:]
                rr = lax.dot_general(mcat, dyp, TN, preferred_element_type=F32)
                diag_cols.append(jnp.where(left, rr[0:CHUNK], rr[CHUNK:]))
            wq = w_c[:, gs] * q_all
            diag_g = jnp.concatenate(diag_cols, axis=1)
            diag_all.append(diag_g)
            dxdt_cols.append(diag_g + wq)
            dww_cols.append(wq * xdt[:, gs])
            dp = (ecs_c[:, gs] * dyr[:, gs]).astype(BF16)
            amat = (w_c[:, gs] * xdt[:, gs]).astype(BF16)
            dsb = dscores.astype(BF16)
            dxbc_ref[:, c_sl] = (jnp.dot(dsb, bg, preferred_element_type=F32)
                                 + jnp.dot(dp, hgb, preferred_element_type=F32))
            dxbc_ref[:, b_sl] = (lax.dot_general(dsb, cg, TN, preferred_element_type=F32)
                                 + jnp.dot(amat, dhgb, preferred_element_type=F32))
            dh_in = lax.dot_general(dp, cg, TN, preferred_element_type=F32)
            for j in range(HPG):
                hh = g * HPG + j
                js = slice(j * HEADDIM, (j + 1) * HEADDIM)
                ecl = jnp.exp(cs_last[:, hh:hh + 1])
                dlast = dlast + jnp.where(lane1 == hh, ecl * jnp.sum(dhg[js, :] * hg[js, :]), 0.0)
                dh_ref[g * GW + j * HEADDIM:g * GW + (j + 1) * HEADDIM, :] = ecl * dhg[js, :] + dh_in[js, :]
        dxdt = jnp.concatenate(dxdt_cols, axis=1)
        dxbc_ref[:, 0:D_SSD] = dxdt * dt_c + dyr * dsk_ref[...]
        dww = _mm(jnp.concatenate(dww_cols, axis=1), fold_ref[...])
        dcs = _dot3(dyrb.astype(F32) * (yraw - x * dsk_ref[...])
                    - xdt.astype(BF16).astype(F32) * jnp.concatenate(diag_all, axis=1), fold_ref[...]) - dww
        rowid = lax.broadcasted_iota(jnp.int32, (CHUNK, 128), 0)
        dcs = dcs + jnp.where(rowid == CHUNK - 1, jnp.sum(dww, axis=0, keepdims=True) + dlast, 0.0)
        dadt = _dot3_left(tri, dcs, TN)
        ddt = _mm(dxdt * x, fold_ref[...]) + dadt * a
        da = jnp.sum(dadt * dt, axis=0, keepdims=True)
        ddt_raw = ddt * _sigmoid(dt_ref[...] + par_ref[0:1])
        ddt_raw = jnp.where(lane < N_HEADS, ddt_raw, 0.0)
        ddt_ref[...] = ddt_raw
        dpar_ref[0:1, :] += jnp.sum(ddt_raw, axis=0, keepdims=True)
        dpar_ref[1:2, :] += jnp.where(lane1 < N_HEADS, da * a, 0.0)

    row = lambda w: pl.BlockSpec((CHUNK, w), lambda i: (blk(i), 0))
    full = lambda s: pl.BlockSpec(s, lambda i: (0,) * len(s))
    return pl.pallas_call(
        body, name="ssd_bwd", grid=(n_chunks,),
        out_shape=(jax.ShapeDtypeStruct((t, D_XBC), F32), jax.ShapeDtypeStruct((t, D_SSD), F32),
                   jax.ShapeDtypeStruct((t, DT_PAD), F32), jax.ShapeDtypeStruct((8, 128), F32),
                   jax.ShapeDtypeStruct((8, D_SSD), F32)),
        in_specs=[row(D_SSD), row(D_SSD), row(D_SSD), row(D_XBC), row(DT_PAD),
                  pl.BlockSpec((1, D_SSD, N_STATE), lambda i: (blk(i), 0, 0)),
                  full((8, 128)), full((1, D_SSD)), full((1, D_SSD)), full(fold.shape)],
        out_specs=(row(D_XBC), row(D_SSD), row(DT_PAD), full((8, 128)), full((8, D_SSD))),
        scratch_shapes=[pltpu.VMEM((D_SSD, N_STATE), F32), pltpu.VMEM((CHUNK, D_SSD), F32)],
        compiler_params=_params(48),
    )(dyssd, yraw, z, xbc, dt_raw, hprev, par, dsk, normw, fold)


def _conv_bwd(dxbc, dsilu, xbc_pre, seq):
    t = xbc_pre.shape[0]
    tb = 512
    npb = seq // tb
    cw = 640

    def body(d_ref, ds_ref, cur_ref, halo_ref, o_ref, acc_ref, win):
        i = pl.program_id(1)

        @pl.when(i == 0)
        def _():
            acc_ref[...] = jnp.zeros_like(acc_ref)

        first = (i % npb) == 0
        win[0:8, :] = jnp.where(first, 0.0, halo_ref[8:16, :].astype(F32))
        win[8:8 + tb, :] = cur_ref[...].astype(F32)
        dpre = d_ref[...] * ds_ref[...].astype(F32)
        o_ref[...] = dpre
        for j in range(4):
            acc_ref[3 - j:4 - j, :] += jnp.sum(dpre * win[8 - j:8 - j + tb, :], axis=0, keepdims=True)
        acc_ref[4:5, :] += jnp.sum(dpre, axis=0, keepdims=True)

    blk = pl.BlockSpec((tb, cw), lambda j, i: (i, j))
    return pl.pallas_call(
        body, name="conv_bwd", grid=(D_XBC // cw, t // tb),
        out_shape=(jax.ShapeDtypeStruct((t, D_XBC), F32), jax.ShapeDtypeStruct((8, D_XBC), F32)),
        in_specs=[blk, blk, blk, pl.BlockSpec((16, cw), lambda j, i: (jnp.maximum(i * (tb // 16) - 1, 0), j))],
        out_specs=(blk, pl.BlockSpec((8, cw), lambda j, i: (0, j))),
        scratch_shapes=[pltpu.VMEM((tb + 8, cw), F32)],
        compiler_params=_params(32),
    )(dxbc, dsilu, xbc_pre, xbc_pre)


def _proj_bwd(dz, dpre, ddt, du5, x2, dxa, mod3, conv_w, w_in_pad, seq):
    t = x2.shape[0]
    tb = 512
    npb = seq // tb
    n_blocks = t // tb

    def body(dz_ref, dp_ref, nxt_ref, ddt_ref, du5_ref, x_ref, dxa_ref, mod_ref, cw_ref, w_hbm,
             gx_ref, u_ref, dxp_ref, bacc_ref, w_vmem, sem):
        i = pl.program_id(0)
        _load_once(w_hbm, w_vmem, sem)

        @pl.when(i % npb == 0)
        def _():
            bacc_ref[...] = jnp.zeros_like(bacc_ref)

        last = (i % npb) == npb - 1
        nxt = jnp.where(last, 0.0, nxt_ref[...])
        cur = dp_ref[...]
        xx = jnp.concatenate([cur, nxt], axis=0)
        w = cw_ref[...]
        dxp = w[3:4] * cur
        for j in (1, 2, 3):
            dxp = dxp + w[3 - j:4 - j] * pltpu.roll(xx, tb + 8 - j, axis=0)[0:tb]
        dxpb = dxp.astype(BF16)
        dxp_ref[...] = dxpb
        o1, o2, o3 = D_SSD, D_SSD + D_XBC, D_SSD + D_XBC + DT_PAD
        du = (jnp.dot(dz_ref[...].astype(BF16), w_vmem[0:o1, :], preferred_element_type=F32)
              + jnp.dot(dxpb, w_vmem[o1:o2, :], preferred_element_type=F32)
              + jnp.dot(ddt_ref[...].astype(BF16), w_vmem[o2:o3, :], preferred_element_type=F32)
              + jnp.dot(du5_ref[...].astype(BF16), w_vmem[o3:, :], preferred_element_type=F32))
        m = mod_ref[0]
        xv = x_ref[...]
        u_ref[...] = (xv * (1.0 + m[1:2]) + m[0:1]).astype(BF16)
        gx_ref[...] = dxa_ref[...] + du * (1.0 + m[1:2])
        bacc_ref[0, 0:1, :] += jnp.sum(du, axis=0, keepdims=True)
        bacc_ref[0, 1:2, :] += jnp.sum(du * xv, axis=0, keepdims=True)

    row = lambda w: pl.BlockSpec((tb, w), lambda i: (i, 0))
    nxt_rows = pl.BlockSpec((8, D_XBC), lambda i: (jnp.minimum((i + 1) * (tb // 8), t // 8 - 1), 0))
    return pl.pallas_call(
        body, name="proj_bwd", grid=(n_blocks,),
        out_shape=(jax.ShapeDtypeStruct((t, D_MODEL), F32), jax.ShapeDtypeStruct((t, D_MODEL), BF16),
                   jax.ShapeDtypeStruct((t, D_XBC), BF16), jax.ShapeDtypeStruct((t // seq, 8, D_MODEL), F32)),
        in_specs=[row(D_SSD), row(D_XBC), nxt_rows, row(DT_PAD), row(D_S5), row(D_MODEL), row(D_MODEL),
                  pl.BlockSpec((1, N_MOD, D_MODEL), lambda i: (i // npb, 0, 0)),
                  pl.BlockSpec((4, D_XBC), lambda i: (0, 0)), ANY],
        out_specs=(row(D_MODEL), row(D_MODEL), row(D_XBC), pl.BlockSpec((1, 8, D_MODEL), lambda i: (i // npb, 0, 0))),
        scratch_shapes=[pltpu.VMEM((D_INP, D_MODEL), BF16), pltpu.SemaphoreType.DMA],
        compiler_params=_params(60),
    )(dz, dpre, dpre, ddt, du5, x2, dxa, mod3, conv_w, w_in_pad)


def _pad_rows(a, mult):
    r = a.shape[0]
    pad = (-r) % mult
    return a if pad == 0 else jnp.concatenate([a, jnp.zeros((pad,) + a.shape[1:], a.dtype)], axis=0)


_SMALL = ["conv_w", "conv_b", "dt_bias", "a_log", "d_ssd", "norm_w", "s5_a_re", "s5_a_im", "s5_log_dt", "s5_b_re",
          "s5_b_im", "s5_c_re", "s5_c_im", "s5_d", "b_glu", "ln1_g", "ln1_b", "b1", "b2", "ln2_g", "ln2_b"]


def _tile_rows(size):
    return 8 * (-(-size // 1024))


def _pack_small(d):
    parts = []
    for n in _SMALL:
        flat = d[n].reshape(-1).astype(F32)
        rows = _tile_rows(flat.shape[0])
        pad = rows * 128 - flat.shape[0]
        if pad:
            flat = jnp.concatenate([flat, jnp.zeros((pad,), F32)])
        parts.append(flat.reshape(rows, 128))
    return jnp.concatenate(parts, axis=0)


def _unpack_small(p, shapes):
    out, off = {}, 0
    for n in _SMALL:
        size = math.prod(shapes[n])
        rows = _tile_rows(size)
        out[n] = p[off:off + rows].reshape(-1)[:size].reshape(shapes[n])
        off += rows
    return out


def kernel(x, c, w_ada, b_ada, w_in, conv_w, conv_b, dt_bias, a_log, d_ssd, norm_w, s5_a_re, s5_a_im, s5_log_dt, s5_b_re, s5_b_im, s5_c_re, s5_c_im, s5_d, w_glu, b_glu, w_out, ln1_g, ln1_b, w1, b1, w2, b2, ln2_g, ln2_b, loss_target, m_w_ada, m_b_ada, m_w_in, m_conv_w, m_conv_b, m_dt_bias, m_a_log, m_d_ssd, m_norm_w, m_s5_a_re, m_s5_a_im, m_s5_log_dt, m_s5_b_re, m_s5_b_im, m_s5_c_re, m_s5_c_im, m_s5_d, m_w_glu, m_b_glu, m_w_out, m_ln1_g, m_ln1_b, m_w1, m_b1, m_w2, m_b2, m_ln2_g, m_ln2_b, v_w_ada, v_b_ada, v_w_in, v_conv_w, v_conv_b, v_dt_bias, v_a_log, v_d_ssd, v_norm_w, v_s5_a_re, v_s5_a_im, v_s5_log_dt, v_s5_b_re, v_s5_b_im, v_s5_c_re, v_s5_c_im, v_s5_d, v_w_glu, v_b_glu, v_w_out, v_ln1_g, v_ln1_b, v_w1, v_b1, v_w2, v_b2, v_ln2_g, v_ln2_b):
    weights = dict(w_ada=w_ada, b_ada=b_ada, w_in=w_in, conv_w=conv_w, conv_b=conv_b, dt_bias=dt_bias, a_log=a_log,
                   d_ssd=d_ssd, norm_w=norm_w, s5_a_re=s5_a_re, s5_a_im=s5_a_im, s5_log_dt=s5_log_dt, s5_b_re=s5_b_re,
                   s5_b_im=s5_b_im, s5_c_re=s5_c_re, s5_c_im=s5_c_im, s5_d=s5_d, w_glu=w_glu, b_glu=b_glu, w_out=w_out,
                   ln1_g=ln1_g, ln1_b=ln1_b, w1=w1, b1=b1, w2=w2, b2=b2, ln2_g=ln2_g, ln2_b=ln2_b)
    mom = dict(w_ada=m_w_ada, b_ada=m_b_ada, w_in=m_w_in, conv_w=m_conv_w, conv_b=m_conv_b, dt_bias=m_dt_bias,
               a_log=m_a_log, d_ssd=m_d_ssd, norm_w=m_norm_w, s5_a_re=m_s5_a_re, s5_a_im=m_s5_a_im,
               s5_log_dt=m_s5_log_dt, s5_b_re=m_s5_b_re, s5_b_im=m_s5_b_im, s5_c_re=m_s5_c_re, s5_c_im=m_s5_c_im,
               s5_d=m_s5_d, w_glu=m_w_glu, b_glu=m_b_glu, w_out=m_w_out, ln1_g=m_ln1_g, ln1_b=m_ln1_b, w1=m_w1, b1=m_b1,
               w2=m_w2, b2=m_b2, ln2_g=m_ln2_g, ln2_b=m_ln2_b)
    var = dict(w_ada=v_w_ada, b_ada=v_b_ada, w_in=v_w_in, conv_w=v_conv_w, conv_b=v_conv_b, dt_bias=v_dt_bias,
               a_log=v_a_log, d_ssd=v_d_ssd, norm_w=v_norm_w, s5_a_re=v_s5_a_re, s5_a_im=v_s5_a_im,
               s5_log_dt=v_s5_log_dt, s5_b_re=v_s5_b_re, s5_b_im=v_s5_b_im, s5_c_re=v_s5_c_re, s5_c_im=v_s5_c_im,
               s5_d=v_s5_d, w_glu=v_w_glu, b_glu=v_b_glu, w_out=v_w_out, ln1_g=v_ln1_g, ln1_b=v_ln1_b, w1=v_w1, b1=v_b1,
               w2=v_w2, b2=v_b2, ln2_g=v_ln2_g, ln2_b=v_ln2_b)
    names = list(weights)
    shapes = {n: weights[n].shape for n in names}

    nb, seq, _ = x.shape
    t = nb * seq
    dev = _dev_index()
    x2 = x.reshape(t, D_MODEL)
    tgt2 = loss_target.reshape(t, D_MODEL)

    cw_cols = conv_w.shape[2]
    small_in = jnp.concatenate([c.reshape(-1), conv_w.reshape(-1)]).reshape(-1, 128)
    big_names = ["w_in", "w_out", "w1", "w2", "w_glu"]
    local = {n: (a[0].T if n == "w_in" else a[0]) for n, a in weights.items() if n in big_names}
    shard_bf16 = {n: local[n].astype(BF16) for n in big_names}
    first = _all_gather([small_in, shard_bf16["w_in"], shard_bf16["w_glu"]], "gather_first")
    small_all = first[0].reshape(N_DEV, -1)
    c_all = small_all[:, :nb * D_MODEL].reshape(N_DEV * nb, D_MODEL)
    conv_w_full = small_all[:, nb * D_MODEL:].reshape(N_DEV, 4, cw_cols).transpose(1, 0, 2).reshape(4, D_XBC)

    w_in_t = first[1].reshape(D_IN, D_MODEL)
    w_in_pad = jnp.concatenate(
        [w_in_t[:D_SSD + D_XBC + N_HEADS], jnp.zeros((DT_PAD - N_HEADS, D_MODEL), BF16),
         w_in_t[D_SSD + D_XBC + N_HEADS:]], axis=0)
    w_glu_f = first[2].reshape(D_S5, D_S5)
    late_names = ["w_out", "w1", "w2"]

    ada_cols = w_ada.shape[2]
    b_cols = lax.dynamic_slice_in_dim(b_ada, dev * ada_cols, ada_cols, axis=1)
    mod_cols = _mod_fwd(c_all, w_ada[0], b_cols)
    mod_all = _all_gather([mod_cols], "gather_mod")[0]
    mod_mine = lax.dynamic_slice_in_dim(mod_all, dev * nb, nb, axis=1)
    mod3 = mod_mine.transpose(1, 0, 2).reshape(nb, N_MOD, D_MODEL)
    late_in, mod3 = lax.optimization_barrier(([shard_bf16[n] for n in late_names], mod3))
    late_sems = _gather_start(late_in, "gather_late_start")
    mod3 = mod3 + late_sems[4][0, 0]

    def pad_lanes(v, n):
        return jnp.concatenate([v, jnp.zeros((v.shape[0], n - v.shape[1]), F32)], axis=1)

    par = _pad_rows(jnp.concatenate([pad_lanes(dt_bias, 128), pad_lanes(a_log, 128)], axis=0), 8)
    dsk = jnp.repeat(d_ssd[0], HEADDIM).reshape(1, D_SSD)
    ar = s5_a_re.reshape(1, S5_N)
    ai = s5_a_im.reshape(1, S5_N)
    ldt = jnp.repeat(s5_log_dt[0], S5_P).reshape(1, S5_N)
    br_t = s5_b_re[0].transpose(2, 0, 1).reshape(S5_CH, S5_N)
    bi_t = s5_b_im[0].transpose(2, 0, 1).reshape(S5_CH, S5_N)
    bb_re_t, bb_im_t, pf_re, pf_im, pr_re, pr_im = _s5_params_fwd(ar, ai, ldt, br_t, bi_t)
    gpb = S5_GROUPS // S5_BLOCKS
    mask_b = (jnp.arange(128)[:, None] // S5_CH) == (jnp.arange(512)[None, :] // S5_P)

    def dense_b(bt_):
        blocks = bt_.reshape(S5_CH, S5_BLOCKS, 512).transpose(1, 0, 2)
        return jnp.where(mask_b, jnp.tile(blocks, (1, gpb, 1)), 0.0).astype(BF16)

    def dense_c(cc):
        blocks = cc[0].transpose(0, 2, 1).reshape(S5_BLOCKS, 512, S5_CH)
        return jnp.where(mask_b.T, jnp.tile(blocks, (1, 1, gpb)), 0.0).astype(BF16)

    bb_re, bb_im = dense_b(bb_re_t), dense_b(bb_im_t)
    cc_re, cc_im = dense_c(s5_c_re), dense_c(s5_c_im)
    s5d = s5_d.reshape(1, D_S5)
    ln1 = jnp.concatenate([ln1_g, ln1_b], axis=0)
    vec1 = _pad_rows(jnp.concatenate([b2, ln2_g, ln2_b], axis=0), 8)

    z, xbc_pre, xbc, dsilu, dt_raw, u5 = _proj_conv_fwd(x2, mod3, w_in_pad, conv_w_full, conv_b, seq)
    yraw, ycat, hprev = _ssd_fwd(xbc, z, dt_raw, par, dsk, norm_w, seq)
    s_re, s_im, ypre, ycat = _s5_fwd(u5, bb_re, bb_im, cc_re, cc_im, pf_re, pf_im, s5d, w_glu_f, b_glu, ycat, seq)
    sent, landed = _gather_wait(late_sems[0], late_sems[1], late_sems[2], late_sems[3], ycat, "gather_late_wait")
    gathered = {n: lax.dynamic_update_index_in_dim(l, x, dev, 0) for n, x, l in zip(late_names, sent, landed)}
    w_out_f = gathered["w_out"].reshape(2 * D_MODEL, D_MODEL)
    w1_blocks = gathered["w1"]
    w2_f = gathered["w2"].reshape(D_FF, D_MODEL)
    mix, x1 = _out_ln1(ycat, x2, mod3, w_out_f, ln1, seq)

    dx1, u2b, hb, dhpb, dob, gacc2, db1, bacc2 = _mlp_fwd_bwd(x1, tgt2, mod3, w1_blocks, w2_f, vec1, b1, seq)
    loss = lax.psum(0.5 / D_MODEL * jnp.sum(gacc2[3]), ("x", "y", "c"))

    dmixb, dxa, dyssd, dy5, gacc1, bacc1 = _ln1_out_bwd(dx1, x2, mix, mod3, w_out_f, ln1, seq)

    g_w2 = _atb(hb, dob, "gw2")
    g_w1 = _atb(u2b, dhpb, "gw1")
    g_wout = _atb(ycat, dmixb, "gwout")
    core = lax.axis_index("c").astype(jnp.int32).reshape(1)
    chip = 2 * lax.axis_index("x") + lax.axis_index("y")

    def chip_sums_of(names, grads, tag):
        by_dest = [g if g.ndim == 2 else g.reshape((4, 2) + g.shape[1:]) for g in grads]
        from_sibling = _sibling_swap(by_dest, "rs_swap_" + tag)
        return [_add_halves(g, r, core, "rs_add_" + n) for g, r, n in zip(by_dest, from_sibling, names)]

    early_names = ["w_out", "w1", "w2"]
    early_sums = chip_sums_of(early_names, [g_wout.reshape((N_DEV,) + w_out.shape[1:]), g_w1,
                                            g_w2.reshape((N_DEV,) + w2.shape[1:])], "early")
    early = _all_to_all_start(early_sums, "rs_early_start")
    s5d_after = s5d + early[4][0, 0]

    du5, vacc, sacc, d_cc, d_bb, g_wglu = _s5_bwd(dy5, ypre, u5, s_re, s_im, bb_re, bb_im, cc_re, cc_im,
                                                  pr_re, pr_im, s5d_after, w_glu_f, b_glu, seq)
    dxbc, dz, ddt, dpar, cacc = _ssd_bwd(dyssd, yraw, z, xbc, dt_raw, hprev, par, dsk, norm_w, seq)
    dpre, conv_acc = _conv_bwd(dxbc, dsilu, xbc_pre, seq)
    grad_x2, ub, dxpb, bacc0 = _proj_bwd(dz, dpre, ddt, du5, x2, dxa, mod3, conv_w_full, w_in_pad, seq)

    g_win_t = jnp.concatenate([_atb(dz, ub, "gwin_z"), _atb(dxpb, ub, "gwin_xbc"),
                               _atb(ddt, ub, "gwin_dt")[:N_HEADS], _atb(du5, ub, "gwin_s5")], axis=0)

    def diag_b(dd):
        kept = jnp.where(mask_b, dd, 0.0).reshape(S5_BLOCKS, gpb, S5_CH, 512).sum(1)
        return kept.transpose(1, 0, 2).reshape(S5_CH, S5_N)

    def diag_c(dd):
        kept = jnp.where(mask_b.T, dd, 0.0).reshape(S5_BLOCKS, 512, gpb, S5_CH).sum(2)
        return kept.reshape(S5_GROUPS, S5_P, S5_CH).transpose(0, 2, 1)

    g_ar, g_ai, g_ldt, g_br_t, g_bi_t = _s5_params_bwd(ar, ai, ldt, br_t, bi_t, vacc[0:1], vacc[1:2],
                                                      diag_b(d_bb[:S5_BLOCKS]), diag_b(d_bb[S5_BLOCKS:]))

    def from_t(gt):
        return gt.reshape(S5_CH, S5_GROUPS, S5_P).transpose(1, 2, 0)

    small_g = dict(
        conv_w=conv_acc[0:4], conv_b=conv_acc[4:5], dt_bias=dpar[0:1, :N_HEADS], a_log=dpar[1:2, :N_HEADS],
        d_ssd=cacc[0].reshape(N_HEADS, HEADDIM).sum(1), norm_w=cacc[1:2],
        s5_a_re=g_ar, s5_a_im=g_ai, s5_log_dt=g_ldt[:, :S5_GROUPS], s5_b_re=from_t(g_br_t), s5_b_im=from_t(g_bi_t),
        s5_c_re=diag_c(d_cc[:S5_BLOCKS]), s5_c_im=diag_c(d_cc[S5_BLOCKS:]), s5_d=sacc[0:1], b_glu=sacc[1:2],
        ln1_g=gacc1[0:1], ln1_b=gacc1[1:2], b1=db1, b2=gacc2[2:3], ln2_g=gacc2[0:1], ln2_b=gacc2[1:2])

    dmod = jnp.concatenate([bacc0[:, 0], bacc0[:, 1], bacc1[:, 0], bacc2[:, 0], bacc2[:, 1], bacc2[:, 2]], axis=1)
    dmod_all, small_parts = _all_gather([dmod, _pack_small(small_g)], "gather_dmod_small_grads")
    dmod_all = dmod_all.reshape(N_DEV * nb, N_MOD * D_MODEL)
    dmod_cols = lax.dynamic_slice_in_dim(dmod_all, dev * ada_cols, ada_cols, axis=1)
    g_wada, g_bada = _mod_bwd(c_all, dmod_cols, dmod_all)

    late_rs = ["w_in", "w_glu"]
    late_sums = chip_sums_of(late_rs, [g_win_t.reshape(N_DEV, w_in.shape[2], D_MODEL),
                                       g_wglu.reshape((N_DEV,) + w_glu.shape[1:])], "late")
    parts = dict(zip(late_rs, _chip_all_to_all(late_sums, "rs_late_all_to_all")))
    sent, landed = _all_to_all_wait(early[0], early[1], early[2], early[3], parts["w_in"], "rs_early_wait")
    for n, l, h in zip(early_names, landed, sent):
        parts[n] = lax.dynamic_update_index_in_dim(l, lax.dynamic_index_in_dim(h, chip, 0, keepdims=False), chip, 0)

    res = {k: {} for k in "gdmv"}
    for n in big_names:
        w_m_v = [(a[n][0].T if n == "w_in" else a[n][0]) for a in (weights, mom, var)]
        outs = _adamw(parts[n], *w_m_v, "adamw_" + n)
        for k, a in zip("gdmv", outs):
            res[k][n] = (a.T if n == "w_in" else a)[None]

    ag, ad, am, av = _adamw(g_wada[None], w_ada[0], m_w_ada[0], v_w_ada[0], "adamw_w_ada")
    for k, a in (("g", ag), ("d", ad), ("m", am), ("v", av)):
        res[k]["w_ada"] = a[None]
    bg_, bd_, bm_, bv_ = _adamw(g_bada.reshape(1, -1, 128), b_ada.reshape(-1, 128), m_b_ada.reshape(-1, 128),
                                v_b_ada.reshape(-1, 128), "adamw_b_ada")
    for k, a in (("g", bg_), ("d", bd_), ("m", bm_), ("v", bv_)):
        res[k]["b_ada"] = a.reshape(shapes["b_ada"])

    small_shapes = dict(shapes)
    small_shapes["conv_w"] = (1, 4, D_XBC)
    rep = {n: (jnp.zeros((1, 4, D_XBC), F32) if n == "conv_w" else weights[n]) for n in _SMALL}
    rep_m = {n: (jnp.zeros((1, 4, D_XBC), F32) if n == "conv_w" else mom[n]) for n in _SMALL}
    rep_v = {n: (jnp.ones((1, 4, D_XBC), F32) if n == "conv_w" else var[n]) for n in _SMALL}
    sg_, sd_, sm_, sv_ = _adamw(small_parts, _pack_small(rep), _pack_small(rep_m), _pack_small(rep_v), "adamw_small")
    for k, p in (("g", sg_), ("d", sd_), ("m", sm_), ("v", sv_)):
        un = _unpack_small(p, small_shapes)
        for n in _SMALL:
            if n != "conv_w":
                res[k][n] = un[n]
    g_conv_full = _unpack_small(sg_, small_shapes)["conv_w"][0]
    g_conv_mine = lax.dynamic_slice_in_dim(g_conv_full, dev * cw_cols, cw_cols, axis=1)
    cg_, cd_, cm_, cv_ = _adamw(g_conv_mine[None], conv_w[0], m_conv_w[0], v_conv_w[0], "adamw_conv_w")
    for k, a in (("g", cg_), ("d", cd_), ("m", cm_), ("v", cv_)):
        res[k]["conv_w"] = a[None]

    grad_x = grad_x2.reshape(nb, seq, D_MODEL)
    return (loss, grad_x, *[res["g"][n] for n in names], *[res["d"][n] for n in names],
            *[res["m"][n] for n in names], *[res["v"][n] for n in names])
```

```python
import functools
import math

import jax
import jax.numpy as jnp
from jax import lax
from jax.experimental import pallas as pl
from jax.experimental.pallas import tpu as pltpu

F32, BF16 = jnp.float32, jnp.bfloat16
MESH = pl.DeviceIdType.MESH
N_DEV = 8

D_MODEL = 1024
D_SSD = 1536
N_HEADS = 24
HEADDIM = 64
N_GROUPS = 4
HPG = 6
GW = HPG * HEADDIM
N_STATE = 128
CHUNK = 128
D_XBC = 2560
D_S5 = 512
S5_GROUPS = 32
S5_CH = 16
S5_P = 64
S5_N = S5_GROUPS * S5_P
D_IN = 4632
DT_PAD = 128
D_INP = D_SSD + D_XBC + DT_PAD + D_S5
D_FF = 4096
N_MOD = 6
ALPHA = 2.0 ** 0.25
EPS = 1e-5
LR, B1, B2, AEPS, WD, STEP = 0.001, 0.9, 0.999, 1e-08, 0.01, 10

NT = (((1,), (1,)), ((), ()))
TN = (((0,), (0,)), ((), ()))
ANY = pl.BlockSpec(memory_space=pl.ANY)
HIGHEST = lax.Precision.HIGHEST


def _mm(a, b):
    return jnp.dot(a.astype(BF16), b.astype(BF16), preferred_element_type=F32)


def _mm_nt(a, b):
    return lax.dot_general(a.astype(BF16), b.astype(BF16), NT, preferred_element_type=F32)


def _mm_tn(a, b):
    return lax.dot_general(a.astype(BF16), b.astype(BF16), TN, preferred_element_type=F32)


def _row_block(r, cap):
    best = r
    for cand in range(8, min(r, cap) + 1, 8):
        if r % cand == 0:
            best = cand
    return best if best <= cap else r


def _params(vmem_mb):
    return pltpu.CompilerParams(vmem_limit_bytes=vmem_mb << 20)


def _sigmoid(x):
    return 0.5 * (jnp.tanh(0.5 * x) + 1.0)


def _softplus(x):
    return jnp.maximum(x, 0.0) + jnp.log(1.0 + jnp.exp(-jnp.abs(x)))


_GK = math.sqrt(2.0 / math.pi)


def _gelu(x):
    return 0.5 * x * (1.0 + jnp.tanh(_GK * (x + 0.044715 * x * x * x)))


def _gelu_grad(x):
    t = jnp.tanh(_GK * (x + 0.044715 * x * x * x))
    return 0.5 * (1.0 + t) + 0.5 * x * (1.0 - t * t) * _GK * (1.0 + 3.0 * 0.044715 * x * x)


def _dev_index():
    return 4 * lax.axis_index("x") + 2 * lax.axis_index("y") + lax.axis_index("c")


def _all_gather(xs, name):
    n = len(xs)

    def body(*refs):
        x_refs, out_refs = refs[:n], refs[n:2 * n]
        send_sems, recv_sems, local_sems = refs[2 * n:]
        ix, iy, ic = lax.axis_index("x"), lax.axis_index("y"), lax.axis_index("c")
        me, sibling = (ix, iy, ic), (ix, iy, 1 - ic)
        chips = [(1 - ix, iy), (ix, 1 - iy), (1 - ix, 1 - iy)]

        def slot(a, px, py, pc):
            return out_refs[a].at[4 * px + 2 * py + pc]

        def copy(a, k, block, to, src=None):
            return pltpu.make_async_remote_copy(
                src_ref=slot(a, *block) if src is None else src, dst_ref=slot(a, *block),
                send_sem=send_sems.at[7 * a + k], recv_sem=recv_sems.at[7 * a + k], device_id=to, device_id_type=MESH)

        mine = [pltpu.make_async_copy(x_refs[a], slot(a, *me), local_sems.at[a]) for a in range(n)]
        for cp in mine:
            cp.start()
        first = []
        for j, chip in enumerate(chips):
            first += [copy(a, 1 + j, me, (*chip, ic), src=x_refs[a]) for a in range(n)]
        first += [copy(a, 0, me, sibling, src=x_refs[a]) for a in range(n)]
        for cp in first:
            cp.start()
        passed = []
        for j, chip in enumerate(chips):
            for a in range(n):
                copy(a, 1 + j, (*chip, ic), me).wait_recv()
                cp = copy(a, 4 + j, (*chip, ic), sibling)
                cp.start()
                passed.append(cp)
        for a in range(n):
            copy(a, 0, sibling, me).wait_recv()
            for j, chip in enumerate(chips):
                copy(a, 4 + j, (*chip, 1 - ic), me).wait_recv()
        for cp in first + passed:
            cp.wait_send()
        for cp in mine:
            cp.wait()

    return pl.pallas_call(
        body, name=name, out_shape=tuple(jax.ShapeDtypeStruct((N_DEV,) + x.shape, x.dtype) for x in xs),
        in_specs=[ANY] * n, out_specs=tuple([ANY] * n),
        scratch_shapes=[pltpu.SemaphoreType.DMA((7 * n,)), pltpu.SemaphoreType.DMA((7 * n,)),
                        pltpu.SemaphoreType.DMA((n,))],
    )(*xs)


HBM = pl.BlockSpec(memory_space=pltpu.HBM)
SEM = pl.BlockSpec(memory_space=pltpu.SEMAPHORE)
DATAFLOW = pltpu.SideEffectType.DATAFLOW_SIDE_EFFECTING


def _peer(k):
    ix, iy, ic = lax.axis_index("x"), lax.axis_index("y"), lax.axis_index("c")
    return (1 - ix if k & 4 else ix, 1 - iy if k & 2 else iy, 1 - ic if k & 1 else ic)


def _block_of(p):
    return 4 * p[0] + 2 * p[1] + p[2]


def _gather_start(xs, name):
    n = len(xs)
    lands = [lax.empty((N_DEV,) + x.shape, x.dtype) for x in xs]

    def body(*refs):
        x_refs, land_refs = refs[:n], refs[n:2 * n]
        send_sems, recv_sems = refs[2 * n], refs[2 * n + 1]
        token = refs[-1]
        me = _block_of(_peer(0))
        for a in range(n):
            for k in range(1, N_DEV):
                pltpu.make_async_remote_copy(
                    src_ref=x_refs[a], dst_ref=land_refs[a].at[me], send_sem=send_sems.at[7 * a + k - 1],
                    recv_sem=recv_sems.at[7 * a + k - 1], device_id=_peer(k), device_id_type=MESH).start()
        token[...] = jnp.zeros_like(token)

    outs = pl.pallas_call(
        body, name=name,
        out_shape=(pltpu.SemaphoreType.DMA((7 * n,)), pltpu.SemaphoreType.DMA((7 * n,)))
        + tuple(pltpu.HBM(x.shape, x.dtype) for x in xs) + tuple(pltpu.HBM(l.shape, l.dtype) for l in lands)
        + (jax.ShapeDtypeStruct((8, 128), F32),),
        in_specs=[HBM] * (2 * n), out_specs=(SEM, SEM) + (HBM,) * (2 * n) + (pl.BlockSpec(memory_space=pltpu.VMEM),),
        input_output_aliases={i: 2 + i for i in range(2 * n)},
        compiler_params=pltpu.CompilerParams(has_side_effects=DATAFLOW),
    )(*[pltpu.with_memory_space_constraint(x, pltpu.HBM) for x in xs],
      *[pltpu.with_memory_space_constraint(l, pltpu.HBM) for l in lands])
    return outs[0], outs[1], outs[2:2 + n], outs[2 + n:2 + 2 * n], outs[-1]


def _gather_wait(send_sems, recv_sems, xs_thru, lands_thru, after, name):
    n = len(xs_thru)

    def body(*refs):
        x_refs, land_refs = refs[:n], refs[n:2 * n]
        send_sems, recv_sems = refs[2 * n], refs[2 * n + 1]
        for a in range(n):
            for k in range(1, N_DEV):
                cp = pltpu.make_async_remote_copy(
                    src_ref=x_refs[a], dst_ref=land_refs[a].at[_block_of(_peer(k))], send_sem=send_sems.at[7 * a + k - 1],
                    recv_sem=recv_sems.at[7 * a + k - 1], device_id=_peer(k), device_id_type=MESH)
                cp.wait_send()
                cp.wait_recv()

    outs = pl.pallas_call(
        body, name=name,
        out_shape=tuple(pltpu.HBM(x.shape, x.dtype) for x in xs_thru)
        + tuple(pltpu.HBM(l.shape, l.dtype) for l in lands_thru),
        in_specs=[HBM] * (2 * n) + [SEM, SEM, ANY], out_specs=(HBM,) * (2 * n),
        input_output_aliases={i: i for i in range(2 * n)},
        compiler_params=pltpu.CompilerParams(has_side_effects=DATAFLOW),
    )(*xs_thru, *lands_thru, send_sems, recv_sems, after)
    return outs[:n], outs[n:]


def _chip_peer(k):
    ix, iy = lax.axis_index("x"), lax.axis_index("y")
    return (1 - ix if k & 2 else ix, 1 - iy if k & 1 else iy)


def _all_to_all_start(hs, name):
    n = len(hs)
    lands = [lax.empty(h.shape, h.dtype) for h in hs]

    def body(*refs):
        h_refs, land_refs = refs[:n], refs[n:2 * n]
        send_sems, recv_sems = refs[2 * n], refs[2 * n + 1]
        token = refs[-1]
        ic = lax.axis_index("c")
        mx, my = _chip_peer(0)
        for a in range(n):
            for k in range(1, 4):
                px, py = _chip_peer(k)
                pltpu.make_async_remote_copy(
                    src_ref=h_refs[a].at[2 * px + py], dst_ref=land_refs[a].at[2 * mx + my],
                    send_sem=send_sems.at[3 * a + k - 1], recv_sem=recv_sems.at[3 * a + k - 1],
                    device_id=(px, py, ic), device_id_type=MESH).start()
        token[...] = jnp.zeros_like(token)

    outs = pl.pallas_call(
        body, name=name,
        out_shape=(pltpu.SemaphoreType.DMA((3 * n,)), pltpu.SemaphoreType.DMA((3 * n,)))
        + tuple(pltpu.HBM(h.shape, h.dtype) for h in hs) + tuple(pltpu.HBM(l.shape, l.dtype) for l in lands)
        + (jax.ShapeDtypeStruct((8, 128), F32),),
        in_specs=[HBM] * (2 * n), out_specs=(SEM, SEM) + (HBM,) * (2 * n) + (pl.BlockSpec(memory_space=pltpu.VMEM),),
        input_output_aliases={i: 2 + i for i in range(2 * n)},
        compiler_params=pltpu.CompilerParams(has_side_effects=DATAFLOW),
    )(*[pltpu.with_memory_space_constraint(h, pltpu.HBM) for h in hs],
      *[pltpu.with_memory_space_constraint(l, pltpu.HBM) for l in lands])
    return outs[0], outs[1], outs[2:2 + n], outs[2 + n:2 + 2 * n], outs[-1]


def _all_to_all_wait(send_sems, recv_sems, hs_thru, lands_thru, after, name):
    n = len(hs_thru)

    def body(*refs):
        h_refs, land_refs = refs[:n], refs[n:2 * n]
        send_sems, recv_sems = refs[2 * n], refs[2 * n + 1]
        ic = lax.axis_index("c")
        for a in range(n):
            for k in range(1, 4):
                px, py = _chip_peer(k)
                cp = pltpu.make_async_remote_copy(
                    src_ref=h_refs[a].at[2 * px + py], dst_ref=land_refs[a].at[2 * px + py],
                    send_sem=send_sems.at[3 * a + k - 1], recv_sem=recv_sems.at[3 * a + k - 1],
                    device_id=(px, py, ic), device_id_type=MESH)
                cp.wait_send()
                cp.wait_recv()

    outs = pl.pallas_call(
        body, name=name,
        out_shape=tuple(pltpu.HBM(h.shape, h.dtype) for h in hs_thru)
        + tuple(pltpu.HBM(l.shape, l.dtype) for l in lands_thru),
        in_specs=[HBM] * (2 * n) + [SEM, SEM, ANY], out_specs=(HBM,) * (2 * n),
        input_output_aliases={i: i for i in range(2 * n)},
        compiler_params=pltpu.CompilerParams(has_side_effects=DATAFLOW),
    )(*hs_thru, *lands_thru, send_sems, recv_sems, after)
    return outs[:n], outs[n:]


def _sibling_swap(gs, name):
    n = len(gs)

    def body(*refs):
        g_refs, recv_refs = refs[:n], refs[n:2 * n]
        send_sems, recv_sems = refs[2 * n:]
        ix, iy, ic = lax.axis_index("x"), lax.axis_index("y"), lax.axis_index("c")

        def block(g_ref, q):
            if len(g_ref.shape) == 4:
                return g_ref.at[q, 1 - ic]
            cw = g_ref.shape[1] // N_DEV
            return g_ref.at[:, pl.ds(pl.multiple_of((2 * q + 1 - ic) * cw, 128), cw)]

        cps = []
        for a in range(n):
            for q in range(4):
                cps.append(pltpu.make_async_remote_copy(
                    src_ref=block(g_refs[a], q), dst_ref=recv_refs[a].at[q],
                    send_sem=send_sems.at[4 * a + q], recv_sem=recv_sems.at[4 * a + q],
                    device_id=(ix, iy, 1 - ic), device_id_type=MESH))
        for cp in cps:
            cp.start()
        for cp in cps:
            cp.wait()

    return pl.pallas_call(
        body, name=name,
        out_shape=tuple(jax.ShapeDtypeStruct(
            (4,) + (g.shape[2:] if g.ndim == 4 else (g.shape[0], g.shape[1] // N_DEV)), g.dtype) for g in gs),
        in_specs=[ANY] * n, out_specs=tuple([ANY] * n),
        scratch_shapes=[pltpu.SemaphoreType.DMA((4 * n,)), pltpu.SemaphoreType.DMA((4 * n,))],
    )(*gs)


def _chip_all_to_all(hs, name):
    n = len(hs)

    def body(*refs):
        h_refs, out_refs = refs[:n], refs[n:2 * n]
        send_sems, recv_sems, local_sems = refs[2 * n:]
        ix, iy, ic = lax.axis_index("x"), lax.axis_index("y"), lax.axis_index("c")
        me = 2 * ix + iy
        peers = [(1 - ix, iy), (ix, 1 - iy), (1 - ix, 1 - iy)]
        mine = [pltpu.make_async_copy(h_refs[a].at[me], out_refs[a].at[me], local_sems.at[a]) for a in range(n)]
        for cp in mine:
            cp.start()

        def copy(a, k, src_slot, dst_slot, peer):
            return pltpu.make_async_remote_copy(
                src_ref=h_refs[a].at[src_slot], dst_ref=out_refs[a].at[dst_slot],
                send_sem=send_sems.at[3 * a + k], recv_sem=recv_sems.at[3 * a + k],
                device_id=(*peer, ic), device_id_type=MESH)

        sends = [copy(a, k, 2 * px + py, me, (px, py)) for a in range(n) for k, (px, py) in enumerate(peers)]
        for cp in sends:
            cp.start()
        for a in range(n):
            for k, (px, py) in enumerate(peers):
                copy(a, k, 2 * px + py, 2 * px + py, (px, py)).wait_recv()
        for cp in sends:
            cp.wait_send()
        for cp in mine:
            cp.wait()

    return pl.pallas_call(
        body, name=name, out_shape=tuple(jax.ShapeDtypeStruct(h.shape, h.dtype) for h in hs),
        in_specs=[ANY] * n, out_specs=tuple([ANY] * n),
        scratch_shapes=[pltpu.SemaphoreType.DMA((3 * n,)), pltpu.SemaphoreType.DMA((3 * n,)),
                        pltpu.SemaphoreType.DMA((n,))],
    )(*hs)


def _add_halves(g, recv, core, name):
    _, r, c = recv.shape
    br = _row_block(r, 512)
    stacked = g.ndim == 4

    def body(core_ref, g_ref, r_ref, o_ref):
        o_ref[0] = ((g_ref[0, 0] if stacked else g_ref[...]) + r_ref[0]).astype(BF16)

    spec = pl.BlockSpec((1, br, c), lambda i, j, core_ref: (i, j, 0))
    if stacked:
        g_spec = pl.BlockSpec((1, 1, br, c), lambda i, j, core_ref: (i, core_ref[0], j, 0))
    else:
        g_spec = pl.BlockSpec((br, c), lambda i, j, core_ref: (j, 2 * i + core_ref[0]))
    return pl.pallas_call(
        body, name=name, out_shape=jax.ShapeDtypeStruct(recv.shape, BF16),
        grid_spec=pltpu.PrefetchScalarGridSpec(
            num_scalar_prefetch=1, grid=(4, r // br), in_specs=[g_spec, spec], out_specs=spec),
        compiler_params=_params(32),
    )(core, g, recv)


def _adamw(parts, w, m, v, name):
    n_parts, r, c = parts.shape
    if r % 8 == 0:
        br, bc = _row_block(r, 512 if c <= 1024 else 256), c
    else:
        br, bc = r, (256 if c % 256 == 0 else c)

    def body(p_ref, w_ref, m_ref, v_ref, g_out, d_out, m_out, v_out):
        g = p_ref[0].astype(F32)
        for p in range(1, n_parts):
            g = g + p_ref[p].astype(F32)
        m2 = B1 * m_ref[...] + (1.0 - B1) * g
        v2 = B2 * v_ref[...] + (1.0 - B2) * (g * g)
        m_hat = m2 / (1.0 - B1 ** STEP)
        v_hat = v2 / (1.0 - B2 ** STEP)
        g_out[...] = g
        d_out[...] = -LR * (m_hat / (jnp.sqrt(v_hat) + AEPS) + WD * w_ref[...])
        m_out[...] = m2
        v_out[...] = v2

    spec = pl.BlockSpec((br, bc), lambda i, j: (i, j))
    out = jax.ShapeDtypeStruct((r, c), F32)
    return pl.pallas_call(
        body, name=name, out_shape=(out, out, out, out), grid=(r // br, c // bc),
        in_specs=[pl.BlockSpec((n_parts, br, bc), lambda i, j: (0, i, j)), spec, spec, spec],
        out_specs=(spec, spec, spec, spec), compiler_params=_params(40),
    )(parts, w, m, v)


def _atb(a, b, name):
    t, k1 = a.shape
    k2 = b.shape[1]
    bt = math.gcd(t, 2048)

    def pick(k):
        for cand in (1024, 768, 512, 384, 256, 128):
            if k % cand == 0:
                return cand
        return k

    b1, b2 = pick(k1), pick(k2)

    def body(a_ref, b_ref, o_ref):
        @pl.when(pl.program_id(2) == 0)
        def _():
            o_ref[...] = jnp.zeros_like(o_ref)
        o_ref[...] += _mm_tn(a_ref[...], b_ref[...])

    return pl.pallas_call(
        body, name=name, out_shape=jax.ShapeDtypeStruct((k1, k2), F32), grid=(k1 // b1, k2 // b2, t // bt),
        in_specs=[pl.BlockSpec((bt, b1), lambda i, j, k: (k, i)), pl.BlockSpec((bt, b2), lambda i, j, k: (k, j))],
        out_specs=pl.BlockSpec((b1, b2), lambda i, j, k: (i, j)), compiler_params=_params(48),
    )(a, b)


def _mod_fwd(c_all, w_ada, b_cols):
    def body(c_ref, w_ref, b_ref, o_ref):
        cc = c_ref[...]
        cond = cc * _sigmoid(cc)
        o_ref[...] = _mm(cond, w_ref[...]) + b_ref[...]

    return pl.pallas_call(body, name="mod_fwd", out_shape=jax.ShapeDtypeStruct((c_all.shape[0], w_ada.shape[1]), F32),
                          compiler_params=_params(32))(c_all, w_ada, b_cols)


def _mod_bwd(c_all, dmod_cols, dmod_all):
    def body(c_ref, dc_ref, da_ref, gw_ref, gb_ref):
        cc = c_ref[...]
        cond = cc * _sigmoid(cc)
        gw_ref[...] = _mm_tn(cond, dc_ref[...])
        gb_ref[...] = jnp.sum(da_ref[...], axis=0, keepdims=True)

    return pl.pallas_call(
        body, name="mod_bwd",
        out_shape=(jax.ShapeDtypeStruct((D_MODEL, dmod_cols.shape[1]), F32), jax.ShapeDtypeStruct((1, dmod_all.shape[1]), F32)),
        compiler_params=_params(32))(c_all, dmod_cols, dmod_all)


LOAD_CHUNKS = 8


def _load_once(hbm_ref, vmem_ref, sems):
    rows = hbm_ref.shape[0] // LOAD_CHUNKS

    @pl.when(pl.program_id(0) == 0)
    def _():
        cps = [pltpu.make_async_copy(hbm_ref.at[pl.ds(k * rows, rows)], vmem_ref.at[pl.ds(k * rows, rows)], sems.at[k])
               for k in range(LOAD_CHUNKS)]
        for cp in cps:
            cp.start()
        for cp in cps:
            cp.wait()


def _conv_taps(win_ref, w, tb, cols):
    shifted = [win_ref[8 - j:8 - j + tb, cols] for j in range(4)]
    acc = w[3:4] * shifted[0]
    for j in (1, 2, 3):
        acc = acc + w[3 - j:4 - j] * shifted[j]
    return acc, shifted


def _proj_conv_fwd(x2, mod3, w_in_pad, conv_w, conv_b, seq):
    t = x2.shape[0]
    tb = 256
    npb = seq // tb
    cw = 512

    def body(x_ref, mod_ref, w_hbm, cw_ref, cb_ref, z_ref, pre_ref, xbc_ref, dsilu_ref, dt_ref, u5_ref, w_vmem, win, sem):
        _load_once(w_hbm, w_vmem, sem)
        first = (pl.program_id(0) % npb) == 0

        @pl.when(first)
        def _():
            win[0:8, :] = jnp.zeros((8, D_XBC), F32)

        @pl.when(jnp.logical_not(first))
        def _():
            win[0:8, :] = win[tb:tb + 8, :]

        m = mod_ref[0]
        u = (x_ref[...] * (1.0 + m[1:2]) + m[0:1]).astype(BF16)
        z_ref[...] = lax.dot_general(u, w_vmem[0:D_SSD, :], NT, preferred_element_type=F32)
        dt_ref[...] = lax.dot_general(u, w_vmem[D_SSD + D_XBC:D_SSD + D_XBC + DT_PAD, :], NT,
                                      preferred_element_type=F32)
        u5_ref[...] = lax.dot_general(u, w_vmem[D_SSD + D_XBC + DT_PAD:, :], NT, preferred_element_type=F32)
        for k in range(D_XBC // cw):
            cols = slice(k * cw, (k + 1) * cw)
            pre_k = lax.dot_general(u, w_vmem[D_SSD + k * cw:D_SSD + (k + 1) * cw, :], NT,
                                    preferred_element_type=F32)
            win[8:8 + tb, cols] = pre_k
            pre_ref[:, cols] = pre_k
            conv, _ = _conv_taps(win, cw_ref[:, cols], tb, cols)
            conv = conv + cb_ref[:, cols]
            sg = _sigmoid(conv)
            xbc_ref[:, cols] = conv * sg
            dsilu_ref[:, cols] = sg * (1.0 + conv * (1.0 - sg))

    row = lambda w: pl.BlockSpec((tb, w), lambda i: (i, 0))
    return pl.pallas_call(
        body, name="proj_conv_fwd", grid=(t // tb,),
        out_shape=(jax.ShapeDtypeStruct((t, D_SSD), F32), jax.ShapeDtypeStruct((t, D_XBC), F32),
                   jax.ShapeDtypeStruct((t, D_XBC), F32), jax.ShapeDtypeStruct((t, D_XBC), F32),
                   jax.ShapeDtypeStruct((t, DT_PAD), F32), jax.ShapeDtypeStruct((t, D_S5), F32)),
        in_specs=[row(D_MODEL), pl.BlockSpec((1, N_MOD, D_MODEL), lambda i: (i // npb, 0, 0)), ANY,
                  pl.BlockSpec((4, D_XBC), lambda i: (0, 0)), pl.BlockSpec((1, D_XBC), lambda i: (0, 0))],
        out_specs=(row(D_SSD), row(D_XBC), row(D_XBC), row(D_XBC), row(DT_PAD), row(D_S5)),
        scratch_shapes=[pltpu.VMEM((D_INP, D_MODEL), BF16), pltpu.VMEM((tb + 8, D_XBC), F32),
                        pltpu.SemaphoreType.DMA((LOAD_CHUNKS,))],
        compiler_params=_params(56),
    )(x2, mod3, w_in_pad, conv_w, conv_b)


N_PAIRS = N_HEADS // 2


def _split3(x):
    hi = x.astype(BF16)
    r = x - hi.astype(F32)
    mid = r.astype(BF16)
    lo = (r - mid.astype(F32)).astype(BF16)
    return hi, mid, lo


def _dot3(x, e, dims=(((1,), (0,)), ((), ()))):
    return sum(lax.dot_general(p, e, dims, preferred_element_type=F32) for p in _split3(x))


def _dot3_left(e, x, dims=(((1,), (0,)), ((), ()))):
    return sum(lax.dot_general(e, p, dims, preferred_element_type=F32) for p in _split3(x))


def _head_fold():
    return (jnp.arange(D_SSD)[:, None] // HEADDIM == jnp.arange(128)[None, :]).astype(BF16)


def _ssd_prep(dt_raw, par):
    dtb = par[0:1]
    a = -jnp.exp(par[1:2])
    dt = _softplus(dt_raw + dtb)
    adt = dt * a
    row = lax.broadcasted_iota(jnp.int32, (CHUNK, CHUNK), 0)
    col = lax.broadcasted_iota(jnp.int32, (CHUNK, CHUNK), 1)
    causal = row >= col
    tri = causal.astype(BF16)
    cs = _dot3_left(tri, adt)
    left = col < HEADDIM

    def lanes(v, h):
        return jnp.broadcast_to(v[:, h:h + 1], (CHUNK, 128))

    dt_c, cs_c, pair_cols = [], [], []
    for p in range(N_PAIRS):
        c0, c1 = lanes(cs, 2 * p), lanes(cs, 2 * p + 1)
        pair_cols.append(jnp.concatenate([c0, c1], axis=1))
        cs_c.append(jnp.where(left, c0, c1))
        dt_c.append(jnp.where(left, lanes(dt, 2 * p), lanes(dt, 2 * p + 1)))
    cs_c = jnp.concatenate(cs_c, axis=1)
    dt_c = jnp.concatenate(dt_c, axis=1)
    return dt, a, cs, cs.T, causal, tri, dt_c, jnp.exp(cs_c), jnp.exp(cs_c[CHUNK - 1:CHUNK, :] - cs_c), pair_cols


def _pair_decay(cols, cst, pair, causal2):
    rows = jnp.concatenate([jnp.broadcast_to(cst[2 * pair:2 * pair + 1, :], (CHUNK, CHUNK)),
                            jnp.broadcast_to(cst[2 * pair + 1:2 * pair + 2, :], (CHUNK, CHUNK))], axis=1)
    return jnp.exp(jnp.where(causal2, cols - rows, -jnp.inf))


def _stack_heads(xp, left):
    return jnp.concatenate([jnp.where(left, xp, 0.0), jnp.where(left, 0.0, xp)], axis=0).astype(BF16)


def _ssd_fwd(xbc, z, dt_raw, par, dsk, normw, seq):
    t = xbc.shape[0]
    nc = seq // CHUNK
    n_chunks = t // CHUNK

    def body(xbc_ref, z_ref, dt_ref, par_ref, dsk_ref, nw_ref, yraw_ref, ycat_ref, hprev_ref, h_ref):
        @pl.when(pl.program_id(0) % nc == 0)
        def _():
            h_ref[...] = jnp.zeros_like(h_ref)
        hprev_ref[0] = h_ref[...]
        _, _, cs, cst, causal, _, dt_c, ecs_c, w_c, pair_cols = _ssd_prep(dt_ref[...], par_ref[...])
        cs_last = cs[CHUNK - 1:CHUNK, :]
        causal2 = jnp.concatenate([causal, causal], axis=1)
        left = lax.broadcasted_iota(jnp.int32, (CHUNK, 128), 1) < HEADDIM
        x = xbc_ref[:, 0:D_SSD]
        xdt = x * dt_c
        amat = (w_c * xdt).astype(BF16)
        zz = z_ref[...]
        silu_z = zz * _sigmoid(zz)
        for g in range(N_GROUPS):
            gs = slice(g * GW, (g + 1) * GW)
            bg = xbc_ref[:, D_SSD + g * N_STATE:D_SSD + (g + 1) * N_STATE].astype(BF16)
            cg = xbc_ref[:, D_SSD + (N_GROUPS + g) * N_STATE:D_SSD + (N_GROUPS + g + 1) * N_STATE].astype(BF16)
            scores = lax.dot_general(cg, bg, NT, preferred_element_type=F32)
            scores2 = jnp.concatenate([scores, scores], axis=1)
            hg = h_ref[gs, :]
            p_all = lax.dot_general(cg, hg.astype(BF16), NT, preferred_element_type=F32)
            ys = []
            for q in range(GW // 128):
                pair = g * (GW // 128) + q
                decay = _pair_decay(pair_cols[pair], cst, pair, causal2)
                mcat = (scores2 * decay).astype(BF16)
                ys.append(jnp.dot(mcat, _stack_heads(xdt[:, pair * 128:(pair + 1) * 128], left),
                                  preferred_element_type=F32))
            yg = jnp.concatenate(ys, axis=1) + ecs_c[:, gs] * p_all + x[:, gs] * dsk_ref[:, gs]
            s_new = lax.dot_general(amat[:, gs], bg, TN, preferred_element_type=F32)
            for j in range(HPG):
                hh = g * HPG + j
                js = slice(j * HEADDIM, (j + 1) * HEADDIM)
                h_ref[g * GW + j * HEADDIM:g * GW + (j + 1) * HEADDIM, :] = (
                    hg[js, :] * jnp.exp(cs_last[:, hh:hh + 1]) + s_new[js, :])
            yraw_ref[:, gs] = yg
            v = yg * silu_z[:, gs]
            r = lax.rsqrt(jnp.mean(v * v, axis=-1, keepdims=True) + EPS)
            ycat_ref[:, gs] = (v * r * nw_ref[:, gs]).astype(BF16)

    row = lambda w: pl.BlockSpec((CHUNK, w), lambda i: (i, 0))
    full = lambda s: pl.BlockSpec(s, lambda i: (0,) * len(s))
    return pl.pallas_call(
        body, name="ssd_fwd", grid=(n_chunks,),
        out_shape=(jax.ShapeDtypeStruct((t, D_SSD), F32), jax.ShapeDtypeStruct((t, D_SSD + D_S5), BF16),
                   jax.ShapeDtypeStruct((n_chunks, D_SSD, N_STATE), F32)),
        in_specs=[row(D_XBC), row(D_SSD), row(DT_PAD), full((8, 128)), full((1, D_SSD)), full((1, D_SSD))],
        out_specs=(row(D_SSD), row(D_SSD), pl.BlockSpec((1, D_SSD, N_STATE), lambda i: (i, 0, 0))),
        scratch_shapes=[pltpu.VMEM((D_SSD, N_STATE), F32)],
        compiler_params=_params(40),
    )(xbc, z, dt_raw, par, dsk, normw)


S5_CW = 512
S5_BLOCKS = 4


def _tile_scan(in_re, in_im, out_re, out_im, carry_re, carry_im, pw_re, pw_im, n_tiles, reverse):
    steps = (1, 2, 4)
    for cc in range(S5_N // S5_CW):
        cols = slice(cc * S5_CW, (cc + 1) * S5_CW)
        a_re, a_im = pw_re[:, cols], pw_im[:, cols]
        rid = lax.broadcasted_iota(jnp.int32, (8, S5_CW), 0)
        pows = []
        for d in steps:
            k = 8 - d if reverse else d - 1
            keep = (rid < 8 - d) if reverse else (rid >= d)
            pows.append((jnp.where(keep, pw_re[k:k + 1, cols], 0.0), jnp.where(keep, pw_im[k:k + 1, cols], 0.0)))

        def tile(i, carry, cols=cols, pows=pows, a_re=a_re, a_im=a_im):
            r = (n_tiles - 1 - i) if reverse else i
            rows = pl.ds(pl.multiple_of(r * 8, 8), 8)
            xr, xi = in_re[rows, cols], in_im[rows, cols]
            for (pr, pi), d in zip(pows, steps):
                shift = 8 - d if reverse else d
                sr, si = pltpu.roll(xr, shift, axis=0), pltpu.roll(xi, shift, axis=0)
                xr, xi = xr + pr * sr - pi * si, xi + pr * si + pi * sr
            cr, ci = carry
            xr, xi = xr + a_re * cr - a_im * ci, xi + a_re * ci + a_im * cr
            out_re[rows, cols] = xr
            out_im[rows, cols] = xi
            edge = slice(0, 1) if reverse else slice(7, 8)
            return (jnp.broadcast_to(xr[edge], (8, S5_CW)), jnp.broadcast_to(xi[edge], (8, S5_CW)))

        c0 = (jnp.broadcast_to(carry_re[0:1, cols], (8, S5_CW)), jnp.broadcast_to(carry_im[0:1, cols], (8, S5_CW)))
        cr, ci = lax.fori_loop(0, n_tiles, tile, c0, unroll=True)
        carry_re[:, cols] = cr
        carry_im[:, cols] = ci


def _s5_params_math(ar, ai, ldt, br, bi):
    dt = jnp.exp(ldt)
    mag = jnp.exp(ar * dt)
    ang = ai * dt
    ab_re = mag * jnp.cos(ang)
    ab_im = mag * jnp.sin(ang)
    den = ar * ar + ai * ai
    n_re = ab_re - 1.0
    coef_re = (n_re * ar + ab_im * ai) / den
    coef_im = (ab_im * ar - n_re * ai) / den
    bb_re = coef_re * br - coef_im * bi
    bb_im = coef_re * bi + coef_im * br
    return ab_re, ab_im, bb_re, bb_im


def _s5_params_fwd(ar, ai, ldt, br, bi):
    def body(ar_ref, ai_ref, ldt_ref, br_ref, bi_ref, bbr_ref, bbi_ref, pfr_ref, pfi_ref, prr_ref, pri_ref):
        ab_re, ab_im, bb_re, bb_im = _s5_params_math(ar_ref[...], ai_ref[...], ldt_ref[...], br_ref[...], bi_ref[...])
        bbr_ref[...] = bb_re
        bbi_ref[...] = bb_im
        pr, pi = ab_re, ab_im
        for k in range(8):
            pfr_ref[k:k + 1, :] = pr
            pfi_ref[k:k + 1, :] = pi
            prr_ref[7 - k:8 - k, :] = pr
            pri_ref[7 - k:8 - k, :] = -pi
            pr, pi = pr * ab_re - pi * ab_im, pr * ab_im + pi * ab_re

    b16 = jax.ShapeDtypeStruct((S5_CH, S5_N), F32)
    p8 = jax.ShapeDtypeStruct((8, S5_N), F32)
    return pl.pallas_call(body, name="s5_params_fwd", out_shape=(b16, b16, p8, p8, p8, p8),
                          compiler_params=_params(32))(ar, ai, ldt, br, bi)


def _s5_params_bwd(ar, ai, ldt, br, bi, d_ab_re, d_ab_im, d_bb_re, d_bb_im):
    def body(ar_ref, ai_ref, ldt_ref, br_ref, bi_ref, dar_ref, dai_ref, dbr_ref, dbi_ref,
             gar_ref, gai_ref, gldt_ref, gbr_ref, gbi_ref):
        _, vjp = jax.vjp(_s5_params_math, ar_ref[...], ai_ref[...], ldt_ref[...], br_ref[...], bi_ref[...])
        g_ar, g_ai, g_ldt, g_br, g_bi = vjp((dar_ref[...], dai_ref[...], dbr_ref[...], dbi_ref[...]))
        gar_ref[...] = g_ar
        gai_ref[...] = g_ai
        gbr_ref[...] = g_br
        gbi_ref[...] = g_bi
        lane = lax.broadcasted_iota(jnp.int32, (S5_N, 128), 0) // S5_P
        grp = lax.broadcasted_iota(jnp.int32, (S5_N, 128), 1)
        fold = (lane == grp).astype(F32)
        gldt_ref[...] = jnp.dot(g_ldt, fold, preferred_element_type=F32, precision=HIGHEST)

    v1 = jax.ShapeDtypeStruct((1, S5_N), F32)
    b16 = jax.ShapeDtypeStruct((S5_CH, S5_N), F32)
    return pl.pallas_call(body, name="s5_params_bwd",
                          out_shape=(v1, v1, jax.ShapeDtypeStruct((1, 128), F32), b16, b16),
                          compiler_params=_params(32))(ar, ai, ldt, br, bi, d_ab_re, d_ab_im, d_bb_re, d_bb_im)


def _s5_fwd(u5, bb_re, bb_im, cc_re, cc_im, pf_re, pf_im, s5d, w_glu, b_glu, ycat, seq):
    t = u5.shape[0]
    tb = 256
    npb = seq // tb

    def body(u_ref, bbr_ref, bbi_ref, ccr_ref, cci_ref, pfr_ref, pfi_ref, d_ref, wg_ref, bg_ref, ycat_hbm,
             sre_ref, sim_ref, ypre_ref, y5_ref, bur, bui, car, cai):
        del ycat_hbm

        @pl.when(pl.program_id(0) % npb == 0)
        def _():
            car[...] = jnp.zeros_like(car)
            cai[...] = jnp.zeros_like(cai)
        u = u_ref[...]
        ub = u.astype(BF16)
        for j in range(S5_BLOCKS):
            ch, st = slice(j * 128, (j + 1) * 128), slice(j * 512, (j + 1) * 512)
            bur[:, st] = jnp.dot(ub[:, ch], bbr_ref[j], preferred_element_type=F32)
            bui[:, st] = jnp.dot(ub[:, ch], bbi_ref[j], preferred_element_type=F32)
        _tile_scan(bur, bui, sre_ref, sim_ref, car, cai, pfr_ref, pfi_ref, tb // 8, reverse=False)
        cs_y = []
        for j in range(S5_BLOCKS):
            st = slice(j * 512, (j + 1) * 512)
            cs_y.append(_mm(sre_ref[:, st], ccr_ref[j]) - _mm(sim_ref[:, st], cci_ref[j]))
        ypre = jnp.concatenate(cs_y, axis=1) + u * d_ref[...]
        ypre_ref[...] = ypre
        yg = _gelu(ypre)
        y5_ref[...] = (yg * _sigmoid(_mm(yg, wg_ref[...]) + bg_ref[...])).astype(BF16)

    row = lambda w: pl.BlockSpec((tb, w), lambda i: (i, 0))
    full = lambda a: pl.BlockSpec(a.shape, lambda i: (0,) * a.ndim)
    return pl.pallas_call(
        body, name="s5_fwd", grid=(t // tb,),
        out_shape=(jax.ShapeDtypeStruct((t, S5_N), F32), jax.ShapeDtypeStruct((t, S5_N), F32),
                   jax.ShapeDtypeStruct((t, D_S5), F32), jax.ShapeDtypeStruct(ycat.shape, BF16)),
        in_specs=[row(D_S5), full(bb_re), full(bb_im), full(cc_re), full(cc_im), full(pf_re), full(pf_im),
                  full(s5d), full(w_glu), full(b_glu), ANY],
        out_specs=(row(S5_N), row(S5_N), row(D_S5), pl.BlockSpec((tb, D_S5), lambda i: (i, D_SSD // D_S5))),
        input_output_aliases={10: 3},
        scratch_shapes=[pltpu.VMEM((tb, S5_N), F32), pltpu.VMEM((tb, S5_N), F32),
                        pltpu.VMEM((8, S5_N), F32), pltpu.VMEM((8, S5_N), F32)],
        compiler_params=_params(48),
    )(u5, bb_re, bb_im, cc_re, cc_im, pf_re, pf_im, s5d, w_glu, b_glu, ycat)


def _layer_norm(r, g, b):
    mu = jnp.mean(r, axis=-1, keepdims=True)
    xc = r - mu
    rstd = lax.rsqrt(jnp.mean(xc * xc, axis=-1, keepdims=True) + EPS)
    xhat = xc * rstd
    return xhat * g + b, xhat, rstd


def _layer_norm_bwd(dy, xhat, rstd, g):
    dxhat = dy * g
    return rstd * (dxhat - jnp.mean(dxhat, axis=-1, keepdims=True)
                   - xhat * jnp.mean(dxhat * xhat, axis=-1, keepdims=True))


def _out_ln1(ycat, x2, mod3, w_out, ln1, seq):
    t = x2.shape[0]
    tb = 512
    npb = seq // tb

    def body(y_ref, x_ref, mod_ref, w_ref, ln_ref, mix_ref, x1_ref):
        m = mod_ref[0]
        mix = jnp.dot(y_ref[...], w_ref[...], preferred_element_type=F32)
        mix_ref[...] = mix
        r1 = ALPHA * x_ref[...] + (1.0 + m[2:3]) * mix
        x1_ref[...] = _layer_norm(r1, ln_ref[0:1], ln_ref[1:2])[0]

    row = lambda w: pl.BlockSpec((tb, w), lambda i: (i, 0))
    return pl.pallas_call(
        body, name="out_ln1", grid=(t // tb,),
        out_shape=(jax.ShapeDtypeStruct((t, D_MODEL), F32), jax.ShapeDtypeStruct((t, D_MODEL), F32)),
        in_specs=[row(D_SSD + D_S5), row(D_MODEL), pl.BlockSpec((1, N_MOD, D_MODEL), lambda i: (i // npb, 0, 0)),
                  pl.BlockSpec(w_out.shape, lambda i: (0, 0)), pl.BlockSpec(ln1.shape, lambda i: (0, 0))],
        out_specs=(row(D_MODEL), row(D_MODEL)), compiler_params=_params(48),
    )(ycat, x2, mod3, w_out, ln1)


def _mlp_fwd_bwd(x1, tgt, mod3, w1, w2, vec1, b1, seq):
    t = x1.shape[0]
    tb = 256
    npb = seq // tb
    n_fb, _, fb = w1.shape

    def body(x1_ref, tgt_ref, mod_ref, w1_hbm, w2_hbm, v_ref, b1_ref,
             dx1_ref, u2_ref, h_ref, dhp_ref, do_ref, gacc_ref, db1_ref, bacc_ref, w1_v, w2_v, sem1, sem2):
        i = pl.program_id(0)
        @pl.when(i == 0)
        def _():
            cps = [pltpu.make_async_copy(w1_hbm.at[k], w1_v.at[:, k * fb:(k + 1) * fb], sem1.at[k])
                   for k in range(n_fb)]
            for cp in cps:
                cp.start()
            for cp in cps:
                cp.wait()
        _load_once(w2_hbm, w2_v, sem2)

        @pl.when(i == 0)
        def _():
            gacc_ref[...] = jnp.zeros_like(gacc_ref)
            db1_ref[...] = jnp.zeros_like(db1_ref)

        @pl.when(i % npb == 0)
        def _():
            bacc_ref[...] = jnp.zeros_like(bacc_ref)

        m = mod_ref[0]
        sh2, sc2, g2 = m[3:4], m[4:5], m[5:6]
        x1v = x1_ref[...]
        u2 = (x1v * (1.0 + sc2) + sh2).astype(BF16)
        u2_ref[...] = u2
        hr = jnp.maximum(jnp.dot(u2, w1_v[...], preferred_element_type=F32) + b1_ref[...], 0.0)
        hb = (hr * hr).astype(BF16)
        h_ref[...] = hb
        o = jnp.dot(hb, w2_v[...], preferred_element_type=F32) + v_ref[0:1]
        r2 = ALPHA * x1v + (1.0 + g2) * o
        y, xhat, rstd = _layer_norm(r2, v_ref[1:2], v_ref[2:3])
        err = y - tgt_ref[...]
        dy = err * (1.0 / D_MODEL)
        dr2 = _layer_norm_bwd(dy, xhat, rstd, v_ref[1:2])
        do = (1.0 + g2) * dr2
        dob = do.astype(BF16)
        do_ref[...] = dob
        gacc_ref[0:1, :] += jnp.sum(dy * xhat, axis=0, keepdims=True)
        gacc_ref[1:2, :] += jnp.sum(dy, axis=0, keepdims=True)
        gacc_ref[2:3, :] += jnp.sum(do, axis=0, keepdims=True)
        gacc_ref[3:4, :] += jnp.sum(err * err, axis=0, keepdims=True)
        dhpre = lax.dot_general(dob, w2_v[...], NT, preferred_element_type=F32) * (2.0 * hr)
        dhpb = dhpre.astype(BF16)
        dhp_ref[...] = dhpb
        db1_ref[...] += jnp.sum(dhpre, axis=0, keepdims=True)
        du2 = lax.dot_general(dhpb, w1_v[...], NT, preferred_element_type=F32)
        dx1_ref[...] = ALPHA * dr2 + du2 * (1.0 + sc2)
        bacc_ref[0, 0:1, :] += jnp.sum(du2, axis=0, keepdims=True)
        bacc_ref[0, 1:2, :] += jnp.sum(du2 * x1v, axis=0, keepdims=True)
        bacc_ref[0, 2:3, :] += jnp.sum(dr2 * o, axis=0, keepdims=True)

    row = lambda w: pl.BlockSpec((tb, w), lambda i: (i, 0))
    return pl.pallas_call(
        body, name="mlp_fwd_bwd", grid=(t // tb,),
        out_shape=(jax.ShapeDtypeStruct((t, D_MODEL), F32), jax.ShapeDtypeStruct((t, D_MODEL), BF16),
                   jax.ShapeDtypeStruct((t, D_FF), BF16), jax.ShapeDtypeStruct((t, D_FF), BF16),
                   jax.ShapeDtypeStruct((t, D_MODEL), BF16), jax.ShapeDtypeStruct((8, D_MODEL), F32),
                   jax.ShapeDtypeStruct((1, D_FF), F32), jax.ShapeDtypeStruct((t // seq, 8, D_MODEL), F32)),
        in_specs=[row(D_MODEL), row(D_MODEL), pl.BlockSpec((1, N_MOD, D_MODEL), lambda i: (i // npb, 0, 0)), ANY, ANY,
                  pl.BlockSpec(vec1.shape, lambda i: (0, 0)), pl.BlockSpec(b1.shape, lambda i: (0, 0))],
        out_specs=(row(D_MODEL), row(D_MODEL), row(D_FF), row(D_FF), row(D_MODEL),
                   pl.BlockSpec((8, D_MODEL), lambda i: (0, 0)), pl.BlockSpec((1, D_FF), lambda i: (0, 0)),
                   pl.BlockSpec((1, 8, D_MODEL), lambda i: (i // npb, 0, 0))),
        scratch_shapes=[pltpu.VMEM((D_MODEL, n_fb * fb), BF16), pltpu.VMEM((D_FF, D_MODEL), BF16),
                        pltpu.SemaphoreType.DMA((n_fb,)), pltpu.SemaphoreType.DMA((LOAD_CHUNKS,))],
        compiler_params=_params(60),
    )(x1, tgt, mod3, w1, w2, vec1, b1)


def _ln1_out_bwd(dx1, x2, mix, mod3, w_out, ln1, seq):
    t = x2.shape[0]
    tb = 512
    npb = seq // tb

    def body(dx1_ref, x_ref, mix_ref, mod_ref, w_ref, ln_ref, dmix_ref, dxa_ref, dys_ref, dy5_ref, gacc_ref, bacc_ref):
        i = pl.program_id(0)

        @pl.when(i == 0)
        def _():
            gacc_ref[...] = jnp.zeros_like(gacc_ref)

        @pl.when(i % npb == 0)
        def _():
            bacc_ref[...] = jnp.zeros_like(bacc_ref)

        m = mod_ref[0]
        mix = mix_ref[...]
        r1 = ALPHA * x_ref[...] + (1.0 + m[2:3]) * mix
        _, xhat, rstd = _layer_norm(r1, ln_ref[0:1], ln_ref[1:2])
        dx1v = dx1_ref[...]
        dr1 = _layer_norm_bwd(dx1v, xhat, rstd, ln_ref[0:1])
        gacc_ref[0:1, :] += jnp.sum(dx1v * xhat, axis=0, keepdims=True)
        gacc_ref[1:2, :] += jnp.sum(dx1v, axis=0, keepdims=True)
        bacc_ref[0, 0:1, :] += jnp.sum(dr1 * mix, axis=0, keepdims=True)
        dmix = ((1.0 + m[2:3]) * dr1).astype(BF16)
        dmix_ref[...] = dmix
        dxa_ref[...] = ALPHA * dr1
        dys_ref[...] = lax.dot_general(dmix, w_ref[0:D_SSD, :], NT, preferred_element_type=F32)
        dy5_ref[...] = lax.dot_general(dmix, w_ref[D_SSD:, :], NT, preferred_element_type=F32)

    row = lambda w: pl.BlockSpec((tb, w), lambda i: (i, 0))
    return pl.pallas_call(
        body, name="ln1_out_bwd", grid=(t // tb,),
        out_shape=(jax.ShapeDtypeStruct((t, D_MODEL), BF16), jax.ShapeDtypeStruct((t, D_MODEL), F32),
                   jax.ShapeDtypeStruct((t, D_SSD), F32), jax.ShapeDtypeStruct((t, D_S5), F32),
                   jax.ShapeDtypeStruct((8, D_MODEL), F32), jax.ShapeDtypeStruct((t // seq, 8, D_MODEL), F32)),
        in_specs=[row(D_MODEL), row(D_MODEL), row(D_MODEL), pl.BlockSpec((1, N_MOD, D_MODEL), lambda i: (i // npb, 0, 0)),
                  pl.BlockSpec(w_out.shape, lambda i: (0, 0)), pl.BlockSpec(ln1.shape, lambda i: (0, 0))],
        out_specs=(row(D_MODEL), row(D_MODEL), row(D_SSD), row(D_S5), pl.BlockSpec((8, D_MODEL), lambda i: (0, 0)),
                   pl.BlockSpec((1, 8, D_MODEL), lambda i: (i // npb, 0, 0))),
        compiler_params=_params(48),
    )(dx1, x2, mix, mod3, w_out, ln1)


def _s5_bwd(dy5, ypre, u5, s_re, s_im, bb_re, bb_im, cc_re, cc_im, pr_re, pr_im, s5d, w_glu, b_glu, seq):
    t = u5.shape[0]
    tb = 256
    npb = seq // tb
    n_blocks = t // tb

    def blk(i):
        return (i // npb) * npb + (npb - 1 - i % npb)

    def body(dy_ref, ypre_ref, u_ref, sre_ref, sim_ref, hre_ref, him_ref, bbr_ref, bbi_ref, ccr_ref, cci_ref,
             prr_ref, pri_ref, d_ref, wg_ref, bg_ref,
             du_ref, vacc_ref, sacc_ref, dcc_ref, dbb_ref, dwg_ref, dsr, dsi, gr, gi, car, cai):
        i = pl.program_id(0)

        @pl.when(i == 0)
        def _():
            for acc in (vacc_ref, sacc_ref, dcc_ref, dbb_ref, dwg_ref):
                acc[...] = jnp.zeros_like(acc)

        @pl.when(i % npb == 0)
        def _():
            car[...] = jnp.zeros_like(car)
            cai[...] = jnp.zeros_like(cai)

        dy = dy_ref[...]
        ypre = ypre_ref[...]
        u = u_ref[...]
        ub = u.astype(BF16)
        yg = _gelu(ypre)
        sg = _sigmoid(_mm(yg, wg_ref[...]) + bg_ref[...])
        dq = dy * yg * sg * (1.0 - sg)
        dqb = dq.astype(BF16)
        dyg = dy * sg + lax.dot_general(dqb, wg_ref[...], NT, preferred_element_type=F32)
        dyp = dyg * _gelu_grad(ypre)
        dypb = dyp.astype(BF16)
        dwg_ref[...] += lax.dot_general(yg.astype(BF16), dqb, TN, preferred_element_type=F32)
        blocks = [(slice(j * 128, (j + 1) * 128), slice(j * 512, (j + 1) * 512)) for j in range(S5_BLOCKS)]
        for j, (ch, st) in enumerate(blocks):
            dsr[:, st] = lax.dot_general(dypb[:, ch], ccr_ref[j], NT, preferred_element_type=F32)
            dsi[:, st] = -lax.dot_general(dypb[:, ch], cci_ref[j], NT, preferred_element_type=F32)
        _tile_scan(dsr, dsi, gr, gi, car, cai, prr_ref, pri_ref, tb // 8, reverse=True)
        g_re, g_im = gr[...], gi[...]
        first_rows = (i % npb) == npb - 1
        hre = jnp.where(first_rows, 0.0, hre_ref[...])
        him = jnp.where(first_rows, 0.0, him_ref[...])
        s_re_v, s_im_v = sre_ref[...], sim_ref[...]
        sp_re = pltpu.roll(jnp.concatenate([hre, s_re_v], axis=0), 1, axis=0)[8:8 + tb]
        sp_im = pltpu.roll(jnp.concatenate([him, s_im_v], axis=0), 1, axis=0)[8:8 + tb]
        vacc_ref[0:1, :] += jnp.sum(g_re * sp_re + g_im * sp_im, axis=0, keepdims=True)
        vacc_ref[1:2, :] += jnp.sum(g_im * sp_re - g_re * sp_im, axis=0, keepdims=True)
        grb, gib = g_re.astype(BF16), g_im.astype(BF16)
        srb, sib = s_re_v.astype(BF16), s_im_v.astype(BF16)
        du_cols = []
        for j, (ch, st) in enumerate(blocks):
            dcc_ref[j] += lax.dot_general(srb[:, st], dypb[:, ch], TN, preferred_element_type=F32)
            dcc_ref[S5_BLOCKS + j] -= lax.dot_general(sib[:, st], dypb[:, ch], TN, preferred_element_type=F32)
            dbb_ref[j] += lax.dot_general(ub[:, ch], grb[:, st], TN, preferred_element_type=F32)
            dbb_ref[S5_BLOCKS + j] += lax.dot_general(ub[:, ch], gib[:, st], TN, preferred_element_type=F32)
            du_cols.append(lax.dot_general(grb[:, st], bbr_ref[j], NT, preferred_element_type=F32)
                           + lax.dot_general(gib[:, st], bbi_ref[j], NT, preferred_element_type=F32))
        du_ref[...] = jnp.concatenate(du_cols, axis=1) + dyp * d_ref[...]
        sacc_ref[0:1, :] += jnp.sum(dyp * u, axis=0, keepdims=True)
        sacc_ref[1:2, :] += jnp.sum(dq, axis=0, keepdims=True)

    row = lambda w: pl.BlockSpec((tb, w), lambda i: (blk(i), 0))
    halo = pl.BlockSpec((8, S5_N), lambda i: (jnp.maximum(blk(i) * (tb // 8) - 1, 0), 0))
    full = lambda a: pl.BlockSpec(a.shape, lambda i: (0,) * a.ndim)
    acc = lambda s: pl.BlockSpec(s, lambda i: (0,) * len(s))
    acc_shapes = [(8, S5_N), (8, D_S5), (2 * S5_BLOCKS, 512, 128), (2 * S5_BLOCKS, 128, 512), (D_S5, D_S5)]
    return pl.pallas_call(
        body, name="s5_bwd", grid=(n_blocks,),
        out_shape=(jax.ShapeDtypeStruct((t, D_S5), F32),) + tuple(jax.ShapeDtypeStruct(s, F32) for s in acc_shapes),
        in_specs=[row(D_S5), row(D_S5), row(D_S5), row(S5_N), row(S5_N), halo, halo, full(bb_re), full(bb_im),
                  full(cc_re), full(cc_im), full(pr_re), full(pr_im), full(s5d), full(w_glu), full(b_glu)],
        out_specs=(row(D_S5),) + tuple(acc(s) for s in acc_shapes),
        scratch_shapes=[pltpu.VMEM((tb, S5_N), F32), pltpu.VMEM((tb, S5_N), F32), pltpu.VMEM((tb, S5_N), F32),
                        pltpu.VMEM((tb, S5_N), F32), pltpu.VMEM((8, S5_N), F32), pltpu.VMEM((8, S5_N), F32)],
        compiler_params=_params(56),
    )(dy5, ypre, u5, s_re, s_im, s_re, s_im, bb_re, bb_im, cc_re, cc_im, pr_re, pr_im, s5d, w_glu, b_glu)


def _ssd_bwd(dyssd, yraw, z, xbc, dt_raw, hprev, par, dsk, normw, seq):
    t = xbc.shape[0]
    nc = seq // CHUNK
    n_chunks = t // CHUNK
    fold = _head_fold()

    def blk(i):
        return (i // nc) * nc + (nc - 1 - i % nc)

    def body(dy_ref, yraw_ref, z_ref, xbc_ref, dt_ref, hprev_ref, par_ref, dsk_ref, nw_ref, fold_ref,
             dxbc_ref, dz_ref, ddt_ref, dpar_ref, cacc_ref, dh_ref, dyr_ref):
        i = pl.program_id(0)

        @pl.when(i == 0)
        def _():
            dpar_ref[...] = jnp.zeros_like(dpar_ref)
            cacc_ref[...] = jnp.zeros_like(cacc_ref)

        @pl.when(i % nc == 0)
        def _():
            dh_ref[...] = jnp.zeros_like(dh_ref)

        zz = z_ref[...]
        sz = _sigmoid(zz)
        silu_z = zz * sz
        yraw = yraw_ref[...]
        for g in range(N_GROUPS):
            sl = slice(g * GW, (g + 1) * GW)
            v = yraw[:, sl] * silu_z[:, sl]
            r = lax.rsqrt(jnp.mean(v * v, axis=-1, keepdims=True) + EPS)
            dyg = dy_ref[:, sl]
            cacc_ref[1:2, sl] += jnp.sum(dyg * v * r, axis=0, keepdims=True)
            dyw = dyg * nw_ref[:, sl]
            dv = r * dyw - v * (r * r * r) * jnp.mean(dyw * v, axis=-1, keepdims=True)
            dyr_ref[:, sl] = dv * silu_z[:, sl]
            dz_ref[:, sl] = dv * yraw[:, sl] * (sz[:, sl] * (1.0 + zz[:, sl] * (1.0 - sz[:, sl])))

        dt, a, cs, cst, causal, tri, dt_c, ecs_c, w_c, pair_cols = _ssd_prep(dt_ref[...], par_ref[...])
        cs_last = cs[CHUNK - 1:CHUNK, :]
        causal2 = jnp.concatenate([causal, causal], axis=1)
        lane = lax.broadcasted_iota(jnp.int32, (CHUNK, 128), 1)
        left = lane < HEADDIM
        lane1 = lax.broadcasted_iota(jnp.int32, (1, 128), 1)
        x = xbc_ref[:, 0:D_SSD]
        xdt = x * dt_c
        dyr = dyr_ref[...]
        dyrb = dyr.astype(BF16)
        cacc_ref[0:1, :] += jnp.sum(dyr * x, axis=0, keepdims=True)
        dlast = jnp.zeros((1, 128), F32)
        dxdt_cols, diag_all, dww_cols = [], [], []
        for g in range(N_GROUPS):
            gs = slice(g * GW, (g + 1) * GW)
            b_sl = slice(D_SSD + g * N_STATE, D_SSD + (g + 1) * N_STATE)
            c_sl = slice(D_SSD + (N_GROUPS + g) * N_STATE, D_SSD + (N_GROUPS + g + 1) * N_STATE)
            bg = xbc_ref[:, b_sl].astype(BF16)
            cg = xbc_ref[:, c_sl].astype(BF16)
            scores = lax.dot_general(cg, bg, NT, preferred_element_type=F32)
            scores2 = jnp.concatenate([scores, scores], axis=1)
            hg = hprev_ref[0, gs, :]
            hgb = hg.astype(BF16)
            dhg = dh_ref[gs, :]
            dhgb = dhg.astype(BF16)
            q_all = lax.dot_general(bg, dhgb, NT, preferred_element_type=F32)
            dscores = jnp.zeros((CHUNK, CHUNK), F32)
            diag_cols = []
            for q in range(GW // 128):
                pair = g * (GW // 128) + q
                ps = slice(pair * 128, (pair + 1) * 128)
                decay = _pair_decay(pair_cols[pair], cst, pair, causal2)
                mcat = (scores2 * decay).astype(BF16)
                dyp = dyrb[:, ps]
                dm = lax.dot_general(dyp, _stack_heads(xdt[:, ps], left), NT, preferred_element_type=F32)
                dmd = dm * decay
                dscores = dscores + dmd[:, 0:CHUNK] + dmd[:, CHUNK:]
                rr = lax.dot_general(mcat, dyp, TN, preferred_element_type=F32)
                diag_cols.append(jnp.where(left, rr[0:CHUNK], rr[CHUNK:]))
            wq = w_c[:, gs] * q_all
            diag_g = jnp.concatenate(diag_cols, axis=1)
            diag_all.append(diag_g)
            dxdt_cols.append(diag_g + wq)
            dww_cols.append(wq * xdt[:, gs])
            dp = (ecs_c[:, gs] * dyr[:, gs]).astype(BF16)
            amat = (w_c[:, gs] * xdt[:, gs]).astype(BF16)
            dsb = dscores.astype(BF16)
            dxbc_ref[:, c_sl] = (jnp.dot(dsb, bg, preferred_element_type=F32)
                                 + jnp.dot(dp, hgb, preferred_element_type=F32))
            dxbc_ref[:, b_sl] = (lax.dot_general(dsb, cg, TN, preferred_element_type=F32)
                                 + jnp.dot(amat, dhgb, preferred_element_type=F32))
            dh_in = lax.dot_general(dp, cg, TN, preferred_element_type=F32)
            for j in range(HPG):
                hh = g * HPG + j
                js = slice(j * HEADDIM, (j + 1) * HEADDIM)
                ecl = jnp.exp(cs_last[:, hh:hh + 1])
                dlast = dlast + jnp.where(lane1 == hh, ecl * jnp.sum(dhg[js, :] * hg[js, :]), 0.0)
                dh_ref[g * GW + j * HEADDIM:g * GW + (j + 1) * HEADDIM, :] = ecl * dhg[js, :] + dh_in[js, :]
        dxdt = jnp.concatenate(dxdt_cols, axis=1)
        dxbc_ref[:, 0:D_SSD] = dxdt * dt_c + dyr * dsk_ref[...]
        dww = _mm(jnp.concatenate(dww_cols, axis=1), fold_ref[...])
        dcs = _dot3(dyrb.astype(F32) * (yraw - x * dsk_ref[...])
                    - xdt.astype(BF16).astype(F32) * jnp.concatenate(diag_all, axis=1), fold_ref[...]) - dww
        rowid = lax.broadcasted_iota(jnp.int32, (CHUNK, 128), 0)
        dcs = dcs + jnp.where(rowid == CHUNK - 1, jnp.sum(dww, axis=0, keepdims=True) + dlast, 0.0)
        dadt = _dot3_left(tri, dcs, TN)
        ddt = _mm(dxdt * x, fold_ref[...]) + dadt * a
        da = jnp.sum(dadt * dt, axis=0, keepdims=True)
        ddt_raw = ddt * _sigmoid(dt_ref[...] + par_ref[0:1])
        ddt_raw = jnp.where(lane < N_HEADS, ddt_raw, 0.0)
        ddt_ref[...] = ddt_raw
        dpar_ref[0:1, :] += jnp.sum(ddt_raw, axis=0, keepdims=True)
        dpar_ref[1:2, :] += jnp.where(lane1 < N_HEADS, da * a, 0.0)

    row = lambda w: pl.BlockSpec((CHUNK, w), lambda i: (blk(i), 0))
    full = lambda s: pl.BlockSpec(s, lambda i: (0,) * len(s))
    return pl.pallas_call(
        body, name="ssd_bwd", grid=(n_chunks,),
        out_shape=(jax.ShapeDtypeStruct((t, D_XBC), F32), jax.ShapeDtypeStruct((t, D_SSD), F32),
                   jax.ShapeDtypeStruct((t, DT_PAD), F32), jax.ShapeDtypeStruct((8, 128), F32),
                   jax.ShapeDtypeStruct((8, D_SSD), F32)),
        in_specs=[row(D_SSD), row(D_SSD), row(D_SSD), row(D_XBC), row(DT_PAD),
                  pl.BlockSpec((1, D_SSD, N_STATE), lambda i: (blk(i), 0, 0)),
                  full((8, 128)), full((1, D_SSD)), full((1, D_SSD)), full(fold.shape)],
        out_specs=(row(D_XBC), row(D_SSD), row(DT_PAD), full((8, 128)), full((8, D_SSD))),
        scratch_shapes=[pltpu.VMEM((D_SSD, N_STATE), F32), pltpu.VMEM((CHUNK, D_SSD), F32)],
        compiler_params=_params(48),
    )(dyssd, yraw, z, xbc, dt_raw, hprev, par, dsk, normw, fold)


def _conv_bwd(dxbc, dsilu, xbc_pre, seq):
    t = xbc_pre.shape[0]
    tb = 512
    npb = seq // tb
    cw = 640

    def body(d_ref, ds_ref, cur_ref, halo_ref, o_ref, acc_ref, win):
        i = pl.program_id(1)

        @pl.when(i == 0)
        def _():
            acc_ref[...] = jnp.zeros_like(acc_ref)

        first = (i % npb) == 0
        win[0:8, :] = jnp.where(first, 0.0, halo_ref[...])
        win[8:8 + tb, :] = cur_ref[...]
        dpre = d_ref[...] * ds_ref[...]
        o_ref[...] = dpre
        for j in range(4):
            acc_ref[3 - j:4 - j, :] += jnp.sum(dpre * win[8 - j:8 - j + tb, :], axis=0, keepdims=True)
        acc_ref[4:5, :] += jnp.sum(dpre, axis=0, keepdims=True)

    blk = pl.BlockSpec((tb, cw), lambda j, i: (i, j))
    return pl.pallas_call(
        body, name="conv_bwd", grid=(D_XBC // cw, t // tb),
        out_shape=(jax.ShapeDtypeStruct((t, D_XBC), F32), jax.ShapeDtypeStruct((8, D_XBC), F32)),
        in_specs=[blk, blk, blk, pl.BlockSpec((8, cw), lambda j, i: (jnp.maximum(i * (tb // 8) - 1, 0), j))],
        out_specs=(blk, pl.BlockSpec((8, cw), lambda j, i: (0, j))),
        scratch_shapes=[pltpu.VMEM((tb + 8, cw), F32)],
        compiler_params=_params(32),
    )(dxbc, dsilu, xbc_pre, xbc_pre)


def _proj_bwd(dz, dpre, ddt, du5, x2, dxa, mod3, conv_w, w_in_pad, seq):
    t = x2.shape[0]
    tb = 512
    npb = seq // tb
    n_blocks = t // tb

    def body(dz_ref, dp_ref, nxt_ref, ddt_ref, du5_ref, x_ref, dxa_ref, mod_ref, cw_ref, w_hbm,
             gx_ref, u_ref, dxp_ref, bacc_ref, w_vmem, sem):
        i = pl.program_id(0)
        _load_once(w_hbm, w_vmem, sem)

        @pl.when(i % npb == 0)
        def _():
            bacc_ref[...] = jnp.zeros_like(bacc_ref)

        last = (i % npb) == npb - 1
        nxt = jnp.where(last, 0.0, nxt_ref[...])
        cur = dp_ref[...]
        xx = jnp.concatenate([cur, nxt], axis=0)
        w = cw_ref[...]
        dxp = w[3:4] * cur
        for j in (1, 2, 3):
            dxp = dxp + w[3 - j:4 - j] * pltpu.roll(xx, tb + 8 - j, axis=0)[0:tb]
        dxpb = dxp.astype(BF16)
        dxp_ref[...] = dxpb
        o1, o2, o3 = D_SSD, D_SSD + D_XBC, D_SSD + D_XBC + DT_PAD
        du = (jnp.dot(dz_ref[...].astype(BF16), w_vmem[0:o1, :], preferred_element_type=F32)
              + jnp.dot(dxpb, w_vmem[o1:o2, :], preferred_element_type=F32)
              + jnp.dot(ddt_ref[...].astype(BF16), w_vmem[o2:o3, :], preferred_element_type=F32)
              + jnp.dot(du5_ref[...].astype(BF16), w_vmem[o3:, :], preferred_element_type=F32))
        m = mod_ref[0]
        xv = x_ref[...]
        u_ref[...] = (xv * (1.0 + m[1:2]) + m[0:1]).astype(BF16)
        gx_ref[...] = dxa_ref[...] + du * (1.0 + m[1:2])
        bacc_ref[0, 0:1, :] += jnp.sum(du, axis=0, keepdims=True)
        bacc_ref[0, 1:2, :] += jnp.sum(du * xv, axis=0, keepdims=True)

    row = lambda w: pl.BlockSpec((tb, w), lambda i: (i, 0))
    nxt_rows = pl.BlockSpec((8, D_XBC), lambda i: (jnp.minimum((i + 1) * (tb // 8), t // 8 - 1), 0))
    return pl.pallas_call(
        body, name="proj_bwd", grid=(n_blocks,),
        out_shape=(jax.ShapeDtypeStruct((t, D_MODEL), F32), jax.ShapeDtypeStruct((t, D_MODEL), BF16),
                   jax.ShapeDtypeStruct((t, D_XBC), BF16), jax.ShapeDtypeStruct((t // seq, 8, D_MODEL), F32)),
        in_specs=[row(D_SSD), row(D_XBC), nxt_rows, row(DT_PAD), row(D_S5), row(D_MODEL), row(D_MODEL),
                  pl.BlockSpec((1, N_MOD, D_MODEL), lambda i: (i // npb, 0, 0)),
                  pl.BlockSpec((4, D_XBC), lambda i: (0, 0)), ANY],
        out_specs=(row(D_MODEL), row(D_MODEL), row(D_XBC), pl.BlockSpec((1, 8, D_MODEL), lambda i: (i // npb, 0, 0))),
        scratch_shapes=[pltpu.VMEM((D_INP, D_MODEL), BF16), pltpu.SemaphoreType.DMA((LOAD_CHUNKS,))],
        compiler_params=_params(60),
    )(dz, dpre, dpre, ddt, du5, x2, dxa, mod3, conv_w, w_in_pad)


def _pad_rows(a, mult):
    r = a.shape[0]
    pad = (-r) % mult
    return a if pad == 0 else jnp.concatenate([a, jnp.zeros((pad,) + a.shape[1:], a.dtype)], axis=0)


_SMALL = ["conv_w", "conv_b", "dt_bias", "a_log", "d_ssd", "norm_w", "s5_a_re", "s5_a_im", "s5_log_dt", "s5_b_re",
          "s5_b_im", "s5_c_re", "s5_c_im", "s5_d", "b_glu", "ln1_g", "ln1_b", "b1", "b2", "ln2_g", "ln2_b"]


def _tile_rows(size):
    return 8 * (-(-size // 1024))


def _pack_small(d):
    parts = []
    for n in _SMALL:
        flat = d[n].reshape(-1).astype(F32)
        rows = _tile_rows(flat.shape[0])
        pad = rows * 128 - flat.shape[0]
        if pad:
            flat = jnp.concatenate([flat, jnp.zeros((pad,), F32)])
        parts.append(flat.reshape(rows, 128))
    return jnp.concatenate(parts, axis=0)


def _unpack_small(p, shapes):
    out, off = {}, 0
    for n in _SMALL:
        size = math.prod(shapes[n])
        rows = _tile_rows(size)
        out[n] = p[off:off + rows].reshape(-1)[:size].reshape(shapes[n])
        off += rows
    return out


def kernel(x, c, w_ada, b_ada, w_in, conv_w, conv_b, dt_bias, a_log, d_ssd, norm_w, s5_a_re, s5_a_im, s5_log_dt, s5_b_re, s5_b_im, s5_c_re, s5_c_im, s5_d, w_glu, b_glu, w_out, ln1_g, ln1_b, w1, b1, w2, b2, ln2_g, ln2_b, loss_target, m_w_ada, m_b_ada, m_w_in, m_conv_w, m_conv_b, m_dt_bias, m_a_log, m_d_ssd, m_norm_w, m_s5_a_re, m_s5_a_im, m_s5_log_dt, m_s5_b_re, m_s5_b_im, m_s5_c_re, m_s5_c_im, m_s5_d, m_w_glu, m_b_glu, m_w_out, m_ln1_g, m_ln1_b, m_w1, m_b1, m_w2, m_b2, m_ln2_g, m_ln2_b, v_w_ada, v_b_ada, v_w_in, v_conv_w, v_conv_b, v_dt_bias, v_a_log, v_d_ssd, v_norm_w, v_s5_a_re, v_s5_a_im, v_s5_log_dt, v_s5_b_re, v_s5_b_im, v_s5_c_re, v_s5_c_im, v_s5_d, v_w_glu, v_b_glu, v_w_out, v_ln1_g, v_ln1_b, v_w1, v_b1, v_w2, v_b2, v_ln2_g, v_ln2_b):
    weights = dict(w_ada=w_ada, b_ada=b_ada, w_in=w_in, conv_w=conv_w, conv_b=conv_b, dt_bias=dt_bias, a_log=a_log,
                   d_ssd=d_ssd, norm_w=norm_w, s5_a_re=s5_a_re, s5_a_im=s5_a_im, s5_log_dt=s5_log_dt, s5_b_re=s5_b_re,
                   s5_b_im=s5_b_im, s5_c_re=s5_c_re, s5_c_im=s5_c_im, s5_d=s5_d, w_glu=w_glu, b_glu=b_glu, w_out=w_out,
                   ln1_g=ln1_g, ln1_b=ln1_b, w1=w1, b1=b1, w2=w2, b2=b2, ln2_g=ln2_g, ln2_b=ln2_b)
    mom = dict(w_ada=m_w_ada, b_ada=m_b_ada, w_in=m_w_in, conv_w=m_conv_w, conv_b=m_conv_b, dt_bias=m_dt_bias,
               a_log=m_a_log, d_ssd=m_d_ssd, norm_w=m_norm_w, s5_a_re=m_s5_a_re, s5_a_im=m_s5_a_im,
               s5_log_dt=m_s5_log_dt, s5_b_re=m_s5_b_re, s5_b_im=m_s5_b_im, s5_c_re=m_s5_c_re, s5_c_im=m_s5_c_im,
               s5_d=m_s5_d, w_glu=m_w_glu, b_glu=m_b_glu, w_out=m_w_out, ln1_g=m_ln1_g, ln1_b=m_ln1_b, w1=m_w1, b1=m_b1,
               w2=m_w2, b2=m_b2, ln2_g=m_ln2_g, ln2_b=m_ln2_b)
    var = dict(w_ada=v_w_ada, b_ada=v_b_ada, w_in=v_w_in, conv_w=v_conv_w, conv_b=v_conv_b, dt_bias=v_dt_bias,
               a_log=v_a_log, d_ssd=v_d_ssd, norm_w=v_norm_w, s5_a_re=v_s5_a_re, s5_a_im=v_s5_a_im,
               s5_log_dt=v_s5_log_dt, s5_b_re=v_s5_b_re, s5_b_im=v_s5_b_im, s5_c_re=v_s5_c_re, s5_c_im=v_s5_c_im,
               s5_d=v_s5_d, w_glu=v_w_glu, b_glu=v_b_glu, w_out=v_w_out, ln1_g=v_ln1_g, ln1_b=v_ln1_b, w1=v_w1, b1=v_b1,
               w2=v_w2, b2=v_b2, ln2_g=v_ln2_g, ln2_b=v_ln2_b)
    names = list(weights)
    shapes = {n: weights[n].shape for n in names}

    nb, seq, _ = x.shape
    t = nb * seq
    dev = _dev_index()
    x2 = x.reshape(t, D_MODEL)
    tgt2 = loss_target.reshape(t, D_MODEL)

    cw_cols = conv_w.shape[2]
    small_in = jnp.concatenate([c.reshape(-1), conv_w.reshape(-1)]).reshape(-1, 128)
    big_names = ["w_in", "w_out", "w1", "w2", "w_glu"]
    local = {n: (a[0].T if n == "w_in" else a[0]) for n, a in weights.items() if n in big_names}
    shard_bf16 = {n: local[n].astype(BF16) for n in big_names}
    first = _all_gather([small_in, shard_bf16["w_in"], shard_bf16["w_glu"]], "gather_first")
    small_all = first[0].reshape(N_DEV, -1)
    c_all = small_all[:, :nb * D_MODEL].reshape(N_DEV * nb, D_MODEL)
    conv_w_full = small_all[:, nb * D_MODEL:].reshape(N_DEV, 4, cw_cols).transpose(1, 0, 2).reshape(4, D_XBC)

    w_in_t = first[1].reshape(D_IN, D_MODEL)
    w_in_pad = jnp.concatenate(
        [w_in_t[:D_SSD + D_XBC + N_HEADS], jnp.zeros((DT_PAD - N_HEADS, D_MODEL), BF16),
         w_in_t[D_SSD + D_XBC + N_HEADS:]], axis=0)
    w_glu_f = first[2].reshape(D_S5, D_S5)
    late_names = ["w_out", "w1", "w2"]

    ada_cols = w_ada.shape[2]
    b_cols = lax.dynamic_slice_in_dim(b_ada, dev * ada_cols, ada_cols, axis=1)
    mod_cols = _mod_fwd(c_all, w_ada[0], b_cols)
    mod_all = _all_gather([mod_cols], "gather_mod")[0]
    mod_mine = lax.dynamic_slice_in_dim(mod_all, dev * nb, nb, axis=1)
    mod3 = mod_mine.transpose(1, 0, 2).reshape(nb, N_MOD, D_MODEL)
    late_in, mod3 = lax.optimization_barrier(([shard_bf16[n] for n in late_names], mod3))
    late_sems = _gather_start(late_in, "gather_late_start")
    mod3 = mod3 + late_sems[4][0, 0]

    def pad_lanes(v, n):
        return jnp.concatenate([v, jnp.zeros((v.shape[0], n - v.shape[1]), F32)], axis=1)

    par = _pad_rows(jnp.concatenate([pad_lanes(dt_bias, 128), pad_lanes(a_log, 128)], axis=0), 8)
    dsk = jnp.repeat(d_ssd[0], HEADDIM).reshape(1, D_SSD)
    ar = s5_a_re.reshape(1, S5_N)
    ai = s5_a_im.reshape(1, S5_N)
    ldt = jnp.repeat(s5_log_dt[0], S5_P).reshape(1, S5_N)
    br_t = s5_b_re[0].transpose(2, 0, 1).reshape(S5_CH, S5_N)
    bi_t = s5_b_im[0].transpose(2, 0, 1).reshape(S5_CH, S5_N)
    bb_re_t, bb_im_t, pf_re, pf_im, pr_re, pr_im = _s5_params_fwd(ar, ai, ldt, br_t, bi_t)
    gpb = S5_GROUPS // S5_BLOCKS
    mask_b = (jnp.arange(128)[:, None] // S5_CH) == (jnp.arange(512)[None, :] // S5_P)

    def dense_b(bt_):
        blocks = bt_.reshape(S5_CH, S5_BLOCKS, 512).transpose(1, 0, 2)
        return jnp.where(mask_b, jnp.tile(blocks, (1, gpb, 1)), 0.0).astype(BF16)

    def dense_c(cc):
        blocks = cc[0].transpose(0, 2, 1).reshape(S5_BLOCKS, 512, S5_CH)
        return jnp.where(mask_b.T, jnp.tile(blocks, (1, 1, gpb)), 0.0).astype(BF16)

    bb_re, bb_im = dense_b(bb_re_t), dense_b(bb_im_t)
    cc_re, cc_im = dense_c(s5_c_re), dense_c(s5_c_im)
    s5d = s5_d.reshape(1, D_S5)
    ln1 = jnp.concatenate([ln1_g, ln1_b], axis=0)
    vec1 = _pad_rows(jnp.concatenate([b2, ln2_g, ln2_b], axis=0), 8)

    z, xbc_pre, xbc, dsilu, dt_raw, u5 = _proj_conv_fwd(x2, mod3, w_in_pad, conv_w_full, conv_b, seq)
    yraw, ycat, hprev = _ssd_fwd(xbc, z, dt_raw, par, dsk, norm_w, seq)
    s_re, s_im, ypre, ycat = _s5_fwd(u5, bb_re, bb_im, cc_re, cc_im, pf_re, pf_im, s5d, w_glu_f, b_glu, ycat, seq)
    sent, landed = _gather_wait(late_sems[0], late_sems[1], late_sems[2], late_sems[3], ycat, "gather_late_wait")
    gathered = {n: lax.dynamic_update_index_in_dim(l, x, dev, 0) for n, x, l in zip(late_names, sent, landed)}
    w_out_f = gathered["w_out"].reshape(2 * D_MODEL, D_MODEL)
    w1_blocks = gathered["w1"]
    w2_f = gathered["w2"].reshape(D_FF, D_MODEL)
    mix, x1 = _out_ln1(ycat, x2, mod3, w_out_f, ln1, seq)

    dx1, u2b, hb, dhpb, dob, gacc2, db1, bacc2 = _mlp_fwd_bwd(x1, tgt2, mod3, w1_blocks, w2_f, vec1, b1, seq)
    loss = lax.psum(0.5 / D_MODEL * jnp.sum(gacc2[3]), ("x", "y", "c"))

    dmixb, dxa, dyssd, dy5, gacc1, bacc1 = _ln1_out_bwd(dx1, x2, mix, mod3, w_out_f, ln1, seq)

    g_w2 = _atb(hb, dob, "gw2")
    g_w1 = _atb(u2b, dhpb, "gw1")
    g_wout = _atb(ycat, dmixb, "gwout")
    core = lax.axis_index("c").astype(jnp.int32).reshape(1)
    chip = 2 * lax.axis_index("x") + lax.axis_index("y")

    def chip_sums_of(names, grads, tag):
        by_dest = [g if g.ndim == 2 else g.reshape((4, 2) + g.shape[1:]) for g in grads]
        from_sibling = _sibling_swap(by_dest, "rs_swap_" + tag)
        return [_add_halves(g, r, core, "rs_add_" + n) for g, r, n in zip(by_dest, from_sibling, names)]

    early_names = ["w_out", "w1", "w2"]
    early_sums = chip_sums_of(early_names, [g_wout.reshape((N_DEV,) + w_out.shape[1:]), g_w1,
                                            g_w2.reshape((N_DEV,) + w2.shape[1:])], "early")
    early = _all_to_all_start(early_sums, "rs_early_start")
    s5d_after = s5d + early[4][0, 0]

    du5, vacc, sacc, d_cc, d_bb, g_wglu = _s5_bwd(dy5, ypre, u5, s_re, s_im, bb_re, bb_im, cc_re, cc_im,
                                                  pr_re, pr_im, s5d_after, w_glu_f, b_glu, seq)
    dxbc, dz, ddt, dpar, cacc = _ssd_bwd(dyssd, yraw, z, xbc, dt_raw, hprev, par, dsk, norm_w, seq)
    dpre, conv_acc = _conv_bwd(dxbc, dsilu, xbc_pre, seq)
    grad_x2, ub, dxpb, bacc0 = _proj_bwd(dz, dpre, ddt, du5, x2, dxa, mod3, conv_w_full, w_in_pad, seq)

    g_win_t = jnp.concatenate([_atb(dz, ub, "gwin_z"), _atb(dxpb, ub, "gwin_xbc"),
                               _atb(ddt, ub, "gwin_dt")[:N_HEADS], _atb(du5, ub, "gwin_s5")], axis=0)

    def diag_b(dd):
        kept = jnp.where(mask_b, dd, 0.0).reshape(S5_BLOCKS, gpb, S5_CH, 512).sum(1)
        return kept.transpose(1, 0, 2).reshape(S5_CH, S5_N)

    def diag_c(dd):
        kept = jnp.where(mask_b.T, dd, 0.0).reshape(S5_BLOCKS, 512, gpb, S5_CH).sum(2)
        return kept.reshape(S5_GROUPS, S5_P, S5_CH).transpose(0, 2, 1)

    g_ar, g_ai, g_ldt, g_br_t, g_bi_t = _s5_params_bwd(ar, ai, ldt, br_t, bi_t, vacc[0:1], vacc[1:2],
                                                      diag_b(d_bb[:S5_BLOCKS]), diag_b(d_bb[S5_BLOCKS:]))

    def from_t(gt):
        return gt.reshape(S5_CH, S5_GROUPS, S5_P).transpose(1, 2, 0)

    small_g = dict(
        conv_w=conv_acc[0:4], conv_b=conv_acc[4:5], dt_bias=dpar[0:1, :N_HEADS], a_log=dpar[1:2, :N_HEADS],
        d_ssd=cacc[0].reshape(N_HEADS, HEADDIM).sum(1), norm_w=cacc[1:2],
        s5_a_re=g_ar, s5_a_im=g_ai, s5_log_dt=g_ldt[:, :S5_GROUPS], s5_b_re=from_t(g_br_t), s5_b_im=from_t(g_bi_t),
        s5_c_re=diag_c(d_cc[:S5_BLOCKS]), s5_c_im=diag_c(d_cc[S5_BLOCKS:]), s5_d=sacc[0:1], b_glu=sacc[1:2],
        ln1_g=gacc1[0:1], ln1_b=gacc1[1:2], b1=db1, b2=gacc2[2:3], ln2_g=gacc2[0:1], ln2_b=gacc2[1:2])

    dmod = jnp.concatenate([bacc0[:, 0], bacc0[:, 1], bacc1[:, 0], bacc2[:, 0], bacc2[:, 1], bacc2[:, 2]], axis=1)
    dmod_all, small_parts = _all_gather([dmod, _pack_small(small_g)], "gather_dmod_small_grads")
    dmod_all = dmod_all.reshape(N_DEV * nb, N_MOD * D_MODEL)
    dmod_cols = lax.dynamic_slice_in_dim(dmod_all, dev * ada_cols, ada_cols, axis=1)
    g_wada, g_bada = _mod_bwd(c_all, dmod_cols, dmod_all)

    late_rs = ["w_in", "w_glu"]
    late_sums = chip_sums_of(late_rs, [g_win_t.reshape(N_DEV, w_in.shape[2], D_MODEL),
                                       g_wglu.reshape((N_DEV,) + w_glu.shape[1:])], "late")
    parts = dict(zip(late_rs, _chip_all_to_all(late_sums, "rs_late_all_to_all")))
    sent, landed = _all_to_all_wait(early[0], early[1], early[2], early[3], parts["w_in"], "rs_early_wait")
    for n, l, h in zip(early_names, landed, sent):
        parts[n] = lax.dynamic_update_index_in_dim(l, lax.dynamic_index_in_dim(h, chip, 0, keepdims=False), chip, 0)

    res = {k: {} for k in "gdmv"}
    for n in big_names:
        w_m_v = [(a[n][0].T if n == "w_in" else a[n][0]) for a in (weights, mom, var)]
        outs = _adamw(parts[n], *w_m_v, "adamw_" + n)
        for k, a in zip("gdmv", outs):
            res[k][n] = (a.T if n == "w_in" else a)[None]

    ag, ad, am, av = _adamw(g_wada[None], w_ada[0], m_w_ada[0], v_w_ada[0], "adamw_w_ada")
    for k, a in (("g", ag), ("d", ad), ("m", am), ("v", av)):
        res[k]["w_ada"] = a[None]
    bg_, bd_, bm_, bv_ = _adamw(g_bada.reshape(1, -1, 128), b_ada.reshape(-1, 128), m_b_ada.reshape(-1, 128),
                                v_b_ada.reshape(-1, 128), "adamw_b_ada")
    for k, a in (("g", bg_), ("d", bd_), ("m", bm_), ("v", bv_)):
        res[k]["b_ada"] = a.reshape(shapes["b_ada"])

    small_shapes = dict(shapes)
    small_shapes["conv_w"] = (1, 4, D_XBC)
    rep = {n: (jnp.zeros((1, 4, D_XBC), F32) if n == "conv_w" else weights[n]) for n in _SMALL}
    rep_m = {n: (jnp.zeros((1, 4, D_XBC), F32) if n == "conv_w" else mom[n]) for n in _SMALL}
    rep_v = {n: (jnp.ones((1, 4, D_XBC), F32) if n == "conv_w" else var[n]) for n in _SMALL}
    sg_, sd_, sm_, sv_ = _adamw(small_parts, _pack_small(rep), _pack_small(rep_m), _pack_small(rep_v), "adamw_small")
    for k, p in (("g", sg_), ("d", sd_), ("m", sm_), ("v", sv_)):
        un = _unpack_small(p, small_shapes)
        for n in _SMALL:
            if n != "conv_w":
                res[k][n] = un[n]
    g_conv_full = _unpack_small(sg_, small_shapes)["conv_w"][0]
    g_conv_mine = lax.dynamic_slice_in_dim(g_conv_full, dev * cw_cols, cw_cols, axis=1)
    cg_, cd_, cm_, cv_ = _adamw(g_conv_mine[None], conv_w[0], m_conv_w[0], v_conv_w[0], "adamw_conv_w")
    for k, a in (("g", cg_), ("d", cd_), ("m", cm_), ("v", cv_)):
        res[k]["conv_w"] = a[None]

    grad_x = grad_x2.reshape(nb, seq, D_MODEL)
    return (loss, grad_x, *[res["g"][n] for n in names], *[res["d"][n] for n in names],
            *[res["m"][n] for n in names], *[res["v"][n] for n in names])
```

```python
import functools
import math

import jax
import jax.numpy as jnp
from jax import lax
from jax.experimental import pallas as pl
from jax.experimental.pallas import tpu as pltpu

F32, BF16 = jnp.float32, jnp.bfloat16
MESH = pl.DeviceIdType.MESH
N_DEV = 8

D_MODEL = 1024
D_SSD = 1536
N_HEADS = 24
HEADDIM = 64
N_GROUPS = 4
HPG = 6
GW = HPG * HEADDIM
N_STATE = 128
CHUNK = 128
D_XBC = 2560
D_S5 = 512
S5_GROUPS = 32
S5_CH = 16
S5_P = 64
S5_N = S5_GROUPS * S5_P
D_IN = 4632
DT_PAD = 128
D_INP = D_SSD + D_XBC + DT_PAD + D_S5
D_FF = 4096
N_MOD = 6
ALPHA = 2.0 ** 0.25
EPS = 1e-5
LR, B1, B2, AEPS, WD, STEP = 0.001, 0.9, 0.999, 1e-08, 0.01, 10

NT = (((1,), (1,)), ((), ()))
TN = (((0,), (0,)), ((), ()))
ANY = pl.BlockSpec(memory_space=pl.ANY)
HIGHEST = lax.Precision.HIGHEST


def _mm(a, b):
    return jnp.dot(a.astype(BF16), b.astype(BF16), preferred_element_type=F32)


def _mm_nt(a, b):
    return lax.dot_general(a.astype(BF16), b.astype(BF16), NT, preferred_element_type=F32)


def _mm_tn(a, b):
    return lax.dot_general(a.astype(BF16), b.astype(BF16), TN, preferred_element_type=F32)


def _row_block(r, cap):
    best = r
    for cand in range(8, min(r, cap) + 1, 8):
        if r % cand == 0:
            best = cand
    return best if best <= cap else r


def _params(vmem_mb):
    return pltpu.CompilerParams(vmem_limit_bytes=vmem_mb << 20)


def _sigmoid(x):
    return 0.5 * (jnp.tanh(0.5 * x) + 1.0)


def _softplus(x):
    return jnp.maximum(x, 0.0) + jnp.log(1.0 + jnp.exp(-jnp.abs(x)))


_GK = math.sqrt(2.0 / math.pi)


def _gelu(x):
    return 0.5 * x * (1.0 + jnp.tanh(_GK * (x + 0.044715 * x * x * x)))


def _gelu_grad(x):
    t = jnp.tanh(_GK * (x + 0.044715 * x * x * x))
    return 0.5 * (1.0 + t) + 0.5 * x * (1.0 - t * t) * _GK * (1.0 + 3.0 * 0.044715 * x * x)


def _dev_index():
    return 4 * lax.axis_index("x") + 2 * lax.axis_index("y") + lax.axis_index("c")


def _all_gather(xs, name):
    n = len(xs)

    def body(*refs):
        x_refs, out_refs = refs[:n], refs[n:2 * n]
        send_sems, recv_sems, local_sems = refs[2 * n:]
        ix, iy, ic = lax.axis_index("x"), lax.axis_index("y"), lax.axis_index("c")
        me, sibling = (ix, iy, ic), (ix, iy, 1 - ic)
        chips = [(1 - ix, iy), (ix, 1 - iy), (1 - ix, 1 - iy)]

        def slot(a, px, py, pc):
            return out_refs[a].at[4 * px + 2 * py + pc]

        def copy(a, k, block, to, src=None):
            return pltpu.make_async_remote_copy(
                src_ref=slot(a, *block) if src is None else src, dst_ref=slot(a, *block),
                send_sem=send_sems.at[7 * a + k], recv_sem=recv_sems.at[7 * a + k], device_id=to, device_id_type=MESH)

        mine = [pltpu.make_async_copy(x_refs[a], slot(a, *me), local_sems.at[a]) for a in range(n)]
        for cp in mine:
            cp.start()
        first = []
        for j, chip in enumerate(chips):
            first += [copy(a, 1 + j, me, (*chip, ic), src=x_refs[a]) for a in range(n)]
        first += [copy(a, 0, me, sibling, src=x_refs[a]) for a in range(n)]
        for cp in first:
            cp.start()
        passed = []
        for j, chip in enumerate(chips):
            for a in range(n):
                copy(a, 1 + j, (*chip, ic), me).wait_recv()
                cp = copy(a, 4 + j, (*chip, ic), sibling)
                cp.start()
                passed.append(cp)
        for a in range(n):
            copy(a, 0, sibling, me).wait_recv()
            for j, chip in enumerate(chips):
                copy(a, 4 + j, (*chip, 1 - ic), me).wait_recv()
        for cp in first + passed:
            cp.wait_send()
        for cp in mine:
            cp.wait()

    return pl.pallas_call(
        body, name=name, out_shape=tuple(jax.ShapeDtypeStruct((N_DEV,) + x.shape, x.dtype) for x in xs),
        in_specs=[ANY] * n, out_specs=tuple([ANY] * n),
        scratch_shapes=[pltpu.SemaphoreType.DMA((7 * n,)), pltpu.SemaphoreType.DMA((7 * n,)),
                        pltpu.SemaphoreType.DMA((n,))],
    )(*xs)


HBM = pl.BlockSpec(memory_space=pltpu.HBM)
SEM = pl.BlockSpec(memory_space=pltpu.SEMAPHORE)
DATAFLOW = pltpu.SideEffectType.DATAFLOW_SIDE_EFFECTING


def _peer(k):
    ix, iy, ic = lax.axis_index("x"), lax.axis_index("y"), lax.axis_index("c")
    return (1 - ix if k & 4 else ix, 1 - iy if k & 2 else iy, 1 - ic if k & 1 else ic)


def _block_of(p):
    return 4 * p[0] + 2 * p[1] + p[2]


def _gather_start(xs, name):
    n = len(xs)
    lands = [lax.empty((N_DEV,) + x.shape, x.dtype) for x in xs]

    def body(*refs):
        x_refs, land_refs = refs[:n], refs[n:2 * n]
        send_sems, recv_sems = refs[2 * n], refs[2 * n + 1]
        token = refs[-1]
        me = _block_of(_peer(0))
        for a in range(n):
            for k in range(1, N_DEV):
                pltpu.make_async_remote_copy(
                    src_ref=x_refs[a], dst_ref=land_refs[a].at[me], send_sem=send_sems.at[7 * a + k - 1],
                    recv_sem=recv_sems.at[7 * a + k - 1], device_id=_peer(k), device_id_type=MESH).start()
        token[...] = jnp.zeros_like(token)

    outs = pl.pallas_call(
        body, name=name,
        out_shape=(pltpu.SemaphoreType.DMA((7 * n,)), pltpu.SemaphoreType.DMA((7 * n,)))
        + tuple(pltpu.HBM(x.shape, x.dtype) for x in xs) + tuple(pltpu.HBM(l.shape, l.dtype) for l in lands)
        + (jax.ShapeDtypeStruct((8, 128), F32),),
        in_specs=[HBM] * (2 * n), out_specs=(SEM, SEM) + (HBM,) * (2 * n) + (pl.BlockSpec(memory_space=pltpu.VMEM),),
        input_output_aliases={i: 2 + i for i in range(2 * n)},
        compiler_params=pltpu.CompilerParams(has_side_effects=DATAFLOW),
    )(*[pltpu.with_memory_space_constraint(x, pltpu.HBM) for x in xs],
      *[pltpu.with_memory_space_constraint(l, pltpu.HBM) for l in lands])
    return outs[0], outs[1], outs[2:2 + n], outs[2 + n:2 + 2 * n], outs[-1]


def _gather_wait(send_sems, recv_sems, xs_thru, lands_thru, after, name):
    n = len(xs_thru)

    def body(*refs):
        x_refs, land_refs = refs[:n], refs[n:2 * n]
        send_sems, recv_sems = refs[2 * n], refs[2 * n + 1]
        for a in range(n):
            for k in range(1, N_DEV):
                cp = pltpu.make_async_remote_copy(
                    src_ref=x_refs[a], dst_ref=land_refs[a].at[_block_of(_peer(k))], send_sem=send_sems.at[7 * a + k - 1],
                    recv_sem=recv_sems.at[7 * a + k - 1], device_id=_peer(k), device_id_type=MESH)
                cp.wait_send()
                cp.wait_recv()

    outs = pl.pallas_call(
        body, name=name,
        out_shape=tuple(pltpu.HBM(x.shape, x.dtype) for x in xs_thru)
        + tuple(pltpu.HBM(l.shape, l.dtype) for l in lands_thru),
        in_specs=[HBM] * (2 * n) + [SEM, SEM, ANY], out_specs=(HBM,) * (2 * n),
        input_output_aliases={i: i for i in range(2 * n)},
        compiler_params=pltpu.CompilerParams(has_side_effects=DATAFLOW),
    )(*xs_thru, *lands_thru, send_sems, recv_sems, after)
    return outs[:n], outs[n:]


def _chip_peer(k):
    ix, iy = lax.axis_index("x"), lax.axis_index("y")
    return (1 - ix if k & 2 else ix, 1 - iy if k & 1 else iy)


def _all_to_all_start(hs, name):
    n = len(hs)
    lands = [lax.empty(h.shape, h.dtype) for h in hs]

    def body(*refs):
        h_refs, land_refs = refs[:n], refs[n:2 * n]
        send_sems, recv_sems = refs[2 * n], refs[2 * n + 1]
        token = refs[-1]
        ic = lax.axis_index("c")
        mx, my = _chip_peer(0)
        for a in range(n):
            for k in range(1, 4):
                px, py = _chip_peer(k)
                pltpu.make_async_remote_copy(
                    src_ref=h_refs[a].at[2 * px + py], dst_ref=land_refs[a].at[2 * mx + my],
                    send_sem=send_sems.at[3 * a + k - 1], recv_sem=recv_sems.at[3 * a + k - 1],
                    device_id=(px, py, ic), device_id_type=MESH).start()
        token[...] = jnp.zeros_like(token)

    outs = pl.pallas_call(
        body, name=name,
        out_shape=(pltpu.SemaphoreType.DMA((3 * n,)), pltpu.SemaphoreType.DMA((3 * n,)))
        + tuple(pltpu.HBM(h.shape, h.dtype) for h in hs) + tuple(pltpu.HBM(l.shape, l.dtype) for l in lands)
        + (jax.ShapeDtypeStruct((8, 128), F32),),
        in_specs=[HBM] * (2 * n), out_specs=(SEM, SEM) + (HBM,) * (2 * n) + (pl.BlockSpec(memory_space=pltpu.VMEM),),
        input_output_aliases={i: 2 + i for i in range(2 * n)},
        compiler_params=pltpu.CompilerParams(has_side_effects=DATAFLOW),
    )(*[pltpu.with_memory_space_constraint(h, pltpu.HBM) for h in hs],
      *[pltpu.with_memory_space_constraint(l, pltpu.HBM) for l in lands])
    return outs[0], outs[1], outs[2:2 + n], outs[2 + n:2 + 2 * n], outs[-1]


def _all_to_all_wait(send_sems, recv_sems, hs_thru, lands_thru, after, name):
    n = len(hs_thru)

    def body(*refs):
        h_refs, land_refs = refs[:n], refs[n:2 * n]
        send_sems, recv_sems = refs[2 * n], refs[2 * n + 1]
        ic = lax.axis_index("c")
        for a in range(n):
            for k in range(1, 4):
                px, py = _chip_peer(k)
                cp = pltpu.make_async_remote_copy(
                    src_ref=h_refs[a].at[2 * px + py], dst_ref=land_refs[a].at[2 * px + py],
                    send_sem=send_sems.at[3 * a + k - 1], recv_sem=recv_sems.at[3 * a + k - 1],
                    device_id=(px, py, ic), device_id_type=MESH)
                cp.wait_send()
                cp.wait_recv()

    outs = pl.pallas_call(
        body, name=name,
        out_shape=tuple(pltpu.HBM(h.shape, h.dtype) for h in hs_thru)
        + tuple(pltpu.HBM(l.shape, l.dtype) for l in lands_thru),
        in_specs=[HBM] * (2 * n) + [SEM, SEM, ANY], out_specs=(HBM,) * (2 * n),
        input_output_aliases={i: i for i in range(2 * n)},
        compiler_params=pltpu.CompilerParams(has_side_effects=DATAFLOW),
    )(*hs_thru, *lands_thru, send_sems, recv_sems, after)
    return outs[:n], outs[n:]


def _sibling_swap(gs, name):
    n = len(gs)

    def body(*refs):
        g_refs, recv_refs = refs[:n], refs[n:2 * n]
        send_sems, recv_sems = refs[2 * n:]
        ix, iy, ic = lax.axis_index("x"), lax.axis_index("y"), lax.axis_index("c")

        def block(g_ref, q):
            if len(g_ref.shape) == 4:
                return g_ref.at[q, 1 - ic]
            cw = g_ref.shape[1] // N_DEV
            return g_ref.at[:, pl.ds(pl.multiple_of((2 * q + 1 - ic) * cw, 128), cw)]

        cps = []
        for a in range(n):
            for q in range(4):
                cps.append(pltpu.make_async_remote_copy(
                    src_ref=block(g_refs[a], q), dst_ref=recv_refs[a].at[q],
                    send_sem=send_sems.at[4 * a + q], recv_sem=recv_sems.at[4 * a + q],
                    device_id=(ix, iy, 1 - ic), device_id_type=MESH))
        for cp in cps:
            cp.start()
        for cp in cps:
            cp.wait()

    return pl.pallas_call(
        body, name=name,
        out_shape=tuple(jax.ShapeDtypeStruct(
            (4,) + (g.shape[2:] if g.ndim == 4 else (g.shape[0], g.shape[1] // N_DEV)), g.dtype) for g in gs),
        in_specs=[ANY] * n, out_specs=tuple([ANY] * n),
        scratch_shapes=[pltpu.SemaphoreType.DMA((4 * n,)), pltpu.SemaphoreType.DMA((4 * n,))],
    )(*gs)


def _chip_all_to_all(hs, name):
    n = len(hs)

    def body(*refs):
        h_refs, out_refs = refs[:n], refs[n:2 * n]
        send_sems, recv_sems, local_sems = refs[2 * n:]
        ix, iy, ic = lax.axis_index("x"), lax.axis_index("y"), lax.axis_index("c")
        me = 2 * ix + iy
        peers = [(1 - ix, iy), (ix, 1 - iy), (1 - ix, 1 - iy)]
        mine = [pltpu.make_async_copy(h_refs[a].at[me], out_refs[a].at[me], local_sems.at[a]) for a in range(n)]
        for cp in mine:
            cp.start()

        def copy(a, k, src_slot, dst_slot, peer):
            return pltpu.make_async_remote_copy(
                src_ref=h_refs[a].at[src_slot], dst_ref=out_refs[a].at[dst_slot],
                send_sem=send_sems.at[3 * a + k], recv_sem=recv_sems.at[3 * a + k],
                device_id=(*peer, ic), device_id_type=MESH)

        sends = [copy(a, k, 2 * px + py, me, (px, py)) for a in range(n) for k, (px, py) in enumerate(peers)]
        for cp in sends:
            cp.start()
        for a in range(n):
            for k, (px, py) in enumerate(peers):
                copy(a, k, 2 * px + py, 2 * px + py, (px, py)).wait_recv()
        for cp in sends:
            cp.wait_send()
        for cp in mine:
            cp.wait()

    return pl.pallas_call(
        body, name=name, out_shape=tuple(jax.ShapeDtypeStruct(h.shape, h.dtype) for h in hs),
        in_specs=[ANY] * n, out_specs=tuple([ANY] * n),
        scratch_shapes=[pltpu.SemaphoreType.DMA((3 * n,)), pltpu.SemaphoreType.DMA((3 * n,)),
                        pltpu.SemaphoreType.DMA((n,))],
    )(*hs)


def _add_halves(g, recv, core, name):
    _, r, c = recv.shape
    br = _row_block(r, 512)
    stacked = g.ndim == 4

    def body(core_ref, g_ref, r_ref, o_ref):
        o_ref[0] = ((g_ref[0, 0] if stacked else g_ref[...]) + r_ref[0]).astype(BF16)

    spec = pl.BlockSpec((1, br, c), lambda i, j, core_ref: (i, j, 0))
    if stacked:
        g_spec = pl.BlockSpec((1, 1, br, c), lambda i, j, core_ref: (i, core_ref[0], j, 0))
    else:
        g_spec = pl.BlockSpec((br, c), lambda i, j, core_ref: (j, 2 * i + core_ref[0]))
    return pl.pallas_call(
        body, name=name, out_shape=jax.ShapeDtypeStruct(recv.shape, BF16),
        grid_spec=pltpu.PrefetchScalarGridSpec(
            num_scalar_prefetch=1, grid=(4, r // br), in_specs=[g_spec, spec], out_specs=spec),
        compiler_params=_params(32),
    )(core, g, recv)


def _adamw(parts, w, m, v, name):
    n_parts, r, c = parts.shape
    if r % 8 == 0:
        br, bc = _row_block(r, 512 if c <= 1024 else 256), c
    else:
        br, bc = r, (256 if c % 256 == 0 else c)

    def body(p_ref, w_ref, m_ref, v_ref, g_out, d_out, m_out, v_out):
        g = p_ref[0].astype(F32)
        for p in range(1, n_parts):
            g = g + p_ref[p].astype(F32)
        m2 = B1 * m_ref[...] + (1.0 - B1) * g
        v2 = B2 * v_ref[...] + (1.0 - B2) * (g * g)
        m_hat = m2 / (1.0 - B1 ** STEP)
        v_hat = v2 / (1.0 - B2 ** STEP)
        g_out[...] = g
        d_out[...] = -LR * (m_hat / (jnp.sqrt(v_hat) + AEPS) + WD * w_ref[...])
        m_out[...] = m2
        v_out[...] = v2

    spec = pl.BlockSpec((br, bc), lambda i, j: (i, j))
    out = jax.ShapeDtypeStruct((r, c), F32)
    return pl.pallas_call(
        body, name=name, out_shape=(out, out, out, out), grid=(r // br, c // bc),
        in_specs=[pl.BlockSpec((n_parts, br, bc), lambda i, j: (0, i, j)), spec, spec, spec],
        out_specs=(spec, spec, spec, spec), compiler_params=_params(40),
    )(parts, w, m, v)


def _atb(a, b, name):
    t, k1 = a.shape
    k2 = b.shape[1]
    bt = math.gcd(t, 2048)

    def pick(k):
        for cand in (1024, 768, 512, 384, 256, 128):
            if k % cand == 0:
                return cand
        return k

    b1, b2 = pick(k1), pick(k2)

    def body(a_ref, b_ref, o_ref):
        @pl.when(pl.program_id(2) == 0)
        def _():
            o_ref[...] = jnp.zeros_like(o_ref)
        o_ref[...] += _mm_tn(a_ref[...], b_ref[...])

    return pl.pallas_call(
        body, name=name, out_shape=jax.ShapeDtypeStruct((k1, k2), F32), grid=(k1 // b1, k2 // b2, t // bt),
        in_specs=[pl.BlockSpec((bt, b1), lambda i, j, k: (k, i)), pl.BlockSpec((bt, b2), lambda i, j, k: (k, j))],
        out_specs=pl.BlockSpec((b1, b2), lambda i, j, k: (i, j)), compiler_params=_params(48),
    )(a, b)


def _mod_fwd(c_all, w_ada, b_cols):
    def body(c_ref, w_ref, b_ref, o_ref):
        cc = c_ref[...]
        cond = cc * _sigmoid(cc)
        o_ref[...] = _mm(cond, w_ref[...]) + b_ref[...]

    return pl.pallas_call(body, name="mod_fwd", out_shape=jax.ShapeDtypeStruct((c_all.shape[0], w_ada.shape[1]), F32),
                          compiler_params=_params(32))(c_all, w_ada, b_cols)


def _mod_bwd(c_all, dmod_cols, dmod_all):
    def body(c_ref, dc_ref, da_ref, gw_ref, gb_ref):
        cc = c_ref[...]
        cond = cc * _sigmoid(cc)
        gw_ref[...] = _mm_tn(cond, dc_ref[...])
        gb_ref[...] = jnp.sum(da_ref[...], axis=0, keepdims=True)

    return pl.pallas_call(
        body, name="mod_bwd",
        out_shape=(jax.ShapeDtypeStruct((D_MODEL, dmod_cols.shape[1]), F32), jax.ShapeDtypeStruct((1, dmod_all.shape[1]), F32)),
        compiler_params=_params(32))(c_all, dmod_cols, dmod_all)


def _load_once(hbm_ref, vmem_ref, sem):
    @pl.when(pl.program_id(0) == 0)
    def _():
        cp = pltpu.make_async_copy(hbm_ref, vmem_ref, sem)
        cp.start()
        cp.wait()


def _conv_taps(win_ref, w, tb, cols):
    shifted = [win_ref[8 - j:8 - j + tb, cols] for j in range(4)]
    acc = w[3:4] * shifted[0]
    for j in (1, 2, 3):
        acc = acc + w[3 - j:4 - j] * shifted[j]
    return acc, shifted


def _proj_conv_fwd(x2, mod3, w_in_pad, conv_w, conv_b, seq):
    t = x2.shape[0]
    tb = 256
    npb = seq // tb
    cw = 512

    def body(x_ref, mod_ref, w_hbm, cw_ref, cb_ref, z_ref, pre_ref, xbc_ref, dsilu_ref, dt_ref, u5_ref, w_vmem, win, sem):
        _load_once(w_hbm, w_vmem, sem)
        first = (pl.program_id(0) % npb) == 0

        @pl.when(first)
        def _():
            win[0:8, :] = jnp.zeros((8, D_XBC), F32)

        @pl.when(jnp.logical_not(first))
        def _():
            win[0:8, :] = win[tb:tb + 8, :]

        m = mod_ref[0]
        u = (x_ref[...] * (1.0 + m[1:2]) + m[0:1]).astype(BF16)
        z_ref[...] = lax.dot_general(u, w_vmem[0:D_SSD, :], NT, preferred_element_type=F32)
        dt_ref[...] = lax.dot_general(u, w_vmem[D_SSD + D_XBC:D_SSD + D_XBC + DT_PAD, :], NT,
                                      preferred_element_type=F32)
        u5_ref[...] = lax.dot_general(u, w_vmem[D_SSD + D_XBC + DT_PAD:, :], NT, preferred_element_type=F32)
        for k in range(D_XBC // cw):
            cols = slice(k * cw, (k + 1) * cw)
            pre_k = lax.dot_general(u, w_vmem[D_SSD + k * cw:D_SSD + (k + 1) * cw, :], NT,
                                    preferred_element_type=F32)
            win[8:8 + tb, cols] = pre_k
            pre_ref[:, cols] = pre_k
            conv, _ = _conv_taps(win, cw_ref[:, cols], tb, cols)
            conv = conv + cb_ref[:, cols]
            sg = _sigmoid(conv)
            xbc_ref[:, cols] = conv * sg
            dsilu_ref[:, cols] = sg * (1.0 + conv * (1.0 - sg))

    row = lambda w: pl.BlockSpec((tb, w), lambda i: (i, 0))
    return pl.pallas_call(
        body, name="proj_conv_fwd", grid=(t // tb,),
        out_shape=(jax.ShapeDtypeStruct((t, D_SSD), F32), jax.ShapeDtypeStruct((t, D_XBC), F32),
                   jax.ShapeDtypeStruct((t, D_XBC), F32), jax.ShapeDtypeStruct((t, D_XBC), F32),
                   jax.ShapeDtypeStruct((t, DT_PAD), F32), jax.ShapeDtypeStruct((t, D_S5), F32)),
        in_specs=[row(D_MODEL), pl.BlockSpec((1, N_MOD, D_MODEL), lambda i: (i // npb, 0, 0)), ANY,
                  pl.BlockSpec((4, D_XBC), lambda i: (0, 0)), pl.BlockSpec((1, D_XBC), lambda i: (0, 0))],
        out_specs=(row(D_SSD), row(D_XBC), row(D_XBC), row(D_XBC), row(DT_PAD), row(D_S5)),
        scratch_shapes=[pltpu.VMEM((D_INP, D_MODEL), BF16), pltpu.VMEM((tb + 8, D_XBC), F32), pltpu.SemaphoreType.DMA],
        compiler_params=_params(56),
    )(x2, mod3, w_in_pad, conv_w, conv_b)


N_PAIRS = N_HEADS // 2


def _split3(x):
    hi = x.astype(BF16)
    r = x - hi.astype(F32)
    mid = r.astype(BF16)
    lo = (r - mid.astype(F32)).astype(BF16)
    return hi, mid, lo


def _dot3(x, e, dims=(((1,), (0,)), ((), ()))):
    return sum(lax.dot_general(p, e, dims, preferred_element_type=F32) for p in _split3(x))


def _dot3_left(e, x, dims=(((1,), (0,)), ((), ()))):
    return sum(lax.dot_general(e, p, dims, preferred_element_type=F32) for p in _split3(x))


def _head_fold():
    return (jnp.arange(D_SSD)[:, None] // HEADDIM == jnp.arange(128)[None, :]).astype(BF16)


def _ssd_prep(dt_raw, par):
    dtb = par[0:1]
    a = -jnp.exp(par[1:2])
    dt = _softplus(dt_raw + dtb)
    adt = dt * a
    row = lax.broadcasted_iota(jnp.int32, (CHUNK, CHUNK), 0)
    col = lax.broadcasted_iota(jnp.int32, (CHUNK, CHUNK), 1)
    causal = row >= col
    tri = causal.astype(BF16)
    cs = _dot3_left(tri, adt)
    left = col < HEADDIM

    def lanes(v, h):
        return jnp.broadcast_to(v[:, h:h + 1], (CHUNK, 128))

    dt_c, cs_c, pair_cols = [], [], []
    for p in range(N_PAIRS):
        c0, c1 = lanes(cs, 2 * p), lanes(cs, 2 * p + 1)
        pair_cols.append(jnp.concatenate([c0, c1], axis=1))
        cs_c.append(jnp.where(left, c0, c1))
        dt_c.append(jnp.where(left, lanes(dt, 2 * p), lanes(dt, 2 * p + 1)))
    cs_c = jnp.concatenate(cs_c, axis=1)
    dt_c = jnp.concatenate(dt_c, axis=1)
    return dt, a, cs, cs.T, causal, tri, dt_c, jnp.exp(cs_c), jnp.exp(cs_c[CHUNK - 1:CHUNK, :] - cs_c), pair_cols


def _pair_decay(cols, cst, pair, causal2):
    rows = jnp.concatenate([jnp.broadcast_to(cst[2 * pair:2 * pair + 1, :], (CHUNK, CHUNK)),
                            jnp.broadcast_to(cst[2 * pair + 1:2 * pair + 2, :], (CHUNK, CHUNK))], axis=1)
    return jnp.exp(jnp.where(causal2, cols - rows, -jnp.inf))


def _stack_heads(xp, left):
    return jnp.concatenate([jnp.where(left, xp, 0.0), jnp.where(left, 0.0, xp)], axis=0).astype(BF16)


def _ssd_fwd(xbc, z, dt_raw, par, dsk, normw, seq):
    t = xbc.shape[0]
    nc = seq // CHUNK
    n_chunks = t // CHUNK

    def body(xbc_ref, z_ref, dt_ref, par_ref, dsk_ref, nw_ref, yraw_ref, ycat_ref, hprev_ref, h_ref):
        @pl.when(pl.program_id(0) % nc == 0)
        def _():
            h_ref[...] = jnp.zeros_like(h_ref)
        hprev_ref[0] = h_ref[...]
        _, _, cs, cst, causal, _, dt_c, ecs_c, w_c, pair_cols = _ssd_prep(dt_ref[...], par_ref[...])
        cs_last = cs[CHUNK - 1:CHUNK, :]
        causal2 = jnp.concatenate([causal, causal], axis=1)
        left = lax.broadcasted_iota(jnp.int32, (CHUNK, 128), 1) < HEADDIM
        x = xbc_ref[:, 0:D_SSD]
        xdt = x * dt_c
        amat = (w_c * xdt).astype(BF16)
        zz = z_ref[...]
        silu_z = zz * _sigmoid(zz)
        for g in range(N_GROUPS):
            gs = slice(g * GW, (g + 1) * GW)
            bg = xbc_ref[:, D_SSD + g * N_STATE:D_SSD + (g + 1) * N_STATE].astype(BF16)
            cg = xbc_ref[:, D_SSD + (N_GROUPS + g) * N_STATE:D_SSD + (N_GROUPS + g + 1) * N_STATE].astype(BF16)
            scores = lax.dot_general(cg, bg, NT, preferred_element_type=F32)
            scores2 = jnp.concatenate([scores, scores], axis=1)
            hg = h_ref[gs, :]
            p_all = lax.dot_general(cg, hg.astype(BF16), NT, preferred_element_type=F32)
            ys = []
            for q in range(GW // 128):
                pair = g * (GW // 128) + q
                decay = _pair_decay(pair_cols[pair], cst, pair, causal2)
                mcat = (scores2 * decay).astype(BF16)
                ys.append(jnp.dot(mcat, _stack_heads(xdt[:, pair * 128:(pair + 1) * 128], left),
                                  preferred_element_type=F32))
            yg = jnp.concatenate(ys, axis=1) + ecs_c[:, gs] * p_all + x[:, gs] * dsk_ref[:, gs]
            s_new = lax.dot_general(amat[:, gs], bg, TN, preferred_element_type=F32)
            for j in range(HPG):
                hh = g * HPG + j
                js = slice(j * HEADDIM, (j + 1) * HEADDIM)
                h_ref[g * GW + j * HEADDIM:g * GW + (j + 1) * HEADDIM, :] = (
                    hg[js, :] * jnp.exp(cs_last[:, hh:hh + 1]) + s_new[js, :])
            yraw_ref[:, gs] = yg
            v = yg * silu_z[:, gs]
            r = lax.rsqrt(jnp.mean(v * v, axis=-1, keepdims=True) + EPS)
            ycat_ref[:, gs] = (v * r * nw_ref[:, gs]).astype(BF16)

    row = lambda w: pl.BlockSpec((CHUNK, w), lambda i: (i, 0))
    full = lambda s: pl.BlockSpec(s, lambda i: (0,) * len(s))
    return pl.pallas_call(
        body, name="ssd_fwd", grid=(n_chunks,),
        out_shape=(jax.ShapeDtypeStruct((t, D_SSD), F32), jax.ShapeDtypeStruct((t, D_SSD + D_S5), BF16),
                   jax.ShapeDtypeStruct((n_chunks, D_SSD, N_STATE), F32)),
        in_specs=[row(D_XBC), row(D_SSD), row(DT_PAD), full((8, 128)), full((1, D_SSD)), full((1, D_SSD))],
        out_specs=(row(D_SSD), row(D_SSD), pl.BlockSpec((1, D_SSD, N_STATE), lambda i: (i, 0, 0))),
        scratch_shapes=[pltpu.VMEM((D_SSD, N_STATE), F32)],
        compiler_params=_params(40),
    )(xbc, z, dt_raw, par, dsk, normw)


S5_CW = 512
S5_BLOCKS = 4


def _tile_scan(in_re, in_im, out_re, out_im, carry_re, carry_im, pw_re, pw_im, n_tiles, reverse):
    steps = (1, 2, 4)
    for cc in range(S5_N // S5_CW):
        cols = slice(cc * S5_CW, (cc + 1) * S5_CW)
        a_re, a_im = pw_re[:, cols], pw_im[:, cols]
        rid = lax.broadcasted_iota(jnp.int32, (8, S5_CW), 0)
        pows = []
        for d in steps:
            k = 8 - d if reverse else d - 1
            keep = (rid < 8 - d) if reverse else (rid >= d)
            pows.append((jnp.where(keep, pw_re[k:k + 1, cols], 0.0), jnp.where(keep, pw_im[k:k + 1, cols], 0.0)))

        def tile(i, carry, cols=cols, pows=pows, a_re=a_re, a_im=a_im):
            r = (n_tiles - 1 - i) if reverse else i
            rows = pl.ds(pl.multiple_of(r * 8, 8), 8)
            xr, xi = in_re[rows, cols], in_im[rows, cols]
            for (pr, pi), d in zip(pows, steps):
                shift = 8 - d if reverse else d
                sr, si = pltpu.roll(xr, shift, axis=0), pltpu.roll(xi, shift, axis=0)
                xr, xi = xr + pr * sr - pi * si, xi + pr * si + pi * sr
            cr, ci = carry
            xr, xi = xr + a_re * cr - a_im * ci, xi + a_re * ci + a_im * cr
            out_re[rows, cols] = xr
            out_im[rows, cols] = xi
            edge = slice(0, 1) if reverse else slice(7, 8)
            return (jnp.broadcast_to(xr[edge], (8, S5_CW)), jnp.broadcast_to(xi[edge], (8, S5_CW)))

        c0 = (jnp.broadcast_to(carry_re[0:1, cols], (8, S5_CW)), jnp.broadcast_to(carry_im[0:1, cols], (8, S5_CW)))
        cr, ci = lax.fori_loop(0, n_tiles, tile, c0, unroll=True)
        carry_re[:, cols] = cr
        carry_im[:, cols] = ci


def _s5_params_math(ar, ai, ldt, br, bi):
    dt = jnp.exp(ldt)
    mag = jnp.exp(ar * dt)
    ang = ai * dt
    ab_re = mag * jnp.cos(ang)
    ab_im = mag * jnp.sin(ang)
    den = ar * ar + ai * ai
    n_re = ab_re - 1.0
    coef_re = (n_re * ar + ab_im * ai) / den
    coef_im = (ab_im * ar - n_re * ai) / den
    bb_re = coef_re * br - coef_im * bi
    bb_im = coef_re * bi + coef_im * br
    return ab_re, ab_im, bb_re, bb_im


def _s5_params_fwd(ar, ai, ldt, br, bi):
    def body(ar_ref, ai_ref, ldt_ref, br_ref, bi_ref, bbr_ref, bbi_ref, pfr_ref, pfi_ref, prr_ref, pri_ref):
        ab_re, ab_im, bb_re, bb_im = _s5_params_math(ar_ref[...], ai_ref[...], ldt_ref[...], br_ref[...], bi_ref[...])
        bbr_ref[...] = bb_re
        bbi_ref[...] = bb_im
        pr, pi = ab_re, ab_im
        for k in range(8):
            pfr_ref[k:k + 1, :] = pr
            pfi_ref[k:k + 1, :] = pi
            prr_ref[7 - k:8 - k, :] = pr
            pri_ref[7 - k:8 - k, :] = -pi
            pr, pi = pr * ab_re - pi * ab_im, pr * ab_im + pi * ab_re

    b16 = jax.ShapeDtypeStruct((S5_CH, S5_N), F32)
    p8 = jax.ShapeDtypeStruct((8, S5_N), F32)
    return pl.pallas_call(body, name="s5_params_fwd", out_shape=(b16, b16, p8, p8, p8, p8),
                          compiler_params=_params(32))(ar, ai, ldt, br, bi)


def _s5_params_bwd(ar, ai, ldt, br, bi, d_ab_re, d_ab_im, d_bb_re, d_bb_im):
    def body(ar_ref, ai_ref, ldt_ref, br_ref, bi_ref, dar_ref, dai_ref, dbr_ref, dbi_ref,
             gar_ref, gai_ref, gldt_ref, gbr_ref, gbi_ref):
        _, vjp = jax.vjp(_s5_params_math, ar_ref[...], ai_ref[...], ldt_ref[...], br_ref[...], bi_ref[...])
        g_ar, g_ai, g_ldt, g_br, g_bi = vjp((dar_ref[...], dai_ref[...], dbr_ref[...], dbi_ref[...]))
        gar_ref[...] = g_ar
        gai_ref[...] = g_ai
        gbr_ref[...] = g_br
        gbi_ref[...] = g_bi
        lane = lax.broadcasted_iota(jnp.int32, (S5_N, 128), 0) // S5_P
        grp = lax.broadcasted_iota(jnp.int32, (S5_N, 128), 1)
        fold = (lane == grp).astype(F32)
        gldt_ref[...] = jnp.dot(g_ldt, fold, preferred_element_type=F32, precision=HIGHEST)

    v1 = jax.ShapeDtypeStruct((1, S5_N), F32)
    b16 = jax.ShapeDtypeStruct((S5_CH, S5_N), F32)
    return pl.pallas_call(body, name="s5_params_bwd",
                          out_shape=(v1, v1, jax.ShapeDtypeStruct((1, 128), F32), b16, b16),
                          compiler_params=_params(32))(ar, ai, ldt, br, bi, d_ab_re, d_ab_im, d_bb_re, d_bb_im)


def _s5_fwd(u5, bb_re, bb_im, cc_re, cc_im, pf_re, pf_im, s5d, w_glu, b_glu, ycat, seq):
    t = u5.shape[0]
    tb = 256
    npb = seq // tb

    def body(u_ref, bbr_ref, bbi_ref, ccr_ref, cci_ref, pfr_ref, pfi_ref, d_ref, wg_ref, bg_ref, ycat_hbm,
             sre_ref, sim_ref, ypre_ref, y5_ref, bur, bui, car, cai):
        del ycat_hbm

        @pl.when(pl.program_id(0) % npb == 0)
        def _():
            car[...] = jnp.zeros_like(car)
            cai[...] = jnp.zeros_like(cai)
        u = u_ref[...]
        ub = u.astype(BF16)
        for j in range(S5_BLOCKS):
            ch, st = slice(j * 128, (j + 1) * 128), slice(j * 512, (j + 1) * 512)
            bur[:, st] = jnp.dot(ub[:, ch], bbr_ref[j], preferred_element_type=F32)
            bui[:, st] = jnp.dot(ub[:, ch], bbi_ref[j], preferred_element_type=F32)
        _tile_scan(bur, bui, sre_ref, sim_ref, car, cai, pfr_ref, pfi_ref, tb // 8, reverse=False)
        cs_y = []
        for j in range(S5_BLOCKS):
            st = slice(j * 512, (j + 1) * 512)
            cs_y.append(_mm(sre_ref[:, st], ccr_ref[j]) - _mm(sim_ref[:, st], cci_ref[j]))
        ypre = jnp.concatenate(cs_y, axis=1) + u * d_ref[...]
        ypre_ref[...] = ypre
        yg = _gelu(ypre)
        y5_ref[...] = (yg * _sigmoid(_mm(yg, wg_ref[...]) + bg_ref[...])).astype(BF16)

    row = lambda w: pl.BlockSpec((tb, w), lambda i: (i, 0))
    full = lambda a: pl.BlockSpec(a.shape, lambda i: (0,) * a.ndim)
    return pl.pallas_call(
        body, name="s5_fwd", grid=(t // tb,),
        out_shape=(jax.ShapeDtypeStruct((t, S5_N), F32), jax.ShapeDtypeStruct((t, S5_N), F32),
                   jax.ShapeDtypeStruct((t, D_S5), F32), jax.ShapeDtypeStruct(ycat.shape, BF16)),
        in_specs=[row(D_S5), full(bb_re), full(bb_im), full(cc_re), full(cc_im), full(pf_re), full(pf_im),
                  full(s5d), full(w_glu), full(b_glu), ANY],
        out_specs=(row(S5_N), row(S5_N), row(D_S5), pl.BlockSpec((tb, D_S5), lambda i: (i, D_SSD // D_S5))),
        input_output_aliases={10: 3},
        scratch_shapes=[pltpu.VMEM((tb, S5_N), F32), pltpu.VMEM((tb, S5_N), F32),
                        pltpu.VMEM((8, S5_N), F32), pltpu.VMEM((8, S5_N), F32)],
        compiler_params=_params(48),
    )(u5, bb_re, bb_im, cc_re, cc_im, pf_re, pf_im, s5d, w_glu, b_glu, ycat)


def _layer_norm(r, g, b):
    mu = jnp.mean(r, axis=-1, keepdims=True)
    xc = r - mu
    rstd = lax.rsqrt(jnp.mean(xc * xc, axis=-1, keepdims=True) + EPS)
    xhat = xc * rstd
    return xhat * g + b, xhat, rstd


def _layer_norm_bwd(dy, xhat, rstd, g):
    dxhat = dy * g
    return rstd * (dxhat - jnp.mean(dxhat, axis=-1, keepdims=True)
                   - xhat * jnp.mean(dxhat * xhat, axis=-1, keepdims=True))


def _out_ln1(ycat, x2, mod3, w_out, ln1, seq):
    t = x2.shape[0]
    tb = 512
    npb = seq // tb

    def body(y_ref, x_ref, mod_ref, w_ref, ln_ref, mix_ref, x1_ref):
        m = mod_ref[0]
        mix = jnp.dot(y_ref[...], w_ref[...], preferred_element_type=F32)
        mix_ref[...] = mix
        r1 = ALPHA * x_ref[...] + (1.0 + m[2:3]) * mix
        x1_ref[...] = _layer_norm(r1, ln_ref[0:1], ln_ref[1:2])[0]

    row = lambda w: pl.BlockSpec((tb, w), lambda i: (i, 0))
    return pl.pallas_call(
        body, name="out_ln1", grid=(t // tb,),
        out_shape=(jax.ShapeDtypeStruct((t, D_MODEL), F32), jax.ShapeDtypeStruct((t, D_MODEL), F32)),
        in_specs=[row(D_SSD + D_S5), row(D_MODEL), pl.BlockSpec((1, N_MOD, D_MODEL), lambda i: (i // npb, 0, 0)),
                  pl.BlockSpec(w_out.shape, lambda i: (0, 0)), pl.BlockSpec(ln1.shape, lambda i: (0, 0))],
        out_specs=(row(D_MODEL), row(D_MODEL)), compiler_params=_params(48),
    )(ycat, x2, mod3, w_out, ln1)


def _mlp_fwd_bwd(x1, tgt, mod3, w1, w2, vec1, b1, seq):
    t = x1.shape[0]
    tb = 256
    npb = seq // tb
    n_fb, _, fb = w1.shape

    def body(x1_ref, tgt_ref, mod_ref, w1_hbm, w2_hbm, v_ref, b1_ref,
             dx1_ref, u2_ref, h_ref, dhp_ref, do_ref, gacc_ref, db1_ref, bacc_ref, w1_v, w2_v, sem1, sem2):
        i = pl.program_id(0)
        @pl.when(i == 0)
        def _():
            cps = [pltpu.make_async_copy(w1_hbm.at[k], w1_v.at[:, k * fb:(k + 1) * fb], sem1.at[k])
                   for k in range(n_fb)]
            for cp in cps:
                cp.start()
            for cp in cps:
                cp.wait()
        _load_once(w2_hbm, w2_v, sem2)

        @pl.when(i == 0)
        def _():
            gacc_ref[...] = jnp.zeros_like(gacc_ref)
            db1_ref[...] = jnp.zeros_like(db1_ref)

        @pl.when(i % npb == 0)
        def _():
            bacc_ref[...] = jnp.zeros_like(bacc_ref)

        m = mod_ref[0]
        sh2, sc2, g2 = m[3:4], m[4:5], m[5:6]
        x1v = x1_ref[...]
        u2 = (x1v * (1.0 + sc2) + sh2).astype(BF16)
        u2_ref[...] = u2
        hr = jnp.maximum(jnp.dot(u2, w1_v[...], preferred_element_type=F32) + b1_ref[...], 0.0)
        hb = (hr * hr).astype(BF16)
        h_ref[...] = hb
        o = jnp.dot(hb, w2_v[...], preferred_element_type=F32) + v_ref[0:1]
        r2 = ALPHA * x1v + (1.0 + g2) * o
        y, xhat, rstd = _layer_norm(r2, v_ref[1:2], v_ref[2:3])
        err = y - tgt_ref[...]
        dy = err * (1.0 / D_MODEL)
        dr2 = _layer_norm_bwd(dy, xhat, rstd, v_ref[1:2])
        do = (1.0 + g2) * dr2
        dob = do.astype(BF16)
        do_ref[...] = dob
        gacc_ref[0:1, :] += jnp.sum(dy * xhat, axis=0, keepdims=True)
        gacc_ref[1:2, :] += jnp.sum(dy, axis=0, keepdims=True)
        gacc_ref[2:3, :] += jnp.sum(do, axis=0, keepdims=True)
        gacc_ref[3:4, :] += jnp.sum(err * err, axis=0, keepdims=True)
        dhpre = lax.dot_general(dob, w2_v[...], NT, preferred_element_type=F32) * (2.0 * hr)
        dhpb = dhpre.astype(BF16)
        dhp_ref[...] = dhpb
        db1_ref[...] += jnp.sum(dhpre, axis=0, keepdims=True)
        du2 = lax.dot_general(dhpb, w1_v[...], NT, preferred_element_type=F32)
        dx1_ref[...] = ALPHA * dr2 + du2 * (1.0 + sc2)
        bacc_ref[0, 0:1, :] += jnp.sum(du2, axis=0, keepdims=True)
        bacc_ref[0, 1:2, :] += jnp.sum(du2 * x1v, axis=0, keepdims=True)
        bacc_ref[0, 2:3, :] += jnp.sum(dr2 * o, axis=0, keepdims=True)

    row = lambda w: pl.BlockSpec((tb, w), lambda i: (i, 0))
    return pl.pallas_call(
        body, name="mlp_fwd_bwd", grid=(t // tb,),
        out_shape=(jax.ShapeDtypeStruct((t, D_MODEL), F32), jax.ShapeDtypeStruct((t, D_MODEL), BF16),
                   jax.ShapeDtypeStruct((t, D_FF), BF16), jax.ShapeDtypeStruct((t, D_FF), BF16),
                   jax.ShapeDtypeStruct((t, D_MODEL), BF16), jax.ShapeDtypeStruct((8, D_MODEL), F32),
                   jax.ShapeDtypeStruct((1, D_FF), F32), jax.ShapeDtypeStruct((t // seq, 8, D_MODEL), F32)),
        in_specs=[row(D_MODEL), row(D_MODEL), pl.BlockSpec((1, N_MOD, D_MODEL), lambda i: (i // npb, 0, 0)), ANY, ANY,
                  pl.BlockSpec(vec1.shape, lambda i: (0, 0)), pl.BlockSpec(b1.shape, lambda i: (0, 0))],
        out_specs=(row(D_MODEL), row(D_MODEL), row(D_FF), row(D_FF), row(D_MODEL),
                   pl.BlockSpec((8, D_MODEL), lambda i: (0, 0)), pl.BlockSpec((1, D_FF), lambda i: (0, 0)),
                   pl.BlockSpec((1, 8, D_MODEL), lambda i: (i // npb, 0, 0))),
        scratch_shapes=[pltpu.VMEM((D_MODEL, n_fb * fb), BF16), pltpu.VMEM((D_FF, D_MODEL), BF16),
                        pltpu.SemaphoreType.DMA((n_fb,)), pltpu.SemaphoreType.DMA],
        compiler_params=_params(60),
    )(x1, tgt, mod3, w1, w2, vec1, b1)


def _ln1_out_bwd(dx1, x2, mix, mod3, w_out, ln1, seq):
    t = x2.shape[0]
    tb = 512
    npb = seq // tb

    def body(dx1_ref, x_ref, mix_ref, mod_ref, w_ref, ln_ref, dmix_ref, dxa_ref, dys_ref, dy5_ref, gacc_ref, bacc_ref):
        i = pl.program_id(0)

        @pl.when(i == 0)
        def _():
            gacc_ref[...] = jnp.zeros_like(gacc_ref)

        @pl.when(i % npb == 0)
        def _():
            bacc_ref[...] = jnp.zeros_like(bacc_ref)

        m = mod_ref[0]
        mix = mix_ref[...]
        r1 = ALPHA * x_ref[...] + (1.0 + m[2:3]) * mix
        _, xhat, rstd = _layer_norm(r1, ln_ref[0:1], ln_ref[1:2])
        dx1v = dx1_ref[...]
        dr1 = _layer_norm_bwd(dx1v, xhat, rstd, ln_ref[0:1])
        gacc_ref[0:1, :] += jnp.sum(dx1v * xhat, axis=0, keepdims=True)
        gacc_ref[1:2, :] += jnp.sum(dx1v, axis=0, keepdims=True)
        bacc_ref[0, 0:1, :] += jnp.sum(dr1 * mix, axis=0, keepdims=True)
        dmix = ((1.0 + m[2:3]) * dr1).astype(BF16)
        dmix_ref[...] = dmix
        dxa_ref[...] = ALPHA * dr1
        dys_ref[...] = lax.dot_general(dmix, w_ref[0:D_SSD, :], NT, preferred_element_type=F32)
        dy5_ref[...] = lax.dot_general(dmix, w_ref[D_SSD:, :], NT, preferred_element_type=F32)

    row = lambda w: pl.BlockSpec((tb, w), lambda i: (i, 0))
    return pl.pallas_call(
        body, name="ln1_out_bwd", grid=(t // tb,),
        out_shape=(jax.ShapeDtypeStruct((t, D_MODEL), BF16), jax.ShapeDtypeStruct((t, D_MODEL), F32),
                   jax.ShapeDtypeStruct((t, D_SSD), F32), jax.ShapeDtypeStruct((t, D_S5), F32),
                   jax.ShapeDtypeStruct((8, D_MODEL), F32), jax.ShapeDtypeStruct((t // seq, 8, D_MODEL), F32)),
        in_specs=[row(D_MODEL), row(D_MODEL), row(D_MODEL), pl.BlockSpec((1, N_MOD, D_MODEL), lambda i: (i // npb, 0, 0)),
                  pl.BlockSpec(w_out.shape, lambda i: (0, 0)), pl.BlockSpec(ln1.shape, lambda i: (0, 0))],
        out_specs=(row(D_MODEL), row(D_MODEL), row(D_SSD), row(D_S5), pl.BlockSpec((8, D_MODEL), lambda i: (0, 0)),
                   pl.BlockSpec((1, 8, D_MODEL), lambda i: (i // npb, 0, 0))),
        compiler_params=_params(48),
    )(dx1, x2, mix, mod3, w_out, ln1)


def _s5_bwd(dy5, ypre, u5, s_re, s_im, bb_re, bb_im, cc_re, cc_im, pr_re, pr_im, s5d, w_glu, b_glu, seq):
    t = u5.shape[0]
    tb = 256
    npb = seq // tb
    n_blocks = t // tb

    def blk(i):
        return (i // npb) * npb + (npb - 1 - i % npb)

    def body(dy_ref, ypre_ref, u_ref, sre_ref, sim_ref, hre_ref, him_ref, bbr_ref, bbi_ref, ccr_ref, cci_ref,
             prr_ref, pri_ref, d_ref, wg_ref, bg_ref,
             du_ref, vacc_ref, sacc_ref, dcc_ref, dbb_ref, dwg_ref, dsr, dsi, gr, gi, car, cai):
        i = pl.program_id(0)

        @pl.when(i == 0)
        def _():
            for acc in (vacc_ref, sacc_ref, dcc_ref, dbb_ref, dwg_ref):
                acc[...] = jnp.zeros_like(acc)

        @pl.when(i % npb == 0)
        def _():
            car[...] = jnp.zeros_like(car)
            cai[...] = jnp.zeros_like(cai)

        dy = dy_ref[...]
        ypre = ypre_ref[...]
        u = u_ref[...]
        ub = u.astype(BF16)
        yg = _gelu(ypre)
        sg = _sigmoid(_mm(yg, wg_ref[...]) + bg_ref[...])
        dq = dy * yg * sg * (1.0 - sg)
        dqb = dq.astype(BF16)
        dyg = dy * sg + lax.dot_general(dqb, wg_ref[...], NT, preferred_element_type=F32)
        dyp = dyg * _gelu_grad(ypre)
        dypb = dyp.astype(BF16)
        dwg_ref[...] += lax.dot_general(yg.astype(BF16), dqb, TN, preferred_element_type=F32)
        blocks = [(slice(j * 128, (j + 1) * 128), slice(j * 512, (j + 1) * 512)) for j in range(S5_BLOCKS)]
        for j, (ch, st) in enumerate(blocks):
            dsr[:, st] = lax.dot_general(dypb[:, ch], ccr_ref[j], NT, preferred_element_type=F32)
            dsi[:, st] = -lax.dot_general(dypb[:, ch], cci_ref[j], NT, preferred_element_type=F32)
        _tile_scan(dsr, dsi, gr, gi, car, cai, prr_ref, pri_ref, tb // 8, reverse=True)
        g_re, g_im = gr[...], gi[...]
        first_rows = (i % npb) == npb - 1
        hre = jnp.where(first_rows, 0.0, hre_ref[...])
        him = jnp.where(first_rows, 0.0, him_ref[...])
        s_re_v, s_im_v = sre_ref[...], sim_ref[...]
        sp_re = pltpu.roll(jnp.concatenate([hre, s_re_v], axis=0), 1, axis=0)[8:8 + tb]
        sp_im = pltpu.roll(jnp.concatenate([him, s_im_v], axis=0), 1, axis=0)[8:8 + tb]
        vacc_ref[0:1, :] += jnp.sum(g_re * sp_re + g_im * sp_im, axis=0, keepdims=True)
        vacc_ref[1:2, :] += jnp.sum(g_im * sp_re - g_re * sp_im, axis=0, keepdims=True)
        grb, gib = g_re.astype(BF16), g_im.astype(BF16)
        srb, sib = s_re_v.astype(BF16), s_im_v.astype(BF16)
        du_cols = []
        for j, (ch, st) in enumerate(blocks):
            dcc_ref[j] += lax.dot_general(srb[:, st], dypb[:, ch], TN, preferred_element_type=F32)
            dcc_ref[S5_BLOCKS + j] -= lax.dot_general(sib[:, st], dypb[:, ch], TN, preferred_element_type=F32)
            dbb_ref[j] += lax.dot_general(ub[:, ch], grb[:, st], TN, preferred_element_type=F32)
            dbb_ref[S5_BLOCKS + j] += lax.dot_general(ub[:, ch], gib[:, st], TN, preferred_element_type=F32)
            du_cols.append(lax.dot_general(grb[:, st], bbr_ref[j], NT, preferred_element_type=F32)
                           + lax.dot_general(gib[:, st], bbi_ref[j], NT, preferred_element_type=F32))
        du_ref[...] = jnp.concatenate(du_cols, axis=1) + dyp * d_ref[...]
        sacc_ref[0:1, :] += jnp.sum(dyp * u, axis=0, keepdims=True)
        sacc_ref[1:2, :] += jnp.sum(dq, axis=0, keepdims=True)

    row = lambda w: pl.BlockSpec((tb, w), lambda i: (blk(i), 0))
    halo = pl.BlockSpec((8, S5_N), lambda i: (jnp.maximum(blk(i) * (tb // 8) - 1, 0), 0))
    full = lambda a: pl.BlockSpec(a.shape, lambda i: (0,) * a.ndim)
    acc = lambda s: pl.BlockSpec(s, lambda i: (0,) * len(s))
    acc_shapes = [(8, S5_N), (8, D_S5), (2 * S5_BLOCKS, 512, 128), (2 * S5_BLOCKS, 128, 512), (D_S5, D_S5)]
    return pl.pallas_call(
        body, name="s5_bwd", grid=(n_blocks,),
        out_shape=(jax.ShapeDtypeStruct((t, D_S5), F32),) + tuple(jax.ShapeDtypeStruct(s, F32) for s in acc_shapes),
        in_specs=[row(D_S5), row(D_S5), row(D_S5), row(S5_N), row(S5_N), halo, halo, full(bb_re), full(bb_im),
                  full(cc_re), full(cc_im), full(pr_re), full(pr_im), full(s5d), full(w_glu), full(b_glu)],
        out_specs=(row(D_S5),) + tuple(acc(s) for s in acc_shapes),
        scratch_shapes=[pltpu.VMEM((tb, S5_N), F32), pltpu.VMEM((tb, S5_N), F32), pltpu.VMEM((tb, S5_N), F32),
                        pltpu.VMEM((tb, S5_N), F32), pltpu.VMEM((8, S5_N), F32), pltpu.VMEM((8, S5_N), F32)],
        compiler_params=_params(56),
    )(dy5, ypre, u5, s_re, s_im, s_re, s_im, bb_re, bb_im, cc_re, cc_im, pr_re, pr_im, s5d, w_glu, b_glu)


def _ssd_bwd(dyssd, yraw, z, xbc, dt_raw, hprev, par, dsk, normw, seq):
    t = xbc.shape[0]
    nc = seq // CHUNK
    n_chunks = t // CHUNK
    fold = _head_fold()

    def blk(i):
        return (i // nc) * nc + (nc - 1 - i % nc)

    def body(dy_ref, yraw_ref, z_ref, xbc_ref, dt_ref, hprev_ref, par_ref, dsk_ref, nw_ref, fold_ref,
             dxbc_ref, dz_ref, ddt_ref, dpar_ref, cacc_ref, dh_ref, dyr_ref):
        i = pl.program_id(0)

        @pl.when(i == 0)
        def _():
            dpar_ref[...] = jnp.zeros_like(dpar_ref)
            cacc_ref[...] = jnp.zeros_like(cacc_ref)

        @pl.when(i % nc == 0)
        def _():
            dh_ref[...] = jnp.zeros_like(dh_ref)

        zz = z_ref[...]
        sz = _sigmoid(zz)
        silu_z = zz * sz
        yraw = yraw_ref[...]
        for g in range(N_GROUPS):
            sl = slice(g * GW, (g + 1) * GW)
            v = yraw[:, sl] * silu_z[:, sl]
            r = lax.rsqrt(jnp.mean(v * v, axis=-1, keepdims=True) + EPS)
            dyg = dy_ref[:, sl]
            cacc_ref[1:2, sl] += jnp.sum(dyg * v * r, axis=0, keepdims=True)
            dyw = dyg * nw_ref[:, sl]
            dv = r * dyw - v * (r * r * r) * jnp.mean(dyw * v, axis=-1, keepdims=True)
            dyr_ref[:, sl] = dv * silu_z[:, sl]
            dz_ref[:, sl] = dv * yraw[:, sl] * (sz[:, sl] * (1.0 + zz[:, sl] * (1.0 - sz[:, sl])))

        dt, a, cs, cst, causal, tri, dt_c, ecs_c, w_c, pair_cols = _ssd_prep(dt_ref[...], par_ref[...])
        cs_last = cs[CHUNK - 1:CHUNK, :]
        causal2 = jnp.concatenate([causal, causal], axis=1)
        lane = lax.broadcasted_iota(jnp.int32, (CHUNK, 128), 1)
        left = lane < HEADDIM
        lane1 = lax.broadcasted_iota(jnp.int32, (1, 128), 1)
        x = xbc_ref[:, 0:D_SSD]
        xdt = x * dt_c
        dyr = dyr_ref[...]
        dyrb = dyr.astype(BF16)
        cacc_ref[0:1, :] += jnp.sum(dyr * x, axis=0, keepdims=True)
        dlast = jnp.zeros((1, 128), F32)
        dxdt_cols, diag_all, dww_cols = [], [], []
        for g in range(N_GROUPS):
            gs = slice(g * GW, (g + 1) * GW)
            b_sl = slice(D_SSD + g * N_STATE, D_SSD + (g + 1) * N_STATE)
            c_sl = slice(D_SSD + (N_GROUPS + g) * N_STATE, D_SSD + (N_GROUPS + g + 1) * N_STATE)
            bg = xbc_ref[:, b_sl].astype(BF16)
            cg = xbc_ref[:, c_sl].astype(BF16)
            scores = lax.dot_general(cg, bg, NT, preferred_element_type=F32)
            scores2 = jnp.concatenate([scores, scores], axis=1)
            hg = hprev_ref[0, gs, :]
            hgb = hg.astype(BF16)
            dhg = dh_ref[gs, :]
            dhgb = dhg.astype(BF16)
            q_all = lax.dot_general(bg, dhgb, NT, preferred_element_type=F32)
            dscores = jnp.zeros((CHUNK, CHUNK), F32)
            diag_cols = []
            for q in range(GW // 128):
                pair = g * (GW // 128) + q
                ps = slice(pair * 128, (pair + 1) * 128)
                decay = _pair_decay(pair_cols[pair], cst, pair, causal2)
                mcat = (scores2 * decay).astype(BF16)
                dyp = dyrb[:, ps]
                dm = lax.dot_general(dyp, _stack_heads(xdt[:, ps], left), NT, preferred_element_type=F32)
                dmd = dm * decay
                dscores = dscores + dmd[:, 0:CHUNK] + dmd[:, CHUNK:]
                rr = lax.dot_general(mcat, dyp, TN, preferred_element_type=F32)
                diag_cols.append(jnp.where(left, rr[0:CHUNK], rr[CHUNK:]))
            wq = w_c[:, gs] * q_all
            diag_g = jnp.concatenate(diag_cols, axis=1)
            diag_all.append(diag_g)
            dxdt_cols.append(diag_g + wq)
            dww_cols.append(wq * xdt[:, gs])
            dp = (ecs_c[:, gs] * dyr[:, gs]).astype(BF16)
            amat = (w_c[:, gs] * xdt[:, gs]).astype(BF16)
            dsb = dscores.astype(BF16)
            dxbc_ref[:, c_sl] = (jnp.dot(dsb, bg, preferred_element_type=F32)
                                 + jnp.dot(dp, hgb, preferred_element_type=F32))
            dxbc_ref[:, b_sl] = (lax.dot_general(dsb, cg, TN, preferred_element_type=F32)
                                 + jnp.dot(amat, dhgb, preferred_element_type=F32))
            dh_in = lax.dot_general(dp, cg, TN, preferred_element_type=F32)
            for j in range(HPG):
                hh = g * HPG + j
                js = slice(j * HEADDIM, (j + 1) * HEADDIM)
                ecl = jnp.exp(cs_last[:, hh:hh + 1])
                dlast = dlast + jnp.where(lane1 == hh, ecl * jnp.sum(dhg[js, :] * hg[js, :]), 0.0)
                dh_ref[g * GW + j * HEADDIM:g * GW + (j + 1) * HEADDIM, :] = ecl * dhg[js, :] + dh_in[js, :]
        dxdt = jnp.concatenate(dxdt_cols, axis=1)
        dxbc_ref[:, 0:D_SSD] = dxdt * dt_c + dyr * dsk_ref[...]
        dww = _mm(jnp.concatenate(dww_cols, axis=1), fold_ref[...])
        dcs = _dot3(dyrb.astype(F32) * (yraw - x * dsk_ref[...])
                    - xdt.astype(BF16).astype(F32) * jnp.concatenate(diag_all, axis=1), fold_ref[...]) - dww
        rowid = lax.broadcasted_iota(jnp.int32, (CHUNK, 128), 0)
        dcs = dcs + jnp.where(rowid == CHUNK - 1, jnp.sum(dww, axis=0, keepdims=True) + dlast, 0.0)
        dadt = _dot3_left(tri, dcs, TN)
        ddt = _mm(dxdt * x, fold_ref[...]) + dadt * a
        da = jnp.sum(dadt * dt, axis=0, keepdims=True)
        ddt_raw = ddt * _sigmoid(dt_ref[...] + par_ref[0:1])
        ddt_raw = jnp.where(lane < N_HEADS, ddt_raw, 0.0)
        ddt_ref[...] = ddt_raw
        dpar_ref[0:1, :] += jnp.sum(ddt_raw, axis=0, keepdims=True)
        dpar_ref[1:2, :] += jnp.where(lane1 < N_HEADS, da * a, 0.0)

    row = lambda w: pl.BlockSpec((CHUNK, w), lambda i: (blk(i), 0))
    full = lambda s: pl.BlockSpec(s, lambda i: (0,) * len(s))
    return pl.pallas_call(
        body, name="ssd_bwd", grid=(n_chunks,),
        out_shape=(jax.ShapeDtypeStruct((t, D_XBC), F32), jax.ShapeDtypeStruct((t, D_SSD), F32),
                   jax.ShapeDtypeStruct((t, DT_PAD), F32), jax.ShapeDtypeStruct((8, 128), F32),
                   jax.ShapeDtypeStruct((8, D_SSD), F32)),
        in_specs=[row(D_SSD), row(D_SSD), row(D_SSD), row(D_XBC), row(DT_PAD),
                  pl.BlockSpec((1, D_SSD, N_STATE), lambda i: (blk(i), 0, 0)),
                  full((8, 128)), full((1, D_SSD)), full((1, D_SSD)), full(fold.shape)],
        out_specs=(row(D_XBC), row(D_SSD), row(DT_PAD), full((8, 128)), full((8, D_SSD))),
        scratch_shapes=[pltpu.VMEM((D_SSD, N_STATE), F32), pltpu.VMEM((CHUNK, D_SSD), F32)],
        compiler_params=_params(48),
    )(dyssd, yraw, z, xbc, dt_raw, hprev, par, dsk, normw, fold)


def _conv_proj_bwd(dz, dxbc, dsilu, xbc_pre, ddt, du5, x2, dxa, mod3, conv_w, w_in_pad, seq):
    t = x2.shape[0]
    tb = 256
    npb = seq // tb
    n_blocks = t // tb
    cw = 512

    def blk(i):
        return (i // npb) * npb + (npb - 1 - i % npb)

    def body(dz_ref, d_ref, ds_ref, cur_ref, halo_ref, ddt_ref, du5_ref, x_ref, dxa_ref, mod_ref, cw_ref, w_hbm,
             gx_ref, u_ref, dxp_ref, bacc_ref, acc_ref, w_vmem, win_x, win_d, sem):
        i = pl.program_id(0)
        _load_once(w_hbm, w_vmem, sem)

        @pl.when(i == 0)
        def _():
            acc_ref[...] = jnp.zeros_like(acc_ref)

        @pl.when(i % npb == 0)
        def _():
            bacc_ref[...] = jnp.zeros_like(bacc_ref)
            win_d[tb:tb + 8, :] = jnp.zeros((8, D_XBC), F32)

        @pl.when(i % npb != 0)
        def _():
            win_d[tb:tb + 8, :] = win_d[0:8, :]

        first_rows = (i % npb) == npb - 1
        win_x[0:8, :] = jnp.where(first_rows, 0.0, halo_ref[...])
        win_x[8:8 + tb, :] = cur_ref[...]
        w = cw_ref[...]
        for k in range(D_XBC // cw):
            cols = slice(k * cw, (k + 1) * cw)
            dpre = d_ref[:, cols] * ds_ref[:, cols]
            win_d[0:tb, cols] = dpre
            for j in range(4):
                acc_ref[3 - j:4 - j, cols] += jnp.sum(dpre * win_x[8 - j:8 - j + tb, cols], axis=0, keepdims=True)
            acc_ref[4:5, cols] += jnp.sum(dpre, axis=0, keepdims=True)
            dxp = w[3:4, cols] * dpre
            for j in (1, 2, 3):
                dxp = dxp + w[3 - j:4 - j, cols] * win_d[j:j + tb, cols]
            dxp_ref[:, cols] = dxp.astype(BF16)
        o1, o2, o3 = D_SSD, D_SSD + D_XBC, D_SSD + D_XBC + DT_PAD
        du = (jnp.dot(dz_ref[...].astype(BF16), w_vmem[0:o1, :], preferred_element_type=F32)
              + jnp.dot(dxp_ref[...], w_vmem[o1:o2, :], preferred_element_type=F32)
              + jnp.dot(ddt_ref[...].astype(BF16), w_vmem[o2:o3, :], preferred_element_type=F32)
              + jnp.dot(du5_ref[...].astype(BF16), w_vmem[o3:, :], preferred_element_type=F32))
        m = mod_ref[0]
        xv = x_ref[...]
        u_ref[...] = (xv * (1.0 + m[1:2]) + m[0:1]).astype(BF16)
        gx_ref[...] = dxa_ref[...] + du * (1.0 + m[1:2])
        bacc_ref[0, 0:1, :] += jnp.sum(du, axis=0, keepdims=True)
        bacc_ref[0, 1:2, :] += jnp.sum(du * xv, axis=0, keepdims=True)

    row = lambda w: pl.BlockSpec((tb, w), lambda i: (blk(i), 0))
    halo = pl.BlockSpec((8, D_XBC), lambda i: (jnp.maximum(blk(i) * (tb // 8) - 1, 0), 0))
    return pl.pallas_call(
        body, name="conv_proj_bwd", grid=(n_blocks,),
        out_shape=(jax.ShapeDtypeStruct((t, D_MODEL), F32), jax.ShapeDtypeStruct((t, D_MODEL), BF16),
                   jax.ShapeDtypeStruct((t, D_XBC), BF16), jax.ShapeDtypeStruct((t // seq, 8, D_MODEL), F32),
                   jax.ShapeDtypeStruct((8, D_XBC), F32)),
        in_specs=[row(D_SSD), row(D_XBC), row(D_XBC), row(D_XBC), halo, row(DT_PAD), row(D_S5), row(D_MODEL),
                  row(D_MODEL), pl.BlockSpec((1, N_MOD, D_MODEL), lambda i: (i // npb, 0, 0)),
                  pl.BlockSpec((4, D_XBC), lambda i: (0, 0)), ANY],
        out_specs=(row(D_MODEL), row(D_MODEL), row(D_XBC), pl.BlockSpec((1, 8, D_MODEL), lambda i: (i // npb, 0, 0)),
                   pl.BlockSpec((8, D_XBC), lambda i: (0, 0))),
        scratch_shapes=[pltpu.VMEM((D_INP, D_MODEL), BF16), pltpu.VMEM((tb + 8, D_XBC), F32),
                        pltpu.VMEM((tb + 8, D_XBC), F32), pltpu.SemaphoreType.DMA],
        compiler_params=_params(60),
    )(dz, dxbc, dsilu, xbc_pre, xbc_pre, ddt, du5, x2, dxa, mod3, conv_w, w_in_pad)


def _pad_rows(a, mult):
    r = a.shape[0]
    pad = (-r) % mult
    return a if pad == 0 else jnp.concatenate([a, jnp.zeros((pad,) + a.shape[1:], a.dtype)], axis=0)


_SMALL = ["conv_w", "conv_b", "dt_bias", "a_log", "d_ssd", "norm_w", "s5_a_re", "s5_a_im", "s5_log_dt", "s5_b_re",
          "s5_b_im", "s5_c_re", "s5_c_im", "s5_d", "b_glu", "ln1_g", "ln1_b", "b1", "b2", "ln2_g", "ln2_b"]


def _tile_rows(size):
    return 8 * (-(-size // 1024))


def _pack_small(d):
    parts = []
    for n in _SMALL:
        flat = d[n].reshape(-1).astype(F32)
        rows = _tile_rows(flat.shape[0])
        pad = rows * 128 - flat.shape[0]
        if pad:
            flat = jnp.concatenate([flat, jnp.zeros((pad,), F32)])
        parts.append(flat.reshape(rows, 128))
    return jnp.concatenate(parts, axis=0)


def _unpack_small(p, shapes):
    out, off = {}, 0
    for n in _SMALL:
        size = math.prod(shapes[n])
        rows = _tile_rows(size)
        out[n] = p[off:off + rows].reshape(-1)[:size].reshape(shapes[n])
        off += rows
    return out


def kernel(x, c, w_ada, b_ada, w_in, conv_w, conv_b, dt_bias, a_log, d_ssd, norm_w, s5_a_re, s5_a_im, s5_log_dt, s5_b_re, s5_b_im, s5_c_re, s5_c_im, s5_d, w_glu, b_glu, w_out, ln1_g, ln1_b, w1, b1, w2, b2, ln2_g, ln2_b, loss_target, m_w_ada, m_b_ada, m_w_in, m_conv_w, m_conv_b, m_dt_bias, m_a_log, m_d_ssd, m_norm_w, m_s5_a_re, m_s5_a_im, m_s5_log_dt, m_s5_b_re, m_s5_b_im, m_s5_c_re, m_s5_c_im, m_s5_d, m_w_glu, m_b_glu, m_w_out, m_ln1_g, m_ln1_b, m_w1, m_b1, m_w2, m_b2, m_ln2_g, m_ln2_b, v_w_ada, v_b_ada, v_w_in, v_conv_w, v_conv_b, v_dt_bias, v_a_log, v_d_ssd, v_norm_w, v_s5_a_re, v_s5_a_im, v_s5_log_dt, v_s5_b_re, v_s5_b_im, v_s5_c_re, v_s5_c_im, v_s5_d, v_w_glu, v_b_glu, v_w_out, v_ln1_g, v_ln1_b, v_w1, v_b1, v_w2, v_b2, v_ln2_g, v_ln2_b):
    weights = dict(w_ada=w_ada, b_ada=b_ada, w_in=w_in, conv_w=conv_w, conv_b=conv_b, dt_bias=dt_bias, a_log=a_log,
                   d_ssd=d_ssd, norm_w=norm_w, s5_a_re=s5_a_re, s5_a_im=s5_a_im, s5_log_dt=s5_log_dt, s5_b_re=s5_b_re,
                   s5_b_im=s5_b_im, s5_c_re=s5_c_re, s5_c_im=s5_c_im, s5_d=s5_d, w_glu=w_glu, b_glu=b_glu, w_out=w_out,
                   ln1_g=ln1_g, ln1_b=ln1_b, w1=w1, b1=b1, w2=w2, b2=b2, ln2_g=ln2_g, ln2_b=ln2_b)
    mom = dict(w_ada=m_w_ada, b_ada=m_b_ada, w_in=m_w_in, conv_w=m_conv_w, conv_b=m_conv_b, dt_bias=m_dt_bias,
               a_log=m_a_log, d_ssd=m_d_ssd, norm_w=m_norm_w, s5_a_re=m_s5_a_re, s5_a_im=m_s5_a_im,
               s5_log_dt=m_s5_log_dt, s5_b_re=m_s5_b_re, s5_b_im=m_s5_b_im, s5_c_re=m_s5_c_re, s5_c_im=m_s5_c_im,
               s5_d=m_s5_d, w_glu=m_w_glu, b_glu=m_b_glu, w_out=m_w_out, ln1_g=m_ln1_g, ln1_b=m_ln1_b, w1=m_w1, b1=m_b1,
               w2=m_w2, b2=m_b2, ln2_g=m_ln2_g, ln2_b=m_ln2_b)
    var = dict(w_ada=v_w_ada, b_ada=v_b_ada, w_in=v_w_in, conv_w=v_conv_w, conv_b=v_conv_b, dt_bias=v_dt_bias,
               a_log=v_a_log, d_ssd=v_d_ssd, norm_w=v_norm_w, s5_a_re=v_s5_a_re, s5_a_im=v_s5_a_im,
               s5_log_dt=v_s5_log_dt, s5_b_re=v_s5_b_re, s5_b_im=v_s5_b_im, s5_c_re=v_s5_c_re, s5_c_im=v_s5_c_im,
               s5_d=v_s5_d, w_glu=v_w_glu, b_glu=v_b_glu, w_out=v_w_out, ln1_g=v_ln1_g, ln1_b=v_ln1_b, w1=v_w1, b1=v_b1,
               w2=v_w2, b2=v_b2, ln2_g=v_ln2_g, ln2_b=v_ln2_b)
    names = list(weights)
    shapes = {n: weights[n].shape for n in names}

    nb, seq, _ = x.shape
    t = nb * seq
    dev = _dev_index()
    x2 = x.reshape(t, D_MODEL)
    tgt2 = loss_target.reshape(t, D_MODEL)

    cw_cols = conv_w.shape[2]
    small_in = jnp.concatenate([c.reshape(-1), conv_w.reshape(-1)]).reshape(-1, 128)
    big_names = ["w_in", "w_out", "w1", "w2", "w_glu"]
    local = {n: (a[0].T if n == "w_in" else a[0]) for n, a in weights.items() if n in big_names}
    shard_bf16 = {n: local[n].astype(BF16) for n in big_names}
    first = _all_gather([small_in, shard_bf16["w_in"], shard_bf16["w_glu"]], "gather_first")
    small_all = first[0].reshape(N_DEV, -1)
    c_all = small_all[:, :nb * D_MODEL].reshape(N_DEV * nb, D_MODEL)
    conv_w_full = small_all[:, nb * D_MODEL:].reshape(N_DEV, 4, cw_cols).transpose(1, 0, 2).reshape(4, D_XBC)

    w_in_t = first[1].reshape(D_IN, D_MODEL)
    w_in_pad = jnp.concatenate(
        [w_in_t[:D_SSD + D_XBC + N_HEADS], jnp.zeros((DT_PAD - N_HEADS, D_MODEL), BF16),
         w_in_t[D_SSD + D_XBC + N_HEADS:]], axis=0)
    w_glu_f = first[2].reshape(D_S5, D_S5)
    late_names = ["w_out", "w1", "w2"]

    ada_cols = w_ada.shape[2]
    b_cols = lax.dynamic_slice_in_dim(b_ada, dev * ada_cols, ada_cols, axis=1)
    mod_cols = _mod_fwd(c_all, w_ada[0], b_cols)
    mod_all = _all_gather([mod_cols], "gather_mod")[0]
    mod_mine = lax.dynamic_slice_in_dim(mod_all, dev * nb, nb, axis=1)
    mod3 = mod_mine.transpose(1, 0, 2).reshape(nb, N_MOD, D_MODEL)
    late_in, mod3 = lax.optimization_barrier(([shard_bf16[n] for n in late_names], mod3))
    late_sems = _gather_start(late_in, "gather_late_start")
    mod3 = mod3 + late_sems[4][0, 0]

    def pad_lanes(v, n):
        return jnp.concatenate([v, jnp.zeros((v.shape[0], n - v.shape[1]), F32)], axis=1)

    par = _pad_rows(jnp.concatenate([pad_lanes(dt_bias, 128), pad_lanes(a_log, 128)], axis=0), 8)
    dsk = jnp.repeat(d_ssd[0], HEADDIM).reshape(1, D_SSD)
    ar = s5_a_re.reshape(1, S5_N)
    ai = s5_a_im.reshape(1, S5_N)
    ldt = jnp.repeat(s5_log_dt[0], S5_P).reshape(1, S5_N)
    br_t = s5_b_re[0].transpose(2, 0, 1).reshape(S5_CH, S5_N)
    bi_t = s5_b_im[0].transpose(2, 0, 1).reshape(S5_CH, S5_N)
    bb_re_t, bb_im_t, pf_re, pf_im, pr_re, pr_im = _s5_params_fwd(ar, ai, ldt, br_t, bi_t)
    gpb = S5_GROUPS // S5_BLOCKS
    mask_b = (jnp.arange(128)[:, None] // S5_CH) == (jnp.arange(512)[None, :] // S5_P)

    def dense_b(bt_):
        blocks = bt_.reshape(S5_CH, S5_BLOCKS, 512).transpose(1, 0, 2)
        return jnp.where(mask_b, jnp.tile(blocks, (1, gpb, 1)), 0.0).astype(BF16)

    def dense_c(cc):
        blocks = cc[0].transpose(0, 2, 1).reshape(S5_BLOCKS, 512, S5_CH)
        return jnp.where(mask_b.T, jnp.tile(blocks, (1, 1, gpb)), 0.0).astype(BF16)

    bb_re, bb_im = dense_b(bb_re_t), dense_b(bb_im_t)
    cc_re, cc_im = dense_c(s5_c_re), dense_c(s5_c_im)
    s5d = s5_d.reshape(1, D_S5)
    ln1 = jnp.concatenate([ln1_g, ln1_b], axis=0)
    vec1 = _pad_rows(jnp.concatenate([b2, ln2_g, ln2_b], axis=0), 8)

    z, xbc_pre, xbc, dsilu, dt_raw, u5 = _proj_conv_fwd(x2, mod3, w_in_pad, conv_w_full, conv_b, seq)
    yraw, ycat, hprev = _ssd_fwd(xbc, z, dt_raw, par, dsk, norm_w, seq)
    s_re, s_im, ypre, ycat = _s5_fwd(u5, bb_re, bb_im, cc_re, cc_im, pf_re, pf_im, s5d, w_glu_f, b_glu, ycat, seq)
    sent, landed = _gather_wait(late_sems[0], late_sems[1], late_sems[2], late_sems[3], ycat, "gather_late_wait")
    gathered = {n: lax.dynamic_update_index_in_dim(l, x, dev, 0) for n, x, l in zip(late_names, sent, landed)}
    w_out_f = gathered["w_out"].reshape(2 * D_MODEL, D_MODEL)
    w1_blocks = gathered["w1"]
    w2_f = gathered["w2"].reshape(D_FF, D_MODEL)
    mix, x1 = _out_ln1(ycat, x2, mod3, w_out_f, ln1, seq)

    dx1, u2b, hb, dhpb, dob, gacc2, db1, bacc2 = _mlp_fwd_bwd(x1, tgt2, mod3, w1_blocks, w2_f, vec1, b1, seq)
    loss = lax.psum(0.5 / D_MODEL * jnp.sum(gacc2[3]), ("x", "y", "c"))

    dmixb, dxa, dyssd, dy5, gacc1, bacc1 = _ln1_out_bwd(dx1, x2, mix, mod3, w_out_f, ln1, seq)

    g_w2 = _atb(hb, dob, "gw2")
    g_w1 = _atb(u2b, dhpb, "gw1")
    g_wout = _atb(ycat, dmixb, "gwout")
    core = lax.axis_index("c").astype(jnp.int32).reshape(1)
    chip = 2 * lax.axis_index("x") + lax.axis_index("y")

    def chip_sums_of(names, grads, tag):
        by_dest = [g if g.ndim == 2 else g.reshape((4, 2) + g.shape[1:]) for g in grads]
        from_sibling = _sibling_swap(by_dest, "rs_swap_" + tag)
        return [_add_halves(g, r, core, "rs_add_" + n) for g, r, n in zip(by_dest, from_sibling, names)]

    early_names = ["w_out", "w1", "w2"]
    early_sums = chip_sums_of(early_names, [g_wout.reshape((N_DEV,) + w_out.shape[1:]), g_w1,
                                            g_w2.reshape((N_DEV,) + w2.shape[1:])], "early")
    early = _all_to_all_start(early_sums, "rs_early_start")
    s5d_after = s5d + early[4][0, 0]

    du5, vacc, sacc, d_cc, d_bb, g_wglu = _s5_bwd(dy5, ypre, u5, s_re, s_im, bb_re, bb_im, cc_re, cc_im,
                                                  pr_re, pr_im, s5d_after, w_glu_f, b_glu, seq)
    dxbc, dz, ddt, dpar, cacc = _ssd_bwd(dyssd, yraw, z, xbc, dt_raw, hprev, par, dsk, norm_w, seq)
    grad_x2, ub, dxpb, bacc0, conv_acc = _conv_proj_bwd(dz, dxbc, dsilu, xbc_pre, ddt, du5, x2, dxa, mod3,
                                                        conv_w_full, w_in_pad, seq)

    g_win_t = jnp.concatenate([_atb(dz, ub, "gwin_z"), _atb(dxpb, ub, "gwin_xbc"),
                               _atb(ddt, ub, "gwin_dt")[:N_HEADS], _atb(du5, ub, "gwin_s5")], axis=0)

    def diag_b(dd):
        kept = jnp.where(mask_b, dd, 0.0).reshape(S5_BLOCKS, gpb, S5_CH, 512).sum(1)
        return kept.transpose(1, 0, 2).reshape(S5_CH, S5_N)

    def diag_c(dd):
        kept = jnp.where(mask_b.T, dd, 0.0).reshape(S5_BLOCKS, 512, gpb, S5_CH).sum(2)
        return kept.reshape(S5_GROUPS, S5_P, S5_CH).transpose(0, 2, 1)

    g_ar, g_ai, g_ldt, g_br_t, g_bi_t = _s5_params_bwd(ar, ai, ldt, br_t, bi_t, vacc[0:1], vacc[1:2],
                                                      diag_b(d_bb[:S5_BLOCKS]), diag_b(d_bb[S5_BLOCKS:]))

    def from_t(gt):
        return gt.reshape(S5_CH, S5_GROUPS, S5_P).transpose(1, 2, 0)

    small_g = dict(
        conv_w=conv_acc[0:4], conv_b=conv_acc[4:5], dt_bias=dpar[0:1, :N_HEADS], a_log=dpar[1:2, :N_HEADS],
        d_ssd=cacc[0].reshape(N_HEADS, HEADDIM).sum(1), norm_w=cacc[1:2],
        s5_a_re=g_ar, s5_a_im=g_ai, s5_log_dt=g_ldt[:, :S5_GROUPS], s5_b_re=from_t(g_br_t), s5_b_im=from_t(g_bi_t),
        s5_c_re=diag_c(d_cc[:S5_BLOCKS]), s5_c_im=diag_c(d_cc[S5_BLOCKS:]), s5_d=sacc[0:1], b_glu=sacc[1:2],
        ln1_g=gacc1[0:1], ln1_b=gacc1[1:2], b1=db1, b2=gacc2[2:3], ln2_g=gacc2[0:1], ln2_b=gacc2[1:2])

    dmod = jnp.concatenate([bacc0[:, 0], bacc0[:, 1], bacc1[:, 0], bacc2[:, 0], bacc2[:, 1], bacc2[:, 2]], axis=1)
    dmod_all, small_parts = _all_gather([dmod, _pack_small(small_g)], "gather_dmod_small_grads")
    dmod_all = dmod_all.reshape(N_DEV * nb, N_MOD * D_MODEL)
    dmod_cols = lax.dynamic_slice_in_dim(dmod_all, dev * ada_cols, ada_cols, axis=1)
    g_wada, g_bada = _mod_bwd(c_all, dmod_cols, dmod_all)

    late_rs = ["w_in", "w_glu"]
    late_sums = chip_sums_of(late_rs, [g_win_t.reshape(N_DEV, w_in.shape[2], D_MODEL),
                                       g_wglu.reshape((N_DEV,) + w_glu.shape[1:])], "late")
    parts = dict(zip(late_rs, _chip_all_to_all(late_sums, "rs_late_all_to_all")))
    sent, landed = _all_to_all_wait(early[0], early[1], early[2], early[3], parts["w_in"], "rs_early_wait")
    for n, l, h in zip(early_names, landed, sent):
        parts[n] = lax.dynamic_update_index_in_dim(l, lax.dynamic_index_in_dim(h, chip, 0, keepdims=False), chip, 0)

    res = {k: {} for k in "gdmv"}
    for n in big_names:
        w_m_v = [(a[n][0].T if n == "w_in" else a[n][0]) for a in (weights, mom, var)]
        outs = _adamw(parts[n], *w_m_v, "adamw_" + n)
        for k, a in zip("gdmv", outs):
            res[k][n] = (a.T if n == "w_in" else a)[None]

    ag, ad, am, av = _adamw(g_wada[None], w_ada[0], m_w_ada[0], v_w_ada[0], "adamw_w_ada")
    for k, a in (("g", ag), ("d", ad), ("m", am), ("v", av)):
        res[k]["w_ada"] = a[None]
    bg_, bd_, bm_, bv_ = _adamw(g_bada.reshape(1, -1, 128), b_ada.reshape(-1, 128), m_b_ada.reshape(-1, 128),
                                v_b_ada.reshape(-1, 128), "adamw_b_ada")
    for k, a in (("g", bg_), ("d", bd_), ("m", bm_), ("v", bv_)):
        res[k]["b_ada"] = a.reshape(shapes["b_ada"])

    small_shapes = dict(shapes)
    small_shapes["conv_w"] = (1, 4, D_XBC)
    rep = {n: (jnp.zeros((1, 4, D_XBC), F32) if n == "conv_w" else weights[n]) for n in _SMALL}
    rep_m = {n: (jnp.zeros((1, 4, D_XBC), F32) if n == "conv_w" else mom[n]) for n in _SMALL}
    rep_v = {n: (jnp.ones((1, 4, D_XBC), F32) if n == "conv_w" else var[n]) for n in _SMALL}
    sg_, sd_, sm_, sv_ = _adamw(small_parts, _pack_small(rep), _pack_small(rep_m), _pack_small(rep_v), "adamw_small")
    for k, p in (("g", sg_), ("d", sd_), ("m", sm_), ("v", sv_)):
        un = _unpack_small(p, small_shapes)
        for n in _SMALL:
            if n != "conv_w":
                res[k][n] = un[n]
    g_conv_full = _unpack_small(sg_, small_shapes)["conv_w"][0]
    g_conv_mine = lax.dynamic_slice_in_dim(g_conv_full, dev * cw_cols, cw_cols, axis=1)
    cg_, cd_, cm_, cv_ = _adamw(g_conv_mine[None], conv_w[0], m_conv_w[0], v_conv_w[0], "adamw_conv_w")
    for k, a in (("g", cg_), ("d", cd_), ("m", cm_), ("v", cv_)):
        res[k]["conv_w"] = a[None]

    grad_x = grad_x2.reshape(nb, seq, D_MODEL)
    return (loss, grad_x, *[res["g"][n] for n in names], *[res["d"][n] for n in names],
            *[res["m"][n] for n in names], *[res["v"][n] for n in names])
```

```python
import functools
import math

import jax
import jax.numpy as jnp
from jax import lax
from jax.experimental import pallas as pl
from jax.experimental.pallas import tpu as pltpu

F32, BF16 = jnp.float32, jnp.bfloat16
MESH = pl.DeviceIdType.MESH
N_DEV = 8

D_MODEL = 1024
D_SSD = 1536
N_HEADS = 24
HEADDIM = 64
N_GROUPS = 4
HPG = 6
GW = HPG * HEADDIM
N_STATE = 128
CHUNK = 128
D_XBC = 2560
D_S5 = 512
S5_GROUPS = 32
S5_CH = 16
S5_P = 64
S5_N = S5_GROUPS * S5_P
D_IN = 4632
DT_PAD = 128
D_INP = D_SSD + D_XBC + DT_PAD + D_S5
D_FF = 4096
N_MOD = 6
ALPHA = 2.0 ** 0.25
EPS = 1e-5
LR, B1, B2, AEPS, WD, STEP = 0.001, 0.9, 0.999, 1e-08, 0.01, 10

NT = (((1,), (1,)), ((), ()))
TN = (((0,), (0,)), ((), ()))
ANY = pl.BlockSpec(memory_space=pl.ANY)
HIGHEST = lax.Precision.HIGHEST


def _mm(a, b):
    return jnp.dot(a.astype(BF16), b.astype(BF16), preferred_element_type=F32)


def _mm_nt(a, b):
    return lax.dot_general(a.astype(BF16), b.astype(BF16), NT, preferred_element_type=F32)


def _mm_tn(a, b):
    return lax.dot_general(a.astype(BF16), b.astype(BF16), TN, preferred_element_type=F32)


def _row_block(r, cap):
    best = r
    for cand in range(8, min(r, cap) + 1, 8):
        if r % cand == 0:
            best = cand
    return best if best <= cap else r


def _params(vmem_mb):
    return pltpu.CompilerParams(vmem_limit_bytes=vmem_mb << 20)


def _sigmoid(x):
    return 0.5 * (jnp.tanh(0.5 * x) + 1.0)


def _softplus(x):
    return jnp.maximum(x, 0.0) + jnp.log(1.0 + jnp.exp(-jnp.abs(x)))


_GK = math.sqrt(2.0 / math.pi)


def _gelu(x):
    return 0.5 * x * (1.0 + jnp.tanh(_GK * (x + 0.044715 * x * x * x)))


def _gelu_grad(x):
    t = jnp.tanh(_GK * (x + 0.044715 * x * x * x))
    return 0.5 * (1.0 + t) + 0.5 * x * (1.0 - t * t) * _GK * (1.0 + 3.0 * 0.044715 * x * x)


def _dev_index():
    return 4 * lax.axis_index("x") + 2 * lax.axis_index("y") + lax.axis_index("c")


def _all_gather(xs, name):
    n = len(xs)

    def body(*refs):
        x_refs, out_refs = refs[:n], refs[n:2 * n]
        send_sems, recv_sems, local_sems = refs[2 * n:]
        ix, iy, ic = lax.axis_index("x"), lax.axis_index("y"), lax.axis_index("c")
        me, sibling = (ix, iy, ic), (ix, iy, 1 - ic)
        chips = [(1 - ix, iy), (ix, 1 - iy), (1 - ix, 1 - iy)]

        def slot(a, px, py, pc):
            return out_refs[a].at[4 * px + 2 * py + pc]

        def copy(a, k, block, to, src=None):
            return pltpu.make_async_remote_copy(
                src_ref=slot(a, *block) if src is None else src, dst_ref=slot(a, *block),
                send_sem=send_sems.at[7 * a + k], recv_sem=recv_sems.at[7 * a + k], device_id=to, device_id_type=MESH)

        mine = [pltpu.make_async_copy(x_refs[a], slot(a, *me), local_sems.at[a]) for a in range(n)]
        for cp in mine:
            cp.start()
        first = []
        for j, chip in enumerate(chips):
            first += [copy(a, 1 + j, me, (*chip, ic), src=x_refs[a]) for a in range(n)]
        first += [copy(a, 0, me, sibling, src=x_refs[a]) for a in range(n)]
        for cp in first:
            cp.start()
        passed = []
        for j, chip in enumerate(chips):
            for a in range(n):
                copy(a, 1 + j, (*chip, ic), me).wait_recv()
                cp = copy(a, 4 + j, (*chip, ic), sibling)
                cp.start()
                passed.append(cp)
        for a in range(n):
            copy(a, 0, sibling, me).wait_recv()
            for j, chip in enumerate(chips):
                copy(a, 4 + j, (*chip, 1 - ic), me).wait_recv()
        for cp in first + passed:
            cp.wait_send()
        for cp in mine:
            cp.wait()

    return pl.pallas_call(
        body, name=name, out_shape=tuple(jax.ShapeDtypeStruct((N_DEV,) + x.shape, x.dtype) for x in xs),
        in_specs=[ANY] * n, out_specs=tuple([ANY] * n),
        scratch_shapes=[pltpu.SemaphoreType.DMA((7 * n,)), pltpu.SemaphoreType.DMA((7 * n,)),
                        pltpu.SemaphoreType.DMA((n,))],
    )(*xs)


HBM = pl.BlockSpec(memory_space=pltpu.HBM)
SEM = pl.BlockSpec(memory_space=pltpu.SEMAPHORE)
DATAFLOW = pltpu.SideEffectType.DATAFLOW_SIDE_EFFECTING


def _peer(k):
    ix, iy, ic = lax.axis_index("x"), lax.axis_index("y"), lax.axis_index("c")
    return (1 - ix if k & 4 else ix, 1 - iy if k & 2 else iy, 1 - ic if k & 1 else ic)


def _block_of(p):
    return 4 * p[0] + 2 * p[1] + p[2]


def _gather_start(xs, name):
    n = len(xs)
    lands = [lax.empty((N_DEV,) + x.shape, x.dtype) for x in xs]

    def body(*refs):
        x_refs, land_refs = refs[:n], refs[n:2 * n]
        send_sems, recv_sems = refs[2 * n], refs[2 * n + 1]
        token = refs[-1]
        me = _block_of(_peer(0))
        for a in range(n):
            for k in range(1, N_DEV):
                pltpu.make_async_remote_copy(
                    src_ref=x_refs[a], dst_ref=land_refs[a].at[me], send_sem=send_sems.at[7 * a + k - 1],
                    recv_sem=recv_sems.at[7 * a + k - 1], device_id=_peer(k), device_id_type=MESH).start()
        token[...] = jnp.zeros_like(token)

    outs = pl.pallas_call(
        body, name=name,
        out_shape=(pltpu.SemaphoreType.DMA((7 * n,)), pltpu.SemaphoreType.DMA((7 * n,)))
        + tuple(pltpu.HBM(x.shape, x.dtype) for x in xs) + tuple(pltpu.HBM(l.shape, l.dtype) for l in lands)
        + (jax.ShapeDtypeStruct((8, 128), F32),),
        in_specs=[HBM] * (2 * n), out_specs=(SEM, SEM) + (HBM,) * (2 * n) + (pl.BlockSpec(memory_space=pltpu.VMEM),),
        input_output_aliases={i: 2 + i for i in range(2 * n)},
        compiler_params=pltpu.CompilerParams(has_side_effects=DATAFLOW),
    )(*[pltpu.with_memory_space_constraint(x, pltpu.HBM) for x in xs],
      *[pltpu.with_memory_space_constraint(l, pltpu.HBM) for l in lands])
    return outs[0], outs[1], outs[2:2 + n], outs[2 + n:2 + 2 * n], outs[-1]


def _gather_wait(send_sems, recv_sems, xs_thru, lands_thru, after, name):
    n = len(xs_thru)

    def body(*refs):
        x_refs, land_refs = refs[:n], refs[n:2 * n]
        send_sems, recv_sems = refs[2 * n], refs[2 * n + 1]
        for a in range(n):
            for k in range(1, N_DEV):
                cp = pltpu.make_async_remote_copy(
                    src_ref=x_refs[a], dst_ref=land_refs[a].at[_block_of(_peer(k))], send_sem=send_sems.at[7 * a + k - 1],
                    recv_sem=recv_sems.at[7 * a + k - 1], device_id=_peer(k), device_id_type=MESH)
                cp.wait_send()
                cp.wait_recv()

    outs = pl.pallas_call(
        body, name=name,
        out_shape=tuple(pltpu.HBM(x.shape, x.dtype) for x in xs_thru)
        + tuple(pltpu.HBM(l.shape, l.dtype) for l in lands_thru),
        in_specs=[HBM] * (2 * n) + [SEM, SEM, ANY], out_specs=(HBM,) * (2 * n),
        input_output_aliases={i: i for i in range(2 * n)},
        compiler_params=pltpu.CompilerParams(has_side_effects=DATAFLOW),
    )(*xs_thru, *lands_thru, send_sems, recv_sems, after)
    return outs[:n], outs[n:]


def _chip_peer(k):
    ix, iy = lax.axis_index("x"), lax.axis_index("y")
    return (1 - ix if k & 2 else ix, 1 - iy if k & 1 else iy)


def _all_to_all_start(hs, name):
    n = len(hs)
    lands = [lax.empty(h.shape, h.dtype) for h in hs]

    def body(*refs):
        h_refs, land_refs = refs[:n], refs[n:2 * n]
        send_sems, recv_sems = refs[2 * n], refs[2 * n + 1]
        token = refs[-1]
        ic = lax.axis_index("c")
        mx, my = _chip_peer(0)
        for a in range(n):
            for k in range(1, 4):
                px, py = _chip_peer(k)
                pltpu.make_async_remote_copy(
                    src_ref=h_refs[a].at[2 * px + py], dst_ref=land_refs[a].at[2 * mx + my],
                    send_sem=send_sems.at[3 * a + k - 1], recv_sem=recv_sems.at[3 * a + k - 1],
                    device_id=(px, py, ic), device_id_type=MESH).start()
        token[...] = jnp.zeros_like(token)

    outs = pl.pallas_call(
        body, name=name,
        out_shape=(pltpu.SemaphoreType.DMA((3 * n,)), pltpu.SemaphoreType.DMA((3 * n,)))
        + tuple(pltpu.HBM(h.shape, h.dtype) for h in hs) + tuple(pltpu.HBM(l.shape, l.dtype) for l in lands)
        + (jax.ShapeDtypeStruct((8, 128), F32),),
        in_specs=[HBM] * (2 * n), out_specs=(SEM, SEM) + (HBM,) * (2 * n) + (pl.BlockSpec(memory_space=pltpu.VMEM),),
        input_output_aliases={i: 2 + i for i in range(2 * n)},
        compiler_params=pltpu.CompilerParams(has_side_effects=DATAFLOW),
    )(*[pltpu.with_memory_space_constraint(h, pltpu.HBM) for h in hs],
      *[pltpu.with_memory_space_constraint(l, pltpu.HBM) for l in lands])
    return outs[0], outs[1], outs[2:2 + n], outs[2 + n:2 + 2 * n], outs[-1]


def _all_to_all_wait(send_sems, recv_sems, hs_thru, lands_thru, after, name):
    n = len(hs_thru)

    def body(*refs):
        h_refs, land_refs = refs[:n], refs[n:2 * n]
        send_sems, recv_sems = refs[2 * n], refs[2 * n + 1]
        ic = lax.axis_index("c")
        for a in range(n):
            for k in range(1, 4):
                px, py = _chip_peer(k)
                cp = pltpu.make_async_remote_copy(
                    src_ref=h_refs[a].at[2 * px + py], dst_ref=land_refs[a].at[2 * px + py],
                    send_sem=send_sems.at[3 * a + k - 1], recv_sem=recv_sems.at[3 * a + k - 1],
                    device_id=(px, py, ic), device_id_type=MESH)
                cp.wait_send()
                cp.wait_recv()

    outs = pl.pallas_call(
        body, name=name,
        out_shape=tuple(pltpu.HBM(h.shape, h.dtype) for h in hs_thru)
        + tuple(pltpu.HBM(l.shape, l.dtype) for l in lands_thru),
        in_specs=[HBM] * (2 * n) + [SEM, SEM, ANY], out_specs=(HBM,) * (2 * n),
        input_output_aliases={i: i for i in range(2 * n)},
        compiler_params=pltpu.CompilerParams(has_side_effects=DATAFLOW),
    )(*hs_thru, *lands_thru, send_sems, recv_sems, after)
    return outs[:n], outs[n:]


def _sibling_block(g_ref, q):
    ic = lax.axis_index("c")
    if len(g_ref.shape) == 4:
        return g_ref.at[q, 1 - ic]
    cw = g_ref.shape[1] // N_DEV
    return g_ref.at[:, pl.ds(pl.multiple_of((2 * q + 1 - ic) * cw, 128), cw)]


def _sibling_swap_start(gs, name):
    n = len(gs)
    lands = [lax.empty((4,) + (g.shape[2:] if g.ndim == 4 else (g.shape[0], g.shape[1] // N_DEV)), g.dtype) for g in gs]

    def body(*refs):
        g_refs, land_refs = refs[:n], refs[n:2 * n]
        send_sems, recv_sems = refs[2 * n], refs[2 * n + 1]
        token = refs[-1]
        for a in range(n):
            for q in range(4):
                pltpu.make_async_remote_copy(
                    src_ref=_sibling_block(g_refs[a], q), dst_ref=land_refs[a].at[q],
                    send_sem=send_sems.at[4 * a + q], recv_sem=recv_sems.at[4 * a + q],
                    device_id=_peer(1), device_id_type=MESH).start()
        token[...] = jnp.zeros_like(token)

    outs = pl.pallas_call(
        body, name=name,
        out_shape=(pltpu.SemaphoreType.DMA((4 * n,)), pltpu.SemaphoreType.DMA((4 * n,)))
        + tuple(pltpu.HBM(g.shape, g.dtype) for g in gs) + tuple(pltpu.HBM(l.shape, l.dtype) for l in lands)
        + (jax.ShapeDtypeStruct((8, 128), F32),),
        in_specs=[HBM] * (2 * n), out_specs=(SEM, SEM) + (HBM,) * (2 * n) + (pl.BlockSpec(memory_space=pltpu.VMEM),),
        input_output_aliases={i: 2 + i for i in range(2 * n)},
        compiler_params=pltpu.CompilerParams(has_side_effects=DATAFLOW),
    )(*[pltpu.with_memory_space_constraint(g, pltpu.HBM) for g in gs],
      *[pltpu.with_memory_space_constraint(l, pltpu.HBM) for l in lands])
    return outs[0], outs[1], outs[2:2 + n], outs[2 + n:2 + 2 * n], outs[-1]


def _sibling_swap_wait(send_sems, recv_sems, gs_thru, lands_thru, after, name):
    n = len(gs_thru)

    def body(*refs):
        g_refs, land_refs = refs[:n], refs[n:2 * n]
        send_sems, recv_sems = refs[2 * n], refs[2 * n + 1]
        for a in range(n):
            for q in range(4):
                cp = pltpu.make_async_remote_copy(
                    src_ref=_sibling_block(g_refs[a], q), dst_ref=land_refs[a].at[q],
                    send_sem=send_sems.at[4 * a + q], recv_sem=recv_sems.at[4 * a + q],
                    device_id=_peer(1), device_id_type=MESH)
                cp.wait_send()
                cp.wait_recv()

    outs = pl.pallas_call(
        body, name=name,
        out_shape=tuple(pltpu.HBM(g.shape, g.dtype) for g in gs_thru)
        + tuple(pltpu.HBM(l.shape, l.dtype) for l in lands_thru),
        in_specs=[HBM] * (2 * n) + [SEM, SEM, ANY], out_specs=(HBM,) * (2 * n),
        input_output_aliases={i: i for i in range(2 * n)},
        compiler_params=pltpu.CompilerParams(has_side_effects=DATAFLOW),
    )(*gs_thru, *lands_thru, send_sems, recv_sems, after)
    return outs[:n], outs[n:]


def _sibling_swap(gs, name):
    n = len(gs)

    def body(*refs):
        g_refs, recv_refs = refs[:n], refs[n:2 * n]
        send_sems, recv_sems = refs[2 * n:]
        ix, iy, ic = lax.axis_index("x"), lax.axis_index("y"), lax.axis_index("c")
        cps = []
        for a in range(n):
            for q in range(4):
                cps.append(pltpu.make_async_remote_copy(
                    src_ref=_sibling_block(g_refs[a], q), dst_ref=recv_refs[a].at[q],
                    send_sem=send_sems.at[4 * a + q], recv_sem=recv_sems.at[4 * a + q],
                    device_id=(ix, iy, 1 - ic), device_id_type=MESH))
        for cp in cps:
            cp.start()
        for cp in cps:
            cp.wait()

    return pl.pallas_call(
        body, name=name,
        out_shape=tuple(jax.ShapeDtypeStruct(
            (4,) + (g.shape[2:] if g.ndim == 4 else (g.shape[0], g.shape[1] // N_DEV)), g.dtype) for g in gs),
        in_specs=[ANY] * n, out_specs=tuple([ANY] * n),
        scratch_shapes=[pltpu.SemaphoreType.DMA((4 * n,)), pltpu.SemaphoreType.DMA((4 * n,))],
    )(*gs)


def _add_halves(g, recv, core, name):
    _, r, c = recv.shape
    br = _row_block(r, 512)
    stacked = g.ndim == 4

    def body(core_ref, g_ref, r_ref, o_ref):
        o_ref[0] = ((g_ref[0, 0] if stacked else g_ref[...]) + r_ref[0]).astype(BF16)

    spec = pl.BlockSpec((1, br, c), lambda i, j, core_ref: (i, j, 0))
    if stacked:
        g_spec = pl.BlockSpec((1, 1, br, c), lambda i, j, core_ref: (i, core_ref[0], j, 0))
    else:
        g_spec = pl.BlockSpec((br, c), lambda i, j, core_ref: (j, 2 * i + core_ref[0]))
    return pl.pallas_call(
        body, name=name, out_shape=jax.ShapeDtypeStruct(recv.shape, BF16),
        grid_spec=pltpu.PrefetchScalarGridSpec(
            num_scalar_prefetch=1, grid=(4, r // br), in_specs=[g_spec, spec], out_specs=spec),
        compiler_params=_params(32),
    )(core, g, recv)


def _adamw(parts, w, m, v, name):
    n_parts, r, c = parts.shape
    if r % 8 == 0:
        br, bc = _row_block(r, 512 if c <= 1024 else 256), c
    else:
        br, bc = r, (256 if c % 256 == 0 else c)

    def body(p_ref, w_ref, m_ref, v_ref, g_out, d_out, m_out, v_out):
        g = p_ref[0].astype(F32)
        for p in range(1, n_parts):
            g = g + p_ref[p].astype(F32)
        m2 = B1 * m_ref[...] + (1.0 - B1) * g
        v2 = B2 * v_ref[...] + (1.0 - B2) * (g * g)
        m_hat = m2 / (1.0 - B1 ** STEP)
        v_hat = v2 / (1.0 - B2 ** STEP)
        g_out[...] = g
        d_out[...] = -LR * (m_hat / (jnp.sqrt(v_hat) + AEPS) + WD * w_ref[...])
        m_out[...] = m2
        v_out[...] = v2

    spec = pl.BlockSpec((br, bc), lambda i, j: (i, j))
    out = jax.ShapeDtypeStruct((r, c), F32)
    return pl.pallas_call(
        body, name=name, out_shape=(out, out, out, out), grid=(r // br, c // bc),
        in_specs=[pl.BlockSpec((n_parts, br, bc), lambda i, j: (0, i, j)), spec, spec, spec],
        out_specs=(spec, spec, spec, spec), compiler_params=_params(40),
    )(parts, w, m, v)


def _atb(a, b, name):
    t, k1 = a.shape
    k2 = b.shape[1]
    bt = math.gcd(t, 2048)

    def pick(k):
        for cand in (1024, 768, 512, 384, 256, 128):
            if k % cand == 0:
                return cand
        return k

    b1, b2 = pick(k1), pick(k2)

    def body(a_ref, b_ref, o_ref):
        @pl.when(pl.program_id(2) == 0)
        def _():
            o_ref[...] = jnp.zeros_like(o_ref)
        o_ref[...] += _mm_tn(a_ref[...], b_ref[...])

    return pl.pallas_call(
        body, name=name, out_shape=jax.ShapeDtypeStruct((k1, k2), F32), grid=(k1 // b1, k2 // b2, t // bt),
        in_specs=[pl.BlockSpec((bt, b1), lambda i, j, k: (k, i)), pl.BlockSpec((bt, b2), lambda i, j, k: (k, j))],
        out_specs=pl.BlockSpec((b1, b2), lambda i, j, k: (i, j)), compiler_params=_params(48),
    )(a, b)


def _mod_fwd(c_all, w_ada, b_cols):
    def body(c_ref, w_ref, b_ref, o_ref):
        cc = c_ref[...]
        cond = cc * _sigmoid(cc)
        o_ref[...] = _mm(cond, w_ref[...]) + b_ref[...]

    return pl.pallas_call(body, name="mod_fwd", out_shape=jax.ShapeDtypeStruct((c_all.shape[0], w_ada.shape[1]), F32),
                          compiler_params=_params(32))(c_all, w_ada, b_cols)


def _mod_bwd(c_all, dmod_cols, dmod_all):
    def body(c_ref, dc_ref, da_ref, gw_ref, gb_ref):
        cc = c_ref[...]
        cond = cc * _sigmoid(cc)
        gw_ref[...] = _mm_tn(cond, dc_ref[...])
        gb_ref[...] = jnp.sum(da_ref[...], axis=0, keepdims=True)

    return pl.pallas_call(
        body, name="mod_bwd",
        out_shape=(jax.ShapeDtypeStruct((D_MODEL, dmod_cols.shape[1]), F32), jax.ShapeDtypeStruct((1, dmod_all.shape[1]), F32)),
        compiler_params=_params(32))(c_all, dmod_cols, dmod_all)


def _load_once(hbm_ref, vmem_ref, sem):
    @pl.when(pl.program_id(0) == 0)
    def _():
        cp = pltpu.make_async_copy(hbm_ref, vmem_ref, sem)
        cp.start()
        cp.wait()


def _conv_taps(win_ref, w, tb, cols):
    shifted = [win_ref[8 - j:8 - j + tb, cols] for j in range(4)]
    acc = w[3:4] * shifted[0]
    for j in (1, 2, 3):
        acc = acc + w[3 - j:4 - j] * shifted[j]
    return acc, shifted


def _proj_conv_fwd(x2, mod3, w_in_pad, conv_w, conv_b, seq):
    t = x2.shape[0]
    tb = 256
    npb = seq // tb
    cw = 512

    def body(x_ref, mod_ref, w_hbm, cw_ref, cb_ref, z_ref, pre_ref, xbc_ref, dsilu_ref, dt_ref, u5_ref, w_vmem, win, sem):
        _load_once(w_hbm, w_vmem, sem)
        first = (pl.program_id(0) % npb) == 0

        @pl.when(first)
        def _():
            win[0:8, :] = jnp.zeros((8, D_XBC), F32)

        @pl.when(jnp.logical_not(first))
        def _():
            win[0:8, :] = win[tb:tb + 8, :]

        m = mod_ref[0]
        u = (x_ref[...] * (1.0 + m[1:2]) + m[0:1]).astype(BF16)
        z_ref[...] = lax.dot_general(u, w_vmem[0:D_SSD, :], NT, preferred_element_type=F32)
        dt_ref[...] = lax.dot_general(u, w_vmem[D_SSD + D_XBC:D_SSD + D_XBC + DT_PAD, :], NT,
                                      preferred_element_type=F32)
        u5_ref[...] = lax.dot_general(u, w_vmem[D_SSD + D_XBC + DT_PAD:, :], NT, preferred_element_type=F32)
        for k in range(D_XBC // cw):
            cols = slice(k * cw, (k + 1) * cw)
            pre_k = lax.dot_general(u, w_vmem[D_SSD + k * cw:D_SSD + (k + 1) * cw, :], NT,
                                    preferred_element_type=F32)
            win[8:8 + tb, cols] = pre_k
            pre_ref[:, cols] = pre_k
            conv, _ = _conv_taps(win, cw_ref[:, cols], tb, cols)
            conv = conv + cb_ref[:, cols]
            sg = _sigmoid(conv)
            xbc_ref[:, cols] = conv * sg
            dsilu_ref[:, cols] = sg * (1.0 + conv * (1.0 - sg))

    row = lambda w: pl.BlockSpec((tb, w), lambda i: (i, 0))
    return pl.pallas_call(
        body, name="proj_conv_fwd", grid=(t // tb,),
        out_shape=(jax.ShapeDtypeStruct((t, D_SSD), F32), jax.ShapeDtypeStruct((t, D_XBC), F32),
                   jax.ShapeDtypeStruct((t, D_XBC), F32), jax.ShapeDtypeStruct((t, D_XBC), F32),
                   jax.ShapeDtypeStruct((t, DT_PAD), F32), jax.ShapeDtypeStruct((t, D_S5), F32)),
        in_specs=[row(D_MODEL), pl.BlockSpec((1, N_MOD, D_MODEL), lambda i: (i // npb, 0, 0)), ANY,
                  pl.BlockSpec((4, D_XBC), lambda i: (0, 0)), pl.BlockSpec((1, D_XBC), lambda i: (0, 0))],
        out_specs=(row(D_SSD), row(D_XBC), row(D_XBC), row(D_XBC), row(DT_PAD), row(D_S5)),
        scratch_shapes=[pltpu.VMEM((D_INP, D_MODEL), BF16), pltpu.VMEM((tb + 8, D_XBC), F32), pltpu.SemaphoreType.DMA],
        compiler_params=_params(56),
    )(x2, mod3, w_in_pad, conv_w, conv_b)


N_PAIRS = N_HEADS // 2


def _split3(x):
    hi = x.astype(BF16)
    r = x - hi.astype(F32)
    mid = r.astype(BF16)
    lo = (r - mid.astype(F32)).astype(BF16)
    return hi, mid, lo


def _dot3(x, e, dims=(((1,), (0,)), ((), ()))):
    return sum(lax.dot_general(p, e, dims, preferred_element_type=F32) for p in _split3(x))


def _dot3_left(e, x, dims=(((1,), (0,)), ((), ()))):
    return sum(lax.dot_general(e, p, dims, preferred_element_type=F32) for p in _split3(x))


def _head_fold():
    return (jnp.arange(D_SSD)[:, None] // HEADDIM == jnp.arange(128)[None, :]).astype(BF16)


def _ssd_prep(dt_raw, par):
    dtb = par[0:1]
    a = -jnp.exp(par[1:2])
    dt = _softplus(dt_raw + dtb)
    adt = dt * a
    row = lax.broadcasted_iota(jnp.int32, (CHUNK, CHUNK), 0)
    col = lax.broadcasted_iota(jnp.int32, (CHUNK, CHUNK), 1)
    causal = row >= col
    tri = causal.astype(BF16)
    cs = _dot3_left(tri, adt)
    left = col < HEADDIM

    def lanes(v, h):
        return jnp.broadcast_to(v[:, h:h + 1], (CHUNK, 128))

    dt_c, cs_c, pair_cols = [], [], []
    for p in range(N_PAIRS):
        c0, c1 = lanes(cs, 2 * p), lanes(cs, 2 * p + 1)
        pair_cols.append(jnp.concatenate([c0, c1], axis=1))
        cs_c.append(jnp.where(left, c0, c1))
        dt_c.append(jnp.where(left, lanes(dt, 2 * p), lanes(dt, 2 * p + 1)))
    cs_c = jnp.concatenate(cs_c, axis=1)
    dt_c = jnp.concatenate(dt_c, axis=1)
    return dt, a, cs, cs.T, causal, tri, dt_c, jnp.exp(cs_c), jnp.exp(cs_c[CHUNK - 1:CHUNK, :] - cs_c), pair_cols


def _pair_decay(cols, cst, pair, causal2):
    rows = jnp.concatenate([jnp.broadcast_to(cst[2 * pair:2 * pair + 1, :], (CHUNK, CHUNK)),
                            jnp.broadcast_to(cst[2 * pair + 1:2 * pair + 2, :], (CHUNK, CHUNK))], axis=1)
    return jnp.exp(jnp.where(causal2, cols - rows, -jnp.inf))


def _stack_heads(xp, left):
    return jnp.concatenate([jnp.where(left, xp, 0.0), jnp.where(left, 0.0, xp)], axis=0).astype(BF16)


def _ssd_fwd(xbc, z, dt_raw, par, dsk, normw, seq):
    t = xbc.shape[0]
    nc = seq // CHUNK
    n_chunks = t // CHUNK

    def body(xbc_ref, z_ref, dt_ref, par_ref, dsk_ref, nw_ref, yraw_ref, ycat_ref, hprev_ref, h_ref):
        @pl.when(pl.program_id(0) % nc == 0)
        def _():
            h_ref[...] = jnp.zeros_like(h_ref)
        hprev_ref[0] = h_ref[...]
        _, _, cs, cst, causal, _, dt_c, ecs_c, w_c, pair_cols = _ssd_prep(dt_ref[...], par_ref[...])
        cs_last = cs[CHUNK - 1:CHUNK, :]
        causal2 = jnp.concatenate([causal, causal], axis=1)
        left = lax.broadcasted_iota(jnp.int32, (CHUNK, 128), 1) < HEADDIM
        x = xbc_ref[:, 0:D_SSD]
        xdt = x * dt_c
        amat = (w_c * xdt).astype(BF16)
        zz = z_ref[...]
        silu_z = zz * _sigmoid(zz)
        for g in range(N_GROUPS):
            gs = slice(g * GW, (g + 1) * GW)
            bg = xbc_ref[:, D_SSD + g * N_STATE:D_SSD + (g + 1) * N_STATE].astype(BF16)
            cg = xbc_ref[:, D_SSD + (N_GROUPS + g) * N_STATE:D_SSD + (N_GROUPS + g + 1) * N_STATE].astype(BF16)
            scores = lax.dot_general(cg, bg, NT, preferred_element_type=F32)
            scores2 = jnp.concatenate([scores, scores], axis=1)
            hg = h_ref[gs, :]
            p_all = lax.dot_general(cg, hg.astype(BF16), NT, preferred_element_type=F32)
            ys = []
            for q in range(GW // 128):
                pair = g * (GW // 128) + q
                decay = _pair_decay(pair_cols[pair], cst, pair, causal2)
                mcat = (scores2 * decay).astype(BF16)
                ys.append(jnp.dot(mcat, _stack_heads(xdt[:, pair * 128:(pair + 1) * 128], left),
                                  preferred_element_type=F32))
            yg = jnp.concatenate(ys, axis=1) + ecs_c[:, gs] * p_all + x[:, gs] * dsk_ref[:, gs]
            s_new = lax.dot_general(amat[:, gs], bg, TN, preferred_element_type=F32)
            for j in range(HPG):
                hh = g * HPG + j
                js = slice(j * HEADDIM, (j + 1) * HEADDIM)
                h_ref[g * GW + j * HEADDIM:g * GW + (j + 1) * HEADDIM, :] = (
                    hg[js, :] * jnp.exp(cs_last[:, hh:hh + 1]) + s_new[js, :])
            yraw_ref[:, gs] = yg
            v = yg * silu_z[:, gs]
            r = lax.rsqrt(jnp.mean(v * v, axis=-1, keepdims=True) + EPS)
            ycat_ref[:, gs] = (v * r * nw_ref[:, gs]).astype(BF16)

    row = lambda w: pl.BlockSpec((CHUNK, w), lambda i: (i, 0))
    full = lambda s: pl.BlockSpec(s, lambda i: (0,) * len(s))
    return pl.pallas_call(
        body, name="ssd_fwd", grid=(n_chunks,),
        out_shape=(jax.ShapeDtypeStruct((t, D_SSD), F32), jax.ShapeDtypeStruct((t, D_SSD + D_S5), BF16),
                   jax.ShapeDtypeStruct((n_chunks, D_SSD, N_STATE), F32)),
        in_specs=[row(D_XBC), row(D_SSD), row(DT_PAD), full((8, 128)), full((1, D_SSD)), full((1, D_SSD))],
        out_specs=(row(D_SSD), row(D_SSD), pl.BlockSpec((1, D_SSD, N_STATE), lambda i: (i, 0, 0))),
        scratch_shapes=[pltpu.VMEM((D_SSD, N_STATE), F32)],
        compiler_params=_params(40),
    )(xbc, z, dt_raw, par, dsk, normw)


S5_CW = 512
S5_BLOCKS = 4


def _tile_scan(in_re, in_im, out_re, out_im, carry_re, carry_im, pw_re, pw_im, n_tiles, reverse):
    steps = (1, 2, 4)
    for cc in range(S5_N // S5_CW):
        cols = slice(cc * S5_CW, (cc + 1) * S5_CW)
        a_re, a_im = pw_re[:, cols], pw_im[:, cols]
        rid = lax.broadcasted_iota(jnp.int32, (8, S5_CW), 0)
        pows = []
        for d in steps:
            k = 8 - d if reverse else d - 1
            keep = (rid < 8 - d) if reverse else (rid >= d)
            pows.append((jnp.where(keep, pw_re[k:k + 1, cols], 0.0), jnp.where(keep, pw_im[k:k + 1, cols], 0.0)))

        def tile(i, carry, cols=cols, pows=pows, a_re=a_re, a_im=a_im):
            r = (n_tiles - 1 - i) if reverse else i
            rows = pl.ds(pl.multiple_of(r * 8, 8), 8)
            xr, xi = in_re[rows, cols], in_im[rows, cols]
            for (pr, pi), d in zip(pows, steps):
                shift = 8 - d if reverse else d
                sr, si = pltpu.roll(xr, shift, axis=0), pltpu.roll(xi, shift, axis=0)
                xr, xi = xr + pr * sr - pi * si, xi + pr * si + pi * sr
            cr, ci = carry
            xr, xi = xr + a_re * cr - a_im * ci, xi + a_re * ci + a_im * cr
            out_re[rows, cols] = xr
            out_im[rows, cols] = xi
            edge = slice(0, 1) if reverse else slice(7, 8)
            return (jnp.broadcast_to(xr[edge], (8, S5_CW)), jnp.broadcast_to(xi[edge], (8, S5_CW)))

        c0 = (jnp.broadcast_to(carry_re[0:1, cols], (8, S5_CW)), jnp.broadcast_to(carry_im[0:1, cols], (8, S5_CW)))
        cr, ci = lax.fori_loop(0, n_tiles, tile, c0, unroll=True)
        carry_re[:, cols] = cr
        carry_im[:, cols] = ci


def _s5_params_math(ar, ai, ldt, br, bi):
    dt = jnp.exp(ldt)
    mag = jnp.exp(ar * dt)
    ang = ai * dt
    ab_re = mag * jnp.cos(ang)
    ab_im = mag * jnp.sin(ang)
    den = ar * ar + ai * ai
    n_re = ab_re - 1.0
    coef_re = (n_re * ar + ab_im * ai) / den
    coef_im = (ab_im * ar - n_re * ai) / den
    bb_re = coef_re * br - coef_im * bi
    bb_im = coef_re * bi + coef_im * br
    return ab_re, ab_im, bb_re, bb_im


def _s5_params_fwd(ar, ai, ldt, br, bi):
    def body(ar_ref, ai_ref, ldt_ref, br_ref, bi_ref, bbr_ref, bbi_ref, pfr_ref, pfi_ref, prr_ref, pri_ref):
        ab_re, ab_im, bb_re, bb_im = _s5_params_math(ar_ref[...], ai_ref[...], ldt_ref[...], br_ref[...], bi_ref[...])
        bbr_ref[...] = bb_re
        bbi_ref[...] = bb_im
        pr, pi = ab_re, ab_im
        for k in range(8):
            pfr_ref[k:k + 1, :] = pr
            pfi_ref[k:k + 1, :] = pi
            prr_ref[7 - k:8 - k, :] = pr
            pri_ref[7 - k:8 - k, :] = -pi
            pr, pi = pr * ab_re - pi * ab_im, pr * ab_im + pi * ab_re

    b16 = jax.ShapeDtypeStruct((S5_CH, S5_N), F32)
    p8 = jax.ShapeDtypeStruct((8, S5_N), F32)
    return pl.pallas_call(body, name="s5_params_fwd", out_shape=(b16, b16, p8, p8, p8, p8),
                          compiler_params=_params(32))(ar, ai, ldt, br, bi)


def _s5_params_bwd(ar, ai, ldt, br, bi, d_ab_re, d_ab_im, d_bb_re, d_bb_im):
    def body(ar_ref, ai_ref, ldt_ref, br_ref, bi_ref, dar_ref, dai_ref, dbr_ref, dbi_ref,
             gar_ref, gai_ref, gldt_ref, gbr_ref, gbi_ref):
        _, vjp = jax.vjp(_s5_params_math, ar_ref[...], ai_ref[...], ldt_ref[...], br_ref[...], bi_ref[...])
        g_ar, g_ai, g_ldt, g_br, g_bi = vjp((dar_ref[...], dai_ref[...], dbr_ref[...], dbi_ref[...]))
        gar_ref[...] = g_ar
        gai_ref[...] = g_ai
        gbr_ref[...] = g_br
        gbi_ref[...] = g_bi
        lane = lax.broadcasted_iota(jnp.int32, (S5_N, 128), 0) // S5_P
        grp = lax.broadcasted_iota(jnp.int32, (S5_N, 128), 1)
        fold = (lane == grp).astype(F32)
        gldt_ref[...] = jnp.dot(g_ldt, fold, preferred_element_type=F32, precision=HIGHEST)

    v1 = jax.ShapeDtypeStruct((1, S5_N), F32)
    b16 = jax.ShapeDtypeStruct((S5_CH, S5_N), F32)
    return pl.pallas_call(body, name="s5_params_bwd",
                          out_shape=(v1, v1, jax.ShapeDtypeStruct((1, 128), F32), b16, b16),
                          compiler_params=_params(32))(ar, ai, ldt, br, bi, d_ab_re, d_ab_im, d_bb_re, d_bb_im)


def _s5_fwd(u5, bb_re, bb_im, cc_re, cc_im, pf_re, pf_im, s5d, w_glu, b_glu, ycat, seq):
    t = u5.shape[0]
    tb = 256
    npb = seq // tb

    def body(u_ref, bbr_ref, bbi_ref, ccr_ref, cci_ref, pfr_ref, pfi_ref, d_ref, wg_ref, bg_ref, ycat_hbm,
             sre_ref, sim_ref, ypre_ref, y5_ref, bur, bui, car, cai):
        del ycat_hbm

        @pl.when(pl.program_id(0) % npb == 0)
        def _():
            car[...] = jnp.zeros_like(car)
            cai[...] = jnp.zeros_like(cai)
        u = u_ref[...]
        ub = u.astype(BF16)
        for j in range(S5_BLOCKS):
            ch, st = slice(j * 128, (j + 1) * 128), slice(j * 512, (j + 1) * 512)
            bur[:, st] = jnp.dot(ub[:, ch], bbr_ref[j], preferred_element_type=F32)
            bui[:, st] = jnp.dot(ub[:, ch], bbi_ref[j], preferred_element_type=F32)
        _tile_scan(bur, bui, sre_ref, sim_ref, car, cai, pfr_ref, pfi_ref, tb // 8, reverse=False)
        cs_y = []
        for j in range(S5_BLOCKS):
            st = slice(j * 512, (j + 1) * 512)
            cs_y.append(_mm(sre_ref[:, st], ccr_ref[j]) - _mm(sim_ref[:, st], cci_ref[j]))
        ypre = jnp.concatenate(cs_y, axis=1) + u * d_ref[...]
        ypre_ref[...] = ypre
        yg = _gelu(ypre)
        y5_ref[...] = (yg * _sigmoid(_mm(yg, wg_ref[...]) + bg_ref[...])).astype(BF16)

    row = lambda w: pl.BlockSpec((tb, w), lambda i: (i, 0))
    full = lambda a: pl.BlockSpec(a.shape, lambda i: (0,) * a.ndim)
    return pl.pallas_call(
        body, name="s5_fwd", grid=(t // tb,),
        out_shape=(jax.ShapeDtypeStruct((t, S5_N), F32), jax.ShapeDtypeStruct((t, S5_N), F32),
                   jax.ShapeDtypeStruct((t, D_S5), F32), jax.ShapeDtypeStruct(ycat.shape, BF16)),
        in_specs=[row(D_S5), full(bb_re), full(bb_im), full(cc_re), full(cc_im), full(pf_re), full(pf_im),
                  full(s5d), full(w_glu), full(b_glu), ANY],
        out_specs=(row(S5_N), row(S5_N), row(D_S5), pl.BlockSpec((tb, D_S5), lambda i: (i, D_SSD // D_S5))),
        input_output_aliases={10: 3},
        scratch_shapes=[pltpu.VMEM((tb, S5_N), F32), pltpu.VMEM((tb, S5_N), F32),
                        pltpu.VMEM((8, S5_N), F32), pltpu.VMEM((8, S5_N), F32)],
        compiler_params=_params(48),
    )(u5, bb_re, bb_im, cc_re, cc_im, pf_re, pf_im, s5d, w_glu, b_glu, ycat)


def _layer_norm(r, g, b):
    mu = jnp.mean(r, axis=-1, keepdims=True)
    xc = r - mu
    rstd = lax.rsqrt(jnp.mean(xc * xc, axis=-1, keepdims=True) + EPS)
    xhat = xc * rstd
    return xhat * g + b, xhat, rstd


def _layer_norm_bwd(dy, xhat, rstd, g):
    dxhat = dy * g
    return rstd * (dxhat - jnp.mean(dxhat, axis=-1, keepdims=True)
                   - xhat * jnp.mean(dxhat * xhat, axis=-1, keepdims=True))


def _out_ln1(ycat, x2, mod3, w_out, ln1, seq):
    t = x2.shape[0]
    tb = 512
    npb = seq // tb

    def body(y_ref, x_ref, mod_ref, w_ref, ln_ref, mix_ref, x1_ref):
        m = mod_ref[0]
        mix = jnp.dot(y_ref[...], w_ref[...], preferred_element_type=F32)
        mix_ref[...] = mix
        r1 = ALPHA * x_ref[...] + (1.0 + m[2:3]) * mix
        x1_ref[...] = _layer_norm(r1, ln_ref[0:1], ln_ref[1:2])[0]

    row = lambda w: pl.BlockSpec((tb, w), lambda i: (i, 0))
    return pl.pallas_call(
        body, name="out_ln1", grid=(t // tb,),
        out_shape=(jax.ShapeDtypeStruct((t, D_MODEL), F32), jax.ShapeDtypeStruct((t, D_MODEL), F32)),
        in_specs=[row(D_SSD + D_S5), row(D_MODEL), pl.BlockSpec((1, N_MOD, D_MODEL), lambda i: (i // npb, 0, 0)),
                  pl.BlockSpec(w_out.shape, lambda i: (0, 0)), pl.BlockSpec(ln1.shape, lambda i: (0, 0))],
        out_specs=(row(D_MODEL), row(D_MODEL)), compiler_params=_params(48),
    )(ycat, x2, mod3, w_out, ln1)


def _mlp_fwd_bwd(x1, tgt, mod3, w1, w2, vec1, b1, seq):
    t = x1.shape[0]
    tb = 256
    npb = seq // tb
    n_fb, _, fb = w1.shape

    def body(x1_ref, tgt_ref, mod_ref, w1_hbm, w2_hbm, v_ref, b1_ref,
             dx1_ref, u2_ref, h_ref, dhp_ref, do_ref, gacc_ref, db1_ref, bacc_ref, w1_v, w2_v, sem1, sem2):
        i = pl.program_id(0)
        @pl.when(i == 0)
        def _():
            cps = [pltpu.make_async_copy(w1_hbm.at[k], w1_v.at[:, k * fb:(k + 1) * fb], sem1.at[k])
                   for k in range(n_fb)]
            for cp in cps:
                cp.start()
            for cp in cps:
                cp.wait()
        _load_once(w2_hbm, w2_v, sem2)

        @pl.when(i == 0)
        def _():
            gacc_ref[...] = jnp.zeros_like(gacc_ref)
            db1_ref[...] = jnp.zeros_like(db1_ref)

        @pl.when(i % npb == 0)
        def _():
            bacc_ref[...] = jnp.zeros_like(bacc_ref)

        m = mod_ref[0]
        sh2, sc2, g2 = m[3:4], m[4:5], m[5:6]
        x1v = x1_ref[...]
        u2 = (x1v * (1.0 + sc2) + sh2).astype(BF16)
        u2_ref[...] = u2
        hr = jnp.maximum(jnp.dot(u2, w1_v[...], preferred_element_type=F32) + b1_ref[...], 0.0)
        hb = (hr * hr).astype(BF16)
        h_ref[...] = hb
        o = jnp.dot(hb, w2_v[...], preferred_element_type=F32) + v_ref[0:1]
        r2 = ALPHA * x1v + (1.0 + g2) * o
        y, xhat, rstd = _layer_norm(r2, v_ref[1:2], v_ref[2:3])
        err = y - tgt_ref[...]
        dy = err * (1.0 / D_MODEL)
        dr2 = _layer_norm_bwd(dy, xhat, rstd, v_ref[1:2])
        do = (1.0 + g2) * dr2
        dob = do.astype(BF16)
        do_ref[...] = dob
        gacc_ref[0:1, :] += jnp.sum(dy * xhat, axis=0, keepdims=True)
        gacc_ref[1:2, :] += jnp.sum(dy, axis=0, keepdims=True)
        gacc_ref[2:3, :] += jnp.sum(do, axis=0, keepdims=True)
        gacc_ref[3:4, :] += jnp.sum(err * err, axis=0, keepdims=True)
        dhpre = lax.dot_general(dob, w2_v[...], NT, preferred_element_type=F32) * (2.0 * hr)
        dhpb = dhpre.astype(BF16)
        dhp_ref[...] = dhpb
        db1_ref[...] += jnp.sum(dhpre, axis=0, keepdims=True)
        du2 = lax.dot_general(dhpb, w1_v[...], NT, preferred_element_type=F32)
        dx1_ref[...] = ALPHA * dr2 + du2 * (1.0 + sc2)
        bacc_ref[0, 0:1, :] += jnp.sum(du2, axis=0, keepdims=True)
        bacc_ref[0, 1:2, :] += jnp.sum(du2 * x1v, axis=0, keepdims=True)
        bacc_ref[0, 2:3, :] += jnp.sum(dr2 * o, axis=0, keepdims=True)

    row = lambda w: pl.BlockSpec((tb, w), lambda i: (i, 0))
    return pl.pallas_call(
        body, name="mlp_fwd_bwd", grid=(t // tb,),
        out_shape=(jax.ShapeDtypeStruct((t, D_MODEL), F32), jax.ShapeDtypeStruct((t, D_MODEL), BF16),
                   jax.ShapeDtypeStruct((t, D_FF), BF16), jax.ShapeDtypeStruct((t, D_FF), BF16),
                   jax.ShapeDtypeStruct((t, D_MODEL), BF16), jax.ShapeDtypeStruct((8, D_MODEL), F32),
                   jax.ShapeDtypeStruct((1, D_FF), F32), jax.ShapeDtypeStruct((t // seq, 8, D_MODEL), F32)),
        in_specs=[row(D_MODEL), row(D_MODEL), pl.BlockSpec((1, N_MOD, D_MODEL), lambda i: (i // npb, 0, 0)), ANY, ANY,
                  pl.BlockSpec(vec1.shape, lambda i: (0, 0)), pl.BlockSpec(b1.shape, lambda i: (0, 0))],
        out_specs=(row(D_MODEL), row(D_MODEL), row(D_FF), row(D_FF), row(D_MODEL),
                   pl.BlockSpec((8, D_MODEL), lambda i: (0, 0)), pl.BlockSpec((1, D_FF), lambda i: (0, 0)),
                   pl.BlockSpec((1, 8, D_MODEL), lambda i: (i // npb, 0, 0))),
        scratch_shapes=[pltpu.VMEM((D_MODEL, n_fb * fb), BF16), pltpu.VMEM((D_FF, D_MODEL), BF16),
                        pltpu.SemaphoreType.DMA((n_fb,)), pltpu.SemaphoreType.DMA],
        compiler_params=_params(60),
    )(x1, tgt, mod3, w1, w2, vec1, b1)


def _ln1_out_bwd(dx1, x2, mix, mod3, w_out, ln1, seq):
    t = x2.shape[0]
    tb = 512
    npb = seq // tb

    def body(dx1_ref, x_ref, mix_ref, mod_ref, w_ref, ln_ref, dmix_ref, dxa_ref, dys_ref, dy5_ref, gacc_ref, bacc_ref):
        i = pl.program_id(0)

        @pl.when(i == 0)
        def _():
            gacc_ref[...] = jnp.zeros_like(gacc_ref)

        @pl.when(i % npb == 0)
        def _():
            bacc_ref[...] = jnp.zeros_like(bacc_ref)

        m = mod_ref[0]
        mix = mix_ref[...]
        r1 = ALPHA * x_ref[...] + (1.0 + m[2:3]) * mix
        _, xhat, rstd = _layer_norm(r1, ln_ref[0:1], ln_ref[1:2])
        dx1v = dx1_ref[...]
        dr1 = _layer_norm_bwd(dx1v, xhat, rstd, ln_ref[0:1])
        gacc_ref[0:1, :] += jnp.sum(dx1v * xhat, axis=0, keepdims=True)
        gacc_ref[1:2, :] += jnp.sum(dx1v, axis=0, keepdims=True)
        bacc_ref[0, 0:1, :] += jnp.sum(dr1 * mix, axis=0, keepdims=True)
        dmix = ((1.0 + m[2:3]) * dr1).astype(BF16)
        dmix_ref[...] = dmix
        dxa_ref[...] = ALPHA * dr1
        dys_ref[...] = lax.dot_general(dmix, w_ref[0:D_SSD, :], NT, preferred_element_type=F32)
        dy5_ref[...] = lax.dot_general(dmix, w_ref[D_SSD:, :], NT, preferred_element_type=F32)

    row = lambda w: pl.BlockSpec((tb, w), lambda i: (i, 0))
    return pl.pallas_call(
        body, name="ln1_out_bwd", grid=(t // tb,),
        out_shape=(jax.ShapeDtypeStruct((t, D_MODEL), BF16), jax.ShapeDtypeStruct((t, D_MODEL), F32),
                   jax.ShapeDtypeStruct((t, D_SSD), F32), jax.ShapeDtypeStruct((t, D_S5), F32),
                   jax.ShapeDtypeStruct((8, D_MODEL), F32), jax.ShapeDtypeStruct((t // seq, 8, D_MODEL), F32)),
        in_specs=[row(D_MODEL), row(D_MODEL), row(D_MODEL), pl.BlockSpec((1, N_MOD, D_MODEL), lambda i: (i // npb, 0, 0)),
                  pl.BlockSpec(w_out.shape, lambda i: (0, 0)), pl.BlockSpec(ln1.shape, lambda i: (0, 0))],
        out_specs=(row(D_MODEL), row(D_MODEL), row(D_SSD), row(D_S5), pl.BlockSpec((8, D_MODEL), lambda i: (0, 0)),
                   pl.BlockSpec((1, 8, D_MODEL), lambda i: (i // npb, 0, 0))),
        compiler_params=_params(48),
    )(dx1, x2, mix, mod3, w_out, ln1)


def _s5_bwd(dy5, ypre, u5, s_re, s_im, bb_re, bb_im, cc_re, cc_im, pr_re, pr_im, s5d, w_glu, b_glu, seq):
    t = u5.shape[0]
    tb = 256
    npb = seq // tb
    n_blocks = t // tb

    def blk(i):
        return (i // npb) * npb + (npb - 1 - i % npb)

    def body(dy_ref, ypre_ref, u_ref, sre_ref, sim_ref, hre_ref, him_ref, bbr_ref, bbi_ref, ccr_ref, cci_ref,
             prr_ref, pri_ref, d_ref, wg_ref, bg_ref,
             du_ref, vacc_ref, sacc_ref, dcc_ref, dbb_ref, dwg_ref, dsr, dsi, gr, gi, car, cai):
        i = pl.program_id(0)

        @pl.when(i == 0)
        def _():
            for acc in (vacc_ref, sacc_ref, dcc_ref, dbb_ref, dwg_ref):
                acc[...] = jnp.zeros_like(acc)

        @pl.when(i % npb == 0)
        def _():
            car[...] = jnp.zeros_like(car)
            cai[...] = jnp.zeros_like(cai)

        dy = dy_ref[...]
        ypre = ypre_ref[...]
        u = u_ref[...]
        ub = u.astype(BF16)
        yg = _gelu(ypre)
        sg = _sigmoid(_mm(yg, wg_ref[...]) + bg_ref[...])
        dq = dy * yg * sg * (1.0 - sg)
        dqb = dq.astype(BF16)
        dyg = dy * sg + lax.dot_general(dqb, wg_ref[...], NT, preferred_element_type=F32)
        dyp = dyg * _gelu_grad(ypre)
        dypb = dyp.astype(BF16)
        dwg_ref[...] += lax.dot_general(yg.astype(BF16), dqb, TN, preferred_element_type=F32)
        blocks = [(slice(j * 128, (j + 1) * 128), slice(j * 512, (j + 1) * 512)) for j in range(S5_BLOCKS)]
        for j, (ch, st) in enumerate(blocks):
            dsr[:, st] = lax.dot_general(dypb[:, ch], ccr_ref[j], NT, preferred_element_type=F32)
            dsi[:, st] = -lax.dot_general(dypb[:, ch], cci_ref[j], NT, preferred_element_type=F32)
        _tile_scan(dsr, dsi, gr, gi, car, cai, prr_ref, pri_ref, tb // 8, reverse=True)
        g_re, g_im = gr[...], gi[...]
        first_rows = (i % npb) == npb - 1
        hre = jnp.where(first_rows, 0.0, hre_ref[...])
        him = jnp.where(first_rows, 0.0, him_ref[...])
        s_re_v, s_im_v = sre_ref[...], sim_ref[...]
        sp_re = pltpu.roll(jnp.concatenate([hre, s_re_v], axis=0), 1, axis=0)[8:8 + tb]
        sp_im = pltpu.roll(jnp.concatenate([him, s_im_v], axis=0), 1, axis=0)[8:8 + tb]
        vacc_ref[0:1, :] += jnp.sum(g_re * sp_re + g_im * sp_im, axis=0, keepdims=True)
        vacc_ref[1:2, :] += jnp.sum(g_im * sp_re - g_re * sp_im, axis=0, keepdims=True)
        grb, gib = g_re.astype(BF16), g_im.astype(BF16)
        srb, sib = s_re_v.astype(BF16), s_im_v.astype(BF16)
        du_cols = []
        for j, (ch, st) in enumerate(blocks):
            dcc_ref[j] += lax.dot_general(srb[:, st], dypb[:, ch], TN, preferred_element_type=F32)
            dcc_ref[S5_BLOCKS + j] -= lax.dot_general(sib[:, st], dypb[:, ch], TN, preferred_element_type=F32)
            dbb_ref[j] += lax.dot_general(ub[:, ch], grb[:, st], TN, preferred_element_type=F32)
            dbb_ref[S5_BLOCKS + j] += lax.dot_general(ub[:, ch], gib[:, st], TN, preferred_element_type=F32)
            du_cols.append(lax.dot_general(grb[:, st], bbr_ref[j], NT, preferred_element_type=F32)
                           + lax.dot_general(gib[:, st], bbi_ref[j], NT, preferred_element_type=F32))
        du_ref[...] = jnp.concatenate(du_cols, axis=1) + dyp * d_ref[...]
        sacc_ref[0:1, :] += jnp.sum(dyp * u, axis=0, keepdims=True)
        sacc_ref[1:2, :] += jnp.sum(dq, axis=0, keepdims=True)

    row = lambda w: pl.BlockSpec((tb, w), lambda i: (blk(i), 0))
    halo = pl.BlockSpec((8, S5_N), lambda i: (jnp.maximum(blk(i) * (tb // 8) - 1, 0), 0))
    full = lambda a: pl.BlockSpec(a.shape, lambda i: (0,) * a.ndim)
    acc = lambda s: pl.BlockSpec(s, lambda i: (0,) * len(s))
    acc_shapes = [(8, S5_N), (8, D_S5), (2 * S5_BLOCKS, 512, 128), (2 * S5_BLOCKS, 128, 512), (D_S5, D_S5)]
    return pl.pallas_call(
        body, name="s5_bwd", grid=(n_blocks,),
        out_shape=(jax.ShapeDtypeStruct((t, D_S5), F32),) + tuple(jax.ShapeDtypeStruct(s, F32) for s in acc_shapes),
        in_specs=[row(D_S5), row(D_S5), row(D_S5), row(S5_N), row(S5_N), halo, halo, full(bb_re), full(bb_im),
                  full(cc_re), full(cc_im), full(pr_re), full(pr_im), full(s5d), full(w_glu), full(b_glu)],
        out_specs=(row(D_S5),) + tuple(acc(s) for s in acc_shapes),
        scratch_shapes=[pltpu.VMEM((tb, S5_N), F32), pltpu.VMEM((tb, S5_N), F32), pltpu.VMEM((tb, S5_N), F32),
                        pltpu.VMEM((tb, S5_N), F32), pltpu.VMEM((8, S5_N), F32), pltpu.VMEM((8, S5_N), F32)],
        compiler_params=_params(56),
    )(dy5, ypre, u5, s_re, s_im, s_re, s_im, bb_re, bb_im, cc_re, cc_im, pr_re, pr_im, s5d, w_glu, b_glu)


def _ssd_bwd(dyssd, yraw, z, xbc, dt_raw, hprev, par, dsk, normw, seq):
    t = xbc.shape[0]
    nc = seq // CHUNK
    n_chunks = t // CHUNK
    fold = _head_fold()

    def blk(i):
        return (i // nc) * nc + (nc - 1 - i % nc)

    def body(dy_ref, yraw_ref, z_ref, xbc_ref, dt_ref, hprev_ref, par_ref, dsk_ref, nw_ref, fold_ref,
             dxbc_ref, dz_ref, ddt_ref, dpar_ref, cacc_ref, dh_ref, dyr_ref):
        i = pl.program_id(0)

        @pl.when(i == 0)
        def _():
            dpar_ref[...] = jnp.zeros_like(dpar_ref)
            cacc_ref[...] = jnp.zeros_like(cacc_ref)

        @pl.when(i % nc == 0)
        def _():
            dh_ref[...] = jnp.zeros_like(dh_ref)

        zz = z_ref[...]
        sz = _sigmoid(zz)
        silu_z = zz * sz
        yraw = yraw_ref[...]
        for g in range(N_GROUPS):
            sl = slice(g * GW, (g + 1) * GW)
            v = yraw[:, sl] * silu_z[:, sl]
            r = lax.rsqrt(jnp.mean(v * v, axis=-1, keepdims=True) + EPS)
            dyg = dy_ref[:, sl]
            cacc_ref[1:2, sl] += jnp.sum(dyg * v * r, axis=0, keepdims=True)
            dyw = dyg * nw_ref[:, sl]
            dv = r * dyw - v * (r * r * r) * jnp.mean(dyw * v, axis=-1, keepdims=True)
            dyr_ref[:, sl] = dv * silu_z[:, sl]
            dz_ref[:, sl] = dv * yraw[:, sl] * (sz[:, sl] * (1.0 + zz[:, sl] * (1.0 - sz[:, sl])))

        dt, a, cs, cst, causal, tri, dt_c, ecs_c, w_c, pair_cols = _ssd_prep(dt_ref[...], par_ref[...])
        cs_last = cs[CHUNK - 1:CHUNK, :]
        causal2 = jnp.concatenate([causal, causal], axis=1)
        lane = lax.broadcasted_iota(jnp.int32, (CHUNK, 128), 1)
        left = lane < HEADDIM
        lane1 = lax.broadcasted_iota(jnp.int32, (1, 128), 1)
        x = xbc_ref[:, 0:D_SSD]
        xdt = x * dt_c
        dyr = dyr_ref[...]
        dyrb = dyr.astype(BF16)
        cacc_ref[0:1, :] += jnp.sum(dyr * x, axis=0, keepdims=True)
        dlast = jnp.zeros((1, 128), F32)
        dxdt_cols, diag_all, dww_cols = [], [], []
        for g in range(N_GROUPS):
            gs = slice(g * GW, (g + 1) * GW)
            b_sl = slice(D_SSD + g * N_STATE, D_SSD + (g + 1) * N_STATE)
            c_sl = slice(D_SSD + (N_GROUPS + g) * N_STATE, D_SSD + (N_GROUPS + g + 1) * N_STATE)
            bg = xbc_ref[:, b_sl].astype(BF16)
            cg = xbc_ref[:, c_sl].astype(BF16)
            scores = lax.dot_general(cg, bg, NT, preferred_element_type=F32)
            scores2 = jnp.concatenate([scores, scores], axis=1)
            hg = hprev_ref[0, gs, :]
            hgb = hg.astype(BF16)
            dhg = dh_ref[gs, :]
            dhgb = dhg.astype(BF16)
            q_all = lax.dot_general(bg, dhgb, NT, preferred_element_type=F32)
            dscores = jnp.zeros((CHUNK, CHUNK), F32)
            diag_cols = []
            for q in range(GW // 128):
                pair = g * (GW // 128) + q
                ps = slice(pair * 128, (pair + 1) * 128)
                decay = _pair_decay(pair_cols[pair], cst, pair, causal2)
                mcat = (scores2 * decay).astype(BF16)
                dyp = dyrb[:, ps]
                dm = lax.dot_general(dyp, _stack_heads(xdt[:, ps], left), NT, preferred_element_type=F32)
                dmd = dm * decay
                dscores = dscores + dmd[:, 0:CHUNK] + dmd[:, CHUNK:]
                rr = lax.dot_general(mcat, dyp, TN, preferred_element_type=F32)
                diag_cols.append(jnp.where(left, rr[0:CHUNK], rr[CHUNK:]))
            wq = w_c[:, gs] * q_all
            diag_g = jnp.concatenate(diag_cols, axis=1)
            diag_all.append(diag_g)
            dxdt_cols.append(diag_g + wq)
            dww_cols.append(wq * xdt[:, gs])
            dp = (ecs_c[:, gs] * dyr[:, gs]).astype(BF16)
            amat = (w_c[:, gs] * xdt[:, gs]).astype(BF16)
            dsb = dscores.astype(BF16)
            dxbc_ref[:, c_sl] = (jnp.dot(dsb, bg, preferred_element_type=F32)
                                 + jnp.dot(dp, hgb, preferred_element_type=F32))
            dxbc_ref[:, b_sl] = (lax.dot_general(dsb, cg, TN, preferred_element_type=F32)
                                 + jnp.dot(amat, dhgb, preferred_element_type=F32))
            dh_in = lax.dot_general(dp, cg, TN, preferred_element_type=F32)
            for j in range(HPG):
                hh = g * HPG + j
                js = slice(j * HEADDIM, (j + 1) * HEADDIM)
                ecl = jnp.exp(cs_last[:, hh:hh + 1])
                dlast = dlast + jnp.where(lane1 == hh, ecl * jnp.sum(dhg[js, :] * hg[js, :]), 0.0)
                dh_ref[g * GW + j * HEADDIM:g * GW + (j + 1) * HEADDIM, :] = ecl * dhg[js, :] + dh_in[js, :]
        dxdt = jnp.concatenate(dxdt_cols, axis=1)
        dxbc_ref[:, 0:D_SSD] = dxdt * dt_c + dyr * dsk_ref[...]
        dww = _mm(jnp.concatenate(dww_cols, axis=1), fold_ref[...])
        dcs = _dot3(dyrb.astype(F32) * (yraw - x * dsk_ref[...])
                    - xdt.astype(BF16).astype(F32) * jnp.concatenate(diag_all, axis=1), fold_ref[...]) - dww
        rowid = lax.broadcasted_iota(jnp.int32, (CHUNK, 128), 0)
        dcs = dcs + jnp.where(rowid == CHUNK - 1, jnp.sum(dww, axis=0, keepdims=True) + dlast, 0.0)
        dadt = _dot3_left(tri, dcs, TN)
        ddt = _mm(dxdt * x, fold_ref[...]) + dadt * a
        da = jnp.sum(dadt * dt, axis=0, keepdims=True)
        ddt_raw = ddt * _sigmoid(dt_ref[...] + par_ref[0:1])
        ddt_raw = jnp.where(lane < N_HEADS, ddt_raw, 0.0)
        ddt_ref[...] = ddt_raw
        dpar_ref[0:1, :] += jnp.sum(ddt_raw, axis=0, keepdims=True)
        dpar_ref[1:2, :] += jnp.where(lane1 < N_HEADS, da * a, 0.0)

    row = lambda w: pl.BlockSpec((CHUNK, w), lambda i: (blk(i), 0))
    full = lambda s: pl.BlockSpec(s, lambda i: (0,) * len(s))
    return pl.pallas_call(
        body, name="ssd_bwd", grid=(n_chunks,),
        out_shape=(jax.ShapeDtypeStruct((t, D_XBC), F32), jax.ShapeDtypeStruct((t, D_SSD), F32),
                   jax.ShapeDtypeStruct((t, DT_PAD), F32), jax.ShapeDtypeStruct((8, 128), F32),
                   jax.ShapeDtypeStruct((8, D_SSD), F32)),
        in_specs=[row(D_SSD), row(D_SSD), row(D_SSD), row(D_XBC), row(DT_PAD),
                  pl.BlockSpec((1, D_SSD, N_STATE), lambda i: (blk(i), 0, 0)),
                  full((8, 128)), full((1, D_SSD)), full((1, D_SSD)), full(fold.shape)],
        out_specs=(row(D_XBC), row(D_SSD), row(DT_PAD), full((8, 128)), full((8, D_SSD))),
        scratch_shapes=[pltpu.VMEM((D_SSD, N_STATE), F32), pltpu.VMEM((CHUNK, D_SSD), F32)],
        compiler_params=_params(48),
    )(dyssd, yraw, z, xbc, dt_raw, hprev, par, dsk, normw, fold)


def _conv_proj_bwd(dz, dxbc, dsilu, xbc_pre, ddt, du5, x2, dxa, mod3, conv_w, w_in_pad, seq):
    t = x2.shape[0]
    tb = 256
    npb = seq // tb
    n_blocks = t // tb
    cw = 512

    def blk(i):
        return (i // npb) * npb + (npb - 1 - i % npb)

    def body(dz_ref, d_ref, ds_ref, cur_ref, halo_ref, ddt_ref, du5_ref, x_ref, dxa_ref, mod_ref, cw_ref, w_hbm,
             gx_ref, u_ref, dxp_ref, bacc_ref, acc_ref, w_vmem, win_x, win_d, sem):
        i = pl.program_id(0)
        _load_once(w_hbm, w_vmem, sem)

        @pl.when(i == 0)
        def _():
            acc_ref[...] = jnp.zeros_like(acc_ref)

        @pl.when(i % npb == 0)
        def _():
            bacc_ref[...] = jnp.zeros_like(bacc_ref)
            win_d[tb:tb + 8, :] = jnp.zeros((8, D_XBC), F32)

        @pl.when(i % npb != 0)
        def _():
            win_d[tb:tb + 8, :] = win_d[0:8, :]

        first_rows = (i % npb) == npb - 1
        win_x[0:8, :] = jnp.where(first_rows, 0.0, halo_ref[...])
        win_x[8:8 + tb, :] = cur_ref[...]
        w = cw_ref[...]
        for k in range(D_XBC // cw):
            cols = slice(k * cw, (k + 1) * cw)
            dpre = d_ref[:, cols] * ds_ref[:, cols]
            win_d[0:tb, cols] = dpre
            for j in range(4):
                acc_ref[3 - j:4 - j, cols] += jnp.sum(dpre * win_x[8 - j:8 - j + tb, cols], axis=0, keepdims=True)
            acc_ref[4:5, cols] += jnp.sum(dpre, axis=0, keepdims=True)
            dxp = w[3:4, cols] * dpre
            for j in (1, 2, 3):
                dxp = dxp + w[3 - j:4 - j, cols] * win_d[j:j + tb, cols]
            dxp_ref[:, cols] = dxp.astype(BF16)
        o1, o2, o3 = D_SSD, D_SSD + D_XBC, D_SSD + D_XBC + DT_PAD
        du = (jnp.dot(dz_ref[...].astype(BF16), w_vmem[0:o1, :], preferred_element_type=F32)
              + jnp.dot(dxp_ref[...], w_vmem[o1:o2, :], preferred_element_type=F32)
              + jnp.dot(ddt_ref[...].astype(BF16), w_vmem[o2:o3, :], preferred_element_type=F32)
              + jnp.dot(du5_ref[...].astype(BF16), w_vmem[o3:, :], preferred_element_type=F32))
        m = mod_ref[0]
        xv = x_ref[...]
        u_ref[...] = (xv * (1.0 + m[1:2]) + m[0:1]).astype(BF16)
        gx_ref[...] = dxa_ref[...] + du * (1.0 + m[1:2])
        bacc_ref[0, 0:1, :] += jnp.sum(du, axis=0, keepdims=True)
        bacc_ref[0, 1:2, :] += jnp.sum(du * xv, axis=0, keepdims=True)

    row = lambda w: pl.BlockSpec((tb, w), lambda i: (blk(i), 0))
    halo = pl.BlockSpec((8, D_XBC), lambda i: (jnp.maximum(blk(i) * (tb // 8) - 1, 0), 0))
    return pl.pallas_call(
        body, name="conv_proj_bwd", grid=(n_blocks,),
        out_shape=(jax.ShapeDtypeStruct((t, D_MODEL), F32), jax.ShapeDtypeStruct((t, D_MODEL), BF16),
                   jax.ShapeDtypeStruct((t, D_XBC), BF16), jax.ShapeDtypeStruct((t // seq, 8, D_MODEL), F32),
                   jax.ShapeDtypeStruct((8, D_XBC), F32)),
        in_specs=[row(D_SSD), row(D_XBC), row(D_XBC), row(D_XBC), halo, row(DT_PAD), row(D_S5), row(D_MODEL),
                  row(D_MODEL), pl.BlockSpec((1, N_MOD, D_MODEL), lambda i: (i // npb, 0, 0)),
                  pl.BlockSpec((4, D_XBC), lambda i: (0, 0)), ANY],
        out_specs=(row(D_MODEL), row(D_MODEL), row(D_XBC), pl.BlockSpec((1, 8, D_MODEL), lambda i: (i // npb, 0, 0)),
                   pl.BlockSpec((8, D_XBC), lambda i: (0, 0))),
        scratch_shapes=[pltpu.VMEM((D_INP, D_MODEL), BF16), pltpu.VMEM((tb + 8, D_XBC), F32),
                        pltpu.VMEM((tb + 8, D_XBC), F32), pltpu.SemaphoreType.DMA],
        compiler_params=_params(60),
    )(dz, dxbc, dsilu, xbc_pre, xbc_pre, ddt, du5, x2, dxa, mod3, conv_w, w_in_pad)


def _pad_rows(a, mult):
    r = a.shape[0]
    pad = (-r) % mult
    return a if pad == 0 else jnp.concatenate([a, jnp.zeros((pad,) + a.shape[1:], a.dtype)], axis=0)


_SMALL = ["conv_w", "conv_b", "dt_bias", "a_log", "d_ssd", "norm_w", "s5_a_re", "s5_a_im", "s5_log_dt", "s5_b_re",
          "s5_b_im", "s5_c_re", "s5_c_im", "s5_d", "b_glu", "ln1_g", "ln1_b", "b1", "b2", "ln2_g", "ln2_b"]


def _tile_rows(size):
    return 8 * (-(-size // 1024))


def _pack_small(d):
    parts = []
    for n in _SMALL:
        flat = d[n].reshape(-1).astype(F32)
        rows = _tile_rows(flat.shape[0])
        pad = rows * 128 - flat.shape[0]
        if pad:
            flat = jnp.concatenate([flat, jnp.zeros((pad,), F32)])
        parts.append(flat.reshape(rows, 128))
    return jnp.concatenate(parts, axis=0)


def _unpack_small(p, shapes):
    out, off = {}, 0
    for n in _SMALL:
        size = math.prod(shapes[n])
        rows = _tile_rows(size)
        out[n] = p[off:off + rows].reshape(-1)[:size].reshape(shapes[n])
        off += rows
    return out


def kernel(x, c, w_ada, b_ada, w_in, conv_w, conv_b, dt_bias, a_log, d_ssd, norm_w, s5_a_re, s5_a_im, s5_log_dt, s5_b_re, s5_b_im, s5_c_re, s5_c_im, s5_d, w_glu, b_glu, w_out, ln1_g, ln1_b, w1, b1, w2, b2, ln2_g, ln2_b, loss_target, m_w_ada, m_b_ada, m_w_in, m_conv_w, m_conv_b, m_dt_bias, m_a_log, m_d_ssd, m_norm_w, m_s5_a_re, m_s5_a_im, m_s5_log_dt, m_s5_b_re, m_s5_b_im, m_s5_c_re, m_s5_c_im, m_s5_d, m_w_glu, m_b_glu, m_w_out, m_ln1_g, m_ln1_b, m_w1, m_b1, m_w2, m_b2, m_ln2_g, m_ln2_b, v_w_ada, v_b_ada, v_w_in, v_conv_w, v_conv_b, v_dt_bias, v_a_log, v_d_ssd, v_norm_w, v_s5_a_re, v_s5_a_im, v_s5_log_dt, v_s5_b_re, v_s5_b_im, v_s5_c_re, v_s5_c_im, v_s5_d, v_w_glu, v_b_glu, v_w_out, v_ln1_g, v_ln1_b, v_w1, v_b1, v_w2, v_b2, v_ln2_g, v_ln2_b):
    weights = dict(w_ada=w_ada, b_ada=b_ada, w_in=w_in, conv_w=conv_w, conv_b=conv_b, dt_bias=dt_bias, a_log=a_log,
                   d_ssd=d_ssd, norm_w=norm_w, s5_a_re=s5_a_re, s5_a_im=s5_a_im, s5_log_dt=s5_log_dt, s5_b_re=s5_b_re,
                   s5_b_im=s5_b_im, s5_c_re=s5_c_re, s5_c_im=s5_c_im, s5_d=s5_d, w_glu=w_glu, b_glu=b_glu, w_out=w_out,
                   ln1_g=ln1_g, ln1_b=ln1_b, w1=w1, b1=b1, w2=w2, b2=b2, ln2_g=ln2_g, ln2_b=ln2_b)
    mom = dict(w_ada=m_w_ada, b_ada=m_b_ada, w_in=m_w_in, conv_w=m_conv_w, conv_b=m_conv_b, dt_bias=m_dt_bias,
               a_log=m_a_log, d_ssd=m_d_ssd, norm_w=m_norm_w, s5_a_re=m_s5_a_re, s5_a_im=m_s5_a_im,
               s5_log_dt=m_s5_log_dt, s5_b_re=m_s5_b_re, s5_b_im=m_s5_b_im, s5_c_re=m_s5_c_re, s5_c_im=m_s5_c_im,
               s5_d=m_s5_d, w_glu=m_w_glu, b_glu=m_b_glu, w_out=m_w_out, ln1_g=m_ln1_g, ln1_b=m_ln1_b, w1=m_w1, b1=m_b1,
               w2=m_w2, b2=m_b2, ln2_g=m_ln2_g, ln2_b=m_ln2_b)
    var = dict(w_ada=v_w_ada, b_ada=v_b_ada, w_in=v_w_in, conv_w=v_conv_w, conv_b=v_conv_b, dt_bias=v_dt_bias,
               a_log=v_a_log, d_ssd=v_d_ssd, norm_w=v_norm_w, s5_a_re=v_s5_a_re, s5_a_im=v_s5_a_im,
               s5_log_dt=v_s5_log_dt, s5_b_re=v_s5_b_re, s5_b_im=v_s5_b_im, s5_c_re=v_s5_c_re, s5_c_im=v_s5_c_im,
               s5_d=v_s5_d, w_glu=v_w_glu, b_glu=v_b_glu, w_out=v_w_out, ln1_g=v_ln1_g, ln1_b=v_ln1_b, w1=v_w1, b1=v_b1,
               w2=v_w2, b2=v_b2, ln2_g=v_ln2_g, ln2_b=v_ln2_b)
    names = list(weights)
    shapes = {n: weights[n].shape for n in names}

    nb, seq, _ = x.shape
    t = nb * seq
    dev = _dev_index()
    x2 = x.reshape(t, D_MODEL)
    tgt2 = loss_target.reshape(t, D_MODEL)

    cw_cols = conv_w.shape[2]
    small_in = jnp.concatenate([c.reshape(-1), conv_w.reshape(-1)]).reshape(-1, 128)
    big_names = ["w_in", "w_out", "w1", "w2", "w_glu"]
    local = {n: (a[0].T if n == "w_in" else a[0]) for n, a in weights.items() if n in big_names}
    shard_bf16 = {n: local[n].astype(BF16) for n in big_names}
    first = _all_gather([small_in, shard_bf16["w_in"], shard_bf16["w_glu"]], "gather_first")
    small_all = first[0].reshape(N_DEV, -1)
    c_all = small_all[:, :nb * D_MODEL].reshape(N_DEV * nb, D_MODEL)
    conv_w_full = small_all[:, nb * D_MODEL:].reshape(N_DEV, 4, cw_cols).transpose(1, 0, 2).reshape(4, D_XBC)

    w_in_t = first[1].reshape(D_IN, D_MODEL)
    w_in_pad = jnp.concatenate(
        [w_in_t[:D_SSD + D_XBC + N_HEADS], jnp.zeros((DT_PAD - N_HEADS, D_MODEL), BF16),
         w_in_t[D_SSD + D_XBC + N_HEADS:]], axis=0)
    w_glu_f = first[2].reshape(D_S5, D_S5)
    late_names = ["w_out", "w1", "w2"]

    ada_cols = w_ada.shape[2]
    b_cols = lax.dynamic_slice_in_dim(b_ada, dev * ada_cols, ada_cols, axis=1)
    mod_cols = _mod_fwd(c_all, w_ada[0], b_cols)
    mod_all = _all_gather([mod_cols], "gather_mod")[0]
    mod_mine = lax.dynamic_slice_in_dim(mod_all, dev * nb, nb, axis=1)
    mod3 = mod_mine.transpose(1, 0, 2).reshape(nb, N_MOD, D_MODEL)
    late_in, mod3 = lax.optimization_barrier(([shard_bf16[n] for n in late_names], mod3))
    late_sems = _gather_start(late_in, "gather_late_start")
    mod3 = mod3 + late_sems[4][0, 0]

    def pad_lanes(v, n):
        return jnp.concatenate([v, jnp.zeros((v.shape[0], n - v.shape[1]), F32)], axis=1)

    par = _pad_rows(jnp.concatenate([pad_lanes(dt_bias, 128), pad_lanes(a_log, 128)], axis=0), 8)
    dsk = jnp.repeat(d_ssd[0], HEADDIM).reshape(1, D_SSD)
    ar = s5_a_re.reshape(1, S5_N)
    ai = s5_a_im.reshape(1, S5_N)
    ldt = jnp.repeat(s5_log_dt[0], S5_P).reshape(1, S5_N)
    br_t = s5_b_re[0].transpose(2, 0, 1).reshape(S5_CH, S5_N)
    bi_t = s5_b_im[0].transpose(2, 0, 1).reshape(S5_CH, S5_N)
    bb_re_t, bb_im_t, pf_re, pf_im, pr_re, pr_im = _s5_params_fwd(ar, ai, ldt, br_t, bi_t)
    gpb = S5_GROUPS // S5_BLOCKS
    mask_b = (jnp.arange(128)[:, None] // S5_CH) == (jnp.arange(512)[None, :] // S5_P)

    def dense_b(bt_):
        blocks = bt_.reshape(S5_CH, S5_BLOCKS, 512).transpose(1, 0, 2)
        return jnp.where(mask_b, jnp.tile(blocks, (1, gpb, 1)), 0.0).astype(BF16)

    def dense_c(cc):
        blocks = cc[0].transpose(0, 2, 1).reshape(S5_BLOCKS, 512, S5_CH)
        return jnp.where(mask_b.T, jnp.tile(blocks, (1, 1, gpb)), 0.0).astype(BF16)

    bb_re, bb_im = dense_b(bb_re_t), dense_b(bb_im_t)
    cc_re, cc_im = dense_c(s5_c_re), dense_c(s5_c_im)
    s5d = s5_d.reshape(1, D_S5)
    ln1 = jnp.concatenate([ln1_g, ln1_b], axis=0)
    vec1 = _pad_rows(jnp.concatenate([b2, ln2_g, ln2_b], axis=0), 8)

    z, xbc_pre, xbc, dsilu, dt_raw, u5 = _proj_conv_fwd(x2, mod3, w_in_pad, conv_w_full, conv_b, seq)
    yraw, ycat, hprev = _ssd_fwd(xbc, z, dt_raw, par, dsk, norm_w, seq)
    s_re, s_im, ypre, ycat = _s5_fwd(u5, bb_re, bb_im, cc_re, cc_im, pf_re, pf_im, s5d, w_glu_f, b_glu, ycat, seq)
    sent, landed = _gather_wait(late_sems[0], late_sems[1], late_sems[2], late_sems[3], ycat, "gather_late_wait")
    gathered = {n: lax.dynamic_update_index_in_dim(l, x, dev, 0) for n, x, l in zip(late_names, sent, landed)}
    w_out_f = gathered["w_out"].reshape(2 * D_MODEL, D_MODEL)
    w1_blocks = gathered["w1"]
    w2_f = gathered["w2"].reshape(D_FF, D_MODEL)
    mix, x1 = _out_ln1(ycat, x2, mod3, w_out_f, ln1, seq)

    dx1, u2b, hb, dhpb, dob, gacc2, db1, bacc2 = _mlp_fwd_bwd(x1, tgt2, mod3, w1_blocks, w2_f, vec1, b1, seq)
    loss = lax.psum(0.5 / D_MODEL * jnp.sum(gacc2[3]), ("x", "y", "c"))

    dmixb, dxa, dyssd, dy5, gacc1, bacc1 = _ln1_out_bwd(dx1, x2, mix, mod3, w_out_f, ln1, seq)

    g_w2 = _atb(hb, dob, "gw2")
    g_w1 = _atb(u2b, dhpb, "gw1")
    g_wout = _atb(ycat, dmixb, "gwout")
    core = lax.axis_index("c").astype(jnp.int32).reshape(1)
    chip = 2 * lax.axis_index("x") + lax.axis_index("y")

    def chip_sums_of(names, grads, tag):
        by_dest = [g if g.ndim == 2 else g.reshape((4, 2) + g.shape[1:]) for g in grads]
        from_sibling = _sibling_swap(by_dest, "rs_swap_" + tag)
        return [_add_halves(g, r, core, "rs_add_" + n) for g, r, n in zip(by_dest, from_sibling, names)]

    early_names = ["w_out", "w1", "w2"]
    early_dest = [g_wout.reshape((4, 2) + w_out.shape[1:]), g_w1, g_w2.reshape((4, 2) + w2.shape[1:])]
    swap = _sibling_swap_start(early_dest, "rs_early_swap_start")
    du5, vacc, sacc, d_cc, d_bb, g_wglu = _s5_bwd(dy5, ypre, u5, s_re, s_im, bb_re, bb_im, cc_re, cc_im,
                                                  pr_re, pr_im, s5d + swap[4][0, 0], w_glu_f, b_glu, seq)
    early_dest, from_sibling = _sibling_swap_wait(swap[0], swap[1], swap[2], swap[3], du5, "rs_early_swap_wait")
    early_sums = [_add_halves(g, r, core, "rs_add_" + n) for g, r, n in zip(early_dest, from_sibling, early_names)]
    early = _all_to_all_start(early_sums, "rs_early_start")
    dxbc, dz, ddt, dpar, cacc = _ssd_bwd(dyssd, yraw, z, xbc, dt_raw, hprev, par + early[4][0, 0], dsk, norm_w, seq)
    grad_x2, ub, dxpb, bacc0, conv_acc = _conv_proj_bwd(dz, dxbc, dsilu, xbc_pre, ddt, du5, x2, dxa, mod3,
                                                        conv_w_full, w_in_pad, seq)

    g_win_t = jnp.concatenate([_atb(dz, ub, "gwin_z"), _atb(dxpb, ub, "gwin_xbc"),
                               _atb(ddt, ub, "gwin_dt")[:N_HEADS], _atb(du5, ub, "gwin_s5")], axis=0)

    def diag_b(dd):
        kept = jnp.where(mask_b, dd, 0.0).reshape(S5_BLOCKS, gpb, S5_CH, 512).sum(1)
        return kept.transpose(1, 0, 2).reshape(S5_CH, S5_N)

    def diag_c(dd):
        kept = jnp.where(mask_b.T, dd, 0.0).reshape(S5_BLOCKS, 512, gpb, S5_CH).sum(2)
        return kept.reshape(S5_GROUPS, S5_P, S5_CH).transpose(0, 2, 1)

    g_ar, g_ai, g_ldt, g_br_t, g_bi_t = _s5_params_bwd(ar, ai, ldt, br_t, bi_t, vacc[0:1], vacc[1:2],
                                                      diag_b(d_bb[:S5_BLOCKS]), diag_b(d_bb[S5_BLOCKS:]))

    def from_t(gt):
        return gt.reshape(S5_CH, S5_GROUPS, S5_P).transpose(1, 2, 0)

    small_g = dict(
        conv_w=conv_acc[0:4], conv_b=conv_acc[4:5], dt_bias=dpar[0:1, :N_HEADS], a_log=dpar[1:2, :N_HEADS],
        d_ssd=cacc[0].reshape(N_HEADS, HEADDIM).sum(1), norm_w=cacc[1:2],
        s5_a_re=g_ar, s5_a_im=g_ai, s5_log_dt=g_ldt[:, :S5_GROUPS], s5_b_re=from_t(g_br_t), s5_b_im=from_t(g_bi_t),
        s5_c_re=diag_c(d_cc[:S5_BLOCKS]), s5_c_im=diag_c(d_cc[S5_BLOCKS:]), s5_d=sacc[0:1], b_glu=sacc[1:2],
        ln1_g=gacc1[0:1], ln1_b=gacc1[1:2], b1=db1, b2=gacc2[2:3], ln2_g=gacc2[0:1], ln2_b=gacc2[1:2])

    dmod = jnp.concatenate([bacc0[:, 0], bacc0[:, 1], bacc1[:, 0], bacc2[:, 0], bacc2[:, 1], bacc2[:, 2]], axis=1)
    dmod_all, small_parts = _all_gather([dmod, _pack_small(small_g)], "gather_dmod_small_grads")
    dmod_all = dmod_all.reshape(N_DEV * nb, N_MOD * D_MODEL)
    dmod_cols = lax.dynamic_slice_in_dim(dmod_all, dev * ada_cols, ada_cols, axis=1)
    g_wada, g_bada = _mod_bwd(c_all, dmod_cols, dmod_all)

    late_rs = ["w_in", "w_glu"]
    late_g, small_parts = lax.optimization_barrier(
        ([g_win_t.reshape(N_DEV, w_in.shape[2], D_MODEL), g_wglu.reshape((N_DEV,) + w_glu.shape[1:])], small_parts))
    late = _all_to_all_start(chip_sums_of(late_rs, late_g, "late"), "rs_late_start")

    def own_block_in(landed, sent):
        return [lax.dynamic_update_index_in_dim(l, lax.dynamic_index_in_dim(h, chip, 0, keepdims=False), chip, 0)
                for l, h in zip(landed, sent)]

    sent, landed = _all_to_all_wait(early[0], early[1], early[2], early[3], late[4], "rs_early_wait")
    parts = dict(zip(early_names, own_block_in(landed, sent)))
    res = {k: {} for k in "gdmv"}

    def update(n):
        w_m_v = [(a[n][0].T if n == "w_in" else a[n][0]) for a in (weights, mom, var)]
        outs = _adamw(parts[n], *w_m_v, "adamw_" + n)
        for k, a in zip("gdmv", outs):
            res[k][n] = (a.T if n == "w_in" else a)[None]

    for n in early_names:
        update(n)
    sent, landed = _all_to_all_wait(late[0], late[1], late[2], late[3], res["d"]["w2"], "rs_late_wait")
    parts.update(zip(late_rs, own_block_in(landed, sent)))
    for n in late_rs:
        update(n)

    ag, ad, am, av = _adamw(g_wada[None], w_ada[0], m_w_ada[0], v_w_ada[0], "adamw_w_ada")
    for k, a in (("g", ag), ("d", ad), ("m", am), ("v", av)):
        res[k]["w_ada"] = a[None]
    bg_, bd_, bm_, bv_ = _adamw(g_bada.reshape(1, -1, 128), b_ada.reshape(-1, 128), m_b_ada.reshape(-1, 128),
                                v_b_ada.reshape(-1, 128), "adamw_b_ada")
    for k, a in (("g", bg_), ("d", bd_), ("m", bm_), ("v", bv_)):
        res[k]["b_ada"] = a.reshape(shapes["b_ada"])

    small_shapes = dict(shapes)
    small_shapes["conv_w"] = (1, 4, D_XBC)
    rep = {n: (jnp.zeros((1, 4, D_XBC), F32) if n == "conv_w" else weights[n]) for n in _SMALL}
    rep_m = {n: (jnp.zeros((1, 4, D_XBC), F32) if n == "conv_w" else mom[n]) for n in _SMALL}
    rep_v = {n: (jnp.ones((1, 4, D_XBC), F32) if n == "conv_w" else var[n]) for n in _SMALL}
    sg_, sd_, sm_, sv_ = _adamw(small_parts, _pack_small(rep), _pack_small(rep_m), _pack_small(rep_v), "adamw_small")
    for k, p in (("g", sg_), ("d", sd_), ("m", sm_), ("v", sv_)):
        un = _unpack_small(p, small_shapes)
        for n in _SMALL:
            if n != "conv_w":
                res[k][n] = un[n]
    g_conv_full = _unpack_small(sg_, small_shapes)["conv_w"][0]
    g_conv_mine = lax.dynamic_slice_in_dim(g_conv_full, dev * cw_cols, cw_cols, axis=1)
    cg_, cd_, cm_, cv_ = _adamw(g_conv_mine[None], conv_w[0], m_conv_w[0], v_conv_w[0], "adamw_conv_w")
    for k, a in (("g", cg_), ("d", cd_), ("m", cm_), ("v", cv_)):
        res[k]["conv_w"] = a[None]

    grad_x = grad_x2.reshape(nb, seq, D_MODEL)
    return (loss, grad_x, *[res["g"][n] for n in names], *[res["d"][n] for n in names],
            *[res["m"][n] for n in names], *[res["v"][n] for n in names])
```

```python
import functools
import math

import jax
import jax.numpy as jnp
from jax import lax
from jax.experimental import pallas as pl
from jax.experimental.pallas import tpu as pltpu

F32, BF16 = jnp.float32, jnp.bfloat16
MESH = pl.DeviceIdType.MESH
N_DEV = 8

D_MODEL = 1024
D_SSD = 1536
N_HEADS = 24
HEADDIM = 64
N_GROUPS = 4
HPG = 6
GW = HPG * HEADDIM
N_STATE = 128
CHUNK = 128
D_XBC = 2560
D_S5 = 512
S5_GROUPS = 32
S5_CH = 16
S5_P = 64
S5_N = S5_GROUPS * S5_P
D_IN = 4632
DT_PAD = 128
D_INP = D_SSD + D_XBC + DT_PAD + D_S5
D_FF = 4096
N_MOD = 6
ALPHA = 2.0 ** 0.25
EPS = 1e-5
LR, B1, B2, AEPS, WD, STEP = 0.001, 0.9, 0.999, 1e-08, 0.01, 10

NT = (((1,), (1,)), ((), ()))
TN = (((0,), (0,)), ((), ()))
ANY = pl.BlockSpec(memory_space=pl.ANY)
HIGHEST = lax.Precision.HIGHEST


def _mm(a, b):
    return jnp.dot(a.astype(BF16), b.astype(BF16), preferred_element_type=F32)


def _mm_nt(a, b):
    return lax.dot_general(a.astype(BF16), b.astype(BF16), NT, preferred_element_type=F32)


def _mm_tn(a, b):
    return lax.dot_general(a.astype(BF16), b.astype(BF16), TN, preferred_element_type=F32)


def _row_block(r, cap):
    best = r
    for cand in range(8, min(r, cap) + 1, 8):
        if r % cand == 0:
            best = cand
    return best if best <= cap else r


def _params(vmem_mb):
    return pltpu.CompilerParams(vmem_limit_bytes=vmem_mb << 20)


def _sigmoid(x):
    return 0.5 * (jnp.tanh(0.5 * x) + 1.0)


def _softplus(x):
    return jnp.maximum(x, 0.0) + jnp.log(1.0 + jnp.exp(-jnp.abs(x)))


_GK = math.sqrt(2.0 / math.pi)


def _gelu(x):
    return 0.5 * x * (1.0 + jnp.tanh(_GK * (x + 0.044715 * x * x * x)))


def _gelu_grad(x):
    t = jnp.tanh(_GK * (x + 0.044715 * x * x * x))
    return 0.5 * (1.0 + t) + 0.5 * x * (1.0 - t * t) * _GK * (1.0 + 3.0 * 0.044715 * x * x)


def _dev_index():
    return 4 * lax.axis_index("x") + 2 * lax.axis_index("y") + lax.axis_index("c")


def _all_gather(xs, name):
    n = len(xs)

    def body(*refs):
        x_refs, out_refs = refs[:n], refs[n:2 * n]
        send_sems, recv_sems, local_sems = refs[2 * n:]
        ix, iy, ic = lax.axis_index("x"), lax.axis_index("y"), lax.axis_index("c")
        me, sibling = (ix, iy, ic), (ix, iy, 1 - ic)
        chips = [(1 - ix, iy), (ix, 1 - iy), (1 - ix, 1 - iy)]

        def slot(a, px, py, pc):
            return out_refs[a].at[4 * px + 2 * py + pc]

        def copy(a, k, block, to, src=None):
            return pltpu.make_async_remote_copy(
                src_ref=slot(a, *block) if src is None else src, dst_ref=slot(a, *block),
                send_sem=send_sems.at[7 * a + k], recv_sem=recv_sems.at[7 * a + k], device_id=to, device_id_type=MESH)

        mine = [pltpu.make_async_copy(x_refs[a], slot(a, *me), local_sems.at[a]) for a in range(n)]
        for cp in mine:
            cp.start()
        first = []
        for j, chip in enumerate(chips):
            first += [copy(a, 1 + j, me, (*chip, ic), src=x_refs[a]) for a in range(n)]
        first += [copy(a, 0, me, sibling, src=x_refs[a]) for a in range(n)]
        for cp in first:
            cp.start()
        passed = []
        for j, chip in enumerate(chips):
            for a in range(n):
                copy(a, 1 + j, (*chip, ic), me).wait_recv()
                cp = copy(a, 4 + j, (*chip, ic), sibling)
                cp.start()
                passed.append(cp)
        for a in range(n):
            copy(a, 0, sibling, me).wait_recv()
            for j, chip in enumerate(chips):
                copy(a, 4 + j, (*chip, 1 - ic), me).wait_recv()
        for cp in first + passed:
            cp.wait_send()
        for cp in mine:
            cp.wait()

    return pl.pallas_call(
        body, name=name, out_shape=tuple(jax.ShapeDtypeStruct((N_DEV,) + x.shape, x.dtype) for x in xs),
        in_specs=[ANY] * n, out_specs=tuple([ANY] * n),
        scratch_shapes=[pltpu.SemaphoreType.DMA((7 * n,)), pltpu.SemaphoreType.DMA((7 * n,)),
                        pltpu.SemaphoreType.DMA((n,))],
    )(*xs)


HBM = pl.BlockSpec(memory_space=pltpu.HBM)
SEM = pl.BlockSpec(memory_space=pltpu.SEMAPHORE)
DATAFLOW = pltpu.SideEffectType.DATAFLOW_SIDE_EFFECTING


def _peer(k):
    ix, iy, ic = lax.axis_index("x"), lax.axis_index("y"), lax.axis_index("c")
    return (1 - ix if k & 4 else ix, 1 - iy if k & 2 else iy, 1 - ic if k & 1 else ic)


def _block_of(p):
    return 4 * p[0] + 2 * p[1] + p[2]


def _gather_start(xs, name):
    n = len(xs)
    lands = [lax.empty((N_DEV,) + x.shape, x.dtype) for x in xs]

    def body(*refs):
        x_refs, land_refs = refs[:n], refs[n:2 * n]
        send_sems, recv_sems = refs[2 * n], refs[2 * n + 1]
        token = refs[-1]
        me = _block_of(_peer(0))
        for a in range(n):
            for k in range(1, N_DEV):
                pltpu.make_async_remote_copy(
                    src_ref=x_refs[a], dst_ref=land_refs[a].at[me], send_sem=send_sems.at[7 * a + k - 1],
                    recv_sem=recv_sems.at[7 * a + k - 1], device_id=_peer(k), device_id_type=MESH).start()
        token[...] = jnp.zeros_like(token)

    outs = pl.pallas_call(
        body, name=name,
        out_shape=(pltpu.SemaphoreType.DMA((7 * n,)), pltpu.SemaphoreType.DMA((7 * n,)))
        + tuple(pltpu.HBM(x.shape, x.dtype) for x in xs) + tuple(pltpu.HBM(l.shape, l.dtype) for l in lands)
        + (jax.ShapeDtypeStruct((8, 128), F32),),
        in_specs=[HBM] * (2 * n), out_specs=(SEM, SEM) + (HBM,) * (2 * n) + (pl.BlockSpec(memory_space=pltpu.VMEM),),
        input_output_aliases={i: 2 + i for i in range(2 * n)},
        compiler_params=pltpu.CompilerParams(has_side_effects=DATAFLOW),
    )(*[pltpu.with_memory_space_constraint(x, pltpu.HBM) for x in xs],
      *[pltpu.with_memory_space_constraint(l, pltpu.HBM) for l in lands])
    return outs[0], outs[1], outs[2:2 + n], outs[2 + n:2 + 2 * n], outs[-1]


def _gather_wait(send_sems, recv_sems, xs_thru, lands_thru, after, name):
    n = len(xs_thru)

    def body(*refs):
        x_refs, land_refs = refs[:n], refs[n:2 * n]
        send_sems, recv_sems = refs[2 * n], refs[2 * n + 1]
        for a in range(n):
            for k in range(1, N_DEV):
                cp = pltpu.make_async_remote_copy(
                    src_ref=x_refs[a], dst_ref=land_refs[a].at[_block_of(_peer(k))], send_sem=send_sems.at[7 * a + k - 1],
                    recv_sem=recv_sems.at[7 * a + k - 1], device_id=_peer(k), device_id_type=MESH)
                cp.wait_send()
                cp.wait_recv()

    outs = pl.pallas_call(
        body, name=name,
        out_shape=tuple(pltpu.HBM(x.shape, x.dtype) for x in xs_thru)
        + tuple(pltpu.HBM(l.shape, l.dtype) for l in lands_thru),
        in_specs=[HBM] * (2 * n) + [SEM, SEM, ANY], out_specs=(HBM,) * (2 * n),
        input_output_aliases={i: i for i in range(2 * n)},
        compiler_params=pltpu.CompilerParams(has_side_effects=DATAFLOW),
    )(*xs_thru, *lands_thru, send_sems, recv_sems, after)
    return outs[:n], outs[n:]


def _chip_peer(k):
    ix, iy = lax.axis_index("x"), lax.axis_index("y")
    return (1 - ix if k & 2 else ix, 1 - iy if k & 1 else iy)


def _all_to_all_start(hs, name):
    n = len(hs)
    lands = [lax.empty(h.shape, h.dtype) for h in hs]

    def body(*refs):
        h_refs, land_refs = refs[:n], refs[n:2 * n]
        send_sems, recv_sems = refs[2 * n], refs[2 * n + 1]
        token = refs[-1]
        ic = lax.axis_index("c")
        mx, my = _chip_peer(0)
        for a in range(n):
            for k in range(1, 4):
                px, py = _chip_peer(k)
                pltpu.make_async_remote_copy(
                    src_ref=h_refs[a].at[2 * px + py], dst_ref=land_refs[a].at[2 * mx + my],
                    send_sem=send_sems.at[3 * a + k - 1], recv_sem=recv_sems.at[3 * a + k - 1],
                    device_id=(px, py, ic), device_id_type=MESH).start()
        token[...] = jnp.zeros_like(token)

    outs = pl.pallas_call(
        body, name=name,
        out_shape=(pltpu.SemaphoreType.DMA((3 * n,)), pltpu.SemaphoreType.DMA((3 * n,)))
        + tuple(pltpu.HBM(h.shape, h.dtype) for h in hs) + tuple(pltpu.HBM(l.shape, l.dtype) for l in lands)
        + (jax.ShapeDtypeStruct((8, 128), F32),),
        in_specs=[HBM] * (2 * n), out_specs=(SEM, SEM) + (HBM,) * (2 * n) + (pl.BlockSpec(memory_space=pltpu.VMEM),),
        input_output_aliases={i: 2 + i for i in range(2 * n)},
        compiler_params=pltpu.CompilerParams(has_side_effects=DATAFLOW),
    )(*[pltpu.with_memory_space_constraint(h, pltpu.HBM) for h in hs],
      *[pltpu.with_memory_space_constraint(l, pltpu.HBM) for l in lands])
    return outs[0], outs[1], outs[2:2 + n], outs[2 + n:2 + 2 * n], outs[-1]


def _all_to_all_wait(send_sems, recv_sems, hs_thru, lands_thru, after, name):
    n = len(hs_thru)

    def body(*refs):
        h_refs, land_refs = refs[:n], refs[n:2 * n]
        send_sems, recv_sems = refs[2 * n], refs[2 * n + 1]
        ic = lax.axis_index("c")
        for a in range(n):
            for k in range(1, 4):
                px, py = _chip_peer(k)
                cp = pltpu.make_async_remote_copy(
                    src_ref=h_refs[a].at[2 * px + py], dst_ref=land_refs[a].at[2 * px + py],
                    send_sem=send_sems.at[3 * a + k - 1], recv_sem=recv_sems.at[3 * a + k - 1],
                    device_id=(px, py, ic), device_id_type=MESH)
                cp.wait_send()
                cp.wait_recv()

    outs = pl.pallas_call(
        body, name=name,
        out_shape=tuple(pltpu.HBM(h.shape, h.dtype) for h in hs_thru)
        + tuple(pltpu.HBM(l.shape, l.dtype) for l in lands_thru),
        in_specs=[HBM] * (2 * n) + [SEM, SEM, ANY], out_specs=(HBM,) * (2 * n),
        input_output_aliases={i: i for i in range(2 * n)},
        compiler_params=pltpu.CompilerParams(has_side_effects=DATAFLOW),
    )(*hs_thru, *lands_thru, send_sems, recv_sems, after)
    return outs[:n], outs[n:]


def _sibling_block(g_ref, q):
    ic = lax.axis_index("c")
    if len(g_ref.shape) == 4:
        return g_ref.at[q, 1 - ic]
    cw = g_ref.shape[1] // N_DEV
    return g_ref.at[:, pl.ds(pl.multiple_of((2 * q + 1 - ic) * cw, 128), cw)]


def _sibling_swap_start(gs, name):
    n = len(gs)
    lands = [lax.empty((4,) + (g.shape[2:] if g.ndim == 4 else (g.shape[0], g.shape[1] // N_DEV)), g.dtype) for g in gs]

    def body(*refs):
        g_refs, land_refs = refs[:n], refs[n:2 * n]
        send_sems, recv_sems = refs[2 * n], refs[2 * n + 1]
        token = refs[-1]
        for a in range(n):
            for q in range(4):
                pltpu.make_async_remote_copy(
                    src_ref=_sibling_block(g_refs[a], q), dst_ref=land_refs[a].at[q],
                    send_sem=send_sems.at[4 * a + q], recv_sem=recv_sems.at[4 * a + q],
                    device_id=_peer(1), device_id_type=MESH).start()
        token[...] = jnp.zeros_like(token)

    outs = pl.pallas_call(
        body, name=name,
        out_shape=(pltpu.SemaphoreType.DMA((4 * n,)), pltpu.SemaphoreType.DMA((4 * n,)))
        + tuple(pltpu.HBM(g.shape, g.dtype) for g in gs) + tuple(pltpu.HBM(l.shape, l.dtype) for l in lands)
        + (jax.ShapeDtypeStruct((8, 128), F32),),
        in_specs=[HBM] * (2 * n), out_specs=(SEM, SEM) + (HBM,) * (2 * n) + (pl.BlockSpec(memory_space=pltpu.VMEM),),
        input_output_aliases={i: 2 + i for i in range(2 * n)},
        compiler_params=pltpu.CompilerParams(has_side_effects=DATAFLOW),
    )(*[pltpu.with_memory_space_constraint(g, pltpu.HBM) for g in gs],
      *[pltpu.with_memory_space_constraint(l, pltpu.HBM) for l in lands])
    return outs[0], outs[1], outs[2:2 + n], outs[2 + n:2 + 2 * n], outs[-1]


def _sibling_swap_wait(send_sems, recv_sems, gs_thru, lands_thru, after, name):
    n = len(gs_thru)

    def body(*refs):
        g_refs, land_refs = refs[:n], refs[n:2 * n]
        send_sems, recv_sems = refs[2 * n], refs[2 * n + 1]
        for a in range(n):
            for q in range(4):
                cp = pltpu.make_async_remote_copy(
                    src_ref=_sibling_block(g_refs[a], q), dst_ref=land_refs[a].at[q],
                    send_sem=send_sems.at[4 * a + q], recv_sem=recv_sems.at[4 * a + q],
                    device_id=_peer(1), device_id_type=MESH)
                cp.wait_send()
                cp.wait_recv()

    outs = pl.pallas_call(
        body, name=name,
        out_shape=tuple(pltpu.HBM(g.shape, g.dtype) for g in gs_thru)
        + tuple(pltpu.HBM(l.shape, l.dtype) for l in lands_thru),
        in_specs=[HBM] * (2 * n) + [SEM, SEM, ANY], out_specs=(HBM,) * (2 * n),
        input_output_aliases={i: i for i in range(2 * n)},
        compiler_params=pltpu.CompilerParams(has_side_effects=DATAFLOW),
    )(*gs_thru, *lands_thru, send_sems, recv_sems, after)
    return outs[:n], outs[n:]


def _sibling_swap(gs, name):
    n = len(gs)

    def body(*refs):
        g_refs, recv_refs = refs[:n], refs[n:2 * n]
        send_sems, recv_sems = refs[2 * n:]
        ix, iy, ic = lax.axis_index("x"), lax.axis_index("y"), lax.axis_index("c")
        cps = []
        for a in range(n):
            for q in range(4):
                cps.append(pltpu.make_async_remote_copy(
                    src_ref=_sibling_block(g_refs[a], q), dst_ref=recv_refs[a].at[q],
                    send_sem=send_sems.at[4 * a + q], recv_sem=recv_sems.at[4 * a + q],
                    device_id=(ix, iy, 1 - ic), device_id_type=MESH))
        for cp in cps:
            cp.start()
        for cp in cps:
            cp.wait()

    return pl.pallas_call(
        body, name=name,
        out_shape=tuple(jax.ShapeDtypeStruct(
            (4,) + (g.shape[2:] if g.ndim == 4 else (g.shape[0], g.shape[1] // N_DEV)), g.dtype) for g in gs),
        in_specs=[ANY] * n, out_specs=tuple([ANY] * n),
        scratch_shapes=[pltpu.SemaphoreType.DMA((4 * n,)), pltpu.SemaphoreType.DMA((4 * n,))],
    )(*gs)


def _add_halves(g, recv, core, name):
    _, r, c = recv.shape
    br = _row_block(r, 512)
    stacked = g.ndim == 4

    def body(core_ref, g_ref, r_ref, o_ref):
        o_ref[0] = ((g_ref[0, 0] if stacked else g_ref[...]) + r_ref[0]).astype(BF16)

    spec = pl.BlockSpec((1, br, c), lambda i, j, core_ref: (i, j, 0))
    if stacked:
        g_spec = pl.BlockSpec((1, 1, br, c), lambda i, j, core_ref: (i, core_ref[0], j, 0))
    else:
        g_spec = pl.BlockSpec((br, c), lambda i, j, core_ref: (j, 2 * i + core_ref[0]))
    return pl.pallas_call(
        body, name=name, out_shape=jax.ShapeDtypeStruct(recv.shape, BF16),
        grid_spec=pltpu.PrefetchScalarGridSpec(
            num_scalar_prefetch=1, grid=(4, r // br), in_specs=[g_spec, spec], out_specs=spec),
        compiler_params=_params(32),
    )(core, g, recv)


def _adamw(parts, w, m, v, name):
    n_parts, r, c = parts.shape
    if r % 8 == 0:
        br, bc = _row_block(r, 512 if c <= 1024 else 256), c
    else:
        br, bc = r, (256 if c % 256 == 0 else c)

    def body(p_ref, w_ref, m_ref, v_ref, g_out, d_out, m_out, v_out):
        g = p_ref[0].astype(F32)
        for p in range(1, n_parts):
            g = g + p_ref[p].astype(F32)
        m2 = B1 * m_ref[...] + (1.0 - B1) * g
        v2 = B2 * v_ref[...] + (1.0 - B2) * (g * g)
        m_hat = m2 / (1.0 - B1 ** STEP)
        v_hat = v2 / (1.0 - B2 ** STEP)
        g_out[...] = g
        d_out[...] = -LR * (m_hat / (jnp.sqrt(v_hat) + AEPS) + WD * w_ref[...])
        m_out[...] = m2
        v_out[...] = v2

    spec = pl.BlockSpec((br, bc), lambda i, j: (i, j))
    out = jax.ShapeDtypeStruct((r, c), F32)
    return pl.pallas_call(
        body, name=name, out_shape=(out, out, out, out), grid=(r // br, c // bc),
        in_specs=[pl.BlockSpec((n_parts, br, bc), lambda i, j: (0, i, j)), spec, spec, spec],
        out_specs=(spec, spec, spec, spec), compiler_params=_params(40),
    )(parts, w, m, v)


def _atb(a, b, name):
    t, k1 = a.shape
    k2 = b.shape[1]
    bt = math.gcd(t, 2048)

    def pick(k):
        for cand in (1024, 768, 512, 384, 256, 128):
            if k % cand == 0:
                return cand
        return k

    b1, b2 = pick(k1), pick(k2)

    def body(a_ref, b_ref, o_ref):
        @pl.when(pl.program_id(2) == 0)
        def _():
            o_ref[...] = jnp.zeros_like(o_ref)
        o_ref[...] += _mm_tn(a_ref[...], b_ref[...])

    return pl.pallas_call(
        body, name=name, out_shape=jax.ShapeDtypeStruct((k1, k2), F32), grid=(k1 // b1, k2 // b2, t // bt),
        in_specs=[pl.BlockSpec((bt, b1), lambda i, j, k: (k, i)), pl.BlockSpec((bt, b2), lambda i, j, k: (k, j))],
        out_specs=pl.BlockSpec((b1, b2), lambda i, j, k: (i, j)), compiler_params=_params(48),
    )(a, b)


def _mod_fwd(c_all, w_ada, b_cols):
    def body(c_ref, w_ref, b_ref, o_ref):
        cc = c_ref[...]
        cond = cc * _sigmoid(cc)
        o_ref[...] = _mm(cond, w_ref[...]) + b_ref[...]

    return pl.pallas_call(body, name="mod_fwd", out_shape=jax.ShapeDtypeStruct((c_all.shape[0], w_ada.shape[1]), F32),
                          compiler_params=_params(32))(c_all, w_ada, b_cols)


def _mod_bwd(c_all, dmod_cols, dmod_all):
    def body(c_ref, dc_ref, da_ref, gw_ref, gb_ref):
        cc = c_ref[...]
        cond = cc * _sigmoid(cc)
        gw_ref[...] = _mm_tn(cond, dc_ref[...])
        gb_ref[...] = jnp.sum(da_ref[...], axis=0, keepdims=True)

    return pl.pallas_call(
        body, name="mod_bwd",
        out_shape=(jax.ShapeDtypeStruct((D_MODEL, dmod_cols.shape[1]), F32), jax.ShapeDtypeStruct((1, dmod_all.shape[1]), F32)),
        compiler_params=_params(32))(c_all, dmod_cols, dmod_all)


def _load_once(hbm_ref, vmem_ref, sem):
    @pl.when(pl.program_id(0) == 0)
    def _():
        cp = pltpu.make_async_copy(hbm_ref, vmem_ref, sem)
        cp.start()
        cp.wait()


def _conv_taps(win_ref, w, tb, cols):
    shifted = [win_ref[8 - j:8 - j + tb, cols] for j in range(4)]
    acc = w[3:4] * shifted[0]
    for j in (1, 2, 3):
        acc = acc + w[3 - j:4 - j] * shifted[j]
    return acc, shifted


def _proj_conv_fwd(x2, mod3, w_in_pad, conv_w, conv_b, seq):
    t = x2.shape[0]
    tb = 256
    npb = seq // tb
    cw = 512

    def body(x_ref, mod_ref, w_hbm, cw_ref, cb_ref, z_ref, pre_ref, xbc_ref, dsilu_ref, dt_ref, u5_ref, w_vmem, win, sem):
        _load_once(w_hbm, w_vmem, sem)
        first = (pl.program_id(0) % npb) == 0

        @pl.when(first)
        def _():
            win[0:8, :] = jnp.zeros((8, D_XBC), F32)

        @pl.when(jnp.logical_not(first))
        def _():
            win[0:8, :] = win[tb:tb + 8, :]

        m = mod_ref[0]
        u = (x_ref[...] * (1.0 + m[1:2]) + m[0:1]).astype(BF16)
        z_ref[...] = lax.dot_general(u, w_vmem[0:D_SSD, :], NT, preferred_element_type=F32)
        dt_ref[...] = lax.dot_general(u, w_vmem[D_SSD + D_XBC:D_SSD + D_XBC + DT_PAD, :], NT,
                                      preferred_element_type=F32)
        u5_ref[...] = lax.dot_general(u, w_vmem[D_SSD + D_XBC + DT_PAD:, :], NT, preferred_element_type=F32)
        for k in range(D_XBC // cw):
            cols = slice(k * cw, (k + 1) * cw)
            pre_k = lax.dot_general(u, w_vmem[D_SSD + k * cw:D_SSD + (k + 1) * cw, :], NT,
                                    preferred_element_type=F32)
            win[8:8 + tb, cols] = pre_k
            pre_ref[:, cols] = pre_k
            conv, _ = _conv_taps(win, cw_ref[:, cols], tb, cols)
            conv = conv + cb_ref[:, cols]
            sg = _sigmoid(conv)
            xbc_ref[:, cols] = conv * sg
            dsilu_ref[:, cols] = sg * (1.0 + conv * (1.0 - sg))

    row = lambda w: pl.BlockSpec((tb, w), lambda i: (i, 0))
    return pl.pallas_call(
        body, name="proj_conv_fwd", grid=(t // tb,),
        out_shape=(jax.ShapeDtypeStruct((t, D_SSD), F32), jax.ShapeDtypeStruct((t, D_XBC), F32),
                   jax.ShapeDtypeStruct((t, D_XBC), F32), jax.ShapeDtypeStruct((t, D_XBC), F32),
                   jax.ShapeDtypeStruct((t, DT_PAD), F32), jax.ShapeDtypeStruct((t, D_S5), F32)),
        in_specs=[row(D_MODEL), pl.BlockSpec((1, N_MOD, D_MODEL), lambda i: (i // npb, 0, 0)), ANY,
                  pl.BlockSpec((4, D_XBC), lambda i: (0, 0)), pl.BlockSpec((1, D_XBC), lambda i: (0, 0))],
        out_specs=(row(D_SSD), row(D_XBC), row(D_XBC), row(D_XBC), row(DT_PAD), row(D_S5)),
        scratch_shapes=[pltpu.VMEM((D_INP, D_MODEL), BF16), pltpu.VMEM((tb + 8, D_XBC), F32), pltpu.SemaphoreType.DMA],
        compiler_params=_params(56),
    )(x2, mod3, w_in_pad, conv_w, conv_b)


N_PAIRS = N_HEADS // 2


def _split3(x):
    hi = x.astype(BF16)
    r = x - hi.astype(F32)
    mid = r.astype(BF16)
    lo = (r - mid.astype(F32)).astype(BF16)
    return hi, mid, lo


def _dot3(x, e, dims=(((1,), (0,)), ((), ()))):
    return sum(lax.dot_general(p, e, dims, preferred_element_type=F32) for p in _split3(x))


def _dot3_left(e, x, dims=(((1,), (0,)), ((), ()))):
    return sum(lax.dot_general(e, p, dims, preferred_element_type=F32) for p in _split3(x))


def _head_fold():
    return (jnp.arange(D_SSD)[:, None] // HEADDIM == jnp.arange(128)[None, :]).astype(BF16)


def _ssd_prep(dt_raw, par):
    dtb = par[0:1]
    a = -jnp.exp(par[1:2])
    dt = _softplus(dt_raw + dtb)
    adt = dt * a
    row = lax.broadcasted_iota(jnp.int32, (CHUNK, CHUNK), 0)
    col = lax.broadcasted_iota(jnp.int32, (CHUNK, CHUNK), 1)
    causal = row >= col
    tri = causal.astype(BF16)
    cs = _dot3_left(tri, adt)
    left = col < HEADDIM

    def lanes(v, h):
        return jnp.broadcast_to(v[:, h:h + 1], (CHUNK, 128))

    dt_c, cs_c, pair_cols = [], [], []
    for p in range(N_PAIRS):
        c0, c1 = lanes(cs, 2 * p), lanes(cs, 2 * p + 1)
        pair_cols.append(jnp.concatenate([c0, c1], axis=1))
        cs_c.append(jnp.where(left, c0, c1))
        dt_c.append(jnp.where(left, lanes(dt, 2 * p), lanes(dt, 2 * p + 1)))
    cs_c = jnp.concatenate(cs_c, axis=1)
    dt_c = jnp.concatenate(dt_c, axis=1)
    return dt, a, cs, cs.T, causal, tri, dt_c, jnp.exp(cs_c), jnp.exp(cs_c[CHUNK - 1:CHUNK, :] - cs_c), pair_cols


def _pair_decay(cols, cst, pair, causal2):
    rows = jnp.concatenate([jnp.broadcast_to(cst[2 * pair:2 * pair + 1, :], (CHUNK, CHUNK)),
                            jnp.broadcast_to(cst[2 * pair + 1:2 * pair + 2, :], (CHUNK, CHUNK))], axis=1)
    return jnp.exp(jnp.where(causal2, cols - rows, -jnp.inf))


def _stack_heads(xp, left):
    return jnp.concatenate([jnp.where(left, xp, 0.0), jnp.where(left, 0.0, xp)], axis=0).astype(BF16)


def _ssd_fwd(xbc, z, dt_raw, par, dsk, normw, seq):
    t = xbc.shape[0]
    nc = seq // CHUNK
    n_chunks = t // CHUNK

    def body(xbc_ref, z_ref, dt_ref, par_ref, dsk_ref, nw_ref, yraw_ref, ycat_ref, hprev_ref, h_ref):
        @pl.when(pl.program_id(0) % nc == 0)
        def _():
            h_ref[...] = jnp.zeros_like(h_ref)
        hprev_ref[0] = h_ref[...]
        _, _, cs, cst, causal, _, dt_c, ecs_c, w_c, pair_cols = _ssd_prep(dt_ref[...], par_ref[...])
        cs_last = cs[CHUNK - 1:CHUNK, :]
        causal2 = jnp.concatenate([causal, causal], axis=1)
        left = lax.broadcasted_iota(jnp.int32, (CHUNK, 128), 1) < HEADDIM
        x = xbc_ref[:, 0:D_SSD]
        xdt = x * dt_c
        amat = (w_c * xdt).astype(BF16)
        zz = z_ref[...]
        silu_z = zz * _sigmoid(zz)
        for g in range(N_GROUPS):
            gs = slice(g * GW, (g + 1) * GW)
            bg = xbc_ref[:, D_SSD + g * N_STATE:D_SSD + (g + 1) * N_STATE].astype(BF16)
            cg = xbc_ref[:, D_SSD + (N_GROUPS + g) * N_STATE:D_SSD + (N_GROUPS + g + 1) * N_STATE].astype(BF16)
            scores = lax.dot_general(cg, bg, NT, preferred_element_type=F32)
            scores2 = jnp.concatenate([scores, scores], axis=1)
            hg = h_ref[gs, :]
            p_all = lax.dot_general(cg, hg.astype(BF16), NT, preferred_element_type=F32)
            ys = []
            for q in range(GW // 128):
                pair = g * (GW // 128) + q
                decay = _pair_decay(pair_cols[pair], cst, pair, causal2)
                mcat = (scores2 * decay).astype(BF16)
                ys.append(jnp.dot(mcat, _stack_heads(xdt[:, pair * 128:(pair + 1) * 128], left),
                                  preferred_element_type=F32))
            yg = jnp.concatenate(ys, axis=1) + ecs_c[:, gs] * p_all + x[:, gs] * dsk_ref[:, gs]
            s_new = lax.dot_general(amat[:, gs], bg, TN, preferred_element_type=F32)
            for j in range(HPG):
                hh = g * HPG + j
                js = slice(j * HEADDIM, (j + 1) * HEADDIM)
                h_ref[g * GW + j * HEADDIM:g * GW + (j + 1) * HEADDIM, :] = (
                    hg[js, :] * jnp.exp(cs_last[:, hh:hh + 1]) + s_new[js, :])
            yraw_ref[:, gs] = yg
            v = yg * silu_z[:, gs]
            r = lax.rsqrt(jnp.mean(v * v, axis=-1, keepdims=True) + EPS)
            ycat_ref[:, gs] = (v * r * nw_ref[:, gs]).astype(BF16)

    row = lambda w: pl.BlockSpec((CHUNK, w), lambda i: (i, 0))
    full = lambda s: pl.BlockSpec(s, lambda i: (0,) * len(s))
    return pl.pallas_call(
        body, name="ssd_fwd", grid=(n_chunks,),
        out_shape=(jax.ShapeDtypeStruct((t, D_SSD), F32), jax.ShapeDtypeStruct((t, D_SSD + D_S5), BF16),
                   jax.ShapeDtypeStruct((n_chunks, D_SSD, N_STATE), F32)),
        in_specs=[row(D_XBC), row(D_SSD), row(DT_PAD), full((8, 128)), full((1, D_SSD)), full((1, D_SSD))],
        out_specs=(row(D_SSD), row(D_SSD), pl.BlockSpec((1, D_SSD, N_STATE), lambda i: (i, 0, 0))),
        scratch_shapes=[pltpu.VMEM((D_SSD, N_STATE), F32)],
        compiler_params=_params(40),
    )(xbc, z, dt_raw, par, dsk, normw)


S5_CW = 512
S5_BLOCKS = 4


def _tile_scan(in_re, in_im, out_re, out_im, carry_re, carry_im, pw_re, pw_im, n_tiles, reverse):
    steps = (1, 2, 4)
    for cc in range(S5_N // S5_CW):
        cols = slice(cc * S5_CW, (cc + 1) * S5_CW)
        a_re, a_im = pw_re[:, cols], pw_im[:, cols]
        rid = lax.broadcasted_iota(jnp.int32, (8, S5_CW), 0)
        pows = []
        for d in steps:
            k = 8 - d if reverse else d - 1
            keep = (rid < 8 - d) if reverse else (rid >= d)
            pows.append((jnp.where(keep, pw_re[k:k + 1, cols], 0.0), jnp.where(keep, pw_im[k:k + 1, cols], 0.0)))

        def tile(i, carry, cols=cols, pows=pows, a_re=a_re, a_im=a_im):
            r = (n_tiles - 1 - i) if reverse else i
            rows = pl.ds(pl.multiple_of(r * 8, 8), 8)
            xr, xi = in_re[rows, cols], in_im[rows, cols]
            for (pr, pi), d in zip(pows, steps):
                shift = 8 - d if reverse else d
                sr, si = pltpu.roll(xr, shift, axis=0), pltpu.roll(xi, shift, axis=0)
                xr, xi = xr + pr * sr - pi * si, xi + pr * si + pi * sr
            cr, ci = carry
            xr, xi = xr + a_re * cr - a_im * ci, xi + a_re * ci + a_im * cr
            out_re[rows, cols] = xr
            out_im[rows, cols] = xi
            edge = slice(0, 1) if reverse else slice(7, 8)
            return (jnp.broadcast_to(xr[edge], (8, S5_CW)), jnp.broadcast_to(xi[edge], (8, S5_CW)))

        c0 = (jnp.broadcast_to(carry_re[0:1, cols], (8, S5_CW)), jnp.broadcast_to(carry_im[0:1, cols], (8, S5_CW)))
        cr, ci = lax.fori_loop(0, n_tiles, tile, c0, unroll=True)
        carry_re[:, cols] = cr
        carry_im[:, cols] = ci


def _s5_params_math(ar, ai, ldt, br, bi):
    dt = jnp.exp(ldt)
    mag = jnp.exp(ar * dt)
    ang = ai * dt
    ab_re = mag * jnp.cos(ang)
    ab_im = mag * jnp.sin(ang)
    den = ar * ar + ai * ai
    n_re = ab_re - 1.0
    coef_re = (n_re * ar + ab_im * ai) / den
    coef_im = (ab_im * ar - n_re * ai) / den
    bb_re = coef_re * br - coef_im * bi
    bb_im = coef_re * bi + coef_im * br
    return ab_re, ab_im, bb_re, bb_im


def _s5_params_fwd(ar, ai, ldt, br, bi):
    def body(ar_ref, ai_ref, ldt_ref, br_ref, bi_ref, bbr_ref, bbi_ref, pfr_ref, pfi_ref, prr_ref, pri_ref):
        ab_re, ab_im, bb_re, bb_im = _s5_params_math(ar_ref[...], ai_ref[...], ldt_ref[...], br_ref[...], bi_ref[...])
        bbr_ref[...] = bb_re
        bbi_ref[...] = bb_im
        pr, pi = ab_re, ab_im
        for k in range(8):
            pfr_ref[k:k + 1, :] = pr
            pfi_ref[k:k + 1, :] = pi
            prr_ref[7 - k:8 - k, :] = pr
            pri_ref[7 - k:8 - k, :] = -pi
            pr, pi = pr * ab_re - pi * ab_im, pr * ab_im + pi * ab_re

    b16 = jax.ShapeDtypeStruct((S5_CH, S5_N), F32)
    p8 = jax.ShapeDtypeStruct((8, S5_N), F32)
    return pl.pallas_call(body, name="s5_params_fwd", out_shape=(b16, b16, p8, p8, p8, p8),
                          compiler_params=_params(32))(ar, ai, ldt, br, bi)


def _s5_params_bwd(ar, ai, ldt, br, bi, d_ab_re, d_ab_im, d_bb_re, d_bb_im):
    def body(ar_ref, ai_ref, ldt_ref, br_ref, bi_ref, dar_ref, dai_ref, dbr_ref, dbi_ref,
             gar_ref, gai_ref, gldt_ref, gbr_ref, gbi_ref):
        _, vjp = jax.vjp(_s5_params_math, ar_ref[...], ai_ref[...], ldt_ref[...], br_ref[...], bi_ref[...])
        g_ar, g_ai, g_ldt, g_br, g_bi = vjp((dar_ref[...], dai_ref[...], dbr_ref[...], dbi_ref[...]))
        gar_ref[...] = g_ar
        gai_ref[...] = g_ai
        gbr_ref[...] = g_br
        gbi_ref[...] = g_bi
        lane = lax.broadcasted_iota(jnp.int32, (S5_N, 128), 0) // S5_P
        grp = lax.broadcasted_iota(jnp.int32, (S5_N, 128), 1)
        fold = (lane == grp).astype(F32)
        gldt_ref[...] = jnp.dot(g_ldt, fold, preferred_element_type=F32, precision=HIGHEST)

    v1 = jax.ShapeDtypeStruct((1, S5_N), F32)
    b16 = jax.ShapeDtypeStruct((S5_CH, S5_N), F32)
    return pl.pallas_call(body, name="s5_params_bwd",
                          out_shape=(v1, v1, jax.ShapeDtypeStruct((1, 128), F32), b16, b16),
                          compiler_params=_params(32))(ar, ai, ldt, br, bi, d_ab_re, d_ab_im, d_bb_re, d_bb_im)


def _s5_fwd(u5, bb_re, bb_im, cc_re, cc_im, pf_re, pf_im, s5d, w_glu, b_glu, ycat, seq):
    t = u5.shape[0]
    tb = 256
    npb = seq // tb

    def body(u_ref, bbr_ref, bbi_ref, ccr_ref, cci_ref, pfr_ref, pfi_ref, d_ref, wg_ref, bg_ref, ycat_hbm,
             sre_ref, sim_ref, ypre_ref, y5_ref, bur, bui, car, cai):
        del ycat_hbm

        @pl.when(pl.program_id(0) % npb == 0)
        def _():
            car[...] = jnp.zeros_like(car)
            cai[...] = jnp.zeros_like(cai)
        u = u_ref[...]
        ub = u.astype(BF16)
        for j in range(S5_BLOCKS):
            ch, st = slice(j * 128, (j + 1) * 128), slice(j * 512, (j + 1) * 512)
            bur[:, st] = jnp.dot(ub[:, ch], bbr_ref[j], preferred_element_type=F32)
            bui[:, st] = jnp.dot(ub[:, ch], bbi_ref[j], preferred_element_type=F32)
        _tile_scan(bur, bui, sre_ref, sim_ref, car, cai, pfr_ref, pfi_ref, tb // 8, reverse=False)
        cs_y = []
        for j in range(S5_BLOCKS):
            st = slice(j * 512, (j + 1) * 512)
            cs_y.append(_mm(sre_ref[:, st], ccr_ref[j]) - _mm(sim_ref[:, st], cci_ref[j]))
        ypre = jnp.concatenate(cs_y, axis=1) + u * d_ref[...]
        ypre_ref[...] = ypre
        yg = _gelu(ypre)
        y5_ref[...] = (yg * _sigmoid(_mm(yg, wg_ref[...]) + bg_ref[...])).astype(BF16)

    row = lambda w: pl.BlockSpec((tb, w), lambda i: (i, 0))
    full = lambda a: pl.BlockSpec(a.shape, lambda i: (0,) * a.ndim)
    return pl.pallas_call(
        body, name="s5_fwd", grid=(t // tb,),
        out_shape=(jax.ShapeDtypeStruct((t, S5_N), F32), jax.ShapeDtypeStruct((t, S5_N), F32),
                   jax.ShapeDtypeStruct((t, D_S5), F32), jax.ShapeDtypeStruct(ycat.shape, BF16)),
        in_specs=[row(D_S5), full(bb_re), full(bb_im), full(cc_re), full(cc_im), full(pf_re), full(pf_im),
                  full(s5d), full(w_glu), full(b_glu), ANY],
        out_specs=(row(S5_N), row(S5_N), row(D_S5), pl.BlockSpec((tb, D_S5), lambda i: (i, D_SSD // D_S5))),
        input_output_aliases={10: 3},
        scratch_shapes=[pltpu.VMEM((tb, S5_N), F32), pltpu.VMEM((tb, S5_N), F32),
                        pltpu.VMEM((8, S5_N), F32), pltpu.VMEM((8, S5_N), F32)],
        compiler_params=_params(48),
    )(u5, bb_re, bb_im, cc_re, cc_im, pf_re, pf_im, s5d, w_glu, b_glu, ycat)


def _layer_norm(r, g, b):
    mu = jnp.mean(r, axis=-1, keepdims=True)
    xc = r - mu
    rstd = lax.rsqrt(jnp.mean(xc * xc, axis=-1, keepdims=True) + EPS)
    xhat = xc * rstd
    return xhat * g + b, xhat, rstd


def _layer_norm_bwd(dy, xhat, rstd, g):
    dxhat = dy * g
    return rstd * (dxhat - jnp.mean(dxhat, axis=-1, keepdims=True)
                   - xhat * jnp.mean(dxhat * xhat, axis=-1, keepdims=True))


def _out_ln1(ycat, x2, mod3, w_out, ln1, seq):
    t = x2.shape[0]
    tb = 512
    npb = seq // tb

    def body(y_ref, x_ref, mod_ref, w_ref, ln_ref, mix_ref, x1_ref):
        m = mod_ref[0]
        mix = jnp.dot(y_ref[...], w_ref[...], preferred_element_type=F32)
        mix_ref[...] = mix
        r1 = ALPHA * x_ref[...] + (1.0 + m[2:3]) * mix
        x1_ref[...] = _layer_norm(r1, ln_ref[0:1], ln_ref[1:2])[0]

    row = lambda w: pl.BlockSpec((tb, w), lambda i: (i, 0))
    return pl.pallas_call(
        body, name="out_ln1", grid=(t // tb,),
        out_shape=(jax.ShapeDtypeStruct((t, D_MODEL), F32), jax.ShapeDtypeStruct((t, D_MODEL), F32)),
        in_specs=[row(D_SSD + D_S5), row(D_MODEL), pl.BlockSpec((1, N_MOD, D_MODEL), lambda i: (i // npb, 0, 0)),
                  pl.BlockSpec(w_out.shape, lambda i: (0, 0)), pl.BlockSpec(ln1.shape, lambda i: (0, 0))],
        out_specs=(row(D_MODEL), row(D_MODEL)), compiler_params=_params(48),
    )(ycat, x2, mod3, w_out, ln1)


def _mlp_fwd_bwd(x1, tgt, mod3, w1, w2, vec1, b1, seq):
    t = x1.shape[0]
    tb = 256
    npb = seq // tb
    n_fb, _, fb = w1.shape

    def body(x1_ref, tgt_ref, mod_ref, w1_hbm, w2_hbm, v_ref, b1_ref,
             dx1_ref, u2_ref, h_ref, dhp_ref, do_ref, gacc_ref, db1_ref, bacc_ref, w1_v, w2_v, sem1, sem2):
        i = pl.program_id(0)
        @pl.when(i == 0)
        def _():
            cps = [pltpu.make_async_copy(w1_hbm.at[k], w1_v.at[:, k * fb:(k + 1) * fb], sem1.at[k])
                   for k in range(n_fb)]
            for cp in cps:
                cp.start()
            for cp in cps:
                cp.wait()
        _load_once(w2_hbm, w2_v, sem2)

        @pl.when(i == 0)
        def _():
            gacc_ref[...] = jnp.zeros_like(gacc_ref)
            db1_ref[...] = jnp.zeros_like(db1_ref)

        @pl.when(i % npb == 0)
        def _():
            bacc_ref[...] = jnp.zeros_like(bacc_ref)

        m = mod_ref[0]
        sh2, sc2, g2 = m[3:4], m[4:5], m[5:6]
        x1v = x1_ref[...]
        u2 = (x1v * (1.0 + sc2) + sh2).astype(BF16)
        u2_ref[...] = u2
        hr = jnp.maximum(jnp.dot(u2, w1_v[...], preferred_element_type=F32) + b1_ref[...], 0.0)
        hb = (hr * hr).astype(BF16)
        h_ref[...] = hb
        o = jnp.dot(hb, w2_v[...], preferred_element_type=F32) + v_ref[0:1]
        r2 = ALPHA * x1v + (1.0 + g2) * o
        y, xhat, rstd = _layer_norm(r2, v_ref[1:2], v_ref[2:3])
        err = y - tgt_ref[...]
        dy = err * (1.0 / D_MODEL)
        dr2 = _layer_norm_bwd(dy, xhat, rstd, v_ref[1:2])
        do = (1.0 + g2) * dr2
        dob = do.astype(BF16)
        do_ref[...] = dob
        gacc_ref[0:1, :] += jnp.sum(dy * xhat, axis=0, keepdims=True)
        gacc_ref[1:2, :] += jnp.sum(dy, axis=0, keepdims=True)
        gacc_ref[2:3, :] += jnp.sum(do, axis=0, keepdims=True)
        gacc_ref[3:4, :] += jnp.sum(err * err, axis=0, keepdims=True)
        dhpre = lax.dot_general(dob, w2_v[...], NT, preferred_element_type=F32) * (2.0 * hr)
        dhpb = dhpre.astype(BF16)
        dhp_ref[...] = dhpb
        db1_ref[...] += jnp.sum(dhpre, axis=0, keepdims=True)
        du2 = lax.dot_general(dhpb, w1_v[...], NT, preferred_element_type=F32)
        dx1_ref[...] = ALPHA * dr2 + du2 * (1.0 + sc2)
        bacc_ref[0, 0:1, :] += jnp.sum(du2, axis=0, keepdims=True)
        bacc_ref[0, 1:2, :] += jnp.sum(du2 * x1v, axis=0, keepdims=True)
        bacc_ref[0, 2:3, :] += jnp.sum(dr2 * o, axis=0, keepdims=True)

    row = lambda w: pl.BlockSpec((tb, w), lambda i: (i, 0))
    return pl.pallas_call(
        body, name="mlp_fwd_bwd", grid=(t // tb,),
        out_shape=(jax.ShapeDtypeStruct((t, D_MODEL), F32), jax.ShapeDtypeStruct((t, D_MODEL), BF16),
                   jax.ShapeDtypeStruct((t, D_FF), BF16), jax.ShapeDtypeStruct((t, D_FF), BF16),
                   jax.ShapeDtypeStruct((t, D_MODEL), BF16), jax.ShapeDtypeStruct((8, D_MODEL), F32),
                   jax.ShapeDtypeStruct((1, D_FF), F32), jax.ShapeDtypeStruct((t // seq, 8, D_MODEL), F32)),
        in_specs=[row(D_MODEL), row(D_MODEL), pl.BlockSpec((1, N_MOD, D_MODEL), lambda i: (i // npb, 0, 0)), ANY, ANY,
                  pl.BlockSpec(vec1.shape, lambda i: (0, 0)), pl.BlockSpec(b1.shape, lambda i: (0, 0))],
        out_specs=(row(D_MODEL), row(D_MODEL), row(D_FF), row(D_FF), row(D_MODEL),
                   pl.BlockSpec((8, D_MODEL), lambda i: (0, 0)), pl.BlockSpec((1, D_FF), lambda i: (0, 0)),
                   pl.BlockSpec((1, 8, D_MODEL), lambda i: (i // npb, 0, 0))),
        scratch_shapes=[pltpu.VMEM((D_MODEL, n_fb * fb), BF16), pltpu.VMEM((D_FF, D_MODEL), BF16),
                        pltpu.SemaphoreType.DMA((n_fb,)), pltpu.SemaphoreType.DMA],
        compiler_params=_params(60),
    )(x1, tgt, mod3, w1, w2, vec1, b1)


def _ln1_out_bwd(dx1, x2, mix, mod3, w_out, ln1, seq):
    t = x2.shape[0]
    tb = 512
    npb = seq // tb

    def body(dx1_ref, x_ref, mix_ref, mod_ref, w_ref, ln_ref, dmix_ref, dxa_ref, dys_ref, dy5_ref, gacc_ref, bacc_ref):
        i = pl.program_id(0)

        @pl.when(i == 0)
        def _():
            gacc_ref[...] = jnp.zeros_like(gacc_ref)

        @pl.when(i % npb == 0)
        def _():
            bacc_ref[...] = jnp.zeros_like(bacc_ref)

        m = mod_ref[0]
        mix = mix_ref[...]
        r1 = ALPHA * x_ref[...] + (1.0 + m[2:3]) * mix
        _, xhat, rstd = _layer_norm(r1, ln_ref[0:1], ln_ref[1:2])
        dx1v = dx1_ref[...]
        dr1 = _layer_norm_bwd(dx1v, xhat, rstd, ln_ref[0:1])
        gacc_ref[0:1, :] += jnp.sum(dx1v * xhat, axis=0, keepdims=True)
        gacc_ref[1:2, :] += jnp.sum(dx1v, axis=0, keepdims=True)
        bacc_ref[0, 0:1, :] += jnp.sum(dr1 * mix, axis=0, keepdims=True)
        dmix = ((1.0 + m[2:3]) * dr1).astype(BF16)
        dmix_ref[...] = dmix
        dxa_ref[...] = ALPHA * dr1
        dys_ref[...] = lax.dot_general(dmix, w_ref[0:D_SSD, :], NT, preferred_element_type=F32)
        dy5_ref[...] = lax.dot_general(dmix, w_ref[D_SSD:, :], NT, preferred_element_type=F32)

    row = lambda w: pl.BlockSpec((tb, w), lambda i: (i, 0))
    return pl.pallas_call(
        body, name="ln1_out_bwd", grid=(t // tb,),
        out_shape=(jax.ShapeDtypeStruct((t, D_MODEL), BF16), jax.ShapeDtypeStruct((t, D_MODEL), F32),
                   jax.ShapeDtypeStruct((t, D_SSD), F32), jax.ShapeDtypeStruct((t, D_S5), F32),
                   jax.ShapeDtypeStruct((8, D_MODEL), F32), jax.ShapeDtypeStruct((t // seq, 8, D_MODEL), F32)),
        in_specs=[row(D_MODEL), row(D_MODEL), row(D_MODEL), pl.BlockSpec((1, N_MOD, D_MODEL), lambda i: (i // npb, 0, 0)),
                  pl.BlockSpec(w_out.shape, lambda i: (0, 0)), pl.BlockSpec(ln1.shape, lambda i: (0, 0))],
        out_specs=(row(D_MODEL), row(D_MODEL), row(D_SSD), row(D_S5), pl.BlockSpec((8, D_MODEL), lambda i: (0, 0)),
                   pl.BlockSpec((1, 8, D_MODEL), lambda i: (i // npb, 0, 0))),
        compiler_params=_params(48),
    )(dx1, x2, mix, mod3, w_out, ln1)


def _s5_bwd(dy5, ypre, u5, s_re, s_im, bb_re, bb_im, cc_re, cc_im, pr_re, pr_im, s5d, w_glu, b_glu, seq):
    t = u5.shape[0]
    tb = 256
    npb = seq // tb
    n_blocks = t // tb

    def blk(i):
        return (i // npb) * npb + (npb - 1 - i % npb)

    def body(dy_ref, ypre_ref, u_ref, sre_ref, sim_ref, hre_ref, him_ref, bbr_ref, bbi_ref, ccr_ref, cci_ref,
             prr_ref, pri_ref, d_ref, wg_ref, bg_ref,
             du_ref, vacc_ref, sacc_ref, dcc_ref, dbb_ref, dwg_ref, dsr, dsi, gr, gi, car, cai):
        i = pl.program_id(0)

        @pl.when(i == 0)
        def _():
            for acc in (vacc_ref, sacc_ref, dcc_ref, dbb_ref, dwg_ref):
                acc[...] = jnp.zeros_like(acc)

        @pl.when(i % npb == 0)
        def _():
            car[...] = jnp.zeros_like(car)
            cai[...] = jnp.zeros_like(cai)

        dy = dy_ref[...]
        ypre = ypre_ref[...]
        u = u_ref[...]
        ub = u.astype(BF16)
        yg = _gelu(ypre)
        sg = _sigmoid(_mm(yg, wg_ref[...]) + bg_ref[...])
        dq = dy * yg * sg * (1.0 - sg)
        dqb = dq.astype(BF16)
        dyg = dy * sg + lax.dot_general(dqb, wg_ref[...], NT, preferred_element_type=F32)
        dyp = dyg * _gelu_grad(ypre)
        dypb = dyp.astype(BF16)
        dwg_ref[...] += lax.dot_general(yg.astype(BF16), dqb, TN, preferred_element_type=F32)
        blocks = [(slice(j * 128, (j + 1) * 128), slice(j * 512, (j + 1) * 512)) for j in range(S5_BLOCKS)]
        for j, (ch, st) in enumerate(blocks):
            dsr[:, st] = lax.dot_general(dypb[:, ch], ccr_ref[j], NT, preferred_element_type=F32)
            dsi[:, st] = -lax.dot_general(dypb[:, ch], cci_ref[j], NT, preferred_element_type=F32)
        _tile_scan(dsr, dsi, gr, gi, car, cai, prr_ref, pri_ref, tb // 8, reverse=True)
        g_re, g_im = gr[...], gi[...]
        first_rows = (i % npb) == npb - 1
        hre = jnp.where(first_rows, 0.0, hre_ref[...])
        him = jnp.where(first_rows, 0.0, him_ref[...])
        s_re_v, s_im_v = sre_ref[...], sim_ref[...]
        sp_re = pltpu.roll(jnp.concatenate([hre, s_re_v], axis=0), 1, axis=0)[8:8 + tb]
        sp_im = pltpu.roll(jnp.concatenate([him, s_im_v], axis=0), 1, axis=0)[8:8 + tb]
        vacc_ref[0:1, :] += jnp.sum(g_re * sp_re + g_im * sp_im, axis=0, keepdims=True)
        vacc_ref[1:2, :] += jnp.sum(g_im * sp_re - g_re * sp_im, axis=0, keepdims=True)
        grb, gib = g_re.astype(BF16), g_im.astype(BF16)
        srb, sib = s_re_v.astype(BF16), s_im_v.astype(BF16)
        du_cols = []
        for j, (ch, st) in enumerate(blocks):
            dcc_ref[j] += lax.dot_general(srb[:, st], dypb[:, ch], TN, preferred_element_type=F32)
            dcc_ref[S5_BLOCKS + j] -= lax.dot_general(sib[:, st], dypb[:, ch], TN, preferred_element_type=F32)
            dbb_ref[j] += lax.dot_general(ub[:, ch], grb[:, st], TN, preferred_element_type=F32)
            dbb_ref[S5_BLOCKS + j] += lax.dot_general(ub[:, ch], gib[:, st], TN, preferred_element_type=F32)
            du_cols.append(lax.dot_general(grb[:, st], bbr_ref[j], NT, preferred_element_type=F32)
                           + lax.dot_general(gib[:, st], bbi_ref[j], NT, preferred_element_type=F32))
        du_ref[...] = jnp.concatenate(du_cols, axis=1) + dyp * d_ref[...]
        sacc_ref[0:1, :] += jnp.sum(dyp * u, axis=0, keepdims=True)
        sacc_ref[1:2, :] += jnp.sum(dq, axis=0, keepdims=True)

    row = lambda w: pl.BlockSpec((tb, w), lambda i: (blk(i), 0))
    halo = pl.BlockSpec((8, S5_N), lambda i: (jnp.maximum(blk(i) * (tb // 8) - 1, 0), 0))
    full = lambda a: pl.BlockSpec(a.shape, lambda i: (0,) * a.ndim)
    acc = lambda s: pl.BlockSpec(s, lambda i: (0,) * len(s))
    acc_shapes = [(8, S5_N), (8, D_S5), (2 * S5_BLOCKS, 512, 128), (2 * S5_BLOCKS, 128, 512), (D_S5, D_S5)]
    return pl.pallas_call(
        body, name="s5_bwd", grid=(n_blocks,),
        out_shape=(jax.ShapeDtypeStruct((t, D_S5), F32),) + tuple(jax.ShapeDtypeStruct(s, F32) for s in acc_shapes),
        in_specs=[row(D_S5), row(D_S5), row(D_S5), row(S5_N), row(S5_N), halo, halo, full(bb_re), full(bb_im),
                  full(cc_re), full(cc_im), full(pr_re), full(pr_im), full(s5d), full(w_glu), full(b_glu)],
        out_specs=(row(D_S5),) + tuple(acc(s) for s in acc_shapes),
        scratch_shapes=[pltpu.VMEM((tb, S5_N), F32), pltpu.VMEM((tb, S5_N), F32), pltpu.VMEM((tb, S5_N), F32),
                        pltpu.VMEM((tb, S5_N), F32), pltpu.VMEM((8, S5_N), F32), pltpu.VMEM((8, S5_N), F32)],
        compiler_params=_params(56),
    )(dy5, ypre, u5, s_re, s_im, s_re, s_im, bb_re, bb_im, cc_re, cc_im, pr_re, pr_im, s5d, w_glu, b_glu)


def _ssd_bwd(dyssd, yraw, z, xbc, dt_raw, hprev, par, dsk, normw, seq):
    t = xbc.shape[0]
    nc = seq // CHUNK
    n_chunks = t // CHUNK
    fold = _head_fold()

    def blk(i):
        return (i // nc) * nc + (nc - 1 - i % nc)

    def body(dy_ref, yraw_ref, z_ref, xbc_ref, dt_ref, hprev_ref, par_ref, dsk_ref, nw_ref, fold_ref,
             dxbc_ref, dz_ref, ddt_ref, dpar_ref, cacc_ref, dh_ref, dyr_ref):
        i = pl.program_id(0)

        @pl.when(i == 0)
        def _():
            dpar_ref[...] = jnp.zeros_like(dpar_ref)
            cacc_ref[...] = jnp.zeros_like(cacc_ref)

        @pl.when(i % nc == 0)
        def _():
            dh_ref[...] = jnp.zeros_like(dh_ref)

        zz = z_ref[...]
        sz = _sigmoid(zz)
        silu_z = zz * sz
        yraw = yraw_ref[...]
        for g in range(N_GROUPS):
            sl = slice(g * GW, (g + 1) * GW)
            v = yraw[:, sl] * silu_z[:, sl]
            r = lax.rsqrt(jnp.mean(v * v, axis=-1, keepdims=True) + EPS)
            dyg = dy_ref[:, sl]
            cacc_ref[1:2, sl] += jnp.sum(dyg * v * r, axis=0, keepdims=True)
            dyw = dyg * nw_ref[:, sl]
            dv = r * dyw - v * (r * r * r) * jnp.mean(dyw * v, axis=-1, keepdims=True)
            dyr_ref[:, sl] = dv * silu_z[:, sl]
            dz_ref[:, sl] = dv * yraw[:, sl] * (sz[:, sl] * (1.0 + zz[:, sl] * (1.0 - sz[:, sl])))

        dt, a, cs, cst, causal, tri, dt_c, ecs_c, w_c, pair_cols = _ssd_prep(dt_ref[...], par_ref[...])
        cs_last = cs[CHUNK - 1:CHUNK, :]
        causal2 = jnp.concatenate([causal, causal], axis=1)
        lane = lax.broadcasted_iota(jnp.int32, (CHUNK, 128), 1)
        left = lane < HEADDIM
        lane1 = lax.broadcasted_iota(jnp.int32, (1, 128), 1)
        x = xbc_ref[:, 0:D_SSD]
        xdt = x * dt_c
        dyr = dyr_ref[...]
        dyrb = dyr.astype(BF16)
        cacc_ref[0:1, :] += jnp.sum(dyr * x, axis=0, keepdims=True)
        dlast = jnp.zeros((1, 128), F32)
        dxdt_cols, diag_all, dww_cols = [], [], []
        for g in range(N_GROUPS):
            gs = slice(g * GW, (g + 1) * GW)
            b_sl = slice(D_SSD + g * N_STATE, D_SSD + (g + 1) * N_STATE)
            c_sl = slice(D_SSD + (N_GROUPS + g) * N_STATE, D_SSD + (N_GROUPS + g + 1) * N_STATE)
            bg = xbc_ref[:, b_sl].astype(BF16)
            cg = xbc_ref[:, c_sl].astype(BF16)
            scores = lax.dot_general(cg, bg, NT, preferred_element_type=F32)
            scores2 = jnp.concatenate([scores, scores], axis=1)
            hg = hprev_ref[0, gs, :]
            hgb = hg.astype(BF16)
            dhg = dh_ref[gs, :]
            dhgb = dhg.astype(BF16)
            q_all = lax.dot_general(bg, dhgb, NT, preferred_element_type=F32)
            dscores = jnp.zeros((CHUNK, CHUNK), F32)
            diag_cols = []
            for q in range(GW // 128):
                pair = g * (GW // 128) + q
                ps = slice(pair * 128, (pair + 1) * 128)
                decay = _pair_decay(pair_cols[pair], cst, pair, causal2)
                mcat = (scores2 * decay).astype(BF16)
                dyp = dyrb[:, ps]
                dm = lax.dot_general(dyp, _stack_heads(xdt[:, ps], left), NT, preferred_element_type=F32)
                dmd = dm * decay
                dscores = dscores + dmd[:, 0:CHUNK] + dmd[:, CHUNK:]
                rr = lax.dot_general(mcat, dyp, TN, preferred_element_type=F32)
                diag_cols.append(jnp.where(left, rr[0:CHUNK], rr[CHUNK:]))
            wq = w_c[:, gs] * q_all
            diag_g = jnp.concatenate(diag_cols, axis=1)
            diag_all.append(diag_g)
            dxdt_cols.append(diag_g + wq)
            dww_cols.append(wq * xdt[:, gs])
            dp = (ecs_c[:, gs] * dyr[:, gs]).astype(BF16)
            amat = (w_c[:, gs] * xdt[:, gs]).astype(BF16)
            dsb = dscores.astype(BF16)
            dxbc_ref[:, c_sl] = (jnp.dot(dsb, bg, preferred_element_type=F32)
                                 + jnp.dot(dp, hgb, preferred_element_type=F32))
            dxbc_ref[:, b_sl] = (lax.dot_general(dsb, cg, TN, preferred_element_type=F32)
                                 + jnp.dot(amat, dhgb, preferred_element_type=F32))
            dh_in = lax.dot_general(dp, cg, TN, preferred_element_type=F32)
            for j in range(HPG):
                hh = g * HPG + j
                js = slice(j * HEADDIM, (j + 1) * HEADDIM)
                ecl = jnp.exp(cs_last[:, hh:hh + 1])
                dlast = dlast + jnp.where(lane1 == hh, ecl * jnp.sum(dhg[js, :] * hg[js, :]), 0.0)
                dh_ref[g * GW + j * HEADDIM:g * GW + (j + 1) * HEADDIM, :] = ecl * dhg[js, :] + dh_in[js, :]
        dxdt = jnp.concatenate(dxdt_cols, axis=1)
        dxbc_ref[:, 0:D_SSD] = dxdt * dt_c + dyr * dsk_ref[...]
        dww = _mm(jnp.concatenate(dww_cols, axis=1), fold_ref[...])
        dcs = _dot3(dyrb.astype(F32) * (yraw - x * dsk_ref[...])
                    - xdt.astype(BF16).astype(F32) * jnp.concatenate(diag_all, axis=1), fold_ref[...]) - dww
        rowid = lax.broadcasted_iota(jnp.int32, (CHUNK, 128), 0)
        dcs = dcs + jnp.where(rowid == CHUNK - 1, jnp.sum(dww, axis=0, keepdims=True) + dlast, 0.0)
        dadt = _dot3_left(tri, dcs, TN)
        ddt = _mm(dxdt * x, fold_ref[...]) + dadt * a
        da = jnp.sum(dadt * dt, axis=0, keepdims=True)
        ddt_raw = ddt * _sigmoid(dt_ref[...] + par_ref[0:1])
        ddt_raw = jnp.where(lane < N_HEADS, ddt_raw, 0.0)
        ddt_ref[...] = ddt_raw
        dpar_ref[0:1, :] += jnp.sum(ddt_raw, axis=0, keepdims=True)
        dpar_ref[1:2, :] += jnp.where(lane1 < N_HEADS, da * a, 0.0)

    row = lambda w: pl.BlockSpec((CHUNK, w), lambda i: (blk(i), 0))
    full = lambda s: pl.BlockSpec(s, lambda i: (0,) * len(s))
    return pl.pallas_call(
        body, name="ssd_bwd", grid=(n_chunks,),
        out_shape=(jax.ShapeDtypeStruct((t, D_XBC), F32), jax.ShapeDtypeStruct((t, D_SSD), F32),
                   jax.ShapeDtypeStruct((t, DT_PAD), F32), jax.ShapeDtypeStruct((8, 128), F32),
                   jax.ShapeDtypeStruct((8, D_SSD), F32)),
        in_specs=[row(D_SSD), row(D_SSD), row(D_SSD), row(D_XBC), row(DT_PAD),
                  pl.BlockSpec((1, D_SSD, N_STATE), lambda i: (blk(i), 0, 0)),
                  full((8, 128)), full((1, D_SSD)), full((1, D_SSD)), full(fold.shape)],
        out_specs=(row(D_XBC), row(D_SSD), row(DT_PAD), full((8, 128)), full((8, D_SSD))),
        scratch_shapes=[pltpu.VMEM((D_SSD, N_STATE), F32), pltpu.VMEM((CHUNK, D_SSD), F32)],
        compiler_params=_params(48),
    )(dyssd, yraw, z, xbc, dt_raw, hprev, par, dsk, normw, fold)


def _conv_proj_bwd(dz, dxbc, dsilu, xbc_pre, ddt, du5, x2, dxa, mod3, conv_w, w_in_pad, seq):
    t = x2.shape[0]
    tb = 256
    npb = seq // tb
    n_blocks = t // tb
    cw = 512

    def blk(i):
        return (i // npb) * npb + (npb - 1 - i % npb)

    def body(dz_ref, d_ref, ds_ref, cur_ref, halo_ref, ddt_ref, du5_ref, x_ref, dxa_ref, mod_ref, cw_ref, w_hbm,
             gx_ref, u_ref, dxp_ref, bacc_ref, acc_ref, w_vmem, win_x, win_d, sem):
        i = pl.program_id(0)
        _load_once(w_hbm, w_vmem, sem)

        @pl.when(i == 0)
        def _():
            acc_ref[...] = jnp.zeros_like(acc_ref)

        @pl.when(i % npb == 0)
        def _():
            bacc_ref[...] = jnp.zeros_like(bacc_ref)
            win_d[tb:tb + 8, :] = jnp.zeros((8, D_XBC), F32)

        @pl.when(i % npb != 0)
        def _():
            win_d[tb:tb + 8, :] = win_d[0:8, :]

        first_rows = (i % npb) == npb - 1
        win_x[0:8, :] = jnp.where(first_rows, 0.0, halo_ref[...])
        win_x[8:8 + tb, :] = cur_ref[...]
        w = cw_ref[...]
        for k in range(D_XBC // cw):
            cols = slice(k * cw, (k + 1) * cw)
            dpre = d_ref[:, cols] * ds_ref[:, cols]
            win_d[0:tb, cols] = dpre
            for j in range(4):
                acc_ref[3 - j:4 - j, cols] += jnp.sum(dpre * win_x[8 - j:8 - j + tb, cols], axis=0, keepdims=True)
            acc_ref[4:5, cols] += jnp.sum(dpre, axis=0, keepdims=True)
            dxp = w[3:4, cols] * dpre
            for j in (1, 2, 3):
                dxp = dxp + w[3 - j:4 - j, cols] * win_d[j:j + tb, cols]
            dxp_ref[:, cols] = dxp.astype(BF16)
        o1, o2, o3 = D_SSD, D_SSD + D_XBC, D_SSD + D_XBC + DT_PAD
        du = (jnp.dot(dz_ref[...].astype(BF16), w_vmem[0:o1, :], preferred_element_type=F32)
              + jnp.dot(dxp_ref[...], w_vmem[o1:o2, :], preferred_element_type=F32)
              + jnp.dot(ddt_ref[...].astype(BF16), w_vmem[o2:o3, :], preferred_element_type=F32)
              + jnp.dot(du5_ref[...].astype(BF16), w_vmem[o3:, :], preferred_element_type=F32))
        m = mod_ref[0]
        xv = x_ref[...]
        u_ref[...] = (xv * (1.0 + m[1:2]) + m[0:1]).astype(BF16)
        gx_ref[...] = dxa_ref[...] + du * (1.0 + m[1:2])
        bacc_ref[0, 0:1, :] += jnp.sum(du, axis=0, keepdims=True)
        bacc_ref[0, 1:2, :] += jnp.sum(du * xv, axis=0, keepdims=True)

    row = lambda w: pl.BlockSpec((tb, w), lambda i: (blk(i), 0))
    halo = pl.BlockSpec((8, D_XBC), lambda i: (jnp.maximum(blk(i) * (tb // 8) - 1, 0), 0))
    return pl.pallas_call(
        body, name="conv_proj_bwd", grid=(n_blocks,),
        out_shape=(jax.ShapeDtypeStruct((t, D_MODEL), F32), jax.ShapeDtypeStruct((t, D_MODEL), BF16),
                   jax.ShapeDtypeStruct((t, D_XBC), BF16), jax.ShapeDtypeStruct((t // seq, 8, D_MODEL), F32),
                   jax.ShapeDtypeStruct((8, D_XBC), F32)),
        in_specs=[row(D_SSD), row(D_XBC), row(D_XBC), row(D_XBC), halo, row(DT_PAD), row(D_S5), row(D_MODEL),
                  row(D_MODEL), pl.BlockSpec((1, N_MOD, D_MODEL), lambda i: (i // npb, 0, 0)),
                  pl.BlockSpec((4, D_XBC), lambda i: (0, 0)), ANY],
        out_specs=(row(D_MODEL), row(D_MODEL), row(D_XBC), pl.BlockSpec((1, 8, D_MODEL), lambda i: (i // npb, 0, 0)),
                   pl.BlockSpec((8, D_XBC), lambda i: (0, 0))),
        scratch_shapes=[pltpu.VMEM((D_INP, D_MODEL), BF16), pltpu.VMEM((tb + 8, D_XBC), F32),
                        pltpu.VMEM((tb + 8, D_XBC), F32), pltpu.SemaphoreType.DMA],
        compiler_params=_params(60),
    )(dz, dxbc, dsilu, xbc_pre, xbc_pre, ddt, du5, x2, dxa, mod3, conv_w, w_in_pad)


def _pad_rows(a, mult):
    r = a.shape[0]
    pad = (-r) % mult
    return a if pad == 0 else jnp.concatenate([a, jnp.zeros((pad,) + a.shape[1:], a.dtype)], axis=0)


_SMALL = ["conv_w", "conv_b", "dt_bias", "a_log", "d_ssd", "norm_w", "s5_a_re", "s5_a_im", "s5_log_dt", "s5_b_re",
          "s5_b_im", "s5_c_re", "s5_c_im", "s5_d", "b_glu", "ln1_g", "ln1_b", "b1", "b2", "ln2_g", "ln2_b"]


def _tile_rows(size):
    return 8 * (-(-size // 1024))


def _pack_small(d):
    parts = []
    for n in _SMALL:
        flat = d[n].reshape(-1).astype(F32)
        rows = _tile_rows(flat.shape[0])
        pad = rows * 128 - flat.shape[0]
        if pad:
            flat = jnp.concatenate([flat, jnp.zeros((pad,), F32)])
        parts.append(flat.reshape(rows, 128))
    return jnp.concatenate(parts, axis=0)


def _unpack_small(p, shapes):
    out, off = {}, 0
    for n in _SMALL:
        size = math.prod(shapes[n])
        rows = _tile_rows(size)
        out[n] = p[off:off + rows].reshape(-1)[:size].reshape(shapes[n])
        off += rows
    return out


def kernel(x, c, w_ada, b_ada, w_in, conv_w, conv_b, dt_bias, a_log, d_ssd, norm_w, s5_a_re, s5_a_im, s5_log_dt, s5_b_re, s5_b_im, s5_c_re, s5_c_im, s5_d, w_glu, b_glu, w_out, ln1_g, ln1_b, w1, b1, w2, b2, ln2_g, ln2_b, loss_target, m_w_ada, m_b_ada, m_w_in, m_conv_w, m_conv_b, m_dt_bias, m_a_log, m_d_ssd, m_norm_w, m_s5_a_re, m_s5_a_im, m_s5_log_dt, m_s5_b_re, m_s5_b_im, m_s5_c_re, m_s5_c_im, m_s5_d, m_w_glu, m_b_glu, m_w_out, m_ln1_g, m_ln1_b, m_w1, m_b1, m_w2, m_b2, m_ln2_g, m_ln2_b, v_w_ada, v_b_ada, v_w_in, v_conv_w, v_conv_b, v_dt_bias, v_a_log, v_d_ssd, v_norm_w, v_s5_a_re, v_s5_a_im, v_s5_log_dt, v_s5_b_re, v_s5_b_im, v_s5_c_re, v_s5_c_im, v_s5_d, v_w_glu, v_b_glu, v_w_out, v_ln1_g, v_ln1_b, v_w1, v_b1, v_w2, v_b2, v_ln2_g, v_ln2_b):
    weights = dict(w_ada=w_ada, b_ada=b_ada, w_in=w_in, conv_w=conv_w, conv_b=conv_b, dt_bias=dt_bias, a_log=a_log,
                   d_ssd=d_ssd, norm_w=norm_w, s5_a_re=s5_a_re, s5_a_im=s5_a_im, s5_log_dt=s5_log_dt, s5_b_re=s5_b_re,
                   s5_b_im=s5_b_im, s5_c_re=s5_c_re, s5_c_im=s5_c_im, s5_d=s5_d, w_glu=w_glu, b_glu=b_glu, w_out=w_out,
                   ln1_g=ln1_g, ln1_b=ln1_b, w1=w1, b1=b1, w2=w2, b2=b2, ln2_g=ln2_g, ln2_b=ln2_b)
    mom = dict(w_ada=m_w_ada, b_ada=m_b_ada, w_in=m_w_in, conv_w=m_conv_w, conv_b=m_conv_b, dt_bias=m_dt_bias,
               a_log=m_a_log, d_ssd=m_d_ssd, norm_w=m_norm_w, s5_a_re=m_s5_a_re, s5_a_im=m_s5_a_im,
               s5_log_dt=m_s5_log_dt, s5_b_re=m_s5_b_re, s5_b_im=m_s5_b_im, s5_c_re=m_s5_c_re, s5_c_im=m_s5_c_im,
               s5_d=m_s5_d, w_glu=m_w_glu, b_glu=m_b_glu, w_out=m_w_out, ln1_g=m_ln1_g, ln1_b=m_ln1_b, w1=m_w1, b1=m_b1,
               w2=m_w2, b2=m_b2, ln2_g=m_ln2_g, ln2_b=m_ln2_b)
    var = dict(w_ada=v_w_ada, b_ada=v_b_ada, w_in=v_w_in, conv_w=v_conv_w, conv_b=v_conv_b, dt_bias=v_dt_bias,
               a_log=v_a_log, d_ssd=v_d_ssd, norm_w=v_norm_w, s5_a_re=v_s5_a_re, s5_a_im=v_s5_a_im,
               s5_log_dt=v_s5_log_dt, s5_b_re=v_s5_b_re, s5_b_im=v_s5_b_im, s5_c_re=v_s5_c_re, s5_c_im=v_s5_c_im,
               s5_d=v_s5_d, w_glu=v_w_glu, b_glu=v_b_glu, w_out=v_w_out, ln1_g=v_ln1_g, ln1_b=v_ln1_b, w1=v_w1, b1=v_b1,
               w2=v_w2, b2=v_b2, ln2_g=v_ln2_g, ln2_b=v_ln2_b)
    names = list(weights)
    shapes = {n: weights[n].shape for n in names}

    nb, seq, _ = x.shape
    t = nb * seq
    dev = _dev_index()
    x2 = x.reshape(t, D_MODEL)
    tgt2 = loss_target.reshape(t, D_MODEL)

    cw_cols = conv_w.shape[2]
    small_in = jnp.concatenate([c.reshape(-1), conv_w.reshape(-1)]).reshape(-1, 128)
    big_names = ["w_in", "w_out", "w1", "w2", "w_glu"]
    local = {n: (a[0].T if n == "w_in" else a[0]) for n, a in weights.items() if n in big_names}
    shard_bf16 = {n: local[n].astype(BF16) for n in big_names}
    first = _all_gather([small_in, shard_bf16["w_in"], shard_bf16["w_glu"]], "gather_first")
    small_all = first[0].reshape(N_DEV, -1)
    c_all = small_all[:, :nb * D_MODEL].reshape(N_DEV * nb, D_MODEL)
    conv_w_full = small_all[:, nb * D_MODEL:].reshape(N_DEV, 4, cw_cols).transpose(1, 0, 2).reshape(4, D_XBC)

    w_in_t = first[1].reshape(D_IN, D_MODEL)
    w_in_pad = jnp.concatenate(
        [w_in_t[:D_SSD + D_XBC + N_HEADS], jnp.zeros((DT_PAD - N_HEADS, D_MODEL), BF16),
         w_in_t[D_SSD + D_XBC + N_HEADS:]], axis=0)
    w_glu_f = first[2].reshape(D_S5, D_S5)
    late_names = ["w_out", "w1", "w2"]

    ada_cols = w_ada.shape[2]
    b_cols = lax.dynamic_slice_in_dim(b_ada, dev * ada_cols, ada_cols, axis=1)
    mod_cols = _mod_fwd(c_all, w_ada[0], b_cols)
    mod_all = _all_gather([mod_cols], "gather_mod")[0]
    mod_mine = lax.dynamic_slice_in_dim(mod_all, dev * nb, nb, axis=1)
    mod3 = mod_mine.transpose(1, 0, 2).reshape(nb, N_MOD, D_MODEL)

    def pad_lanes(v, n):
        return jnp.concatenate([v, jnp.zeros((v.shape[0], n - v.shape[1]), F32)], axis=1)

    par = _pad_rows(jnp.concatenate([pad_lanes(dt_bias, 128), pad_lanes(a_log, 128)], axis=0), 8)
    dsk = jnp.repeat(d_ssd[0], HEADDIM).reshape(1, D_SSD)
    ar = s5_a_re.reshape(1, S5_N)
    ai = s5_a_im.reshape(1, S5_N)
    ldt = jnp.repeat(s5_log_dt[0], S5_P).reshape(1, S5_N)
    br_t = s5_b_re[0].transpose(2, 0, 1).reshape(S5_CH, S5_N)
    bi_t = s5_b_im[0].transpose(2, 0, 1).reshape(S5_CH, S5_N)
    bb_re_t, bb_im_t, pf_re, pf_im, pr_re, pr_im = _s5_params_fwd(ar, ai, ldt, br_t, bi_t)
    gpb = S5_GROUPS // S5_BLOCKS
    mask_b = (jnp.arange(128)[:, None] // S5_CH) == (jnp.arange(512)[None, :] // S5_P)

    def dense_b(bt_):
        blocks = bt_.reshape(S5_CH, S5_BLOCKS, 512).transpose(1, 0, 2)
        return jnp.where(mask_b, jnp.tile(blocks, (1, gpb, 1)), 0.0).astype(BF16)

    def dense_c(cc):
        blocks = cc[0].transpose(0, 2, 1).reshape(S5_BLOCKS, 512, S5_CH)
        return jnp.where(mask_b.T, jnp.tile(blocks, (1, 1, gpb)), 0.0).astype(BF16)

    bb_re, bb_im = dense_b(bb_re_t), dense_b(bb_im_t)
    cc_re, cc_im = dense_c(s5_c_re), dense_c(s5_c_im)
    s5d = s5_d.reshape(1, D_S5)
    ln1 = jnp.concatenate([ln1_g, ln1_b], axis=0)
    vec1 = _pad_rows(jnp.concatenate([b2, ln2_g, ln2_b], axis=0), 8)

    z, xbc_pre, xbc, dsilu, dt_raw, u5 = _proj_conv_fwd(x2, mod3, w_in_pad, conv_w_full, conv_b, seq)
    late_in, dt_raw = lax.optimization_barrier(([shard_bf16[n] for n in late_names], dt_raw))
    late_sems = _gather_start(late_in, "gather_late_start")
    yraw, ycat, hprev = _ssd_fwd(xbc, z, dt_raw, par + late_sems[4][0, 0], dsk, norm_w, seq)
    s_re, s_im, ypre, ycat = _s5_fwd(u5, bb_re, bb_im, cc_re, cc_im, pf_re, pf_im, s5d, w_glu_f, b_glu, ycat, seq)
    sent, landed = _gather_wait(late_sems[0], late_sems[1], late_sems[2], late_sems[3], ycat, "gather_late_wait")
    gathered = {n: lax.dynamic_update_index_in_dim(l, x, dev, 0) for n, x, l in zip(late_names, sent, landed)}
    w_out_f = gathered["w_out"].reshape(2 * D_MODEL, D_MODEL)
    w1_blocks = gathered["w1"]
    w2_f = gathered["w2"].reshape(D_FF, D_MODEL)
    mix, x1 = _out_ln1(ycat, x2, mod3, w_out_f, ln1, seq)

    dx1, u2b, hb, dhpb, dob, gacc2, db1, bacc2 = _mlp_fwd_bwd(x1, tgt2, mod3, w1_blocks, w2_f, vec1, b1, seq)
    loss = lax.psum(0.5 / D_MODEL * jnp.sum(gacc2[3]), ("x", "y", "c"))

    dmixb, dxa, dyssd, dy5, gacc1, bacc1 = _ln1_out_bwd(dx1, x2, mix, mod3, w_out_f, ln1, seq)

    g_w2 = _atb(hb, dob, "gw2")
    g_w1 = _atb(u2b, dhpb, "gw1")
    g_wout = _atb(ycat, dmixb, "gwout")
    core = lax.axis_index("c").astype(jnp.int32).reshape(1)
    chip = 2 * lax.axis_index("x") + lax.axis_index("y")

    def chip_sums_of(names, grads, tag):
        by_dest = [g if g.ndim == 2 else g.reshape((4, 2) + g.shape[1:]) for g in grads]
        from_sibling = _sibling_swap(by_dest, "rs_swap_" + tag)
        return [_add_halves(g, r, core, "rs_add_" + n) for g, r, n in zip(by_dest, from_sibling, names)]

    early_names = ["w_out", "w1", "w2"]
    early_dest = [g_wout.reshape((4, 2) + w_out.shape[1:]), g_w1, g_w2.reshape((4, 2) + w2.shape[1:])]
    swap = _sibling_swap_start(early_dest, "rs_early_swap_start")
    du5, vacc, sacc, d_cc, d_bb, g_wglu = _s5_bwd(dy5, ypre, u5, s_re, s_im, bb_re, bb_im, cc_re, cc_im,
                                                  pr_re, pr_im, s5d + swap[4][0, 0], w_glu_f, b_glu, seq)
    early_dest, from_sibling = _sibling_swap_wait(swap[0], swap[1], swap[2], swap[3], du5, "rs_early_swap_wait")
    early_sums = [_add_halves(g, r, core, "rs_add_" + n) for g, r, n in zip(early_dest, from_sibling, early_names)]
    early = _all_to_all_start(early_sums, "rs_early_start")
    dxbc, dz, ddt, dpar, cacc = _ssd_bwd(dyssd, yraw, z, xbc, dt_raw, hprev, par + early[4][0, 0], dsk, norm_w, seq)
    grad_x2, ub, dxpb, bacc0, conv_acc = _conv_proj_bwd(dz, dxbc, dsilu, xbc_pre, ddt, du5, x2, dxa, mod3,
                                                        conv_w_full, w_in_pad, seq)

    def diag_b(dd):
        kept = jnp.where(mask_b, dd, 0.0).reshape(S5_BLOCKS, gpb, S5_CH, 512).sum(1)
        return kept.transpose(1, 0, 2).reshape(S5_CH, S5_N)

    def diag_c(dd):
        kept = jnp.where(mask_b.T, dd, 0.0).reshape(S5_BLOCKS, 512, gpb, S5_CH).sum(2)
        return kept.reshape(S5_GROUPS, S5_P, S5_CH).transpose(0, 2, 1)

    g_ar, g_ai, g_ldt, g_br_t, g_bi_t = _s5_params_bwd(ar, ai, ldt, br_t, bi_t, vacc[0:1], vacc[1:2],
                                                      diag_b(d_bb[:S5_BLOCKS]), diag_b(d_bb[S5_BLOCKS:]))

    def from_t(gt):
        return gt.reshape(S5_CH, S5_GROUPS, S5_P).transpose(1, 2, 0)

    small_g = dict(
        conv_w=conv_acc[0:4], conv_b=conv_acc[4:5], dt_bias=dpar[0:1, :N_HEADS], a_log=dpar[1:2, :N_HEADS],
        d_ssd=cacc[0].reshape(N_HEADS, HEADDIM).sum(1), norm_w=cacc[1:2],
        s5_a_re=g_ar, s5_a_im=g_ai, s5_log_dt=g_ldt[:, :S5_GROUPS], s5_b_re=from_t(g_br_t), s5_b_im=from_t(g_bi_t),
        s5_c_re=diag_c(d_cc[:S5_BLOCKS]), s5_c_im=diag_c(d_cc[S5_BLOCKS:]), s5_d=sacc[0:1], b_glu=sacc[1:2],
        ln1_g=gacc1[0:1], ln1_b=gacc1[1:2], b1=db1, b2=gacc2[2:3], ln2_g=gacc2[0:1], ln2_b=gacc2[1:2])

    dmod = jnp.concatenate([bacc0[:, 0], bacc0[:, 1], bacc1[:, 0], bacc2[:, 0], bacc2[:, 1], bacc2[:, 2]], axis=1)
    small_sems = _gather_start([dmod, _pack_small(small_g)], "gather_small_start")
    (dz, dxpb, ddt, du5, ub), _ = lax.optimization_barrier(((dz, dxpb, ddt, du5, ub), small_sems[4]))
    g_win_t = jnp.concatenate([_atb(dz, ub, "gwin_z"), _atb(dxpb, ub, "gwin_xbc"),
                               _atb(ddt, ub, "gwin_dt")[:N_HEADS], _atb(du5, ub, "gwin_s5")], axis=0)
    sent, landed = _gather_wait(small_sems[0], small_sems[1], small_sems[2], small_sems[3], g_win_t,
                                "gather_small_wait")
    dmod_all, small_parts = [lax.dynamic_update_index_in_dim(l, x, dev, 0) for x, l in zip(sent, landed)]
    dmod_all = dmod_all.reshape(N_DEV * nb, N_MOD * D_MODEL)
    dmod_cols = lax.dynamic_slice_in_dim(dmod_all, dev * ada_cols, ada_cols, axis=1)
    g_wada, g_bada = _mod_bwd(c_all, dmod_cols, dmod_all)

    late_rs = ["w_in", "w_glu"]
    late_g, small_parts = lax.optimization_barrier(
        ([g_win_t.reshape(N_DEV, w_in.shape[2], D_MODEL), g_wglu.reshape((N_DEV,) + w_glu.shape[1:])], small_parts))
    late = _all_to_all_start(chip_sums_of(late_rs, late_g, "late"), "rs_late_start")

    def own_block_in(landed, sent):
        return [lax.dynamic_update_index_in_dim(l, lax.dynamic_index_in_dim(h, chip, 0, keepdims=False), chip, 0)
                for l, h in zip(landed, sent)]

    sent, landed = _all_to_all_wait(early[0], early[1], early[2], early[3], late[4], "rs_early_wait")
    parts = dict(zip(early_names, own_block_in(landed, sent)))
    res = {k: {} for k in "gdmv"}

    def update(n):
        w_m_v = [(a[n][0].T if n == "w_in" else a[n][0]) for a in (weights, mom, var)]
        outs = _adamw(parts[n], *w_m_v, "adamw_" + n)
        for k, a in zip("gdmv", outs):
            res[k][n] = (a.T if n == "w_in" else a)[None]

    for n in early_names:
        update(n)
    sent, landed = _all_to_all_wait(late[0], late[1], late[2], late[3], res["d"]["w2"], "rs_late_wait")
    parts.update(zip(late_rs, own_block_in(landed, sent)))
    for n in late_rs:
        update(n)

    ag, ad, am, av = _adamw(g_wada[None], w_ada[0], m_w_ada[0], v_w_ada[0], "adamw_w_ada")
    for k, a in (("g", ag), ("d", ad), ("m", am), ("v", av)):
        res[k]["w_ada"] = a[None]
    bg_, bd_, bm_, bv_ = _adamw(g_bada.reshape(1, -1, 128), b_ada.reshape(-1, 128), m_b_ada.reshape(-1, 128),
                                v_b_ada.reshape(-1, 128), "adamw_b_ada")
    for k, a in (("g", bg_), ("d", bd_), ("m", bm_), ("v", bv_)):
        res[k]["b_ada"] = a.reshape(shapes["b_ada"])

    small_shapes = dict(shapes)
    small_shapes["conv_w"] = (1, 4, D_XBC)
    rep = {n: (jnp.zeros((1, 4, D_XBC), F32) if n == "conv_w" else weights[n]) for n in _SMALL}
    rep_m = {n: (jnp.zeros((1, 4, D_XBC), F32) if n == "conv_w" else mom[n]) for n in _SMALL}
    rep_v = {n: (jnp.ones((1, 4, D_XBC), F32) if n == "conv_w" else var[n]) for n in _SMALL}
    sg_, sd_, sm_, sv_ = _adamw(small_parts, _pack_small(rep), _pack_small(rep_m), _pack_small(rep_v), "adamw_small")
    for k, p in (("g", sg_), ("d", sd_), ("m", sm_), ("v", sv_)):
        un = _unpack_small(p, small_shapes)
        for n in _SMALL:
            if n != "conv_w":
                res[k][n] = un[n]
    g_conv_full = _unpack_small(sg_, small_shapes)["conv_w"][0]
    g_conv_mine = lax.dynamic_slice_in_dim(g_conv_full, dev * cw_cols, cw_cols, axis=1)
    cg_, cd_, cm_, cv_ = _adamw(g_conv_mine[None], conv_w[0], m_conv_w[0], v_conv_w[0], "adamw_conv_w")
    for k, a in (("g", cg_), ("d", cd_), ("m", cm_), ("v", cv_)):
        res[k]["conv_w"] = a[None]

    grad_x = grad_x2.reshape(nb, seq, D_MODEL)
    return (loss, grad_x, *[res["g"][n] for n in names], *[res["d"][n] for n in names],
            *[res["m"][n] for n in names], *[res["v"][n] for n in names])
```

```python
import functools
import math

import jax
import jax.numpy as jnp
from jax import lax
from jax.experimental import pallas as pl
from jax.experimental.pallas import tpu as pltpu

F32, BF16 = jnp.float32, jnp.bfloat16
MESH = pl.DeviceIdType.MESH
N_DEV = 8

D_MODEL = 1024
D_SSD = 1536
N_HEADS = 24
HEADDIM = 64
N_GROUPS = 4
HPG = 6
GW = HPG * HEADDIM
N_STATE = 128
CHUNK = 128
D_XBC = 2560
D_S5 = 512
S5_GROUPS = 32
S5_CH = 16
S5_P = 64
S5_N = S5_GROUPS * S5_P
D_IN = 4632
DT_PAD = 128
D_INP = D_SSD + D_XBC + DT_PAD + D_S5
D_FF = 4096
N_MOD = 6
ALPHA = 2.0 ** 0.25
EPS = 1e-5
LR, B1, B2, AEPS, WD, STEP = 0.001, 0.9, 0.999, 1e-08, 0.01, 10

NT = (((1,), (1,)), ((), ()))
TN = (((0,), (0,)), ((), ()))
ANY = pl.BlockSpec(memory_space=pl.ANY)
HIGHEST = lax.Precision.HIGHEST


def _mm(a, b):
    return jnp.dot(a.astype(BF16), b.astype(BF16), preferred_element_type=F32)


def _mm_nt(a, b):
    return lax.dot_general(a.astype(BF16), b.astype(BF16), NT, preferred_element_type=F32)


def _mm_tn(a, b):
    return lax.dot_general(a.astype(BF16), b.astype(BF16), TN, preferred_element_type=F32)


def _row_block(r, cap):
    best = r
    for cand in range(8, min(r, cap) + 1, 8):
        if r % cand == 0:
            best = cand
    return best if best <= cap else r


def _params(vmem_mb):
    return pltpu.CompilerParams(vmem_limit_bytes=vmem_mb << 20)


def _sigmoid(x):
    return 0.5 * (jnp.tanh(0.5 * x) + 1.0)


def _softplus(x):
    return jnp.maximum(x, 0.0) + jnp.log(1.0 + jnp.exp(-jnp.abs(x)))


_GK = math.sqrt(2.0 / math.pi)


def _gelu(x):
    return 0.5 * x * (1.0 + jnp.tanh(_GK * (x + 0.044715 * x * x * x)))


def _gelu_grad(x):
    t = jnp.tanh(_GK * (x + 0.044715 * x * x * x))
    return 0.5 * (1.0 + t) + 0.5 * x * (1.0 - t * t) * _GK * (1.0 + 3.0 * 0.044715 * x * x)


def _dev_index():
    return 4 * lax.axis_index("x") + 2 * lax.axis_index("y") + lax.axis_index("c")


def _all_gather(xs, name):
    n = len(xs)

    def body(*refs):
        x_refs, out_refs = refs[:n], refs[n:2 * n]
        send_sems, recv_sems, local_sems = refs[2 * n:]
        ix, iy, ic = lax.axis_index("x"), lax.axis_index("y"), lax.axis_index("c")
        me, sibling = (ix, iy, ic), (ix, iy, 1 - ic)
        chips = [(1 - ix, iy), (ix, 1 - iy), (1 - ix, 1 - iy)]

        def slot(a, px, py, pc):
            return out_refs[a].at[4 * px + 2 * py + pc]

        def copy(a, k, block, to, src=None):
            return pltpu.make_async_remote_copy(
                src_ref=slot(a, *block) if src is None else src, dst_ref=slot(a, *block),
                send_sem=send_sems.at[7 * a + k], recv_sem=recv_sems.at[7 * a + k], device_id=to, device_id_type=MESH)

        mine = [pltpu.make_async_copy(x_refs[a], slot(a, *me), local_sems.at[a]) for a in range(n)]
        for cp in mine:
            cp.start()
        first = []
        for j, chip in enumerate(chips):
            first += [copy(a, 1 + j, me, (*chip, ic), src=x_refs[a]) for a in range(n)]
        first += [copy(a, 0, me, sibling, src=x_refs[a]) for a in range(n)]
        for cp in first:
            cp.start()
        passed = []
        for j, chip in enumerate(chips):
            for a in range(n):
                copy(a, 1 + j, (*chip, ic), me).wait_recv()
                cp = copy(a, 4 + j, (*chip, ic), sibling)
                cp.start()
                passed.append(cp)
        for a in range(n):
            copy(a, 0, sibling, me).wait_recv()
            for j, chip in enumerate(chips):
                copy(a, 4 + j, (*chip, 1 - ic), me).wait_recv()
        for cp in first + passed:
            cp.wait_send()
        for cp in mine:
            cp.wait()

    return pl.pallas_call(
        body, name=name, out_shape=tuple(jax.ShapeDtypeStruct((N_DEV,) + x.shape, x.dtype) for x in xs),
        in_specs=[ANY] * n, out_specs=tuple([ANY] * n),
        scratch_shapes=[pltpu.SemaphoreType.DMA((7 * n,)), pltpu.SemaphoreType.DMA((7 * n,)),
                        pltpu.SemaphoreType.DMA((n,))],
    )(*xs)


HBM = pl.BlockSpec(memory_space=pltpu.HBM)
SEM = pl.BlockSpec(memory_space=pltpu.SEMAPHORE)
DATAFLOW = pltpu.SideEffectType.DATAFLOW_SIDE_EFFECTING


def _peer(k):
    ix, iy, ic = lax.axis_index("x"), lax.axis_index("y"), lax.axis_index("c")
    return (1 - ix if k & 4 else ix, 1 - iy if k & 2 else iy, 1 - ic if k & 1 else ic)


def _block_of(p):
    return 4 * p[0] + 2 * p[1] + p[2]


def _gather_start(xs, name):
    n = len(xs)
    lands = [lax.empty((N_DEV,) + x.shape, x.dtype) for x in xs]

    def body(*refs):
        x_refs, land_refs = refs[:n], refs[n:2 * n]
        send_sems, recv_sems = refs[2 * n], refs[2 * n + 1]
        token = refs[-1]
        me = _block_of(_peer(0))
        for a in range(n):
            for k in range(1, N_DEV):
                pltpu.make_async_remote_copy(
                    src_ref=x_refs[a], dst_ref=land_refs[a].at[me], send_sem=send_sems.at[7 * a + k - 1],
                    recv_sem=recv_sems.at[7 * a + k - 1], device_id=_peer(k), device_id_type=MESH).start()
        token[...] = jnp.zeros_like(token)

    outs = pl.pallas_call(
        body, name=name,
        out_shape=(pltpu.SemaphoreType.DMA((7 * n,)), pltpu.SemaphoreType.DMA((7 * n,)))
        + tuple(pltpu.HBM(x.shape, x.dtype) for x in xs) + tuple(pltpu.HBM(l.shape, l.dtype) for l in lands)
        + (jax.ShapeDtypeStruct((8, 128), F32),),
        in_specs=[HBM] * (2 * n), out_specs=(SEM, SEM) + (HBM,) * (2 * n) + (pl.BlockSpec(memory_space=pltpu.VMEM),),
        input_output_aliases={i: 2 + i for i in range(2 * n)},
        compiler_params=pltpu.CompilerParams(has_side_effects=DATAFLOW),
    )(*[pltpu.with_memory_space_constraint(x, pltpu.HBM) for x in xs],
      *[pltpu.with_memory_space_constraint(l, pltpu.HBM) for l in lands])
    return outs[0], outs[1], outs[2:2 + n], outs[2 + n:2 + 2 * n], outs[-1]


def _gather_wait(send_sems, recv_sems, xs_thru, lands_thru, after, name):
    n = len(xs_thru)

    def body(*refs):
        x_refs, land_refs = refs[:n], refs[n:2 * n]
        send_sems, recv_sems = refs[2 * n], refs[2 * n + 1]
        for a in range(n):
            for k in range(1, N_DEV):
                cp = pltpu.make_async_remote_copy(
                    src_ref=x_refs[a], dst_ref=land_refs[a].at[_block_of(_peer(k))], send_sem=send_sems.at[7 * a + k - 1],
                    recv_sem=recv_sems.at[7 * a + k - 1], device_id=_peer(k), device_id_type=MESH)
                cp.wait_send()
                cp.wait_recv()

    outs = pl.pallas_call(
        body, name=name,
        out_shape=tuple(pltpu.HBM(x.shape, x.dtype) for x in xs_thru)
        + tuple(pltpu.HBM(l.shape, l.dtype) for l in lands_thru),
        in_specs=[HBM] * (2 * n) + [SEM, SEM, ANY], out_specs=(HBM,) * (2 * n),
        input_output_aliases={i: i for i in range(2 * n)},
        compiler_params=pltpu.CompilerParams(has_side_effects=DATAFLOW),
    )(*xs_thru, *lands_thru, send_sems, recv_sems, after)
    return outs[:n], outs[n:]


def _chip_peer(k):
    ix, iy = lax.axis_index("x"), lax.axis_index("y")
    return (1 - ix if k & 2 else ix, 1 - iy if k & 1 else iy)


def _all_to_all_start(hs, name):
    n = len(hs)
    lands = [lax.empty(h.shape, h.dtype) for h in hs]

    def body(*refs):
        h_refs, land_refs = refs[:n], refs[n:2 * n]
        send_sems, recv_sems = refs[2 * n], refs[2 * n + 1]
        token = refs[-1]
        ic = lax.axis_index("c")
        mx, my = _chip_peer(0)
        for a in range(n):
            for k in range(1, 4):
                px, py = _chip_peer(k)
                pltpu.make_async_remote_copy(
                    src_ref=h_refs[a].at[2 * px + py], dst_ref=land_refs[a].at[2 * mx + my],
                    send_sem=send_sems.at[3 * a + k - 1], recv_sem=recv_sems.at[3 * a + k - 1],
                    device_id=(px, py, ic), device_id_type=MESH).start()
        token[...] = jnp.zeros_like(token)

    outs = pl.pallas_call(
        body, name=name,
        out_shape=(pltpu.SemaphoreType.DMA((3 * n,)), pltpu.SemaphoreType.DMA((3 * n,)))
        + tuple(pltpu.HBM(h.shape, h.dtype) for h in hs) + tuple(pltpu.HBM(l.shape, l.dtype) for l in lands)
        + (jax.ShapeDtypeStruct((8, 128), F32),),
        in_specs=[HBM] * (2 * n), out_specs=(SEM, SEM) + (HBM,) * (2 * n) + (pl.BlockSpec(memory_space=pltpu.VMEM),),
        input_output_aliases={i: 2 + i for i in range(2 * n)},
        compiler_params=pltpu.CompilerParams(has_side_effects=DATAFLOW),
    )(*[pltpu.with_memory_space_constraint(h, pltpu.HBM) for h in hs],
      *[pltpu.with_memory_space_constraint(l, pltpu.HBM) for l in lands])
    return outs[0], outs[1], outs[2:2 + n], outs[2 + n:2 + 2 * n], outs[-1]


def _all_to_all_wait(send_sems, recv_sems, hs_thru, lands_thru, after, name):
    n = len(hs_thru)

    def body(*refs):
        h_refs, land_refs = refs[:n], refs[n:2 * n]
        send_sems, recv_sems = refs[2 * n], refs[2 * n + 1]
        ic = lax.axis_index("c")
        for a in range(n):
            for k in range(1, 4):
                px, py = _chip_peer(k)
                cp = pltpu.make_async_remote_copy(
                    src_ref=h_refs[a].at[2 * px + py], dst_ref=land_refs[a].at[2 * px + py],
                    send_sem=send_sems.at[3 * a + k - 1], recv_sem=recv_sems.at[3 * a + k - 1],
                    device_id=(px, py, ic), device_id_type=MESH)
                cp.wait_send()
                cp.wait_recv()

    outs = pl.pallas_call(
        body, name=name,
        out_shape=tuple(pltpu.HBM(h.shape, h.dtype) for h in hs_thru)
        + tuple(pltpu.HBM(l.shape, l.dtype) for l in lands_thru),
        in_specs=[HBM] * (2 * n) + [SEM, SEM, ANY], out_specs=(HBM,) * (2 * n),
        input_output_aliases={i: i for i in range(2 * n)},
        compiler_params=pltpu.CompilerParams(has_side_effects=DATAFLOW),
    )(*hs_thru, *lands_thru, send_sems, recv_sems, after)
    return outs[:n], outs[n:]


def _sibling_block(g_ref, q):
    ic = lax.axis_index("c")
    if len(g_ref.shape) == 4:
        return g_ref.at[q, 1 - ic]
    cw = g_ref.shape[1] // N_DEV
    return g_ref.at[:, pl.ds(pl.multiple_of((2 * q + 1 - ic) * cw, 128), cw)]


def _sibling_swap_start(gs, name):
    n = len(gs)
    lands = [lax.empty((4,) + (g.shape[2:] if g.ndim == 4 else (g.shape[0], g.shape[1] // N_DEV)), g.dtype) for g in gs]

    def body(*refs):
        g_refs, land_refs = refs[:n], refs[n:2 * n]
        send_sems, recv_sems = refs[2 * n], refs[2 * n + 1]
        token = refs[-1]
        for a in range(n):
            for q in range(4):
                pltpu.make_async_remote_copy(
                    src_ref=_sibling_block(g_refs[a], q), dst_ref=land_refs[a].at[q],
                    send_sem=send_sems.at[4 * a + q], recv_sem=recv_sems.at[4 * a + q],
                    device_id=_peer(1), device_id_type=MESH).start()
        token[...] = jnp.zeros_like(token)

    outs = pl.pallas_call(
        body, name=name,
        out_shape=(pltpu.SemaphoreType.DMA((4 * n,)), pltpu.SemaphoreType.DMA((4 * n,)))
        + tuple(pltpu.HBM(g.shape, g.dtype) for g in gs) + tuple(pltpu.HBM(l.shape, l.dtype) for l in lands)
        + (jax.ShapeDtypeStruct((8, 128), F32),),
        in_specs=[HBM] * (2 * n), out_specs=(SEM, SEM) + (HBM,) * (2 * n) + (pl.BlockSpec(memory_space=pltpu.VMEM),),
        input_output_aliases={i: 2 + i for i in range(2 * n)},
        compiler_params=pltpu.CompilerParams(has_side_effects=DATAFLOW),
    )(*[pltpu.with_memory_space_constraint(g, pltpu.HBM) for g in gs],
      *[pltpu.with_memory_space_constraint(l, pltpu.HBM) for l in lands])
    return outs[0], outs[1], outs[2:2 + n], outs[2 + n:2 + 2 * n], outs[-1]


def _sibling_swap_wait(send_sems, recv_sems, gs_thru, lands_thru, after, name):
    n = len(gs_thru)

    def body(*refs):
        g_refs, land_refs = refs[:n], refs[n:2 * n]
        send_sems, recv_sems = refs[2 * n], refs[2 * n + 1]
        for a in range(n):
            for q in range(4):
                cp = pltpu.make_async_remote_copy(
                    src_ref=_sibling_block(g_refs[a], q), dst_ref=land_refs[a].at[q],
                    send_sem=send_sems.at[4 * a + q], recv_sem=recv_sems.at[4 * a + q],
                    device_id=_peer(1), device_id_type=MESH)
                cp.wait_send()
                cp.wait_recv()

    outs = pl.pallas_call(
        body, name=name,
        out_shape=tuple(pltpu.HBM(g.shape, g.dtype) for g in gs_thru)
        + tuple(pltpu.HBM(l.shape, l.dtype) for l in lands_thru),
        in_specs=[HBM] * (2 * n) + [SEM, SEM, ANY], out_specs=(HBM,) * (2 * n),
        input_output_aliases={i: i for i in range(2 * n)},
        compiler_params=pltpu.CompilerParams(has_side_effects=DATAFLOW),
    )(*gs_thru, *lands_thru, send_sems, recv_sems, after)
    return outs[:n], outs[n:]


def _sibling_swap(gs, name):
    n = len(gs)

    def body(*refs):
        g_refs, recv_refs = refs[:n], refs[n:2 * n]
        send_sems, recv_sems = refs[2 * n:]
        ix, iy, ic = lax.axis_index("x"), lax.axis_index("y"), lax.axis_index("c")
        cps = []
        for a in range(n):
            for q in range(4):
                cps.append(pltpu.make_async_remote_copy(
                    src_ref=_sibling_block(g_refs[a], q), dst_ref=recv_refs[a].at[q],
                    send_sem=send_sems.at[4 * a + q], recv_sem=recv_sems.at[4 * a + q],
                    device_id=(ix, iy, 1 - ic), device_id_type=MESH))
        for cp in cps:
            cp.start()
        for cp in cps:
            cp.wait()

    return pl.pallas_call(
        body, name=name,
        out_shape=tuple(jax.ShapeDtypeStruct(
            (4,) + (g.shape[2:] if g.ndim == 4 else (g.shape[0], g.shape[1] // N_DEV)), g.dtype) for g in gs),
        in_specs=[ANY] * n, out_specs=tuple([ANY] * n),
        scratch_shapes=[pltpu.SemaphoreType.DMA((4 * n,)), pltpu.SemaphoreType.DMA((4 * n,))],
    )(*gs)


def _add_halves(g, recv, core, name):
    _, r, c = recv.shape
    br = _row_block(r, 512)
    stacked = g.ndim == 4

    def body(core_ref, g_ref, r_ref, o_ref):
        o_ref[0] = ((g_ref[0, 0] if stacked else g_ref[...]) + r_ref[0]).astype(BF16)

    spec = pl.BlockSpec((1, br, c), lambda i, j, core_ref: (i, j, 0))
    if stacked:
        g_spec = pl.BlockSpec((1, 1, br, c), lambda i, j, core_ref: (i, core_ref[0], j, 0))
    else:
        g_spec = pl.BlockSpec((br, c), lambda i, j, core_ref: (j, 2 * i + core_ref[0]))
    return pl.pallas_call(
        body, name=name, out_shape=jax.ShapeDtypeStruct(recv.shape, BF16),
        grid_spec=pltpu.PrefetchScalarGridSpec(
            num_scalar_prefetch=1, grid=(4, r // br), in_specs=[g_spec, spec], out_specs=spec),
        compiler_params=_params(32),
    )(core, g, recv)


def _adamw(parts, w, m, v, name):
    n_parts, r, c = parts.shape
    if r % 8 == 0:
        br, bc = _row_block(r, 512 if c <= 1024 else 256), c
    else:
        br, bc = r, (256 if c % 256 == 0 else c)

    def body(p_ref, w_ref, m_ref, v_ref, g_out, d_out, m_out, v_out):
        g = p_ref[0].astype(F32)
        for p in range(1, n_parts):
            g = g + p_ref[p].astype(F32)
        m2 = B1 * m_ref[...] + (1.0 - B1) * g
        v2 = B2 * v_ref[...] + (1.0 - B2) * (g * g)
        m_hat = m2 / (1.0 - B1 ** STEP)
        v_hat = v2 / (1.0 - B2 ** STEP)
        g_out[...] = g
        d_out[...] = -LR * (m_hat / (jnp.sqrt(v_hat) + AEPS) + WD * w_ref[...])
        m_out[...] = m2
        v_out[...] = v2

    spec = pl.BlockSpec((br, bc), lambda i, j: (i, j))
    out = jax.ShapeDtypeStruct((r, c), F32)
    return pl.pallas_call(
        body, name=name, out_shape=(out, out, out, out), grid=(r // br, c // bc),
        in_specs=[pl.BlockSpec((n_parts, br, bc), lambda i, j: (0, i, j)), spec, spec, spec],
        out_specs=(spec, spec, spec, spec), compiler_params=_params(40),
    )(parts, w, m, v)


def _atb(a, b, name, after=None):
    t, k1 = a.shape
    k2 = b.shape[1]
    bt = math.gcd(t, 2048)

    def pick(k):
        for cand in (1024, 768, 512, 384, 256, 128):
            if k % cand == 0:
                return cand
        return k

    b1, b2 = pick(k1), pick(k2)

    def body(a_ref, b_ref, *rest):
        o_ref = rest[-1]

        @pl.when(pl.program_id(2) == 0)
        def _():
            o_ref[...] = jnp.zeros_like(o_ref)
        o_ref[...] += _mm_tn(a_ref[...], b_ref[...])

    extra = [] if after is None else [after]
    return pl.pallas_call(
        body, name=name, out_shape=jax.ShapeDtypeStruct((k1, k2), F32), grid=(k1 // b1, k2 // b2, t // bt),
        in_specs=[pl.BlockSpec((bt, b1), lambda i, j, k: (k, i)), pl.BlockSpec((bt, b2), lambda i, j, k: (k, j))]
        + [ANY] * len(extra),
        out_specs=pl.BlockSpec((b1, b2), lambda i, j, k: (i, j)), compiler_params=_params(48),
    )(a, b, *extra)


def _mod_fwd(c_all, w_ada, b_cols):
    def body(c_ref, w_ref, b_ref, o_ref):
        cc = c_ref[...]
        cond = cc * _sigmoid(cc)
        o_ref[...] = _mm(cond, w_ref[...]) + b_ref[...]

    return pl.pallas_call(body, name="mod_fwd", out_shape=jax.ShapeDtypeStruct((c_all.shape[0], w_ada.shape[1]), F32),
                          compiler_params=_params(32))(c_all, w_ada, b_cols)


def _mod_bwd(c_all, dmod_cols, dmod_all):
    def body(c_ref, dc_ref, da_ref, gw_ref, gb_ref):
        cc = c_ref[...]
        cond = cc * _sigmoid(cc)
        gw_ref[...] = _mm_tn(cond, dc_ref[...])
        gb_ref[...] = jnp.sum(da_ref[...], axis=0, keepdims=True)

    return pl.pallas_call(
        body, name="mod_bwd",
        out_shape=(jax.ShapeDtypeStruct((D_MODEL, dmod_cols.shape[1]), F32), jax.ShapeDtypeStruct((1, dmod_all.shape[1]), F32)),
        compiler_params=_params(32))(c_all, dmod_cols, dmod_all)


def _load_once(hbm_ref, vmem_ref, sem):
    @pl.when(pl.program_id(0) == 0)
    def _():
        cp = pltpu.make_async_copy(hbm_ref, vmem_ref, sem)
        cp.start()
        cp.wait()


def _conv_taps(win_ref, w, tb, cols):
    shifted = [win_ref[8 - j:8 - j + tb, cols] for j in range(4)]
    acc = w[3:4] * shifted[0]
    for j in (1, 2, 3):
        acc = acc + w[3 - j:4 - j] * shifted[j]
    return acc, shifted


def _proj_conv_fwd(x2, mod3, w_in_pad, conv_w, conv_b, seq):
    t = x2.shape[0]
    tb = 256
    npb = seq // tb
    cw = 512

    def body(x_ref, mod_ref, w_hbm, cw_ref, cb_ref, z_ref, pre_ref, xbc_ref, dsilu_ref, dt_ref, u5_ref, w_vmem, win, sem):
        _load_once(w_hbm, w_vmem, sem)
        first = (pl.program_id(0) % npb) == 0

        @pl.when(first)
        def _():
            win[0:8, :] = jnp.zeros((8, D_XBC), F32)

        @pl.when(jnp.logical_not(first))
        def _():
            win[0:8, :] = win[tb:tb + 8, :]

        m = mod_ref[0]
        u = (x_ref[...] * (1.0 + m[1:2]) + m[0:1]).astype(BF16)
        z_ref[...] = lax.dot_general(u, w_vmem[0:D_SSD, :], NT, preferred_element_type=F32)
        dt_ref[...] = lax.dot_general(u, w_vmem[D_SSD + D_XBC:D_SSD + D_XBC + DT_PAD, :], NT,
                                      preferred_element_type=F32)
        u5_ref[...] = lax.dot_general(u, w_vmem[D_SSD + D_XBC + DT_PAD:, :], NT, preferred_element_type=F32)
        for k in range(D_XBC // cw):
            cols = slice(k * cw, (k + 1) * cw)
            pre_k = lax.dot_general(u, w_vmem[D_SSD + k * cw:D_SSD + (k + 1) * cw, :], NT,
                                    preferred_element_type=F32)
            win[8:8 + tb, cols] = pre_k
            pre_ref[:, cols] = pre_k
            conv, _ = _conv_taps(win, cw_ref[:, cols], tb, cols)
            conv = conv + cb_ref[:, cols]
            sg = _sigmoid(conv)
            xbc_ref[:, cols] = conv * sg
            dsilu_ref[:, cols] = sg * (1.0 + conv * (1.0 - sg))

    row = lambda w: pl.BlockSpec((tb, w), lambda i: (i, 0))
    return pl.pallas_call(
        body, name="proj_conv_fwd", grid=(t // tb,),
        out_shape=(jax.ShapeDtypeStruct((t, D_SSD), F32), jax.ShapeDtypeStruct((t, D_XBC), F32),
                   jax.ShapeDtypeStruct((t, D_XBC), F32), jax.ShapeDtypeStruct((t, D_XBC), F32),
                   jax.ShapeDtypeStruct((t, DT_PAD), F32), jax.ShapeDtypeStruct((t, D_S5), F32)),
        in_specs=[row(D_MODEL), pl.BlockSpec((1, N_MOD, D_MODEL), lambda i: (i // npb, 0, 0)), ANY,
                  pl.BlockSpec((4, D_XBC), lambda i: (0, 0)), pl.BlockSpec((1, D_XBC), lambda i: (0, 0))],
        out_specs=(row(D_SSD), row(D_XBC), row(D_XBC), row(D_XBC), row(DT_PAD), row(D_S5)),
        scratch_shapes=[pltpu.VMEM((D_INP, D_MODEL), BF16), pltpu.VMEM((tb + 8, D_XBC), F32), pltpu.SemaphoreType.DMA],
        compiler_params=_params(56),
    )(x2, mod3, w_in_pad, conv_w, conv_b)


N_PAIRS = N_HEADS // 2


def _split3(x):
    hi = x.astype(BF16)
    r = x - hi.astype(F32)
    mid = r.astype(BF16)
    lo = (r - mid.astype(F32)).astype(BF16)
    return hi, mid, lo


def _dot3(x, e, dims=(((1,), (0,)), ((), ()))):
    return sum(lax.dot_general(p, e, dims, preferred_element_type=F32) for p in _split3(x))


def _dot3_left(e, x, dims=(((1,), (0,)), ((), ()))):
    return sum(lax.dot_general(e, p, dims, preferred_element_type=F32) for p in _split3(x))


def _head_fold():
    return (jnp.arange(D_SSD)[:, None] // HEADDIM == jnp.arange(128)[None, :]).astype(BF16)


def _ssd_prep(dt_raw, par):
    dtb = par[0:1]
    a = -jnp.exp(par[1:2])
    dt = _softplus(dt_raw + dtb)
    adt = dt * a
    row = lax.broadcasted_iota(jnp.int32, (CHUNK, CHUNK), 0)
    col = lax.broadcasted_iota(jnp.int32, (CHUNK, CHUNK), 1)
    causal = row >= col
    tri = causal.astype(BF16)
    cs = _dot3_left(tri, adt)
    left = col < HEADDIM

    def lanes(v, h):
        return jnp.broadcast_to(v[:, h:h + 1], (CHUNK, 128))

    dt_c, cs_c, pair_cols = [], [], []
    for p in range(N_PAIRS):
        c0, c1 = lanes(cs, 2 * p), lanes(cs, 2 * p + 1)
        pair_cols.append(jnp.concatenate([c0, c1], axis=1))
        cs_c.append(jnp.where(left, c0, c1))
        dt_c.append(jnp.where(left, lanes(dt, 2 * p), lanes(dt, 2 * p + 1)))
    cs_c = jnp.concatenate(cs_c, axis=1)
    dt_c = jnp.concatenate(dt_c, axis=1)
    return dt, a, cs, cs.T, causal, tri, dt_c, jnp.exp(cs_c), jnp.exp(cs_c[CHUNK - 1:CHUNK, :] - cs_c), pair_cols


def _pair_decay(cols, cst, pair, causal2):
    rows = jnp.concatenate([jnp.broadcast_to(cst[2 * pair:2 * pair + 1, :], (CHUNK, CHUNK)),
                            jnp.broadcast_to(cst[2 * pair + 1:2 * pair + 2, :], (CHUNK, CHUNK))], axis=1)
    return jnp.exp(jnp.where(causal2, cols - rows, -jnp.inf))


def _stack_heads(xp, left):
    return jnp.concatenate([jnp.where(left, xp, 0.0), jnp.where(left, 0.0, xp)], axis=0).astype(BF16)


def _ssd_fwd(xbc, z, dt_raw, par, dsk, normw, seq):
    t = xbc.shape[0]
    nc = seq // CHUNK
    n_chunks = t // CHUNK

    def body(xbc_ref, z_ref, dt_ref, par_ref, dsk_ref, nw_ref, yraw_ref, ycat_ref, hprev_ref, h_ref):
        @pl.when(pl.program_id(0) % nc == 0)
        def _():
            h_ref[...] = jnp.zeros_like(h_ref)
        hprev_ref[0] = h_ref[...]
        _, _, cs, cst, causal, _, dt_c, ecs_c, w_c, pair_cols = _ssd_prep(dt_ref[...], par_ref[...])
        cs_last = cs[CHUNK - 1:CHUNK, :]
        causal2 = jnp.concatenate([causal, causal], axis=1)
        left = lax.broadcasted_iota(jnp.int32, (CHUNK, 128), 1) < HEADDIM
        x = xbc_ref[:, 0:D_SSD]
        xdt = x * dt_c
        amat = (w_c * xdt).astype(BF16)
        zz = z_ref[...]
        silu_z = zz * _sigmoid(zz)
        for g in range(N_GROUPS):
            gs = slice(g * GW, (g + 1) * GW)
            bg = xbc_ref[:, D_SSD + g * N_STATE:D_SSD + (g + 1) * N_STATE].astype(BF16)
            cg = xbc_ref[:, D_SSD + (N_GROUPS + g) * N_STATE:D_SSD + (N_GROUPS + g + 1) * N_STATE].astype(BF16)
            scores = lax.dot_general(cg, bg, NT, preferred_element_type=F32)
            scores2 = jnp.concatenate([scores, scores], axis=1)
            hg = h_ref[gs, :]
            p_all = lax.dot_general(cg, hg.astype(BF16), NT, preferred_element_type=F32)
            ys = []
            for q in range(GW // 128):
                pair = g * (GW // 128) + q
                decay = _pair_decay(pair_cols[pair], cst, pair, causal2)
                mcat = (scores2 * decay).astype(BF16)
                ys.append(jnp.dot(mcat, _stack_heads(xdt[:, pair * 128:(pair + 1) * 128], left),
                                  preferred_element_type=F32))
            yg = jnp.concatenate(ys, axis=1) + ecs_c[:, gs] * p_all + x[:, gs] * dsk_ref[:, gs]
            s_new = lax.dot_general(amat[:, gs], bg, TN, preferred_element_type=F32)
            for j in range(HPG):
                hh = g * HPG + j
                js = slice(j * HEADDIM, (j + 1) * HEADDIM)
                h_ref[g * GW + j * HEADDIM:g * GW + (j + 1) * HEADDIM, :] = (
                    hg[js, :] * jnp.exp(cs_last[:, hh:hh + 1]) + s_new[js, :])
            yraw_ref[:, gs] = yg
            v = yg * silu_z[:, gs]
            r = lax.rsqrt(jnp.mean(v * v, axis=-1, keepdims=True) + EPS)
            ycat_ref[:, gs] = (v * r * nw_ref[:, gs]).astype(BF16)

    row = lambda w: pl.BlockSpec((CHUNK, w), lambda i: (i, 0))
    full = lambda s: pl.BlockSpec(s, lambda i: (0,) * len(s))
    return pl.pallas_call(
        body, name="ssd_fwd", grid=(n_chunks,),
        out_shape=(jax.ShapeDtypeStruct((t, D_SSD), F32), jax.ShapeDtypeStruct((t, D_SSD + D_S5), BF16),
                   jax.ShapeDtypeStruct((n_chunks, D_SSD, N_STATE), F32)),
        in_specs=[row(D_XBC), row(D_SSD), row(DT_PAD), full((8, 128)), full((1, D_SSD)), full((1, D_SSD))],
        out_specs=(row(D_SSD), row(D_SSD), pl.BlockSpec((1, D_SSD, N_STATE), lambda i: (i, 0, 0))),
        scratch_shapes=[pltpu.VMEM((D_SSD, N_STATE), F32)],
        compiler_params=_params(40),
    )(xbc, z, dt_raw, par, dsk, normw)


S5_CW = 512
S5_BLOCKS = 4


def _tile_scan(in_re, in_im, out_re, out_im, carry_re, carry_im, pw_re, pw_im, n_tiles, reverse):
    steps = (1, 2, 4)
    for cc in range(S5_N // S5_CW):
        cols = slice(cc * S5_CW, (cc + 1) * S5_CW)
        a_re, a_im = pw_re[:, cols], pw_im[:, cols]
        rid = lax.broadcasted_iota(jnp.int32, (8, S5_CW), 0)
        pows = []
        for d in steps:
            k = 8 - d if reverse else d - 1
            keep = (rid < 8 - d) if reverse else (rid >= d)
            pows.append((jnp.where(keep, pw_re[k:k + 1, cols], 0.0), jnp.where(keep, pw_im[k:k + 1, cols], 0.0)))

        def tile(i, carry, cols=cols, pows=pows, a_re=a_re, a_im=a_im):
            r = (n_tiles - 1 - i) if reverse else i
            rows = pl.ds(pl.multiple_of(r * 8, 8), 8)
            xr, xi = in_re[rows, cols], in_im[rows, cols]
            for (pr, pi), d in zip(pows, steps):
                shift = 8 - d if reverse else d
                sr, si = pltpu.roll(xr, shift, axis=0), pltpu.roll(xi, shift, axis=0)
                xr, xi = xr + pr * sr - pi * si, xi + pr * si + pi * sr
            cr, ci = carry
            xr, xi = xr + a_re * cr - a_im * ci, xi + a_re * ci + a_im * cr
            out_re[rows, cols] = xr
            out_im[rows, cols] = xi
            edge = slice(0, 1) if reverse else slice(7, 8)
            return (jnp.broadcast_to(xr[edge], (8, S5_CW)), jnp.broadcast_to(xi[edge], (8, S5_CW)))

        c0 = (jnp.broadcast_to(carry_re[0:1, cols], (8, S5_CW)), jnp.broadcast_to(carry_im[0:1, cols], (8, S5_CW)))
        cr, ci = lax.fori_loop(0, n_tiles, tile, c0, unroll=True)
        carry_re[:, cols] = cr
        carry_im[:, cols] = ci


def _s5_params_math(ar, ai, ldt, br, bi):
    dt = jnp.exp(ldt)
    mag = jnp.exp(ar * dt)
    ang = ai * dt
    ab_re = mag * jnp.cos(ang)
    ab_im = mag * jnp.sin(ang)
    den = ar * ar + ai * ai
    n_re = ab_re - 1.0
    coef_re = (n_re * ar + ab_im * ai) / den
    coef_im = (ab_im * ar - n_re * ai) / den
    bb_re = coef_re * br - coef_im * bi
    bb_im = coef_re * bi + coef_im * br
    return ab_re, ab_im, bb_re, bb_im


def _s5_params_fwd(ar, ai, ldt, br, bi):
    def body(ar_ref, ai_ref, ldt_ref, br_ref, bi_ref, bbr_ref, bbi_ref, pfr_ref, pfi_ref, prr_ref, pri_ref):
        ab_re, ab_im, bb_re, bb_im = _s5_params_math(ar_ref[...], ai_ref[...], ldt_ref[...], br_ref[...], bi_ref[...])
        bbr_ref[...] = bb_re
        bbi_ref[...] = bb_im
        pr, pi = ab_re, ab_im
        for k in range(8):
            pfr_ref[k:k + 1, :] = pr
            pfi_ref[k:k + 1, :] = pi
            prr_ref[7 - k:8 - k, :] = pr
            pri_ref[7 - k:8 - k, :] = -pi
            pr, pi = pr * ab_re - pi * ab_im, pr * ab_im + pi * ab_re

    b16 = jax.ShapeDtypeStruct((S5_CH, S5_N), F32)
    p8 = jax.ShapeDtypeStruct((8, S5_N), F32)
    return pl.pallas_call(body, name="s5_params_fwd", out_shape=(b16, b16, p8, p8, p8, p8),
                          compiler_params=_params(32))(ar, ai, ldt, br, bi)


def _s5_params_bwd(ar, ai, ldt, br, bi, d_ab_re, d_ab_im, d_bb_re, d_bb_im):
    def body(ar_ref, ai_ref, ldt_ref, br_ref, bi_ref, dar_ref, dai_ref, dbr_ref, dbi_ref,
             gar_ref, gai_ref, gldt_ref, gbr_ref, gbi_ref):
        _, vjp = jax.vjp(_s5_params_math, ar_ref[...], ai_ref[...], ldt_ref[...], br_ref[...], bi_ref[...])
        g_ar, g_ai, g_ldt, g_br, g_bi = vjp((dar_ref[...], dai_ref[...], dbr_ref[...], dbi_ref[...]))
        gar_ref[...] = g_ar
        gai_ref[...] = g_ai
        gbr_ref[...] = g_br
        gbi_ref[...] = g_bi
        lane = lax.broadcasted_iota(jnp.int32, (S5_N, 128), 0) // S5_P
        grp = lax.broadcasted_iota(jnp.int32, (S5_N, 128), 1)
        fold = (lane == grp).astype(F32)
        gldt_ref[...] = jnp.dot(g_ldt, fold, preferred_element_type=F32, precision=HIGHEST)

    v1 = jax.ShapeDtypeStruct((1, S5_N), F32)
    b16 = jax.ShapeDtypeStruct((S5_CH, S5_N), F32)
    return pl.pallas_call(body, name="s5_params_bwd",
                          out_shape=(v1, v1, jax.ShapeDtypeStruct((1, 128), F32), b16, b16),
                          compiler_params=_params(32))(ar, ai, ldt, br, bi, d_ab_re, d_ab_im, d_bb_re, d_bb_im)


def _s5_fwd(u5, bb_re, bb_im, cc_re, cc_im, pf_re, pf_im, s5d, w_glu, b_glu, ycat, seq):
    t = u5.shape[0]
    tb = 256
    npb = seq // tb

    def body(u_ref, bbr_ref, bbi_ref, ccr_ref, cci_ref, pfr_ref, pfi_ref, d_ref, wg_ref, bg_ref, ycat_hbm,
             sre_ref, sim_ref, ypre_ref, y5_ref, bur, bui, car, cai):
        del ycat_hbm

        @pl.when(pl.program_id(0) % npb == 0)
        def _():
            car[...] = jnp.zeros_like(car)
            cai[...] = jnp.zeros_like(cai)
        u = u_ref[...]
        ub = u.astype(BF16)
        for j in range(S5_BLOCKS):
            ch, st = slice(j * 128, (j + 1) * 128), slice(j * 512, (j + 1) * 512)
            bur[:, st] = jnp.dot(ub[:, ch], bbr_ref[j], preferred_element_type=F32)
            bui[:, st] = jnp.dot(ub[:, ch], bbi_ref[j], preferred_element_type=F32)
        _tile_scan(bur, bui, sre_ref, sim_ref, car, cai, pfr_ref, pfi_ref, tb // 8, reverse=False)
        cs_y = []
        for j in range(S5_BLOCKS):
            st = slice(j * 512, (j + 1) * 512)
            cs_y.append(_mm(sre_ref[:, st], ccr_ref[j]) - _mm(sim_ref[:, st], cci_ref[j]))
        ypre = jnp.concatenate(cs_y, axis=1) + u * d_ref[...]
        ypre_ref[...] = ypre
        yg = _gelu(ypre)
        y5_ref[...] = (yg * _sigmoid(_mm(yg, wg_ref[...]) + bg_ref[...])).astype(BF16)

    row = lambda w: pl.BlockSpec((tb, w), lambda i: (i, 0))
    full = lambda a: pl.BlockSpec(a.shape, lambda i: (0,) * a.ndim)
    return pl.pallas_call(
        body, name="s5_fwd", grid=(t // tb,),
        out_shape=(jax.ShapeDtypeStruct((t, S5_N), F32), jax.ShapeDtypeStruct((t, S5_N), F32),
                   jax.ShapeDtypeStruct((t, D_S5), F32), jax.ShapeDtypeStruct(ycat.shape, BF16)),
        in_specs=[row(D_S5), full(bb_re), full(bb_im), full(cc_re), full(cc_im), full(pf_re), full(pf_im),
                  full(s5d), full(w_glu), full(b_glu), ANY],
        out_specs=(row(S5_N), row(S5_N), row(D_S5), pl.BlockSpec((tb, D_S5), lambda i: (i, D_SSD // D_S5))),
        input_output_aliases={10: 3},
        scratch_shapes=[pltpu.VMEM((tb, S5_N), F32), pltpu.VMEM((tb, S5_N), F32),
                        pltpu.VMEM((8, S5_N), F32), pltpu.VMEM((8, S5_N), F32)],
        compiler_params=_params(48),
    )(u5, bb_re, bb_im, cc_re, cc_im, pf_re, pf_im, s5d, w_glu, b_glu, ycat)


def _layer_norm(r, g, b):
    mu = jnp.mean(r, axis=-1, keepdims=True)
    xc = r - mu
    rstd = lax.rsqrt(jnp.mean(xc * xc, axis=-1, keepdims=True) + EPS)
    xhat = xc * rstd
    return xhat * g + b, xhat, rstd


def _layer_norm_bwd(dy, xhat, rstd, g):
    dxhat = dy * g
    return rstd * (dxhat - jnp.mean(dxhat, axis=-1, keepdims=True)
                   - xhat * jnp.mean(dxhat * xhat, axis=-1, keepdims=True))


def _out_ln1(ycat, x2, mod3, w_out, ln1, seq):
    t = x2.shape[0]
    tb = 512
    npb = seq // tb

    def body(y_ref, x_ref, mod_ref, w_ref, ln_ref, mix_ref, x1_ref):
        m = mod_ref[0]
        mix = jnp.dot(y_ref[...], w_ref[...], preferred_element_type=F32)
        mix_ref[...] = mix
        r1 = ALPHA * x_ref[...] + (1.0 + m[2:3]) * mix
        x1_ref[...] = _layer_norm(r1, ln_ref[0:1], ln_ref[1:2])[0]

    row = lambda w: pl.BlockSpec((tb, w), lambda i: (i, 0))
    return pl.pallas_call(
        body, name="out_ln1", grid=(t // tb,),
        out_shape=(jax.ShapeDtypeStruct((t, D_MODEL), F32), jax.ShapeDtypeStruct((t, D_MODEL), F32)),
        in_specs=[row(D_SSD + D_S5), row(D_MODEL), pl.BlockSpec((1, N_MOD, D_MODEL), lambda i: (i // npb, 0, 0)),
                  pl.BlockSpec(w_out.shape, lambda i: (0, 0)), pl.BlockSpec(ln1.shape, lambda i: (0, 0))],
        out_specs=(row(D_MODEL), row(D_MODEL)), compiler_params=_params(48),
    )(ycat, x2, mod3, w_out, ln1)


def _mlp_fwd_bwd(x1, tgt, mod3, w1, w2, vec1, b1, seq):
    t = x1.shape[0]
    tb = 256
    npb = seq // tb
    n_fb, _, fb = w1.shape

    def body(x1_ref, tgt_ref, mod_ref, w1_hbm, w2_hbm, v_ref, b1_ref,
             dx1_ref, u2_ref, h_ref, dhp_ref, do_ref, gacc_ref, db1_ref, bacc_ref, w1_v, w2_v, sem1, sem2):
        i = pl.program_id(0)
        @pl.when(i == 0)
        def _():
            cps = [pltpu.make_async_copy(w1_hbm.at[k], w1_v.at[:, k * fb:(k + 1) * fb], sem1.at[k])
                   for k in range(n_fb)]
            for cp in cps:
                cp.start()
            for cp in cps:
                cp.wait()
        _load_once(w2_hbm, w2_v, sem2)

        @pl.when(i == 0)
        def _():
            gacc_ref[...] = jnp.zeros_like(gacc_ref)
            db1_ref[...] = jnp.zeros_like(db1_ref)

        @pl.when(i % npb == 0)
        def _():
            bacc_ref[...] = jnp.zeros_like(bacc_ref)

        m = mod_ref[0]
        sh2, sc2, g2 = m[3:4], m[4:5], m[5:6]
        x1v = x1_ref[...]
        u2 = (x1v * (1.0 + sc2) + sh2).astype(BF16)
        u2_ref[...] = u2
        hr = jnp.maximum(jnp.dot(u2, w1_v[...], preferred_element_type=F32) + b1_ref[...], 0.0)
        hb = (hr * hr).astype(BF16)
        h_ref[...] = hb
        o = jnp.dot(hb, w2_v[...], preferred_element_type=F32) + v_ref[0:1]
        r2 = ALPHA * x1v + (1.0 + g2) * o
        y, xhat, rstd = _layer_norm(r2, v_ref[1:2], v_ref[2:3])
        err = y - tgt_ref[...]
        dy = err * (1.0 / D_MODEL)
        dr2 = _layer_norm_bwd(dy, xhat, rstd, v_ref[1:2])
        do = (1.0 + g2) * dr2
        dob = do.astype(BF16)
        do_ref[...] = dob
        gacc_ref[0:1, :] += jnp.sum(dy * xhat, axis=0, keepdims=True)
        gacc_ref[1:2, :] += jnp.sum(dy, axis=0, keepdims=True)
        gacc_ref[2:3, :] += jnp.sum(do, axis=0, keepdims=True)
        gacc_ref[3:4, :] += jnp.sum(err * err, axis=0, keepdims=True)
        dhpre = lax.dot_general(dob, w2_v[...], NT, preferred_element_type=F32) * (2.0 * hr)
        dhpb = dhpre.astype(BF16)
        dhp_ref[...] = dhpb
        db1_ref[...] += jnp.sum(dhpre, axis=0, keepdims=True)
        du2 = lax.dot_general(dhpb, w1_v[...], NT, preferred_element_type=F32)
        dx1_ref[...] = ALPHA * dr2 + du2 * (1.0 + sc2)
        bacc_ref[0, 0:1, :] += jnp.sum(du2, axis=0, keepdims=True)
        bacc_ref[0, 1:2, :] += jnp.sum(du2 * x1v, axis=0, keepdims=True)
        bacc_ref[0, 2:3, :] += jnp.sum(dr2 * o, axis=0, keepdims=True)

    row = lambda w: pl.BlockSpec((tb, w), lambda i: (i, 0))
    return pl.pallas_call(
        body, name="mlp_fwd_bwd", grid=(t // tb,),
        out_shape=(jax.ShapeDtypeStruct((t, D_MODEL), F32), jax.ShapeDtypeStruct((t, D_MODEL), BF16),
                   jax.ShapeDtypeStruct((t, D_FF), BF16), jax.ShapeDtypeStruct((t, D_FF), BF16),
                   jax.ShapeDtypeStruct((t, D_MODEL), BF16), jax.ShapeDtypeStruct((8, D_MODEL), F32),
                   jax.ShapeDtypeStruct((1, D_FF), F32), jax.ShapeDtypeStruct((t // seq, 8, D_MODEL), F32)),
        in_specs=[row(D_MODEL), row(D_MODEL), pl.BlockSpec((1, N_MOD, D_MODEL), lambda i: (i // npb, 0, 0)), ANY, ANY,
                  pl.BlockSpec(vec1.shape, lambda i: (0, 0)), pl.BlockSpec(b1.shape, lambda i: (0, 0))],
        out_specs=(row(D_MODEL), row(D_MODEL), row(D_FF), row(D_FF), row(D_MODEL),
                   pl.BlockSpec((8, D_MODEL), lambda i: (0, 0)), pl.BlockSpec((1, D_FF), lambda i: (0, 0)),
                   pl.BlockSpec((1, 8, D_MODEL), lambda i: (i // npb, 0, 0))),
        scratch_shapes=[pltpu.VMEM((D_MODEL, n_fb * fb), BF16), pltpu.VMEM((D_FF, D_MODEL), BF16),
                        pltpu.SemaphoreType.DMA((n_fb,)), pltpu.SemaphoreType.DMA],
        compiler_params=_params(60),
    )(x1, tgt, mod3, w1, w2, vec1, b1)


def _ln1_out_bwd(dx1, x2, mix, mod3, w_out, ln1, seq):
    t = x2.shape[0]
    tb = 512
    npb = seq // tb

    def body(dx1_ref, x_ref, mix_ref, mod_ref, w_ref, ln_ref, dmix_ref, dxa_ref, dys_ref, dy5_ref, gacc_ref, bacc_ref):
        i = pl.program_id(0)

        @pl.when(i == 0)
        def _():
            gacc_ref[...] = jnp.zeros_like(gacc_ref)

        @pl.when(i % npb == 0)
        def _():
            bacc_ref[...] = jnp.zeros_like(bacc_ref)

        m = mod_ref[0]
        mix = mix_ref[...]
        r1 = ALPHA * x_ref[...] + (1.0 + m[2:3]) * mix
        _, xhat, rstd = _layer_norm(r1, ln_ref[0:1], ln_ref[1:2])
        dx1v = dx1_ref[...]
        dr1 = _layer_norm_bwd(dx1v, xhat, rstd, ln_ref[0:1])
        gacc_ref[0:1, :] += jnp.sum(dx1v * xhat, axis=0, keepdims=True)
        gacc_ref[1:2, :] += jnp.sum(dx1v, axis=0, keepdims=True)
        bacc_ref[0, 0:1, :] += jnp.sum(dr1 * mix, axis=0, keepdims=True)
        dmix = ((1.0 + m[2:3]) * dr1).astype(BF16)
        dmix_ref[...] = dmix
        dxa_ref[...] = ALPHA * dr1
        dys_ref[...] = lax.dot_general(dmix, w_ref[0:D_SSD, :], NT, preferred_element_type=F32)
        dy5_ref[...] = lax.dot_general(dmix, w_ref[D_SSD:, :], NT, preferred_element_type=F32)

    row = lambda w: pl.BlockSpec((tb, w), lambda i: (i, 0))
    return pl.pallas_call(
        body, name="ln1_out_bwd", grid=(t // tb,),
        out_shape=(jax.ShapeDtypeStruct((t, D_MODEL), BF16), jax.ShapeDtypeStruct((t, D_MODEL), F32),
                   jax.ShapeDtypeStruct((t, D_SSD), F32), jax.ShapeDtypeStruct((t, D_S5), F32),
                   jax.ShapeDtypeStruct((8, D_MODEL), F32), jax.ShapeDtypeStruct((t // seq, 8, D_MODEL), F32)),
        in_specs=[row(D_MODEL), row(D_MODEL), row(D_MODEL), pl.BlockSpec((1, N_MOD, D_MODEL), lambda i: (i // npb, 0, 0)),
                  pl.BlockSpec(w_out.shape, lambda i: (0, 0)), pl.BlockSpec(ln1.shape, lambda i: (0, 0))],
        out_specs=(row(D_MODEL), row(D_MODEL), row(D_SSD), row(D_S5), pl.BlockSpec((8, D_MODEL), lambda i: (0, 0)),
                   pl.BlockSpec((1, 8, D_MODEL), lambda i: (i // npb, 0, 0))),
        compiler_params=_params(48),
    )(dx1, x2, mix, mod3, w_out, ln1)


def _s5_bwd(dy5, ypre, u5, s_re, s_im, bb_re, bb_im, cc_re, cc_im, pr_re, pr_im, s5d, w_glu, b_glu, seq):
    t = u5.shape[0]
    tb = 256
    npb = seq // tb
    n_blocks = t // tb

    def blk(i):
        return (i // npb) * npb + (npb - 1 - i % npb)

    def body(dy_ref, ypre_ref, u_ref, sre_ref, sim_ref, hre_ref, him_ref, bbr_ref, bbi_ref, ccr_ref, cci_ref,
             prr_ref, pri_ref, d_ref, wg_ref, bg_ref,
             du_ref, vacc_ref, sacc_ref, dcc_ref, dbb_ref, dwg_ref, dsr, dsi, gr, gi, car, cai):
        i = pl.program_id(0)

        @pl.when(i == 0)
        def _():
            for acc in (vacc_ref, sacc_ref, dcc_ref, dbb_ref, dwg_ref):
                acc[...] = jnp.zeros_like(acc)

        @pl.when(i % npb == 0)
        def _():
            car[...] = jnp.zeros_like(car)
            cai[...] = jnp.zeros_like(cai)

        dy = dy_ref[...]
        ypre = ypre_ref[...]
        u = u_ref[...]
        ub = u.astype(BF16)
        yg = _gelu(ypre)
        sg = _sigmoid(_mm(yg, wg_ref[...]) + bg_ref[...])
        dq = dy * yg * sg * (1.0 - sg)
        dqb = dq.astype(BF16)
        dyg = dy * sg + lax.dot_general(dqb, wg_ref[...], NT, preferred_element_type=F32)
        dyp = dyg * _gelu_grad(ypre)
        dypb = dyp.astype(BF16)
        dwg_ref[...] += lax.dot_general(yg.astype(BF16), dqb, TN, preferred_element_type=F32)
        blocks = [(slice(j * 128, (j + 1) * 128), slice(j * 512, (j + 1) * 512)) for j in range(S5_BLOCKS)]
        for j, (ch, st) in enumerate(blocks):
            dsr[:, st] = lax.dot_general(dypb[:, ch], ccr_ref[j], NT, preferred_element_type=F32)
            dsi[:, st] = -lax.dot_general(dypb[:, ch], cci_ref[j], NT, preferred_element_type=F32)
        _tile_scan(dsr, dsi, gr, gi, car, cai, prr_ref, pri_ref, tb // 8, reverse=True)
        g_re, g_im = gr[...], gi[...]
        first_rows = (i % npb) == npb - 1
        hre = jnp.where(first_rows, 0.0, hre_ref[...])
        him = jnp.where(first_rows, 0.0, him_ref[...])
        s_re_v, s_im_v = sre_ref[...], sim_ref[...]
        sp_re = pltpu.roll(jnp.concatenate([hre, s_re_v], axis=0), 1, axis=0)[8:8 + tb]
        sp_im = pltpu.roll(jnp.concatenate([him, s_im_v], axis=0), 1, axis=0)[8:8 + tb]
        vacc_ref[0:1, :] += jnp.sum(g_re * sp_re + g_im * sp_im, axis=0, keepdims=True)
        vacc_ref[1:2, :] += jnp.sum(g_im * sp_re - g_re * sp_im, axis=0, keepdims=True)
        grb, gib = g_re.astype(BF16), g_im.astype(BF16)
        srb, sib = s_re_v.astype(BF16), s_im_v.astype(BF16)
        du_cols = []
        for j, (ch, st) in enumerate(blocks):
            dcc_ref[j] += lax.dot_general(srb[:, st], dypb[:, ch], TN, preferred_element_type=F32)
            dcc_ref[S5_BLOCKS + j] -= lax.dot_general(sib[:, st], dypb[:, ch], TN, preferred_element_type=F32)
            dbb_ref[j] += lax.dot_general(ub[:, ch], grb[:, st], TN, preferred_element_type=F32)
            dbb_ref[S5_BLOCKS + j] += lax.dot_general(ub[:, ch], gib[:, st], TN, preferred_element_type=F32)
            du_cols.append(lax.dot_general(grb[:, st], bbr_ref[j], NT, preferred_element_type=F32)
                           + lax.dot_general(gib[:, st], bbi_ref[j], NT, preferred_element_type=F32))
        du_ref[...] = jnp.concatenate(du_cols, axis=1) + dyp * d_ref[...]
        sacc_ref[0:1, :] += jnp.sum(dyp * u, axis=0, keepdims=True)
        sacc_ref[1:2, :] += jnp.sum(dq, axis=0, keepdims=True)

    row = lambda w: pl.BlockSpec((tb, w), lambda i: (blk(i), 0))
    halo = pl.BlockSpec((8, S5_N), lambda i: (jnp.maximum(blk(i) * (tb // 8) - 1, 0), 0))
    full = lambda a: pl.BlockSpec(a.shape, lambda i: (0,) * a.ndim)
    acc = lambda s: pl.BlockSpec(s, lambda i: (0,) * len(s))
    acc_shapes = [(8, S5_N), (8, D_S5), (2 * S5_BLOCKS, 512, 128), (2 * S5_BLOCKS, 128, 512), (D_S5, D_S5)]
    return pl.pallas_call(
        body, name="s5_bwd", grid=(n_blocks,),
        out_shape=(jax.ShapeDtypeStruct((t, D_S5), F32),) + tuple(jax.ShapeDtypeStruct(s, F32) for s in acc_shapes),
        in_specs=[row(D_S5), row(D_S5), row(D_S5), row(S5_N), row(S5_N), halo, halo, full(bb_re), full(bb_im),
                  full(cc_re), full(cc_im), full(pr_re), full(pr_im), full(s5d), full(w_glu), full(b_glu)],
        out_specs=(row(D_S5),) + tuple(acc(s) for s in acc_shapes),
        scratch_shapes=[pltpu.VMEM((tb, S5_N), F32), pltpu.VMEM((tb, S5_N), F32), pltpu.VMEM((tb, S5_N), F32),
                        pltpu.VMEM((tb, S5_N), F32), pltpu.VMEM((8, S5_N), F32), pltpu.VMEM((8, S5_N), F32)],
        compiler_params=_params(56),
    )(dy5, ypre, u5, s_re, s_im, s_re, s_im, bb_re, bb_im, cc_re, cc_im, pr_re, pr_im, s5d, w_glu, b_glu)


def _ssd_bwd(dyssd, yraw, z, xbc, dt_raw, hprev, par, dsk, normw, seq):
    t = xbc.shape[0]
    nc = seq // CHUNK
    n_chunks = t // CHUNK
    fold = _head_fold()

    def blk(i):
        return (i // nc) * nc + (nc - 1 - i % nc)

    def body(dy_ref, yraw_ref, z_ref, xbc_ref, dt_ref, hprev_ref, par_ref, dsk_ref, nw_ref, fold_ref,
             dxbc_ref, dz_ref, ddt_ref, dpar_ref, cacc_ref, dh_ref, dyr_ref):
        i = pl.program_id(0)

        @pl.when(i == 0)
        def _():
            dpar_ref[...] = jnp.zeros_like(dpar_ref)
            cacc_ref[...] = jnp.zeros_like(cacc_ref)

        @pl.when(i % nc == 0)
        def _():
            dh_ref[...] = jnp.zeros_like(dh_ref)

        zz = z_ref[...]
        sz = _sigmoid(zz)
        silu_z = zz * sz
        yraw = yraw_ref[...]
        for g in range(N_GROUPS):
            sl = slice(g * GW, (g + 1) * GW)
            v = yraw[:, sl] * silu_z[:, sl]
            r = lax.rsqrt(jnp.mean(v * v, axis=-1, keepdims=True) + EPS)
            dyg = dy_ref[:, sl]
            cacc_ref[1:2, sl] += jnp.sum(dyg * v * r, axis=0, keepdims=True)
            dyw = dyg * nw_ref[:, sl]
            dv = r * dyw - v * (r * r * r) * jnp.mean(dyw * v, axis=-1, keepdims=True)
            dyr_ref[:, sl] = dv * silu_z[:, sl]
            dz_ref[:, sl] = dv * yraw[:, sl] * (sz[:, sl] * (1.0 + zz[:, sl] * (1.0 - sz[:, sl])))

        dt, a, cs, cst, causal, tri, dt_c, ecs_c, w_c, pair_cols = _ssd_prep(dt_ref[...], par_ref[...])
        cs_last = cs[CHUNK - 1:CHUNK, :]
        causal2 = jnp.concatenate([causal, causal], axis=1)
        lane = lax.broadcasted_iota(jnp.int32, (CHUNK, 128), 1)
        left = lane < HEADDIM
        lane1 = lax.broadcasted_iota(jnp.int32, (1, 128), 1)
        x = xbc_ref[:, 0:D_SSD]
        xdt = x * dt_c
        dyr = dyr_ref[...]
        dyrb = dyr.astype(BF16)
        cacc_ref[0:1, :] += jnp.sum(dyr * x, axis=0, keepdims=True)
        dlast = jnp.zeros((1, 128), F32)
        dxdt_cols, diag_all, dww_cols = [], [], []
        for g in range(N_GROUPS):
            gs = slice(g * GW, (g + 1) * GW)
            b_sl = slice(D_SSD + g * N_STATE, D_SSD + (g + 1) * N_STATE)
            c_sl = slice(D_SSD + (N_GROUPS + g) * N_STATE, D_SSD + (N_GROUPS + g + 1) * N_STATE)
            bg = xbc_ref[:, b_sl].astype(BF16)
            cg = xbc_ref[:, c_sl].astype(BF16)
            scores = lax.dot_general(cg, bg, NT, preferred_element_type=F32)
            scores2 = jnp.concatenate([scores, scores], axis=1)
            hg = hprev_ref[0, gs, :]
            hgb = hg.astype(BF16)
            dhg = dh_ref[gs, :]
            dhgb = dhg.astype(BF16)
            q_all = lax.dot_general(bg, dhgb, NT, preferred_element_type=F32)
            dscores = jnp.zeros((CHUNK, CHUNK), F32)
            diag_cols = []
            for q in range(GW // 128):
                pair = g * (GW // 128) + q
                ps = slice(pair * 128, (pair + 1) * 128)
                decay = _pair_decay(pair_cols[pair], cst, pair, causal2)
                mcat = (scores2 * decay).astype(BF16)
                dyp = dyrb[:, ps]
                dm = lax.dot_general(dyp, _stack_heads(xdt[:, ps], left), NT, preferred_element_type=F32)
                dmd = dm * decay
                dscores = dscores + dmd[:, 0:CHUNK] + dmd[:, CHUNK:]
                rr = lax.dot_general(mcat, dyp, TN, preferred_element_type=F32)
                diag_cols.append(jnp.where(left, rr[0:CHUNK], rr[CHUNK:]))
            wq = w_c[:, gs] * q_all
            diag_g = jnp.concatenate(diag_cols, axis=1)
            diag_all.append(diag_g)
            dxdt_cols.append(diag_g + wq)
            dww_cols.append(wq * xdt[:, gs])
            dp = (ecs_c[:, gs] * dyr[:, gs]).astype(BF16)
            amat = (w_c[:, gs] * xdt[:, gs]).astype(BF16)
            dsb = dscores.astype(BF16)
            dxbc_ref[:, c_sl] = (jnp.dot(dsb, bg, preferred_element_type=F32)
                                 + jnp.dot(dp, hgb, preferred_element_type=F32))
            dxbc_ref[:, b_sl] = (lax.dot_general(dsb, cg, TN, preferred_element_type=F32)
                                 + jnp.dot(amat, dhgb, preferred_element_type=F32))
            dh_in = lax.dot_general(dp, cg, TN, preferred_element_type=F32)
            for j in range(HPG):
                hh = g * HPG + j
                js = slice(j * HEADDIM, (j + 1) * HEADDIM)
                ecl = jnp.exp(cs_last[:, hh:hh + 1])
                dlast = dlast + jnp.where(lane1 == hh, ecl * jnp.sum(dhg[js, :] * hg[js, :]), 0.0)
                dh_ref[g * GW + j * HEADDIM:g * GW + (j + 1) * HEADDIM, :] = ecl * dhg[js, :] + dh_in[js, :]
        dxdt = jnp.concatenate(dxdt_cols, axis=1)
        dxbc_ref[:, 0:D_SSD] = dxdt * dt_c + dyr * dsk_ref[...]
        dww = _mm(jnp.concatenate(dww_cols, axis=1), fold_ref[...])
        dcs = _dot3(dyrb.astype(F32) * (yraw - x * dsk_ref[...])
                    - xdt.astype(BF16).astype(F32) * jnp.concatenate(diag_all, axis=1), fold_ref[...]) - dww
        rowid = lax.broadcasted_iota(jnp.int32, (CHUNK, 128), 0)
        dcs = dcs + jnp.where(rowid == CHUNK - 1, jnp.sum(dww, axis=0, keepdims=True) + dlast, 0.0)
        dadt = _dot3_left(tri, dcs, TN)
        ddt = _mm(dxdt * x, fold_ref[...]) + dadt * a
        da = jnp.sum(dadt * dt, axis=0, keepdims=True)
        ddt_raw = ddt * _sigmoid(dt_ref[...] + par_ref[0:1])
        ddt_raw = jnp.where(lane < N_HEADS, ddt_raw, 0.0)
        ddt_ref[...] = ddt_raw
        dpar_ref[0:1, :] += jnp.sum(ddt_raw, axis=0, keepdims=True)
        dpar_ref[1:2, :] += jnp.where(lane1 < N_HEADS, da * a, 0.0)

    row = lambda w: pl.BlockSpec((CHUNK, w), lambda i: (blk(i), 0))
    full = lambda s: pl.BlockSpec(s, lambda i: (0,) * len(s))
    return pl.pallas_call(
        body, name="ssd_bwd", grid=(n_chunks,),
        out_shape=(jax.ShapeDtypeStruct((t, D_XBC), F32), jax.ShapeDtypeStruct((t, D_SSD), F32),
                   jax.ShapeDtypeStruct((t, DT_PAD), F32), jax.ShapeDtypeStruct((8, 128), F32),
                   jax.ShapeDtypeStruct((8, D_SSD), F32)),
        in_specs=[row(D_SSD), row(D_SSD), row(D_SSD), row(D_XBC), row(DT_PAD),
                  pl.BlockSpec((1, D_SSD, N_STATE), lambda i: (blk(i), 0, 0)),
                  full((8, 128)), full((1, D_SSD)), full((1, D_SSD)), full(fold.shape)],
        out_specs=(row(D_XBC), row(D_SSD), row(DT_PAD), full((8, 128)), full((8, D_SSD))),
        scratch_shapes=[pltpu.VMEM((D_SSD, N_STATE), F32), pltpu.VMEM((CHUNK, D_SSD), F32)],
        compiler_params=_params(48),
    )(dyssd, yraw, z, xbc, dt_raw, hprev, par, dsk, normw, fold)


def _conv_proj_bwd(dz, dxbc, dsilu, xbc_pre, ddt, du5, x2, dxa, mod3, conv_w, w_in_pad, seq):
    t = x2.shape[0]
    tb = 256
    npb = seq // tb
    n_blocks = t // tb
    cw = 512

    def blk(i):
        return (i // npb) * npb + (npb - 1 - i % npb)

    def body(dz_ref, d_ref, ds_ref, cur_ref, halo_ref, ddt_ref, du5_ref, x_ref, dxa_ref, mod_ref, cw_ref, w_hbm,
             gx_ref, u_ref, dxp_ref, bacc_ref, acc_ref, w_vmem, win_x, win_d, sem):
        i = pl.program_id(0)
        _load_once(w_hbm, w_vmem, sem)

        @pl.when(i == 0)
        def _():
            acc_ref[...] = jnp.zeros_like(acc_ref)

        @pl.when(i % npb == 0)
        def _():
            bacc_ref[...] = jnp.zeros_like(bacc_ref)
            win_d[tb:tb + 8, :] = jnp.zeros((8, D_XBC), F32)

        @pl.when(i % npb != 0)
        def _():
            win_d[tb:tb + 8, :] = win_d[0:8, :]

        first_rows = (i % npb) == npb - 1
        win_x[0:8, :] = jnp.where(first_rows, 0.0, halo_ref[...])
        win_x[8:8 + tb, :] = cur_ref[...]
        w = cw_ref[...]
        for k in range(D_XBC // cw):
            cols = slice(k * cw, (k + 1) * cw)
            dpre = d_ref[:, cols] * ds_ref[:, cols]
            win_d[0:tb, cols] = dpre
            for j in range(4):
                acc_ref[3 - j:4 - j, cols] += jnp.sum(dpre * win_x[8 - j:8 - j + tb, cols], axis=0, keepdims=True)
            acc_ref[4:5, cols] += jnp.sum(dpre, axis=0, keepdims=True)
            dxp = w[3:4, cols] * dpre
            for j in (1, 2, 3):
                dxp = dxp + w[3 - j:4 - j, cols] * win_d[j:j + tb, cols]
            dxp_ref[:, cols] = dxp.astype(BF16)
        o1, o2, o3 = D_SSD, D_SSD + D_XBC, D_SSD + D_XBC + DT_PAD
        du = (jnp.dot(dz_ref[...].astype(BF16), w_vmem[0:o1, :], preferred_element_type=F32)
              + jnp.dot(dxp_ref[...], w_vmem[o1:o2, :], preferred_element_type=F32)
              + jnp.dot(ddt_ref[...].astype(BF16), w_vmem[o2:o3, :], preferred_element_type=F32)
              + jnp.dot(du5_ref[...].astype(BF16), w_vmem[o3:, :], preferred_element_type=F32))
        m = mod_ref[0]
        xv = x_ref[...]
        u_ref[...] = (xv * (1.0 + m[1:2]) + m[0:1]).astype(BF16)
        gx_ref[...] = dxa_ref[...] + du * (1.0 + m[1:2])
        bacc_ref[0, 0:1, :] += jnp.sum(du, axis=0, keepdims=True)
        bacc_ref[0, 1:2, :] += jnp.sum(du * xv, axis=0, keepdims=True)

    row = lambda w: pl.BlockSpec((tb, w), lambda i: (blk(i), 0))
    halo = pl.BlockSpec((8, D_XBC), lambda i: (jnp.maximum(blk(i) * (tb // 8) - 1, 0), 0))
    return pl.pallas_call(
        body, name="conv_proj_bwd", grid=(n_blocks,),
        out_shape=(jax.ShapeDtypeStruct((t, D_MODEL), F32), jax.ShapeDtypeStruct((t, D_MODEL), BF16),
                   jax.ShapeDtypeStruct((t, D_XBC), BF16), jax.ShapeDtypeStruct((t // seq, 8, D_MODEL), F32),
                   jax.ShapeDtypeStruct((8, D_XBC), F32)),
        in_specs=[row(D_SSD), row(D_XBC), row(D_XBC), row(D_XBC), halo, row(DT_PAD), row(D_S5), row(D_MODEL),
                  row(D_MODEL), pl.BlockSpec((1, N_MOD, D_MODEL), lambda i: (i // npb, 0, 0)),
                  pl.BlockSpec((4, D_XBC), lambda i: (0, 0)), ANY],
        out_specs=(row(D_MODEL), row(D_MODEL), row(D_XBC), pl.BlockSpec((1, 8, D_MODEL), lambda i: (i // npb, 0, 0)),
                   pl.BlockSpec((8, D_XBC), lambda i: (0, 0))),
        scratch_shapes=[pltpu.VMEM((D_INP, D_MODEL), BF16), pltpu.VMEM((tb + 8, D_XBC), F32),
                        pltpu.VMEM((tb + 8, D_XBC), F32), pltpu.SemaphoreType.DMA],
        compiler_params=_params(60),
    )(dz, dxbc, dsilu, xbc_pre, xbc_pre, ddt, du5, x2, dxa, mod3, conv_w, w_in_pad)


def _pad_rows(a, mult):
    r = a.shape[0]
    pad = (-r) % mult
    return a if pad == 0 else jnp.concatenate([a, jnp.zeros((pad,) + a.shape[1:], a.dtype)], axis=0)


_SMALL = ["conv_w", "conv_b", "dt_bias", "a_log", "d_ssd", "norm_w", "s5_a_re", "s5_a_im", "s5_log_dt", "s5_b_re",
          "s5_b_im", "s5_c_re", "s5_c_im", "s5_d", "b_glu", "ln1_g", "ln1_b", "b1", "b2", "ln2_g", "ln2_b"]


def _tile_rows(size):
    return 8 * (-(-size // 1024))


def _pack_small(d):
    parts = []
    for n in _SMALL:
        flat = d[n].reshape(-1).astype(F32)
        rows = _tile_rows(flat.shape[0])
        pad = rows * 128 - flat.shape[0]
        if pad:
            flat = jnp.concatenate([flat, jnp.zeros((pad,), F32)])
        parts.append(flat.reshape(rows, 128))
    return jnp.concatenate(parts, axis=0)


def _unpack_small(p, shapes):
    out, off = {}, 0
    for n in _SMALL:
        size = math.prod(shapes[n])
        rows = _tile_rows(size)
        out[n] = p[off:off + rows].reshape(-1)[:size].reshape(shapes[n])
        off += rows
    return out


def kernel(x, c, w_ada, b_ada, w_in, conv_w, conv_b, dt_bias, a_log, d_ssd, norm_w, s5_a_re, s5_a_im, s5_log_dt, s5_b_re, s5_b_im, s5_c_re, s5_c_im, s5_d, w_glu, b_glu, w_out, ln1_g, ln1_b, w1, b1, w2, b2, ln2_g, ln2_b, loss_target, m_w_ada, m_b_ada, m_w_in, m_conv_w, m_conv_b, m_dt_bias, m_a_log, m_d_ssd, m_norm_w, m_s5_a_re, m_s5_a_im, m_s5_log_dt, m_s5_b_re, m_s5_b_im, m_s5_c_re, m_s5_c_im, m_s5_d, m_w_glu, m_b_glu, m_w_out, m_ln1_g, m_ln1_b, m_w1, m_b1, m_w2, m_b2, m_ln2_g, m_ln2_b, v_w_ada, v_b_ada, v_w_in, v_conv_w, v_conv_b, v_dt_bias, v_a_log, v_d_ssd, v_norm_w, v_s5_a_re, v_s5_a_im, v_s5_log_dt, v_s5_b_re, v_s5_b_im, v_s5_c_re, v_s5_c_im, v_s5_d, v_w_glu, v_b_glu, v_w_out, v_ln1_g, v_ln1_b, v_w1, v_b1, v_w2, v_b2, v_ln2_g, v_ln2_b):
    weights = dict(w_ada=w_ada, b_ada=b_ada, w_in=w_in, conv_w=conv_w, conv_b=conv_b, dt_bias=dt_bias, a_log=a_log,
                   d_ssd=d_ssd, norm_w=norm_w, s5_a_re=s5_a_re, s5_a_im=s5_a_im, s5_log_dt=s5_log_dt, s5_b_re=s5_b_re,
                   s5_b_im=s5_b_im, s5_c_re=s5_c_re, s5_c_im=s5_c_im, s5_d=s5_d, w_glu=w_glu, b_glu=b_glu, w_out=w_out,
                   ln1_g=ln1_g, ln1_b=ln1_b, w1=w1, b1=b1, w2=w2, b2=b2, ln2_g=ln2_g, ln2_b=ln2_b)
    mom = dict(w_ada=m_w_ada, b_ada=m_b_ada, w_in=m_w_in, conv_w=m_conv_w, conv_b=m_conv_b, dt_bias=m_dt_bias,
               a_log=m_a_log, d_ssd=m_d_ssd, norm_w=m_norm_w, s5_a_re=m_s5_a_re, s5_a_im=m_s5_a_im,
               s5_log_dt=m_s5_log_dt, s5_b_re=m_s5_b_re, s5_b_im=m_s5_b_im, s5_c_re=m_s5_c_re, s5_c_im=m_s5_c_im,
               s5_d=m_s5_d, w_glu=m_w_glu, b_glu=m_b_glu, w_out=m_w_out, ln1_g=m_ln1_g, ln1_b=m_ln1_b, w1=m_w1, b1=m_b1,
               w2=m_w2, b2=m_b2, ln2_g=m_ln2_g, ln2_b=m_ln2_b)
    var = dict(w_ada=v_w_ada, b_ada=v_b_ada, w_in=v_w_in, conv_w=v_conv_w, conv_b=v_conv_b, dt_bias=v_dt_bias,
               a_log=v_a_log, d_ssd=v_d_ssd, norm_w=v_norm_w, s5_a_re=v_s5_a_re, s5_a_im=v_s5_a_im,
               s5_log_dt=v_s5_log_dt, s5_b_re=v_s5_b_re, s5_b_im=v_s5_b_im, s5_c_re=v_s5_c_re, s5_c_im=v_s5_c_im,
               s5_d=v_s5_d, w_glu=v_w_glu, b_glu=v_b_glu, w_out=v_w_out, ln1_g=v_ln1_g, ln1_b=v_ln1_b, w1=v_w1, b1=v_b1,
               w2=v_w2, b2=v_b2, ln2_g=v_ln2_g, ln2_b=v_ln2_b)
    names = list(weights)
    shapes = {n: weights[n].shape for n in names}

    nb, seq, _ = x.shape
    t = nb * seq
    dev = _dev_index()
    x2 = x.reshape(t, D_MODEL)
    tgt2 = loss_target.reshape(t, D_MODEL)

    cw_cols = conv_w.shape[2]
    small_in = jnp.concatenate([c.reshape(-1), conv_w.reshape(-1)]).reshape(-1, 128)
    big_names = ["w_in", "w_out", "w1", "w2", "w_glu"]
    local = {n: (a[0].T if n == "w_in" else a[0]) for n, a in weights.items() if n in big_names}
    shard_bf16 = {n: local[n].astype(BF16) for n in big_names}
    first = _all_gather([small_in, shard_bf16["w_in"], shard_bf16["w_glu"]], "gather_first")
    small_all = first[0].reshape(N_DEV, -1)
    c_all = small_all[:, :nb * D_MODEL].reshape(N_DEV * nb, D_MODEL)
    conv_w_full = small_all[:, nb * D_MODEL:].reshape(N_DEV, 4, cw_cols).transpose(1, 0, 2).reshape(4, D_XBC)

    w_in_t = first[1].reshape(D_IN, D_MODEL)
    w_in_pad = jnp.concatenate(
        [w_in_t[:D_SSD + D_XBC + N_HEADS], jnp.zeros((DT_PAD - N_HEADS, D_MODEL), BF16),
         w_in_t[D_SSD + D_XBC + N_HEADS:]], axis=0)
    w_glu_f = first[2].reshape(D_S5, D_S5)
    late_names = ["w_out", "w1", "w2"]

    ada_cols = w_ada.shape[2]
    b_cols = lax.dynamic_slice_in_dim(b_ada, dev * ada_cols, ada_cols, axis=1)
    mod_cols = _mod_fwd(c_all, w_ada[0], b_cols)
    mod_all = _all_gather([mod_cols], "gather_mod")[0]
    mod_mine = lax.dynamic_slice_in_dim(mod_all, dev * nb, nb, axis=1)
    mod3 = mod_mine.transpose(1, 0, 2).reshape(nb, N_MOD, D_MODEL)

    def pad_lanes(v, n):
        return jnp.concatenate([v, jnp.zeros((v.shape[0], n - v.shape[1]), F32)], axis=1)

    par = _pad_rows(jnp.concatenate([pad_lanes(dt_bias, 128), pad_lanes(a_log, 128)], axis=0), 8)
    dsk = jnp.repeat(d_ssd[0], HEADDIM).reshape(1, D_SSD)
    ar = s5_a_re.reshape(1, S5_N)
    ai = s5_a_im.reshape(1, S5_N)
    ldt = jnp.repeat(s5_log_dt[0], S5_P).reshape(1, S5_N)
    br_t = s5_b_re[0].transpose(2, 0, 1).reshape(S5_CH, S5_N)
    bi_t = s5_b_im[0].transpose(2, 0, 1).reshape(S5_CH, S5_N)
    bb_re_t, bb_im_t, pf_re, pf_im, pr_re, pr_im = _s5_params_fwd(ar, ai, ldt, br_t, bi_t)
    gpb = S5_GROUPS // S5_BLOCKS
    mask_b = (jnp.arange(128)[:, None] // S5_CH) == (jnp.arange(512)[None, :] // S5_P)

    def dense_b(bt_):
        blocks = bt_.reshape(S5_CH, S5_BLOCKS, 512).transpose(1, 0, 2)
        return jnp.where(mask_b, jnp.tile(blocks, (1, gpb, 1)), 0.0).astype(BF16)

    def dense_c(cc):
        blocks = cc[0].transpose(0, 2, 1).reshape(S5_BLOCKS, 512, S5_CH)
        return jnp.where(mask_b.T, jnp.tile(blocks, (1, 1, gpb)), 0.0).astype(BF16)

    bb_re, bb_im = dense_b(bb_re_t), dense_b(bb_im_t)
    cc_re, cc_im = dense_c(s5_c_re), dense_c(s5_c_im)
    s5d = s5_d.reshape(1, D_S5)
    ln1 = jnp.concatenate([ln1_g, ln1_b], axis=0)
    vec1 = _pad_rows(jnp.concatenate([b2, ln2_g, ln2_b], axis=0), 8)

    z, xbc_pre, xbc, dsilu, dt_raw, u5 = _proj_conv_fwd(x2, mod3, w_in_pad, conv_w_full, conv_b, seq)
    late_in, dt_raw = lax.optimization_barrier(([shard_bf16[n] for n in late_names], dt_raw))
    late_sems = _gather_start(late_in, "gather_late_start")
    yraw, ycat, hprev = _ssd_fwd(xbc, z, dt_raw, par + late_sems[4][0, 0], dsk, norm_w, seq)
    s_re, s_im, ypre, ycat = _s5_fwd(u5, bb_re, bb_im, cc_re, cc_im, pf_re, pf_im, s5d, w_glu_f, b_glu, ycat, seq)
    sent, landed = _gather_wait(late_sems[0], late_sems[1], late_sems[2], late_sems[3], ycat, "gather_late_wait")
    gathered = {n: lax.dynamic_update_index_in_dim(l, x, dev, 0) for n, x, l in zip(late_names, sent, landed)}
    w_out_f = gathered["w_out"].reshape(2 * D_MODEL, D_MODEL)
    w1_blocks = gathered["w1"]
    w2_f = gathered["w2"].reshape(D_FF, D_MODEL)
    mix, x1 = _out_ln1(ycat, x2, mod3, w_out_f, ln1, seq)

    dx1, u2b, hb, dhpb, dob, gacc2, db1, bacc2 = _mlp_fwd_bwd(x1, tgt2, mod3, w1_blocks, w2_f, vec1, b1, seq)
    loss = lax.psum(0.5 / D_MODEL * jnp.sum(gacc2[3]), ("x", "y", "c"))

    dmixb, dxa, dyssd, dy5, gacc1, bacc1 = _ln1_out_bwd(dx1, x2, mix, mod3, w_out_f, ln1, seq)

    g_w2 = _atb(hb, dob, "gw2")
    g_w1 = _atb(u2b, dhpb, "gw1")
    g_wout = _atb(ycat, dmixb, "gwout")
    core = lax.axis_index("c").astype(jnp.int32).reshape(1)
    chip = 2 * lax.axis_index("x") + lax.axis_index("y")

    def chip_sums_of(names, grads, tag):
        by_dest = [g if g.ndim == 2 else g.reshape((4, 2) + g.shape[1:]) for g in grads]
        from_sibling = _sibling_swap(by_dest, "rs_swap_" + tag)
        return [_add_halves(g, r, core, "rs_add_" + n) for g, r, n in zip(by_dest, from_sibling, names)]

    early_names = ["w_out", "w1", "w2"]
    early_dest = [g_wout.reshape((4, 2) + w_out.shape[1:]), g_w1, g_w2.reshape((4, 2) + w2.shape[1:])]
    swap = _sibling_swap_start(early_dest, "rs_early_swap_start")
    du5, vacc, sacc, d_cc, d_bb, g_wglu = _s5_bwd(dy5, ypre, u5, s_re, s_im, bb_re, bb_im, cc_re, cc_im,
                                                  pr_re, pr_im, s5d + swap[4][0, 0], w_glu_f, b_glu, seq)
    early_dest, from_sibling = _sibling_swap_wait(swap[0], swap[1], swap[2], swap[3], du5, "rs_early_swap_wait")
    early_sums = [_add_halves(g, r, core, "rs_add_" + n) for g, r, n in zip(early_dest, from_sibling, early_names)]
    early = _all_to_all_start(early_sums, "rs_early_start")
    dxbc, dz, ddt, dpar, cacc = _ssd_bwd(dyssd, yraw, z, xbc, dt_raw, hprev, par + early[4][0, 0], dsk, norm_w, seq)
    grad_x2, ub, dxpb, bacc0, conv_acc = _conv_proj_bwd(dz, dxbc, dsilu, xbc_pre, ddt, du5, x2, dxa, mod3,
                                                        conv_w_full, w_in_pad, seq)

    def diag_b(dd):
        kept = jnp.where(mask_b, dd, 0.0).reshape(S5_BLOCKS, gpb, S5_CH, 512).sum(1)
        return kept.transpose(1, 0, 2).reshape(S5_CH, S5_N)

    def diag_c(dd):
        kept = jnp.where(mask_b.T, dd, 0.0).reshape(S5_BLOCKS, 512, gpb, S5_CH).sum(2)
        return kept.reshape(S5_GROUPS, S5_P, S5_CH).transpose(0, 2, 1)

    g_ar, g_ai, g_ldt, g_br_t, g_bi_t = _s5_params_bwd(ar, ai, ldt, br_t, bi_t, vacc[0:1], vacc[1:2],
                                                      diag_b(d_bb[:S5_BLOCKS]), diag_b(d_bb[S5_BLOCKS:]))

    def from_t(gt):
        return gt.reshape(S5_CH, S5_GROUPS, S5_P).transpose(1, 2, 0)

    small_g = dict(
        conv_w=conv_acc[0:4], conv_b=conv_acc[4:5], dt_bias=dpar[0:1, :N_HEADS], a_log=dpar[1:2, :N_HEADS],
        d_ssd=cacc[0].reshape(N_HEADS, HEADDIM).sum(1), norm_w=cacc[1:2],
        s5_a_re=g_ar, s5_a_im=g_ai, s5_log_dt=g_ldt[:, :S5_GROUPS], s5_b_re=from_t(g_br_t), s5_b_im=from_t(g_bi_t),
        s5_c_re=diag_c(d_cc[:S5_BLOCKS]), s5_c_im=diag_c(d_cc[S5_BLOCKS:]), s5_d=sacc[0:1], b_glu=sacc[1:2],
        ln1_g=gacc1[0:1], ln1_b=gacc1[1:2], b1=db1, b2=gacc2[2:3], ln2_g=gacc2[0:1], ln2_b=gacc2[1:2])

    dmod = jnp.concatenate([bacc0[:, 0], bacc0[:, 1], bacc1[:, 0], bacc2[:, 0], bacc2[:, 1], bacc2[:, 2]], axis=1)
    small_sems = _gather_start([dmod, _pack_small(small_g)], "gather_small_start")
    tok = small_sems[4]
    g_win_t = jnp.concatenate([_atb(dz, ub, "gwin_z", tok), _atb(dxpb, ub, "gwin_xbc", tok),
                               _atb(ddt, ub, "gwin_dt", tok)[:N_HEADS], _atb(du5, ub, "gwin_s5", tok)], axis=0)
    sent, landed = _gather_wait(small_sems[0], small_sems[1], small_sems[2], small_sems[3], g_win_t,
                                "gather_small_wait")
    dmod_all, small_parts = [lax.dynamic_update_index_in_dim(l, x, dev, 0) for x, l in zip(sent, landed)]
    dmod_all = dmod_all.reshape(N_DEV * nb, N_MOD * D_MODEL)
    dmod_cols = lax.dynamic_slice_in_dim(dmod_all, dev * ada_cols, ada_cols, axis=1)
    g_wada, g_bada = _mod_bwd(c_all, dmod_cols, dmod_all)

    late_rs = ["w_in", "w_glu"]
    late_g, small_parts = lax.optimization_barrier(
        ([g_win_t.reshape(N_DEV, w_in.shape[2], D_MODEL), g_wglu.reshape((N_DEV,) + w_glu.shape[1:])], small_parts))
    late = _all_to_all_start(chip_sums_of(late_rs, late_g, "late"), "rs_late_start")

    def own_block_in(landed, sent):
        return [lax.dynamic_update_index_in_dim(l, lax.dynamic_index_in_dim(h, chip, 0, keepdims=False), chip, 0)
                for l, h in zip(landed, sent)]

    sent, landed = _all_to_all_wait(early[0], early[1], early[2], early[3], late[4], "rs_early_wait")
    parts = dict(zip(early_names, own_block_in(landed, sent)))
    res = {k: {} for k in "gdmv"}

    def update(n):
        w_m_v = [(a[n][0].T if n == "w_in" else a[n][0]) for a in (weights, mom, var)]
        outs = _adamw(parts[n], *w_m_v, "adamw_" + n)
        for k, a in zip("gdmv", outs):
            res[k][n] = (a.T if n == "w_in" else a)[None]

    for n in early_names:
        update(n)
    sent, landed = _all_to_all_wait(late[0], late[1], late[2], late[3], res["d"]["w2"], "rs_late_wait")
    parts.update(zip(late_rs, own_block_in(landed, sent)))
    for n in late_rs:
        update(n)

    ag, ad, am, av = _adamw(g_wada[None], w_ada[0], m_w_ada[0], v_w_ada[0], "adamw_w_ada")
    for k, a in (("g", ag), ("d", ad), ("m", am), ("v", av)):
        res[k]["w_ada"] = a[None]
    bg_, bd_, bm_, bv_ = _adamw(g_bada.reshape(1, -1, 128), b_ada.reshape(-1, 128), m_b_ada.reshape(-1, 128),
                                v_b_ada.reshape(-1, 128), "adamw_b_ada")
    for k, a in (("g", bg_), ("d", bd_), ("m", bm_), ("v", bv_)):
        res[k]["b_ada"] = a.reshape(shapes["b_ada"])

    small_shapes = dict(shapes)
    small_shapes["conv_w"] = (1, 4, D_XBC)
    rep = {n: (jnp.zeros((1, 4, D_XBC), F32) if n == "conv_w" else weights[n]) for n in _SMALL}
    rep_m = {n: (jnp.zeros((1, 4, D_XBC), F32) if n == "conv_w" else mom[n]) for n in _SMALL}
    rep_v = {n: (jnp.ones((1, 4, D_XBC), F32) if n == "conv_w" else var[n]) for n in _SMALL}
    sg_, sd_, sm_, sv_ = _adamw(small_parts, _pack_small(rep), _pack_small(rep_m), _pack_small(rep_v), "adamw_small")
    for k, p in (("g", sg_), ("d", sd_), ("m", sm_), ("v", sv_)):
        un = _unpack_small(p, small_shapes)
        for n in _SMALL:
            if n != "conv_w":
                res[k][n] = un[n]
    g_conv_full = _unpack_small(sg_, small_shapes)["conv_w"][0]
    g_conv_mine = lax.dynamic_slice_in_dim(g_conv_full, dev * cw_cols, cw_cols, axis=1)
    cg_, cd_, cm_, cv_ = _adamw(g_conv_mine[None], conv_w[0], m_conv_w[0], v_conv_w[0], "adamw_conv_w")
    for k, a in (("g", cg_), ("d", cd_), ("m", cm_), ("v", cv_)):
        res[k]["conv_w"] = a[None]

    grad_x = grad_x2.reshape(nb, seq, D_MODEL)
    return (loss, grad_x, *[res["g"][n] for n in names], *[res["d"][n] for n in names],
            *[res["m"][n] for n in names], *[res["v"][n] for n in names])
```

```python
import functools
import math

import jax
import jax.numpy as jnp
from jax import lax
from jax.experimental import pallas as pl
from jax.experimental.pallas import tpu as pltpu

F32, BF16 = jnp.float32, jnp.bfloat16
MESH = pl.DeviceIdType.MESH
N_DEV = 8

D_MODEL = 1024
D_SSD = 1536
N_HEADS = 24
HEADDIM = 64
N_GROUPS = 4
HPG = 6
GW = HPG * HEADDIM
N_STATE = 128
CHUNK = 128
D_XBC = 2560
D_S5 = 512
S5_GROUPS = 32
S5_CH = 16
S5_P = 64
S5_N = S5_GROUPS * S5_P
D_IN = 4632
DT_PAD = 128
D_INP = D_SSD + D_XBC + DT_PAD + D_S5
D_FF = 4096
N_MOD = 6
ALPHA = 2.0 ** 0.25
EPS = 1e-5
LR, B1, B2, AEPS, WD, STEP = 0.001, 0.9, 0.999, 1e-08, 0.01, 10

NT = (((1,), (1,)), ((), ()))
TN = (((0,), (0,)), ((), ()))
ANY = pl.BlockSpec(memory_space=pl.ANY)
HIGHEST = lax.Precision.HIGHEST


def _mm(a, b):
    return jnp.dot(a.astype(BF16), b.astype(BF16), preferred_element_type=F32)


def _mm_nt(a, b):
    return lax.dot_general(a.astype(BF16), b.astype(BF16), NT, preferred_element_type=F32)


def _mm_tn(a, b):
    return lax.dot_general(a.astype(BF16), b.astype(BF16), TN, preferred_element_type=F32)


def _row_block(r, cap):
    best = r
    for cand in range(8, min(r, cap) + 1, 8):
        if r % cand == 0:
            best = cand
    return best if best <= cap else r


def _params(vmem_mb):
    return pltpu.CompilerParams(vmem_limit_bytes=vmem_mb << 20)


def _sigmoid(x):
    return 0.5 * (jnp.tanh(0.5 * x) + 1.0)


def _softplus(x):
    return jnp.maximum(x, 0.0) + jnp.log(1.0 + jnp.exp(-jnp.abs(x)))


_GK = math.sqrt(2.0 / math.pi)


def _gelu(x):
    return 0.5 * x * (1.0 + jnp.tanh(_GK * (x + 0.044715 * x * x * x)))


def _gelu_grad(x):
    t = jnp.tanh(_GK * (x + 0.044715 * x * x * x))
    return 0.5 * (1.0 + t) + 0.5 * x * (1.0 - t * t) * _GK * (1.0 + 3.0 * 0.044715 * x * x)


def _dev_index():
    return 4 * lax.axis_index("x") + 2 * lax.axis_index("y") + lax.axis_index("c")


def _all_gather(xs, name):
    n = len(xs)

    def body(*refs):
        x_refs, out_refs = refs[:n], refs[n:2 * n]
        send_sems, recv_sems, local_sems = refs[2 * n:]
        ix, iy, ic = lax.axis_index("x"), lax.axis_index("y"), lax.axis_index("c")
        me, sibling = (ix, iy, ic), (ix, iy, 1 - ic)
        chips = [(1 - ix, iy), (ix, 1 - iy), (1 - ix, 1 - iy)]

        def slot(a, px, py, pc):
            return out_refs[a].at[4 * px + 2 * py + pc]

        def copy(a, k, block, to, src=None):
            return pltpu.make_async_remote_copy(
                src_ref=slot(a, *block) if src is None else src, dst_ref=slot(a, *block),
                send_sem=send_sems.at[7 * a + k], recv_sem=recv_sems.at[7 * a + k], device_id=to, device_id_type=MESH)

        mine = [pltpu.make_async_copy(x_refs[a], slot(a, *me), local_sems.at[a]) for a in range(n)]
        for cp in mine:
            cp.start()
        first = []
        for j, chip in enumerate(chips):
            first += [copy(a, 1 + j, me, (*chip, ic), src=x_refs[a]) for a in range(n)]
        first += [copy(a, 0, me, sibling, src=x_refs[a]) for a in range(n)]
        for cp in first:
            cp.start()
        passed = []
        for j, chip in enumerate(chips):
            for a in range(n):
                copy(a, 1 + j, (*chip, ic), me).wait_recv()
                cp = copy(a, 4 + j, (*chip, ic), sibling)
                cp.start()
                passed.append(cp)
        for a in range(n):
            copy(a, 0, sibling, me).wait_recv()
            for j, chip in enumerate(chips):
                copy(a, 4 + j, (*chip, 1 - ic), me).wait_recv()
        for cp in first + passed:
            cp.wait_send()
        for cp in mine:
            cp.wait()

    return pl.pallas_call(
        body, name=name, out_shape=tuple(jax.ShapeDtypeStruct((N_DEV,) + x.shape, x.dtype) for x in xs),
        in_specs=[ANY] * n, out_specs=tuple([ANY] * n),
        scratch_shapes=[pltpu.SemaphoreType.DMA((7 * n,)), pltpu.SemaphoreType.DMA((7 * n,)),
                        pltpu.SemaphoreType.DMA((n,))],
    )(*xs)


HBM = pl.BlockSpec(memory_space=pltpu.HBM)
SEM = pl.BlockSpec(memory_space=pltpu.SEMAPHORE)
DATAFLOW = pltpu.SideEffectType.DATAFLOW_SIDE_EFFECTING


def _peer(k):
    ix, iy, ic = lax.axis_index("x"), lax.axis_index("y"), lax.axis_index("c")
    return (1 - ix if k & 4 else ix, 1 - iy if k & 2 else iy, 1 - ic if k & 1 else ic)


def _block_of(p):
    return 4 * p[0] + 2 * p[1] + p[2]


def _gather_start(xs, name):
    n = len(xs)
    lands = [lax.empty((N_DEV,) + x.shape, x.dtype) for x in xs]

    def body(*refs):
        x_refs, land_refs = refs[:n], refs[n:2 * n]
        send_sems, recv_sems = refs[2 * n], refs[2 * n + 1]
        token = refs[-1]
        me = _block_of(_peer(0))
        for a in range(n):
            for k in range(1, N_DEV):
                pltpu.make_async_remote_copy(
                    src_ref=x_refs[a], dst_ref=land_refs[a].at[me], send_sem=send_sems.at[7 * a + k - 1],
                    recv_sem=recv_sems.at[7 * a + k - 1], device_id=_peer(k), device_id_type=MESH).start()
        token[...] = jnp.zeros_like(token)

    outs = pl.pallas_call(
        body, name=name,
        out_shape=(pltpu.SemaphoreType.DMA((7 * n,)), pltpu.SemaphoreType.DMA((7 * n,)))
        + tuple(pltpu.HBM(x.shape, x.dtype) for x in xs) + tuple(pltpu.HBM(l.shape, l.dtype) for l in lands)
        + (jax.ShapeDtypeStruct((8, 128), F32),),
        in_specs=[HBM] * (2 * n), out_specs=(SEM, SEM) + (HBM,) * (2 * n) + (pl.BlockSpec(memory_space=pltpu.VMEM),),
        input_output_aliases={i: 2 + i for i in range(2 * n)},
        compiler_params=pltpu.CompilerParams(has_side_effects=DATAFLOW),
    )(*[pltpu.with_memory_space_constraint(x, pltpu.HBM) for x in xs],
      *[pltpu.with_memory_space_constraint(l, pltpu.HBM) for l in lands])
    return outs[0], outs[1], outs[2:2 + n], outs[2 + n:2 + 2 * n], outs[-1]


def _gather_wait(send_sems, recv_sems, xs_thru, lands_thru, after, name):
    n = len(xs_thru)

    def body(*refs):
        x_refs, land_refs = refs[:n], refs[n:2 * n]
        send_sems, recv_sems = refs[2 * n], refs[2 * n + 1]
        for a in range(n):
            for k in range(1, N_DEV):
                cp = pltpu.make_async_remote_copy(
                    src_ref=x_refs[a], dst_ref=land_refs[a].at[_block_of(_peer(k))], send_sem=send_sems.at[7 * a + k - 1],
                    recv_sem=recv_sems.at[7 * a + k - 1], device_id=_peer(k), device_id_type=MESH)
                cp.wait_send()
                cp.wait_recv()

    outs = pl.pallas_call(
        body, name=name,
        out_shape=tuple(pltpu.HBM(x.shape, x.dtype) for x in xs_thru)
        + tuple(pltpu.HBM(l.shape, l.dtype) for l in lands_thru),
        in_specs=[HBM] * (2 * n) + [SEM, SEM, ANY], out_specs=(HBM,) * (2 * n),
        input_output_aliases={i: i for i in range(2 * n)},
        compiler_params=pltpu.CompilerParams(has_side_effects=DATAFLOW),
    )(*xs_thru, *lands_thru, send_sems, recv_sems, after)
    return outs[:n], outs[n:]


def _chip_peer(k):
    ix, iy = lax.axis_index("x"), lax.axis_index("y")
    return (1 - ix if k & 2 else ix, 1 - iy if k & 1 else iy)


def _all_to_all_start(hs, name):
    n = len(hs)
    lands = [lax.empty(h.shape, h.dtype) for h in hs]

    def body(*refs):
        h_refs, land_refs = refs[:n], refs[n:2 * n]
        send_sems, recv_sems = refs[2 * n], refs[2 * n + 1]
        token = refs[-1]
        ic = lax.axis_index("c")
        mx, my = _chip_peer(0)
        for a in range(n):
            for k in range(1, 4):
                px, py = _chip_peer(k)
                pltpu.make_async_remote_copy(
                    src_ref=h_refs[a].at[2 * px + py], dst_ref=land_refs[a].at[2 * mx + my],
                    send_sem=send_sems.at[3 * a + k - 1], recv_sem=recv_sems.at[3 * a + k - 1],
                    device_id=(px, py, ic), device_id_type=MESH).start()
        token[...] = jnp.zeros_like(token)

    outs = pl.pallas_call(
        body, name=name,
        out_shape=(pltpu.SemaphoreType.DMA((3 * n,)), pltpu.SemaphoreType.DMA((3 * n,)))
        + tuple(pltpu.HBM(h.shape, h.dtype) for h in hs) + tuple(pltpu.HBM(l.shape, l.dtype) for l in lands)
        + (jax.ShapeDtypeStruct((8, 128), F32),),
        in_specs=[HBM] * (2 * n), out_specs=(SEM, SEM) + (HBM,) * (2 * n) + (pl.BlockSpec(memory_space=pltpu.VMEM),),
        input_output_aliases={i: 2 + i for i in range(2 * n)},
        compiler_params=pltpu.CompilerParams(has_side_effects=DATAFLOW),
    )(*[pltpu.with_memory_space_constraint(h, pltpu.HBM) for h in hs],
      *[pltpu.with_memory_space_constraint(l, pltpu.HBM) for l in lands])
    return outs[0], outs[1], outs[2:2 + n], outs[2 + n:2 + 2 * n], outs[-1]


def _all_to_all_wait(send_sems, recv_sems, hs_thru, lands_thru, after, name):
    n = len(hs_thru)

    def body(*refs):
        h_refs, land_refs = refs[:n], refs[n:2 * n]
        send_sems, recv_sems = refs[2 * n], refs[2 * n + 1]
        ic = lax.axis_index("c")
        for a in range(n):
            for k in range(1, 4):
                px, py = _chip_peer(k)
                cp = pltpu.make_async_remote_copy(
                    src_ref=h_refs[a].at[2 * px + py], dst_ref=land_refs[a].at[2 * px + py],
                    send_sem=send_sems.at[3 * a + k - 1], recv_sem=recv_sems.at[3 * a + k - 1],
                    device_id=(px, py, ic), device_id_type=MESH)
                cp.wait_send()
                cp.wait_recv()

    outs = pl.pallas_call(
        body, name=name,
        out_shape=tuple(pltpu.HBM(h.shape, h.dtype) for h in hs_thru)
        + tuple(pltpu.HBM(l.shape, l.dtype) for l in lands_thru),
        in_specs=[HBM] * (2 * n) + [SEM, SEM, ANY], out_specs=(HBM,) * (2 * n),
        input_output_aliases={i: i for i in range(2 * n)},
        compiler_params=pltpu.CompilerParams(has_side_effects=DATAFLOW),
    )(*hs_thru, *lands_thru, send_sems, recv_sems, after)
    return outs[:n], outs[n:]


def _sibling_block(g_ref, q):
    ic = lax.axis_index("c")
    if len(g_ref.shape) == 4:
        return g_ref.at[q, 1 - ic]
    cw = g_ref.shape[1] // N_DEV
    return g_ref.at[:, pl.ds(pl.multiple_of((2 * q + 1 - ic) * cw, 128), cw)]


def _sibling_swap_start(gs, name):
    n = len(gs)
    lands = [lax.empty((4,) + (g.shape[2:] if g.ndim == 4 else (g.shape[0], g.shape[1] // N_DEV)), g.dtype) for g in gs]

    def body(*refs):
        g_refs, land_refs = refs[:n], refs[n:2 * n]
        send_sems, recv_sems = refs[2 * n], refs[2 * n + 1]
        token = refs[-1]
        for a in range(n):
            for q in range(4):
                pltpu.make_async_remote_copy(
                    src_ref=_sibling_block(g_refs[a], q), dst_ref=land_refs[a].at[q],
                    send_sem=send_sems.at[4 * a + q], recv_sem=recv_sems.at[4 * a + q],
                    device_id=_peer(1), device_id_type=MESH).start()
        token[...] = jnp.zeros_like(token)

    outs = pl.pallas_call(
        body, name=name,
        out_shape=(pltpu.SemaphoreType.DMA((4 * n,)), pltpu.SemaphoreType.DMA((4 * n,)))
        + tuple(pltpu.HBM(g.shape, g.dtype) for g in gs) + tuple(pltpu.HBM(l.shape, l.dtype) for l in lands)
        + (jax.ShapeDtypeStruct((8, 128), F32),),
        in_specs=[HBM] * (2 * n), out_specs=(SEM, SEM) + (HBM,) * (2 * n) + (pl.BlockSpec(memory_space=pltpu.VMEM),),
        input_output_aliases={i: 2 + i for i in range(2 * n)},
        compiler_params=pltpu.CompilerParams(has_side_effects=DATAFLOW),
    )(*[pltpu.with_memory_space_constraint(g, pltpu.HBM) for g in gs],
      *[pltpu.with_memory_space_constraint(l, pltpu.HBM) for l in lands])
    return outs[0], outs[1], outs[2:2 + n], outs[2 + n:2 + 2 * n], outs[-1]


def _sibling_swap_wait(send_sems, recv_sems, gs_thru, lands_thru, after, name):
    n = len(gs_thru)

    def body(*refs):
        g_refs, land_refs = refs[:n], refs[n:2 * n]
        send_sems, recv_sems = refs[2 * n], refs[2 * n + 1]
        for a in range(n):
            for q in range(4):
                cp = pltpu.make_async_remote_copy(
                    src_ref=_sibling_block(g_refs[a], q), dst_ref=land_refs[a].at[q],
                    send_sem=send_sems.at[4 * a + q], recv_sem=recv_sems.at[4 * a + q],
                    device_id=_peer(1), device_id_type=MESH)
                cp.wait_send()
                cp.wait_recv()

    outs = pl.pallas_call(
        body, name=name,
        out_shape=tuple(pltpu.HBM(g.shape, g.dtype) for g in gs_thru)
        + tuple(pltpu.HBM(l.shape, l.dtype) for l in lands_thru),
        in_specs=[HBM] * (2 * n) + [SEM, SEM, ANY], out_specs=(HBM,) * (2 * n),
        input_output_aliases={i: i for i in range(2 * n)},
        compiler_params=pltpu.CompilerParams(has_side_effects=DATAFLOW),
    )(*gs_thru, *lands_thru, send_sems, recv_sems, after)
    return outs[:n], outs[n:]


def _sibling_swap(gs, name):
    n = len(gs)

    def body(*refs):
        g_refs, recv_refs = refs[:n], refs[n:2 * n]
        send_sems, recv_sems = refs[2 * n:]
        ix, iy, ic = lax.axis_index("x"), lax.axis_index("y"), lax.axis_index("c")
        cps = []
        for a in range(n):
            for q in range(4):
                cps.append(pltpu.make_async_remote_copy(
                    src_ref=_sibling_block(g_refs[a], q), dst_ref=recv_refs[a].at[q],
                    send_sem=send_sems.at[4 * a + q], recv_sem=recv_sems.at[4 * a + q],
                    device_id=(ix, iy, 1 - ic), device_id_type=MESH))
        for cp in cps:
            cp.start()
        for cp in cps:
            cp.wait()

    return pl.pallas_call(
        body, name=name,
        out_shape=tuple(jax.ShapeDtypeStruct(
            (4,) + (g.shape[2:] if g.ndim == 4 else (g.shape[0], g.shape[1] // N_DEV)), g.dtype) for g in gs),
        in_specs=[ANY] * n, out_specs=tuple([ANY] * n),
        scratch_shapes=[pltpu.SemaphoreType.DMA((4 * n,)), pltpu.SemaphoreType.DMA((4 * n,))],
    )(*gs)


def _add_halves(g, recv, core, name):
    _, r, c = recv.shape
    br = _row_block(r, 512)
    stacked = g.ndim == 4

    def body(core_ref, g_ref, r_ref, o_ref):
        o_ref[0] = ((g_ref[0, 0] if stacked else g_ref[...]) + r_ref[0]).astype(BF16)

    spec = pl.BlockSpec((1, br, c), lambda i, j, core_ref: (i, j, 0))
    if stacked:
        g_spec = pl.BlockSpec((1, 1, br, c), lambda i, j, core_ref: (i, core_ref[0], j, 0))
    else:
        g_spec = pl.BlockSpec((br, c), lambda i, j, core_ref: (j, 2 * i + core_ref[0]))
    return pl.pallas_call(
        body, name=name, out_shape=jax.ShapeDtypeStruct(recv.shape, BF16),
        grid_spec=pltpu.PrefetchScalarGridSpec(
            num_scalar_prefetch=1, grid=(4, r // br), in_specs=[g_spec, spec], out_specs=spec),
        compiler_params=_params(32),
    )(core, g, recv)


def _adamw(parts, w, m, v, name):
    n_parts, r, c = parts.shape
    if r % 8 == 0:
        br, bc = _row_block(r, 512 if c <= 1024 else 256), c
    else:
        br, bc = r, (256 if c % 256 == 0 else c)

    def body(p_ref, w_ref, m_ref, v_ref, g_out, d_out, m_out, v_out):
        g = p_ref[0].astype(F32)
        for p in range(1, n_parts):
            g = g + p_ref[p].astype(F32)
        m2 = B1 * m_ref[...] + (1.0 - B1) * g
        v2 = B2 * v_ref[...] + (1.0 - B2) * (g * g)
        m_hat = m2 / (1.0 - B1 ** STEP)
        v_hat = v2 / (1.0 - B2 ** STEP)
        g_out[...] = g
        d_out[...] = -LR * (m_hat / (jnp.sqrt(v_hat) + AEPS) + WD * w_ref[...])
        m_out[...] = m2
        v_out[...] = v2

    spec = pl.BlockSpec((br, bc), lambda i, j: (i, j))
    out = jax.ShapeDtypeStruct((r, c), F32)
    return pl.pallas_call(
        body, name=name, out_shape=(out, out, out, out), grid=(r // br, c // bc),
        in_specs=[pl.BlockSpec((n_parts, br, bc), lambda i, j: (0, i, j)), spec, spec, spec],
        out_specs=(spec, spec, spec, spec), compiler_params=_params(40),
    )(parts, w, m, v)


def _atb(a, b, name, after=None):
    t, k1 = a.shape
    k2 = b.shape[1]
    bt = math.gcd(t, 4096 if a.dtype == BF16 and b.dtype == BF16 else 2048)

    def pick(k):
        for cand in (1024, 768, 512, 384, 256, 128):
            if k % cand == 0:
                return cand
        return k

    b1, b2 = pick(k1), pick(k2)

    def body(a_ref, b_ref, *rest):
        o_ref = rest[-1]

        @pl.when(pl.program_id(2) == 0)
        def _():
            o_ref[...] = jnp.zeros_like(o_ref)
        o_ref[...] += _mm_tn(a_ref[...], b_ref[...])

    extra = [] if after is None else [after]
    return pl.pallas_call(
        body, name=name, out_shape=jax.ShapeDtypeStruct((k1, k2), F32), grid=(k1 // b1, k2 // b2, t // bt),
        in_specs=[pl.BlockSpec((bt, b1), lambda i, j, k: (k, i)), pl.BlockSpec((bt, b2), lambda i, j, k: (k, j))]
        + [ANY] * len(extra),
        out_specs=pl.BlockSpec((b1, b2), lambda i, j, k: (i, j)), compiler_params=_params(56),
    )(a, b, *extra)


def _mod_fwd(c_all, w_ada, b_cols):
    def body(c_ref, w_ref, b_ref, o_ref):
        cc = c_ref[...]
        cond = cc * _sigmoid(cc)
        o_ref[...] = _mm(cond, w_ref[...]) + b_ref[...]

    return pl.pallas_call(body, name="mod_fwd", out_shape=jax.ShapeDtypeStruct((c_all.shape[0], w_ada.shape[1]), F32),
                          compiler_params=_params(32))(c_all, w_ada, b_cols)


def _mod_bwd(c_all, dmod_cols, dmod_all):
    def body(c_ref, dc_ref, da_ref, gw_ref, gb_ref):
        cc = c_ref[...]
        cond = cc * _sigmoid(cc)
        gw_ref[...] = _mm_tn(cond, dc_ref[...])
        gb_ref[...] = jnp.sum(da_ref[...], axis=0, keepdims=True)

    return pl.pallas_call(
        body, name="mod_bwd",
        out_shape=(jax.ShapeDtypeStruct((D_MODEL, dmod_cols.shape[1]), F32), jax.ShapeDtypeStruct((1, dmod_all.shape[1]), F32)),
        compiler_params=_params(32))(c_all, dmod_cols, dmod_all)


def _load_once(hbm_ref, vmem_ref, sem):
    @pl.when(pl.program_id(0) == 0)
    def _():
        cp = pltpu.make_async_copy(hbm_ref, vmem_ref, sem)
        cp.start()
        cp.wait()


def _conv_taps(win_ref, w, tb, cols):
    shifted = [win_ref[8 - j:8 - j + tb, cols] for j in range(4)]
    acc = w[3:4] * shifted[0]
    for j in (1, 2, 3):
        acc = acc + w[3 - j:4 - j] * shifted[j]
    return acc, shifted


def _proj_conv_fwd(x2, mod3, w_in_pad, conv_w, conv_b, seq):
    t = x2.shape[0]
    tb = 256
    npb = seq // tb
    cw = 512

    def body(x_ref, mod_ref, w_hbm, cw_ref, cb_ref, z_ref, pre_ref, xbc_ref, dsilu_ref, dt_ref, u5_ref, w_vmem, win, sem):
        _load_once(w_hbm, w_vmem, sem)
        first = (pl.program_id(0) % npb) == 0

        @pl.when(first)
        def _():
            win[0:8, :] = jnp.zeros((8, D_XBC), F32)

        @pl.when(jnp.logical_not(first))
        def _():
            win[0:8, :] = win[tb:tb + 8, :]

        m = mod_ref[0]
        u = (x_ref[...] * (1.0 + m[1:2]) + m[0:1]).astype(BF16)
        z_ref[...] = lax.dot_general(u, w_vmem[0:D_SSD, :], NT, preferred_element_type=F32)
        dt_ref[...] = lax.dot_general(u, w_vmem[D_SSD + D_XBC:D_SSD + D_XBC + DT_PAD, :], NT,
                                      preferred_element_type=F32)
        u5_ref[...] = lax.dot_general(u, w_vmem[D_SSD + D_XBC + DT_PAD:, :], NT, preferred_element_type=F32)
        for k in range(D_XBC // cw):
            cols = slice(k * cw, (k + 1) * cw)
            pre_k = lax.dot_general(u, w_vmem[D_SSD + k * cw:D_SSD + (k + 1) * cw, :], NT,
                                    preferred_element_type=F32)
            win[8:8 + tb, cols] = pre_k
            pre_ref[:, cols] = pre_k
            conv, _ = _conv_taps(win, cw_ref[:, cols], tb, cols)
            conv = conv + cb_ref[:, cols]
            sg = _sigmoid(conv)
            xbc_ref[:, cols] = conv * sg
            dsilu_ref[:, cols] = sg * (1.0 + conv * (1.0 - sg))

    row = lambda w: pl.BlockSpec((tb, w), lambda i: (i, 0))
    return pl.pallas_call(
        body, name="proj_conv_fwd", grid=(t // tb,),
        out_shape=(jax.ShapeDtypeStruct((t, D_SSD), F32), jax.ShapeDtypeStruct((t, D_XBC), F32),
                   jax.ShapeDtypeStruct((t, D_XBC), F32), jax.ShapeDtypeStruct((t, D_XBC), F32),
                   jax.ShapeDtypeStruct((t, DT_PAD), F32), jax.ShapeDtypeStruct((t, D_S5), F32)),
        in_specs=[row(D_MODEL), pl.BlockSpec((1, N_MOD, D_MODEL), lambda i: (i // npb, 0, 0)), ANY,
                  pl.BlockSpec((4, D_XBC), lambda i: (0, 0)), pl.BlockSpec((1, D_XBC), lambda i: (0, 0))],
        out_specs=(row(D_SSD), row(D_XBC), row(D_XBC), row(D_XBC), row(DT_PAD), row(D_S5)),
        scratch_shapes=[pltpu.VMEM((D_INP, D_MODEL), BF16), pltpu.VMEM((tb + 8, D_XBC), F32), pltpu.SemaphoreType.DMA],
        compiler_params=_params(56),
    )(x2, mod3, w_in_pad, conv_w, conv_b)


N_PAIRS = N_HEADS // 2


def _split3(x):
    hi = x.astype(BF16)
    r = x - hi.astype(F32)
    mid = r.astype(BF16)
    lo = (r - mid.astype(F32)).astype(BF16)
    return hi, mid, lo


def _dot3(x, e, dims=(((1,), (0,)), ((), ()))):
    return sum(lax.dot_general(p, e, dims, preferred_element_type=F32) for p in _split3(x))


def _dot3_left(e, x, dims=(((1,), (0,)), ((), ()))):
    return sum(lax.dot_general(e, p, dims, preferred_element_type=F32) for p in _split3(x))


def _head_fold():
    return (jnp.arange(D_SSD)[:, None] // HEADDIM == jnp.arange(128)[None, :]).astype(BF16)


def _ssd_prep(dt_raw, par):
    dtb = par[0:1]
    a = -jnp.exp(par[1:2])
    dt = _softplus(dt_raw + dtb)
    adt = dt * a
    row = lax.broadcasted_iota(jnp.int32, (CHUNK, CHUNK), 0)
    col = lax.broadcasted_iota(jnp.int32, (CHUNK, CHUNK), 1)
    causal = row >= col
    tri = causal.astype(BF16)
    cs = _dot3_left(tri, adt)
    left = col < HEADDIM

    def lanes(v, h):
        return jnp.broadcast_to(v[:, h:h + 1], (CHUNK, 128))

    dt_c, cs_c, pair_cols = [], [], []
    for p in range(N_PAIRS):
        c0, c1 = lanes(cs, 2 * p), lanes(cs, 2 * p + 1)
        pair_cols.append(jnp.concatenate([c0, c1], axis=1))
        cs_c.append(jnp.where(left, c0, c1))
        dt_c.append(jnp.where(left, lanes(dt, 2 * p), lanes(dt, 2 * p + 1)))
    cs_c = jnp.concatenate(cs_c, axis=1)
    dt_c = jnp.concatenate(dt_c, axis=1)
    return dt, a, cs, cs.T, causal, tri, dt_c, jnp.exp(cs_c), jnp.exp(cs_c[CHUNK - 1:CHUNK, :] - cs_c), pair_cols


def _pair_decay(cols, cst, pair, causal2):
    rows = jnp.concatenate([jnp.broadcast_to(cst[2 * pair:2 * pair + 1, :], (CHUNK, CHUNK)),
                            jnp.broadcast_to(cst[2 * pair + 1:2 * pair + 2, :], (CHUNK, CHUNK))], axis=1)
    return jnp.exp(jnp.where(causal2, cols - rows, -jnp.inf))


def _stack_heads(xp, left):
    return jnp.concatenate([jnp.where(left, xp, 0.0), jnp.where(left, 0.0, xp)], axis=0).astype(BF16)


def _ssd_fwd(xbc, z, dt_raw, par, dsk, normw, seq):
    t = xbc.shape[0]
    nc = seq // CHUNK
    n_chunks = t // CHUNK

    def body(xbc_ref, z_ref, dt_ref, par_ref, dsk_ref, nw_ref, yraw_ref, ycat_ref, hprev_ref, h_ref):
        @pl.when(pl.program_id(0) % nc == 0)
        def _():
            h_ref[...] = jnp.zeros_like(h_ref)
        hprev_ref[0] = h_ref[...]
        _, _, cs, cst, causal, _, dt_c, ecs_c, w_c, pair_cols = _ssd_prep(dt_ref[...], par_ref[...])
        cs_last = cs[CHUNK - 1:CHUNK, :]
        causal2 = jnp.concatenate([causal, causal], axis=1)
        left = lax.broadcasted_iota(jnp.int32, (CHUNK, 128), 1) < HEADDIM
        x = xbc_ref[:, 0:D_SSD]
        xdt = x * dt_c
        amat = (w_c * xdt).astype(BF16)
        zz = z_ref[...]
        silu_z = zz * _sigmoid(zz)
        for g in range(N_GROUPS):
            gs = slice(g * GW, (g + 1) * GW)
            bg = xbc_ref[:, D_SSD + g * N_STATE:D_SSD + (g + 1) * N_STATE].astype(BF16)
            cg = xbc_ref[:, D_SSD + (N_GROUPS + g) * N_STATE:D_SSD + (N_GROUPS + g + 1) * N_STATE].astype(BF16)
            scores = lax.dot_general(cg, bg, NT, preferred_element_type=F32)
            scores2 = jnp.concatenate([scores, scores], axis=1)
            hg = h_ref[gs, :]
            p_all = lax.dot_general(cg, hg.astype(BF16), NT, preferred_element_type=F32)
            ys = []
            for q in range(GW // 128):
                pair = g * (GW // 128) + q
                decay = _pair_decay(pair_cols[pair], cst, pair, causal2)
                mcat = (scores2 * decay).astype(BF16)
                ys.append(jnp.dot(mcat, _stack_heads(xdt[:, pair * 128:(pair + 1) * 128], left),
                                  preferred_element_type=F32))
            yg = jnp.concatenate(ys, axis=1) + ecs_c[:, gs] * p_all + x[:, gs] * dsk_ref[:, gs]
            s_new = lax.dot_general(amat[:, gs], bg, TN, preferred_element_type=F32)
            for j in range(HPG):
                hh = g * HPG + j
                js = slice(j * HEADDIM, (j + 1) * HEADDIM)
                h_ref[g * GW + j * HEADDIM:g * GW + (j + 1) * HEADDIM, :] = (
                    hg[js, :] * jnp.exp(cs_last[:, hh:hh + 1]) + s_new[js, :])
            yraw_ref[:, gs] = yg
            v = yg * silu_z[:, gs]
            r = lax.rsqrt(jnp.mean(v * v, axis=-1, keepdims=True) + EPS)
            ycat_ref[:, gs] = (v * r * nw_ref[:, gs]).astype(BF16)

    row = lambda w: pl.BlockSpec((CHUNK, w), lambda i: (i, 0))
    full = lambda s: pl.BlockSpec(s, lambda i: (0,) * len(s))
    return pl.pallas_call(
        body, name="ssd_fwd", grid=(n_chunks,),
        out_shape=(jax.ShapeDtypeStruct((t, D_SSD), F32), jax.ShapeDtypeStruct((t, D_SSD + D_S5), BF16),
                   jax.ShapeDtypeStruct((n_chunks, D_SSD, N_STATE), F32)),
        in_specs=[row(D_XBC), row(D_SSD), row(DT_PAD), full((8, 128)), full((1, D_SSD)), full((1, D_SSD))],
        out_specs=(row(D_SSD), row(D_SSD), pl.BlockSpec((1, D_SSD, N_STATE), lambda i: (i, 0, 0))),
        scratch_shapes=[pltpu.VMEM((D_SSD, N_STATE), F32)],
        compiler_params=_params(40),
    )(xbc, z, dt_raw, par, dsk, normw)


S5_CW = 512
S5_BLOCKS = 4


def _tile_scan(in_re, in_im, out_re, out_im, carry_re, carry_im, pw_re, pw_im, n_tiles, reverse):
    steps = (1, 2, 4)
    for cc in range(S5_N // S5_CW):
        cols = slice(cc * S5_CW, (cc + 1) * S5_CW)
        a_re, a_im = pw_re[:, cols], pw_im[:, cols]
        rid = lax.broadcasted_iota(jnp.int32, (8, S5_CW), 0)
        pows = []
        for d in steps:
            k = 8 - d if reverse else d - 1
            keep = (rid < 8 - d) if reverse else (rid >= d)
            pows.append((jnp.where(keep, pw_re[k:k + 1, cols], 0.0), jnp.where(keep, pw_im[k:k + 1, cols], 0.0)))

        def tile(i, carry, cols=cols, pows=pows, a_re=a_re, a_im=a_im):
            r = (n_tiles - 1 - i) if reverse else i
            rows = pl.ds(pl.multiple_of(r * 8, 8), 8)
            xr, xi = in_re[rows, cols], in_im[rows, cols]
            for (pr, pi), d in zip(pows, steps):
                shift = 8 - d if reverse else d
                sr, si = pltpu.roll(xr, shift, axis=0), pltpu.roll(xi, shift, axis=0)
                xr, xi = xr + pr * sr - pi * si, xi + pr * si + pi * sr
            cr, ci = carry
            xr, xi = xr + a_re * cr - a_im * ci, xi + a_re * ci + a_im * cr
            out_re[rows, cols] = xr
            out_im[rows, cols] = xi
            edge = slice(0, 1) if reverse else slice(7, 8)
            return (jnp.broadcast_to(xr[edge], (8, S5_CW)), jnp.broadcast_to(xi[edge], (8, S5_CW)))

        c0 = (jnp.broadcast_to(carry_re[0:1, cols], (8, S5_CW)), jnp.broadcast_to(carry_im[0:1, cols], (8, S5_CW)))
        cr, ci = lax.fori_loop(0, n_tiles, tile, c0, unroll=True)
        carry_re[:, cols] = cr
        carry_im[:, cols] = ci


def _s5_params_math(ar, ai, ldt, br, bi):
    dt = jnp.exp(ldt)
    mag = jnp.exp(ar * dt)
    ang = ai * dt
    ab_re = mag * jnp.cos(ang)
    ab_im = mag * jnp.sin(ang)
    den = ar * ar + ai * ai
    n_re = ab_re - 1.0
    coef_re = (n_re * ar + ab_im * ai) / den
    coef_im = (ab_im * ar - n_re * ai) / den
    bb_re = coef_re * br - coef_im * bi
    bb_im = coef_re * bi + coef_im * br
    return ab_re, ab_im, bb_re, bb_im


def _s5_params_fwd(ar, ai, ldt, br, bi):
    def body(ar_ref, ai_ref, ldt_ref, br_ref, bi_ref, bbr_ref, bbi_ref, pfr_ref, pfi_ref, prr_ref, pri_ref):
        ab_re, ab_im, bb_re, bb_im = _s5_params_math(ar_ref[...], ai_ref[...], ldt_ref[...], br_ref[...], bi_ref[...])
        bbr_ref[...] = bb_re
        bbi_ref[...] = bb_im
        pr, pi = ab_re, ab_im
        for k in range(8):
            pfr_ref[k:k + 1, :] = pr
            pfi_ref[k:k + 1, :] = pi
            prr_ref[7 - k:8 - k, :] = pr
            pri_ref[7 - k:8 - k, :] = -pi
            pr, pi = pr * ab_re - pi * ab_im, pr * ab_im + pi * ab_re

    b16 = jax.ShapeDtypeStruct((S5_CH, S5_N), F32)
    p8 = jax.ShapeDtypeStruct((8, S5_N), F32)
    return pl.pallas_call(body, name="s5_params_fwd", out_shape=(b16, b16, p8, p8, p8, p8),
                          compiler_params=_params(32))(ar, ai, ldt, br, bi)


def _s5_params_bwd(ar, ai, ldt, br, bi, d_ab_re, d_ab_im, d_bb_re, d_bb_im):
    def body(ar_ref, ai_ref, ldt_ref, br_ref, bi_ref, dar_ref, dai_ref, dbr_ref, dbi_ref,
             gar_ref, gai_ref, gldt_ref, gbr_ref, gbi_ref):
        _, vjp = jax.vjp(_s5_params_math, ar_ref[...], ai_ref[...], ldt_ref[...], br_ref[...], bi_ref[...])
        g_ar, g_ai, g_ldt, g_br, g_bi = vjp((dar_ref[...], dai_ref[...], dbr_ref[...], dbi_ref[...]))
        gar_ref[...] = g_ar
        gai_ref[...] = g_ai
        gbr_ref[...] = g_br
        gbi_ref[...] = g_bi
        lane = lax.broadcasted_iota(jnp.int32, (S5_N, 128), 0) // S5_P
        grp = lax.broadcasted_iota(jnp.int32, (S5_N, 128), 1)
        fold = (lane == grp).astype(F32)
        gldt_ref[...] = jnp.dot(g_ldt, fold, preferred_element_type=F32, precision=HIGHEST)

    v1 = jax.ShapeDtypeStruct((1, S5_N), F32)
    b16 = jax.ShapeDtypeStruct((S5_CH, S5_N), F32)
    return pl.pallas_call(body, name="s5_params_bwd",
                          out_shape=(v1, v1, jax.ShapeDtypeStruct((1, 128), F32), b16, b16),
                          compiler_params=_params(32))(ar, ai, ldt, br, bi, d_ab_re, d_ab_im, d_bb_re, d_bb_im)


def _s5_fwd(u5, bb_re, bb_im, cc_re, cc_im, pf_re, pf_im, s5d, w_glu, b_glu, ycat, seq):
    t = u5.shape[0]
    tb = 256
    npb = seq // tb

    def body(u_ref, bbr_ref, bbi_ref, ccr_ref, cci_ref, pfr_ref, pfi_ref, d_ref, wg_ref, bg_ref, ycat_hbm,
             sre_ref, sim_ref, ypre_ref, y5_ref, bur, bui, car, cai):
        del ycat_hbm

        @pl.when(pl.program_id(0) % npb == 0)
        def _():
            car[...] = jnp.zeros_like(car)
            cai[...] = jnp.zeros_like(cai)
        u = u_ref[...]
        ub = u.astype(BF16)
        for j in range(S5_BLOCKS):
            ch, st = slice(j * 128, (j + 1) * 128), slice(j * 512, (j + 1) * 512)
            bur[:, st] = jnp.dot(ub[:, ch], bbr_ref[j], preferred_element_type=F32)
            bui[:, st] = jnp.dot(ub[:, ch], bbi_ref[j], preferred_element_type=F32)
        _tile_scan(bur, bui, sre_ref, sim_ref, car, cai, pfr_ref, pfi_ref, tb // 8, reverse=False)
        cs_y = []
        for j in range(S5_BLOCKS):
            st = slice(j * 512, (j + 1) * 512)
            cs_y.append(_mm(sre_ref[:, st], ccr_ref[j]) - _mm(sim_ref[:, st], cci_ref[j]))
        ypre = jnp.concatenate(cs_y, axis=1) + u * d_ref[...]
        ypre_ref[...] = ypre
        yg = _gelu(ypre)
        y5_ref[...] = (yg * _sigmoid(_mm(yg, wg_ref[...]) + bg_ref[...])).astype(BF16)

    row = lambda w: pl.BlockSpec((tb, w), lambda i: (i, 0))
    full = lambda a: pl.BlockSpec(a.shape, lambda i: (0,) * a.ndim)
    return pl.pallas_call(
        body, name="s5_fwd", grid=(t // tb,),
        out_shape=(jax.ShapeDtypeStruct((t, S5_N), F32), jax.ShapeDtypeStruct((t, S5_N), F32),
                   jax.ShapeDtypeStruct((t, D_S5), F32), jax.ShapeDtypeStruct(ycat.shape, BF16)),
        in_specs=[row(D_S5), full(bb_re), full(bb_im), full(cc_re), full(cc_im), full(pf_re), full(pf_im),
                  full(s5d), full(w_glu), full(b_glu), ANY],
        out_specs=(row(S5_N), row(S5_N), row(D_S5), pl.BlockSpec((tb, D_S5), lambda i: (i, D_SSD // D_S5))),
        input_output_aliases={10: 3},
        scratch_shapes=[pltpu.VMEM((tb, S5_N), F32), pltpu.VMEM((tb, S5_N), F32),
                        pltpu.VMEM((8, S5_N), F32), pltpu.VMEM((8, S5_N), F32)],
        compiler_params=_params(48),
    )(u5, bb_re, bb_im, cc_re, cc_im, pf_re, pf_im, s5d, w_glu, b_glu, ycat)


def _layer_norm(r, g, b):
    mu = jnp.mean(r, axis=-1, keepdims=True)
    xc = r - mu
    rstd = lax.rsqrt(jnp.mean(xc * xc, axis=-1, keepdims=True) + EPS)
    xhat = xc * rstd
    return xhat * g + b, xhat, rstd


def _layer_norm_bwd(dy, xhat, rstd, g):
    dxhat = dy * g
    return rstd * (dxhat - jnp.mean(dxhat, axis=-1, keepdims=True)
                   - xhat * jnp.mean(dxhat * xhat, axis=-1, keepdims=True))


def _out_ln1(ycat, x2, mod3, w_out, ln1, seq):
    t = x2.shape[0]
    tb = 512
    npb = seq // tb

    def body(y_ref, x_ref, mod_ref, w_ref, ln_ref, mix_ref, x1_ref):
        m = mod_ref[0]
        mix = jnp.dot(y_ref[...], w_ref[...], preferred_element_type=F32)
        mix_ref[...] = mix
        r1 = ALPHA * x_ref[...] + (1.0 + m[2:3]) * mix
        x1_ref[...] = _layer_norm(r1, ln_ref[0:1], ln_ref[1:2])[0]

    row = lambda w: pl.BlockSpec((tb, w), lambda i: (i, 0))
    return pl.pallas_call(
        body, name="out_ln1", grid=(t // tb,),
        out_shape=(jax.ShapeDtypeStruct((t, D_MODEL), F32), jax.ShapeDtypeStruct((t, D_MODEL), F32)),
        in_specs=[row(D_SSD + D_S5), row(D_MODEL), pl.BlockSpec((1, N_MOD, D_MODEL), lambda i: (i // npb, 0, 0)),
                  pl.BlockSpec(w_out.shape, lambda i: (0, 0)), pl.BlockSpec(ln1.shape, lambda i: (0, 0))],
        out_specs=(row(D_MODEL), row(D_MODEL)), compiler_params=_params(48),
    )(ycat, x2, mod3, w_out, ln1)


def _mlp_fwd_bwd(x1, tgt, mod3, w1, w2, vec1, b1, seq):
    t = x1.shape[0]
    tb = 256
    npb = seq // tb
    n_fb, _, fb = w1.shape

    def body(x1_ref, tgt_ref, mod_ref, w1_hbm, w2_hbm, v_ref, b1_ref,
             dx1_ref, u2_ref, h_ref, dhp_ref, do_ref, gacc_ref, db1_ref, bacc_ref, w1_v, w2_v, sem1, sem2):
        i = pl.program_id(0)
        @pl.when(i == 0)
        def _():
            cps = [pltpu.make_async_copy(w1_hbm.at[k], w1_v.at[:, k * fb:(k + 1) * fb], sem1.at[k])
                   for k in range(n_fb)]
            for cp in cps:
                cp.start()
            for cp in cps:
                cp.wait()
        _load_once(w2_hbm, w2_v, sem2)

        @pl.when(i == 0)
        def _():
            gacc_ref[...] = jnp.zeros_like(gacc_ref)
            db1_ref[...] = jnp.zeros_like(db1_ref)

        @pl.when(i % npb == 0)
        def _():
            bacc_ref[...] = jnp.zeros_like(bacc_ref)

        m = mod_ref[0]
        sh2, sc2, g2 = m[3:4], m[4:5], m[5:6]
        x1v = x1_ref[...]
        u2 = (x1v * (1.0 + sc2) + sh2).astype(BF16)
        u2_ref[...] = u2
        hr = jnp.maximum(jnp.dot(u2, w1_v[...], preferred_element_type=F32) + b1_ref[...], 0.0)
        hb = (hr * hr).astype(BF16)
        h_ref[...] = hb
        o = jnp.dot(hb, w2_v[...], preferred_element_type=F32) + v_ref[0:1]
        r2 = ALPHA * x1v + (1.0 + g2) * o
        y, xhat, rstd = _layer_norm(r2, v_ref[1:2], v_ref[2:3])
        err = y - tgt_ref[...]
        dy = err * (1.0 / D_MODEL)
        dr2 = _layer_norm_bwd(dy, xhat, rstd, v_ref[1:2])
        do = (1.0 + g2) * dr2
        dob = do.astype(BF16)
        do_ref[...] = dob
        gacc_ref[0:1, :] += jnp.sum(dy * xhat, axis=0, keepdims=True)
        gacc_ref[1:2, :] += jnp.sum(dy, axis=0, keepdims=True)
        gacc_ref[2:3, :] += jnp.sum(do, axis=0, keepdims=True)
        gacc_ref[3:4, :] += jnp.sum(err * err, axis=0, keepdims=True)
        dhpre = lax.dot_general(dob, w2_v[...], NT, preferred_element_type=F32) * (2.0 * hr)
        dhpb = dhpre.astype(BF16)
        dhp_ref[...] = dhpb
        db1_ref[...] += jnp.sum(dhpre, axis=0, keepdims=True)
        du2 = lax.dot_general(dhpb, w1_v[...], NT, preferred_element_type=F32)
        dx1_ref[...] = ALPHA * dr2 + du2 * (1.0 + sc2)
        bacc_ref[0, 0:1, :] += jnp.sum(du2, axis=0, keepdims=True)
        bacc_ref[0, 1:2, :] += jnp.sum(du2 * x1v, axis=0, keepdims=True)
        bacc_ref[0, 2:3, :] += jnp.sum(dr2 * o, axis=0, keepdims=True)

    row = lambda w: pl.BlockSpec((tb, w), lambda i: (i, 0))
    return pl.pallas_call(
        body, name="mlp_fwd_bwd", grid=(t // tb,),
        out_shape=(jax.ShapeDtypeStruct((t, D_MODEL), F32), jax.ShapeDtypeStruct((t, D_MODEL), BF16),
                   jax.ShapeDtypeStruct((t, D_FF), BF16), jax.ShapeDtypeStruct((t, D_FF), BF16),
                   jax.ShapeDtypeStruct((t, D_MODEL), BF16), jax.ShapeDtypeStruct((8, D_MODEL), F32),
                   jax.ShapeDtypeStruct((1, D_FF), F32), jax.ShapeDtypeStruct((t // seq, 8, D_MODEL), F32)),
        in_specs=[row(D_MODEL), row(D_MODEL), pl.BlockSpec((1, N_MOD, D_MODEL), lambda i: (i // npb, 0, 0)), ANY, ANY,
                  pl.BlockSpec(vec1.shape, lambda i: (0, 0)), pl.BlockSpec(b1.shape, lambda i: (0, 0))],
        out_specs=(row(D_MODEL), row(D_MODEL), row(D_FF), row(D_FF), row(D_MODEL),
                   pl.BlockSpec((8, D_MODEL), lambda i: (0, 0)), pl.BlockSpec((1, D_FF), lambda i: (0, 0)),
                   pl.BlockSpec((1, 8, D_MODEL), lambda i: (i // npb, 0, 0))),
        scratch_shapes=[pltpu.VMEM((D_MODEL, n_fb * fb), BF16), pltpu.VMEM((D_FF, D_MODEL), BF16),
                        pltpu.SemaphoreType.DMA((n_fb,)), pltpu.SemaphoreType.DMA],
        compiler_params=_params(60),
    )(x1, tgt, mod3, w1, w2, vec1, b1)


def _ln1_out_bwd(dx1, x2, mix, mod3, w_out, ln1, seq):
    t = x2.shape[0]
    tb = 512
    npb = seq // tb

    def body(dx1_ref, x_ref, mix_ref, mod_ref, w_ref, ln_ref, dmix_ref, dxa_ref, dys_ref, dy5_ref, gacc_ref, bacc_ref):
        i = pl.program_id(0)

        @pl.when(i == 0)
        def _():
            gacc_ref[...] = jnp.zeros_like(gacc_ref)

        @pl.when(i % npb == 0)
        def _():
            bacc_ref[...] = jnp.zeros_like(bacc_ref)

        m = mod_ref[0]
        mix = mix_ref[...]
        r1 = ALPHA * x_ref[...] + (1.0 + m[2:3]) * mix
        _, xhat, rstd = _layer_norm(r1, ln_ref[0:1], ln_ref[1:2])
        dx1v = dx1_ref[...]
        dr1 = _layer_norm_bwd(dx1v, xhat, rstd, ln_ref[0:1])
        gacc_ref[0:1, :] += jnp.sum(dx1v * xhat, axis=0, keepdims=True)
        gacc_ref[1:2, :] += jnp.sum(dx1v, axis=0, keepdims=True)
        bacc_ref[0, 0:1, :] += jnp.sum(dr1 * mix, axis=0, keepdims=True)
        dmix = ((1.0 + m[2:3]) * dr1).astype(BF16)
        dmix_ref[...] = dmix
        dxa_ref[...] = ALPHA * dr1
        dys_ref[...] = lax.dot_general(dmix, w_ref[0:D_SSD, :], NT, preferred_element_type=F32)
        dy5_ref[...] = lax.dot_general(dmix, w_ref[D_SSD:, :], NT, preferred_element_type=F32)

    row = lambda w: pl.BlockSpec((tb, w), lambda i: (i, 0))
    return pl.pallas_call(
        body, name="ln1_out_bwd", grid=(t // tb,),
        out_shape=(jax.ShapeDtypeStruct((t, D_MODEL), BF16), jax.ShapeDtypeStruct((t, D_MODEL), F32),
                   jax.ShapeDtypeStruct((t, D_SSD), F32), jax.ShapeDtypeStruct((t, D_S5), F32),
                   jax.ShapeDtypeStruct((8, D_MODEL), F32), jax.ShapeDtypeStruct((t // seq, 8, D_MODEL), F32)),
        in_specs=[row(D_MODEL), row(D_MODEL), row(D_MODEL), pl.BlockSpec((1, N_MOD, D_MODEL), lambda i: (i // npb, 0, 0)),
                  pl.BlockSpec(w_out.shape, lambda i: (0, 0)), pl.BlockSpec(ln1.shape, lambda i: (0, 0))],
        out_specs=(row(D_MODEL), row(D_MODEL), row(D_SSD), row(D_S5), pl.BlockSpec((8, D_MODEL), lambda i: (0, 0)),
                   pl.BlockSpec((1, 8, D_MODEL), lambda i: (i // npb, 0, 0))),
        compiler_params=_params(48),
    )(dx1, x2, mix, mod3, w_out, ln1)


def _s5_bwd(dy5, ypre, u5, s_re, s_im, bb_re, bb_im, cc_re, cc_im, pr_re, pr_im, s5d, w_glu, b_glu, seq):
    t = u5.shape[0]
    tb = 256
    npb = seq // tb
    n_blocks = t // tb

    def blk(i):
        return (i // npb) * npb + (npb - 1 - i % npb)

    def body(dy_ref, ypre_ref, u_ref, sre_ref, sim_ref, hre_ref, him_ref, bbr_ref, bbi_ref, ccr_ref, cci_ref,
             prr_ref, pri_ref, d_ref, wg_ref, bg_ref,
             du_ref, vacc_ref, sacc_ref, dcc_ref, dbb_ref, dwg_ref, dsr, dsi, gr, gi, car, cai):
        i = pl.program_id(0)

        @pl.when(i == 0)
        def _():
            for acc in (vacc_ref, sacc_ref, dcc_ref, dbb_ref, dwg_ref):
                acc[...] = jnp.zeros_like(acc)

        @pl.when(i % npb == 0)
        def _():
            car[...] = jnp.zeros_like(car)
            cai[...] = jnp.zeros_like(cai)

        dy = dy_ref[...]
        ypre = ypre_ref[...]
        u = u_ref[...]
        ub = u.astype(BF16)
        yg = _gelu(ypre)
        sg = _sigmoid(_mm(yg, wg_ref[...]) + bg_ref[...])
        dq = dy * yg * sg * (1.0 - sg)
        dqb = dq.astype(BF16)
        dyg = dy * sg + lax.dot_general(dqb, wg_ref[...], NT, preferred_element_type=F32)
        dyp = dyg * _gelu_grad(ypre)
        dypb = dyp.astype(BF16)
        dwg_ref[...] += lax.dot_general(yg.astype(BF16), dqb, TN, preferred_element_type=F32)
        blocks = [(slice(j * 128, (j + 1) * 128), slice(j * 512, (j + 1) * 512)) for j in range(S5_BLOCKS)]
        for j, (ch, st) in enumerate(blocks):
            dsr[:, st] = lax.dot_general(dypb[:, ch], ccr_ref[j], NT, preferred_element_type=F32)
            dsi[:, st] = -lax.dot_general(dypb[:, ch], cci_ref[j], NT, preferred_element_type=F32)
        _tile_scan(dsr, dsi, gr, gi, car, cai, prr_ref, pri_ref, tb // 8, reverse=True)
        g_re, g_im = gr[...], gi[...]
        first_rows = (i % npb) == npb - 1
        hre = jnp.where(first_rows, 0.0, hre_ref[...])
        him = jnp.where(first_rows, 0.0, him_ref[...])
        s_re_v, s_im_v = sre_ref[...], sim_ref[...]
        sp_re = pltpu.roll(jnp.concatenate([hre, s_re_v], axis=0), 1, axis=0)[8:8 + tb]
        sp_im = pltpu.roll(jnp.concatenate([him, s_im_v], axis=0), 1, axis=0)[8:8 + tb]
        vacc_ref[0:1, :] += jnp.sum(g_re * sp_re + g_im * sp_im, axis=0, keepdims=True)
        vacc_ref[1:2, :] += jnp.sum(g_im * sp_re - g_re * sp_im, axis=0, keepdims=True)
        grb, gib = g_re.astype(BF16), g_im.astype(BF16)
        srb, sib = s_re_v.astype(BF16), s_im_v.astype(BF16)
        du_cols = []
        for j, (ch, st) in enumerate(blocks):
            dcc_ref[j] += lax.dot_general(srb[:, st], dypb[:, ch], TN, preferred_element_type=F32)
            dcc_ref[S5_BLOCKS + j] -= lax.dot_general(sib[:, st], dypb[:, ch], TN, preferred_element_type=F32)
            dbb_ref[j] += lax.dot_general(ub[:, ch], grb[:, st], TN, preferred_element_type=F32)
            dbb_ref[S5_BLOCKS + j] += lax.dot_general(ub[:, ch], gib[:, st], TN, preferred_element_type=F32)
            du_cols.append(lax.dot_general(grb[:, st], bbr_ref[j], NT, preferred_element_type=F32)
                           + lax.dot_general(gib[:, st], bbi_ref[j], NT, preferred_element_type=F32))
        du_ref[...] = jnp.concatenate(du_cols, axis=1) + dyp * d_ref[...]
        sacc_ref[0:1, :] += jnp.sum(dyp * u, axis=0, keepdims=True)
        sacc_ref[1:2, :] += jnp.sum(dq, axis=0, keepdims=True)

    row = lambda w: pl.BlockSpec((tb, w), lambda i: (blk(i), 0))
    halo = pl.BlockSpec((8, S5_N), lambda i: (jnp.maximum(blk(i) * (tb // 8) - 1, 0), 0))
    full = lambda a: pl.BlockSpec(a.shape, lambda i: (0,) * a.ndim)
    acc = lambda s: pl.BlockSpec(s, lambda i: (0,) * len(s))
    acc_shapes = [(8, S5_N), (8, D_S5), (2 * S5_BLOCKS, 512, 128), (2 * S5_BLOCKS, 128, 512), (D_S5, D_S5)]
    return pl.pallas_call(
        body, name="s5_bwd", grid=(n_blocks,),
        out_shape=(jax.ShapeDtypeStruct((t, D_S5), F32),) + tuple(jax.ShapeDtypeStruct(s, F32) for s in acc_shapes),
        in_specs=[row(D_S5), row(D_S5), row(D_S5), row(S5_N), row(S5_N), halo, halo, full(bb_re), full(bb_im),
                  full(cc_re), full(cc_im), full(pr_re), full(pr_im), full(s5d), full(w_glu), full(b_glu)],
        out_specs=(row(D_S5),) + tuple(acc(s) for s in acc_shapes),
        scratch_shapes=[pltpu.VMEM((tb, S5_N), F32), pltpu.VMEM((tb, S5_N), F32), pltpu.VMEM((tb, S5_N), F32),
                        pltpu.VMEM((tb, S5_N), F32), pltpu.VMEM((8, S5_N), F32), pltpu.VMEM((8, S5_N), F32)],
        compiler_params=_params(56),
    )(dy5, ypre, u5, s_re, s_im, s_re, s_im, bb_re, bb_im, cc_re, cc_im, pr_re, pr_im, s5d, w_glu, b_glu)


def _ssd_bwd(dyssd, yraw, z, xbc, dt_raw, hprev, par, dsk, normw, seq):
    t = xbc.shape[0]
    nc = seq // CHUNK
    n_chunks = t // CHUNK
    fold = _head_fold()

    def blk(i):
        return (i // nc) * nc + (nc - 1 - i % nc)

    def body(dy_ref, yraw_ref, z_ref, xbc_ref, dt_ref, hprev_ref, par_ref, dsk_ref, nw_ref, fold_ref,
             dxbc_ref, dz_ref, ddt_ref, dpar_ref, cacc_ref, dh_ref, dyr_ref):
        i = pl.program_id(0)

        @pl.when(i == 0)
        def _():
            dpar_ref[...] = jnp.zeros_like(dpar_ref)
            cacc_ref[...] = jnp.zeros_like(cacc_ref)

        @pl.when(i % nc == 0)
        def _():
            dh_ref[...] = jnp.zeros_like(dh_ref)

        zz = z_ref[...]
        sz = _sigmoid(zz)
        silu_z = zz * sz
        yraw = yraw_ref[...]
        for g in range(N_GROUPS):
            sl = slice(g * GW, (g + 1) * GW)
            v = yraw[:, sl] * silu_z[:, sl]
            r = lax.rsqrt(jnp.mean(v * v, axis=-1, keepdims=True) + EPS)
            dyg = dy_ref[:, sl]
            cacc_ref[1:2, sl] += jnp.sum(dyg * v * r, axis=0, keepdims=True)
            dyw = dyg * nw_ref[:, sl]
            dv = r * dyw - v * (r * r * r) * jnp.mean(dyw * v, axis=-1, keepdims=True)
            dyr_ref[:, sl] = dv * silu_z[:, sl]
            dz_ref[:, sl] = dv * yraw[:, sl] * (sz[:, sl] * (1.0 + zz[:, sl] * (1.0 - sz[:, sl])))

        dt, a, cs, cst, causal, tri, dt_c, ecs_c, w_c, pair_cols = _ssd_prep(dt_ref[...], par_ref[...])
        cs_last = cs[CHUNK - 1:CHUNK, :]
        causal2 = jnp.concatenate([causal, causal], axis=1)
        lane = lax.broadcasted_iota(jnp.int32, (CHUNK, 128), 1)
        left = lane < HEADDIM
        lane1 = lax.broadcasted_iota(jnp.int32, (1, 128), 1)
        x = xbc_ref[:, 0:D_SSD]
        xdt = x * dt_c
        dyr = dyr_ref[...]
        dyrb = dyr.astype(BF16)
        cacc_ref[0:1, :] += jnp.sum(dyr * x, axis=0, keepdims=True)
        dlast = jnp.zeros((1, 128), F32)
        dxdt_cols, diag_all, dww_cols = [], [], []
        for g in range(N_GROUPS):
            gs = slice(g * GW, (g + 1) * GW)
            b_sl = slice(D_SSD + g * N_STATE, D_SSD + (g + 1) * N_STATE)
            c_sl = slice(D_SSD + (N_GROUPS + g) * N_STATE, D_SSD + (N_GROUPS + g + 1) * N_STATE)
            bg = xbc_ref[:, b_sl].astype(BF16)
            cg = xbc_ref[:, c_sl].astype(BF16)
            scores = lax.dot_general(cg, bg, NT, preferred_element_type=F32)
            scores2 = jnp.concatenate([scores, scores], axis=1)
            hg = hprev_ref[0, gs, :]
            hgb = hg.astype(BF16)
            dhg = dh_ref[gs, :]
            dhgb = dhg.astype(BF16)
            q_all = lax.dot_general(bg, dhgb, NT, preferred_element_type=F32)
            dscores = jnp.zeros((CHUNK, CHUNK), F32)
            diag_cols = []
            for q in range(GW // 128):
                pair = g * (GW // 128) + q
                ps = slice(pair * 128, (pair + 1) * 128)
                decay = _pair_decay(pair_cols[pair], cst, pair, causal2)
                mcat = (scores2 * decay).astype(BF16)
                dyp = dyrb[:, ps]
                dm = lax.dot_general(dyp, _stack_heads(xdt[:, ps], left), NT, preferred_element_type=F32)
                dmd = dm * decay
                dscores = dscores + dmd[:, 0:CHUNK] + dmd[:, CHUNK:]
                rr = lax.dot_general(mcat, dyp, TN, preferred_element_type=F32)
                diag_cols.append(jnp.where(left, rr[0:CHUNK], rr[CHUNK:]))
            wq = w_c[:, gs] * q_all
            diag_g = jnp.concatenate(diag_cols, axis=1)
            diag_all.append(diag_g)
            dxdt_cols.append(diag_g + wq)
            dww_cols.append(wq * xdt[:, gs])
            dp = (ecs_c[:, gs] * dyr[:, gs]).astype(BF16)
            amat = (w_c[:, gs] * xdt[:, gs]).astype(BF16)
            dsb = dscores.astype(BF16)
            dxbc_ref[:, c_sl] = (jnp.dot(dsb, bg, preferred_element_type=F32)
                                 + jnp.dot(dp, hgb, preferred_element_type=F32))
            dxbc_ref[:, b_sl] = (lax.dot_general(dsb, cg, TN, preferred_element_type=F32)
                                 + jnp.dot(amat, dhgb, preferred_element_type=F32))
            dh_in = lax.dot_general(dp, cg, TN, preferred_element_type=F32)
            for j in range(HPG):
                hh = g * HPG + j
                js = slice(j * HEADDIM, (j + 1) * HEADDIM)
                ecl = jnp.exp(cs_last[:, hh:hh + 1])
                dlast = dlast + jnp.where(lane1 == hh, ecl * jnp.sum(dhg[js, :] * hg[js, :]), 0.0)
                dh_ref[g * GW + j * HEADDIM:g * GW + (j + 1) * HEADDIM, :] = ecl * dhg[js, :] + dh_in[js, :]
        dxdt = jnp.concatenate(dxdt_cols, axis=1)
        dxbc_ref[:, 0:D_SSD] = dxdt * dt_c + dyr * dsk_ref[...]
        dww = _mm(jnp.concatenate(dww_cols, axis=1), fold_ref[...])
        dcs = _dot3(dyrb.astype(F32) * (yraw - x * dsk_ref[...])
                    - xdt.astype(BF16).astype(F32) * jnp.concatenate(diag_all, axis=1), fold_ref[...]) - dww
        rowid = lax.broadcasted_iota(jnp.int32, (CHUNK, 128), 0)
        dcs = dcs + jnp.where(rowid == CHUNK - 1, jnp.sum(dww, axis=0, keepdims=True) + dlast, 0.0)
        dadt = _dot3_left(tri, dcs, TN)
        ddt = _mm(dxdt * x, fold_ref[...]) + dadt * a
        da = jnp.sum(dadt * dt, axis=0, keepdims=True)
        ddt_raw = ddt * _sigmoid(dt_ref[...] + par_ref[0:1])
        ddt_raw = jnp.where(lane < N_HEADS, ddt_raw, 0.0)
        ddt_ref[...] = ddt_raw
        dpar_ref[0:1, :] += jnp.sum(ddt_raw, axis=0, keepdims=True)
        dpar_ref[1:2, :] += jnp.where(lane1 < N_HEADS, da * a, 0.0)

    row = lambda w: pl.BlockSpec((CHUNK, w), lambda i: (blk(i), 0))
    full = lambda s: pl.BlockSpec(s, lambda i: (0,) * len(s))
    return pl.pallas_call(
        body, name="ssd_bwd", grid=(n_chunks,),
        out_shape=(jax.ShapeDtypeStruct((t, D_XBC), F32), jax.ShapeDtypeStruct((t, D_SSD), F32),
                   jax.ShapeDtypeStruct((t, DT_PAD), F32), jax.ShapeDtypeStruct((8, 128), F32),
                   jax.ShapeDtypeStruct((8, D_SSD), F32)),
        in_specs=[row(D_SSD), row(D_SSD), row(D_SSD), row(D_XBC), row(DT_PAD),
                  pl.BlockSpec((1, D_SSD, N_STATE), lambda i: (blk(i), 0, 0)),
                  full((8, 128)), full((1, D_SSD)), full((1, D_SSD)), full(fold.shape)],
        out_specs=(row(D_XBC), row(D_SSD), row(DT_PAD), full((8, 128)), full((8, D_SSD))),
        scratch_shapes=[pltpu.VMEM((D_SSD, N_STATE), F32), pltpu.VMEM((CHUNK, D_SSD), F32)],
        compiler_params=_params(48),
    )(dyssd, yraw, z, xbc, dt_raw, hprev, par, dsk, normw, fold)


def _conv_proj_bwd(dz, dxbc, dsilu, xbc_pre, ddt, du5, x2, dxa, mod3, conv_w, w_in_pad, seq):
    t = x2.shape[0]
    tb = 256
    npb = seq // tb
    n_blocks = t // tb
    cw = 512

    def blk(i):
        return (i // npb) * npb + (npb - 1 - i % npb)

    def body(dz_ref, d_ref, ds_ref, cur_ref, halo_ref, ddt_ref, du5_ref, x_ref, dxa_ref, mod_ref, cw_ref, w_hbm,
             gx_ref, u_ref, dxp_ref, bacc_ref, acc_ref, w_vmem, win_x, win_d, sem):
        i = pl.program_id(0)
        _load_once(w_hbm, w_vmem, sem)

        @pl.when(i == 0)
        def _():
            acc_ref[...] = jnp.zeros_like(acc_ref)

        @pl.when(i % npb == 0)
        def _():
            bacc_ref[...] = jnp.zeros_like(bacc_ref)
            win_d[tb:tb + 8, :] = jnp.zeros((8, D_XBC), F32)

        @pl.when(i % npb != 0)
        def _():
            win_d[tb:tb + 8, :] = win_d[0:8, :]

        first_rows = (i % npb) == npb - 1
        win_x[0:8, :] = jnp.where(first_rows, 0.0, halo_ref[...])
        win_x[8:8 + tb, :] = cur_ref[...]
        w = cw_ref[...]
        for k in range(D_XBC // cw):
            cols = slice(k * cw, (k + 1) * cw)
            dpre = d_ref[:, cols] * ds_ref[:, cols]
            win_d[0:tb, cols] = dpre
            for j in range(4):
                acc_ref[3 - j:4 - j, cols] += jnp.sum(dpre * win_x[8 - j:8 - j + tb, cols], axis=0, keepdims=True)
            acc_ref[4:5, cols] += jnp.sum(dpre, axis=0, keepdims=True)
            dxp = w[3:4, cols] * dpre
            for j in (1, 2, 3):
                dxp = dxp + w[3 - j:4 - j, cols] * win_d[j:j + tb, cols]
            dxp_ref[:, cols] = dxp.astype(BF16)
        o1, o2, o3 = D_SSD, D_SSD + D_XBC, D_SSD + D_XBC + DT_PAD
        du = (jnp.dot(dz_ref[...].astype(BF16), w_vmem[0:o1, :], preferred_element_type=F32)
              + jnp.dot(dxp_ref[...], w_vmem[o1:o2, :], preferred_element_type=F32)
              + jnp.dot(ddt_ref[...].astype(BF16), w_vmem[o2:o3, :], preferred_element_type=F32)
              + jnp.dot(du5_ref[...].astype(BF16), w_vmem[o3:, :], preferred_element_type=F32))
        m = mod_ref[0]
        xv = x_ref[...]
        u_ref[...] = (xv * (1.0 + m[1:2]) + m[0:1]).astype(BF16)
        gx_ref[...] = dxa_ref[...] + du * (1.0 + m[1:2])
        bacc_ref[0, 0:1, :] += jnp.sum(du, axis=0, keepdims=True)
        bacc_ref[0, 1:2, :] += jnp.sum(du * xv, axis=0, keepdims=True)

    row = lambda w: pl.BlockSpec((tb, w), lambda i: (blk(i), 0))
    halo = pl.BlockSpec((8, D_XBC), lambda i: (jnp.maximum(blk(i) * (tb // 8) - 1, 0), 0))
    return pl.pallas_call(
        body, name="conv_proj_bwd", grid=(n_blocks,),
        out_shape=(jax.ShapeDtypeStruct((t, D_MODEL), F32), jax.ShapeDtypeStruct((t, D_MODEL), BF16),
                   jax.ShapeDtypeStruct((t, D_XBC), BF16), jax.ShapeDtypeStruct((t // seq, 8, D_MODEL), F32),
                   jax.ShapeDtypeStruct((8, D_XBC), F32)),
        in_specs=[row(D_SSD), row(D_XBC), row(D_XBC), row(D_XBC), halo, row(DT_PAD), row(D_S5), row(D_MODEL),
                  row(D_MODEL), pl.BlockSpec((1, N_MOD, D_MODEL), lambda i: (i // npb, 0, 0)),
                  pl.BlockSpec((4, D_XBC), lambda i: (0, 0)), ANY],
        out_specs=(row(D_MODEL), row(D_MODEL), row(D_XBC), pl.BlockSpec((1, 8, D_MODEL), lambda i: (i // npb, 0, 0)),
                   pl.BlockSpec((8, D_XBC), lambda i: (0, 0))),
        scratch_shapes=[pltpu.VMEM((D_INP, D_MODEL), BF16), pltpu.VMEM((tb + 8, D_XBC), F32),
                        pltpu.VMEM((tb + 8, D_XBC), F32), pltpu.SemaphoreType.DMA],
        compiler_params=_params(60),
    )(dz, dxbc, dsilu, xbc_pre, xbc_pre, ddt, du5, x2, dxa, mod3, conv_w, w_in_pad)


def _pad_rows(a, mult):
    r = a.shape[0]
    pad = (-r) % mult
    return a if pad == 0 else jnp.concatenate([a, jnp.zeros((pad,) + a.shape[1:], a.dtype)], axis=0)


_SMALL = ["conv_w", "conv_b", "dt_bias", "a_log", "d_ssd", "norm_w", "s5_a_re", "s5_a_im", "s5_log_dt", "s5_b_re",
          "s5_b_im", "s5_c_re", "s5_c_im", "s5_d", "b_glu", "ln1_g", "ln1_b", "b1", "b2", "ln2_g", "ln2_b"]


def _tile_rows(size):
    return 8 * (-(-size // 1024))


def _pack_small(d):
    parts = []
    for n in _SMALL:
        flat = d[n].reshape(-1).astype(F32)
        rows = _tile_rows(flat.shape[0])
        pad = rows * 128 - flat.shape[0]
        if pad:
            flat = jnp.concatenate([flat, jnp.zeros((pad,), F32)])
        parts.append(flat.reshape(rows, 128))
    return jnp.concatenate(parts, axis=0)


def _unpack_small(p, shapes):
    out, off = {}, 0
    for n in _SMALL:
        size = math.prod(shapes[n])
        rows = _tile_rows(size)
        out[n] = p[off:off + rows].reshape(-1)[:size].reshape(shapes[n])
        off += rows
    return out


def kernel(x, c, w_ada, b_ada, w_in, conv_w, conv_b, dt_bias, a_log, d_ssd, norm_w, s5_a_re, s5_a_im, s5_log_dt, s5_b_re, s5_b_im, s5_c_re, s5_c_im, s5_d, w_glu, b_glu, w_out, ln1_g, ln1_b, w1, b1, w2, b2, ln2_g, ln2_b, loss_target, m_w_ada, m_b_ada, m_w_in, m_conv_w, m_conv_b, m_dt_bias, m_a_log, m_d_ssd, m_norm_w, m_s5_a_re, m_s5_a_im, m_s5_log_dt, m_s5_b_re, m_s5_b_im, m_s5_c_re, m_s5_c_im, m_s5_d, m_w_glu, m_b_glu, m_w_out, m_ln1_g, m_ln1_b, m_w1, m_b1, m_w2, m_b2, m_ln2_g, m_ln2_b, v_w_ada, v_b_ada, v_w_in, v_conv_w, v_conv_b, v_dt_bias, v_a_log, v_d_ssd, v_norm_w, v_s5_a_re, v_s5_a_im, v_s5_log_dt, v_s5_b_re, v_s5_b_im, v_s5_c_re, v_s5_c_im, v_s5_d, v_w_glu, v_b_glu, v_w_out, v_ln1_g, v_ln1_b, v_w1, v_b1, v_w2, v_b2, v_ln2_g, v_ln2_b):
    weights = dict(w_ada=w_ada, b_ada=b_ada, w_in=w_in, conv_w=conv_w, conv_b=conv_b, dt_bias=dt_bias, a_log=a_log,
                   d_ssd=d_ssd, norm_w=norm_w, s5_a_re=s5_a_re, s5_a_im=s5_a_im, s5_log_dt=s5_log_dt, s5_b_re=s5_b_re,
                   s5_b_im=s5_b_im, s5_c_re=s5_c_re, s5_c_im=s5_c_im, s5_d=s5_d, w_glu=w_glu, b_glu=b_glu, w_out=w_out,
                   ln1_g=ln1_g, ln1_b=ln1_b, w1=w1, b1=b1, w2=w2, b2=b2, ln2_g=ln2_g, ln2_b=ln2_b)
    mom = dict(w_ada=m_w_ada, b_ada=m_b_ada, w_in=m_w_in, conv_w=m_conv_w, conv_b=m_conv_b, dt_bias=m_dt_bias,
               a_log=m_a_log, d_ssd=m_d_ssd, norm_w=m_norm_w, s5_a_re=m_s5_a_re, s5_a_im=m_s5_a_im,
               s5_log_dt=m_s5_log_dt, s5_b_re=m_s5_b_re, s5_b_im=m_s5_b_im, s5_c_re=m_s5_c_re, s5_c_im=m_s5_c_im,
               s5_d=m_s5_d, w_glu=m_w_glu, b_glu=m_b_glu, w_out=m_w_out, ln1_g=m_ln1_g, ln1_b=m_ln1_b, w1=m_w1, b1=m_b1,
               w2=m_w2, b2=m_b2, ln2_g=m_ln2_g, ln2_b=m_ln2_b)
    var = dict(w_ada=v_w_ada, b_ada=v_b_ada, w_in=v_w_in, conv_w=v_conv_w, conv_b=v_conv_b, dt_bias=v_dt_bias,
               a_log=v_a_log, d_ssd=v_d_ssd, norm_w=v_norm_w, s5_a_re=v_s5_a_re, s5_a_im=v_s5_a_im,
               s5_log_dt=v_s5_log_dt, s5_b_re=v_s5_b_re, s5_b_im=v_s5_b_im, s5_c_re=v_s5_c_re, s5_c_im=v_s5_c_im,
               s5_d=v_s5_d, w_glu=v_w_glu, b_glu=v_b_glu, w_out=v_w_out, ln1_g=v_ln1_g, ln1_b=v_ln1_b, w1=v_w1, b1=v_b1,
               w2=v_w2, b2=v_b2, ln2_g=v_ln2_g, ln2_b=v_ln2_b)
    names = list(weights)
    shapes = {n: weights[n].shape for n in names}

    nb, seq, _ = x.shape
    t = nb * seq
    dev = _dev_index()
    x2 = x.reshape(t, D_MODEL)
    tgt2 = loss_target.reshape(t, D_MODEL)

    cw_cols = conv_w.shape[2]
    small_in = jnp.concatenate([c.reshape(-1), conv_w.reshape(-1)]).reshape(-1, 128)
    big_names = ["w_in", "w_out", "w1", "w2", "w_glu"]
    local = {n: (a[0].T if n == "w_in" else a[0]) for n, a in weights.items() if n in big_names}
    shard_bf16 = {n: local[n].astype(BF16) for n in big_names}
    first = _all_gather([small_in, shard_bf16["w_in"], shard_bf16["w_glu"]], "gather_first")
    small_all = first[0].reshape(N_DEV, -1)
    c_all = small_all[:, :nb * D_MODEL].reshape(N_DEV * nb, D_MODEL)
    conv_w_full = small_all[:, nb * D_MODEL:].reshape(N_DEV, 4, cw_cols).transpose(1, 0, 2).reshape(4, D_XBC)

    w_in_t = first[1].reshape(D_IN, D_MODEL)
    w_in_pad = jnp.concatenate(
        [w_in_t[:D_SSD + D_XBC + N_HEADS], jnp.zeros((DT_PAD - N_HEADS, D_MODEL), BF16),
         w_in_t[D_SSD + D_XBC + N_HEADS:]], axis=0)
    w_glu_f = first[2].reshape(D_S5, D_S5)
    late_names = ["w_out", "w1", "w2"]

    ada_cols = w_ada.shape[2]
    b_cols = lax.dynamic_slice_in_dim(b_ada, dev * ada_cols, ada_cols, axis=1)
    mod_cols = _mod_fwd(c_all, w_ada[0], b_cols)
    mod_all = _all_gather([mod_cols], "gather_mod")[0]
    mod_mine = lax.dynamic_slice_in_dim(mod_all, dev * nb, nb, axis=1)
    mod3 = mod_mine.transpose(1, 0, 2).reshape(nb, N_MOD, D_MODEL)

    def pad_lanes(v, n):
        return jnp.concatenate([v, jnp.zeros((v.shape[0], n - v.shape[1]), F32)], axis=1)

    par = _pad_rows(jnp.concatenate([pad_lanes(dt_bias, 128), pad_lanes(a_log, 128)], axis=0), 8)
    dsk = jnp.repeat(d_ssd[0], HEADDIM).reshape(1, D_SSD)
    ar = s5_a_re.reshape(1, S5_N)
    ai = s5_a_im.reshape(1, S5_N)
    ldt = jnp.repeat(s5_log_dt[0], S5_P).reshape(1, S5_N)
    br_t = s5_b_re[0].transpose(2, 0, 1).reshape(S5_CH, S5_N)
    bi_t = s5_b_im[0].transpose(2, 0, 1).reshape(S5_CH, S5_N)
    bb_re_t, bb_im_t, pf_re, pf_im, pr_re, pr_im = _s5_params_fwd(ar, ai, ldt, br_t, bi_t)
    gpb = S5_GROUPS // S5_BLOCKS
    mask_b = (jnp.arange(128)[:, None] // S5_CH) == (jnp.arange(512)[None, :] // S5_P)

    def dense_b(bt_):
        blocks = bt_.reshape(S5_CH, S5_BLOCKS, 512).transpose(1, 0, 2)
        return jnp.where(mask_b, jnp.tile(blocks, (1, gpb, 1)), 0.0).astype(BF16)

    def dense_c(cc):
        blocks = cc[0].transpose(0, 2, 1).reshape(S5_BLOCKS, 512, S5_CH)
        return jnp.where(mask_b.T, jnp.tile(blocks, (1, 1, gpb)), 0.0).astype(BF16)

    bb_re, bb_im = dense_b(bb_re_t), dense_b(bb_im_t)
    cc_re, cc_im = dense_c(s5_c_re), dense_c(s5_c_im)
    s5d = s5_d.reshape(1, D_S5)
    ln1 = jnp.concatenate([ln1_g, ln1_b], axis=0)
    vec1 = _pad_rows(jnp.concatenate([b2, ln2_g, ln2_b], axis=0), 8)

    z, xbc_pre, xbc, dsilu, dt_raw, u5 = _proj_conv_fwd(x2, mod3, w_in_pad, conv_w_full, conv_b, seq)
    late_in, dt_raw = lax.optimization_barrier(([shard_bf16[n] for n in late_names], dt_raw))
    late_sems = _gather_start(late_in, "gather_late_start")
    yraw, ycat, hprev = _ssd_fwd(xbc, z, dt_raw, par + late_sems[4][0, 0], dsk, norm_w, seq)
    s_re, s_im, ypre, ycat = _s5_fwd(u5, bb_re, bb_im, cc_re, cc_im, pf_re, pf_im, s5d, w_glu_f, b_glu, ycat, seq)
    sent, landed = _gather_wait(late_sems[0], late_sems[1], late_sems[2], late_sems[3], ycat, "gather_late_wait")
    gathered = {n: lax.dynamic_update_index_in_dim(l, x, dev, 0) for n, x, l in zip(late_names, sent, landed)}
    w_out_f = gathered["w_out"].reshape(2 * D_MODEL, D_MODEL)
    w1_blocks = gathered["w1"]
    w2_f = gathered["w2"].reshape(D_FF, D_MODEL)
    mix, x1 = _out_ln1(ycat, x2, mod3, w_out_f, ln1, seq)

    dx1, u2b, hb, dhpb, dob, gacc2, db1, bacc2 = _mlp_fwd_bwd(x1, tgt2, mod3, w1_blocks, w2_f, vec1, b1, seq)
    loss = lax.psum(0.5 / D_MODEL * jnp.sum(gacc2[3]), ("x", "y", "c"))

    dmixb, dxa, dyssd, dy5, gacc1, bacc1 = _ln1_out_bwd(dx1, x2, mix, mod3, w_out_f, ln1, seq)

    g_w2 = _atb(hb, dob, "gw2")
    g_w1 = _atb(u2b, dhpb, "gw1")
    g_wout = _atb(ycat, dmixb, "gwout")
    core = lax.axis_index("c").astype(jnp.int32).reshape(1)
    chip = 2 * lax.axis_index("x") + lax.axis_index("y")

    def chip_sums_of(names, grads, tag):
        by_dest = [g if g.ndim == 2 else g.reshape((4, 2) + g.shape[1:]) for g in grads]
        from_sibling = _sibling_swap(by_dest, "rs_swap_" + tag)
        return [_add_halves(g, r, core, "rs_add_" + n) for g, r, n in zip(by_dest, from_sibling, names)]

    early_names = ["w_out", "w1", "w2"]
    early_dest = [g_wout.reshape((4, 2) + w_out.shape[1:]), g_w1, g_w2.reshape((4, 2) + w2.shape[1:])]
    swap = _sibling_swap_start(early_dest, "rs_early_swap_start")
    du5, vacc, sacc, d_cc, d_bb, g_wglu = _s5_bwd(dy5, ypre, u5, s_re, s_im, bb_re, bb_im, cc_re, cc_im,
                                                  pr_re, pr_im, s5d + swap[4][0, 0], w_glu_f, b_glu, seq)
    early_dest, from_sibling = _sibling_swap_wait(swap[0], swap[1], swap[2], swap[3], du5, "rs_early_swap_wait")
    early_sums = [_add_halves(g, r, core, "rs_add_" + n) for g, r, n in zip(early_dest, from_sibling, early_names)]
    early = _all_to_all_start(early_sums, "rs_early_start")
    dxbc, dz, ddt, dpar, cacc = _ssd_bwd(dyssd, yraw, z, xbc, dt_raw, hprev, par + early[4][0, 0], dsk, norm_w, seq)
    grad_x2, ub, dxpb, bacc0, conv_acc = _conv_proj_bwd(dz, dxbc, dsilu, xbc_pre, ddt, du5, x2, dxa, mod3,
                                                        conv_w_full, w_in_pad, seq)

    def diag_b(dd):
        kept = jnp.where(mask_b, dd, 0.0).reshape(S5_BLOCKS, gpb, S5_CH, 512).sum(1)
        return kept.transpose(1, 0, 2).reshape(S5_CH, S5_N)

    def diag_c(dd):
        kept = jnp.where(mask_b.T, dd, 0.0).reshape(S5_BLOCKS, 512, gpb, S5_CH).sum(2)
        return kept.reshape(S5_GROUPS, S5_P, S5_CH).transpose(0, 2, 1)

    g_ar, g_ai, g_ldt, g_br_t, g_bi_t = _s5_params_bwd(ar, ai, ldt, br_t, bi_t, vacc[0:1], vacc[1:2],
                                                      diag_b(d_bb[:S5_BLOCKS]), diag_b(d_bb[S5_BLOCKS:]))

    def from_t(gt):
        return gt.reshape(S5_CH, S5_GROUPS, S5_P).transpose(1, 2, 0)

    small_g = dict(
        conv_w=conv_acc[0:4], conv_b=conv_acc[4:5], dt_bias=dpar[0:1, :N_HEADS], a_log=dpar[1:2, :N_HEADS],
        d_ssd=cacc[0].reshape(N_HEADS, HEADDIM).sum(1), norm_w=cacc[1:2],
        s5_a_re=g_ar, s5_a_im=g_ai, s5_log_dt=g_ldt[:, :S5_GROUPS], s5_b_re=from_t(g_br_t), s5_b_im=from_t(g_bi_t),
        s5_c_re=diag_c(d_cc[:S5_BLOCKS]), s5_c_im=diag_c(d_cc[S5_BLOCKS:]), s5_d=sacc[0:1], b_glu=sacc[1:2],
        ln1_g=gacc1[0:1], ln1_b=gacc1[1:2], b1=db1, b2=gacc2[2:3], ln2_g=gacc2[0:1], ln2_b=gacc2[1:2])

    dmod = jnp.concatenate([bacc0[:, 0], bacc0[:, 1], bacc1[:, 0], bacc2[:, 0], bacc2[:, 1], bacc2[:, 2]], axis=1)
    small_sems = _gather_start([dmod, _pack_small(small_g)], "gather_small_start")
    tok = small_sems[4]
    g_win_t = jnp.concatenate([_atb(dz, ub, "gwin_z", tok), _atb(dxpb, ub, "gwin_xbc", tok),
                               _atb(ddt, ub, "gwin_dt", tok)[:N_HEADS], _atb(du5, ub, "gwin_s5", tok)], axis=0)
    sent, landed = _gather_wait(small_sems[0], small_sems[1], small_sems[2], small_sems[3], g_win_t,
                                "gather_small_wait")
    dmod_all, small_parts = [lax.dynamic_update_index_in_dim(l, x, dev, 0) for x, l in zip(sent, landed)]
    dmod_all = dmod_all.reshape(N_DEV * nb, N_MOD * D_MODEL)
    dmod_cols = lax.dynamic_slice_in_dim(dmod_all, dev * ada_cols, ada_cols, axis=1)
    g_wada, g_bada = _mod_bwd(c_all, dmod_cols, dmod_all)

    late_rs = ["w_in", "w_glu"]
    late_g, small_parts = lax.optimization_barrier(
        ([g_win_t.reshape(N_DEV, w_in.shape[2], D_MODEL), g_wglu.reshape((N_DEV,) + w_glu.shape[1:])], small_parts))
    late = _all_to_all_start(chip_sums_of(late_rs, late_g, "late"), "rs_late_start")

    def own_block_in(landed, sent):
        return [lax.dynamic_update_index_in_dim(l, lax.dynamic_index_in_dim(h, chip, 0, keepdims=False), chip, 0)
                for l, h in zip(landed, sent)]

    sent, landed = _all_to_all_wait(early[0], early[1], early[2], early[3], late[4], "rs_early_wait")
    parts = dict(zip(early_names, own_block_in(landed, sent)))
    res = {k: {} for k in "gdmv"}

    def update(n):
        w_m_v = [(a[n][0].T if n == "w_in" else a[n][0]) for a in (weights, mom, var)]
        outs = _adamw(parts[n], *w_m_v, "adamw_" + n)
        for k, a in zip("gdmv", outs):
            res[k][n] = (a.T if n == "w_in" else a)[None]

    for n in early_names:
        update(n)
    sent, landed = _all_to_all_wait(late[0], late[1], late[2], late[3], res["d"]["w2"], "rs_late_wait")
    parts.update(zip(late_rs, own_block_in(landed, sent)))
    for n in late_rs:
        update(n)

    ag, ad, am, av = _adamw(g_wada[None], w_ada[0], m_w_ada[0], v_w_ada[0], "adamw_w_ada")
    for k, a in (("g", ag), ("d", ad), ("m", am), ("v", av)):
        res[k]["w_ada"] = a[None]
    bg_, bd_, bm_, bv_ = _adamw(g_bada.reshape(1, -1, 128), b_ada.reshape(-1, 128), m_b_ada.reshape(-1, 128),
                                v_b_ada.reshape(-1, 128), "adamw_b_ada")
    for k, a in (("g", bg_), ("d", bd_), ("m", bm_), ("v", bv_)):
        res[k]["b_ada"] = a.reshape(shapes["b_ada"])

    small_shapes = dict(shapes)
    small_shapes["conv_w"] = (1, 4, D_XBC)
    rep = {n: (jnp.zeros((1, 4, D_XBC), F32) if n == "conv_w" else weights[n]) for n in _SMALL}
    rep_m = {n: (jnp.zeros((1, 4, D_XBC), F32) if n == "conv_w" else mom[n]) for n in _SMALL}
    rep_v = {n: (jnp.ones((1, 4, D_XBC), F32) if n == "conv_w" else var[n]) for n in _SMALL}
    sg_, sd_, sm_, sv_ = _adamw(small_parts, _pack_small(rep), _pack_small(rep_m), _pack_small(rep_v), "adamw_small")
    for k, p in (("g", sg_), ("d", sd_), ("m", sm_), ("v", sv_)):
        un = _unpack_small(p, small_shapes)
        for n in _SMALL:
            if n != "conv_w":
                res[k][n] = un[n]
    g_conv_full = _unpack_small(sg_, small_shapes)["conv_w"][0]
    g_conv_mine = lax.dynamic_slice_in_dim(g_conv_full, dev * cw_cols, cw_cols, axis=1)
    cg_, cd_, cm_, cv_ = _adamw(g_conv_mine[None], conv_w[0], m_conv_w[0], v_conv_w[0], "adamw_conv_w")
    for k, a in (("g", cg_), ("d", cd_), ("m", cm_), ("v", cv_)):
        res[k]["conv_w"] = a[None]

    grad_x = grad_x2.reshape(nb, seq, D_MODEL)
    return (loss, grad_x, *[res["g"][n] for n in names], *[res["d"][n] for n in names],
            *[res["m"][n] for n in names], *[res["v"][n] for n in names])
```

```python
import functools
import math

import jax
import jax.numpy as jnp
from jax import lax
from jax.experimental import pallas as pl
from jax.experimental.pallas import tpu as pltpu

F32, BF16 = jnp.float32, jnp.bfloat16
MESH = pl.DeviceIdType.MESH
N_DEV = 8

D_MODEL = 1024
D_SSD = 1536
N_HEADS = 24
HEADDIM = 64
N_GROUPS = 4
HPG = 6
GW = HPG * HEADDIM
N_STATE = 128
CHUNK = 128
D_XBC = 2560
D_S5 = 512
S5_GROUPS = 32
S5_CH = 16
S5_P = 64
S5_N = S5_GROUPS * S5_P
D_IN = 4632
DT_PAD = 128
D_INP = D_SSD + D_XBC + DT_PAD + D_S5
D_FF = 4096
N_MOD = 6
ALPHA = 2.0 ** 0.25
EPS = 1e-5
LR, B1, B2, AEPS, WD, STEP = 0.001, 0.9, 0.999, 1e-08, 0.01, 10

NT = (((1,), (1,)), ((), ()))
TN = (((0,), (0,)), ((), ()))
ANY = pl.BlockSpec(memory_space=pl.ANY)
HIGHEST = lax.Precision.HIGHEST


def _mm(a, b):
    return jnp.dot(a.astype(BF16), b.astype(BF16), preferred_element_type=F32)


def _mm_nt(a, b):
    return lax.dot_general(a.astype(BF16), b.astype(BF16), NT, preferred_element_type=F32)


def _mm_tn(a, b):
    return lax.dot_general(a.astype(BF16), b.astype(BF16), TN, preferred_element_type=F32)


def _row_block(r, cap):
    best = r
    for cand in range(8, min(r, cap) + 1, 8):
        if r % cand == 0:
            best = cand
    return best if best <= cap else r


def _params(vmem_mb):
    return pltpu.CompilerParams(vmem_limit_bytes=vmem_mb << 20)


def _sigmoid(x):
    return 0.5 * (jnp.tanh(0.5 * x) + 1.0)


def _softplus(x):
    return jnp.maximum(x, 0.0) + jnp.log(1.0 + jnp.exp(-jnp.abs(x)))


_GK = math.sqrt(2.0 / math.pi)


def _gelu(x):
    return 0.5 * x * (1.0 + jnp.tanh(_GK * (x + 0.044715 * x * x * x)))


def _gelu_grad(x):
    t = jnp.tanh(_GK * (x + 0.044715 * x * x * x))
    return 0.5 * (1.0 + t) + 0.5 * x * (1.0 - t * t) * _GK * (1.0 + 3.0 * 0.044715 * x * x)


def _dev_index():
    return 4 * lax.axis_index("x") + 2 * lax.axis_index("y") + lax.axis_index("c")


def _all_gather(xs, name):
    n = len(xs)

    def body(*refs):
        x_refs, out_refs = refs[:n], refs[n:2 * n]
        send_sems, recv_sems, local_sems = refs[2 * n:]
        ix, iy, ic = lax.axis_index("x"), lax.axis_index("y"), lax.axis_index("c")
        me, sibling = (ix, iy, ic), (ix, iy, 1 - ic)
        chips = [(1 - ix, iy), (ix, 1 - iy), (1 - ix, 1 - iy)]

        def slot(a, px, py, pc):
            return out_refs[a].at[4 * px + 2 * py + pc]

        def copy(a, k, block, to, src=None):
            return pltpu.make_async_remote_copy(
                src_ref=slot(a, *block) if src is None else src, dst_ref=slot(a, *block),
                send_sem=send_sems.at[7 * a + k], recv_sem=recv_sems.at[7 * a + k], device_id=to, device_id_type=MESH)

        mine = [pltpu.make_async_copy(x_refs[a], slot(a, *me), local_sems.at[a]) for a in range(n)]
        for cp in mine:
            cp.start()
        first = []
        for j, chip in enumerate(chips):
            first += [copy(a, 1 + j, me, (*chip, ic), src=x_refs[a]) for a in range(n)]
        first += [copy(a, 0, me, sibling, src=x_refs[a]) for a in range(n)]
        for cp in first:
            cp.start()
        passed = []
        for j, chip in enumerate(chips):
            for a in range(n):
                copy(a, 1 + j, (*chip, ic), me).wait_recv()
                cp = copy(a, 4 + j, (*chip, ic), sibling)
                cp.start()
                passed.append(cp)
        for a in range(n):
            copy(a, 0, sibling, me).wait_recv()
            for j, chip in enumerate(chips):
                copy(a, 4 + j, (*chip, 1 - ic), me).wait_recv()
        for cp in first + passed:
            cp.wait_send()
        for cp in mine:
            cp.wait()

    return pl.pallas_call(
        body, name=name, out_shape=tuple(jax.ShapeDtypeStruct((N_DEV,) + x.shape, x.dtype) for x in xs),
        in_specs=[ANY] * n, out_specs=tuple([ANY] * n),
        scratch_shapes=[pltpu.SemaphoreType.DMA((7 * n,)), pltpu.SemaphoreType.DMA((7 * n,)),
                        pltpu.SemaphoreType.DMA((n,))],
    )(*xs)


HBM = pl.BlockSpec(memory_space=pltpu.HBM)
SEM = pl.BlockSpec(memory_space=pltpu.SEMAPHORE)
DATAFLOW = pltpu.SideEffectType.DATAFLOW_SIDE_EFFECTING


def _peer(k):
    ix, iy, ic = lax.axis_index("x"), lax.axis_index("y"), lax.axis_index("c")
    return (1 - ix if k & 4 else ix, 1 - iy if k & 2 else iy, 1 - ic if k & 1 else ic)


def _block_of(p):
    return 4 * p[0] + 2 * p[1] + p[2]


def _gather_start(xs, name):
    n = len(xs)
    lands = [lax.empty((N_DEV,) + x.shape, x.dtype) for x in xs]

    def body(*refs):
        x_refs, land_refs = refs[:n], refs[n:2 * n]
        send_sems, recv_sems = refs[2 * n], refs[2 * n + 1]
        token = refs[-1]
        me = _block_of(_peer(0))
        for a in range(n):
            for k in range(1, N_DEV):
                pltpu.make_async_remote_copy(
                    src_ref=x_refs[a], dst_ref=land_refs[a].at[me], send_sem=send_sems.at[7 * a + k - 1],
                    recv_sem=recv_sems.at[7 * a + k - 1], device_id=_peer(k), device_id_type=MESH).start()
        token[...] = jnp.zeros_like(token)

    outs = pl.pallas_call(
        body, name=name,
        out_shape=(pltpu.SemaphoreType.DMA((7 * n,)), pltpu.SemaphoreType.DMA((7 * n,)))
        + tuple(pltpu.HBM(x.shape, x.dtype) for x in xs) + tuple(pltpu.HBM(l.shape, l.dtype) for l in lands)
        + (jax.ShapeDtypeStruct((8, 128), F32),),
        in_specs=[HBM] * (2 * n), out_specs=(SEM, SEM) + (HBM,) * (2 * n) + (pl.BlockSpec(memory_space=pltpu.VMEM),),
        input_output_aliases={i: 2 + i for i in range(2 * n)},
        compiler_params=pltpu.CompilerParams(has_side_effects=DATAFLOW),
    )(*[pltpu.with_memory_space_constraint(x, pltpu.HBM) for x in xs],
      *[pltpu.with_memory_space_constraint(l, pltpu.HBM) for l in lands])
    return outs[0], outs[1], outs[2:2 + n], outs[2 + n:2 + 2 * n], outs[-1]


def _gather_wait(send_sems, recv_sems, xs_thru, lands_thru, after, name):
    n = len(xs_thru)

    def body(*refs):
        x_refs, land_refs = refs[:n], refs[n:2 * n]
        send_sems, recv_sems = refs[2 * n], refs[2 * n + 1]
        for a in range(n):
            for k in range(1, N_DEV):
                cp = pltpu.make_async_remote_copy(
                    src_ref=x_refs[a], dst_ref=land_refs[a].at[_block_of(_peer(k))], send_sem=send_sems.at[7 * a + k - 1],
                    recv_sem=recv_sems.at[7 * a + k - 1], device_id=_peer(k), device_id_type=MESH)
                cp.wait_send()
                cp.wait_recv()

    outs = pl.pallas_call(
        body, name=name,
        out_shape=tuple(pltpu.HBM(x.shape, x.dtype) for x in xs_thru)
        + tuple(pltpu.HBM(l.shape, l.dtype) for l in lands_thru),
        in_specs=[HBM] * (2 * n) + [SEM, SEM, ANY], out_specs=(HBM,) * (2 * n),
        input_output_aliases={i: i for i in range(2 * n)},
        compiler_params=pltpu.CompilerParams(has_side_effects=DATAFLOW),
    )(*xs_thru, *lands_thru, send_sems, recv_sems, after)
    return outs[:n], outs[n:]


def _chip_peer(k):
    ix, iy = lax.axis_index("x"), lax.axis_index("y")
    return (1 - ix if k & 2 else ix, 1 - iy if k & 1 else iy)


def _all_to_all_start(hs, name):
    n = len(hs)
    lands = [lax.empty(h.shape, h.dtype) for h in hs]

    def body(*refs):
        h_refs, land_refs = refs[:n], refs[n:2 * n]
        send_sems, recv_sems = refs[2 * n], refs[2 * n + 1]
        token = refs[-1]
        ic = lax.axis_index("c")
        mx, my = _chip_peer(0)
        for a in range(n):
            for k in range(1, 4):
                px, py = _chip_peer(k)
                pltpu.make_async_remote_copy(
                    src_ref=h_refs[a].at[2 * px + py], dst_ref=land_refs[a].at[2 * mx + my],
                    send_sem=send_sems.at[3 * a + k - 1], recv_sem=recv_sems.at[3 * a + k - 1],
                    device_id=(px, py, ic), device_id_type=MESH).start()
        token[...] = jnp.zeros_like(token)

    outs = pl.pallas_call(
        body, name=name,
        out_shape=(pltpu.SemaphoreType.DMA((3 * n,)), pltpu.SemaphoreType.DMA((3 * n,)))
        + tuple(pltpu.HBM(h.shape, h.dtype) for h in hs) + tuple(pltpu.HBM(l.shape, l.dtype) for l in lands)
        + (jax.ShapeDtypeStruct((8, 128), F32),),
        in_specs=[HBM] * (2 * n), out_specs=(SEM, SEM) + (HBM,) * (2 * n) + (pl.BlockSpec(memory_space=pltpu.VMEM),),
        input_output_aliases={i: 2 + i for i in range(2 * n)},
        compiler_params=pltpu.CompilerParams(has_side_effects=DATAFLOW),
    )(*[pltpu.with_memory_space_constraint(h, pltpu.HBM) for h in hs],
      *[pltpu.with_memory_space_constraint(l, pltpu.HBM) for l in lands])
    return outs[0], outs[1], outs[2:2 + n], outs[2 + n:2 + 2 * n], outs[-1]


def _all_to_all_wait(send_sems, recv_sems, hs_thru, lands_thru, after, name):
    n = len(hs_thru)

    def body(*refs):
        h_refs, land_refs = refs[:n], refs[n:2 * n]
        send_sems, recv_sems = refs[2 * n], refs[2 * n + 1]
        ic = lax.axis_index("c")
        for a in range(n):
            for k in range(1, 4):
                px, py = _chip_peer(k)
                cp = pltpu.make_async_remote_copy(
                    src_ref=h_refs[a].at[2 * px + py], dst_ref=land_refs[a].at[2 * px + py],
                    send_sem=send_sems.at[3 * a + k - 1], recv_sem=recv_sems.at[3 * a + k - 1],
                    device_id=(px, py, ic), device_id_type=MESH)
                cp.wait_send()
                cp.wait_recv()

    outs = pl.pallas_call(
        body, name=name,
        out_shape=tuple(pltpu.HBM(h.shape, h.dtype) for h in hs_thru)
        + tuple(pltpu.HBM(l.shape, l.dtype) for l in lands_thru),
        in_specs=[HBM] * (2 * n) + [SEM, SEM, ANY], out_specs=(HBM,) * (2 * n),
        input_output_aliases={i: i for i in range(2 * n)},
        compiler_params=pltpu.CompilerParams(has_side_effects=DATAFLOW),
    )(*hs_thru, *lands_thru, send_sems, recv_sems, after)
    return outs[:n], outs[n:]


def _sibling_block(g_ref, q):
    ic = lax.axis_index("c")
    if len(g_ref.shape) == 4:
        return g_ref.at[q, 1 - ic]
    cw = g_ref.shape[1] // N_DEV
    return g_ref.at[:, pl.ds(pl.multiple_of((2 * q + 1 - ic) * cw, 128), cw)]


def _sibling_swap_start(gs, name):
    n = len(gs)
    lands = [lax.empty((4,) + (g.shape[2:] if g.ndim == 4 else (g.shape[0], g.shape[1] // N_DEV)), g.dtype) for g in gs]

    def body(*refs):
        g_refs, land_refs = refs[:n], refs[n:2 * n]
        send_sems, recv_sems = refs[2 * n], refs[2 * n + 1]
        token = refs[-1]
        for a in range(n):
            for q in range(4):
                pltpu.make_async_remote_copy(
                    src_ref=_sibling_block(g_refs[a], q), dst_ref=land_refs[a].at[q],
                    send_sem=send_sems.at[4 * a + q], recv_sem=recv_sems.at[4 * a + q],
                    device_id=_peer(1), device_id_type=MESH).start()
        token[...] = jnp.zeros_like(token)

    outs = pl.pallas_call(
        body, name=name,
        out_shape=(pltpu.SemaphoreType.DMA((4 * n,)), pltpu.SemaphoreType.DMA((4 * n,)))
        + tuple(pltpu.HBM(g.shape, g.dtype) for g in gs) + tuple(pltpu.HBM(l.shape, l.dtype) for l in lands)
        + (jax.ShapeDtypeStruct((8, 128), F32),),
        in_specs=[HBM] * (2 * n), out_specs=(SEM, SEM) + (HBM,) * (2 * n) + (pl.BlockSpec(memory_space=pltpu.VMEM),),
        input_output_aliases={i: 2 + i for i in range(2 * n)},
        compiler_params=pltpu.CompilerParams(has_side_effects=DATAFLOW),
    )(*[pltpu.with_memory_space_constraint(g, pltpu.HBM) for g in gs],
      *[pltpu.with_memory_space_constraint(l, pltpu.HBM) for l in lands])
    return outs[0], outs[1], outs[2:2 + n], outs[2 + n:2 + 2 * n], outs[-1]


def _sibling_swap_wait(send_sems, recv_sems, gs_thru, lands_thru, after, name):
    n = len(gs_thru)

    def body(*refs):
        g_refs, land_refs = refs[:n], refs[n:2 * n]
        send_sems, recv_sems = refs[2 * n], refs[2 * n + 1]
        for a in range(n):
            for q in range(4):
                cp = pltpu.make_async_remote_copy(
                    src_ref=_sibling_block(g_refs[a], q), dst_ref=land_refs[a].at[q],
                    send_sem=send_sems.at[4 * a + q], recv_sem=recv_sems.at[4 * a + q],
                    device_id=_peer(1), device_id_type=MESH)
                cp.wait_send()
                cp.wait_recv()

    outs = pl.pallas_call(
        body, name=name,
        out_shape=tuple(pltpu.HBM(g.shape, g.dtype) for g in gs_thru)
        + tuple(pltpu.HBM(l.shape, l.dtype) for l in lands_thru),
        in_specs=[HBM] * (2 * n) + [SEM, SEM, ANY], out_specs=(HBM,) * (2 * n),
        input_output_aliases={i: i for i in range(2 * n)},
        compiler_params=pltpu.CompilerParams(has_side_effects=DATAFLOW),
    )(*gs_thru, *lands_thru, send_sems, recv_sems, after)
    return outs[:n], outs[n:]


def _sibling_swap(gs, name):
    n = len(gs)

    def body(*refs):
        g_refs, recv_refs = refs[:n], refs[n:2 * n]
        send_sems, recv_sems = refs[2 * n:]
        ix, iy, ic = lax.axis_index("x"), lax.axis_index("y"), lax.axis_index("c")
        cps = []
        for a in range(n):
            for q in range(4):
                cps.append(pltpu.make_async_remote_copy(
                    src_ref=_sibling_block(g_refs[a], q), dst_ref=recv_refs[a].at[q],
                    send_sem=send_sems.at[4 * a + q], recv_sem=recv_sems.at[4 * a + q],
                    device_id=(ix, iy, 1 - ic), device_id_type=MESH))
        for cp in cps:
            cp.start()
        for cp in cps:
            cp.wait()

    return pl.pallas_call(
        body, name=name,
        out_shape=tuple(jax.ShapeDtypeStruct(
            (4,) + (g.shape[2:] if g.ndim == 4 else (g.shape[0], g.shape[1] // N_DEV)), g.dtype) for g in gs),
        in_specs=[ANY] * n, out_specs=tuple([ANY] * n),
        scratch_shapes=[pltpu.SemaphoreType.DMA((4 * n,)), pltpu.SemaphoreType.DMA((4 * n,))],
    )(*gs)


def _add_halves(g, recv, core, name):
    _, r, c = recv.shape
    br = _row_block(r, 512)
    stacked = g.ndim == 4

    def body(core_ref, g_ref, r_ref, o_ref):
        o_ref[0] = ((g_ref[0, 0] if stacked else g_ref[...]) + r_ref[0]).astype(BF16)

    spec = pl.BlockSpec((1, br, c), lambda i, j, core_ref: (i, j, 0))
    if stacked:
        g_spec = pl.BlockSpec((1, 1, br, c), lambda i, j, core_ref: (i, core_ref[0], j, 0))
    else:
        g_spec = pl.BlockSpec((br, c), lambda i, j, core_ref: (j, 2 * i + core_ref[0]))
    return pl.pallas_call(
        body, name=name, out_shape=jax.ShapeDtypeStruct(recv.shape, BF16),
        grid_spec=pltpu.PrefetchScalarGridSpec(
            num_scalar_prefetch=1, grid=(4, r // br), in_specs=[g_spec, spec], out_specs=spec),
        compiler_params=_params(32),
    )(core, g, recv)


def _adamw(parts, w, m, v, name):
    n_parts, r, c = parts.shape
    if r % 8 == 0:
        br, bc = _row_block(r, 512 if c <= 1024 else 256), c
    else:
        br, bc = r, (256 if c % 256 == 0 else c)

    def body(p_ref, w_ref, m_ref, v_ref, g_out, d_out, m_out, v_out):
        g = p_ref[0].astype(F32)
        for p in range(1, n_parts):
            g = g + p_ref[p].astype(F32)
        m2 = B1 * m_ref[...] + (1.0 - B1) * g
        v2 = B2 * v_ref[...] + (1.0 - B2) * (g * g)
        m_hat = m2 / (1.0 - B1 ** STEP)
        v_hat = v2 / (1.0 - B2 ** STEP)
        g_out[...] = g
        d_out[...] = -LR * (m_hat / (jnp.sqrt(v_hat) + AEPS) + WD * w_ref[...])
        m_out[...] = m2
        v_out[...] = v2

    spec = pl.BlockSpec((br, bc), lambda i, j: (i, j))
    out = jax.ShapeDtypeStruct((r, c), F32)
    return pl.pallas_call(
        body, name=name, out_shape=(out, out, out, out), grid=(r // br, c // bc),
        in_specs=[pl.BlockSpec((n_parts, br, bc), lambda i, j: (0, i, j)), spec, spec, spec],
        out_specs=(spec, spec, spec, spec), compiler_params=_params(40),
    )(parts, w, m, v)


def _atb(a, b, name, after=None):
    t, k1 = a.shape
    k2 = b.shape[1]
    bt = math.gcd(t, 2048)

    def pick(k):
        for cand in (1024, 768, 512, 384, 256, 128):
            if k % cand == 0:
                return cand
        return k

    b1, b2 = pick(k1), pick(k2)

    def body(a_ref, b_ref, *rest):
        o_ref = rest[-1]

        @pl.when(pl.program_id(2) == 0)
        def _():
            o_ref[...] = jnp.zeros_like(o_ref)
        o_ref[...] += _mm_tn(a_ref[...], b_ref[...])

    extra = [] if after is None else [after]
    return pl.pallas_call(
        body, name=name, out_shape=jax.ShapeDtypeStruct((k1, k2), F32), grid=(k1 // b1, k2 // b2, t // bt),
        in_specs=[pl.BlockSpec((bt, b1), lambda i, j, k: (k, i)), pl.BlockSpec((bt, b2), lambda i, j, k: (k, j))]
        + [ANY] * len(extra),
        out_specs=pl.BlockSpec((b1, b2), lambda i, j, k: (i, j)), compiler_params=_params(48),
    )(a, b, *extra)


def _mod_fwd(c_all, w_ada, b_cols):
    def body(c_ref, w_ref, b_ref, o_ref):
        cc = c_ref[...]
        cond = cc * _sigmoid(cc)
        o_ref[...] = _mm(cond, w_ref[...]) + b_ref[...]

    return pl.pallas_call(body, name="mod_fwd", out_shape=jax.ShapeDtypeStruct((c_all.shape[0], w_ada.shape[1]), F32),
                          compiler_params=_params(32))(c_all, w_ada, b_cols)


def _mod_bwd(c_all, dmod_cols, dmod_all):
    def body(c_ref, dc_ref, da_ref, gw_ref, gb_ref):
        cc = c_ref[...]
        cond = cc * _sigmoid(cc)
        gw_ref[...] = _mm_tn(cond, dc_ref[...])
        gb_ref[...] = jnp.sum(da_ref[...], axis=0, keepdims=True)

    return pl.pallas_call(
        body, name="mod_bwd",
        out_shape=(jax.ShapeDtypeStruct((D_MODEL, dmod_cols.shape[1]), F32), jax.ShapeDtypeStruct((1, dmod_all.shape[1]), F32)),
        compiler_params=_params(32))(c_all, dmod_cols, dmod_all)


def _load_once(hbm_ref, vmem_ref, sem):
    @pl.when(pl.program_id(0) == 0)
    def _():
        cp = pltpu.make_async_copy(hbm_ref, vmem_ref, sem)
        cp.start()
        cp.wait()


def _conv_taps(win_ref, w, tb, cols):
    shifted = [win_ref[8 - j:8 - j + tb, cols] for j in range(4)]
    acc = w[3:4] * shifted[0]
    for j in (1, 2, 3):
        acc = acc + w[3 - j:4 - j] * shifted[j]
    return acc, shifted


def _proj_conv_fwd(x2, mod3, w_in_pad, conv_w, conv_b, seq):
    t = x2.shape[0]
    tb = 256
    npb = seq // tb
    cw = 512

    def body(x_ref, mod_ref, w_hbm, cw_ref, cb_ref, z_ref, pre_ref, xbc_ref, dsilu_ref, dt_ref, u5_ref, w_vmem, win, sem):
        _load_once(w_hbm, w_vmem, sem)
        first = (pl.program_id(0) % npb) == 0

        @pl.when(first)
        def _():
            win[0:8, :] = jnp.zeros((8, D_XBC), F32)

        @pl.when(jnp.logical_not(first))
        def _():
            win[0:8, :] = win[tb:tb + 8, :]

        m = mod_ref[0]
        u = (x_ref[...] * (1.0 + m[1:2]) + m[0:1]).astype(BF16)
        z_ref[...] = lax.dot_general(u, w_vmem[0:D_SSD, :], NT, preferred_element_type=F32)
        dt_ref[...] = lax.dot_general(u, w_vmem[D_SSD + D_XBC:D_SSD + D_XBC + DT_PAD, :], NT,
                                      preferred_element_type=F32)
        u5_ref[...] = lax.dot_general(u, w_vmem[D_SSD + D_XBC + DT_PAD:, :], NT, preferred_element_type=F32)
        for k in range(D_XBC // cw):
            cols = slice(k * cw, (k + 1) * cw)
            pre_k = lax.dot_general(u, w_vmem[D_SSD + k * cw:D_SSD + (k + 1) * cw, :], NT,
                                    preferred_element_type=F32)
            win[8:8 + tb, cols] = pre_k
            pre_ref[:, cols] = pre_k
            conv, _ = _conv_taps(win, cw_ref[:, cols], tb, cols)
            conv = conv + cb_ref[:, cols]
            sg = _sigmoid(conv)
            xbc_ref[:, cols] = conv * sg
            dsilu_ref[:, cols] = sg * (1.0 + conv * (1.0 - sg))

    row = lambda w: pl.BlockSpec((tb, w), lambda i: (i, 0))
    return pl.pallas_call(
        body, name="proj_conv_fwd", grid=(t // tb,),
        out_shape=(jax.ShapeDtypeStruct((t, D_SSD), F32), jax.ShapeDtypeStruct((t, D_XBC), F32),
                   jax.ShapeDtypeStruct((t, D_XBC), F32), jax.ShapeDtypeStruct((t, D_XBC), F32),
                   jax.ShapeDtypeStruct((t, DT_PAD), F32), jax.ShapeDtypeStruct((t, D_S5), F32)),
        in_specs=[row(D_MODEL), pl.BlockSpec((1, N_MOD, D_MODEL), lambda i: (i // npb, 0, 0)), ANY,
                  pl.BlockSpec((4, D_XBC), lambda i: (0, 0)), pl.BlockSpec((1, D_XBC), lambda i: (0, 0))],
        out_specs=(row(D_SSD), row(D_XBC), row(D_XBC), row(D_XBC), row(DT_PAD), row(D_S5)),
        scratch_shapes=[pltpu.VMEM((D_INP, D_MODEL), BF16), pltpu.VMEM((tb + 8, D_XBC), F32), pltpu.SemaphoreType.DMA],
        compiler_params=_params(56),
    )(x2, mod3, w_in_pad, conv_w, conv_b)


N_PAIRS = N_HEADS // 2


def _split3(x):
    hi = x.astype(BF16)
    r = x - hi.astype(F32)
    mid = r.astype(BF16)
    lo = (r - mid.astype(F32)).astype(BF16)
    return hi, mid, lo


def _dot3(x, e, dims=(((1,), (0,)), ((), ()))):
    return sum(lax.dot_general(p, e, dims, preferred_element_type=F32) for p in _split3(x))


def _dot3_left(e, x, dims=(((1,), (0,)), ((), ()))):
    return sum(lax.dot_general(e, p, dims, preferred_element_type=F32) for p in _split3(x))


def _head_fold():
    return (jnp.arange(D_SSD)[:, None] // HEADDIM == jnp.arange(128)[None, :]).astype(BF16)


def _ssd_prep(dt_raw, par):
    dtb = par[0:1]
    a = -jnp.exp(par[1:2])
    dt = _softplus(dt_raw + dtb)
    adt = dt * a
    row = lax.broadcasted_iota(jnp.int32, (CHUNK, CHUNK), 0)
    col = lax.broadcasted_iota(jnp.int32, (CHUNK, CHUNK), 1)
    causal = row >= col
    tri = causal.astype(BF16)
    cs = _dot3_left(tri, adt)
    left = col < HEADDIM

    def lanes(v, h):
        return jnp.broadcast_to(v[:, h:h + 1], (CHUNK, 128))

    dt_c, cs_c, pair_cols = [], [], []
    for p in range(N_PAIRS):
        c0, c1 = lanes(cs, 2 * p), lanes(cs, 2 * p + 1)
        pair_cols.append(jnp.concatenate([c0, c1], axis=1))
        cs_c.append(jnp.where(left, c0, c1))
        dt_c.append(jnp.where(left, lanes(dt, 2 * p), lanes(dt, 2 * p + 1)))
    cs_c = jnp.concatenate(cs_c, axis=1)
    dt_c = jnp.concatenate(dt_c, axis=1)
    return dt, a, cs, cs.T, causal, tri, dt_c, jnp.exp(cs_c), jnp.exp(cs_c[CHUNK - 1:CHUNK, :] - cs_c), pair_cols


def _pair_decay(cols, cst, pair, causal2):
    rows = jnp.concatenate([jnp.broadcast_to(cst[2 * pair:2 * pair + 1, :], (CHUNK, CHUNK)),
                            jnp.broadcast_to(cst[2 * pair + 1:2 * pair + 2, :], (CHUNK, CHUNK))], axis=1)
    return jnp.exp(jnp.where(causal2, cols - rows, -jnp.inf))


def _stack_heads(xp, left):
    return jnp.concatenate([jnp.where(left, xp, 0.0), jnp.where(left, 0.0, xp)], axis=0).astype(BF16)


def _ssd_fwd(xbc, z, dt_raw, par, dsk, normw, seq):
    t = xbc.shape[0]
    nc = seq // CHUNK
    n_chunks = t // CHUNK

    def body(xbc_ref, z_ref, dt_ref, par_ref, dsk_ref, nw_ref, yraw_ref, ycat_ref, hprev_ref, h_ref):
        @pl.when(pl.program_id(0) % nc == 0)
        def _():
            h_ref[...] = jnp.zeros_like(h_ref)
        hprev_ref[0] = h_ref[...]
        _, _, cs, cst, causal, _, dt_c, ecs_c, w_c, pair_cols = _ssd_prep(dt_ref[...], par_ref[...])
        cs_last = cs[CHUNK - 1:CHUNK, :]
        causal2 = jnp.concatenate([causal, causal], axis=1)
        left = lax.broadcasted_iota(jnp.int32, (CHUNK, 128), 1) < HEADDIM
        x = xbc_ref[:, 0:D_SSD]
        xdt = x * dt_c
        amat = (w_c * xdt).astype(BF16)
        zz = z_ref[...]
        silu_z = zz * _sigmoid(zz)
        for g in range(N_GROUPS):
            gs = slice(g * GW, (g + 1) * GW)
            bg = xbc_ref[:, D_SSD + g * N_STATE:D_SSD + (g + 1) * N_STATE].astype(BF16)
            cg = xbc_ref[:, D_SSD + (N_GROUPS + g) * N_STATE:D_SSD + (N_GROUPS + g + 1) * N_STATE].astype(BF16)
            scores = lax.dot_general(cg, bg, NT, preferred_element_type=F32)
            scores2 = jnp.concatenate([scores, scores], axis=1)
            hg = h_ref[gs, :]
            p_all = lax.dot_general(cg, hg.astype(BF16), NT, preferred_element_type=F32)
            ys = []
            for q in range(GW // 128):
                pair = g * (GW // 128) + q
                decay = _pair_decay(pair_cols[pair], cst, pair, causal2)
                mcat = (scores2 * decay).astype(BF16)
                ys.append(jnp.dot(mcat, _stack_heads(xdt[:, pair * 128:(pair + 1) * 128], left),
                                  preferred_element_type=F32))
            yg = jnp.concatenate(ys, axis=1) + ecs_c[:, gs] * p_all + x[:, gs] * dsk_ref[:, gs]
            s_new = lax.dot_general(amat[:, gs], bg, TN, preferred_element_type=F32)
            for j in range(HPG):
                hh = g * HPG + j
                js = slice(j * HEADDIM, (j + 1) * HEADDIM)
                h_ref[g * GW + j * HEADDIM:g * GW + (j + 1) * HEADDIM, :] = (
                    hg[js, :] * jnp.exp(cs_last[:, hh:hh + 1]) + s_new[js, :])
            yraw_ref[:, gs] = yg
            v = yg * silu_z[:, gs]
            r = lax.rsqrt(jnp.mean(v * v, axis=-1, keepdims=True) + EPS)
            ycat_ref[:, gs] = (v * r * nw_ref[:, gs]).astype(BF16)

    row = lambda w: pl.BlockSpec((CHUNK, w), lambda i: (i, 0))
    full = lambda s: pl.BlockSpec(s, lambda i: (0,) * len(s))
    return pl.pallas_call(
        body, name="ssd_fwd", grid=(n_chunks,),
        out_shape=(jax.ShapeDtypeStruct((t, D_SSD), F32), jax.ShapeDtypeStruct((t, D_SSD + D_S5), BF16),
                   jax.ShapeDtypeStruct((n_chunks, D_SSD, N_STATE), F32)),
        in_specs=[row(D_XBC), row(D_SSD), row(DT_PAD), full((8, 128)), full((1, D_SSD)), full((1, D_SSD))],
        out_specs=(row(D_SSD), row(D_SSD), pl.BlockSpec((1, D_SSD, N_STATE), lambda i: (i, 0, 0))),
        scratch_shapes=[pltpu.VMEM((D_SSD, N_STATE), F32)],
        compiler_params=_params(40),
    )(xbc, z, dt_raw, par, dsk, normw)


S5_CW = 512
S5_BLOCKS = 4


def _tile_scan(in_re, in_im, out_re, out_im, carry_re, carry_im, pw_re, pw_im, n_tiles, reverse):
    steps = (1, 2, 4)
    for cc in range(S5_N // S5_CW):
        cols = slice(cc * S5_CW, (cc + 1) * S5_CW)
        a_re, a_im = pw_re[:, cols], pw_im[:, cols]
        rid = lax.broadcasted_iota(jnp.int32, (8, S5_CW), 0)
        pows = []
        for d in steps:
            k = 8 - d if reverse else d - 1
            keep = (rid < 8 - d) if reverse else (rid >= d)
            pows.append((jnp.where(keep, pw_re[k:k + 1, cols], 0.0), jnp.where(keep, pw_im[k:k + 1, cols], 0.0)))

        def tile(i, carry, cols=cols, pows=pows, a_re=a_re, a_im=a_im):
            r = (n_tiles - 1 - i) if reverse else i
            rows = pl.ds(pl.multiple_of(r * 8, 8), 8)
            xr, xi = in_re[rows, cols], in_im[rows, cols]
            for (pr, pi), d in zip(pows, steps):
                shift = 8 - d if reverse else d
                sr, si = pltpu.roll(xr, shift, axis=0), pltpu.roll(xi, shift, axis=0)
                xr, xi = xr + pr * sr - pi * si, xi + pr * si + pi * sr
            cr, ci = carry
            xr, xi = xr + a_re * cr - a_im * ci, xi + a_re * ci + a_im * cr
            out_re[rows, cols] = xr
            out_im[rows, cols] = xi
            edge = slice(0, 1) if reverse else slice(7, 8)
            return (jnp.broadcast_to(xr[edge], (8, S5_CW)), jnp.broadcast_to(xi[edge], (8, S5_CW)))

        c0 = (jnp.broadcast_to(carry_re[0:1, cols], (8, S5_CW)), jnp.broadcast_to(carry_im[0:1, cols], (8, S5_CW)))
        cr, ci = lax.fori_loop(0, n_tiles, tile, c0, unroll=True)
        carry_re[:, cols] = cr
        carry_im[:, cols] = ci


def _s5_params_math(ar, ai, ldt, br, bi):
    dt = jnp.exp(ldt)
    mag = jnp.exp(ar * dt)
    ang = ai * dt
    ab_re = mag * jnp.cos(ang)
    ab_im = mag * jnp.sin(ang)
    den = ar * ar + ai * ai
    n_re = ab_re - 1.0
    coef_re = (n_re * ar + ab_im * ai) / den
    coef_im = (ab_im * ar - n_re * ai) / den
    bb_re = coef_re * br - coef_im * bi
    bb_im = coef_re * bi + coef_im * br
    return ab_re, ab_im, bb_re, bb_im


def _s5_params_fwd(ar, ai, ldt, br, bi):
    def body(ar_ref, ai_ref, ldt_ref, br_ref, bi_ref, bbr_ref, bbi_ref, pfr_ref, pfi_ref, prr_ref, pri_ref):
        ab_re, ab_im, bb_re, bb_im = _s5_params_math(ar_ref[...], ai_ref[...], ldt_ref[...], br_ref[...], bi_ref[...])
        bbr_ref[...] = bb_re
        bbi_ref[...] = bb_im
        pr, pi = ab_re, ab_im
        for k in range(8):
            pfr_ref[k:k + 1, :] = pr
            pfi_ref[k:k + 1, :] = pi
            prr_ref[7 - k:8 - k, :] = pr
            pri_ref[7 - k:8 - k, :] = -pi
            pr, pi = pr * ab_re - pi * ab_im, pr * ab_im + pi * ab_re

    b16 = jax.ShapeDtypeStruct((S5_CH, S5_N), F32)
    p8 = jax.ShapeDtypeStruct((8, S5_N), F32)
    return pl.pallas_call(body, name="s5_params_fwd", out_shape=(b16, b16, p8, p8, p8, p8),
                          compiler_params=_params(32))(ar, ai, ldt, br, bi)


def _s5_params_bwd(ar, ai, ldt, br, bi, d_ab_re, d_ab_im, d_bb_re, d_bb_im):
    def body(ar_ref, ai_ref, ldt_ref, br_ref, bi_ref, dar_ref, dai_ref, dbr_ref, dbi_ref,
             gar_ref, gai_ref, gldt_ref, gbr_ref, gbi_ref):
        _, vjp = jax.vjp(_s5_params_math, ar_ref[...], ai_ref[...], ldt_ref[...], br_ref[...], bi_ref[...])
        g_ar, g_ai, g_ldt, g_br, g_bi = vjp((dar_ref[...], dai_ref[...], dbr_ref[...], dbi_ref[...]))
        gar_ref[...] = g_ar
        gai_ref[...] = g_ai
        gbr_ref[...] = g_br
        gbi_ref[...] = g_bi
        lane = lax.broadcasted_iota(jnp.int32, (S5_N, 128), 0) // S5_P
        grp = lax.broadcasted_iota(jnp.int32, (S5_N, 128), 1)
        fold = (lane == grp).astype(F32)
        gldt_ref[...] = jnp.dot(g_ldt, fold, preferred_element_type=F32, precision=HIGHEST)

    v1 = jax.ShapeDtypeStruct((1, S5_N), F32)
    b16 = jax.ShapeDtypeStruct((S5_CH, S5_N), F32)
    return pl.pallas_call(body, name="s5_params_bwd",
                          out_shape=(v1, v1, jax.ShapeDtypeStruct((1, 128), F32), b16, b16),
                          compiler_params=_params(32))(ar, ai, ldt, br, bi, d_ab_re, d_ab_im, d_bb_re, d_bb_im)


def _s5_fwd(u5, bb_re, bb_im, cc_re, cc_im, pf_re, pf_im, s5d, w_glu, b_glu, ycat, seq):
    t = u5.shape[0]
    tb = 256
    npb = seq // tb

    def body(u_ref, bbr_ref, bbi_ref, ccr_ref, cci_ref, pfr_ref, pfi_ref, d_ref, wg_ref, bg_ref, ycat_hbm,
             sre_ref, sim_ref, ypre_ref, y5_ref, bur, bui, car, cai):
        del ycat_hbm

        @pl.when(pl.program_id(0) % npb == 0)
        def _():
            car[...] = jnp.zeros_like(car)
            cai[...] = jnp.zeros_like(cai)
        u = u_ref[...]
        ub = u.astype(BF16)
        for j in range(S5_BLOCKS):
            ch, st = slice(j * 128, (j + 1) * 128), slice(j * 512, (j + 1) * 512)
            bur[:, st] = jnp.dot(ub[:, ch], bbr_ref[j], preferred_element_type=F32)
            bui[:, st] = jnp.dot(ub[:, ch], bbi_ref[j], preferred_element_type=F32)
        _tile_scan(bur, bui, sre_ref, sim_ref, car, cai, pfr_ref, pfi_ref, tb // 8, reverse=False)
        cs_y = []
        for j in range(S5_BLOCKS):
            st = slice(j * 512, (j + 1) * 512)
            cs_y.append(_mm(sre_ref[:, st], ccr_ref[j]) - _mm(sim_ref[:, st], cci_ref[j]))
        ypre = jnp.concatenate(cs_y, axis=1) + u * d_ref[...]
        ypre_ref[...] = ypre
        yg = _gelu(ypre)
        y5_ref[...] = (yg * _sigmoid(_mm(yg, wg_ref[...]) + bg_ref[...])).astype(BF16)

    row = lambda w: pl.BlockSpec((tb, w), lambda i: (i, 0))
    full = lambda a: pl.BlockSpec(a.shape, lambda i: (0,) * a.ndim)
    return pl.pallas_call(
        body, name="s5_fwd", grid=(t // tb,),
        out_shape=(jax.ShapeDtypeStruct((t, S5_N), F32), jax.ShapeDtypeStruct((t, S5_N), F32),
                   jax.ShapeDtypeStruct((t, D_S5), F32), jax.ShapeDtypeStruct(ycat.shape, BF16)),
        in_specs=[row(D_S5), full(bb_re), full(bb_im), full(cc_re), full(cc_im), full(pf_re), full(pf_im),
                  full(s5d), full(w_glu), full(b_glu), ANY],
        out_specs=(row(S5_N), row(S5_N), row(D_S5), pl.BlockSpec((tb, D_S5), lambda i: (i, D_SSD // D_S5))),
        input_output_aliases={10: 3},
        scratch_shapes=[pltpu.VMEM((tb, S5_N), F32), pltpu.VMEM((tb, S5_N), F32),
                        pltpu.VMEM((8, S5_N), F32), pltpu.VMEM((8, S5_N), F32)],
        compiler_params=_params(48),
    )(u5, bb_re, bb_im, cc_re, cc_im, pf_re, pf_im, s5d, w_glu, b_glu, ycat)


def _layer_norm(r, g, b):
    mu = jnp.mean(r, axis=-1, keepdims=True)
    xc = r - mu
    rstd = lax.rsqrt(jnp.mean(xc * xc, axis=-1, keepdims=True) + EPS)
    xhat = xc * rstd
    return xhat * g + b, xhat, rstd


def _layer_norm_bwd(dy, xhat, rstd, g):
    dxhat = dy * g
    return rstd * (dxhat - jnp.mean(dxhat, axis=-1, keepdims=True)
                   - xhat * jnp.mean(dxhat * xhat, axis=-1, keepdims=True))


def _out_ln1(ycat, x2, mod3, w_out, ln1, seq):
    t = x2.shape[0]
    tb = 512
    npb = seq // tb

    def body(y_ref, x_ref, mod_ref, w_ref, ln_ref, mix_ref, x1_ref):
        m = mod_ref[0]
        mix = jnp.dot(y_ref[...], w_ref[...], preferred_element_type=F32)
        mix_ref[...] = mix
        r1 = ALPHA * x_ref[...] + (1.0 + m[2:3]) * mix
        x1_ref[...] = _layer_norm(r1, ln_ref[0:1], ln_ref[1:2])[0]

    row = lambda w: pl.BlockSpec((tb, w), lambda i: (i, 0))
    return pl.pallas_call(
        body, name="out_ln1", grid=(t // tb,),
        out_shape=(jax.ShapeDtypeStruct((t, D_MODEL), F32), jax.ShapeDtypeStruct((t, D_MODEL), F32)),
        in_specs=[row(D_SSD + D_S5), row(D_MODEL), pl.BlockSpec((1, N_MOD, D_MODEL), lambda i: (i // npb, 0, 0)),
                  pl.BlockSpec(w_out.shape, lambda i: (0, 0)), pl.BlockSpec(ln1.shape, lambda i: (0, 0))],
        out_specs=(row(D_MODEL), row(D_MODEL)), compiler_params=_params(48),
    )(ycat, x2, mod3, w_out, ln1)


def _mlp_fwd_bwd(x1, tgt, mod3, w1, w2, vec1, b1, seq):
    t = x1.shape[0]
    tb = 256
    npb = seq // tb
    n_fb, _, fb = w1.shape

    def body(x1_ref, tgt_ref, mod_ref, w1_hbm, w2_hbm, v_ref, b1_ref,
             dx1_ref, u2_ref, h_ref, dhp_ref, do_ref, gacc_ref, db1_ref, bacc_ref, w1_v, w2_v, sem1, sem2):
        i = pl.program_id(0)
        @pl.when(i == 0)
        def _():
            cps = [pltpu.make_async_copy(w1_hbm.at[k], w1_v.at[:, k * fb:(k + 1) * fb], sem1.at[k])
                   for k in range(n_fb)]
            for cp in cps:
                cp.start()
            for cp in cps:
                cp.wait()
        _load_once(w2_hbm, w2_v, sem2)

        @pl.when(i == 0)
        def _():
            gacc_ref[...] = jnp.zeros_like(gacc_ref)
            db1_ref[...] = jnp.zeros_like(db1_ref)

        @pl.when(i % npb == 0)
        def _():
            bacc_ref[...] = jnp.zeros_like(bacc_ref)

        m = mod_ref[0]
        sh2, sc2, g2 = m[3:4], m[4:5], m[5:6]
        x1v = x1_ref[...]
        u2 = (x1v * (1.0 + sc2) + sh2).astype(BF16)
        u2_ref[...] = u2
        hr = jnp.maximum(jnp.dot(u2, w1_v[...], preferred_element_type=F32) + b1_ref[...], 0.0)
        hb = (hr * hr).astype(BF16)
        h_ref[...] = hb
        o = jnp.dot(hb, w2_v[...], preferred_element_type=F32) + v_ref[0:1]
        r2 = ALPHA * x1v + (1.0 + g2) * o
        y, xhat, rstd = _layer_norm(r2, v_ref[1:2], v_ref[2:3])
        err = y - tgt_ref[...]
        dy = err * (1.0 / D_MODEL)
        dr2 = _layer_norm_bwd(dy, xhat, rstd, v_ref[1:2])
        do = (1.0 + g2) * dr2
        dob = do.astype(BF16)
        do_ref[...] = dob
        gacc_ref[0:1, :] += jnp.sum(dy * xhat, axis=0, keepdims=True)
        gacc_ref[1:2, :] += jnp.sum(dy, axis=0, keepdims=True)
        gacc_ref[2:3, :] += jnp.sum(do, axis=0, keepdims=True)
        gacc_ref[3:4, :] += jnp.sum(err * err, axis=0, keepdims=True)
        dhpre = lax.dot_general(dob, w2_v[...], NT, preferred_element_type=F32) * (2.0 * hr)
        dhpb = dhpre.astype(BF16)
        dhp_ref[...] = dhpb
        db1_ref[...] += jnp.sum(dhpre, axis=0, keepdims=True)
        du2 = lax.dot_general(dhpb, w1_v[...], NT, preferred_element_type=F32)
        dx1_ref[...] = ALPHA * dr2 + du2 * (1.0 + sc2)
        bacc_ref[0, 0:1, :] += jnp.sum(du2, axis=0, keepdims=True)
        bacc_ref[0, 1:2, :] += jnp.sum(du2 * x1v, axis=0, keepdims=True)
        bacc_ref[0, 2:3, :] += jnp.sum(dr2 * o, axis=0, keepdims=True)

    row = lambda w: pl.BlockSpec((tb, w), lambda i: (i, 0))
    return pl.pallas_call(
        body, name="mlp_fwd_bwd", grid=(t // tb,),
        out_shape=(jax.ShapeDtypeStruct((t, D_MODEL), F32), jax.ShapeDtypeStruct((t, D_MODEL), BF16),
                   jax.ShapeDtypeStruct((t, D_FF), BF16), jax.ShapeDtypeStruct((t, D_FF), BF16),
                   jax.ShapeDtypeStruct((t, D_MODEL), BF16), jax.ShapeDtypeStruct((8, D_MODEL), F32),
                   jax.ShapeDtypeStruct((1, D_FF), F32), jax.ShapeDtypeStruct((t // seq, 8, D_MODEL), F32)),
        in_specs=[row(D_MODEL), row(D_MODEL), pl.BlockSpec((1, N_MOD, D_MODEL), lambda i: (i // npb, 0, 0)), ANY, ANY,
                  pl.BlockSpec(vec1.shape, lambda i: (0, 0)), pl.BlockSpec(b1.shape, lambda i: (0, 0))],
        out_specs=(row(D_MODEL), row(D_MODEL), row(D_FF), row(D_FF), row(D_MODEL),
                   pl.BlockSpec((8, D_MODEL), lambda i: (0, 0)), pl.BlockSpec((1, D_FF), lambda i: (0, 0)),
                   pl.BlockSpec((1, 8, D_MODEL), lambda i: (i // npb, 0, 0))),
        scratch_shapes=[pltpu.VMEM((D_MODEL, n_fb * fb), BF16), pltpu.VMEM((D_FF, D_MODEL), BF16),
                        pltpu.SemaphoreType.DMA((n_fb,)), pltpu.SemaphoreType.DMA],
        compiler_params=_params(60),
    )(x1, tgt, mod3, w1, w2, vec1, b1)


def _ln1_out_bwd(dx1, x2, mix, mod3, w_out, ln1, seq):
    t = x2.shape[0]
    tb = 512
    npb = seq // tb

    def body(dx1_ref, x_ref, mix_ref, mod_ref, w_ref, ln_ref, dmix_ref, dxa_ref, dys_ref, dy5_ref, gacc_ref, bacc_ref):
        i = pl.program_id(0)

        @pl.when(i == 0)
        def _():
            gacc_ref[...] = jnp.zeros_like(gacc_ref)

        @pl.when(i % npb == 0)
        def _():
            bacc_ref[...] = jnp.zeros_like(bacc_ref)

        m = mod_ref[0]
        mix = mix_ref[...]
        r1 = ALPHA * x_ref[...] + (1.0 + m[2:3]) * mix
        _, xhat, rstd = _layer_norm(r1, ln_ref[0:1], ln_ref[1:2])
        dx1v = dx1_ref[...]
        dr1 = _layer_norm_bwd(dx1v, xhat, rstd, ln_ref[0:1])
        gacc_ref[0:1, :] += jnp.sum(dx1v * xhat, axis=0, keepdims=True)
        gacc_ref[1:2, :] += jnp.sum(dx1v, axis=0, keepdims=True)
        bacc_ref[0, 0:1, :] += jnp.sum(dr1 * mix, axis=0, keepdims=True)
        dmix = ((1.0 + m[2:3]) * dr1).astype(BF16)
        dmix_ref[...] = dmix
        dxa_ref[...] = ALPHA * dr1
        dys_ref[...] = lax.dot_general(dmix, w_ref[0:D_SSD, :], NT, preferred_element_type=F32)
        dy5_ref[...] = lax.dot_general(dmix, w_ref[D_SSD:, :], NT, preferred_element_type=F32)

    row = lambda w: pl.BlockSpec((tb, w), lambda i: (i, 0))
    return pl.pallas_call(
        body, name="ln1_out_bwd", grid=(t // tb,),
        out_shape=(jax.ShapeDtypeStruct((t, D_MODEL), BF16), jax.ShapeDtypeStruct((t, D_MODEL), F32),
                   jax.ShapeDtypeStruct((t, D_SSD), F32), jax.ShapeDtypeStruct((t, D_S5), F32),
                   jax.ShapeDtypeStruct((8, D_MODEL), F32), jax.ShapeDtypeStruct((t // seq, 8, D_MODEL), F32)),
        in_specs=[row(D_MODEL), row(D_MODEL), row(D_MODEL), pl.BlockSpec((1, N_MOD, D_MODEL), lambda i: (i // npb, 0, 0)),
                  pl.BlockSpec(w_out.shape, lambda i: (0, 0)), pl.BlockSpec(ln1.shape, lambda i: (0, 0))],
        out_specs=(row(D_MODEL), row(D_MODEL), row(D_SSD), row(D_S5), pl.BlockSpec((8, D_MODEL), lambda i: (0, 0)),
                   pl.BlockSpec((1, 8, D_MODEL), lambda i: (i // npb, 0, 0))),
        compiler_params=_params(48),
    )(dx1, x2, mix, mod3, w_out, ln1)


def _s5_bwd(dy5, ypre, u5, s_re, s_im, bb_re, bb_im, cc_re, cc_im, pr_re, pr_im, s5d, w_glu, b_glu, seq):
    t = u5.shape[0]
    tb = 256
    npb = seq // tb
    n_blocks = t // tb

    def blk(i):
        return (i // npb) * npb + (npb - 1 - i % npb)

    def body(dy_ref, ypre_ref, u_ref, sre_ref, sim_ref, hre_ref, him_ref, bbr_ref, bbi_ref, ccr_ref, cci_ref,
             prr_ref, pri_ref, d_ref, wg_ref, bg_ref,
             du_ref, vacc_ref, sacc_ref, dcc_ref, dbb_ref, dwg_ref, dsr, dsi, gr, gi, car, cai):
        i = pl.program_id(0)

        @pl.when(i == 0)
        def _():
            for acc in (vacc_ref, sacc_ref, dcc_ref, dbb_ref, dwg_ref):
                acc[...] = jnp.zeros_like(acc)

        @pl.when(i % npb == 0)
        def _():
            car[...] = jnp.zeros_like(car)
            cai[...] = jnp.zeros_like(cai)

        dy = dy_ref[...]
        ypre = ypre_ref[...]
        u = u_ref[...]
        ub = u.astype(BF16)
        yg = _gelu(ypre)
        sg = _sigmoid(_mm(yg, wg_ref[...]) + bg_ref[...])
        dq = dy * yg * sg * (1.0 - sg)
        dqb = dq.astype(BF16)
        dyg = dy * sg + lax.dot_general(dqb, wg_ref[...], NT, preferred_element_type=F32)
        dyp = dyg * _gelu_grad(ypre)
        dypb = dyp.astype(BF16)
        dwg_ref[...] += lax.dot_general(yg.astype(BF16), dqb, TN, preferred_element_type=F32)
        blocks = [(slice(j * 128, (j + 1) * 128), slice(j * 512, (j + 1) * 512)) for j in range(S5_BLOCKS)]
        for j, (ch, st) in enumerate(blocks):
            dsr[:, st] = lax.dot_general(dypb[:, ch], ccr_ref[j], NT, preferred_element_type=F32)
            dsi[:, st] = -lax.dot_general(dypb[:, ch], cci_ref[j], NT, preferred_element_type=F32)
        _tile_scan(dsr, dsi, gr, gi, car, cai, prr_ref, pri_ref, tb // 8, reverse=True)
        g_re, g_im = gr[...], gi[...]
        first_rows = (i % npb) == npb - 1
        hre = jnp.where(first_rows, 0.0, hre_ref[...])
        him = jnp.where(first_rows, 0.0, him_ref[...])
        s_re_v, s_im_v = sre_ref[...], sim_ref[...]
        sp_re = pltpu.roll(jnp.concatenate([hre, s_re_v], axis=0), 1, axis=0)[8:8 + tb]
        sp_im = pltpu.roll(jnp.concatenate([him, s_im_v], axis=0), 1, axis=0)[8:8 + tb]
        vacc_ref[0:1, :] += jnp.sum(g_re * sp_re + g_im * sp_im, axis=0, keepdims=True)
        vacc_ref[1:2, :] += jnp.sum(g_im * sp_re - g_re * sp_im, axis=0, keepdims=True)
        grb, gib = g_re.astype(BF16), g_im.astype(BF16)
        srb, sib = s_re_v.astype(BF16), s_im_v.astype(BF16)
        du_cols = []
        for j, (ch, st) in enumerate(blocks):
            dcc_ref[j] += lax.dot_general(srb[:, st], dypb[:, ch], TN, preferred_element_type=F32)
            dcc_ref[S5_BLOCKS + j] -= lax.dot_general(sib[:, st], dypb[:, ch], TN, preferred_element_type=F32)
            dbb_ref[j] += lax.dot_general(ub[:, ch], grb[:, st], TN, preferred_element_type=F32)
            dbb_ref[S5_BLOCKS + j] += lax.dot_general(ub[:, ch], gib[:, st], TN, preferred_element_type=F32)
            du_cols.append(lax.dot_general(grb[:, st], bbr_ref[j], NT, preferred_element_type=F32)
                           + lax.dot_general(gib[:, st], bbi_ref[j], NT, preferred_element_type=F32))
        du_ref[...] = jnp.concatenate(du_cols, axis=1) + dyp * d_ref[...]
        sacc_ref[0:1, :] += jnp.sum(dyp * u, axis=0, keepdims=True)
        sacc_ref[1:2, :] += jnp.sum(dq, axis=0, keepdims=True)

    row = lambda w: pl.BlockSpec((tb, w), lambda i: (blk(i), 0))
    halo = pl.BlockSpec((8, S5_N), lambda i: (jnp.maximum(blk(i) * (tb // 8) - 1, 0), 0))
    full = lambda a: pl.BlockSpec(a.shape, lambda i: (0,) * a.ndim)
    acc = lambda s: pl.BlockSpec(s, lambda i: (0,) * len(s))
    acc_shapes = [(8, S5_N), (8, D_S5), (2 * S5_BLOCKS, 512, 128), (2 * S5_BLOCKS, 128, 512), (D_S5, D_S5)]
    return pl.pallas_call(
        body, name="s5_bwd", grid=(n_blocks,),
        out_shape=(jax.ShapeDtypeStruct((t, D_S5), F32),) + tuple(jax.ShapeDtypeStruct(s, F32) for s in acc_shapes),
        in_specs=[row(D_S5), row(D_S5), row(D_S5), row(S5_N), row(S5_N), halo, halo, full(bb_re), full(bb_im),
                  full(cc_re), full(cc_im), full(pr_re), full(pr_im), full(s5d), full(w_glu), full(b_glu)],
        out_specs=(row(D_S5),) + tuple(acc(s) for s in acc_shapes),
        scratch_shapes=[pltpu.VMEM((tb, S5_N), F32), pltpu.VMEM((tb, S5_N), F32), pltpu.VMEM((tb, S5_N), F32),
                        pltpu.VMEM((tb, S5_N), F32), pltpu.VMEM((8, S5_N), F32), pltpu.VMEM((8, S5_N), F32)],
        compiler_params=_params(56),
    )(dy5, ypre, u5, s_re, s_im, s_re, s_im, bb_re, bb_im, cc_re, cc_im, pr_re, pr_im, s5d, w_glu, b_glu)


def _ssd_bwd(dyssd, yraw, z, xbc, dt_raw, hprev, par, dsk, normw, seq):
    t = xbc.shape[0]
    nc = seq // CHUNK
    n_chunks = t // CHUNK
    fold = _head_fold()

    def blk(i):
        return (i // nc) * nc + (nc - 1 - i % nc)

    def body(dy_ref, yraw_ref, z_ref, xbc_ref, dt_ref, hprev_ref, par_ref, dsk_ref, nw_ref, fold_ref,
             dxbc_ref, dz_ref, ddt_ref, dpar_ref, cacc_ref, dh_ref, dyr_ref):
        i = pl.program_id(0)

        @pl.when(i == 0)
        def _():
            dpar_ref[...] = jnp.zeros_like(dpar_ref)
            cacc_ref[...] = jnp.zeros_like(cacc_ref)

        @pl.when(i % nc == 0)
        def _():
            dh_ref[...] = jnp.zeros_like(dh_ref)

        zz = z_ref[...]
        sz = _sigmoid(zz)
        silu_z = zz * sz
        yraw = yraw_ref[...]
        for g in range(N_GROUPS):
            sl = slice(g * GW, (g + 1) * GW)
            v = yraw[:, sl] * silu_z[:, sl]
            r = lax.rsqrt(jnp.mean(v * v, axis=-1, keepdims=True) + EPS)
            dyg = dy_ref[:, sl]
            cacc_ref[1:2, sl] += jnp.sum(dyg * v * r, axis=0, keepdims=True)
            dyw = dyg * nw_ref[:, sl]
            dv = r * dyw - v * (r * r * r) * jnp.mean(dyw * v, axis=-1, keepdims=True)
            dyr_ref[:, sl] = dv * silu_z[:, sl]
            dz_ref[:, sl] = dv * yraw[:, sl] * (sz[:, sl] * (1.0 + zz[:, sl] * (1.0 - sz[:, sl])))

        dt, a, cs, cst, causal, tri, dt_c, ecs_c, w_c, pair_cols = _ssd_prep(dt_ref[...], par_ref[...])
        cs_last = cs[CHUNK - 1:CHUNK, :]
        causal2 = jnp.concatenate([causal, causal], axis=1)
        lane = lax.broadcasted_iota(jnp.int32, (CHUNK, 128), 1)
        left = lane < HEADDIM
        lane1 = lax.broadcasted_iota(jnp.int32, (1, 128), 1)
        x = xbc_ref[:, 0:D_SSD]
        xdt = x * dt_c
        dyr = dyr_ref[...]
        dyrb = dyr.astype(BF16)
        cacc_ref[0:1, :] += jnp.sum(dyr * x, axis=0, keepdims=True)
        dlast = jnp.zeros((1, 128), F32)
        dxdt_cols, diag_all, dww_cols = [], [], []
        for g in range(N_GROUPS):
            gs = slice(g * GW, (g + 1) * GW)
            b_sl = slice(D_SSD + g * N_STATE, D_SSD + (g + 1) * N_STATE)
            c_sl = slice(D_SSD + (N_GROUPS + g) * N_STATE, D_SSD + (N_GROUPS + g + 1) * N_STATE)
            bg = xbc_ref[:, b_sl].astype(BF16)
            cg = xbc_ref[:, c_sl].astype(BF16)
            scores = lax.dot_general(cg, bg, NT, preferred_element_type=F32)
            scores2 = jnp.concatenate([scores, scores], axis=1)
            hg = hprev_ref[0, gs, :]
            hgb = hg.astype(BF16)
            dhg = dh_ref[gs, :]
            dhgb = dhg.astype(BF16)
            q_all = lax.dot_general(bg, dhgb, NT, preferred_element_type=F32)
            dscores = jnp.zeros((CHUNK, CHUNK), F32)
            diag_cols = []
            for q in range(GW // 128):
                pair = g * (GW // 128) + q
                ps = slice(pair * 128, (pair + 1) * 128)
                decay = _pair_decay(pair_cols[pair], cst, pair, causal2)
                mcat = (scores2 * decay).astype(BF16)
                dyp = dyrb[:, ps]
                dm = lax.dot_general(dyp, _stack_heads(xdt[:, ps], left), NT, preferred_element_type=F32)
                dmd = dm * decay
                dscores = dscores + dmd[:, 0:CHUNK] + dmd[:, CHUNK:]
                rr = lax.dot_general(mcat, dyp, TN, preferred_element_type=F32)
                diag_cols.append(jnp.where(left, rr[0:CHUNK], rr[CHUNK:]))
            wq = w_c[:, gs] * q_all
            diag_g = jnp.concatenate(diag_cols, axis=1)
            diag_all.append(diag_g)
            dxdt_cols.append(diag_g + wq)
            dww_cols.append(wq * xdt[:, gs])
            dp = (ecs_c[:, gs] * dyr[:, gs]).astype(BF16)
            amat = (w_c[:, gs] * xdt[:, gs]).astype(BF16)
            dsb = dscores.astype(BF16)
            dxbc_ref[:, c_sl] = (jnp.dot(dsb, bg, preferred_element_type=F32)
                                 + jnp.dot(dp, hgb, preferred_element_type=F32))
            dxbc_ref[:, b_sl] = (lax.dot_general(dsb, cg, TN, preferred_element_type=F32)
                                 + jnp.dot(amat, dhgb, preferred_element_type=F32))
            dh_in = lax.dot_general(dp, cg, TN, preferred_element_type=F32)
            for j in range(HPG):
                hh = g * HPG + j
                js = slice(j * HEADDIM, (j + 1) * HEADDIM)
                ecl = jnp.exp(cs_last[:, hh:hh + 1])
                dlast = dlast + jnp.where(lane1 == hh, ecl * jnp.sum(dhg[js, :] * hg[js, :]), 0.0)
                dh_ref[g * GW + j * HEADDIM:g * GW + (j + 1) * HEADDIM, :] = ecl * dhg[js, :] + dh_in[js, :]
        dxdt = jnp.concatenate(dxdt_cols, axis=1)
        dxbc_ref[:, 0:D_SSD] = dxdt * dt_c + dyr * dsk_ref[...]
        dww = _mm(jnp.concatenate(dww_cols, axis=1), fold_ref[...])
        dcs = _dot3(dyrb.astype(F32) * (yraw - x * dsk_ref[...])
                    - xdt.astype(BF16).astype(F32) * jnp.concatenate(diag_all, axis=1), fold_ref[...]) - dww
        rowid = lax.broadcasted_iota(jnp.int32, (CHUNK, 128), 0)
        dcs = dcs + jnp.where(rowid == CHUNK - 1, jnp.sum(dww, axis=0, keepdims=True) + dlast, 0.0)
        dadt = _dot3_left(tri, dcs, TN)
        ddt = _mm(dxdt * x, fold_ref[...]) + dadt * a
        da = jnp.sum(dadt * dt, axis=0, keepdims=True)
        ddt_raw = ddt * _sigmoid(dt_ref[...] + par_ref[0:1])
        ddt_raw = jnp.where(lane < N_HEADS, ddt_raw, 0.0)
        ddt_ref[...] = ddt_raw
        dpar_ref[0:1, :] += jnp.sum(ddt_raw, axis=0, keepdims=True)
        dpar_ref[1:2, :] += jnp.where(lane1 < N_HEADS, da * a, 0.0)

    row = lambda w: pl.BlockSpec((CHUNK, w), lambda i: (blk(i), 0))
    full = lambda s: pl.BlockSpec(s, lambda i: (0,) * len(s))
    return pl.pallas_call(
        body, name="ssd_bwd", grid=(n_chunks,),
        out_shape=(jax.ShapeDtypeStruct((t, D_XBC), F32), jax.ShapeDtypeStruct((t, D_SSD), F32),
                   jax.ShapeDtypeStruct((t, DT_PAD), F32), jax.ShapeDtypeStruct((8, 128), F32),
                   jax.ShapeDtypeStruct((8, D_SSD), F32)),
        in_specs=[row(D_SSD), row(D_SSD), row(D_SSD), row(D_XBC), row(DT_PAD),
                  pl.BlockSpec((1, D_SSD, N_STATE), lambda i: (blk(i), 0, 0)),
                  full((8, 128)), full((1, D_SSD)), full((1, D_SSD)), full(fold.shape)],
        out_specs=(row(D_XBC), row(D_SSD), row(DT_PAD), full((8, 128)), full((8, D_SSD))),
        scratch_shapes=[pltpu.VMEM((D_SSD, N_STATE), F32), pltpu.VMEM((CHUNK, D_SSD), F32)],
        compiler_params=_params(48),
    )(dyssd, yraw, z, xbc, dt_raw, hprev, par, dsk, normw, fold)


def _conv_proj_bwd(dz, dxbc, dsilu, xbc_pre, ddt, du5, x2, dxa, mod3, conv_w, w_in_pad, seq):
    t = x2.shape[0]
    tb = 256
    npb = seq // tb
    n_blocks = t // tb
    cw = 512

    def blk(i):
        return (i // npb) * npb + (npb - 1 - i % npb)

    def body(dz_ref, d_ref, ds_ref, cur_ref, halo_ref, ddt_ref, du5_ref, x_ref, dxa_ref, mod_ref, cw_ref, w_hbm,
             gx_ref, u_ref, dxp_ref, bacc_ref, acc_ref, w_vmem, win_x, win_d, sem):
        i = pl.program_id(0)
        _load_once(w_hbm, w_vmem, sem)

        @pl.when(i == 0)
        def _():
            acc_ref[...] = jnp.zeros_like(acc_ref)

        @pl.when(i % npb == 0)
        def _():
            bacc_ref[...] = jnp.zeros_like(bacc_ref)
            win_d[tb:tb + 8, :] = jnp.zeros((8, D_XBC), F32)

        @pl.when(i % npb != 0)
        def _():
            win_d[tb:tb + 8, :] = win_d[0:8, :]

        first_rows = (i % npb) == npb - 1
        win_x[0:8, :] = jnp.where(first_rows, 0.0, halo_ref[...])
        win_x[8:8 + tb, :] = cur_ref[...]
        w = cw_ref[...]
        for k in range(D_XBC // cw):
            cols = slice(k * cw, (k + 1) * cw)
            dpre = d_ref[:, cols] * ds_ref[:, cols]
            win_d[0:tb, cols] = dpre
            for j in range(4):
                acc_ref[3 - j:4 - j, cols] += jnp.sum(dpre * win_x[8 - j:8 - j + tb, cols], axis=0, keepdims=True)
            acc_ref[4:5, cols] += jnp.sum(dpre, axis=0, keepdims=True)
            dxp = w[3:4, cols] * dpre
            for j in (1, 2, 3):
                dxp = dxp + w[3 - j:4 - j, cols] * win_d[j:j + tb, cols]
            dxp_ref[:, cols] = dxp.astype(BF16)
        o1, o2, o3 = D_SSD, D_SSD + D_XBC, D_SSD + D_XBC + DT_PAD
        du = (jnp.dot(dz_ref[...].astype(BF16), w_vmem[0:o1, :], preferred_element_type=F32)
              + jnp.dot(dxp_ref[...], w_vmem[o1:o2, :], preferred_element_type=F32)
              + jnp.dot(ddt_ref[...].astype(BF16), w_vmem[o2:o3, :], preferred_element_type=F32)
              + jnp.dot(du5_ref[...].astype(BF16), w_vmem[o3:, :], preferred_element_type=F32))
        m = mod_ref[0]
        xv = x_ref[...]
        u_ref[...] = (xv * (1.0 + m[1:2]) + m[0:1]).astype(BF16)
        gx_ref[...] = dxa_ref[...] + du * (1.0 + m[1:2])
        bacc_ref[0, 0:1, :] += jnp.sum(du, axis=0, keepdims=True)
        bacc_ref[0, 1:2, :] += jnp.sum(du * xv, axis=0, keepdims=True)

    row = lambda w: pl.BlockSpec((tb, w), lambda i: (blk(i), 0))
    halo = pl.BlockSpec((8, D_XBC), lambda i: (jnp.maximum(blk(i) * (tb // 8) - 1, 0), 0))
    return pl.pallas_call(
        body, name="conv_proj_bwd", grid=(n_blocks,),
        out_shape=(jax.ShapeDtypeStruct((t, D_MODEL), F32), jax.ShapeDtypeStruct((t, D_MODEL), BF16),
                   jax.ShapeDtypeStruct((t, D_XBC), BF16), jax.ShapeDtypeStruct((t // seq, 8, D_MODEL), F32),
                   jax.ShapeDtypeStruct((8, D_XBC), F32)),
        in_specs=[row(D_SSD), row(D_XBC), row(D_XBC), row(D_XBC), halo, row(DT_PAD), row(D_S5), row(D_MODEL),
                  row(D_MODEL), pl.BlockSpec((1, N_MOD, D_MODEL), lambda i: (i // npb, 0, 0)),
                  pl.BlockSpec((4, D_XBC), lambda i: (0, 0)), ANY],
        out_specs=(row(D_MODEL), row(D_MODEL), row(D_XBC), pl.BlockSpec((1, 8, D_MODEL), lambda i: (i // npb, 0, 0)),
                   pl.BlockSpec((8, D_XBC), lambda i: (0, 0))),
        scratch_shapes=[pltpu.VMEM((D_INP, D_MODEL), BF16), pltpu.VMEM((tb + 8, D_XBC), F32),
                        pltpu.VMEM((tb + 8, D_XBC), F32), pltpu.SemaphoreType.DMA],
        compiler_params=_params(60),
    )(dz, dxbc, dsilu, xbc_pre, xbc_pre, ddt, du5, x2, dxa, mod3, conv_w, w_in_pad)


def _pad_rows(a, mult):
    r = a.shape[0]
    pad = (-r) % mult
    return a if pad == 0 else jnp.concatenate([a, jnp.zeros((pad,) + a.shape[1:], a.dtype)], axis=0)


_SMALL = ["conv_w", "conv_b", "dt_bias", "a_log", "d_ssd", "norm_w", "s5_a_re", "s5_a_im", "s5_log_dt", "s5_b_re",
          "s5_b_im", "s5_c_re", "s5_c_im", "s5_d", "b_glu", "ln1_g", "ln1_b", "b1", "b2", "ln2_g", "ln2_b",
          "loss_lanes"]
_NOT_UPDATED = {"conv_w": (1, 4, D_XBC), "loss_lanes": (1, D_MODEL)}


def _tile_rows(size):
    return 8 * (-(-size // 1024))


def _pack_small(d):
    parts = []
    for n in _SMALL:
        flat = d[n].reshape(-1).astype(F32)
        rows = _tile_rows(flat.shape[0])
        pad = rows * 128 - flat.shape[0]
        if pad:
            flat = jnp.concatenate([flat, jnp.zeros((pad,), F32)])
        parts.append(flat.reshape(rows, 128))
    return _pad_rows(jnp.concatenate(parts, axis=0), 256)


def _unpack_small(p, shapes):
    out, off = {}, 0
    for n in _SMALL:
        size = math.prod(shapes[n])
        rows = _tile_rows(size)
        out[n] = p[off:off + rows].reshape(-1)[:size].reshape(shapes[n])
        off += rows
    return out


def kernel(x, c, w_ada, b_ada, w_in, conv_w, conv_b, dt_bias, a_log, d_ssd, norm_w, s5_a_re, s5_a_im, s5_log_dt, s5_b_re, s5_b_im, s5_c_re, s5_c_im, s5_d, w_glu, b_glu, w_out, ln1_g, ln1_b, w1, b1, w2, b2, ln2_g, ln2_b, loss_target, m_w_ada, m_b_ada, m_w_in, m_conv_w, m_conv_b, m_dt_bias, m_a_log, m_d_ssd, m_norm_w, m_s5_a_re, m_s5_a_im, m_s5_log_dt, m_s5_b_re, m_s5_b_im, m_s5_c_re, m_s5_c_im, m_s5_d, m_w_glu, m_b_glu, m_w_out, m_ln1_g, m_ln1_b, m_w1, m_b1, m_w2, m_b2, m_ln2_g, m_ln2_b, v_w_ada, v_b_ada, v_w_in, v_conv_w, v_conv_b, v_dt_bias, v_a_log, v_d_ssd, v_norm_w, v_s5_a_re, v_s5_a_im, v_s5_log_dt, v_s5_b_re, v_s5_b_im, v_s5_c_re, v_s5_c_im, v_s5_d, v_w_glu, v_b_glu, v_w_out, v_ln1_g, v_ln1_b, v_w1, v_b1, v_w2, v_b2, v_ln2_g, v_ln2_b):
    weights = dict(w_ada=w_ada, b_ada=b_ada, w_in=w_in, conv_w=conv_w, conv_b=conv_b, dt_bias=dt_bias, a_log=a_log,
                   d_ssd=d_ssd, norm_w=norm_w, s5_a_re=s5_a_re, s5_a_im=s5_a_im, s5_log_dt=s5_log_dt, s5_b_re=s5_b_re,
                   s5_b_im=s5_b_im, s5_c_re=s5_c_re, s5_c_im=s5_c_im, s5_d=s5_d, w_glu=w_glu, b_glu=b_glu, w_out=w_out,
                   ln1_g=ln1_g, ln1_b=ln1_b, w1=w1, b1=b1, w2=w2, b2=b2, ln2_g=ln2_g, ln2_b=ln2_b)
    mom = dict(w_ada=m_w_ada, b_ada=m_b_ada, w_in=m_w_in, conv_w=m_conv_w, conv_b=m_conv_b, dt_bias=m_dt_bias,
               a_log=m_a_log, d_ssd=m_d_ssd, norm_w=m_norm_w, s5_a_re=m_s5_a_re, s5_a_im=m_s5_a_im,
               s5_log_dt=m_s5_log_dt, s5_b_re=m_s5_b_re, s5_b_im=m_s5_b_im, s5_c_re=m_s5_c_re, s5_c_im=m_s5_c_im,
               s5_d=m_s5_d, w_glu=m_w_glu, b_glu=m_b_glu, w_out=m_w_out, ln1_g=m_ln1_g, ln1_b=m_ln1_b, w1=m_w1, b1=m_b1,
               w2=m_w2, b2=m_b2, ln2_g=m_ln2_g, ln2_b=m_ln2_b)
    var = dict(w_ada=v_w_ada, b_ada=v_b_ada, w_in=v_w_in, conv_w=v_conv_w, conv_b=v_conv_b, dt_bias=v_dt_bias,
               a_log=v_a_log, d_ssd=v_d_ssd, norm_w=v_norm_w, s5_a_re=v_s5_a_re, s5_a_im=v_s5_a_im,
               s5_log_dt=v_s5_log_dt, s5_b_re=v_s5_b_re, s5_b_im=v_s5_b_im, s5_c_re=v_s5_c_re, s5_c_im=v_s5_c_im,
               s5_d=v_s5_d, w_glu=v_w_glu, b_glu=v_b_glu, w_out=v_w_out, ln1_g=v_ln1_g, ln1_b=v_ln1_b, w1=v_w1, b1=v_b1,
               w2=v_w2, b2=v_b2, ln2_g=v_ln2_g, ln2_b=v_ln2_b)
    names = list(weights)
    shapes = {n: weights[n].shape for n in names}

    nb, seq, _ = x.shape
    t = nb * seq
    dev = _dev_index()
    x2 = x.reshape(t, D_MODEL)
    tgt2 = loss_target.reshape(t, D_MODEL)

    cw_cols = conv_w.shape[2]
    small_in = jnp.concatenate([c.reshape(-1), conv_w.reshape(-1)]).reshape(-1, 128)
    big_names = ["w_in", "w_out", "w1", "w2", "w_glu"]
    local = {n: (a[0].T if n == "w_in" else a[0]) for n, a in weights.items() if n in big_names}
    shard_bf16 = {n: local[n].astype(BF16) for n in big_names}
    first = _all_gather([small_in, shard_bf16["w_in"], shard_bf16["w_glu"]], "gather_first")
    small_all = first[0].reshape(N_DEV, -1)
    c_all = small_all[:, :nb * D_MODEL].reshape(N_DEV * nb, D_MODEL)
    conv_w_full = small_all[:, nb * D_MODEL:].reshape(N_DEV, 4, cw_cols).transpose(1, 0, 2).reshape(4, D_XBC)

    w_in_t = first[1].reshape(D_IN, D_MODEL)
    w_in_pad = jnp.concatenate(
        [w_in_t[:D_SSD + D_XBC + N_HEADS], jnp.zeros((DT_PAD - N_HEADS, D_MODEL), BF16),
         w_in_t[D_SSD + D_XBC + N_HEADS:]], axis=0)
    w_glu_f = first[2].reshape(D_S5, D_S5)
    late_names = ["w_out", "w1", "w2"]

    ada_cols = w_ada.shape[2]
    b_cols = lax.dynamic_slice_in_dim(b_ada, dev * ada_cols, ada_cols, axis=1)
    mod_cols = _mod_fwd(c_all, w_ada[0], b_cols)
    mod_all = _all_gather([mod_cols], "gather_mod")[0]
    mod_mine = lax.dynamic_slice_in_dim(mod_all, dev * nb, nb, axis=1)
    mod3 = mod_mine.transpose(1, 0, 2).reshape(nb, N_MOD, D_MODEL)

    def pad_lanes(v, n):
        return jnp.concatenate([v, jnp.zeros((v.shape[0], n - v.shape[1]), F32)], axis=1)

    par = _pad_rows(jnp.concatenate([pad_lanes(dt_bias, 128), pad_lanes(a_log, 128)], axis=0), 8)
    dsk = jnp.repeat(d_ssd[0], HEADDIM).reshape(1, D_SSD)
    ar = s5_a_re.reshape(1, S5_N)
    ai = s5_a_im.reshape(1, S5_N)
    ldt = jnp.repeat(s5_log_dt[0], S5_P).reshape(1, S5_N)
    br_t = s5_b_re[0].transpose(2, 0, 1).reshape(S5_CH, S5_N)
    bi_t = s5_b_im[0].transpose(2, 0, 1).reshape(S5_CH, S5_N)
    bb_re_t, bb_im_t, pf_re, pf_im, pr_re, pr_im = _s5_params_fwd(ar, ai, ldt, br_t, bi_t)
    gpb = S5_GROUPS // S5_BLOCKS
    mask_b = (jnp.arange(128)[:, None] // S5_CH) == (jnp.arange(512)[None, :] // S5_P)

    def dense_b(bt_):
        blocks = bt_.reshape(S5_CH, S5_BLOCKS, 512).transpose(1, 0, 2)
        return jnp.where(mask_b, jnp.tile(blocks, (1, gpb, 1)), 0.0).astype(BF16)

    def dense_c(cc):
        blocks = cc[0].transpose(0, 2, 1).reshape(S5_BLOCKS, 512, S5_CH)
        return jnp.where(mask_b.T, jnp.tile(blocks, (1, 1, gpb)), 0.0).astype(BF16)

    bb_re, bb_im = dense_b(bb_re_t), dense_b(bb_im_t)
    cc_re, cc_im = dense_c(s5_c_re), dense_c(s5_c_im)
    s5d = s5_d.reshape(1, D_S5)
    ln1 = jnp.concatenate([ln1_g, ln1_b], axis=0)
    vec1 = _pad_rows(jnp.concatenate([b2, ln2_g, ln2_b], axis=0), 8)

    z, xbc_pre, xbc, dsilu, dt_raw, u5 = _proj_conv_fwd(x2, mod3, w_in_pad, conv_w_full, conv_b, seq)
    late_in, dt_raw = lax.optimization_barrier(([shard_bf16[n] for n in late_names], dt_raw))
    late_sems = _gather_start(late_in, "gather_late_start")
    yraw, ycat, hprev = _ssd_fwd(xbc, z, dt_raw, par + late_sems[4][0, 0], dsk, norm_w, seq)
    s_re, s_im, ypre, ycat = _s5_fwd(u5, bb_re, bb_im, cc_re, cc_im, pf_re, pf_im, s5d, w_glu_f, b_glu, ycat, seq)
    sent, landed = _gather_wait(late_sems[0], late_sems[1], late_sems[2], late_sems[3], ycat, "gather_late_wait")
    gathered = {n: lax.dynamic_update_index_in_dim(l, x, dev, 0) for n, x, l in zip(late_names, sent, landed)}
    w_out_f = gathered["w_out"].reshape(2 * D_MODEL, D_MODEL)
    w1_blocks = gathered["w1"]
    w2_f = gathered["w2"].reshape(D_FF, D_MODEL)
    mix, x1 = _out_ln1(ycat, x2, mod3, w_out_f, ln1, seq)

    dx1, u2b, hb, dhpb, dob, gacc2, db1, bacc2 = _mlp_fwd_bwd(x1, tgt2, mod3, w1_blocks, w2_f, vec1, b1, seq)

    dmixb, dxa, dyssd, dy5, gacc1, bacc1 = _ln1_out_bwd(dx1, x2, mix, mod3, w_out_f, ln1, seq)

    g_w2 = _atb(hb, dob, "gw2")
    g_w1 = _atb(u2b, dhpb, "gw1")
    g_wout = _atb(ycat, dmixb, "gwout")
    core = lax.axis_index("c").astype(jnp.int32).reshape(1)
    chip = 2 * lax.axis_index("x") + lax.axis_index("y")

    def chip_sums_of(names, grads, tag):
        by_dest = [g if g.ndim == 2 else g.reshape((4, 2) + g.shape[1:]) for g in grads]
        from_sibling = _sibling_swap(by_dest, "rs_swap_" + tag)
        return [_add_halves(g, r, core, "rs_add_" + n) for g, r, n in zip(by_dest, from_sibling, names)]

    early_names = ["w_out", "w1", "w2"]
    early_dest = [g_wout.reshape((4, 2) + w_out.shape[1:]), g_w1, g_w2.reshape((4, 2) + w2.shape[1:])]
    swap = _sibling_swap_start(early_dest, "rs_early_swap_start")
    du5, vacc, sacc, d_cc, d_bb, g_wglu = _s5_bwd(dy5, ypre, u5, s_re, s_im, bb_re, bb_im, cc_re, cc_im,
                                                  pr_re, pr_im, s5d + swap[4][0, 0], w_glu_f, b_glu, seq)
    early_dest, from_sibling = _sibling_swap_wait(swap[0], swap[1], swap[2], swap[3], du5, "rs_early_swap_wait")
    early_sums = [_add_halves(g, r, core, "rs_add_" + n) for g, r, n in zip(early_dest, from_sibling, early_names)]
    early = _all_to_all_start(early_sums, "rs_early_start")
    dxbc, dz, ddt, dpar, cacc = _ssd_bwd(dyssd, yraw, z, xbc, dt_raw, hprev, par + early[4][0, 0], dsk, norm_w, seq)
    grad_x2, ub, dxpb, bacc0, conv_acc = _conv_proj_bwd(dz, dxbc, dsilu, xbc_pre, ddt, du5, x2, dxa, mod3,
                                                        conv_w_full, w_in_pad, seq)

    def diag_b(dd):
        kept = jnp.where(mask_b, dd, 0.0).reshape(S5_BLOCKS, gpb, S5_CH, 512).sum(1)
        return kept.transpose(1, 0, 2).reshape(S5_CH, S5_N)

    def diag_c(dd):
        kept = jnp.where(mask_b.T, dd, 0.0).reshape(S5_BLOCKS, 512, gpb, S5_CH).sum(2)
        return kept.reshape(S5_GROUPS, S5_P, S5_CH).transpose(0, 2, 1)

    g_ar, g_ai, g_ldt, g_br_t, g_bi_t = _s5_params_bwd(ar, ai, ldt, br_t, bi_t, vacc[0:1], vacc[1:2],
                                                      diag_b(d_bb[:S5_BLOCKS]), diag_b(d_bb[S5_BLOCKS:]))

    def from_t(gt):
        return gt.reshape(S5_CH, S5_GROUPS, S5_P).transpose(1, 2, 0)

    small_g = dict(
        conv_w=conv_acc[0:4], conv_b=conv_acc[4:5], dt_bias=dpar[0:1, :N_HEADS], a_log=dpar[1:2, :N_HEADS],
        d_ssd=cacc[0].reshape(N_HEADS, HEADDIM).sum(1), norm_w=cacc[1:2],
        s5_a_re=g_ar, s5_a_im=g_ai, s5_log_dt=g_ldt[:, :S5_GROUPS], s5_b_re=from_t(g_br_t), s5_b_im=from_t(g_bi_t),
        s5_c_re=diag_c(d_cc[:S5_BLOCKS]), s5_c_im=diag_c(d_cc[S5_BLOCKS:]), s5_d=sacc[0:1], b_glu=sacc[1:2],
        ln1_g=gacc1[0:1], ln1_b=gacc1[1:2], b1=db1, b2=gacc2[2:3], ln2_g=gacc2[0:1], ln2_b=gacc2[1:2],
        loss_lanes=gacc2[3:4])

    dmod = jnp.concatenate([bacc0[:, 0], bacc0[:, 1], bacc1[:, 0], bacc2[:, 0], bacc2[:, 1], bacc2[:, 2]], axis=1)
    small_sems = _gather_start([dmod, _pack_small(small_g)], "gather_small_start")
    tok = small_sems[4]
    g_win_t = jnp.concatenate([_atb(dz, ub, "gwin_z", tok), _atb(dxpb, ub, "gwin_xbc", tok),
                               _atb(ddt, ub, "gwin_dt", tok)[:N_HEADS], _atb(du5, ub, "gwin_s5", tok)], axis=0)
    sent, landed = _gather_wait(small_sems[0], small_sems[1], small_sems[2], small_sems[3], g_win_t,
                                "gather_small_wait")
    dmod_all, small_parts = [lax.dynamic_update_index_in_dim(l, x, dev, 0) for x, l in zip(sent, landed)]
    dmod_all = dmod_all.reshape(N_DEV * nb, N_MOD * D_MODEL)
    dmod_cols = lax.dynamic_slice_in_dim(dmod_all, dev * ada_cols, ada_cols, axis=1)
    g_wada, g_bada = _mod_bwd(c_all, dmod_cols, dmod_all)

    late_rs = ["w_in", "w_glu"]
    late_g, small_parts = lax.optimization_barrier(
        ([g_win_t.reshape(N_DEV, w_in.shape[2], D_MODEL), g_wglu.reshape((N_DEV,) + w_glu.shape[1:])], small_parts))
    late = _all_to_all_start(chip_sums_of(late_rs, late_g, "late"), "rs_late_start")

    def own_block_in(landed, sent):
        return [lax.dynamic_update_index_in_dim(l, lax.dynamic_index_in_dim(h, chip, 0, keepdims=False), chip, 0)
                for l, h in zip(landed, sent)]

    sent, landed = _all_to_all_wait(early[0], early[1], early[2], early[3], late[4], "rs_early_wait")
    parts = dict(zip(early_names, own_block_in(landed, sent)))
    res = {k: {} for k in "gdmv"}

    def update(n):
        w_m_v = [(a[n][0].T if n == "w_in" else a[n][0]) for a in (weights, mom, var)]
        outs = _adamw(parts[n], *w_m_v, "adamw_" + n)
        for k, a in zip("gdmv", outs):
            res[k][n] = (a.T if n == "w_in" else a)[None]

    for n in early_names:
        update(n)
    sent, landed = _all_to_all_wait(late[0], late[1], late[2], late[3], res["d"]["w2"], "rs_late_wait")
    parts.update(zip(late_rs, own_block_in(landed, sent)))
    for n in late_rs:
        update(n)

    ag, ad, am, av = _adamw(g_wada[None], w_ada[0], m_w_ada[0], v_w_ada[0], "adamw_w_ada")
    for k, a in (("g", ag), ("d", ad), ("m", am), ("v", av)):
        res[k]["w_ada"] = a[None]
    bg_, bd_, bm_, bv_ = _adamw(g_bada.reshape(1, -1, 128), b_ada.reshape(-1, 128), m_b_ada.reshape(-1, 128),
                                v_b_ada.reshape(-1, 128), "adamw_b_ada")
    for k, a in (("g", bg_), ("d", bd_), ("m", bm_), ("v", bv_)):
        res[k]["b_ada"] = a.reshape(shapes["b_ada"])

    small_shapes = {**shapes, **_NOT_UPDATED}
    rep = {n: (jnp.zeros(_NOT_UPDATED[n], F32) if n in _NOT_UPDATED else weights[n]) for n in _SMALL}
    rep_m = {n: (jnp.zeros(_NOT_UPDATED[n], F32) if n in _NOT_UPDATED else mom[n]) for n in _SMALL}
    rep_v = {n: (jnp.ones(_NOT_UPDATED[n], F32) if n in _NOT_UPDATED else var[n]) for n in _SMALL}
    sg_, sd_, sm_, sv_ = _adamw(small_parts, _pack_small(rep), _pack_small(rep_m), _pack_small(rep_v), "adamw_small")
    for k, p in (("g", sg_), ("d", sd_), ("m", sm_), ("v", sv_)):
        un = _unpack_small(p, small_shapes)
        for n in _SMALL:
            if n not in _NOT_UPDATED:
                res[k][n] = un[n]
    summed = _unpack_small(sg_, small_shapes)
    loss = 0.5 / D_MODEL * jnp.sum(summed["loss_lanes"])
    g_conv_full = summed["conv_w"][0]
    g_conv_mine = lax.dynamic_slice_in_dim(g_conv_full, dev * cw_cols, cw_cols, axis=1)
    cg_, cd_, cm_, cv_ = _adamw(g_conv_mine[None], conv_w[0], m_conv_w[0], v_conv_w[0], "adamw_conv_w")
    for k, a in (("g", cg_), ("d", cd_), ("m", cm_), ("v", cv_)):
        res[k]["conv_w"] = a[None]

    grad_x = grad_x2.reshape(nb, seq, D_MODEL)
    return (loss, grad_x, *[res["g"][n] for n in names], *[res["d"][n] for n in names],
            *[res["m"][n] for n in names], *[res["v"][n] for n in names])
```

```python
import functools
import math

import jax
import jax.numpy as jnp
from jax import lax
from jax.experimental import pallas as pl
from jax.experimental.pallas import tpu as pltpu

F32, BF16 = jnp.float32, jnp.bfloat16
MESH = pl.DeviceIdType.MESH
N_DEV = 8

D_MODEL = 1024
D_SSD = 1536
N_HEADS = 24
HEADDIM = 64
N_GROUPS = 4
HPG = 6
GW = HPG * HEADDIM
N_STATE = 128
CHUNK = 128
D_XBC = 2560
D_S5 = 512
S5_GROUPS = 32
S5_CH = 16
S5_P = 64
S5_N = S5_GROUPS * S5_P
D_IN = 4632
DT_PAD = 128
D_INP = D_SSD + D_XBC + DT_PAD + D_S5
D_FF = 4096
N_MOD = 6
ALPHA = 2.0 ** 0.25
EPS = 1e-5
LR, B1, B2, AEPS, WD, STEP = 0.001, 0.9, 0.999, 1e-08, 0.01, 10

NT = (((1,), (1,)), ((), ()))
TN = (((0,), (0,)), ((), ()))
ANY = pl.BlockSpec(memory_space=pl.ANY)
HIGHEST = lax.Precision.HIGHEST


def _mm(a, b):
    return jnp.dot(a.astype(BF16), b.astype(BF16), preferred_element_type=F32)


def _mm_nt(a, b):
    return lax.dot_general(a.astype(BF16), b.astype(BF16), NT, preferred_element_type=F32)


def _mm_tn(a, b):
    return lax.dot_general(a.astype(BF16), b.astype(BF16), TN, preferred_element_type=F32)


def _row_block(r, cap):
    best = r
    for cand in range(8, min(r, cap) + 1, 8):
        if r % cand == 0:
            best = cand
    return best if best <= cap else r


def _params(vmem_mb):
    return pltpu.CompilerParams(vmem_limit_bytes=vmem_mb << 20)


def _sigmoid(x):
    return 0.5 * (jnp.tanh(0.5 * x) + 1.0)


def _softplus(x):
    return jnp.maximum(x, 0.0) + jnp.log(1.0 + jnp.exp(-jnp.abs(x)))


_GK = math.sqrt(2.0 / math.pi)


def _gelu(x):
    return 0.5 * x * (1.0 + jnp.tanh(_GK * (x + 0.044715 * x * x * x)))


def _gelu_grad(x):
    t = jnp.tanh(_GK * (x + 0.044715 * x * x * x))
    return 0.5 * (1.0 + t) + 0.5 * x * (1.0 - t * t) * _GK * (1.0 + 3.0 * 0.044715 * x * x)


def _dev_index():
    return 4 * lax.axis_index("x") + 2 * lax.axis_index("y") + lax.axis_index("c")


def _all_gather(xs, name):
    n = len(xs)

    def body(*refs):
        x_refs, out_refs = refs[:n], refs[n:2 * n]
        send_sems, recv_sems, local_sems = refs[2 * n:]
        ix, iy, ic = lax.axis_index("x"), lax.axis_index("y"), lax.axis_index("c")
        me, sibling = (ix, iy, ic), (ix, iy, 1 - ic)
        chips = [(1 - ix, iy), (ix, 1 - iy), (1 - ix, 1 - iy)]

        def slot(a, px, py, pc):
            return out_refs[a].at[4 * px + 2 * py + pc]

        def copy(a, k, block, to, src=None):
            return pltpu.make_async_remote_copy(
                src_ref=slot(a, *block) if src is None else src, dst_ref=slot(a, *block),
                send_sem=send_sems.at[7 * a + k], recv_sem=recv_sems.at[7 * a + k], device_id=to, device_id_type=MESH)

        mine = [pltpu.make_async_copy(x_refs[a], slot(a, *me), local_sems.at[a]) for a in range(n)]
        for cp in mine:
            cp.start()
        first = []
        for j, chip in enumerate(chips):
            first += [copy(a, 1 + j, me, (*chip, ic), src=x_refs[a]) for a in range(n)]
        first += [copy(a, 0, me, sibling, src=x_refs[a]) for a in range(n)]
        for cp in first:
            cp.start()
        passed = []
        for j, chip in enumerate(chips):
            for a in range(n):
                copy(a, 1 + j, (*chip, ic), me).wait_recv()
                cp = copy(a, 4 + j, (*chip, ic), sibling)
                cp.start()
                passed.append(cp)
        for a in range(n):
            copy(a, 0, sibling, me).wait_recv()
            for j, chip in enumerate(chips):
                copy(a, 4 + j, (*chip, 1 - ic), me).wait_recv()
        for cp in first + passed:
            cp.wait_send()
        for cp in mine:
            cp.wait()

    return pl.pallas_call(
        body, name=name, out_shape=tuple(jax.ShapeDtypeStruct((N_DEV,) + x.shape, x.dtype) for x in xs),
        in_specs=[ANY] * n, out_specs=tuple([ANY] * n),
        scratch_shapes=[pltpu.SemaphoreType.DMA((7 * n,)), pltpu.SemaphoreType.DMA((7 * n,)),
                        pltpu.SemaphoreType.DMA((n,))],
    )(*xs)


HBM = pl.BlockSpec(memory_space=pltpu.HBM)
SEM = pl.BlockSpec(memory_space=pltpu.SEMAPHORE)
DATAFLOW = pltpu.SideEffectType.DATAFLOW_SIDE_EFFECTING


def _peer(k):
    ix, iy, ic = lax.axis_index("x"), lax.axis_index("y"), lax.axis_index("c")
    return (1 - ix if k & 4 else ix, 1 - iy if k & 2 else iy, 1 - ic if k & 1 else ic)


def _block_of(p):
    return 4 * p[0] + 2 * p[1] + p[2]


def _gather_start(xs, name):
    n = len(xs)
    lands = [lax.empty((N_DEV,) + x.shape, x.dtype) for x in xs]

    def body(*refs):
        x_refs, land_refs = refs[:n], refs[n:2 * n]
        send_sems, recv_sems = refs[2 * n], refs[2 * n + 1]
        token = refs[-1]
        me = _block_of(_peer(0))
        for a in range(n):
            for k in range(1, N_DEV):
                pltpu.make_async_remote_copy(
                    src_ref=x_refs[a], dst_ref=land_refs[a].at[me], send_sem=send_sems.at[7 * a + k - 1],
                    recv_sem=recv_sems.at[7 * a + k - 1], device_id=_peer(k), device_id_type=MESH).start()
        token[...] = jnp.zeros_like(token)

    outs = pl.pallas_call(
        body, name=name,
        out_shape=(pltpu.SemaphoreType.DMA((7 * n,)), pltpu.SemaphoreType.DMA((7 * n,)))
        + tuple(pltpu.HBM(x.shape, x.dtype) for x in xs) + tuple(pltpu.HBM(l.shape, l.dtype) for l in lands)
        + (jax.ShapeDtypeStruct((8, 128), F32),),
        in_specs=[HBM] * (2 * n), out_specs=(SEM, SEM) + (HBM,) * (2 * n) + (pl.BlockSpec(memory_space=pltpu.VMEM),),
        input_output_aliases={i: 2 + i for i in range(2 * n)},
        compiler_params=pltpu.CompilerParams(has_side_effects=DATAFLOW),
    )(*[pltpu.with_memory_space_constraint(x, pltpu.HBM) for x in xs],
      *[pltpu.with_memory_space_constraint(l, pltpu.HBM) for l in lands])
    return outs[0], outs[1], outs[2:2 + n], outs[2 + n:2 + 2 * n], outs[-1]


def _gather_wait(send_sems, recv_sems, xs_thru, lands_thru, after, name):
    n = len(xs_thru)

    def body(*refs):
        x_refs, land_refs = refs[:n], refs[n:2 * n]
        send_sems, recv_sems = refs[2 * n], refs[2 * n + 1]
        for a in range(n):
            for k in range(1, N_DEV):
                cp = pltpu.make_async_remote_copy(
                    src_ref=x_refs[a], dst_ref=land_refs[a].at[_block_of(_peer(k))], send_sem=send_sems.at[7 * a + k - 1],
                    recv_sem=recv_sems.at[7 * a + k - 1], device_id=_peer(k), device_id_type=MESH)
                cp.wait_send()
                cp.wait_recv()

    outs = pl.pallas_call(
        body, name=name,
        out_shape=tuple(pltpu.HBM(x.shape, x.dtype) for x in xs_thru)
        + tuple(pltpu.HBM(l.shape, l.dtype) for l in lands_thru),
        in_specs=[HBM] * (2 * n) + [SEM, SEM, ANY], out_specs=(HBM,) * (2 * n),
        input_output_aliases={i: i for i in range(2 * n)},
        compiler_params=pltpu.CompilerParams(has_side_effects=DATAFLOW),
    )(*xs_thru, *lands_thru, send_sems, recv_sems, after)
    return outs[:n], outs[n:]


def _chip_peer(k):
    ix, iy = lax.axis_index("x"), lax.axis_index("y")
    return (1 - ix if k & 2 else ix, 1 - iy if k & 1 else iy)


def _all_to_all_start(hs, name):
    n = len(hs)
    lands = [lax.empty(h.shape, h.dtype) for h in hs]

    def body(*refs):
        h_refs, land_refs = refs[:n], refs[n:2 * n]
        send_sems, recv_sems = refs[2 * n], refs[2 * n + 1]
        token = refs[-1]
        ic = lax.axis_index("c")
        mx, my = _chip_peer(0)
        for a in range(n):
            for k in range(1, 4):
                px, py = _chip_peer(k)
                pltpu.make_async_remote_copy(
                    src_ref=h_refs[a].at[2 * px + py], dst_ref=land_refs[a].at[2 * mx + my],
                    send_sem=send_sems.at[3 * a + k - 1], recv_sem=recv_sems.at[3 * a + k - 1],
                    device_id=(px, py, ic), device_id_type=MESH).start()
        token[...] = jnp.zeros_like(token)

    outs = pl.pallas_call(
        body, name=name,
        out_shape=(pltpu.SemaphoreType.DMA((3 * n,)), pltpu.SemaphoreType.DMA((3 * n,)))
        + tuple(pltpu.HBM(h.shape, h.dtype) for h in hs) + tuple(pltpu.HBM(l.shape, l.dtype) for l in lands)
        + (jax.ShapeDtypeStruct((8, 128), F32),),
        in_specs=[HBM] * (2 * n), out_specs=(SEM, SEM) + (HBM,) * (2 * n) + (pl.BlockSpec(memory_space=pltpu.VMEM),),
        input_output_aliases={i: 2 + i for i in range(2 * n)},
        compiler_params=pltpu.CompilerParams(has_side_effects=DATAFLOW),
    )(*[pltpu.with_memory_space_constraint(h, pltpu.HBM) for h in hs],
      *[pltpu.with_memory_space_constraint(l, pltpu.HBM) for l in lands])
    return outs[0], outs[1], outs[2:2 + n], outs[2 + n:2 + 2 * n], outs[-1]


def _all_to_all_wait(send_sems, recv_sems, hs_thru, lands_thru, after, name):
    n = len(hs_thru)

    def body(*refs):
        h_refs, land_refs = refs[:n], refs[n:2 * n]
        send_sems, recv_sems = refs[2 * n], refs[2 * n + 1]
        ic = lax.axis_index("c")
        for a in range(n):
            for k in range(1, 4):
                px, py = _chip_peer(k)
                cp = pltpu.make_async_remote_copy(
                    src_ref=h_refs[a].at[2 * px + py], dst_ref=land_refs[a].at[2 * px + py],
                    send_sem=send_sems.at[3 * a + k - 1], recv_sem=recv_sems.at[3 * a + k - 1],
                    device_id=(px, py, ic), device_id_type=MESH)
                cp.wait_send()
                cp.wait_recv()

    outs = pl.pallas_call(
        body, name=name,
        out_shape=tuple(pltpu.HBM(h.shape, h.dtype) for h in hs_thru)
        + tuple(pltpu.HBM(l.shape, l.dtype) for l in lands_thru),
        in_specs=[HBM] * (2 * n) + [SEM, SEM, ANY], out_specs=(HBM,) * (2 * n),
        input_output_aliases={i: i for i in range(2 * n)},
        compiler_params=pltpu.CompilerParams(has_side_effects=DATAFLOW),
    )(*hs_thru, *lands_thru, send_sems, recv_sems, after)
    return outs[:n], outs[n:]


def _sibling_block(g_ref, q):
    ic = lax.axis_index("c")
    if len(g_ref.shape) == 4:
        return g_ref.at[q, 1 - ic]
    cw = g_ref.shape[1] // N_DEV
    return g_ref.at[:, pl.ds(pl.multiple_of((2 * q + 1 - ic) * cw, 128), cw)]


def _sibling_swap_start(gs, name):
    n = len(gs)
    lands = [lax.empty((4,) + (g.shape[2:] if g.ndim == 4 else (g.shape[0], g.shape[1] // N_DEV)), g.dtype) for g in gs]

    def body(*refs):
        g_refs, land_refs = refs[:n], refs[n:2 * n]
        send_sems, recv_sems = refs[2 * n], refs[2 * n + 1]
        token = refs[-1]
        for a in range(n):
            for q in range(4):
                pltpu.make_async_remote_copy(
                    src_ref=_sibling_block(g_refs[a], q), dst_ref=land_refs[a].at[q],
                    send_sem=send_sems.at[4 * a + q], recv_sem=recv_sems.at[4 * a + q],
                    device_id=_peer(1), device_id_type=MESH).start()
        token[...] = jnp.zeros_like(token)

    outs = pl.pallas_call(
        body, name=name,
        out_shape=(pltpu.SemaphoreType.DMA((4 * n,)), pltpu.SemaphoreType.DMA((4 * n,)))
        + tuple(pltpu.HBM(g.shape, g.dtype) for g in gs) + tuple(pltpu.HBM(l.shape, l.dtype) for l in lands)
        + (jax.ShapeDtypeStruct((8, 128), F32),),
        in_specs=[HBM] * (2 * n), out_specs=(SEM, SEM) + (HBM,) * (2 * n) + (pl.BlockSpec(memory_space=pltpu.VMEM),),
        input_output_aliases={i: 2 + i for i in range(2 * n)},
        compiler_params=pltpu.CompilerParams(has_side_effects=DATAFLOW),
    )(*[pltpu.with_memory_space_constraint(g, pltpu.HBM) for g in gs],
      *[pltpu.with_memory_space_constraint(l, pltpu.HBM) for l in lands])
    return outs[0], outs[1], outs[2:2 + n], outs[2 + n:2 + 2 * n], outs[-1]


def _sibling_swap_wait(send_sems, recv_sems, gs_thru, lands_thru, after, name):
    n = len(gs_thru)

    def body(*refs):
        g_refs, land_refs = refs[:n], refs[n:2 * n]
        send_sems, recv_sems = refs[2 * n], refs[2 * n + 1]
        for a in range(n):
            for q in range(4):
                cp = pltpu.make_async_remote_copy(
                    src_ref=_sibling_block(g_refs[a], q), dst_ref=land_refs[a].at[q],
                    send_sem=send_sems.at[4 * a + q], recv_sem=recv_sems.at[4 * a + q],
                    device_id=_peer(1), device_id_type=MESH)
                cp.wait_send()
                cp.wait_recv()

    outs = pl.pallas_call(
        body, name=name,
        out_shape=tuple(pltpu.HBM(g.shape, g.dtype) for g in gs_thru)
        + tuple(pltpu.HBM(l.shape, l.dtype) for l in lands_thru),
        in_specs=[HBM] * (2 * n) + [SEM, SEM, ANY], out_specs=(HBM,) * (2 * n),
        input_output_aliases={i: i for i in range(2 * n)},
        compiler_params=pltpu.CompilerParams(has_side_effects=DATAFLOW),
    )(*gs_thru, *lands_thru, send_sems, recv_sems, after)
    return outs[:n], outs[n:]


def _sibling_swap(gs, name):
    n = len(gs)

    def body(*refs):
        g_refs, recv_refs = refs[:n], refs[n:2 * n]
        send_sems, recv_sems = refs[2 * n:]
        ix, iy, ic = lax.axis_index("x"), lax.axis_index("y"), lax.axis_index("c")
        cps = []
        for a in range(n):
            for q in range(4):
                cps.append(pltpu.make_async_remote_copy(
                    src_ref=_sibling_block(g_refs[a], q), dst_ref=recv_refs[a].at[q],
                    send_sem=send_sems.at[4 * a + q], recv_sem=recv_sems.at[4 * a + q],
                    device_id=(ix, iy, 1 - ic), device_id_type=MESH))
        for cp in cps:
            cp.start()
        for cp in cps:
            cp.wait()

    return pl.pallas_call(
        body, name=name,
        out_shape=tuple(jax.ShapeDtypeStruct(
            (4,) + (g.shape[2:] if g.ndim == 4 else (g.shape[0], g.shape[1] // N_DEV)), g.dtype) for g in gs),
        in_specs=[ANY] * n, out_specs=tuple([ANY] * n),
        scratch_shapes=[pltpu.SemaphoreType.DMA((4 * n,)), pltpu.SemaphoreType.DMA((4 * n,))],
    )(*gs)


def _add_halves(g, recv, core, name):
    _, r, c = recv.shape
    br = _row_block(r, 512)
    stacked = g.ndim == 4

    def body(core_ref, g_ref, r_ref, o_ref):
        o_ref[0] = ((g_ref[0, 0] if stacked else g_ref[...]) + r_ref[0]).astype(BF16)

    spec = pl.BlockSpec((1, br, c), lambda i, j, core_ref: (i, j, 0))
    if stacked:
        g_spec = pl.BlockSpec((1, 1, br, c), lambda i, j, core_ref: (i, core_ref[0], j, 0))
    else:
        g_spec = pl.BlockSpec((br, c), lambda i, j, core_ref: (j, 2 * i + core_ref[0]))
    return pl.pallas_call(
        body, name=name, out_shape=jax.ShapeDtypeStruct(recv.shape, BF16),
        grid_spec=pltpu.PrefetchScalarGridSpec(
            num_scalar_prefetch=1, grid=(4, r // br), in_specs=[g_spec, spec], out_specs=spec),
        compiler_params=_params(32),
    )(core, g, recv)


def _adamw(parts, w, m, v, name):
    n_parts, r, c = parts.shape
    if r % 8 == 0:
        br, bc = _row_block(r, 512 if c <= 1024 else 256), c
    else:
        br, bc = r, (256 if c % 256 == 0 else c)

    def body(p_ref, w_ref, m_ref, v_ref, g_out, d_out, m_out, v_out):
        g = p_ref[0].astype(F32)
        for p in range(1, n_parts):
            g = g + p_ref[p].astype(F32)
        m2 = B1 * m_ref[...] + (1.0 - B1) * g
        v2 = B2 * v_ref[...] + (1.0 - B2) * (g * g)
        m_hat = m2 / (1.0 - B1 ** STEP)
        v_hat = v2 / (1.0 - B2 ** STEP)
        g_out[...] = g
        d_out[...] = -LR * (m_hat / (jnp.sqrt(v_hat) + AEPS) + WD * w_ref[...])
        m_out[...] = m2
        v_out[...] = v2

    spec = pl.BlockSpec((br, bc), lambda i, j: (i, j))
    out = jax.ShapeDtypeStruct((r, c), F32)
    return pl.pallas_call(
        body, name=name, out_shape=(out, out, out, out), grid=(r // br, c // bc),
        in_specs=[pl.BlockSpec((n_parts, br, bc), lambda i, j: (0, i, j)), spec, spec, spec],
        out_specs=(spec, spec, spec, spec), compiler_params=_params(40),
    )(parts, w, m, v)


def _atb(a, b, name, after=None):
    t, k1 = a.shape
    k2 = b.shape[1]
    bt = math.gcd(t, 2048)

    def pick(k):
        for cand in (1024, 768, 512, 384, 256, 128):
            if k % cand == 0:
                return cand
        return k

    b1, b2 = pick(k1), pick(k2)

    def body(a_ref, b_ref, *rest):
        o_ref = rest[-1]

        @pl.when(pl.program_id(2) == 0)
        def _():
            o_ref[...] = jnp.zeros_like(o_ref)
        o_ref[...] += _mm_tn(a_ref[...], b_ref[...])

    extra = [] if after is None else [after]
    return pl.pallas_call(
        body, name=name, out_shape=jax.ShapeDtypeStruct((k1, k2), F32), grid=(k1 // b1, k2 // b2, t // bt),
        in_specs=[pl.BlockSpec((bt, b1), lambda i, j, k: (k, i)), pl.BlockSpec((bt, b2), lambda i, j, k: (k, j))]
        + [ANY] * len(extra),
        out_specs=pl.BlockSpec((b1, b2), lambda i, j, k: (i, j)), compiler_params=_params(48),
    )(a, b, *extra)


def _mod_fwd(c_all, w_ada, b_cols):
    def body(c_ref, w_ref, b_ref, o_ref):
        cc = c_ref[...]
        cond = cc * _sigmoid(cc)
        o_ref[...] = _mm(cond, w_ref[...]) + b_ref[...]

    return pl.pallas_call(body, name="mod_fwd", out_shape=jax.ShapeDtypeStruct((c_all.shape[0], w_ada.shape[1]), F32),
                          compiler_params=_params(32))(c_all, w_ada, b_cols)


def _mod_bwd(c_all, dmod_cols, dmod_all):
    def body(c_ref, dc_ref, da_ref, gw_ref, gb_ref):
        cc = c_ref[...]
        cond = cc * _sigmoid(cc)
        gw_ref[...] = _mm_tn(cond, dc_ref[...])
        gb_ref[...] = jnp.sum(da_ref[...], axis=0, keepdims=True)

    return pl.pallas_call(
        body, name="mod_bwd",
        out_shape=(jax.ShapeDtypeStruct((D_MODEL, dmod_cols.shape[1]), F32), jax.ShapeDtypeStruct((1, dmod_all.shape[1]), F32)),
        compiler_params=_params(32))(c_all, dmod_cols, dmod_all)


def _load_once(hbm_ref, vmem_ref, sem):
    @pl.when(pl.program_id(0) == 0)
    def _():
        cp = pltpu.make_async_copy(hbm_ref, vmem_ref, sem)
        cp.start()
        cp.wait()


def _conv_taps(win_ref, w, tb, cols):
    shifted = [win_ref[8 - j:8 - j + tb, cols] for j in range(4)]
    acc = w[3:4] * shifted[0]
    for j in (1, 2, 3):
        acc = acc + w[3 - j:4 - j] * shifted[j]
    return acc, shifted


def _proj_conv_fwd(x2, mod3, w_in_pad, conv_w, conv_b, seq):
    t = x2.shape[0]
    tb = 256
    npb = seq // tb
    cw = 512

    def body(x_ref, mod_ref, w_hbm, cw_ref, cb_ref, z_ref, pre_ref, xbc_ref, dsilu_ref, dt_ref, u5_ref, w_vmem, win, sem):
        _load_once(w_hbm, w_vmem, sem)
        first = (pl.program_id(0) % npb) == 0

        @pl.when(first)
        def _():
            win[0:8, :] = jnp.zeros((8, D_XBC), F32)

        @pl.when(jnp.logical_not(first))
        def _():
            win[0:8, :] = win[tb:tb + 8, :]

        m = mod_ref[0]
        u = (x_ref[...] * (1.0 + m[1:2]) + m[0:1]).astype(BF16)
        z_ref[...] = lax.dot_general(u, w_vmem[0:D_SSD, :], NT, preferred_element_type=F32)
        dt_ref[...] = lax.dot_general(u, w_vmem[D_SSD + D_XBC:D_SSD + D_XBC + DT_PAD, :], NT,
                                      preferred_element_type=F32)
        u5_ref[...] = lax.dot_general(u, w_vmem[D_SSD + D_XBC + DT_PAD:, :], NT, preferred_element_type=F32)
        for k in range(D_XBC // cw):
            cols = slice(k * cw, (k + 1) * cw)
            pre_k = lax.dot_general(u, w_vmem[D_SSD + k * cw:D_SSD + (k + 1) * cw, :], NT,
                                    preferred_element_type=F32)
            win[8:8 + tb, cols] = pre_k
            pre_ref[:, cols] = pre_k
            conv, _ = _conv_taps(win, cw_ref[:, cols], tb, cols)
            conv = conv + cb_ref[:, cols]
            sg = _sigmoid(conv)
            xbc_ref[:, cols] = conv * sg
            dsilu_ref[:, cols] = sg * (1.0 + conv * (1.0 - sg))

    row = lambda w: pl.BlockSpec((tb, w), lambda i: (i, 0))
    return pl.pallas_call(
        body, name="proj_conv_fwd", grid=(t // tb,),
        out_shape=(jax.ShapeDtypeStruct((t, D_SSD), F32), jax.ShapeDtypeStruct((t, D_XBC), F32),
                   jax.ShapeDtypeStruct((t, D_XBC), F32), jax.ShapeDtypeStruct((t, D_XBC), F32),
                   jax.ShapeDtypeStruct((t, DT_PAD), F32), jax.ShapeDtypeStruct((t, D_S5), F32)),
        in_specs=[row(D_MODEL), pl.BlockSpec((1, N_MOD, D_MODEL), lambda i: (i // npb, 0, 0)), ANY,
                  pl.BlockSpec((4, D_XBC), lambda i: (0, 0)), pl.BlockSpec((1, D_XBC), lambda i: (0, 0))],
        out_specs=(row(D_SSD), row(D_XBC), row(D_XBC), row(D_XBC), row(DT_PAD), row(D_S5)),
        scratch_shapes=[pltpu.VMEM((D_INP, D_MODEL), BF16), pltpu.VMEM((tb + 8, D_XBC), F32), pltpu.SemaphoreType.DMA],
        compiler_params=_params(56),
    )(x2, mod3, w_in_pad, conv_w, conv_b)


N_PAIRS = N_HEADS // 2


def _split3(x):
    hi = x.astype(BF16)
    r = x - hi.astype(F32)
    mid = r.astype(BF16)
    lo = (r - mid.astype(F32)).astype(BF16)
    return hi, mid, lo


def _dot3(x, e, dims=(((1,), (0,)), ((), ()))):
    return sum(lax.dot_general(p, e, dims, preferred_element_type=F32) for p in _split3(x))


def _dot3_left(e, x, dims=(((1,), (0,)), ((), ()))):
    return sum(lax.dot_general(e, p, dims, preferred_element_type=F32) for p in _split3(x))


def _head_fold():
    return (jnp.arange(D_SSD)[:, None] // HEADDIM == jnp.arange(128)[None, :]).astype(BF16)


def _ssd_prep(dt_raw, par):
    dtb = par[0:1]
    a = -jnp.exp(par[1:2])
    dt = _softplus(dt_raw + dtb)
    adt = dt * a
    row = lax.broadcasted_iota(jnp.int32, (CHUNK, CHUNK), 0)
    col = lax.broadcasted_iota(jnp.int32, (CHUNK, CHUNK), 1)
    causal = row >= col
    tri = causal.astype(BF16)
    cs = _dot3_left(tri, adt)
    left = col < HEADDIM

    def lanes(v, h):
        return jnp.broadcast_to(v[:, h:h + 1], (CHUNK, 128))

    dt_c, cs_c, pair_cols = [], [], []
    for p in range(N_PAIRS):
        c0, c1 = lanes(cs, 2 * p), lanes(cs, 2 * p + 1)
        pair_cols.append(jnp.concatenate([c0, c1], axis=1))
        cs_c.append(jnp.where(left, c0, c1))
        dt_c.append(jnp.where(left, lanes(dt, 2 * p), lanes(dt, 2 * p + 1)))
    cs_c = jnp.concatenate(cs_c, axis=1)
    dt_c = jnp.concatenate(dt_c, axis=1)
    return dt, a, cs, cs.T, causal, tri, dt_c, jnp.exp(cs_c), jnp.exp(cs_c[CHUNK - 1:CHUNK, :] - cs_c), pair_cols


def _pair_decay(cols, cst, pair, causal2):
    rows = jnp.concatenate([jnp.broadcast_to(cst[2 * pair:2 * pair + 1, :], (CHUNK, CHUNK)),
                            jnp.broadcast_to(cst[2 * pair + 1:2 * pair + 2, :], (CHUNK, CHUNK))], axis=1)
    return jnp.exp(jnp.where(causal2, cols - rows, -jnp.inf))


def _stack_heads(xp, left):
    return jnp.concatenate([jnp.where(left, xp, 0.0), jnp.where(left, 0.0, xp)], axis=0).astype(BF16)


def _ssd_fwd(xbc, z, dt_raw, par, dsk, normw, seq):
    t = xbc.shape[0]
    nc = seq // CHUNK
    n_chunks = t // CHUNK

    def body(xbc_ref, z_ref, dt_ref, par_ref, dsk_ref, nw_ref, yraw_ref, ycat_ref, hprev_ref, h_ref):
        @pl.when(pl.program_id(0) % nc == 0)
        def _():
            h_ref[...] = jnp.zeros_like(h_ref)
        hprev_ref[0] = h_ref[...]
        _, _, cs, cst, causal, _, dt_c, ecs_c, w_c, pair_cols = _ssd_prep(dt_ref[...], par_ref[...])
        cs_last = cs[CHUNK - 1:CHUNK, :]
        causal2 = jnp.concatenate([causal, causal], axis=1)
        left = lax.broadcasted_iota(jnp.int32, (CHUNK, 128), 1) < HEADDIM
        x = xbc_ref[:, 0:D_SSD]
        xdt = x * dt_c
        amat = (w_c * xdt).astype(BF16)
        zz = z_ref[...]
        silu_z = zz * _sigmoid(zz)
        for g in range(N_GROUPS):
            gs = slice(g * GW, (g + 1) * GW)
            bg = xbc_ref[:, D_SSD + g * N_STATE:D_SSD + (g + 1) * N_STATE].astype(BF16)
            cg = xbc_ref[:, D_SSD + (N_GROUPS + g) * N_STATE:D_SSD + (N_GROUPS + g + 1) * N_STATE].astype(BF16)
            scores = lax.dot_general(cg, bg, NT, preferred_element_type=F32)
            scores2 = jnp.concatenate([scores, scores], axis=1)
            hg = h_ref[gs, :]
            p_all = lax.dot_general(cg, hg.astype(BF16), NT, preferred_element_type=F32)
            ys = []
            for q in range(GW // 128):
                pair = g * (GW // 128) + q
                decay = _pair_decay(pair_cols[pair], cst, pair, causal2)
                mcat = (scores2 * decay).astype(BF16)
                ys.append(jnp.dot(mcat, _stack_heads(xdt[:, pair * 128:(pair + 1) * 128], left),
                                  preferred_element_type=F32))
            yg = jnp.concatenate(ys, axis=1) + ecs_c[:, gs] * p_all + x[:, gs] * dsk_ref[:, gs]
            s_new = lax.dot_general(amat[:, gs], bg, TN, preferred_element_type=F32)
            for j in range(HPG):
                hh = g * HPG + j
                js = slice(j * HEADDIM, (j + 1) * HEADDIM)
                h_ref[g * GW + j * HEADDIM:g * GW + (j + 1) * HEADDIM, :] = (
                    hg[js, :] * jnp.exp(cs_last[:, hh:hh + 1]) + s_new[js, :])
            yraw_ref[:, gs] = yg
            v = yg * silu_z[:, gs]
            r = lax.rsqrt(jnp.mean(v * v, axis=-1, keepdims=True) + EPS)
            ycat_ref[:, gs] = (v * r * nw_ref[:, gs]).astype(BF16)

    row = lambda w: pl.BlockSpec((CHUNK, w), lambda i: (i, 0))
    full = lambda s: pl.BlockSpec(s, lambda i: (0,) * len(s))
    return pl.pallas_call(
        body, name="ssd_fwd", grid=(n_chunks,),
        out_shape=(jax.ShapeDtypeStruct((t, D_SSD), F32), jax.ShapeDtypeStruct((t, D_SSD + D_S5), BF16),
                   jax.ShapeDtypeStruct((n_chunks, D_SSD, N_STATE), F32)),
        in_specs=[row(D_XBC), row(D_SSD), row(DT_PAD), full((8, 128)), full((1, D_SSD)), full((1, D_SSD))],
        out_specs=(row(D_SSD), row(D_SSD), pl.BlockSpec((1, D_SSD, N_STATE), lambda i: (i, 0, 0))),
        scratch_shapes=[pltpu.VMEM((D_SSD, N_STATE), F32)],
        compiler_params=_params(40),
    )(xbc, z, dt_raw, par, dsk, normw)


S5_CW = 512
S5_BLOCKS = 4


def _tile_scan(in_re, in_im, out_re, out_im, carry_re, carry_im, pw_re, pw_im, n_tiles, reverse):
    steps = (1, 2, 4)
    for cc in range(S5_N // S5_CW):
        cols = slice(cc * S5_CW, (cc + 1) * S5_CW)
        a_re, a_im = pw_re[:, cols], pw_im[:, cols]
        rid = lax.broadcasted_iota(jnp.int32, (8, S5_CW), 0)
        pows = []
        for d in steps:
            k = 8 - d if reverse else d - 1
            keep = (rid < 8 - d) if reverse else (rid >= d)
            pows.append((jnp.where(keep, pw_re[k:k + 1, cols], 0.0), jnp.where(keep, pw_im[k:k + 1, cols], 0.0)))

        def tile(i, carry, cols=cols, pows=pows, a_re=a_re, a_im=a_im):
            r = (n_tiles - 1 - i) if reverse else i
            rows = pl.ds(pl.multiple_of(r * 8, 8), 8)
            xr, xi = in_re[rows, cols], in_im[rows, cols]
            for (pr, pi), d in zip(pows, steps):
                shift = 8 - d if reverse else d
                sr, si = pltpu.roll(xr, shift, axis=0), pltpu.roll(xi, shift, axis=0)
                xr, xi = xr + pr * sr - pi * si, xi + pr * si + pi * sr
            cr, ci = carry
            xr, xi = xr + a_re * cr - a_im * ci, xi + a_re * ci + a_im * cr
            out_re[rows, cols] = xr
            out_im[rows, cols] = xi
            edge = slice(0, 1) if reverse else slice(7, 8)
            return (jnp.broadcast_to(xr[edge], (8, S5_CW)), jnp.broadcast_to(xi[edge], (8, S5_CW)))

        c0 = (jnp.broadcast_to(carry_re[0:1, cols], (8, S5_CW)), jnp.broadcast_to(carry_im[0:1, cols], (8, S5_CW)))
        cr, ci = lax.fori_loop(0, n_tiles, tile, c0, unroll=True)
        carry_re[:, cols] = cr
        carry_im[:, cols] = ci


def _s5_params_math(ar, ai, ldt, br, bi):
    dt = jnp.exp(ldt)
    mag = jnp.exp(ar * dt)
    ang = ai * dt
    ab_re = mag * jnp.cos(ang)
    ab_im = mag * jnp.sin(ang)
    den = ar * ar + ai * ai
    n_re = ab_re - 1.0
    coef_re = (n_re * ar + ab_im * ai) / den
    coef_im = (ab_im * ar - n_re * ai) / den
    bb_re = coef_re * br - coef_im * bi
    bb_im = coef_re * bi + coef_im * br
    return ab_re, ab_im, bb_re, bb_im


def _s5_params_fwd(ar, ai, ldt, br, bi):
    def body(ar_ref, ai_ref, ldt_ref, br_ref, bi_ref, bbr_ref, bbi_ref, pfr_ref, pfi_ref, prr_ref, pri_ref):
        ab_re, ab_im, bb_re, bb_im = _s5_params_math(ar_ref[...], ai_ref[...], ldt_ref[...], br_ref[...], bi_ref[...])
        bbr_ref[...] = bb_re
        bbi_ref[...] = bb_im
        pr, pi = ab_re, ab_im
        for k in range(8):
            pfr_ref[k:k + 1, :] = pr
            pfi_ref[k:k + 1, :] = pi
            prr_ref[7 - k:8 - k, :] = pr
            pri_ref[7 - k:8 - k, :] = -pi
            pr, pi = pr * ab_re - pi * ab_im, pr * ab_im + pi * ab_re

    b16 = jax.ShapeDtypeStruct((S5_CH, S5_N), F32)
    p8 = jax.ShapeDtypeStruct((8, S5_N), F32)
    return pl.pallas_call(body, name="s5_params_fwd", out_shape=(b16, b16, p8, p8, p8, p8),
                          compiler_params=_params(32))(ar, ai, ldt, br, bi)


def _s5_params_bwd(ar, ai, ldt, br, bi, d_ab_re, d_ab_im, d_bb_re, d_bb_im):
    def body(ar_ref, ai_ref, ldt_ref, br_ref, bi_ref, dar_ref, dai_ref, dbr_ref, dbi_ref,
             gar_ref, gai_ref, gldt_ref, gbr_ref, gbi_ref):
        _, vjp = jax.vjp(_s5_params_math, ar_ref[...], ai_ref[...], ldt_ref[...], br_ref[...], bi_ref[...])
        g_ar, g_ai, g_ldt, g_br, g_bi = vjp((dar_ref[...], dai_ref[...], dbr_ref[...], dbi_ref[...]))
        gar_ref[...] = g_ar
        gai_ref[...] = g_ai
        gbr_ref[...] = g_br
        gbi_ref[...] = g_bi
        lane = lax.broadcasted_iota(jnp.int32, (S5_N, 128), 0) // S5_P
        grp = lax.broadcasted_iota(jnp.int32, (S5_N, 128), 1)
        fold = (lane == grp).astype(F32)
        gldt_ref[...] = jnp.dot(g_ldt, fold, preferred_element_type=F32, precision=HIGHEST)

    v1 = jax.ShapeDtypeStruct((1, S5_N), F32)
    b16 = jax.ShapeDtypeStruct((S5_CH, S5_N), F32)
    return pl.pallas_call(body, name="s5_params_bwd",
                          out_shape=(v1, v1, jax.ShapeDtypeStruct((1, 128), F32), b16, b16),
                          compiler_params=_params(32))(ar, ai, ldt, br, bi, d_ab_re, d_ab_im, d_bb_re, d_bb_im)


def _s5_fwd(u5, bb_re, bb_im, cc_re, cc_im, pf_re, pf_im, s5d, w_glu, b_glu, ycat, seq):
    t = u5.shape[0]
    tb = 512
    npb = seq // tb

    def body(u_ref, bbr_ref, bbi_ref, ccr_ref, cci_ref, pfr_ref, pfi_ref, d_ref, wg_ref, bg_ref, ycat_hbm,
             sre_ref, sim_ref, ypre_ref, y5_ref, bur, bui, car, cai):
        del ycat_hbm

        @pl.when(pl.program_id(0) % npb == 0)
        def _():
            car[...] = jnp.zeros_like(car)
            cai[...] = jnp.zeros_like(cai)
        u = u_ref[...]
        ub = u.astype(BF16)
        for j in range(S5_BLOCKS):
            ch, st = slice(j * 128, (j + 1) * 128), slice(j * 512, (j + 1) * 512)
            bur[:, st] = jnp.dot(ub[:, ch], bbr_ref[j], preferred_element_type=F32)
            bui[:, st] = jnp.dot(ub[:, ch], bbi_ref[j], preferred_element_type=F32)
        _tile_scan(bur, bui, sre_ref, sim_ref, car, cai, pfr_ref, pfi_ref, tb // 8, reverse=False)
        cs_y = []
        for j in range(S5_BLOCKS):
            st = slice(j * 512, (j + 1) * 512)
            cs_y.append(_mm(sre_ref[:, st], ccr_ref[j]) - _mm(sim_ref[:, st], cci_ref[j]))
        ypre = jnp.concatenate(cs_y, axis=1) + u * d_ref[...]
        ypre_ref[...] = ypre
        yg = _gelu(ypre)
        y5_ref[...] = (yg * _sigmoid(_mm(yg, wg_ref[...]) + bg_ref[...])).astype(BF16)

    row = lambda w: pl.BlockSpec((tb, w), lambda i: (i, 0))
    full = lambda a: pl.BlockSpec(a.shape, lambda i: (0,) * a.ndim)
    return pl.pallas_call(
        body, name="s5_fwd", grid=(t // tb,),
        out_shape=(jax.ShapeDtypeStruct((t, S5_N), F32), jax.ShapeDtypeStruct((t, S5_N), F32),
                   jax.ShapeDtypeStruct((t, D_S5), F32), jax.ShapeDtypeStruct(ycat.shape, BF16)),
        in_specs=[row(D_S5), full(bb_re), full(bb_im), full(cc_re), full(cc_im), full(pf_re), full(pf_im),
                  full(s5d), full(w_glu), full(b_glu), ANY],
        out_specs=(row(S5_N), row(S5_N), row(D_S5), pl.BlockSpec((tb, D_S5), lambda i: (i, D_SSD // D_S5))),
        input_output_aliases={10: 3},
        scratch_shapes=[pltpu.VMEM((tb, S5_N), F32), pltpu.VMEM((tb, S5_N), F32),
                        pltpu.VMEM((8, S5_N), F32), pltpu.VMEM((8, S5_N), F32)],
        compiler_params=_params(48),
    )(u5, bb_re, bb_im, cc_re, cc_im, pf_re, pf_im, s5d, w_glu, b_glu, ycat)


def _layer_norm(r, g, b):
    mu = jnp.mean(r, axis=-1, keepdims=True)
    xc = r - mu
    rstd = lax.rsqrt(jnp.mean(xc * xc, axis=-1, keepdims=True) + EPS)
    xhat = xc * rstd
    return xhat * g + b, xhat, rstd


def _layer_norm_bwd(dy, xhat, rstd, g):
    dxhat = dy * g
    return rstd * (dxhat - jnp.mean(dxhat, axis=-1, keepdims=True)
                   - xhat * jnp.mean(dxhat * xhat, axis=-1, keepdims=True))


def _out_ln1(ycat, x2, mod3, w_out, ln1, seq):
    t = x2.shape[0]
    tb = 512
    npb = seq // tb

    def body(y_ref, x_ref, mod_ref, w_ref, ln_ref, mix_ref, x1_ref):
        m = mod_ref[0]
        mix = jnp.dot(y_ref[...], w_ref[...], preferred_element_type=F32)
        mix_ref[...] = mix
        r1 = ALPHA * x_ref[...] + (1.0 + m[2:3]) * mix
        x1_ref[...] = _layer_norm(r1, ln_ref[0:1], ln_ref[1:2])[0]

    row = lambda w: pl.BlockSpec((tb, w), lambda i: (i, 0))
    return pl.pallas_call(
        body, name="out_ln1", grid=(t // tb,),
        out_shape=(jax.ShapeDtypeStruct((t, D_MODEL), F32), jax.ShapeDtypeStruct((t, D_MODEL), F32)),
        in_specs=[row(D_SSD + D_S5), row(D_MODEL), pl.BlockSpec((1, N_MOD, D_MODEL), lambda i: (i // npb, 0, 0)),
                  pl.BlockSpec(w_out.shape, lambda i: (0, 0)), pl.BlockSpec(ln1.shape, lambda i: (0, 0))],
        out_specs=(row(D_MODEL), row(D_MODEL)), compiler_params=_params(48),
    )(ycat, x2, mod3, w_out, ln1)


def _mlp_fwd_bwd(x1, tgt, mod3, w1, w2, vec1, b1, seq):
    t = x1.shape[0]
    tb = 256
    npb = seq // tb
    n_fb, _, fb = w1.shape

    def body(x1_ref, tgt_ref, mod_ref, w1_hbm, w2_hbm, v_ref, b1_ref,
             dx1_ref, u2_ref, h_ref, dhp_ref, do_ref, gacc_ref, db1_ref, bacc_ref, w1_v, w2_v, sem1, sem2):
        i = pl.program_id(0)
        @pl.when(i == 0)
        def _():
            cps = [pltpu.make_async_copy(w1_hbm.at[k], w1_v.at[:, k * fb:(k + 1) * fb], sem1.at[k])
                   for k in range(n_fb)]
            for cp in cps:
                cp.start()
            for cp in cps:
                cp.wait()
        _load_once(w2_hbm, w2_v, sem2)

        @pl.when(i == 0)
        def _():
            gacc_ref[...] = jnp.zeros_like(gacc_ref)
            db1_ref[...] = jnp.zeros_like(db1_ref)

        @pl.when(i % npb == 0)
        def _():
            bacc_ref[...] = jnp.zeros_like(bacc_ref)

        m = mod_ref[0]
        sh2, sc2, g2 = m[3:4], m[4:5], m[5:6]
        x1v = x1_ref[...]
        u2 = (x1v * (1.0 + sc2) + sh2).astype(BF16)
        u2_ref[...] = u2
        hr = jnp.maximum(jnp.dot(u2, w1_v[...], preferred_element_type=F32) + b1_ref[...], 0.0)
        hb = (hr * hr).astype(BF16)
        h_ref[...] = hb
        o = jnp.dot(hb, w2_v[...], preferred_element_type=F32) + v_ref[0:1]
        r2 = ALPHA * x1v + (1.0 + g2) * o
        y, xhat, rstd = _layer_norm(r2, v_ref[1:2], v_ref[2:3])
        err = y - tgt_ref[...]
        dy = err * (1.0 / D_MODEL)
        dr2 = _layer_norm_bwd(dy, xhat, rstd, v_ref[1:2])
        do = (1.0 + g2) * dr2
        dob = do.astype(BF16)
        do_ref[...] = dob
        gacc_ref[0:1, :] += jnp.sum(dy * xhat, axis=0, keepdims=True)
        gacc_ref[1:2, :] += jnp.sum(dy, axis=0, keepdims=True)
        gacc_ref[2:3, :] += jnp.sum(do, axis=0, keepdims=True)
        gacc_ref[3:4, :] += jnp.sum(err * err, axis=0, keepdims=True)
        dhpre = lax.dot_general(dob, w2_v[...], NT, preferred_element_type=F32) * (2.0 * hr)
        dhpb = dhpre.astype(BF16)
        dhp_ref[...] = dhpb
        db1_ref[...] += jnp.sum(dhpre, axis=0, keepdims=True)
        du2 = lax.dot_general(dhpb, w1_v[...], NT, preferred_element_type=F32)
        dx1_ref[...] = ALPHA * dr2 + du2 * (1.0 + sc2)
        bacc_ref[0, 0:1, :] += jnp.sum(du2, axis=0, keepdims=True)
        bacc_ref[0, 1:2, :] += jnp.sum(du2 * x1v, axis=0, keepdims=True)
        bacc_ref[0, 2:3, :] += jnp.sum(dr2 * o, axis=0, keepdims=True)

    row = lambda w: pl.BlockSpec((tb, w), lambda i: (i, 0))
    return pl.pallas_call(
        body, name="mlp_fwd_bwd", grid=(t // tb,),
        out_shape=(jax.ShapeDtypeStruct((t, D_MODEL), F32), jax.ShapeDtypeStruct((t, D_MODEL), BF16),
                   jax.ShapeDtypeStruct((t, D_FF), BF16), jax.ShapeDtypeStruct((t, D_FF), BF16),
                   jax.ShapeDtypeStruct((t, D_MODEL), BF16), jax.ShapeDtypeStruct((8, D_MODEL), F32),
                   jax.ShapeDtypeStruct((1, D_FF), F32), jax.ShapeDtypeStruct((t // seq, 8, D_MODEL), F32)),
        in_specs=[row(D_MODEL), row(D_MODEL), pl.BlockSpec((1, N_MOD, D_MODEL), lambda i: (i // npb, 0, 0)), ANY, ANY,
                  pl.BlockSpec(vec1.shape, lambda i: (0, 0)), pl.BlockSpec(b1.shape, lambda i: (0, 0))],
        out_specs=(row(D_MODEL), row(D_MODEL), row(D_FF), row(D_FF), row(D_MODEL),
                   pl.BlockSpec((8, D_MODEL), lambda i: (0, 0)), pl.BlockSpec((1, D_FF), lambda i: (0, 0)),
                   pl.BlockSpec((1, 8, D_MODEL), lambda i: (i // npb, 0, 0))),
        scratch_shapes=[pltpu.VMEM((D_MODEL, n_fb * fb), BF16), pltpu.VMEM((D_FF, D_MODEL), BF16),
                        pltpu.SemaphoreType.DMA((n_fb,)), pltpu.SemaphoreType.DMA],
        compiler_params=_params(60),
    )(x1, tgt, mod3, w1, w2, vec1, b1)


def _ln1_out_bwd(dx1, x2, mix, mod3, w_out, ln1, seq):
    t = x2.shape[0]
    tb = 512
    npb = seq // tb

    def body(dx1_ref, x_ref, mix_ref, mod_ref, w_ref, ln_ref, dmix_ref, dxa_ref, dys_ref, dy5_ref, gacc_ref, bacc_ref):
        i = pl.program_id(0)

        @pl.when(i == 0)
        def _():
            gacc_ref[...] = jnp.zeros_like(gacc_ref)

        @pl.when(i % npb == 0)
        def _():
            bacc_ref[...] = jnp.zeros_like(bacc_ref)

        m = mod_ref[0]
        mix = mix_ref[...]
        r1 = ALPHA * x_ref[...] + (1.0 + m[2:3]) * mix
        _, xhat, rstd = _layer_norm(r1, ln_ref[0:1], ln_ref[1:2])
        dx1v = dx1_ref[...]
        dr1 = _layer_norm_bwd(dx1v, xhat, rstd, ln_ref[0:1])
        gacc_ref[0:1, :] += jnp.sum(dx1v * xhat, axis=0, keepdims=True)
        gacc_ref[1:2, :] += jnp.sum(dx1v, axis=0, keepdims=True)
        bacc_ref[0, 0:1, :] += jnp.sum(dr1 * mix, axis=0, keepdims=True)
        dmix = ((1.0 + m[2:3]) * dr1).astype(BF16)
        dmix_ref[...] = dmix
        dxa_ref[...] = ALPHA * dr1
        dys_ref[...] = lax.dot_general(dmix, w_ref[0:D_SSD, :], NT, preferred_element_type=F32)
        dy5_ref[...] = lax.dot_general(dmix, w_ref[D_SSD:, :], NT, preferred_element_type=F32)

    row = lambda w: pl.BlockSpec((tb, w), lambda i: (i, 0))
    return pl.pallas_call(
        body, name="ln1_out_bwd", grid=(t // tb,),
        out_shape=(jax.ShapeDtypeStruct((t, D_MODEL), BF16), jax.ShapeDtypeStruct((t, D_MODEL), F32),
                   jax.ShapeDtypeStruct((t, D_SSD), F32), jax.ShapeDtypeStruct((t, D_S5), F32),
                   jax.ShapeDtypeStruct((8, D_MODEL), F32), jax.ShapeDtypeStruct((t // seq, 8, D_MODEL), F32)),
        in_specs=[row(D_MODEL), row(D_MODEL), row(D_MODEL), pl.BlockSpec((1, N_MOD, D_MODEL), lambda i: (i // npb, 0, 0)),
                  pl.BlockSpec(w_out.shape, lambda i: (0, 0)), pl.BlockSpec(ln1.shape, lambda i: (0, 0))],
        out_specs=(row(D_MODEL), row(D_MODEL), row(D_SSD), row(D_S5), pl.BlockSpec((8, D_MODEL), lambda i: (0, 0)),
                   pl.BlockSpec((1, 8, D_MODEL), lambda i: (i // npb, 0, 0))),
        compiler_params=_params(48),
    )(dx1, x2, mix, mod3, w_out, ln1)


def _s5_bwd(dy5, ypre, u5, s_re, s_im, bb_re, bb_im, cc_re, cc_im, pr_re, pr_im, s5d, w_glu, b_glu, seq):
    t = u5.shape[0]
    tb = 256
    npb = seq // tb
    n_blocks = t // tb

    def blk(i):
        return (i // npb) * npb + (npb - 1 - i % npb)

    def body(dy_ref, ypre_ref, u_ref, sre_ref, sim_ref, hre_ref, him_ref, bbr_ref, bbi_ref, ccr_ref, cci_ref,
             prr_ref, pri_ref, d_ref, wg_ref, bg_ref,
             du_ref, vacc_ref, sacc_ref, dcc_ref, dbb_ref, dwg_ref, dsr, dsi, gr, gi, car, cai):
        i = pl.program_id(0)

        @pl.when(i == 0)
        def _():
            for acc in (vacc_ref, sacc_ref, dcc_ref, dbb_ref, dwg_ref):
                acc[...] = jnp.zeros_like(acc)

        @pl.when(i % npb == 0)
        def _():
            car[...] = jnp.zeros_like(car)
            cai[...] = jnp.zeros_like(cai)

        dy = dy_ref[...]
        ypre = ypre_ref[...]
        u = u_ref[...]
        ub = u.astype(BF16)
        yg = _gelu(ypre)
        sg = _sigmoid(_mm(yg, wg_ref[...]) + bg_ref[...])
        dq = dy * yg * sg * (1.0 - sg)
        dqb = dq.astype(BF16)
        dyg = dy * sg + lax.dot_general(dqb, wg_ref[...], NT, preferred_element_type=F32)
        dyp = dyg * _gelu_grad(ypre)
        dypb = dyp.astype(BF16)
        dwg_ref[...] += lax.dot_general(yg.astype(BF16), dqb, TN, preferred_element_type=F32)
        blocks = [(slice(j * 128, (j + 1) * 128), slice(j * 512, (j + 1) * 512)) for j in range(S5_BLOCKS)]
        for j, (ch, st) in enumerate(blocks):
            dsr[:, st] = lax.dot_general(dypb[:, ch], ccr_ref[j], NT, preferred_element_type=F32)
            dsi[:, st] = -lax.dot_general(dypb[:, ch], cci_ref[j], NT, preferred_element_type=F32)
        _tile_scan(dsr, dsi, gr, gi, car, cai, prr_ref, pri_ref, tb // 8, reverse=True)
        g_re, g_im = gr[...], gi[...]
        first_rows = (i % npb) == npb - 1
        hre = jnp.where(first_rows, 0.0, hre_ref[...])
        him = jnp.where(first_rows, 0.0, him_ref[...])
        s_re_v, s_im_v = sre_ref[...], sim_ref[...]
        sp_re = pltpu.roll(jnp.concatenate([hre, s_re_v], axis=0), 1, axis=0)[8:8 + tb]
        sp_im = pltpu.roll(jnp.concatenate([him, s_im_v], axis=0), 1, axis=0)[8:8 + tb]
        vacc_ref[0:1, :] += jnp.sum(g_re * sp_re + g_im * sp_im, axis=0, keepdims=True)
        vacc_ref[1:2, :] += jnp.sum(g_im * sp_re - g_re * sp_im, axis=0, keepdims=True)
        grb, gib = g_re.astype(BF16), g_im.astype(BF16)
        srb, sib = s_re_v.astype(BF16), s_im_v.astype(BF16)
        du_cols = []
        for j, (ch, st) in enumerate(blocks):
            dcc_ref[j] += lax.dot_general(srb[:, st], dypb[:, ch], TN, preferred_element_type=F32)
            dcc_ref[S5_BLOCKS + j] -= lax.dot_general(sib[:, st], dypb[:, ch], TN, preferred_element_type=F32)
            dbb_ref[j] += lax.dot_general(ub[:, ch], grb[:, st], TN, preferred_element_type=F32)
            dbb_ref[S5_BLOCKS + j] += lax.dot_general(ub[:, ch], gib[:, st], TN, preferred_element_type=F32)
            du_cols.append(lax.dot_general(grb[:, st], bbr_ref[j], NT, preferred_element_type=F32)
                           + lax.dot_general(gib[:, st], bbi_ref[j], NT, preferred_element_type=F32))
        du_ref[...] = jnp.concatenate(du_cols, axis=1) + dyp * d_ref[...]
        sacc_ref[0:1, :] += jnp.sum(dyp * u, axis=0, keepdims=True)
        sacc_ref[1:2, :] += jnp.sum(dq, axis=0, keepdims=True)

    row = lambda w: pl.BlockSpec((tb, w), lambda i: (blk(i), 0))
    halo = pl.BlockSpec((8, S5_N), lambda i: (jnp.maximum(blk(i) * (tb // 8) - 1, 0), 0))
    full = lambda a: pl.BlockSpec(a.shape, lambda i: (0,) * a.ndim)
    acc = lambda s: pl.BlockSpec(s, lambda i: (0,) * len(s))
    acc_shapes = [(8, S5_N), (8, D_S5), (2 * S5_BLOCKS, 512, 128), (2 * S5_BLOCKS, 128, 512), (D_S5, D_S5)]
    return pl.pallas_call(
        body, name="s5_bwd", grid=(n_blocks,),
        out_shape=(jax.ShapeDtypeStruct((t, D_S5), F32),) + tuple(jax.ShapeDtypeStruct(s, F32) for s in acc_shapes),
        in_specs=[row(D_S5), row(D_S5), row(D_S5), row(S5_N), row(S5_N), halo, halo, full(bb_re), full(bb_im),
                  full(cc_re), full(cc_im), full(pr_re), full(pr_im), full(s5d), full(w_glu), full(b_glu)],
        out_specs=(row(D_S5),) + tuple(acc(s) for s in acc_shapes),
        scratch_shapes=[pltpu.VMEM((tb, S5_N), F32), pltpu.VMEM((tb, S5_N), F32), pltpu.VMEM((tb, S5_N), F32),
                        pltpu.VMEM((tb, S5_N), F32), pltpu.VMEM((8, S5_N), F32), pltpu.VMEM((8, S5_N), F32)],
        compiler_params=_params(56),
    )(dy5, ypre, u5, s_re, s_im, s_re, s_im, bb_re, bb_im, cc_re, cc_im, pr_re, pr_im, s5d, w_glu, b_glu)


def _ssd_bwd(dyssd, yraw, z, xbc, dt_raw, hprev, par, dsk, normw, seq):
    t = xbc.shape[0]
    nc = seq // CHUNK
    n_chunks = t // CHUNK
    fold = _head_fold()

    def blk(i):
        return (i // nc) * nc + (nc - 1 - i % nc)

    def body(dy_ref, yraw_ref, z_ref, xbc_ref, dt_ref, hprev_ref, par_ref, dsk_ref, nw_ref, fold_ref,
             dxbc_ref, dz_ref, ddt_ref, dpar_ref, cacc_ref, dh_ref, dyr_ref):
        i = pl.program_id(0)

        @pl.when(i == 0)
        def _():
            dpar_ref[...] = jnp.zeros_like(dpar_ref)
            cacc_ref[...] = jnp.zeros_like(cacc_ref)

        @pl.when(i % nc == 0)
        def _():
            dh_ref[...] = jnp.zeros_like(dh_ref)

        zz = z_ref[...]
        sz = _sigmoid(zz)
        silu_z = zz * sz
        yraw = yraw_ref[...]
        for g in range(N_GROUPS):
            sl = slice(g * GW, (g + 1) * GW)
            v = yraw[:, sl] * silu_z[:, sl]
            r = lax.rsqrt(jnp.mean(v * v, axis=-1, keepdims=True) + EPS)
            dyg = dy_ref[:, sl]
            cacc_ref[1:2, sl] += jnp.sum(dyg * v * r, axis=0, keepdims=True)
            dyw = dyg * nw_ref[:, sl]
            dv = r * dyw - v * (r * r * r) * jnp.mean(dyw * v, axis=-1, keepdims=True)
            dyr_ref[:, sl] = dv * silu_z[:, sl]
            dz_ref[:, sl] = dv * yraw[:, sl] * (sz[:, sl] * (1.0 + zz[:, sl] * (1.0 - sz[:, sl])))

        dt, a, cs, cst, causal, tri, dt_c, ecs_c, w_c, pair_cols = _ssd_prep(dt_ref[...], par_ref[...])
        cs_last = cs[CHUNK - 1:CHUNK, :]
        causal2 = jnp.concatenate([causal, causal], axis=1)
        lane = lax.broadcasted_iota(jnp.int32, (CHUNK, 128), 1)
        left = lane < HEADDIM
        lane1 = lax.broadcasted_iota(jnp.int32, (1, 128), 1)
        x = xbc_ref[:, 0:D_SSD]
        xdt = x * dt_c
        dyr = dyr_ref[...]
        dyrb = dyr.astype(BF16)
        cacc_ref[0:1, :] += jnp.sum(dyr * x, axis=0, keepdims=True)
        dlast = jnp.zeros((1, 128), F32)
        dxdt_cols, diag_all, dww_cols = [], [], []
        for g in range(N_GROUPS):
            gs = slice(g * GW, (g + 1) * GW)
            b_sl = slice(D_SSD + g * N_STATE, D_SSD + (g + 1) * N_STATE)
            c_sl = slice(D_SSD + (N_GROUPS + g) * N_STATE, D_SSD + (N_GROUPS + g + 1) * N_STATE)
            bg = xbc_ref[:, b_sl].astype(BF16)
            cg = xbc_ref[:, c_sl].astype(BF16)
            scores = lax.dot_general(cg, bg, NT, preferred_element_type=F32)
            scores2 = jnp.concatenate([scores, scores], axis=1)
            hg = hprev_ref[0, gs, :]
            hgb = hg.astype(BF16)
            dhg = dh_ref[gs, :]
            dhgb = dhg.astype(BF16)
            q_all = lax.dot_general(bg, dhgb, NT, preferred_element_type=F32)
            dscores = jnp.zeros((CHUNK, CHUNK), F32)
            diag_cols = []
            for q in range(GW // 128):
                pair = g * (GW // 128) + q
                ps = slice(pair * 128, (pair + 1) * 128)
                decay = _pair_decay(pair_cols[pair], cst, pair, causal2)
                mcat = (scores2 * decay).astype(BF16)
                dyp = dyrb[:, ps]
                dm = lax.dot_general(dyp, _stack_heads(xdt[:, ps], left), NT, preferred_element_type=F32)
                dmd = dm * decay
                dscores = dscores + dmd[:, 0:CHUNK] + dmd[:, CHUNK:]
                rr = lax.dot_general(mcat, dyp, TN, preferred_element_type=F32)
                diag_cols.append(jnp.where(left, rr[0:CHUNK], rr[CHUNK:]))
            wq = w_c[:, gs] * q_all
            diag_g = jnp.concatenate(diag_cols, axis=1)
            diag_all.append(diag_g)
            dxdt_cols.append(diag_g + wq)
            dww_cols.append(wq * xdt[:, gs])
            dp = (ecs_c[:, gs] * dyr[:, gs]).astype(BF16)
            amat = (w_c[:, gs] * xdt[:, gs]).astype(BF16)
            dsb = dscores.astype(BF16)
            dxbc_ref[:, c_sl] = (jnp.dot(dsb, bg, preferred_element_type=F32)
                                 + jnp.dot(dp, hgb, preferred_element_type=F32))
            dxbc_ref[:, b_sl] = (lax.dot_general(dsb, cg, TN, preferred_element_type=F32)
                                 + jnp.dot(amat, dhgb, preferred_element_type=F32))
            dh_in = lax.dot_general(dp, cg, TN, preferred_element_type=F32)
            for j in range(HPG):
                hh = g * HPG + j
                js = slice(j * HEADDIM, (j + 1) * HEADDIM)
                ecl = jnp.exp(cs_last[:, hh:hh + 1])
                dlast = dlast + jnp.where(lane1 == hh, ecl * jnp.sum(dhg[js, :] * hg[js, :]), 0.0)
                dh_ref[g * GW + j * HEADDIM:g * GW + (j + 1) * HEADDIM, :] = ecl * dhg[js, :] + dh_in[js, :]
        dxdt = jnp.concatenate(dxdt_cols, axis=1)
        dxbc_ref[:, 0:D_SSD] = dxdt * dt_c + dyr * dsk_ref[...]
        dww = _mm(jnp.concatenate(dww_cols, axis=1), fold_ref[...])
        dcs = _dot3(dyrb.astype(F32) * (yraw - x * dsk_ref[...])
                    - xdt.astype(BF16).astype(F32) * jnp.concatenate(diag_all, axis=1), fold_ref[...]) - dww
        rowid = lax.broadcasted_iota(jnp.int32, (CHUNK, 128), 0)
        dcs = dcs + jnp.where(rowid == CHUNK - 1, jnp.sum(dww, axis=0, keepdims=True) + dlast, 0.0)
        dadt = _dot3_left(tri, dcs, TN)
        ddt = _mm(dxdt * x, fold_ref[...]) + dadt * a
        da = jnp.sum(dadt * dt, axis=0, keepdims=True)
        ddt_raw = ddt * _sigmoid(dt_ref[...] + par_ref[0:1])
        ddt_raw = jnp.where(lane < N_HEADS, ddt_raw, 0.0)
        ddt_ref[...] = ddt_raw
        dpar_ref[0:1, :] += jnp.sum(ddt_raw, axis=0, keepdims=True)
        dpar_ref[1:2, :] += jnp.where(lane1 < N_HEADS, da * a, 0.0)

    row = lambda w: pl.BlockSpec((CHUNK, w), lambda i: (blk(i), 0))
    full = lambda s: pl.BlockSpec(s, lambda i: (0,) * len(s))
    return pl.pallas_call(
        body, name="ssd_bwd", grid=(n_chunks,),
        out_shape=(jax.ShapeDtypeStruct((t, D_XBC), F32), jax.ShapeDtypeStruct((t, D_SSD), F32),
                   jax.ShapeDtypeStruct((t, DT_PAD), F32), jax.ShapeDtypeStruct((8, 128), F32),
                   jax.ShapeDtypeStruct((8, D_SSD), F32)),
        in_specs=[row(D_SSD), row(D_SSD), row(D_SSD), row(D_XBC), row(DT_PAD),
                  pl.BlockSpec((1, D_SSD, N_STATE), lambda i: (blk(i), 0, 0)),
                  full((8, 128)), full((1, D_SSD)), full((1, D_SSD)), full(fold.shape)],
        out_specs=(row(D_XBC), row(D_SSD), row(DT_PAD), full((8, 128)), full((8, D_SSD))),
        scratch_shapes=[pltpu.VMEM((D_SSD, N_STATE), F32), pltpu.VMEM((CHUNK, D_SSD), F32)],
        compiler_params=_params(48),
    )(dyssd, yraw, z, xbc, dt_raw, hprev, par, dsk, normw, fold)


def _conv_proj_bwd(dz, dxbc, dsilu, xbc_pre, ddt, du5, x2, dxa, mod3, conv_w, w_in_pad, seq):
    t = x2.shape[0]
    tb = 256
    npb = seq // tb
    n_blocks = t // tb
    cw = 512

    def blk(i):
        return (i // npb) * npb + (npb - 1 - i % npb)

    def body(dz_ref, d_ref, ds_ref, cur_ref, halo_ref, ddt_ref, du5_ref, x_ref, dxa_ref, mod_ref, cw_ref, w_hbm,
             gx_ref, u_ref, dxp_ref, bacc_ref, acc_ref, w_vmem, win_x, win_d, sem):
        i = pl.program_id(0)
        _load_once(w_hbm, w_vmem, sem)

        @pl.when(i == 0)
        def _():
            acc_ref[...] = jnp.zeros_like(acc_ref)

        @pl.when(i % npb == 0)
        def _():
            bacc_ref[...] = jnp.zeros_like(bacc_ref)
            win_d[tb:tb + 8, :] = jnp.zeros((8, D_XBC), F32)

        @pl.when(i % npb != 0)
        def _():
            win_d[tb:tb + 8, :] = win_d[0:8, :]

        first_rows = (i % npb) == npb - 1
        win_x[0:8, :] = jnp.where(first_rows, 0.0, halo_ref[...])
        win_x[8:8 + tb, :] = cur_ref[...]
        w = cw_ref[...]
        for k in range(D_XBC // cw):
            cols = slice(k * cw, (k + 1) * cw)
            dpre = d_ref[:, cols] * ds_ref[:, cols]
            win_d[0:tb, cols] = dpre
            for j in range(4):
                acc_ref[3 - j:4 - j, cols] += jnp.sum(dpre * win_x[8 - j:8 - j + tb, cols], axis=0, keepdims=True)
            acc_ref[4:5, cols] += jnp.sum(dpre, axis=0, keepdims=True)
            dxp = w[3:4, cols] * dpre
            for j in (1, 2, 3):
                dxp = dxp + w[3 - j:4 - j, cols] * win_d[j:j + tb, cols]
            dxp_ref[:, cols] = dxp.astype(BF16)
        o1, o2, o3 = D_SSD, D_SSD + D_XBC, D_SSD + D_XBC + DT_PAD
        du = (jnp.dot(dz_ref[...].astype(BF16), w_vmem[0:o1, :], preferred_element_type=F32)
              + jnp.dot(dxp_ref[...], w_vmem[o1:o2, :], preferred_element_type=F32)
              + jnp.dot(ddt_ref[...].astype(BF16), w_vmem[o2:o3, :], preferred_element_type=F32)
              + jnp.dot(du5_ref[...].astype(BF16), w_vmem[o3:, :], preferred_element_type=F32))
        m = mod_ref[0]
        xv = x_ref[...]
        u_ref[...] = (xv * (1.0 + m[1:2]) + m[0:1]).astype(BF16)
        gx_ref[...] = dxa_ref[...] + du * (1.0 + m[1:2])
        bacc_ref[0, 0:1, :] += jnp.sum(du, axis=0, keepdims=True)
        bacc_ref[0, 1:2, :] += jnp.sum(du * xv, axis=0, keepdims=True)

    row = lambda w: pl.BlockSpec((tb, w), lambda i: (blk(i), 0))
    halo = pl.BlockSpec((8, D_XBC), lambda i: (jnp.maximum(blk(i) * (tb // 8) - 1, 0), 0))
    return pl.pallas_call(
        body, name="conv_proj_bwd", grid=(n_blocks,),
        out_shape=(jax.ShapeDtypeStruct((t, D_MODEL), F32), jax.ShapeDtypeStruct((t, D_MODEL), BF16),
                   jax.ShapeDtypeStruct((t, D_XBC), BF16), jax.ShapeDtypeStruct((t // seq, 8, D_MODEL), F32),
                   jax.ShapeDtypeStruct((8, D_XBC), F32)),
        in_specs=[row(D_SSD), row(D_XBC), row(D_XBC), row(D_XBC), halo, row(DT_PAD), row(D_S5), row(D_MODEL),
                  row(D_MODEL), pl.BlockSpec((1, N_MOD, D_MODEL), lambda i: (i // npb, 0, 0)),
                  pl.BlockSpec((4, D_XBC), lambda i: (0, 0)), ANY],
        out_specs=(row(D_MODEL), row(D_MODEL), row(D_XBC), pl.BlockSpec((1, 8, D_MODEL), lambda i: (i // npb, 0, 0)),
                   pl.BlockSpec((8, D_XBC), lambda i: (0, 0))),
        scratch_shapes=[pltpu.VMEM((D_INP, D_MODEL), BF16), pltpu.VMEM((tb + 8, D_XBC), F32),
                        pltpu.VMEM((tb + 8, D_XBC), F32), pltpu.SemaphoreType.DMA],
        compiler_params=_params(60),
    )(dz, dxbc, dsilu, xbc_pre, xbc_pre, ddt, du5, x2, dxa, mod3, conv_w, w_in_pad)


def _pad_rows(a, mult):
    r = a.shape[0]
    pad = (-r) % mult
    return a if pad == 0 else jnp.concatenate([a, jnp.zeros((pad,) + a.shape[1:], a.dtype)], axis=0)


_SMALL = ["conv_w", "conv_b", "dt_bias", "a_log", "d_ssd", "norm_w", "s5_a_re", "s5_a_im", "s5_log_dt", "s5_b_re",
          "s5_b_im", "s5_c_re", "s5_c_im", "s5_d", "b_glu", "ln1_g", "ln1_b", "b1", "b2", "ln2_g", "ln2_b",
          "loss_lanes"]
_NOT_UPDATED = {"conv_w": (1, 4, D_XBC), "loss_lanes": (1, D_MODEL)}


def _tile_rows(size):
    return 8 * (-(-size // 1024))


def _pack_small(d):
    parts = []
    for n in _SMALL:
        flat = d[n].reshape(-1).astype(F32)
        rows = _tile_rows(flat.shape[0])
        pad = rows * 128 - flat.shape[0]
        if pad:
            flat = jnp.concatenate([flat, jnp.zeros((pad,), F32)])
        parts.append(flat.reshape(rows, 128))
    return _pad_rows(jnp.concatenate(parts, axis=0), 256)


def _unpack_small(p, shapes):
    out, off = {}, 0
    for n in _SMALL:
        size = math.prod(shapes[n])
        rows = _tile_rows(size)
        out[n] = p[off:off + rows].reshape(-1)[:size].reshape(shapes[n])
        off += rows
    return out


def kernel(x, c, w_ada, b_ada, w_in, conv_w, conv_b, dt_bias, a_log, d_ssd, norm_w, s5_a_re, s5_a_im, s5_log_dt, s5_b_re, s5_b_im, s5_c_re, s5_c_im, s5_d, w_glu, b_glu, w_out, ln1_g, ln1_b, w1, b1, w2, b2, ln2_g, ln2_b, loss_target, m_w_ada, m_b_ada, m_w_in, m_conv_w, m_conv_b, m_dt_bias, m_a_log, m_d_ssd, m_norm_w, m_s5_a_re, m_s5_a_im, m_s5_log_dt, m_s5_b_re, m_s5_b_im, m_s5_c_re, m_s5_c_im, m_s5_d, m_w_glu, m_b_glu, m_w_out, m_ln1_g, m_ln1_b, m_w1, m_b1, m_w2, m_b2, m_ln2_g, m_ln2_b, v_w_ada, v_b_ada, v_w_in, v_conv_w, v_conv_b, v_dt_bias, v_a_log, v_d_ssd, v_norm_w, v_s5_a_re, v_s5_a_im, v_s5_log_dt, v_s5_b_re, v_s5_b_im, v_s5_c_re, v_s5_c_im, v_s5_d, v_w_glu, v_b_glu, v_w_out, v_ln1_g, v_ln1_b, v_w1, v_b1, v_w2, v_b2, v_ln2_g, v_ln2_b):
    weights = dict(w_ada=w_ada, b_ada=b_ada, w_in=w_in, conv_w=conv_w, conv_b=conv_b, dt_bias=dt_bias, a_log=a_log,
                   d_ssd=d_ssd, norm_w=norm_w, s5_a_re=s5_a_re, s5_a_im=s5_a_im, s5_log_dt=s5_log_dt, s5_b_re=s5_b_re,
                   s5_b_im=s5_b_im, s5_c_re=s5_c_re, s5_c_im=s5_c_im, s5_d=s5_d, w_glu=w_glu, b_glu=b_glu, w_out=w_out,
                   ln1_g=ln1_g, ln1_b=ln1_b, w1=w1, b1=b1, w2=w2, b2=b2, ln2_g=ln2_g, ln2_b=ln2_b)
    mom = dict(w_ada=m_w_ada, b_ada=m_b_ada, w_in=m_w_in, conv_w=m_conv_w, conv_b=m_conv_b, dt_bias=m_dt_bias,
               a_log=m_a_log, d_ssd=m_d_ssd, norm_w=m_norm_w, s5_a_re=m_s5_a_re, s5_a_im=m_s5_a_im,
               s5_log_dt=m_s5_log_dt, s5_b_re=m_s5_b_re, s5_b_im=m_s5_b_im, s5_c_re=m_s5_c_re, s5_c_im=m_s5_c_im,
               s5_d=m_s5_d, w_glu=m_w_glu, b_glu=m_b_glu, w_out=m_w_out, ln1_g=m_ln1_g, ln1_b=m_ln1_b, w1=m_w1, b1=m_b1,
               w2=m_w2, b2=m_b2, ln2_g=m_ln2_g, ln2_b=m_ln2_b)
    var = dict(w_ada=v_w_ada, b_ada=v_b_ada, w_in=v_w_in, conv_w=v_conv_w, conv_b=v_conv_b, dt_bias=v_dt_bias,
               a_log=v_a_log, d_ssd=v_d_ssd, norm_w=v_norm_w, s5_a_re=v_s5_a_re, s5_a_im=v_s5_a_im,
               s5_log_dt=v_s5_log_dt, s5_b_re=v_s5_b_re, s5_b_im=v_s5_b_im, s5_c_re=v_s5_c_re, s5_c_im=v_s5_c_im,
               s5_d=v_s5_d, w_glu=v_w_glu, b_glu=v_b_glu, w_out=v_w_out, ln1_g=v_ln1_g, ln1_b=v_ln1_b, w1=v_w1, b1=v_b1,
               w2=v_w2, b2=v_b2, ln2_g=v_ln2_g, ln2_b=v_ln2_b)
    names = list(weights)
    shapes = {n: weights[n].shape for n in names}

    nb, seq, _ = x.shape
    t = nb * seq
    dev = _dev_index()
    x2 = x.reshape(t, D_MODEL)
    tgt2 = loss_target.reshape(t, D_MODEL)

    cw_cols = conv_w.shape[2]
    small_in = jnp.concatenate([c.reshape(-1), conv_w.reshape(-1)]).reshape(-1, 128)
    big_names = ["w_in", "w_out", "w1", "w2", "w_glu"]
    local = {n: (a[0].T if n == "w_in" else a[0]) for n, a in weights.items() if n in big_names}
    shard_bf16 = {n: local[n].astype(BF16) for n in big_names}
    first = _all_gather([small_in, shard_bf16["w_in"], shard_bf16["w_glu"]], "gather_first")
    small_all = first[0].reshape(N_DEV, -1)
    c_all = small_all[:, :nb * D_MODEL].reshape(N_DEV * nb, D_MODEL)
    conv_w_full = small_all[:, nb * D_MODEL:].reshape(N_DEV, 4, cw_cols).transpose(1, 0, 2).reshape(4, D_XBC)

    w_in_t = first[1].reshape(D_IN, D_MODEL)
    w_in_pad = jnp.concatenate(
        [w_in_t[:D_SSD + D_XBC + N_HEADS], jnp.zeros((DT_PAD - N_HEADS, D_MODEL), BF16),
         w_in_t[D_SSD + D_XBC + N_HEADS:]], axis=0)
    w_glu_f = first[2].reshape(D_S5, D_S5)
    late_names = ["w_out", "w1", "w2"]

    ada_cols = w_ada.shape[2]
    b_cols = lax.dynamic_slice_in_dim(b_ada, dev * ada_cols, ada_cols, axis=1)
    mod_cols = _mod_fwd(c_all, w_ada[0], b_cols)
    mod_all = _all_gather([mod_cols], "gather_mod")[0]
    mod_mine = lax.dynamic_slice_in_dim(mod_all, dev * nb, nb, axis=1)
    mod3 = mod_mine.transpose(1, 0, 2).reshape(nb, N_MOD, D_MODEL)

    def pad_lanes(v, n):
        return jnp.concatenate([v, jnp.zeros((v.shape[0], n - v.shape[1]), F32)], axis=1)

    par = _pad_rows(jnp.concatenate([pad_lanes(dt_bias, 128), pad_lanes(a_log, 128)], axis=0), 8)
    dsk = jnp.repeat(d_ssd[0], HEADDIM).reshape(1, D_SSD)
    ar = s5_a_re.reshape(1, S5_N)
    ai = s5_a_im.reshape(1, S5_N)
    ldt = jnp.repeat(s5_log_dt[0], S5_P).reshape(1, S5_N)
    br_t = s5_b_re[0].transpose(2, 0, 1).reshape(S5_CH, S5_N)
    bi_t = s5_b_im[0].transpose(2, 0, 1).reshape(S5_CH, S5_N)
    bb_re_t, bb_im_t, pf_re, pf_im, pr_re, pr_im = _s5_params_fwd(ar, ai, ldt, br_t, bi_t)
    gpb = S5_GROUPS // S5_BLOCKS
    mask_b = (jnp.arange(128)[:, None] // S5_CH) == (jnp.arange(512)[None, :] // S5_P)

    def dense_b(bt_):
        blocks = bt_.reshape(S5_CH, S5_BLOCKS, 512).transpose(1, 0, 2)
        return jnp.where(mask_b, jnp.tile(blocks, (1, gpb, 1)), 0.0).astype(BF16)

    def dense_c(cc):
        blocks = cc[0].transpose(0, 2, 1).reshape(S5_BLOCKS, 512, S5_CH)
        return jnp.where(mask_b.T, jnp.tile(blocks, (1, 1, gpb)), 0.0).astype(BF16)

    bb_re, bb_im = dense_b(bb_re_t), dense_b(bb_im_t)
    cc_re, cc_im = dense_c(s5_c_re), dense_c(s5_c_im)
    s5d = s5_d.reshape(1, D_S5)
    ln1 = jnp.concatenate([ln1_g, ln1_b], axis=0)
    vec1 = _pad_rows(jnp.concatenate([b2, ln2_g, ln2_b], axis=0), 8)

    z, xbc_pre, xbc, dsilu, dt_raw, u5 = _proj_conv_fwd(x2, mod3, w_in_pad, conv_w_full, conv_b, seq)
    late_in, dt_raw = lax.optimization_barrier(([shard_bf16[n] for n in late_names], dt_raw))
    late_sems = _gather_start(late_in, "gather_late_start")
    yraw, ycat, hprev = _ssd_fwd(xbc, z, dt_raw, par + late_sems[4][0, 0], dsk, norm_w, seq)
    s_re, s_im, ypre, ycat = _s5_fwd(u5, bb_re, bb_im, cc_re, cc_im, pf_re, pf_im, s5d, w_glu_f, b_glu, ycat, seq)
    sent, landed = _gather_wait(late_sems[0], late_sems[1], late_sems[2], late_sems[3], ycat, "gather_late_wait")
    gathered = {n: lax.dynamic_update_index_in_dim(l, x, dev, 0) for n, x, l in zip(late_names, sent, landed)}
    w_out_f = gathered["w_out"].reshape(2 * D_MODEL, D_MODEL)
    w1_blocks = gathered["w1"]
    w2_f = gathered["w2"].reshape(D_FF, D_MODEL)
    mix, x1 = _out_ln1(ycat, x2, mod3, w_out_f, ln1, seq)

    dx1, u2b, hb, dhpb, dob, gacc2, db1, bacc2 = _mlp_fwd_bwd(x1, tgt2, mod3, w1_blocks, w2_f, vec1, b1, seq)

    dmixb, dxa, dyssd, dy5, gacc1, bacc1 = _ln1_out_bwd(dx1, x2, mix, mod3, w_out_f, ln1, seq)

    g_w2 = _atb(hb, dob, "gw2")
    g_w1 = _atb(u2b, dhpb, "gw1")
    g_wout = _atb(ycat, dmixb, "gwout")
    core = lax.axis_index("c").astype(jnp.int32).reshape(1)
    chip = 2 * lax.axis_index("x") + lax.axis_index("y")

    def chip_sums_of(names, grads, tag):
        by_dest = [g if g.ndim == 2 else g.reshape((4, 2) + g.shape[1:]) for g in grads]
        from_sibling = _sibling_swap(by_dest, "rs_swap_" + tag)
        return [_add_halves(g, r, core, "rs_add_" + n) for g, r, n in zip(by_dest, from_sibling, names)]

    early_names = ["w_out", "w1", "w2"]
    early_dest = [g_wout.reshape((4, 2) + w_out.shape[1:]), g_w1, g_w2.reshape((4, 2) + w2.shape[1:])]
    swap = _sibling_swap_start(early_dest, "rs_early_swap_start")
    du5, vacc, sacc, d_cc, d_bb, g_wglu = _s5_bwd(dy5, ypre, u5, s_re, s_im, bb_re, bb_im, cc_re, cc_im,
                                                  pr_re, pr_im, s5d + swap[4][0, 0], w_glu_f, b_glu, seq)
    early_dest, from_sibling = _sibling_swap_wait(swap[0], swap[1], swap[2], swap[3], du5, "rs_early_swap_wait")
    early_sums = [_add_halves(g, r, core, "rs_add_" + n) for g, r, n in zip(early_dest, from_sibling, early_names)]
    early = _all_to_all_start(early_sums, "rs_early_start")
    dxbc, dz, ddt, dpar, cacc = _ssd_bwd(dyssd, yraw, z, xbc, dt_raw, hprev, par + early[4][0, 0], dsk, norm_w, seq)
    grad_x2, ub, dxpb, bacc0, conv_acc = _conv_proj_bwd(dz, dxbc, dsilu, xbc_pre, ddt, du5, x2, dxa, mod3,
                                                        conv_w_full, w_in_pad, seq)

    def diag_b(dd):
        kept = jnp.where(mask_b, dd, 0.0).reshape(S5_BLOCKS, gpb, S5_CH, 512).sum(1)
        return kept.transpose(1, 0, 2).reshape(S5_CH, S5_N)

    def diag_c(dd):
        kept = jnp.where(mask_b.T, dd, 0.0).reshape(S5_BLOCKS, 512, gpb, S5_CH).sum(2)
        return kept.reshape(S5_GROUPS, S5_P, S5_CH).transpose(0, 2, 1)

    g_ar, g_ai, g_ldt, g_br_t, g_bi_t = _s5_params_bwd(ar, ai, ldt, br_t, bi_t, vacc[0:1], vacc[1:2],
                                                      diag_b(d_bb[:S5_BLOCKS]), diag_b(d_bb[S5_BLOCKS:]))

    def from_t(gt):
        return gt.reshape(S5_CH, S5_GROUPS, S5_P).transpose(1, 2, 0)

    small_g = dict(
        conv_w=conv_acc[0:4], conv_b=conv_acc[4:5], dt_bias=dpar[0:1, :N_HEADS], a_log=dpar[1:2, :N_HEADS],
        d_ssd=cacc[0].reshape(N_HEADS, HEADDIM).sum(1), norm_w=cacc[1:2],
        s5_a_re=g_ar, s5_a_im=g_ai, s5_log_dt=g_ldt[:, :S5_GROUPS], s5_b_re=from_t(g_br_t), s5_b_im=from_t(g_bi_t),
        s5_c_re=diag_c(d_cc[:S5_BLOCKS]), s5_c_im=diag_c(d_cc[S5_BLOCKS:]), s5_d=sacc[0:1], b_glu=sacc[1:2],
        ln1_g=gacc1[0:1], ln1_b=gacc1[1:2], b1=db1, b2=gacc2[2:3], ln2_g=gacc2[0:1], ln2_b=gacc2[1:2],
        loss_lanes=gacc2[3:4])

    dmod = jnp.concatenate([bacc0[:, 0], bacc0[:, 1], bacc1[:, 0], bacc2[:, 0], bacc2[:, 1], bacc2[:, 2]], axis=1)
    small_sems = _gather_start([dmod, _pack_small(small_g)], "gather_small_start")
    tok = small_sems[4]
    g_win_t = jnp.concatenate([_atb(dz, ub, "gwin_z", tok), _atb(dxpb, ub, "gwin_xbc", tok),
                               _atb(ddt, ub, "gwin_dt", tok)[:N_HEADS], _atb(du5, ub, "gwin_s5", tok)], axis=0)
    sent, landed = _gather_wait(small_sems[0], small_sems[1], small_sems[2], small_sems[3], g_win_t,
                                "gather_small_wait")
    dmod_all, small_parts = [lax.dynamic_update_index_in_dim(l, x, dev, 0) for x, l in zip(sent, landed)]
    dmod_all = dmod_all.reshape(N_DEV * nb, N_MOD * D_MODEL)
    dmod_cols = lax.dynamic_slice_in_dim(dmod_all, dev * ada_cols, ada_cols, axis=1)
    g_wada, g_bada = _mod_bwd(c_all, dmod_cols, dmod_all)

    late_rs = ["w_in", "w_glu"]
    late_g, small_parts = lax.optimization_barrier(
        ([g_win_t.reshape(N_DEV, w_in.shape[2], D_MODEL), g_wglu.reshape((N_DEV,) + w_glu.shape[1:])], small_parts))
    late = _all_to_all_start(chip_sums_of(late_rs, late_g, "late"), "rs_late_start")

    def own_block_in(landed, sent):
        return [lax.dynamic_update_index_in_dim(l, lax.dynamic_index_in_dim(h, chip, 0, keepdims=False), chip, 0)
                for l, h in zip(landed, sent)]

    sent, landed = _all_to_all_wait(early[0], early[1], early[2], early[3], late[4], "rs_early_wait")
    parts = dict(zip(early_names, own_block_in(landed, sent)))
    res = {k: {} for k in "gdmv"}

    def update(n):
        w_m_v = [(a[n][0].T if n == "w_in" else a[n][0]) for a in (weights, mom, var)]
        outs = _adamw(parts[n], *w_m_v, "adamw_" + n)
        for k, a in zip("gdmv", outs):
            res[k][n] = (a.T if n == "w_in" else a)[None]

    for n in early_names:
        update(n)
    sent, landed = _all_to_all_wait(late[0], late[1], late[2], late[3], res["d"]["w2"], "rs_late_wait")
    parts.update(zip(late_rs, own_block_in(landed, sent)))
    for n in late_rs:
        update(n)

    ag, ad, am, av = _adamw(g_wada[None], w_ada[0], m_w_ada[0], v_w_ada[0], "adamw_w_ada")
    for k, a in (("g", ag), ("d", ad), ("m", am), ("v", av)):
        res[k]["w_ada"] = a[None]
    bg_, bd_, bm_, bv_ = _adamw(g_bada.reshape(1, -1, 128), b_ada.reshape(-1, 128), m_b_ada.reshape(-1, 128),
                                v_b_ada.reshape(-1, 128), "adamw_b_ada")
    for k, a in (("g", bg_), ("d", bd_), ("m", bm_), ("v", bv_)):
        res[k]["b_ada"] = a.reshape(shapes["b_ada"])

    small_shapes = {**shapes, **_NOT_UPDATED}
    rep = {n: (jnp.zeros(_NOT_UPDATED[n], F32) if n in _NOT_UPDATED else weights[n]) for n in _SMALL}
    rep_m = {n: (jnp.zeros(_NOT_UPDATED[n], F32) if n in _NOT_UPDATED else mom[n]) for n in _SMALL}
    rep_v = {n: (jnp.ones(_NOT_UPDATED[n], F32) if n in _NOT_UPDATED else var[n]) for n in _SMALL}
    sg_, sd_, sm_, sv_ = _adamw(small_parts, _pack_small(rep), _pack_small(rep_m), _pack_small(rep_v), "adamw_small")
    for k, p in (("g", sg_), ("d", sd_), ("m", sm_), ("v", sv_)):
        un = _unpack_small(p, small_shapes)
        for n in _SMALL:
            if n not in _NOT_UPDATED:
                res[k][n] = un[n]
    summed = _unpack_small(sg_, small_shapes)
    loss = 0.5 / D_MODEL * jnp.sum(summed["loss_lanes"])
    g_conv_full = summed["conv_w"][0]
    g_conv_mine = lax.dynamic_slice_in_dim(g_conv_full, dev * cw_cols, cw_cols, axis=1)
    cg_, cd_, cm_, cv_ = _adamw(g_conv_mine[None], conv_w[0], m_conv_w[0], v_conv_w[0], "adamw_conv_w")
    for k, a in (("g", cg_), ("d", cd_), ("m", cm_), ("v", cv_)):
        res[k]["conv_w"] = a[None]

    grad_x = grad_x2.reshape(nb, seq, D_MODEL)
    return (loss, grad_x, *[res["g"][n] for n in names], *[res["d"][n] for n in names],
            *[res["m"][n] for n in names], *[res["v"][n] for n in names])
```

```python
import functools
import math

import jax
import jax.numpy as jnp
from jax import lax
from jax.experimental import pallas as pl
from jax.experimental.pallas import tpu as pltpu

F32, BF16 = jnp.float32, jnp.bfloat16
MESH = pl.DeviceIdType.MESH
N_DEV = 8

D_MODEL = 1024
D_SSD = 1536
N_HEADS = 24
HEADDIM = 64
N_GROUPS = 4
HPG = 6
GW = HPG * HEADDIM
N_STATE = 128
CHUNK = 128
D_XBC = 2560
D_S5 = 512
S5_GROUPS = 32
S5_CH = 16
S5_P = 64
S5_N = S5_GROUPS * S5_P
D_IN = 4632
DT_PAD = 128
D_INP = D_SSD + D_XBC + DT_PAD + D_S5
D_FF = 4096
N_MOD = 6
ALPHA = 2.0 ** 0.25
EPS = 1e-5
LR, B1, B2, AEPS, WD, STEP = 0.001, 0.9, 0.999, 1e-08, 0.01, 10

NT = (((1,), (1,)), ((), ()))
TN = (((0,), (0,)), ((), ()))
ANY = pl.BlockSpec(memory_space=pl.ANY)
HIGHEST = lax.Precision.HIGHEST


def _mm(a, b):
    return jnp.dot(a.astype(BF16), b.astype(BF16), preferred_element_type=F32)


def _mm_nt(a, b):
    return lax.dot_general(a.astype(BF16), b.astype(BF16), NT, preferred_element_type=F32)


def _mm_tn(a, b):
    return lax.dot_general(a.astype(BF16), b.astype(BF16), TN, preferred_element_type=F32)


def _row_block(r, cap):
    best = r
    for cand in range(8, min(r, cap) + 1, 8):
        if r % cand == 0:
            best = cand
    return best if best <= cap else r


def _params(vmem_mb):
    return pltpu.CompilerParams(vmem_limit_bytes=vmem_mb << 20)


def _sigmoid(x):
    return 0.5 * (jnp.tanh(0.5 * x) + 1.0)


def _softplus(x):
    return jnp.maximum(x, 0.0) + jnp.log(1.0 + jnp.exp(-jnp.abs(x)))


_GK = math.sqrt(2.0 / math.pi)


def _gelu(x):
    return 0.5 * x * (1.0 + jnp.tanh(_GK * (x + 0.044715 * x * x * x)))


def _gelu_grad(x):
    t = jnp.tanh(_GK * (x + 0.044715 * x * x * x))
    return 0.5 * (1.0 + t) + 0.5 * x * (1.0 - t * t) * _GK * (1.0 + 3.0 * 0.044715 * x * x)


def _dev_index():
    return 4 * lax.axis_index("x") + 2 * lax.axis_index("y") + lax.axis_index("c")


def _all_gather(xs, name):
    n = len(xs)

    def body(*refs):
        x_refs, out_refs = refs[:n], refs[n:2 * n]
        send_sems, recv_sems, local_sems = refs[2 * n:]
        ix, iy, ic = lax.axis_index("x"), lax.axis_index("y"), lax.axis_index("c")
        me, sibling = (ix, iy, ic), (ix, iy, 1 - ic)
        chips = [(1 - ix, iy), (ix, 1 - iy), (1 - ix, 1 - iy)]

        def slot(a, px, py, pc):
            return out_refs[a].at[4 * px + 2 * py + pc]

        def copy(a, k, block, to, src=None):
            return pltpu.make_async_remote_copy(
                src_ref=slot(a, *block) if src is None else src, dst_ref=slot(a, *block),
                send_sem=send_sems.at[7 * a + k], recv_sem=recv_sems.at[7 * a + k], device_id=to, device_id_type=MESH)

        mine = [pltpu.make_async_copy(x_refs[a], slot(a, *me), local_sems.at[a]) for a in range(n)]
        for cp in mine:
            cp.start()
        first = []
        for j, chip in enumerate(chips):
            first += [copy(a, 1 + j, me, (*chip, ic), src=x_refs[a]) for a in range(n)]
        first += [copy(a, 0, me, sibling, src=x_refs[a]) for a in range(n)]
        for cp in first:
            cp.start()
        passed = []
        for j, chip in enumerate(chips):
            for a in range(n):
                copy(a, 1 + j, (*chip, ic), me).wait_recv()
                cp = copy(a, 4 + j, (*chip, ic), sibling)
                cp.start()
                passed.append(cp)
        for a in range(n):
            copy(a, 0, sibling, me).wait_recv()
            for j, chip in enumerate(chips):
                copy(a, 4 + j, (*chip, 1 - ic), me).wait_recv()
        for cp in first + passed:
            cp.wait_send()
        for cp in mine:
            cp.wait()

    return pl.pallas_call(
        body, name=name, out_shape=tuple(jax.ShapeDtypeStruct((N_DEV,) + x.shape, x.dtype) for x in xs),
        in_specs=[ANY] * n, out_specs=tuple([ANY] * n),
        scratch_shapes=[pltpu.SemaphoreType.DMA((7 * n,)), pltpu.SemaphoreType.DMA((7 * n,)),
                        pltpu.SemaphoreType.DMA((n,))],
    )(*xs)


HBM = pl.BlockSpec(memory_space=pltpu.HBM)
SEM = pl.BlockSpec(memory_space=pltpu.SEMAPHORE)
DATAFLOW = pltpu.SideEffectType.DATAFLOW_SIDE_EFFECTING


def _peer(k):
    ix, iy, ic = lax.axis_index("x"), lax.axis_index("y"), lax.axis_index("c")
    return (1 - ix if k & 4 else ix, 1 - iy if k & 2 else iy, 1 - ic if k & 1 else ic)


def _block_of(p):
    return 4 * p[0] + 2 * p[1] + p[2]


def _gather_start(xs, name):
    n = len(xs)
    lands = [lax.empty((N_DEV,) + x.shape, x.dtype) for x in xs]

    def body(*refs):
        x_refs, land_refs = refs[:n], refs[n:2 * n]
        send_sems, recv_sems = refs[2 * n], refs[2 * n + 1]
        token = refs[-1]
        me = _block_of(_peer(0))
        for a in range(n):
            for k in range(1, N_DEV):
                pltpu.make_async_remote_copy(
                    src_ref=x_refs[a], dst_ref=land_refs[a].at[me], send_sem=send_sems.at[7 * a + k - 1],
                    recv_sem=recv_sems.at[7 * a + k - 1], device_id=_peer(k), device_id_type=MESH).start()
        token[...] = jnp.zeros_like(token)

    outs = pl.pallas_call(
        body, name=name,
        out_shape=(pltpu.SemaphoreType.DMA((7 * n,)), pltpu.SemaphoreType.DMA((7 * n,)))
        + tuple(pltpu.HBM(x.shape, x.dtype) for x in xs) + tuple(pltpu.HBM(l.shape, l.dtype) for l in lands)
        + (jax.ShapeDtypeStruct((8, 128), F32),),
        in_specs=[HBM] * (2 * n), out_specs=(SEM, SEM) + (HBM,) * (2 * n) + (pl.BlockSpec(memory_space=pltpu.VMEM),),
        input_output_aliases={i: 2 + i for i in range(2 * n)},
        compiler_params=pltpu.CompilerParams(has_side_effects=DATAFLOW),
    )(*[pltpu.with_memory_space_constraint(x, pltpu.HBM) for x in xs],
      *[pltpu.with_memory_space_constraint(l, pltpu.HBM) for l in lands])
    return outs[0], outs[1], outs[2:2 + n], outs[2 + n:2 + 2 * n], outs[-1]


def _gather_wait(send_sems, recv_sems, xs_thru, lands_thru, after, name):
    n = len(xs_thru)

    def body(*refs):
        x_refs, land_refs = refs[:n], refs[n:2 * n]
        send_sems, recv_sems = refs[2 * n], refs[2 * n + 1]
        for a in range(n):
            for k in range(1, N_DEV):
                cp = pltpu.make_async_remote_copy(
                    src_ref=x_refs[a], dst_ref=land_refs[a].at[_block_of(_peer(k))], send_sem=send_sems.at[7 * a + k - 1],
                    recv_sem=recv_sems.at[7 * a + k - 1], device_id=_peer(k), device_id_type=MESH)
                cp.wait_send()
                cp.wait_recv()

    outs = pl.pallas_call(
        body, name=name,
        out_shape=tuple(pltpu.HBM(x.shape, x.dtype) for x in xs_thru)
        + tuple(pltpu.HBM(l.shape, l.dtype) for l in lands_thru),
        in_specs=[HBM] * (2 * n) + [SEM, SEM, ANY], out_specs=(HBM,) * (2 * n),
        input_output_aliases={i: i for i in range(2 * n)},
        compiler_params=pltpu.CompilerParams(has_side_effects=DATAFLOW),
    )(*xs_thru, *lands_thru, send_sems, recv_sems, after)
    return outs[:n], outs[n:]


def _chip_peer(k):
    ix, iy = lax.axis_index("x"), lax.axis_index("y")
    return (1 - ix if k & 2 else ix, 1 - iy if k & 1 else iy)


def _all_to_all_start(hs, name):
    n = len(hs)
    lands = [lax.empty(h.shape, h.dtype) for h in hs]

    def body(*refs):
        h_refs, land_refs = refs[:n], refs[n:2 * n]
        send_sems, recv_sems = refs[2 * n], refs[2 * n + 1]
        token = refs[-1]
        ic = lax.axis_index("c")
        mx, my = _chip_peer(0)
        for a in range(n):
            for k in range(1, 4):
                px, py = _chip_peer(k)
                pltpu.make_async_remote_copy(
                    src_ref=h_refs[a].at[2 * px + py], dst_ref=land_refs[a].at[2 * mx + my],
                    send_sem=send_sems.at[3 * a + k - 1], recv_sem=recv_sems.at[3 * a + k - 1],
                    device_id=(px, py, ic), device_id_type=MESH).start()
        token[...] = jnp.zeros_like(token)

    outs = pl.pallas_call(
        body, name=name,
        out_shape=(pltpu.SemaphoreType.DMA((3 * n,)), pltpu.SemaphoreType.DMA((3 * n,)))
        + tuple(pltpu.HBM(h.shape, h.dtype) for h in hs) + tuple(pltpu.HBM(l.shape, l.dtype) for l in lands)
        + (jax.ShapeDtypeStruct((8, 128), F32),),
        in_specs=[HBM] * (2 * n), out_specs=(SEM, SEM) + (HBM,) * (2 * n) + (pl.BlockSpec(memory_space=pltpu.VMEM),),
        input_output_aliases={i: 2 + i for i in range(2 * n)},
        compiler_params=pltpu.CompilerParams(has_side_effects=DATAFLOW),
    )(*[pltpu.with_memory_space_constraint(h, pltpu.HBM) for h in hs],
      *[pltpu.with_memory_space_constraint(l, pltpu.HBM) for l in lands])
    return outs[0], outs[1], outs[2:2 + n], outs[2 + n:2 + 2 * n], outs[-1]


def _all_to_all_wait(send_sems, recv_sems, hs_thru, lands_thru, after, name):
    n = len(hs_thru)

    def body(*refs):
        h_refs, land_refs = refs[:n], refs[n:2 * n]
        send_sems, recv_sems = refs[2 * n], refs[2 * n + 1]
        ic = lax.axis_index("c")
        for a in range(n):
            for k in range(1, 4):
                px, py = _chip_peer(k)
                cp = pltpu.make_async_remote_copy(
                    src_ref=h_refs[a].at[2 * px + py], dst_ref=land_refs[a].at[2 * px + py],
                    send_sem=send_sems.at[3 * a + k - 1], recv_sem=recv_sems.at[3 * a + k - 1],
                    device_id=(px, py, ic), device_id_type=MESH)
                cp.wait_send()
                cp.wait_recv()

    outs = pl.pallas_call(
        body, name=name,
        out_shape=tuple(pltpu.HBM(h.shape, h.dtype) for h in hs_thru)
        + tuple(pltpu.HBM(l.shape, l.dtype) for l in lands_thru),
        in_specs=[HBM] * (2 * n) + [SEM, SEM, ANY], out_specs=(HBM,) * (2 * n),
        input_output_aliases={i: i for i in range(2 * n)},
        compiler_params=pltpu.CompilerParams(has_side_effects=DATAFLOW),
    )(*hs_thru, *lands_thru, send_sems, recv_sems, after)
    return outs[:n], outs[n:]


def _sibling_block(g_ref, q):
    ic = lax.axis_index("c")
    if len(g_ref.shape) == 4:
        return g_ref.at[q, 1 - ic]
    cw = g_ref.shape[1] // N_DEV
    return g_ref.at[:, pl.ds(pl.multiple_of((2 * q + 1 - ic) * cw, 128), cw)]


def _sibling_swap_start(gs, name):
    n = len(gs)
    lands = [lax.empty((4,) + (g.shape[2:] if g.ndim == 4 else (g.shape[0], g.shape[1] // N_DEV)), g.dtype) for g in gs]

    def body(*refs):
        g_refs, land_refs = refs[:n], refs[n:2 * n]
        send_sems, recv_sems = refs[2 * n], refs[2 * n + 1]
        token = refs[-1]
        barrier = pltpu.get_barrier_semaphore()
        pl.semaphore_signal(barrier, inc=1, device_id=_peer(1), device_id_type=MESH)
        pl.semaphore_wait(barrier, 1)
        for a in range(n):
            for q in range(4):
                pltpu.make_async_remote_copy(
                    src_ref=_sibling_block(g_refs[a], q), dst_ref=land_refs[a].at[q],
                    send_sem=send_sems.at[4 * a + q], recv_sem=recv_sems.at[4 * a + q],
                    device_id=_peer(1), device_id_type=MESH).start()
        token[...] = jnp.zeros_like(token)

    outs = pl.pallas_call(
        body, name=name,
        out_shape=(pltpu.SemaphoreType.DMA((4 * n,)), pltpu.SemaphoreType.DMA((4 * n,)))
        + tuple(pltpu.HBM(g.shape, g.dtype) for g in gs) + tuple(pltpu.HBM(l.shape, l.dtype) for l in lands)
        + (jax.ShapeDtypeStruct((8, 128), F32),),
        in_specs=[HBM] * (2 * n), out_specs=(SEM, SEM) + (HBM,) * (2 * n) + (pl.BlockSpec(memory_space=pltpu.VMEM),),
        input_output_aliases={i: 2 + i for i in range(2 * n)},
        compiler_params=pltpu.CompilerParams(has_side_effects=DATAFLOW, collective_id=0),
    )(*[pltpu.with_memory_space_constraint(g, pltpu.HBM) for g in gs],
      *[pltpu.with_memory_space_constraint(l, pltpu.HBM) for l in lands])
    return outs[0], outs[1], outs[2:2 + n], outs[2 + n:2 + 2 * n], outs[-1]


def _sibling_swap_wait(send_sems, recv_sems, gs_thru, lands_thru, after, name):
    n = len(gs_thru)

    def body(*refs):
        g_refs, land_refs = refs[:n], refs[n:2 * n]
        send_sems, recv_sems = refs[2 * n], refs[2 * n + 1]
        for a in range(n):
            for q in range(4):
                cp = pltpu.make_async_remote_copy(
                    src_ref=_sibling_block(g_refs[a], q), dst_ref=land_refs[a].at[q],
                    send_sem=send_sems.at[4 * a + q], recv_sem=recv_sems.at[4 * a + q],
                    device_id=_peer(1), device_id_type=MESH)
                cp.wait_send()
                cp.wait_recv()

    outs = pl.pallas_call(
        body, name=name,
        out_shape=tuple(pltpu.HBM(g.shape, g.dtype) for g in gs_thru)
        + tuple(pltpu.HBM(l.shape, l.dtype) for l in lands_thru),
        in_specs=[HBM] * (2 * n) + [SEM, SEM, ANY], out_specs=(HBM,) * (2 * n),
        input_output_aliases={i: i for i in range(2 * n)},
        compiler_params=pltpu.CompilerParams(has_side_effects=DATAFLOW),
    )(*gs_thru, *lands_thru, send_sems, recv_sems, after)
    return outs[:n], outs[n:]


def _sibling_swap(gs, name):
    n = len(gs)

    def body(*refs):
        g_refs, recv_refs = refs[:n], refs[n:2 * n]
        send_sems, recv_sems = refs[2 * n:]
        ix, iy, ic = lax.axis_index("x"), lax.axis_index("y"), lax.axis_index("c")
        cps = []
        for a in range(n):
            for q in range(4):
                cps.append(pltpu.make_async_remote_copy(
                    src_ref=_sibling_block(g_refs[a], q), dst_ref=recv_refs[a].at[q],
                    send_sem=send_sems.at[4 * a + q], recv_sem=recv_sems.at[4 * a + q],
                    device_id=(ix, iy, 1 - ic), device_id_type=MESH))
        for cp in cps:
            cp.start()
        for cp in cps:
            cp.wait()

    return pl.pallas_call(
        body, name=name,
        out_shape=tuple(jax.ShapeDtypeStruct(
            (4,) + (g.shape[2:] if g.ndim == 4 else (g.shape[0], g.shape[1] // N_DEV)), g.dtype) for g in gs),
        in_specs=[ANY] * n, out_specs=tuple([ANY] * n),
        scratch_shapes=[pltpu.SemaphoreType.DMA((4 * n,)), pltpu.SemaphoreType.DMA((4 * n,))],
    )(*gs)


def _add_halves(g, recv, core, name):
    _, r, c = recv.shape
    br = _row_block(r, 512)
    stacked = g.ndim == 4

    def body(core_ref, g_ref, r_ref, o_ref):
        o_ref[0] = ((g_ref[0, 0] if stacked else g_ref[...]) + r_ref[0]).astype(BF16)

    spec = pl.BlockSpec((1, br, c), lambda i, j, core_ref: (i, j, 0))
    if stacked:
        g_spec = pl.BlockSpec((1, 1, br, c), lambda i, j, core_ref: (i, core_ref[0], j, 0))
    else:
        g_spec = pl.BlockSpec((br, c), lambda i, j, core_ref: (j, 2 * i + core_ref[0]))
    return pl.pallas_call(
        body, name=name, out_shape=jax.ShapeDtypeStruct(recv.shape, BF16),
        grid_spec=pltpu.PrefetchScalarGridSpec(
            num_scalar_prefetch=1, grid=(4, r // br), in_specs=[g_spec, spec], out_specs=spec),
        compiler_params=_params(32),
    )(core, g, recv)


def _adamw(parts, w, m, v, name):
    n_parts, r, c = parts.shape
    if r % 8 == 0:
        br, bc = _row_block(r, 512 if c <= 1024 else 256), c
    else:
        br, bc = r, (256 if c % 256 == 0 else c)

    def body(p_ref, w_ref, m_ref, v_ref, g_out, d_out, m_out, v_out):
        g = p_ref[0].astype(F32)
        for p in range(1, n_parts):
            g = g + p_ref[p].astype(F32)
        m2 = B1 * m_ref[...] + (1.0 - B1) * g
        v2 = B2 * v_ref[...] + (1.0 - B2) * (g * g)
        m_hat = m2 / (1.0 - B1 ** STEP)
        v_hat = v2 / (1.0 - B2 ** STEP)
        g_out[...] = g
        d_out[...] = -LR * (m_hat / (jnp.sqrt(v_hat) + AEPS) + WD * w_ref[...])
        m_out[...] = m2
        v_out[...] = v2

    spec = pl.BlockSpec((br, bc), lambda i, j: (i, j))
    out = jax.ShapeDtypeStruct((r, c), F32)
    return pl.pallas_call(
        body, name=name, out_shape=(out, out, out, out), grid=(r // br, c // bc),
        in_specs=[pl.BlockSpec((n_parts, br, bc), lambda i, j: (0, i, j)), spec, spec, spec],
        out_specs=(spec, spec, spec, spec), compiler_params=_params(40),
    )(parts, w, m, v)


def _atb(a, b, name, after=None):
    t, k1 = a.shape
    k2 = b.shape[1]
    bt = math.gcd(t, 2048)

    def pick(k):
        for cand in (1024, 768, 512, 384, 256, 128):
            if k % cand == 0:
                return cand
        return k

    b1, b2 = pick(k1), pick(k2)

    def body(a_ref, b_ref, *rest):
        o_ref = rest[-1]

        @pl.when(pl.program_id(2) == 0)
        def _():
            o_ref[...] = jnp.zeros_like(o_ref)
        o_ref[...] += _mm_tn(a_ref[...], b_ref[...])

    extra = [] if after is None else [after]
    return pl.pallas_call(
        body, name=name, out_shape=jax.ShapeDtypeStruct((k1, k2), F32), grid=(k1 // b1, k2 // b2, t // bt),
        in_specs=[pl.BlockSpec((bt, b1), lambda i, j, k: (k, i)), pl.BlockSpec((bt, b2), lambda i, j, k: (k, j))]
        + [ANY] * len(extra),
        out_specs=pl.BlockSpec((b1, b2), lambda i, j, k: (i, j)), compiler_params=_params(48),
    )(a, b, *extra)


def _mod_fwd(c_all, w_ada, b_cols):
    def body(c_ref, w_ref, b_ref, o_ref):
        cc = c_ref[...]
        cond = cc * _sigmoid(cc)
        o_ref[...] = _mm(cond, w_ref[...]) + b_ref[...]

    return pl.pallas_call(body, name="mod_fwd", out_shape=jax.ShapeDtypeStruct((c_all.shape[0], w_ada.shape[1]), F32),
                          compiler_params=_params(32))(c_all, w_ada, b_cols)


def _mod_bwd(c_all, dmod_cols, dmod_all):
    def body(c_ref, dc_ref, da_ref, gw_ref, gb_ref):
        cc = c_ref[...]
        cond = cc * _sigmoid(cc)
        gw_ref[...] = _mm_tn(cond, dc_ref[...])
        gb_ref[...] = jnp.sum(da_ref[...], axis=0, keepdims=True)

    return pl.pallas_call(
        body, name="mod_bwd",
        out_shape=(jax.ShapeDtypeStruct((D_MODEL, dmod_cols.shape[1]), F32), jax.ShapeDtypeStruct((1, dmod_all.shape[1]), F32)),
        compiler_params=_params(32))(c_all, dmod_cols, dmod_all)


def _load_once(hbm_ref, vmem_ref, sem):
    @pl.when(pl.program_id(0) == 0)
    def _():
        cp = pltpu.make_async_copy(hbm_ref, vmem_ref, sem)
        cp.start()
        cp.wait()


def _conv_taps(win_ref, w, tb, cols):
    shifted = [win_ref[8 - j:8 - j + tb, cols] for j in range(4)]
    acc = w[3:4] * shifted[0]
    for j in (1, 2, 3):
        acc = acc + w[3 - j:4 - j] * shifted[j]
    return acc, shifted


def _proj_conv_fwd(x2, mod3, w_in_pad, conv_w, conv_b, seq):
    t = x2.shape[0]
    tb = 256
    npb = seq // tb
    cw = 512

    def body(x_ref, mod_ref, w_hbm, cw_ref, cb_ref, z_ref, pre_ref, xbc_ref, dsilu_ref, dt_ref, u5_ref, w_vmem, win, sem):
        _load_once(w_hbm, w_vmem, sem)
        first = (pl.program_id(0) % npb) == 0

        @pl.when(first)
        def _():
            win[0:8, :] = jnp.zeros((8, D_XBC), F32)

        @pl.when(jnp.logical_not(first))
        def _():
            win[0:8, :] = win[tb:tb + 8, :]

        m = mod_ref[0]
        u = (x_ref[...] * (1.0 + m[1:2]) + m[0:1]).astype(BF16)
        z_ref[...] = lax.dot_general(u, w_vmem[0:D_SSD, :], NT, preferred_element_type=F32)
        dt_ref[...] = lax.dot_general(u, w_vmem[D_SSD + D_XBC:D_SSD + D_XBC + DT_PAD, :], NT,
                                      preferred_element_type=F32)
        u5_ref[...] = lax.dot_general(u, w_vmem[D_SSD + D_XBC + DT_PAD:, :], NT, preferred_element_type=F32)
        for k in range(D_XBC // cw):
            cols = slice(k * cw, (k + 1) * cw)
            pre_k = lax.dot_general(u, w_vmem[D_SSD + k * cw:D_SSD + (k + 1) * cw, :], NT,
                                    preferred_element_type=F32)
            win[8:8 + tb, cols] = pre_k
            pre_ref[:, cols] = pre_k
            conv, _ = _conv_taps(win, cw_ref[:, cols], tb, cols)
            conv = conv + cb_ref[:, cols]
            sg = _sigmoid(conv)
            xbc_ref[:, cols] = conv * sg
            dsilu_ref[:, cols] = sg * (1.0 + conv * (1.0 - sg))

    row = lambda w: pl.BlockSpec((tb, w), lambda i: (i, 0))
    return pl.pallas_call(
        body, name="proj_conv_fwd", grid=(t // tb,),
        out_shape=(jax.ShapeDtypeStruct((t, D_SSD), F32), jax.ShapeDtypeStruct((t, D_XBC), F32),
                   jax.ShapeDtypeStruct((t, D_XBC), F32), jax.ShapeDtypeStruct((t, D_XBC), F32),
                   jax.ShapeDtypeStruct((t, DT_PAD), F32), jax.ShapeDtypeStruct((t, D_S5), F32)),
        in_specs=[row(D_MODEL), pl.BlockSpec((1, N_MOD, D_MODEL), lambda i: (i // npb, 0, 0)), ANY,
                  pl.BlockSpec((4, D_XBC), lambda i: (0, 0)), pl.BlockSpec((1, D_XBC), lambda i: (0, 0))],
        out_specs=(row(D_SSD), row(D_XBC), row(D_XBC), row(D_XBC), row(DT_PAD), row(D_S5)),
        scratch_shapes=[pltpu.VMEM((D_INP, D_MODEL), BF16), pltpu.VMEM((tb + 8, D_XBC), F32), pltpu.SemaphoreType.DMA],
        compiler_params=_params(56),
    )(x2, mod3, w_in_pad, conv_w, conv_b)


N_PAIRS = N_HEADS // 2


def _split3(x):
    hi = x.astype(BF16)
    r = x - hi.astype(F32)
    mid = r.astype(BF16)
    lo = (r - mid.astype(F32)).astype(BF16)
    return hi, mid, lo


def _dot3(x, e, dims=(((1,), (0,)), ((), ()))):
    return sum(lax.dot_general(p, e, dims, preferred_element_type=F32) for p in _split3(x))


def _dot3_left(e, x, dims=(((1,), (0,)), ((), ()))):
    return sum(lax.dot_general(e, p, dims, preferred_element_type=F32) for p in _split3(x))


def _head_fold():
    return (jnp.arange(D_SSD)[:, None] // HEADDIM == jnp.arange(128)[None, :]).astype(BF16)


def _ssd_prep(dt_raw, par):
    dtb = par[0:1]
    a = -jnp.exp(par[1:2])
    dt = _softplus(dt_raw + dtb)
    adt = dt * a
    row = lax.broadcasted_iota(jnp.int32, (CHUNK, CHUNK), 0)
    col = lax.broadcasted_iota(jnp.int32, (CHUNK, CHUNK), 1)
    causal = row >= col
    tri = causal.astype(BF16)
    cs = _dot3_left(tri, adt)
    left = col < HEADDIM

    def lanes(v, h):
        return jnp.broadcast_to(v[:, h:h + 1], (CHUNK, 128))

    dt_c, cs_c, pair_cols = [], [], []
    for p in range(N_PAIRS):
        c0, c1 = lanes(cs, 2 * p), lanes(cs, 2 * p + 1)
        pair_cols.append(jnp.concatenate([c0, c1], axis=1))
        cs_c.append(jnp.where(left, c0, c1))
        dt_c.append(jnp.where(left, lanes(dt, 2 * p), lanes(dt, 2 * p + 1)))
    cs_c = jnp.concatenate(cs_c, axis=1)
    dt_c = jnp.concatenate(dt_c, axis=1)
    return dt, a, cs, cs.T, causal, tri, dt_c, jnp.exp(cs_c), jnp.exp(cs_c[CHUNK - 1:CHUNK, :] - cs_c), pair_cols


def _pair_decay(cols, cst, pair, causal2):
    rows = jnp.concatenate([jnp.broadcast_to(cst[2 * pair:2 * pair + 1, :], (CHUNK, CHUNK)),
                            jnp.broadcast_to(cst[2 * pair + 1:2 * pair + 2, :], (CHUNK, CHUNK))], axis=1)
    return jnp.exp(jnp.where(causal2, cols - rows, -jnp.inf))


def _stack_heads(xp, left):
    return jnp.concatenate([jnp.where(left, xp, 0.0), jnp.where(left, 0.0, xp)], axis=0).astype(BF16)


def _ssd_fwd(xbc, z, dt_raw, par, dsk, normw, seq):
    t = xbc.shape[0]
    nc = seq // CHUNK
    n_chunks = t // CHUNK

    def body(xbc_ref, z_ref, dt_ref, par_ref, dsk_ref, nw_ref, yraw_ref, ycat_ref, hprev_ref, h_ref):
        @pl.when(pl.program_id(0) % nc == 0)
        def _():
            h_ref[...] = jnp.zeros_like(h_ref)
        hprev_ref[0] = h_ref[...]
        _, _, cs, cst, causal, _, dt_c, ecs_c, w_c, pair_cols = _ssd_prep(dt_ref[...], par_ref[...])
        cs_last = cs[CHUNK - 1:CHUNK, :]
        causal2 = jnp.concatenate([causal, causal], axis=1)
        left = lax.broadcasted_iota(jnp.int32, (CHUNK, 128), 1) < HEADDIM
        x = xbc_ref[:, 0:D_SSD]
        xdt = x * dt_c
        amat = (w_c * xdt).astype(BF16)
        zz = z_ref[...]
        silu_z = zz * _sigmoid(zz)
        for g in range(N_GROUPS):
            gs = slice(g * GW, (g + 1) * GW)
            bg = xbc_ref[:, D_SSD + g * N_STATE:D_SSD + (g + 1) * N_STATE].astype(BF16)
            cg = xbc_ref[:, D_SSD + (N_GROUPS + g) * N_STATE:D_SSD + (N_GROUPS + g + 1) * N_STATE].astype(BF16)
            scores = lax.dot_general(cg, bg, NT, preferred_element_type=F32)
            scores2 = jnp.concatenate([scores, scores], axis=1)
            hg = h_ref[gs, :]
            p_all = lax.dot_general(cg, hg.astype(BF16), NT, preferred_element_type=F32)
            ys = []
            for q in range(GW // 128):
                pair = g * (GW // 128) + q
                decay = _pair_decay(pair_cols[pair], cst, pair, causal2)
                mcat = (scores2 * decay).astype(BF16)
                ys.append(jnp.dot(mcat, _stack_heads(xdt[:, pair * 128:(pair + 1) * 128], left),
                                  preferred_element_type=F32))
            yg = jnp.concatenate(ys, axis=1) + ecs_c[:, gs] * p_all + x[:, gs] * dsk_ref[:, gs]
            s_new = lax.dot_general(amat[:, gs], bg, TN, preferred_element_type=F32)
            for j in range(HPG):
                hh = g * HPG + j
                js = slice(j * HEADDIM, (j + 1) * HEADDIM)
                h_ref[g * GW + j * HEADDIM:g * GW + (j + 1) * HEADDIM, :] = (
                    hg[js, :] * jnp.exp(cs_last[:, hh:hh + 1]) + s_new[js, :])
            yraw_ref[:, gs] = yg
            v = yg * silu_z[:, gs]
            r = lax.rsqrt(jnp.mean(v * v, axis=-1, keepdims=True) + EPS)
            ycat_ref[:, gs] = (v * r * nw_ref[:, gs]).astype(BF16)

    row = lambda w: pl.BlockSpec((CHUNK, w), lambda i: (i, 0))
    full = lambda s: pl.BlockSpec(s, lambda i: (0,) * len(s))
    return pl.pallas_call(
        body, name="ssd_fwd", grid=(n_chunks,),
        out_shape=(jax.ShapeDtypeStruct((t, D_SSD), F32), jax.ShapeDtypeStruct((t, D_SSD + D_S5), BF16),
                   jax.ShapeDtypeStruct((n_chunks, D_SSD, N_STATE), F32)),
        in_specs=[row(D_XBC), row(D_SSD), row(DT_PAD), full((8, 128)), full((1, D_SSD)), full((1, D_SSD))],
        out_specs=(row(D_SSD), row(D_SSD), pl.BlockSpec((1, D_SSD, N_STATE), lambda i: (i, 0, 0))),
        scratch_shapes=[pltpu.VMEM((D_SSD, N_STATE), F32)],
        compiler_params=_params(40),
    )(xbc, z, dt_raw, par, dsk, normw)


S5_CW = 512
S5_BLOCKS = 4


def _tile_scan(in_re, in_im, out_re, out_im, carry_re, carry_im, pw_re, pw_im, n_tiles, reverse):
    steps = (1, 2, 4)
    for cc in range(S5_N // S5_CW):
        cols = slice(cc * S5_CW, (cc + 1) * S5_CW)
        a_re, a_im = pw_re[:, cols], pw_im[:, cols]
        rid = lax.broadcasted_iota(jnp.int32, (8, S5_CW), 0)
        pows = []
        for d in steps:
            k = 8 - d if reverse else d - 1
            keep = (rid < 8 - d) if reverse else (rid >= d)
            pows.append((jnp.where(keep, pw_re[k:k + 1, cols], 0.0), jnp.where(keep, pw_im[k:k + 1, cols], 0.0)))

        def tile(i, carry, cols=cols, pows=pows, a_re=a_re, a_im=a_im):
            r = (n_tiles - 1 - i) if reverse else i
            rows = pl.ds(pl.multiple_of(r * 8, 8), 8)
            xr, xi = in_re[rows, cols], in_im[rows, cols]
            for (pr, pi), d in zip(pows, steps):
                shift = 8 - d if reverse else d
                sr, si = pltpu.roll(xr, shift, axis=0), pltpu.roll(xi, shift, axis=0)
                xr, xi = xr + pr * sr - pi * si, xi + pr * si + pi * sr
            cr, ci = carry
            xr, xi = xr + a_re * cr - a_im * ci, xi + a_re * ci + a_im * cr
            out_re[rows, cols] = xr
            out_im[rows, cols] = xi
            edge = slice(0, 1) if reverse else slice(7, 8)
            return (jnp.broadcast_to(xr[edge], (8, S5_CW)), jnp.broadcast_to(xi[edge], (8, S5_CW)))

        c0 = (jnp.broadcast_to(carry_re[0:1, cols], (8, S5_CW)), jnp.broadcast_to(carry_im[0:1, cols], (8, S5_CW)))
        cr, ci = lax.fori_loop(0, n_tiles, tile, c0, unroll=True)
        carry_re[:, cols] = cr
        carry_im[:, cols] = ci


def _s5_params_math(ar, ai, ldt, br, bi):
    dt = jnp.exp(ldt)
    mag = jnp.exp(ar * dt)
    ang = ai * dt
    ab_re = mag * jnp.cos(ang)
    ab_im = mag * jnp.sin(ang)
    den = ar * ar + ai * ai
    n_re = ab_re - 1.0
    coef_re = (n_re * ar + ab_im * ai) / den
    coef_im = (ab_im * ar - n_re * ai) / den
    bb_re = coef_re * br - coef_im * bi
    bb_im = coef_re * bi + coef_im * br
    return ab_re, ab_im, bb_re, bb_im


def _s5_params_fwd(ar, ai, ldt, br, bi):
    def body(ar_ref, ai_ref, ldt_ref, br_ref, bi_ref, bbr_ref, bbi_ref, pfr_ref, pfi_ref, prr_ref, pri_ref):
        ab_re, ab_im, bb_re, bb_im = _s5_params_math(ar_ref[...], ai_ref[...], ldt_ref[...], br_ref[...], bi_ref[...])
        bbr_ref[...] = bb_re
        bbi_ref[...] = bb_im
        pr, pi = ab_re, ab_im
        for k in range(8):
            pfr_ref[k:k + 1, :] = pr
            pfi_ref[k:k + 1, :] = pi
            prr_ref[7 - k:8 - k, :] = pr
            pri_ref[7 - k:8 - k, :] = -pi
            pr, pi = pr * ab_re - pi * ab_im, pr * ab_im + pi * ab_re

    b16 = jax.ShapeDtypeStruct((S5_CH, S5_N), F32)
    p8 = jax.ShapeDtypeStruct((8, S5_N), F32)
    return pl.pallas_call(body, name="s5_params_fwd", out_shape=(b16, b16, p8, p8, p8, p8),
                          compiler_params=_params(32))(ar, ai, ldt, br, bi)


def _s5_params_bwd(ar, ai, ldt, br, bi, d_ab_re, d_ab_im, d_bb_re, d_bb_im):
    def body(ar_ref, ai_ref, ldt_ref, br_ref, bi_ref, dar_ref, dai_ref, dbr_ref, dbi_ref,
             gar_ref, gai_ref, gldt_ref, gbr_ref, gbi_ref):
        _, vjp = jax.vjp(_s5_params_math, ar_ref[...], ai_ref[...], ldt_ref[...], br_ref[...], bi_ref[...])
        g_ar, g_ai, g_ldt, g_br, g_bi = vjp((dar_ref[...], dai_ref[...], dbr_ref[...], dbi_ref[...]))
        gar_ref[...] = g_ar
        gai_ref[...] = g_ai
        gbr_ref[...] = g_br
        gbi_ref[...] = g_bi
        lane = lax.broadcasted_iota(jnp.int32, (S5_N, 128), 0) // S5_P
        grp = lax.broadcasted_iota(jnp.int32, (S5_N, 128), 1)
        fold = (lane == grp).astype(F32)
        gldt_ref[...] = jnp.dot(g_ldt, fold, preferred_element_type=F32, precision=HIGHEST)

    v1 = jax.ShapeDtypeStruct((1, S5_N), F32)
    b16 = jax.ShapeDtypeStruct((S5_CH, S5_N), F32)
    return pl.pallas_call(body, name="s5_params_bwd",
                          out_shape=(v1, v1, jax.ShapeDtypeStruct((1, 128), F32), b16, b16),
                          compiler_params=_params(32))(ar, ai, ldt, br, bi, d_ab_re, d_ab_im, d_bb_re, d_bb_im)


def _s5_fwd(u5, bb_re, bb_im, cc_re, cc_im, pf_re, pf_im, s5d, w_glu, b_glu, ycat, seq):
    t = u5.shape[0]
    tb = 256
    npb = seq // tb

    def body(u_ref, bbr_ref, bbi_ref, ccr_ref, cci_ref, pfr_ref, pfi_ref, d_ref, wg_ref, bg_ref, ycat_hbm,
             sre_ref, sim_ref, ypre_ref, y5_ref, bur, bui, car, cai):
        del ycat_hbm

        @pl.when(pl.program_id(0) % npb == 0)
        def _():
            car[...] = jnp.zeros_like(car)
            cai[...] = jnp.zeros_like(cai)
        u = u_ref[...]
        ub = u.astype(BF16)
        for j in range(S5_BLOCKS):
            ch, st = slice(j * 128, (j + 1) * 128), slice(j * 512, (j + 1) * 512)
            bur[:, st] = jnp.dot(ub[:, ch], bbr_ref[j], preferred_element_type=F32)
            bui[:, st] = jnp.dot(ub[:, ch], bbi_ref[j], preferred_element_type=F32)
        _tile_scan(bur, bui, sre_ref, sim_ref, car, cai, pfr_ref, pfi_ref, tb // 8, reverse=False)
        cs_y = []
        for j in range(S5_BLOCKS):
            st = slice(j * 512, (j + 1) * 512)
            cs_y.append(_mm(sre_ref[:, st], ccr_ref[j]) - _mm(sim_ref[:, st], cci_ref[j]))
        ypre = jnp.concatenate(cs_y, axis=1) + u * d_ref[...]
        ypre_ref[...] = ypre
        yg = _gelu(ypre)
        y5_ref[...] = (yg * _sigmoid(_mm(yg, wg_ref[...]) + bg_ref[...])).astype(BF16)

    row = lambda w: pl.BlockSpec((tb, w), lambda i: (i, 0))
    full = lambda a: pl.BlockSpec(a.shape, lambda i: (0,) * a.ndim)
    return pl.pallas_call(
        body, name="s5_fwd", grid=(t // tb,),
        out_shape=(jax.ShapeDtypeStruct((t, S5_N), F32), jax.ShapeDtypeStruct((t, S5_N), F32),
                   jax.ShapeDtypeStruct((t, D_S5), F32), jax.ShapeDtypeStruct(ycat.shape, BF16)),
        in_specs=[row(D_S5), full(bb_re), full(bb_im), full(cc_re), full(cc_im), full(pf_re), full(pf_im),
                  full(s5d), full(w_glu), full(b_glu), ANY],
        out_specs=(row(S5_N), row(S5_N), row(D_S5), pl.BlockSpec((tb, D_S5), lambda i: (i, D_SSD // D_S5))),
        input_output_aliases={10: 3},
        scratch_shapes=[pltpu.VMEM((tb, S5_N), F32), pltpu.VMEM((tb, S5_N), F32),
                        pltpu.VMEM((8, S5_N), F32), pltpu.VMEM((8, S5_N), F32)],
        compiler_params=_params(48),
    )(u5, bb_re, bb_im, cc_re, cc_im, pf_re, pf_im, s5d, w_glu, b_glu, ycat)


def _layer_norm(r, g, b):
    mu = jnp.mean(r, axis=-1, keepdims=True)
    xc = r - mu
    rstd = lax.rsqrt(jnp.mean(xc * xc, axis=-1, keepdims=True) + EPS)
    xhat = xc * rstd
    return xhat * g + b, xhat, rstd


def _layer_norm_bwd(dy, xhat, rstd, g):
    dxhat = dy * g
    return rstd * (dxhat - jnp.mean(dxhat, axis=-1, keepdims=True)
                   - xhat * jnp.mean(dxhat * xhat, axis=-1, keepdims=True))


def _out_ln1(ycat, x2, mod3, w_out, ln1, seq):
    t = x2.shape[0]
    tb = 512
    npb = seq // tb

    def body(y_ref, x_ref, mod_ref, w_ref, ln_ref, mix_ref, x1_ref):
        m = mod_ref[0]
        mix = jnp.dot(y_ref[...], w_ref[...], preferred_element_type=F32)
        mix_ref[...] = mix
        r1 = ALPHA * x_ref[...] + (1.0 + m[2:3]) * mix
        x1_ref[...] = _layer_norm(r1, ln_ref[0:1], ln_ref[1:2])[0]

    row = lambda w: pl.BlockSpec((tb, w), lambda i: (i, 0))
    return pl.pallas_call(
        body, name="out_ln1", grid=(t // tb,),
        out_shape=(jax.ShapeDtypeStruct((t, D_MODEL), F32), jax.ShapeDtypeStruct((t, D_MODEL), F32)),
        in_specs=[row(D_SSD + D_S5), row(D_MODEL), pl.BlockSpec((1, N_MOD, D_MODEL), lambda i: (i // npb, 0, 0)),
                  pl.BlockSpec(w_out.shape, lambda i: (0, 0)), pl.BlockSpec(ln1.shape, lambda i: (0, 0))],
        out_specs=(row(D_MODEL), row(D_MODEL)), compiler_params=_params(48),
    )(ycat, x2, mod3, w_out, ln1)


def _mlp_fwd_bwd(x1, tgt, mod3, w1, w2, vec1, b1, seq):
    t = x1.shape[0]
    tb = 256
    npb = seq // tb
    n_fb, _, fb = w1.shape

    def body(x1_ref, tgt_ref, mod_ref, w1_hbm, w2_hbm, v_ref, b1_ref,
             dx1_ref, u2_ref, h_ref, dhp_ref, do_ref, gacc_ref, db1_ref, bacc_ref, w1_v, w2_v, sem1, sem2):
        i = pl.program_id(0)
        @pl.when(i == 0)
        def _():
            cps = [pltpu.make_async_copy(w1_hbm.at[k], w1_v.at[:, k * fb:(k + 1) * fb], sem1.at[k])
                   for k in range(n_fb)]
            for cp in cps:
                cp.start()
            for cp in cps:
                cp.wait()
        _load_once(w2_hbm, w2_v, sem2)

        @pl.when(i == 0)
        def _():
            gacc_ref[...] = jnp.zeros_like(gacc_ref)
            db1_ref[...] = jnp.zeros_like(db1_ref)

        @pl.when(i % npb == 0)
        def _():
            bacc_ref[...] = jnp.zeros_like(bacc_ref)

        m = mod_ref[0]
        sh2, sc2, g2 = m[3:4], m[4:5], m[5:6]
        x1v = x1_ref[...]
        u2 = (x1v * (1.0 + sc2) + sh2).astype(BF16)
        u2_ref[...] = u2
        hr = jnp.maximum(jnp.dot(u2, w1_v[...], preferred_element_type=F32) + b1_ref[...], 0.0)
        hb = (hr * hr).astype(BF16)
        h_ref[...] = hb
        o = jnp.dot(hb, w2_v[...], preferred_element_type=F32) + v_ref[0:1]
        r2 = ALPHA * x1v + (1.0 + g2) * o
        y, xhat, rstd = _layer_norm(r2, v_ref[1:2], v_ref[2:3])
        err = y - tgt_ref[...]
        dy = err * (1.0 / D_MODEL)
        dr2 = _layer_norm_bwd(dy, xhat, rstd, v_ref[1:2])
        do = (1.0 + g2) * dr2
        dob = do.astype(BF16)
        do_ref[...] = dob
        gacc_ref[0:1, :] += jnp.sum(dy * xhat, axis=0, keepdims=True)
        gacc_ref[1:2, :] += jnp.sum(dy, axis=0, keepdims=True)
        gacc_ref[2:3, :] += jnp.sum(do, axis=0, keepdims=True)
        gacc_ref[3:4, :] += jnp.sum(err * err, axis=0, keepdims=True)
        dhpre = lax.dot_general(dob, w2_v[...], NT, preferred_element_type=F32) * (2.0 * hr)
        dhpb = dhpre.astype(BF16)
        dhp_ref[...] = dhpb
        db1_ref[...] += jnp.sum(dhpre, axis=0, keepdims=True)
        du2 = lax.dot_general(dhpb, w1_v[...], NT, preferred_element_type=F32)
        dx1_ref[...] = ALPHA * dr2 + du2 * (1.0 + sc2)
        bacc_ref[0, 0:1, :] += jnp.sum(du2, axis=0, keepdims=True)
        bacc_ref[0, 1:2, :] += jnp.sum(du2 * x1v, axis=0, keepdims=True)
        bacc_ref[0, 2:3, :] += jnp.sum(dr2 * o, axis=0, keepdims=True)

    row = lambda w: pl.BlockSpec((tb, w), lambda i: (i, 0))
    return pl.pallas_call(
        body, name="mlp_fwd_bwd", grid=(t // tb,),
        out_shape=(jax.ShapeDtypeStruct((t, D_MODEL), F32), jax.ShapeDtypeStruct((t, D_MODEL), BF16),
                   jax.ShapeDtypeStruct((t, D_FF), BF16), jax.ShapeDtypeStruct((t, D_FF), BF16),
                   jax.ShapeDtypeStruct((t, D_MODEL), BF16), jax.ShapeDtypeStruct((8, D_MODEL), F32),
                   jax.ShapeDtypeStruct((1, D_FF), F32), jax.ShapeDtypeStruct((t // seq, 8, D_MODEL), F32)),
        in_specs=[row(D_MODEL), row(D_MODEL), pl.BlockSpec((1, N_MOD, D_MODEL), lambda i: (i // npb, 0, 0)), ANY, ANY,
                  pl.BlockSpec(vec1.shape, lambda i: (0, 0)), pl.BlockSpec(b1.shape, lambda i: (0, 0))],
        out_specs=(row(D_MODEL), row(D_MODEL), row(D_FF), row(D_FF), row(D_MODEL),
                   pl.BlockSpec((8, D_MODEL), lambda i: (0, 0)), pl.BlockSpec((1, D_FF), lambda i: (0, 0)),
                   pl.BlockSpec((1, 8, D_MODEL), lambda i: (i // npb, 0, 0))),
        scratch_shapes=[pltpu.VMEM((D_MODEL, n_fb * fb), BF16), pltpu.VMEM((D_FF, D_MODEL), BF16),
                        pltpu.SemaphoreType.DMA((n_fb,)), pltpu.SemaphoreType.DMA],
        compiler_params=_params(60),
    )(x1, tgt, mod3, w1, w2, vec1, b1)


def _ln1_out_bwd(dx1, x2, mix, mod3, w_out, ln1, seq):
    t = x2.shape[0]
    tb = 512
    npb = seq // tb

    def body(dx1_ref, x_ref, mix_ref, mod_ref, w_ref, ln_ref, dmix_ref, dxa_ref, dys_ref, dy5_ref, gacc_ref, bacc_ref):
        i = pl.program_id(0)

        @pl.when(i == 0)
        def _():
            gacc_ref[...] = jnp.zeros_like(gacc_ref)

        @pl.when(i % npb == 0)
        def _():
            bacc_ref[...] = jnp.zeros_like(bacc_ref)

        m = mod_ref[0]
        mix = mix_ref[...]
        r1 = ALPHA * x_ref[...] + (1.0 + m[2:3]) * mix
        _, xhat, rstd = _layer_norm(r1, ln_ref[0:1], ln_ref[1:2])
        dx1v = dx1_ref[...]
        dr1 = _layer_norm_bwd(dx1v, xhat, rstd, ln_ref[0:1])
        gacc_ref[0:1, :] += jnp.sum(dx1v * xhat, axis=0, keepdims=True)
        gacc_ref[1:2, :] += jnp.sum(dx1v, axis=0, keepdims=True)
        bacc_ref[0, 0:1, :] += jnp.sum(dr1 * mix, axis=0, keepdims=True)
        dmix = ((1.0 + m[2:3]) * dr1).astype(BF16)
        dmix_ref[...] = dmix
        dxa_ref[...] = ALPHA * dr1
        dys_ref[...] = lax.dot_general(dmix, w_ref[0:D_SSD, :], NT, preferred_element_type=F32)
        dy5_ref[...] = lax.dot_general(dmix, w_ref[D_SSD:, :], NT, preferred_element_type=F32)

    row = lambda w: pl.BlockSpec((tb, w), lambda i: (i, 0))
    return pl.pallas_call(
        body, name="ln1_out_bwd", grid=(t // tb,),
        out_shape=(jax.ShapeDtypeStruct((t, D_MODEL), BF16), jax.ShapeDtypeStruct((t, D_MODEL), F32),
                   jax.ShapeDtypeStruct((t, D_SSD), F32), jax.ShapeDtypeStruct((t, D_S5), F32),
                   jax.ShapeDtypeStruct((8, D_MODEL), F32), jax.ShapeDtypeStruct((t // seq, 8, D_MODEL), F32)),
        in_specs=[row(D_MODEL), row(D_MODEL), row(D_MODEL), pl.BlockSpec((1, N_MOD, D_MODEL), lambda i: (i // npb, 0, 0)),
                  pl.BlockSpec(w_out.shape, lambda i: (0, 0)), pl.BlockSpec(ln1.shape, lambda i: (0, 0))],
        out_specs=(row(D_MODEL), row(D_MODEL), row(D_SSD), row(D_S5), pl.BlockSpec((8, D_MODEL), lambda i: (0, 0)),
                   pl.BlockSpec((1, 8, D_MODEL), lambda i: (i // npb, 0, 0))),
        compiler_params=_params(48),
    )(dx1, x2, mix, mod3, w_out, ln1)


def _s5_bwd(dy5, ypre, u5, s_re, s_im, bb_re, bb_im, cc_re, cc_im, pr_re, pr_im, s5d, w_glu, b_glu, seq):
    t = u5.shape[0]
    tb = 256
    npb = seq // tb
    n_blocks = t // tb

    def blk(i):
        return (i // npb) * npb + (npb - 1 - i % npb)

    def body(dy_ref, ypre_ref, u_ref, sre_ref, sim_ref, hre_ref, him_ref, bbr_ref, bbi_ref, ccr_ref, cci_ref,
             prr_ref, pri_ref, d_ref, wg_ref, bg_ref,
             du_ref, vacc_ref, sacc_ref, dcc_ref, dbb_ref, dwg_ref, dsr, dsi, gr, gi, car, cai):
        i = pl.program_id(0)

        @pl.when(i == 0)
        def _():
            for acc in (vacc_ref, sacc_ref, dcc_ref, dbb_ref, dwg_ref):
                acc[...] = jnp.zeros_like(acc)

        @pl.when(i % npb == 0)
        def _():
            car[...] = jnp.zeros_like(car)
            cai[...] = jnp.zeros_like(cai)

        dy = dy_ref[...]
        ypre = ypre_ref[...]
        u = u_ref[...]
        ub = u.astype(BF16)
        yg = _gelu(ypre)
        sg = _sigmoid(_mm(yg, wg_ref[...]) + bg_ref[...])
        dq = dy * yg * sg * (1.0 - sg)
        dqb = dq.astype(BF16)
        dyg = dy * sg + lax.dot_general(dqb, wg_ref[...], NT, preferred_element_type=F32)
        dyp = dyg * _gelu_grad(ypre)
        dypb = dyp.astype(BF16)
        dwg_ref[...] += lax.dot_general(yg.astype(BF16), dqb, TN, preferred_element_type=F32)
        blocks = [(slice(j * 128, (j + 1) * 128), slice(j * 512, (j + 1) * 512)) for j in range(S5_BLOCKS)]
        for j, (ch, st) in enumerate(blocks):
            dsr[:, st] = lax.dot_general(dypb[:, ch], ccr_ref[j], NT, preferred_element_type=F32)
            dsi[:, st] = -lax.dot_general(dypb[:, ch], cci_ref[j], NT, preferred_element_type=F32)
        _tile_scan(dsr, dsi, gr, gi, car, cai, prr_ref, pri_ref, tb // 8, reverse=True)
        g_re, g_im = gr[...], gi[...]
        first_rows = (i % npb) == npb - 1
        hre = jnp.where(first_rows, 0.0, hre_ref[...])
        him = jnp.where(first_rows, 0.0, him_ref[...])
        s_re_v, s_im_v = sre_ref[...], sim_ref[...]
        sp_re = pltpu.roll(jnp.concatenate([hre, s_re_v], axis=0), 1, axis=0)[8:8 + tb]
        sp_im = pltpu.roll(jnp.concatenate([him, s_im_v], axis=0), 1, axis=0)[8:8 + tb]
        vacc_ref[0:1, :] += jnp.sum(g_re * sp_re + g_im * sp_im, axis=0, keepdims=True)
        vacc_ref[1:2, :] += jnp.sum(g_im * sp_re - g_re * sp_im, axis=0, keepdims=True)
        grb, gib = g_re.astype(BF16), g_im.astype(BF16)
        srb, sib = s_re_v.astype(BF16), s_im_v.astype(BF16)
        du_cols = []
        for j, (ch, st) in enumerate(blocks):
            dcc_ref[j] += lax.dot_general(srb[:, st], dypb[:, ch], TN, preferred_element_type=F32)
            dcc_ref[S5_BLOCKS + j] -= lax.dot_general(sib[:, st], dypb[:, ch], TN, preferred_element_type=F32)
            dbb_ref[j] += lax.dot_general(ub[:, ch], grb[:, st], TN, preferred_element_type=F32)
            dbb_ref[S5_BLOCKS + j] += lax.dot_general(ub[:, ch], gib[:, st], TN, preferred_element_type=F32)
            du_cols.append(lax.dot_general(grb[:, st], bbr_ref[j], NT, preferred_element_type=F32)
                           + lax.dot_general(gib[:, st], bbi_ref[j], NT, preferred_element_type=F32))
        du_ref[...] = jnp.concatenate(du_cols, axis=1) + dyp * d_ref[...]
        sacc_ref[0:1, :] += jnp.sum(dyp * u, axis=0, keepdims=True)
        sacc_ref[1:2, :] += jnp.sum(dq, axis=0, keepdims=True)

    row = lambda w: pl.BlockSpec((tb, w), lambda i: (blk(i), 0))
    halo = pl.BlockSpec((8, S5_N), lambda i: (jnp.maximum(blk(i) * (tb // 8) - 1, 0), 0))
    full = lambda a: pl.BlockSpec(a.shape, lambda i: (0,) * a.ndim)
    acc = lambda s: pl.BlockSpec(s, lambda i: (0,) * len(s))
    acc_shapes = [(8, S5_N), (8, D_S5), (2 * S5_BLOCKS, 512, 128), (2 * S5_BLOCKS, 128, 512), (D_S5, D_S5)]
    return pl.pallas_call(
        body, name="s5_bwd", grid=(n_blocks,),
        out_shape=(jax.ShapeDtypeStruct((t, D_S5), F32),) + tuple(jax.ShapeDtypeStruct(s, F32) for s in acc_shapes),
        in_specs=[row(D_S5), row(D_S5), row(D_S5), row(S5_N), row(S5_N), halo, halo, full(bb_re), full(bb_im),
                  full(cc_re), full(cc_im), full(pr_re), full(pr_im), full(s5d), full(w_glu), full(b_glu)],
        out_specs=(row(D_S5),) + tuple(acc(s) for s in acc_shapes),
        scratch_shapes=[pltpu.VMEM((tb, S5_N), F32), pltpu.VMEM((tb, S5_N), F32), pltpu.VMEM((tb, S5_N), F32),
                        pltpu.VMEM((tb, S5_N), F32), pltpu.VMEM((8, S5_N), F32), pltpu.VMEM((8, S5_N), F32)],
        compiler_params=_params(56),
    )(dy5, ypre, u5, s_re, s_im, s_re, s_im, bb_re, bb_im, cc_re, cc_im, pr_re, pr_im, s5d, w_glu, b_glu)


def _ssd_bwd(dyssd, yraw, z, xbc, dt_raw, hprev, par, dsk, normw, seq):
    t = xbc.shape[0]
    nc = seq // CHUNK
    n_chunks = t // CHUNK
    fold = _head_fold()

    def blk(i):
        return (i // nc) * nc + (nc - 1 - i % nc)

    def body(dy_ref, yraw_ref, z_ref, xbc_ref, dt_ref, hprev_ref, par_ref, dsk_ref, nw_ref, fold_ref,
             dxbc_ref, dz_ref, ddt_ref, dpar_ref, cacc_ref, dh_ref, dyr_ref):
        i = pl.program_id(0)

        @pl.when(i == 0)
        def _():
            dpar_ref[...] = jnp.zeros_like(dpar_ref)
            cacc_ref[...] = jnp.zeros_like(cacc_ref)

        @pl.when(i % nc == 0)
        def _():
            dh_ref[...] = jnp.zeros_like(dh_ref)

        zz = z_ref[...]
        sz = _sigmoid(zz)
        silu_z = zz * sz
        yraw = yraw_ref[...]
        for g in range(N_GROUPS):
            sl = slice(g * GW, (g + 1) * GW)
            v = yraw[:, sl] * silu_z[:, sl]
            r = lax.rsqrt(jnp.mean(v * v, axis=-1, keepdims=True) + EPS)
            dyg = dy_ref[:, sl]
            cacc_ref[1:2, sl] += jnp.sum(dyg * v * r, axis=0, keepdims=True)
            dyw = dyg * nw_ref[:, sl]
            dv = r * dyw - v * (r * r * r) * jnp.mean(dyw * v, axis=-1, keepdims=True)
            dyr_ref[:, sl] = dv * silu_z[:, sl]
            dz_ref[:, sl] = dv * yraw[:, sl] * (sz[:, sl] * (1.0 + zz[:, sl] * (1.0 - sz[:, sl])))

        dt, a, cs, cst, causal, tri, dt_c, ecs_c, w_c, pair_cols = _ssd_prep(dt_ref[...], par_ref[...])
        cs_last = cs[CHUNK - 1:CHUNK, :]
        causal2 = jnp.concatenate([causal, causal], axis=1)
        lane = lax.broadcasted_iota(jnp.int32, (CHUNK, 128), 1)
        left = lane < HEADDIM
        lane1 = lax.broadcasted_iota(jnp.int32, (1, 128), 1)
        x = xbc_ref[:, 0:D_SSD]
        xdt = x * dt_c
        dyr = dyr_ref[...]
        dyrb = dyr.astype(BF16)
        cacc_ref[0:1, :] += jnp.sum(dyr * x, axis=0, keepdims=True)
        dlast = jnp.zeros((1, 128), F32)
        dxdt_cols, diag_all, dww_cols = [], [], []
        for g in range(N_GROUPS):
            gs = slice(g * GW, (g + 1) * GW)
            b_sl = slice(D_SSD + g * N_STATE, D_SSD + (g + 1) * N_STATE)
            c_sl = slice(D_SSD + (N_GROUPS + g) * N_STATE, D_SSD + (N_GROUPS + g + 1) * N_STATE)
            bg = xbc_ref[:, b_sl].astype(BF16)
            cg = xbc_ref[:, c_sl].astype(BF16)
            scores = lax.dot_general(cg, bg, NT, preferred_element_type=F32)
            scores2 = jnp.concatenate([scores, scores], axis=1)
            hg = hprev_ref[0, gs, :]
            hgb = hg.astype(BF16)
            dhg = dh_ref[gs, :]
            dhgb = dhg.astype(BF16)
            q_all = lax.dot_general(bg, dhgb, NT, preferred_element_type=F32)
            dscores = jnp.zeros((CHUNK, CHUNK), F32)
            diag_cols = []
            for q in range(GW // 128):
                pair = g * (GW // 128) + q
                ps = slice(pair * 128, (pair + 1) * 128)
                decay = _pair_decay(pair_cols[pair], cst, pair, causal2)
                mcat = (scores2 * decay).astype(BF16)
                dyp = dyrb[:, ps]
                dm = lax.dot_general(dyp, _stack_heads(xdt[:, ps], left), NT, preferred_element_type=F32)
                dmd = dm * decay
                dscores = dscores + dmd[:, 0:CHUNK] + dmd[:, CHUNK:]
                rr = lax.dot_general(mcat, dyp, TN, preferred_element_type=F32)
                diag_cols.append(jnp.where(left, rr[0:CHUNK], rr[CHUNK:]))
            wq = w_c[:, gs] * q_all
            diag_g = jnp.concatenate(diag_cols, axis=1)
            diag_all.append(diag_g)
            dxdt_cols.append(diag_g + wq)
            dww_cols.append(wq * xdt[:, gs])
            dp = (ecs_c[:, gs] * dyr[:, gs]).astype(BF16)
            amat = (w_c[:, gs] * xdt[:, gs]).astype(BF16)
            dsb = dscores.astype(BF16)
            dxbc_ref[:, c_sl] = (jnp.dot(dsb, bg, preferred_element_type=F32)
                                 + jnp.dot(dp, hgb, preferred_element_type=F32))
            dxbc_ref[:, b_sl] = (lax.dot_general(dsb, cg, TN, preferred_element_type=F32)
                                 + jnp.dot(amat, dhgb, preferred_element_type=F32))
            dh_in = lax.dot_general(dp, cg, TN, preferred_element_type=F32)
            for j in range(HPG):
                hh = g * HPG + j
                js = slice(j * HEADDIM, (j + 1) * HEADDIM)
                ecl = jnp.exp(cs_last[:, hh:hh + 1])
                dlast = dlast + jnp.where(lane1 == hh, ecl * jnp.sum(dhg[js, :] * hg[js, :]), 0.0)
                dh_ref[g * GW + j * HEADDIM:g * GW + (j + 1) * HEADDIM, :] = ecl * dhg[js, :] + dh_in[js, :]
        dxdt = jnp.concatenate(dxdt_cols, axis=1)
        dxbc_ref[:, 0:D_SSD] = dxdt * dt_c + dyr * dsk_ref[...]
        dww = _mm(jnp.concatenate(dww_cols, axis=1), fold_ref[...])
        dcs = _dot3(dyrb.astype(F32) * (yraw - x * dsk_ref[...])
                    - xdt.astype(BF16).astype(F32) * jnp.concatenate(diag_all, axis=1), fold_ref[...]) - dww
        rowid = lax.broadcasted_iota(jnp.int32, (CHUNK, 128), 0)
        dcs = dcs + jnp.where(rowid == CHUNK - 1, jnp.sum(dww, axis=0, keepdims=True) + dlast, 0.0)
        dadt = _dot3_left(tri, dcs, TN)
        ddt = _mm(dxdt * x, fold_ref[...]) + dadt * a
        da = jnp.sum(dadt * dt, axis=0, keepdims=True)
        ddt_raw = ddt * _sigmoid(dt_ref[...] + par_ref[0:1])
        ddt_raw = jnp.where(lane < N_HEADS, ddt_raw, 0.0)
        ddt_ref[...] = ddt_raw
        dpar_ref[0:1, :] += jnp.sum(ddt_raw, axis=0, keepdims=True)
        dpar_ref[1:2, :] += jnp.where(lane1 < N_HEADS, da * a, 0.0)

    row = lambda w: pl.BlockSpec((CHUNK, w), lambda i: (blk(i), 0))
    full = lambda s: pl.BlockSpec(s, lambda i: (0,) * len(s))
    return pl.pallas_call(
        body, name="ssd_bwd", grid=(n_chunks,),
        out_shape=(jax.ShapeDtypeStruct((t, D_XBC), F32), jax.ShapeDtypeStruct((t, D_SSD), F32),
                   jax.ShapeDtypeStruct((t, DT_PAD), F32), jax.ShapeDtypeStruct((8, 128), F32),
                   jax.ShapeDtypeStruct((8, D_SSD), F32)),
        in_specs=[row(D_SSD), row(D_SSD), row(D_SSD), row(D_XBC), row(DT_PAD),
                  pl.BlockSpec((1, D_SSD, N_STATE), lambda i: (blk(i), 0, 0)),
                  full((8, 128)), full((1, D_SSD)), full((1, D_SSD)), full(fold.shape)],
        out_specs=(row(D_XBC), row(D_SSD), row(DT_PAD), full((8, 128)), full((8, D_SSD))),
        scratch_shapes=[pltpu.VMEM((D_SSD, N_STATE), F32), pltpu.VMEM((CHUNK, D_SSD), F32)],
        compiler_params=_params(48),
    )(dyssd, yraw, z, xbc, dt_raw, hprev, par, dsk, normw, fold)


def _conv_proj_bwd(dz, dxbc, dsilu, xbc_pre, ddt, du5, x2, dxa, mod3, conv_w, w_in_pad, seq):
    t = x2.shape[0]
    tb = 256
    npb = seq // tb
    n_blocks = t // tb
    cw = 512

    def blk(i):
        return (i // npb) * npb + (npb - 1 - i % npb)

    def body(dz_ref, d_ref, ds_ref, cur_ref, halo_ref, ddt_ref, du5_ref, x_ref, dxa_ref, mod_ref, cw_ref, w_hbm,
             gx_ref, u_ref, dxp_ref, bacc_ref, acc_ref, w_vmem, win_x, win_d, sem):
        i = pl.program_id(0)
        _load_once(w_hbm, w_vmem, sem)

        @pl.when(i == 0)
        def _():
            acc_ref[...] = jnp.zeros_like(acc_ref)

        @pl.when(i % npb == 0)
        def _():
            bacc_ref[...] = jnp.zeros_like(bacc_ref)
            win_d[tb:tb + 8, :] = jnp.zeros((8, D_XBC), F32)

        @pl.when(i % npb != 0)
        def _():
            win_d[tb:tb + 8, :] = win_d[0:8, :]

        first_rows = (i % npb) == npb - 1
        win_x[0:8, :] = jnp.where(first_rows, 0.0, halo_ref[...])
        win_x[8:8 + tb, :] = cur_ref[...]
        w = cw_ref[...]
        for k in range(D_XBC // cw):
            cols = slice(k * cw, (k + 1) * cw)
            dpre = d_ref[:, cols] * ds_ref[:, cols]
            win_d[0:tb, cols] = dpre
            for j in range(4):
                acc_ref[3 - j:4 - j, cols] += jnp.sum(dpre * win_x[8 - j:8 - j + tb, cols], axis=0, keepdims=True)
            acc_ref[4:5, cols] += jnp.sum(dpre, axis=0, keepdims=True)
            dxp = w[3:4, cols] * dpre
            for j in (1, 2, 3):
                dxp = dxp + w[3 - j:4 - j, cols] * win_d[j:j + tb, cols]
            dxp_ref[:, cols] = dxp.astype(BF16)
        o1, o2, o3 = D_SSD, D_SSD + D_XBC, D_SSD + D_XBC + DT_PAD
        du = (jnp.dot(dz_ref[...].astype(BF16), w_vmem[0:o1, :], preferred_element_type=F32)
              + jnp.dot(dxp_ref[...], w_vmem[o1:o2, :], preferred_element_type=F32)
              + jnp.dot(ddt_ref[...].astype(BF16), w_vmem[o2:o3, :], preferred_element_type=F32)
              + jnp.dot(du5_ref[...].astype(BF16), w_vmem[o3:, :], preferred_element_type=F32))
        m = mod_ref[0]
        xv = x_ref[...]
        u_ref[...] = (xv * (1.0 + m[1:2]) + m[0:1]).astype(BF16)
        gx_ref[...] = dxa_ref[...] + du * (1.0 + m[1:2])
        bacc_ref[0, 0:1, :] += jnp.sum(du, axis=0, keepdims=True)
        bacc_ref[0, 1:2, :] += jnp.sum(du * xv, axis=0, keepdims=True)

    row = lambda w: pl.BlockSpec((tb, w), lambda i: (blk(i), 0))
    halo = pl.BlockSpec((8, D_XBC), lambda i: (jnp.maximum(blk(i) * (tb // 8) - 1, 0), 0))
    return pl.pallas_call(
        body, name="conv_proj_bwd", grid=(n_blocks,),
        out_shape=(jax.ShapeDtypeStruct((t, D_MODEL), F32), jax.ShapeDtypeStruct((t, D_MODEL), BF16),
                   jax.ShapeDtypeStruct((t, D_XBC), BF16), jax.ShapeDtypeStruct((t // seq, 8, D_MODEL), F32),
                   jax.ShapeDtypeStruct((8, D_XBC), F32)),
        in_specs=[row(D_SSD), row(D_XBC), row(D_XBC), row(D_XBC), halo, row(DT_PAD), row(D_S5), row(D_MODEL),
                  row(D_MODEL), pl.BlockSpec((1, N_MOD, D_MODEL), lambda i: (i // npb, 0, 0)),
                  pl.BlockSpec((4, D_XBC), lambda i: (0, 0)), ANY],
        out_specs=(row(D_MODEL), row(D_MODEL), row(D_XBC), pl.BlockSpec((1, 8, D_MODEL), lambda i: (i // npb, 0, 0)),
                   pl.BlockSpec((8, D_XBC), lambda i: (0, 0))),
        scratch_shapes=[pltpu.VMEM((D_INP, D_MODEL), BF16), pltpu.VMEM((tb + 8, D_XBC), F32),
                        pltpu.VMEM((tb + 8, D_XBC), F32), pltpu.SemaphoreType.DMA],
        compiler_params=_params(60),
    )(dz, dxbc, dsilu, xbc_pre, xbc_pre, ddt, du5, x2, dxa, mod3, conv_w, w_in_pad)


def _pad_rows(a, mult):
    r = a.shape[0]
    pad = (-r) % mult
    return a if pad == 0 else jnp.concatenate([a, jnp.zeros((pad,) + a.shape[1:], a.dtype)], axis=0)


_SMALL = ["conv_w", "conv_b", "dt_bias", "a_log", "d_ssd", "norm_w", "s5_a_re", "s5_a_im", "s5_log_dt", "s5_b_re",
          "s5_b_im", "s5_c_re", "s5_c_im", "s5_d", "b_glu", "ln1_g", "ln1_b", "b1", "b2", "ln2_g", "ln2_b",
          "loss_lanes"]
_NOT_UPDATED = {"conv_w": (1, 4, D_XBC), "loss_lanes": (1, D_MODEL)}


def _tile_rows(size):
    return 8 * (-(-size // 1024))


def _pack_small(d):
    parts = []
    for n in _SMALL:
        flat = d[n].reshape(-1).astype(F32)
        rows = _tile_rows(flat.shape[0])
        pad = rows * 128 - flat.shape[0]
        if pad:
            flat = jnp.concatenate([flat, jnp.zeros((pad,), F32)])
        parts.append(flat.reshape(rows, 128))
    return _pad_rows(jnp.concatenate(parts, axis=0), 256)


def _unpack_small(p, shapes):
    out, off = {}, 0
    for n in _SMALL:
        size = math.prod(shapes[n])
        rows = _tile_rows(size)
        out[n] = p[off:off + rows].reshape(-1)[:size].reshape(shapes[n])
        off += rows
    return out


def kernel(x, c, w_ada, b_ada, w_in, conv_w, conv_b, dt_bias, a_log, d_ssd, norm_w, s5_a_re, s5_a_im, s5_log_dt, s5_b_re, s5_b_im, s5_c_re, s5_c_im, s5_d, w_glu, b_glu, w_out, ln1_g, ln1_b, w1, b1, w2, b2, ln2_g, ln2_b, loss_target, m_w_ada, m_b_ada, m_w_in, m_conv_w, m_conv_b, m_dt_bias, m_a_log, m_d_ssd, m_norm_w, m_s5_a_re, m_s5_a_im, m_s5_log_dt, m_s5_b_re, m_s5_b_im, m_s5_c_re, m_s5_c_im, m_s5_d, m_w_glu, m_b_glu, m_w_out, m_ln1_g, m_ln1_b, m_w1, m_b1, m_w2, m_b2, m_ln2_g, m_ln2_b, v_w_ada, v_b_ada, v_w_in, v_conv_w, v_conv_b, v_dt_bias, v_a_log, v_d_ssd, v_norm_w, v_s5_a_re, v_s5_a_im, v_s5_log_dt, v_s5_b_re, v_s5_b_im, v_s5_c_re, v_s5_c_im, v_s5_d, v_w_glu, v_b_glu, v_w_out, v_ln1_g, v_ln1_b, v_w1, v_b1, v_w2, v_b2, v_ln2_g, v_ln2_b):
    weights = dict(w_ada=w_ada, b_ada=b_ada, w_in=w_in, conv_w=conv_w, conv_b=conv_b, dt_bias=dt_bias, a_log=a_log,
                   d_ssd=d_ssd, norm_w=norm_w, s5_a_re=s5_a_re, s5_a_im=s5_a_im, s5_log_dt=s5_log_dt, s5_b_re=s5_b_re,
                   s5_b_im=s5_b_im, s5_c_re=s5_c_re, s5_c_im=s5_c_im, s5_d=s5_d, w_glu=w_glu, b_glu=b_glu, w_out=w_out,
                   ln1_g=ln1_g, ln1_b=ln1_b, w1=w1, b1=b1, w2=w2, b2=b2, ln2_g=ln2_g, ln2_b=ln2_b)
    mom = dict(w_ada=m_w_ada, b_ada=m_b_ada, w_in=m_w_in, conv_w=m_conv_w, conv_b=m_conv_b, dt_bias=m_dt_bias,
               a_log=m_a_log, d_ssd=m_d_ssd, norm_w=m_norm_w, s5_a_re=m_s5_a_re, s5_a_im=m_s5_a_im,
               s5_log_dt=m_s5_log_dt, s5_b_re=m_s5_b_re, s5_b_im=m_s5_b_im, s5_c_re=m_s5_c_re, s5_c_im=m_s5_c_im,
               s5_d=m_s5_d, w_glu=m_w_glu, b_glu=m_b_glu, w_out=m_w_out, ln1_g=m_ln1_g, ln1_b=m_ln1_b, w1=m_w1, b1=m_b1,
               w2=m_w2, b2=m_b2, ln2_g=m_ln2_g, ln2_b=m_ln2_b)
    var = dict(w_ada=v_w_ada, b_ada=v_b_ada, w_in=v_w_in, conv_w=v_conv_w, conv_b=v_conv_b, dt_bias=v_dt_bias,
               a_log=v_a_log, d_ssd=v_d_ssd, norm_w=v_norm_w, s5_a_re=v_s5_a_re, s5_a_im=v_s5_a_im,
               s5_log_dt=v_s5_log_dt, s5_b_re=v_s5_b_re, s5_b_im=v_s5_b_im, s5_c_re=v_s5_c_re, s5_c_im=v_s5_c_im,
               s5_d=v_s5_d, w_glu=v_w_glu, b_glu=v_b_glu, w_out=v_w_out, ln1_g=v_ln1_g, ln1_b=v_ln1_b, w1=v_w1, b1=v_b1,
               w2=v_w2, b2=v_b2, ln2_g=v_ln2_g, ln2_b=v_ln2_b)
    names = list(weights)
    shapes = {n: weights[n].shape for n in names}

    nb, seq, _ = x.shape
    t = nb * seq
    dev = _dev_index()
    x2 = x.reshape(t, D_MODEL)
    tgt2 = loss_target.reshape(t, D_MODEL)

    cw_cols = conv_w.shape[2]
    small_in = jnp.concatenate([c.reshape(-1), conv_w.reshape(-1)]).reshape(-1, 128)
    big_names = ["w_in", "w_out", "w1", "w2", "w_glu"]
    local = {n: (a[0].T if n == "w_in" else a[0]) for n, a in weights.items() if n in big_names}
    shard_bf16 = {n: local[n].astype(BF16) for n in big_names}
    first = _all_gather([small_in, shard_bf16["w_in"], shard_bf16["w_glu"]], "gather_first")
    small_all = first[0].reshape(N_DEV, -1)
    c_all = small_all[:, :nb * D_MODEL].reshape(N_DEV * nb, D_MODEL)
    conv_w_full = small_all[:, nb * D_MODEL:].reshape(N_DEV, 4, cw_cols).transpose(1, 0, 2).reshape(4, D_XBC)

    w_in_t = first[1].reshape(D_IN, D_MODEL)
    w_in_pad = jnp.concatenate(
        [w_in_t[:D_SSD + D_XBC + N_HEADS], jnp.zeros((DT_PAD - N_HEADS, D_MODEL), BF16),
         w_in_t[D_SSD + D_XBC + N_HEADS:]], axis=0)
    w_glu_f = first[2].reshape(D_S5, D_S5)
    late_names = ["w_out", "w1", "w2"]

    ada_cols = w_ada.shape[2]
    b_cols = lax.dynamic_slice_in_dim(b_ada, dev * ada_cols, ada_cols, axis=1)
    mod_cols = _mod_fwd(c_all, w_ada[0], b_cols)
    mod_all = _all_gather([mod_cols], "gather_mod")[0]
    mod_mine = lax.dynamic_slice_in_dim(mod_all, dev * nb, nb, axis=1)
    mod3 = mod_mine.transpose(1, 0, 2).reshape(nb, N_MOD, D_MODEL)

    def pad_lanes(v, n):
        return jnp.concatenate([v, jnp.zeros((v.shape[0], n - v.shape[1]), F32)], axis=1)

    par = _pad_rows(jnp.concatenate([pad_lanes(dt_bias, 128), pad_lanes(a_log, 128)], axis=0), 8)
    dsk = jnp.repeat(d_ssd[0], HEADDIM).reshape(1, D_SSD)
    ar = s5_a_re.reshape(1, S5_N)
    ai = s5_a_im.reshape(1, S5_N)
    ldt = jnp.repeat(s5_log_dt[0], S5_P).reshape(1, S5_N)
    br_t = s5_b_re[0].transpose(2, 0, 1).reshape(S5_CH, S5_N)
    bi_t = s5_b_im[0].transpose(2, 0, 1).reshape(S5_CH, S5_N)
    bb_re_t, bb_im_t, pf_re, pf_im, pr_re, pr_im = _s5_params_fwd(ar, ai, ldt, br_t, bi_t)
    gpb = S5_GROUPS // S5_BLOCKS
    mask_b = (jnp.arange(128)[:, None] // S5_CH) == (jnp.arange(512)[None, :] // S5_P)

    def dense_b(bt_):
        blocks = bt_.reshape(S5_CH, S5_BLOCKS, 512).transpose(1, 0, 2)
        return jnp.where(mask_b, jnp.tile(blocks, (1, gpb, 1)), 0.0).astype(BF16)

    def dense_c(cc):
        blocks = cc[0].transpose(0, 2, 1).reshape(S5_BLOCKS, 512, S5_CH)
        return jnp.where(mask_b.T, jnp.tile(blocks, (1, 1, gpb)), 0.0).astype(BF16)

    bb_re, bb_im = dense_b(bb_re_t), dense_b(bb_im_t)
    cc_re, cc_im = dense_c(s5_c_re), dense_c(s5_c_im)
    s5d = s5_d.reshape(1, D_S5)
    ln1 = jnp.concatenate([ln1_g, ln1_b], axis=0)
    vec1 = _pad_rows(jnp.concatenate([b2, ln2_g, ln2_b], axis=0), 8)

    z, xbc_pre, xbc, dsilu, dt_raw, u5 = _proj_conv_fwd(x2, mod3, w_in_pad, conv_w_full, conv_b, seq)
    late_in, dt_raw = lax.optimization_barrier(([shard_bf16[n] for n in late_names], dt_raw))
    late_sems = _gather_start(late_in, "gather_late_start")
    yraw, ycat, hprev = _ssd_fwd(xbc, z, dt_raw, par + late_sems[4][0, 0], dsk, norm_w, seq)
    s_re, s_im, ypre, ycat = _s5_fwd(u5, bb_re, bb_im, cc_re, cc_im, pf_re, pf_im, s5d, w_glu_f, b_glu, ycat, seq)
    sent, landed = _gather_wait(late_sems[0], late_sems[1], late_sems[2], late_sems[3], ycat, "gather_late_wait")
    gathered = {n: lax.dynamic_update_index_in_dim(l, x, dev, 0) for n, x, l in zip(late_names, sent, landed)}
    w_out_f = gathered["w_out"].reshape(2 * D_MODEL, D_MODEL)
    w1_blocks = gathered["w1"]
    w2_f = gathered["w2"].reshape(D_FF, D_MODEL)
    mix, x1 = _out_ln1(ycat, x2, mod3, w_out_f, ln1, seq)

    dx1, u2b, hb, dhpb, dob, gacc2, db1, bacc2 = _mlp_fwd_bwd(x1, tgt2, mod3, w1_blocks, w2_f, vec1, b1, seq)

    dmixb, dxa, dyssd, dy5, gacc1, bacc1 = _ln1_out_bwd(dx1, x2, mix, mod3, w_out_f, ln1, seq)

    g_w2 = _atb(hb, dob, "gw2")
    g_w1 = _atb(u2b, dhpb, "gw1")
    g_wout = _atb(ycat, dmixb, "gwout")
    core = lax.axis_index("c").astype(jnp.int32).reshape(1)
    chip = 2 * lax.axis_index("x") + lax.axis_index("y")

    def chip_sums_of(names, grads, tag):
        by_dest = [g if g.ndim == 2 else g.reshape((4, 2) + g.shape[1:]) for g in grads]
        from_sibling = _sibling_swap(by_dest, "rs_swap_" + tag)
        return [_add_halves(g, r, core, "rs_add_" + n) for g, r, n in zip(by_dest, from_sibling, names)]

    early_names = ["w_out", "w1", "w2"]
    early_dest = [g_wout.reshape((4, 2) + w_out.shape[1:]), g_w1, g_w2.reshape((4, 2) + w2.shape[1:])]
    swap = _sibling_swap_start(early_dest, "rs_early_swap_start")
    du5, vacc, sacc, d_cc, d_bb, g_wglu = _s5_bwd(dy5, ypre, u5, s_re, s_im, bb_re, bb_im, cc_re, cc_im,
                                                  pr_re, pr_im, s5d + swap[4][0, 0], w_glu_f, b_glu, seq)
    early_dest, from_sibling = _sibling_swap_wait(swap[0], swap[1], swap[2], swap[3], du5, "rs_early_swap_wait")
    early_sums = [_add_halves(g, r, core, "rs_add_" + n) for g, r, n in zip(early_dest, from_sibling, early_names)]
    early = _all_to_all_start(early_sums, "rs_early_start")
    dxbc, dz, ddt, dpar, cacc = _ssd_bwd(dyssd, yraw, z, xbc, dt_raw, hprev, par + early[4][0, 0], dsk, norm_w, seq)
    grad_x2, ub, dxpb, bacc0, conv_acc = _conv_proj_bwd(dz, dxbc, dsilu, xbc_pre, ddt, du5, x2, dxa, mod3,
                                                        conv_w_full, w_in_pad, seq)

    def diag_b(dd):
        kept = jnp.where(mask_b, dd, 0.0).reshape(S5_BLOCKS, gpb, S5_CH, 512).sum(1)
        return kept.transpose(1, 0, 2).reshape(S5_CH, S5_N)

    def diag_c(dd):
        kept = jnp.where(mask_b.T, dd, 0.0).reshape(S5_BLOCKS, 512, gpb, S5_CH).sum(2)
        return kept.reshape(S5_GROUPS, S5_P, S5_CH).transpose(0, 2, 1)

    g_ar, g_ai, g_ldt, g_br_t, g_bi_t = _s5_params_bwd(ar, ai, ldt, br_t, bi_t, vacc[0:1], vacc[1:2],
                                                      diag_b(d_bb[:S5_BLOCKS]), diag_b(d_bb[S5_BLOCKS:]))

    def from_t(gt):
        return gt.reshape(S5_CH, S5_GROUPS, S5_P).transpose(1, 2, 0)

    small_g = dict(
        conv_w=conv_acc[0:4], conv_b=conv_acc[4:5], dt_bias=dpar[0:1, :N_HEADS], a_log=dpar[1:2, :N_HEADS],
        d_ssd=cacc[0].reshape(N_HEADS, HEADDIM).sum(1), norm_w=cacc[1:2],
        s5_a_re=g_ar, s5_a_im=g_ai, s5_log_dt=g_ldt[:, :S5_GROUPS], s5_b_re=from_t(g_br_t), s5_b_im=from_t(g_bi_t),
        s5_c_re=diag_c(d_cc[:S5_BLOCKS]), s5_c_im=diag_c(d_cc[S5_BLOCKS:]), s5_d=sacc[0:1], b_glu=sacc[1:2],
        ln1_g=gacc1[0:1], ln1_b=gacc1[1:2], b1=db1, b2=gacc2[2:3], ln2_g=gacc2[0:1], ln2_b=gacc2[1:2],
        loss_lanes=gacc2[3:4])

    dmod = jnp.concatenate([bacc0[:, 0], bacc0[:, 1], bacc1[:, 0], bacc2[:, 0], bacc2[:, 1], bacc2[:, 2]], axis=1)
    small_sems = _gather_start([dmod, _pack_small(small_g)], "gather_small_start")
    tok = small_sems[4]
    g_win_t = jnp.concatenate([_atb(dz, ub, "gwin_z", tok), _atb(dxpb, ub, "gwin_xbc", tok),
                               _atb(ddt, ub, "gwin_dt", tok)[:N_HEADS], _atb(du5, ub, "gwin_s5", tok)], axis=0)
    sent, landed = _gather_wait(small_sems[0], small_sems[1], small_sems[2], small_sems[3], g_win_t,
                                "gather_small_wait")
    dmod_all, small_parts = [lax.dynamic_update_index_in_dim(l, x, dev, 0) for x, l in zip(sent, landed)]
    dmod_all = dmod_all.reshape(N_DEV * nb, N_MOD * D_MODEL)
    dmod_cols = lax.dynamic_slice_in_dim(dmod_all, dev * ada_cols, ada_cols, axis=1)
    g_wada, g_bada = _mod_bwd(c_all, dmod_cols, dmod_all)

    late_rs = ["w_in", "w_glu"]
    late_g, small_parts = lax.optimization_barrier(
        ([g_win_t.reshape(N_DEV, w_in.shape[2], D_MODEL), g_wglu.reshape((N_DEV,) + w_glu.shape[1:])], small_parts))
    late = _all_to_all_start(chip_sums_of(late_rs, late_g, "late"), "rs_late_start")

    def own_block_in(landed, sent):
        return [lax.dynamic_update_index_in_dim(l, lax.dynamic_index_in_dim(h, chip, 0, keepdims=False), chip, 0)
                for l, h in zip(landed, sent)]

    sent, landed = _all_to_all_wait(early[0], early[1], early[2], early[3], late[4], "rs_early_wait")
    parts = dict(zip(early_names, own_block_in(landed, sent)))
    res = {k: {} for k in "gdmv"}

    def update(n):
        w_m_v = [(a[n][0].T if n == "w_in" else a[n][0]) for a in (weights, mom, var)]
        outs = _adamw(parts[n], *w_m_v, "adamw_" + n)
        for k, a in zip("gdmv", outs):
            res[k][n] = (a.T if n == "w_in" else a)[None]

    for n in early_names:
        update(n)
    sent, landed = _all_to_all_wait(late[0], late[1], late[2], late[3], res["d"]["w2"], "rs_late_wait")
    parts.update(zip(late_rs, own_block_in(landed, sent)))
    for n in late_rs:
        update(n)

    ag, ad, am, av = _adamw(g_wada[None], w_ada[0], m_w_ada[0], v_w_ada[0], "adamw_w_ada")
    for k, a in (("g", ag), ("d", ad), ("m", am), ("v", av)):
        res[k]["w_ada"] = a[None]
    bg_, bd_, bm_, bv_ = _adamw(g_bada.reshape(1, -1, 128), b_ada.reshape(-1, 128), m_b_ada.reshape(-1, 128),
                                v_b_ada.reshape(-1, 128), "adamw_b_ada")
    for k, a in (("g", bg_), ("d", bd_), ("m", bm_), ("v", bv_)):
        res[k]["b_ada"] = a.reshape(shapes["b_ada"])

    small_shapes = {**shapes, **_NOT_UPDATED}
    rep = {n: (jnp.zeros(_NOT_UPDATED[n], F32) if n in _NOT_UPDATED else weights[n]) for n in _SMALL}
    rep_m = {n: (jnp.zeros(_NOT_UPDATED[n], F32) if n in _NOT_UPDATED else mom[n]) for n in _SMALL}
    rep_v = {n: (jnp.ones(_NOT_UPDATED[n], F32) if n in _NOT_UPDATED else var[n]) for n in _SMALL}
    sg_, sd_, sm_, sv_ = _adamw(small_parts, _pack_small(rep), _pack_small(rep_m), _pack_small(rep_v), "adamw_small")
    for k, p in (("g", sg_), ("d", sd_), ("m", sm_), ("v", sv_)):
        un = _unpack_small(p, small_shapes)
        for n in _SMALL:
            if n not in _NOT_UPDATED:
                res[k][n] = un[n]
    summed = _unpack_small(sg_, small_shapes)
    loss = 0.5 / D_MODEL * jnp.sum(summed["loss_lanes"])
    g_conv_full = summed["conv_w"][0]
    g_conv_mine = lax.dynamic_slice_in_dim(g_conv_full, dev * cw_cols, cw_cols, axis=1)
    cg_, cd_, cm_, cv_ = _adamw(g_conv_mine[None], conv_w[0], m_conv_w[0], v_conv_w[0], "adamw_conv_w")
    for k, a in (("g", cg_), ("d", cd_), ("m", cm_), ("v", cv_)):
        res[k]["conv_w"] = a[None]

    grad_x = grad_x2.reshape(nb, seq, D_MODEL)
    return (loss, grad_x, *[res["g"][n] for n in names], *[res["d"][n] for n in names],
            *[res["m"][n] for n in names], *[res["v"][n] for n in names])
```
